```python
import math
import jax, jax.numpy as jnp
from jax import lax
import numpy as np

D_MODEL = 1024
BATCH = 32
SEQ = 2048
DEPTH = 1

HEAD_DIM = 64
DIL_GROUPS = ((128, 1), (512, 4), (2048, 16))
A_HEADS_PER_GROUP = 4
A_HEADS = A_HEADS_PER_GROUP * len(DIL_GROUPS)
A_OUT = A_HEADS_PER_GROUP * HEAD_DIM
B_Q_HEADS = 8
B_KV_HEADS = 2
B_WINDOW = 128
B_OUT = B_Q_HEADS * HEAD_DIM
BLOCK = 128
N_BUCKETS = 32
MAX_DISTANCE = 2048
TOTAL_HEADS = A_HEADS + B_Q_HEADS
D_FF = 2816
EPS = 1e-6
A_QKV_W = A_HEADS * HEAD_DIM
B_KV_W = B_KV_HEADS * HEAD_DIM
SPLITS = (A_QKV_W, A_QKV_W, A_QKV_W, B_OUT, B_KV_W, B_KV_W, D_MODEL, D_MODEL)
D_IN = sum(SPLITS)

kernel_name = "hybrid_dilated_swa_sink_macaron"


def rms_norm(x, g):
    x32 = x.astype(jnp.float32)
    y = x32 * lax.rsqrt(jnp.mean(x32 * x32, axis=-1, keepdims=True) + EPS)
    return (y * g.astype(jnp.float32)).astype(x.dtype)


def swiglu(x, w_gate, w_up, w_down):
    return (jax.nn.silu(x @ w_gate) * (x @ w_up)) @ w_down


def t5_bucket(dist):
    max_exact = N_BUCKETS // 2
    n = jnp.maximum(dist, 0)
    nf = jnp.maximum(n, 1).astype(jnp.float32)
    large = max_exact + (jnp.log(nf / max_exact) / math.log(MAX_DISTANCE / max_exact)
                         * (N_BUCKETS - max_exact)).astype(jnp.int32)
    large = jnp.minimum(large, N_BUCKETS - 1)
    return jnp.where(n < max_exact, n, large)


def band_geometry(max_steps):
    qi = jnp.arange(BLOCK)[:, None]
    ki = jnp.arange(2 * BLOCK)[None, :]
    dist = qi + BLOCK - ki
    band = (dist >= 0) & (dist <= max_steps)
    return dist, band


def rel_bias_block(table, dist, stride, h0, h1):
    b = table[t5_bucket(dist * stride)][..., h0:h1]
    return jnp.transpose(b, (2, 0, 1))


def banded_attention(q, k, v, bias, max_steps, sinks=None):
    N, L, H, hd = q.shape
    hkv = k.shape[2]
    g = H // hkv
    nb = -(-L // BLOCK)
    lp = nb * BLOCK
    pad = ((0, 0), (0, lp - L), (0, 0), (0, 0))
    qb = jnp.pad(q, pad).reshape(N, nb, BLOCK, hkv, g, hd)
    kb = jnp.pad(k, pad).reshape(N, nb, BLOCK, hkv, hd)
    vb = jnp.pad(v, pad).reshape(N, nb, BLOCK, hkv, hd)

    def with_prev(t):
        prev = jnp.pad(t, ((0, 0), (1, 0), (0, 0), (0, 0), (0, 0)))[:, :-1]
        return jnp.concatenate([prev, t], axis=2)

    kc, vc = with_prev(kb), with_prev(vb)
    _, band = band_geometry(max_steps)
    first = (jnp.arange(nb) == 0)[:, None, None]
    before_start = (jnp.arange(2 * BLOCK) < BLOCK)[None, None, :]
    valid = band[None] & ~(first & before_start)

    logits = jnp.einsum('nbqhgd,nbkhd->nbhgqk', qb, kc).astype(jnp.float32) * (hd ** -0.5)
    logits = logits + bias.astype(jnp.float32).reshape(hkv, g, BLOCK, 2 * BLOCK)
    logits = jnp.where(valid[None, :, None, None], logits, -jnp.inf)
    m = jnp.max(logits, axis=-1)
    if sinks is not None:
        s = sinks.astype(jnp.float32).reshape(hkv, g)[:, :, None]
        m = jnp.maximum(m, s)
    p = jnp.exp(logits - m[..., None])
    denom = jnp.sum(p, axis=-1)
    if sinks is not None:
        denom = denom + jnp.exp(s - m)
    out = jnp.einsum('nbhgqk,nbkhd->nbqhgd', p.astype(vc.dtype), vc)
    out = (out / jnp.transpose(denom, (0, 1, 4, 2, 3))[..., None]).astype(q.dtype)
    out = out.reshape(N, lp, H, hd)[:, :L]
    lse = jnp.transpose(m + jnp.log(denom), (0, 1, 4, 2, 3)).reshape(N, lp, H)[:, :L]
    return out, lse


def fold_residues(t, d):
    B, S, H, hd = t.shape
    t = t.reshape(B, S // d, d, H, hd)
    return jnp.transpose(t, (0, 2, 1, 3, 4)).reshape(B * d, S // d, H, hd)


def unfold_residues(t, B, d):
    Bd, Ld = t.shape[:2]
    rest = t.shape[2:]
    t = t.reshape((B, d, Ld) + rest)
    perm = (0, 2, 1) + tuple(range(3, t.ndim))
    return jnp.transpose(t, perm).reshape((B, d * Ld) + rest)


def dilated_attention(q, k, v, table):
    B, S = q.shape[:2]
    outs, lses = [], []
    for gi, (window, d) in enumerate(DIL_GROUPS):
        h0, h1 = gi * A_HEADS_PER_GROUP, (gi + 1) * A_HEADS_PER_GROUP
        steps = window // d
        dist, _ = band_geometry(steps)
        bias = rel_bias_block(table, dist, d, h0, h1)
        o, lse = banded_attention(fold_residues(q[:, :, h0:h1], d), fold_residues(k[:, :, h0:h1], d),
                                  fold_residues(v[:, :, h0:h1], d), bias, steps)
        outs.append(unfold_residues(o, B, d))
        lses.append(unfold_residues(lse, B, d))
    alpha = jax.nn.softmax(jnp.stack(lses, axis=0), axis=0)
    y = jnp.sum(alpha[..., None] * jnp.stack(outs, axis=0).astype(jnp.float32), axis=0)
    return y.reshape(B, S, A_OUT).astype(q.dtype)


def sliding_window_sink_attention(q, k, v, sinks, table):
    B, S = q.shape[:2]
    dist, _ = band_geometry(B_WINDOW - 1)
    bias = rel_bias_block(table, dist, 1, A_HEADS, TOTAL_HEADS)
    o, _ = banded_attention(q, k, v, bias, B_WINDOW - 1, sinks)
    return o.reshape(B, S, B_OUT)


def hybrid_layer(x, ffn1_norm, ffn1_w_gate, ffn1_w_up, ffn1_w_down, mix_norm, w_in, b_in,
                 w_branch_a, w_branch_b, w_out, sinks, rel_bias,
                 ffn2_norm, ffn2_w_gate, ffn2_w_up, ffn2_w_down):
    B, S, _ = x.shape
    h = x + 0.5 * swiglu(rms_norm(x, ffn1_norm), ffn1_w_gate, ffn1_w_up, ffn1_w_down)
    u = rms_norm(h, mix_norm)
    z = u @ w_in + b_in
    aq, ak, av, bq, bk, bv, ga, gb = jnp.split(z, np.cumsum(SPLITS)[:-1].tolist(), axis=-1)
    heads = lambda t: t.reshape(B, S, -1, HEAD_DIM)
    ya = dilated_attention(heads(aq), heads(ak), heads(av), rel_bias)
    yb = sliding_window_sink_attention(heads(bq), heads(bk), heads(bv), sinks, rel_bias)
    merged = jax.nn.sigmoid(ga) * (ya @ w_branch_a) + jax.nn.sigmoid(gb) * (yb @ w_branch_b)
    h = h + merged @ w_out
    h = h + 0.5 * swiglu(rms_norm(h, ffn2_norm), ffn2_w_gate, ffn2_w_up, ffn2_w_down)
    return h


def _fwd_setup_inputs(seed: int = 0) -> dict:
    key = jax.random.key(seed)
    ks = jax.random.split(key, 20)
    L = DEPTH

    def w(k, shape, fan_in):
        return jax.random.normal(k, shape, jnp.float32) * (fan_in ** -0.5)

    def gain(k, shape):
        return 1.0 + 0.01 * jax.random.normal(k, shape, jnp.float32)

    return {
        "x": jax.random.normal(ks[0], (BATCH, SEQ, D_MODEL), jnp.float32),
        "ffn1_norm": gain(ks[1], (L, D_MODEL)),
        "ffn1_w_gate": w(ks[2], (L, D_MODEL, D_FF), D_MODEL),
        "ffn1_w_up": w(ks[3], (L, D_MODEL, D_FF), D_MODEL),
        "ffn1_w_down": w(ks[4], (L, D_FF, D_MODEL), D_FF),
        "mix_norm": gain(ks[5], (L, D_MODEL)),
        "w_in": w(ks[6], (L, D_MODEL, D_IN), D_MODEL),
        "b_in": 0.01 * jax.random.normal(ks[7], (L, D_IN), jnp.float32),
        "w_branch_a": w(ks[8], (L, A_OUT, D_MODEL), A_OUT),
        "w_branch_b": w(ks[9], (L, B_OUT, D_MODEL), B_OUT),
        "w_out": w(ks[10], (L, D_MODEL, D_MODEL), D_MODEL),
        "sinks": 0.5 * jax.random.normal(ks[11], (L, B_Q_HEADS), jnp.float32),
        "rel_bias": 0.5 * jax.random.normal(ks[12], (N_BUCKETS, TOTAL_HEADS), jnp.float32),
        "ffn2_norm": gain(ks[13], (L, D_MODEL)),
        "ffn2_w_gate": w(ks[14], (L, D_MODEL, D_FF), D_MODEL),
        "ffn2_w_up": w(ks[15], (L, D_MODEL, D_FF), D_MODEL),
        "ffn2_w_down": w(ks[16], (L, D_FF, D_MODEL), D_FF),
        "final_norm": gain(ks[17], (D_MODEL,)),
    }


def _fwd_reference(x, ffn1_norm, ffn1_w_gate, ffn1_w_up, ffn1_w_down, mix_norm, w_in, b_in,
              w_branch_a, w_branch_b, w_out, sinks, rel_bias,
              ffn2_norm, ffn2_w_gate, ffn2_w_up, ffn2_w_down, final_norm):
    h = x
    for l in range(DEPTH):
        h = hybrid_layer(h, ffn1_norm[l], ffn1_w_gate[l], ffn1_w_up[l], ffn1_w_down[l], mix_norm[l],
                         w_in[l], b_in[l], w_branch_a[l], w_branch_b[l], w_out[l], sinks[l], rel_bias,
                         ffn2_norm[l], ffn2_w_gate[l], ffn2_w_up[l], ffn2_w_down[l])
    return rms_norm(h, final_norm)


import jax as _jax
import jax.numpy as _jnp

TWIN_FORMAT = 'train_step'
FWD_PARAMS = ['x', 'ffn1_norm', 'ffn1_w_gate', 'ffn1_w_up', 'ffn1_w_down', 'mix_norm', 'w_in', 'b_in', 'w_branch_a', 'w_branch_b', 'w_out', 'sinks', 'rel_bias', 'ffn2_norm', 'ffn2_w_gate', 'ffn2_w_up', 'ffn2_w_down', 'final_norm']
TWIN_WEIGHTS = ['ffn1_norm', 'ffn1_w_gate', 'ffn1_w_up', 'ffn1_w_down', 'mix_norm', 'w_in', 'b_in', 'w_branch_a', 'w_branch_b', 'w_out', 'sinks', 'rel_bias', 'ffn2_norm', 'ffn2_w_gate', 'ffn2_w_up', 'ffn2_w_down', 'final_norm']
TWIN_DIFF_INPUT = 'x'
TWIN_INPUTS = ['x', 'ffn1_norm', 'ffn1_w_gate', 'ffn1_w_up', 'ffn1_w_down', 'mix_norm', 'w_in', 'b_in', 'w_branch_a', 'w_branch_b', 'w_out', 'sinks', 'rel_bias', 'ffn2_norm', 'ffn2_w_gate', 'ffn2_w_up', 'ffn2_w_down', 'final_norm', 'loss_target', 'm_ffn1_norm', 'm_ffn1_w_gate', 'm_ffn1_w_up', 'm_ffn1_w_down', 'm_mix_norm', 'm_w_in', 'm_b_in', 'm_w_branch_a', 'm_w_branch_b', 'm_w_out', 'm_sinks', 'm_rel_bias', 'm_ffn2_norm', 'm_ffn2_w_gate', 'm_ffn2_w_up', 'm_ffn2_w_down', 'm_final_norm', 'v_ffn1_norm', 'v_ffn1_w_gate', 'v_ffn1_w_up', 'v_ffn1_w_down', 'v_mix_norm', 'v_w_in', 'v_b_in', 'v_w_branch_a', 'v_w_branch_b', 'v_w_out', 'v_sinks', 'v_rel_bias', 'v_ffn2_norm', 'v_ffn2_w_gate', 'v_ffn2_w_up', 'v_ffn2_w_down', 'v_final_norm']
TWIN_OUTPUTS = ['loss', 'grad_x', 'grad_ffn1_norm', 'grad_ffn1_w_gate', 'grad_ffn1_w_up', 'grad_ffn1_w_down', 'grad_mix_norm', 'grad_w_in', 'grad_b_in', 'grad_w_branch_a', 'grad_w_branch_b', 'grad_w_out', 'grad_sinks', 'grad_rel_bias', 'grad_ffn2_norm', 'grad_ffn2_w_gate', 'grad_ffn2_w_up', 'grad_ffn2_w_down', 'grad_final_norm', 'delta_ffn1_norm', 'delta_ffn1_w_gate', 'delta_ffn1_w_up', 'delta_ffn1_w_down', 'delta_mix_norm', 'delta_w_in', 'delta_b_in', 'delta_w_branch_a', 'delta_w_branch_b', 'delta_w_out', 'delta_sinks', 'delta_rel_bias', 'delta_ffn2_norm', 'delta_ffn2_w_gate', 'delta_ffn2_w_up', 'delta_ffn2_w_down', 'delta_final_norm', 'new_m_ffn1_norm', 'new_m_ffn1_w_gate', 'new_m_ffn1_w_up', 'new_m_ffn1_w_down', 'new_m_mix_norm', 'new_m_w_in', 'new_m_b_in', 'new_m_w_branch_a', 'new_m_w_branch_b', 'new_m_w_out', 'new_m_sinks', 'new_m_rel_bias', 'new_m_ffn2_norm', 'new_m_ffn2_w_gate', 'new_m_ffn2_w_up', 'new_m_ffn2_w_down', 'new_m_final_norm', 'new_v_ffn1_norm', 'new_v_ffn1_w_gate', 'new_v_ffn1_w_up', 'new_v_ffn1_w_down', 'new_v_mix_norm', 'new_v_w_in', 'new_v_b_in', 'new_v_w_branch_a', 'new_v_w_branch_b', 'new_v_w_out', 'new_v_sinks', 'new_v_rel_bias', 'new_v_ffn2_norm', 'new_v_ffn2_w_gate', 'new_v_ffn2_w_up', 'new_v_ffn2_w_down', 'new_v_final_norm']
TWIN_LEAF_KINDS = {'loss': 'loss', 'grad_x': 'grad_x', 'grad_ffn1_norm': 'grad_w', 'grad_ffn1_w_gate': 'grad_w', 'grad_ffn1_w_up': 'grad_w', 'grad_ffn1_w_down': 'grad_w', 'grad_mix_norm': 'grad_w', 'grad_w_in': 'grad_w', 'grad_b_in': 'grad_w', 'grad_w_branch_a': 'grad_w', 'grad_w_branch_b': 'grad_w', 'grad_w_out': 'grad_w', 'grad_sinks': 'grad_w', 'grad_rel_bias': 'grad_w', 'grad_ffn2_norm': 'grad_w', 'grad_ffn2_w_gate': 'grad_w', 'grad_ffn2_w_up': 'grad_w', 'grad_ffn2_w_down': 'grad_w', 'grad_final_norm': 'grad_w', 'delta_ffn1_norm': 'delta_w', 'delta_ffn1_w_gate': 'delta_w', 'delta_ffn1_w_up': 'delta_w', 'delta_ffn1_w_down': 'delta_w', 'delta_mix_norm': 'delta_w', 'delta_w_in': 'delta_w', 'delta_b_in': 'delta_w', 'delta_w_branch_a': 'delta_w', 'delta_w_branch_b': 'delta_w', 'delta_w_out': 'delta_w', 'delta_sinks': 'delta_w', 'delta_rel_bias': 'delta_w', 'delta_ffn2_norm': 'delta_w', 'delta_ffn2_w_gate': 'delta_w', 'delta_ffn2_w_up': 'delta_w', 'delta_ffn2_w_down': 'delta_w', 'delta_final_norm': 'delta_w', 'new_m_ffn1_norm': 'new_m', 'new_m_ffn1_w_gate': 'new_m', 'new_m_ffn1_w_up': 'new_m', 'new_m_ffn1_w_down': 'new_m', 'new_m_mix_norm': 'new_m', 'new_m_w_in': 'new_m', 'new_m_b_in': 'new_m', 'new_m_w_branch_a': 'new_m', 'new_m_w_branch_b': 'new_m', 'new_m_w_out': 'new_m', 'new_m_sinks': 'new_m', 'new_m_rel_bias': 'new_m', 'new_m_ffn2_norm': 'new_m', 'new_m_ffn2_w_gate': 'new_m', 'new_m_ffn2_w_up': 'new_m', 'new_m_ffn2_w_down': 'new_m', 'new_m_final_norm': 'new_m', 'new_v_ffn1_norm': 'new_v', 'new_v_ffn1_w_gate': 'new_v', 'new_v_ffn1_w_up': 'new_v', 'new_v_ffn1_w_down': 'new_v', 'new_v_mix_norm': 'new_v', 'new_v_w_in': 'new_v', 'new_v_b_in': 'new_v', 'new_v_w_branch_a': 'new_v', 'new_v_w_branch_b': 'new_v', 'new_v_w_out': 'new_v', 'new_v_sinks': 'new_v', 'new_v_rel_bias': 'new_v', 'new_v_ffn2_norm': 'new_v', 'new_v_ffn2_w_gate': 'new_v', 'new_v_ffn2_w_up': 'new_v', 'new_v_ffn2_w_down': 'new_v', 'new_v_final_norm': 'new_v'}


def _forward(args):
    return _fwd_reference(*[args[k] for k in FWD_PARAMS])


def _output_shape():
    out = _jax.eval_shape(lambda: _forward(_fwd_setup_inputs(0)))
    return out.shape, out.dtype

N_MICROBATCH = 1
ADAM_LR = 0.001
ADAM_B1 = 0.9
ADAM_B2 = 0.999
ADAM_EPS = 1e-08
ADAM_WD = 0.01
ADAM_STEP = 10
PER_EXAMPLE_BATCH_AXIS = {'x': 0, 'loss_target': 0}
SHARED_INPUTS = []
_WEIGHT_DTYPES = {'ffn1_norm': _jnp.float32, 'ffn1_w_gate': _jnp.float32, 'ffn1_w_up': _jnp.float32, 'ffn1_w_down': _jnp.float32, 'mix_norm': _jnp.float32, 'w_in': _jnp.float32, 'b_in': _jnp.float32, 'w_branch_a': _jnp.float32, 'w_branch_b': _jnp.float32, 'w_out': _jnp.float32, 'sinks': _jnp.float32, 'rel_bias': _jnp.float32, 'ffn2_norm': _jnp.float32, 'ffn2_w_gate': _jnp.float32, 'ffn2_w_up': _jnp.float32, 'ffn2_w_down': _jnp.float32, 'final_norm': _jnp.float32}
MOMENT_SCALE = {'ffn1_norm': 1.236235e-01, 'ffn1_w_gate': 4.717447e-02, 'ffn1_w_up': 4.561864e-02, 'ffn1_w_down': 7.567674e-02, 'mix_norm': 6.542182e-02, 'w_in': 2.868342e-02, 'b_in': 7.900715e-02, 'w_branch_a': 2.676285e-02, 'w_branch_b': 2.920754e-02, 'w_out': 3.932685e-02, 'sinks': 2.000145e-02, 'rel_bias': 3.814828e-02, 'ffn2_norm': 1.017682e-01, 'ffn2_w_gate': 4.297533e-02, 'ffn2_w_up': 4.169186e-02, 'ffn2_w_down': 6.928271e-02, 'final_norm': 6.389807e+01}


def _to_microbatches(a, axis):
    t = _jnp.moveaxis(a, axis, 0)
    t = t.reshape((N_MICROBATCH, t.shape[0] // N_MICROBATCH) + t.shape[1:])
    return _jnp.moveaxis(t, 1, axis + 1)


def setup_inputs(seed: int = 0) -> dict:
    inp = _fwd_setup_inputs(seed)
    key = _jax.random.fold_in(_jax.random.key(seed), 7919)
    shape, _ = _output_shape()
    out = dict(inp)
    out["loss_target"] = _jax.random.normal(_jax.random.fold_in(key, 0), shape, _jnp.float32)
    for i, name in enumerate(TWIN_WEIGHTS):
        w = inp[name].astype(_jnp.float32)
        if MOMENT_SCALE is None:
            s = _jnp.sqrt(_jnp.mean(_jnp.square(w)) + 1e-30)
        else:
            s = MOMENT_SCALE[name]
        km, kv = _jax.random.split(_jax.random.fold_in(key, i + 1))
        out[name] = w
        out["m_" + name] = s * _jax.random.normal(km, w.shape, _jnp.float32)
        out["v_" + name] = (s * s) * _jax.random.uniform(kv, w.shape, _jnp.float32, 0.5, 1.5)
    if N_MICROBATCH > 1:
        for name, axis in PER_EXAMPLE_BATCH_AXIS.items():
            out[name] = _to_microbatches(out[name], axis)
    return {'x': out['x'], 'ffn1_norm': out['ffn1_norm'], 'ffn1_w_gate': out['ffn1_w_gate'], 'ffn1_w_up': out['ffn1_w_up'], 'ffn1_w_down': out['ffn1_w_down'], 'mix_norm': out['mix_norm'], 'w_in': out['w_in'], 'b_in': out['b_in'], 'w_branch_a': out['w_branch_a'], 'w_branch_b': out['w_branch_b'], 'w_out': out['w_out'], 'sinks': out['sinks'], 'rel_bias': out['rel_bias'], 'ffn2_norm': out['ffn2_norm'], 'ffn2_w_gate': out['ffn2_w_gate'], 'ffn2_w_up': out['ffn2_w_up'], 'ffn2_w_down': out['ffn2_w_down'], 'final_norm': out['final_norm'], 'loss_target': out['loss_target'], 'm_ffn1_norm': out['m_ffn1_norm'], 'm_ffn1_w_gate': out['m_ffn1_w_gate'], 'm_ffn1_w_up': out['m_ffn1_w_up'], 'm_ffn1_w_down': out['m_ffn1_w_down'], 'm_mix_norm': out['m_mix_norm'], 'm_w_in': out['m_w_in'], 'm_b_in': out['m_b_in'], 'm_w_branch_a': out['m_w_branch_a'], 'm_w_branch_b': out['m_w_branch_b'], 'm_w_out': out['m_w_out'], 'm_sinks': out['m_sinks'], 'm_rel_bias': out['m_rel_bias'], 'm_ffn2_norm': out['m_ffn2_norm'], 'm_ffn2_w_gate': out['m_ffn2_w_gate'], 'm_ffn2_w_up': out['m_ffn2_w_up'], 'm_ffn2_w_down': out['m_ffn2_w_down'], 'm_final_norm': out['m_final_norm'], 'v_ffn1_norm': out['v_ffn1_norm'], 'v_ffn1_w_gate': out['v_ffn1_w_gate'], 'v_ffn1_w_up': out['v_ffn1_w_up'], 'v_ffn1_w_down': out['v_ffn1_w_down'], 'v_mix_norm': out['v_mix_norm'], 'v_w_in': out['v_w_in'], 'v_b_in': out['v_b_in'], 'v_w_branch_a': out['v_w_branch_a'], 'v_w_branch_b': out['v_w_branch_b'], 'v_w_out': out['v_w_out'], 'v_sinks': out['v_sinks'], 'v_rel_bias': out['v_rel_bias'], 'v_ffn2_norm': out['v_ffn2_norm'], 'v_ffn2_w_gate': out['v_ffn2_w_gate'], 'v_ffn2_w_up': out['v_ffn2_w_up'], 'v_ffn2_w_down': out['v_ffn2_w_down'], 'v_final_norm': out['v_final_norm']}


def _loss(weights, diff, rest, loss_target):
    with _jax.named_scope("forward"):
        args = {**rest, TWIN_DIFF_INPUT: diff, **{k: w.astype(_WEIGHT_DTYPES[k]) for k, w in weights.items()}}
        y = _forward(args)
    with _jax.named_scope("loss_head"):
        err = _jnp.square(y.astype(_jnp.float32) - loss_target)
        return 0.5 * _jnp.sum(_jnp.mean(err, axis=-1)) if err.ndim else 0.5 * err


def _adamw(w, g, m, v):
    m = ADAM_B1 * m + (1.0 - ADAM_B1) * g
    v = ADAM_B2 * v + (1.0 - ADAM_B2) * _jnp.square(g)
    m_hat = m / (1.0 - ADAM_B1 ** ADAM_STEP)
    v_hat = v / (1.0 - ADAM_B2 ** ADAM_STEP)
    delta = -ADAM_LR * (m_hat / (_jnp.sqrt(v_hat) + ADAM_EPS) + ADAM_WD * w)
    return delta, m, v


def reference(x, ffn1_norm, ffn1_w_gate, ffn1_w_up, ffn1_w_down, mix_norm, w_in, b_in, w_branch_a, w_branch_b, w_out, sinks, rel_bias, ffn2_norm, ffn2_w_gate, ffn2_w_up, ffn2_w_down, final_norm, loss_target, m_ffn1_norm, m_ffn1_w_gate, m_ffn1_w_up, m_ffn1_w_down, m_mix_norm, m_w_in, m_b_in, m_w_branch_a, m_w_branch_b, m_w_out, m_sinks, m_rel_bias, m_ffn2_norm, m_ffn2_w_gate, m_ffn2_w_up, m_ffn2_w_down, m_final_norm, v_ffn1_norm, v_ffn1_w_gate, v_ffn1_w_up, v_ffn1_w_down, v_mix_norm, v_w_in, v_b_in, v_w_branch_a, v_w_branch_b, v_w_out, v_sinks, v_rel_bias, v_ffn2_norm, v_ffn2_w_gate, v_ffn2_w_up, v_ffn2_w_down, v_final_norm):
    given = dict(x=x, ffn1_norm=ffn1_norm, ffn1_w_gate=ffn1_w_gate, ffn1_w_up=ffn1_w_up, ffn1_w_down=ffn1_w_down, mix_norm=mix_norm, w_in=w_in, b_in=b_in, w_branch_a=w_branch_a, w_branch_b=w_branch_b, w_out=w_out, sinks=sinks, rel_bias=rel_bias, ffn2_norm=ffn2_norm, ffn2_w_gate=ffn2_w_gate, ffn2_w_up=ffn2_w_up, ffn2_w_down=ffn2_w_down, final_norm=final_norm, loss_target=loss_target, m_ffn1_norm=m_ffn1_norm, m_ffn1_w_gate=m_ffn1_w_gate, m_ffn1_w_up=m_ffn1_w_up, m_ffn1_w_down=m_ffn1_w_down, m_mix_norm=m_mix_norm, m_w_in=m_w_in, m_b_in=m_b_in, m_w_branch_a=m_w_branch_a, m_w_branch_b=m_w_branch_b, m_w_out=m_w_out, m_sinks=m_sinks, m_rel_bias=m_rel_bias, m_ffn2_norm=m_ffn2_norm, m_ffn2_w_gate=m_ffn2_w_gate, m_ffn2_w_up=m_ffn2_w_up, m_ffn2_w_down=m_ffn2_w_down, m_final_norm=m_final_norm, v_ffn1_norm=v_ffn1_norm, v_ffn1_w_gate=v_ffn1_w_gate, v_ffn1_w_up=v_ffn1_w_up, v_ffn1_w_down=v_ffn1_w_down, v_mix_norm=v_mix_norm, v_w_in=v_w_in, v_b_in=v_b_in, v_w_branch_a=v_w_branch_a, v_w_branch_b=v_w_branch_b, v_w_out=v_w_out, v_sinks=v_sinks, v_rel_bias=v_rel_bias, v_ffn2_norm=v_ffn2_norm, v_ffn2_w_gate=v_ffn2_w_gate, v_ffn2_w_up=v_ffn2_w_up, v_ffn2_w_down=v_ffn2_w_down, v_final_norm=v_final_norm)
    weights = {n: given[n] for n in TWIN_WEIGHTS}
    shared = {n: given[n] for n in SHARED_INPUTS}
    per_example = {n: given[n] for n in ['x']}
    grad_fn = _jax.value_and_grad(_loss, argnums=(0, 1))

    def one_microbatch(ex, loss_target):
        ex = dict(ex)
        diff = ex.pop(TWIN_DIFF_INPUT)
        return grad_fn(weights, diff, {**shared, **ex}, loss_target)

    if N_MICROBATCH == 1:
        loss, (grad_w, grad_x) = one_microbatch(per_example, given["loss_target"])
    else:
        def body(carry, xs):
            loss_sum, grad_sum = carry
            l_k, (gw_k, gx_k) = one_microbatch(xs[0], xs[1])
            with _jax.named_scope("update"):
                return (loss_sum + l_k, _jax.tree.map(_jnp.add, grad_sum, gw_k)), gx_k

        init = (_jnp.zeros((), _jnp.float32), _jax.tree.map(_jnp.zeros_like, weights))
        (loss, grad_w), grad_x = _jax.lax.scan(body, init, (per_example, given["loss_target"]))
    with _jax.named_scope("update"):
        delta_w, new_m, new_v = {}, {}, {}
        for n in TWIN_WEIGHTS:
            delta_w[n], new_m[n], new_v[n] = _adamw(weights[n], grad_w[n], given["m_" + n], given["v_" + n])
    return (loss, grad_x, *[grad_w[n] for n in TWIN_WEIGHTS], *[delta_w[n] for n in TWIN_WEIGHTS],
            *[new_m[n] for n in TWIN_WEIGHTS], *[new_v[n] for n in TWIN_WEIGHTS])
```

```python
import functools
import math

import numpy as np
import jax
import jax.numpy as jnp
from jax import lax
from jax.experimental import pallas as pl
from jax.experimental.pallas import tpu as pltpu

D_MODEL = 1024
D_FF = 2816
HEAD_DIM = 64
BLOCK = 128
N_BUCKETS = 32
MAX_DISTANCE = 2048
A_HEADS = 12
TOTAL_HEADS = 20
DIL_GROUPS = ((128, 1), (512, 4), (2048, 16))
B_WINDOW = 128
QKV_W = 3072
GATE_W = 2048
D_IN = QKV_W + GATE_W
EPS = 1e-6
NEG = -1e30
N_DEV = 8
N_CHIP = 4
ADAM_LR, ADAM_B1, ADAM_B2, ADAM_EPS, ADAM_WD, ADAM_STEP = 0.001, 0.9, 0.999, 1e-08, 0.01, 10
VMEM_LIMIT = 56 * 1024 * 1024
MESH = pl.DeviceIdType.MESH
BF16 = jnp.bfloat16
F32 = jnp.float32
ANY = pl.BlockSpec(memory_space=pl.ANY)


def _params(*sem):
    return pltpu.CompilerParams(dimension_semantics=sem, vmem_limit_bytes=VMEM_LIMIT)


def _nt(a, b):
    return lax.dot_general(a, b, (((1,), (1,)), ((), ())), preferred_element_type=F32)


def _nn(a, b):
    return lax.dot_general(a, b, (((1,), (0,)), ((), ())), preferred_element_type=F32)


def _tn(a, b):
    return lax.dot_general(a, b, (((0,), (0,)), ((), ())), preferred_element_type=F32)


def _rms(x, g):
    r = lax.rsqrt(jnp.mean(x * x, axis=-1, keepdims=True) + EPS)
    return x * r, r


def _rms_bwd(dn, xhat, r, g):
    dg = jnp.sum(dn * xhat, axis=0, keepdims=True)
    dxh = dn * g
    dx = r * (dxh - xhat * jnp.mean(dxh * xhat, axis=-1, keepdims=True))
    return dx, dg


def _ffn_fwd(x, g, wg_t, wu_t, wd, name):
    t = x.shape[0]
    tm, fc = 1024, 256
    nj = D_FF // fc

    def body(x_ref, g_ref, wg_ref, wu_ref, wd_ref, h_ref, n_ref, acc_ref):
        j = pl.program_id(1)

        @pl.when(j == 0)
        def _():
            xhat, _ = _rms(x_ref[...], g_ref[...])
            n_ref[...] = (xhat * g_ref[...]).astype(BF16)
            acc_ref[...] = jnp.zeros_like(acc_ref)

        n = n_ref[...]
        a = _nt(n, wg_ref[...])
        b = _nt(n, wu_ref[...])
        hff = (a * jax.nn.sigmoid(a) * b).astype(BF16)
        acc_ref[...] += _nn(hff, wd_ref[...])

        @pl.when(j == nj - 1)
        def _():
            h_ref[...] = x_ref[...] + 0.5 * acc_ref[...]

    row = pl.BlockSpec((tm, D_MODEL), lambda i, j: (i, 0))
    wblk = pl.BlockSpec((fc, D_MODEL), lambda i, j: (j, 0))
    return pl.pallas_call(
        body, name=name, grid=(t // tm, nj),
        in_specs=[row, pl.BlockSpec((1, D_MODEL), lambda i, j: (0, 0)), wblk, wblk, wblk],
        out_specs=[row, row],
        out_shape=[jax.ShapeDtypeStruct((t, D_MODEL), F32), jax.ShapeDtypeStruct((t, D_MODEL), BF16)],
        scratch_shapes=[pltpu.VMEM((tm, D_MODEL), F32)],
        compiler_params=_params("parallel", "arbitrary"),
    )(x, g, wg_t, wu_t, wd)


def _ffn_bwd(x, n, g, dh, wg_t, wu_t, wd, name):
    t = x.shape[0]
    tm, fc = 512, 256
    nj = D_FF // fc

    def body(x_ref, n_ref, g_ref, dh_ref, wg_ref, wu_ref, wd_ref,
             dx_ref, dg_ref, hff_ref, da_ref, db_ref, dhh_ref, acc_ref):
        i, j = pl.program_id(0), pl.program_id(1)

        @pl.when(j == 0)
        def _():
            dhh_ref[...] = (0.5 * dh_ref[...]).astype(BF16)
            acc_ref[...] = jnp.zeros_like(acc_ref)

        @pl.when((i == 0) & (j == 0))
        def _():
            dg_ref[...] = jnp.zeros_like(dg_ref)

        nb = n_ref[...]
        a = _nt(nb, wg_ref[...])
        b = _nt(nb, wu_ref[...])
        s = jax.nn.sigmoid(a)
        silu = a * s
        dhff = _nt(dhh_ref[...], wd_ref[...])
        da = (dhff * b * (s * (1.0 + a * (1.0 - s)))).astype(BF16)
        db = (dhff * silu).astype(BF16)
        hff_ref[...] = (silu * b).astype(BF16)
        da_ref[...] = da
        db_ref[...] = db
        acc_ref[...] += _nn(da, wg_ref[...]) + _nn(db, wu_ref[...])

        @pl.when(j == nj - 1)
        def _():
            xhat, r = _rms(x_ref[...], g_ref[...])
            dx, dg = _rms_bwd(acc_ref[...], xhat, r, g_ref[...])
            dx_ref[...] = dh_ref[...] + dx
            dg_ref[...] += dg

    row = lambda: pl.BlockSpec((tm, D_MODEL), lambda i, j: (i, 0))
    vec = pl.BlockSpec((1, D_MODEL), lambda i, j: (0, 0))
    wblk = pl.BlockSpec((fc, D_MODEL), lambda i, j: (j, 0))
    hid = pl.BlockSpec((tm, fc), lambda i, j: (i, j))
    return pl.pallas_call(
        body, name=name, grid=(t // tm, nj),
        in_specs=[row(), row(), vec, row(), wblk, wblk, wblk],
        out_specs=[row(), vec, hid, hid, hid, row()],
        out_shape=[jax.ShapeDtypeStruct((t, D_MODEL), F32), jax.ShapeDtypeStruct((1, D_MODEL), F32),
                   jax.ShapeDtypeStruct((t, D_FF), BF16), jax.ShapeDtypeStruct((t, D_FF), BF16),
                   jax.ShapeDtypeStruct((t, D_FF), BF16), jax.ShapeDtypeStruct((t, D_MODEL), BF16)],
        scratch_shapes=[pltpu.VMEM((tm, D_MODEL), F32)],
        compiler_params=_params("arbitrary", "arbitrary"),
    )(x, n, g, dh, wg_t, wu_t, wd)


def _tn_matmul(a, b, rc, name):
    t, r = a.shape
    c = b.shape[1]
    tk = 1024

    def body(a_ref, b_ref, o_ref):
        @pl.when(pl.program_id(1) == 0)
        def _():
            o_ref[...] = jnp.zeros_like(o_ref)

        o_ref[...] += _tn(a_ref[...].astype(BF16), b_ref[...].astype(BF16))

    return pl.pallas_call(
        body, name=name, grid=(r // rc, t // tk),
        in_specs=[pl.BlockSpec((tk, rc), lambda i, k: (k, i)), pl.BlockSpec((tk, c), lambda i, k: (k, 0))],
        out_specs=pl.BlockSpec((rc, c), lambda i, k: (i, 0)),
        out_shape=jax.ShapeDtypeStruct((r, c), F32),
        compiler_params=_params("parallel", "arbitrary"),
    )(a, b)


PIECE_W = (256,) * 9 + (512, 128, 128, GATE_W)


def _inproj_fwd(h, g, w_t, b_in, name):
    t = h.shape[0]
    tm, nc = 512, 512

    def body(h_ref, g_ref, w_ref, b_ref, u_ref, zq_ref, zg_ref):
        xhat, _ = _rms(h_ref[...], g_ref[...])
        u = (xhat * g_ref[...]).astype(BF16)
        u_ref[...] = u
        for c in range(D_IN // nc):
            z = _nt(u, w_ref[c * nc:(c + 1) * nc, :]) + b_ref[:, c * nc:(c + 1) * nc]
            if c < QKV_W // nc:
                zq_ref[:, c * nc:(c + 1) * nc] = z.astype(BF16)
            else:
                zg_ref[:, c * nc - QKV_W:(c + 1) * nc - QKV_W] = z

    row = lambda w: pl.BlockSpec((tm, w), lambda i: (i, 0))
    full = lambda a: pl.BlockSpec(a.shape, lambda i: (0, 0))
    return pl.pallas_call(
        body, name=name, grid=(t // tm,),
        in_specs=[row(D_MODEL), full(g), full(w_t), full(b_in)],
        out_specs=[row(D_MODEL), row(QKV_W), row(GATE_W)],
        out_shape=[jax.ShapeDtypeStruct((t, D_MODEL), BF16), jax.ShapeDtypeStruct((t, QKV_W), BF16),
                   jax.ShapeDtypeStruct((t, GATE_W), F32)],
        compiler_params=_params("parallel"),
    )(h, g, w_t, b_in)


def _inproj_bwd(pieces, h, g, dh_res, w_t, name):
    t = h.shape[0]
    tm = 256
    npiece = len(PIECE_W)
    offs = np.concatenate([[0], np.cumsum(PIECE_W)]).tolist()

    def body(*refs):
        p_refs = refs[:npiece]
        h_ref, g_ref, dhr_ref, w_ref, dh_ref, dz_ref, db_ref, dg_ref = refs[npiece:]
        i = pl.program_id(0)

        @pl.when(i == 0)
        def _():
            db_ref[...] = jnp.zeros_like(db_ref)
            dg_ref[...] = jnp.zeros_like(dg_ref)

        du = jnp.zeros((tm, D_MODEL), F32)
        for k in range(npiece):
            o, w = offs[k], PIECE_W[k]
            for c0 in range(0, w, 512):
                cw = min(512, w - c0)
                pz = p_refs[k][:, c0:c0 + cw]
                dz_ref[:, o + c0:o + c0 + cw] = pz
                db_ref[:, o + c0:o + c0 + cw] += jnp.sum(pz.astype(F32), axis=0, keepdims=True)
                du = du + _nn(pz, w_ref[o + c0:o + c0 + cw, :])
        xhat, r = _rms(h_ref[...], g_ref[...])
        dx, dg = _rms_bwd(du, xhat, r, g_ref[...])
        dh_ref[...] = dhr_ref[...] + dx
        dg_ref[...] += dg

    row = lambda w: pl.BlockSpec((tm, w), lambda i: (i, 0))
    full = lambda shp: pl.BlockSpec(shp, lambda i: (0, 0))
    return pl.pallas_call(
        body, name=name, grid=(t // tm,),
        in_specs=[row(w) for w in PIECE_W] + [row(D_MODEL), full((1, D_MODEL)), row(D_MODEL), full(w_t.shape)],
        out_specs=[row(D_MODEL), row(D_IN), full((1, D_IN)), full((1, D_MODEL))],
        out_shape=[jax.ShapeDtypeStruct((t, D_MODEL), F32), jax.ShapeDtypeStruct((t, D_IN), BF16),
                   jax.ShapeDtypeStruct((1, D_IN), F32), jax.ShapeDtypeStruct((1, D_MODEL), F32)],
        compiler_params=_params("arbitrary"),
    )(*pieces, h, g, dh_res, w_t)


def _t5_bucket(dist):
    max_exact = N_BUCKETS // 2
    n = jnp.maximum(dist, 0)
    nf = jnp.maximum(n, 1).astype(jnp.float32)
    large = max_exact + (jnp.log(nf / max_exact) / math.log(MAX_DISTANCE / max_exact)
                         * (N_BUCKETS - max_exact)).astype(jnp.int32)
    large = jnp.minimum(large, N_BUCKETS - 1)
    return jnp.where(n < max_exact, n, large)


ATT_CFG = ((1, 128, 0, 4), (4, 128, 4, 4), (16, 128, 8, 4), (1, B_WINDOW - 1, A_HEADS, 8))


def _bucket_tiles():
    qi = jnp.arange(BLOCK)[:, None]
    ki = jnp.arange(2 * BLOCK)[None, :]
    dist = qi + BLOCK - ki
    return jnp.stack([_t5_bucket(dist * cfg[0]) for cfg in ATT_CFG]).astype(jnp.int32)


def _band(max_steps):
    row = lax.broadcasted_iota(jnp.int32, (BLOCK, 2 * BLOCK), 0)
    col = lax.broadcasted_iota(jnp.int32, (BLOCK, 2 * BLOCK), 1)
    dist = row + BLOCK - col
    return (dist >= 0) & (dist <= max_steps)


def _bias_build(table, buckets):
    def body(tab_ref, bt_ref, out_ref):
        for ci, (_, max_steps, h0, nh) in enumerate(ATT_CFG):
            bt = bt_ref[ci]
            band = _band(max_steps)
            for h in range(h0, h0 + nh):
                acc = lax.fori_loop(0, N_BUCKETS, lambda b, acc: jnp.where(bt == b, tab_ref[b, h], acc),
                                    jnp.zeros((BLOCK, 2 * BLOCK), F32))
                out_ref[h] = jnp.where(band, acc, NEG)

    return pl.pallas_call(
        body, name="bias_build",
        in_specs=[pl.BlockSpec(memory_space=pltpu.SMEM), pl.BlockSpec(memory_space=pltpu.VMEM)],
        out_specs=pl.BlockSpec(memory_space=pltpu.VMEM),
        out_shape=jax.ShapeDtypeStruct((TOTAL_HEADS, BLOCK, 2 * BLOCK), F32),
    )(table, buckets)


def _bias_reduce(dbias, buckets, dsink_rows):
    def body(db_ref, bt_ref, ds_ref, out_ref, sink_ref):
        ri = lax.broadcasted_iota(jnp.int32, (N_BUCKETS, 128), 0)
        ci = lax.broadcasted_iota(jnp.int32, (N_BUCKETS, 128), 1)

        def per_bucket(b, acc):
            for cfg_i, (_, _, h0, nh) in enumerate(ATT_CFG):
                hit = bt_ref[cfg_i] == b
                for h in range(h0, h0 + nh):
                    val = jnp.sum(jnp.where(hit, db_ref[h], 0.0))
                    acc = jnp.where((ri == b) & (ci == h), val, acc)
            return acc

        out_ref[...] = lax.fori_loop(0, N_BUCKETS, per_bucket, jnp.zeros((N_BUCKETS, 128), F32))
        sink_ref[...] = jnp.broadcast_to(jnp.sum(ds_ref[...], axis=0, keepdims=True), (8, 128))

    return pl.pallas_call(
        body, name="bias_reduce",
        in_specs=[pl.BlockSpec(memory_space=pltpu.VMEM)] * 3,
        out_specs=[pl.BlockSpec(memory_space=pltpu.VMEM)] * 2,
        out_shape=[jax.ShapeDtypeStruct((N_BUCKETS, 128), F32), jax.ShapeDtypeStruct((8, 128), F32)],
    )(dbias, buckets, dsink_rows)


class _Att:
    def __init__(self, cfg_i):
        stride, _, h0, nh = ATT_CFG[cfg_i]
        self.d = stride if cfg_i < 3 else 1
        self.h0, self.nh = h0, nh
        if cfg_i < 3:
            self.nq, self.wkv = 1, 256
            self.q_unit = [cfg_i]
            self.k_unit, self.v_unit = 3 + cfg_i, 6 + cfg_i
            self.heads = [(0, 64 * i, 64 * i) for i in range(4)]
            self.sinks = False
        else:
            self.nq, self.wkv = 2, 128
            self.q_unit = [9, 10]
            self.k_unit, self.v_unit = 22, 23
            self.heads = [(qh // 4, 64 * (qh % 4), 64 * (qh // 4)) for qh in range(8)]
            self.sinks = True
        self.wq = 256 * self.nq


def _att_in_specs(cf, nb):
    uq, ukv = QKV_W // 256, QKV_W // cf.wkv
    specs = [pl.BlockSpec((None, BLOCK, 256), functools.partial(lambda b, r, j, u: (b, j, r * uq + u), u=u))
             for u in cf.q_unit]
    for unit in (cf.k_unit, cf.v_unit):
        specs.append(pl.BlockSpec((None, BLOCK, cf.wkv),
                                  functools.partial(lambda b, r, j, u: (b, jnp.maximum(j - 1, 0), r * ukv + u), u=unit)))
        specs.append(pl.BlockSpec((None, BLOCK, cf.wkv),
                                  functools.partial(lambda b, r, j, u: (b, j, r * ukv + u), u=unit)))
    return specs


def _att_logits(cf, hi, q, kh, bias_ref, first):
    s = _nt(q, kh) * (HEAD_DIM ** -0.5) + bias_ref[hi]
    return jnp.where(first, NEG, s)


def _attn_fwd(cf, zf, bias, sinks, name):
    bsz, l, _ = zf.shape
    nb = l // BLOCK

    def body(*refs):
        q_refs = refs[:cf.nq]
        kp_ref, kc_ref, vp_ref, vc_ref, bias_ref = refs[cf.nq:cf.nq + 5]
        rest = refs[cf.nq + 5:]
        sink_ref = rest[0] if cf.sinks else None
        o_ref, lse_ref = rest[-2:]
        j = pl.program_id(2)
        k = jnp.concatenate([kp_ref[...], kc_ref[...]], axis=0)
        v = jnp.concatenate([vp_ref[...], vc_ref[...]], axis=0)
        col = lax.broadcasted_iota(jnp.int32, (BLOCK, 2 * BLOCK), 1)
        first = (j == 0) & (col < BLOCK)
        for hi, (qb, qo, ko) in enumerate(cf.heads):
            s = _att_logits(cf, hi, q_refs[qb][:, qo:qo + HEAD_DIM], k[:, ko:ko + HEAD_DIM], bias_ref, first)
            m = jnp.max(s, axis=-1, keepdims=True)
            if cf.sinks:
                sk = sink_ref[hi:hi + 1, 0:1]
                m = jnp.maximum(m, sk)
            p = jnp.exp(s - m)
            den = jnp.sum(p, axis=-1, keepdims=True)
            if cf.sinks:
                den = den + jnp.exp(sk - m)
            o = _nn(p.astype(BF16), v[:, ko:ko + HEAD_DIM]) / den
            c0 = 256 * qb + qo
            o_ref[:, c0:c0 + HEAD_DIM] = o
            lse_ref[:, c0:c0 + HEAD_DIM] = jnp.broadcast_to(m + jnp.log(den), (BLOCK, HEAD_DIM))

    in_specs = _att_in_specs(cf, nb) + [pl.BlockSpec((cf.nh, BLOCK, 2 * BLOCK), lambda b, r, j: (0, 0, 0))]
    args = [zf] * (cf.nq + 4) + [bias]
    if cf.sinks:
        in_specs.append(pl.BlockSpec((cf.nh, 128), lambda b, r, j: (0, 0)))
        args.append(sinks)
    out = pl.BlockSpec((None, BLOCK, cf.wq), lambda b, r, j: (b, j, r))
    shape = jax.ShapeDtypeStruct((bsz, l, cf.d * cf.wq), F32)
    return pl.pallas_call(
        body, name=name, grid=(bsz, cf.d, nb), in_specs=in_specs, out_specs=[out, out], out_shape=[shape, shape],
        compiler_params=_params("parallel", "parallel", "arbitrary"),
    )(*args)


def _attn_bwd(cf, zf, bias, sinks, lse, lse_tot, dy, y, name):
    bsz, l, _ = zf.shape
    nb = l // BLOCK
    kv_offs = sorted({ko for _, _, ko in cf.heads})

    def body(*refs):
        q_refs = refs[:cf.nq]
        kp_ref, kc_ref, vp_ref, vc_ref, bias_ref = refs[cf.nq:cf.nq + 5]
        rest = list(refs[cf.nq + 5:])
        sink_ref = rest.pop(0) if cf.sinks else None
        lse_ref, lt_ref, dy_ref, y_ref = rest[:4]
        outs = rest[4:]
        dq_ref, dk_ref, dv_ref, dbias_ref = outs[:4]
        dsink_ref = outs[4] if cf.sinks else None
        dk_acc, dv_acc = outs[-2:]
        b, r, j = pl.program_id(0), pl.program_id(1), pl.program_id(2)

        @pl.when((b == 0) & (r == 0) & (j == 0))
        def _():
            dbias_ref[...] = jnp.zeros_like(dbias_ref)
            if cf.sinks:
                dsink_ref[...] = jnp.zeros_like(dsink_ref)

        @pl.when(j == 0)
        def _():
            dk_acc[...] = jnp.zeros_like(dk_acc)
            dv_acc[...] = jnp.zeros_like(dv_acc)

        k = jnp.concatenate([kp_ref[...], kc_ref[...]], axis=0)
        v = jnp.concatenate([vp_ref[...], vc_ref[...]], axis=0)
        col = lax.broadcasted_iota(jnp.int32, (BLOCK, 2 * BLOCK), 1)
        first = (j == 0) & (col < BLOCK)
        dk_parts = {ko: None for ko in kv_offs}
        dv_parts = {ko: None for ko in kv_offs}
        for hi, (qb, qo, ko) in enumerate(cf.heads):
            c0 = 256 * qb + qo
            q = q_refs[qb][:, qo:qo + HEAD_DIM]
            kh, vh = k[:, ko:ko + HEAD_DIM], v[:, ko:ko + HEAD_DIM]
            s = _att_logits(cf, hi, q, kh, bias_ref, first)
            lse_h = lse_ref[:, c0:c0 + 1]
            alpha = jnp.exp(lse_h - lt_ref[:, c0:c0 + 1])
            pa = alpha * jnp.exp(s - lse_h)
            dyh = dy_ref[:, c0:c0 + HEAD_DIM]
            e = jnp.sum(dyh * y_ref[:, c0:c0 + HEAD_DIM], axis=-1, keepdims=True)
            dyb = dyh.astype(BF16)
            ds = pa * (_nt(dyb, vh) - e)
            dbias_ref[hi] += ds
            if cf.sinks:
                dsink_ref[:, hi:hi + 1] += -(jnp.exp(sink_ref[hi:hi + 1, 0:1] - lse_h) * e)
            dsb = ds.astype(BF16)
            dq_ref[:, c0:c0 + HEAD_DIM] = (_nn(dsb, kh) * (HEAD_DIM ** -0.5)).astype(BF16)
            dk_h = _tn(dsb, q) * (HEAD_DIM ** -0.5)
            dv_h = _tn(pa.astype(BF16), dyb)
            dk_parts[ko] = dk_h if dk_parts[ko] is None else dk_parts[ko] + dk_h
            dv_parts[ko] = dv_h if dv_parts[ko] is None else dv_parts[ko] + dv_h
        cur = pl.ds(pl.multiple_of(j * BLOCK, BLOCK), BLOCK)
        for ko in kv_offs:
            dk_acc[cur, ko:ko + HEAD_DIM] += dk_parts[ko][BLOCK:]
            dv_acc[cur, ko:ko + HEAD_DIM] += dv_parts[ko][BLOCK:]

        @pl.when(j > 0)
        def _():
            prev = pl.ds(pl.multiple_of((j - 1) * BLOCK, BLOCK), BLOCK)
            for ko in kv_offs:
                dk_acc[prev, ko:ko + HEAD_DIM] += dk_parts[ko][:BLOCK]
                dv_acc[prev, ko:ko + HEAD_DIM] += dv_parts[ko][:BLOCK]

        @pl.when(j == nb - 1)
        def _():
            dk_ref[...] = dk_acc[...].astype(BF16)
            dv_ref[...] = dv_acc[...].astype(BF16)

    in_specs = _att_in_specs(cf, nb) + [pl.BlockSpec((cf.nh, BLOCK, 2 * BLOCK), lambda b, r, j: (0, 0, 0))]
    args = [zf] * (cf.nq + 4) + [bias]
    if cf.sinks:
        in_specs.append(pl.BlockSpec((cf.nh, 128), lambda b, r, j: (0, 0)))
        args.append(sinks)
    tok = pl.BlockSpec((None, BLOCK, cf.wq), lambda b, r, j: (b, j, r))
    in_specs += [tok] * 4
    args += [lse, lse_tot, dy, y]
    seq = pl.BlockSpec((None, l, cf.wkv), lambda b, r, j: (b, 0, r))
    out_specs = [tok, seq, seq, pl.BlockSpec((cf.nh, BLOCK, 2 * BLOCK), lambda b, r, j: (0, 0, 0))]
    out_shape = [jax.ShapeDtypeStruct((bsz, l, cf.d * cf.wq), BF16),
                 jax.ShapeDtypeStruct((bsz, l, cf.d * cf.wkv), BF16),
                 jax.ShapeDtypeStruct((bsz, l, cf.d * cf.wkv), BF16),
                 jax.ShapeDtypeStruct((cf.nh, BLOCK, 2 * BLOCK), F32)]
    if cf.sinks:
        out_specs.append(pl.BlockSpec((BLOCK, 128), lambda b, r, j: (0, 0)))
        out_shape.append(jax.ShapeDtypeStruct((BLOCK, 128), F32))
    return pl.pallas_call(
        body, name=name, grid=(bsz, cf.d, nb), in_specs=in_specs, out_specs=out_specs, out_shape=out_shape,
        scratch_shapes=[pltpu.VMEM((l, cf.wkv), F32), pltpu.VMEM((l, cf.wkv), F32)],
        compiler_params=_params("arbitrary", "arbitrary", "arbitrary"),
    )(*args)


def _merge_fwd(o_a, lse_a, o_b, zg, h, wa_t, wb_t, wout, name):
    t = h.shape[0]
    tm = 512

    def body(o1, o2, o3, l1, l2, l3, ob_ref, zg_ref, h_ref, wa_ref, wb_ref, wo_ref, h2_ref, ya_ref, lt_ref):
        m = jnp.maximum(jnp.maximum(l1[...], l2[...]), l3[...])
        e1, e2, e3 = jnp.exp(l1[...] - m), jnp.exp(l2[...] - m), jnp.exp(l3[...] - m)
        se = e1 + e2 + e3
        ya = (e1 / se) * o1[...] + (e2 / se) * o2[...] + (e3 / se) * o3[...]
        ya_ref[...] = ya
        lt_ref[...] = m + jnp.log(se)
        pa = _nt(ya.astype(BF16), wa_ref[...])
        pb = _nt(ob_ref[...].astype(BF16), wb_ref[...])
        merged = jax.nn.sigmoid(zg_ref[:, :D_MODEL]) * pa + jax.nn.sigmoid(zg_ref[:, D_MODEL:]) * pb
        h2_ref[...] = h_ref[...] + _nn(merged.astype(BF16), wo_ref[...])

    row = lambda w: pl.BlockSpec((tm, w), lambda i: (i, 0))
    full = lambda a: pl.BlockSpec(a.shape, lambda i: (0, 0))
    return pl.pallas_call(
        body, name=name, grid=(t // tm,),
        in_specs=[row(256)] * 6 + [row(512), row(GATE_W), row(D_MODEL), full(wa_t), full(wb_t), full(wout)],
        out_specs=[row(D_MODEL), row(256), row(256)],
        out_shape=[jax.ShapeDtypeStruct((t, D_MODEL), F32), jax.ShapeDtypeStruct((t, 256), F32),
                   jax.ShapeDtypeStruct((t, 256), F32)],
        compiler_params=_params("parallel"),
    )(*o_a, *lse_a, o_b, zg, h, wa_t, wb_t, wout)


def _merge_bwd(dh, ya, o_b, zg, wa_t, wb_t, wout, name):
    t = dh.shape[0]
    tm = 256

    def body(dh_ref, ya_ref, ob_ref, zg_ref, wa_ref, wb_ref, wo_ref,
             mg_ref, dpa_ref, dpb_ref, dzg_ref, dya_ref, dyb_ref):
        dm = _nt(dh_ref[...].astype(BF16), wo_ref[...])
        pa = _nt(ya_ref[...].astype(BF16), wa_ref[...])
        pb = _nt(ob_ref[...].astype(BF16), wb_ref[...])
        sa = jax.nn.sigmoid(zg_ref[:, :D_MODEL])
        sb = jax.nn.sigmoid(zg_ref[:, D_MODEL:])
        mg_ref[...] = (sa * pa + sb * pb).astype(BF16)
        dpa = (dm * sa).astype(BF16)
        dpb = (dm * sb).astype(BF16)
        dpa_ref[...] = dpa
        dpb_ref[...] = dpb
        dzg_ref[:, :D_MODEL] = (dm * pa * (sa * (1.0 - sa))).astype(BF16)
        dzg_ref[:, D_MODEL:] = (dm * pb * (sb * (1.0 - sb))).astype(BF16)
        dya_ref[...] = _nn(dpa, wa_ref[...])
        dyb_ref[...] = _nn(dpb, wb_ref[...])

    row = lambda w: pl.BlockSpec((tm, w), lambda i: (i, 0))
    full = lambda a: pl.BlockSpec(a.shape, lambda i: (0, 0))
    sds = jax.ShapeDtypeStruct
    return pl.pallas_call(
        body, name=name, grid=(t // tm,),
        in_specs=[row(D_MODEL), row(256), row(512), row(GATE_W), full(wa_t), full(wb_t), full(wout)],
        out_specs=[row(D_MODEL), row(D_MODEL), row(D_MODEL), row(GATE_W), row(256), row(512)],
        out_shape=[sds((t, D_MODEL), BF16), sds((t, D_MODEL), BF16), sds((t, D_MODEL), BF16),
                   sds((t, GATE_W), BF16), sds((t, 256), F32), sds((t, 512), F32)],
        compiler_params=_params("parallel"),
    )(dh, ya, o_b, zg, wa_t, wb_t, wout)


def _loss_head(h, g, target, name):
    t = h.shape[0]
    tm = 1024

    def body(h_ref, g_ref, t_ref, dh_ref, loss_ref, dg_ref):
        @pl.when(pl.program_id(0) == 0)
        def _():
            loss_ref[...] = jnp.zeros_like(loss_ref)
            dg_ref[...] = jnp.zeros_like(dg_ref)

        xhat, r = _rms(h_ref[...], g_ref[...])
        err = xhat * g_ref[...] - t_ref[...]
        loss_ref[...] += 0.5 * jnp.sum(jnp.mean(err * err, axis=-1, keepdims=True), axis=0, keepdims=True)
        dx, dg = _rms_bwd(err * (1.0 / D_MODEL), xhat, r, g_ref[...])
        dh_ref[...] = dx
        dg_ref[...] += dg

    row = pl.BlockSpec((tm, D_MODEL), lambda i: (i, 0))
    vec = pl.BlockSpec((1, D_MODEL), lambda i: (0, 0))
    return pl.pallas_call(
        body, name=name, grid=(t // tm,),
        in_specs=[row, vec, row], out_specs=[row, pl.BlockSpec((1, 128), lambda i: (0, 0)), vec],
        out_shape=[jax.ShapeDtypeStruct((t, D_MODEL), F32), jax.ShapeDtypeStruct((1, 128), F32),
                   jax.ShapeDtypeStruct((1, D_MODEL), F32)],
        compiler_params=_params("arbitrary"),
    )(h, g, target)


def _place():
    return lax.axis_index("x"), lax.axis_index("y"), lax.axis_index("c")


def _all_gather(shards, name):
    n = len(shards)

    def body(*refs):
        ins, outs = refs[:n], refs[n:2 * n]
        send_sems, recv_sems, local_sems = refs[2 * n:]
        x, y, c = _place()
        me, sibling = (x, y, c), (x, y, 1 - c)
        chips = [(1 - x, y), (x, 1 - y), (1 - x, 1 - y)]

        def copy(i, k, block, to, src=None):
            dst = outs[i].at[4 * block[0] + 2 * block[1] + block[2]]
            return pltpu.make_async_remote_copy(
                src_ref=dst if src is None else src, dst_ref=dst, send_sem=send_sems.at[7 * i + k],
                recv_sem=recv_sems.at[7 * i + k], device_id=to, device_id_type=MESH)

        mine = [pltpu.make_async_copy(ins[i], outs[i].at[4 * x + 2 * y + c], local_sems.at[i]) for i in range(n)]
        first = []
        for i in range(n):
            mine[i].start()
            first.append(copy(i, 0, me, sibling, src=ins[i]))
            first += [copy(i, 1 + j, me, (*chip, c), src=ins[i]) for j, chip in enumerate(chips)]
        for cp in first:
            cp.start()
        passed = []
        for j, chip in enumerate(chips):
            for i in range(n):
                copy(i, 1 + j, (*chip, c), me).wait_recv()
                fwd = copy(i, 4 + j, (*chip, c), sibling)
                fwd.start()
                passed.append(fwd)
        for i in range(n):
            copy(i, 0, sibling, me).wait_recv()
            for j, chip in enumerate(chips):
                copy(i, 4 + j, (*chip, 1 - c), me).wait_recv()
        for cp in first + passed:
            cp.wait_send()
        for cp in mine:
            cp.wait()

    return pl.pallas_call(
        body, name=name, in_specs=[ANY] * n, out_specs=[ANY] * n,
        out_shape=[jax.ShapeDtypeStruct((N_DEV,) + s.shape, s.dtype) for s in shards],
        scratch_shapes=[pltpu.SemaphoreType.DMA((7 * n,)), pltpu.SemaphoreType.DMA((7 * n,)),
                        pltpu.SemaphoreType.DMA((n,))],
    )(*shards)


def _exchange_core(grads, name):
    n = len(grads)

    def body(*refs):
        ins, outs = refs[:n], refs[n:2 * n]
        send_sems, recv_sems = refs[2 * n:]
        x, y, c = _place()
        copies = [pltpu.make_async_remote_copy(
            src_ref=ins[i].at[:, pl.ds(1 - c, 1)], dst_ref=outs[i], send_sem=send_sems.at[i],
            recv_sem=recv_sems.at[i], device_id=(x, y, 1 - c), device_id_type=MESH) for i in range(n)]
        for cp in copies:
            cp.start()
        for cp in copies:
            cp.wait()

    return pl.pallas_call(
        body, name=name, in_specs=[ANY] * n, out_specs=[ANY] * n,
        out_shape=[jax.ShapeDtypeStruct((N_CHIP, 1) + g.shape[2:], g.dtype) for g in grads],
        scratch_shapes=[pltpu.SemaphoreType.DMA((n,)), pltpu.SemaphoreType.DMA((n,))],
    )(*grads)


def _pair_sum(grad, got, name):
    _, _, r, cdim = grad.shape
    core = lax.axis_index("c").astype(jnp.int32).reshape(1)

    def body(core_ref, g_ref, s_ref, o_ref):
        o_ref[...] = (g_ref[...] + s_ref[...]).astype(BF16)

    return pl.pallas_call(
        body, name=name,
        grid_spec=pltpu.PrefetchScalarGridSpec(
            num_scalar_prefetch=1, grid=(N_CHIP,),
            in_specs=[pl.BlockSpec((None, None, r, cdim), lambda q, core_ref: (q, core_ref[0], 0, 0)),
                      pl.BlockSpec((None, None, r, cdim), lambda q, core_ref: (q, 0, 0, 0))],
            out_specs=pl.BlockSpec((None, r, cdim), lambda q, core_ref: (q, 0, 0))),
        out_shape=jax.ShapeDtypeStruct((N_CHIP, r, cdim), BF16),
        compiler_params=_params("parallel"),
    )(core, grad, got)


def _exchange_chip(parts, name):
    n = len(parts)

    def body(*refs):
        ins, outs = refs[:n], refs[n:2 * n]
        send_sems, recv_sems, local_sems = refs[2 * n:]
        x, y, c = _place()
        chips = [(1 - x, y), (x, 1 - y), (1 - x, 1 - y)]
        my_chip = 2 * x + y
        copies = []
        for i in range(n):
            own = pltpu.make_async_copy(ins[i].at[my_chip], outs[i].at[my_chip], local_sems.at[i])
            own.start()
            copies.append(own)
            for k, (qx, qy) in enumerate(chips):
                cp = pltpu.make_async_remote_copy(
                    src_ref=ins[i].at[2 * qx + qy], dst_ref=outs[i].at[my_chip], send_sem=send_sems.at[3 * i + k],
                    recv_sem=recv_sems.at[3 * i + k], device_id=(qx, qy, c), device_id_type=MESH)
                cp.start()
                copies.append(cp)
        for cp in copies:
            cp.wait()

    return pl.pallas_call(
        body, name=name, in_specs=[ANY] * n, out_specs=[ANY] * n,
        out_shape=[jax.ShapeDtypeStruct(p.shape, p.dtype) for p in parts],
        scratch_shapes=[pltpu.SemaphoreType.DMA((3 * n,)), pltpu.SemaphoreType.DMA((3 * n,)),
                        pltpu.SemaphoreType.DMA((n,))],
    )(*parts)


def _chip_sum(parts, name):
    def body(p_ref, o_ref):
        o_ref[...] = ((p_ref[0].astype(F32) + p_ref[1].astype(F32)) + p_ref[2].astype(F32)) + p_ref[3].astype(F32)

    return pl.pallas_call(
        body, name=name, out_shape=jax.ShapeDtypeStruct(parts.shape[1:], F32),
        compiler_params=pltpu.CompilerParams(vmem_limit_bytes=VMEM_LIMIT),
    )(parts)


def _adamw_math(w, g, m, v):
    m = ADAM_B1 * m + (1.0 - ADAM_B1) * g
    v = ADAM_B2 * v + (1.0 - ADAM_B2) * jnp.square(g)
    m_hat = m / (1.0 - ADAM_B1 ** ADAM_STEP)
    v_hat = v / (1.0 - ADAM_B2 ** ADAM_STEP)
    delta = -ADAM_LR * (m_hat / (jnp.sqrt(v_hat) + ADAM_EPS) + ADAM_WD * w)
    return delta, m, v


def _adamw(w, g, m, v, name):
    def body(w_ref, g_ref, m_ref, v_ref, d_ref, m2_ref, v2_ref):
        d_ref[...], m2_ref[...], v2_ref[...] = _adamw_math(w_ref[...], g_ref[...], m_ref[...], v_ref[...])

    sds = jax.ShapeDtypeStruct(w.shape, F32)
    return pl.pallas_call(body, name=name, out_shape=[sds, sds, sds],
                          compiler_params=pltpu.CompilerParams(vmem_limit_bytes=VMEM_LIMIT))(w, g, m, v)


SMALL_ROWS = 80


def _small_update(g, w, m, v, name):
    def body(g_ref, w_ref, m_ref, v_ref, gs_ref, d_ref, m2_ref, v2_ref, got_ref, send_sems, recv_sems):
        x, y, c = _place()
        me = 4 * x + 2 * y + c
        got_ref[me] = g_ref[...]
        copies = []
        for k in range(1, N_DEV):
            peer = (x ^ (k >> 2), y ^ ((k >> 1) & 1), c ^ (k & 1))
            cp = pltpu.make_async_remote_copy(
                src_ref=g_ref, dst_ref=got_ref.at[me], send_sem=send_sems.at[k - 1], recv_sem=recv_sems.at[k - 1],
                device_id=peer, device_id_type=MESH)
            cp.start()
            copies.append(cp)
        for cp in copies:
            cp.wait()
        total = got_ref[0]
        for k in range(1, N_DEV):
            total = total + got_ref[k]
        gs_ref[...] = total
        d_ref[...], m2_ref[...], v2_ref[...] = _adamw_math(w_ref[...], total, m_ref[...], v_ref[...])

    sds = jax.ShapeDtypeStruct((SMALL_ROWS, 128), F32)
    vm = pl.BlockSpec(memory_space=pltpu.VMEM)
    return pl.pallas_call(
        body, name=name, in_specs=[vm] * 4, out_specs=[vm] * 4, out_shape=[sds] * 4,
        scratch_shapes=[pltpu.VMEM((N_DEV, SMALL_ROWS, 128), F32), pltpu.SemaphoreType.DMA((N_DEV - 1,)),
                        pltpu.SemaphoreType.DMA((N_DEV - 1,))],
    )(g, w, m, v)


def _pack_small(gains, b_in, rel_bias, sinks, last):
    rows = [a.reshape(8, 128) for a in gains] + [b_in.reshape(40, 128), rel_bias.reshape(5, 128),
                                                 jnp.pad(sinks.reshape(1, 8), ((0, 0), (0, 120))), last]
    rows.append(jnp.zeros((SMALL_ROWS - 79, 128), F32))
    return jnp.concatenate(rows, axis=0)


def _unpack_small(p, like):
    out = [p[8 * i:8 * i + 8].reshape(like[i].shape) for i in range(4)]
    out.append(p[32:72].reshape(like[4].shape))
    out.append(p[72:77].reshape(like[5].shape))
    out.append(p[77, :8].reshape(like[6].shape))
    return out


def kernel(x, ffn1_norm, ffn1_w_gate, ffn1_w_up, ffn1_w_down, mix_norm, w_in, b_in, w_branch_a, w_branch_b, w_out, sinks, rel_bias, ffn2_norm, ffn2_w_gate, ffn2_w_up, ffn2_w_down, final_norm, loss_target, m_ffn1_norm, m_ffn1_w_gate, m_ffn1_w_up, m_ffn1_w_down, m_mix_norm, m_w_in, m_b_in, m_w_branch_a, m_w_branch_b, m_w_out, m_sinks, m_rel_bias, m_ffn2_norm, m_ffn2_w_gate, m_ffn2_w_up, m_ffn2_w_down, m_final_norm, v_ffn1_norm, v_ffn1_w_gate, v_ffn1_w_up, v_ffn1_w_down, v_mix_norm, v_w_in, v_b_in, v_w_branch_a, v_w_branch_b, v_w_out, v_sinks, v_rel_bias, v_ffn2_norm, v_ffn2_w_gate, v_ffn2_w_up, v_ffn2_w_down, v_final_norm):
    bsz, seq, _ = x.shape
    t = bsz * seq
    xt = x.reshape(t, D_MODEL)
    target = loss_target.reshape(t, D_MODEL)

    big = [("ffn1_w_gate", ffn1_w_gate, m_ffn1_w_gate, v_ffn1_w_gate, True),
           ("ffn1_w_up", ffn1_w_up, m_ffn1_w_up, v_ffn1_w_up, True),
           ("ffn1_w_down", ffn1_w_down, m_ffn1_w_down, v_ffn1_w_down, False),
           ("w_in", w_in, m_w_in, v_w_in, True),
           ("w_branch_a", w_branch_a, m_w_branch_a, v_w_branch_a, True),
           ("w_branch_b", w_branch_b, m_w_branch_b, v_w_branch_b, True),
           ("w_out", w_out, m_w_out, v_w_out, False),
           ("ffn2_w_gate", ffn2_w_gate, m_ffn2_w_gate, v_ffn2_w_gate, True),
           ("ffn2_w_up", ffn2_w_up, m_ffn2_w_up, v_ffn2_w_up, True),
           ("ffn2_w_down", ffn2_w_down, m_ffn2_w_down, v_ffn2_w_down, False)]
    shards = [(w[0].T if tr else w[0]).astype(BF16) for _, w, _, _, tr in big]
    gathered = _all_gather(shards, "all_gather_weights")
    full = {nm: gw.reshape(-1, gw.shape[-1]) for (nm, _, _, _, _), gw in zip(big, gathered)}

    g1, gm, g2, gf = ffn1_norm, mix_norm, ffn2_norm, final_norm.reshape(1, D_MODEL)

    h1, n1 = _ffn_fwd(xt, g1, full["ffn1_w_gate"], full["ffn1_w_up"], full["ffn1_w_down"], "ffn1_fwd")
    u, zq, zg = _inproj_fwd(h1, gm, full["w_in"], b_in, "inproj_fwd")
    buckets = _bucket_tiles()
    bias = _bias_build(rel_bias, buckets)
    sink_rows = jnp.broadcast_to(sinks.reshape(8, 1), (8, 128))
    cfs = [_Att(i) for i in range(4)]
    zfold = [zq.reshape(bsz, seq // cf.d, cf.d * QKV_W) for cf in cfs]
    att = []
    for i, cf in enumerate(cfs):
        o, lse = _attn_fwd(cf, zfold[i], bias[cf.h0:cf.h0 + cf.nh], sink_rows, f"attn{i}_fwd")
        att.append((o.reshape(t, cf.wq), lse.reshape(t, cf.wq)))
    o_b, lse_b = att[3]
    h2, ya, lse_tot = _merge_fwd([a[0] for a in att[:3]], [a[1] for a in att[:3]], o_b, zg, h1,
                                 full["w_branch_a"], full["w_branch_b"], full["w_out"], "merge_fwd")
    h3, n2 = _ffn_fwd(h2, g2, full["ffn2_w_gate"], full["ffn2_w_up"], full["ffn2_w_down"], "ffn2_fwd")
    dh3, loss_part, dgf = _loss_head(h3, gf, target, "loss_head")

    grads = {}
    dh2, dg2, hff, da, db, dhh = _ffn_bwd(h2, n2, g2, dh3, full["ffn2_w_gate"], full["ffn2_w_up"],
                                          full["ffn2_w_down"], "ffn2_bwd")
    grads["ffn2_w_gate"] = _tn_matmul(da, n2, 1408, "ffn2_dgate")
    grads["ffn2_w_up"] = _tn_matmul(db, n2, 1408, "ffn2_dup")
    grads["ffn2_w_down"] = _tn_matmul(hff, dhh, 1408, "ffn2_ddown")
    merged, dpa, dpb, dzg, dya, dyb = _merge_bwd(dh2, ya, o_b, zg, full["w_branch_a"], full["w_branch_b"],
                                                 full["w_out"], "merge_bwd")
    grads["w_out"] = _tn_matmul(merged, dh2, 1024, "dw_out")
    grads["w_branch_a"] = _tn_matmul(dpa, ya, 1024, "dw_branch_a")
    grads["w_branch_b"] = _tn_matmul(dpb, o_b, 1024, "dw_branch_b")
    dq, dk, dv, dbias, dsink = [], [], [], [], None
    for i, cf in enumerate(cfs):
        if i < 3:
            lse_i, lt_i, dy_i, y_i = att[i][1], lse_tot, dya, ya
        else:
            lse_i, lt_i, dy_i, y_i = lse_b, lse_b, dyb, o_b
        shp = (bsz, seq // cf.d, cf.d * cf.wq)
        res = _attn_bwd(cf, zfold[i], bias[cf.h0:cf.h0 + cf.nh], sink_rows, lse_i.reshape(shp), lt_i.reshape(shp),
                        dy_i.reshape(shp), y_i.reshape(shp), f"attn{i}_bwd")
        dq.append(res[0].reshape(t, cf.wq))
        dk.append(res[1].reshape(t, cf.wkv))
        dv.append(res[2].reshape(t, cf.wkv))
        dbias.append(res[3])
        if cf.sinks:
            dsink = res[4]
    pieces = dq[:3] + dk[:3] + dv[:3] + [dq[3], dk[3], dv[3], dzg]
    dh1, dz, db_in, dgm = _inproj_bwd(pieces, h1, gm, dh2, full["w_in"], "inproj_bwd")
    grads["w_in"] = _tn_matmul(dz, u, 1280, "dw_in")
    dx, dg1, hff, da, db, dhh = _ffn_bwd(xt, n1, g1, dh1, full["ffn1_w_gate"], full["ffn1_w_up"],
                                         full["ffn1_w_down"], "ffn1_bwd")
    grads["ffn1_w_gate"] = _tn_matmul(da, n1, 1408, "ffn1_dgate")
    grads["ffn1_w_up"] = _tn_matmul(db, n1, 1408, "ffn1_dup")
    grads["ffn1_w_down"] = _tn_matmul(hff, dhh, 1408, "ffn1_ddown")
    dtable, dsinks = _bias_reduce(jnp.concatenate(dbias, axis=0), buckets, dsink)

    names = [b[0] for b in big]
    by_owner = [grads[nm].reshape(N_CHIP, 2, -1, grads[nm].shape[-1]) for nm in names]
    from_sibling = _exchange_core(by_owner, "reduce_scatter_core")
    pair = [_pair_sum(g, s, f"pair_sum_{nm}") for nm, g, s in zip(names, by_owner, from_sibling)]
    from_chips = _exchange_chip(pair, "reduce_scatter_chip")
    out_g, out_d, out_m, out_v = {}, {}, {}, {}
    for (nm, w, m, v, tr), parts in zip(big, from_chips):
        g = _chip_sum(parts, f"chip_sum_{nm}")
        g = (g.T if tr else g).reshape(w.shape)
        out_g[nm] = g
        d2, m2, v2 = _adamw(w[0], g[0], m[0], v[0], f"adamw_{nm}")
        out_d[nm], out_m[nm], out_v[nm] = d2[None], m2[None], v2[None]

    small = [("ffn1_norm", ffn1_norm, m_ffn1_norm, v_ffn1_norm), ("mix_norm", mix_norm, m_mix_norm, v_mix_norm),
             ("ffn2_norm", ffn2_norm, m_ffn2_norm, v_ffn2_norm), ("final_norm", final_norm, m_final_norm, v_final_norm),
             ("b_in", b_in, m_b_in, v_b_in), ("rel_bias", rel_bias, m_rel_bias, v_rel_bias),
             ("sinks", sinks, m_sinks, v_sinks)]
    zero_row = jnp.zeros((1, 128), F32)
    pack = lambda arrs, last: _pack_small(arrs[:4], arrs[4], arrs[5], arrs[6], last)
    g_small = pack([dg1, dgm, dg2, dgf, db_in, dtable[:, :TOTAL_HEADS], dsinks[0, :8]], loss_part)
    packed = [pack([s[k] for s in small], zero_row) for k in (1, 2, 3)]
    gs, ds, ms, vs = _small_update(g_small, *packed, "small_update")
    like = [s[1] for s in small]
    for nm_s, g_, d_, m_, v_ in zip([s[0] for s in small], _unpack_small(gs, like), _unpack_small(ds, like),
                                    _unpack_small(ms, like), _unpack_small(vs, like)):
        out_g[nm_s], out_d[nm_s], out_m[nm_s], out_v[nm_s] = g_, d_, m_, v_
    loss = gs[78, 0]

    order = ["ffn1_norm", "ffn1_w_gate", "ffn1_w_up", "ffn1_w_down", "mix_norm", "w_in", "b_in", "w_branch_a",
             "w_branch_b", "w_out", "sinks", "rel_bias", "ffn2_norm", "ffn2_w_gate", "ffn2_w_up", "ffn2_w_down",
             "final_norm"]
    return (loss, dx.reshape(x.shape), *[out_g[k] for k in order], *[out_d[k] for k in order],
            *[out_m[k] for k in order], *[out_v[k] for k in order])
```

```python
import functools
import math

import numpy as np
import jax
import jax.numpy as jnp
from jax import lax
from jax.experimental import pallas as pl
from jax.experimental.pallas import tpu as pltpu

D_MODEL = 1024
D_FF = 2816
FF_CHUNK = 256
HEAD_DIM = 64
BLOCK = 128
N_BUCKETS = 32
MAX_DISTANCE = 2048
A_HEADS = 12
TOTAL_HEADS = 20
DIL_GROUPS = ((128, 1), (512, 4), (2048, 16))
B_WINDOW = 128
QKV_W = 3072
GATE_W = 2048
D_IN = QKV_W + GATE_W
EPS = 1e-6
NEG = -1e30
N_DEV = 8
N_CHIP = 4
ADAM_LR, ADAM_B1, ADAM_B2, ADAM_EPS, ADAM_WD, ADAM_STEP = 0.001, 0.9, 0.999, 1e-08, 0.01, 10
VMEM_LIMIT = 56 * 1024 * 1024
MESH = pl.DeviceIdType.MESH
BF16 = jnp.bfloat16
F32 = jnp.float32
ANY = pl.BlockSpec(memory_space=pl.ANY)


def _params(*sem):
    return pltpu.CompilerParams(dimension_semantics=sem, vmem_limit_bytes=VMEM_LIMIT)


def _resident(a):
    return pl.BlockSpec(a.shape, lambda i: (0, 0), pipeline_mode=pl.Buffered(1))


def _nt(a, b):
    return lax.dot_general(a, b, (((1,), (1,)), ((), ())), preferred_element_type=F32)


def _nn(a, b):
    return lax.dot_general(a, b, (((1,), (0,)), ((), ())), preferred_element_type=F32)


def _tn(a, b):
    return lax.dot_general(a, b, (((0,), (0,)), ((), ())), preferred_element_type=F32)


def _rms(x, g):
    r = lax.rsqrt(jnp.mean(x * x, axis=-1, keepdims=True) + EPS)
    return x * r, r


def _rms_bwd(dn, xhat, r, g):
    dg = jnp.sum(dn * xhat, axis=0, keepdims=True)
    dxh = dn * g
    dx = r * (dxh - xhat * jnp.mean(dxh * xhat, axis=-1, keepdims=True))
    return dx, dg


def _ffn_fwd(x, g, wg_t, wu_t, wd, name):
    t = x.shape[0]
    tm = 512

    def body(x_ref, g_ref, wg_ref, wu_ref, wd_ref, h_ref, n_ref, hff_ref):
        xhat, _ = _rms(x_ref[...], g_ref[...])
        n = (xhat * g_ref[...]).astype(BF16)
        n_ref[...] = n
        for c in range(0, D_FF, FF_CHUNK):
            a = _nt(n, wg_ref[c:c + FF_CHUNK, :])
            b = _nt(n, wu_ref[c:c + FF_CHUNK, :])
            hff_ref[:, c:c + FF_CHUNK] = (a * jax.nn.sigmoid(a) * b).astype(BF16)
        h_ref[...] = x_ref[...] + 0.5 * _nn(hff_ref[...], wd_ref[...])

    row = pl.BlockSpec((tm, D_MODEL), lambda i: (i, 0))
    return pl.pallas_call(
        body, name=name, grid=(t // tm,),
        in_specs=[row, _resident(g), _resident(wg_t), _resident(wu_t), _resident(wd)],
        out_specs=[row, row],
        out_shape=[jax.ShapeDtypeStruct((t, D_MODEL), F32), jax.ShapeDtypeStruct((t, D_MODEL), BF16)],
        scratch_shapes=[pltpu.VMEM((tm, D_FF), BF16)],
        compiler_params=_params("parallel"),
    )(x, g, wg_t, wu_t, wd)


def _ffn_bwd(x, n, g, dh, wg_t, wu_t, wd, name):
    t = x.shape[0]
    tm = 256

    def body(x_ref, n_ref, g_ref, dh_ref, wg_ref, wu_ref, wd_ref,
             dx_ref, dg_ref, hff_ref, da_ref, db_ref, dhh_ref):
        @pl.when(pl.program_id(0) == 0)
        def _():
            dg_ref[...] = jnp.zeros_like(dg_ref)

        dhh = (0.5 * dh_ref[...]).astype(BF16)
        dhh_ref[...] = dhh
        nb = n_ref[...]
        for c in range(0, D_FF, FF_CHUNK):
            cols = slice(c, c + FF_CHUNK)
            a = _nt(nb, wg_ref[cols, :])
            b = _nt(nb, wu_ref[cols, :])
            s = jax.nn.sigmoid(a)
            silu = a * s
            dhff = _nt(dhh, wd_ref[cols, :])
            hff_ref[:, cols] = (silu * b).astype(BF16)
            da_ref[:, cols] = (dhff * b * (s * (1.0 + a * (1.0 - s)))).astype(BF16)
            db_ref[:, cols] = (dhff * silu).astype(BF16)
        dn = _nn(da_ref[...], wg_ref[...]) + _nn(db_ref[...], wu_ref[...])
        xhat, r = _rms(x_ref[...], g_ref[...])
        dx, dg = _rms_bwd(dn, xhat, r, g_ref[...])
        dx_ref[...] = dh_ref[...] + dx
        dg_ref[...] += dg

    row = pl.BlockSpec((tm, D_MODEL), lambda i: (i, 0))
    hid = pl.BlockSpec((tm, D_FF), lambda i: (i, 0))
    return pl.pallas_call(
        body, name=name, grid=(t // tm,),
        in_specs=[row, row, _resident(g), row, _resident(wg_t), _resident(wu_t), _resident(wd)],
        out_specs=[row, pl.BlockSpec((1, D_MODEL), lambda i: (0, 0)), hid, hid, hid, row],
        out_shape=[jax.ShapeDtypeStruct((t, D_MODEL), F32), jax.ShapeDtypeStruct((1, D_MODEL), F32),
                   jax.ShapeDtypeStruct((t, D_FF), BF16), jax.ShapeDtypeStruct((t, D_FF), BF16),
                   jax.ShapeDtypeStruct((t, D_FF), BF16), jax.ShapeDtypeStruct((t, D_MODEL), BF16)],
        compiler_params=_params("arbitrary"),
    )(x, n, g, dh, wg_t, wu_t, wd)


def _tn_matmul(a, b, rc, name):
    t, r = a.shape
    c = b.shape[1]
    tk = 1024

    def body(a_ref, b_ref, o_ref):
        @pl.when(pl.program_id(1) == 0)
        def _():
            o_ref[...] = jnp.zeros_like(o_ref)

        o_ref[...] += _tn(a_ref[...].astype(BF16), b_ref[...].astype(BF16))

    return pl.pallas_call(
        body, name=name, grid=(r // rc, t // tk),
        in_specs=[pl.BlockSpec((tk, rc), lambda i, k: (k, i)), pl.BlockSpec((tk, c), lambda i, k: (k, 0))],
        out_specs=pl.BlockSpec((rc, c), lambda i, k: (i, 0)),
        out_shape=jax.ShapeDtypeStruct((r, c), F32),
        compiler_params=_params("parallel", "arbitrary"),
    )(a, b)


PIECE_W = (256,) * 9 + (512, 128, 128, GATE_W)


def _inproj_fwd(h, g, w_t, b_in, name):
    t = h.shape[0]
    tm, nc = 512, 512

    def body(h_ref, g_ref, w_ref, b_ref, u_ref, zq_ref, zg_ref):
        xhat, _ = _rms(h_ref[...], g_ref[...])
        u = (xhat * g_ref[...]).astype(BF16)
        u_ref[...] = u
        for c in range(D_IN // nc):
            z = _nt(u, w_ref[c * nc:(c + 1) * nc, :]) + b_ref[:, c * nc:(c + 1) * nc]
            if c < QKV_W // nc:
                zq_ref[:, c * nc:(c + 1) * nc] = z.astype(BF16)
            else:
                zg_ref[:, c * nc - QKV_W:(c + 1) * nc - QKV_W] = z

    row = lambda w: pl.BlockSpec((tm, w), lambda i: (i, 0))
    full = _resident
    return pl.pallas_call(
        body, name=name, grid=(t // tm,),
        in_specs=[row(D_MODEL), full(g), full(w_t), full(b_in)],
        out_specs=[row(D_MODEL), row(QKV_W), row(GATE_W)],
        out_shape=[jax.ShapeDtypeStruct((t, D_MODEL), BF16), jax.ShapeDtypeStruct((t, QKV_W), BF16),
                   jax.ShapeDtypeStruct((t, GATE_W), F32)],
        compiler_params=_params("parallel"),
    )(h, g, w_t, b_in)


def _inproj_bwd(pieces, h, g, dh_res, w_t, name):
    t = h.shape[0]
    tm = 256
    npiece = len(PIECE_W)
    offs = np.concatenate([[0], np.cumsum(PIECE_W)]).tolist()

    def body(*refs):
        p_refs = refs[:npiece]
        h_ref, g_ref, dhr_ref, w_ref, dh_ref, dz_ref, db_ref, dg_ref = refs[npiece:]
        i = pl.program_id(0)

        @pl.when(i == 0)
        def _():
            db_ref[...] = jnp.zeros_like(db_ref)
            dg_ref[...] = jnp.zeros_like(dg_ref)

        du = jnp.zeros((tm, D_MODEL), F32)
        for k in range(npiece):
            o, w = offs[k], PIECE_W[k]
            for c0 in range(0, w, 512):
                cw = min(512, w - c0)
                pz = p_refs[k][:, c0:c0 + cw]
                dz_ref[:, o + c0:o + c0 + cw] = pz
                db_ref[:, o + c0:o + c0 + cw] += jnp.sum(pz.astype(F32), axis=0, keepdims=True)
                du = du + _nn(pz, w_ref[o + c0:o + c0 + cw, :])
        xhat, r = _rms(h_ref[...], g_ref[...])
        dx, dg = _rms_bwd(du, xhat, r, g_ref[...])
        dh_ref[...] = dhr_ref[...] + dx
        dg_ref[...] += dg

    row = lambda w: pl.BlockSpec((tm, w), lambda i: (i, 0))
    full = lambda shp: pl.BlockSpec(shp, lambda i: (0, 0))
    return pl.pallas_call(
        body, name=name, grid=(t // tm,),
        in_specs=[row(w) for w in PIECE_W] + [row(D_MODEL), _resident(g), row(D_MODEL), _resident(w_t)],
        out_specs=[row(D_MODEL), row(D_IN), full((1, D_IN)), full((1, D_MODEL))],
        out_shape=[jax.ShapeDtypeStruct((t, D_MODEL), F32), jax.ShapeDtypeStruct((t, D_IN), BF16),
                   jax.ShapeDtypeStruct((1, D_IN), F32), jax.ShapeDtypeStruct((1, D_MODEL), F32)],
        compiler_params=_params("arbitrary"),
    )(*pieces, h, g, dh_res, w_t)


def _t5_bucket(dist):
    max_exact = N_BUCKETS // 2
    n = jnp.maximum(dist, 0)
    nf = jnp.maximum(n, 1).astype(jnp.float32)
    large = max_exact + (jnp.log(nf / max_exact) / math.log(MAX_DISTANCE / max_exact)
                         * (N_BUCKETS - max_exact)).astype(jnp.int32)
    large = jnp.minimum(large, N_BUCKETS - 1)
    return jnp.where(n < max_exact, n, large)


ATT_CFG = ((1, 128, 0, 4), (4, 128, 4, 4), (16, 128, 8, 4), (1, B_WINDOW - 1, A_HEADS, 8))


def _bucket_tiles():
    qi = jnp.arange(BLOCK)[:, None]
    ki = jnp.arange(2 * BLOCK)[None, :]
    dist = qi + BLOCK - ki
    return jnp.stack([_t5_bucket(dist * cfg[0]) for cfg in ATT_CFG]).astype(jnp.int32)


def _band(max_steps):
    row = lax.broadcasted_iota(jnp.int32, (BLOCK, 2 * BLOCK), 0)
    col = lax.broadcasted_iota(jnp.int32, (BLOCK, 2 * BLOCK), 1)
    dist = row + BLOCK - col
    return (dist >= 0) & (dist <= max_steps)


def _bias_build(table, buckets):
    def body(tab_ref, bt_ref, out_ref):
        for ci, (_, max_steps, h0, nh) in enumerate(ATT_CFG):
            bt = bt_ref[ci]
            band = _band(max_steps)
            for h in range(h0, h0 + nh):
                acc = lax.fori_loop(0, N_BUCKETS, lambda b, acc: jnp.where(bt == b, tab_ref[b, h], acc),
                                    jnp.zeros((BLOCK, 2 * BLOCK), F32))
                out_ref[h] = jnp.where(band, acc, NEG)

    return pl.pallas_call(
        body, name="bias_build",
        in_specs=[pl.BlockSpec(memory_space=pltpu.SMEM), pl.BlockSpec(memory_space=pltpu.VMEM)],
        out_specs=pl.BlockSpec(memory_space=pltpu.VMEM),
        out_shape=jax.ShapeDtypeStruct((TOTAL_HEADS, BLOCK, 2 * BLOCK), F32),
    )(table, buckets)


def _bias_reduce(dbias, buckets, dsink_rows):
    def body(db_ref, bt_ref, ds_ref, out_ref, sink_ref):
        ri = lax.broadcasted_iota(jnp.int32, (N_BUCKETS, 128), 0)
        ci = lax.broadcasted_iota(jnp.int32, (N_BUCKETS, 128), 1)

        def per_bucket(b, acc):
            for cfg_i, (_, _, h0, nh) in enumerate(ATT_CFG):
                hit = bt_ref[cfg_i] == b
                for h in range(h0, h0 + nh):
                    val = jnp.sum(jnp.where(hit, db_ref[h], 0.0))
                    acc = jnp.where((ri == b) & (ci == h), val, acc)
            return acc

        out_ref[...] = lax.fori_loop(0, N_BUCKETS, per_bucket, jnp.zeros((N_BUCKETS, 128), F32))
        sink_ref[...] = jnp.broadcast_to(jnp.sum(ds_ref[...], axis=0, keepdims=True), (8, 128))

    return pl.pallas_call(
        body, name="bias_reduce",
        in_specs=[pl.BlockSpec(memory_space=pltpu.VMEM)] * 3,
        out_specs=[pl.BlockSpec(memory_space=pltpu.VMEM)] * 2,
        out_shape=[jax.ShapeDtypeStruct((N_BUCKETS, 128), F32), jax.ShapeDtypeStruct((8, 128), F32)],
    )(dbias, buckets, dsink_rows)


class _Att:
    def __init__(self, cfg_i):
        stride, _, h0, nh = ATT_CFG[cfg_i]
        self.d = stride if cfg_i < 3 else 1
        self.h0, self.nh = h0, nh
        if cfg_i < 3:
            self.nq, self.wkv = 1, 256
            self.q_unit = [cfg_i]
            self.k_unit, self.v_unit = 3 + cfg_i, 6 + cfg_i
            self.heads = [(0, 64 * i, 64 * i) for i in range(4)]
            self.sinks = False
        else:
            self.nq, self.wkv = 2, 128
            self.q_unit = [9, 10]
            self.k_unit, self.v_unit = 22, 23
            self.heads = [(qh // 4, 64 * (qh % 4), 64 * (qh // 4)) for qh in range(8)]
            self.sinks = True
        self.wq = 256 * self.nq


def _att_in_specs(cf, nb):
    uq, ukv = QKV_W // 256, QKV_W // cf.wkv
    specs = [pl.BlockSpec((None, BLOCK, 256), functools.partial(lambda b, r, j, u: (b, j, r * uq + u), u=u))
             for u in cf.q_unit]
    for unit in (cf.k_unit, cf.v_unit):
        specs.append(pl.BlockSpec((None, BLOCK, cf.wkv),
                                  functools.partial(lambda b, r, j, u: (b, jnp.maximum(j - 1, 0), r * ukv + u), u=unit)))
        specs.append(pl.BlockSpec((None, BLOCK, cf.wkv),
                                  functools.partial(lambda b, r, j, u: (b, j, r * ukv + u), u=unit)))
    return specs


def _att_logits(cf, hi, q, kh, bias_ref, first):
    s = _nt(q, kh) * (HEAD_DIM ** -0.5) + bias_ref[hi]
    return jnp.where(first, NEG, s)


def _attn_fwd(cf, zf, bias, sinks, name):
    bsz, l, _ = zf.shape
    nb = l // BLOCK

    def body(*refs):
        q_refs = refs[:cf.nq]
        kp_ref, kc_ref, vp_ref, vc_ref, bias_ref = refs[cf.nq:cf.nq + 5]
        rest = refs[cf.nq + 5:]
        sink_ref = rest[0] if cf.sinks else None
        o_ref, lse_ref = rest[-2:]
        j = pl.program_id(2)
        k = jnp.concatenate([kp_ref[...], kc_ref[...]], axis=0)
        v = jnp.concatenate([vp_ref[...], vc_ref[...]], axis=0)
        col = lax.broadcasted_iota(jnp.int32, (BLOCK, 2 * BLOCK), 1)
        first = (j == 0) & (col < BLOCK)
        for hi, (qb, qo, ko) in enumerate(cf.heads):
            s = _att_logits(cf, hi, q_refs[qb][:, qo:qo + HEAD_DIM], k[:, ko:ko + HEAD_DIM], bias_ref, first)
            m = jnp.max(s, axis=-1, keepdims=True)
            if cf.sinks:
                sk = sink_ref[hi:hi + 1, 0:1]
                m = jnp.maximum(m, sk)
            p = jnp.exp(s - m)
            den = jnp.sum(p, axis=-1, keepdims=True)
            if cf.sinks:
                den = den + jnp.exp(sk - m)
            o = _nn(p.astype(BF16), v[:, ko:ko + HEAD_DIM]) / den
            c0 = 256 * qb + qo
            o_ref[:, c0:c0 + HEAD_DIM] = o
            lse_ref[:, c0:c0 + HEAD_DIM] = jnp.broadcast_to(m + jnp.log(den), (BLOCK, HEAD_DIM))

    in_specs = _att_in_specs(cf, nb) + [pl.BlockSpec((cf.nh, BLOCK, 2 * BLOCK), lambda b, r, j: (0, 0, 0))]
    args = [zf] * (cf.nq + 4) + [bias]
    if cf.sinks:
        in_specs.append(pl.BlockSpec((cf.nh, 128), lambda b, r, j: (0, 0)))
        args.append(sinks)
    out = pl.BlockSpec((None, BLOCK, cf.wq), lambda b, r, j: (b, j, r))
    shape = jax.ShapeDtypeStruct((bsz, l, cf.d * cf.wq), F32)
    return pl.pallas_call(
        body, name=name, grid=(bsz, cf.d, nb), in_specs=in_specs, out_specs=[out, out], out_shape=[shape, shape],
        compiler_params=_params("parallel", "parallel", "arbitrary"),
    )(*args)


def _attn_bwd(cf, zf, bias, sinks, lse, lse_tot, dy, y, name):
    bsz, l, _ = zf.shape
    nb = l // BLOCK
    kv_offs = sorted({ko for _, _, ko in cf.heads})

    def body(*refs):
        q_refs = refs[:cf.nq]
        kp_ref, kc_ref, vp_ref, vc_ref, bias_ref = refs[cf.nq:cf.nq + 5]
        rest = list(refs[cf.nq + 5:])
        sink_ref = rest.pop(0) if cf.sinks else None
        lse_ref, lt_ref, dy_ref, y_ref = rest[:4]
        outs = rest[4:]
        dq_ref, dk_ref, dv_ref, dbias_ref = outs[:4]
        dsink_ref = outs[4] if cf.sinks else None
        dk_acc, dv_acc = outs[-2:]
        b, r, j = pl.program_id(0), pl.program_id(1), pl.program_id(2)

        @pl.when((b == 0) & (r == 0) & (j == 0))
        def _():
            dbias_ref[...] = jnp.zeros_like(dbias_ref)
            if cf.sinks:
                dsink_ref[...] = jnp.zeros_like(dsink_ref)

        @pl.when(j == 0)
        def _():
            dk_acc[...] = jnp.zeros_like(dk_acc)
            dv_acc[...] = jnp.zeros_like(dv_acc)

        k = jnp.concatenate([kp_ref[...], kc_ref[...]], axis=0)
        v = jnp.concatenate([vp_ref[...], vc_ref[...]], axis=0)
        col = lax.broadcasted_iota(jnp.int32, (BLOCK, 2 * BLOCK), 1)
        first = (j == 0) & (col < BLOCK)
        dk_parts = {ko: None for ko in kv_offs}
        dv_parts = {ko: None for ko in kv_offs}
        for hi, (qb, qo, ko) in enumerate(cf.heads):
            c0 = 256 * qb + qo
            q = q_refs[qb][:, qo:qo + HEAD_DIM]
            kh, vh = k[:, ko:ko + HEAD_DIM], v[:, ko:ko + HEAD_DIM]
            s = _att_logits(cf, hi, q, kh, bias_ref, first)
            lse_h = lse_ref[:, c0:c0 + 1]
            alpha = jnp.exp(lse_h - lt_ref[:, c0:c0 + 1])
            pa = alpha * jnp.exp(s - lse_h)
            dyh = dy_ref[:, c0:c0 + HEAD_DIM]
            e = jnp.sum(dyh * y_ref[:, c0:c0 + HEAD_DIM], axis=-1, keepdims=True)
            dyb = dyh.astype(BF16)
            ds = pa * (_nt(dyb, vh) - e)
            dbias_ref[hi] += ds
            if cf.sinks:
                dsink_ref[:, hi:hi + 1] += -(jnp.exp(sink_ref[hi:hi + 1, 0:1] - lse_h) * e)
            dsb = ds.astype(BF16)
            dq_ref[:, c0:c0 + HEAD_DIM] = (_nn(dsb, kh) * (HEAD_DIM ** -0.5)).astype(BF16)
            dk_h = _tn(dsb, q) * (HEAD_DIM ** -0.5)
            dv_h = _tn(pa.astype(BF16), dyb)
            dk_parts[ko] = dk_h if dk_parts[ko] is None else dk_parts[ko] + dk_h
            dv_parts[ko] = dv_h if dv_parts[ko] is None else dv_parts[ko] + dv_h
        cur = pl.ds(pl.multiple_of(j * BLOCK, BLOCK), BLOCK)
        for ko in kv_offs:
            dk_acc[cur, ko:ko + HEAD_DIM] += dk_parts[ko][BLOCK:]
            dv_acc[cur, ko:ko + HEAD_DIM] += dv_parts[ko][BLOCK:]

        @pl.when(j > 0)
        def _():
            prev = pl.ds(pl.multiple_of((j - 1) * BLOCK, BLOCK), BLOCK)
            for ko in kv_offs:
                dk_acc[prev, ko:ko + HEAD_DIM] += dk_parts[ko][:BLOCK]
                dv_acc[prev, ko:ko + HEAD_DIM] += dv_parts[ko][:BLOCK]

        @pl.when(j == nb - 1)
        def _():
            dk_ref[...] = dk_acc[...].astype(BF16)
            dv_ref[...] = dv_acc[...].astype(BF16)

    in_specs = _att_in_specs(cf, nb) + [pl.BlockSpec((cf.nh, BLOCK, 2 * BLOCK), lambda b, r, j: (0, 0, 0))]
    args = [zf] * (cf.nq + 4) + [bias]
    if cf.sinks:
        in_specs.append(pl.BlockSpec((cf.nh, 128), lambda b, r, j: (0, 0)))
        args.append(sinks)
    tok = pl.BlockSpec((None, BLOCK, cf.wq), lambda b, r, j: (b, j, r))
    in_specs += [tok] * 4
    args += [lse, lse_tot, dy, y]
    seq = pl.BlockSpec((None, l, cf.wkv), lambda b, r, j: (b, 0, r))
    out_specs = [tok, seq, seq, pl.BlockSpec((cf.nh, BLOCK, 2 * BLOCK), lambda b, r, j: (0, 0, 0))]
    out_shape = [jax.ShapeDtypeStruct((bsz, l, cf.d * cf.wq), BF16),
                 jax.ShapeDtypeStruct((bsz, l, cf.d * cf.wkv), BF16),
                 jax.ShapeDtypeStruct((bsz, l, cf.d * cf.wkv), BF16),
                 jax.ShapeDtypeStruct((cf.nh, BLOCK, 2 * BLOCK), F32)]
    if cf.sinks:
        out_specs.append(pl.BlockSpec((BLOCK, 128), lambda b, r, j: (0, 0)))
        out_shape.append(jax.ShapeDtypeStruct((BLOCK, 128), F32))
    return pl.pallas_call(
        body, name=name, grid=(bsz, cf.d, nb), in_specs=in_specs, out_specs=out_specs, out_shape=out_shape,
        scratch_shapes=[pltpu.VMEM((l, cf.wkv), F32), pltpu.VMEM((l, cf.wkv), F32)],
        compiler_params=_params("arbitrary", "arbitrary", "arbitrary"),
    )(*args)


def _merge_fwd(o_a, lse_a, o_b, zg, h, wa_t, wb_t, wout, name):
    t = h.shape[0]
    tm = 512

    def body(o1, o2, o3, l1, l2, l3, ob_ref, zg_ref, h_ref, wa_ref, wb_ref, wo_ref, h2_ref, ya_ref, lt_ref):
        m = jnp.maximum(jnp.maximum(l1[...], l2[...]), l3[...])
        e1, e2, e3 = jnp.exp(l1[...] - m), jnp.exp(l2[...] - m), jnp.exp(l3[...] - m)
        se = e1 + e2 + e3
        ya = (e1 / se) * o1[...] + (e2 / se) * o2[...] + (e3 / se) * o3[...]
        ya_ref[...] = ya
        lt_ref[...] = m + jnp.log(se)
        pa = _nt(ya.astype(BF16), wa_ref[...])
        pb = _nt(ob_ref[...].astype(BF16), wb_ref[...])
        merged = jax.nn.sigmoid(zg_ref[:, :D_MODEL]) * pa + jax.nn.sigmoid(zg_ref[:, D_MODEL:]) * pb
        h2_ref[...] = h_ref[...] + _nn(merged.astype(BF16), wo_ref[...])

    row = lambda w: pl.BlockSpec((tm, w), lambda i: (i, 0))
    full = _resident
    return pl.pallas_call(
        body, name=name, grid=(t // tm,),
        in_specs=[row(256)] * 6 + [row(512), row(GATE_W), row(D_MODEL), full(wa_t), full(wb_t), full(wout)],
        out_specs=[row(D_MODEL), row(256), row(256)],
        out_shape=[jax.ShapeDtypeStruct((t, D_MODEL), F32), jax.ShapeDtypeStruct((t, 256), F32),
                   jax.ShapeDtypeStruct((t, 256), F32)],
        compiler_params=_params("parallel"),
    )(*o_a, *lse_a, o_b, zg, h, wa_t, wb_t, wout)


def _merge_bwd(dh, ya, o_b, zg, wa_t, wb_t, wout, name):
    t = dh.shape[0]
    tm = 256

    def body(dh_ref, ya_ref, ob_ref, zg_ref, wa_ref, wb_ref, wo_ref,
             mg_ref, dpa_ref, dpb_ref, dzg_ref, dya_ref, dyb_ref):
        dm = _nt(dh_ref[...].astype(BF16), wo_ref[...])
        pa = _nt(ya_ref[...].astype(BF16), wa_ref[...])
        pb = _nt(ob_ref[...].astype(BF16), wb_ref[...])
        sa = jax.nn.sigmoid(zg_ref[:, :D_MODEL])
        sb = jax.nn.sigmoid(zg_ref[:, D_MODEL:])
        mg_ref[...] = (sa * pa + sb * pb).astype(BF16)
        dpa = (dm * sa).astype(BF16)
        dpb = (dm * sb).astype(BF16)
        dpa_ref[...] = dpa
        dpb_ref[...] = dpb
        dzg_ref[:, :D_MODEL] = (dm * pa * (sa * (1.0 - sa))).astype(BF16)
        dzg_ref[:, D_MODEL:] = (dm * pb * (sb * (1.0 - sb))).astype(BF16)
        dya_ref[...] = _nn(dpa, wa_ref[...])
        dyb_ref[...] = _nn(dpb, wb_ref[...])

    row = lambda w: pl.BlockSpec((tm, w), lambda i: (i, 0))
    full = _resident
    sds = jax.ShapeDtypeStruct
    return pl.pallas_call(
        body, name=name, grid=(t // tm,),
        in_specs=[row(D_MODEL), row(256), row(512), row(GATE_W), full(wa_t), full(wb_t), full(wout)],
        out_specs=[row(D_MODEL), row(D_MODEL), row(D_MODEL), row(GATE_W), row(256), row(512)],
        out_shape=[sds((t, D_MODEL), BF16), sds((t, D_MODEL), BF16), sds((t, D_MODEL), BF16),
                   sds((t, GATE_W), BF16), sds((t, 256), F32), sds((t, 512), F32)],
        compiler_params=_params("parallel"),
    )(dh, ya, o_b, zg, wa_t, wb_t, wout)


def _loss_head(h, g, target, name):
    t = h.shape[0]
    tm = 1024

    def body(h_ref, g_ref, t_ref, dh_ref, loss_ref, dg_ref):
        @pl.when(pl.program_id(0) == 0)
        def _():
            loss_ref[...] = jnp.zeros_like(loss_ref)
            dg_ref[...] = jnp.zeros_like(dg_ref)

        xhat, r = _rms(h_ref[...], g_ref[...])
        err = xhat * g_ref[...] - t_ref[...]
        loss_ref[...] += 0.5 * jnp.sum(jnp.mean(err * err, axis=-1, keepdims=True), axis=0, keepdims=True)
        dx, dg = _rms_bwd(err * (1.0 / D_MODEL), xhat, r, g_ref[...])
        dh_ref[...] = dx
        dg_ref[...] += dg

    row = pl.BlockSpec((tm, D_MODEL), lambda i: (i, 0))
    vec = pl.BlockSpec((1, D_MODEL), lambda i: (0, 0))
    return pl.pallas_call(
        body, name=name, grid=(t // tm,),
        in_specs=[row, vec, row], out_specs=[row, pl.BlockSpec((1, 128), lambda i: (0, 0)), vec],
        out_shape=[jax.ShapeDtypeStruct((t, D_MODEL), F32), jax.ShapeDtypeStruct((1, 128), F32),
                   jax.ShapeDtypeStruct((1, D_MODEL), F32)],
        compiler_params=_params("arbitrary"),
    )(h, g, target)


def _place():
    return lax.axis_index("x"), lax.axis_index("y"), lax.axis_index("c")


def _all_gather(shards, name):
    n = len(shards)

    def body(*refs):
        ins, outs = refs[:n], refs[n:2 * n]
        send_sems, recv_sems, local_sems = refs[2 * n:]
        x, y, c = _place()
        me, sibling = (x, y, c), (x, y, 1 - c)
        chips = [(1 - x, y), (x, 1 - y), (1 - x, 1 - y)]

        def copy(i, k, block, to, src=None):
            dst = outs[i].at[4 * block[0] + 2 * block[1] + block[2]]
            return pltpu.make_async_remote_copy(
                src_ref=dst if src is None else src, dst_ref=dst, send_sem=send_sems.at[7 * i + k],
                recv_sem=recv_sems.at[7 * i + k], device_id=to, device_id_type=MESH)

        mine = [pltpu.make_async_copy(ins[i], outs[i].at[4 * x + 2 * y + c], local_sems.at[i]) for i in range(n)]
        first = []
        for i in range(n):
            mine[i].start()
            first.append(copy(i, 0, me, sibling, src=ins[i]))
            first += [copy(i, 1 + j, me, (*chip, c), src=ins[i]) for j, chip in enumerate(chips)]
        for cp in first:
            cp.start()
        passed = []
        for j, chip in enumerate(chips):
            for i in range(n):
                copy(i, 1 + j, (*chip, c), me).wait_recv()
                fwd = copy(i, 4 + j, (*chip, c), sibling)
                fwd.start()
                passed.append(fwd)
        for i in range(n):
            copy(i, 0, sibling, me).wait_recv()
            for j, chip in enumerate(chips):
                copy(i, 4 + j, (*chip, 1 - c), me).wait_recv()
        for cp in first + passed:
            cp.wait_send()
        for cp in mine:
            cp.wait()

    return pl.pallas_call(
        body, name=name, in_specs=[ANY] * n, out_specs=[ANY] * n,
        out_shape=[jax.ShapeDtypeStruct((N_DEV,) + s.shape, s.dtype) for s in shards],
        scratch_shapes=[pltpu.SemaphoreType.DMA((7 * n,)), pltpu.SemaphoreType.DMA((7 * n,)),
                        pltpu.SemaphoreType.DMA((n,))],
    )(*shards)


def _exchange_core(grads, name):
    n = len(grads)

    def body(*refs):
        ins, outs = refs[:n], refs[n:2 * n]
        send_sems, recv_sems = refs[2 * n:]
        x, y, c = _place()
        copies = [pltpu.make_async_remote_copy(
            src_ref=ins[i].at[:, pl.ds(1 - c, 1)], dst_ref=outs[i], send_sem=send_sems.at[i],
            recv_sem=recv_sems.at[i], device_id=(x, y, 1 - c), device_id_type=MESH) for i in range(n)]
        for cp in copies:
            cp.start()
        for cp in copies:
            cp.wait()

    return pl.pallas_call(
        body, name=name, in_specs=[ANY] * n, out_specs=[ANY] * n,
        out_shape=[jax.ShapeDtypeStruct((N_CHIP, 1) + g.shape[2:], g.dtype) for g in grads],
        scratch_shapes=[pltpu.SemaphoreType.DMA((n,)), pltpu.SemaphoreType.DMA((n,))],
    )(*grads)


def _pair_sum(grad, got, name):
    _, _, r, cdim = grad.shape
    core = lax.axis_index("c").astype(jnp.int32).reshape(1)

    def body(core_ref, g_ref, s_ref, o_ref):
        o_ref[...] = (g_ref[...] + s_ref[...]).astype(BF16)

    return pl.pallas_call(
        body, name=name,
        grid_spec=pltpu.PrefetchScalarGridSpec(
            num_scalar_prefetch=1, grid=(N_CHIP,),
            in_specs=[pl.BlockSpec((None, None, r, cdim), lambda q, core_ref: (q, core_ref[0], 0, 0)),
                      pl.BlockSpec((None, None, r, cdim), lambda q, core_ref: (q, 0, 0, 0))],
            out_specs=pl.BlockSpec((None, r, cdim), lambda q, core_ref: (q, 0, 0))),
        out_shape=jax.ShapeDtypeStruct((N_CHIP, r, cdim), BF16),
        compiler_params=_params("parallel"),
    )(core, grad, got)


def _exchange_chip(parts, name):
    n = len(parts)

    def body(*refs):
        ins, outs = refs[:n], refs[n:2 * n]
        send_sems, recv_sems, local_sems = refs[2 * n:]
        x, y, c = _place()
        chips = [(1 - x, y), (x, 1 - y), (1 - x, 1 - y)]
        my_chip = 2 * x + y
        copies = []
        for i in range(n):
            own = pltpu.make_async_copy(ins[i].at[my_chip], outs[i].at[my_chip], local_sems.at[i])
            own.start()
            copies.append(own)
            for k, (qx, qy) in enumerate(chips):
                cp = pltpu.make_async_remote_copy(
                    src_ref=ins[i].at[2 * qx + qy], dst_ref=outs[i].at[my_chip], send_sem=send_sems.at[3 * i + k],
                    recv_sem=recv_sems.at[3 * i + k], device_id=(qx, qy, c), device_id_type=MESH)
                cp.start()
                copies.append(cp)
        for cp in copies:
            cp.wait()

    return pl.pallas_call(
        body, name=name, in_specs=[ANY] * n, out_specs=[ANY] * n,
        out_shape=[jax.ShapeDtypeStruct(p.shape, p.dtype) for p in parts],
        scratch_shapes=[pltpu.SemaphoreType.DMA((3 * n,)), pltpu.SemaphoreType.DMA((3 * n,)),
                        pltpu.SemaphoreType.DMA((n,))],
    )(*parts)


def _chip_sum(parts, name):
    def body(p_ref, o_ref):
        o_ref[...] = ((p_ref[0].astype(F32) + p_ref[1].astype(F32)) + p_ref[2].astype(F32)) + p_ref[3].astype(F32)

    return pl.pallas_call(
        body, name=name, out_shape=jax.ShapeDtypeStruct(parts.shape[1:], F32),
        compiler_params=pltpu.CompilerParams(vmem_limit_bytes=VMEM_LIMIT),
    )(parts)


def _adamw_math(w, g, m, v):
    m = ADAM_B1 * m + (1.0 - ADAM_B1) * g
    v = ADAM_B2 * v + (1.0 - ADAM_B2) * jnp.square(g)
    m_hat = m / (1.0 - ADAM_B1 ** ADAM_STEP)
    v_hat = v / (1.0 - ADAM_B2 ** ADAM_STEP)
    delta = -ADAM_LR * (m_hat / (jnp.sqrt(v_hat) + ADAM_EPS) + ADAM_WD * w)
    return delta, m, v


def _adamw(w, g, m, v, name):
    def body(w_ref, g_ref, m_ref, v_ref, d_ref, m2_ref, v2_ref):
        d_ref[...], m2_ref[...], v2_ref[...] = _adamw_math(w_ref[...], g_ref[...], m_ref[...], v_ref[...])

    sds = jax.ShapeDtypeStruct(w.shape, F32)
    return pl.pallas_call(body, name=name, out_shape=[sds, sds, sds],
                          compiler_params=pltpu.CompilerParams(vmem_limit_bytes=VMEM_LIMIT))(w, g, m, v)


SMALL_ROWS = 80


def _small_update(g, w, m, v, name):
    def body(g_ref, w_ref, m_ref, v_ref, gs_ref, d_ref, m2_ref, v2_ref, got_ref, send_sems, recv_sems):
        x, y, c = _place()
        me = 4 * x + 2 * y + c
        got_ref[me] = g_ref[...]
        copies = []
        for k in range(1, N_DEV):
            peer = (x ^ (k >> 2), y ^ ((k >> 1) & 1), c ^ (k & 1))
            cp = pltpu.make_async_remote_copy(
                src_ref=g_ref, dst_ref=got_ref.at[me], send_sem=send_sems.at[k - 1], recv_sem=recv_sems.at[k - 1],
                device_id=peer, device_id_type=MESH)
            cp.start()
            copies.append(cp)
        for cp in copies:
            cp.wait()
        total = got_ref[0]
        for k in range(1, N_DEV):
            total = total + got_ref[k]
        gs_ref[...] = total
        d_ref[...], m2_ref[...], v2_ref[...] = _adamw_math(w_ref[...], total, m_ref[...], v_ref[...])

    sds = jax.ShapeDtypeStruct((SMALL_ROWS, 128), F32)
    vm = pl.BlockSpec(memory_space=pltpu.VMEM)
    return pl.pallas_call(
        body, name=name, in_specs=[vm] * 4, out_specs=[vm] * 4, out_shape=[sds] * 4,
        scratch_shapes=[pltpu.VMEM((N_DEV, SMALL_ROWS, 128), F32), pltpu.SemaphoreType.DMA((N_DEV - 1,)),
                        pltpu.SemaphoreType.DMA((N_DEV - 1,))],
    )(g, w, m, v)


def _pack_small(gains, b_in, rel_bias, sinks, last):
    rows = [a.reshape(8, 128) for a in gains] + [b_in.reshape(40, 128), rel_bias.reshape(5, 128),
                                                 jnp.pad(sinks.reshape(1, 8), ((0, 0), (0, 120))), last]
    rows.append(jnp.zeros((SMALL_ROWS - 79, 128), F32))
    return jnp.concatenate(rows, axis=0)


def _unpack_small(p, like):
    out = [p[8 * i:8 * i + 8].reshape(like[i].shape) for i in range(4)]
    out.append(p[32:72].reshape(like[4].shape))
    out.append(p[72:77].reshape(like[5].shape))
    out.append(p[77, :8].reshape(like[6].shape))
    return out


def kernel(x, ffn1_norm, ffn1_w_gate, ffn1_w_up, ffn1_w_down, mix_norm, w_in, b_in, w_branch_a, w_branch_b, w_out, sinks, rel_bias, ffn2_norm, ffn2_w_gate, ffn2_w_up, ffn2_w_down, final_norm, loss_target, m_ffn1_norm, m_ffn1_w_gate, m_ffn1_w_up, m_ffn1_w_down, m_mix_norm, m_w_in, m_b_in, m_w_branch_a, m_w_branch_b, m_w_out, m_sinks, m_rel_bias, m_ffn2_norm, m_ffn2_w_gate, m_ffn2_w_up, m_ffn2_w_down, m_final_norm, v_ffn1_norm, v_ffn1_w_gate, v_ffn1_w_up, v_ffn1_w_down, v_mix_norm, v_w_in, v_b_in, v_w_branch_a, v_w_branch_b, v_w_out, v_sinks, v_rel_bias, v_ffn2_norm, v_ffn2_w_gate, v_ffn2_w_up, v_ffn2_w_down, v_final_norm):
    bsz, seq, _ = x.shape
    t = bsz * seq
    xt = x.reshape(t, D_MODEL)
    target = loss_target.reshape(t, D_MODEL)

    big = [("ffn1_w_gate", ffn1_w_gate, m_ffn1_w_gate, v_ffn1_w_gate, True),
           ("ffn1_w_up", ffn1_w_up, m_ffn1_w_up, v_ffn1_w_up, True),
           ("ffn1_w_down", ffn1_w_down, m_ffn1_w_down, v_ffn1_w_down, False),
           ("w_in", w_in, m_w_in, v_w_in, True),
           ("w_branch_a", w_branch_a, m_w_branch_a, v_w_branch_a, True),
           ("w_branch_b", w_branch_b, m_w_branch_b, v_w_branch_b, True),
           ("w_out", w_out, m_w_out, v_w_out, False),
           ("ffn2_w_gate", ffn2_w_gate, m_ffn2_w_gate, v_ffn2_w_gate, True),
           ("ffn2_w_up", ffn2_w_up, m_ffn2_w_up, v_ffn2_w_up, True),
           ("ffn2_w_down", ffn2_w_down, m_ffn2_w_down, v_ffn2_w_down, False)]
    shards = [(w[0].T if tr else w[0]).astype(BF16) for _, w, _, _, tr in big]
    gathered = _all_gather(shards, "all_gather_weights")
    full = {nm: gw.reshape(-1, gw.shape[-1]) for (nm, _, _, _, _), gw in zip(big, gathered)}

    g1, gm, g2, gf = ffn1_norm, mix_norm, ffn2_norm, final_norm.reshape(1, D_MODEL)

    h1, n1 = _ffn_fwd(xt, g1, full["ffn1_w_gate"], full["ffn1_w_up"], full["ffn1_w_down"], "ffn1_fwd")
    u, zq, zg = _inproj_fwd(h1, gm, full["w_in"], b_in, "inproj_fwd")
    buckets = _bucket_tiles()
    bias = _bias_build(rel_bias, buckets)
    sink_rows = jnp.broadcast_to(sinks.reshape(8, 1), (8, 128))
    cfs = [_Att(i) for i in range(4)]
    zfold = [zq.reshape(bsz, seq // cf.d, cf.d * QKV_W) for cf in cfs]
    att = []
    for i, cf in enumerate(cfs):
        o, lse = _attn_fwd(cf, zfold[i], bias[cf.h0:cf.h0 + cf.nh], sink_rows, f"attn{i}_fwd")
        att.append((o.reshape(t, cf.wq), lse.reshape(t, cf.wq)))
    o_b, lse_b = att[3]
    h2, ya, lse_tot = _merge_fwd([a[0] for a in att[:3]], [a[1] for a in att[:3]], o_b, zg, h1,
                                 full["w_branch_a"], full["w_branch_b"], full["w_out"], "merge_fwd")
    h3, n2 = _ffn_fwd(h2, g2, full["ffn2_w_gate"], full["ffn2_w_up"], full["ffn2_w_down"], "ffn2_fwd")
    dh3, loss_part, dgf = _loss_head(h3, gf, target, "loss_head")

    grads = {}
    dh2, dg2, hff, da, db, dhh = _ffn_bwd(h2, n2, g2, dh3, full["ffn2_w_gate"], full["ffn2_w_up"],
                                          full["ffn2_w_down"], "ffn2_bwd")
    grads["ffn2_w_gate"] = _tn_matmul(da, n2, 1408, "ffn2_dgate")
    grads["ffn2_w_up"] = _tn_matmul(db, n2, 1408, "ffn2_dup")
    grads["ffn2_w_down"] = _tn_matmul(hff, dhh, 1408, "ffn2_ddown")
    merged, dpa, dpb, dzg, dya, dyb = _merge_bwd(dh2, ya, o_b, zg, full["w_branch_a"], full["w_branch_b"],
                                                 full["w_out"], "merge_bwd")
    grads["w_out"] = _tn_matmul(merged, dh2, 1024, "dw_out")
    grads["w_branch_a"] = _tn_matmul(dpa, ya, 1024, "dw_branch_a")
    grads["w_branch_b"] = _tn_matmul(dpb, o_b, 1024, "dw_branch_b")
    dq, dk, dv, dbias, dsink = [], [], [], [], None
    for i, cf in enumerate(cfs):
        if i < 3:
            lse_i, lt_i, dy_i, y_i = att[i][1], lse_tot, dya, ya
        else:
            lse_i, lt_i, dy_i, y_i = lse_b, lse_b, dyb, o_b
        shp = (bsz, seq // cf.d, cf.d * cf.wq)
        res = _attn_bwd(cf, zfold[i], bias[cf.h0:cf.h0 + cf.nh], sink_rows, lse_i.reshape(shp), lt_i.reshape(shp),
                        dy_i.reshape(shp), y_i.reshape(shp), f"attn{i}_bwd")
        dq.append(res[0].reshape(t, cf.wq))
        dk.append(res[1].reshape(t, cf.wkv))
        dv.append(res[2].reshape(t, cf.wkv))
        dbias.append(res[3])
        if cf.sinks:
            dsink = res[4]
    pieces = dq[:3] + dk[:3] + dv[:3] + [dq[3], dk[3], dv[3], dzg]
    dh1, dz, db_in, dgm = _inproj_bwd(pieces, h1, gm, dh2, full["w_in"], "inproj_bwd")
    grads["w_in"] = _tn_matmul(dz, u, 1280, "dw_in")
    dx, dg1, hff, da, db, dhh = _ffn_bwd(xt, n1, g1, dh1, full["ffn1_w_gate"], full["ffn1_w_up"],
                                         full["ffn1_w_down"], "ffn1_bwd")
    grads["ffn1_w_gate"] = _tn_matmul(da, n1, 1408, "ffn1_dgate")
    grads["ffn1_w_up"] = _tn_matmul(db, n1, 1408, "ffn1_dup")
    grads["ffn1_w_down"] = _tn_matmul(hff, dhh, 1408, "ffn1_ddown")
    dtable, dsinks = _bias_reduce(jnp.concatenate(dbias, axis=0), buckets, dsink)

    names = [b[0] for b in big]
    by_owner = [grads[nm].reshape(N_CHIP, 2, -1, grads[nm].shape[-1]) for nm in names]
    from_sibling = _exchange_core(by_owner, "reduce_scatter_core")
    pair = [_pair_sum(g, s, f"pair_sum_{nm}") for nm, g, s in zip(names, by_owner, from_sibling)]
    from_chips = _exchange_chip(pair, "reduce_scatter_chip")
    out_g, out_d, out_m, out_v = {}, {}, {}, {}
    for (nm, w, m, v, tr), parts in zip(big, from_chips):
        g = _chip_sum(parts, f"chip_sum_{nm}")
        g = (g.T if tr else g).reshape(w.shape)
        out_g[nm] = g
        d2, m2, v2 = _adamw(w[0], g[0], m[0], v[0], f"adamw_{nm}")
        out_d[nm], out_m[nm], out_v[nm] = d2[None], m2[None], v2[None]

    small = [("ffn1_norm", ffn1_norm, m_ffn1_norm, v_ffn1_norm), ("mix_norm", mix_norm, m_mix_norm, v_mix_norm),
             ("ffn2_norm", ffn2_norm, m_ffn2_norm, v_ffn2_norm), ("final_norm", final_norm, m_final_norm, v_final_norm),
             ("b_in", b_in, m_b_in, v_b_in), ("rel_bias", rel_bias, m_rel_bias, v_rel_bias),
             ("sinks", sinks, m_sinks, v_sinks)]
    zero_row = jnp.zeros((1, 128), F32)
    pack = lambda arrs, last: _pack_small(arrs[:4], arrs[4], arrs[5], arrs[6], last)
    g_small = pack([dg1, dgm, dg2, dgf, db_in, dtable[:, :TOTAL_HEADS], dsinks[0, :8]], loss_part)
    packed = [pack([s[k] for s in small], zero_row) for k in (1, 2, 3)]
    gs, ds, ms, vs = _small_update(g_small, *packed, "small_update")
    like = [s[1] for s in small]
    for nm_s, g_, d_, m_, v_ in zip([s[0] for s in small], _unpack_small(gs, like), _unpack_small(ds, like),
                                    _unpack_small(ms, like), _unpack_small(vs, like)):
        out_g[nm_s], out_d[nm_s], out_m[nm_s], out_v[nm_s] = g_, d_, m_, v_
    loss = gs[78, 0]

    order = ["ffn1_norm", "ffn1_w_gate", "ffn1_w_up", "ffn1_w_down", "mix_norm", "w_in", "b_in", "w_branch_a",
             "w_branch_b", "w_out", "sinks", "rel_bias", "ffn2_norm", "ffn2_w_gate", "ffn2_w_up", "ffn2_w_down",
             "final_norm"]
    return (loss, dx.reshape(x.shape), *[out_g[k] for k in order], *[out_d[k] for k in order],
            *[out_m[k] for k in order], *[out_v[k] for k in order])
```

```python
import functools
import math

import numpy as np
import jax
import jax.numpy as jnp
from jax import lax
from jax.experimental import pallas as pl
from jax.experimental.pallas import tpu as pltpu

D_MODEL = 1024
D_FF = 2816
FF_CHUNK = 256
HEAD_DIM = 64
BLOCK = 128
N_BUCKETS = 32
MAX_DISTANCE = 2048
A_HEADS = 12
TOTAL_HEADS = 20
DIL_GROUPS = ((128, 1), (512, 4), (2048, 16))
B_WINDOW = 128
QKV_W = 3072
GATE_W = 2048
D_IN = QKV_W + GATE_W
EPS = 1e-6
NEG = -1e30
N_DEV = 8
N_CHIP = 4
ADAM_LR, ADAM_B1, ADAM_B2, ADAM_EPS, ADAM_WD, ADAM_STEP = 0.001, 0.9, 0.999, 1e-08, 0.01, 10
VMEM_LIMIT = 56 * 1024 * 1024
MESH = pl.DeviceIdType.MESH
BF16 = jnp.bfloat16
F32 = jnp.float32
ANY = pl.BlockSpec(memory_space=pl.ANY)


def _params(*sem):
    return pltpu.CompilerParams(dimension_semantics=sem, vmem_limit_bytes=VMEM_LIMIT)


def _resident(a):
    return pl.BlockSpec(a.shape, lambda i: (0, 0), pipeline_mode=pl.Buffered(1))


def _place():
    return lax.axis_index("x"), lax.axis_index("y"), lax.axis_index("c")


class _Gather:
    def __init__(self, shards):
        self.ins = list(shards)
        n = self.n = len(shards)
        self.out_shape = [jax.ShapeDtypeStruct((N_DEV,) + s.shape, s.dtype) for s in shards]
        self.scratch = [pltpu.SemaphoreType.DMA((7 * n,)), pltpu.SemaphoreType.DMA((7 * n,)),
                        pltpu.SemaphoreType.DMA((n,))]

    def _copies(self, ins, outs, sems):
        send_sems, recv_sems, local_sems = sems
        x, y, c = _place()
        me, sibling = (x, y, c), (x, y, 1 - c)
        chips = [(1 - x, y), (x, 1 - y), (1 - x, 1 - y)]

        def copy(i, k, block, to, src=None):
            dst = outs[i].at[4 * block[0] + 2 * block[1] + block[2]]
            return pltpu.make_async_remote_copy(
                src_ref=dst if src is None else src, dst_ref=dst, send_sem=send_sems.at[7 * i + k],
                recv_sem=recv_sems.at[7 * i + k], device_id=to, device_id_type=MESH)

        n = self.n
        mine = [pltpu.make_async_copy(ins[i], outs[i].at[4 * x + 2 * y + c], local_sems.at[i]) for i in range(n)]
        first = [copy(i, 0, me, sibling, src=ins[i]) for i in range(n)]
        first += [copy(i, 1 + j, me, (*chip, c), src=ins[i]) for i in range(n) for j, chip in enumerate(chips)]
        landed = [copy(i, 1 + j, (*chip, c), me) for j, chip in enumerate(chips) for i in range(n)]
        passed = [copy(i, 4 + j, (*chip, c), sibling) for j, chip in enumerate(chips) for i in range(n)]
        from_sibling = [copy(i, 0, sibling, me) for i in range(n)]
        from_sibling += [copy(i, 4 + j, (*chip, 1 - c), me) for i in range(n) for j, chip in enumerate(chips)]
        return mine, first, landed, passed, from_sibling

    def start(self, ins, outs, sems):
        mine, first, _, _, _ = self._copies(ins, outs, sems)
        for cp in mine + first:
            cp.start()

    def mid(self, ins, outs, sems):
        _, _, landed, passed, _ = self._copies(ins, outs, sems)
        for got, fwd in zip(landed, passed):
            got.wait_recv()
            fwd.start()

    def end(self, ins, outs, sems):
        mine, first, _, passed, from_sibling = self._copies(ins, outs, sems)
        for cp in from_sibling:
            cp.wait_recv()
        for cp in first + passed:
            cp.wait_send()
        for cp in mine:
            cp.wait()


class _CoreExchange:
    def __init__(self, grads):
        self.ins = list(grads)
        n = self.n = len(grads)
        self.out_shape = [jax.ShapeDtypeStruct((N_CHIP, 1) + g.shape[2:], g.dtype) for g in grads]
        self.scratch = [pltpu.SemaphoreType.DMA((n,)), pltpu.SemaphoreType.DMA((n,))]

    def _copies(self, ins, outs, sems):
        x, y, c = _place()
        return [pltpu.make_async_remote_copy(
            src_ref=ins[i].at[:, pl.ds(1 - c, 1)], dst_ref=outs[i], send_sem=sems[0].at[i],
            recv_sem=sems[1].at[i], device_id=(x, y, 1 - c), device_id_type=MESH) for i in range(self.n)]

    def start(self, ins, outs, sems):
        for cp in self._copies(ins, outs, sems):
            cp.start()

    mid = None

    def end(self, ins, outs, sems):
        for cp in self._copies(ins, outs, sems):
            cp.wait()


class _ChipExchange:
    def __init__(self, parts):
        self.ins = list(parts)
        n = self.n = len(parts)
        self.out_shape = [jax.ShapeDtypeStruct(p.shape, p.dtype) for p in parts]
        self.scratch = [pltpu.SemaphoreType.DMA((3 * n,)), pltpu.SemaphoreType.DMA((3 * n,)),
                        pltpu.SemaphoreType.DMA((n,))]

    def _copies(self, ins, outs, sems):
        send_sems, recv_sems, local_sems = sems
        x, y, c = _place()
        my_chip = 2 * x + y
        copies = []
        for i in range(self.n):
            copies.append(pltpu.make_async_copy(ins[i].at[my_chip], outs[i].at[my_chip], local_sems.at[i]))
            for k, (qx, qy) in enumerate([(1 - x, y), (x, 1 - y), (1 - x, 1 - y)]):
                copies.append(pltpu.make_async_remote_copy(
                    src_ref=ins[i].at[2 * qx + qy], dst_ref=outs[i].at[my_chip], send_sem=send_sems.at[3 * i + k],
                    recv_sem=recv_sems.at[3 * i + k], device_id=(qx, qy, c), device_id_type=MESH))
        return copies

    def start(self, ins, outs, sems):
        for cp in self._copies(ins, outs, sems):
            cp.start()

    mid = None

    def end(self, ins, outs, sems):
        for cp in self._copies(ins, outs, sems):
            cp.wait()


def _call(body, *, name, grid, in_specs, out_specs, out_shape, args, scratch=(), sem=None, hosted=()):
    n_in, n_out, n_scr = len(in_specs), len(out_specs), len(scratch)
    x_in = [len(p.ins) for p in hosted]
    x_scr = [len(p.scratch) for p in hosted]
    steps = int(np.prod(grid))
    assert not hosted or steps >= 3

    def wrapped(*refs):
        refs = list(refs)
        ins, refs = refs[:n_in], refs[n_in:]
        x_ins = [[refs.pop(0) for _ in range(k)] for k in x_in]
        outs, refs = refs[:n_out], refs[n_out:]
        x_outs = [[refs.pop(0) for _ in range(k)] for k in x_in]
        scr, refs = refs[:n_scr], refs[n_scr:]
        x_sems = [[refs.pop(0) for _ in range(k)] for k in x_scr]
        step = 0
        for d in range(len(grid)):
            step = step * grid[d] + pl.program_id(d)

        def phase(which, at):
            fns = [(getattr(p, which), a) for p, a in zip(hosted, zip(x_ins, x_outs, x_sems)) if getattr(p, which)]
            if fns:
                @pl.when(step == at)
                def _():
                    for fn, a in fns:
                        fn(*a)

        phase("start", 0)
        phase("mid", steps // 2)
        body(*ins, *outs, *scr)
        phase("end", steps - 1)

    results = pl.pallas_call(
        wrapped, name=name, grid=grid,
        in_specs=list(in_specs) + [ANY] * sum(x_in), out_specs=list(out_specs) + [ANY] * sum(x_in),
        out_shape=list(out_shape) + [s for p in hosted for s in p.out_shape],
        scratch_shapes=list(scratch) + [s for p in hosted for s in p.scratch],
        compiler_params=_params(*(("arbitrary",) * len(grid) if hosted else sem)),
    )(*args, *[a for p in hosted for a in p.ins])
    own, rest = list(results[:n_out]), list(results[n_out:])
    return own, [[rest.pop(0) for _ in range(k)] for k in x_in]


def _exchange(programs, name):
    x_in = [len(p.ins) for p in programs]
    x_scr = [len(p.scratch) for p in programs]

    def body(*refs):
        refs = list(refs)
        x_ins = [[refs.pop(0) for _ in range(k)] for k in x_in]
        x_outs = [[refs.pop(0) for _ in range(k)] for k in x_in]
        x_sems = [[refs.pop(0) for _ in range(k)] for k in x_scr]
        for which in ("start", "mid", "end"):
            for p, a in zip(programs, zip(x_ins, x_outs, x_sems)):
                if getattr(p, which):
                    getattr(p, which)(*a)

    results = list(pl.pallas_call(
        body, name=name, in_specs=[ANY] * sum(x_in), out_specs=[ANY] * sum(x_in),
        out_shape=[s for p in programs for s in p.out_shape],
        scratch_shapes=[s for p in programs for s in p.scratch],
    )(*[a for p in programs for a in p.ins]))
    return [[results.pop(0) for _ in range(k)] for k in x_in]


def _nt(a, b):
    return lax.dot_general(a, b, (((1,), (1,)), ((), ())), preferred_element_type=F32)


def _nn(a, b):
    return lax.dot_general(a, b, (((1,), (0,)), ((), ())), preferred_element_type=F32)


def _tn(a, b):
    return lax.dot_general(a, b, (((0,), (0,)), ((), ())), preferred_element_type=F32)


def _rms(x, g):
    r = lax.rsqrt(jnp.mean(x * x, axis=-1, keepdims=True) + EPS)
    return x * r, r


def _rms_bwd(dn, xhat, r, g):
    dg = jnp.sum(dn * xhat, axis=0, keepdims=True)
    dxh = dn * g
    dx = r * (dxh - xhat * jnp.mean(dxh * xhat, axis=-1, keepdims=True))
    return dx, dg


def _ffn_fwd(x, g, wg_t, wu_t, wd, name, hosted=()):
    t = x.shape[0]
    tm = 512

    def body(x_ref, g_ref, wg_ref, wu_ref, wd_ref, h_ref, n_ref, hff_ref):
        xhat, _ = _rms(x_ref[...], g_ref[...])
        n = (xhat * g_ref[...]).astype(BF16)
        n_ref[...] = n
        for c in range(0, D_FF, FF_CHUNK):
            a = _nt(n, wg_ref[c:c + FF_CHUNK, :])
            b = _nt(n, wu_ref[c:c + FF_CHUNK, :])
            hff_ref[:, c:c + FF_CHUNK] = (a * jax.nn.sigmoid(a) * b).astype(BF16)
        h_ref[...] = x_ref[...] + 0.5 * _nn(hff_ref[...], wd_ref[...])

    row = pl.BlockSpec((tm, D_MODEL), lambda i: (i, 0))
    return _call(
        body, name=name, grid=(t // tm,),
        in_specs=[row, _resident(g), _resident(wg_t), _resident(wu_t), _resident(wd)],
        out_specs=[row, row],
        out_shape=[jax.ShapeDtypeStruct((t, D_MODEL), F32), jax.ShapeDtypeStruct((t, D_MODEL), BF16)],
        scratch=[pltpu.VMEM((tm, D_FF), BF16)], sem=("parallel",), args=(x, g, wg_t, wu_t, wd), hosted=hosted)


def _ffn_bwd(x, n, g, dh, wg_t, wu_t, wd, name):
    t = x.shape[0]
    tm = 256

    def body(x_ref, n_ref, g_ref, dh_ref, wg_ref, wu_ref, wd_ref,
             dx_ref, dg_ref, hff_ref, da_ref, db_ref, dhh_ref):
        @pl.when(pl.program_id(0) == 0)
        def _():
            dg_ref[...] = jnp.zeros_like(dg_ref)

        dhh = (0.5 * dh_ref[...]).astype(BF16)
        dhh_ref[...] = dhh
        nb = n_ref[...]
        for c in range(0, D_FF, FF_CHUNK):
            cols = slice(c, c + FF_CHUNK)
            a = _nt(nb, wg_ref[cols, :])
            b = _nt(nb, wu_ref[cols, :])
            s = jax.nn.sigmoid(a)
            silu = a * s
            dhff = _nt(dhh, wd_ref[cols, :])
            hff_ref[:, cols] = (silu * b).astype(BF16)
            da_ref[:, cols] = (dhff * b * (s * (1.0 + a * (1.0 - s)))).astype(BF16)
            db_ref[:, cols] = (dhff * silu).astype(BF16)
        dn = _nn(da_ref[...], wg_ref[...]) + _nn(db_ref[...], wu_ref[...])
        xhat, r = _rms(x_ref[...], g_ref[...])
        dx, dg = _rms_bwd(dn, xhat, r, g_ref[...])
        dx_ref[...] = dh_ref[...] + dx
        dg_ref[...] += dg

    row = pl.BlockSpec((tm, D_MODEL), lambda i: (i, 0))
    hid = pl.BlockSpec((tm, D_FF), lambda i: (i, 0))
    return pl.pallas_call(
        body, name=name, grid=(t // tm,),
        in_specs=[row, row, _resident(g), row, _resident(wg_t), _resident(wu_t), _resident(wd)],
        out_specs=[row, pl.BlockSpec((1, D_MODEL), lambda i: (0, 0)), hid, hid, hid, row],
        out_shape=[jax.ShapeDtypeStruct((t, D_MODEL), F32), jax.ShapeDtypeStruct((1, D_MODEL), F32),
                   jax.ShapeDtypeStruct((t, D_FF), BF16), jax.ShapeDtypeStruct((t, D_FF), BF16),
                   jax.ShapeDtypeStruct((t, D_FF), BF16), jax.ShapeDtypeStruct((t, D_MODEL), BF16)],
        compiler_params=_params("arbitrary"),
    )(x, n, g, dh, wg_t, wu_t, wd)


def _tn_matmul(a, b, rc, name, hosted=()):
    t, r = a.shape
    c = b.shape[1]
    tk = 1024

    def body(a_ref, b_ref, o_ref):
        @pl.when(pl.program_id(1) == 0)
        def _():
            o_ref[...] = jnp.zeros_like(o_ref)

        o_ref[...] += _tn(a_ref[...].astype(BF16), b_ref[...].astype(BF16))

    (out,), got = _call(
        body, name=name, grid=(r // rc, t // tk),
        in_specs=[pl.BlockSpec((tk, rc), lambda i, k: (k, i)), pl.BlockSpec((tk, c), lambda i, k: (k, 0))],
        out_specs=[pl.BlockSpec((rc, c), lambda i, k: (i, 0))],
        out_shape=[jax.ShapeDtypeStruct((r, c), F32)],
        sem=("parallel", "arbitrary"), args=(a, b), hosted=hosted)
    return out, got


PIECE_W = (256,) * 9 + (512, 128, 128, GATE_W)


def _inproj_fwd(h, g, w_t, b_in, name):
    t = h.shape[0]
    tm, nc = 512, 512

    def body(h_ref, g_ref, w_ref, b_ref, u_ref, zq_ref, zg_ref):
        xhat, _ = _rms(h_ref[...], g_ref[...])
        u = (xhat * g_ref[...]).astype(BF16)
        u_ref[...] = u
        for c in range(D_IN // nc):
            z = _nt(u, w_ref[c * nc:(c + 1) * nc, :]) + b_ref[:, c * nc:(c + 1) * nc]
            if c < QKV_W // nc:
                zq_ref[:, c * nc:(c + 1) * nc] = z.astype(BF16)
            else:
                zg_ref[:, c * nc - QKV_W:(c + 1) * nc - QKV_W] = z

    row = lambda w: pl.BlockSpec((tm, w), lambda i: (i, 0))
    full = _resident
    return pl.pallas_call(
        body, name=name, grid=(t // tm,),
        in_specs=[row(D_MODEL), full(g), full(w_t), full(b_in)],
        out_specs=[row(D_MODEL), row(QKV_W), row(GATE_W)],
        out_shape=[jax.ShapeDtypeStruct((t, D_MODEL), BF16), jax.ShapeDtypeStruct((t, QKV_W), BF16),
                   jax.ShapeDtypeStruct((t, GATE_W), F32)],
        compiler_params=_params("parallel"),
    )(h, g, w_t, b_in)


def _inproj_bwd(pieces, h, g, dh_res, w_t, name):
    t = h.shape[0]
    tm = 256
    npiece = len(PIECE_W)
    offs = np.concatenate([[0], np.cumsum(PIECE_W)]).tolist()

    def body(*refs):
        p_refs = refs[:npiece]
        h_ref, g_ref, dhr_ref, w_ref, dh_ref, dz_ref, db_ref, dg_ref = refs[npiece:]
        i = pl.program_id(0)

        @pl.when(i == 0)
        def _():
            db_ref[...] = jnp.zeros_like(db_ref)
            dg_ref[...] = jnp.zeros_like(dg_ref)

        du = jnp.zeros((tm, D_MODEL), F32)
        for k in range(npiece):
            o, w = offs[k], PIECE_W[k]
            for c0 in range(0, w, 512):
                cw = min(512, w - c0)
                pz = p_refs[k][:, c0:c0 + cw]
                dz_ref[:, o + c0:o + c0 + cw] = pz
                db_ref[:, o + c0:o + c0 + cw] += jnp.sum(pz.astype(F32), axis=0, keepdims=True)
                du = du + _nn(pz, w_ref[o + c0:o + c0 + cw, :])
        xhat, r = _rms(h_ref[...], g_ref[...])
        dx, dg = _rms_bwd(du, xhat, r, g_ref[...])
        dh_ref[...] = dhr_ref[...] + dx
        dg_ref[...] += dg

    row = lambda w: pl.BlockSpec((tm, w), lambda i: (i, 0))
    full = lambda shp: pl.BlockSpec(shp, lambda i: (0, 0))
    return pl.pallas_call(
        body, name=name, grid=(t // tm,),
        in_specs=[row(w) for w in PIECE_W] + [row(D_MODEL), _resident(g), row(D_MODEL), _resident(w_t)],
        out_specs=[row(D_MODEL), row(D_IN), full((1, D_IN)), full((1, D_MODEL))],
        out_shape=[jax.ShapeDtypeStruct((t, D_MODEL), F32), jax.ShapeDtypeStruct((t, D_IN), BF16),
                   jax.ShapeDtypeStruct((1, D_IN), F32), jax.ShapeDtypeStruct((1, D_MODEL), F32)],
        compiler_params=_params("arbitrary"),
    )(*pieces, h, g, dh_res, w_t)


def _t5_bucket(dist):
    max_exact = N_BUCKETS // 2
    n = jnp.maximum(dist, 0)
    nf = jnp.maximum(n, 1).astype(jnp.float32)
    large = max_exact + (jnp.log(nf / max_exact) / math.log(MAX_DISTANCE / max_exact)
                         * (N_BUCKETS - max_exact)).astype(jnp.int32)
    large = jnp.minimum(large, N_BUCKETS - 1)
    return jnp.where(n < max_exact, n, large)


ATT_CFG = ((1, 128, 0, 4), (4, 128, 4, 4), (16, 128, 8, 4), (1, B_WINDOW - 1, A_HEADS, 8))


def _bucket_tiles():
    qi = jnp.arange(BLOCK)[:, None]
    ki = jnp.arange(2 * BLOCK)[None, :]
    dist = qi + BLOCK - ki
    return jnp.stack([_t5_bucket(dist * cfg[0]) for cfg in ATT_CFG]).astype(jnp.int32)


def _band(max_steps):
    row = lax.broadcasted_iota(jnp.int32, (BLOCK, 2 * BLOCK), 0)
    col = lax.broadcasted_iota(jnp.int32, (BLOCK, 2 * BLOCK), 1)
    dist = row + BLOCK - col
    return (dist >= 0) & (dist <= max_steps)


def _bias_build(table, buckets):
    def body(tab_ref, bt_ref, out_ref):
        for ci, (_, max_steps, h0, nh) in enumerate(ATT_CFG):
            bt = bt_ref[ci]
            band = _band(max_steps)
            for h in range(h0, h0 + nh):
                acc = lax.fori_loop(0, N_BUCKETS, lambda b, acc: jnp.where(bt == b, tab_ref[b, h], acc),
                                    jnp.zeros((BLOCK, 2 * BLOCK), F32))
                out_ref[h] = jnp.where(band, acc, NEG)

    return pl.pallas_call(
        body, name="bias_build",
        in_specs=[pl.BlockSpec(memory_space=pltpu.SMEM), pl.BlockSpec(memory_space=pltpu.VMEM)],
        out_specs=pl.BlockSpec(memory_space=pltpu.VMEM),
        out_shape=jax.ShapeDtypeStruct((TOTAL_HEADS, BLOCK, 2 * BLOCK), F32),
    )(table, buckets)


def _bias_reduce(dbias, buckets, dsink_rows):
    def body(db_ref, bt_ref, ds_ref, out_ref, sink_ref):
        ri = lax.broadcasted_iota(jnp.int32, (N_BUCKETS, 128), 0)
        ci = lax.broadcasted_iota(jnp.int32, (N_BUCKETS, 128), 1)

        def per_bucket(b, acc):
            for cfg_i, (_, _, h0, nh) in enumerate(ATT_CFG):
                hit = bt_ref[cfg_i] == b
                for h in range(h0, h0 + nh):
                    val = jnp.sum(jnp.where(hit, db_ref[h], 0.0))
                    acc = jnp.where((ri == b) & (ci == h), val, acc)
            return acc

        out_ref[...] = lax.fori_loop(0, N_BUCKETS, per_bucket, jnp.zeros((N_BUCKETS, 128), F32))
        sink_ref[...] = jnp.broadcast_to(jnp.sum(ds_ref[...], axis=0, keepdims=True), (8, 128))

    return pl.pallas_call(
        body, name="bias_reduce",
        in_specs=[pl.BlockSpec(memory_space=pltpu.VMEM)] * 3,
        out_specs=[pl.BlockSpec(memory_space=pltpu.VMEM)] * 2,
        out_shape=[jax.ShapeDtypeStruct((N_BUCKETS, 128), F32), jax.ShapeDtypeStruct((8, 128), F32)],
    )(dbias, buckets, dsink_rows)


class _Att:
    def __init__(self, cfg_i):
        stride, _, h0, nh = ATT_CFG[cfg_i]
        self.d = stride if cfg_i < 3 else 1
        self.h0, self.nh = h0, nh
        if cfg_i < 3:
            self.nq, self.wkv = 1, 256
            self.q_unit = [cfg_i]
            self.k_unit, self.v_unit = 3 + cfg_i, 6 + cfg_i
            self.heads = [(0, 64 * i, 64 * i) for i in range(4)]
            self.sinks = False
        else:
            self.nq, self.wkv = 2, 128
            self.q_unit = [9, 10]
            self.k_unit, self.v_unit = 22, 23
            self.heads = [(qh // 4, 64 * (qh % 4), 64 * (qh // 4)) for qh in range(8)]
            self.sinks = True
        self.wq = 256 * self.nq


def _att_in_specs(cf, nb):
    uq, ukv = QKV_W // 256, QKV_W // cf.wkv
    specs = [pl.BlockSpec((None, BLOCK, 256), functools.partial(lambda b, r, j, u: (b, j, r * uq + u), u=u))
             for u in cf.q_unit]
    for unit in (cf.k_unit, cf.v_unit):
        specs.append(pl.BlockSpec((None, BLOCK, cf.wkv),
                                  functools.partial(lambda b, r, j, u: (b, jnp.maximum(j - 1, 0), r * ukv + u), u=unit)))
        specs.append(pl.BlockSpec((None, BLOCK, cf.wkv),
                                  functools.partial(lambda b, r, j, u: (b, j, r * ukv + u), u=unit)))
    return specs


def _att_logits(cf, hi, q, kh, bias_ref, first):
    s = _nt(q, kh) * (HEAD_DIM ** -0.5) + bias_ref[hi]
    return jnp.where(first, NEG, s)


def _attn_fwd(cf, zf, bias, sinks, name, hosted=()):
    bsz, l, _ = zf.shape
    nb = l // BLOCK

    def body(*refs):
        q_refs = refs[:cf.nq]
        kp_ref, kc_ref, vp_ref, vc_ref, bias_ref = refs[cf.nq:cf.nq + 5]
        rest = refs[cf.nq + 5:]
        sink_ref = rest[0] if cf.sinks else None
        o_ref, lse_ref = rest[-2:]
        j = pl.program_id(2)
        k = jnp.concatenate([kp_ref[...], kc_ref[...]], axis=0)
        v = jnp.concatenate([vp_ref[...], vc_ref[...]], axis=0)
        col = lax.broadcasted_iota(jnp.int32, (BLOCK, 2 * BLOCK), 1)
        first = (j == 0) & (col < BLOCK)
        for hi, (qb, qo, ko) in enumerate(cf.heads):
            s = _att_logits(cf, hi, q_refs[qb][:, qo:qo + HEAD_DIM], k[:, ko:ko + HEAD_DIM], bias_ref, first)
            m = jnp.max(s, axis=-1, keepdims=True)
            if cf.sinks:
                sk = sink_ref[hi:hi + 1, 0:1]
                m = jnp.maximum(m, sk)
            p = jnp.exp(s - m)
            den = jnp.sum(p, axis=-1, keepdims=True)
            if cf.sinks:
                den = den + jnp.exp(sk - m)
            o = _nn(p.astype(BF16), v[:, ko:ko + HEAD_DIM]) / den
            c0 = 256 * qb + qo
            o_ref[:, c0:c0 + HEAD_DIM] = o
            lse_ref[:, c0:c0 + HEAD_DIM] = jnp.broadcast_to(m + jnp.log(den), (BLOCK, HEAD_DIM))

    in_specs = _att_in_specs(cf, nb) + [pl.BlockSpec((cf.nh, BLOCK, 2 * BLOCK), lambda b, r, j: (0, 0, 0))]
    args = [zf] * (cf.nq + 4) + [bias]
    if cf.sinks:
        in_specs.append(pl.BlockSpec((cf.nh, 128), lambda b, r, j: (0, 0)))
        args.append(sinks)
    out = pl.BlockSpec((None, BLOCK, cf.wq), lambda b, r, j: (b, j, r))
    shape = jax.ShapeDtypeStruct((bsz, l, cf.d * cf.wq), F32)
    return _call(
        body, name=name, grid=(bsz, cf.d, nb), in_specs=in_specs, out_specs=[out, out], out_shape=[shape, shape],
        sem=("parallel", "parallel", "arbitrary"), args=args, hosted=hosted)


def _attn_bwd(cf, zf, bias, sinks, lse, lse_tot, dy, y, name, hosted=()):
    bsz, l, _ = zf.shape
    nb = l // BLOCK
    kv_offs = sorted({ko for _, _, ko in cf.heads})

    def body(*refs):
        q_refs = refs[:cf.nq]
        kp_ref, kc_ref, vp_ref, vc_ref, bias_ref = refs[cf.nq:cf.nq + 5]
        rest = list(refs[cf.nq + 5:])
        sink_ref = rest.pop(0) if cf.sinks else None
        lse_ref, lt_ref, dy_ref, y_ref = rest[:4]
        outs = rest[4:]
        dq_ref, dk_ref, dv_ref, dbias_ref = outs[:4]
        dsink_ref = outs[4] if cf.sinks else None
        dk_acc, dv_acc = outs[-2:]
        b, r, j = pl.program_id(0), pl.program_id(1), pl.program_id(2)

        @pl.when((b == 0) & (r == 0) & (j == 0))
        def _():
            dbias_ref[...] = jnp.zeros_like(dbias_ref)
            if cf.sinks:
                dsink_ref[...] = jnp.zeros_like(dsink_ref)

        @pl.when(j == 0)
        def _():
            dk_acc[...] = jnp.zeros_like(dk_acc)
            dv_acc[...] = jnp.zeros_like(dv_acc)

        k = jnp.concatenate([kp_ref[...], kc_ref[...]], axis=0)
        v = jnp.concatenate([vp_ref[...], vc_ref[...]], axis=0)
        col = lax.broadcasted_iota(jnp.int32, (BLOCK, 2 * BLOCK), 1)
        first = (j == 0) & (col < BLOCK)
        dk_parts = {ko: None for ko in kv_offs}
        dv_parts = {ko: None for ko in kv_offs}
        for hi, (qb, qo, ko) in enumerate(cf.heads):
            c0 = 256 * qb + qo
            q = q_refs[qb][:, qo:qo + HEAD_DIM]
            kh, vh = k[:, ko:ko + HEAD_DIM], v[:, ko:ko + HEAD_DIM]
            s = _att_logits(cf, hi, q, kh, bias_ref, first)
            lse_h = lse_ref[:, c0:c0 + 1]
            alpha = jnp.exp(lse_h - lt_ref[:, c0:c0 + 1])
            pa = alpha * jnp.exp(s - lse_h)
            dyh = dy_ref[:, c0:c0 + HEAD_DIM]
            e = jnp.sum(dyh * y_ref[:, c0:c0 + HEAD_DIM], axis=-1, keepdims=True)
            dyb = dyh.astype(BF16)
            ds = pa * (_nt(dyb, vh) - e)
            dbias_ref[hi] += ds
            if cf.sinks:
                dsink_ref[:, hi:hi + 1] += -(jnp.exp(sink_ref[hi:hi + 1, 0:1] - lse_h) * e)
            dsb = ds.astype(BF16)
            dq_ref[:, c0:c0 + HEAD_DIM] = (_nn(dsb, kh) * (HEAD_DIM ** -0.5)).astype(BF16)
            dk_h = _tn(dsb, q) * (HEAD_DIM ** -0.5)
            dv_h = _tn(pa.astype(BF16), dyb)
            dk_parts[ko] = dk_h if dk_parts[ko] is None else dk_parts[ko] + dk_h
            dv_parts[ko] = dv_h if dv_parts[ko] is None else dv_parts[ko] + dv_h
        cur = pl.ds(pl.multiple_of(j * BLOCK, BLOCK), BLOCK)
        for ko in kv_offs:
            dk_acc[cur, ko:ko + HEAD_DIM] += dk_parts[ko][BLOCK:]
            dv_acc[cur, ko:ko + HEAD_DIM] += dv_parts[ko][BLOCK:]

        @pl.when(j > 0)
        def _():
            prev = pl.ds(pl.multiple_of((j - 1) * BLOCK, BLOCK), BLOCK)
            for ko in kv_offs:
                dk_acc[prev, ko:ko + HEAD_DIM] += dk_parts[ko][:BLOCK]
                dv_acc[prev, ko:ko + HEAD_DIM] += dv_parts[ko][:BLOCK]

        @pl.when(j == nb - 1)
        def _():
            dk_ref[...] = dk_acc[...].astype(BF16)
            dv_ref[...] = dv_acc[...].astype(BF16)

    in_specs = _att_in_specs(cf, nb) + [pl.BlockSpec((cf.nh, BLOCK, 2 * BLOCK), lambda b, r, j: (0, 0, 0))]
    args = [zf] * (cf.nq + 4) + [bias]
    if cf.sinks:
        in_specs.append(pl.BlockSpec((cf.nh, 128), lambda b, r, j: (0, 0)))
        args.append(sinks)
    tok = pl.BlockSpec((None, BLOCK, cf.wq), lambda b, r, j: (b, j, r))
    in_specs += [tok] * 4
    args += [lse, lse_tot, dy, y]
    seq = pl.BlockSpec((None, l, cf.wkv), lambda b, r, j: (b, 0, r))
    out_specs = [tok, seq, seq, pl.BlockSpec((cf.nh, BLOCK, 2 * BLOCK), lambda b, r, j: (0, 0, 0))]
    out_shape = [jax.ShapeDtypeStruct((bsz, l, cf.d * cf.wq), BF16),
                 jax.ShapeDtypeStruct((bsz, l, cf.d * cf.wkv), BF16),
                 jax.ShapeDtypeStruct((bsz, l, cf.d * cf.wkv), BF16),
                 jax.ShapeDtypeStruct((cf.nh, BLOCK, 2 * BLOCK), F32)]
    if cf.sinks:
        out_specs.append(pl.BlockSpec((BLOCK, 128), lambda b, r, j: (0, 0)))
        out_shape.append(jax.ShapeDtypeStruct((BLOCK, 128), F32))
    return _call(
        body, name=name, grid=(bsz, cf.d, nb), in_specs=in_specs, out_specs=out_specs, out_shape=out_shape,
        scratch=[pltpu.VMEM((l, cf.wkv), F32), pltpu.VMEM((l, cf.wkv), F32)],
        sem=("arbitrary", "arbitrary", "arbitrary"), args=args, hosted=hosted)


def _merge_fwd(o_a, lse_a, o_b, zg, h, wa_t, wb_t, wout, name):
    t = h.shape[0]
    tm = 512

    def body(o1, o2, o3, l1, l2, l3, ob_ref, zg_ref, h_ref, wa_ref, wb_ref, wo_ref, h2_ref, ya_ref, lt_ref):
        m = jnp.maximum(jnp.maximum(l1[...], l2[...]), l3[...])
        e1, e2, e3 = jnp.exp(l1[...] - m), jnp.exp(l2[...] - m), jnp.exp(l3[...] - m)
        se = e1 + e2 + e3
        ya = (e1 / se) * o1[...] + (e2 / se) * o2[...] + (e3 / se) * o3[...]
        ya_ref[...] = ya
        lt_ref[...] = m + jnp.log(se)
        pa = _nt(ya.astype(BF16), wa_ref[...])
        pb = _nt(ob_ref[...].astype(BF16), wb_ref[...])
        merged = jax.nn.sigmoid(zg_ref[:, :D_MODEL]) * pa + jax.nn.sigmoid(zg_ref[:, D_MODEL:]) * pb
        h2_ref[...] = h_ref[...] + _nn(merged.astype(BF16), wo_ref[...])

    row = lambda w: pl.BlockSpec((tm, w), lambda i: (i, 0))
    full = _resident
    return pl.pallas_call(
        body, name=name, grid=(t // tm,),
        in_specs=[row(256)] * 6 + [row(512), row(GATE_W), row(D_MODEL), full(wa_t), full(wb_t), full(wout)],
        out_specs=[row(D_MODEL), row(256), row(256)],
        out_shape=[jax.ShapeDtypeStruct((t, D_MODEL), F32), jax.ShapeDtypeStruct((t, 256), F32),
                   jax.ShapeDtypeStruct((t, 256), F32)],
        compiler_params=_params("parallel"),
    )(*o_a, *lse_a, o_b, zg, h, wa_t, wb_t, wout)


def _merge_bwd(dh, ya, o_b, zg, wa_t, wb_t, wout, name, hosted=()):
    t = dh.shape[0]
    tm = 256

    def body(dh_ref, ya_ref, ob_ref, zg_ref, wa_ref, wb_ref, wo_ref,
             mg_ref, dpa_ref, dpb_ref, dzg_ref, dya_ref, dyb_ref):
        dm = _nt(dh_ref[...].astype(BF16), wo_ref[...])
        pa = _nt(ya_ref[...].astype(BF16), wa_ref[...])
        pb = _nt(ob_ref[...].astype(BF16), wb_ref[...])
        sa = jax.nn.sigmoid(zg_ref[:, :D_MODEL])
        sb = jax.nn.sigmoid(zg_ref[:, D_MODEL:])
        mg_ref[...] = (sa * pa + sb * pb).astype(BF16)
        dpa = (dm * sa).astype(BF16)
        dpb = (dm * sb).astype(BF16)
        dpa_ref[...] = dpa
        dpb_ref[...] = dpb
        dzg_ref[:, :D_MODEL] = (dm * pa * (sa * (1.0 - sa))).astype(BF16)
        dzg_ref[:, D_MODEL:] = (dm * pb * (sb * (1.0 - sb))).astype(BF16)
        dya_ref[...] = _nn(dpa, wa_ref[...])
        dyb_ref[...] = _nn(dpb, wb_ref[...])

    row = lambda w: pl.BlockSpec((tm, w), lambda i: (i, 0))
    full = _resident
    sds = jax.ShapeDtypeStruct
    return _call(
        body, name=name, grid=(t // tm,),
        in_specs=[row(D_MODEL), row(256), row(512), row(GATE_W), full(wa_t), full(wb_t), full(wout)],
        out_specs=[row(D_MODEL), row(D_MODEL), row(D_MODEL), row(GATE_W), row(256), row(512)],
        out_shape=[sds((t, D_MODEL), BF16), sds((t, D_MODEL), BF16), sds((t, D_MODEL), BF16),
                   sds((t, GATE_W), BF16), sds((t, 256), F32), sds((t, 512), F32)],
        sem=("parallel",), args=(dh, ya, o_b, zg, wa_t, wb_t, wout), hosted=hosted)


def _loss_head(h, g, target, name):
    t = h.shape[0]
    tm = 1024

    def body(h_ref, g_ref, t_ref, dh_ref, loss_ref, dg_ref):
        @pl.when(pl.program_id(0) == 0)
        def _():
            loss_ref[...] = jnp.zeros_like(loss_ref)
            dg_ref[...] = jnp.zeros_like(dg_ref)

        xhat, r = _rms(h_ref[...], g_ref[...])
        err = xhat * g_ref[...] - t_ref[...]
        loss_ref[...] += 0.5 * jnp.sum(jnp.mean(err * err, axis=-1, keepdims=True), axis=0, keepdims=True)
        dx, dg = _rms_bwd(err * (1.0 / D_MODEL), xhat, r, g_ref[...])
        dh_ref[...] = dx
        dg_ref[...] += dg

    row = pl.BlockSpec((tm, D_MODEL), lambda i: (i, 0))
    vec = pl.BlockSpec((1, D_MODEL), lambda i: (0, 0))
    return pl.pallas_call(
        body, name=name, grid=(t // tm,),
        in_specs=[row, vec, row], out_specs=[row, pl.BlockSpec((1, 128), lambda i: (0, 0)), vec],
        out_shape=[jax.ShapeDtypeStruct((t, D_MODEL), F32), jax.ShapeDtypeStruct((1, 128), F32),
                   jax.ShapeDtypeStruct((1, D_MODEL), F32)],
        compiler_params=_params("arbitrary"),
    )(h, g, target)


def _pair_sum(grad, got, name):
    _, _, r, cdim = grad.shape
    core = lax.axis_index("c").astype(jnp.int32).reshape(1)

    def body(core_ref, g_ref, s_ref, o_ref):
        o_ref[...] = (g_ref[...] + s_ref[...]).astype(BF16)

    return pl.pallas_call(
        body, name=name,
        grid_spec=pltpu.PrefetchScalarGridSpec(
            num_scalar_prefetch=1, grid=(N_CHIP,),
            in_specs=[pl.BlockSpec((None, None, r, cdim), lambda q, core_ref: (q, core_ref[0], 0, 0)),
                      pl.BlockSpec((None, None, r, cdim), lambda q, core_ref: (q, 0, 0, 0))],
            out_specs=pl.BlockSpec((None, r, cdim), lambda q, core_ref: (q, 0, 0))),
        out_shape=jax.ShapeDtypeStruct((N_CHIP, r, cdim), BF16),
        compiler_params=_params("parallel"),
    )(core, grad, got)


def _chip_sum(parts, name):
    def body(p_ref, o_ref):
        o_ref[...] = ((p_ref[0].astype(F32) + p_ref[1].astype(F32)) + p_ref[2].astype(F32)) + p_ref[3].astype(F32)

    return pl.pallas_call(
        body, name=name, out_shape=jax.ShapeDtypeStruct(parts.shape[1:], F32),
        compiler_params=pltpu.CompilerParams(vmem_limit_bytes=VMEM_LIMIT),
    )(parts)


def _adamw_math(w, g, m, v):
    m = ADAM_B1 * m + (1.0 - ADAM_B1) * g
    v = ADAM_B2 * v + (1.0 - ADAM_B2) * jnp.square(g)
    m_hat = m / (1.0 - ADAM_B1 ** ADAM_STEP)
    v_hat = v / (1.0 - ADAM_B2 ** ADAM_STEP)
    delta = -ADAM_LR * (m_hat / (jnp.sqrt(v_hat) + ADAM_EPS) + ADAM_WD * w)
    return delta, m, v


def _adamw(w, g, m, v, name):
    def body(w_ref, g_ref, m_ref, v_ref, d_ref, m2_ref, v2_ref):
        d_ref[...], m2_ref[...], v2_ref[...] = _adamw_math(w_ref[...], g_ref[...], m_ref[...], v_ref[...])

    sds = jax.ShapeDtypeStruct(w.shape, F32)
    return pl.pallas_call(body, name=name, out_shape=[sds, sds, sds],
                          compiler_params=pltpu.CompilerParams(vmem_limit_bytes=VMEM_LIMIT))(w, g, m, v)


SMALL_ROWS = 80


def _small_update(g, w, m, v, name):
    def body(g_ref, w_ref, m_ref, v_ref, gs_ref, d_ref, m2_ref, v2_ref, got_ref, send_sems, recv_sems):
        x, y, c = _place()
        me = 4 * x + 2 * y + c
        got_ref[me] = g_ref[...]
        copies = []
        for k in range(1, N_DEV):
            peer = (x ^ (k >> 2), y ^ ((k >> 1) & 1), c ^ (k & 1))
            cp = pltpu.make_async_remote_copy(
                src_ref=g_ref, dst_ref=got_ref.at[me], send_sem=send_sems.at[k - 1], recv_sem=recv_sems.at[k - 1],
                device_id=peer, device_id_type=MESH)
            cp.start()
            copies.append(cp)
        for cp in copies:
            cp.wait()
        total = got_ref[0]
        for k in range(1, N_DEV):
            total = total + got_ref[k]
        gs_ref[...] = total
        d_ref[...], m2_ref[...], v2_ref[...] = _adamw_math(w_ref[...], total, m_ref[...], v_ref[...])

    sds = jax.ShapeDtypeStruct((SMALL_ROWS, 128), F32)
    vm = pl.BlockSpec(memory_space=pltpu.VMEM)
    return pl.pallas_call(
        body, name=name, in_specs=[vm] * 4, out_specs=[vm] * 4, out_shape=[sds] * 4,
        scratch_shapes=[pltpu.VMEM((N_DEV, SMALL_ROWS, 128), F32), pltpu.SemaphoreType.DMA((N_DEV - 1,)),
                        pltpu.SemaphoreType.DMA((N_DEV - 1,))],
    )(g, w, m, v)


def _pack_small(gains, b_in, rel_bias, sinks, last):
    rows = [a.reshape(8, 128) for a in gains] + [b_in.reshape(40, 128), rel_bias.reshape(5, 128),
                                                 jnp.pad(sinks.reshape(1, 8), ((0, 0), (0, 120))), last]
    rows.append(jnp.zeros((SMALL_ROWS - 79, 128), F32))
    return jnp.concatenate(rows, axis=0)


def _unpack_small(p, like):
    out = [p[8 * i:8 * i + 8].reshape(like[i].shape) for i in range(4)]
    out.append(p[32:72].reshape(like[4].shape))
    out.append(p[72:77].reshape(like[5].shape))
    out.append(p[77, :8].reshape(like[6].shape))
    return out


def kernel(x, ffn1_norm, ffn1_w_gate, ffn1_w_up, ffn1_w_down, mix_norm, w_in, b_in, w_branch_a, w_branch_b, w_out, sinks, rel_bias, ffn2_norm, ffn2_w_gate, ffn2_w_up, ffn2_w_down, final_norm, loss_target, m_ffn1_norm, m_ffn1_w_gate, m_ffn1_w_up, m_ffn1_w_down, m_mix_norm, m_w_in, m_b_in, m_w_branch_a, m_w_branch_b, m_w_out, m_sinks, m_rel_bias, m_ffn2_norm, m_ffn2_w_gate, m_ffn2_w_up, m_ffn2_w_down, m_final_norm, v_ffn1_norm, v_ffn1_w_gate, v_ffn1_w_up, v_ffn1_w_down, v_mix_norm, v_w_in, v_b_in, v_w_branch_a, v_w_branch_b, v_w_out, v_sinks, v_rel_bias, v_ffn2_norm, v_ffn2_w_gate, v_ffn2_w_up, v_ffn2_w_down, v_final_norm):
    bsz, seq, _ = x.shape
    t = bsz * seq
    xt = x.reshape(t, D_MODEL)
    target = loss_target.reshape(t, D_MODEL)

    big = [("ffn1_w_gate", ffn1_w_gate, m_ffn1_w_gate, v_ffn1_w_gate, True),
           ("ffn1_w_up", ffn1_w_up, m_ffn1_w_up, v_ffn1_w_up, True),
           ("ffn1_w_down", ffn1_w_down, m_ffn1_w_down, v_ffn1_w_down, False),
           ("w_in", w_in, m_w_in, v_w_in, True),
           ("w_branch_a", w_branch_a, m_w_branch_a, v_w_branch_a, True),
           ("w_branch_b", w_branch_b, m_w_branch_b, v_w_branch_b, True),
           ("w_out", w_out, m_w_out, v_w_out, False),
           ("ffn2_w_gate", ffn2_w_gate, m_ffn2_w_gate, v_ffn2_w_gate, True),
           ("ffn2_w_up", ffn2_w_up, m_ffn2_w_up, v_ffn2_w_up, True),
           ("ffn2_w_down", ffn2_w_down, m_ffn2_w_down, v_ffn2_w_down, False)]
    shard = {nm: (w[0].T if tr else w[0]).astype(BF16) for nm, w, _, _, tr in big}
    full = {}

    def gather(names):
        return _Gather([shard[nm] for nm in names])

    def keep(names, got):
        for nm, gw in zip(names, got):
            full[nm] = gw.reshape(-1, gw.shape[-1])

    ffn1_names = ["ffn1_w_gate", "ffn1_w_up", "ffn1_w_down"]
    mix_names = ["w_in", "w_branch_a", "w_branch_b", "w_out"]
    ffn2_names = ["ffn2_w_gate", "ffn2_w_up", "ffn2_w_down"]
    g1, gm, g2, gf = ffn1_norm, mix_norm, ffn2_norm, final_norm.reshape(1, D_MODEL)

    keep(ffn1_names, _exchange([gather(ffn1_names)], "gather_ffn1")[0])
    (h1, n1), (got,) = _ffn_fwd(xt, g1, full["ffn1_w_gate"], full["ffn1_w_up"], full["ffn1_w_down"], "ffn1_fwd",
                                hosted=[gather(mix_names)])
    keep(mix_names, got)
    u, zq, zg = _inproj_fwd(h1, gm, full["w_in"], b_in, "inproj_fwd")
    buckets = _bucket_tiles()
    bias = _bias_build(rel_bias, buckets)
    sink_rows = jnp.broadcast_to(sinks.reshape(8, 1), (8, 128))
    cfs = [_Att(i) for i in range(4)]
    zfold = [zq.reshape(bsz, seq // cf.d, cf.d * QKV_W) for cf in cfs]
    att = [None] * 4
    for i in (3, 0, 1, 2):
        cf = cfs[i]
        (o, lse), got = _attn_fwd(cf, zfold[i], bias[cf.h0:cf.h0 + cf.nh], sink_rows, f"attn{i}_fwd",
                                  hosted=[gather(ffn2_names)] if i == 3 else ())
        if i == 3:
            keep(ffn2_names, got[0])
        att[i] = (o.reshape(t, cf.wq), lse.reshape(t, cf.wq))
    o_b, lse_b = att[3]
    h2, ya, lse_tot = _merge_fwd([a[0] for a in att[:3]], [a[1] for a in att[:3]], o_b, zg, h1,
                                 full["w_branch_a"], full["w_branch_b"], full["w_out"], "merge_fwd")
    (h3, n2), _ = _ffn_fwd(h2, g2, full["ffn2_w_gate"], full["ffn2_w_up"], full["ffn2_w_down"], "ffn2_fwd")
    dh3, loss_part, dgf = _loss_head(h3, gf, target, "loss_head")

    grads, pair, from_chips = {}, {}, {}

    def by_owner(nm):
        return grads[nm].reshape(N_CHIP, 2, -1, grads[nm].shape[-1])

    def to_core(names):
        return _CoreExchange([by_owner(nm) for nm in names])

    def pair_up(names, got):
        for nm, sib in zip(names, got):
            pair[nm] = _pair_sum(by_owner(nm), sib, f"pair_sum_{nm}")

    def to_chips(names):
        return _ChipExchange([pair[nm] for nm in names])

    def landed(names, got):
        for nm, parts in zip(names, got):
            from_chips[nm] = parts

    dh2, dg2, hff, da, db, dhh = _ffn_bwd(h2, n2, g2, dh3, full["ffn2_w_gate"], full["ffn2_w_up"],
                                          full["ffn2_w_down"], "ffn2_bwd")
    grads["ffn2_w_gate"], _ = _tn_matmul(da, n2, 1408, "ffn2_dgate")
    grads["ffn2_w_up"], _ = _tn_matmul(db, n2, 1408, "ffn2_dup")
    grads["ffn2_w_down"], _ = _tn_matmul(hff, dhh, 1408, "ffn2_ddown")
    (merged, dpa, dpb, dzg, dya, dyb), (got,) = _merge_bwd(
        dh2, ya, o_b, zg, full["w_branch_a"], full["w_branch_b"], full["w_out"], "merge_bwd",
        hosted=[to_core(ffn2_names)])
    pair_up(ffn2_names, got)
    grads["w_out"], _ = _tn_matmul(merged, dh2, 1024, "dw_out")
    grads["w_branch_a"], _ = _tn_matmul(dpa, ya, 1024, "dw_branch_a")
    grads["w_branch_b"], _ = _tn_matmul(dpb, o_b, 1024, "dw_branch_b")
    out_names = ["w_out", "w_branch_a", "w_branch_b"]
    dq, dk, dv, dbias, dsink = [None] * 4, [None] * 4, [None] * 4, [None] * 4, None
    for i in (3, 0, 1, 2):
        cf = cfs[i]
        if i < 3:
            lse_i, lt_i, dy_i, y_i = att[i][1], lse_tot, dya, ya
        else:
            lse_i, lt_i, dy_i, y_i = lse_b, lse_b, dyb, o_b
        shp = (bsz, seq // cf.d, cf.d * cf.wq)
        hosted = {3: lambda: [to_chips(ffn2_names)], 0: lambda: [to_core(out_names)],
                  1: lambda: [to_chips(out_names)], 2: lambda: []}[i]()
        res, got = _attn_bwd(cf, zfold[i], bias[cf.h0:cf.h0 + cf.nh], sink_rows, lse_i.reshape(shp),
                             lt_i.reshape(shp), dy_i.reshape(shp), y_i.reshape(shp), f"attn{i}_bwd", hosted=hosted)
        if i == 3:
            landed(ffn2_names, got[0])
        elif i == 0:
            pair_up(out_names, got[0])
        elif i == 1:
            landed(out_names, got[0])
        dq[i] = res[0].reshape(t, cf.wq)
        dk[i] = res[1].reshape(t, cf.wkv)
        dv[i] = res[2].reshape(t, cf.wkv)
        dbias[i] = res[3]
        if cf.sinks:
            dsink = res[4]
    pieces = dq[:3] + dk[:3] + dv[:3] + [dq[3], dk[3], dv[3], dzg]
    dh1, dz, db_in, dgm = _inproj_bwd(pieces, h1, gm, dh2, full["w_in"], "inproj_bwd")
    dx, dg1, hff, da, db, dhh = _ffn_bwd(xt, n1, g1, dh1, full["ffn1_w_gate"], full["ffn1_w_up"],
                                         full["ffn1_w_down"], "ffn1_bwd")
    grads["w_in"], _ = _tn_matmul(dz, u, 1280, "dw_in")
    grads["ffn1_w_down"], (got,) = _tn_matmul(hff, dhh, 1408, "ffn1_ddown", hosted=[to_core(["w_in"])])
    pair_up(["w_in"], got)
    grads["ffn1_w_gate"], got = _tn_matmul(da, n1, 1408, "ffn1_dgate",
                                           hosted=[to_chips(["w_in"]), to_core(["ffn1_w_down"])])
    landed(["w_in"], got[0])
    pair_up(["ffn1_w_down"], got[1])
    grads["ffn1_w_up"], got = _tn_matmul(db, n1, 1408, "ffn1_dup",
                                         hosted=[to_chips(["ffn1_w_down"]), to_core(["ffn1_w_gate"])])
    landed(["ffn1_w_down"], got[0])
    pair_up(["ffn1_w_gate"], got[1])
    got = _exchange([to_chips(["ffn1_w_gate"]), to_core(["ffn1_w_up"])], "reduce_scatter_tail1")
    landed(["ffn1_w_gate"], got[0])
    pair_up(["ffn1_w_up"], got[1])
    landed(["ffn1_w_up"], _exchange([to_chips(["ffn1_w_up"])], "reduce_scatter_tail2")[0])
    dtable, dsinks = _bias_reduce(jnp.concatenate(dbias, axis=0), buckets, dsink)

    out_g, out_d, out_m, out_v = {}, {}, {}, {}
    for nm, w, m, v, tr in big:
        g = _chip_sum(from_chips[nm], f"chip_sum_{nm}")
        g = (g.T if tr else g).reshape(w.shape)
        out_g[nm] = g
        d2, m2, v2 = _adamw(w[0], g[0], m[0], v[0], f"adamw_{nm}")
        out_d[nm], out_m[nm], out_v[nm] = d2[None], m2[None], v2[None]

    small = [("ffn1_norm", ffn1_norm, m_ffn1_norm, v_ffn1_norm), ("mix_norm", mix_norm, m_mix_norm, v_mix_norm),
             ("ffn2_norm", ffn2_norm, m_ffn2_norm, v_ffn2_norm), ("final_norm", final_norm, m_final_norm, v_final_norm),
             ("b_in", b_in, m_b_in, v_b_in), ("rel_bias", rel_bias, m_rel_bias, v_rel_bias),
             ("sinks", sinks, m_sinks, v_sinks)]
    zero_row = jnp.zeros((1, 128), F32)
    pack = lambda arrs, last: _pack_small(arrs[:4], arrs[4], arrs[5], arrs[6], last)
    g_small = pack([dg1, dgm, dg2, dgf, db_in, dtable[:, :TOTAL_HEADS], dsinks[0, :8]], loss_part)
    packed = [pack([s[k] for s in small], zero_row) for k in (1, 2, 3)]
    gs, ds, ms, vs = _small_update(g_small, *packed, "small_update")
    like = [s[1] for s in small]
    for nm_s, g_, d_, m_, v_ in zip([s[0] for s in small], _unpack_small(gs, like), _unpack_small(ds, like),
                                    _unpack_small(ms, like), _unpack_small(vs, like)):
        out_g[nm_s], out_d[nm_s], out_m[nm_s], out_v[nm_s] = g_, d_, m_, v_
    loss = gs[78, 0]

    order = ["ffn1_norm", "ffn1_w_gate", "ffn1_w_up", "ffn1_w_down", "mix_norm", "w_in", "b_in", "w_branch_a",
             "w_branch_b", "w_out", "sinks", "rel_bias", "ffn2_norm", "ffn2_w_gate", "ffn2_w_up", "ffn2_w_down",
             "final_norm"]
    return (loss, dx.reshape(x.shape), *[out_g[k] for k in order], *[out_d[k] for k in order],
            *[out_m[k] for k in order], *[out_v[k] for k in order])
```

```python
import functools
import math

import numpy as np
import jax
import jax.numpy as jnp
from jax import lax
from jax.experimental import pallas as pl
from jax.experimental.pallas import tpu as pltpu

D_MODEL = 1024
D_FF = 2816
FF_CHUNK = 256
HEAD_DIM = 64
BLOCK = 128
N_BUCKETS = 32
MAX_DISTANCE = 2048
A_HEADS = 12
TOTAL_HEADS = 20
DIL_GROUPS = ((128, 1), (512, 4), (2048, 16))
B_WINDOW = 128
QKV_W = 3072
GATE_W = 2048
D_IN = QKV_W + GATE_W
EPS = 1e-6
NEG = -1e30
N_DEV = 8
N_CHIP = 4
ADAM_LR, ADAM_B1, ADAM_B2, ADAM_EPS, ADAM_WD, ADAM_STEP = 0.001, 0.9, 0.999, 1e-08, 0.01, 10
VMEM_LIMIT = 56 * 1024 * 1024
MESH = pl.DeviceIdType.MESH
BF16 = jnp.bfloat16
F32 = jnp.float32
ANY = pl.BlockSpec(memory_space=pl.ANY)


def _params(*sem):
    return pltpu.CompilerParams(dimension_semantics=sem, vmem_limit_bytes=VMEM_LIMIT)


def _resident(a):
    return pl.BlockSpec(a.shape, lambda i: (0, 0), pipeline_mode=pl.Buffered(1))


def _place():
    return lax.axis_index("x"), lax.axis_index("y"), lax.axis_index("c")


class _Gather:
    def __init__(self, shards):
        self.ins = list(shards)
        n = self.n = len(shards)
        self.out_shape = [jax.ShapeDtypeStruct((N_DEV,) + s.shape, s.dtype) for s in shards]
        self.scratch = [pltpu.SemaphoreType.DMA((7 * n,)), pltpu.SemaphoreType.DMA((7 * n,)),
                        pltpu.SemaphoreType.DMA((n,))]

    def _copies(self, ins, outs, sems):
        send_sems, recv_sems, local_sems = sems
        x, y, c = _place()
        me, sibling = (x, y, c), (x, y, 1 - c)
        chips = [(1 - x, y), (x, 1 - y), (1 - x, 1 - y)]

        def copy(i, k, block, to, src=None):
            dst = outs[i].at[4 * block[0] + 2 * block[1] + block[2]]
            return pltpu.make_async_remote_copy(
                src_ref=dst if src is None else src, dst_ref=dst, send_sem=send_sems.at[7 * i + k],
                recv_sem=recv_sems.at[7 * i + k], device_id=to, device_id_type=MESH)

        n = self.n
        mine = [pltpu.make_async_copy(ins[i], outs[i].at[4 * x + 2 * y + c], local_sems.at[i]) for i in range(n)]
        first = [copy(i, 0, me, sibling, src=ins[i]) for i in range(n)]
        first += [copy(i, 1 + j, me, (*chip, c), src=ins[i]) for i in range(n) for j, chip in enumerate(chips)]
        landed = [copy(i, 1 + j, (*chip, c), me) for j, chip in enumerate(chips) for i in range(n)]
        passed = [copy(i, 4 + j, (*chip, c), sibling) for j, chip in enumerate(chips) for i in range(n)]
        from_sibling = [copy(i, 0, sibling, me) for i in range(n)]
        from_sibling += [copy(i, 4 + j, (*chip, 1 - c), me) for i in range(n) for j, chip in enumerate(chips)]
        return mine, first, landed, passed, from_sibling

    def start(self, ins, outs, sems):
        mine, first, _, _, _ = self._copies(ins, outs, sems)
        for cp in mine + first:
            cp.start()

    def mid(self, ins, outs, sems):
        _, _, landed, passed, _ = self._copies(ins, outs, sems)
        for got, fwd in zip(landed, passed):
            got.wait_recv()
            fwd.start()

    def end(self, ins, outs, sems):
        mine, first, _, passed, from_sibling = self._copies(ins, outs, sems)
        for cp in from_sibling:
            cp.wait_recv()
        for cp in first + passed:
            cp.wait_send()
        for cp in mine:
            cp.wait()


class _CoreExchange:
    def __init__(self, grads):
        self.ins = list(grads)
        n = self.n = len(grads)
        self.out_shape = [jax.ShapeDtypeStruct((N_CHIP, 1) + g.shape[2:], g.dtype) for g in grads]
        self.scratch = [pltpu.SemaphoreType.DMA((n,)), pltpu.SemaphoreType.DMA((n,))]

    def _copies(self, ins, outs, sems):
        x, y, c = _place()
        return [pltpu.make_async_remote_copy(
            src_ref=ins[i].at[:, pl.ds(1 - c, 1)], dst_ref=outs[i], send_sem=sems[0].at[i],
            recv_sem=sems[1].at[i], device_id=(x, y, 1 - c), device_id_type=MESH) for i in range(self.n)]

    def start(self, ins, outs, sems):
        for cp in self._copies(ins, outs, sems):
            cp.start()

    mid = None

    def end(self, ins, outs, sems):
        for cp in self._copies(ins, outs, sems):
            cp.wait()


class _ChipExchange:
    def __init__(self, parts):
        self.ins = list(parts)
        n = self.n = len(parts)
        self.out_shape = [jax.ShapeDtypeStruct(p.shape, p.dtype) for p in parts]
        self.scratch = [pltpu.SemaphoreType.DMA((3 * n,)), pltpu.SemaphoreType.DMA((3 * n,)),
                        pltpu.SemaphoreType.DMA((n,))]

    def _copies(self, ins, outs, sems):
        send_sems, recv_sems, local_sems = sems
        x, y, c = _place()
        my_chip = 2 * x + y
        copies = []
        for i in range(self.n):
            copies.append(pltpu.make_async_copy(ins[i].at[my_chip], outs[i].at[my_chip], local_sems.at[i]))
            for k, (qx, qy) in enumerate([(1 - x, y), (x, 1 - y), (1 - x, 1 - y)]):
                copies.append(pltpu.make_async_remote_copy(
                    src_ref=ins[i].at[2 * qx + qy], dst_ref=outs[i].at[my_chip], send_sem=send_sems.at[3 * i + k],
                    recv_sem=recv_sems.at[3 * i + k], device_id=(qx, qy, c), device_id_type=MESH))
        return copies

    def start(self, ins, outs, sems):
        for cp in self._copies(ins, outs, sems):
            cp.start()

    mid = None

    def end(self, ins, outs, sems):
        for cp in self._copies(ins, outs, sems):
            cp.wait()


def _call(body, *, name, grid, in_specs, out_specs, out_shape, args, scratch=(), sem=None, hosted=()):
    n_in, n_out, n_scr = len(in_specs), len(out_specs), len(scratch)
    x_in = [len(p.ins) for p in hosted]
    x_scr = [len(p.scratch) for p in hosted]
    steps = int(np.prod(grid))
    assert not hosted or steps >= 3

    def wrapped(*refs):
        refs = list(refs)
        ins, refs = refs[:n_in], refs[n_in:]
        x_ins = [[refs.pop(0) for _ in range(k)] for k in x_in]
        outs, refs = refs[:n_out], refs[n_out:]
        x_outs = [[refs.pop(0) for _ in range(k)] for k in x_in]
        scr, refs = refs[:n_scr], refs[n_scr:]
        x_sems = [[refs.pop(0) for _ in range(k)] for k in x_scr]
        step = 0
        for d in range(len(grid)):
            step = step * grid[d] + pl.program_id(d)

        def phase(which, at):
            fns = [(getattr(p, which), a) for p, a in zip(hosted, zip(x_ins, x_outs, x_sems)) if getattr(p, which)]
            if fns:
                @pl.when(step == at)
                def _():
                    for fn, a in fns:
                        fn(*a)

        phase("start", 0)
        phase("mid", steps // 2)
        body(*ins, *outs, *scr)
        phase("end", steps - 1)

    results = pl.pallas_call(
        wrapped, name=name, grid=grid,
        in_specs=list(in_specs) + [ANY] * sum(x_in), out_specs=list(out_specs) + [ANY] * sum(x_in),
        out_shape=list(out_shape) + [s for p in hosted for s in p.out_shape],
        scratch_shapes=list(scratch) + [s for p in hosted for s in p.scratch],
        compiler_params=_params(*(("arbitrary",) * len(grid) if hosted else sem)),
    )(*args, *[a for p in hosted for a in p.ins])
    own, rest = list(results[:n_out]), list(results[n_out:])
    return own, [[rest.pop(0) for _ in range(k)] for k in x_in]


def _exchange(programs, name):
    x_in = [len(p.ins) for p in programs]
    x_scr = [len(p.scratch) for p in programs]

    def body(*refs):
        refs = list(refs)
        x_ins = [[refs.pop(0) for _ in range(k)] for k in x_in]
        x_outs = [[refs.pop(0) for _ in range(k)] for k in x_in]
        x_sems = [[refs.pop(0) for _ in range(k)] for k in x_scr]
        for which in ("start", "mid", "end"):
            for p, a in zip(programs, zip(x_ins, x_outs, x_sems)):
                if getattr(p, which):
                    getattr(p, which)(*a)

    results = list(pl.pallas_call(
        body, name=name, in_specs=[ANY] * sum(x_in), out_specs=[ANY] * sum(x_in),
        out_shape=[s for p in programs for s in p.out_shape],
        scratch_shapes=[s for p in programs for s in p.scratch],
    )(*[a for p in programs for a in p.ins]))
    return [[results.pop(0) for _ in range(k)] for k in x_in]


def _nt(a, b):
    return lax.dot_general(a, b, (((1,), (1,)), ((), ())), preferred_element_type=F32)


def _nn(a, b):
    return lax.dot_general(a, b, (((1,), (0,)), ((), ())), preferred_element_type=F32)


def _tn(a, b):
    return lax.dot_general(a, b, (((0,), (0,)), ((), ())), preferred_element_type=F32)


def _rms(x, g):
    r = lax.rsqrt(jnp.mean(x * x, axis=-1, keepdims=True) + EPS)
    return x * r, r


def _rms_bwd(dn, xhat, r, g):
    dg = jnp.sum(dn * xhat, axis=0, keepdims=True)
    dxh = dn * g
    dx = r * (dxh - xhat * jnp.mean(dxh * xhat, axis=-1, keepdims=True))
    return dx, dg


def _ffn_fwd(x, g, wg_t, wu_t, wd, name, hosted=()):
    t = x.shape[0]
    tm = 512

    def body(x_ref, g_ref, wg_ref, wu_ref, wd_ref, h_ref, n_ref, hff_ref):
        xhat, _ = _rms(x_ref[...], g_ref[...])
        n = (xhat * g_ref[...]).astype(BF16)
        n_ref[...] = n
        for c in range(0, D_FF, FF_CHUNK):
            a = _nt(n, wg_ref[c:c + FF_CHUNK, :])
            b = _nt(n, wu_ref[c:c + FF_CHUNK, :])
            hff_ref[:, c:c + FF_CHUNK] = (a * jax.nn.sigmoid(a) * b).astype(BF16)
        h_ref[...] = x_ref[...] + 0.5 * _nn(hff_ref[...], wd_ref[...])

    row = pl.BlockSpec((tm, D_MODEL), lambda i: (i, 0))
    return _call(
        body, name=name, grid=(t // tm,),
        in_specs=[row, _resident(g), _resident(wg_t), _resident(wu_t), _resident(wd)],
        out_specs=[row, row],
        out_shape=[jax.ShapeDtypeStruct((t, D_MODEL), F32), jax.ShapeDtypeStruct((t, D_MODEL), BF16)],
        scratch=[pltpu.VMEM((tm, D_FF), BF16)], sem=("parallel",), args=(x, g, wg_t, wu_t, wd), hosted=hosted)


def _ffn_bwd(x, n, g, dh, wg_t, wu_t, wd, name):
    t = x.shape[0]
    tm = 256

    def body(x_ref, n_ref, g_ref, dh_ref, wg_ref, wu_ref, wd_ref,
             dx_ref, dg_ref, hff_ref, da_ref, db_ref, dhh_ref):
        @pl.when(pl.program_id(0) == 0)
        def _():
            dg_ref[...] = jnp.zeros_like(dg_ref)

        dhh = (0.5 * dh_ref[...]).astype(BF16)
        dhh_ref[...] = dhh
        nb = n_ref[...]
        for c in range(0, D_FF, FF_CHUNK):
            cols = slice(c, c + FF_CHUNK)
            a = _nt(nb, wg_ref[cols, :])
            b = _nt(nb, wu_ref[cols, :])
            s = jax.nn.sigmoid(a)
            silu = a * s
            dhff = _nt(dhh, wd_ref[cols, :])
            hff_ref[:, cols] = (silu * b).astype(BF16)
            da_ref[:, cols] = (dhff * b * (s * (1.0 + a * (1.0 - s)))).astype(BF16)
            db_ref[:, cols] = (dhff * silu).astype(BF16)
        dn = _nn(da_ref[...], wg_ref[...]) + _nn(db_ref[...], wu_ref[...])
        xhat, r = _rms(x_ref[...], g_ref[...])
        dx, dg = _rms_bwd(dn, xhat, r, g_ref[...])
        dx_ref[...] = dh_ref[...] + dx
        dg_ref[...] += dg

    row = pl.BlockSpec((tm, D_MODEL), lambda i: (i, 0))
    hid = pl.BlockSpec((tm, D_FF), lambda i: (i, 0))
    return pl.pallas_call(
        body, name=name, grid=(t // tm,),
        in_specs=[row, row, _resident(g), row, _resident(wg_t), _resident(wu_t), _resident(wd)],
        out_specs=[row, pl.BlockSpec((1, D_MODEL), lambda i: (0, 0)), hid, hid, hid, row],
        out_shape=[jax.ShapeDtypeStruct((t, D_MODEL), F32), jax.ShapeDtypeStruct((1, D_MODEL), F32),
                   jax.ShapeDtypeStruct((t, D_FF), BF16), jax.ShapeDtypeStruct((t, D_FF), BF16),
                   jax.ShapeDtypeStruct((t, D_FF), BF16), jax.ShapeDtypeStruct((t, D_MODEL), BF16)],
        compiler_params=_params("arbitrary"),
    )(x, n, g, dh, wg_t, wu_t, wd)


def _tn_matmul(a, b, rc, name, hosted=()):
    t, r = a.shape
    c = b.shape[1]
    tk = 1024

    def body(a_ref, b_ref, o_ref):
        @pl.when(pl.program_id(1) == 0)
        def _():
            o_ref[...] = jnp.zeros_like(o_ref)

        o_ref[...] += _tn(a_ref[...].astype(BF16), b_ref[...].astype(BF16))

    (out,), got = _call(
        body, name=name, grid=(r // rc, t // tk),
        in_specs=[pl.BlockSpec((tk, rc), lambda i, k: (k, i)), pl.BlockSpec((tk, c), lambda i, k: (k, 0))],
        out_specs=[pl.BlockSpec((rc, c), lambda i, k: (i, 0))],
        out_shape=[jax.ShapeDtypeStruct((r, c), F32)],
        sem=("parallel", "arbitrary"), args=(a, b), hosted=hosted)
    return out, got


PIECE_W = (256,) * 9 + (512, 128, 128, GATE_W)


def _inproj_fwd(h, g, w_t, b_in, name):
    t = h.shape[0]
    tm, nc = 512, 512

    def body(h_ref, g_ref, w_ref, b_ref, u_ref, zq_ref, zg_ref):
        xhat, _ = _rms(h_ref[...], g_ref[...])
        u = (xhat * g_ref[...]).astype(BF16)
        u_ref[...] = u
        for c in range(D_IN // nc):
            z = _nt(u, w_ref[c * nc:(c + 1) * nc, :]) + b_ref[:, c * nc:(c + 1) * nc]
            if c < QKV_W // nc:
                zq_ref[:, c * nc:(c + 1) * nc] = z.astype(BF16)
            else:
                zg_ref[:, c * nc - QKV_W:(c + 1) * nc - QKV_W] = z

    row = lambda w: pl.BlockSpec((tm, w), lambda i: (i, 0))
    full = _resident
    return pl.pallas_call(
        body, name=name, grid=(t // tm,),
        in_specs=[row(D_MODEL), full(g), full(w_t), full(b_in)],
        out_specs=[row(D_MODEL), row(QKV_W), row(GATE_W)],
        out_shape=[jax.ShapeDtypeStruct((t, D_MODEL), BF16), jax.ShapeDtypeStruct((t, QKV_W), BF16),
                   jax.ShapeDtypeStruct((t, GATE_W), F32)],
        compiler_params=_params("parallel"),
    )(h, g, w_t, b_in)


def _inproj_bwd(pieces, h, g, dh_res, w_t, name):
    t = h.shape[0]
    tm = 256
    npiece = len(PIECE_W)
    offs = np.concatenate([[0], np.cumsum(PIECE_W)]).tolist()

    def body(*refs):
        p_refs = refs[:npiece]
        h_ref, g_ref, dhr_ref, w_ref, dh_ref, dz_ref, db_ref, dg_ref = refs[npiece:]
        i = pl.program_id(0)

        @pl.when(i == 0)
        def _():
            db_ref[...] = jnp.zeros_like(db_ref)
            dg_ref[...] = jnp.zeros_like(dg_ref)

        du = jnp.zeros((tm, D_MODEL), F32)
        for k in range(npiece):
            o, w = offs[k], PIECE_W[k]
            for c0 in range(0, w, 512):
                cw = min(512, w - c0)
                pz = p_refs[k][:, c0:c0 + cw]
                dz_ref[:, o + c0:o + c0 + cw] = pz
                db_ref[:, o + c0:o + c0 + cw] += jnp.sum(pz.astype(F32), axis=0, keepdims=True)
                du = du + _nn(pz, w_ref[o + c0:o + c0 + cw, :])
        xhat, r = _rms(h_ref[...], g_ref[...])
        dx, dg = _rms_bwd(du, xhat, r, g_ref[...])
        dh_ref[...] = dhr_ref[...] + dx
        dg_ref[...] += dg

    row = lambda w: pl.BlockSpec((tm, w), lambda i: (i, 0))
    full = lambda shp: pl.BlockSpec(shp, lambda i: (0, 0))
    return pl.pallas_call(
        body, name=name, grid=(t // tm,),
        in_specs=[row(w) for w in PIECE_W] + [row(D_MODEL), _resident(g), row(D_MODEL), _resident(w_t)],
        out_specs=[row(D_MODEL), row(D_IN), full((1, D_IN)), full((1, D_MODEL))],
        out_shape=[jax.ShapeDtypeStruct((t, D_MODEL), F32), jax.ShapeDtypeStruct((t, D_IN), BF16),
                   jax.ShapeDtypeStruct((1, D_IN), F32), jax.ShapeDtypeStruct((1, D_MODEL), F32)],
        compiler_params=_params("arbitrary"),
    )(*pieces, h, g, dh_res, w_t)


def _t5_bucket(dist):
    max_exact = N_BUCKETS // 2
    n = jnp.maximum(dist, 0)
    nf = jnp.maximum(n, 1).astype(jnp.float32)
    large = max_exact + (jnp.log(nf / max_exact) / math.log(MAX_DISTANCE / max_exact)
                         * (N_BUCKETS - max_exact)).astype(jnp.int32)
    large = jnp.minimum(large, N_BUCKETS - 1)
    return jnp.where(n < max_exact, n, large)


ATT_CFG = ((1, 128, 0, 4), (4, 128, 4, 4), (16, 128, 8, 4), (1, B_WINDOW - 1, A_HEADS, 8))


def _bucket_tiles():
    qi = jnp.arange(BLOCK)[:, None]
    ki = jnp.arange(2 * BLOCK)[None, :]
    dist = qi + BLOCK - ki
    return jnp.stack([_t5_bucket(dist * cfg[0]) for cfg in ATT_CFG]).astype(jnp.int32)


def _band(max_steps):
    row = lax.broadcasted_iota(jnp.int32, (BLOCK, 2 * BLOCK), 0)
    col = lax.broadcasted_iota(jnp.int32, (BLOCK, 2 * BLOCK), 1)
    dist = row + BLOCK - col
    return (dist >= 0) & (dist <= max_steps)


def _bias_build(table, buckets):
    def body(tab_ref, bt_ref, out_ref):
        col = lax.broadcasted_iota(jnp.int32, (BLOCK, 2 * BLOCK), 1)
        for ci, (_, max_steps, h0, nh) in enumerate(ATT_CFG):
            bt = bt_ref[ci]
            band = _band(max_steps)
            for h in range(h0, h0 + nh):
                acc = lax.fori_loop(0, N_BUCKETS, lambda b, acc: jnp.where(bt == b, tab_ref[b, h], acc),
                                    jnp.zeros((BLOCK, 2 * BLOCK), F32))
                out_ref[0, BLOCK * h:BLOCK * (h + 1), :] = jnp.where(band & (col >= BLOCK), acc, NEG)
                out_ref[1, BLOCK * h:BLOCK * (h + 1), :] = jnp.where(band, acc, NEG)

    return pl.pallas_call(
        body, name="bias_build",
        in_specs=[pl.BlockSpec(memory_space=pltpu.SMEM), pl.BlockSpec(memory_space=pltpu.VMEM)],
        out_specs=pl.BlockSpec(memory_space=pltpu.VMEM),
        out_shape=jax.ShapeDtypeStruct((2, TOTAL_HEADS * BLOCK, 2 * BLOCK), F32),
    )(table, buckets)


def _bias_reduce(dbias, buckets, dsink_rows):
    def body(db_ref, bt_ref, ds_ref, out_ref, sink_ref):
        ri = lax.broadcasted_iota(jnp.int32, (N_BUCKETS, 128), 0)
        ci = lax.broadcasted_iota(jnp.int32, (N_BUCKETS, 128), 1)

        def per_bucket(b, acc):
            for cfg_i, (_, _, h0, nh) in enumerate(ATT_CFG):
                hit = bt_ref[cfg_i] == b
                for h in range(h0, h0 + nh):
                    val = jnp.sum(jnp.where(hit, db_ref[BLOCK * h:BLOCK * (h + 1), :], 0.0))
                    acc = jnp.where((ri == b) & (ci == h), val, acc)
            return acc

        out_ref[...] = lax.fori_loop(0, N_BUCKETS, per_bucket, jnp.zeros((N_BUCKETS, 128), F32))
        for h in range(8):
            sink_ref[h:h + 1, :] = jnp.sum(ds_ref[BLOCK * h:BLOCK * (h + 1), :], axis=0, keepdims=True)

    return pl.pallas_call(
        body, name="bias_reduce",
        in_specs=[pl.BlockSpec(memory_space=pltpu.VMEM)] * 3,
        out_specs=[pl.BlockSpec(memory_space=pltpu.VMEM)] * 2,
        out_shape=[jax.ShapeDtypeStruct((N_BUCKETS, 128), F32), jax.ShapeDtypeStruct((8, 128), F32)],
    )(dbias, buckets, dsink_rows)


class _Att:
    def __init__(self, cfg_i):
        stride, _, h0, nh = ATT_CFG[cfg_i]
        self.d = stride if cfg_i < 3 else 1
        self.h0, self.nh = h0, nh
        if cfg_i < 3:
            self.nq, self.wkv = 1, 256
            self.q_unit = [cfg_i]
            self.k_unit, self.v_unit = 3 + cfg_i, 6 + cfg_i
            self.sinks = False
        else:
            self.nq, self.wkv = 2, 128
            self.q_unit = [9, 10]
            self.k_unit, self.v_unit = 22, 23
            self.sinks = True
        self.wq = 256 * self.nq


def _att_in_specs(cf, bsz):
    uq, ukv = QKV_W // 256, QKV_W // cf.wkv
    specs = [pl.BlockSpec((bsz, BLOCK, 256), functools.partial(lambda r, j, u: (0, j, r * uq + u), u=u))
             for u in cf.q_unit]
    for unit in (cf.k_unit, cf.v_unit):
        specs.append(pl.BlockSpec((bsz, BLOCK, cf.wkv),
                                  functools.partial(lambda r, j, u: (0, jnp.maximum(j - 1, 0), r * ukv + u), u=unit)))
        specs.append(pl.BlockSpec((bsz, BLOCK, cf.wkv),
                                  functools.partial(lambda r, j, u: (0, j, r * ukv + u), u=unit)))
    for qb in range(cf.nq):
        specs.append(pl.BlockSpec((None, HEADS_PER_BLOCK * BLOCK, 2 * BLOCK), functools.partial(
            lambda r, j, u: (jnp.minimum(j, 1), u, 0), u=cf.h0 // HEADS_PER_BLOCK + qb)))
    if cf.sinks:
        specs += [pl.BlockSpec((HEADS_PER_BLOCK * BLOCK, 128), functools.partial(lambda r, j, u: (u, 0), u=qb))
                  for qb in range(cf.nq)]
    return specs


HEADS_PER_BLOCK = 4


def _head_masks(rows):
    head = lax.broadcasted_iota(jnp.int32, (rows, 256), 1) // HEAD_DIM
    return [head == h for h in range(HEADS_PER_BLOCK)]


def _stack_heads(x, masks):
    return jnp.concatenate([jnp.where(m, x, jnp.zeros_like(x)) for m in masks], axis=0)


def _unstack_heads(x4, masks):
    blocks = [x4[BLOCK * h:BLOCK * (h + 1)] for h in range(HEADS_PER_BLOCK)]
    return jnp.where(masks[0], blocks[0], jnp.where(masks[1], blocks[1], jnp.where(masks[2], blocks[2], blocks[3])))


def _row_value(x):
    return jnp.max(x, axis=-1, keepdims=True)


def _kv_operand(cf, x, qb):
    if cf.wkv == 256:
        return x
    lane = lax.broadcasted_iota(jnp.int32, x.shape, 1)
    swapped = pltpu.roll(x, HEAD_DIM, 1)
    half = jnp.where(lane < HEAD_DIM, x, swapped) if qb == 0 else jnp.where(lane < HEAD_DIM, swapped, x)
    return jnp.concatenate([half, half], axis=1)


def _kv_fold(cf, grads):
    if cf.wkv == 256:
        return grads[0]
    folded = []
    for g in grads:
        x = g[:, :128] + g[:, 128:]
        folded.append(x + pltpu.roll(x, HEAD_DIM, 1))
    lane = lax.broadcasted_iota(jnp.int32, folded[0].shape, 1)
    return jnp.where(lane < HEAD_DIM, folded[0], folded[1])


def _attn_fwd(cf, zf, bias, sinks, name, hosted=()):
    bsz, l, _ = zf.shape
    nb = l // BLOCK

    def body(*refs):
        refs = list(refs)
        q_refs = [refs.pop(0) for _ in range(cf.nq)]
        kp_ref, kc_ref, vp_ref, vc_ref = [refs.pop(0) for _ in range(4)]
        bias_refs = [refs.pop(0) for _ in range(cf.nq)]
        sink_refs = [refs.pop(0) for _ in range(cf.nq)] if cf.sinks else None
        o_ref, lse_ref = refs
        masks, kv_masks = _head_masks(BLOCK), _head_masks(2 * BLOCK)
        sinks4 = [_row_value(ref[...]) for ref in sink_refs] if cf.sinks else None
        for bi in range(bsz):
            k = jnp.concatenate([kp_ref[bi], kc_ref[bi]], axis=0)
            v = jnp.concatenate([vp_ref[bi], vc_ref[bi]], axis=0)
            for qb in range(cf.nq):
                cols = slice(256 * qb, 256 * (qb + 1))
                kb, vb = _kv_operand(cf, k, qb), _kv_operand(cf, v, qb)
                q4 = _stack_heads(q_refs[qb][bi] * (HEAD_DIM ** -0.5), masks)
                s = _nt(q4, kb) + bias_refs[qb][...]
                m = jnp.max(s, axis=-1, keepdims=True)
                if cf.sinks:
                    sk = sinks4[qb]
                    m = jnp.maximum(m, sk)
                p = jnp.exp(s - m)
                den = jnp.sum(p, axis=-1, keepdims=True)
                if cf.sinks:
                    den = den + jnp.exp(sk - m)
                pn = (p * (1.0 / den)).astype(BF16)
                p_lanes = jnp.concatenate([pn[BLOCK * h:BLOCK * (h + 1)] for h in range(HEADS_PER_BLOCK)], axis=1)
                v4 = jnp.concatenate([jnp.where(mk, vb, jnp.zeros_like(vb)) for mk in kv_masks], axis=0)
                o_ref[bi, :, cols] = _nn(p_lanes, v4)
                lse_ref[bi, :, cols] = _unstack_heads(
                    jnp.broadcast_to(m + jnp.log(den), (HEADS_PER_BLOCK * BLOCK, 256)), masks)

    in_specs = _att_in_specs(cf, bsz)
    args = [zf] * (cf.nq + 4) + [bias] * cf.nq + ([sinks] * cf.nq if cf.sinks else [])
    out = pl.BlockSpec((bsz, BLOCK, cf.wq), lambda r, j: (0, j, r))
    shape = jax.ShapeDtypeStruct((bsz, l, cf.d * cf.wq), F32)
    return _call(
        body, name=name, grid=(cf.d, nb), in_specs=in_specs, out_specs=[out, out], out_shape=[shape, shape],
        sem=("parallel", "arbitrary"), args=args, hosted=hosted)


def _attn_bwd(cf, zf, bias, sinks, lse_tot, dy, y, name, hosted=()):
    bsz, l, _ = zf.shape
    nb = l // BLOCK

    def body(*refs):
        refs = list(refs)
        q_refs = [refs.pop(0) for _ in range(cf.nq)]
        kp_ref, kc_ref, vp_ref, vc_ref = [refs.pop(0) for _ in range(4)]
        bias_refs = [refs.pop(0) for _ in range(cf.nq)]
        sink_refs = [refs.pop(0) for _ in range(cf.nq)] if cf.sinks else None
        lt_ref, dy_ref, y_ref, dq_ref, dk_ref, dv_ref, dbias_ref = [refs.pop(0) for _ in range(7)]
        dsink_ref = refs.pop(0) if cf.sinks else None
        dk_acc, dv_acc = refs
        r, j = pl.program_id(0), pl.program_id(1)

        @pl.when((r == 0) & (j == 0))
        def _():
            dbias_ref[...] = jnp.zeros_like(dbias_ref)
            if cf.sinks:
                dsink_ref[...] = jnp.zeros_like(dsink_ref)

        @pl.when(j == 0)
        def _():
            dk_acc[...] = jnp.zeros_like(dk_acc)
            dv_acc[...] = jnp.zeros_like(dv_acc)

        masks = _head_masks(BLOCK)
        cur = pl.ds(pl.multiple_of(j * BLOCK, BLOCK), BLOCK)
        prev = pl.ds(pl.multiple_of(jnp.maximum(j - 1, 0) * BLOCK, BLOCK), BLOCK)
        sinks4 = [_row_value(ref[...]) for ref in sink_refs] if cf.sinks else None
        for bi in range(bsz):
            k = jnp.concatenate([kp_ref[bi], kc_ref[bi]], axis=0)
            v = jnp.concatenate([vp_ref[bi], vc_ref[bi]], axis=0)
            dk_blocks, dv_blocks = [], []
            for qb in range(cf.nq):
                cols = slice(256 * qb, 256 * (qb + 1))
                rows = slice(HEADS_PER_BLOCK * BLOCK * qb, HEADS_PER_BLOCK * BLOCK * (qb + 1))
                kb, vb = _kv_operand(cf, k, qb), _kv_operand(cf, v, qb)
                q4 = _stack_heads(q_refs[qb][bi] * (HEAD_DIM ** -0.5), masks)
                lt = lt_ref[bi, :, cols]
                lt4 = jnp.concatenate([_row_value(jnp.where(mk, lt, NEG)) for mk in masks], axis=0)
                pa = jnp.exp(_nt(q4, kb) + bias_refs[qb][...] - lt4)
                dy = dy_ref[bi, :, cols]
                prod = dy * y_ref[bi, :, cols]
                e4 = jnp.concatenate([jnp.sum(jnp.where(mk, prod, 0.0), axis=-1, keepdims=True) for mk in masks],
                                     axis=0)
                dy4 = _stack_heads(dy.astype(BF16), masks)
                ds = pa * (_nt(dy4, vb) - e4)
                dbias_ref[rows, :] += ds
                if cf.sinks:
                    dsink_ref[rows, :] += -(jnp.exp(sinks4[qb] - lt4) * e4)
                dsb = ds.astype(BF16)
                dq_ref[bi, :, cols] = (_unstack_heads(_nn(dsb, kb), masks) * (HEAD_DIM ** -0.5)).astype(BF16)
                dk_blocks.append(_tn(dsb, q4))
                dv_blocks.append(_tn(pa.astype(BF16), dy4))
            dk_new, dv_new = _kv_fold(cf, dk_blocks), _kv_fold(cf, dv_blocks)
            dk_acc[bi, cur, :] += dk_new[BLOCK:]
            dv_acc[bi, cur, :] += dv_new[BLOCK:]
            dk_acc[bi, prev, :] += dk_new[:BLOCK]
            dv_acc[bi, prev, :] += dv_new[:BLOCK]

        @pl.when(j == nb - 1)
        def _():
            dk_ref[...] = dk_acc[...].astype(BF16)
            dv_ref[...] = dv_acc[...].astype(BF16)

    tok = pl.BlockSpec((bsz, BLOCK, cf.wq), lambda r, j: (0, j, r))
    in_specs = _att_in_specs(cf, bsz) + [tok] * 3
    args = [zf] * (cf.nq + 4) + [bias] * cf.nq + ([sinks] * cf.nq if cf.sinks else []) + [lse_tot, dy, y]
    seq = pl.BlockSpec((bsz, l, cf.wkv), lambda r, j: (0, 0, r))
    out_specs = [tok, seq, seq, pl.BlockSpec((cf.nh * BLOCK, 2 * BLOCK), lambda r, j: (0, 0))]
    out_shape = [jax.ShapeDtypeStruct((bsz, l, cf.d * cf.wq), BF16),
                 jax.ShapeDtypeStruct((bsz, l, cf.d * cf.wkv), BF16),
                 jax.ShapeDtypeStruct((bsz, l, cf.d * cf.wkv), BF16),
                 jax.ShapeDtypeStruct((cf.nh * BLOCK, 2 * BLOCK), F32)]
    if cf.sinks:
        out_specs.append(pl.BlockSpec((cf.nh * BLOCK, 128), lambda r, j: (0, 0)))
        out_shape.append(jax.ShapeDtypeStruct((cf.nh * BLOCK, 128), F32))
    return _call(
        body, name=name, grid=(cf.d, nb), in_specs=in_specs, out_specs=out_specs, out_shape=out_shape,
        scratch=[pltpu.VMEM((bsz, l, cf.wkv), F32), pltpu.VMEM((bsz, l, cf.wkv), F32)],
        sem=("arbitrary", "arbitrary"), args=args, hosted=hosted)


def _merge_fwd(o_a, lse_a, o_b, zg, h, wa_t, wb_t, wout, name):
    t = h.shape[0]
    tm = 512

    def body(o1, o2, o3, l1, l2, l3, ob_ref, zg_ref, h_ref, wa_ref, wb_ref, wo_ref, h2_ref, ya_ref, lt_ref):
        m = jnp.maximum(jnp.maximum(l1[...], l2[...]), l3[...])
        e1, e2, e3 = jnp.exp(l1[...] - m), jnp.exp(l2[...] - m), jnp.exp(l3[...] - m)
        se = e1 + e2 + e3
        ya = (e1 / se) * o1[...] + (e2 / se) * o2[...] + (e3 / se) * o3[...]
        ya_ref[...] = ya
        lt_ref[...] = m + jnp.log(se)
        pa = _nt(ya.astype(BF16), wa_ref[...])
        pb = _nt(ob_ref[...].astype(BF16), wb_ref[...])
        merged = jax.nn.sigmoid(zg_ref[:, :D_MODEL]) * pa + jax.nn.sigmoid(zg_ref[:, D_MODEL:]) * pb
        h2_ref[...] = h_ref[...] + _nn(merged.astype(BF16), wo_ref[...])

    row = lambda w: pl.BlockSpec((tm, w), lambda i: (i, 0))
    full = _resident
    return pl.pallas_call(
        body, name=name, grid=(t // tm,),
        in_specs=[row(256)] * 6 + [row(512), row(GATE_W), row(D_MODEL), full(wa_t), full(wb_t), full(wout)],
        out_specs=[row(D_MODEL), row(256), row(256)],
        out_shape=[jax.ShapeDtypeStruct((t, D_MODEL), F32), jax.ShapeDtypeStruct((t, 256), F32),
                   jax.ShapeDtypeStruct((t, 256), F32)],
        compiler_params=_params("parallel"),
    )(*o_a, *lse_a, o_b, zg, h, wa_t, wb_t, wout)


def _merge_bwd(dh, ya, o_b, zg, wa_t, wb_t, wout, name, hosted=()):
    t = dh.shape[0]
    tm = 256

    def body(dh_ref, ya_ref, ob_ref, zg_ref, wa_ref, wb_ref, wo_ref,
             mg_ref, dpa_ref, dpb_ref, dzg_ref, dya_ref, dyb_ref):
        dm = _nt(dh_ref[...].astype(BF16), wo_ref[...])
        pa = _nt(ya_ref[...].astype(BF16), wa_ref[...])
        pb = _nt(ob_ref[...].astype(BF16), wb_ref[...])
        sa = jax.nn.sigmoid(zg_ref[:, :D_MODEL])
        sb = jax.nn.sigmoid(zg_ref[:, D_MODEL:])
        mg_ref[...] = (sa * pa + sb * pb).astype(BF16)
        dpa = (dm * sa).astype(BF16)
        dpb = (dm * sb).astype(BF16)
        dpa_ref[...] = dpa
        dpb_ref[...] = dpb
        dzg_ref[:, :D_MODEL] = (dm * pa * (sa * (1.0 - sa))).astype(BF16)
        dzg_ref[:, D_MODEL:] = (dm * pb * (sb * (1.0 - sb))).astype(BF16)
        dya_ref[...] = _nn(dpa, wa_ref[...])
        dyb_ref[...] = _nn(dpb, wb_ref[...])

    row = lambda w: pl.BlockSpec((tm, w), lambda i: (i, 0))
    full = _resident
    sds = jax.ShapeDtypeStruct
    return _call(
        body, name=name, grid=(t // tm,),
        in_specs=[row(D_MODEL), row(256), row(512), row(GATE_W), full(wa_t), full(wb_t), full(wout)],
        out_specs=[row(D_MODEL), row(D_MODEL), row(D_MODEL), row(GATE_W), row(256), row(512)],
        out_shape=[sds((t, D_MODEL), BF16), sds((t, D_MODEL), BF16), sds((t, D_MODEL), BF16),
                   sds((t, GATE_W), BF16), sds((t, 256), F32), sds((t, 512), F32)],
        sem=("parallel",), args=(dh, ya, o_b, zg, wa_t, wb_t, wout), hosted=hosted)


def _loss_head(h, g, target, name):
    t = h.shape[0]
    tm = 1024

    def body(h_ref, g_ref, t_ref, dh_ref, loss_ref, dg_ref):
        @pl.when(pl.program_id(0) == 0)
        def _():
            loss_ref[...] = jnp.zeros_like(loss_ref)
            dg_ref[...] = jnp.zeros_like(dg_ref)

        xhat, r = _rms(h_ref[...], g_ref[...])
        err = xhat * g_ref[...] - t_ref[...]
        loss_ref[...] += 0.5 * jnp.sum(jnp.mean(err * err, axis=-1, keepdims=True), axis=0, keepdims=True)
        dx, dg = _rms_bwd(err * (1.0 / D_MODEL), xhat, r, g_ref[...])
        dh_ref[...] = dx
        dg_ref[...] += dg

    row = pl.BlockSpec((tm, D_MODEL), lambda i: (i, 0))
    vec = pl.BlockSpec((1, D_MODEL), lambda i: (0, 0))
    return pl.pallas_call(
        body, name=name, grid=(t // tm,),
        in_specs=[row, vec, row], out_specs=[row, pl.BlockSpec((1, 128), lambda i: (0, 0)), vec],
        out_shape=[jax.ShapeDtypeStruct((t, D_MODEL), F32), jax.ShapeDtypeStruct((1, 128), F32),
                   jax.ShapeDtypeStruct((1, D_MODEL), F32)],
        compiler_params=_params("arbitrary"),
    )(h, g, target)


def _pair_sum(grad, got, name):
    _, _, r, cdim = grad.shape
    core = lax.axis_index("c").astype(jnp.int32).reshape(1)

    def body(core_ref, g_ref, s_ref, o_ref):
        o_ref[...] = (g_ref[...] + s_ref[...]).astype(BF16)

    return pl.pallas_call(
        body, name=name,
        grid_spec=pltpu.PrefetchScalarGridSpec(
            num_scalar_prefetch=1, grid=(N_CHIP,),
            in_specs=[pl.BlockSpec((None, None, r, cdim), lambda q, core_ref: (q, core_ref[0], 0, 0)),
                      pl.BlockSpec((None, None, r, cdim), lambda q, core_ref: (q, 0, 0, 0))],
            out_specs=pl.BlockSpec((None, r, cdim), lambda q, core_ref: (q, 0, 0))),
        out_shape=jax.ShapeDtypeStruct((N_CHIP, r, cdim), BF16),
        compiler_params=_params("parallel"),
    )(core, grad, got)


def _chip_sum(parts, name):
    def body(p_ref, o_ref):
        o_ref[...] = ((p_ref[0].astype(F32) + p_ref[1].astype(F32)) + p_ref[2].astype(F32)) + p_ref[3].astype(F32)

    return pl.pallas_call(
        body, name=name, out_shape=jax.ShapeDtypeStruct(parts.shape[1:], F32),
        compiler_params=pltpu.CompilerParams(vmem_limit_bytes=VMEM_LIMIT),
    )(parts)


def _adamw_math(w, g, m, v):
    m = ADAM_B1 * m + (1.0 - ADAM_B1) * g
    v = ADAM_B2 * v + (1.0 - ADAM_B2) * jnp.square(g)
    m_hat = m / (1.0 - ADAM_B1 ** ADAM_STEP)
    v_hat = v / (1.0 - ADAM_B2 ** ADAM_STEP)
    delta = -ADAM_LR * (m_hat / (jnp.sqrt(v_hat) + ADAM_EPS) + ADAM_WD * w)
    return delta, m, v


def _adamw(w, g, m, v, name):
    def body(w_ref, g_ref, m_ref, v_ref, d_ref, m2_ref, v2_ref):
        d_ref[...], m2_ref[...], v2_ref[...] = _adamw_math(w_ref[...], g_ref[...], m_ref[...], v_ref[...])

    sds = jax.ShapeDtypeStruct(w.shape, F32)
    return pl.pallas_call(body, name=name, out_shape=[sds, sds, sds],
                          compiler_params=pltpu.CompilerParams(vmem_limit_bytes=VMEM_LIMIT))(w, g, m, v)


SMALL_ROWS = 80


def _small_update(g, w, m, v, name):
    def body(g_ref, w_ref, m_ref, v_ref, gs_ref, d_ref, m2_ref, v2_ref, got_ref, send_sems, recv_sems):
        x, y, c = _place()
        me = 4 * x + 2 * y + c
        got_ref[me] = g_ref[...]
        copies = []
        for k in range(1, N_DEV):
            peer = (x ^ (k >> 2), y ^ ((k >> 1) & 1), c ^ (k & 1))
            cp = pltpu.make_async_remote_copy(
                src_ref=g_ref, dst_ref=got_ref.at[me], send_sem=send_sems.at[k - 1], recv_sem=recv_sems.at[k - 1],
                device_id=peer, device_id_type=MESH)
            cp.start()
            copies.append(cp)
        for cp in copies:
            cp.wait()
        total = got_ref[0]
        for k in range(1, N_DEV):
            total = total + got_ref[k]
        gs_ref[...] = total
        d_ref[...], m2_ref[...], v2_ref[...] = _adamw_math(w_ref[...], total, m_ref[...], v_ref[...])

    sds = jax.ShapeDtypeStruct((SMALL_ROWS, 128), F32)
    vm = pl.BlockSpec(memory_space=pltpu.VMEM)
    return pl.pallas_call(
        body, name=name, in_specs=[vm] * 4, out_specs=[vm] * 4, out_shape=[sds] * 4,
        scratch_shapes=[pltpu.VMEM((N_DEV, SMALL_ROWS, 128), F32), pltpu.SemaphoreType.DMA((N_DEV - 1,)),
                        pltpu.SemaphoreType.DMA((N_DEV - 1,))],
    )(g, w, m, v)


def _pack_small(gains, b_in, rel_bias, sinks, last):
    rows = [a.reshape(8, 128) for a in gains] + [b_in.reshape(40, 128), rel_bias.reshape(5, 128),
                                                 jnp.pad(sinks.reshape(1, 8), ((0, 0), (0, 120))), last]
    rows.append(jnp.zeros((SMALL_ROWS - 79, 128), F32))
    return jnp.concatenate(rows, axis=0)


def _unpack_small(p, like):
    out = [p[8 * i:8 * i + 8].reshape(like[i].shape) for i in range(4)]
    out.append(p[32:72].reshape(like[4].shape))
    out.append(p[72:77].reshape(like[5].shape))
    out.append(p[77, :8].reshape(like[6].shape))
    return out


def kernel(x, ffn1_norm, ffn1_w_gate, ffn1_w_up, ffn1_w_down, mix_norm, w_in, b_in, w_branch_a, w_branch_b, w_out, sinks, rel_bias, ffn2_norm, ffn2_w_gate, ffn2_w_up, ffn2_w_down, final_norm, loss_target, m_ffn1_norm, m_ffn1_w_gate, m_ffn1_w_up, m_ffn1_w_down, m_mix_norm, m_w_in, m_b_in, m_w_branch_a, m_w_branch_b, m_w_out, m_sinks, m_rel_bias, m_ffn2_norm, m_ffn2_w_gate, m_ffn2_w_up, m_ffn2_w_down, m_final_norm, v_ffn1_norm, v_ffn1_w_gate, v_ffn1_w_up, v_ffn1_w_down, v_mix_norm, v_w_in, v_b_in, v_w_branch_a, v_w_branch_b, v_w_out, v_sinks, v_rel_bias, v_ffn2_norm, v_ffn2_w_gate, v_ffn2_w_up, v_ffn2_w_down, v_final_norm):
    bsz, seq, _ = x.shape
    t = bsz * seq
    xt = x.reshape(t, D_MODEL)
    target = loss_target.reshape(t, D_MODEL)

    big = [("ffn1_w_gate", ffn1_w_gate, m_ffn1_w_gate, v_ffn1_w_gate, True),
           ("ffn1_w_up", ffn1_w_up, m_ffn1_w_up, v_ffn1_w_up, True),
           ("ffn1_w_down", ffn1_w_down, m_ffn1_w_down, v_ffn1_w_down, False),
           ("w_in", w_in, m_w_in, v_w_in, True),
           ("w_branch_a", w_branch_a, m_w_branch_a, v_w_branch_a, True),
           ("w_branch_b", w_branch_b, m_w_branch_b, v_w_branch_b, True),
           ("w_out", w_out, m_w_out, v_w_out, False),
           ("ffn2_w_gate", ffn2_w_gate, m_ffn2_w_gate, v_ffn2_w_gate, True),
           ("ffn2_w_up", ffn2_w_up, m_ffn2_w_up, v_ffn2_w_up, True),
           ("ffn2_w_down", ffn2_w_down, m_ffn2_w_down, v_ffn2_w_down, False)]
    shard = {nm: (w[0].T if tr else w[0]).astype(BF16) for nm, w, _, _, tr in big}
    full = {}

    def gather(names):
        return _Gather([shard[nm] for nm in names])

    def keep(names, got):
        for nm, gw in zip(names, got):
            full[nm] = gw.reshape(-1, gw.shape[-1])

    ffn1_names = ["ffn1_w_gate", "ffn1_w_up", "ffn1_w_down"]
    mix_names = ["w_in", "w_branch_a", "w_branch_b", "w_out"]
    ffn2_names = ["ffn2_w_gate", "ffn2_w_up", "ffn2_w_down"]
    g1, gm, g2, gf = ffn1_norm, mix_norm, ffn2_norm, final_norm.reshape(1, D_MODEL)

    keep(ffn1_names, _exchange([gather(ffn1_names)], "gather_ffn1")[0])
    (h1, n1), (got,) = _ffn_fwd(xt, g1, full["ffn1_w_gate"], full["ffn1_w_up"], full["ffn1_w_down"], "ffn1_fwd",
                                hosted=[gather(mix_names)])
    keep(mix_names, got)
    u, zq, zg = _inproj_fwd(h1, gm, full["w_in"], b_in, "inproj_fwd")
    buckets = _bucket_tiles()
    bias = _bias_build(rel_bias, buckets)
    sink_rows = jnp.broadcast_to(sinks.reshape(8, 1, 1), (8, BLOCK, 128)).reshape(8 * BLOCK, 128)
    cfs = [_Att(i) for i in range(4)]
    zfold = [zq.reshape(bsz, seq // cf.d, cf.d * QKV_W) for cf in cfs]
    att = [None] * 4
    for i in (3, 0, 1, 2):
        cf = cfs[i]
        (o, lse), got = _attn_fwd(cf, zfold[i], bias, sink_rows, f"attn{i}_fwd",
                                  hosted=[gather(ffn2_names)] if i == 3 else ())
        if i == 3:
            keep(ffn2_names, got[0])
        att[i] = (o.reshape(t, cf.wq), lse.reshape(t, cf.wq))
    o_b, lse_b = att[3]
    h2, ya, lse_tot = _merge_fwd([a[0] for a in att[:3]], [a[1] for a in att[:3]], o_b, zg, h1,
                                 full["w_branch_a"], full["w_branch_b"], full["w_out"], "merge_fwd")
    (h3, n2), _ = _ffn_fwd(h2, g2, full["ffn2_w_gate"], full["ffn2_w_up"], full["ffn2_w_down"], "ffn2_fwd")
    dh3, loss_part, dgf = _loss_head(h3, gf, target, "loss_head")

    grads, pair, from_chips = {}, {}, {}

    def by_owner(nm):
        return grads[nm].reshape(N_CHIP, 2, -1, grads[nm].shape[-1])

    def to_core(names):
        return _CoreExchange([by_owner(nm) for nm in names])

    def pair_up(names, got):
        for nm, sib in zip(names, got):
            pair[nm] = _pair_sum(by_owner(nm), sib, f"pair_sum_{nm}")

    def to_chips(names):
        return _ChipExchange([pair[nm] for nm in names])

    def landed(names, got):
        for nm, parts in zip(names, got):
            from_chips[nm] = parts

    dh2, dg2, hff, da, db, dhh = _ffn_bwd(h2, n2, g2, dh3, full["ffn2_w_gate"], full["ffn2_w_up"],
                                          full["ffn2_w_down"], "ffn2_bwd")
    grads["ffn2_w_gate"], _ = _tn_matmul(da, n2, 1408, "ffn2_dgate")
    grads["ffn2_w_up"], _ = _tn_matmul(db, n2, 1408, "ffn2_dup")
    grads["ffn2_w_down"], _ = _tn_matmul(hff, dhh, 1408, "ffn2_ddown")
    (merged, dpa, dpb, dzg, dya, dyb), (got,) = _merge_bwd(
        dh2, ya, o_b, zg, full["w_branch_a"], full["w_branch_b"], full["w_out"], "merge_bwd",
        hosted=[to_core(ffn2_names)])
    pair_up(ffn2_names, got)
    grads["w_out"], _ = _tn_matmul(merged, dh2, 1024, "dw_out")
    grads["w_branch_a"], _ = _tn_matmul(dpa, ya, 1024, "dw_branch_a")
    grads["w_branch_b"], _ = _tn_matmul(dpb, o_b, 1024, "dw_branch_b")
    out_names = ["w_out", "w_branch_a", "w_branch_b"]
    dq, dk, dv, dbias, dsink = [None] * 4, [None] * 4, [None] * 4, [None] * 4, None
    for i in (3, 0, 1, 2):
        cf = cfs[i]
        lt_i, dy_i, y_i = (lse_tot, dya, ya) if i < 3 else (lse_b, dyb, o_b)
        shp = (bsz, seq // cf.d, cf.d * cf.wq)
        hosted = {3: lambda: [to_chips(ffn2_names)], 0: lambda: [to_core(out_names)],
                  1: lambda: [to_chips(out_names)], 2: lambda: []}[i]()
        res, got = _attn_bwd(cf, zfold[i], bias, sink_rows, lt_i.reshape(shp), dy_i.reshape(shp), y_i.reshape(shp),
                             f"attn{i}_bwd", hosted=hosted)
        if i == 3:
            landed(ffn2_names, got[0])
        elif i == 0:
            pair_up(out_names, got[0])
        elif i == 1:
            landed(out_names, got[0])
        dq[i] = res[0].reshape(t, cf.wq)
        dk[i] = res[1].reshape(t, cf.wkv)
        dv[i] = res[2].reshape(t, cf.wkv)
        dbias[i] = res[3]
        if cf.sinks:
            dsink = res[4]
    pieces = dq[:3] + dk[:3] + dv[:3] + [dq[3], dk[3], dv[3], dzg]
    dh1, dz, db_in, dgm = _inproj_bwd(pieces, h1, gm, dh2, full["w_in"], "inproj_bwd")
    dx, dg1, hff, da, db, dhh = _ffn_bwd(xt, n1, g1, dh1, full["ffn1_w_gate"], full["ffn1_w_up"],
                                         full["ffn1_w_down"], "ffn1_bwd")
    grads["w_in"], _ = _tn_matmul(dz, u, 1280, "dw_in")
    grads["ffn1_w_down"], (got,) = _tn_matmul(hff, dhh, 1408, "ffn1_ddown", hosted=[to_core(["w_in"])])
    pair_up(["w_in"], got)
    grads["ffn1_w_gate"], got = _tn_matmul(da, n1, 1408, "ffn1_dgate",
                                           hosted=[to_chips(["w_in"]), to_core(["ffn1_w_down"])])
    landed(["w_in"], got[0])
    pair_up(["ffn1_w_down"], got[1])
    grads["ffn1_w_up"], got = _tn_matmul(db, n1, 1408, "ffn1_dup",
                                         hosted=[to_chips(["ffn1_w_down"]), to_core(["ffn1_w_gate"])])
    landed(["ffn1_w_down"], got[0])
    pair_up(["ffn1_w_gate"], got[1])
    got = _exchange([to_chips(["ffn1_w_gate"]), to_core(["ffn1_w_up"])], "reduce_scatter_tail1")
    landed(["ffn1_w_gate"], got[0])
    pair_up(["ffn1_w_up"], got[1])
    landed(["ffn1_w_up"], _exchange([to_chips(["ffn1_w_up"])], "reduce_scatter_tail2")[0])
    dtable, dsinks = _bias_reduce(jnp.concatenate(dbias, axis=0), buckets, dsink)

    out_g, out_d, out_m, out_v = {}, {}, {}, {}
    for nm, w, m, v, tr in big:
        g = _chip_sum(from_chips[nm], f"chip_sum_{nm}")
        g = (g.T if tr else g).reshape(w.shape)
        out_g[nm] = g
        d2, m2, v2 = _adamw(w[0], g[0], m[0], v[0], f"adamw_{nm}")
        out_d[nm], out_m[nm], out_v[nm] = d2[None], m2[None], v2[None]

    small = [("ffn1_norm", ffn1_norm, m_ffn1_norm, v_ffn1_norm), ("mix_norm", mix_norm, m_mix_norm, v_mix_norm),
             ("ffn2_norm", ffn2_norm, m_ffn2_norm, v_ffn2_norm), ("final_norm", final_norm, m_final_norm, v_final_norm),
             ("b_in", b_in, m_b_in, v_b_in), ("rel_bias", rel_bias, m_rel_bias, v_rel_bias),
             ("sinks", sinks, m_sinks, v_sinks)]
    zero_row = jnp.zeros((1, 128), F32)
    pack = lambda arrs, last: _pack_small(arrs[:4], arrs[4], arrs[5], arrs[6], last)
    g_small = pack([dg1, dgm, dg2, dgf, db_in, dtable[:, :TOTAL_HEADS], dsinks[:, 0]], loss_part)
    packed = [pack([s[k] for s in small], zero_row) for k in (1, 2, 3)]
    gs, ds, ms, vs = _small_update(g_small, *packed, "small_update")
    like = [s[1] for s in small]
    for nm_s, g_, d_, m_, v_ in zip([s[0] for s in small], _unpack_small(gs, like), _unpack_small(ds, like),
                                    _unpack_small(ms, like), _unpack_small(vs, like)):
        out_g[nm_s], out_d[nm_s], out_m[nm_s], out_v[nm_s] = g_, d_, m_, v_
    loss = gs[78, 0]

    order = ["ffn1_norm", "ffn1_w_gate", "ffn1_w_up", "ffn1_w_down", "mix_norm", "w_in", "b_in", "w_branch_a",
             "w_branch_b", "w_out", "sinks", "rel_bias", "ffn2_norm", "ffn2_w_gate", "ffn2_w_up", "ffn2_w_down",
             "final_norm"]
    return (loss, dx.reshape(x.shape), *[out_g[k] for k in order], *[out_d[k] for k in order],
            *[out_m[k] for k in order], *[out_v[k] for k in order])
```

```python
import functools
import math

import numpy as np
import jax
import jax.numpy as jnp
from jax import lax
from jax.experimental import pallas as pl
from jax.experimental.pallas import tpu as pltpu

D_MODEL = 1024
D_FF = 2816
FF_CHUNK = 256
HEAD_DIM = 64
BLOCK = 128
N_BUCKETS = 32
MAX_DISTANCE = 2048
A_HEADS = 12
TOTAL_HEADS = 20
DIL_GROUPS = ((128, 1), (512, 4), (2048, 16))
B_WINDOW = 128
QKV_W = 3072
GATE_W = 2048
D_IN = QKV_W + GATE_W
EPS = 1e-6
NEG = -1e30
N_DEV = 8
N_CHIP = 4
ADAM_LR, ADAM_B1, ADAM_B2, ADAM_EPS, ADAM_WD, ADAM_STEP = 0.001, 0.9, 0.999, 1e-08, 0.01, 10
VMEM_LIMIT = 56 * 1024 * 1024
MESH = pl.DeviceIdType.MESH
BF16 = jnp.bfloat16
F32 = jnp.float32
ANY = pl.BlockSpec(memory_space=pl.ANY)


def _params(*sem):
    return pltpu.CompilerParams(dimension_semantics=sem, vmem_limit_bytes=VMEM_LIMIT)


def _resident(a):
    return pl.BlockSpec(a.shape, lambda i: (0, 0), pipeline_mode=pl.Buffered(1))


def _place():
    return lax.axis_index("x"), lax.axis_index("y"), lax.axis_index("c")


class _Gather:
    def __init__(self, shards):
        self.ins = list(shards)
        n = self.n = len(shards)
        self.out_shape = [jax.ShapeDtypeStruct((N_DEV,) + s.shape, s.dtype) for s in shards]
        self.scratch = [pltpu.SemaphoreType.DMA((7 * n,)), pltpu.SemaphoreType.DMA((7 * n,)),
                        pltpu.SemaphoreType.DMA((n,))]

    def _copies(self, ins, outs, sems):
        send_sems, recv_sems, local_sems = sems
        x, y, c = _place()
        me, sibling = (x, y, c), (x, y, 1 - c)
        chips = [(1 - x, y), (x, 1 - y), (1 - x, 1 - y)]

        def copy(i, k, block, to, src=None):
            dst = outs[i].at[4 * block[0] + 2 * block[1] + block[2]]
            return pltpu.make_async_remote_copy(
                src_ref=dst if src is None else src, dst_ref=dst, send_sem=send_sems.at[7 * i + k],
                recv_sem=recv_sems.at[7 * i + k], device_id=to, device_id_type=MESH)

        n = self.n
        mine = [pltpu.make_async_copy(ins[i], outs[i].at[4 * x + 2 * y + c], local_sems.at[i]) for i in range(n)]
        first = [copy(i, 0, me, sibling, src=ins[i]) for i in range(n)]
        first += [copy(i, 1 + j, me, (*chip, c), src=ins[i]) for i in range(n) for j, chip in enumerate(chips)]
        landed = [copy(i, 1 + j, (*chip, c), me) for j, chip in enumerate(chips) for i in range(n)]
        passed = [copy(i, 4 + j, (*chip, c), sibling) for j, chip in enumerate(chips) for i in range(n)]
        from_sibling = [copy(i, 0, sibling, me) for i in range(n)]
        from_sibling += [copy(i, 4 + j, (*chip, 1 - c), me) for i in range(n) for j, chip in enumerate(chips)]
        return mine, first, landed, passed, from_sibling

    def start(self, ins, outs, sems):
        mine, first, _, _, _ = self._copies(ins, outs, sems)
        for cp in mine + first:
            cp.start()

    def mid(self, ins, outs, sems):
        _, _, landed, passed, _ = self._copies(ins, outs, sems)
        for got, fwd in zip(landed, passed):
            got.wait_recv()
            fwd.start()

    def end(self, ins, outs, sems):
        mine, first, _, passed, from_sibling = self._copies(ins, outs, sems)
        for cp in from_sibling:
            cp.wait_recv()
        for cp in first + passed:
            cp.wait_send()
        for cp in mine:
            cp.wait()


class _CoreExchange:
    def __init__(self, grads):
        self.ins = list(grads)
        n = self.n = len(grads)
        self.out_shape = [jax.ShapeDtypeStruct((N_CHIP, 1) + g.shape[2:], g.dtype) for g in grads]
        self.scratch = [pltpu.SemaphoreType.DMA((n,)), pltpu.SemaphoreType.DMA((n,))]

    def _copies(self, ins, outs, sems):
        x, y, c = _place()
        return [pltpu.make_async_remote_copy(
            src_ref=ins[i].at[:, pl.ds(1 - c, 1)], dst_ref=outs[i], send_sem=sems[0].at[i],
            recv_sem=sems[1].at[i], device_id=(x, y, 1 - c), device_id_type=MESH) for i in range(self.n)]

    def start(self, ins, outs, sems):
        for cp in self._copies(ins, outs, sems):
            cp.start()

    mid = None

    def end(self, ins, outs, sems):
        for cp in self._copies(ins, outs, sems):
            cp.wait()


class _ChipExchange:
    def __init__(self, parts):
        self.ins = list(parts)
        n = self.n = len(parts)
        self.out_shape = [jax.ShapeDtypeStruct(p.shape, p.dtype) for p in parts]
        self.scratch = [pltpu.SemaphoreType.DMA((3 * n,)), pltpu.SemaphoreType.DMA((3 * n,)),
                        pltpu.SemaphoreType.DMA((n,))]

    def _copies(self, ins, outs, sems):
        send_sems, recv_sems, local_sems = sems
        x, y, c = _place()
        my_chip = 2 * x + y
        copies = []
        for i in range(self.n):
            copies.append(pltpu.make_async_copy(ins[i].at[my_chip], outs[i].at[my_chip], local_sems.at[i]))
            for k, (qx, qy) in enumerate([(1 - x, y), (x, 1 - y), (1 - x, 1 - y)]):
                copies.append(pltpu.make_async_remote_copy(
                    src_ref=ins[i].at[2 * qx + qy], dst_ref=outs[i].at[my_chip], send_sem=send_sems.at[3 * i + k],
                    recv_sem=recv_sems.at[3 * i + k], device_id=(qx, qy, c), device_id_type=MESH))
        return copies

    def start(self, ins, outs, sems):
        for cp in self._copies(ins, outs, sems):
            cp.start()

    mid = None

    def end(self, ins, outs, sems):
        for cp in self._copies(ins, outs, sems):
            cp.wait()


def _call(body, *, name, grid, in_specs, out_specs, out_shape, args, scratch=(), sem=None, hosted=()):
    n_in, n_out, n_scr = len(in_specs), len(out_specs), len(scratch)
    x_in = [len(p.ins) for p in hosted]
    x_scr = [len(p.scratch) for p in hosted]
    steps = int(np.prod(grid))
    assert not hosted or steps >= 3

    def wrapped(*refs):
        refs = list(refs)
        ins, refs = refs[:n_in], refs[n_in:]
        x_ins = [[refs.pop(0) for _ in range(k)] for k in x_in]
        outs, refs = refs[:n_out], refs[n_out:]
        x_outs = [[refs.pop(0) for _ in range(k)] for k in x_in]
        scr, refs = refs[:n_scr], refs[n_scr:]
        x_sems = [[refs.pop(0) for _ in range(k)] for k in x_scr]
        step = 0
        for d in range(len(grid)):
            step = step * grid[d] + pl.program_id(d)

        def phase(which, at):
            fns = [(getattr(p, which), a) for p, a in zip(hosted, zip(x_ins, x_outs, x_sems)) if getattr(p, which)]
            if fns:
                @pl.when(step == at)
                def _():
                    for fn, a in fns:
                        fn(*a)

        phase("start", 0)
        phase("mid", steps // 2)
        body(*ins, *outs, *scr)
        phase("end", steps - 1)

    results = pl.pallas_call(
        wrapped, name=name, grid=grid,
        in_specs=list(in_specs) + [ANY] * sum(x_in), out_specs=list(out_specs) + [ANY] * sum(x_in),
        out_shape=list(out_shape) + [s for p in hosted for s in p.out_shape],
        scratch_shapes=list(scratch) + [s for p in hosted for s in p.scratch],
        compiler_params=_params(*(("arbitrary",) * len(grid) if hosted else sem)),
    )(*args, *[a for p in hosted for a in p.ins])
    own, rest = list(results[:n_out]), list(results[n_out:])
    return own, [[rest.pop(0) for _ in range(k)] for k in x_in]


def _exchange(programs, name):
    x_in = [len(p.ins) for p in programs]
    x_scr = [len(p.scratch) for p in programs]

    def body(*refs):
        refs = list(refs)
        x_ins = [[refs.pop(0) for _ in range(k)] for k in x_in]
        x_outs = [[refs.pop(0) for _ in range(k)] for k in x_in]
        x_sems = [[refs.pop(0) for _ in range(k)] for k in x_scr]
        for which in ("start", "mid", "end"):
            for p, a in zip(programs, zip(x_ins, x_outs, x_sems)):
                if getattr(p, which):
                    getattr(p, which)(*a)

    results = list(pl.pallas_call(
        body, name=name, in_specs=[ANY] * sum(x_in), out_specs=[ANY] * sum(x_in),
        out_shape=[s for p in programs for s in p.out_shape],
        scratch_shapes=[s for p in programs for s in p.scratch],
    )(*[a for p in programs for a in p.ins]))
    return [[results.pop(0) for _ in range(k)] for k in x_in]


def _nt(a, b):
    return lax.dot_general(a, b, (((1,), (1,)), ((), ())), preferred_element_type=F32)


def _nn(a, b):
    return lax.dot_general(a, b, (((1,), (0,)), ((), ())), preferred_element_type=F32)


def _tn(a, b):
    return lax.dot_general(a, b, (((0,), (0,)), ((), ())), preferred_element_type=F32)


def _rms(x, g):
    r = lax.rsqrt(jnp.mean(x * x, axis=-1, keepdims=True) + EPS)
    return x * r, r


def _rms_bwd(dn, xhat, r, g):
    dg = jnp.sum(dn * xhat, axis=0, keepdims=True)
    dxh = dn * g
    dx = r * (dxh - xhat * jnp.mean(dxh * xhat, axis=-1, keepdims=True))
    return dx, dg


def _ffn_fwd(x, g, wg_t, wu_t, wd, name, hosted=()):
    t = x.shape[0]
    tm = 512

    def body(x_ref, g_ref, wg_ref, wu_ref, wd_ref, h_ref, n_ref, hff_ref):
        xhat, _ = _rms(x_ref[...], g_ref[...])
        n = (xhat * g_ref[...]).astype(BF16)
        n_ref[...] = n
        for c in range(0, D_FF, FF_CHUNK):
            a = _nt(n, wg_ref[c:c + FF_CHUNK, :])
            b = _nt(n, wu_ref[c:c + FF_CHUNK, :])
            hff_ref[:, c:c + FF_CHUNK] = (a * jax.nn.sigmoid(a) * b).astype(BF16)
        h_ref[...] = x_ref[...] + 0.5 * _nn(hff_ref[...], wd_ref[...])

    row = pl.BlockSpec((tm, D_MODEL), lambda i: (i, 0))
    return _call(
        body, name=name, grid=(t // tm,),
        in_specs=[row, _resident(g), _resident(wg_t), _resident(wu_t), _resident(wd)],
        out_specs=[row, row],
        out_shape=[jax.ShapeDtypeStruct((t, D_MODEL), F32), jax.ShapeDtypeStruct((t, D_MODEL), BF16)],
        scratch=[pltpu.VMEM((tm, D_FF), BF16)], sem=("parallel",), args=(x, g, wg_t, wu_t, wd), hosted=hosted)


def _ffn_bwd(x, n, g, dh, wg_t, wu_t, wd, name):
    t = x.shape[0]
    tm = 256

    def body(x_ref, n_ref, g_ref, dh_ref, wg_ref, wu_ref, wd_ref,
             dx_ref, dg_ref, hff_ref, da_ref, db_ref, dhh_ref):
        @pl.when(pl.program_id(0) == 0)
        def _():
            dg_ref[...] = jnp.zeros_like(dg_ref)

        dhh = (0.5 * dh_ref[...]).astype(BF16)
        dhh_ref[...] = dhh
        nb = n_ref[...]
        for c in range(0, D_FF, FF_CHUNK):
            cols = slice(c, c + FF_CHUNK)
            a = _nt(nb, wg_ref[cols, :])
            b = _nt(nb, wu_ref[cols, :])
            s = jax.nn.sigmoid(a)
            silu = a * s
            dhff = _nt(dhh, wd_ref[cols, :])
            hff_ref[:, cols] = (silu * b).astype(BF16)
            da_ref[:, cols] = (dhff * b * (s * (1.0 + a * (1.0 - s)))).astype(BF16)
            db_ref[:, cols] = (dhff * silu).astype(BF16)
        dn = _nn(da_ref[...], wg_ref[...]) + _nn(db_ref[...], wu_ref[...])
        xhat, r = _rms(x_ref[...], g_ref[...])
        dx, dg = _rms_bwd(dn, xhat, r, g_ref[...])
        dx_ref[...] = dh_ref[...] + dx
        dg_ref[...] += dg

    row = pl.BlockSpec((tm, D_MODEL), lambda i: (i, 0))
    hid = pl.BlockSpec((tm, D_FF), lambda i: (i, 0))
    return pl.pallas_call(
        body, name=name, grid=(t // tm,),
        in_specs=[row, row, _resident(g), row, _resident(wg_t), _resident(wu_t), _resident(wd)],
        out_specs=[row, pl.BlockSpec((1, D_MODEL), lambda i: (0, 0)), hid, hid, hid, row],
        out_shape=[jax.ShapeDtypeStruct((t, D_MODEL), F32), jax.ShapeDtypeStruct((1, D_MODEL), F32),
                   jax.ShapeDtypeStruct((t, D_FF), BF16), jax.ShapeDtypeStruct((t, D_FF), BF16),
                   jax.ShapeDtypeStruct((t, D_FF), BF16), jax.ShapeDtypeStruct((t, D_MODEL), BF16)],
        compiler_params=_params("arbitrary"),
    )(x, n, g, dh, wg_t, wu_t, wd)


def _tn_matmul(a, b, rc, name, hosted=()):
    t, r = a.shape
    c = b.shape[1]
    tk = 1024

    def body(a_ref, b_ref, o_ref):
        @pl.when(pl.program_id(1) == 0)
        def _():
            o_ref[...] = jnp.zeros_like(o_ref)

        o_ref[...] += _tn(a_ref[...].astype(BF16), b_ref[...].astype(BF16))

    (out,), got = _call(
        body, name=name, grid=(r // rc, t // tk),
        in_specs=[pl.BlockSpec((tk, rc), lambda i, k: (k, i)), pl.BlockSpec((tk, c), lambda i, k: (k, 0))],
        out_specs=[pl.BlockSpec((rc, c), lambda i, k: (i, 0))],
        out_shape=[jax.ShapeDtypeStruct((r, c), F32)],
        sem=("parallel", "arbitrary"), args=(a, b), hosted=hosted)
    return out, got


PIECE_W = (256,) * 9 + (512, 128, 128, GATE_W)


def _inproj_fwd(h, g, w_t, b_in, name):
    t = h.shape[0]
    tm, nc = 512, 512

    def body(h_ref, g_ref, w_ref, b_ref, u_ref, zq_ref, zg_ref):
        xhat, _ = _rms(h_ref[...], g_ref[...])
        u = (xhat * g_ref[...]).astype(BF16)
        u_ref[...] = u
        for c in range(D_IN // nc):
            z = _nt(u, w_ref[c * nc:(c + 1) * nc, :]) + b_ref[:, c * nc:(c + 1) * nc]
            if c < QKV_W // nc:
                zq_ref[:, c * nc:(c + 1) * nc] = z.astype(BF16)
            else:
                zg_ref[:, c * nc - QKV_W:(c + 1) * nc - QKV_W] = z

    row = lambda w: pl.BlockSpec((tm, w), lambda i: (i, 0))
    full = _resident
    return pl.pallas_call(
        body, name=name, grid=(t // tm,),
        in_specs=[row(D_MODEL), full(g), full(w_t), full(b_in)],
        out_specs=[row(D_MODEL), row(QKV_W), row(GATE_W)],
        out_shape=[jax.ShapeDtypeStruct((t, D_MODEL), BF16), jax.ShapeDtypeStruct((t, QKV_W), BF16),
                   jax.ShapeDtypeStruct((t, GATE_W), F32)],
        compiler_params=_params("parallel"),
    )(h, g, w_t, b_in)


def _inproj_bwd(pieces, h, g, dh_res, w_t, name):
    t = h.shape[0]
    tm = 256
    npiece = len(PIECE_W)
    offs = np.concatenate([[0], np.cumsum(PIECE_W)]).tolist()

    def body(*refs):
        p_refs = refs[:npiece]
        h_ref, g_ref, dhr_ref, w_ref, dh_ref, dz_ref, db_ref, dg_ref = refs[npiece:]
        i = pl.program_id(0)

        @pl.when(i == 0)
        def _():
            db_ref[...] = jnp.zeros_like(db_ref)
            dg_ref[...] = jnp.zeros_like(dg_ref)

        du = jnp.zeros((tm, D_MODEL), F32)
        for k in range(npiece):
            o, w = offs[k], PIECE_W[k]
            for c0 in range(0, w, 512):
                cw = min(512, w - c0)
                pz = p_refs[k][:, c0:c0 + cw]
                dz_ref[:, o + c0:o + c0 + cw] = pz
                db_ref[:, o + c0:o + c0 + cw] += jnp.sum(pz.astype(F32), axis=0, keepdims=True)
                du = du + _nn(pz, w_ref[o + c0:o + c0 + cw, :])
        xhat, r = _rms(h_ref[...], g_ref[...])
        dx, dg = _rms_bwd(du, xhat, r, g_ref[...])
        dh_ref[...] = dhr_ref[...] + dx
        dg_ref[...] += dg

    row = lambda w: pl.BlockSpec((tm, w), lambda i: (i, 0))
    full = lambda shp: pl.BlockSpec(shp, lambda i: (0, 0))
    return pl.pallas_call(
        body, name=name, grid=(t // tm,),
        in_specs=[row(w) for w in PIECE_W] + [row(D_MODEL), _resident(g), row(D_MODEL), _resident(w_t)],
        out_specs=[row(D_MODEL), row(D_IN), full((1, D_IN)), full((1, D_MODEL))],
        out_shape=[jax.ShapeDtypeStruct((t, D_MODEL), F32), jax.ShapeDtypeStruct((t, D_IN), BF16),
                   jax.ShapeDtypeStruct((1, D_IN), F32), jax.ShapeDtypeStruct((1, D_MODEL), F32)],
        compiler_params=_params("arbitrary"),
    )(*pieces, h, g, dh_res, w_t)


def _t5_bucket(dist):
    max_exact = N_BUCKETS // 2
    n = jnp.maximum(dist, 0)
    nf = jnp.maximum(n, 1).astype(jnp.float32)
    large = max_exact + (jnp.log(nf / max_exact) / math.log(MAX_DISTANCE / max_exact)
                         * (N_BUCKETS - max_exact)).astype(jnp.int32)
    large = jnp.minimum(large, N_BUCKETS - 1)
    return jnp.where(n < max_exact, n, large)


ATT_CFG = ((1, 128, 0, 4), (4, 128, 4, 4), (16, 128, 8, 4), (1, B_WINDOW - 1, A_HEADS, 8))


def _bucket_tiles():
    qi = jnp.arange(BLOCK)[:, None]
    ki = jnp.arange(2 * BLOCK)[None, :]
    dist = qi + BLOCK - ki
    return jnp.stack([_t5_bucket(dist * cfg[0]) for cfg in ATT_CFG]).astype(jnp.int32)


def _band(max_steps):
    row = lax.broadcasted_iota(jnp.int32, (BLOCK, 2 * BLOCK), 0)
    col = lax.broadcasted_iota(jnp.int32, (BLOCK, 2 * BLOCK), 1)
    dist = row + BLOCK - col
    return (dist >= 0) & (dist <= max_steps)


def _bias_build(table, buckets):
    def body(tab_ref, bt_ref, out_ref):
        col = lax.broadcasted_iota(jnp.int32, (BLOCK, 2 * BLOCK), 1)
        for ci, (_, max_steps, h0, nh) in enumerate(ATT_CFG):
            bt = bt_ref[ci]
            band = _band(max_steps)
            for h in range(h0, h0 + nh):
                acc = lax.fori_loop(0, N_BUCKETS, lambda b, acc: jnp.where(bt == b, tab_ref[b, h], acc),
                                    jnp.zeros((BLOCK, 2 * BLOCK), F32))
                out_ref[0, BLOCK * h:BLOCK * (h + 1), :] = jnp.where(band & (col >= BLOCK), acc, NEG)
                out_ref[1, BLOCK * h:BLOCK * (h + 1), :] = jnp.where(band, acc, NEG)

    return pl.pallas_call(
        body, name="bias_build",
        in_specs=[pl.BlockSpec(memory_space=pltpu.SMEM), pl.BlockSpec(memory_space=pltpu.VMEM)],
        out_specs=pl.BlockSpec(memory_space=pltpu.VMEM),
        out_shape=jax.ShapeDtypeStruct((2, TOTAL_HEADS * BLOCK, 2 * BLOCK), F32),
    )(table, buckets)


def _bias_reduce(dbias, buckets, dsink_rows):
    def body(db_ref, bt_ref, ds_ref, out_ref, sink_ref):
        ri = lax.broadcasted_iota(jnp.int32, (N_BUCKETS, 128), 0)
        ci = lax.broadcasted_iota(jnp.int32, (N_BUCKETS, 128), 1)

        def per_bucket(b, acc):
            for cfg_i, (_, _, h0, nh) in enumerate(ATT_CFG):
                hit = bt_ref[cfg_i] == b
                for h in range(h0, h0 + nh):
                    val = jnp.sum(jnp.where(hit, db_ref[BLOCK * h:BLOCK * (h + 1), :], 0.0))
                    acc = jnp.where((ri == b) & (ci == h), val, acc)
            return acc

        out_ref[...] = lax.fori_loop(0, N_BUCKETS, per_bucket, jnp.zeros((N_BUCKETS, 128), F32))
        for h in range(8):
            sink_ref[h:h + 1, :] = jnp.sum(ds_ref[BLOCK * h:BLOCK * (h + 1), :], axis=0, keepdims=True)

    return pl.pallas_call(
        body, name="bias_reduce",
        in_specs=[pl.BlockSpec(memory_space=pltpu.VMEM)] * 3,
        out_specs=[pl.BlockSpec(memory_space=pltpu.VMEM)] * 2,
        out_shape=[jax.ShapeDtypeStruct((N_BUCKETS, 128), F32), jax.ShapeDtypeStruct((8, 128), F32)],
    )(dbias, buckets, dsink_rows)


class _Att:
    def __init__(self, cfg_i):
        stride, _, h0, nh = ATT_CFG[cfg_i]
        self.d = stride if cfg_i < 3 else 1
        self.h0, self.nh = h0, nh
        self.row_w = QKV_W if self.d == 1 else 3 * 256
        if cfg_i < 3:
            self.nq, self.wkv = 1, 256
            self.q_unit = [cfg_i if self.d == 1 else 0]
            self.k_unit, self.v_unit = (3 + cfg_i, 6 + cfg_i) if self.d == 1 else (1, 2)
            self.sinks = False
        else:
            self.nq, self.wkv = 2, 128
            self.q_unit = [9, 10]
            self.k_unit, self.v_unit = 22, 23
            self.sinks = True
        self.wq = 256 * self.nq


def _att_in_specs(cf, bsz):
    uq, ukv = cf.row_w // 256, cf.row_w // cf.wkv
    specs = [pl.BlockSpec((bsz, BLOCK, 256), functools.partial(lambda r, j, u: (0, j, r * uq + u), u=u))
             for u in cf.q_unit]
    for unit in (cf.k_unit, cf.v_unit):
        specs.append(pl.BlockSpec((bsz, BLOCK, cf.wkv),
                                  functools.partial(lambda r, j, u: (0, jnp.maximum(j - 1, 0), r * ukv + u), u=unit)))
        specs.append(pl.BlockSpec((bsz, BLOCK, cf.wkv),
                                  functools.partial(lambda r, j, u: (0, j, r * ukv + u), u=unit)))
    for qb in range(cf.nq):
        specs.append(pl.BlockSpec((None, HEADS_PER_BLOCK * BLOCK, 2 * BLOCK), functools.partial(
            lambda r, j, u: (jnp.minimum(j, 1), u, 0), u=cf.h0 // HEADS_PER_BLOCK + qb)))
    if cf.sinks:
        specs += [pl.BlockSpec((HEADS_PER_BLOCK * BLOCK, 128), functools.partial(lambda r, j, u: (u, 0), u=qb))
                  for qb in range(cf.nq)]
    return specs


HEADS_PER_BLOCK = 4


def _head_masks(rows):
    head = lax.broadcasted_iota(jnp.int32, (rows, 256), 1) // HEAD_DIM
    return [head == h for h in range(HEADS_PER_BLOCK)]


def _stack_heads(x, masks):
    return jnp.concatenate([jnp.where(m, x, jnp.zeros_like(x)) for m in masks], axis=0)


def _unstack_heads(x4, masks):
    blocks = [x4[BLOCK * h:BLOCK * (h + 1)] for h in range(HEADS_PER_BLOCK)]
    return jnp.where(masks[0], blocks[0], jnp.where(masks[1], blocks[1], jnp.where(masks[2], blocks[2], blocks[3])))


def _row_value(x):
    return jnp.max(x, axis=-1, keepdims=True)


def _kv_operand(cf, x, qb):
    if cf.wkv == 256:
        return x
    lane = lax.broadcasted_iota(jnp.int32, x.shape, 1)
    swapped = pltpu.roll(x, HEAD_DIM, 1)
    half = jnp.where(lane < HEAD_DIM, x, swapped) if qb == 0 else jnp.where(lane < HEAD_DIM, swapped, x)
    return jnp.concatenate([half, half], axis=1)


def _kv_fold(cf, grads):
    if cf.wkv == 256:
        return grads[0]
    folded = []
    for g in grads:
        x = g[:, :128] + g[:, 128:]
        folded.append(x + pltpu.roll(x, HEAD_DIM, 1))
    lane = lax.broadcasted_iota(jnp.int32, folded[0].shape, 1)
    return jnp.where(lane < HEAD_DIM, folded[0], folded[1])


def _attn_fwd(cf, zf, bias, sinks, name, hosted=()):
    bsz, l, _ = zf.shape
    nb = l // BLOCK

    def body(*refs):
        refs = list(refs)
        q_refs = [refs.pop(0) for _ in range(cf.nq)]
        kp_ref, kc_ref, vp_ref, vc_ref = [refs.pop(0) for _ in range(4)]
        bias_refs = [refs.pop(0) for _ in range(cf.nq)]
        sink_refs = [refs.pop(0) for _ in range(cf.nq)] if cf.sinks else None
        o_ref, lse_ref = refs
        masks, kv_masks = _head_masks(BLOCK), _head_masks(2 * BLOCK)
        sinks4 = [_row_value(ref[...]) for ref in sink_refs] if cf.sinks else None
        for bi in range(bsz):
            k = jnp.concatenate([kp_ref[bi], kc_ref[bi]], axis=0)
            v = jnp.concatenate([vp_ref[bi], vc_ref[bi]], axis=0)
            for qb in range(cf.nq):
                cols = slice(256 * qb, 256 * (qb + 1))
                kb, vb = _kv_operand(cf, k, qb), _kv_operand(cf, v, qb)
                q4 = _stack_heads(q_refs[qb][bi] * (HEAD_DIM ** -0.5), masks)
                s = _nt(q4, kb) + bias_refs[qb][...]
                m = jnp.max(s, axis=-1, keepdims=True)
                if cf.sinks:
                    sk = sinks4[qb]
                    m = jnp.maximum(m, sk)
                p = jnp.exp(s - m)
                den = jnp.sum(p, axis=-1, keepdims=True)
                if cf.sinks:
                    den = den + jnp.exp(sk - m)
                pn = (p * (1.0 / den)).astype(BF16)
                p_lanes = jnp.concatenate([pn[BLOCK * h:BLOCK * (h + 1)] for h in range(HEADS_PER_BLOCK)], axis=1)
                v4 = jnp.concatenate([jnp.where(mk, vb, jnp.zeros_like(vb)) for mk in kv_masks], axis=0)
                o_ref[bi, :, cols] = _nn(p_lanes, v4)
                lse_ref[bi, :, cols] = _unstack_heads(
                    jnp.broadcast_to(m + jnp.log(den), (HEADS_PER_BLOCK * BLOCK, 256)), masks)

    in_specs = _att_in_specs(cf, bsz)
    args = [zf] * (cf.nq + 4) + [bias] * cf.nq + ([sinks] * cf.nq if cf.sinks else [])
    out = pl.BlockSpec((bsz, BLOCK, cf.wq), lambda r, j: (0, j, r))
    shape = jax.ShapeDtypeStruct((bsz, l, cf.d * cf.wq), F32)
    return _call(
        body, name=name, grid=(cf.d, nb), in_specs=in_specs, out_specs=[out, out], out_shape=[shape, shape],
        sem=("parallel", "arbitrary"), args=args, hosted=hosted)


def _attn_bwd(cf, zf, bias, sinks, lse_tot, dy, y, name, hosted=()):
    bsz, l, _ = zf.shape
    nb = l // BLOCK

    def body(*refs):
        refs = list(refs)
        q_refs = [refs.pop(0) for _ in range(cf.nq)]
        kp_ref, kc_ref, vp_ref, vc_ref = [refs.pop(0) for _ in range(4)]
        bias_refs = [refs.pop(0) for _ in range(cf.nq)]
        sink_refs = [refs.pop(0) for _ in range(cf.nq)] if cf.sinks else None
        lt_ref, dy_ref, y_ref, dq_ref, dk_ref, dv_ref, dbias_ref = [refs.pop(0) for _ in range(7)]
        dsink_ref = refs.pop(0) if cf.sinks else None
        dk_acc, dv_acc = refs
        r, j = pl.program_id(0), pl.program_id(1)

        @pl.when((r == 0) & (j == 0))
        def _():
            dbias_ref[...] = jnp.zeros_like(dbias_ref)
            if cf.sinks:
                dsink_ref[...] = jnp.zeros_like(dsink_ref)

        @pl.when(j == 0)
        def _():
            dk_acc[...] = jnp.zeros_like(dk_acc)
            dv_acc[...] = jnp.zeros_like(dv_acc)

        masks = _head_masks(BLOCK)
        cur = pl.ds(pl.multiple_of(j * BLOCK, BLOCK), BLOCK)
        prev = pl.ds(pl.multiple_of(jnp.maximum(j - 1, 0) * BLOCK, BLOCK), BLOCK)
        sinks4 = [_row_value(ref[...]) for ref in sink_refs] if cf.sinks else None
        for bi in range(bsz):
            k = jnp.concatenate([kp_ref[bi], kc_ref[bi]], axis=0)
            v = jnp.concatenate([vp_ref[bi], vc_ref[bi]], axis=0)
            dk_blocks, dv_blocks = [], []
            for qb in range(cf.nq):
                cols = slice(256 * qb, 256 * (qb + 1))
                rows = slice(HEADS_PER_BLOCK * BLOCK * qb, HEADS_PER_BLOCK * BLOCK * (qb + 1))
                kb, vb = _kv_operand(cf, k, qb), _kv_operand(cf, v, qb)
                q4 = _stack_heads(q_refs[qb][bi] * (HEAD_DIM ** -0.5), masks)
                lt = lt_ref[bi, :, cols]
                lt4 = jnp.concatenate([_row_value(jnp.where(mk, lt, NEG)) for mk in masks], axis=0)
                pa = jnp.exp(_nt(q4, kb) + bias_refs[qb][...] - lt4)
                dy = dy_ref[bi, :, cols]
                prod = dy * y_ref[bi, :, cols]
                e4 = jnp.concatenate([jnp.sum(jnp.where(mk, prod, 0.0), axis=-1, keepdims=True) for mk in masks],
                                     axis=0)
                dy4 = _stack_heads(dy.astype(BF16), masks)
                ds = pa * (_nt(dy4, vb) - e4)
                dbias_ref[rows, :] += ds
                if cf.sinks:
                    dsink_ref[rows, :] += -(jnp.exp(sinks4[qb] - lt4) * e4)
                dsb = ds.astype(BF16)
                dq_ref[bi, :, cols] = (_unstack_heads(_nn(dsb, kb), masks) * (HEAD_DIM ** -0.5)).astype(BF16)
                dk_blocks.append(_tn(dsb, q4))
                dv_blocks.append(_tn(pa.astype(BF16), dy4))
            dk_new, dv_new = _kv_fold(cf, dk_blocks), _kv_fold(cf, dv_blocks)
            dk_acc[bi, cur, :] += dk_new[BLOCK:]
            dv_acc[bi, cur, :] += dv_new[BLOCK:]
            dk_acc[bi, prev, :] += dk_new[:BLOCK]
            dv_acc[bi, prev, :] += dv_new[:BLOCK]

        @pl.when(j == nb - 1)
        def _():
            dk_ref[...] = dk_acc[...].astype(BF16)
            dv_ref[...] = dv_acc[...].astype(BF16)

    tok = pl.BlockSpec((bsz, BLOCK, cf.wq), lambda r, j: (0, j, r))
    in_specs = _att_in_specs(cf, bsz) + [tok] * 3
    args = [zf] * (cf.nq + 4) + [bias] * cf.nq + ([sinks] * cf.nq if cf.sinks else []) + [lse_tot, dy, y]
    seq = pl.BlockSpec((bsz, l, cf.wkv), lambda r, j: (0, 0, r))
    out_specs = [tok, seq, seq, pl.BlockSpec((cf.nh * BLOCK, 2 * BLOCK), lambda r, j: (0, 0))]
    out_shape = [jax.ShapeDtypeStruct((bsz, l, cf.d * cf.wq), BF16),
                 jax.ShapeDtypeStruct((bsz, l, cf.d * cf.wkv), BF16),
                 jax.ShapeDtypeStruct((bsz, l, cf.d * cf.wkv), BF16),
                 jax.ShapeDtypeStruct((cf.nh * BLOCK, 2 * BLOCK), F32)]
    if cf.sinks:
        out_specs.append(pl.BlockSpec((cf.nh * BLOCK, 128), lambda r, j: (0, 0)))
        out_shape.append(jax.ShapeDtypeStruct((cf.nh * BLOCK, 128), F32))
    return _call(
        body, name=name, grid=(cf.d, nb), in_specs=in_specs, out_specs=out_specs, out_shape=out_shape,
        scratch=[pltpu.VMEM((bsz, l, cf.wkv), F32), pltpu.VMEM((bsz, l, cf.wkv), F32)],
        sem=("arbitrary", "arbitrary"), args=args, hosted=hosted)


def _merge_fwd(o_a, lse_a, o_b, zg, h, wa_t, wb_t, wout, name):
    t = h.shape[0]
    tm = 512

    def body(o1, o2, o3, l1, l2, l3, ob_ref, zg_ref, h_ref, wa_ref, wb_ref, wo_ref, h2_ref, ya_ref, lt_ref):
        m = jnp.maximum(jnp.maximum(l1[...], l2[...]), l3[...])
        e1, e2, e3 = jnp.exp(l1[...] - m), jnp.exp(l2[...] - m), jnp.exp(l3[...] - m)
        se = e1 + e2 + e3
        ya = (e1 / se) * o1[...] + (e2 / se) * o2[...] + (e3 / se) * o3[...]
        ya_ref[...] = ya
        lt_ref[...] = m + jnp.log(se)
        pa = _nt(ya.astype(BF16), wa_ref[...])
        pb = _nt(ob_ref[...].astype(BF16), wb_ref[...])
        merged = jax.nn.sigmoid(zg_ref[:, :D_MODEL]) * pa + jax.nn.sigmoid(zg_ref[:, D_MODEL:]) * pb
        h2_ref[...] = h_ref[...] + _nn(merged.astype(BF16), wo_ref[...])

    row = lambda w: pl.BlockSpec((tm, w), lambda i: (i, 0))
    full = _resident
    return pl.pallas_call(
        body, name=name, grid=(t // tm,),
        in_specs=[row(256)] * 6 + [row(512), row(GATE_W), row(D_MODEL), full(wa_t), full(wb_t), full(wout)],
        out_specs=[row(D_MODEL), row(256), row(256)],
        out_shape=[jax.ShapeDtypeStruct((t, D_MODEL), F32), jax.ShapeDtypeStruct((t, 256), F32),
                   jax.ShapeDtypeStruct((t, 256), F32)],
        compiler_params=_params("parallel"),
    )(*o_a, *lse_a, o_b, zg, h, wa_t, wb_t, wout)


def _merge_bwd(dh, ya, o_b, zg, wa_t, wb_t, wout, name, hosted=()):
    t = dh.shape[0]
    tm = 256

    def body(dh_ref, ya_ref, ob_ref, zg_ref, wa_ref, wb_ref, wo_ref,
             mg_ref, dpa_ref, dpb_ref, dzg_ref, dya_ref, dyb_ref):
        dm = _nt(dh_ref[...].astype(BF16), wo_ref[...])
        pa = _nt(ya_ref[...].astype(BF16), wa_ref[...])
        pb = _nt(ob_ref[...].astype(BF16), wb_ref[...])
        sa = jax.nn.sigmoid(zg_ref[:, :D_MODEL])
        sb = jax.nn.sigmoid(zg_ref[:, D_MODEL:])
        mg_ref[...] = (sa * pa + sb * pb).astype(BF16)
        dpa = (dm * sa).astype(BF16)
        dpb = (dm * sb).astype(BF16)
        dpa_ref[...] = dpa
        dpb_ref[...] = dpb
        dzg_ref[:, :D_MODEL] = (dm * pa * (sa * (1.0 - sa))).astype(BF16)
        dzg_ref[:, D_MODEL:] = (dm * pb * (sb * (1.0 - sb))).astype(BF16)
        dya_ref[...] = _nn(dpa, wa_ref[...])
        dyb_ref[...] = _nn(dpb, wb_ref[...])

    row = lambda w: pl.BlockSpec((tm, w), lambda i: (i, 0))
    full = _resident
    sds = jax.ShapeDtypeStruct
    return _call(
        body, name=name, grid=(t // tm,),
        in_specs=[row(D_MODEL), row(256), row(512), row(GATE_W), full(wa_t), full(wb_t), full(wout)],
        out_specs=[row(D_MODEL), row(D_MODEL), row(D_MODEL), row(GATE_W), row(256), row(512)],
        out_shape=[sds((t, D_MODEL), BF16), sds((t, D_MODEL), BF16), sds((t, D_MODEL), BF16),
                   sds((t, GATE_W), BF16), sds((t, 256), F32), sds((t, 512), F32)],
        sem=("parallel",), args=(dh, ya, o_b, zg, wa_t, wb_t, wout), hosted=hosted)


def _loss_head(h, g, target, name):
    t = h.shape[0]
    tm = 1024

    def body(h_ref, g_ref, t_ref, dh_ref, loss_ref, dg_ref):
        @pl.when(pl.program_id(0) == 0)
        def _():
            loss_ref[...] = jnp.zeros_like(loss_ref)
            dg_ref[...] = jnp.zeros_like(dg_ref)

        xhat, r = _rms(h_ref[...], g_ref[...])
        err = xhat * g_ref[...] - t_ref[...]
        loss_ref[...] += 0.5 * jnp.sum(jnp.mean(err * err, axis=-1, keepdims=True), axis=0, keepdims=True)
        dx, dg = _rms_bwd(err * (1.0 / D_MODEL), xhat, r, g_ref[...])
        dh_ref[...] = dx
        dg_ref[...] += dg

    row = pl.BlockSpec((tm, D_MODEL), lambda i: (i, 0))
    vec = pl.BlockSpec((1, D_MODEL), lambda i: (0, 0))
    return pl.pallas_call(
        body, name=name, grid=(t // tm,),
        in_specs=[row, vec, row], out_specs=[row, pl.BlockSpec((1, 128), lambda i: (0, 0)), vec],
        out_shape=[jax.ShapeDtypeStruct((t, D_MODEL), F32), jax.ShapeDtypeStruct((1, 128), F32),
                   jax.ShapeDtypeStruct((1, D_MODEL), F32)],
        compiler_params=_params("arbitrary"),
    )(h, g, target)


def _pair_sum(grad, got, name):
    _, _, r, cdim = grad.shape
    core = lax.axis_index("c").astype(jnp.int32).reshape(1)

    def body(core_ref, g_ref, s_ref, o_ref):
        o_ref[...] = (g_ref[...] + s_ref[...]).astype(BF16)

    return pl.pallas_call(
        body, name=name,
        grid_spec=pltpu.PrefetchScalarGridSpec(
            num_scalar_prefetch=1, grid=(N_CHIP,),
            in_specs=[pl.BlockSpec((None, None, r, cdim), lambda q, core_ref: (q, core_ref[0], 0, 0)),
                      pl.BlockSpec((None, None, r, cdim), lambda q, core_ref: (q, 0, 0, 0))],
            out_specs=pl.BlockSpec((None, r, cdim), lambda q, core_ref: (q, 0, 0))),
        out_shape=jax.ShapeDtypeStruct((N_CHIP, r, cdim), BF16),
        compiler_params=_params("parallel"),
    )(core, grad, got)


def _chip_sum(parts, name):
    def body(p_ref, o_ref):
        o_ref[...] = ((p_ref[0].astype(F32) + p_ref[1].astype(F32)) + p_ref[2].astype(F32)) + p_ref[3].astype(F32)

    return pl.pallas_call(
        body, name=name, out_shape=jax.ShapeDtypeStruct(parts.shape[1:], F32),
        compiler_params=pltpu.CompilerParams(vmem_limit_bytes=VMEM_LIMIT),
    )(parts)


def _adamw_math(w, g, m, v):
    m = ADAM_B1 * m + (1.0 - ADAM_B1) * g
    v = ADAM_B2 * v + (1.0 - ADAM_B2) * jnp.square(g)
    m_hat = m / (1.0 - ADAM_B1 ** ADAM_STEP)
    v_hat = v / (1.0 - ADAM_B2 ** ADAM_STEP)
    delta = -ADAM_LR * (m_hat / (jnp.sqrt(v_hat) + ADAM_EPS) + ADAM_WD * w)
    return delta, m, v


def _adamw(w, g, m, v, name):
    def body(w_ref, g_ref, m_ref, v_ref, d_ref, m2_ref, v2_ref):
        d_ref[...], m2_ref[...], v2_ref[...] = _adamw_math(w_ref[...], g_ref[...], m_ref[...], v_ref[...])

    sds = jax.ShapeDtypeStruct(w.shape, F32)
    return pl.pallas_call(body, name=name, out_shape=[sds, sds, sds],
                          compiler_params=pltpu.CompilerParams(vmem_limit_bytes=VMEM_LIMIT))(w, g, m, v)


SMALL_ROWS = 80


def _small_update(g, w, m, v, name):
    def body(g_ref, w_ref, m_ref, v_ref, gs_ref, d_ref, m2_ref, v2_ref, got_ref, send_sems, recv_sems):
        x, y, c = _place()
        me = 4 * x + 2 * y + c
        got_ref[me] = g_ref[...]
        copies = []
        for k in range(1, N_DEV):
            peer = (x ^ (k >> 2), y ^ ((k >> 1) & 1), c ^ (k & 1))
            cp = pltpu.make_async_remote_copy(
                src_ref=g_ref, dst_ref=got_ref.at[me], send_sem=send_sems.at[k - 1], recv_sem=recv_sems.at[k - 1],
                device_id=peer, device_id_type=MESH)
            cp.start()
            copies.append(cp)
        for cp in copies:
            cp.wait()
        total = got_ref[0]
        for k in range(1, N_DEV):
            total = total + got_ref[k]
        gs_ref[...] = total
        d_ref[...], m2_ref[...], v2_ref[...] = _adamw_math(w_ref[...], total, m_ref[...], v_ref[...])

    sds = jax.ShapeDtypeStruct((SMALL_ROWS, 128), F32)
    vm = pl.BlockSpec(memory_space=pltpu.VMEM)
    return pl.pallas_call(
        body, name=name, in_specs=[vm] * 4, out_specs=[vm] * 4, out_shape=[sds] * 4,
        scratch_shapes=[pltpu.VMEM((N_DEV, SMALL_ROWS, 128), F32), pltpu.SemaphoreType.DMA((N_DEV - 1,)),
                        pltpu.SemaphoreType.DMA((N_DEV - 1,))],
    )(g, w, m, v)


def _pack_small(gains, b_in, rel_bias, sinks, last):
    rows = [a.reshape(8, 128) for a in gains] + [b_in.reshape(40, 128), rel_bias.reshape(5, 128),
                                                 jnp.pad(sinks.reshape(1, 8), ((0, 0), (0, 120))), last]
    rows.append(jnp.zeros((SMALL_ROWS - 79, 128), F32))
    return jnp.concatenate(rows, axis=0)


def _unpack_small(p, like):
    out = [p[8 * i:8 * i + 8].reshape(like[i].shape) for i in range(4)]
    out.append(p[32:72].reshape(like[4].shape))
    out.append(p[72:77].reshape(like[5].shape))
    out.append(p[77, :8].reshape(like[6].shape))
    return out


def kernel(x, ffn1_norm, ffn1_w_gate, ffn1_w_up, ffn1_w_down, mix_norm, w_in, b_in, w_branch_a, w_branch_b, w_out, sinks, rel_bias, ffn2_norm, ffn2_w_gate, ffn2_w_up, ffn2_w_down, final_norm, loss_target, m_ffn1_norm, m_ffn1_w_gate, m_ffn1_w_up, m_ffn1_w_down, m_mix_norm, m_w_in, m_b_in, m_w_branch_a, m_w_branch_b, m_w_out, m_sinks, m_rel_bias, m_ffn2_norm, m_ffn2_w_gate, m_ffn2_w_up, m_ffn2_w_down, m_final_norm, v_ffn1_norm, v_ffn1_w_gate, v_ffn1_w_up, v_ffn1_w_down, v_mix_norm, v_w_in, v_b_in, v_w_branch_a, v_w_branch_b, v_w_out, v_sinks, v_rel_bias, v_ffn2_norm, v_ffn2_w_gate, v_ffn2_w_up, v_ffn2_w_down, v_final_norm):
    bsz, seq, _ = x.shape
    t = bsz * seq
    xt = x.reshape(t, D_MODEL)
    target = loss_target.reshape(t, D_MODEL)

    big = [("ffn1_w_gate", ffn1_w_gate, m_ffn1_w_gate, v_ffn1_w_gate, True),
           ("ffn1_w_up", ffn1_w_up, m_ffn1_w_up, v_ffn1_w_up, True),
           ("ffn1_w_down", ffn1_w_down, m_ffn1_w_down, v_ffn1_w_down, False),
           ("w_in", w_in, m_w_in, v_w_in, True),
           ("w_branch_a", w_branch_a, m_w_branch_a, v_w_branch_a, True),
           ("w_branch_b", w_branch_b, m_w_branch_b, v_w_branch_b, True),
           ("w_out", w_out, m_w_out, v_w_out, False),
           ("ffn2_w_gate", ffn2_w_gate, m_ffn2_w_gate, v_ffn2_w_gate, True),
           ("ffn2_w_up", ffn2_w_up, m_ffn2_w_up, v_ffn2_w_up, True),
           ("ffn2_w_down", ffn2_w_down, m_ffn2_w_down, v_ffn2_w_down, False)]
    shard = {nm: (w[0].T if tr else w[0]).astype(BF16) for nm, w, _, _, tr in big}
    full = {}

    def gather(names):
        return _Gather([shard[nm] for nm in names])

    def keep(names, got):
        for nm, gw in zip(names, got):
            full[nm] = gw.reshape(-1, gw.shape[-1])

    ffn1_names = ["ffn1_w_gate", "ffn1_w_up", "ffn1_w_down"]
    mix_names = ["w_in", "w_branch_a", "w_branch_b", "w_out"]
    ffn2_names = ["ffn2_w_gate", "ffn2_w_up", "ffn2_w_down"]
    g1, gm, g2, gf = ffn1_norm, mix_norm, ffn2_norm, final_norm.reshape(1, D_MODEL)

    keep(ffn1_names, _exchange([gather(ffn1_names)], "gather_ffn1")[0])
    (h1, n1), (got,) = _ffn_fwd(xt, g1, full["ffn1_w_gate"], full["ffn1_w_up"], full["ffn1_w_down"], "ffn1_fwd",
                                hosted=[gather(mix_names)])
    keep(mix_names, got)
    u, zq, zg = _inproj_fwd(h1, gm, full["w_in"], b_in, "inproj_fwd")
    buckets = _bucket_tiles()
    bias = _bias_build(rel_bias, buckets)
    sink_rows = jnp.broadcast_to(sinks.reshape(8, 1, 1), (8, BLOCK, 128)).reshape(8 * BLOCK, 128)
    cfs = [_Att(i) for i in range(4)]
    zfold = []
    for i, cf in enumerate(cfs):
        if cf.d == 1:
            zfold.append(zq.reshape(bsz, seq, QKV_W))
        else:
            own = jnp.concatenate([zq[:, 768 * part + 256 * i:768 * part + 256 * (i + 1)] for part in range(3)], axis=1)
            zfold.append(own.reshape(bsz, seq // cf.d, cf.d * cf.row_w))
    att = [None] * 4
    for i in (3, 0, 1, 2):
        cf = cfs[i]
        (o, lse), got = _attn_fwd(cf, zfold[i], bias, sink_rows, f"attn{i}_fwd",
                                  hosted=[gather(ffn2_names)] if i == 3 else ())
        if i == 3:
            keep(ffn2_names, got[0])
        att[i] = (o.reshape(t, cf.wq), lse.reshape(t, cf.wq))
    o_b, lse_b = att[3]
    h2, ya, lse_tot = _merge_fwd([a[0] for a in att[:3]], [a[1] for a in att[:3]], o_b, zg, h1,
                                 full["w_branch_a"], full["w_branch_b"], full["w_out"], "merge_fwd")
    (h3, n2), _ = _ffn_fwd(h2, g2, full["ffn2_w_gate"], full["ffn2_w_up"], full["ffn2_w_down"], "ffn2_fwd")
    dh3, loss_part, dgf = _loss_head(h3, gf, target, "loss_head")

    grads, pair, from_chips = {}, {}, {}

    def by_owner(nm):
        return grads[nm].reshape(N_CHIP, 2, -1, grads[nm].shape[-1])

    def to_core(names):
        return _CoreExchange([by_owner(nm) for nm in names])

    def pair_up(names, got):
        for nm, sib in zip(names, got):
            pair[nm] = _pair_sum(by_owner(nm), sib, f"pair_sum_{nm}")

    def to_chips(names):
        return _ChipExchange([pair[nm] for nm in names])

    def landed(names, got):
        for nm, parts in zip(names, got):
            from_chips[nm] = parts

    dh2, dg2, hff, da, db, dhh = _ffn_bwd(h2, n2, g2, dh3, full["ffn2_w_gate"], full["ffn2_w_up"],
                                          full["ffn2_w_down"], "ffn2_bwd")
    grads["ffn2_w_gate"], _ = _tn_matmul(da, n2, 1408, "ffn2_dgate")
    grads["ffn2_w_up"], _ = _tn_matmul(db, n2, 1408, "ffn2_dup")
    grads["ffn2_w_down"], _ = _tn_matmul(hff, dhh, 1408, "ffn2_ddown")
    (merged, dpa, dpb, dzg, dya, dyb), (got,) = _merge_bwd(
        dh2, ya, o_b, zg, full["w_branch_a"], full["w_branch_b"], full["w_out"], "merge_bwd",
        hosted=[to_core(ffn2_names)])
    pair_up(ffn2_names, got)
    grads["w_out"], _ = _tn_matmul(merged, dh2, 1024, "dw_out")
    grads["w_branch_a"], _ = _tn_matmul(dpa, ya, 1024, "dw_branch_a")
    grads["w_branch_b"], _ = _tn_matmul(dpb, o_b, 1024, "dw_branch_b")
    out_names = ["w_out", "w_branch_a", "w_branch_b"]
    dq, dk, dv, dbias, dsink = [None] * 4, [None] * 4, [None] * 4, [None] * 4, None
    for i in (3, 0, 1, 2):
        cf = cfs[i]
        lt_i, dy_i, y_i = (lse_tot, dya, ya) if i < 3 else (lse_b, dyb, o_b)
        shp = (bsz, seq // cf.d, cf.d * cf.wq)
        hosted = {3: lambda: [to_chips(ffn2_names)], 0: lambda: [to_core(out_names)],
                  1: lambda: [to_chips(out_names)], 2: lambda: []}[i]()
        res, got = _attn_bwd(cf, zfold[i], bias, sink_rows, lt_i.reshape(shp), dy_i.reshape(shp), y_i.reshape(shp),
                             f"attn{i}_bwd", hosted=hosted)
        if i == 3:
            landed(ffn2_names, got[0])
        elif i == 0:
            pair_up(out_names, got[0])
        elif i == 1:
            landed(out_names, got[0])
        dq[i] = res[0].reshape(t, cf.wq)
        dk[i] = res[1].reshape(t, cf.wkv)
        dv[i] = res[2].reshape(t, cf.wkv)
        dbias[i] = res[3]
        if cf.sinks:
            dsink = res[4]
    pieces = dq[:3] + dk[:3] + dv[:3] + [dq[3], dk[3], dv[3], dzg]
    dh1, dz, db_in, dgm = _inproj_bwd(pieces, h1, gm, dh2, full["w_in"], "inproj_bwd")
    dx, dg1, hff, da, db, dhh = _ffn_bwd(xt, n1, g1, dh1, full["ffn1_w_gate"], full["ffn1_w_up"],
                                         full["ffn1_w_down"], "ffn1_bwd")
    grads["w_in"], _ = _tn_matmul(dz, u, 1280, "dw_in")
    grads["ffn1_w_down"], (got,) = _tn_matmul(hff, dhh, 1408, "ffn1_ddown", hosted=[to_core(["w_in"])])
    pair_up(["w_in"], got)
    grads["ffn1_w_gate"], got = _tn_matmul(da, n1, 1408, "ffn1_dgate",
                                           hosted=[to_chips(["w_in"]), to_core(["ffn1_w_down"])])
    landed(["w_in"], got[0])
    pair_up(["ffn1_w_down"], got[1])
    grads["ffn1_w_up"], got = _tn_matmul(db, n1, 1408, "ffn1_dup",
                                         hosted=[to_chips(["ffn1_w_down"]), to_core(["ffn1_w_gate"])])
    landed(["ffn1_w_down"], got[0])
    pair_up(["ffn1_w_gate"], got[1])
    got = _exchange([to_chips(["ffn1_w_gate"]), to_core(["ffn1_w_up"])], "reduce_scatter_tail1")
    landed(["ffn1_w_gate"], got[0])
    pair_up(["ffn1_w_up"], got[1])
    landed(["ffn1_w_up"], _exchange([to_chips(["ffn1_w_up"])], "reduce_scatter_tail2")[0])
    dtable, dsinks = _bias_reduce(jnp.concatenate(dbias, axis=0), buckets, dsink)

    out_g, out_d, out_m, out_v = {}, {}, {}, {}
    for nm, w, m, v, tr in big:
        g = _chip_sum(from_chips[nm], f"chip_sum_{nm}")
        g = (g.T if tr else g).reshape(w.shape)
        out_g[nm] = g
        d2, m2, v2 = _adamw(w[0], g[0], m[0], v[0], f"adamw_{nm}")
        out_d[nm], out_m[nm], out_v[nm] = d2[None], m2[None], v2[None]

    small = [("ffn1_norm", ffn1_norm, m_ffn1_norm, v_ffn1_norm), ("mix_norm", mix_norm, m_mix_norm, v_mix_norm),
             ("ffn2_norm", ffn2_norm, m_ffn2_norm, v_ffn2_norm), ("final_norm", final_norm, m_final_norm, v_final_norm),
             ("b_in", b_in, m_b_in, v_b_in), ("rel_bias", rel_bias, m_rel_bias, v_rel_bias),
             ("sinks", sinks, m_sinks, v_sinks)]
    zero_row = jnp.zeros((1, 128), F32)
    pack = lambda arrs, last: _pack_small(arrs[:4], arrs[4], arrs[5], arrs[6], last)
    g_small = pack([dg1, dgm, dg2, dgf, db_in, dtable[:, :TOTAL_HEADS], dsinks[:, 0]], loss_part)
    packed = [pack([s[k] for s in small], zero_row) for k in (1, 2, 3)]
    gs, ds, ms, vs = _small_update(g_small, *packed, "small_update")
    like = [s[1] for s in small]
    for nm_s, g_, d_, m_, v_ in zip([s[0] for s in small], _unpack_small(gs, like), _unpack_small(ds, like),
                                    _unpack_small(ms, like), _unpack_small(vs, like)):
        out_g[nm_s], out_d[nm_s], out_m[nm_s], out_v[nm_s] = g_, d_, m_, v_
    loss = gs[78, 0]

    order = ["ffn1_norm", "ffn1_w_gate", "ffn1_w_up", "ffn1_w_down", "mix_norm", "w_in", "b_in", "w_branch_a",
             "w_branch_b", "w_out", "sinks", "rel_bias", "ffn2_norm", "ffn2_w_gate", "ffn2_w_up", "ffn2_w_down",
             "final_norm"]
    return (loss, dx.reshape(x.shape), *[out_g[k] for k in order], *[out_d[k] for k in order],
            *[out_m[k] for k in order], *[out_v[k] for k in order])
```

```python
import functools
import math

import numpy as np
import jax
import jax.numpy as jnp
from jax import lax
from jax.experimental import pallas as pl
from jax.experimental.pallas import tpu as pltpu

D_MODEL = 1024
D_FF = 2816
FF_CHUNK = 256
HEAD_DIM = 64
BLOCK = 128
N_BUCKETS = 32
MAX_DISTANCE = 2048
A_HEADS = 12
TOTAL_HEADS = 20
DIL_GROUPS = ((128, 1), (512, 4), (2048, 16))
B_WINDOW = 128
QKV_W = 3072
GATE_W = 2048
D_IN = QKV_W + GATE_W
EPS = 1e-6
NEG = -1e30
N_DEV = 8
N_CHIP = 4
ADAM_LR, ADAM_B1, ADAM_B2, ADAM_EPS, ADAM_WD, ADAM_STEP = 0.001, 0.9, 0.999, 1e-08, 0.01, 10
VMEM_LIMIT = 56 * 1024 * 1024
MESH = pl.DeviceIdType.MESH
BF16 = jnp.bfloat16
F32 = jnp.float32
ANY = pl.BlockSpec(memory_space=pl.ANY)


def _params(*sem):
    return pltpu.CompilerParams(dimension_semantics=sem, vmem_limit_bytes=VMEM_LIMIT)


def _resident(a):
    return pl.BlockSpec(a.shape, lambda i: (0, 0), pipeline_mode=pl.Buffered(1))


def _place():
    return lax.axis_index("x"), lax.axis_index("y"), lax.axis_index("c")


class _Gather:
    def __init__(self, shards):
        self.ins = list(shards)
        n = self.n = len(shards)
        self.out_shape = [jax.ShapeDtypeStruct((N_DEV,) + s.shape, s.dtype) for s in shards]
        self.scratch = [pltpu.SemaphoreType.DMA((7 * n,)), pltpu.SemaphoreType.DMA((7 * n,)),
                        pltpu.SemaphoreType.DMA((n,))]

    def _copies(self, ins, outs, sems):
        send_sems, recv_sems, local_sems = sems
        x, y, c = _place()
        me, sibling = (x, y, c), (x, y, 1 - c)
        chips = [(1 - x, y), (x, 1 - y), (1 - x, 1 - y)]

        def copy(i, k, block, to, src=None):
            dst = outs[i].at[4 * block[0] + 2 * block[1] + block[2]]
            return pltpu.make_async_remote_copy(
                src_ref=dst if src is None else src, dst_ref=dst, send_sem=send_sems.at[7 * i + k],
                recv_sem=recv_sems.at[7 * i + k], device_id=to, device_id_type=MESH)

        n = self.n
        mine = [pltpu.make_async_copy(ins[i], outs[i].at[4 * x + 2 * y + c], local_sems.at[i]) for i in range(n)]
        first = [copy(i, 0, me, sibling, src=ins[i]) for i in range(n)]
        first += [copy(i, 1 + j, me, (*chip, c), src=ins[i]) for i in range(n) for j, chip in enumerate(chips)]
        landed = [copy(i, 1 + j, (*chip, c), me) for j, chip in enumerate(chips) for i in range(n)]
        passed = [copy(i, 4 + j, (*chip, c), sibling) for j, chip in enumerate(chips) for i in range(n)]
        from_sibling = [copy(i, 0, sibling, me) for i in range(n)]
        from_sibling += [copy(i, 4 + j, (*chip, 1 - c), me) for i in range(n) for j, chip in enumerate(chips)]
        return mine, first, landed, passed, from_sibling

    def start(self, ins, outs, sems):
        mine, first, _, _, _ = self._copies(ins, outs, sems)
        for cp in mine + first:
            cp.start()

    def mid(self, ins, outs, sems):
        _, _, landed, passed, _ = self._copies(ins, outs, sems)
        for got, fwd in zip(landed, passed):
            got.wait_recv()
            fwd.start()

    def end(self, ins, outs, sems):
        mine, first, _, passed, from_sibling = self._copies(ins, outs, sems)
        for cp in from_sibling:
            cp.wait_recv()
        for cp in first + passed:
            cp.wait_send()
        for cp in mine:
            cp.wait()


class _CoreExchange:
    def __init__(self, grads):
        self.ins = list(grads)
        n = self.n = len(grads)
        self.out_shape = [jax.ShapeDtypeStruct((N_CHIP, 1) + g.shape[2:], g.dtype) for g in grads]
        self.scratch = [pltpu.SemaphoreType.DMA((n,)), pltpu.SemaphoreType.DMA((n,))]

    def _copies(self, ins, outs, sems):
        x, y, c = _place()
        return [pltpu.make_async_remote_copy(
            src_ref=ins[i].at[:, pl.ds(1 - c, 1)], dst_ref=outs[i], send_sem=sems[0].at[i],
            recv_sem=sems[1].at[i], device_id=(x, y, 1 - c), device_id_type=MESH) for i in range(self.n)]

    def start(self, ins, outs, sems):
        for cp in self._copies(ins, outs, sems):
            cp.start()

    mid = None

    def end(self, ins, outs, sems):
        for cp in self._copies(ins, outs, sems):
            cp.wait()


class _ChipExchange:
    def __init__(self, parts):
        self.ins = list(parts)
        n = self.n = len(parts)
        self.out_shape = [jax.ShapeDtypeStruct(p.shape, p.dtype) for p in parts]
        self.scratch = [pltpu.SemaphoreType.DMA((3 * n,)), pltpu.SemaphoreType.DMA((3 * n,)),
                        pltpu.SemaphoreType.DMA((n,))]

    def _copies(self, ins, outs, sems):
        send_sems, recv_sems, local_sems = sems
        x, y, c = _place()
        my_chip = 2 * x + y
        copies = []
        for i in range(self.n):
            copies.append(pltpu.make_async_copy(ins[i].at[my_chip], outs[i].at[my_chip], local_sems.at[i]))
            for k, (qx, qy) in enumerate([(1 - x, y), (x, 1 - y), (1 - x, 1 - y)]):
                copies.append(pltpu.make_async_remote_copy(
                    src_ref=ins[i].at[2 * qx + qy], dst_ref=outs[i].at[my_chip], send_sem=send_sems.at[3 * i + k],
                    recv_sem=recv_sems.at[3 * i + k], device_id=(qx, qy, c), device_id_type=MESH))
        return copies

    def start(self, ins, outs, sems):
        for cp in self._copies(ins, outs, sems):
            cp.start()

    mid = None

    def end(self, ins, outs, sems):
        for cp in self._copies(ins, outs, sems):
            cp.wait()


def _call(body, *, name, grid, in_specs, out_specs, out_shape, args, scratch=(), sem=None, hosted=()):
    n_in, n_out, n_scr = len(in_specs), len(out_specs), len(scratch)
    x_in = [len(p.ins) for p in hosted]
    x_scr = [len(p.scratch) for p in hosted]
    steps = int(np.prod(grid))
    assert not hosted or steps >= 3

    def wrapped(*refs):
        refs = list(refs)
        ins, refs = refs[:n_in], refs[n_in:]
        x_ins = [[refs.pop(0) for _ in range(k)] for k in x_in]
        outs, refs = refs[:n_out], refs[n_out:]
        x_outs = [[refs.pop(0) for _ in range(k)] for k in x_in]
        scr, refs = refs[:n_scr], refs[n_scr:]
        x_sems = [[refs.pop(0) for _ in range(k)] for k in x_scr]
        step = 0
        for d in range(len(grid)):
            step = step * grid[d] + pl.program_id(d)

        def phase(which, at):
            fns = [(getattr(p, which), a) for p, a in zip(hosted, zip(x_ins, x_outs, x_sems)) if getattr(p, which)]
            if fns:
                @pl.when(step == at)
                def _():
                    for fn, a in fns:
                        fn(*a)

        phase("start", 0)
        phase("mid", steps // 2)
        body(*ins, *outs, *scr)
        phase("end", steps - 1)

    results = pl.pallas_call(
        wrapped, name=name, grid=grid,
        in_specs=list(in_specs) + [ANY] * sum(x_in), out_specs=list(out_specs) + [ANY] * sum(x_in),
        out_shape=list(out_shape) + [s for p in hosted for s in p.out_shape],
        scratch_shapes=list(scratch) + [s for p in hosted for s in p.scratch],
        compiler_params=_params(*(("arbitrary",) * len(grid) if hosted else sem)),
    )(*args, *[a for p in hosted for a in p.ins])
    own, rest = list(results[:n_out]), list(results[n_out:])
    return own, [[rest.pop(0) for _ in range(k)] for k in x_in]


def _exchange(programs, name):
    x_in = [len(p.ins) for p in programs]
    x_scr = [len(p.scratch) for p in programs]

    def body(*refs):
        refs = list(refs)
        x_ins = [[refs.pop(0) for _ in range(k)] for k in x_in]
        x_outs = [[refs.pop(0) for _ in range(k)] for k in x_in]
        x_sems = [[refs.pop(0) for _ in range(k)] for k in x_scr]
        for which in ("start", "mid", "end"):
            for p, a in zip(programs, zip(x_ins, x_outs, x_sems)):
                if getattr(p, which):
                    getattr(p, which)(*a)

    results = list(pl.pallas_call(
        body, name=name, in_specs=[ANY] * sum(x_in), out_specs=[ANY] * sum(x_in),
        out_shape=[s for p in programs for s in p.out_shape],
        scratch_shapes=[s for p in programs for s in p.scratch],
    )(*[a for p in programs for a in p.ins]))
    return [[results.pop(0) for _ in range(k)] for k in x_in]


def _nt(a, b):
    return lax.dot_general(a, b, (((1,), (1,)), ((), ())), preferred_element_type=F32)


def _nn(a, b):
    return lax.dot_general(a, b, (((1,), (0,)), ((), ())), preferred_element_type=F32)


def _tn(a, b):
    return lax.dot_general(a, b, (((0,), (0,)), ((), ())), preferred_element_type=F32)


def _rms(x, g):
    r = lax.rsqrt(jnp.mean(x * x, axis=-1, keepdims=True) + EPS)
    return x * r, r


def _rms_bwd(dn, xhat, r, g):
    dg = jnp.sum(dn * xhat, axis=0, keepdims=True)
    dxh = dn * g
    dx = r * (dxh - xhat * jnp.mean(dxh * xhat, axis=-1, keepdims=True))
    return dx, dg


def _ffn_fwd(x, g, wg_t, wu_t, wd, name, hosted=()):
    t = x.shape[0]
    tm = 512

    def body(x_ref, g_ref, wg_ref, wu_ref, wd_ref, h_ref, n_ref, hff_ref):
        xhat, _ = _rms(x_ref[...], g_ref[...])
        n = (xhat * g_ref[...]).astype(BF16)
        n_ref[...] = n
        for c in range(0, D_FF, FF_CHUNK):
            a = _nt(n, wg_ref[c:c + FF_CHUNK, :])
            b = _nt(n, wu_ref[c:c + FF_CHUNK, :])
            hff_ref[:, c:c + FF_CHUNK] = (a * jax.nn.sigmoid(a) * b).astype(BF16)
        h_ref[...] = x_ref[...] + 0.5 * _nn(hff_ref[...], wd_ref[...])

    row = pl.BlockSpec((tm, D_MODEL), lambda i: (i, 0))
    return _call(
        body, name=name, grid=(t // tm,),
        in_specs=[row, _resident(g), _resident(wg_t), _resident(wu_t), _resident(wd)],
        out_specs=[row, row],
        out_shape=[jax.ShapeDtypeStruct((t, D_MODEL), F32), jax.ShapeDtypeStruct((t, D_MODEL), BF16)],
        scratch=[pltpu.VMEM((tm, D_FF), BF16)], sem=("parallel",), args=(x, g, wg_t, wu_t, wd), hosted=hosted)


def _ffn_bwd(x, n, g, dh, wg_t, wu_t, wd, name):
    t = x.shape[0]
    tm = 256

    def body(x_ref, n_ref, g_ref, dh_ref, wg_ref, wu_ref, wd_ref,
             dx_ref, dg_ref, hff_ref, da_ref, db_ref, dhh_ref):
        @pl.when(pl.program_id(0) == 0)
        def _():
            dg_ref[...] = jnp.zeros_like(dg_ref)

        dhh = (0.5 * dh_ref[...]).astype(BF16)
        dhh_ref[...] = dhh
        nb = n_ref[...]
        for c in range(0, D_FF, FF_CHUNK):
            cols = slice(c, c + FF_CHUNK)
            a = _nt(nb, wg_ref[cols, :])
            b = _nt(nb, wu_ref[cols, :])
            s = jax.nn.sigmoid(a)
            silu = a * s
            dhff = _nt(dhh, wd_ref[cols, :])
            hff_ref[:, cols] = (silu * b).astype(BF16)
            da_ref[:, cols] = (dhff * b * (s * (1.0 + a * (1.0 - s)))).astype(BF16)
            db_ref[:, cols] = (dhff * silu).astype(BF16)
        dn = _nn(da_ref[...], wg_ref[...]) + _nn(db_ref[...], wu_ref[...])
        xhat, r = _rms(x_ref[...], g_ref[...])
        dx, dg = _rms_bwd(dn, xhat, r, g_ref[...])
        dx_ref[...] = dh_ref[...] + dx
        dg_ref[...] += dg

    row = pl.BlockSpec((tm, D_MODEL), lambda i: (i, 0))
    hid = pl.BlockSpec((tm, D_FF), lambda i: (i, 0))
    return pl.pallas_call(
        body, name=name, grid=(t // tm,),
        in_specs=[row, row, _resident(g), row, _resident(wg_t), _resident(wu_t), _resident(wd)],
        out_specs=[row, pl.BlockSpec((1, D_MODEL), lambda i: (0, 0)), hid, hid, hid, row],
        out_shape=[jax.ShapeDtypeStruct((t, D_MODEL), F32), jax.ShapeDtypeStruct((1, D_MODEL), F32),
                   jax.ShapeDtypeStruct((t, D_FF), BF16), jax.ShapeDtypeStruct((t, D_FF), BF16),
                   jax.ShapeDtypeStruct((t, D_FF), BF16), jax.ShapeDtypeStruct((t, D_MODEL), BF16)],
        compiler_params=_params("arbitrary"),
    )(x, n, g, dh, wg_t, wu_t, wd)


def _tn_matmul(a, b, rc, name, hosted=()):
    t, r = a.shape
    c = b.shape[1]
    tk = 1024

    def body(a_ref, b_ref, o_ref):
        @pl.when(pl.program_id(1) == 0)
        def _():
            o_ref[...] = jnp.zeros_like(o_ref)

        o_ref[...] += _tn(a_ref[...].astype(BF16), b_ref[...].astype(BF16))

    (out,), got = _call(
        body, name=name, grid=(r // rc, t // tk),
        in_specs=[pl.BlockSpec((tk, rc), lambda i, k: (k, i)), pl.BlockSpec((tk, c), lambda i, k: (k, 0))],
        out_specs=[pl.BlockSpec((rc, c), lambda i, k: (i, 0))],
        out_shape=[jax.ShapeDtypeStruct((r, c), F32)],
        sem=("parallel", "arbitrary"), args=(a, b), hosted=hosted)
    return out, got


PIECE_W = (256,) * 9 + (512, 128, 128, GATE_W)


def _inproj_fwd(h, g, w_t, b_in, name, hosted=()):
    t = h.shape[0]
    tm, nc = 512, 512

    def body(h_ref, g_ref, w_ref, b_ref, u_ref, zq_ref, zg_ref):
        xhat, _ = _rms(h_ref[...], g_ref[...])
        u = (xhat * g_ref[...]).astype(BF16)
        u_ref[...] = u
        for c in range(D_IN // nc):
            z = _nt(u, w_ref[c * nc:(c + 1) * nc, :]) + b_ref[:, c * nc:(c + 1) * nc]
            if c < QKV_W // nc:
                zq_ref[:, c * nc:(c + 1) * nc] = z.astype(BF16)
            else:
                zg_ref[:, c * nc - QKV_W:(c + 1) * nc - QKV_W] = z

    row = lambda w: pl.BlockSpec((tm, w), lambda i: (i, 0))
    full = _resident
    return _call(
        body, name=name, grid=(t // tm,),
        in_specs=[row(D_MODEL), full(g), full(w_t), full(b_in)],
        out_specs=[row(D_MODEL), row(QKV_W), row(GATE_W)],
        out_shape=[jax.ShapeDtypeStruct((t, D_MODEL), BF16), jax.ShapeDtypeStruct((t, QKV_W), BF16),
                   jax.ShapeDtypeStruct((t, GATE_W), F32)],
        sem=("parallel",), args=(h, g, w_t, b_in), hosted=hosted)


def _inproj_bwd(pieces, h, g, dh_res, w_t, name):
    t = h.shape[0]
    tm = 256
    npiece = len(PIECE_W)
    offs = np.concatenate([[0], np.cumsum(PIECE_W)]).tolist()

    def body(*refs):
        p_refs = refs[:npiece]
        h_ref, g_ref, dhr_ref, w_ref, dh_ref, dz_ref, db_ref, dg_ref = refs[npiece:]
        i = pl.program_id(0)

        @pl.when(i == 0)
        def _():
            db_ref[...] = jnp.zeros_like(db_ref)
            dg_ref[...] = jnp.zeros_like(dg_ref)

        du = jnp.zeros((tm, D_MODEL), F32)
        for k in range(npiece):
            o, w = offs[k], PIECE_W[k]
            for c0 in range(0, w, 512):
                cw = min(512, w - c0)
                pz = p_refs[k][:, c0:c0 + cw]
                dz_ref[:, o + c0:o + c0 + cw] = pz
                db_ref[:, o + c0:o + c0 + cw] += jnp.sum(pz.astype(F32), axis=0, keepdims=True)
                du = du + _nn(pz, w_ref[o + c0:o + c0 + cw, :])
        xhat, r = _rms(h_ref[...], g_ref[...])
        dx, dg = _rms_bwd(du, xhat, r, g_ref[...])
        dh_ref[...] = dhr_ref[...] + dx
        dg_ref[...] += dg

    row = lambda w: pl.BlockSpec((tm, w), lambda i: (i, 0))
    full = lambda shp: pl.BlockSpec(shp, lambda i: (0, 0))
    return pl.pallas_call(
        body, name=name, grid=(t // tm,),
        in_specs=[row(w) for w in PIECE_W] + [row(D_MODEL), _resident(g), row(D_MODEL), _resident(w_t)],
        out_specs=[row(D_MODEL), row(D_IN), full((1, D_IN)), full((1, D_MODEL))],
        out_shape=[jax.ShapeDtypeStruct((t, D_MODEL), F32), jax.ShapeDtypeStruct((t, D_IN), BF16),
                   jax.ShapeDtypeStruct((1, D_IN), F32), jax.ShapeDtypeStruct((1, D_MODEL), F32)],
        compiler_params=_params("arbitrary"),
    )(*pieces, h, g, dh_res, w_t)


def _t5_bucket(dist):
    max_exact = N_BUCKETS // 2
    n = jnp.maximum(dist, 0)
    nf = jnp.maximum(n, 1).astype(jnp.float32)
    large = max_exact + (jnp.log(nf / max_exact) / math.log(MAX_DISTANCE / max_exact)
                         * (N_BUCKETS - max_exact)).astype(jnp.int32)
    large = jnp.minimum(large, N_BUCKETS - 1)
    return jnp.where(n < max_exact, n, large)


ATT_CFG = ((1, 128, 0, 4), (4, 128, 4, 4), (16, 128, 8, 4), (1, B_WINDOW - 1, A_HEADS, 8))


def _bucket_tiles():
    qi = jnp.arange(BLOCK)[:, None]
    ki = jnp.arange(2 * BLOCK)[None, :]
    dist = qi + BLOCK - ki
    return jnp.stack([_t5_bucket(dist * cfg[0]) for cfg in ATT_CFG]).astype(jnp.int32)


def _band(max_steps):
    row = lax.broadcasted_iota(jnp.int32, (BLOCK, 2 * BLOCK), 0)
    col = lax.broadcasted_iota(jnp.int32, (BLOCK, 2 * BLOCK), 1)
    dist = row + BLOCK - col
    return (dist >= 0) & (dist <= max_steps)


def _bias_build(table, buckets):
    def body(tab_ref, bt_ref, out_ref):
        col = lax.broadcasted_iota(jnp.int32, (BLOCK, 2 * BLOCK), 1)
        for ci, (_, max_steps, h0, nh) in enumerate(ATT_CFG):
            bt = bt_ref[ci]
            band = _band(max_steps)
            for h in range(h0, h0 + nh):
                acc = lax.fori_loop(0, N_BUCKETS, lambda b, acc: jnp.where(bt == b, tab_ref[b, h], acc),
                                    jnp.zeros((BLOCK, 2 * BLOCK), F32))
                out_ref[0, BLOCK * h:BLOCK * (h + 1), :] = jnp.where(band & (col >= BLOCK), acc, NEG)
                out_ref[1, BLOCK * h:BLOCK * (h + 1), :] = jnp.where(band, acc, NEG)

    return pl.pallas_call(
        body, name="bias_build",
        in_specs=[pl.BlockSpec(memory_space=pltpu.SMEM), pl.BlockSpec(memory_space=pltpu.VMEM)],
        out_specs=pl.BlockSpec(memory_space=pltpu.VMEM),
        out_shape=jax.ShapeDtypeStruct((2, TOTAL_HEADS * BLOCK, 2 * BLOCK), F32),
    )(table, buckets)


def _bias_reduce(dbias, buckets, dsink_rows):
    def body(db_ref, bt_ref, ds_ref, out_ref, sink_ref):
        ri = lax.broadcasted_iota(jnp.int32, (N_BUCKETS, 128), 0)
        ci = lax.broadcasted_iota(jnp.int32, (N_BUCKETS, 128), 1)

        def per_bucket(b, acc):
            for cfg_i, (_, _, h0, nh) in enumerate(ATT_CFG):
                hit = bt_ref[cfg_i] == b
                for h in range(h0, h0 + nh):
                    val = jnp.sum(jnp.where(hit, db_ref[BLOCK * h:BLOCK * (h + 1), :], 0.0))
                    acc = jnp.where((ri == b) & (ci == h), val, acc)
            return acc

        out_ref[...] = lax.fori_loop(0, N_BUCKETS, per_bucket, jnp.zeros((N_BUCKETS, 128), F32))
        for h in range(8):
            sink_ref[h:h + 1, :] = jnp.sum(ds_ref[BLOCK * h:BLOCK * (h + 1), :], axis=0, keepdims=True)

    return pl.pallas_call(
        body, name="bias_reduce",
        in_specs=[pl.BlockSpec(memory_space=pltpu.VMEM)] * 3,
        out_specs=[pl.BlockSpec(memory_space=pltpu.VMEM)] * 2,
        out_shape=[jax.ShapeDtypeStruct((N_BUCKETS, 128), F32), jax.ShapeDtypeStruct((8, 128), F32)],
    )(dbias, buckets, dsink_rows)


class _Att:
    def __init__(self, cfg_i):
        stride, _, h0, nh = ATT_CFG[cfg_i]
        self.d = stride if cfg_i < 3 else 1
        self.h0, self.nh = h0, nh
        self.row_w = QKV_W if self.d == 1 else 3 * 256
        if cfg_i < 3:
            self.nq, self.wkv = 1, 256
            self.q_unit = [cfg_i if self.d == 1 else 0]
            self.k_unit, self.v_unit = (3 + cfg_i, 6 + cfg_i) if self.d == 1 else (1, 2)
            self.sinks = False
        else:
            self.nq, self.wkv = 2, 128
            self.q_unit = [9, 10]
            self.k_unit, self.v_unit = 22, 23
            self.sinks = True
        self.wq = 256 * self.nq


def _att_in_specs(cf, bsz):
    uq, ukv = cf.row_w // 256, cf.row_w // cf.wkv
    specs = [pl.BlockSpec((bsz, BLOCK, 256), functools.partial(lambda r, j, u: (0, j, r * uq + u), u=u))
             for u in cf.q_unit]
    for unit in (cf.k_unit, cf.v_unit):
        specs.append(pl.BlockSpec((bsz, BLOCK, cf.wkv),
                                  functools.partial(lambda r, j, u: (0, jnp.maximum(j - 1, 0), r * ukv + u), u=unit)))
        specs.append(pl.BlockSpec((bsz, BLOCK, cf.wkv),
                                  functools.partial(lambda r, j, u: (0, j, r * ukv + u), u=unit)))
    for qb in range(cf.nq):
        specs.append(pl.BlockSpec((None, HEADS_PER_BLOCK * BLOCK, 2 * BLOCK), functools.partial(
            lambda r, j, u: (jnp.minimum(j, 1), u, 0), u=cf.h0 // HEADS_PER_BLOCK + qb)))
    if cf.sinks:
        specs += [pl.BlockSpec((HEADS_PER_BLOCK * BLOCK, 128), functools.partial(lambda r, j, u: (u, 0), u=qb))
                  for qb in range(cf.nq)]
    return specs


HEADS_PER_BLOCK = 4


def _head_masks(rows):
    head = lax.broadcasted_iota(jnp.int32, (rows, 256), 1) // HEAD_DIM
    return [head == h for h in range(HEADS_PER_BLOCK)]


def _stack_heads(x, masks):
    return jnp.concatenate([jnp.where(m, x, jnp.zeros_like(x)) for m in masks], axis=0)


def _unstack_heads(x4, masks):
    blocks = [x4[BLOCK * h:BLOCK * (h + 1)] for h in range(HEADS_PER_BLOCK)]
    return jnp.where(masks[0], blocks[0], jnp.where(masks[1], blocks[1], jnp.where(masks[2], blocks[2], blocks[3])))


def _row_value(x):
    return jnp.max(x, axis=-1, keepdims=True)


def _kv_operand(cf, x, qb):
    if cf.wkv == 256:
        return x
    lane = lax.broadcasted_iota(jnp.int32, x.shape, 1)
    swapped = pltpu.roll(x, HEAD_DIM, 1)
    half = jnp.where(lane < HEAD_DIM, x, swapped) if qb == 0 else jnp.where(lane < HEAD_DIM, swapped, x)
    return jnp.concatenate([half, half], axis=1)


def _kv_fold(cf, grads):
    if cf.wkv == 256:
        return grads[0]
    folded = []
    for g in grads:
        x = g[:, :128] + g[:, 128:]
        folded.append(x + pltpu.roll(x, HEAD_DIM, 1))
    lane = lax.broadcasted_iota(jnp.int32, folded[0].shape, 1)
    return jnp.where(lane < HEAD_DIM, folded[0], folded[1])


def _attn_fwd(cf, zf, bias, sinks, name, hosted=()):
    bsz, l, _ = zf.shape
    nb = l // BLOCK

    def body(*refs):
        refs = list(refs)
        q_refs = [refs.pop(0) for _ in range(cf.nq)]
        kp_ref, kc_ref, vp_ref, vc_ref = [refs.pop(0) for _ in range(4)]
        bias_refs = [refs.pop(0) for _ in range(cf.nq)]
        sink_refs = [refs.pop(0) for _ in range(cf.nq)] if cf.sinks else None
        o_ref, lse_ref = refs
        masks, kv_masks = _head_masks(BLOCK), _head_masks(2 * BLOCK)
        sinks4 = [_row_value(ref[...]) for ref in sink_refs] if cf.sinks else None
        for bi in range(bsz):
            k = jnp.concatenate([kp_ref[bi], kc_ref[bi]], axis=0)
            v = jnp.concatenate([vp_ref[bi], vc_ref[bi]], axis=0)
            for qb in range(cf.nq):
                cols = slice(256 * qb, 256 * (qb + 1))
                kb, vb = _kv_operand(cf, k, qb), _kv_operand(cf, v, qb)
                q4 = _stack_heads(q_refs[qb][bi] * (HEAD_DIM ** -0.5), masks)
                s = _nt(q4, kb) + bias_refs[qb][...]
                m = jnp.max(s, axis=-1, keepdims=True)
                if cf.sinks:
                    sk = sinks4[qb]
                    m = jnp.maximum(m, sk)
                p = jnp.exp(s - m)
                den = jnp.sum(p, axis=-1, keepdims=True)
                if cf.sinks:
                    den = den + jnp.exp(sk - m)
                pn = (p * (1.0 / den)).astype(BF16)
                p_lanes = jnp.concatenate([pn[BLOCK * h:BLOCK * (h + 1)] for h in range(HEADS_PER_BLOCK)], axis=1)
                v4 = jnp.concatenate([jnp.where(mk, vb, jnp.zeros_like(vb)) for mk in kv_masks], axis=0)
                o_ref[bi, :, cols] = _nn(p_lanes, v4)
                lse_ref[bi, :, cols] = _unstack_heads(
                    jnp.broadcast_to(m + jnp.log(den), (HEADS_PER_BLOCK * BLOCK, 256)), masks)

    in_specs = _att_in_specs(cf, bsz)
    args = [zf] * (cf.nq + 4) + [bias] * cf.nq + ([sinks] * cf.nq if cf.sinks else [])
    out = pl.BlockSpec((bsz, BLOCK, cf.wq), lambda r, j: (0, j, r))
    shape = jax.ShapeDtypeStruct((bsz, l, cf.d * cf.wq), F32)
    return _call(
        body, name=name, grid=(cf.d, nb), in_specs=in_specs, out_specs=[out, out], out_shape=[shape, shape],
        sem=("parallel", "arbitrary"), args=args, hosted=hosted)


def _attn_bwd(cf, zf, bias, sinks, lse_tot, dy, y, name, hosted=()):
    bsz, l, _ = zf.shape
    nb = l // BLOCK

    def body(*refs):
        refs = list(refs)
        q_refs = [refs.pop(0) for _ in range(cf.nq)]
        kp_ref, kc_ref, vp_ref, vc_ref = [refs.pop(0) for _ in range(4)]
        bias_refs = [refs.pop(0) for _ in range(cf.nq)]
        sink_refs = [refs.pop(0) for _ in range(cf.nq)] if cf.sinks else None
        lt_ref, dy_ref, y_ref, dq_ref, dk_ref, dv_ref, dbias_ref = [refs.pop(0) for _ in range(7)]
        dsink_ref = refs.pop(0) if cf.sinks else None
        dk_acc, dv_acc = refs
        r, j = pl.program_id(0), pl.program_id(1)

        @pl.when((r == 0) & (j == 0))
        def _():
            dbias_ref[...] = jnp.zeros_like(dbias_ref)
            if cf.sinks:
                dsink_ref[...] = jnp.zeros_like(dsink_ref)

        @pl.when(j == 0)
        def _():
            dk_acc[...] = jnp.zeros_like(dk_acc)
            dv_acc[...] = jnp.zeros_like(dv_acc)

        masks = _head_masks(BLOCK)
        cur = pl.ds(pl.multiple_of(j * BLOCK, BLOCK), BLOCK)
        prev = pl.ds(pl.multiple_of(jnp.maximum(j - 1, 0) * BLOCK, BLOCK), BLOCK)
        sinks4 = [_row_value(ref[...]) for ref in sink_refs] if cf.sinks else None
        for bi in range(bsz):
            k = jnp.concatenate([kp_ref[bi], kc_ref[bi]], axis=0)
            v = jnp.concatenate([vp_ref[bi], vc_ref[bi]], axis=0)
            dk_blocks, dv_blocks = [], []
            for qb in range(cf.nq):
                cols = slice(256 * qb, 256 * (qb + 1))
                rows = slice(HEADS_PER_BLOCK * BLOCK * qb, HEADS_PER_BLOCK * BLOCK * (qb + 1))
                kb, vb = _kv_operand(cf, k, qb), _kv_operand(cf, v, qb)
                q4 = _stack_heads(q_refs[qb][bi] * (HEAD_DIM ** -0.5), masks)
                lt = lt_ref[bi, :, cols]
                lt4 = jnp.concatenate([_row_value(jnp.where(mk, lt, NEG)) for mk in masks], axis=0)
                pa = jnp.exp(_nt(q4, kb) + bias_refs[qb][...] - lt4)
                dy = dy_ref[bi, :, cols]
                prod = dy * y_ref[bi, :, cols]
                e4 = jnp.concatenate([jnp.sum(jnp.where(mk, prod, 0.0), axis=-1, keepdims=True) for mk in masks],
                                     axis=0)
                dy4 = _stack_heads(dy.astype(BF16), masks)
                ds = pa * (_nt(dy4, vb) - e4)
                dbias_ref[rows, :] += ds
                if cf.sinks:
                    dsink_ref[rows, :] += -(jnp.exp(sinks4[qb] - lt4) * e4)
                dsb = ds.astype(BF16)
                dq_ref[bi, :, cols] = (_unstack_heads(_nn(dsb, kb), masks) * (HEAD_DIM ** -0.5)).astype(BF16)
                dk_blocks.append(_tn(dsb, q4))
                dv_blocks.append(_tn(pa.astype(BF16), dy4))
            dk_new, dv_new = _kv_fold(cf, dk_blocks), _kv_fold(cf, dv_blocks)
            dk_acc[bi, cur, :] += dk_new[BLOCK:]
            dv_acc[bi, cur, :] += dv_new[BLOCK:]
            dk_acc[bi, prev, :] += dk_new[:BLOCK]
            dv_acc[bi, prev, :] += dv_new[:BLOCK]

        @pl.when(j == nb - 1)
        def _():
            dk_ref[...] = dk_acc[...].astype(BF16)
            dv_ref[...] = dv_acc[...].astype(BF16)

    tok = pl.BlockSpec((bsz, BLOCK, cf.wq), lambda r, j: (0, j, r))
    in_specs = _att_in_specs(cf, bsz) + [tok] * 3
    args = [zf] * (cf.nq + 4) + [bias] * cf.nq + ([sinks] * cf.nq if cf.sinks else []) + [lse_tot, dy, y]
    seq = pl.BlockSpec((bsz, l, cf.wkv), lambda r, j: (0, 0, r))
    out_specs = [tok, seq, seq, pl.BlockSpec((cf.nh * BLOCK, 2 * BLOCK), lambda r, j: (0, 0))]
    out_shape = [jax.ShapeDtypeStruct((bsz, l, cf.d * cf.wq), BF16),
                 jax.ShapeDtypeStruct((bsz, l, cf.d * cf.wkv), BF16),
                 jax.ShapeDtypeStruct((bsz, l, cf.d * cf.wkv), BF16),
                 jax.ShapeDtypeStruct((cf.nh * BLOCK, 2 * BLOCK), F32)]
    if cf.sinks:
        out_specs.append(pl.BlockSpec((cf.nh * BLOCK, 128), lambda r, j: (0, 0)))
        out_shape.append(jax.ShapeDtypeStruct((cf.nh * BLOCK, 128), F32))
    return _call(
        body, name=name, grid=(cf.d, nb), in_specs=in_specs, out_specs=out_specs, out_shape=out_shape,
        scratch=[pltpu.VMEM((bsz, l, cf.wkv), F32), pltpu.VMEM((bsz, l, cf.wkv), F32)],
        sem=("arbitrary", "arbitrary"), args=args, hosted=hosted)


def _merge_fwd(o_a, lse_a, o_b, zg, h, wa_t, wb_t, wout, name):
    t = h.shape[0]
    tm = 512

    def body(o1, o2, o3, l1, l2, l3, ob_ref, zg_ref, h_ref, wa_ref, wb_ref, wo_ref, h2_ref, ya_ref, lt_ref):
        m = jnp.maximum(jnp.maximum(l1[...], l2[...]), l3[...])
        e1, e2, e3 = jnp.exp(l1[...] - m), jnp.exp(l2[...] - m), jnp.exp(l3[...] - m)
        se = e1 + e2 + e3
        ya = (e1 / se) * o1[...] + (e2 / se) * o2[...] + (e3 / se) * o3[...]
        ya_ref[...] = ya
        lt_ref[...] = m + jnp.log(se)
        pa = _nt(ya.astype(BF16), wa_ref[...])
        pb = _nt(ob_ref[...].astype(BF16), wb_ref[...])
        merged = jax.nn.sigmoid(zg_ref[:, :D_MODEL]) * pa + jax.nn.sigmoid(zg_ref[:, D_MODEL:]) * pb
        h2_ref[...] = h_ref[...] + _nn(merged.astype(BF16), wo_ref[...])

    row = lambda w: pl.BlockSpec((tm, w), lambda i: (i, 0))
    full = _resident
    return pl.pallas_call(
        body, name=name, grid=(t // tm,),
        in_specs=[row(256)] * 6 + [row(512), row(GATE_W), row(D_MODEL), full(wa_t), full(wb_t), full(wout)],
        out_specs=[row(D_MODEL), row(256), row(256)],
        out_shape=[jax.ShapeDtypeStruct((t, D_MODEL), F32), jax.ShapeDtypeStruct((t, 256), F32),
                   jax.ShapeDtypeStruct((t, 256), F32)],
        compiler_params=_params("parallel"),
    )(*o_a, *lse_a, o_b, zg, h, wa_t, wb_t, wout)


def _merge_bwd(dh, ya, o_b, zg, wa_t, wb_t, wout, name, hosted=()):
    t = dh.shape[0]
    tm = 256

    def body(dh_ref, ya_ref, ob_ref, zg_ref, wa_ref, wb_ref, wo_ref,
             mg_ref, dpa_ref, dpb_ref, dzg_ref, dya_ref, dyb_ref):
        dm = _nt(dh_ref[...].astype(BF16), wo_ref[...])
        pa = _nt(ya_ref[...].astype(BF16), wa_ref[...])
        pb = _nt(ob_ref[...].astype(BF16), wb_ref[...])
        sa = jax.nn.sigmoid(zg_ref[:, :D_MODEL])
        sb = jax.nn.sigmoid(zg_ref[:, D_MODEL:])
        mg_ref[...] = (sa * pa + sb * pb).astype(BF16)
        dpa = (dm * sa).astype(BF16)
        dpb = (dm * sb).astype(BF16)
        dpa_ref[...] = dpa
        dpb_ref[...] = dpb
        dzg_ref[:, :D_MODEL] = (dm * pa * (sa * (1.0 - sa))).astype(BF16)
        dzg_ref[:, D_MODEL:] = (dm * pb * (sb * (1.0 - sb))).astype(BF16)
        dya_ref[...] = _nn(dpa, wa_ref[...])
        dyb_ref[...] = _nn(dpb, wb_ref[...])

    row = lambda w: pl.BlockSpec((tm, w), lambda i: (i, 0))
    full = _resident
    sds = jax.ShapeDtypeStruct
    return _call(
        body, name=name, grid=(t // tm,),
        in_specs=[row(D_MODEL), row(256), row(512), row(GATE_W), full(wa_t), full(wb_t), full(wout)],
        out_specs=[row(D_MODEL), row(D_MODEL), row(D_MODEL), row(GATE_W), row(256), row(512)],
        out_shape=[sds((t, D_MODEL), BF16), sds((t, D_MODEL), BF16), sds((t, D_MODEL), BF16),
                   sds((t, GATE_W), BF16), sds((t, 256), F32), sds((t, 512), F32)],
        sem=("parallel",), args=(dh, ya, o_b, zg, wa_t, wb_t, wout), hosted=hosted)


def _loss_head(h, g, target, name):
    t = h.shape[0]
    tm = 1024

    def body(h_ref, g_ref, t_ref, dh_ref, loss_ref, dg_ref):
        @pl.when(pl.program_id(0) == 0)
        def _():
            loss_ref[...] = jnp.zeros_like(loss_ref)
            dg_ref[...] = jnp.zeros_like(dg_ref)

        xhat, r = _rms(h_ref[...], g_ref[...])
        err = xhat * g_ref[...] - t_ref[...]
        loss_ref[...] += 0.5 * jnp.sum(jnp.mean(err * err, axis=-1, keepdims=True), axis=0, keepdims=True)
        dx, dg = _rms_bwd(err * (1.0 / D_MODEL), xhat, r, g_ref[...])
        dh_ref[...] = dx
        dg_ref[...] += dg

    row = pl.BlockSpec((tm, D_MODEL), lambda i: (i, 0))
    vec = pl.BlockSpec((1, D_MODEL), lambda i: (0, 0))
    return pl.pallas_call(
        body, name=name, grid=(t // tm,),
        in_specs=[row, vec, row], out_specs=[row, pl.BlockSpec((1, 128), lambda i: (0, 0)), vec],
        out_shape=[jax.ShapeDtypeStruct((t, D_MODEL), F32), jax.ShapeDtypeStruct((1, 128), F32),
                   jax.ShapeDtypeStruct((1, D_MODEL), F32)],
        compiler_params=_params("arbitrary"),
    )(h, g, target)


def _pair_sum(grad, got, name):
    _, _, r, cdim = grad.shape
    core = lax.axis_index("c").astype(jnp.int32).reshape(1)

    def body(core_ref, g_ref, s_ref, o_ref):
        o_ref[...] = (g_ref[...] + s_ref[...]).astype(BF16)

    return pl.pallas_call(
        body, name=name,
        grid_spec=pltpu.PrefetchScalarGridSpec(
            num_scalar_prefetch=1, grid=(N_CHIP,),
            in_specs=[pl.BlockSpec((None, None, r, cdim), lambda q, core_ref: (q, core_ref[0], 0, 0)),
                      pl.BlockSpec((None, None, r, cdim), lambda q, core_ref: (q, 0, 0, 0))],
            out_specs=pl.BlockSpec((None, r, cdim), lambda q, core_ref: (q, 0, 0))),
        out_shape=jax.ShapeDtypeStruct((N_CHIP, r, cdim), BF16),
        compiler_params=_params("parallel"),
    )(core, grad, got)


def _chip_sum(parts, name):
    def body(p_ref, o_ref):
        o_ref[...] = ((p_ref[0].astype(F32) + p_ref[1].astype(F32)) + p_ref[2].astype(F32)) + p_ref[3].astype(F32)

    return pl.pallas_call(
        body, name=name, out_shape=jax.ShapeDtypeStruct(parts.shape[1:], F32),
        compiler_params=pltpu.CompilerParams(vmem_limit_bytes=VMEM_LIMIT),
    )(parts)


def _adamw_math(w, g, m, v):
    m = ADAM_B1 * m + (1.0 - ADAM_B1) * g
    v = ADAM_B2 * v + (1.0 - ADAM_B2) * jnp.square(g)
    m_hat = m / (1.0 - ADAM_B1 ** ADAM_STEP)
    v_hat = v / (1.0 - ADAM_B2 ** ADAM_STEP)
    delta = -ADAM_LR * (m_hat / (jnp.sqrt(v_hat) + ADAM_EPS) + ADAM_WD * w)
    return delta, m, v


def _adamw(w, g, m, v, name):
    def body(w_ref, g_ref, m_ref, v_ref, d_ref, m2_ref, v2_ref):
        d_ref[...], m2_ref[...], v2_ref[...] = _adamw_math(w_ref[...], g_ref[...], m_ref[...], v_ref[...])

    sds = jax.ShapeDtypeStruct(w.shape, F32)
    return pl.pallas_call(body, name=name, out_shape=[sds, sds, sds],
                          compiler_params=pltpu.CompilerParams(vmem_limit_bytes=VMEM_LIMIT))(w, g, m, v)


SMALL_ROWS = 80


def _small_update(g, w, m, v, name):
    def body(g_ref, w_ref, m_ref, v_ref, gs_ref, d_ref, m2_ref, v2_ref, got_ref, send_sems, recv_sems):
        x, y, c = _place()
        me = 4 * x + 2 * y + c
        got_ref[me] = g_ref[...]
        copies = []
        for k in range(1, N_DEV):
            peer = (x ^ (k >> 2), y ^ ((k >> 1) & 1), c ^ (k & 1))
            cp = pltpu.make_async_remote_copy(
                src_ref=g_ref, dst_ref=got_ref.at[me], send_sem=send_sems.at[k - 1], recv_sem=recv_sems.at[k - 1],
                device_id=peer, device_id_type=MESH)
            cp.start()
            copies.append(cp)
        for cp in copies:
            cp.wait()
        total = got_ref[0]
        for k in range(1, N_DEV):
            total = total + got_ref[k]
        gs_ref[...] = total
        d_ref[...], m2_ref[...], v2_ref[...] = _adamw_math(w_ref[...], total, m_ref[...], v_ref[...])

    sds = jax.ShapeDtypeStruct((SMALL_ROWS, 128), F32)
    vm = pl.BlockSpec(memory_space=pltpu.VMEM)
    return pl.pallas_call(
        body, name=name, in_specs=[vm] * 4, out_specs=[vm] * 4, out_shape=[sds] * 4,
        scratch_shapes=[pltpu.VMEM((N_DEV, SMALL_ROWS, 128), F32), pltpu.SemaphoreType.DMA((N_DEV - 1,)),
                        pltpu.SemaphoreType.DMA((N_DEV - 1,))],
    )(g, w, m, v)


def _pack_small(gains, b_in, rel_bias, sinks, last):
    rows = [a.reshape(8, 128) for a in gains] + [b_in.reshape(40, 128), rel_bias.reshape(5, 128),
                                                 jnp.pad(sinks.reshape(1, 8), ((0, 0), (0, 120))), last]
    rows.append(jnp.zeros((SMALL_ROWS - 79, 128), F32))
    return jnp.concatenate(rows, axis=0)


def _unpack_small(p, like):
    out = [p[8 * i:8 * i + 8].reshape(like[i].shape) for i in range(4)]
    out.append(p[32:72].reshape(like[4].shape))
    out.append(p[72:77].reshape(like[5].shape))
    out.append(p[77, :8].reshape(like[6].shape))
    return out


def kernel(x, ffn1_norm, ffn1_w_gate, ffn1_w_up, ffn1_w_down, mix_norm, w_in, b_in, w_branch_a, w_branch_b, w_out, sinks, rel_bias, ffn2_norm, ffn2_w_gate, ffn2_w_up, ffn2_w_down, final_norm, loss_target, m_ffn1_norm, m_ffn1_w_gate, m_ffn1_w_up, m_ffn1_w_down, m_mix_norm, m_w_in, m_b_in, m_w_branch_a, m_w_branch_b, m_w_out, m_sinks, m_rel_bias, m_ffn2_norm, m_ffn2_w_gate, m_ffn2_w_up, m_ffn2_w_down, m_final_norm, v_ffn1_norm, v_ffn1_w_gate, v_ffn1_w_up, v_ffn1_w_down, v_mix_norm, v_w_in, v_b_in, v_w_branch_a, v_w_branch_b, v_w_out, v_sinks, v_rel_bias, v_ffn2_norm, v_ffn2_w_gate, v_ffn2_w_up, v_ffn2_w_down, v_final_norm):
    bsz, seq, _ = x.shape
    t = bsz * seq
    xt = x.reshape(t, D_MODEL)
    target = loss_target.reshape(t, D_MODEL)

    big = [("ffn1_w_gate", ffn1_w_gate, m_ffn1_w_gate, v_ffn1_w_gate, True),
           ("ffn1_w_up", ffn1_w_up, m_ffn1_w_up, v_ffn1_w_up, True),
           ("ffn1_w_down", ffn1_w_down, m_ffn1_w_down, v_ffn1_w_down, False),
           ("w_in", w_in, m_w_in, v_w_in, True),
           ("w_branch_a", w_branch_a, m_w_branch_a, v_w_branch_a, True),
           ("w_branch_b", w_branch_b, m_w_branch_b, v_w_branch_b, True),
           ("w_out", w_out, m_w_out, v_w_out, False),
           ("ffn2_w_gate", ffn2_w_gate, m_ffn2_w_gate, v_ffn2_w_gate, True),
           ("ffn2_w_up", ffn2_w_up, m_ffn2_w_up, v_ffn2_w_up, True),
           ("ffn2_w_down", ffn2_w_down, m_ffn2_w_down, v_ffn2_w_down, False)]
    shard = {nm: (w[0].T if tr else w[0]).astype(BF16) for nm, w, _, _, tr in big}
    full = {}

    def gather(names):
        return _Gather([shard[nm] for nm in names])

    def keep(names, got):
        for nm, gw in zip(names, got):
            full[nm] = gw.reshape(-1, gw.shape[-1])

    ffn1_names = ["ffn1_w_gate", "ffn1_w_up", "ffn1_w_down"]
    mix_names = ["w_in", "w_branch_a", "w_branch_b", "w_out"]
    ffn2_names = ["ffn2_w_gate", "ffn2_w_up", "ffn2_w_down"]
    g1, gm, g2, gf = ffn1_norm, mix_norm, ffn2_norm, final_norm.reshape(1, D_MODEL)

    keep(ffn1_names, _exchange([gather(ffn1_names)], "gather_ffn1")[0])
    (h1, n1), (got,) = _ffn_fwd(xt, g1, full["ffn1_w_gate"], full["ffn1_w_up"], full["ffn1_w_down"], "ffn1_fwd",
                                hosted=[gather(mix_names)])
    keep(mix_names, got)
    (u, zq, zg), (got,) = _inproj_fwd(h1, gm, full["w_in"], b_in, "inproj_fwd", hosted=[gather(ffn2_names[:2])])
    keep(ffn2_names[:2], got)
    buckets = _bucket_tiles()
    bias = _bias_build(rel_bias, buckets)
    sink_rows = jnp.broadcast_to(sinks.reshape(8, 1, 1), (8, BLOCK, 128)).reshape(8 * BLOCK, 128)
    cfs = [_Att(i) for i in range(4)]
    zfold = []
    for i, cf in enumerate(cfs):
        if cf.d == 1:
            zfold.append(zq.reshape(bsz, seq, QKV_W))
        else:
            own = jnp.concatenate([zq[:, 768 * part + 256 * i:768 * part + 256 * (i + 1)] for part in range(3)], axis=1)
            zfold.append(own.reshape(bsz, seq // cf.d, cf.d * cf.row_w))
    att = [None] * 4
    for i in (3, 0, 1, 2):
        cf = cfs[i]
        (o, lse), got = _attn_fwd(cf, zfold[i], bias, sink_rows, f"attn{i}_fwd",
                                  hosted=[gather(ffn2_names[2:])] if i == 3 else ())
        if i == 3:
            keep(ffn2_names[2:], got[0])
        att[i] = (o.reshape(t, cf.wq), lse.reshape(t, cf.wq))
    o_b, lse_b = att[3]
    h2, ya, lse_tot = _merge_fwd([a[0] for a in att[:3]], [a[1] for a in att[:3]], o_b, zg, h1,
                                 full["w_branch_a"], full["w_branch_b"], full["w_out"], "merge_fwd")
    (h3, n2), _ = _ffn_fwd(h2, g2, full["ffn2_w_gate"], full["ffn2_w_up"], full["ffn2_w_down"], "ffn2_fwd")
    dh3, loss_part, dgf = _loss_head(h3, gf, target, "loss_head")

    grads, pair, from_chips = {}, {}, {}

    def by_owner(nm):
        return grads[nm].reshape(N_CHIP, 2, -1, grads[nm].shape[-1])

    def to_core(names):
        return _CoreExchange([by_owner(nm) for nm in names])

    def pair_up(names, got):
        for nm, sib in zip(names, got):
            pair[nm] = _pair_sum(by_owner(nm), sib, f"pair_sum_{nm}")

    def to_chips(names):
        return _ChipExchange([pair[nm] for nm in names])

    def landed(names, got):
        for nm, parts in zip(names, got):
            from_chips[nm] = parts

    dh2, dg2, hff, da, db, dhh = _ffn_bwd(h2, n2, g2, dh3, full["ffn2_w_gate"], full["ffn2_w_up"],
                                          full["ffn2_w_down"], "ffn2_bwd")
    grads["ffn2_w_gate"], _ = _tn_matmul(da, n2, 1408, "ffn2_dgate")
    grads["ffn2_w_up"], _ = _tn_matmul(db, n2, 1408, "ffn2_dup")
    grads["ffn2_w_down"], _ = _tn_matmul(hff, dhh, 1408, "ffn2_ddown")
    (merged, dpa, dpb, dzg, dya, dyb), (got,) = _merge_bwd(
        dh2, ya, o_b, zg, full["w_branch_a"], full["w_branch_b"], full["w_out"], "merge_bwd",
        hosted=[to_core(ffn2_names)])
    pair_up(ffn2_names, got)
    grads["w_out"], _ = _tn_matmul(merged, dh2, 1024, "dw_out")
    grads["w_branch_a"], _ = _tn_matmul(dpa, ya, 1024, "dw_branch_a")
    grads["w_branch_b"], _ = _tn_matmul(dpb, o_b, 1024, "dw_branch_b")
    out_names = ["w_out", "w_branch_a", "w_branch_b"]
    dq, dk, dv, dbias, dsink = [None] * 4, [None] * 4, [None] * 4, [None] * 4, None
    for i in (3, 0, 1, 2):
        cf = cfs[i]
        lt_i, dy_i, y_i = (lse_tot, dya, ya) if i < 3 else (lse_b, dyb, o_b)
        shp = (bsz, seq // cf.d, cf.d * cf.wq)
        hosted = {3: lambda: [to_chips(ffn2_names[:2])], 0: lambda: [to_chips(ffn2_names[2:]), to_core(out_names)],
                  1: lambda: [to_chips(out_names)], 2: lambda: []}[i]()
        res, got = _attn_bwd(cf, zfold[i], bias, sink_rows, lt_i.reshape(shp), dy_i.reshape(shp), y_i.reshape(shp),
                             f"attn{i}_bwd", hosted=hosted)
        if i == 3:
            landed(ffn2_names[:2], got[0])
        elif i == 0:
            landed(ffn2_names[2:], got[0])
            pair_up(out_names, got[1])
        elif i == 1:
            landed(out_names, got[0])
        dq[i] = res[0].reshape(t, cf.wq)
        dk[i] = res[1].reshape(t, cf.wkv)
        dv[i] = res[2].reshape(t, cf.wkv)
        dbias[i] = res[3]
        if cf.sinks:
            dsink = res[4]
    pieces = dq[:3] + dk[:3] + dv[:3] + [dq[3], dk[3], dv[3], dzg]
    dh1, dz, db_in, dgm = _inproj_bwd(pieces, h1, gm, dh2, full["w_in"], "inproj_bwd")
    dx, dg1, hff, da, db, dhh = _ffn_bwd(xt, n1, g1, dh1, full["ffn1_w_gate"], full["ffn1_w_up"],
                                         full["ffn1_w_down"], "ffn1_bwd")
    grads["w_in"], _ = _tn_matmul(dz, u, 1280, "dw_in")
    grads["ffn1_w_down"], (got,) = _tn_matmul(hff, dhh, 1408, "ffn1_ddown", hosted=[to_core(["w_in"])])
    pair_up(["w_in"], got)
    grads["ffn1_w_gate"], got = _tn_matmul(da, n1, 1408, "ffn1_dgate",
                                           hosted=[to_chips(["w_in"]), to_core(["ffn1_w_down"])])
    landed(["w_in"], got[0])
    pair_up(["ffn1_w_down"], got[1])
    grads["ffn1_w_up"], got = _tn_matmul(db, n1, 1408, "ffn1_dup",
                                         hosted=[to_chips(["ffn1_w_down"]), to_core(["ffn1_w_gate"])])
    landed(["ffn1_w_down"], got[0])
    pair_up(["ffn1_w_gate"], got[1])
    got = _exchange([to_chips(["ffn1_w_gate"]), to_core(["ffn1_w_up"])], "reduce_scatter_tail1")
    landed(["ffn1_w_gate"], got[0])
    pair_up(["ffn1_w_up"], got[1])
    landed(["ffn1_w_up"], _exchange([to_chips(["ffn1_w_up"])], "reduce_scatter_tail2")[0])
    dtable, dsinks = _bias_reduce(jnp.concatenate(dbias, axis=0), buckets, dsink)

    out_g, out_d, out_m, out_v = {}, {}, {}, {}
    for nm, w, m, v, tr in big:
        g = _chip_sum(from_chips[nm], f"chip_sum_{nm}")
        g = (g.T if tr else g).reshape(w.shape)
        out_g[nm] = g
        d2, m2, v2 = _adamw(w[0], g[0], m[0], v[0], f"adamw_{nm}")
        out_d[nm], out_m[nm], out_v[nm] = d2[None], m2[None], v2[None]

    small = [("ffn1_norm", ffn1_norm, m_ffn1_norm, v_ffn1_norm), ("mix_norm", mix_norm, m_mix_norm, v_mix_norm),
             ("ffn2_norm", ffn2_norm, m_ffn2_norm, v_ffn2_norm), ("final_norm", final_norm, m_final_norm, v_final_norm),
             ("b_in", b_in, m_b_in, v_b_in), ("rel_bias", rel_bias, m_rel_bias, v_rel_bias),
             ("sinks", sinks, m_sinks, v_sinks)]
    zero_row = jnp.zeros((1, 128), F32)
    pack = lambda arrs, last: _pack_small(arrs[:4], arrs[4], arrs[5], arrs[6], last)
    g_small = pack([dg1, dgm, dg2, dgf, db_in, dtable[:, :TOTAL_HEADS], dsinks[:, 0]], loss_part)
    packed = [pack([s[k] for s in small], zero_row) for k in (1, 2, 3)]
    gs, ds, ms, vs = _small_update(g_small, *packed, "small_update")
    like = [s[1] for s in small]
    for nm_s, g_, d_, m_, v_ in zip([s[0] for s in small], _unpack_small(gs, like), _unpack_small(ds, like),
                                    _unpack_small(ms, like), _unpack_small(vs, like)):
        out_g[nm_s], out_d[nm_s], out_m[nm_s], out_v[nm_s] = g_, d_, m_, v_
    loss = gs[78, 0]

    order = ["ffn1_norm", "ffn1_w_gate", "ffn1_w_up", "ffn1_w_down", "mix_norm", "w_in", "b_in", "w_branch_a",
             "w_branch_b", "w_out", "sinks", "rel_bias", "ffn2_norm", "ffn2_w_gate", "ffn2_w_up", "ffn2_w_down",
             "final_norm"]
    return (loss, dx.reshape(x.shape), *[out_g[k] for k in order], *[out_d[k] for k in order],
            *[out_m[k] for k in order], *[out_v[k] for k in order])
```

```python
import functools
import math

import numpy as np
import jax
import jax.numpy as jnp
from jax import lax
from jax.experimental import pallas as pl
from jax.experimental.pallas import tpu as pltpu

D_MODEL = 1024
D_FF = 2816
FF_CHUNK = 256
HEAD_DIM = 64
BLOCK = 128
N_BUCKETS = 32
MAX_DISTANCE = 2048
A_HEADS = 12
TOTAL_HEADS = 20
DIL_GROUPS = ((128, 1), (512, 4), (2048, 16))
B_WINDOW = 128
QKV_W = 3072
GATE_W = 2048
D_IN = QKV_W + GATE_W
EPS = 1e-6
NEG = -1e30
N_DEV = 8
N_CHIP = 4
ADAM_LR, ADAM_B1, ADAM_B2, ADAM_EPS, ADAM_WD, ADAM_STEP = 0.001, 0.9, 0.999, 1e-08, 0.01, 10
VMEM_LIMIT = 56 * 1024 * 1024
MESH = pl.DeviceIdType.MESH
BF16 = jnp.bfloat16
F32 = jnp.float32
ANY = pl.BlockSpec(memory_space=pl.ANY)


def _params(*sem):
    return pltpu.CompilerParams(dimension_semantics=sem, vmem_limit_bytes=VMEM_LIMIT)


def _resident(a):
    return pl.BlockSpec(a.shape, lambda i: (0, 0), pipeline_mode=pl.Buffered(1))


def _place():
    return lax.axis_index("x"), lax.axis_index("y"), lax.axis_index("c")


class _Gather:
    def __init__(self, shards):
        self.ins = list(shards)
        n = self.n = len(shards)
        self.out_shape = [jax.ShapeDtypeStruct((N_DEV,) + s.shape, s.dtype) for s in shards]
        self.scratch = [pltpu.SemaphoreType.DMA((7 * n,)), pltpu.SemaphoreType.DMA((7 * n,)),
                        pltpu.SemaphoreType.DMA((n,))]

    def _copies(self, ins, outs, sems):
        send_sems, recv_sems, local_sems = sems
        x, y, c = _place()
        me, sibling = (x, y, c), (x, y, 1 - c)
        chips = [(1 - x, y), (x, 1 - y), (1 - x, 1 - y)]

        def copy(i, k, block, to, src=None):
            dst = outs[i].at[4 * block[0] + 2 * block[1] + block[2]]
            return pltpu.make_async_remote_copy(
                src_ref=dst if src is None else src, dst_ref=dst, send_sem=send_sems.at[7 * i + k],
                recv_sem=recv_sems.at[7 * i + k], device_id=to, device_id_type=MESH)

        n = self.n
        mine = [pltpu.make_async_copy(ins[i], outs[i].at[4 * x + 2 * y + c], local_sems.at[i]) for i in range(n)]
        first = [copy(i, 0, me, sibling, src=ins[i]) for i in range(n)]
        first += [copy(i, 1 + j, me, (*chip, c), src=ins[i]) for i in range(n) for j, chip in enumerate(chips)]
        landed = [copy(i, 1 + j, (*chip, c), me) for j, chip in enumerate(chips) for i in range(n)]
        passed = [copy(i, 4 + j, (*chip, c), sibling) for j, chip in enumerate(chips) for i in range(n)]
        from_sibling = [copy(i, 0, sibling, me) for i in range(n)]
        from_sibling += [copy(i, 4 + j, (*chip, 1 - c), me) for i in range(n) for j, chip in enumerate(chips)]
        return mine, first, landed, passed, from_sibling

    def start(self, ins, outs, sems):
        mine, first, _, _, _ = self._copies(ins, outs, sems)
        for cp in mine + first:
            cp.start()

    def mid(self, ins, outs, sems):
        _, _, landed, passed, _ = self._copies(ins, outs, sems)
        for got, fwd in zip(landed, passed):
            got.wait_recv()
            fwd.start()

    def end(self, ins, outs, sems):
        mine, first, _, passed, from_sibling = self._copies(ins, outs, sems)
        for cp in from_sibling:
            cp.wait_recv()
        for cp in first + passed:
            cp.wait_send()
        for cp in mine:
            cp.wait()


class _CoreExchange:
    def __init__(self, grads):
        self.ins = list(grads)
        n = self.n = len(grads)
        self.out_shape = [jax.ShapeDtypeStruct((N_CHIP, 1) + g.shape[2:], g.dtype) for g in grads]
        self.scratch = [pltpu.SemaphoreType.DMA((n,)), pltpu.SemaphoreType.DMA((n,))]

    def _copies(self, ins, outs, sems):
        x, y, c = _place()
        return [pltpu.make_async_remote_copy(
            src_ref=ins[i].at[:, pl.ds(1 - c, 1)], dst_ref=outs[i], send_sem=sems[0].at[i],
            recv_sem=sems[1].at[i], device_id=(x, y, 1 - c), device_id_type=MESH) for i in range(self.n)]

    def start(self, ins, outs, sems):
        for cp in self._copies(ins, outs, sems):
            cp.start()

    mid = None

    def end(self, ins, outs, sems):
        for cp in self._copies(ins, outs, sems):
            cp.wait()


class _ChipExchange:
    def __init__(self, parts):
        self.ins = list(parts)
        n = self.n = len(parts)
        self.out_shape = [jax.ShapeDtypeStruct(p.shape, p.dtype) for p in parts]
        self.scratch = [pltpu.SemaphoreType.DMA((3 * n,)), pltpu.SemaphoreType.DMA((3 * n,)),
                        pltpu.SemaphoreType.DMA((n,))]

    def _copies(self, ins, outs, sems):
        send_sems, recv_sems, local_sems = sems
        x, y, c = _place()
        my_chip = 2 * x + y
        copies = []
        for i in range(self.n):
            copies.append(pltpu.make_async_copy(ins[i].at[my_chip], outs[i].at[my_chip], local_sems.at[i]))
            for k, (qx, qy) in enumerate([(1 - x, y), (x, 1 - y), (1 - x, 1 - y)]):
                copies.append(pltpu.make_async_remote_copy(
                    src_ref=ins[i].at[2 * qx + qy], dst_ref=outs[i].at[my_chip], send_sem=send_sems.at[3 * i + k],
                    recv_sem=recv_sems.at[3 * i + k], device_id=(qx, qy, c), device_id_type=MESH))
        return copies

    def start(self, ins, outs, sems):
        for cp in self._copies(ins, outs, sems):
            cp.start()

    mid = None

    def end(self, ins, outs, sems):
        for cp in self._copies(ins, outs, sems):
            cp.wait()


def _call(body, *, name, grid, in_specs, out_specs, out_shape, args, scratch=(), sem=None, hosted=()):
    n_in, n_out, n_scr = len(in_specs), len(out_specs), len(scratch)
    x_in = [len(p.ins) for p in hosted]
    x_scr = [len(p.scratch) for p in hosted]
    steps = int(np.prod(grid))
    assert not hosted or steps >= 3

    def wrapped(*refs):
        refs = list(refs)
        ins, refs = refs[:n_in], refs[n_in:]
        x_ins = [[refs.pop(0) for _ in range(k)] for k in x_in]
        outs, refs = refs[:n_out], refs[n_out:]
        x_outs = [[refs.pop(0) for _ in range(k)] for k in x_in]
        scr, refs = refs[:n_scr], refs[n_scr:]
        x_sems = [[refs.pop(0) for _ in range(k)] for k in x_scr]
        step = 0
        for d in range(len(grid)):
            step = step * grid[d] + pl.program_id(d)

        def phase(which, at):
            fns = [(getattr(p, which), a) for p, a in zip(hosted, zip(x_ins, x_outs, x_sems)) if getattr(p, which)]
            if fns:
                @pl.when(step == at)
                def _():
                    for fn, a in fns:
                        fn(*a)

        phase("start", 0)
        phase("mid", (3 * steps) // 4)
        body(*ins, *outs, *scr)
        phase("end", steps - 1)

    results = pl.pallas_call(
        wrapped, name=name, grid=grid,
        in_specs=list(in_specs) + [ANY] * sum(x_in), out_specs=list(out_specs) + [ANY] * sum(x_in),
        out_shape=list(out_shape) + [s for p in hosted for s in p.out_shape],
        scratch_shapes=list(scratch) + [s for p in hosted for s in p.scratch],
        compiler_params=_params(*(("arbitrary",) * len(grid) if hosted else sem)),
    )(*args, *[a for p in hosted for a in p.ins])
    own, rest = list(results[:n_out]), list(results[n_out:])
    return own, [[rest.pop(0) for _ in range(k)] for k in x_in]


def _exchange(programs, name):
    x_in = [len(p.ins) for p in programs]
    x_scr = [len(p.scratch) for p in programs]

    def body(*refs):
        refs = list(refs)
        x_ins = [[refs.pop(0) for _ in range(k)] for k in x_in]
        x_outs = [[refs.pop(0) for _ in range(k)] for k in x_in]
        x_sems = [[refs.pop(0) for _ in range(k)] for k in x_scr]
        for which in ("start", "mid", "end"):
            for p, a in zip(programs, zip(x_ins, x_outs, x_sems)):
                if getattr(p, which):
                    getattr(p, which)(*a)

    results = list(pl.pallas_call(
        body, name=name, in_specs=[ANY] * sum(x_in), out_specs=[ANY] * sum(x_in),
        out_shape=[s for p in programs for s in p.out_shape],
        scratch_shapes=[s for p in programs for s in p.scratch],
    )(*[a for p in programs for a in p.ins]))
    return [[results.pop(0) for _ in range(k)] for k in x_in]


def _nt(a, b):
    return lax.dot_general(a, b, (((1,), (1,)), ((), ())), preferred_element_type=F32)


def _nn(a, b):
    return lax.dot_general(a, b, (((1,), (0,)), ((), ())), preferred_element_type=F32)


def _tn(a, b):
    return lax.dot_general(a, b, (((0,), (0,)), ((), ())), preferred_element_type=F32)


def _rms(x, g):
    r = lax.rsqrt(jnp.mean(x * x, axis=-1, keepdims=True) + EPS)
    return x * r, r


def _rms_bwd(dn, xhat, r, g):
    dg = jnp.sum(dn * xhat, axis=0, keepdims=True)
    dxh = dn * g
    dx = r * (dxh - xhat * jnp.mean(dxh * xhat, axis=-1, keepdims=True))
    return dx, dg


def _ffn_fwd(x, g, wg_t, wu_t, wd, name, hosted=()):
    t = x.shape[0]
    tm = 512

    def body(x_ref, g_ref, wg_ref, wu_ref, wd_ref, h_ref, n_ref, hff_ref):
        xhat, _ = _rms(x_ref[...], g_ref[...])
        n = (xhat * g_ref[...]).astype(BF16)
        n_ref[...] = n
        for c in range(0, D_FF, FF_CHUNK):
            a = _nt(n, wg_ref[c:c + FF_CHUNK, :])
            b = _nt(n, wu_ref[c:c + FF_CHUNK, :])
            hff_ref[:, c:c + FF_CHUNK] = (a * jax.nn.sigmoid(a) * b).astype(BF16)
        h_ref[...] = x_ref[...] + 0.5 * _nn(hff_ref[...], wd_ref[...])

    row = pl.BlockSpec((tm, D_MODEL), lambda i: (i, 0))
    return _call(
        body, name=name, grid=(t // tm,),
        in_specs=[row, _resident(g), _resident(wg_t), _resident(wu_t), _resident(wd)],
        out_specs=[row, row],
        out_shape=[jax.ShapeDtypeStruct((t, D_MODEL), F32), jax.ShapeDtypeStruct((t, D_MODEL), BF16)],
        scratch=[pltpu.VMEM((tm, D_FF), BF16)], sem=("parallel",), args=(x, g, wg_t, wu_t, wd), hosted=hosted)


def _ffn_bwd(x, n, g, dh, wg_t, wu_t, wd, name):
    t = x.shape[0]
    tm = 256

    def body(x_ref, n_ref, g_ref, dh_ref, wg_ref, wu_ref, wd_ref,
             dx_ref, dg_ref, hff_ref, da_ref, db_ref, dhh_ref):
        @pl.when(pl.program_id(0) == 0)
        def _():
            dg_ref[...] = jnp.zeros_like(dg_ref)

        dhh = (0.5 * dh_ref[...]).astype(BF16)
        dhh_ref[...] = dhh
        nb = n_ref[...]
        for c in range(0, D_FF, FF_CHUNK):
            cols = slice(c, c + FF_CHUNK)
            a = _nt(nb, wg_ref[cols, :])
            b = _nt(nb, wu_ref[cols, :])
            s = jax.nn.sigmoid(a)
            silu = a * s
            dhff = _nt(dhh, wd_ref[cols, :])
            hff_ref[:, cols] = (silu * b).astype(BF16)
            da_ref[:, cols] = (dhff * b * (s * (1.0 + a * (1.0 - s)))).astype(BF16)
            db_ref[:, cols] = (dhff * silu).astype(BF16)
        dn = _nn(da_ref[...], wg_ref[...]) + _nn(db_ref[...], wu_ref[...])
        xhat, r = _rms(x_ref[...], g_ref[...])
        dx, dg = _rms_bwd(dn, xhat, r, g_ref[...])
        dx_ref[...] = dh_ref[...] + dx
        dg_ref[...] += dg

    row = pl.BlockSpec((tm, D_MODEL), lambda i: (i, 0))
    hid = pl.BlockSpec((tm, D_FF), lambda i: (i, 0))
    return pl.pallas_call(
        body, name=name, grid=(t // tm,),
        in_specs=[row, row, _resident(g), row, _resident(wg_t), _resident(wu_t), _resident(wd)],
        out_specs=[row, pl.BlockSpec((1, D_MODEL), lambda i: (0, 0)), hid, hid, hid, row],
        out_shape=[jax.ShapeDtypeStruct((t, D_MODEL), F32), jax.ShapeDtypeStruct((1, D_MODEL), F32),
                   jax.ShapeDtypeStruct((t, D_FF), BF16), jax.ShapeDtypeStruct((t, D_FF), BF16),
                   jax.ShapeDtypeStruct((t, D_FF), BF16), jax.ShapeDtypeStruct((t, D_MODEL), BF16)],
        compiler_params=_params("arbitrary"),
    )(x, n, g, dh, wg_t, wu_t, wd)


def _tn_matmul(a, b, rc, name, hosted=()):
    t, r = a.shape
    c = b.shape[1]
    tk = 1024

    def body(a_ref, b_ref, o_ref):
        @pl.when(pl.program_id(1) == 0)
        def _():
            o_ref[...] = jnp.zeros_like(o_ref)

        o_ref[...] += _tn(a_ref[...].astype(BF16), b_ref[...].astype(BF16))

    (out,), got = _call(
        body, name=name, grid=(r // rc, t // tk),
        in_specs=[pl.BlockSpec((tk, rc), lambda i, k: (k, i)), pl.BlockSpec((tk, c), lambda i, k: (k, 0))],
        out_specs=[pl.BlockSpec((rc, c), lambda i, k: (i, 0))],
        out_shape=[jax.ShapeDtypeStruct((r, c), F32)],
        sem=("parallel", "arbitrary"), args=(a, b), hosted=hosted)
    return out, got


PIECE_W = (256,) * 9 + (512, 128, 128, GATE_W)


def _inproj_fwd(h, g, w_t, b_in, name, hosted=()):
    t = h.shape[0]
    tm, nc = 512, 512

    def body(h_ref, g_ref, w_ref, b_ref, u_ref, zq_ref, zg_ref):
        xhat, _ = _rms(h_ref[...], g_ref[...])
        u = (xhat * g_ref[...]).astype(BF16)
        u_ref[...] = u
        for c in range(D_IN // nc):
            z = _nt(u, w_ref[c * nc:(c + 1) * nc, :]) + b_ref[:, c * nc:(c + 1) * nc]
            if c < QKV_W // nc:
                zq_ref[:, c * nc:(c + 1) * nc] = z.astype(BF16)
            else:
                zg_ref[:, c * nc - QKV_W:(c + 1) * nc - QKV_W] = z

    row = lambda w: pl.BlockSpec((tm, w), lambda i: (i, 0))
    full = _resident
    return _call(
        body, name=name, grid=(t // tm,),
        in_specs=[row(D_MODEL), full(g), full(w_t), full(b_in)],
        out_specs=[row(D_MODEL), row(QKV_W), row(GATE_W)],
        out_shape=[jax.ShapeDtypeStruct((t, D_MODEL), BF16), jax.ShapeDtypeStruct((t, QKV_W), BF16),
                   jax.ShapeDtypeStruct((t, GATE_W), F32)],
        sem=("parallel",), args=(h, g, w_t, b_in), hosted=hosted)


def _inproj_bwd(pieces, h, g, dh_res, w_t, name):
    t = h.shape[0]
    tm = 256
    npiece = len(PIECE_W)
    offs = np.concatenate([[0], np.cumsum(PIECE_W)]).tolist()

    def body(*refs):
        p_refs = refs[:npiece]
        h_ref, g_ref, dhr_ref, w_ref, dh_ref, dz_ref, db_ref, dg_ref = refs[npiece:]
        i = pl.program_id(0)

        @pl.when(i == 0)
        def _():
            db_ref[...] = jnp.zeros_like(db_ref)
            dg_ref[...] = jnp.zeros_like(dg_ref)

        du = jnp.zeros((tm, D_MODEL), F32)
        for k in range(npiece):
            o, w = offs[k], PIECE_W[k]
            for c0 in range(0, w, 512):
                cw = min(512, w - c0)
                pz = p_refs[k][:, c0:c0 + cw]
                dz_ref[:, o + c0:o + c0 + cw] = pz
                db_ref[:, o + c0:o + c0 + cw] += jnp.sum(pz.astype(F32), axis=0, keepdims=True)
                du = du + _nn(pz, w_ref[o + c0:o + c0 + cw, :])
        xhat, r = _rms(h_ref[...], g_ref[...])
        dx, dg = _rms_bwd(du, xhat, r, g_ref[...])
        dh_ref[...] = dhr_ref[...] + dx
        dg_ref[...] += dg

    row = lambda w: pl.BlockSpec((tm, w), lambda i: (i, 0))
    full = lambda shp: pl.BlockSpec(shp, lambda i: (0, 0))
    return pl.pallas_call(
        body, name=name, grid=(t // tm,),
        in_specs=[row(w) for w in PIECE_W] + [row(D_MODEL), _resident(g), row(D_MODEL), _resident(w_t)],
        out_specs=[row(D_MODEL), row(D_IN), full((1, D_IN)), full((1, D_MODEL))],
        out_shape=[jax.ShapeDtypeStruct((t, D_MODEL), F32), jax.ShapeDtypeStruct((t, D_IN), BF16),
                   jax.ShapeDtypeStruct((1, D_IN), F32), jax.ShapeDtypeStruct((1, D_MODEL), F32)],
        compiler_params=_params("arbitrary"),
    )(*pieces, h, g, dh_res, w_t)


def _t5_bucket(dist):
    max_exact = N_BUCKETS // 2
    n = jnp.maximum(dist, 0)
    nf = jnp.maximum(n, 1).astype(jnp.float32)
    large = max_exact + (jnp.log(nf / max_exact) / math.log(MAX_DISTANCE / max_exact)
                         * (N_BUCKETS - max_exact)).astype(jnp.int32)
    large = jnp.minimum(large, N_BUCKETS - 1)
    return jnp.where(n < max_exact, n, large)


ATT_CFG = ((1, 128, 0, 4), (4, 128, 4, 4), (16, 128, 8, 4), (1, B_WINDOW - 1, A_HEADS, 8))


def _bucket_tiles():
    qi = jnp.arange(BLOCK)[:, None]
    ki = jnp.arange(2 * BLOCK)[None, :]
    dist = qi + BLOCK - ki
    return jnp.stack([_t5_bucket(dist * cfg[0]) for cfg in ATT_CFG]).astype(jnp.int32)


def _band(max_steps):
    row = lax.broadcasted_iota(jnp.int32, (BLOCK, 2 * BLOCK), 0)
    col = lax.broadcasted_iota(jnp.int32, (BLOCK, 2 * BLOCK), 1)
    dist = row + BLOCK - col
    return (dist >= 0) & (dist <= max_steps)


def _bias_build(table, buckets):
    def body(tab_ref, bt_ref, out_ref):
        col = lax.broadcasted_iota(jnp.int32, (BLOCK, 2 * BLOCK), 1)
        for ci, (_, max_steps, h0, nh) in enumerate(ATT_CFG):
            bt = bt_ref[ci]
            band = _band(max_steps)
            for h in range(h0, h0 + nh):
                acc = lax.fori_loop(0, N_BUCKETS, lambda b, acc: jnp.where(bt == b, tab_ref[b, h], acc),
                                    jnp.zeros((BLOCK, 2 * BLOCK), F32))
                out_ref[0, BLOCK * h:BLOCK * (h + 1), :] = jnp.where(band & (col >= BLOCK), acc, NEG)
                out_ref[1, BLOCK * h:BLOCK * (h + 1), :] = jnp.where(band, acc, NEG)

    return pl.pallas_call(
        body, name="bias_build",
        in_specs=[pl.BlockSpec(memory_space=pltpu.SMEM), pl.BlockSpec(memory_space=pltpu.VMEM)],
        out_specs=pl.BlockSpec(memory_space=pltpu.VMEM),
        out_shape=jax.ShapeDtypeStruct((2, TOTAL_HEADS * BLOCK, 2 * BLOCK), F32),
    )(table, buckets)


def _bias_reduce(dbias, buckets, dsink_rows):
    def body(db_ref, bt_ref, ds_ref, out_ref, sink_ref):
        ri = lax.broadcasted_iota(jnp.int32, (N_BUCKETS, 128), 0)
        ci = lax.broadcasted_iota(jnp.int32, (N_BUCKETS, 128), 1)

        def per_bucket(b, acc):
            for cfg_i, (_, _, h0, nh) in enumerate(ATT_CFG):
                hit = bt_ref[cfg_i] == b
                for h in range(h0, h0 + nh):
                    val = jnp.sum(jnp.where(hit, db_ref[BLOCK * h:BLOCK * (h + 1), :], 0.0))
                    acc = jnp.where((ri == b) & (ci == h), val, acc)
            return acc

        out_ref[...] = lax.fori_loop(0, N_BUCKETS, per_bucket, jnp.zeros((N_BUCKETS, 128), F32))
        for h in range(8):
            sink_ref[h:h + 1, :] = jnp.sum(ds_ref[BLOCK * h:BLOCK * (h + 1), :], axis=0, keepdims=True)

    return pl.pallas_call(
        body, name="bias_reduce",
        in_specs=[pl.BlockSpec(memory_space=pltpu.VMEM)] * 3,
        out_specs=[pl.BlockSpec(memory_space=pltpu.VMEM)] * 2,
        out_shape=[jax.ShapeDtypeStruct((N_BUCKETS, 128), F32), jax.ShapeDtypeStruct((8, 128), F32)],
    )(dbias, buckets, dsink_rows)


class _Att:
    def __init__(self, cfg_i):
        stride, _, h0, nh = ATT_CFG[cfg_i]
        self.d = stride if cfg_i < 3 else 1
        self.h0, self.nh = h0, nh
        self.row_w = QKV_W if self.d == 1 else 3 * 256
        if cfg_i < 3:
            self.nq, self.wkv = 1, 256
            self.q_unit = [cfg_i if self.d == 1 else 0]
            self.k_unit, self.v_unit = (3 + cfg_i, 6 + cfg_i) if self.d == 1 else (1, 2)
            self.sinks = False
        else:
            self.nq, self.wkv = 2, 128
            self.q_unit = [9, 10]
            self.k_unit, self.v_unit = 22, 23
            self.sinks = True
        self.wq = 256 * self.nq


def _att_in_specs(cf, bsz):
    uq, ukv = cf.row_w // 256, cf.row_w // cf.wkv
    specs = [pl.BlockSpec((bsz, BLOCK, 256), functools.partial(lambda r, j, u: (0, j, r * uq + u), u=u))
             for u in cf.q_unit]
    for unit in (cf.k_unit, cf.v_unit):
        specs.append(pl.BlockSpec((bsz, BLOCK, cf.wkv),
                                  functools.partial(lambda r, j, u: (0, jnp.maximum(j - 1, 0), r * ukv + u), u=unit)))
        specs.append(pl.BlockSpec((bsz, BLOCK, cf.wkv),
                                  functools.partial(lambda r, j, u: (0, j, r * ukv + u), u=unit)))
    for qb in range(cf.nq):
        specs.append(pl.BlockSpec((None, HEADS_PER_BLOCK * BLOCK, 2 * BLOCK), functools.partial(
            lambda r, j, u: (jnp.minimum(j, 1), u, 0), u=cf.h0 // HEADS_PER_BLOCK + qb)))
    if cf.sinks:
        specs += [pl.BlockSpec((HEADS_PER_BLOCK * BLOCK, 128), functools.partial(lambda r, j, u: (u, 0), u=qb))
                  for qb in range(cf.nq)]
    return specs


HEADS_PER_BLOCK = 4


def _head_masks(rows):
    head = lax.broadcasted_iota(jnp.int32, (rows, 256), 1) // HEAD_DIM
    return [head == h for h in range(HEADS_PER_BLOCK)]


def _stack_heads(x, masks):
    return jnp.concatenate([jnp.where(m, x, jnp.zeros_like(x)) for m in masks], axis=0)


def _unstack_heads(x4, masks):
    blocks = [x4[BLOCK * h:BLOCK * (h + 1)] for h in range(HEADS_PER_BLOCK)]
    return jnp.where(masks[0], blocks[0], jnp.where(masks[1], blocks[1], jnp.where(masks[2], blocks[2], blocks[3])))


def _row_value(x):
    return jnp.max(x, axis=-1, keepdims=True)


def _kv_operand(cf, x, qb):
    if cf.wkv == 256:
        return x
    lane = lax.broadcasted_iota(jnp.int32, x.shape, 1)
    swapped = pltpu.roll(x, HEAD_DIM, 1)
    half = jnp.where(lane < HEAD_DIM, x, swapped) if qb == 0 else jnp.where(lane < HEAD_DIM, swapped, x)
    return jnp.concatenate([half, half], axis=1)


def _kv_fold(cf, grads):
    if cf.wkv == 256:
        return grads[0]
    folded = []
    for g in grads:
        x = g[:, :128] + g[:, 128:]
        folded.append(x + pltpu.roll(x, HEAD_DIM, 1))
    lane = lax.broadcasted_iota(jnp.int32, folded[0].shape, 1)
    return jnp.where(lane < HEAD_DIM, folded[0], folded[1])


def _attn_fwd(cf, zf, bias, sinks, name, hosted=()):
    bsz, l, _ = zf.shape
    nb = l // BLOCK

    def body(*refs):
        refs = list(refs)
        q_refs = [refs.pop(0) for _ in range(cf.nq)]
        kp_ref, kc_ref, vp_ref, vc_ref = [refs.pop(0) for _ in range(4)]
        bias_refs = [refs.pop(0) for _ in range(cf.nq)]
        sink_refs = [refs.pop(0) for _ in range(cf.nq)] if cf.sinks else None
        o_ref, lse_ref = refs
        masks, kv_masks = _head_masks(BLOCK), _head_masks(2 * BLOCK)
        sinks4 = [_row_value(ref[...]) for ref in sink_refs] if cf.sinks else None
        for bi in range(bsz):
            k = jnp.concatenate([kp_ref[bi], kc_ref[bi]], axis=0)
            v = jnp.concatenate([vp_ref[bi], vc_ref[bi]], axis=0)
            for qb in range(cf.nq):
                cols = slice(256 * qb, 256 * (qb + 1))
                kb, vb = _kv_operand(cf, k, qb), _kv_operand(cf, v, qb)
                q4 = _stack_heads(q_refs[qb][bi] * (HEAD_DIM ** -0.5), masks)
                s = _nt(q4, kb) + bias_refs[qb][...]
                m = jnp.max(s, axis=-1, keepdims=True)
                if cf.sinks:
                    sk = sinks4[qb]
                    m = jnp.maximum(m, sk)
                p = jnp.exp(s - m)
                den = jnp.sum(p, axis=-1, keepdims=True)
                if cf.sinks:
                    den = den + jnp.exp(sk - m)
                pn = (p * (1.0 / den)).astype(BF16)
                p_lanes = jnp.concatenate([pn[BLOCK * h:BLOCK * (h + 1)] for h in range(HEADS_PER_BLOCK)], axis=1)
                v4 = jnp.concatenate([jnp.where(mk, vb, jnp.zeros_like(vb)) for mk in kv_masks], axis=0)
                o_ref[bi, :, cols] = _nn(p_lanes, v4)
                lse_ref[bi, :, cols] = _unstack_heads(
                    jnp.broadcast_to(m + jnp.log(den), (HEADS_PER_BLOCK * BLOCK, 256)), masks)

    in_specs = _att_in_specs(cf, bsz)
    args = [zf] * (cf.nq + 4) + [bias] * cf.nq + ([sinks] * cf.nq if cf.sinks else [])
    out = pl.BlockSpec((bsz, BLOCK, cf.wq), lambda r, j: (0, j, r))
    shape = jax.ShapeDtypeStruct((bsz, l, cf.d * cf.wq), F32)
    return _call(
        body, name=name, grid=(cf.d, nb), in_specs=in_specs, out_specs=[out, out], out_shape=[shape, shape],
        sem=("parallel", "arbitrary"), args=args, hosted=hosted)


def _attn_bwd(cf, zf, bias, sinks, lse_tot, dy, y, name, hosted=()):
    bsz, l, _ = zf.shape
    nb = l // BLOCK

    def body(*refs):
        refs = list(refs)
        q_refs = [refs.pop(0) for _ in range(cf.nq)]
        kp_ref, kc_ref, vp_ref, vc_ref = [refs.pop(0) for _ in range(4)]
        bias_refs = [refs.pop(0) for _ in range(cf.nq)]
        sink_refs = [refs.pop(0) for _ in range(cf.nq)] if cf.sinks else None
        lt_ref, dy_ref, y_ref, dq_ref, dk_ref, dv_ref, dbias_ref = [refs.pop(0) for _ in range(7)]
        dsink_ref = refs.pop(0) if cf.sinks else None
        dk_acc, dv_acc = refs
        r, j = pl.program_id(0), pl.program_id(1)

        @pl.when((r == 0) & (j == 0))
        def _():
            dbias_ref[...] = jnp.zeros_like(dbias_ref)
            if cf.sinks:
                dsink_ref[...] = jnp.zeros_like(dsink_ref)

        @pl.when(j == 0)
        def _():
            dk_acc[...] = jnp.zeros_like(dk_acc)
            dv_acc[...] = jnp.zeros_like(dv_acc)

        masks = _head_masks(BLOCK)
        cur = pl.ds(pl.multiple_of(j * BLOCK, BLOCK), BLOCK)
        prev = pl.ds(pl.multiple_of(jnp.maximum(j - 1, 0) * BLOCK, BLOCK), BLOCK)
        sinks4 = [_row_value(ref[...]) for ref in sink_refs] if cf.sinks else None
        for bi in range(bsz):
            k = jnp.concatenate([kp_ref[bi], kc_ref[bi]], axis=0)
            v = jnp.concatenate([vp_ref[bi], vc_ref[bi]], axis=0)
            dk_blocks, dv_blocks = [], []
            for qb in range(cf.nq):
                cols = slice(256 * qb, 256 * (qb + 1))
                rows = slice(HEADS_PER_BLOCK * BLOCK * qb, HEADS_PER_BLOCK * BLOCK * (qb + 1))
                kb, vb = _kv_operand(cf, k, qb), _kv_operand(cf, v, qb)
                q4 = _stack_heads(q_refs[qb][bi] * (HEAD_DIM ** -0.5), masks)
                lt = lt_ref[bi, :, cols]
                lt4 = jnp.concatenate([_row_value(jnp.where(mk, lt, NEG)) for mk in masks], axis=0)
                pa = jnp.exp(_nt(q4, kb) + bias_refs[qb][...] - lt4)
                dy = dy_ref[bi, :, cols]
                prod = dy * y_ref[bi, :, cols]
                e4 = jnp.concatenate([jnp.sum(jnp.where(mk, prod, 0.0), axis=-1, keepdims=True) for mk in masks],
                                     axis=0)
                dy4 = _stack_heads(dy.astype(BF16), masks)
                ds = pa * (_nt(dy4, vb) - e4)
                dbias_ref[rows, :] += ds
                if cf.sinks:
                    dsink_ref[rows, :] += -(jnp.exp(sinks4[qb] - lt4) * e4)
                dsb = ds.astype(BF16)
                dq_ref[bi, :, cols] = (_unstack_heads(_nn(dsb, kb), masks) * (HEAD_DIM ** -0.5)).astype(BF16)
                dk_blocks.append(_tn(dsb, q4))
                dv_blocks.append(_tn(pa.astype(BF16), dy4))
            dk_new, dv_new = _kv_fold(cf, dk_blocks), _kv_fold(cf, dv_blocks)
            dk_acc[bi, cur, :] += dk_new[BLOCK:]
            dv_acc[bi, cur, :] += dv_new[BLOCK:]
            dk_acc[bi, prev, :] += dk_new[:BLOCK]
            dv_acc[bi, prev, :] += dv_new[:BLOCK]

        @pl.when(j == nb - 1)
        def _():
            dk_ref[...] = dk_acc[...].astype(BF16)
            dv_ref[...] = dv_acc[...].astype(BF16)

    tok = pl.BlockSpec((bsz, BLOCK, cf.wq), lambda r, j: (0, j, r))
    in_specs = _att_in_specs(cf, bsz) + [tok] * 3
    args = [zf] * (cf.nq + 4) + [bias] * cf.nq + ([sinks] * cf.nq if cf.sinks else []) + [lse_tot, dy, y]
    seq = pl.BlockSpec((bsz, l, cf.wkv), lambda r, j: (0, 0, r))
    out_specs = [tok, seq, seq, pl.BlockSpec((cf.nh * BLOCK, 2 * BLOCK), lambda r, j: (0, 0))]
    out_shape = [jax.ShapeDtypeStruct((bsz, l, cf.d * cf.wq), BF16),
                 jax.ShapeDtypeStruct((bsz, l, cf.d * cf.wkv), BF16),
                 jax.ShapeDtypeStruct((bsz, l, cf.d * cf.wkv), BF16),
                 jax.ShapeDtypeStruct((cf.nh * BLOCK, 2 * BLOCK), F32)]
    if cf.sinks:
        out_specs.append(pl.BlockSpec((cf.nh * BLOCK, 128), lambda r, j: (0, 0)))
        out_shape.append(jax.ShapeDtypeStruct((cf.nh * BLOCK, 128), F32))
    return _call(
        body, name=name, grid=(cf.d, nb), in_specs=in_specs, out_specs=out_specs, out_shape=out_shape,
        scratch=[pltpu.VMEM((bsz, l, cf.wkv), F32), pltpu.VMEM((bsz, l, cf.wkv), F32)],
        sem=("arbitrary", "arbitrary"), args=args, hosted=hosted)


def _merge_fwd(o_a, lse_a, o_b, zg, h, wa_t, wb_t, wout, name):
    t = h.shape[0]
    tm = 512

    def body(o1, o2, o3, l1, l2, l3, ob_ref, zg_ref, h_ref, wa_ref, wb_ref, wo_ref, h2_ref, ya_ref, lt_ref):
        m = jnp.maximum(jnp.maximum(l1[...], l2[...]), l3[...])
        e1, e2, e3 = jnp.exp(l1[...] - m), jnp.exp(l2[...] - m), jnp.exp(l3[...] - m)
        se = e1 + e2 + e3
        ya = (e1 / se) * o1[...] + (e2 / se) * o2[...] + (e3 / se) * o3[...]
        ya_ref[...] = ya
        lt_ref[...] = m + jnp.log(se)
        pa = _nt(ya.astype(BF16), wa_ref[...])
        pb = _nt(ob_ref[...].astype(BF16), wb_ref[...])
        merged = jax.nn.sigmoid(zg_ref[:, :D_MODEL]) * pa + jax.nn.sigmoid(zg_ref[:, D_MODEL:]) * pb
        h2_ref[...] = h_ref[...] + _nn(merged.astype(BF16), wo_ref[...])

    row = lambda w: pl.BlockSpec((tm, w), lambda i: (i, 0))
    full = _resident
    return pl.pallas_call(
        body, name=name, grid=(t // tm,),
        in_specs=[row(256)] * 6 + [row(512), row(GATE_W), row(D_MODEL), full(wa_t), full(wb_t), full(wout)],
        out_specs=[row(D_MODEL), row(256), row(256)],
        out_shape=[jax.ShapeDtypeStruct((t, D_MODEL), F32), jax.ShapeDtypeStruct((t, 256), F32),
                   jax.ShapeDtypeStruct((t, 256), F32)],
        compiler_params=_params("parallel"),
    )(*o_a, *lse_a, o_b, zg, h, wa_t, wb_t, wout)


def _merge_bwd(dh, ya, o_b, zg, wa_t, wb_t, wout, name, hosted=()):
    t = dh.shape[0]
    tm = 256

    def body(dh_ref, ya_ref, ob_ref, zg_ref, wa_ref, wb_ref, wo_ref,
             mg_ref, dpa_ref, dpb_ref, dzg_ref, dya_ref, dyb_ref):
        dm = _nt(dh_ref[...].astype(BF16), wo_ref[...])
        pa = _nt(ya_ref[...].astype(BF16), wa_ref[...])
        pb = _nt(ob_ref[...].astype(BF16), wb_ref[...])
        sa = jax.nn.sigmoid(zg_ref[:, :D_MODEL])
        sb = jax.nn.sigmoid(zg_ref[:, D_MODEL:])
        mg_ref[...] = (sa * pa + sb * pb).astype(BF16)
        dpa = (dm * sa).astype(BF16)
        dpb = (dm * sb).astype(BF16)
        dpa_ref[...] = dpa
        dpb_ref[...] = dpb
        dzg_ref[:, :D_MODEL] = (dm * pa * (sa * (1.0 - sa))).astype(BF16)
        dzg_ref[:, D_MODEL:] = (dm * pb * (sb * (1.0 - sb))).astype(BF16)
        dya_ref[...] = _nn(dpa, wa_ref[...])
        dyb_ref[...] = _nn(dpb, wb_ref[...])

    row = lambda w: pl.BlockSpec((tm, w), lambda i: (i, 0))
    full = _resident
    sds = jax.ShapeDtypeStruct
    return _call(
        body, name=name, grid=(t // tm,),
        in_specs=[row(D_MODEL), row(256), row(512), row(GATE_W), full(wa_t), full(wb_t), full(wout)],
        out_specs=[row(D_MODEL), row(D_MODEL), row(D_MODEL), row(GATE_W), row(256), row(512)],
        out_shape=[sds((t, D_MODEL), BF16), sds((t, D_MODEL), BF16), sds((t, D_MODEL), BF16),
                   sds((t, GATE_W), BF16), sds((t, 256), F32), sds((t, 512), F32)],
        sem=("parallel",), args=(dh, ya, o_b, zg, wa_t, wb_t, wout), hosted=hosted)


def _loss_head(h, g, target, name):
    t = h.shape[0]
    tm = 1024

    def body(h_ref, g_ref, t_ref, dh_ref, loss_ref, dg_ref):
        @pl.when(pl.program_id(0) == 0)
        def _():
            loss_ref[...] = jnp.zeros_like(loss_ref)
            dg_ref[...] = jnp.zeros_like(dg_ref)

        xhat, r = _rms(h_ref[...], g_ref[...])
        err = xhat * g_ref[...] - t_ref[...]
        loss_ref[...] += 0.5 * jnp.sum(jnp.mean(err * err, axis=-1, keepdims=True), axis=0, keepdims=True)
        dx, dg = _rms_bwd(err * (1.0 / D_MODEL), xhat, r, g_ref[...])
        dh_ref[...] = dx
        dg_ref[...] += dg

    row = pl.BlockSpec((tm, D_MODEL), lambda i: (i, 0))
    vec = pl.BlockSpec((1, D_MODEL), lambda i: (0, 0))
    return pl.pallas_call(
        body, name=name, grid=(t // tm,),
        in_specs=[row, vec, row], out_specs=[row, pl.BlockSpec((1, 128), lambda i: (0, 0)), vec],
        out_shape=[jax.ShapeDtypeStruct((t, D_MODEL), F32), jax.ShapeDtypeStruct((1, 128), F32),
                   jax.ShapeDtypeStruct((1, D_MODEL), F32)],
        compiler_params=_params("arbitrary"),
    )(h, g, target)


def _pair_sum(grad, got, name):
    _, _, r, cdim = grad.shape
    core = lax.axis_index("c").astype(jnp.int32).reshape(1)

    def body(core_ref, g_ref, s_ref, o_ref):
        o_ref[...] = (g_ref[...] + s_ref[...]).astype(BF16)

    return pl.pallas_call(
        body, name=name,
        grid_spec=pltpu.PrefetchScalarGridSpec(
            num_scalar_prefetch=1, grid=(N_CHIP,),
            in_specs=[pl.BlockSpec((None, None, r, cdim), lambda q, core_ref: (q, core_ref[0], 0, 0)),
                      pl.BlockSpec((None, None, r, cdim), lambda q, core_ref: (q, 0, 0, 0))],
            out_specs=pl.BlockSpec((None, r, cdim), lambda q, core_ref: (q, 0, 0))),
        out_shape=jax.ShapeDtypeStruct((N_CHIP, r, cdim), BF16),
        compiler_params=_params("parallel"),
    )(core, grad, got)


def _chip_sum(parts, name):
    def body(p_ref, o_ref):
        o_ref[...] = ((p_ref[0].astype(F32) + p_ref[1].astype(F32)) + p_ref[2].astype(F32)) + p_ref[3].astype(F32)

    return pl.pallas_call(
        body, name=name, out_shape=jax.ShapeDtypeStruct(parts.shape[1:], F32),
        compiler_params=pltpu.CompilerParams(vmem_limit_bytes=VMEM_LIMIT),
    )(parts)


def _adamw_math(w, g, m, v):
    m = ADAM_B1 * m + (1.0 - ADAM_B1) * g
    v = ADAM_B2 * v + (1.0 - ADAM_B2) * jnp.square(g)
    m_hat = m / (1.0 - ADAM_B1 ** ADAM_STEP)
    v_hat = v / (1.0 - ADAM_B2 ** ADAM_STEP)
    delta = -ADAM_LR * (m_hat / (jnp.sqrt(v_hat) + ADAM_EPS) + ADAM_WD * w)
    return delta, m, v


def _adamw(w, g, m, v, name):
    def body(w_ref, g_ref, m_ref, v_ref, d_ref, m2_ref, v2_ref):
        d_ref[...], m2_ref[...], v2_ref[...] = _adamw_math(w_ref[...], g_ref[...], m_ref[...], v_ref[...])

    sds = jax.ShapeDtypeStruct(w.shape, F32)
    return pl.pallas_call(body, name=name, out_shape=[sds, sds, sds],
                          compiler_params=pltpu.CompilerParams(vmem_limit_bytes=VMEM_LIMIT))(w, g, m, v)


SMALL_ROWS = 80


def _small_update(g, w, m, v, name):
    def body(g_ref, w_ref, m_ref, v_ref, gs_ref, d_ref, m2_ref, v2_ref, got_ref, send_sems, recv_sems):
        x, y, c = _place()
        me = 4 * x + 2 * y + c
        got_ref[me] = g_ref[...]
        copies = []
        for k in range(1, N_DEV):
            peer = (x ^ (k >> 2), y ^ ((k >> 1) & 1), c ^ (k & 1))
            cp = pltpu.make_async_remote_copy(
                src_ref=g_ref, dst_ref=got_ref.at[me], send_sem=send_sems.at[k - 1], recv_sem=recv_sems.at[k - 1],
                device_id=peer, device_id_type=MESH)
            cp.start()
            copies.append(cp)
        for cp in copies:
            cp.wait()
        total = got_ref[0]
        for k in range(1, N_DEV):
            total = total + got_ref[k]
        gs_ref[...] = total
        d_ref[...], m2_ref[...], v2_ref[...] = _adamw_math(w_ref[...], total, m_ref[...], v_ref[...])

    sds = jax.ShapeDtypeStruct((SMALL_ROWS, 128), F32)
    vm = pl.BlockSpec(memory_space=pltpu.VMEM)
    return pl.pallas_call(
        body, name=name, in_specs=[vm] * 4, out_specs=[vm] * 4, out_shape=[sds] * 4,
        scratch_shapes=[pltpu.VMEM((N_DEV, SMALL_ROWS, 128), F32), pltpu.SemaphoreType.DMA((N_DEV - 1,)),
                        pltpu.SemaphoreType.DMA((N_DEV - 1,))],
    )(g, w, m, v)


def _pack_small(gains, b_in, rel_bias, sinks, last):
    rows = [a.reshape(8, 128) for a in gains] + [b_in.reshape(40, 128), rel_bias.reshape(5, 128),
                                                 jnp.pad(sinks.reshape(1, 8), ((0, 0), (0, 120))), last]
    rows.append(jnp.zeros((SMALL_ROWS - 79, 128), F32))
    return jnp.concatenate(rows, axis=0)


def _unpack_small(p, like):
    out = [p[8 * i:8 * i + 8].reshape(like[i].shape) for i in range(4)]
    out.append(p[32:72].reshape(like[4].shape))
    out.append(p[72:77].reshape(like[5].shape))
    out.append(p[77, :8].reshape(like[6].shape))
    return out


def kernel(x, ffn1_norm, ffn1_w_gate, ffn1_w_up, ffn1_w_down, mix_norm, w_in, b_in, w_branch_a, w_branch_b, w_out, sinks, rel_bias, ffn2_norm, ffn2_w_gate, ffn2_w_up, ffn2_w_down, final_norm, loss_target, m_ffn1_norm, m_ffn1_w_gate, m_ffn1_w_up, m_ffn1_w_down, m_mix_norm, m_w_in, m_b_in, m_w_branch_a, m_w_branch_b, m_w_out, m_sinks, m_rel_bias, m_ffn2_norm, m_ffn2_w_gate, m_ffn2_w_up, m_ffn2_w_down, m_final_norm, v_ffn1_norm, v_ffn1_w_gate, v_ffn1_w_up, v_ffn1_w_down, v_mix_norm, v_w_in, v_b_in, v_w_branch_a, v_w_branch_b, v_w_out, v_sinks, v_rel_bias, v_ffn2_norm, v_ffn2_w_gate, v_ffn2_w_up, v_ffn2_w_down, v_final_norm):
    bsz, seq, _ = x.shape
    t = bsz * seq
    xt = x.reshape(t, D_MODEL)
    target = loss_target.reshape(t, D_MODEL)

    big = [("ffn1_w_gate", ffn1_w_gate, m_ffn1_w_gate, v_ffn1_w_gate, True),
           ("ffn1_w_up", ffn1_w_up, m_ffn1_w_up, v_ffn1_w_up, True),
           ("ffn1_w_down", ffn1_w_down, m_ffn1_w_down, v_ffn1_w_down, False),
           ("w_in", w_in, m_w_in, v_w_in, True),
           ("w_branch_a", w_branch_a, m_w_branch_a, v_w_branch_a, True),
           ("w_branch_b", w_branch_b, m_w_branch_b, v_w_branch_b, True),
           ("w_out", w_out, m_w_out, v_w_out, False),
           ("ffn2_w_gate", ffn2_w_gate, m_ffn2_w_gate, v_ffn2_w_gate, True),
           ("ffn2_w_up", ffn2_w_up, m_ffn2_w_up, v_ffn2_w_up, True),
           ("ffn2_w_down", ffn2_w_down, m_ffn2_w_down, v_ffn2_w_down, False)]
    shard = {nm: (w[0].T if tr else w[0]).astype(BF16) for nm, w, _, _, tr in big}
    full = {}

    def gather(names):
        return _Gather([shard[nm] for nm in names])

    def keep(names, got):
        for nm, gw in zip(names, got):
            full[nm] = gw.reshape(-1, gw.shape[-1])

    ffn1_names = ["ffn1_w_gate", "ffn1_w_up", "ffn1_w_down"]
    mix_names = ["w_in", "w_branch_a", "w_branch_b", "w_out"]
    ffn2_names = ["ffn2_w_gate", "ffn2_w_up", "ffn2_w_down"]
    g1, gm, g2, gf = ffn1_norm, mix_norm, ffn2_norm, final_norm.reshape(1, D_MODEL)

    keep(ffn1_names, _exchange([gather(ffn1_names)], "gather_ffn1")[0])
    (h1, n1), (got,) = _ffn_fwd(xt, g1, full["ffn1_w_gate"], full["ffn1_w_up"], full["ffn1_w_down"], "ffn1_fwd",
                                hosted=[gather(mix_names)])
    keep(mix_names, got)
    (u, zq, zg), (got,) = _inproj_fwd(h1, gm, full["w_in"], b_in, "inproj_fwd", hosted=[gather(ffn2_names[:2])])
    keep(ffn2_names[:2], got)
    buckets = _bucket_tiles()
    bias = _bias_build(rel_bias, buckets)
    sink_rows = jnp.broadcast_to(sinks.reshape(8, 1, 1), (8, BLOCK, 128)).reshape(8 * BLOCK, 128)
    cfs = [_Att(i) for i in range(4)]
    zfold = []
    for i, cf in enumerate(cfs):
        if cf.d == 1:
            zfold.append(zq.reshape(bsz, seq, QKV_W))
        else:
            own = jnp.concatenate([zq[:, 768 * part + 256 * i:768 * part + 256 * (i + 1)] for part in range(3)], axis=1)
            zfold.append(own.reshape(bsz, seq // cf.d, cf.d * cf.row_w))
    att = [None] * 4
    for i in (3, 0, 1, 2):
        cf = cfs[i]
        (o, lse), got = _attn_fwd(cf, zfold[i], bias, sink_rows, f"attn{i}_fwd",
                                  hosted=[gather(ffn2_names[2:])] if i == 3 else ())
        if i == 3:
            keep(ffn2_names[2:], got[0])
        att[i] = (o.reshape(t, cf.wq), lse.reshape(t, cf.wq))
    o_b, lse_b = att[3]
    h2, ya, lse_tot = _merge_fwd([a[0] for a in att[:3]], [a[1] for a in att[:3]], o_b, zg, h1,
                                 full["w_branch_a"], full["w_branch_b"], full["w_out"], "merge_fwd")
    (h3, n2), _ = _ffn_fwd(h2, g2, full["ffn2_w_gate"], full["ffn2_w_up"], full["ffn2_w_down"], "ffn2_fwd")
    dh3, loss_part, dgf = _loss_head(h3, gf, target, "loss_head")

    grads, pair, from_chips = {}, {}, {}

    def by_owner(nm):
        return grads[nm].reshape(N_CHIP, 2, -1, grads[nm].shape[-1])

    def to_core(names):
        return _CoreExchange([by_owner(nm) for nm in names])

    def pair_up(names, got):
        for nm, sib in zip(names, got):
            pair[nm] = _pair_sum(by_owner(nm), sib, f"pair_sum_{nm}")

    def to_chips(names):
        return _ChipExchange([pair[nm] for nm in names])

    def landed(names, got):
        for nm, parts in zip(names, got):
            from_chips[nm] = parts

    dh2, dg2, hff, da, db, dhh = _ffn_bwd(h2, n2, g2, dh3, full["ffn2_w_gate"], full["ffn2_w_up"],
                                          full["ffn2_w_down"], "ffn2_bwd")
    grads["ffn2_w_gate"], _ = _tn_matmul(da, n2, 1408, "ffn2_dgate")
    grads["ffn2_w_up"], _ = _tn_matmul(db, n2, 1408, "ffn2_dup")
    grads["ffn2_w_down"], _ = _tn_matmul(hff, dhh, 1408, "ffn2_ddown")
    (merged, dpa, dpb, dzg, dya, dyb), (got,) = _merge_bwd(
        dh2, ya, o_b, zg, full["w_branch_a"], full["w_branch_b"], full["w_out"], "merge_bwd",
        hosted=[to_core(ffn2_names)])
    pair_up(ffn2_names, got)
    dq, dk, dv, dbias, dsink = [None] * 4, [None] * 4, [None] * 4, [None] * 4, None
    for i in (3, 0, 1, 2):
        cf = cfs[i]
        lt_i, dy_i, y_i = (lse_tot, dya, ya) if i < 3 else (lse_b, dyb, o_b)
        shp = (bsz, seq // cf.d, cf.d * cf.wq)
        hosted = {3: lambda: [to_chips(ffn2_names[:2])], 0: lambda: [to_chips(ffn2_names[2:])]}.get(i, list)()
        res, got = _attn_bwd(cf, zfold[i], bias, sink_rows, lt_i.reshape(shp), dy_i.reshape(shp), y_i.reshape(shp),
                             f"attn{i}_bwd", hosted=hosted)
        if i == 3:
            landed(ffn2_names[:2], got[0])
        elif i == 0:
            landed(ffn2_names[2:], got[0])
        dq[i] = res[0].reshape(t, cf.wq)
        dk[i] = res[1].reshape(t, cf.wkv)
        dv[i] = res[2].reshape(t, cf.wkv)
        dbias[i] = res[3]
        if cf.sinks:
            dsink = res[4]
    pieces = dq[:3] + dk[:3] + dv[:3] + [dq[3], dk[3], dv[3], dzg]
    dh1, dz, db_in, dgm = _inproj_bwd(pieces, h1, gm, dh2, full["w_in"], "inproj_bwd")
    dx, dg1, hff, da, db, dhh = _ffn_bwd(xt, n1, g1, dh1, full["ffn1_w_gate"], full["ffn1_w_up"],
                                         full["ffn1_w_down"], "ffn1_bwd")
    grads["w_in"], _ = _tn_matmul(dz, u, 1280, "dw_in")
    grads["ffn1_w_down"], (got,) = _tn_matmul(hff, dhh, 1408, "ffn1_ddown", hosted=[to_core(["w_in"])])
    pair_up(["w_in"], got)
    grads["ffn1_w_gate"], got = _tn_matmul(da, n1, 1408, "ffn1_dgate",
                                           hosted=[to_chips(["w_in"]), to_core(["ffn1_w_down"])])
    landed(["w_in"], got[0])
    pair_up(["ffn1_w_down"], got[1])
    grads["ffn1_w_up"], got = _tn_matmul(db, n1, 1408, "ffn1_dup",
                                         hosted=[to_chips(["ffn1_w_down"]), to_core(["ffn1_w_gate"])])
    landed(["ffn1_w_down"], got[0])
    pair_up(["ffn1_w_gate"], got[1])
    grads["w_out"], got = _tn_matmul(merged, dh2, 1024, "dw_out",
                                     hosted=[to_chips(["ffn1_w_gate"]), to_core(["ffn1_w_up"])])
    landed(["ffn1_w_gate"], got[0])
    pair_up(["ffn1_w_up"], got[1])
    grads["w_branch_b"], got = _tn_matmul(dpb, o_b, 1024, "dw_branch_b",
                                          hosted=[to_chips(["ffn1_w_up"]), to_core(["w_out"])])
    landed(["ffn1_w_up"], got[0])
    pair_up(["w_out"], got[1])
    grads["w_branch_a"], got = _tn_matmul(dpa, ya, 1024, "dw_branch_a",
                                          hosted=[to_chips(["w_out"]), to_core(["w_branch_b"])])
    landed(["w_out"], got[0])
    pair_up(["w_branch_b"], got[1])
    got = _exchange([to_chips(["w_branch_b"]), to_core(["w_branch_a"])], "reduce_scatter_tail1")
    landed(["w_branch_b"], got[0])
    pair_up(["w_branch_a"], got[1])
    landed(["w_branch_a"], _exchange([to_chips(["w_branch_a"])], "reduce_scatter_tail2")[0])
    dtable, dsinks = _bias_reduce(jnp.concatenate(dbias, axis=0), buckets, dsink)

    out_g, out_d, out_m, out_v = {}, {}, {}, {}
    for nm, w, m, v, tr in big:
        g = _chip_sum(from_chips[nm], f"chip_sum_{nm}")
        g = (g.T if tr else g).reshape(w.shape)
        out_g[nm] = g
        d2, m2, v2 = _adamw(w[0], g[0], m[0], v[0], f"adamw_{nm}")
        out_d[nm], out_m[nm], out_v[nm] = d2[None], m2[None], v2[None]

    small = [("ffn1_norm", ffn1_norm, m_ffn1_norm, v_ffn1_norm), ("mix_norm", mix_norm, m_mix_norm, v_mix_norm),
             ("ffn2_norm", ffn2_norm, m_ffn2_norm, v_ffn2_norm), ("final_norm", final_norm, m_final_norm, v_final_norm),
             ("b_in", b_in, m_b_in, v_b_in), ("rel_bias", rel_bias, m_rel_bias, v_rel_bias),
             ("sinks", sinks, m_sinks, v_sinks)]
    zero_row = jnp.zeros((1, 128), F32)
    pack = lambda arrs, last: _pack_small(arrs[:4], arrs[4], arrs[5], arrs[6], last)
    g_small = pack([dg1, dgm, dg2, dgf, db_in, dtable[:, :TOTAL_HEADS], dsinks[:, 0]], loss_part)
    packed = [pack([s[k] for s in small], zero_row) for k in (1, 2, 3)]
    gs, ds, ms, vs = _small_update(g_small, *packed, "small_update")
    like = [s[1] for s in small]
    for nm_s, g_, d_, m_, v_ in zip([s[0] for s in small], _unpack_small(gs, like), _unpack_small(ds, like),
                                    _unpack_small(ms, like), _unpack_small(vs, like)):
        out_g[nm_s], out_d[nm_s], out_m[nm_s], out_v[nm_s] = g_, d_, m_, v_
    loss = gs[78, 0]

    order = ["ffn1_norm", "ffn1_w_gate", "ffn1_w_up", "ffn1_w_down", "mix_norm", "w_in", "b_in", "w_branch_a",
             "w_branch_b", "w_out", "sinks", "rel_bias", "ffn2_norm", "ffn2_w_gate", "ffn2_w_up", "ffn2_w_down",
             "final_norm"]
    return (loss, dx.reshape(x.shape), *[out_g[k] for k in order], *[out_d[k] for k in order],
            *[out_m[k] for k in order], *[out_v[k] for k in order])
```

```python
import functools
import math

import numpy as np
import jax
import jax.numpy as jnp
from jax import lax
from jax.experimental import pallas as pl
from jax.experimental.pallas import tpu as pltpu

D_MODEL = 1024
D_FF = 2816
FF_CHUNK = 256
HEAD_DIM = 64
BLOCK = 128
N_BUCKETS = 32
MAX_DISTANCE = 2048
A_HEADS = 12
TOTAL_HEADS = 20
DIL_GROUPS = ((128, 1), (512, 4), (2048, 16))
B_WINDOW = 128
QKV_W = 3072
GATE_W = 2048
D_IN = QKV_W + GATE_W
EPS = 1e-6
NEG = -1e30
N_DEV = 8
N_CHIP = 4
ADAM_LR, ADAM_B1, ADAM_B2, ADAM_EPS, ADAM_WD, ADAM_STEP = 0.001, 0.9, 0.999, 1e-08, 0.01, 10
VMEM_LIMIT = 56 * 1024 * 1024
MESH = pl.DeviceIdType.MESH
BF16 = jnp.bfloat16
F32 = jnp.float32
ANY = pl.BlockSpec(memory_space=pl.ANY)


def _params(*sem):
    return pltpu.CompilerParams(dimension_semantics=sem, vmem_limit_bytes=VMEM_LIMIT)


def _resident(a):
    return pl.BlockSpec(a.shape, lambda i: (0, 0), pipeline_mode=pl.Buffered(1))


def _place():
    return lax.axis_index("x"), lax.axis_index("y"), lax.axis_index("c")


class _Gather:
    def __init__(self, shards):
        self.ins = list(shards)
        n = self.n = len(shards)
        self.out_shape = [jax.ShapeDtypeStruct((N_DEV,) + s.shape, s.dtype) for s in shards]
        self.scratch = [pltpu.SemaphoreType.DMA((7 * n,)), pltpu.SemaphoreType.DMA((7 * n,)),
                        pltpu.SemaphoreType.DMA((n,))]

    def _copies(self, ins, outs, sems):
        send_sems, recv_sems, local_sems = sems
        x, y, c = _place()
        me, sibling = (x, y, c), (x, y, 1 - c)
        chips = [(1 - x, y), (x, 1 - y), (1 - x, 1 - y)]

        def copy(i, k, block, to, src=None):
            dst = outs[i].at[4 * block[0] + 2 * block[1] + block[2]]
            return pltpu.make_async_remote_copy(
                src_ref=dst if src is None else src, dst_ref=dst, send_sem=send_sems.at[7 * i + k],
                recv_sem=recv_sems.at[7 * i + k], device_id=to, device_id_type=MESH)

        n = self.n
        mine = [pltpu.make_async_copy(ins[i], outs[i].at[4 * x + 2 * y + c], local_sems.at[i]) for i in range(n)]
        first = [copy(i, 0, me, sibling, src=ins[i]) for i in range(n)]
        first += [copy(i, 1 + j, me, (*chip, c), src=ins[i]) for i in range(n) for j, chip in enumerate(chips)]
        landed = [copy(i, 1 + j, (*chip, c), me) for j, chip in enumerate(chips) for i in range(n)]
        passed = [copy(i, 4 + j, (*chip, c), sibling) for j, chip in enumerate(chips) for i in range(n)]
        from_sibling = [copy(i, 0, sibling, me) for i in range(n)]
        from_sibling += [copy(i, 4 + j, (*chip, 1 - c), me) for i in range(n) for j, chip in enumerate(chips)]
        return mine, first, landed, passed, from_sibling

    def start(self, ins, outs, sems):
        mine, first, _, _, _ = self._copies(ins, outs, sems)
        for cp in mine + first:
            cp.start()

    def mid(self, ins, outs, sems):
        _, _, landed, passed, _ = self._copies(ins, outs, sems)
        for got, fwd in zip(landed, passed):
            got.wait_recv()
            fwd.start()

    def end(self, ins, outs, sems):
        mine, first, _, passed, from_sibling = self._copies(ins, outs, sems)
        for cp in from_sibling:
            cp.wait_recv()
        for cp in first + passed:
            cp.wait_send()
        for cp in mine:
            cp.wait()


class _CoreExchange:
    def __init__(self, grads):
        self.ins = list(grads)
        n = self.n = len(grads)
        self.out_shape = [jax.ShapeDtypeStruct((N_CHIP, 1) + g.shape[2:], g.dtype) for g in grads]
        self.scratch = [pltpu.SemaphoreType.DMA((n,)), pltpu.SemaphoreType.DMA((n,))]

    def _copies(self, ins, outs, sems):
        x, y, c = _place()
        return [pltpu.make_async_remote_copy(
            src_ref=ins[i].at[:, pl.ds(1 - c, 1)], dst_ref=outs[i], send_sem=sems[0].at[i],
            recv_sem=sems[1].at[i], device_id=(x, y, 1 - c), device_id_type=MESH) for i in range(self.n)]

    def start(self, ins, outs, sems):
        for cp in self._copies(ins, outs, sems):
            cp.start()

    mid = None

    def end(self, ins, outs, sems):
        for cp in self._copies(ins, outs, sems):
            cp.wait()


class _ChipExchange:
    def __init__(self, parts):
        self.ins = list(parts)
        n = self.n = len(parts)
        self.out_shape = [jax.ShapeDtypeStruct(p.shape, p.dtype) for p in parts]
        self.scratch = [pltpu.SemaphoreType.DMA((3 * n,)), pltpu.SemaphoreType.DMA((3 * n,)),
                        pltpu.SemaphoreType.DMA((n,))]

    def _copies(self, ins, outs, sems):
        send_sems, recv_sems, local_sems = sems
        x, y, c = _place()
        my_chip = 2 * x + y
        copies = []
        for i in range(self.n):
            copies.append(pltpu.make_async_copy(ins[i].at[my_chip], outs[i].at[my_chip], local_sems.at[i]))
            for k, (qx, qy) in enumerate([(1 - x, y), (x, 1 - y), (1 - x, 1 - y)]):
                copies.append(pltpu.make_async_remote_copy(
                    src_ref=ins[i].at[2 * qx + qy], dst_ref=outs[i].at[my_chip], send_sem=send_sems.at[3 * i + k],
                    recv_sem=recv_sems.at[3 * i + k], device_id=(qx, qy, c), device_id_type=MESH))
        return copies

    def start(self, ins, outs, sems):
        for cp in self._copies(ins, outs, sems):
            cp.start()

    mid = None

    def end(self, ins, outs, sems):
        for cp in self._copies(ins, outs, sems):
            cp.wait()


def _call(body, *, name, grid, in_specs, out_specs, out_shape, args, scratch=(), sem=None, hosted=()):
    n_in, n_out, n_scr = len(in_specs), len(out_specs), len(scratch)
    x_in = [len(p.ins) for p in hosted]
    x_scr = [len(p.scratch) for p in hosted]
    steps = int(np.prod(grid))
    assert not hosted or steps >= 3

    def wrapped(*refs):
        refs = list(refs)
        ins, refs = refs[:n_in], refs[n_in:]
        x_ins = [[refs.pop(0) for _ in range(k)] for k in x_in]
        outs, refs = refs[:n_out], refs[n_out:]
        x_outs = [[refs.pop(0) for _ in range(k)] for k in x_in]
        scr, refs = refs[:n_scr], refs[n_scr:]
        x_sems = [[refs.pop(0) for _ in range(k)] for k in x_scr]
        step = 0
        for d in range(len(grid)):
            step = step * grid[d] + pl.program_id(d)

        def phase(which, at):
            fns = [(getattr(p, which), a) for p, a in zip(hosted, zip(x_ins, x_outs, x_sems)) if getattr(p, which)]
            if fns:
                @pl.when(step == at)
                def _():
                    for fn, a in fns:
                        fn(*a)

        phase("start", 0)
        phase("mid", (3 * steps) // 4)
        body(*ins, *outs, *scr)
        phase("end", steps - 1)

    results = pl.pallas_call(
        wrapped, name=name, grid=grid,
        in_specs=list(in_specs) + [ANY] * sum(x_in), out_specs=list(out_specs) + [ANY] * sum(x_in),
        out_shape=list(out_shape) + [s for p in hosted for s in p.out_shape],
        scratch_shapes=list(scratch) + [s for p in hosted for s in p.scratch],
        compiler_params=_params(*(("arbitrary",) * len(grid) if hosted else sem)),
    )(*args, *[a for p in hosted for a in p.ins])
    own, rest = list(results[:n_out]), list(results[n_out:])
    return own, [[rest.pop(0) for _ in range(k)] for k in x_in]


def _exchange(programs, name):
    x_in = [len(p.ins) for p in programs]
    x_scr = [len(p.scratch) for p in programs]

    def body(*refs):
        refs = list(refs)
        x_ins = [[refs.pop(0) for _ in range(k)] for k in x_in]
        x_outs = [[refs.pop(0) for _ in range(k)] for k in x_in]
        x_sems = [[refs.pop(0) for _ in range(k)] for k in x_scr]
        for which in ("start", "mid", "end"):
            for p, a in zip(programs, zip(x_ins, x_outs, x_sems)):
                if getattr(p, which):
                    getattr(p, which)(*a)

    results = list(pl.pallas_call(
        body, name=name, in_specs=[ANY] * sum(x_in), out_specs=[ANY] * sum(x_in),
        out_shape=[s for p in programs for s in p.out_shape],
        scratch_shapes=[s for p in programs for s in p.scratch],
    )(*[a for p in programs for a in p.ins]))
    return [[results.pop(0) for _ in range(k)] for k in x_in]


def _nt(a, b):
    return lax.dot_general(a, b, (((1,), (1,)), ((), ())), preferred_element_type=F32)


def _nn(a, b):
    return lax.dot_general(a, b, (((1,), (0,)), ((), ())), preferred_element_type=F32)


def _tn(a, b):
    return lax.dot_general(a, b, (((0,), (0,)), ((), ())), preferred_element_type=F32)


def _rms(x, g):
    r = lax.rsqrt(jnp.mean(x * x, axis=-1, keepdims=True) + EPS)
    return x * r, r


def _rms_bwd(dn, xhat, r, g):
    dg = jnp.sum(dn * xhat, axis=0, keepdims=True)
    dxh = dn * g
    dx = r * (dxh - xhat * jnp.mean(dxh * xhat, axis=-1, keepdims=True))
    return dx, dg


def _ffn_fwd(x, g, wg_t, wu_t, wd, name, hosted=()):
    t = x.shape[0]
    tm = 256

    def body(x_ref, g_ref, wg_ref, wu_ref, wd_ref, h_ref, n_ref, a_ref, b_ref, hff_ref):
        xhat, _ = _rms(x_ref[...], g_ref[...])
        n = (xhat * g_ref[...]).astype(BF16)
        n_ref[...] = n
        for c in range(0, D_FF, FF_CHUNK):
            cols = slice(c, c + FF_CHUNK)
            a = _nt(n, wg_ref[cols, :])
            b = _nt(n, wu_ref[cols, :])
            a_ref[:, cols] = a
            b_ref[:, cols] = b
            hff_ref[:, cols] = (a * jax.nn.sigmoid(a) * b).astype(BF16)
        h_ref[...] = x_ref[...] + 0.5 * _nn(hff_ref[...], wd_ref[...])

    row = pl.BlockSpec((tm, D_MODEL), lambda i: (i, 0))
    hid = pl.BlockSpec((tm, D_FF), lambda i: (i, 0))
    return _call(
        body, name=name, grid=(t // tm,),
        in_specs=[row, _resident(g), _resident(wg_t), _resident(wu_t), _resident(wd)],
        out_specs=[row, row, hid, hid],
        out_shape=[jax.ShapeDtypeStruct((t, D_MODEL), F32), jax.ShapeDtypeStruct((t, D_MODEL), BF16),
                   jax.ShapeDtypeStruct((t, D_FF), F32), jax.ShapeDtypeStruct((t, D_FF), F32)],
        scratch=[pltpu.VMEM((tm, D_FF), BF16)], sem=("parallel",), args=(x, g, wg_t, wu_t, wd), hosted=hosted)


def _ffn_bwd(x, a_pre, b_pre, g, dh, wg_t, wu_t, wd, name):
    t = x.shape[0]
    tm = 256

    def body(x_ref, a_ref, b_ref, g_ref, dh_ref, wg_ref, wu_ref, wd_ref,
             dx_ref, dg_ref, hff_ref, da_ref, db_ref, dhh_ref):
        @pl.when(pl.program_id(0) == 0)
        def _():
            dg_ref[...] = jnp.zeros_like(dg_ref)

        dhh = (0.5 * dh_ref[...]).astype(BF16)
        dhh_ref[...] = dhh
        for c in range(0, D_FF, FF_CHUNK):
            cols = slice(c, c + FF_CHUNK)
            a = a_ref[:, cols]
            b = b_ref[:, cols]
            s = jax.nn.sigmoid(a)
            silu = a * s
            dhff = _nt(dhh, wd_ref[cols, :])
            hff_ref[:, cols] = (silu * b).astype(BF16)
            da_ref[:, cols] = (dhff * b * (s * (1.0 + a * (1.0 - s)))).astype(BF16)
            db_ref[:, cols] = (dhff * silu).astype(BF16)
        dn = _nn(da_ref[...], wg_ref[...]) + _nn(db_ref[...], wu_ref[...])
        xhat, r = _rms(x_ref[...], g_ref[...])
        dx, dg = _rms_bwd(dn, xhat, r, g_ref[...])
        dx_ref[...] = dh_ref[...] + dx
        dg_ref[...] += dg

    row = pl.BlockSpec((tm, D_MODEL), lambda i: (i, 0))
    hid = pl.BlockSpec((tm, D_FF), lambda i: (i, 0))
    return pl.pallas_call(
        body, name=name, grid=(t // tm,),
        in_specs=[row, hid, hid, _resident(g), row, _resident(wg_t), _resident(wu_t), _resident(wd)],
        out_specs=[row, pl.BlockSpec((1, D_MODEL), lambda i: (0, 0)), hid, hid, hid, row],
        out_shape=[jax.ShapeDtypeStruct((t, D_MODEL), F32), jax.ShapeDtypeStruct((1, D_MODEL), F32),
                   jax.ShapeDtypeStruct((t, D_FF), BF16), jax.ShapeDtypeStruct((t, D_FF), BF16),
                   jax.ShapeDtypeStruct((t, D_FF), BF16), jax.ShapeDtypeStruct((t, D_MODEL), BF16)],
        compiler_params=_params("arbitrary"),
    )(x, a_pre, b_pre, g, dh, wg_t, wu_t, wd)


def _tn_matmul(a, b, rc, name, hosted=()):
    t, r = a.shape
    c = b.shape[1]
    tk = 1024

    def body(a_ref, b_ref, o_ref):
        @pl.when(pl.program_id(1) == 0)
        def _():
            o_ref[...] = jnp.zeros_like(o_ref)

        o_ref[...] += _tn(a_ref[...].astype(BF16), b_ref[...].astype(BF16))

    (out,), got = _call(
        body, name=name, grid=(r // rc, t // tk),
        in_specs=[pl.BlockSpec((tk, rc), lambda i, k: (k, i)), pl.BlockSpec((tk, c), lambda i, k: (k, 0))],
        out_specs=[pl.BlockSpec((rc, c), lambda i, k: (i, 0))],
        out_shape=[jax.ShapeDtypeStruct((r, c), F32)],
        sem=("parallel", "arbitrary"), args=(a, b), hosted=hosted)
    return out, got


PIECE_W = (256,) * 9 + (512, 128, 128, GATE_W)


def _inproj_fwd(h, g, w_t, b_in, name, hosted=()):
    t = h.shape[0]
    tm, nc = 512, 512

    def body(h_ref, g_ref, w_ref, b_ref, u_ref, zq_ref, zg_ref):
        xhat, _ = _rms(h_ref[...], g_ref[...])
        u = (xhat * g_ref[...]).astype(BF16)
        u_ref[...] = u
        for c in range(D_IN // nc):
            z = _nt(u, w_ref[c * nc:(c + 1) * nc, :]) + b_ref[:, c * nc:(c + 1) * nc]
            if c < QKV_W // nc:
                zq_ref[:, c * nc:(c + 1) * nc] = z.astype(BF16)
            else:
                zg_ref[:, c * nc - QKV_W:(c + 1) * nc - QKV_W] = z

    row = lambda w: pl.BlockSpec((tm, w), lambda i: (i, 0))
    full = _resident
    return _call(
        body, name=name, grid=(t // tm,),
        in_specs=[row(D_MODEL), full(g), full(w_t), full(b_in)],
        out_specs=[row(D_MODEL), row(QKV_W), row(GATE_W)],
        out_shape=[jax.ShapeDtypeStruct((t, D_MODEL), BF16), jax.ShapeDtypeStruct((t, QKV_W), BF16),
                   jax.ShapeDtypeStruct((t, GATE_W), F32)],
        sem=("parallel",), args=(h, g, w_t, b_in), hosted=hosted)


def _inproj_bwd(pieces, h, g, dh_res, w_t, name):
    t = h.shape[0]
    tm = 256
    npiece = len(PIECE_W)
    offs = np.concatenate([[0], np.cumsum(PIECE_W)]).tolist()

    def body(*refs):
        p_refs = refs[:npiece]
        h_ref, g_ref, dhr_ref, w_ref, dh_ref, dz_ref, db_ref, dg_ref = refs[npiece:]
        i = pl.program_id(0)

        @pl.when(i == 0)
        def _():
            db_ref[...] = jnp.zeros_like(db_ref)
            dg_ref[...] = jnp.zeros_like(dg_ref)

        du = jnp.zeros((tm, D_MODEL), F32)
        for k in range(npiece):
            o, w = offs[k], PIECE_W[k]
            for c0 in range(0, w, 512):
                cw = min(512, w - c0)
                pz = p_refs[k][:, c0:c0 + cw]
                dz_ref[:, o + c0:o + c0 + cw] = pz
                db_ref[:, o + c0:o + c0 + cw] += jnp.sum(pz.astype(F32), axis=0, keepdims=True)
                du = du + _nn(pz, w_ref[o + c0:o + c0 + cw, :])
        xhat, r = _rms(h_ref[...], g_ref[...])
        dx, dg = _rms_bwd(du, xhat, r, g_ref[...])
        dh_ref[...] = dhr_ref[...] + dx
        dg_ref[...] += dg

    row = lambda w: pl.BlockSpec((tm, w), lambda i: (i, 0))
    full = lambda shp: pl.BlockSpec(shp, lambda i: (0, 0))
    return pl.pallas_call(
        body, name=name, grid=(t // tm,),
        in_specs=[row(w) for w in PIECE_W] + [row(D_MODEL), _resident(g), row(D_MODEL), _resident(w_t)],
        out_specs=[row(D_MODEL), row(D_IN), full((1, D_IN)), full((1, D_MODEL))],
        out_shape=[jax.ShapeDtypeStruct((t, D_MODEL), F32), jax.ShapeDtypeStruct((t, D_IN), BF16),
                   jax.ShapeDtypeStruct((1, D_IN), F32), jax.ShapeDtypeStruct((1, D_MODEL), F32)],
        compiler_params=_params("arbitrary"),
    )(*pieces, h, g, dh_res, w_t)


def _t5_bucket(dist):
    max_exact = N_BUCKETS // 2
    n = jnp.maximum(dist, 0)
    nf = jnp.maximum(n, 1).astype(jnp.float32)
    large = max_exact + (jnp.log(nf / max_exact) / math.log(MAX_DISTANCE / max_exact)
                         * (N_BUCKETS - max_exact)).astype(jnp.int32)
    large = jnp.minimum(large, N_BUCKETS - 1)
    return jnp.where(n < max_exact, n, large)


ATT_CFG = ((1, 128, 0, 4), (4, 128, 4, 4), (16, 128, 8, 4), (1, B_WINDOW - 1, A_HEADS, 8))


def _bucket_tiles():
    qi = jnp.arange(BLOCK)[:, None]
    ki = jnp.arange(2 * BLOCK)[None, :]
    dist = qi + BLOCK - ki
    return jnp.stack([_t5_bucket(dist * cfg[0]) for cfg in ATT_CFG]).astype(jnp.int32)


def _band(max_steps):
    row = lax.broadcasted_iota(jnp.int32, (BLOCK, 2 * BLOCK), 0)
    col = lax.broadcasted_iota(jnp.int32, (BLOCK, 2 * BLOCK), 1)
    dist = row + BLOCK - col
    return (dist >= 0) & (dist <= max_steps)


def _bias_build(table, buckets):
    def body(tab_ref, bt_ref, out_ref):
        col = lax.broadcasted_iota(jnp.int32, (BLOCK, 2 * BLOCK), 1)
        for ci, (_, max_steps, h0, nh) in enumerate(ATT_CFG):
            bt = bt_ref[ci]
            band = _band(max_steps)
            for h in range(h0, h0 + nh):
                acc = lax.fori_loop(0, N_BUCKETS, lambda b, acc: jnp.where(bt == b, tab_ref[b, h], acc),
                                    jnp.zeros((BLOCK, 2 * BLOCK), F32))
                out_ref[0, BLOCK * h:BLOCK * (h + 1), :] = jnp.where(band & (col >= BLOCK), acc, NEG)
                out_ref[1, BLOCK * h:BLOCK * (h + 1), :] = jnp.where(band, acc, NEG)

    return pl.pallas_call(
        body, name="bias_build",
        in_specs=[pl.BlockSpec(memory_space=pltpu.SMEM), pl.BlockSpec(memory_space=pltpu.VMEM)],
        out_specs=pl.BlockSpec(memory_space=pltpu.VMEM),
        out_shape=jax.ShapeDtypeStruct((2, TOTAL_HEADS * BLOCK, 2 * BLOCK), F32),
    )(table, buckets)


def _bias_reduce(dbias, buckets, dsink_rows):
    def body(db_ref, bt_ref, ds_ref, out_ref, sink_ref):
        ri = lax.broadcasted_iota(jnp.int32, (N_BUCKETS, 128), 0)
        ci = lax.broadcasted_iota(jnp.int32, (N_BUCKETS, 128), 1)

        def per_bucket(b, acc):
            for cfg_i, (_, _, h0, nh) in enumerate(ATT_CFG):
                hit = bt_ref[cfg_i] == b
                for h in range(h0, h0 + nh):
                    val = jnp.sum(jnp.where(hit, db_ref[BLOCK * h:BLOCK * (h + 1), :], 0.0))
                    acc = jnp.where((ri == b) & (ci == h), val, acc)
            return acc

        out_ref[...] = lax.fori_loop(0, N_BUCKETS, per_bucket, jnp.zeros((N_BUCKETS, 128), F32))
        for h in range(8):
            sink_ref[h:h + 1, :] = jnp.sum(ds_ref[BLOCK * h:BLOCK * (h + 1), :], axis=0, keepdims=True)

    return pl.pallas_call(
        body, name="bias_reduce",
        in_specs=[pl.BlockSpec(memory_space=pltpu.VMEM)] * 3,
        out_specs=[pl.BlockSpec(memory_space=pltpu.VMEM)] * 2,
        out_shape=[jax.ShapeDtypeStruct((N_BUCKETS, 128), F32), jax.ShapeDtypeStruct((8, 128), F32)],
    )(dbias, buckets, dsink_rows)


class _Att:
    def __init__(self, cfg_i):
        stride, _, h0, nh = ATT_CFG[cfg_i]
        self.d = stride if cfg_i < 3 else 1
        self.h0, self.nh = h0, nh
        self.row_w = QKV_W if self.d == 1 else 3 * 256
        if cfg_i < 3:
            self.nq, self.wkv = 1, 256
            self.q_unit = [cfg_i if self.d == 1 else 0]
            self.k_unit, self.v_unit = (3 + cfg_i, 6 + cfg_i) if self.d == 1 else (1, 2)
            self.sinks = False
        else:
            self.nq, self.wkv = 2, 128
            self.q_unit = [9, 10]
            self.k_unit, self.v_unit = 22, 23
            self.sinks = True
        self.wq = 256 * self.nq


def _att_in_specs(cf, bsz):
    uq, ukv = cf.row_w // 256, cf.row_w // cf.wkv
    specs = [pl.BlockSpec((bsz, BLOCK, 256), functools.partial(lambda r, j, u: (0, j, r * uq + u), u=u))
             for u in cf.q_unit]
    for unit in (cf.k_unit, cf.v_unit):
        specs.append(pl.BlockSpec((bsz, BLOCK, cf.wkv),
                                  functools.partial(lambda r, j, u: (0, jnp.maximum(j - 1, 0), r * ukv + u), u=unit)))
        specs.append(pl.BlockSpec((bsz, BLOCK, cf.wkv),
                                  functools.partial(lambda r, j, u: (0, j, r * ukv + u), u=unit)))
    for qb in range(cf.nq):
        specs.append(pl.BlockSpec((None, HEADS_PER_BLOCK * BLOCK, 2 * BLOCK), functools.partial(
            lambda r, j, u: (jnp.minimum(j, 1), u, 0), u=cf.h0 // HEADS_PER_BLOCK + qb)))
    if cf.sinks:
        specs += [pl.BlockSpec((HEADS_PER_BLOCK * BLOCK, 128), functools.partial(lambda r, j, u: (u, 0), u=qb))
                  for qb in range(cf.nq)]
    return specs


HEADS_PER_BLOCK = 4


def _head_masks(rows):
    head = lax.broadcasted_iota(jnp.int32, (rows, 256), 1) // HEAD_DIM
    return [head == h for h in range(HEADS_PER_BLOCK)]


def _stack_heads(x, masks):
    return jnp.concatenate([jnp.where(m, x, jnp.zeros_like(x)) for m in masks], axis=0)


def _unstack_heads(x4, masks):
    blocks = [x4[BLOCK * h:BLOCK * (h + 1)] for h in range(HEADS_PER_BLOCK)]
    return jnp.where(masks[0], blocks[0], jnp.where(masks[1], blocks[1], jnp.where(masks[2], blocks[2], blocks[3])))


def _row_value(x):
    return jnp.max(x, axis=-1, keepdims=True)


def _kv_operand(cf, x, qb):
    if cf.wkv == 256:
        return x
    lane = lax.broadcasted_iota(jnp.int32, x.shape, 1)
    swapped = pltpu.roll(x, HEAD_DIM, 1)
    half = jnp.where(lane < HEAD_DIM, x, swapped) if qb == 0 else jnp.where(lane < HEAD_DIM, swapped, x)
    return jnp.concatenate([half, half], axis=1)


def _kv_fold(cf, grads):
    if cf.wkv == 256:
        return grads[0]
    folded = []
    for g in grads:
        x = g[:, :128] + g[:, 128:]
        folded.append(x + pltpu.roll(x, HEAD_DIM, 1))
    lane = lax.broadcasted_iota(jnp.int32, folded[0].shape, 1)
    return jnp.where(lane < HEAD_DIM, folded[0], folded[1])


def _attn_fwd(cf, zf, bias, sinks, name, hosted=()):
    bsz, l, _ = zf.shape
    nb = l // BLOCK

    def body(*refs):
        refs = list(refs)
        q_refs = [refs.pop(0) for _ in range(cf.nq)]
        kp_ref, kc_ref, vp_ref, vc_ref = [refs.pop(0) for _ in range(4)]
        bias_refs = [refs.pop(0) for _ in range(cf.nq)]
        sink_refs = [refs.pop(0) for _ in range(cf.nq)] if cf.sinks else None
        o_ref, lse_ref = refs
        masks, kv_masks = _head_masks(BLOCK), _head_masks(2 * BLOCK)
        sinks4 = [_row_value(ref[...]) for ref in sink_refs] if cf.sinks else None
        for bi in range(bsz):
            k = jnp.concatenate([kp_ref[bi], kc_ref[bi]], axis=0)
            v = jnp.concatenate([vp_ref[bi], vc_ref[bi]], axis=0)
            for qb in range(cf.nq):
                cols = slice(256 * qb, 256 * (qb + 1))
                kb, vb = _kv_operand(cf, k, qb), _kv_operand(cf, v, qb)
                q4 = _stack_heads(q_refs[qb][bi] * (HEAD_DIM ** -0.5), masks)
                s = _nt(q4, kb) + bias_refs[qb][...]
                m = jnp.max(s, axis=-1, keepdims=True)
                if cf.sinks:
                    sk = sinks4[qb]
                    m = jnp.maximum(m, sk)
                p = jnp.exp(s - m)
                den = jnp.sum(p, axis=-1, keepdims=True)
                if cf.sinks:
                    den = den + jnp.exp(sk - m)
                pn = (p * (1.0 / den)).astype(BF16)
                p_lanes = jnp.concatenate([pn[BLOCK * h:BLOCK * (h + 1)] for h in range(HEADS_PER_BLOCK)], axis=1)
                v4 = jnp.concatenate([jnp.where(mk, vb, jnp.zeros_like(vb)) for mk in kv_masks], axis=0)
                o_ref[bi, :, cols] = _nn(p_lanes, v4)
                lse_ref[bi, :, cols] = _unstack_heads(
                    jnp.broadcast_to(m + jnp.log(den), (HEADS_PER_BLOCK * BLOCK, 256)), masks)

    in_specs = _att_in_specs(cf, bsz)
    args = [zf] * (cf.nq + 4) + [bias] * cf.nq + ([sinks] * cf.nq if cf.sinks else [])
    out = pl.BlockSpec((bsz, BLOCK, cf.wq), lambda r, j: (0, j, r))
    shape = jax.ShapeDtypeStruct((bsz, l, cf.d * cf.wq), F32)
    return _call(
        body, name=name, grid=(cf.d, nb), in_specs=in_specs, out_specs=[out, out], out_shape=[shape, shape],
        sem=("parallel", "arbitrary"), args=args, hosted=hosted)


def _attn_bwd(cf, zf, bias, sinks, lse_tot, dy, y, name, hosted=()):
    bsz, l, _ = zf.shape
    nb = l // BLOCK

    def body(*refs):
        refs = list(refs)
        q_refs = [refs.pop(0) for _ in range(cf.nq)]
        kp_ref, kc_ref, vp_ref, vc_ref = [refs.pop(0) for _ in range(4)]
        bias_refs = [refs.pop(0) for _ in range(cf.nq)]
        sink_refs = [refs.pop(0) for _ in range(cf.nq)] if cf.sinks else None
        lt_ref, dy_ref, y_ref, dq_ref, dk_ref, dv_ref, dbias_ref = [refs.pop(0) for _ in range(7)]
        dsink_ref = refs.pop(0) if cf.sinks else None
        dk_acc, dv_acc = refs
        r, j = pl.program_id(0), pl.program_id(1)

        @pl.when((r == 0) & (j == 0))
        def _():
            dbias_ref[...] = jnp.zeros_like(dbias_ref)
            if cf.sinks:
                dsink_ref[...] = jnp.zeros_like(dsink_ref)

        @pl.when(j == 0)
        def _():
            dk_acc[...] = jnp.zeros_like(dk_acc)
            dv_acc[...] = jnp.zeros_like(dv_acc)

        masks = _head_masks(BLOCK)
        cur = pl.ds(pl.multiple_of(j * BLOCK, BLOCK), BLOCK)
        prev = pl.ds(pl.multiple_of(jnp.maximum(j - 1, 0) * BLOCK, BLOCK), BLOCK)
        sinks4 = [_row_value(ref[...]) for ref in sink_refs] if cf.sinks else None
        for bi in range(bsz):
            k = jnp.concatenate([kp_ref[bi], kc_ref[bi]], axis=0)
            v = jnp.concatenate([vp_ref[bi], vc_ref[bi]], axis=0)
            dk_blocks, dv_blocks = [], []
            for qb in range(cf.nq):
                cols = slice(256 * qb, 256 * (qb + 1))
                rows = slice(HEADS_PER_BLOCK * BLOCK * qb, HEADS_PER_BLOCK * BLOCK * (qb + 1))
                kb, vb = _kv_operand(cf, k, qb), _kv_operand(cf, v, qb)
                q4 = _stack_heads(q_refs[qb][bi] * (HEAD_DIM ** -0.5), masks)
                lt = lt_ref[bi, :, cols]
                lt4 = jnp.concatenate([_row_value(jnp.where(mk, lt, NEG)) for mk in masks], axis=0)
                pa = jnp.exp(_nt(q4, kb) + bias_refs[qb][...] - lt4)
                dy = dy_ref[bi, :, cols]
                prod = dy * y_ref[bi, :, cols]
                e4 = jnp.concatenate([jnp.sum(jnp.where(mk, prod, 0.0), axis=-1, keepdims=True) for mk in masks],
                                     axis=0)
                dy4 = _stack_heads(dy.astype(BF16), masks)
                ds = pa * (_nt(dy4, vb) - e4)
                dbias_ref[rows, :] += ds
                if cf.sinks:
                    dsink_ref[rows, :] += -(jnp.exp(sinks4[qb] - lt4) * e4)
                dsb = ds.astype(BF16)
                dq_ref[bi, :, cols] = (_unstack_heads(_nn(dsb, kb), masks) * (HEAD_DIM ** -0.5)).astype(BF16)
                dk_blocks.append(_tn(dsb, q4))
                dv_blocks.append(_tn(pa.astype(BF16), dy4))
            dk_new, dv_new = _kv_fold(cf, dk_blocks), _kv_fold(cf, dv_blocks)
            dk_acc[bi, cur, :] += dk_new[BLOCK:]
            dv_acc[bi, cur, :] += dv_new[BLOCK:]
            dk_acc[bi, prev, :] += dk_new[:BLOCK]
            dv_acc[bi, prev, :] += dv_new[:BLOCK]

        @pl.when(j == nb - 1)
        def _():
            dk_ref[...] = dk_acc[...].astype(BF16)
            dv_ref[...] = dv_acc[...].astype(BF16)

    tok = pl.BlockSpec((bsz, BLOCK, cf.wq), lambda r, j: (0, j, r))
    in_specs = _att_in_specs(cf, bsz) + [tok] * 3
    args = [zf] * (cf.nq + 4) + [bias] * cf.nq + ([sinks] * cf.nq if cf.sinks else []) + [lse_tot, dy, y]
    seq = pl.BlockSpec((bsz, l, cf.wkv), lambda r, j: (0, 0, r))
    out_specs = [tok, seq, seq, pl.BlockSpec((cf.nh * BLOCK, 2 * BLOCK), lambda r, j: (0, 0))]
    out_shape = [jax.ShapeDtypeStruct((bsz, l, cf.d * cf.wq), BF16),
                 jax.ShapeDtypeStruct((bsz, l, cf.d * cf.wkv), BF16),
                 jax.ShapeDtypeStruct((bsz, l, cf.d * cf.wkv), BF16),
                 jax.ShapeDtypeStruct((cf.nh * BLOCK, 2 * BLOCK), F32)]
    if cf.sinks:
        out_specs.append(pl.BlockSpec((cf.nh * BLOCK, 128), lambda r, j: (0, 0)))
        out_shape.append(jax.ShapeDtypeStruct((cf.nh * BLOCK, 128), F32))
    return _call(
        body, name=name, grid=(cf.d, nb), in_specs=in_specs, out_specs=out_specs, out_shape=out_shape,
        scratch=[pltpu.VMEM((bsz, l, cf.wkv), F32), pltpu.VMEM((bsz, l, cf.wkv), F32)],
        sem=("arbitrary", "arbitrary"), args=args, hosted=hosted)


def _merge_fwd(o_a, lse_a, o_b, zg, h, wa_t, wb_t, wout, name):
    t = h.shape[0]
    tm = 512

    def body(o1, o2, o3, l1, l2, l3, ob_ref, zg_ref, h_ref, wa_ref, wb_ref, wo_ref, h2_ref, ya_ref, lt_ref):
        m = jnp.maximum(jnp.maximum(l1[...], l2[...]), l3[...])
        e1, e2, e3 = jnp.exp(l1[...] - m), jnp.exp(l2[...] - m), jnp.exp(l3[...] - m)
        se = e1 + e2 + e3
        ya = (e1 / se) * o1[...] + (e2 / se) * o2[...] + (e3 / se) * o3[...]
        ya_ref[...] = ya
        lt_ref[...] = m + jnp.log(se)
        pa = _nt(ya.astype(BF16), wa_ref[...])
        pb = _nt(ob_ref[...].astype(BF16), wb_ref[...])
        merged = jax.nn.sigmoid(zg_ref[:, :D_MODEL]) * pa + jax.nn.sigmoid(zg_ref[:, D_MODEL:]) * pb
        h2_ref[...] = h_ref[...] + _nn(merged.astype(BF16), wo_ref[...])

    row = lambda w: pl.BlockSpec((tm, w), lambda i: (i, 0))
    full = _resident
    return pl.pallas_call(
        body, name=name, grid=(t // tm,),
        in_specs=[row(256)] * 6 + [row(512), row(GATE_W), row(D_MODEL), full(wa_t), full(wb_t), full(wout)],
        out_specs=[row(D_MODEL), row(256), row(256)],
        out_shape=[jax.ShapeDtypeStruct((t, D_MODEL), F32), jax.ShapeDtypeStruct((t, 256), F32),
                   jax.ShapeDtypeStruct((t, 256), F32)],
        compiler_params=_params("parallel"),
    )(*o_a, *lse_a, o_b, zg, h, wa_t, wb_t, wout)


def _merge_bwd(dh, ya, o_b, zg, wa_t, wb_t, wout, name, hosted=()):
    t = dh.shape[0]
    tm = 256

    def body(dh_ref, ya_ref, ob_ref, zg_ref, wa_ref, wb_ref, wo_ref,
             mg_ref, dpa_ref, dpb_ref, dzg_ref, dya_ref, dyb_ref):
        dm = _nt(dh_ref[...].astype(BF16), wo_ref[...])
        pa = _nt(ya_ref[...].astype(BF16), wa_ref[...])
        pb = _nt(ob_ref[...].astype(BF16), wb_ref[...])
        sa = jax.nn.sigmoid(zg_ref[:, :D_MODEL])
        sb = jax.nn.sigmoid(zg_ref[:, D_MODEL:])
        mg_ref[...] = (sa * pa + sb * pb).astype(BF16)
        dpa = (dm * sa).astype(BF16)
        dpb = (dm * sb).astype(BF16)
        dpa_ref[...] = dpa
        dpb_ref[...] = dpb
        dzg_ref[:, :D_MODEL] = (dm * pa * (sa * (1.0 - sa))).astype(BF16)
        dzg_ref[:, D_MODEL:] = (dm * pb * (sb * (1.0 - sb))).astype(BF16)
        dya_ref[...] = _nn(dpa, wa_ref[...])
        dyb_ref[...] = _nn(dpb, wb_ref[...])

    row = lambda w: pl.BlockSpec((tm, w), lambda i: (i, 0))
    full = _resident
    sds = jax.ShapeDtypeStruct
    return _call(
        body, name=name, grid=(t // tm,),
        in_specs=[row(D_MODEL), row(256), row(512), row(GATE_W), full(wa_t), full(wb_t), full(wout)],
        out_specs=[row(D_MODEL), row(D_MODEL), row(D_MODEL), row(GATE_W), row(256), row(512)],
        out_shape=[sds((t, D_MODEL), BF16), sds((t, D_MODEL), BF16), sds((t, D_MODEL), BF16),
                   sds((t, GATE_W), BF16), sds((t, 256), F32), sds((t, 512), F32)],
        sem=("parallel",), args=(dh, ya, o_b, zg, wa_t, wb_t, wout), hosted=hosted)


def _loss_head(h, g, target, name):
    t = h.shape[0]
    tm = 1024

    def body(h_ref, g_ref, t_ref, dh_ref, loss_ref, dg_ref):
        @pl.when(pl.program_id(0) == 0)
        def _():
            loss_ref[...] = jnp.zeros_like(loss_ref)
            dg_ref[...] = jnp.zeros_like(dg_ref)

        xhat, r = _rms(h_ref[...], g_ref[...])
        err = xhat * g_ref[...] - t_ref[...]
        loss_ref[...] += 0.5 * jnp.sum(jnp.mean(err * err, axis=-1, keepdims=True), axis=0, keepdims=True)
        dx, dg = _rms_bwd(err * (1.0 / D_MODEL), xhat, r, g_ref[...])
        dh_ref[...] = dx
        dg_ref[...] += dg

    row = pl.BlockSpec((tm, D_MODEL), lambda i: (i, 0))
    vec = pl.BlockSpec((1, D_MODEL), lambda i: (0, 0))
    return pl.pallas_call(
        body, name=name, grid=(t // tm,),
        in_specs=[row, vec, row], out_specs=[row, pl.BlockSpec((1, 128), lambda i: (0, 0)), vec],
        out_shape=[jax.ShapeDtypeStruct((t, D_MODEL), F32), jax.ShapeDtypeStruct((1, 128), F32),
                   jax.ShapeDtypeStruct((1, D_MODEL), F32)],
        compiler_params=_params("arbitrary"),
    )(h, g, target)


def _pair_sum(grad, got, name):
    _, _, r, cdim = grad.shape
    core = lax.axis_index("c").astype(jnp.int32).reshape(1)

    def body(core_ref, g_ref, s_ref, o_ref):
        o_ref[...] = (g_ref[...] + s_ref[...]).astype(BF16)

    return pl.pallas_call(
        body, name=name,
        grid_spec=pltpu.PrefetchScalarGridSpec(
            num_scalar_prefetch=1, grid=(N_CHIP,),
            in_specs=[pl.BlockSpec((None, None, r, cdim), lambda q, core_ref: (q, core_ref[0], 0, 0)),
                      pl.BlockSpec((None, None, r, cdim), lambda q, core_ref: (q, 0, 0, 0))],
            out_specs=pl.BlockSpec((None, r, cdim), lambda q, core_ref: (q, 0, 0))),
        out_shape=jax.ShapeDtypeStruct((N_CHIP, r, cdim), BF16),
        compiler_params=_params("parallel"),
    )(core, grad, got)


def _chip_sum(parts, name):
    def body(p_ref, o_ref):
        o_ref[...] = ((p_ref[0].astype(F32) + p_ref[1].astype(F32)) + p_ref[2].astype(F32)) + p_ref[3].astype(F32)

    return pl.pallas_call(
        body, name=name, out_shape=jax.ShapeDtypeStruct(parts.shape[1:], F32),
        compiler_params=pltpu.CompilerParams(vmem_limit_bytes=VMEM_LIMIT),
    )(parts)


def _adamw_math(w, g, m, v):
    m = ADAM_B1 * m + (1.0 - ADAM_B1) * g
    v = ADAM_B2 * v + (1.0 - ADAM_B2) * jnp.square(g)
    m_hat = m / (1.0 - ADAM_B1 ** ADAM_STEP)
    v_hat = v / (1.0 - ADAM_B2 ** ADAM_STEP)
    delta = -ADAM_LR * (m_hat / (jnp.sqrt(v_hat) + ADAM_EPS) + ADAM_WD * w)
    return delta, m, v


def _adamw(w, g, m, v, name):
    def body(w_ref, g_ref, m_ref, v_ref, d_ref, m2_ref, v2_ref):
        d_ref[...], m2_ref[...], v2_ref[...] = _adamw_math(w_ref[...], g_ref[...], m_ref[...], v_ref[...])

    sds = jax.ShapeDtypeStruct(w.shape, F32)
    return pl.pallas_call(body, name=name, out_shape=[sds, sds, sds],
                          compiler_params=pltpu.CompilerParams(vmem_limit_bytes=VMEM_LIMIT))(w, g, m, v)


SMALL_ROWS = 80


def _small_update(g, w, m, v, name):
    def body(g_ref, w_ref, m_ref, v_ref, gs_ref, d_ref, m2_ref, v2_ref, got_ref, send_sems, recv_sems):
        x, y, c = _place()
        me = 4 * x + 2 * y + c
        got_ref[me] = g_ref[...]
        copies = []
        for k in range(1, N_DEV):
            peer = (x ^ (k >> 2), y ^ ((k >> 1) & 1), c ^ (k & 1))
            cp = pltpu.make_async_remote_copy(
                src_ref=g_ref, dst_ref=got_ref.at[me], send_sem=send_sems.at[k - 1], recv_sem=recv_sems.at[k - 1],
                device_id=peer, device_id_type=MESH)
            cp.start()
            copies.append(cp)
        for cp in copies:
            cp.wait()
        total = got_ref[0]
        for k in range(1, N_DEV):
            total = total + got_ref[k]
        gs_ref[...] = total
        d_ref[...], m2_ref[...], v2_ref[...] = _adamw_math(w_ref[...], total, m_ref[...], v_ref[...])

    sds = jax.ShapeDtypeStruct((SMALL_ROWS, 128), F32)
    vm = pl.BlockSpec(memory_space=pltpu.VMEM)
    return pl.pallas_call(
        body, name=name, in_specs=[vm] * 4, out_specs=[vm] * 4, out_shape=[sds] * 4,
        scratch_shapes=[pltpu.VMEM((N_DEV, SMALL_ROWS, 128), F32), pltpu.SemaphoreType.DMA((N_DEV - 1,)),
                        pltpu.SemaphoreType.DMA((N_DEV - 1,))],
    )(g, w, m, v)


def _pack_small(gains, b_in, rel_bias, sinks, last):
    rows = [a.reshape(8, 128) for a in gains] + [b_in.reshape(40, 128), rel_bias.reshape(5, 128),
                                                 jnp.pad(sinks.reshape(1, 8), ((0, 0), (0, 120))), last]
    rows.append(jnp.zeros((SMALL_ROWS - 79, 128), F32))
    return jnp.concatenate(rows, axis=0)


def _unpack_small(p, like):
    out = [p[8 * i:8 * i + 8].reshape(like[i].shape) for i in range(4)]
    out.append(p[32:72].reshape(like[4].shape))
    out.append(p[72:77].reshape(like[5].shape))
    out.append(p[77, :8].reshape(like[6].shape))
    return out


def kernel(x, ffn1_norm, ffn1_w_gate, ffn1_w_up, ffn1_w_down, mix_norm, w_in, b_in, w_branch_a, w_branch_b, w_out, sinks, rel_bias, ffn2_norm, ffn2_w_gate, ffn2_w_up, ffn2_w_down, final_norm, loss_target, m_ffn1_norm, m_ffn1_w_gate, m_ffn1_w_up, m_ffn1_w_down, m_mix_norm, m_w_in, m_b_in, m_w_branch_a, m_w_branch_b, m_w_out, m_sinks, m_rel_bias, m_ffn2_norm, m_ffn2_w_gate, m_ffn2_w_up, m_ffn2_w_down, m_final_norm, v_ffn1_norm, v_ffn1_w_gate, v_ffn1_w_up, v_ffn1_w_down, v_mix_norm, v_w_in, v_b_in, v_w_branch_a, v_w_branch_b, v_w_out, v_sinks, v_rel_bias, v_ffn2_norm, v_ffn2_w_gate, v_ffn2_w_up, v_ffn2_w_down, v_final_norm):
    bsz, seq, _ = x.shape
    t = bsz * seq
    xt = x.reshape(t, D_MODEL)
    target = loss_target.reshape(t, D_MODEL)

    big = [("ffn1_w_gate", ffn1_w_gate, m_ffn1_w_gate, v_ffn1_w_gate, True),
           ("ffn1_w_up", ffn1_w_up, m_ffn1_w_up, v_ffn1_w_up, True),
           ("ffn1_w_down", ffn1_w_down, m_ffn1_w_down, v_ffn1_w_down, False),
           ("w_in", w_in, m_w_in, v_w_in, True),
           ("w_branch_a", w_branch_a, m_w_branch_a, v_w_branch_a, True),
           ("w_branch_b", w_branch_b, m_w_branch_b, v_w_branch_b, True),
           ("w_out", w_out, m_w_out, v_w_out, False),
           ("ffn2_w_gate", ffn2_w_gate, m_ffn2_w_gate, v_ffn2_w_gate, True),
           ("ffn2_w_up", ffn2_w_up, m_ffn2_w_up, v_ffn2_w_up, True),
           ("ffn2_w_down", ffn2_w_down, m_ffn2_w_down, v_ffn2_w_down, False)]
    shard = {nm: (w[0].T if tr else w[0]).astype(BF16) for nm, w, _, _, tr in big}
    full = {}

    def gather(names):
        return _Gather([shard[nm] for nm in names])

    def keep(names, got):
        for nm, gw in zip(names, got):
            full[nm] = gw.reshape(-1, gw.shape[-1])

    ffn1_names = ["ffn1_w_gate", "ffn1_w_up", "ffn1_w_down"]
    mix_names = ["w_in", "w_branch_a", "w_branch_b", "w_out"]
    ffn2_names = ["ffn2_w_gate", "ffn2_w_up", "ffn2_w_down"]
    g1, gm, g2, gf = ffn1_norm, mix_norm, ffn2_norm, final_norm.reshape(1, D_MODEL)

    keep(ffn1_names, _exchange([gather(ffn1_names)], "gather_ffn1")[0])
    (h1, n1, a1, b1), (got,) = _ffn_fwd(xt, g1, full["ffn1_w_gate"], full["ffn1_w_up"], full["ffn1_w_down"], "ffn1_fwd",
                                hosted=[gather(mix_names)])
    keep(mix_names, got)
    (u, zq, zg), (got,) = _inproj_fwd(h1, gm, full["w_in"], b_in, "inproj_fwd", hosted=[gather(ffn2_names[:2])])
    keep(ffn2_names[:2], got)
    buckets = _bucket_tiles()
    bias = _bias_build(rel_bias, buckets)
    sink_rows = jnp.broadcast_to(sinks.reshape(8, 1, 1), (8, BLOCK, 128)).reshape(8 * BLOCK, 128)
    cfs = [_Att(i) for i in range(4)]
    zfold = []
    for i, cf in enumerate(cfs):
        if cf.d == 1:
            zfold.append(zq.reshape(bsz, seq, QKV_W))
        else:
            own = jnp.concatenate([zq[:, 768 * part + 256 * i:768 * part + 256 * (i + 1)] for part in range(3)], axis=1)
            zfold.append(own.reshape(bsz, seq // cf.d, cf.d * cf.row_w))
    att = [None] * 4
    for i in (3, 0, 1, 2):
        cf = cfs[i]
        (o, lse), got = _attn_fwd(cf, zfold[i], bias, sink_rows, f"attn{i}_fwd",
                                  hosted=[gather(ffn2_names[2:])] if i == 3 else ())
        if i == 3:
            keep(ffn2_names[2:], got[0])
        att[i] = (o.reshape(t, cf.wq), lse.reshape(t, cf.wq))
    o_b, lse_b = att[3]
    h2, ya, lse_tot = _merge_fwd([a[0] for a in att[:3]], [a[1] for a in att[:3]], o_b, zg, h1,
                                 full["w_branch_a"], full["w_branch_b"], full["w_out"], "merge_fwd")
    (h3, n2, a2, b2), _ = _ffn_fwd(h2, g2, full["ffn2_w_gate"], full["ffn2_w_up"], full["ffn2_w_down"], "ffn2_fwd")
    dh3, loss_part, dgf = _loss_head(h3, gf, target, "loss_head")

    grads, pair, from_chips = {}, {}, {}

    def by_owner(nm):
        return grads[nm].reshape(N_CHIP, 2, -1, grads[nm].shape[-1])

    def to_core(names):
        return _CoreExchange([by_owner(nm) for nm in names])

    def pair_up(names, got):
        for nm, sib in zip(names, got):
            pair[nm] = _pair_sum(by_owner(nm), sib, f"pair_sum_{nm}")

    def to_chips(names):
        return _ChipExchange([pair[nm] for nm in names])

    def landed(names, got):
        for nm, parts in zip(names, got):
            from_chips[nm] = parts

    dh2, dg2, hff, da, db, dhh = _ffn_bwd(h2, a2, b2, g2, dh3, full["ffn2_w_gate"], full["ffn2_w_up"],
                                          full["ffn2_w_down"], "ffn2_bwd")
    grads["ffn2_w_gate"], _ = _tn_matmul(da, n2, 1408, "ffn2_dgate")
    grads["ffn2_w_up"], _ = _tn_matmul(db, n2, 1408, "ffn2_dup")
    grads["ffn2_w_down"], _ = _tn_matmul(hff, dhh, 1408, "ffn2_ddown")
    (merged, dpa, dpb, dzg, dya, dyb), (got,) = _merge_bwd(
        dh2, ya, o_b, zg, full["w_branch_a"], full["w_branch_b"], full["w_out"], "merge_bwd",
        hosted=[to_core(ffn2_names)])
    pair_up(ffn2_names, got)
    dq, dk, dv, dbias, dsink = [None] * 4, [None] * 4, [None] * 4, [None] * 4, None
    for i in (3, 0, 1, 2):
        cf = cfs[i]
        lt_i, dy_i, y_i = (lse_tot, dya, ya) if i < 3 else (lse_b, dyb, o_b)
        shp = (bsz, seq // cf.d, cf.d * cf.wq)
        hosted = {3: lambda: [to_chips(ffn2_names[:2])], 0: lambda: [to_chips(ffn2_names[2:])]}.get(i, list)()
        res, got = _attn_bwd(cf, zfold[i], bias, sink_rows, lt_i.reshape(shp), dy_i.reshape(shp), y_i.reshape(shp),
                             f"attn{i}_bwd", hosted=hosted)
        if i == 3:
            landed(ffn2_names[:2], got[0])
        elif i == 0:
            landed(ffn2_names[2:], got[0])
        dq[i] = res[0].reshape(t, cf.wq)
        dk[i] = res[1].reshape(t, cf.wkv)
        dv[i] = res[2].reshape(t, cf.wkv)
        dbias[i] = res[3]
        if cf.sinks:
            dsink = res[4]
    pieces = dq[:3] + dk[:3] + dv[:3] + [dq[3], dk[3], dv[3], dzg]
    dh1, dz, db_in, dgm = _inproj_bwd(pieces, h1, gm, dh2, full["w_in"], "inproj_bwd")
    dx, dg1, hff, da, db, dhh = _ffn_bwd(xt, a1, b1, g1, dh1, full["ffn1_w_gate"], full["ffn1_w_up"],
                                         full["ffn1_w_down"], "ffn1_bwd")
    grads["w_in"], _ = _tn_matmul(dz, u, 1280, "dw_in")
    grads["ffn1_w_down"], (got,) = _tn_matmul(hff, dhh, 1408, "ffn1_ddown", hosted=[to_core(["w_in"])])
    pair_up(["w_in"], got)
    grads["ffn1_w_gate"], got = _tn_matmul(da, n1, 1408, "ffn1_dgate",
                                           hosted=[to_chips(["w_in"]), to_core(["ffn1_w_down"])])
    landed(["w_in"], got[0])
    pair_up(["ffn1_w_down"], got[1])
    grads["ffn1_w_up"], got = _tn_matmul(db, n1, 1408, "ffn1_dup",
                                         hosted=[to_chips(["ffn1_w_down"]), to_core(["ffn1_w_gate"])])
    landed(["ffn1_w_down"], got[0])
    pair_up(["ffn1_w_gate"], got[1])
    grads["w_out"], got = _tn_matmul(merged, dh2, 1024, "dw_out",
                                     hosted=[to_chips(["ffn1_w_gate"]), to_core(["ffn1_w_up"])])
    landed(["ffn1_w_gate"], got[0])
    pair_up(["ffn1_w_up"], got[1])
    grads["w_branch_b"], got = _tn_matmul(dpb, o_b, 1024, "dw_branch_b",
                                          hosted=[to_chips(["ffn1_w_up"]), to_core(["w_out"])])
    landed(["ffn1_w_up"], got[0])
    pair_up(["w_out"], got[1])
    grads["w_branch_a"], got = _tn_matmul(dpa, ya, 1024, "dw_branch_a",
                                          hosted=[to_chips(["w_out"]), to_core(["w_branch_b"])])
    landed(["w_out"], got[0])
    pair_up(["w_branch_b"], got[1])
    got = _exchange([to_chips(["w_branch_b"]), to_core(["w_branch_a"])], "reduce_scatter_tail1")
    landed(["w_branch_b"], got[0])
    pair_up(["w_branch_a"], got[1])
    landed(["w_branch_a"], _exchange([to_chips(["w_branch_a"])], "reduce_scatter_tail2")[0])
    dtable, dsinks = _bias_reduce(jnp.concatenate(dbias, axis=0), buckets, dsink)

    out_g, out_d, out_m, out_v = {}, {}, {}, {}
    for nm, w, m, v, tr in big:
        g = _chip_sum(from_chips[nm], f"chip_sum_{nm}")
        g = (g.T if tr else g).reshape(w.shape)
        out_g[nm] = g
        d2, m2, v2 = _adamw(w[0], g[0], m[0], v[0], f"adamw_{nm}")
        out_d[nm], out_m[nm], out_v[nm] = d2[None], m2[None], v2[None]

    small = [("ffn1_norm", ffn1_norm, m_ffn1_norm, v_ffn1_norm), ("mix_norm", mix_norm, m_mix_norm, v_mix_norm),
             ("ffn2_norm", ffn2_norm, m_ffn2_norm, v_ffn2_norm), ("final_norm", final_norm, m_final_norm, v_final_norm),
             ("b_in", b_in, m_b_in, v_b_in), ("rel_bias", rel_bias, m_rel_bias, v_rel_bias),
             ("sinks", sinks, m_sinks, v_sinks)]
    zero_row = jnp.zeros((1, 128), F32)
    pack = lambda arrs, last: _pack_small(arrs[:4], arrs[4], arrs[5], arrs[6], last)
    g_small = pack([dg1, dgm, dg2, dgf, db_in, dtable[:, :TOTAL_HEADS], dsinks[:, 0]], loss_part)
    packed = [pack([s[k] for s in small], zero_row) for k in (1, 2, 3)]
    gs, ds, ms, vs = _small_update(g_small, *packed, "small_update")
    like = [s[1] for s in small]
    for nm_s, g_, d_, m_, v_ in zip([s[0] for s in small], _unpack_small(gs, like), _unpack_small(ds, like),
                                    _unpack_small(ms, like), _unpack_small(vs, like)):
        out_g[nm_s], out_d[nm_s], out_m[nm_s], out_v[nm_s] = g_, d_, m_, v_
    loss = gs[78, 0]

    order = ["ffn1_norm", "ffn1_w_gate", "ffn1_w_up", "ffn1_w_down", "mix_norm", "w_in", "b_in", "w_branch_a",
             "w_branch_b", "w_out", "sinks", "rel_bias", "ffn2_norm", "ffn2_w_gate", "ffn2_w_up", "ffn2_w_down",
             "final_norm"]
    return (loss, dx.reshape(x.shape), *[out_g[k] for k in order], *[out_d[k] for k in order],
            *[out_m[k] for k in order], *[out_v[k] for k in order])
```

```python
import functools
import math

import numpy as np
import jax
import jax.numpy as jnp
from jax import lax
from jax.experimental import pallas as pl
from jax.experimental.pallas import tpu as pltpu

D_MODEL = 1024
D_FF = 2816
FF_CHUNK = 256
HEAD_DIM = 64
BLOCK = 128
N_BUCKETS = 32
MAX_DISTANCE = 2048
A_HEADS = 12
TOTAL_HEADS = 20
DIL_GROUPS = ((128, 1), (512, 4), (2048, 16))
B_WINDOW = 128
QKV_W = 3072
GATE_W = 2048
D_IN = QKV_W + GATE_W
EPS = 1e-6
NEG = -1e30
N_DEV = 8
N_CHIP = 4
ADAM_LR, ADAM_B1, ADAM_B2, ADAM_EPS, ADAM_WD, ADAM_STEP = 0.001, 0.9, 0.999, 1e-08, 0.01, 10
VMEM_LIMIT = 56 * 1024 * 1024
MESH = pl.DeviceIdType.MESH
BF16 = jnp.bfloat16
F32 = jnp.float32
ANY = pl.BlockSpec(memory_space=pl.ANY)


def _params(*sem):
    return pltpu.CompilerParams(dimension_semantics=sem, vmem_limit_bytes=VMEM_LIMIT)


def _resident(a):
    return pl.BlockSpec(a.shape, lambda i: (0, 0), pipeline_mode=pl.Buffered(1))


def _place():
    return lax.axis_index("x"), lax.axis_index("y"), lax.axis_index("c")


class _Gather:
    def __init__(self, shards):
        self.ins = list(shards)
        n = self.n = len(shards)
        self.out_shape = [jax.ShapeDtypeStruct((N_DEV,) + s.shape, s.dtype) for s in shards]
        self.scratch = [pltpu.SemaphoreType.DMA((7 * n,)), pltpu.SemaphoreType.DMA((7 * n,)),
                        pltpu.SemaphoreType.DMA((n,))]

    def _copies(self, ins, outs, sems):
        send_sems, recv_sems, local_sems = sems
        x, y, c = _place()
        me, sibling = (x, y, c), (x, y, 1 - c)
        chips = [(1 - x, y), (x, 1 - y), (1 - x, 1 - y)]

        def copy(i, k, block, to, src=None):
            dst = outs[i].at[4 * block[0] + 2 * block[1] + block[2]]
            return pltpu.make_async_remote_copy(
                src_ref=dst if src is None else src, dst_ref=dst, send_sem=send_sems.at[7 * i + k],
                recv_sem=recv_sems.at[7 * i + k], device_id=to, device_id_type=MESH)

        n = self.n
        mine = [pltpu.make_async_copy(ins[i], outs[i].at[4 * x + 2 * y + c], local_sems.at[i]) for i in range(n)]
        first = [copy(i, 0, me, sibling, src=ins[i]) for i in range(n)]
        first += [copy(i, 1 + j, me, (*chip, c), src=ins[i]) for i in range(n) for j, chip in enumerate(chips)]
        landed = [copy(i, 1 + j, (*chip, c), me) for j, chip in enumerate(chips) for i in range(n)]
        passed = [copy(i, 4 + j, (*chip, c), sibling) for j, chip in enumerate(chips) for i in range(n)]
        from_sibling = [copy(i, 0, sibling, me) for i in range(n)]
        from_sibling += [copy(i, 4 + j, (*chip, 1 - c), me) for i in range(n) for j, chip in enumerate(chips)]
        return mine, first, landed, passed, from_sibling

    def start(self, ins, outs, sems):
        mine, first, _, _, _ = self._copies(ins, outs, sems)
        for cp in mine + first:
            cp.start()

    def mid(self, ins, outs, sems):
        _, _, landed, passed, _ = self._copies(ins, outs, sems)
        for got, fwd in zip(landed, passed):
            got.wait_recv()
            fwd.start()

    def end(self, ins, outs, sems):
        mine, first, _, passed, from_sibling = self._copies(ins, outs, sems)
        for cp in from_sibling:
            cp.wait_recv()
        for cp in first + passed:
            cp.wait_send()
        for cp in mine:
            cp.wait()


class _CoreExchange:
    def __init__(self, grads):
        self.ins = list(grads)
        n = self.n = len(grads)
        self.out_shape = [jax.ShapeDtypeStruct((N_CHIP, 1) + g.shape[2:], g.dtype) for g in grads]
        self.scratch = [pltpu.SemaphoreType.DMA((n,)), pltpu.SemaphoreType.DMA((n,))]

    def _copies(self, ins, outs, sems):
        x, y, c = _place()
        return [pltpu.make_async_remote_copy(
            src_ref=ins[i].at[:, pl.ds(1 - c, 1)], dst_ref=outs[i], send_sem=sems[0].at[i],
            recv_sem=sems[1].at[i], device_id=(x, y, 1 - c), device_id_type=MESH) for i in range(self.n)]

    def start(self, ins, outs, sems):
        for cp in self._copies(ins, outs, sems):
            cp.start()

    mid = None

    def end(self, ins, outs, sems):
        for cp in self._copies(ins, outs, sems):
            cp.wait()


class _ChipExchange:
    def __init__(self, parts):
        self.ins = list(parts)
        n = self.n = len(parts)
        self.out_shape = [jax.ShapeDtypeStruct(p.shape, p.dtype) for p in parts]
        self.scratch = [pltpu.SemaphoreType.DMA((3 * n,)), pltpu.SemaphoreType.DMA((3 * n,)),
                        pltpu.SemaphoreType.DMA((n,))]

    def _copies(self, ins, outs, sems):
        send_sems, recv_sems, local_sems = sems
        x, y, c = _place()
        my_chip = 2 * x + y
        copies = []
        for i in range(self.n):
            copies.append(pltpu.make_async_copy(ins[i].at[my_chip], outs[i].at[my_chip], local_sems.at[i]))
            for k, (qx, qy) in enumerate([(1 - x, y), (x, 1 - y), (1 - x, 1 - y)]):
                copies.append(pltpu.make_async_remote_copy(
                    src_ref=ins[i].at[2 * qx + qy], dst_ref=outs[i].at[my_chip], send_sem=send_sems.at[3 * i + k],
                    recv_sem=recv_sems.at[3 * i + k], device_id=(qx, qy, c), device_id_type=MESH))
        return copies

    def start(self, ins, outs, sems):
        for cp in self._copies(ins, outs, sems):
            cp.start()

    mid = None

    def end(self, ins, outs, sems):
        for cp in self._copies(ins, outs, sems):
            cp.wait()


def _call(body, *, name, grid, in_specs, out_specs, out_shape, args, scratch=(), sem=None, hosted=()):
    n_in, n_out, n_scr = len(in_specs), len(out_specs), len(scratch)
    x_in = [len(p.ins) for p in hosted]
    x_scr = [len(p.scratch) for p in hosted]
    steps = int(np.prod(grid))

    def wrapped(*refs):
        refs = list(refs)
        ins, refs = refs[:n_in], refs[n_in:]
        x_ins = [[refs.pop(0) for _ in range(k)] for k in x_in]
        outs, refs = refs[:n_out], refs[n_out:]
        x_outs = [[refs.pop(0) for _ in range(k)] for k in x_in]
        scr, refs = refs[:n_scr], refs[n_scr:]
        x_sems = [[refs.pop(0) for _ in range(k)] for k in x_scr]
        step = 0
        for d in range(len(grid)):
            step = step * grid[d] + pl.program_id(d)

        def phase(which, at):
            fns = [(getattr(p, which), a) for p, a in zip(hosted, zip(x_ins, x_outs, x_sems)) if getattr(p, which)]
            if fns:
                @pl.when(step == at)
                def _():
                    for fn, a in fns:
                        fn(*a)

        phase("start", 0)
        phase("mid", (3 * steps) // 4)
        body(*ins, *outs, *scr)
        phase("end", steps - 1)

    results = pl.pallas_call(
        wrapped, name=name, grid=grid,
        in_specs=list(in_specs) + [ANY] * sum(x_in), out_specs=list(out_specs) + [ANY] * sum(x_in),
        out_shape=list(out_shape) + [s for p in hosted for s in p.out_shape],
        scratch_shapes=list(scratch) + [s for p in hosted for s in p.scratch],
        compiler_params=_params(*(("arbitrary",) * len(grid) if hosted else sem)),
    )(*args, *[a for p in hosted for a in p.ins])
    own, rest = list(results[:n_out]), list(results[n_out:])
    return own, [[rest.pop(0) for _ in range(k)] for k in x_in]


def _exchange(programs, name):
    x_in = [len(p.ins) for p in programs]
    x_scr = [len(p.scratch) for p in programs]

    def body(*refs):
        refs = list(refs)
        x_ins = [[refs.pop(0) for _ in range(k)] for k in x_in]
        x_outs = [[refs.pop(0) for _ in range(k)] for k in x_in]
        x_sems = [[refs.pop(0) for _ in range(k)] for k in x_scr]
        for which in ("start", "mid", "end"):
            for p, a in zip(programs, zip(x_ins, x_outs, x_sems)):
                if getattr(p, which):
                    getattr(p, which)(*a)

    results = list(pl.pallas_call(
        body, name=name, in_specs=[ANY] * sum(x_in), out_specs=[ANY] * sum(x_in),
        out_shape=[s for p in programs for s in p.out_shape],
        scratch_shapes=[s for p in programs for s in p.scratch],
    )(*[a for p in programs for a in p.ins]))
    return [[results.pop(0) for _ in range(k)] for k in x_in]


def _nt(a, b):
    return lax.dot_general(a, b, (((1,), (1,)), ((), ())), preferred_element_type=F32)


def _nn(a, b):
    return lax.dot_general(a, b, (((1,), (0,)), ((), ())), preferred_element_type=F32)


def _tn(a, b):
    return lax.dot_general(a, b, (((0,), (0,)), ((), ())), preferred_element_type=F32)


def _rms(x, g):
    r = lax.rsqrt(jnp.mean(x * x, axis=-1, keepdims=True) + EPS)
    return x * r, r


def _rms_bwd(dn, xhat, r, g):
    dg = jnp.sum(dn * xhat, axis=0, keepdims=True)
    dxh = dn * g
    dx = r * (dxh - xhat * jnp.mean(dxh * xhat, axis=-1, keepdims=True))
    return dx, dg


def _ffn_fwd(x, g, wg_t, wu_t, wd, name, hosted=()):
    t = x.shape[0]
    tm = 256

    def body(x_ref, g_ref, wg_ref, wu_ref, wd_ref, h_ref, n_ref, a_ref, b_ref, hff_ref):
        xhat, _ = _rms(x_ref[...], g_ref[...])
        n = (xhat * g_ref[...]).astype(BF16)
        n_ref[...] = n
        for c in range(0, D_FF, FF_CHUNK):
            cols = slice(c, c + FF_CHUNK)
            a = _nt(n, wg_ref[cols, :])
            b = _nt(n, wu_ref[cols, :])
            a_ref[:, cols] = a
            b_ref[:, cols] = b
            hff_ref[:, cols] = (a * jax.nn.sigmoid(a) * b).astype(BF16)
        h_ref[...] = x_ref[...] + 0.5 * _nn(hff_ref[...], wd_ref[...])

    row = pl.BlockSpec((tm, D_MODEL), lambda i: (i, 0))
    hid = pl.BlockSpec((tm, D_FF), lambda i: (i, 0))
    return _call(
        body, name=name, grid=(t // tm,),
        in_specs=[row, _resident(g), _resident(wg_t), _resident(wu_t), _resident(wd)],
        out_specs=[row, row, hid, hid, hid],
        out_shape=[jax.ShapeDtypeStruct((t, D_MODEL), F32), jax.ShapeDtypeStruct((t, D_MODEL), BF16),
                   jax.ShapeDtypeStruct((t, D_FF), F32), jax.ShapeDtypeStruct((t, D_FF), F32),
                   jax.ShapeDtypeStruct((t, D_FF), BF16)],
        sem=("parallel",), args=(x, g, wg_t, wu_t, wd), hosted=hosted)


def _ffn_bwd(x, a_pre, b_pre, g, dh, wg_t, wu_t, wd, name):
    t = x.shape[0]
    tm = 256

    def body(x_ref, a_ref, b_ref, g_ref, dh_ref, wg_ref, wu_ref, wd_ref,
             dx_ref, dg_ref, da_ref, db_ref):
        @pl.when(pl.program_id(0) == 0)
        def _():
            dg_ref[...] = jnp.zeros_like(dg_ref)

        dhh = (0.5 * dh_ref[...]).astype(BF16)
        for c in range(0, D_FF, FF_CHUNK):
            cols = slice(c, c + FF_CHUNK)
            a = a_ref[:, cols]
            b = b_ref[:, cols]
            s = jax.nn.sigmoid(a)
            silu = a * s
            dhff = _nt(dhh, wd_ref[cols, :])
            da_ref[:, cols] = (dhff * b * (s * (1.0 + a * (1.0 - s)))).astype(BF16)
            db_ref[:, cols] = (dhff * silu).astype(BF16)
        dn = _nn(da_ref[...], wg_ref[...]) + _nn(db_ref[...], wu_ref[...])
        xhat, r = _rms(x_ref[...], g_ref[...])
        dx, dg = _rms_bwd(dn, xhat, r, g_ref[...])
        dx_ref[...] = dh_ref[...] + dx
        dg_ref[...] += dg

    row = pl.BlockSpec((tm, D_MODEL), lambda i: (i, 0))
    hid = pl.BlockSpec((tm, D_FF), lambda i: (i, 0))
    return pl.pallas_call(
        body, name=name, grid=(t // tm,),
        in_specs=[row, hid, hid, _resident(g), row, _resident(wg_t), _resident(wu_t), _resident(wd)],
        out_specs=[row, pl.BlockSpec((1, D_MODEL), lambda i: (0, 0)), hid, hid],
        out_shape=[jax.ShapeDtypeStruct((t, D_MODEL), F32), jax.ShapeDtypeStruct((1, D_MODEL), F32),
                   jax.ShapeDtypeStruct((t, D_FF), BF16), jax.ShapeDtypeStruct((t, D_FF), BF16)],
        compiler_params=_params("arbitrary"),
    )(x, a_pre, b_pre, g, dh, wg_t, wu_t, wd)


def _tn_matmul(a, b, rc, name, hosted=(), scale=None):
    t, r = a.shape
    c = b.shape[1]
    tk = 1024

    def body(a_ref, b_ref, o_ref):
        @pl.when(pl.program_id(1) == 0)
        def _():
            o_ref[...] = jnp.zeros_like(o_ref)

        o_ref[...] += _tn(a_ref[...].astype(BF16), b_ref[...].astype(BF16))
        if scale is not None:
            @pl.when(pl.program_id(1) == t // tk - 1)
            def _():
                o_ref[...] *= scale

    (out,), got = _call(
        body, name=name, grid=(r // rc, t // tk),
        in_specs=[pl.BlockSpec((tk, rc), lambda i, k: (k, i)), pl.BlockSpec((tk, c), lambda i, k: (k, 0))],
        out_specs=[pl.BlockSpec((rc, c), lambda i, k: (i, 0))],
        out_shape=[jax.ShapeDtypeStruct((r, c), F32)],
        sem=("parallel", "arbitrary"), args=(a, b), hosted=hosted)
    return out, got


PIECE_W = (256,) * 9 + (512, 128, 128, GATE_W)


def _inproj_fwd(h, g, w_t, b_in, name, hosted=()):
    t = h.shape[0]
    tm, nc = 512, 512

    def body(h_ref, g_ref, w_ref, b_ref, u_ref, zq_ref, zg_ref):
        xhat, _ = _rms(h_ref[...], g_ref[...])
        u = (xhat * g_ref[...]).astype(BF16)
        u_ref[...] = u
        for c in range(D_IN // nc):
            z = _nt(u, w_ref[c * nc:(c + 1) * nc, :]) + b_ref[:, c * nc:(c + 1) * nc]
            if c < QKV_W // nc:
                zq_ref[:, c * nc:(c + 1) * nc] = z.astype(BF16)
            else:
                zg_ref[:, c * nc - QKV_W:(c + 1) * nc - QKV_W] = z

    row = lambda w: pl.BlockSpec((tm, w), lambda i: (i, 0))
    full = _resident
    return _call(
        body, name=name, grid=(t // tm,),
        in_specs=[row(D_MODEL), full(g), full(w_t), full(b_in)],
        out_specs=[row(D_MODEL), row(QKV_W), row(GATE_W)],
        out_shape=[jax.ShapeDtypeStruct((t, D_MODEL), BF16), jax.ShapeDtypeStruct((t, QKV_W), BF16),
                   jax.ShapeDtypeStruct((t, GATE_W), F32)],
        sem=("parallel",), args=(h, g, w_t, b_in), hosted=hosted)


def _inproj_bwd(pieces, h, g, dh_res, w_t, name):
    t = h.shape[0]
    tm = 256
    npiece = len(PIECE_W)
    offs = np.concatenate([[0], np.cumsum(PIECE_W)]).tolist()

    def body(*refs):
        p_refs = refs[:npiece]
        h_ref, g_ref, dhr_ref, w_ref, dh_ref, dz_ref, db_ref, dg_ref = refs[npiece:]
        i = pl.program_id(0)

        @pl.when(i == 0)
        def _():
            db_ref[...] = jnp.zeros_like(db_ref)
            dg_ref[...] = jnp.zeros_like(dg_ref)

        du = jnp.zeros((tm, D_MODEL), F32)
        for k in range(npiece):
            o, w = offs[k], PIECE_W[k]
            for c0 in range(0, w, 512):
                cw = min(512, w - c0)
                pz = p_refs[k][:, c0:c0 + cw]
                dz_ref[:, o + c0:o + c0 + cw] = pz
                db_ref[:, o + c0:o + c0 + cw] += jnp.sum(pz.astype(F32), axis=0, keepdims=True)
                du = du + _nn(pz, w_ref[o + c0:o + c0 + cw, :])
        xhat, r = _rms(h_ref[...], g_ref[...])
        dx, dg = _rms_bwd(du, xhat, r, g_ref[...])
        dh_ref[...] = dhr_ref[...] + dx
        dg_ref[...] += dg

    row = lambda w: pl.BlockSpec((tm, w), lambda i: (i, 0))
    full = lambda shp: pl.BlockSpec(shp, lambda i: (0, 0))
    return pl.pallas_call(
        body, name=name, grid=(t // tm,),
        in_specs=[row(w) for w in PIECE_W] + [row(D_MODEL), _resident(g), row(D_MODEL), _resident(w_t)],
        out_specs=[row(D_MODEL), row(D_IN), full((1, D_IN)), full((1, D_MODEL))],
        out_shape=[jax.ShapeDtypeStruct((t, D_MODEL), F32), jax.ShapeDtypeStruct((t, D_IN), BF16),
                   jax.ShapeDtypeStruct((1, D_IN), F32), jax.ShapeDtypeStruct((1, D_MODEL), F32)],
        compiler_params=_params("arbitrary"),
    )(*pieces, h, g, dh_res, w_t)


def _t5_bucket(dist):
    max_exact = N_BUCKETS // 2
    n = jnp.maximum(dist, 0)
    nf = jnp.maximum(n, 1).astype(jnp.float32)
    large = max_exact + (jnp.log(nf / max_exact) / math.log(MAX_DISTANCE / max_exact)
                         * (N_BUCKETS - max_exact)).astype(jnp.int32)
    large = jnp.minimum(large, N_BUCKETS - 1)
    return jnp.where(n < max_exact, n, large)


ATT_CFG = ((1, 128, 0, 4), (4, 128, 4, 4), (16, 128, 8, 4), (1, B_WINDOW - 1, A_HEADS, 8))


def _bucket_tiles():
    qi = jnp.arange(BLOCK)[:, None]
    ki = jnp.arange(2 * BLOCK)[None, :]
    dist = qi + BLOCK - ki
    return jnp.stack([_t5_bucket(dist * cfg[0]) for cfg in ATT_CFG]).astype(jnp.int32)


def _band(max_steps):
    row = lax.broadcasted_iota(jnp.int32, (BLOCK, 2 * BLOCK), 0)
    col = lax.broadcasted_iota(jnp.int32, (BLOCK, 2 * BLOCK), 1)
    dist = row + BLOCK - col
    return (dist >= 0) & (dist <= max_steps)


def _bias_build(table, buckets):
    def body(tab_ref, bt_ref, out_ref):
        col = lax.broadcasted_iota(jnp.int32, (BLOCK, 2 * BLOCK), 1)
        for ci, (_, max_steps, h0, nh) in enumerate(ATT_CFG):
            bt = bt_ref[ci]
            band = _band(max_steps)
            for h in range(h0, h0 + nh):
                acc = lax.fori_loop(0, N_BUCKETS, lambda b, acc: jnp.where(bt == b, tab_ref[b, h], acc),
                                    jnp.zeros((BLOCK, 2 * BLOCK), F32))
                out_ref[0, BLOCK * h:BLOCK * (h + 1), :] = jnp.where(band & (col >= BLOCK), acc, NEG)
                out_ref[1, BLOCK * h:BLOCK * (h + 1), :] = jnp.where(band, acc, NEG)

    return pl.pallas_call(
        body, name="bias_build",
        in_specs=[pl.BlockSpec(memory_space=pltpu.SMEM), pl.BlockSpec(memory_space=pltpu.VMEM)],
        out_specs=pl.BlockSpec(memory_space=pltpu.VMEM),
        out_shape=jax.ShapeDtypeStruct((2, TOTAL_HEADS * BLOCK, 2 * BLOCK), F32),
    )(table, buckets)


def _bias_reduce(dbias, buckets, dsink_rows):
    def body(db_ref, bt_ref, ds_ref, out_ref, sink_ref):
        ri = lax.broadcasted_iota(jnp.int32, (N_BUCKETS, 128), 0)
        ci = lax.broadcasted_iota(jnp.int32, (N_BUCKETS, 128), 1)

        def per_bucket(b, acc):
            for cfg_i, (_, _, h0, nh) in enumerate(ATT_CFG):
                hit = bt_ref[cfg_i] == b
                for h in range(h0, h0 + nh):
                    val = jnp.sum(jnp.where(hit, db_ref[BLOCK * h:BLOCK * (h + 1), :], 0.0))
                    acc = jnp.where((ri == b) & (ci == h), val, acc)
            return acc

        out_ref[...] = lax.fori_loop(0, N_BUCKETS, per_bucket, jnp.zeros((N_BUCKETS, 128), F32))
        for h in range(8):
            sink_ref[h:h + 1, :] = jnp.sum(ds_ref[BLOCK * h:BLOCK * (h + 1), :], axis=0, keepdims=True)

    return pl.pallas_call(
        body, name="bias_reduce",
        in_specs=[pl.BlockSpec(memory_space=pltpu.VMEM)] * 3,
        out_specs=[pl.BlockSpec(memory_space=pltpu.VMEM)] * 2,
        out_shape=[jax.ShapeDtypeStruct((N_BUCKETS, 128), F32), jax.ShapeDtypeStruct((8, 128), F32)],
    )(dbias, buckets, dsink_rows)


class _Att:
    def __init__(self, cfg_i):
        stride, _, h0, nh = ATT_CFG[cfg_i]
        self.d = stride if cfg_i < 3 else 1
        self.h0, self.nh = h0, nh
        self.row_w = QKV_W if self.d == 1 else 3 * 256
        if cfg_i < 3:
            self.nq, self.wkv = 1, 256
            self.q_unit = [cfg_i if self.d == 1 else 0]
            self.k_unit, self.v_unit = (3 + cfg_i, 6 + cfg_i) if self.d == 1 else (1, 2)
            self.sinks = False
        else:
            self.nq, self.wkv = 2, 128
            self.q_unit = [9, 10]
            self.k_unit, self.v_unit = 22, 23
            self.sinks = True
        self.wq = 256 * self.nq


def _att_in_specs(cf, bsz):
    uq, ukv = cf.row_w // 256, cf.row_w // cf.wkv
    specs = [pl.BlockSpec((bsz, BLOCK, 256), functools.partial(lambda r, j, u: (0, j, r * uq + u), u=u))
             for u in cf.q_unit]
    for unit in (cf.k_unit, cf.v_unit):
        specs.append(pl.BlockSpec((bsz, BLOCK, cf.wkv),
                                  functools.partial(lambda r, j, u: (0, jnp.maximum(j - 1, 0), r * ukv + u), u=unit)))
        specs.append(pl.BlockSpec((bsz, BLOCK, cf.wkv),
                                  functools.partial(lambda r, j, u: (0, j, r * ukv + u), u=unit)))
    for qb in range(cf.nq):
        specs.append(pl.BlockSpec((None, HEADS_PER_BLOCK * BLOCK, 2 * BLOCK), functools.partial(
            lambda r, j, u: (jnp.minimum(j, 1), u, 0), u=cf.h0 // HEADS_PER_BLOCK + qb)))
    if cf.sinks:
        specs += [pl.BlockSpec((HEADS_PER_BLOCK * BLOCK, 128), functools.partial(lambda r, j, u: (u, 0), u=qb))
                  for qb in range(cf.nq)]
    return specs


HEADS_PER_BLOCK = 4


def _head_masks(rows):
    head = lax.broadcasted_iota(jnp.int32, (rows, 256), 1) // HEAD_DIM
    return [head == h for h in range(HEADS_PER_BLOCK)]


def _stack_heads(x, masks):
    return jnp.concatenate([jnp.where(m, x, jnp.zeros_like(x)) for m in masks], axis=0)


def _unstack_heads(x4, masks):
    blocks = [x4[BLOCK * h:BLOCK * (h + 1)] for h in range(HEADS_PER_BLOCK)]
    return jnp.where(masks[0], blocks[0], jnp.where(masks[1], blocks[1], jnp.where(masks[2], blocks[2], blocks[3])))


def _row_value(x):
    return jnp.max(x, axis=-1, keepdims=True)


def _kv_operand(cf, x, qb):
    if cf.wkv == 256:
        return x
    lane = lax.broadcasted_iota(jnp.int32, x.shape, 1)
    swapped = pltpu.roll(x, HEAD_DIM, 1)
    half = jnp.where(lane < HEAD_DIM, x, swapped) if qb == 0 else jnp.where(lane < HEAD_DIM, swapped, x)
    return jnp.concatenate([half, half], axis=1)


def _kv_fold(cf, grads):
    if cf.wkv == 256:
        return grads[0]
    folded = []
    for g in grads:
        x = g[:, :128] + g[:, 128:]
        folded.append(x + pltpu.roll(x, HEAD_DIM, 1))
    lane = lax.broadcasted_iota(jnp.int32, folded[0].shape, 1)
    return jnp.where(lane < HEAD_DIM, folded[0], folded[1])


def _attn_fwd(cf, zf, bias, sinks, name, hosted=()):
    bsz, l, _ = zf.shape
    nb = l // BLOCK

    def body(*refs):
        refs = list(refs)
        q_refs = [refs.pop(0) for _ in range(cf.nq)]
        kp_ref, kc_ref, vp_ref, vc_ref = [refs.pop(0) for _ in range(4)]
        bias_refs = [refs.pop(0) for _ in range(cf.nq)]
        sink_refs = [refs.pop(0) for _ in range(cf.nq)] if cf.sinks else None
        o_ref, lse_ref = refs
        masks, kv_masks = _head_masks(BLOCK), _head_masks(2 * BLOCK)
        sinks4 = [_row_value(ref[...]) for ref in sink_refs] if cf.sinks else None
        for bi in range(bsz):
            k = jnp.concatenate([kp_ref[bi], kc_ref[bi]], axis=0)
            v = jnp.concatenate([vp_ref[bi], vc_ref[bi]], axis=0)
            for qb in range(cf.nq):
                cols = slice(256 * qb, 256 * (qb + 1))
                kb, vb = _kv_operand(cf, k, qb), _kv_operand(cf, v, qb)
                q4 = _stack_heads(q_refs[qb][bi] * (HEAD_DIM ** -0.5), masks)
                s = _nt(q4, kb) + bias_refs[qb][...]
                m = jnp.max(s, axis=-1, keepdims=True)
                if cf.sinks:
                    sk = sinks4[qb]
                    m = jnp.maximum(m, sk)
                p = jnp.exp(s - m)
                den = jnp.sum(p, axis=-1, keepdims=True)
                if cf.sinks:
                    den = den + jnp.exp(sk - m)
                pn = (p * (1.0 / den)).astype(BF16)
                p_lanes = jnp.concatenate([pn[BLOCK * h:BLOCK * (h + 1)] for h in range(HEADS_PER_BLOCK)], axis=1)
                v4 = jnp.concatenate([jnp.where(mk, vb, jnp.zeros_like(vb)) for mk in kv_masks], axis=0)
                o_ref[bi, :, cols] = _nn(p_lanes, v4)
                lse_ref[bi, :, cols] = _unstack_heads(
                    jnp.broadcast_to(m + jnp.log(den), (HEADS_PER_BLOCK * BLOCK, 256)), masks)

    in_specs = _att_in_specs(cf, bsz)
    args = [zf] * (cf.nq + 4) + [bias] * cf.nq + ([sinks] * cf.nq if cf.sinks else [])
    out = pl.BlockSpec((bsz, BLOCK, cf.wq), lambda r, j: (0, j, r))
    shape = jax.ShapeDtypeStruct((bsz, l, cf.d * cf.wq), F32)
    return _call(
        body, name=name, grid=(cf.d, nb), in_specs=in_specs, out_specs=[out, out], out_shape=[shape, shape],
        sem=("parallel", "arbitrary"), args=args, hosted=hosted)


def _attn_bwd(cf, zf, bias, sinks, lse_tot, dy, y, name, hosted=()):
    bsz, l, _ = zf.shape
    nb = l // BLOCK

    def body(*refs):
        refs = list(refs)
        q_refs = [refs.pop(0) for _ in range(cf.nq)]
        kp_ref, kc_ref, vp_ref, vc_ref = [refs.pop(0) for _ in range(4)]
        bias_refs = [refs.pop(0) for _ in range(cf.nq)]
        sink_refs = [refs.pop(0) for _ in range(cf.nq)] if cf.sinks else None
        lt_ref, dy_ref, y_ref, dq_ref, dk_ref, dv_ref, dbias_ref = [refs.pop(0) for _ in range(7)]
        dsink_ref = refs.pop(0) if cf.sinks else None
        dk_acc, dv_acc = refs
        r, j = pl.program_id(0), pl.program_id(1)

        @pl.when((r == 0) & (j == 0))
        def _():
            dbias_ref[...] = jnp.zeros_like(dbias_ref)
            if cf.sinks:
                dsink_ref[...] = jnp.zeros_like(dsink_ref)

        @pl.when(j == 0)
        def _():
            dk_acc[...] = jnp.zeros_like(dk_acc)
            dv_acc[...] = jnp.zeros_like(dv_acc)

        masks = _head_masks(BLOCK)
        cur = pl.ds(pl.multiple_of(j * BLOCK, BLOCK), BLOCK)
        prev = pl.ds(pl.multiple_of(jnp.maximum(j - 1, 0) * BLOCK, BLOCK), BLOCK)
        sinks4 = [_row_value(ref[...]) for ref in sink_refs] if cf.sinks else None
        ds_sum, dsink_sum = [None] * cf.nq, [None] * cf.nq
        for bi in range(bsz):
            k = jnp.concatenate([kp_ref[bi], kc_ref[bi]], axis=0)
            v = jnp.concatenate([vp_ref[bi], vc_ref[bi]], axis=0)
            dk_blocks, dv_blocks = [], []
            for qb in range(cf.nq):
                cols = slice(256 * qb, 256 * (qb + 1))
                kb, vb = _kv_operand(cf, k, qb), _kv_operand(cf, v, qb)
                q4 = _stack_heads(q_refs[qb][bi] * (HEAD_DIM ** -0.5), masks)
                lt = lt_ref[bi, :, cols]
                lt4 = jnp.concatenate([_row_value(jnp.where(mk, lt, NEG)) for mk in masks], axis=0)
                pa = jnp.exp(_nt(q4, kb) + bias_refs[qb][...] - lt4)
                dy = dy_ref[bi, :, cols]
                prod = dy * y_ref[bi, :, cols]
                e4 = jnp.concatenate([jnp.sum(jnp.where(mk, prod, 0.0), axis=-1, keepdims=True) for mk in masks],
                                     axis=0)
                dy4 = _stack_heads(dy.astype(BF16), masks)
                ds = pa * (_nt(dy4, vb) - e4)
                ds_sum[qb] = ds if ds_sum[qb] is None else ds_sum[qb] + ds
                if cf.sinks:
                    dsk = jnp.exp(sinks4[qb] - lt4) * e4
                    dsink_sum[qb] = dsk if dsink_sum[qb] is None else dsink_sum[qb] + dsk
                dsb = ds.astype(BF16)
                dq_ref[bi, :, cols] = (_unstack_heads(_nn(dsb, kb), masks) * (HEAD_DIM ** -0.5)).astype(BF16)
                dk_blocks.append(_tn(dsb, q4))
                dv_blocks.append(_tn(pa.astype(BF16), dy4))
            dk_new, dv_new = _kv_fold(cf, dk_blocks), _kv_fold(cf, dv_blocks)
            dk_acc[bi, cur, :] += dk_new[BLOCK:]
            dv_acc[bi, cur, :] += dv_new[BLOCK:]
            dk_acc[bi, prev, :] += dk_new[:BLOCK]
            dv_acc[bi, prev, :] += dv_new[:BLOCK]
        for qb in range(cf.nq):
            rows = slice(HEADS_PER_BLOCK * BLOCK * qb, HEADS_PER_BLOCK * BLOCK * (qb + 1))
            dbias_ref[rows, :] += ds_sum[qb]
            if cf.sinks:
                dsink_ref[rows, :] -= dsink_sum[qb]

        @pl.when(j == nb - 1)
        def _():
            dk_ref[...] = dk_acc[...].astype(BF16)
            dv_ref[...] = dv_acc[...].astype(BF16)

    tok = pl.BlockSpec((bsz, BLOCK, cf.wq), lambda r, j: (0, j, r))
    in_specs = _att_in_specs(cf, bsz) + [tok] * 3
    args = [zf] * (cf.nq + 4) + [bias] * cf.nq + ([sinks] * cf.nq if cf.sinks else []) + [lse_tot, dy, y]
    seq = pl.BlockSpec((bsz, l, cf.wkv), lambda r, j: (0, 0, r))
    out_specs = [tok, seq, seq, pl.BlockSpec((cf.nh * BLOCK, 2 * BLOCK), lambda r, j: (0, 0))]
    out_shape = [jax.ShapeDtypeStruct((bsz, l, cf.d * cf.wq), BF16),
                 jax.ShapeDtypeStruct((bsz, l, cf.d * cf.wkv), BF16),
                 jax.ShapeDtypeStruct((bsz, l, cf.d * cf.wkv), BF16),
                 jax.ShapeDtypeStruct((cf.nh * BLOCK, 2 * BLOCK), F32)]
    if cf.sinks:
        out_specs.append(pl.BlockSpec((cf.nh * BLOCK, 128), lambda r, j: (0, 0)))
        out_shape.append(jax.ShapeDtypeStruct((cf.nh * BLOCK, 128), F32))
    return _call(
        body, name=name, grid=(cf.d, nb), in_specs=in_specs, out_specs=out_specs, out_shape=out_shape,
        scratch=[pltpu.VMEM((bsz, l, cf.wkv), F32), pltpu.VMEM((bsz, l, cf.wkv), F32)],
        sem=("arbitrary", "arbitrary"), args=args, hosted=hosted)


def _merge_fwd(o_a, lse_a, o_b, zg, h, wa_t, wb_t, wout, name):
    t = h.shape[0]
    tm = 512

    def body(o1, o2, o3, l1, l2, l3, ob_ref, zg_ref, h_ref, wa_ref, wb_ref, wo_ref, h2_ref, ya_ref, lt_ref):
        m = jnp.maximum(jnp.maximum(l1[...], l2[...]), l3[...])
        e1, e2, e3 = jnp.exp(l1[...] - m), jnp.exp(l2[...] - m), jnp.exp(l3[...] - m)
        se = e1 + e2 + e3
        ya = (e1 / se) * o1[...] + (e2 / se) * o2[...] + (e3 / se) * o3[...]
        ya_ref[...] = ya
        lt_ref[...] = m + jnp.log(se)
        pa = _nt(ya.astype(BF16), wa_ref[...])
        pb = _nt(ob_ref[...].astype(BF16), wb_ref[...])
        merged = jax.nn.sigmoid(zg_ref[:, :D_MODEL]) * pa + jax.nn.sigmoid(zg_ref[:, D_MODEL:]) * pb
        h2_ref[...] = h_ref[...] + _nn(merged.astype(BF16), wo_ref[...])

    row = lambda w: pl.BlockSpec((tm, w), lambda i: (i, 0))
    full = _resident
    return pl.pallas_call(
        body, name=name, grid=(t // tm,),
        in_specs=[row(256)] * 6 + [row(512), row(GATE_W), row(D_MODEL), full(wa_t), full(wb_t), full(wout)],
        out_specs=[row(D_MODEL), row(256), row(256)],
        out_shape=[jax.ShapeDtypeStruct((t, D_MODEL), F32), jax.ShapeDtypeStruct((t, 256), F32),
                   jax.ShapeDtypeStruct((t, 256), F32)],
        compiler_params=_params("parallel"),
    )(*o_a, *lse_a, o_b, zg, h, wa_t, wb_t, wout)


def _merge_bwd(dh, ya, o_b, zg, wa_t, wb_t, wout, name, hosted=()):
    t = dh.shape[0]
    tm = 256

    def body(dh_ref, ya_ref, ob_ref, zg_ref, wa_ref, wb_ref, wo_ref,
             mg_ref, dpa_ref, dpb_ref, dzg_ref, dya_ref, dyb_ref):
        dm = _nt(dh_ref[...].astype(BF16), wo_ref[...])
        pa = _nt(ya_ref[...].astype(BF16), wa_ref[...])
        pb = _nt(ob_ref[...].astype(BF16), wb_ref[...])
        sa = jax.nn.sigmoid(zg_ref[:, :D_MODEL])
        sb = jax.nn.sigmoid(zg_ref[:, D_MODEL:])
        mg_ref[...] = (sa * pa + sb * pb).astype(BF16)
        dpa = (dm * sa).astype(BF16)
        dpb = (dm * sb).astype(BF16)
        dpa_ref[...] = dpa
        dpb_ref[...] = dpb
        dzg_ref[:, :D_MODEL] = (dm * pa * (sa * (1.0 - sa))).astype(BF16)
        dzg_ref[:, D_MODEL:] = (dm * pb * (sb * (1.0 - sb))).astype(BF16)
        dya_ref[...] = _nn(dpa, wa_ref[...])
        dyb_ref[...] = _nn(dpb, wb_ref[...])

    row = lambda w: pl.BlockSpec((tm, w), lambda i: (i, 0))
    full = _resident
    sds = jax.ShapeDtypeStruct
    return _call(
        body, name=name, grid=(t // tm,),
        in_specs=[row(D_MODEL), row(256), row(512), row(GATE_W), full(wa_t), full(wb_t), full(wout)],
        out_specs=[row(D_MODEL), row(D_MODEL), row(D_MODEL), row(GATE_W), row(256), row(512)],
        out_shape=[sds((t, D_MODEL), BF16), sds((t, D_MODEL), BF16), sds((t, D_MODEL), BF16),
                   sds((t, GATE_W), BF16), sds((t, 256), F32), sds((t, 512), F32)],
        sem=("parallel",), args=(dh, ya, o_b, zg, wa_t, wb_t, wout), hosted=hosted)


def _loss_head(h, g, target, name):
    t = h.shape[0]
    tm = 1024

    def body(h_ref, g_ref, t_ref, dh_ref, loss_ref, dg_ref):
        @pl.when(pl.program_id(0) == 0)
        def _():
            loss_ref[...] = jnp.zeros_like(loss_ref)
            dg_ref[...] = jnp.zeros_like(dg_ref)

        xhat, r = _rms(h_ref[...], g_ref[...])
        err = xhat * g_ref[...] - t_ref[...]
        loss_ref[...] += 0.5 * jnp.sum(jnp.mean(err * err, axis=-1, keepdims=True), axis=0, keepdims=True)
        dx, dg = _rms_bwd(err * (1.0 / D_MODEL), xhat, r, g_ref[...])
        dh_ref[...] = dx
        dg_ref[...] += dg

    row = pl.BlockSpec((tm, D_MODEL), lambda i: (i, 0))
    vec = pl.BlockSpec((1, D_MODEL), lambda i: (0, 0))
    return pl.pallas_call(
        body, name=name, grid=(t // tm,),
        in_specs=[row, vec, row], out_specs=[row, pl.BlockSpec((1, 128), lambda i: (0, 0)), vec],
        out_shape=[jax.ShapeDtypeStruct((t, D_MODEL), F32), jax.ShapeDtypeStruct((1, 128), F32),
                   jax.ShapeDtypeStruct((1, D_MODEL), F32)],
        compiler_params=_params("arbitrary"),
    )(h, g, target)


def _pair_sum(grad, got, name):
    _, _, r, cdim = grad.shape
    core = lax.axis_index("c").astype(jnp.int32).reshape(1)

    def body(core_ref, g_ref, s_ref, o_ref):
        o_ref[...] = (g_ref[...] + s_ref[...]).astype(BF16)

    return pl.pallas_call(
        body, name=name,
        grid_spec=pltpu.PrefetchScalarGridSpec(
            num_scalar_prefetch=1, grid=(N_CHIP,),
            in_specs=[pl.BlockSpec((None, None, r, cdim), lambda q, core_ref: (q, core_ref[0], 0, 0)),
                      pl.BlockSpec((None, None, r, cdim), lambda q, core_ref: (q, 0, 0, 0))],
            out_specs=pl.BlockSpec((None, r, cdim), lambda q, core_ref: (q, 0, 0))),
        out_shape=jax.ShapeDtypeStruct((N_CHIP, r, cdim), BF16),
        compiler_params=_params("parallel"),
    )(core, grad, got)


def _adamw_math(w, g, m, v):
    m = ADAM_B1 * m + (1.0 - ADAM_B1) * g
    v = ADAM_B2 * v + (1.0 - ADAM_B2) * jnp.square(g)
    m_hat = m / (1.0 - ADAM_B1 ** ADAM_STEP)
    v_hat = v / (1.0 - ADAM_B2 ** ADAM_STEP)
    delta = -ADAM_LR * (m_hat / (jnp.sqrt(v_hat) + ADAM_EPS) + ADAM_WD * w)
    return delta, m, v


def _update(w, parts, m, v, transposed, name):
    r, c = parts.shape[1:]

    def body(w_ref, p_ref, m_ref, v_ref, g_ref, d_ref, m2_ref, v2_ref):
        def total(rows):
            return ((p_ref[0, rows].astype(F32) + p_ref[1, rows].astype(F32)) + p_ref[2, rows].astype(F32)) \
                + p_ref[3, rows].astype(F32)

        def update(at, g):
            g_ref[at] = g
            d_ref[at], m2_ref[at], v2_ref[at] = _adamw_math(w_ref[at], g, m_ref[at], v_ref[at])

        if not transposed:
            update((slice(None), slice(None)), total(slice(None)))
            return
        for r0 in range(0, r, 128):
            n = min(128, r - r0)
            gt = total(slice(r0, r0 + n))
            if n < 128:
                gt = jnp.concatenate([gt, jnp.zeros((128 - n, c), F32)], axis=0)
            update((slice(None), slice(r0, r0 + n)), gt.T[:, :n])

    sds = jax.ShapeDtypeStruct(w.shape, F32)
    return pl.pallas_call(body, name=name, out_shape=[sds] * 4,
                          compiler_params=pltpu.CompilerParams(vmem_limit_bytes=VMEM_LIMIT))(w, parts, m, v)


SMALL_ROWS = 80


def _small_update(g, w, m, v, name):
    def body(g_ref, w_ref, m_ref, v_ref, gs_ref, d_ref, m2_ref, v2_ref, got_ref, send_sems, recv_sems):
        x, y, c = _place()
        me = 4 * x + 2 * y + c
        got_ref[me] = g_ref[...]
        copies = []
        for k in range(1, N_DEV):
            peer = (x ^ (k >> 2), y ^ ((k >> 1) & 1), c ^ (k & 1))
            cp = pltpu.make_async_remote_copy(
                src_ref=g_ref, dst_ref=got_ref.at[me], send_sem=send_sems.at[k - 1], recv_sem=recv_sems.at[k - 1],
                device_id=peer, device_id_type=MESH)
            cp.start()
            copies.append(cp)
        for cp in copies:
            cp.wait()
        total = got_ref[0]
        for k in range(1, N_DEV):
            total = total + got_ref[k]
        gs_ref[...] = total
        d_ref[...], m2_ref[...], v2_ref[...] = _adamw_math(w_ref[...], total, m_ref[...], v_ref[...])

    sds = jax.ShapeDtypeStruct((SMALL_ROWS, 128), F32)
    vm = pl.BlockSpec(memory_space=pltpu.VMEM)
    return pl.pallas_call(
        body, name=name, in_specs=[vm] * 4, out_specs=[vm] * 4, out_shape=[sds] * 4,
        scratch_shapes=[pltpu.VMEM((N_DEV, SMALL_ROWS, 128), F32), pltpu.SemaphoreType.DMA((N_DEV - 1,)),
                        pltpu.SemaphoreType.DMA((N_DEV - 1,))],
    )(g, w, m, v)


def _pack_small(gains, b_in, rel_bias, sinks, last):
    rows = [a.reshape(8, 128) for a in gains] + [b_in.reshape(40, 128), rel_bias.reshape(5, 128),
                                                 jnp.pad(sinks.reshape(1, 8), ((0, 0), (0, 120))), last]
    rows.append(jnp.zeros((SMALL_ROWS - 79, 128), F32))
    return jnp.concatenate(rows, axis=0)


def _unpack_small(p, like):
    out = [p[8 * i:8 * i + 8].reshape(like[i].shape) for i in range(4)]
    out.append(p[32:72].reshape(like[4].shape))
    out.append(p[72:77].reshape(like[5].shape))
    out.append(p[77, :8].reshape(like[6].shape))
    return out


def kernel(x, ffn1_norm, ffn1_w_gate, ffn1_w_up, ffn1_w_down, mix_norm, w_in, b_in, w_branch_a, w_branch_b, w_out, sinks, rel_bias, ffn2_norm, ffn2_w_gate, ffn2_w_up, ffn2_w_down, final_norm, loss_target, m_ffn1_norm, m_ffn1_w_gate, m_ffn1_w_up, m_ffn1_w_down, m_mix_norm, m_w_in, m_b_in, m_w_branch_a, m_w_branch_b, m_w_out, m_sinks, m_rel_bias, m_ffn2_norm, m_ffn2_w_gate, m_ffn2_w_up, m_ffn2_w_down, m_final_norm, v_ffn1_norm, v_ffn1_w_gate, v_ffn1_w_up, v_ffn1_w_down, v_mix_norm, v_w_in, v_b_in, v_w_branch_a, v_w_branch_b, v_w_out, v_sinks, v_rel_bias, v_ffn2_norm, v_ffn2_w_gate, v_ffn2_w_up, v_ffn2_w_down, v_final_norm):
    bsz, seq, _ = x.shape
    t = bsz * seq
    xt = x.reshape(t, D_MODEL)
    target = loss_target.reshape(t, D_MODEL)

    big = [("ffn1_w_gate", ffn1_w_gate, m_ffn1_w_gate, v_ffn1_w_gate, True),
           ("ffn1_w_up", ffn1_w_up, m_ffn1_w_up, v_ffn1_w_up, True),
           ("ffn1_w_down", ffn1_w_down, m_ffn1_w_down, v_ffn1_w_down, False),
           ("w_in", w_in, m_w_in, v_w_in, True),
           ("w_branch_a", w_branch_a, m_w_branch_a, v_w_branch_a, True),
           ("w_branch_b", w_branch_b, m_w_branch_b, v_w_branch_b, True),
           ("w_out", w_out, m_w_out, v_w_out, False),
           ("ffn2_w_gate", ffn2_w_gate, m_ffn2_w_gate, v_ffn2_w_gate, True),
           ("ffn2_w_up", ffn2_w_up, m_ffn2_w_up, v_ffn2_w_up, True),
           ("ffn2_w_down", ffn2_w_down, m_ffn2_w_down, v_ffn2_w_down, False)]
    shard = {nm: (w[0].T if tr else w[0]).astype(BF16) for nm, w, _, _, tr in big}
    full = {}

    def gather(names):
        return _Gather([shard[nm] for nm in names])

    def keep(names, got):
        for nm, gw in zip(names, got):
            full[nm] = gw.reshape(-1, gw.shape[-1])

    ffn1_names = ["ffn1_w_gate", "ffn1_w_up", "ffn1_w_down"]
    mix_names = ["w_in", "w_branch_a", "w_branch_b", "w_out"]
    ffn2_names = ["ffn2_w_gate", "ffn2_w_up", "ffn2_w_down"]
    g1, gm, g2, gf = ffn1_norm, mix_norm, ffn2_norm, final_norm.reshape(1, D_MODEL)

    keep(ffn1_names, _exchange([gather(ffn1_names)], "gather_ffn1")[0])
    (h1, n1, a1, b1, hff1), (got,) = _ffn_fwd(xt, g1, full["ffn1_w_gate"], full["ffn1_w_up"], full["ffn1_w_down"], "ffn1_fwd",
                                hosted=[gather(mix_names)])
    keep(mix_names, got)
    (u, zq, zg), (got,) = _inproj_fwd(h1, gm, full["w_in"], b_in, "inproj_fwd", hosted=[gather(ffn2_names[:2])])
    keep(ffn2_names[:2], got)
    buckets = _bucket_tiles()
    bias = _bias_build(rel_bias, buckets)
    sink_rows = jnp.broadcast_to(sinks.reshape(8, 1, 1), (8, BLOCK, 128)).reshape(8 * BLOCK, 128)
    cfs = [_Att(i) for i in range(4)]
    zfold = []
    for i, cf in enumerate(cfs):
        if cf.d == 1:
            zfold.append(zq.reshape(bsz, seq, QKV_W))
        else:
            own = jnp.concatenate([zq[:, 768 * part + 256 * i:768 * part + 256 * (i + 1)] for part in range(3)], axis=1)
            zfold.append(own.reshape(bsz, seq // cf.d, cf.d * cf.row_w))
    att = [None] * 4
    for i in (3, 0, 1, 2):
        cf = cfs[i]
        (o, lse), got = _attn_fwd(cf, zfold[i], bias, sink_rows, f"attn{i}_fwd",
                                  hosted=[gather(ffn2_names[2:])] if i == 3 else ())
        if i == 3:
            keep(ffn2_names[2:], got[0])
        att[i] = (o.reshape(t, cf.wq), lse.reshape(t, cf.wq))
    o_b, lse_b = att[3]
    h2, ya, lse_tot = _merge_fwd([a[0] for a in att[:3]], [a[1] for a in att[:3]], o_b, zg, h1,
                                 full["w_branch_a"], full["w_branch_b"], full["w_out"], "merge_fwd")
    (h3, n2, a2, b2, hff2), _ = _ffn_fwd(h2, g2, full["ffn2_w_gate"], full["ffn2_w_up"], full["ffn2_w_down"], "ffn2_fwd")
    dh3, loss_part, dgf = _loss_head(h3, gf, target, "loss_head")

    grads, pair, from_chips = {}, {}, {}

    def by_owner(nm):
        return grads[nm].reshape(N_CHIP, 2, -1, grads[nm].shape[-1])

    def to_core(names):
        return _CoreExchange([by_owner(nm) for nm in names])

    def pair_up(names, got):
        for nm, sib in zip(names, got):
            pair[nm] = _pair_sum(by_owner(nm), sib, f"pair_sum_{nm}")

    def to_chips(names):
        return _ChipExchange([pair[nm] for nm in names])

    def landed(names, got):
        for nm, parts in zip(names, got):
            from_chips[nm] = parts

    dh2, dg2, da, db = _ffn_bwd(h2, a2, b2, g2, dh3, full["ffn2_w_gate"], full["ffn2_w_up"],
                                          full["ffn2_w_down"], "ffn2_bwd")
    grads["ffn2_w_gate"], _ = _tn_matmul(da, n2, 1408, "ffn2_dgate")
    grads["ffn2_w_up"], _ = _tn_matmul(db, n2, 1408, "ffn2_dup")
    grads["ffn2_w_down"], _ = _tn_matmul(hff2, dh3, 1408, "ffn2_ddown", scale=0.5)
    (merged, dpa, dpb, dzg, dya, dyb), (got,) = _merge_bwd(
        dh2, ya, o_b, zg, full["w_branch_a"], full["w_branch_b"], full["w_out"], "merge_bwd",
        hosted=[to_core(ffn2_names)])
    pair_up(ffn2_names, got)
    dq, dk, dv, dbias, dsink = [None] * 4, [None] * 4, [None] * 4, [None] * 4, None
    for i in (3, 0, 1, 2):
        cf = cfs[i]
        lt_i, dy_i, y_i = (lse_tot, dya, ya) if i < 3 else (lse_b, dyb, o_b)
        shp = (bsz, seq // cf.d, cf.d * cf.wq)
        hosted = {3: lambda: [to_chips(ffn2_names[:2])], 0: lambda: [to_chips(ffn2_names[2:])]}.get(i, list)()
        res, got = _attn_bwd(cf, zfold[i], bias, sink_rows, lt_i.reshape(shp), dy_i.reshape(shp), y_i.reshape(shp),
                             f"attn{i}_bwd", hosted=hosted)
        if i == 3:
            landed(ffn2_names[:2], got[0])
        elif i == 0:
            landed(ffn2_names[2:], got[0])
        dq[i] = res[0].reshape(t, cf.wq)
        dk[i] = res[1].reshape(t, cf.wkv)
        dv[i] = res[2].reshape(t, cf.wkv)
        dbias[i] = res[3]
        if cf.sinks:
            dsink = res[4]
    pieces = dq[:3] + dk[:3] + dv[:3] + [dq[3], dk[3], dv[3], dzg]
    dh1, dz, db_in, dgm = _inproj_bwd(pieces, h1, gm, dh2, full["w_in"], "inproj_bwd")
    dx, dg1, da, db = _ffn_bwd(xt, a1, b1, g1, dh1, full["ffn1_w_gate"], full["ffn1_w_up"],
                                         full["ffn1_w_down"], "ffn1_bwd")
    grads["w_in"], _ = _tn_matmul(dz, u, 1280, "dw_in")
    grads["ffn1_w_down"], (got,) = _tn_matmul(hff1, dh1, 1408, "ffn1_ddown", hosted=[to_core(["w_in"])], scale=0.5)
    pair_up(["w_in"], got)
    grads["ffn1_w_gate"], got = _tn_matmul(da, n1, 1408, "ffn1_dgate",
                                           hosted=[to_chips(["w_in"]), to_core(["ffn1_w_down"])])
    landed(["w_in"], got[0])
    pair_up(["ffn1_w_down"], got[1])
    grads["ffn1_w_up"], got = _tn_matmul(db, n1, 1408, "ffn1_dup",
                                         hosted=[to_chips(["ffn1_w_down"]), to_core(["ffn1_w_gate"])])
    landed(["ffn1_w_down"], got[0])
    pair_up(["ffn1_w_gate"], got[1])
    grads["w_out"], got = _tn_matmul(merged, dh2, 1024, "dw_out",
                                     hosted=[to_chips(["ffn1_w_gate"]), to_core(["ffn1_w_up"])])
    landed(["ffn1_w_gate"], got[0])
    pair_up(["ffn1_w_up"], got[1])
    grads["w_branch_b"], got = _tn_matmul(dpb, o_b, 1024, "dw_branch_b",
                                          hosted=[to_chips(["ffn1_w_up"]), to_core(["w_out"])])
    landed(["ffn1_w_up"], got[0])
    pair_up(["w_out"], got[1])
    grads["w_branch_a"], got = _tn_matmul(dpa, ya, 1024, "dw_branch_a",
                                          hosted=[to_chips(["w_out"]), to_core(["w_branch_b"])])
    landed(["w_out"], got[0])
    pair_up(["w_branch_b"], got[1])
    got = _exchange([to_chips(["w_branch_b"]), to_core(["w_branch_a"])], "reduce_scatter_tail1")
    landed(["w_branch_b"], got[0])
    pair_up(["w_branch_a"], got[1])
    landed(["w_branch_a"], _exchange([to_chips(["w_branch_a"])], "reduce_scatter_tail2")[0])
    dtable, dsinks = _bias_reduce(jnp.concatenate(dbias, axis=0), buckets, dsink)

    out_g, out_d, out_m, out_v = {}, {}, {}, {}
    for nm, w, m, v, tr in big:
        g, d2, m2, v2 = _update(w[0], from_chips[nm], m[0], v[0], tr, f"update_{nm}")
        out_g[nm], out_d[nm], out_m[nm], out_v[nm] = g[None], d2[None], m2[None], v2[None]

    small = [("ffn1_norm", ffn1_norm, m_ffn1_norm, v_ffn1_norm), ("mix_norm", mix_norm, m_mix_norm, v_mix_norm),
             ("ffn2_norm", ffn2_norm, m_ffn2_norm, v_ffn2_norm), ("final_norm", final_norm, m_final_norm, v_final_norm),
             ("b_in", b_in, m_b_in, v_b_in), ("rel_bias", rel_bias, m_rel_bias, v_rel_bias),
             ("sinks", sinks, m_sinks, v_sinks)]
    zero_row = jnp.zeros((1, 128), F32)
    pack = lambda arrs, last: _pack_small(arrs[:4], arrs[4], arrs[5], arrs[6], last)
    g_small = pack([dg1, dgm, dg2, dgf, db_in, dtable[:, :TOTAL_HEADS], dsinks[:, 0]], loss_part)
    packed = [pack([s[k] for s in small], zero_row) for k in (1, 2, 3)]
    gs, ds, ms, vs = _small_update(g_small, *packed, "small_update")
    like = [s[1] for s in small]
    for nm_s, g_, d_, m_, v_ in zip([s[0] for s in small], _unpack_small(gs, like), _unpack_small(ds, like),
                                    _unpack_small(ms, like), _unpack_small(vs, like)):
        out_g[nm_s], out_d[nm_s], out_m[nm_s], out_v[nm_s] = g_, d_, m_, v_
    loss = gs[78, 0]

    order = ["ffn1_norm", "ffn1_w_gate", "ffn1_w_up", "ffn1_w_down", "mix_norm", "w_in", "b_in", "w_branch_a",
             "w_branch_b", "w_out", "sinks", "rel_bias", "ffn2_norm", "ffn2_w_gate", "ffn2_w_up", "ffn2_w_down",
             "final_norm"]
    return (loss, dx.reshape(x.shape), *[out_g[k] for k in order], *[out_d[k] for k in order],
            *[out_m[k] for k in order], *[out_v[k] for k in order])
```

```python
import functools
import math

import numpy as np
import jax
import jax.numpy as jnp
from jax import lax
from jax.experimental import pallas as pl
from jax.experimental.pallas import tpu as pltpu

D_MODEL = 1024
D_FF = 2816
FF_CHUNK = 256
HEAD_DIM = 64
BLOCK = 128
N_BUCKETS = 32
MAX_DISTANCE = 2048
A_HEADS = 12
TOTAL_HEADS = 20
DIL_GROUPS = ((128, 1), (512, 4), (2048, 16))
B_WINDOW = 128
QKV_W = 3072
GATE_W = 2048
D_IN = QKV_W + GATE_W
EPS = 1e-6
NEG = -1e30
N_DEV = 8
N_CHIP = 4
ADAM_LR, ADAM_B1, ADAM_B2, ADAM_EPS, ADAM_WD, ADAM_STEP = 0.001, 0.9, 0.999, 1e-08, 0.01, 10
VMEM_LIMIT = 56 * 1024 * 1024
MESH = pl.DeviceIdType.MESH
BF16 = jnp.bfloat16
F32 = jnp.float32
ANY = pl.BlockSpec(memory_space=pl.ANY)


def _params(*sem):
    return pltpu.CompilerParams(dimension_semantics=sem, vmem_limit_bytes=VMEM_LIMIT)


def _resident(a):
    return pl.BlockSpec(a.shape, lambda i: (0, 0), pipeline_mode=pl.Buffered(1))


def _place():
    return lax.axis_index("x"), lax.axis_index("y"), lax.axis_index("c")


class _Gather:
    def __init__(self, shards):
        self.ins = list(shards)
        n = self.n = len(shards)
        self.out_shape = [jax.ShapeDtypeStruct((N_DEV,) + s.shape, s.dtype) for s in shards]
        self.scratch = [pltpu.SemaphoreType.DMA((7 * n,)), pltpu.SemaphoreType.DMA((7 * n,)),
                        pltpu.SemaphoreType.DMA((n,))]

    def _copies(self, ins, outs, sems):
        send_sems, recv_sems, local_sems = sems
        x, y, c = _place()
        me, sibling = (x, y, c), (x, y, 1 - c)
        chips = [(1 - x, y), (x, 1 - y), (1 - x, 1 - y)]

        def copy(i, k, block, to, src=None):
            dst = outs[i].at[4 * block[0] + 2 * block[1] + block[2]]
            return pltpu.make_async_remote_copy(
                src_ref=dst if src is None else src, dst_ref=dst, send_sem=send_sems.at[7 * i + k],
                recv_sem=recv_sems.at[7 * i + k], device_id=to, device_id_type=MESH)

        n = self.n
        mine = [pltpu.make_async_copy(ins[i], outs[i].at[4 * x + 2 * y + c], local_sems.at[i]) for i in range(n)]
        first = [copy(i, 0, me, sibling, src=ins[i]) for i in range(n)]
        first += [copy(i, 1 + j, me, (*chip, c), src=ins[i]) for i in range(n) for j, chip in enumerate(chips)]
        landed = [copy(i, 1 + j, (*chip, c), me) for j, chip in enumerate(chips) for i in range(n)]
        passed = [copy(i, 4 + j, (*chip, c), sibling) for j, chip in enumerate(chips) for i in range(n)]
        from_sibling = [copy(i, 0, sibling, me) for i in range(n)]
        from_sibling += [copy(i, 4 + j, (*chip, 1 - c), me) for i in range(n) for j, chip in enumerate(chips)]
        return mine, first, landed, passed, from_sibling

    def start(self, ins, outs, sems):
        mine, first, _, _, _ = self._copies(ins, outs, sems)
        for cp in mine + first:
            cp.start()

    def mid(self, ins, outs, sems):
        _, _, landed, passed, _ = self._copies(ins, outs, sems)
        for got, fwd in zip(landed, passed):
            got.wait_recv()
            fwd.start()

    def end(self, ins, outs, sems):
        mine, first, _, passed, from_sibling = self._copies(ins, outs, sems)
        for cp in from_sibling:
            cp.wait_recv()
        for cp in first + passed:
            cp.wait_send()
        for cp in mine:
            cp.wait()


class _CoreExchange:
    def __init__(self, grads):
        self.ins = list(grads)
        n = self.n = len(grads)
        self.out_shape = [jax.ShapeDtypeStruct((N_CHIP, 1) + g.shape[2:], g.dtype) for g in grads]
        self.scratch = [pltpu.SemaphoreType.DMA((n,)), pltpu.SemaphoreType.DMA((n,))]

    def _copies(self, ins, outs, sems):
        x, y, c = _place()
        return [pltpu.make_async_remote_copy(
            src_ref=ins[i].at[:, pl.ds(1 - c, 1)], dst_ref=outs[i], send_sem=sems[0].at[i],
            recv_sem=sems[1].at[i], device_id=(x, y, 1 - c), device_id_type=MESH) for i in range(self.n)]

    def start(self, ins, outs, sems):
        for cp in self._copies(ins, outs, sems):
            cp.start()

    mid = None

    def end(self, ins, outs, sems):
        for cp in self._copies(ins, outs, sems):
            cp.wait()


class _ChipExchange:
    def __init__(self, parts):
        self.ins = list(parts)
        n = self.n = len(parts)
        self.out_shape = [jax.ShapeDtypeStruct(p.shape, p.dtype) for p in parts]
        self.scratch = [pltpu.SemaphoreType.DMA((3 * n,)), pltpu.SemaphoreType.DMA((3 * n,)),
                        pltpu.SemaphoreType.DMA((n,))]

    def _copies(self, ins, outs, sems):
        send_sems, recv_sems, local_sems = sems
        x, y, c = _place()
        my_chip = 2 * x + y
        copies = []
        for i in range(self.n):
            copies.append(pltpu.make_async_copy(ins[i].at[my_chip], outs[i].at[my_chip], local_sems.at[i]))
            for k, (qx, qy) in enumerate([(1 - x, y), (x, 1 - y), (1 - x, 1 - y)]):
                copies.append(pltpu.make_async_remote_copy(
                    src_ref=ins[i].at[2 * qx + qy], dst_ref=outs[i].at[my_chip], send_sem=send_sems.at[3 * i + k],
                    recv_sem=recv_sems.at[3 * i + k], device_id=(qx, qy, c), device_id_type=MESH))
        return copies

    def start(self, ins, outs, sems):
        for cp in self._copies(ins, outs, sems):
            cp.start()

    mid = None

    def end(self, ins, outs, sems):
        for cp in self._copies(ins, outs, sems):
            cp.wait()


def _call(body, *, name, grid, in_specs, out_specs, out_shape, args, scratch=(), sem=None, hosted=()):
    n_in, n_out, n_scr = len(in_specs), len(out_specs), len(scratch)
    x_in = [len(p.ins) for p in hosted]
    x_scr = [len(p.scratch) for p in hosted]
    steps = int(np.prod(grid))

    def wrapped(*refs):
        refs = list(refs)
        ins, refs = refs[:n_in], refs[n_in:]
        x_ins = [[refs.pop(0) for _ in range(k)] for k in x_in]
        outs, refs = refs[:n_out], refs[n_out:]
        x_outs = [[refs.pop(0) for _ in range(k)] for k in x_in]
        scr, refs = refs[:n_scr], refs[n_scr:]
        x_sems = [[refs.pop(0) for _ in range(k)] for k in x_scr]
        step = 0
        for d in range(len(grid)):
            step = step * grid[d] + pl.program_id(d)

        def phase(which, at):
            fns = [(getattr(p, which), a) for p, a in zip(hosted, zip(x_ins, x_outs, x_sems)) if getattr(p, which)]
            if fns:
                @pl.when(step == at)
                def _():
                    for fn, a in fns:
                        fn(*a)

        phase("start", 0)
        phase("mid", (3 * steps) // 4)
        body(*ins, *outs, *scr)
        phase("end", steps - 1)

    results = pl.pallas_call(
        wrapped, name=name, grid=grid,
        in_specs=list(in_specs) + [ANY] * sum(x_in), out_specs=list(out_specs) + [ANY] * sum(x_in),
        out_shape=list(out_shape) + [s for p in hosted for s in p.out_shape],
        scratch_shapes=list(scratch) + [s for p in hosted for s in p.scratch],
        compiler_params=_params(*(("arbitrary",) * len(grid) if hosted else sem)),
    )(*args, *[a for p in hosted for a in p.ins])
    own, rest = list(results[:n_out]), list(results[n_out:])
    return own, [[rest.pop(0) for _ in range(k)] for k in x_in]


def _exchange(programs, name):
    x_in = [len(p.ins) for p in programs]
    x_scr = [len(p.scratch) for p in programs]

    def body(*refs):
        refs = list(refs)
        x_ins = [[refs.pop(0) for _ in range(k)] for k in x_in]
        x_outs = [[refs.pop(0) for _ in range(k)] for k in x_in]
        x_sems = [[refs.pop(0) for _ in range(k)] for k in x_scr]
        for which in ("start", "mid", "end"):
            for p, a in zip(programs, zip(x_ins, x_outs, x_sems)):
                if getattr(p, which):
                    getattr(p, which)(*a)

    results = list(pl.pallas_call(
        body, name=name, in_specs=[ANY] * sum(x_in), out_specs=[ANY] * sum(x_in),
        out_shape=[s for p in programs for s in p.out_shape],
        scratch_shapes=[s for p in programs for s in p.scratch],
    )(*[a for p in programs for a in p.ins]))
    return [[results.pop(0) for _ in range(k)] for k in x_in]


def _nt(a, b):
    return lax.dot_general(a, b, (((1,), (1,)), ((), ())), preferred_element_type=F32)


def _nn(a, b):
    return lax.dot_general(a, b, (((1,), (0,)), ((), ())), preferred_element_type=F32)


def _tn(a, b):
    return lax.dot_general(a, b, (((0,), (0,)), ((), ())), preferred_element_type=F32)


def _rms(x, g):
    r = lax.rsqrt(jnp.mean(x * x, axis=-1, keepdims=True) + EPS)
    return x * r, r


def _rms_bwd(dn, xhat, r, g):
    dg = jnp.sum(dn * xhat, axis=0, keepdims=True)
    dxh = dn * g
    dx = r * (dxh - xhat * jnp.mean(dxh * xhat, axis=-1, keepdims=True))
    return dx, dg


def _ffn_fwd(x, g, wg_t, wu_t, wd, name, hosted=()):
    t = x.shape[0]
    tm = 256

    def body(x_ref, g_ref, wg_ref, wu_ref, wd_ref, h_ref, n_ref, a_ref, b_ref, hff_ref):
        xhat, _ = _rms(x_ref[...], g_ref[...])
        n = (xhat * g_ref[...]).astype(BF16)
        n_ref[...] = n
        for c in range(0, D_FF, FF_CHUNK):
            cols = slice(c, c + FF_CHUNK)
            a = _nt(n, wg_ref[cols, :])
            b = _nt(n, wu_ref[cols, :])
            a_ref[:, cols] = a.astype(BF16)
            b_ref[:, cols] = b.astype(BF16)
            hff_ref[:, cols] = (a * jax.nn.sigmoid(a) * b).astype(BF16)
        h_ref[...] = x_ref[...] + 0.5 * _nn(hff_ref[...], wd_ref[...])

    row = pl.BlockSpec((tm, D_MODEL), lambda i: (i, 0))
    hid = pl.BlockSpec((tm, D_FF), lambda i: (i, 0))
    return _call(
        body, name=name, grid=(t // tm,),
        in_specs=[row, _resident(g), _resident(wg_t), _resident(wu_t), _resident(wd)],
        out_specs=[row, row, hid, hid, hid],
        out_shape=[jax.ShapeDtypeStruct((t, D_MODEL), F32), jax.ShapeDtypeStruct((t, D_MODEL), BF16),
                   jax.ShapeDtypeStruct((t, D_FF), BF16), jax.ShapeDtypeStruct((t, D_FF), BF16),
                   jax.ShapeDtypeStruct((t, D_FF), BF16)],
        sem=("parallel",), args=(x, g, wg_t, wu_t, wd), hosted=hosted)


def _ffn_bwd(x, a_pre, b_pre, g, dh, wg_t, wu_t, wd, name):
    t = x.shape[0]
    tm = 256

    def body(x_ref, a_ref, b_ref, g_ref, dh_ref, wg_ref, wu_ref, wd_ref,
             dx_ref, dg_ref, da_ref, db_ref):
        @pl.when(pl.program_id(0) == 0)
        def _():
            dg_ref[...] = jnp.zeros_like(dg_ref)

        dhh = (0.5 * dh_ref[...]).astype(BF16)
        for c in range(0, D_FF, FF_CHUNK):
            cols = slice(c, c + FF_CHUNK)
            a = a_ref[:, cols].astype(F32)
            b = b_ref[:, cols].astype(F32)
            s = jax.nn.sigmoid(a)
            silu = a * s
            dhff = _nt(dhh, wd_ref[cols, :])
            da_ref[:, cols] = (dhff * b * (s * (1.0 + a * (1.0 - s)))).astype(BF16)
            db_ref[:, cols] = (dhff * silu).astype(BF16)
        dn = _nn(da_ref[...], wg_ref[...]) + _nn(db_ref[...], wu_ref[...])
        xhat, r = _rms(x_ref[...], g_ref[...])
        dx, dg = _rms_bwd(dn, xhat, r, g_ref[...])
        dx_ref[...] = dh_ref[...] + dx
        dg_ref[...] += dg

    row = pl.BlockSpec((tm, D_MODEL), lambda i: (i, 0))
    hid = pl.BlockSpec((tm, D_FF), lambda i: (i, 0))
    return pl.pallas_call(
        body, name=name, grid=(t // tm,),
        in_specs=[row, hid, hid, _resident(g), row, _resident(wg_t), _resident(wu_t), _resident(wd)],
        out_specs=[row, pl.BlockSpec((1, D_MODEL), lambda i: (0, 0)), hid, hid],
        out_shape=[jax.ShapeDtypeStruct((t, D_MODEL), F32), jax.ShapeDtypeStruct((1, D_MODEL), F32),
                   jax.ShapeDtypeStruct((t, D_FF), BF16), jax.ShapeDtypeStruct((t, D_FF), BF16)],
        compiler_params=_params("arbitrary"),
    )(x, a_pre, b_pre, g, dh, wg_t, wu_t, wd)


def _tn_matmul(a, b, rc, name, hosted=(), scale=None):
    t, r = a.shape
    c = b.shape[1]
    tk = 1024

    def body(a_ref, b_ref, o_ref):
        @pl.when(pl.program_id(1) == 0)
        def _():
            o_ref[...] = jnp.zeros_like(o_ref)

        o_ref[...] += _tn(a_ref[...].astype(BF16), b_ref[...].astype(BF16))
        if scale is not None:
            @pl.when(pl.program_id(1) == t // tk - 1)
            def _():
                o_ref[...] *= scale

    (out,), got = _call(
        body, name=name, grid=(r // rc, t // tk),
        in_specs=[pl.BlockSpec((tk, rc), lambda i, k: (k, i)), pl.BlockSpec((tk, c), lambda i, k: (k, 0))],
        out_specs=[pl.BlockSpec((rc, c), lambda i, k: (i, 0))],
        out_shape=[jax.ShapeDtypeStruct((r, c), F32)],
        sem=("parallel", "arbitrary"), args=(a, b), hosted=hosted)
    return out, got


PIECE_W = (256,) * 9 + (512, 128, 128, GATE_W)


def _inproj_fwd(h, g, w_t, b_in, name, hosted=()):
    t = h.shape[0]
    tm, nc = 512, 512

    def body(h_ref, g_ref, w_ref, b_ref, u_ref, zq_ref, zg_ref):
        xhat, _ = _rms(h_ref[...], g_ref[...])
        u = (xhat * g_ref[...]).astype(BF16)
        u_ref[...] = u
        for c in range(D_IN // nc):
            z = _nt(u, w_ref[c * nc:(c + 1) * nc, :]) + b_ref[:, c * nc:(c + 1) * nc]
            if c < QKV_W // nc:
                zq_ref[:, c * nc:(c + 1) * nc] = z.astype(BF16)
            else:
                zg_ref[:, c * nc - QKV_W:(c + 1) * nc - QKV_W] = z.astype(BF16)

    row = lambda w: pl.BlockSpec((tm, w), lambda i: (i, 0))
    full = _resident
    return _call(
        body, name=name, grid=(t // tm,),
        in_specs=[row(D_MODEL), full(g), full(w_t), full(b_in)],
        out_specs=[row(D_MODEL), row(QKV_W), row(GATE_W)],
        out_shape=[jax.ShapeDtypeStruct((t, D_MODEL), BF16), jax.ShapeDtypeStruct((t, QKV_W), BF16),
                   jax.ShapeDtypeStruct((t, GATE_W), BF16)],
        sem=("parallel",), args=(h, g, w_t, b_in), hosted=hosted)


def _inproj_bwd(pieces, h, g, dh_res, w_t, name):
    t = h.shape[0]
    tm = 256
    npiece = len(PIECE_W)
    offs = np.concatenate([[0], np.cumsum(PIECE_W)]).tolist()

    def body(*refs):
        p_refs = refs[:npiece]
        h_ref, g_ref, dhr_ref, w_ref, dh_ref, dz_ref, db_ref, dg_ref = refs[npiece:]
        i = pl.program_id(0)

        @pl.when(i == 0)
        def _():
            db_ref[...] = jnp.zeros_like(db_ref)
            dg_ref[...] = jnp.zeros_like(dg_ref)

        du = jnp.zeros((tm, D_MODEL), F32)
        for k in range(npiece):
            o, w = offs[k], PIECE_W[k]
            for c0 in range(0, w, 512):
                cw = min(512, w - c0)
                pz = p_refs[k][:, c0:c0 + cw]
                dz_ref[:, o + c0:o + c0 + cw] = pz
                db_ref[:, o + c0:o + c0 + cw] += jnp.sum(pz.astype(F32), axis=0, keepdims=True)
                du = du + _nn(pz, w_ref[o + c0:o + c0 + cw, :])
        xhat, r = _rms(h_ref[...], g_ref[...])
        dx, dg = _rms_bwd(du, xhat, r, g_ref[...])
        dh_ref[...] = dhr_ref[...] + dx
        dg_ref[...] += dg

    row = lambda w: pl.BlockSpec((tm, w), lambda i: (i, 0))
    full = lambda shp: pl.BlockSpec(shp, lambda i: (0, 0))
    return pl.pallas_call(
        body, name=name, grid=(t // tm,),
        in_specs=[row(w) for w in PIECE_W] + [row(D_MODEL), _resident(g), row(D_MODEL), _resident(w_t)],
        out_specs=[row(D_MODEL), row(D_IN), full((1, D_IN)), full((1, D_MODEL))],
        out_shape=[jax.ShapeDtypeStruct((t, D_MODEL), F32), jax.ShapeDtypeStruct((t, D_IN), BF16),
                   jax.ShapeDtypeStruct((1, D_IN), F32), jax.ShapeDtypeStruct((1, D_MODEL), F32)],
        compiler_params=_params("arbitrary"),
    )(*pieces, h, g, dh_res, w_t)


def _t5_bucket(dist):
    max_exact = N_BUCKETS // 2
    n = jnp.maximum(dist, 0)
    nf = jnp.maximum(n, 1).astype(jnp.float32)
    large = max_exact + (jnp.log(nf / max_exact) / math.log(MAX_DISTANCE / max_exact)
                         * (N_BUCKETS - max_exact)).astype(jnp.int32)
    large = jnp.minimum(large, N_BUCKETS - 1)
    return jnp.where(n < max_exact, n, large)


ATT_CFG = ((1, 128, 0, 4), (4, 128, 4, 4), (16, 128, 8, 4), (1, B_WINDOW - 1, A_HEADS, 8))


def _bucket_tiles():
    qi = jnp.arange(BLOCK)[:, None]
    ki = jnp.arange(2 * BLOCK)[None, :]
    dist = qi + BLOCK - ki
    return jnp.stack([_t5_bucket(dist * cfg[0]) for cfg in ATT_CFG]).astype(jnp.int32)


def _band(max_steps):
    row = lax.broadcasted_iota(jnp.int32, (BLOCK, 2 * BLOCK), 0)
    col = lax.broadcasted_iota(jnp.int32, (BLOCK, 2 * BLOCK), 1)
    dist = row + BLOCK - col
    return (dist >= 0) & (dist <= max_steps)


def _bias_build(table, buckets):
    def body(tab_ref, bt_ref, out_ref):
        col = lax.broadcasted_iota(jnp.int32, (BLOCK, 2 * BLOCK), 1)
        for ci, (_, max_steps, h0, nh) in enumerate(ATT_CFG):
            bt = bt_ref[ci]
            band = _band(max_steps)
            for h in range(h0, h0 + nh):
                acc = lax.fori_loop(0, N_BUCKETS, lambda b, acc: jnp.where(bt == b, tab_ref[b, h], acc),
                                    jnp.zeros((BLOCK, 2 * BLOCK), F32))
                out_ref[0, BLOCK * h:BLOCK * (h + 1), :] = jnp.where(band & (col >= BLOCK), acc, NEG)
                out_ref[1, BLOCK * h:BLOCK * (h + 1), :] = jnp.where(band, acc, NEG)

    return pl.pallas_call(
        body, name="bias_build",
        in_specs=[pl.BlockSpec(memory_space=pltpu.SMEM), pl.BlockSpec(memory_space=pltpu.VMEM)],
        out_specs=pl.BlockSpec(memory_space=pltpu.VMEM),
        out_shape=jax.ShapeDtypeStruct((2, TOTAL_HEADS * BLOCK, 2 * BLOCK), F32),
    )(table, buckets)


def _bias_reduce(dbias, buckets, dsink_rows):
    def body(db_ref, bt_ref, ds_ref, out_ref, sink_ref):
        ri = lax.broadcasted_iota(jnp.int32, (N_BUCKETS, 128), 0)
        ci = lax.broadcasted_iota(jnp.int32, (N_BUCKETS, 128), 1)

        def per_bucket(b, acc):
            for cfg_i, (_, _, h0, nh) in enumerate(ATT_CFG):
                hit = bt_ref[cfg_i] == b
                for h in range(h0, h0 + nh):
                    val = jnp.sum(jnp.where(hit, db_ref[BLOCK * h:BLOCK * (h + 1), :], 0.0))
                    acc = jnp.where((ri == b) & (ci == h), val, acc)
            return acc

        out_ref[...] = lax.fori_loop(0, N_BUCKETS, per_bucket, jnp.zeros((N_BUCKETS, 128), F32))
        for h in range(8):
            sink_ref[h:h + 1, :] = jnp.sum(ds_ref[BLOCK * h:BLOCK * (h + 1), :], axis=0, keepdims=True)

    return pl.pallas_call(
        body, name="bias_reduce",
        in_specs=[pl.BlockSpec(memory_space=pltpu.VMEM)] * 3,
        out_specs=[pl.BlockSpec(memory_space=pltpu.VMEM)] * 2,
        out_shape=[jax.ShapeDtypeStruct((N_BUCKETS, 128), F32), jax.ShapeDtypeStruct((8, 128), F32)],
    )(dbias, buckets, dsink_rows)


class _Att:
    def __init__(self, cfg_i):
        stride, _, h0, nh = ATT_CFG[cfg_i]
        self.d = stride if cfg_i < 3 else 1
        self.h0, self.nh = h0, nh
        self.row_w = QKV_W if self.d == 1 else 3 * 256
        if cfg_i < 3:
            self.nq, self.wkv = 1, 256
            self.q_unit = [cfg_i if self.d == 1 else 0]
            self.k_unit, self.v_unit = (3 + cfg_i, 6 + cfg_i) if self.d == 1 else (1, 2)
            self.sinks = False
        else:
            self.nq, self.wkv = 2, 128
            self.q_unit = [9, 10]
            self.k_unit, self.v_unit = 22, 23
            self.sinks = True
        self.wq = 256 * self.nq


def _att_in_specs(cf, bsz):
    uq, ukv = cf.row_w // 256, cf.row_w // cf.wkv
    specs = [pl.BlockSpec((bsz, BLOCK, 256), functools.partial(lambda r, j, u: (0, j, r * uq + u), u=u))
             for u in cf.q_unit]
    for unit in (cf.k_unit, cf.v_unit):
        specs.append(pl.BlockSpec((bsz, BLOCK, cf.wkv),
                                  functools.partial(lambda r, j, u: (0, jnp.maximum(j - 1, 0), r * ukv + u), u=unit)))
        specs.append(pl.BlockSpec((bsz, BLOCK, cf.wkv),
                                  functools.partial(lambda r, j, u: (0, j, r * ukv + u), u=unit)))
    for qb in range(cf.nq):
        specs.append(pl.BlockSpec((None, HEADS_PER_BLOCK * BLOCK, 2 * BLOCK), functools.partial(
            lambda r, j, u: (jnp.minimum(j, 1), u, 0), u=cf.h0 // HEADS_PER_BLOCK + qb)))
    if cf.sinks:
        specs += [pl.BlockSpec((HEADS_PER_BLOCK * BLOCK, 128), functools.partial(lambda r, j, u: (u, 0), u=qb))
                  for qb in range(cf.nq)]
    return specs


HEADS_PER_BLOCK = 4


def _head_masks(rows):
    head = lax.broadcasted_iota(jnp.int32, (rows, 256), 1) // HEAD_DIM
    return [head == h for h in range(HEADS_PER_BLOCK)]


def _stack_heads(x, masks):
    return jnp.concatenate([jnp.where(m, x, jnp.zeros_like(x)) for m in masks], axis=0)


def _unstack_heads(x4, masks):
    blocks = [x4[BLOCK * h:BLOCK * (h + 1)] for h in range(HEADS_PER_BLOCK)]
    return jnp.where(masks[0], blocks[0], jnp.where(masks[1], blocks[1], jnp.where(masks[2], blocks[2], blocks[3])))


def _row_value(x):
    return jnp.max(x, axis=-1, keepdims=True)


def _kv_operand(cf, x, qb):
    if cf.wkv == 256:
        return x
    lane = lax.broadcasted_iota(jnp.int32, x.shape, 1)
    swapped = pltpu.roll(x, HEAD_DIM, 1)
    half = jnp.where(lane < HEAD_DIM, x, swapped) if qb == 0 else jnp.where(lane < HEAD_DIM, swapped, x)
    return jnp.concatenate([half, half], axis=1)


def _kv_fold(cf, grads):
    if cf.wkv == 256:
        return grads[0]
    folded = []
    for g in grads:
        x = g[:, :128] + g[:, 128:]
        folded.append(x + pltpu.roll(x, HEAD_DIM, 1))
    lane = lax.broadcasted_iota(jnp.int32, folded[0].shape, 1)
    return jnp.where(lane < HEAD_DIM, folded[0], folded[1])


def _attn_fwd(cf, zf, bias, sinks, name, hosted=()):
    bsz, l, _ = zf.shape
    nb = l // BLOCK

    def body(*refs):
        refs = list(refs)
        q_refs = [refs.pop(0) for _ in range(cf.nq)]
        kp_ref, kc_ref, vp_ref, vc_ref = [refs.pop(0) for _ in range(4)]
        bias_refs = [refs.pop(0) for _ in range(cf.nq)]
        sink_refs = [refs.pop(0) for _ in range(cf.nq)] if cf.sinks else None
        o_ref, lse_ref = refs
        masks, kv_masks = _head_masks(BLOCK), _head_masks(2 * BLOCK)
        sinks4 = [_row_value(ref[...]) for ref in sink_refs] if cf.sinks else None
        for bi in range(bsz):
            k = jnp.concatenate([kp_ref[bi], kc_ref[bi]], axis=0)
            v = jnp.concatenate([vp_ref[bi], vc_ref[bi]], axis=0)
            for qb in range(cf.nq):
                cols = slice(256 * qb, 256 * (qb + 1))
                kb, vb = _kv_operand(cf, k, qb), _kv_operand(cf, v, qb)
                q4 = _stack_heads(q_refs[qb][bi] * (HEAD_DIM ** -0.5), masks)
                s = _nt(q4, kb) + bias_refs[qb][...]
                m = jnp.max(s, axis=-1, keepdims=True)
                if cf.sinks:
                    sk = sinks4[qb]
                    m = jnp.maximum(m, sk)
                p = jnp.exp(s - m)
                den = jnp.sum(p, axis=-1, keepdims=True)
                if cf.sinks:
                    den = den + jnp.exp(sk - m)
                pn = (p * (1.0 / den)).astype(BF16)
                p_lanes = jnp.concatenate([pn[BLOCK * h:BLOCK * (h + 1)] for h in range(HEADS_PER_BLOCK)], axis=1)
                v4 = jnp.concatenate([jnp.where(mk, vb, jnp.zeros_like(vb)) for mk in kv_masks], axis=0)
                o_ref[bi, :, cols] = _nn(p_lanes, v4)
                lse_ref[bi, :, cols] = _unstack_heads(
                    jnp.broadcast_to(m + jnp.log(den), (HEADS_PER_BLOCK * BLOCK, 256)), masks)

    in_specs = _att_in_specs(cf, bsz)
    args = [zf] * (cf.nq + 4) + [bias] * cf.nq + ([sinks] * cf.nq if cf.sinks else [])
    out = pl.BlockSpec((bsz, BLOCK, cf.wq), lambda r, j: (0, j, r))
    shape = jax.ShapeDtypeStruct((bsz, l, cf.d * cf.wq), F32)
    return _call(
        body, name=name, grid=(cf.d, nb), in_specs=in_specs, out_specs=[out, out], out_shape=[shape, shape],
        sem=("parallel", "arbitrary"), args=args, hosted=hosted)


def _attn_bwd(cf, zf, bias, sinks, lse_tot, dy, y, name, hosted=()):
    bsz, l, _ = zf.shape
    nb = l // BLOCK

    def body(*refs):
        refs = list(refs)
        q_refs = [refs.pop(0) for _ in range(cf.nq)]
        kp_ref, kc_ref, vp_ref, vc_ref = [refs.pop(0) for _ in range(4)]
        bias_refs = [refs.pop(0) for _ in range(cf.nq)]
        sink_refs = [refs.pop(0) for _ in range(cf.nq)] if cf.sinks else None
        lt_ref, dy_ref, y_ref, dq_ref, dk_ref, dv_ref, dbias_ref = [refs.pop(0) for _ in range(7)]
        dsink_ref = refs.pop(0) if cf.sinks else None
        dk_acc, dv_acc = refs
        r, j = pl.program_id(0), pl.program_id(1)

        @pl.when((r == 0) & (j == 0))
        def _():
            dbias_ref[...] = jnp.zeros_like(dbias_ref)
            if cf.sinks:
                dsink_ref[...] = jnp.zeros_like(dsink_ref)

        @pl.when(j == 0)
        def _():
            dk_acc[...] = jnp.zeros_like(dk_acc)
            dv_acc[...] = jnp.zeros_like(dv_acc)

        masks = _head_masks(BLOCK)
        cur = pl.ds(pl.multiple_of(j * BLOCK, BLOCK), BLOCK)
        prev = pl.ds(pl.multiple_of(jnp.maximum(j - 1, 0) * BLOCK, BLOCK), BLOCK)
        sinks4 = [_row_value(ref[...]) for ref in sink_refs] if cf.sinks else None
        ds_sum, dsink_sum = [None] * cf.nq, [None] * cf.nq
        for bi in range(bsz):
            k = jnp.concatenate([kp_ref[bi], kc_ref[bi]], axis=0)
            v = jnp.concatenate([vp_ref[bi], vc_ref[bi]], axis=0)
            dk_blocks, dv_blocks = [], []
            for qb in range(cf.nq):
                cols = slice(256 * qb, 256 * (qb + 1))
                kb, vb = _kv_operand(cf, k, qb), _kv_operand(cf, v, qb)
                q4 = _stack_heads(q_refs[qb][bi] * (HEAD_DIM ** -0.5), masks)
                lt = lt_ref[bi, :, cols]
                lt4 = jnp.concatenate([_row_value(jnp.where(mk, lt, NEG)) for mk in masks], axis=0)
                pa = jnp.exp(_nt(q4, kb) + bias_refs[qb][...] - lt4)
                dy = dy_ref[bi, :, cols]
                prod = dy * y_ref[bi, :, cols]
                e4 = jnp.concatenate([jnp.sum(jnp.where(mk, prod, 0.0), axis=-1, keepdims=True) for mk in masks],
                                     axis=0)
                dy4 = _stack_heads(dy.astype(BF16), masks)
                ds = pa * (_nt(dy4, vb) - e4)
                ds_sum[qb] = ds if ds_sum[qb] is None else ds_sum[qb] + ds
                if cf.sinks:
                    dsk = jnp.exp(sinks4[qb] - lt4) * e4
                    dsink_sum[qb] = dsk if dsink_sum[qb] is None else dsink_sum[qb] + dsk
                dsb = ds.astype(BF16)
                dq_ref[bi, :, cols] = (_unstack_heads(_nn(dsb, kb), masks) * (HEAD_DIM ** -0.5)).astype(BF16)
                dk_blocks.append(_tn(dsb, q4))
                dv_blocks.append(_tn(pa.astype(BF16), dy4))
            dk_new, dv_new = _kv_fold(cf, dk_blocks), _kv_fold(cf, dv_blocks)
            dk_acc[bi, cur, :] += dk_new[BLOCK:]
            dv_acc[bi, cur, :] += dv_new[BLOCK:]
            dk_acc[bi, prev, :] += dk_new[:BLOCK]
            dv_acc[bi, prev, :] += dv_new[:BLOCK]
        for qb in range(cf.nq):
            rows = slice(HEADS_PER_BLOCK * BLOCK * qb, HEADS_PER_BLOCK * BLOCK * (qb + 1))
            dbias_ref[rows, :] += ds_sum[qb]
            if cf.sinks:
                dsink_ref[rows, :] -= dsink_sum[qb]

        @pl.when(j == nb - 1)
        def _():
            dk_ref[...] = dk_acc[...].astype(BF16)
            dv_ref[...] = dv_acc[...].astype(BF16)

    tok = pl.BlockSpec((bsz, BLOCK, cf.wq), lambda r, j: (0, j, r))
    in_specs = _att_in_specs(cf, bsz) + [tok] * 3
    args = [zf] * (cf.nq + 4) + [bias] * cf.nq + ([sinks] * cf.nq if cf.sinks else []) + [lse_tot, dy, y]
    seq = pl.BlockSpec((bsz, l, cf.wkv), lambda r, j: (0, 0, r))
    out_specs = [tok, seq, seq, pl.BlockSpec((cf.nh * BLOCK, 2 * BLOCK), lambda r, j: (0, 0))]
    out_shape = [jax.ShapeDtypeStruct((bsz, l, cf.d * cf.wq), BF16),
                 jax.ShapeDtypeStruct((bsz, l, cf.d * cf.wkv), BF16),
                 jax.ShapeDtypeStruct((bsz, l, cf.d * cf.wkv), BF16),
                 jax.ShapeDtypeStruct((cf.nh * BLOCK, 2 * BLOCK), F32)]
    if cf.sinks:
        out_specs.append(pl.BlockSpec((cf.nh * BLOCK, 128), lambda r, j: (0, 0)))
        out_shape.append(jax.ShapeDtypeStruct((cf.nh * BLOCK, 128), F32))
    return _call(
        body, name=name, grid=(cf.d, nb), in_specs=in_specs, out_specs=out_specs, out_shape=out_shape,
        scratch=[pltpu.VMEM((bsz, l, cf.wkv), F32), pltpu.VMEM((bsz, l, cf.wkv), F32)],
        sem=("arbitrary", "arbitrary"), args=args, hosted=hosted)


def _merge_fwd(o_a, lse_a, o_b, zg, h, wa_t, wb_t, wout, name):
    t = h.shape[0]
    tm = 512

    def body(o1, o2, o3, l1, l2, l3, ob_ref, zg_ref, h_ref, wa_ref, wb_ref, wo_ref, h2_ref, ya_ref, lt_ref):
        m = jnp.maximum(jnp.maximum(l1[...], l2[...]), l3[...])
        e1, e2, e3 = jnp.exp(l1[...] - m), jnp.exp(l2[...] - m), jnp.exp(l3[...] - m)
        se = e1 + e2 + e3
        ya = (e1 / se) * o1[...] + (e2 / se) * o2[...] + (e3 / se) * o3[...]
        ya_ref[...] = ya
        lt_ref[...] = m + jnp.log(se)
        pa = _nt(ya.astype(BF16), wa_ref[...])
        pb = _nt(ob_ref[...].astype(BF16), wb_ref[...])
        merged = (jax.nn.sigmoid(zg_ref[:, :D_MODEL].astype(F32)) * pa
                  + jax.nn.sigmoid(zg_ref[:, D_MODEL:].astype(F32)) * pb)
        h2_ref[...] = h_ref[...] + _nn(merged.astype(BF16), wo_ref[...])

    row = lambda w: pl.BlockSpec((tm, w), lambda i: (i, 0))
    full = _resident
    return pl.pallas_call(
        body, name=name, grid=(t // tm,),
        in_specs=[row(256)] * 6 + [row(512), row(GATE_W), row(D_MODEL), full(wa_t), full(wb_t), full(wout)],
        out_specs=[row(D_MODEL), row(256), row(256)],
        out_shape=[jax.ShapeDtypeStruct((t, D_MODEL), F32), jax.ShapeDtypeStruct((t, 256), F32),
                   jax.ShapeDtypeStruct((t, 256), F32)],
        compiler_params=_params("parallel"),
    )(*o_a, *lse_a, o_b, zg, h, wa_t, wb_t, wout)


def _merge_bwd(dh, ya, o_b, zg, wa_t, wb_t, wout, name, hosted=()):
    t = dh.shape[0]
    tm = 256

    def body(dh_ref, ya_ref, ob_ref, zg_ref, wa_ref, wb_ref, wo_ref,
             mg_ref, dpa_ref, dpb_ref, dzg_ref, dya_ref, dyb_ref):
        dm = _nt(dh_ref[...].astype(BF16), wo_ref[...])
        pa = _nt(ya_ref[...].astype(BF16), wa_ref[...])
        pb = _nt(ob_ref[...].astype(BF16), wb_ref[...])
        sa = jax.nn.sigmoid(zg_ref[:, :D_MODEL].astype(F32))
        sb = jax.nn.sigmoid(zg_ref[:, D_MODEL:].astype(F32))
        mg_ref[...] = (sa * pa + sb * pb).astype(BF16)
        dpa = (dm * sa).astype(BF16)
        dpb = (dm * sb).astype(BF16)
        dpa_ref[...] = dpa
        dpb_ref[...] = dpb
        dzg_ref[:, :D_MODEL] = (dm * pa * (sa * (1.0 - sa))).astype(BF16)
        dzg_ref[:, D_MODEL:] = (dm * pb * (sb * (1.0 - sb))).astype(BF16)
        dya_ref[...] = _nn(dpa, wa_ref[...])
        dyb_ref[...] = _nn(dpb, wb_ref[...])

    row = lambda w: pl.BlockSpec((tm, w), lambda i: (i, 0))
    full = _resident
    sds = jax.ShapeDtypeStruct
    return _call(
        body, name=name, grid=(t // tm,),
        in_specs=[row(D_MODEL), row(256), row(512), row(GATE_W), full(wa_t), full(wb_t), full(wout)],
        out_specs=[row(D_MODEL), row(D_MODEL), row(D_MODEL), row(GATE_W), row(256), row(512)],
        out_shape=[sds((t, D_MODEL), BF16), sds((t, D_MODEL), BF16), sds((t, D_MODEL), BF16),
                   sds((t, GATE_W), BF16), sds((t, 256), F32), sds((t, 512), F32)],
        sem=("parallel",), args=(dh, ya, o_b, zg, wa_t, wb_t, wout), hosted=hosted)


def _loss_head(h, g, target, name):
    t = h.shape[0]
    tm = 1024

    def body(h_ref, g_ref, t_ref, dh_ref, loss_ref, dg_ref):
        @pl.when(pl.program_id(0) == 0)
        def _():
            loss_ref[...] = jnp.zeros_like(loss_ref)
            dg_ref[...] = jnp.zeros_like(dg_ref)

        xhat, r = _rms(h_ref[...], g_ref[...])
        err = xhat * g_ref[...] - t_ref[...]
        loss_ref[...] += 0.5 * jnp.sum(jnp.mean(err * err, axis=-1, keepdims=True), axis=0, keepdims=True)
        dx, dg = _rms_bwd(err * (1.0 / D_MODEL), xhat, r, g_ref[...])
        dh_ref[...] = dx
        dg_ref[...] += dg

    row = pl.BlockSpec((tm, D_MODEL), lambda i: (i, 0))
    vec = pl.BlockSpec((1, D_MODEL), lambda i: (0, 0))
    return pl.pallas_call(
        body, name=name, grid=(t // tm,),
        in_specs=[row, vec, row], out_specs=[row, pl.BlockSpec((1, 128), lambda i: (0, 0)), vec],
        out_shape=[jax.ShapeDtypeStruct((t, D_MODEL), F32), jax.ShapeDtypeStruct((1, 128), F32),
                   jax.ShapeDtypeStruct((1, D_MODEL), F32)],
        compiler_params=_params("arbitrary"),
    )(h, g, target)


def _pair_sum(grad, got, name):
    _, _, r, cdim = grad.shape
    core = lax.axis_index("c").astype(jnp.int32).reshape(1)

    def body(core_ref, g_ref, s_ref, o_ref):
        o_ref[...] = (g_ref[...] + s_ref[...]).astype(BF16)

    return pl.pallas_call(
        body, name=name,
        grid_spec=pltpu.PrefetchScalarGridSpec(
            num_scalar_prefetch=1, grid=(N_CHIP,),
            in_specs=[pl.BlockSpec((None, None, r, cdim), lambda q, core_ref: (q, core_ref[0], 0, 0)),
                      pl.BlockSpec((None, None, r, cdim), lambda q, core_ref: (q, 0, 0, 0))],
            out_specs=pl.BlockSpec((None, r, cdim), lambda q, core_ref: (q, 0, 0))),
        out_shape=jax.ShapeDtypeStruct((N_CHIP, r, cdim), BF16),
        compiler_params=_params("parallel"),
    )(core, grad, got)


def _adamw_math(w, g, m, v):
    m = ADAM_B1 * m + (1.0 - ADAM_B1) * g
    v = ADAM_B2 * v + (1.0 - ADAM_B2) * jnp.square(g)
    m_hat = m / (1.0 - ADAM_B1 ** ADAM_STEP)
    v_hat = v / (1.0 - ADAM_B2 ** ADAM_STEP)
    delta = -ADAM_LR * (m_hat / (jnp.sqrt(v_hat) + ADAM_EPS) + ADAM_WD * w)
    return delta, m, v


def _update(w, parts, m, v, transposed, name):
    r, c = parts.shape[1:]

    def body(w_ref, p_ref, m_ref, v_ref, g_ref, d_ref, m2_ref, v2_ref):
        def total(rows):
            return ((p_ref[0, rows].astype(F32) + p_ref[1, rows].astype(F32)) + p_ref[2, rows].astype(F32)) \
                + p_ref[3, rows].astype(F32)

        def update(at, g):
            g_ref[at] = g
            d_ref[at], m2_ref[at], v2_ref[at] = _adamw_math(w_ref[at], g, m_ref[at], v_ref[at])

        if not transposed:
            update((slice(None), slice(None)), total(slice(None)))
            return
        for r0 in range(0, r, 128):
            n = min(128, r - r0)
            gt = total(slice(r0, r0 + n))
            if n < 128:
                gt = jnp.concatenate([gt, jnp.zeros((128 - n, c), F32)], axis=0)
            update((slice(None), slice(r0, r0 + n)), gt.T[:, :n])

    sds = jax.ShapeDtypeStruct(w.shape, F32)
    return pl.pallas_call(body, name=name, out_shape=[sds] * 4,
                          compiler_params=pltpu.CompilerParams(vmem_limit_bytes=VMEM_LIMIT))(w, parts, m, v)


SMALL_ROWS = 80


def _small_update(g, w, m, v, name):
    def body(g_ref, w_ref, m_ref, v_ref, gs_ref, d_ref, m2_ref, v2_ref, got_ref, send_sems, recv_sems):
        x, y, c = _place()
        me = 4 * x + 2 * y + c
        got_ref[me] = g_ref[...]
        copies = []
        for k in range(1, N_DEV):
            peer = (x ^ (k >> 2), y ^ ((k >> 1) & 1), c ^ (k & 1))
            cp = pltpu.make_async_remote_copy(
                src_ref=g_ref, dst_ref=got_ref.at[me], send_sem=send_sems.at[k - 1], recv_sem=recv_sems.at[k - 1],
                device_id=peer, device_id_type=MESH)
            cp.start()
            copies.append(cp)
        for cp in copies:
            cp.wait()
        total = got_ref[0]
        for k in range(1, N_DEV):
            total = total + got_ref[k]
        gs_ref[...] = total
        d_ref[...], m2_ref[...], v2_ref[...] = _adamw_math(w_ref[...], total, m_ref[...], v_ref[...])

    sds = jax.ShapeDtypeStruct((SMALL_ROWS, 128), F32)
    vm = pl.BlockSpec(memory_space=pltpu.VMEM)
    return pl.pallas_call(
        body, name=name, in_specs=[vm] * 4, out_specs=[vm] * 4, out_shape=[sds] * 4,
        scratch_shapes=[pltpu.VMEM((N_DEV, SMALL_ROWS, 128), F32), pltpu.SemaphoreType.DMA((N_DEV - 1,)),
                        pltpu.SemaphoreType.DMA((N_DEV - 1,))],
    )(g, w, m, v)


def _pack_small(gains, b_in, rel_bias, sinks, last):
    rows = [a.reshape(8, 128) for a in gains] + [b_in.reshape(40, 128), rel_bias.reshape(5, 128),
                                                 jnp.pad(sinks.reshape(1, 8), ((0, 0), (0, 120))), last]
    rows.append(jnp.zeros((SMALL_ROWS - 79, 128), F32))
    return jnp.concatenate(rows, axis=0)


def _unpack_small(p, like):
    out = [p[8 * i:8 * i + 8].reshape(like[i].shape) for i in range(4)]
    out.append(p[32:72].reshape(like[4].shape))
    out.append(p[72:77].reshape(like[5].shape))
    out.append(p[77, :8].reshape(like[6].shape))
    return out


def kernel(x, ffn1_norm, ffn1_w_gate, ffn1_w_up, ffn1_w_down, mix_norm, w_in, b_in, w_branch_a, w_branch_b, w_out, sinks, rel_bias, ffn2_norm, ffn2_w_gate, ffn2_w_up, ffn2_w_down, final_norm, loss_target, m_ffn1_norm, m_ffn1_w_gate, m_ffn1_w_up, m_ffn1_w_down, m_mix_norm, m_w_in, m_b_in, m_w_branch_a, m_w_branch_b, m_w_out, m_sinks, m_rel_bias, m_ffn2_norm, m_ffn2_w_gate, m_ffn2_w_up, m_ffn2_w_down, m_final_norm, v_ffn1_norm, v_ffn1_w_gate, v_ffn1_w_up, v_ffn1_w_down, v_mix_norm, v_w_in, v_b_in, v_w_branch_a, v_w_branch_b, v_w_out, v_sinks, v_rel_bias, v_ffn2_norm, v_ffn2_w_gate, v_ffn2_w_up, v_ffn2_w_down, v_final_norm):
    bsz, seq, _ = x.shape
    t = bsz * seq
    xt = x.reshape(t, D_MODEL)
    target = loss_target.reshape(t, D_MODEL)

    big = [("ffn1_w_gate", ffn1_w_gate, m_ffn1_w_gate, v_ffn1_w_gate, True),
           ("ffn1_w_up", ffn1_w_up, m_ffn1_w_up, v_ffn1_w_up, True),
           ("ffn1_w_down", ffn1_w_down, m_ffn1_w_down, v_ffn1_w_down, False),
           ("w_in", w_in, m_w_in, v_w_in, True),
           ("w_branch_a", w_branch_a, m_w_branch_a, v_w_branch_a, True),
           ("w_branch_b", w_branch_b, m_w_branch_b, v_w_branch_b, True),
           ("w_out", w_out, m_w_out, v_w_out, False),
           ("ffn2_w_gate", ffn2_w_gate, m_ffn2_w_gate, v_ffn2_w_gate, True),
           ("ffn2_w_up", ffn2_w_up, m_ffn2_w_up, v_ffn2_w_up, True),
           ("ffn2_w_down", ffn2_w_down, m_ffn2_w_down, v_ffn2_w_down, False)]
    shard = {nm: (w[0].T if tr else w[0]).astype(BF16) for nm, w, _, _, tr in big}
    full = {}

    def gather(names):
        return _Gather([shard[nm] for nm in names])

    def keep(names, got):
        for nm, gw in zip(names, got):
            full[nm] = gw.reshape(-1, gw.shape[-1])

    ffn1_names = ["ffn1_w_gate", "ffn1_w_up", "ffn1_w_down"]
    mix_names = ["w_in", "w_branch_a", "w_branch_b", "w_out"]
    ffn2_names = ["ffn2_w_gate", "ffn2_w_up", "ffn2_w_down"]
    g1, gm, g2, gf = ffn1_norm, mix_norm, ffn2_norm, final_norm.reshape(1, D_MODEL)

    keep(ffn1_names, _exchange([gather(ffn1_names)], "gather_ffn1")[0])
    (h1, n1, a1, b1, hff1), (got,) = _ffn_fwd(xt, g1, full["ffn1_w_gate"], full["ffn1_w_up"], full["ffn1_w_down"], "ffn1_fwd",
                                hosted=[gather(mix_names)])
    keep(mix_names, got)
    (u, zq, zg), (got,) = _inproj_fwd(h1, gm, full["w_in"], b_in, "inproj_fwd", hosted=[gather(ffn2_names[:2])])
    keep(ffn2_names[:2], got)
    buckets = _bucket_tiles()
    bias = _bias_build(rel_bias, buckets)
    sink_rows = jnp.broadcast_to(sinks.reshape(8, 1, 1), (8, BLOCK, 128)).reshape(8 * BLOCK, 128)
    cfs = [_Att(i) for i in range(4)]
    zfold = []
    for i, cf in enumerate(cfs):
        if cf.d == 1:
            zfold.append(zq.reshape(bsz, seq, QKV_W))
        else:
            own = jnp.concatenate([zq[:, 768 * part + 256 * i:768 * part + 256 * (i + 1)] for part in range(3)], axis=1)
            zfold.append(own.reshape(bsz, seq // cf.d, cf.d * cf.row_w))
    att = [None] * 4
    for i in (3, 0, 1, 2):
        cf = cfs[i]
        (o, lse), got = _attn_fwd(cf, zfold[i], bias, sink_rows, f"attn{i}_fwd",
                                  hosted=[gather(ffn2_names[2:])] if i == 3 else ())
        if i == 3:
            keep(ffn2_names[2:], got[0])
        att[i] = (o.reshape(t, cf.wq), lse.reshape(t, cf.wq))
    o_b, lse_b = att[3]
    h2, ya, lse_tot = _merge_fwd([a[0] for a in att[:3]], [a[1] for a in att[:3]], o_b, zg, h1,
                                 full["w_branch_a"], full["w_branch_b"], full["w_out"], "merge_fwd")
    (h3, n2, a2, b2, hff2), _ = _ffn_fwd(h2, g2, full["ffn2_w_gate"], full["ffn2_w_up"], full["ffn2_w_down"], "ffn2_fwd")
    dh3, loss_part, dgf = _loss_head(h3, gf, target, "loss_head")

    grads, pair, from_chips = {}, {}, {}

    def by_owner(nm):
        return grads[nm].reshape(N_CHIP, 2, -1, grads[nm].shape[-1])

    def to_core(names):
        return _CoreExchange([by_owner(nm) for nm in names])

    def pair_up(names, got):
        for nm, sib in zip(names, got):
            pair[nm] = _pair_sum(by_owner(nm), sib, f"pair_sum_{nm}")

    def to_chips(names):
        return _ChipExchange([pair[nm] for nm in names])

    def landed(names, got):
        for nm, parts in zip(names, got):
            from_chips[nm] = parts

    dh2, dg2, da, db = _ffn_bwd(h2, a2, b2, g2, dh3, full["ffn2_w_gate"], full["ffn2_w_up"],
                                          full["ffn2_w_down"], "ffn2_bwd")
    grads["ffn2_w_gate"], _ = _tn_matmul(da, n2, 1408, "ffn2_dgate")
    grads["ffn2_w_up"], _ = _tn_matmul(db, n2, 1408, "ffn2_dup")
    grads["ffn2_w_down"], _ = _tn_matmul(hff2, dh3, 1408, "ffn2_ddown", scale=0.5)
    (merged, dpa, dpb, dzg, dya, dyb), (got,) = _merge_bwd(
        dh2, ya, o_b, zg, full["w_branch_a"], full["w_branch_b"], full["w_out"], "merge_bwd",
        hosted=[to_core(ffn2_names)])
    pair_up(ffn2_names, got)
    dq, dk, dv, dbias, dsink = [None] * 4, [None] * 4, [None] * 4, [None] * 4, None
    for i in (3, 0, 1, 2):
        cf = cfs[i]
        lt_i, dy_i, y_i = (lse_tot, dya, ya) if i < 3 else (lse_b, dyb, o_b)
        shp = (bsz, seq // cf.d, cf.d * cf.wq)
        hosted = {3: lambda: [to_chips(ffn2_names[:2])], 0: lambda: [to_chips(ffn2_names[2:])]}.get(i, list)()
        res, got = _attn_bwd(cf, zfold[i], bias, sink_rows, lt_i.reshape(shp), dy_i.reshape(shp), y_i.reshape(shp),
                             f"attn{i}_bwd", hosted=hosted)
        if i == 3:
            landed(ffn2_names[:2], got[0])
        elif i == 0:
            landed(ffn2_names[2:], got[0])
        dq[i] = res[0].reshape(t, cf.wq)
        dk[i] = res[1].reshape(t, cf.wkv)
        dv[i] = res[2].reshape(t, cf.wkv)
        dbias[i] = res[3]
        if cf.sinks:
            dsink = res[4]
    pieces = dq[:3] + dk[:3] + dv[:3] + [dq[3], dk[3], dv[3], dzg]
    dh1, dz, db_in, dgm = _inproj_bwd(pieces, h1, gm, dh2, full["w_in"], "inproj_bwd")
    dx, dg1, da, db = _ffn_bwd(xt, a1, b1, g1, dh1, full["ffn1_w_gate"], full["ffn1_w_up"],
                                         full["ffn1_w_down"], "ffn1_bwd")
    grads["w_in"], _ = _tn_matmul(dz, u, 1280, "dw_in")
    grads["ffn1_w_down"], (got,) = _tn_matmul(hff1, dh1, 1408, "ffn1_ddown", hosted=[to_core(["w_in"])], scale=0.5)
    pair_up(["w_in"], got)
    grads["ffn1_w_gate"], got = _tn_matmul(da, n1, 1408, "ffn1_dgate",
                                           hosted=[to_chips(["w_in"]), to_core(["ffn1_w_down"])])
    landed(["w_in"], got[0])
    pair_up(["ffn1_w_down"], got[1])
    grads["ffn1_w_up"], got = _tn_matmul(db, n1, 1408, "ffn1_dup",
                                         hosted=[to_chips(["ffn1_w_down"]), to_core(["ffn1_w_gate"])])
    landed(["ffn1_w_down"], got[0])
    pair_up(["ffn1_w_gate"], got[1])
    grads["w_out"], got = _tn_matmul(merged, dh2, 1024, "dw_out",
                                     hosted=[to_chips(["ffn1_w_gate"]), to_core(["ffn1_w_up"])])
    landed(["ffn1_w_gate"], got[0])
    pair_up(["ffn1_w_up"], got[1])
    grads["w_branch_b"], got = _tn_matmul(dpb, o_b, 1024, "dw_branch_b",
                                          hosted=[to_chips(["ffn1_w_up"]), to_core(["w_out"])])
    landed(["ffn1_w_up"], got[0])
    pair_up(["w_out"], got[1])
    grads["w_branch_a"], got = _tn_matmul(dpa, ya, 1024, "dw_branch_a",
                                          hosted=[to_chips(["w_out"]), to_core(["w_branch_b"])])
    landed(["w_out"], got[0])
    pair_up(["w_branch_b"], got[1])
    got = _exchange([to_chips(["w_branch_b"]), to_core(["w_branch_a"])], "reduce_scatter_tail1")
    landed(["w_branch_b"], got[0])
    pair_up(["w_branch_a"], got[1])
    landed(["w_branch_a"], _exchange([to_chips(["w_branch_a"])], "reduce_scatter_tail2")[0])
    dtable, dsinks = _bias_reduce(jnp.concatenate(dbias, axis=0), buckets, dsink)

    out_g, out_d, out_m, out_v = {}, {}, {}, {}
    for nm, w, m, v, tr in big:
        g, d2, m2, v2 = _update(w[0], from_chips[nm], m[0], v[0], tr, f"update_{nm}")
        out_g[nm], out_d[nm], out_m[nm], out_v[nm] = g[None], d2[None], m2[None], v2[None]

    small = [("ffn1_norm", ffn1_norm, m_ffn1_norm, v_ffn1_norm), ("mix_norm", mix_norm, m_mix_norm, v_mix_norm),
             ("ffn2_norm", ffn2_norm, m_ffn2_norm, v_ffn2_norm), ("final_norm", final_norm, m_final_norm, v_final_norm),
             ("b_in", b_in, m_b_in, v_b_in), ("rel_bias", rel_bias, m_rel_bias, v_rel_bias),
             ("sinks", sinks, m_sinks, v_sinks)]
    zero_row = jnp.zeros((1, 128), F32)
    pack = lambda arrs, last: _pack_small(arrs[:4], arrs[4], arrs[5], arrs[6], last)
    g_small = pack([dg1, dgm, dg2, dgf, db_in, dtable[:, :TOTAL_HEADS], dsinks[:, 0]], loss_part)
    packed = [pack([s[k] for s in small], zero_row) for k in (1, 2, 3)]
    gs, ds, ms, vs = _small_update(g_small, *packed, "small_update")
    like = [s[1] for s in small]
    for nm_s, g_, d_, m_, v_ in zip([s[0] for s in small], _unpack_small(gs, like), _unpack_small(ds, like),
                                    _unpack_small(ms, like), _unpack_small(vs, like)):
        out_g[nm_s], out_d[nm_s], out_m[nm_s], out_v[nm_s] = g_, d_, m_, v_
    loss = gs[78, 0]

    order = ["ffn1_norm", "ffn1_w_gate", "ffn1_w_up", "ffn1_w_down", "mix_norm", "w_in", "b_in", "w_branch_a",
             "w_branch_b", "w_out", "sinks", "rel_bias", "ffn2_norm", "ffn2_w_gate", "ffn2_w_up", "ffn2_w_down",
             "final_norm"]
    return (loss, dx.reshape(x.shape), *[out_g[k] for k in order], *[out_d[k] for k in order],
            *[out_m[k] for k in order], *[out_v[k] for k in order])
```

```python
import functools
import math

import numpy as np
import jax
import jax.numpy as jnp
from jax import lax
from jax.experimental import pallas as pl
from jax.experimental.pallas import tpu as pltpu

D_MODEL = 1024
D_FF = 2816
FF_CHUNK = 256
HEAD_DIM = 64
BLOCK = 128
N_BUCKETS = 32
MAX_DISTANCE = 2048
A_HEADS = 12
TOTAL_HEADS = 20
DIL_GROUPS = ((128, 1), (512, 4), (2048, 16))
B_WINDOW = 128
QKV_W = 3072
GATE_W = 2048
D_IN = QKV_W + GATE_W
EPS = 1e-6
NEG = -1e30
N_DEV = 8
N_CHIP = 4
ADAM_LR, ADAM_B1, ADAM_B2, ADAM_EPS, ADAM_WD, ADAM_STEP = 0.001, 0.9, 0.999, 1e-08, 0.01, 10
VMEM_LIMIT = 56 * 1024 * 1024
MESH = pl.DeviceIdType.MESH
BF16 = jnp.bfloat16
F32 = jnp.float32
ANY = pl.BlockSpec(memory_space=pl.ANY)


def _params(*sem):
    return pltpu.CompilerParams(dimension_semantics=sem, vmem_limit_bytes=VMEM_LIMIT)


def _resident(a):
    return pl.BlockSpec(a.shape, lambda i: (0, 0), pipeline_mode=pl.Buffered(1))


def _place():
    return lax.axis_index("x"), lax.axis_index("y"), lax.axis_index("c")


class _Gather:
    def __init__(self, shards):
        self.ins = list(shards)
        n = self.n = len(shards)
        self.out_shape = [jax.ShapeDtypeStruct((N_DEV,) + s.shape, s.dtype) for s in shards]
        self.scratch = [pltpu.SemaphoreType.DMA((7 * n,)), pltpu.SemaphoreType.DMA((7 * n,)),
                        pltpu.SemaphoreType.DMA((n,))]

    def _copies(self, ins, outs, sems):
        send_sems, recv_sems, local_sems = sems
        x, y, c = _place()
        me, sibling = (x, y, c), (x, y, 1 - c)
        chips = [(1 - x, y), (x, 1 - y), (1 - x, 1 - y)]

        def copy(i, k, block, to, src=None):
            dst = outs[i].at[4 * block[0] + 2 * block[1] + block[2]]
            return pltpu.make_async_remote_copy(
                src_ref=dst if src is None else src, dst_ref=dst, send_sem=send_sems.at[7 * i + k],
                recv_sem=recv_sems.at[7 * i + k], device_id=to, device_id_type=MESH)

        n = self.n
        mine = [pltpu.make_async_copy(ins[i], outs[i].at[4 * x + 2 * y + c], local_sems.at[i]) for i in range(n)]
        first = [copy(i, 0, me, sibling, src=ins[i]) for i in range(n)]
        first += [copy(i, 1 + j, me, (*chip, c), src=ins[i]) for i in range(n) for j, chip in enumerate(chips)]
        landed = [copy(i, 1 + j, (*chip, c), me) for j, chip in enumerate(chips) for i in range(n)]
        passed = [copy(i, 4 + j, (*chip, c), sibling) for j, chip in enumerate(chips) for i in range(n)]
        from_sibling = [copy(i, 0, sibling, me) for i in range(n)]
        from_sibling += [copy(i, 4 + j, (*chip, 1 - c), me) for i in range(n) for j, chip in enumerate(chips)]
        return mine, first, landed, passed, from_sibling

    def start(self, ins, outs, sems):
        mine, first, _, _, _ = self._copies(ins, outs, sems)
        for cp in mine + first:
            cp.start()

    def mid(self, ins, outs, sems):
        _, _, landed, passed, _ = self._copies(ins, outs, sems)
        for got, fwd in zip(landed, passed):
            got.wait_recv()
            fwd.start()

    def end(self, ins, outs, sems):
        mine, first, _, passed, from_sibling = self._copies(ins, outs, sems)
        for cp in from_sibling:
            cp.wait_recv()
        for cp in first + passed:
            cp.wait_send()
        for cp in mine:
            cp.wait()


class _CoreExchange:
    def __init__(self, grads):
        self.ins = list(grads)
        n = self.n = len(grads)
        self.out_shape = [jax.ShapeDtypeStruct((N_CHIP, 1) + g.shape[2:], g.dtype) for g in grads]
        self.scratch = [pltpu.SemaphoreType.DMA((n,)), pltpu.SemaphoreType.DMA((n,))]

    def _copies(self, ins, outs, sems):
        x, y, c = _place()
        return [pltpu.make_async_remote_copy(
            src_ref=ins[i].at[:, pl.ds(1 - c, 1)], dst_ref=outs[i], send_sem=sems[0].at[i],
            recv_sem=sems[1].at[i], device_id=(x, y, 1 - c), device_id_type=MESH) for i in range(self.n)]

    def start(self, ins, outs, sems):
        for cp in self._copies(ins, outs, sems):
            cp.start()

    mid = None

    def end(self, ins, outs, sems):
        for cp in self._copies(ins, outs, sems):
            cp.wait()


class _ChipExchange:
    def __init__(self, parts):
        self.ins = list(parts)
        n = self.n = len(parts)
        self.out_shape = [jax.ShapeDtypeStruct(p.shape, p.dtype) for p in parts]
        self.scratch = [pltpu.SemaphoreType.DMA((3 * n,)), pltpu.SemaphoreType.DMA((3 * n,)),
                        pltpu.SemaphoreType.DMA((n,))]

    def _copies(self, ins, outs, sems):
        send_sems, recv_sems, local_sems = sems
        x, y, c = _place()
        my_chip = 2 * x + y
        copies = []
        for i in range(self.n):
            copies.append(pltpu.make_async_copy(ins[i].at[my_chip], outs[i].at[my_chip], local_sems.at[i]))
            for k, (qx, qy) in enumerate([(1 - x, y), (x, 1 - y), (1 - x, 1 - y)]):
                copies.append(pltpu.make_async_remote_copy(
                    src_ref=ins[i].at[2 * qx + qy], dst_ref=outs[i].at[my_chip], send_sem=send_sems.at[3 * i + k],
                    recv_sem=recv_sems.at[3 * i + k], device_id=(qx, qy, c), device_id_type=MESH))
        return copies

    def start(self, ins, outs, sems):
        for cp in self._copies(ins, outs, sems):
            cp.start()

    mid = None

    def end(self, ins, outs, sems):
        for cp in self._copies(ins, outs, sems):
            cp.wait()


def _call(body, *, name, grid, in_specs, out_specs, out_shape, args, scratch=(), sem=None, hosted=()):
    n_in, n_out, n_scr = len(in_specs), len(out_specs), len(scratch)
    x_in = [len(p.ins) for p in hosted]
    x_scr = [len(p.scratch) for p in hosted]
    steps = int(np.prod(grid))

    def wrapped(*refs):
        refs = list(refs)
        ins, refs = refs[:n_in], refs[n_in:]
        x_ins = [[refs.pop(0) for _ in range(k)] for k in x_in]
        outs, refs = refs[:n_out], refs[n_out:]
        x_outs = [[refs.pop(0) for _ in range(k)] for k in x_in]
        scr, refs = refs[:n_scr], refs[n_scr:]
        x_sems = [[refs.pop(0) for _ in range(k)] for k in x_scr]
        step = 0
        for d in range(len(grid)):
            step = step * grid[d] + pl.program_id(d)

        def phase(which, at):
            fns = [(getattr(p, which), a) for p, a in zip(hosted, zip(x_ins, x_outs, x_sems)) if getattr(p, which)]
            if fns:
                @pl.when(step == at)
                def _():
                    for fn, a in fns:
                        fn(*a)

        phase("start", 0)
        phase("mid", (3 * steps) // 4)
        body(*ins, *outs, *scr)
        phase("end", steps - 1)

    results = pl.pallas_call(
        wrapped, name=name, grid=grid,
        in_specs=list(in_specs) + [ANY] * sum(x_in), out_specs=list(out_specs) + [ANY] * sum(x_in),
        out_shape=list(out_shape) + [s for p in hosted for s in p.out_shape],
        scratch_shapes=list(scratch) + [s for p in hosted for s in p.scratch],
        compiler_params=_params(*(("arbitrary",) * len(grid) if hosted else sem)),
    )(*args, *[a for p in hosted for a in p.ins])
    own, rest = list(results[:n_out]), list(results[n_out:])
    return own, [[rest.pop(0) for _ in range(k)] for k in x_in]


def _exchange(programs, name):
    x_in = [len(p.ins) for p in programs]
    x_scr = [len(p.scratch) for p in programs]

    def body(*refs):
        refs = list(refs)
        x_ins = [[refs.pop(0) for _ in range(k)] for k in x_in]
        x_outs = [[refs.pop(0) for _ in range(k)] for k in x_in]
        x_sems = [[refs.pop(0) for _ in range(k)] for k in x_scr]
        for which in ("start", "mid", "end"):
            for p, a in zip(programs, zip(x_ins, x_outs, x_sems)):
                if getattr(p, which):
                    getattr(p, which)(*a)

    results = list(pl.pallas_call(
        body, name=name, in_specs=[ANY] * sum(x_in), out_specs=[ANY] * sum(x_in),
        out_shape=[s for p in programs for s in p.out_shape],
        scratch_shapes=[s for p in programs for s in p.scratch],
    )(*[a for p in programs for a in p.ins]))
    return [[results.pop(0) for _ in range(k)] for k in x_in]


def _nt(a, b):
    return lax.dot_general(a, b, (((1,), (1,)), ((), ())), preferred_element_type=F32)


def _nn(a, b):
    return lax.dot_general(a, b, (((1,), (0,)), ((), ())), preferred_element_type=F32)


def _tn(a, b):
    return lax.dot_general(a, b, (((0,), (0,)), ((), ())), preferred_element_type=F32)


def _rms(x, g):
    r = lax.rsqrt(jnp.mean(x * x, axis=-1, keepdims=True) + EPS)
    return x * r, r


def _rms_bwd(dn, xhat, r, g):
    dg = jnp.sum(dn * xhat, axis=0, keepdims=True)
    dxh = dn * g
    dx = r * (dxh - xhat * jnp.mean(dxh * xhat, axis=-1, keepdims=True))
    return dx, dg


def _ffn_fwd(x, g, wg_t, wu_t, wd, name, hosted=()):
    t = x.shape[0]
    tm = 256

    def body(x_ref, g_ref, wg_ref, wu_ref, wd_ref, h_ref, n_ref, a_ref, b_ref, hff_ref):
        xhat, _ = _rms(x_ref[...], g_ref[...])
        n = (xhat * g_ref[...]).astype(BF16)
        n_ref[...] = n
        for c in range(0, D_FF, FF_CHUNK):
            cols = slice(c, c + FF_CHUNK)
            a = _nt(n, wg_ref[cols, :])
            b = _nt(n, wu_ref[cols, :])
            a_ref[:, cols] = a.astype(BF16)
            b_ref[:, cols] = b.astype(BF16)
            hff_ref[:, cols] = (a * jax.nn.sigmoid(a) * b).astype(BF16)
        h_ref[...] = x_ref[...] + 0.5 * _nn(hff_ref[...], wd_ref[...])

    row = pl.BlockSpec((tm, D_MODEL), lambda i: (i, 0))
    hid = pl.BlockSpec((tm, D_FF), lambda i: (i, 0))
    return _call(
        body, name=name, grid=(t // tm,),
        in_specs=[row, _resident(g), _resident(wg_t), _resident(wu_t), _resident(wd)],
        out_specs=[row, row, hid, hid, hid],
        out_shape=[jax.ShapeDtypeStruct((t, D_MODEL), F32), jax.ShapeDtypeStruct((t, D_MODEL), BF16),
                   jax.ShapeDtypeStruct((t, D_FF), BF16), jax.ShapeDtypeStruct((t, D_FF), BF16),
                   jax.ShapeDtypeStruct((t, D_FF), BF16)],
        sem=("parallel",), args=(x, g, wg_t, wu_t, wd), hosted=hosted)


def _ffn_bwd(x, a_pre, b_pre, g, dh, wg_t, wu_t, wd, name):
    t = x.shape[0]
    tm = 256

    def body(x_ref, a_ref, b_ref, g_ref, dh_ref, wg_ref, wu_ref, wd_ref,
             dx_ref, dg_ref, da_ref, db_ref):
        @pl.when(pl.program_id(0) == 0)
        def _():
            dg_ref[...] = jnp.zeros_like(dg_ref)

        dhh = (0.5 * dh_ref[...]).astype(BF16)
        for c in range(0, D_FF, FF_CHUNK):
            cols = slice(c, c + FF_CHUNK)
            a = a_ref[:, cols].astype(F32)
            b = b_ref[:, cols].astype(F32)
            s = jax.nn.sigmoid(a)
            silu = a * s
            dhff = _nt(dhh, wd_ref[cols, :])
            da_ref[:, cols] = (dhff * b * (s * (1.0 + a * (1.0 - s)))).astype(BF16)
            db_ref[:, cols] = (dhff * silu).astype(BF16)
        dn = _nn(da_ref[...], wg_ref[...]) + _nn(db_ref[...], wu_ref[...])
        xhat, r = _rms(x_ref[...], g_ref[...])
        dx, dg = _rms_bwd(dn, xhat, r, g_ref[...])
        dx_ref[...] = dh_ref[...] + dx
        dg_ref[...] += dg

    row = pl.BlockSpec((tm, D_MODEL), lambda i: (i, 0))
    hid = pl.BlockSpec((tm, D_FF), lambda i: (i, 0))
    return pl.pallas_call(
        body, name=name, grid=(t // tm,),
        in_specs=[row, hid, hid, _resident(g), row, _resident(wg_t), _resident(wu_t), _resident(wd)],
        out_specs=[row, pl.BlockSpec((1, D_MODEL), lambda i: (0, 0)), hid, hid],
        out_shape=[jax.ShapeDtypeStruct((t, D_MODEL), F32), jax.ShapeDtypeStruct((1, D_MODEL), F32),
                   jax.ShapeDtypeStruct((t, D_FF), BF16), jax.ShapeDtypeStruct((t, D_FF), BF16)],
        compiler_params=_params("arbitrary"),
    )(x, a_pre, b_pre, g, dh, wg_t, wu_t, wd)


def _tn_matmul(a, b, rc, name, hosted=(), scale=None):
    t, r = a.shape
    c = b.shape[1]
    tk = 1024

    def body(a_ref, b_ref, o_ref):
        @pl.when(pl.program_id(1) == 0)
        def _():
            o_ref[...] = jnp.zeros_like(o_ref)

        o_ref[...] += _tn(a_ref[...].astype(BF16), b_ref[...].astype(BF16))
        if scale is not None:
            @pl.when(pl.program_id(1) == t // tk - 1)
            def _():
                o_ref[...] *= scale

    (out,), got = _call(
        body, name=name, grid=(r // rc, t // tk),
        in_specs=[pl.BlockSpec((tk, rc), lambda i, k: (k, i)), pl.BlockSpec((tk, c), lambda i, k: (k, 0))],
        out_specs=[pl.BlockSpec((rc, c), lambda i, k: (i, 0))],
        out_shape=[jax.ShapeDtypeStruct((r, c), F32)],
        sem=("parallel", "arbitrary"), args=(a, b), hosted=hosted)
    return out, got


PIECE_W = (256,) * 9 + (512, 128, 128, GATE_W)


def _inproj_fwd(h, g, w_t, b_in, name, hosted=()):
    t = h.shape[0]
    tm, nc = 512, 512

    def body(h_ref, g_ref, w_ref, b_ref, u_ref, zq_ref, zg_ref):
        xhat, _ = _rms(h_ref[...], g_ref[...])
        u = (xhat * g_ref[...]).astype(BF16)
        u_ref[...] = u
        for c in range(D_IN // nc):
            z = _nt(u, w_ref[c * nc:(c + 1) * nc, :]) + b_ref[:, c * nc:(c + 1) * nc]
            if c < QKV_W // nc:
                zq_ref[:, c * nc:(c + 1) * nc] = z.astype(BF16)
            else:
                zg_ref[:, c * nc - QKV_W:(c + 1) * nc - QKV_W] = z.astype(BF16)

    row = lambda w: pl.BlockSpec((tm, w), lambda i: (i, 0))
    full = _resident
    return _call(
        body, name=name, grid=(t // tm,),
        in_specs=[row(D_MODEL), full(g), full(w_t), full(b_in)],
        out_specs=[row(D_MODEL), row(QKV_W), row(GATE_W)],
        out_shape=[jax.ShapeDtypeStruct((t, D_MODEL), BF16), jax.ShapeDtypeStruct((t, QKV_W), BF16),
                   jax.ShapeDtypeStruct((t, GATE_W), BF16)],
        sem=("parallel",), args=(h, g, w_t, b_in), hosted=hosted)


def _inproj_bwd(pieces, h, g, dh_res, w_t, name):
    t = h.shape[0]
    tm = 256
    npiece = len(PIECE_W)
    offs = np.concatenate([[0], np.cumsum(PIECE_W)]).tolist()

    def body(*refs):
        p_refs = refs[:npiece]
        h_ref, g_ref, dhr_ref, w_ref, dh_ref, dz_ref, db_ref, dg_ref = refs[npiece:]
        i = pl.program_id(0)

        @pl.when(i == 0)
        def _():
            db_ref[...] = jnp.zeros_like(db_ref)
            dg_ref[...] = jnp.zeros_like(dg_ref)

        du = jnp.zeros((tm, D_MODEL), F32)
        for k in range(npiece):
            o, w = offs[k], PIECE_W[k]
            for c0 in range(0, w, 512):
                cw = min(512, w - c0)
                pz = p_refs[k][:, c0:c0 + cw]
                dz_ref[:, o + c0:o + c0 + cw] = pz
                db_ref[:, o + c0:o + c0 + cw] += jnp.sum(pz.astype(F32), axis=0, keepdims=True)
                du = du + _nn(pz, w_ref[o + c0:o + c0 + cw, :])
        xhat, r = _rms(h_ref[...], g_ref[...])
        dx, dg = _rms_bwd(du, xhat, r, g_ref[...])
        dh_ref[...] = dhr_ref[...] + dx
        dg_ref[...] += dg

    row = lambda w: pl.BlockSpec((tm, w), lambda i: (i, 0))
    full = lambda shp: pl.BlockSpec(shp, lambda i: (0, 0))
    return pl.pallas_call(
        body, name=name, grid=(t // tm,),
        in_specs=[row(w) for w in PIECE_W] + [row(D_MODEL), _resident(g), row(D_MODEL), _resident(w_t)],
        out_specs=[row(D_MODEL), row(D_IN), full((1, D_IN)), full((1, D_MODEL))],
        out_shape=[jax.ShapeDtypeStruct((t, D_MODEL), F32), jax.ShapeDtypeStruct((t, D_IN), BF16),
                   jax.ShapeDtypeStruct((1, D_IN), F32), jax.ShapeDtypeStruct((1, D_MODEL), F32)],
        compiler_params=_params("arbitrary"),
    )(*pieces, h, g, dh_res, w_t)


def _t5_bucket(dist):
    max_exact = N_BUCKETS // 2
    n = jnp.maximum(dist, 0)
    nf = jnp.maximum(n, 1).astype(jnp.float32)
    large = max_exact + (jnp.log(nf / max_exact) / math.log(MAX_DISTANCE / max_exact)
                         * (N_BUCKETS - max_exact)).astype(jnp.int32)
    large = jnp.minimum(large, N_BUCKETS - 1)
    return jnp.where(n < max_exact, n, large)


ATT_CFG = ((1, 128, 0, 4), (4, 128, 4, 4), (16, 128, 8, 4), (1, B_WINDOW - 1, A_HEADS, 8))


def _bucket_tiles():
    qi = jnp.arange(BLOCK)[:, None]
    ki = jnp.arange(2 * BLOCK)[None, :]
    dist = qi + BLOCK - ki
    return jnp.stack([_t5_bucket(dist * cfg[0]) for cfg in ATT_CFG]).astype(jnp.int32)


def _band(max_steps):
    row = lax.broadcasted_iota(jnp.int32, (BLOCK, 2 * BLOCK), 0)
    col = lax.broadcasted_iota(jnp.int32, (BLOCK, 2 * BLOCK), 1)
    dist = row + BLOCK - col
    return (dist >= 0) & (dist <= max_steps)


def _bias_build(table, buckets):
    def body(tab_ref, bt_ref, out_ref):
        col = lax.broadcasted_iota(jnp.int32, (BLOCK, 2 * BLOCK), 1)
        for ci, (_, max_steps, h0, nh) in enumerate(ATT_CFG):
            bt = bt_ref[ci]
            band = _band(max_steps)
            for h in range(h0, h0 + nh):
                acc = lax.fori_loop(0, N_BUCKETS, lambda b, acc: jnp.where(bt == b, tab_ref[b, h], acc),
                                    jnp.zeros((BLOCK, 2 * BLOCK), F32))
                out_ref[0, BLOCK * h:BLOCK * (h + 1), :] = jnp.where(band & (col >= BLOCK), acc, NEG)
                out_ref[1, BLOCK * h:BLOCK * (h + 1), :] = jnp.where(band, acc, NEG)

    return pl.pallas_call(
        body, name="bias_build",
        in_specs=[pl.BlockSpec(memory_space=pltpu.SMEM), pl.BlockSpec(memory_space=pltpu.VMEM)],
        out_specs=pl.BlockSpec(memory_space=pltpu.VMEM),
        out_shape=jax.ShapeDtypeStruct((2, TOTAL_HEADS * BLOCK, 2 * BLOCK), F32),
    )(table, buckets)


def _bias_reduce(dbias, buckets, dsink_rows):
    def body(db_ref, bt_ref, ds_ref, out_ref, sink_ref):
        ri = lax.broadcasted_iota(jnp.int32, (N_BUCKETS, 128), 0)
        ci = lax.broadcasted_iota(jnp.int32, (N_BUCKETS, 128), 1)

        def per_bucket(b, acc):
            for cfg_i, (_, _, h0, nh) in enumerate(ATT_CFG):
                hit = bt_ref[cfg_i] == b
                for h in range(h0, h0 + nh):
                    val = jnp.sum(jnp.where(hit, db_ref[BLOCK * h:BLOCK * (h + 1), :], 0.0))
                    acc = jnp.where((ri == b) & (ci == h), val, acc)
            return acc

        out_ref[...] = lax.fori_loop(0, N_BUCKETS, per_bucket, jnp.zeros((N_BUCKETS, 128), F32))
        for h in range(8):
            sink_ref[h:h + 1, :] = jnp.sum(ds_ref[BLOCK * h:BLOCK * (h + 1), :], axis=0, keepdims=True)

    return pl.pallas_call(
        body, name="bias_reduce",
        in_specs=[pl.BlockSpec(memory_space=pltpu.VMEM)] * 3,
        out_specs=[pl.BlockSpec(memory_space=pltpu.VMEM)] * 2,
        out_shape=[jax.ShapeDtypeStruct((N_BUCKETS, 128), F32), jax.ShapeDtypeStruct((8, 128), F32)],
    )(dbias, buckets, dsink_rows)


class _Att:
    def __init__(self, cfg_i):
        stride, _, h0, nh = ATT_CFG[cfg_i]
        self.d = stride if cfg_i < 3 else 1
        self.h0, self.nh = h0, nh
        self.row_w = QKV_W if self.d == 1 else 3 * 256
        if cfg_i < 3:
            self.nq, self.wkv = 1, 256
            self.q_unit = [cfg_i if self.d == 1 else 0]
            self.k_unit, self.v_unit = (3 + cfg_i, 6 + cfg_i) if self.d == 1 else (1, 2)
            self.sinks = False
        else:
            self.nq, self.wkv = 2, 128
            self.q_unit = [9, 10]
            self.k_unit, self.v_unit = 22, 23
            self.sinks = True
        self.wq = 256 * self.nq


def _att_in_specs(cf, bsz):
    uq, ukv = cf.row_w // 256, cf.row_w // cf.wkv
    specs = [pl.BlockSpec((bsz, BLOCK, 256), functools.partial(lambda r, j, u: (0, j, r * uq + u), u=u))
             for u in cf.q_unit]
    for unit in (cf.k_unit, cf.v_unit):
        specs.append(pl.BlockSpec((bsz, BLOCK, cf.wkv),
                                  functools.partial(lambda r, j, u: (0, jnp.maximum(j - 1, 0), r * ukv + u), u=unit)))
        specs.append(pl.BlockSpec((bsz, BLOCK, cf.wkv),
                                  functools.partial(lambda r, j, u: (0, j, r * ukv + u), u=unit)))
    for qb in range(cf.nq):
        specs.append(pl.BlockSpec((None, HEADS_PER_BLOCK * BLOCK, 2 * BLOCK), functools.partial(
            lambda r, j, u: (jnp.minimum(j, 1), u, 0), u=cf.h0 // HEADS_PER_BLOCK + qb)))
    if cf.sinks:
        specs += [pl.BlockSpec((HEADS_PER_BLOCK * BLOCK, 128), functools.partial(lambda r, j, u: (u, 0), u=qb))
                  for qb in range(cf.nq)]
    return specs


HEADS_PER_BLOCK = 4


def _head_masks(rows):
    head = lax.broadcasted_iota(jnp.int32, (rows, 256), 1) // HEAD_DIM
    return [head == h for h in range(HEADS_PER_BLOCK)]


def _stack_heads(x, masks):
    return jnp.concatenate([jnp.where(m, x, jnp.zeros_like(x)) for m in masks], axis=0)


def _unstack_heads(x4, masks):
    blocks = [x4[BLOCK * h:BLOCK * (h + 1)] for h in range(HEADS_PER_BLOCK)]
    return jnp.where(masks[0], blocks[0], jnp.where(masks[1], blocks[1], jnp.where(masks[2], blocks[2], blocks[3])))


def _row_value(x):
    return jnp.max(x, axis=-1, keepdims=True)


def _kv_operand(cf, x, qb):
    if cf.wkv == 256:
        return x
    lane = lax.broadcasted_iota(jnp.int32, x.shape, 1)
    swapped = pltpu.roll(x, HEAD_DIM, 1)
    half = jnp.where(lane < HEAD_DIM, x, swapped) if qb == 0 else jnp.where(lane < HEAD_DIM, swapped, x)
    return jnp.concatenate([half, half], axis=1)


def _kv_fold(cf, grads):
    if cf.wkv == 256:
        return grads[0]
    folded = []
    for g in grads:
        x = g[:, :128] + g[:, 128:]
        folded.append(x + pltpu.roll(x, HEAD_DIM, 1))
    lane = lax.broadcasted_iota(jnp.int32, folded[0].shape, 1)
    return jnp.where(lane < HEAD_DIM, folded[0], folded[1])


def _attn_fwd(cf, zf, bias, sinks, name, hosted=()):
    bsz, l, _ = zf.shape
    nb = l // BLOCK

    def body(*refs):
        refs = list(refs)
        q_refs = [refs.pop(0) for _ in range(cf.nq)]
        kp_ref, kc_ref, vp_ref, vc_ref = [refs.pop(0) for _ in range(4)]
        bias_refs = [refs.pop(0) for _ in range(cf.nq)]
        sink_refs = [refs.pop(0) for _ in range(cf.nq)] if cf.sinks else None
        o_ref, lse_ref = refs
        masks, kv_masks = _head_masks(BLOCK), _head_masks(2 * BLOCK)
        sinks4 = [_row_value(ref[...]) for ref in sink_refs] if cf.sinks else None
        for bi in range(bsz):
            k = jnp.concatenate([kp_ref[bi], kc_ref[bi]], axis=0)
            v = jnp.concatenate([vp_ref[bi], vc_ref[bi]], axis=0)
            for qb in range(cf.nq):
                cols = slice(256 * qb, 256 * (qb + 1))
                kb, vb = _kv_operand(cf, k, qb), _kv_operand(cf, v, qb)
                q4 = _stack_heads(q_refs[qb][bi] * (HEAD_DIM ** -0.5), masks)
                s = _nt(q4, kb) + bias_refs[qb][...]
                m = jnp.max(s, axis=-1, keepdims=True)
                if cf.sinks:
                    sk = sinks4[qb]
                    m = jnp.maximum(m, sk)
                p = jnp.exp(s - m)
                den = jnp.sum(p, axis=-1, keepdims=True)
                if cf.sinks:
                    den = den + jnp.exp(sk - m)
                pn = (p * (1.0 / den)).astype(BF16)
                p_lanes = jnp.concatenate([pn[BLOCK * h:BLOCK * (h + 1)] for h in range(HEADS_PER_BLOCK)], axis=1)
                v4 = jnp.concatenate([jnp.where(mk, vb, jnp.zeros_like(vb)) for mk in kv_masks], axis=0)
                o_ref[bi, :, cols] = _nn(p_lanes, v4)
                lse_ref[bi, :, cols] = _unstack_heads(
                    jnp.broadcast_to(m + jnp.log(den), (HEADS_PER_BLOCK * BLOCK, 256)), masks)

    in_specs = _att_in_specs(cf, bsz)
    args = [zf] * (cf.nq + 4) + [bias] * cf.nq + ([sinks] * cf.nq if cf.sinks else [])
    out = pl.BlockSpec((bsz, BLOCK, cf.wq), lambda r, j: (0, j, r))
    shape = jax.ShapeDtypeStruct((bsz, l, cf.d * cf.wq), F32)
    return _call(
        body, name=name, grid=(cf.d, nb), in_specs=in_specs, out_specs=[out, out], out_shape=[shape, shape],
        sem=("parallel", "arbitrary"), args=args, hosted=hosted)


def _attn_bwd(cf, zf, bias, sinks, stats, dy, name, hosted=()):
    bsz, l, _ = zf.shape
    nb = l // BLOCK

    def body(*refs):
        refs = list(refs)
        q_refs = [refs.pop(0) for _ in range(cf.nq)]
        kp_ref, kc_ref, vp_ref, vc_ref = [refs.pop(0) for _ in range(4)]
        bias_refs = [refs.pop(0) for _ in range(cf.nq)]
        sink_refs = [refs.pop(0) for _ in range(cf.nq)] if cf.sinks else None
        st_ref, dy_ref, dq_ref, dk_ref, dv_ref, dbias_ref = [refs.pop(0) for _ in range(6)]
        dsink_ref = refs.pop(0) if cf.sinks else None
        dk_acc, dv_acc = refs
        r, j = pl.program_id(0), pl.program_id(1)

        @pl.when((r == 0) & (j == 0))
        def _():
            dbias_ref[...] = jnp.zeros_like(dbias_ref)
            if cf.sinks:
                dsink_ref[...] = jnp.zeros_like(dsink_ref)

        @pl.when(j == 0)
        def _():
            dk_acc[...] = jnp.zeros_like(dk_acc)
            dv_acc[...] = jnp.zeros_like(dv_acc)

        masks = _head_masks(BLOCK)
        first_half = lax.broadcasted_iota(jnp.int32, (BLOCK, 256), 1) % HEAD_DIM < HEAD_DIM // 2
        cur = pl.ds(pl.multiple_of(j * BLOCK, BLOCK), BLOCK)
        prev = pl.ds(pl.multiple_of(jnp.maximum(j - 1, 0) * BLOCK, BLOCK), BLOCK)
        sinks4 = [_row_value(ref[...]) for ref in sink_refs] if cf.sinks else None
        ds_sum, dsink_sum = [None] * cf.nq, [None] * cf.nq
        for bi in range(bsz):
            k = jnp.concatenate([kp_ref[bi], kc_ref[bi]], axis=0)
            v = jnp.concatenate([vp_ref[bi], vc_ref[bi]], axis=0)
            dk_blocks, dv_blocks = [], []
            for qb in range(cf.nq):
                cols = slice(256 * qb, 256 * (qb + 1))
                kb, vb = _kv_operand(cf, k, qb), _kv_operand(cf, v, qb)
                q4 = _stack_heads(q_refs[qb][bi] * (HEAD_DIM ** -0.5), masks)
                st = st_ref[bi, :, cols]
                lt4 = jnp.concatenate([_row_value(jnp.where(mk & first_half, st, NEG)) for mk in masks], axis=0)
                e4 = jnp.concatenate([_row_value(jnp.where(mk & ~first_half, st, NEG)) for mk in masks], axis=0)
                pa = jnp.exp(_nt(q4, kb) + bias_refs[qb][...] - lt4)
                dy = dy_ref[bi, :, cols]
                dy4 = _stack_heads(dy.astype(BF16), masks)
                ds = pa * (_nt(dy4, vb) - e4)
                ds_sum[qb] = ds if ds_sum[qb] is None else ds_sum[qb] + ds
                if cf.sinks:
                    dsk = jnp.exp(sinks4[qb] - lt4) * e4
                    dsink_sum[qb] = dsk if dsink_sum[qb] is None else dsink_sum[qb] + dsk
                dsb = ds.astype(BF16)
                dq_ref[bi, :, cols] = (_unstack_heads(_nn(dsb, kb), masks) * (HEAD_DIM ** -0.5)).astype(BF16)
                dk_blocks.append(_tn(dsb, q4))
                dv_blocks.append(_tn(pa.astype(BF16), dy4))
            dk_new, dv_new = _kv_fold(cf, dk_blocks), _kv_fold(cf, dv_blocks)
            dk_acc[bi, cur, :] += dk_new[BLOCK:]
            dv_acc[bi, cur, :] += dv_new[BLOCK:]
            dk_acc[bi, prev, :] += dk_new[:BLOCK]
            dv_acc[bi, prev, :] += dv_new[:BLOCK]
        for qb in range(cf.nq):
            rows = slice(HEADS_PER_BLOCK * BLOCK * qb, HEADS_PER_BLOCK * BLOCK * (qb + 1))
            dbias_ref[rows, :] += ds_sum[qb]
            if cf.sinks:
                dsink_ref[rows, :] -= dsink_sum[qb]

        @pl.when(j == nb - 1)
        def _():
            dk_ref[...] = dk_acc[...].astype(BF16)
            dv_ref[...] = dv_acc[...].astype(BF16)

    tok = pl.BlockSpec((bsz, BLOCK, cf.wq), lambda r, j: (0, j, r))
    in_specs = _att_in_specs(cf, bsz) + [tok] * 2
    args = [zf] * (cf.nq + 4) + [bias] * cf.nq + ([sinks] * cf.nq if cf.sinks else []) + [stats, dy]
    seq = pl.BlockSpec((bsz, l, cf.wkv), lambda r, j: (0, 0, r))
    out_specs = [tok, seq, seq, pl.BlockSpec((cf.nh * BLOCK, 2 * BLOCK), lambda r, j: (0, 0))]
    out_shape = [jax.ShapeDtypeStruct((bsz, l, cf.d * cf.wq), BF16),
                 jax.ShapeDtypeStruct((bsz, l, cf.d * cf.wkv), BF16),
                 jax.ShapeDtypeStruct((bsz, l, cf.d * cf.wkv), BF16),
                 jax.ShapeDtypeStruct((cf.nh * BLOCK, 2 * BLOCK), F32)]
    if cf.sinks:
        out_specs.append(pl.BlockSpec((cf.nh * BLOCK, 128), lambda r, j: (0, 0)))
        out_shape.append(jax.ShapeDtypeStruct((cf.nh * BLOCK, 128), F32))
    return _call(
        body, name=name, grid=(cf.d, nb), in_specs=in_specs, out_specs=out_specs, out_shape=out_shape,
        scratch=[pltpu.VMEM((bsz, l, cf.wkv), F32), pltpu.VMEM((bsz, l, cf.wkv), F32)],
        sem=("arbitrary", "arbitrary"), args=args, hosted=hosted)


def _unfold(blk_ref, slab_ref, d):
    if d == 1:
        return blk_ref[...]
    n = blk_ref.shape[0]
    for r in range(d):
        for half in range(2):
            c0 = 256 * r + 128 * half
            slab_ref[half, pl.ds(r, n, stride=d), :] = blk_ref[:, c0:c0 + 128]
    return jnp.concatenate([slab_ref[0], slab_ref[1]], axis=1)


def _fold(x, slab_ref, out_ref, d):
    if d == 1:
        out_ref[...] = x
        return
    n = out_ref.shape[0]
    slab_ref[0] = x[:, :128]
    slab_ref[1] = x[:, 128:]
    for r in range(d):
        for half in range(2):
            c0 = 256 * r + 128 * half
            out_ref[:, c0:c0 + 128] = slab_ref[half, pl.ds(r, n, stride=d), :]


DILATIONS = tuple(d for _, d in DIL_GROUPS)


def _merge_fwd(o_a, lse_a, o_b, zg, h, wa_t, wb_t, wout, name):
    t = h.shape[0]
    tm = 512

    def body(o1, o2, o3, l1, l2, l3, ob_ref, zg_ref, h_ref, wa_ref, wb_ref, wo_ref, h2_ref, ya_ref, lt_ref, *slabs):
        o = [_unfold(ref, slabs[i], d) for i, (ref, d) in enumerate(zip((o1, o2, o3), DILATIONS))]
        l = [_unfold(ref, slabs[3 + i], d) for i, (ref, d) in enumerate(zip((l1, l2, l3), DILATIONS))]
        m = jnp.maximum(jnp.maximum(l[0], l[1]), l[2])
        e1, e2, e3 = jnp.exp(l[0] - m), jnp.exp(l[1] - m), jnp.exp(l[2] - m)
        se = e1 + e2 + e3
        ya = (e1 / se) * o[0] + (e2 / se) * o[1] + (e3 / se) * o[2]
        ya_ref[...] = ya
        lt_ref[...] = m + jnp.log(se)
        pa = _nt(ya.astype(BF16), wa_ref[...])
        pb = _nt(ob_ref[...].astype(BF16), wb_ref[...])
        merged = (jax.nn.sigmoid(zg_ref[:, :D_MODEL].astype(F32)) * pa
                  + jax.nn.sigmoid(zg_ref[:, D_MODEL:].astype(F32)) * pb)
        h2_ref[...] = h_ref[...] + _nn(merged.astype(BF16), wo_ref[...])

    row = lambda w: pl.BlockSpec((tm, w), lambda i: (i, 0))
    folded = [pl.BlockSpec((tm // d, d * 256), lambda i: (i, 0)) for d in DILATIONS]
    full = _resident
    return pl.pallas_call(
        body, name=name, grid=(t // tm,),
        in_specs=folded + folded + [row(512), row(GATE_W), row(D_MODEL), full(wa_t), full(wb_t), full(wout)],
        out_specs=[row(D_MODEL), row(256), row(256)],
        out_shape=[jax.ShapeDtypeStruct((t, D_MODEL), F32), jax.ShapeDtypeStruct((t, 256), F32),
                   jax.ShapeDtypeStruct((t, 256), F32)],
        scratch_shapes=[pltpu.VMEM((2, tm, 128), F32)] * 6,
        compiler_params=_params("parallel"),
    )(*o_a, *lse_a, o_b, zg, h, wa_t, wb_t, wout)


def _head_stats(lt, dy, y):
    lane = lax.broadcasted_iota(jnp.int32, dy.shape, 1)
    prod = dy * y
    dot = jnp.zeros_like(prod)
    for hd in range(dy.shape[1] // HEAD_DIM):
        mine = lane // HEAD_DIM == hd
        dot = jnp.where(mine, jnp.sum(jnp.where(mine, prod, 0.0), axis=-1, keepdims=True), dot)
    return jnp.where(lane % HEAD_DIM < HEAD_DIM // 2, lt, dot)


def _merge_bwd(dh, ya, lse_tot, o_b, lse_b, zg, wa_t, wb_t, wout, name, hosted=()):
    t = dh.shape[0]
    tm = 256

    def body(dh_ref, ya_ref, lt_ref, ob_ref, lb_ref, zg_ref, wa_ref, wb_ref, wo_ref,
             mg_ref, dpa_ref, dpb_ref, dzg_ref, dy1, dy2, dy3, st1, st2, st3, dyb_ref, stb_ref, *slabs):
        dm = _nt(dh_ref[...].astype(BF16), wo_ref[...])
        pa = _nt(ya_ref[...].astype(BF16), wa_ref[...])
        pb = _nt(ob_ref[...].astype(BF16), wb_ref[...])
        sa = jax.nn.sigmoid(zg_ref[:, :D_MODEL].astype(F32))
        sb = jax.nn.sigmoid(zg_ref[:, D_MODEL:].astype(F32))
        mg_ref[...] = (sa * pa + sb * pb).astype(BF16)
        dpa = (dm * sa).astype(BF16)
        dpb = (dm * sb).astype(BF16)
        dpa_ref[...] = dpa
        dpb_ref[...] = dpb
        dzg_ref[:, :D_MODEL] = (dm * pa * (sa * (1.0 - sa))).astype(BF16)
        dzg_ref[:, D_MODEL:] = (dm * pb * (sb * (1.0 - sb))).astype(BF16)
        dya = _nn(dpa, wa_ref[...])
        dyb = _nn(dpb, wb_ref[...])
        stats = _head_stats(lt_ref[...], dya, ya_ref[...])
        for i, d in enumerate(DILATIONS):
            _fold(dya, slabs[i], (dy1, dy2, dy3)[i], d)
            _fold(stats, slabs[3 + i], (st1, st2, st3)[i], d)
        dyb_ref[...] = dyb
        stb_ref[...] = _head_stats(lb_ref[...], dyb, ob_ref[...])

    row = lambda w: pl.BlockSpec((tm, w), lambda i: (i, 0))
    folded = [pl.BlockSpec((tm // d, d * 256), lambda i: (i, 0)) for d in DILATIONS]
    full = _resident
    sds = jax.ShapeDtypeStruct
    fshape = [sds((t // d, d * 256), F32) for d in DILATIONS]
    return _call(
        body, name=name, grid=(t // tm,),
        in_specs=[row(D_MODEL), row(256), row(256), row(512), row(512), row(GATE_W), full(wa_t), full(wb_t),
                  full(wout)],
        out_specs=[row(D_MODEL), row(D_MODEL), row(D_MODEL), row(GATE_W)] + folded + folded + [row(512), row(512)],
        out_shape=[sds((t, D_MODEL), BF16), sds((t, D_MODEL), BF16), sds((t, D_MODEL), BF16),
                   sds((t, GATE_W), BF16)] + fshape + fshape + [sds((t, 512), F32), sds((t, 512), F32)],
        scratch=[pltpu.VMEM((2, tm, 128), F32)] * 6,
        sem=("parallel",), args=(dh, ya, lse_tot, o_b, lse_b, zg, wa_t, wb_t, wout), hosted=hosted)


def _loss_head(h, g, target, name):
    t = h.shape[0]
    tm = 1024

    def body(h_ref, g_ref, t_ref, dh_ref, loss_ref, dg_ref):
        @pl.when(pl.program_id(0) == 0)
        def _():
            loss_ref[...] = jnp.zeros_like(loss_ref)
            dg_ref[...] = jnp.zeros_like(dg_ref)

        xhat, r = _rms(h_ref[...], g_ref[...])
        err = xhat * g_ref[...] - t_ref[...]
        loss_ref[...] += 0.5 * jnp.sum(jnp.mean(err * err, axis=-1, keepdims=True), axis=0, keepdims=True)
        dx, dg = _rms_bwd(err * (1.0 / D_MODEL), xhat, r, g_ref[...])
        dh_ref[...] = dx
        dg_ref[...] += dg

    row = pl.BlockSpec((tm, D_MODEL), lambda i: (i, 0))
    vec = pl.BlockSpec((1, D_MODEL), lambda i: (0, 0))
    return pl.pallas_call(
        body, name=name, grid=(t // tm,),
        in_specs=[row, vec, row], out_specs=[row, pl.BlockSpec((1, 128), lambda i: (0, 0)), vec],
        out_shape=[jax.ShapeDtypeStruct((t, D_MODEL), F32), jax.ShapeDtypeStruct((1, 128), F32),
                   jax.ShapeDtypeStruct((1, D_MODEL), F32)],
        compiler_params=_params("arbitrary"),
    )(h, g, target)


def _pair_sum(grad, got, name):
    _, _, r, cdim = grad.shape
    core = lax.axis_index("c").astype(jnp.int32).reshape(1)

    def body(core_ref, g_ref, s_ref, o_ref):
        o_ref[...] = (g_ref[...] + s_ref[...]).astype(BF16)

    return pl.pallas_call(
        body, name=name,
        grid_spec=pltpu.PrefetchScalarGridSpec(
            num_scalar_prefetch=1, grid=(N_CHIP,),
            in_specs=[pl.BlockSpec((None, None, r, cdim), lambda q, core_ref: (q, core_ref[0], 0, 0)),
                      pl.BlockSpec((None, None, r, cdim), lambda q, core_ref: (q, 0, 0, 0))],
            out_specs=pl.BlockSpec((None, r, cdim), lambda q, core_ref: (q, 0, 0))),
        out_shape=jax.ShapeDtypeStruct((N_CHIP, r, cdim), BF16),
        compiler_params=_params("parallel"),
    )(core, grad, got)


def _adamw_math(w, g, m, v):
    m = ADAM_B1 * m + (1.0 - ADAM_B1) * g
    v = ADAM_B2 * v + (1.0 - ADAM_B2) * jnp.square(g)
    m_hat = m / (1.0 - ADAM_B1 ** ADAM_STEP)
    v_hat = v / (1.0 - ADAM_B2 ** ADAM_STEP)
    delta = -ADAM_LR * (m_hat / (jnp.sqrt(v_hat) + ADAM_EPS) + ADAM_WD * w)
    return delta, m, v


def _update(w, parts, m, v, transposed, name):
    r, c = parts.shape[1:]

    def body(w_ref, p_ref, m_ref, v_ref, g_ref, d_ref, m2_ref, v2_ref):
        def total(rows):
            return ((p_ref[0, rows].astype(F32) + p_ref[1, rows].astype(F32)) + p_ref[2, rows].astype(F32)) \
                + p_ref[3, rows].astype(F32)

        def update(at, g):
            g_ref[at] = g
            d_ref[at], m2_ref[at], v2_ref[at] = _adamw_math(w_ref[at], g, m_ref[at], v_ref[at])

        if not transposed:
            update((slice(None), slice(None)), total(slice(None)))
            return
        for r0 in range(0, r, 128):
            n = min(128, r - r0)
            gt = total(slice(r0, r0 + n))
            if n < 128:
                gt = jnp.concatenate([gt, jnp.zeros((128 - n, c), F32)], axis=0)
            update((slice(None), slice(r0, r0 + n)), gt.T[:, :n])

    sds = jax.ShapeDtypeStruct(w.shape, F32)
    return pl.pallas_call(body, name=name, out_shape=[sds] * 4,
                          compiler_params=pltpu.CompilerParams(vmem_limit_bytes=VMEM_LIMIT))(w, parts, m, v)


SMALL_ROWS = 80


def _small_update(g, w, m, v, name):
    def body(g_ref, w_ref, m_ref, v_ref, gs_ref, d_ref, m2_ref, v2_ref, got_ref, send_sems, recv_sems):
        x, y, c = _place()
        me = 4 * x + 2 * y + c
        got_ref[me] = g_ref[...]
        copies = []
        for k in range(1, N_DEV):
            peer = (x ^ (k >> 2), y ^ ((k >> 1) & 1), c ^ (k & 1))
            cp = pltpu.make_async_remote_copy(
                src_ref=g_ref, dst_ref=got_ref.at[me], send_sem=send_sems.at[k - 1], recv_sem=recv_sems.at[k - 1],
                device_id=peer, device_id_type=MESH)
            cp.start()
            copies.append(cp)
        for cp in copies:
            cp.wait()
        total = got_ref[0]
        for k in range(1, N_DEV):
            total = total + got_ref[k]
        gs_ref[...] = total
        d_ref[...], m2_ref[...], v2_ref[...] = _adamw_math(w_ref[...], total, m_ref[...], v_ref[...])

    sds = jax.ShapeDtypeStruct((SMALL_ROWS, 128), F32)
    vm = pl.BlockSpec(memory_space=pltpu.VMEM)
    return pl.pallas_call(
        body, name=name, in_specs=[vm] * 4, out_specs=[vm] * 4, out_shape=[sds] * 4,
        scratch_shapes=[pltpu.VMEM((N_DEV, SMALL_ROWS, 128), F32), pltpu.SemaphoreType.DMA((N_DEV - 1,)),
                        pltpu.SemaphoreType.DMA((N_DEV - 1,))],
    )(g, w, m, v)


def _pack_small(gains, b_in, rel_bias, sinks, last):
    rows = [a.reshape(8, 128) for a in gains] + [b_in.reshape(40, 128), rel_bias.reshape(5, 128),
                                                 jnp.pad(sinks.reshape(1, 8), ((0, 0), (0, 120))), last]
    rows.append(jnp.zeros((SMALL_ROWS - 79, 128), F32))
    return jnp.concatenate(rows, axis=0)


def _unpack_small(p, like):
    out = [p[8 * i:8 * i + 8].reshape(like[i].shape) for i in range(4)]
    out.append(p[32:72].reshape(like[4].shape))
    out.append(p[72:77].reshape(like[5].shape))
    out.append(p[77, :8].reshape(like[6].shape))
    return out


def kernel(x, ffn1_norm, ffn1_w_gate, ffn1_w_up, ffn1_w_down, mix_norm, w_in, b_in, w_branch_a, w_branch_b, w_out, sinks, rel_bias, ffn2_norm, ffn2_w_gate, ffn2_w_up, ffn2_w_down, final_norm, loss_target, m_ffn1_norm, m_ffn1_w_gate, m_ffn1_w_up, m_ffn1_w_down, m_mix_norm, m_w_in, m_b_in, m_w_branch_a, m_w_branch_b, m_w_out, m_sinks, m_rel_bias, m_ffn2_norm, m_ffn2_w_gate, m_ffn2_w_up, m_ffn2_w_down, m_final_norm, v_ffn1_norm, v_ffn1_w_gate, v_ffn1_w_up, v_ffn1_w_down, v_mix_norm, v_w_in, v_b_in, v_w_branch_a, v_w_branch_b, v_w_out, v_sinks, v_rel_bias, v_ffn2_norm, v_ffn2_w_gate, v_ffn2_w_up, v_ffn2_w_down, v_final_norm):
    bsz, seq, _ = x.shape
    t = bsz * seq
    xt = x.reshape(t, D_MODEL)
    target = loss_target.reshape(t, D_MODEL)

    big = [("ffn1_w_gate", ffn1_w_gate, m_ffn1_w_gate, v_ffn1_w_gate, True),
           ("ffn1_w_up", ffn1_w_up, m_ffn1_w_up, v_ffn1_w_up, True),
           ("ffn1_w_down", ffn1_w_down, m_ffn1_w_down, v_ffn1_w_down, False),
           ("w_in", w_in, m_w_in, v_w_in, True),
           ("w_branch_a", w_branch_a, m_w_branch_a, v_w_branch_a, True),
           ("w_branch_b", w_branch_b, m_w_branch_b, v_w_branch_b, True),
           ("w_out", w_out, m_w_out, v_w_out, False),
           ("ffn2_w_gate", ffn2_w_gate, m_ffn2_w_gate, v_ffn2_w_gate, True),
           ("ffn2_w_up", ffn2_w_up, m_ffn2_w_up, v_ffn2_w_up, True),
           ("ffn2_w_down", ffn2_w_down, m_ffn2_w_down, v_ffn2_w_down, False)]
    shard = {nm: (w[0].T if tr else w[0]).astype(BF16) for nm, w, _, _, tr in big}
    full = {}

    def gather(names):
        return _Gather([shard[nm] for nm in names])

    def keep(names, got):
        for nm, gw in zip(names, got):
            full[nm] = gw.reshape(-1, gw.shape[-1])

    ffn1_names = ["ffn1_w_gate", "ffn1_w_up", "ffn1_w_down"]
    mix_names = ["w_in", "w_branch_a", "w_branch_b", "w_out"]
    ffn2_names = ["ffn2_w_gate", "ffn2_w_up", "ffn2_w_down"]
    g1, gm, g2, gf = ffn1_norm, mix_norm, ffn2_norm, final_norm.reshape(1, D_MODEL)

    keep(ffn1_names, _exchange([gather(ffn1_names)], "gather_ffn1")[0])
    (h1, n1, a1, b1, hff1), (got,) = _ffn_fwd(xt, g1, full["ffn1_w_gate"], full["ffn1_w_up"], full["ffn1_w_down"], "ffn1_fwd",
                                hosted=[gather(mix_names)])
    keep(mix_names, got)
    (u, zq, zg), (got,) = _inproj_fwd(h1, gm, full["w_in"], b_in, "inproj_fwd", hosted=[gather(ffn2_names[:2])])
    keep(ffn2_names[:2], got)
    buckets = _bucket_tiles()
    bias = _bias_build(rel_bias, buckets)
    sink_rows = jnp.broadcast_to(sinks.reshape(8, 1, 1), (8, BLOCK, 128)).reshape(8 * BLOCK, 128)
    cfs = [_Att(i) for i in range(4)]
    zfold = []
    for i, cf in enumerate(cfs):
        if cf.d == 1:
            zfold.append(zq.reshape(bsz, seq, QKV_W))
        else:
            own = jnp.concatenate([zq[:, 768 * part + 256 * i:768 * part + 256 * (i + 1)] for part in range(3)], axis=1)
            zfold.append(own.reshape(bsz, seq // cf.d, cf.d * cf.row_w))
    att = [None] * 4
    for i in (3, 0, 1, 2):
        cf = cfs[i]
        (o, lse), got = _attn_fwd(cf, zfold[i], bias, sink_rows, f"attn{i}_fwd",
                                  hosted=[gather(ffn2_names[2:])] if i == 3 else ())
        if i == 3:
            keep(ffn2_names[2:], got[0])
        att[i] = (o.reshape(t // cf.d, cf.d * cf.wq), lse.reshape(t // cf.d, cf.d * cf.wq))
    o_b, lse_b = att[3]
    h2, ya, lse_tot = _merge_fwd([a[0] for a in att[:3]], [a[1] for a in att[:3]], o_b, zg, h1,
                                 full["w_branch_a"], full["w_branch_b"], full["w_out"], "merge_fwd")
    (h3, n2, a2, b2, hff2), _ = _ffn_fwd(h2, g2, full["ffn2_w_gate"], full["ffn2_w_up"], full["ffn2_w_down"], "ffn2_fwd")
    dh3, loss_part, dgf = _loss_head(h3, gf, target, "loss_head")

    grads, pair, from_chips = {}, {}, {}

    def by_owner(nm):
        return grads[nm].reshape(N_CHIP, 2, -1, grads[nm].shape[-1])

    def to_core(names):
        return _CoreExchange([by_owner(nm) for nm in names])

    def pair_up(names, got):
        for nm, sib in zip(names, got):
            pair[nm] = _pair_sum(by_owner(nm), sib, f"pair_sum_{nm}")

    def to_chips(names):
        return _ChipExchange([pair[nm] for nm in names])

    def landed(names, got):
        for nm, parts in zip(names, got):
            from_chips[nm] = parts

    dh2, dg2, da, db = _ffn_bwd(h2, a2, b2, g2, dh3, full["ffn2_w_gate"], full["ffn2_w_up"],
                                          full["ffn2_w_down"], "ffn2_bwd")
    grads["ffn2_w_gate"], _ = _tn_matmul(da, n2, 1408, "ffn2_dgate")
    grads["ffn2_w_up"], _ = _tn_matmul(db, n2, 1408, "ffn2_dup")
    grads["ffn2_w_down"], _ = _tn_matmul(hff2, dh3, 1408, "ffn2_ddown", scale=0.5)
    (merged, dpa, dpb, dzg, *cot), (got,) = _merge_bwd(
        dh2, ya, lse_tot, o_b, lse_b, zg, full["w_branch_a"], full["w_branch_b"], full["w_out"], "merge_bwd",
        hosted=[to_core(ffn2_names)])
    dys, sts = cot[0:3] + [cot[6]], cot[3:6] + [cot[7]]
    pair_up(ffn2_names, got)
    dq, dk, dv, dbias, dsink = [None] * 4, [None] * 4, [None] * 4, [None] * 4, None
    for i in (3, 0, 1, 2):
        cf = cfs[i]
        shp = (bsz, seq // cf.d, cf.d * cf.wq)
        hosted = {3: lambda: [to_chips(ffn2_names[:2])], 0: lambda: [to_chips(ffn2_names[2:])]}.get(i, list)()
        res, got = _attn_bwd(cf, zfold[i], bias, sink_rows, sts[i].reshape(shp), dys[i].reshape(shp),
                             f"attn{i}_bwd", hosted=hosted)
        if i == 3:
            landed(ffn2_names[:2], got[0])
        elif i == 0:
            landed(ffn2_names[2:], got[0])
        dq[i] = res[0].reshape(t, cf.wq)
        dk[i] = res[1].reshape(t, cf.wkv)
        dv[i] = res[2].reshape(t, cf.wkv)
        dbias[i] = res[3]
        if cf.sinks:
            dsink = res[4]
    pieces = dq[:3] + dk[:3] + dv[:3] + [dq[3], dk[3], dv[3], dzg]
    dh1, dz, db_in, dgm = _inproj_bwd(pieces, h1, gm, dh2, full["w_in"], "inproj_bwd")
    dx, dg1, da, db = _ffn_bwd(xt, a1, b1, g1, dh1, full["ffn1_w_gate"], full["ffn1_w_up"],
                                         full["ffn1_w_down"], "ffn1_bwd")
    grads["w_in"], _ = _tn_matmul(dz, u, 1280, "dw_in")
    grads["ffn1_w_down"], (got,) = _tn_matmul(hff1, dh1, 1408, "ffn1_ddown", hosted=[to_core(["w_in"])], scale=0.5)
    pair_up(["w_in"], got)
    grads["ffn1_w_gate"], got = _tn_matmul(da, n1, 1408, "ffn1_dgate",
                                           hosted=[to_chips(["w_in"]), to_core(["ffn1_w_down"])])
    landed(["w_in"], got[0])
    pair_up(["ffn1_w_down"], got[1])
    grads["ffn1_w_up"], got = _tn_matmul(db, n1, 1408, "ffn1_dup",
                                         hosted=[to_chips(["ffn1_w_down"]), to_core(["ffn1_w_gate"])])
    landed(["ffn1_w_down"], got[0])
    pair_up(["ffn1_w_gate"], got[1])
    grads["w_out"], got = _tn_matmul(merged, dh2, 1024, "dw_out",
                                     hosted=[to_chips(["ffn1_w_gate"]), to_core(["ffn1_w_up"])])
    landed(["ffn1_w_gate"], got[0])
    pair_up(["ffn1_w_up"], got[1])
    grads["w_branch_b"], got = _tn_matmul(dpb, o_b, 1024, "dw_branch_b",
                                          hosted=[to_chips(["ffn1_w_up"]), to_core(["w_out"])])
    landed(["ffn1_w_up"], got[0])
    pair_up(["w_out"], got[1])
    grads["w_branch_a"], got = _tn_matmul(dpa, ya, 1024, "dw_branch_a",
                                          hosted=[to_chips(["w_out"]), to_core(["w_branch_b"])])
    landed(["w_out"], got[0])
    pair_up(["w_branch_b"], got[1])
    got = _exchange([to_chips(["w_branch_b"]), to_core(["w_branch_a"])], "reduce_scatter_tail1")
    landed(["w_branch_b"], got[0])
    pair_up(["w_branch_a"], got[1])
    landed(["w_branch_a"], _exchange([to_chips(["w_branch_a"])], "reduce_scatter_tail2")[0])
    dtable, dsinks = _bias_reduce(jnp.concatenate(dbias, axis=0), buckets, dsink)

    out_g, out_d, out_m, out_v = {}, {}, {}, {}
    for nm, w, m, v, tr in big:
        g, d2, m2, v2 = _update(w[0], from_chips[nm], m[0], v[0], tr, f"update_{nm}")
        out_g[nm], out_d[nm], out_m[nm], out_v[nm] = g[None], d2[None], m2[None], v2[None]

    small = [("ffn1_norm", ffn1_norm, m_ffn1_norm, v_ffn1_norm), ("mix_norm", mix_norm, m_mix_norm, v_mix_norm),
             ("ffn2_norm", ffn2_norm, m_ffn2_norm, v_ffn2_norm), ("final_norm", final_norm, m_final_norm, v_final_norm),
             ("b_in", b_in, m_b_in, v_b_in), ("rel_bias", rel_bias, m_rel_bias, v_rel_bias),
             ("sinks", sinks, m_sinks, v_sinks)]
    zero_row = jnp.zeros((1, 128), F32)
    pack = lambda arrs, last: _pack_small(arrs[:4], arrs[4], arrs[5], arrs[6], last)
    g_small = pack([dg1, dgm, dg2, dgf, db_in, dtable[:, :TOTAL_HEADS], dsinks[:, 0]], loss_part)
    packed = [pack([s[k] for s in small], zero_row) for k in (1, 2, 3)]
    gs, ds, ms, vs = _small_update(g_small, *packed, "small_update")
    like = [s[1] for s in small]
    for nm_s, g_, d_, m_, v_ in zip([s[0] for s in small], _unpack_small(gs, like), _unpack_small(ds, like),
                                    _unpack_small(ms, like), _unpack_small(vs, like)):
        out_g[nm_s], out_d[nm_s], out_m[nm_s], out_v[nm_s] = g_, d_, m_, v_
    loss = gs[78, 0]

    order = ["ffn1_norm", "ffn1_w_gate", "ffn1_w_up", "ffn1_w_down", "mix_norm", "w_in", "b_in", "w_branch_a",
             "w_branch_b", "w_out", "sinks", "rel_bias", "ffn2_norm", "ffn2_w_gate", "ffn2_w_up", "ffn2_w_down",
             "final_norm"]
    return (loss, dx.reshape(x.shape), *[out_g[k] for k in order], *[out_d[k] for k in order],
            *[out_m[k] for k in order], *[out_v[k] for k in order])
```

```python
import functools
import math

import numpy as np
import jax
import jax.numpy as jnp
from jax import lax
from jax.experimental import pallas as pl
from jax.experimental.pallas import tpu as pltpu

D_MODEL = 1024
D_FF = 2816
FF_CHUNK = 256
HEAD_DIM = 64
BLOCK = 128
N_BUCKETS = 32
MAX_DISTANCE = 2048
A_HEADS = 12
TOTAL_HEADS = 20
DIL_GROUPS = ((128, 1), (512, 4), (2048, 16))
B_WINDOW = 128
QKV_W = 3072
GATE_W = 2048
D_IN = QKV_W + GATE_W
EPS = 1e-6
NEG = -1e30
N_DEV = 8
N_CHIP = 4
ADAM_LR, ADAM_B1, ADAM_B2, ADAM_EPS, ADAM_WD, ADAM_STEP = 0.001, 0.9, 0.999, 1e-08, 0.01, 10
VMEM_LIMIT = 56 * 1024 * 1024
MESH = pl.DeviceIdType.MESH
BF16 = jnp.bfloat16
F32 = jnp.float32
ANY = pl.BlockSpec(memory_space=pl.ANY)


def _params(*sem):
    return pltpu.CompilerParams(dimension_semantics=sem, vmem_limit_bytes=VMEM_LIMIT)


def _resident(a):
    return pl.BlockSpec(a.shape, lambda i: (0, 0), pipeline_mode=pl.Buffered(1))


def _place():
    return lax.axis_index("x"), lax.axis_index("y"), lax.axis_index("c")


class _Gather:
    def __init__(self, shards):
        self.ins = list(shards)
        n = self.n = len(shards)
        self.out_shape = [jax.ShapeDtypeStruct((N_DEV,) + s.shape, s.dtype) for s in shards]
        self.scratch = [pltpu.SemaphoreType.DMA((7 * n,)), pltpu.SemaphoreType.DMA((7 * n,)),
                        pltpu.SemaphoreType.DMA((n,))]

    def _copies(self, ins, outs, sems):
        send_sems, recv_sems, local_sems = sems
        x, y, c = _place()
        me, sibling = (x, y, c), (x, y, 1 - c)
        chips = [(1 - x, y), (x, 1 - y), (1 - x, 1 - y)]

        def copy(i, k, block, to, src=None):
            dst = outs[i].at[4 * block[0] + 2 * block[1] + block[2]]
            return pltpu.make_async_remote_copy(
                src_ref=dst if src is None else src, dst_ref=dst, send_sem=send_sems.at[7 * i + k],
                recv_sem=recv_sems.at[7 * i + k], device_id=to, device_id_type=MESH)

        n = self.n
        mine = [pltpu.make_async_copy(ins[i], outs[i].at[4 * x + 2 * y + c], local_sems.at[i]) for i in range(n)]
        first = [copy(i, 0, me, sibling, src=ins[i]) for i in range(n)]
        first += [copy(i, 1 + j, me, (*chip, c), src=ins[i]) for i in range(n) for j, chip in enumerate(chips)]
        landed = [copy(i, 1 + j, (*chip, c), me) for j, chip in enumerate(chips) for i in range(n)]
        passed = [copy(i, 4 + j, (*chip, c), sibling) for j, chip in enumerate(chips) for i in range(n)]
        from_sibling = [copy(i, 0, sibling, me) for i in range(n)]
        from_sibling += [copy(i, 4 + j, (*chip, 1 - c), me) for i in range(n) for j, chip in enumerate(chips)]
        return mine, first, landed, passed, from_sibling

    def start(self, ins, outs, sems):
        mine, first, _, _, _ = self._copies(ins, outs, sems)
        for cp in mine + first:
            cp.start()

    def mid(self, ins, outs, sems):
        _, _, landed, passed, _ = self._copies(ins, outs, sems)
        for got, fwd in zip(landed, passed):
            got.wait_recv()
            fwd.start()

    def end(self, ins, outs, sems):
        mine, first, _, passed, from_sibling = self._copies(ins, outs, sems)
        for cp in from_sibling:
            cp.wait_recv()
        for cp in first + passed:
            cp.wait_send()
        for cp in mine:
            cp.wait()


class _CoreExchange:
    def __init__(self, grads):
        self.ins = list(grads)
        n = self.n = len(grads)
        self.out_shape = [jax.ShapeDtypeStruct((N_CHIP, 1) + g.shape[2:], g.dtype) for g in grads]
        self.scratch = [pltpu.SemaphoreType.DMA((n,)), pltpu.SemaphoreType.DMA((n,))]

    def _copies(self, ins, outs, sems):
        x, y, c = _place()
        return [pltpu.make_async_remote_copy(
            src_ref=ins[i].at[:, pl.ds(1 - c, 1)], dst_ref=outs[i], send_sem=sems[0].at[i],
            recv_sem=sems[1].at[i], device_id=(x, y, 1 - c), device_id_type=MESH) for i in range(self.n)]

    def start(self, ins, outs, sems):
        for cp in self._copies(ins, outs, sems):
            cp.start()

    mid = None

    def end(self, ins, outs, sems):
        for cp in self._copies(ins, outs, sems):
            cp.wait()


class _ChipExchange:
    def __init__(self, parts):
        self.ins = list(parts)
        n = self.n = len(parts)
        self.out_shape = [jax.ShapeDtypeStruct(p.shape, p.dtype) for p in parts]
        self.scratch = [pltpu.SemaphoreType.DMA((3 * n,)), pltpu.SemaphoreType.DMA((3 * n,)),
                        pltpu.SemaphoreType.DMA((n,))]

    def _copies(self, ins, outs, sems):
        send_sems, recv_sems, local_sems = sems
        x, y, c = _place()
        my_chip = 2 * x + y
        copies = []
        for i in range(self.n):
            copies.append(pltpu.make_async_copy(ins[i].at[my_chip], outs[i].at[my_chip], local_sems.at[i]))
            for k, (qx, qy) in enumerate([(1 - x, y), (x, 1 - y), (1 - x, 1 - y)]):
                copies.append(pltpu.make_async_remote_copy(
                    src_ref=ins[i].at[2 * qx + qy], dst_ref=outs[i].at[my_chip], send_sem=send_sems.at[3 * i + k],
                    recv_sem=recv_sems.at[3 * i + k], device_id=(qx, qy, c), device_id_type=MESH))
        return copies

    def start(self, ins, outs, sems):
        for cp in self._copies(ins, outs, sems):
            cp.start()

    mid = None

    def end(self, ins, outs, sems):
        for cp in self._copies(ins, outs, sems):
            cp.wait()


def _call(body, *, name, grid, in_specs, out_specs, out_shape, args, scratch=(), sem=None, hosted=()):
    n_in, n_out, n_scr = len(in_specs), len(out_specs), len(scratch)
    x_in = [len(p.ins) for p in hosted]
    x_scr = [len(p.scratch) for p in hosted]
    steps = int(np.prod(grid))

    def wrapped(*refs):
        refs = list(refs)
        ins, refs = refs[:n_in], refs[n_in:]
        x_ins = [[refs.pop(0) for _ in range(k)] for k in x_in]
        outs, refs = refs[:n_out], refs[n_out:]
        x_outs = [[refs.pop(0) for _ in range(k)] for k in x_in]
        scr, refs = refs[:n_scr], refs[n_scr:]
        x_sems = [[refs.pop(0) for _ in range(k)] for k in x_scr]
        step = 0
        for d in range(len(grid)):
            step = step * grid[d] + pl.program_id(d)

        def phase(which, at):
            fns = [(getattr(p, which), a) for p, a in zip(hosted, zip(x_ins, x_outs, x_sems)) if getattr(p, which)]
            if fns:
                @pl.when(step == at)
                def _():
                    for fn, a in fns:
                        fn(*a)

        phase("start", 0)
        phase("mid", (3 * steps) // 4)
        body(*ins, *outs, *scr)
        phase("end", steps - 1)

    results = pl.pallas_call(
        wrapped, name=name, grid=grid,
        in_specs=list(in_specs) + [ANY] * sum(x_in), out_specs=list(out_specs) + [ANY] * sum(x_in),
        out_shape=list(out_shape) + [s for p in hosted for s in p.out_shape],
        scratch_shapes=list(scratch) + [s for p in hosted for s in p.scratch],
        compiler_params=_params(*(("arbitrary",) * len(grid) if hosted else sem)),
    )(*args, *[a for p in hosted for a in p.ins])
    own, rest = list(results[:n_out]), list(results[n_out:])
    return own, [[rest.pop(0) for _ in range(k)] for k in x_in]


def _exchange(programs, name):
    x_in = [len(p.ins) for p in programs]
    x_scr = [len(p.scratch) for p in programs]

    def body(*refs):
        refs = list(refs)
        x_ins = [[refs.pop(0) for _ in range(k)] for k in x_in]
        x_outs = [[refs.pop(0) for _ in range(k)] for k in x_in]
        x_sems = [[refs.pop(0) for _ in range(k)] for k in x_scr]
        for which in ("start", "mid", "end"):
            for p, a in zip(programs, zip(x_ins, x_outs, x_sems)):
                if getattr(p, which):
                    getattr(p, which)(*a)

    results = list(pl.pallas_call(
        body, name=name, in_specs=[ANY] * sum(x_in), out_specs=[ANY] * sum(x_in),
        out_shape=[s for p in programs for s in p.out_shape],
        scratch_shapes=[s for p in programs for s in p.scratch],
    )(*[a for p in programs for a in p.ins]))
    return [[results.pop(0) for _ in range(k)] for k in x_in]


def _nt(a, b):
    return lax.dot_general(a, b, (((1,), (1,)), ((), ())), preferred_element_type=F32)


def _nn(a, b):
    return lax.dot_general(a, b, (((1,), (0,)), ((), ())), preferred_element_type=F32)


def _tn(a, b):
    return lax.dot_general(a, b, (((0,), (0,)), ((), ())), preferred_element_type=F32)


def _rms(x, g):
    r = lax.rsqrt(jnp.mean(x * x, axis=-1, keepdims=True) + EPS)
    return x * r, r


def _rms_bwd(dn, xhat, r, g):
    dg = jnp.sum(dn * xhat, axis=0, keepdims=True)
    dxh = dn * g
    dx = r * (dxh - xhat * jnp.mean(dxh * xhat, axis=-1, keepdims=True))
    return dx, dg


def _ffn_fwd(x, g, wg_t, wu_t, wd, name, hosted=()):
    t = x.shape[0]
    tm = 256

    def body(x_ref, g_ref, wg_ref, wu_ref, wd_ref, h_ref, n_ref, a_ref, b_ref, hff_ref):
        xhat, _ = _rms(x_ref[...], g_ref[...])
        n = (xhat * g_ref[...]).astype(BF16)
        n_ref[...] = n
        for c in range(0, D_FF, FF_CHUNK):
            cols = slice(c, c + FF_CHUNK)
            a = _nt(n, wg_ref[cols, :])
            b = _nt(n, wu_ref[cols, :])
            a_ref[:, cols] = a.astype(BF16)
            b_ref[:, cols] = b.astype(BF16)
            hff_ref[:, cols] = (a * jax.nn.sigmoid(a) * b).astype(BF16)
        h_ref[...] = x_ref[...] + 0.5 * _nn(hff_ref[...], wd_ref[...])

    row = pl.BlockSpec((tm, D_MODEL), lambda i: (i, 0))
    hid = pl.BlockSpec((tm, D_FF), lambda i: (i, 0))
    return _call(
        body, name=name, grid=(t // tm,),
        in_specs=[row, _resident(g), _resident(wg_t), _resident(wu_t), _resident(wd)],
        out_specs=[row, row, hid, hid, hid],
        out_shape=[jax.ShapeDtypeStruct((t, D_MODEL), F32), jax.ShapeDtypeStruct((t, D_MODEL), BF16),
                   jax.ShapeDtypeStruct((t, D_FF), BF16), jax.ShapeDtypeStruct((t, D_FF), BF16),
                   jax.ShapeDtypeStruct((t, D_FF), BF16)],
        sem=("parallel",), args=(x, g, wg_t, wu_t, wd), hosted=hosted)


def _ffn_bwd(x, a_pre, b_pre, g, dh, wg_t, wu_t, wd, name):
    t = x.shape[0]
    tm = 256

    def body(x_ref, a_ref, b_ref, g_ref, dh_ref, wg_ref, wu_ref, wd_ref,
             dx_ref, dg_ref, da_ref, db_ref):
        @pl.when(pl.program_id(0) == 0)
        def _():
            dg_ref[...] = jnp.zeros_like(dg_ref)

        dhh = (0.5 * dh_ref[...]).astype(BF16)
        for c in range(0, D_FF, FF_CHUNK):
            cols = slice(c, c + FF_CHUNK)
            a = a_ref[:, cols].astype(F32)
            b = b_ref[:, cols].astype(F32)
            s = jax.nn.sigmoid(a)
            silu = a * s
            dhff = _nt(dhh, wd_ref[cols, :])
            da_ref[:, cols] = (dhff * b * (s * (1.0 + a * (1.0 - s)))).astype(BF16)
            db_ref[:, cols] = (dhff * silu).astype(BF16)
        dn = _nn(da_ref[...], wg_ref[...]) + _nn(db_ref[...], wu_ref[...])
        xhat, r = _rms(x_ref[...], g_ref[...])
        dx, dg = _rms_bwd(dn, xhat, r, g_ref[...])
        dx_ref[...] = dh_ref[...] + dx
        dg_ref[...] += dg

    row = pl.BlockSpec((tm, D_MODEL), lambda i: (i, 0))
    hid = pl.BlockSpec((tm, D_FF), lambda i: (i, 0))
    return pl.pallas_call(
        body, name=name, grid=(t // tm,),
        in_specs=[row, hid, hid, _resident(g), row, _resident(wg_t), _resident(wu_t), _resident(wd)],
        out_specs=[row, pl.BlockSpec((1, D_MODEL), lambda i: (0, 0)), hid, hid],
        out_shape=[jax.ShapeDtypeStruct((t, D_MODEL), F32), jax.ShapeDtypeStruct((1, D_MODEL), F32),
                   jax.ShapeDtypeStruct((t, D_FF), BF16), jax.ShapeDtypeStruct((t, D_FF), BF16)],
        compiler_params=_params("arbitrary"),
    )(x, a_pre, b_pre, g, dh, wg_t, wu_t, wd)


def _tn_matmul(a, b, rc, name, hosted=(), scale=None):
    t, r = a.shape
    c = b.shape[1]
    tk = 1024

    def body(a_ref, b_ref, o_ref):
        @pl.when(pl.program_id(1) == 0)
        def _():
            o_ref[...] = jnp.zeros_like(o_ref)

        o_ref[...] += _tn(a_ref[...].astype(BF16), b_ref[...].astype(BF16))
        if scale is not None:
            @pl.when(pl.program_id(1) == t // tk - 1)
            def _():
                o_ref[...] *= scale

    (out,), got = _call(
        body, name=name, grid=(r // rc, t // tk),
        in_specs=[pl.BlockSpec((tk, rc), lambda i, k: (k, i)), pl.BlockSpec((tk, c), lambda i, k: (k, 0))],
        out_specs=[pl.BlockSpec((rc, c), lambda i, k: (i, 0))],
        out_shape=[jax.ShapeDtypeStruct((r, c), F32)],
        sem=("parallel", "arbitrary"), args=(a, b), hosted=hosted)
    return out, got


def _unfold(blk_ref, slab_ref, d):
    if d == 1:
        return blk_ref[...]
    n = blk_ref.shape[0]
    for r in range(d):
        for half in range(2):
            c0 = 256 * r + 128 * half
            slab_ref[half, pl.ds(r, n, stride=d), :] = blk_ref[:, c0:c0 + 128]
    return jnp.concatenate([slab_ref[0], slab_ref[1]], axis=1)


def _fold(x, slab_ref, out_ref, d):
    if d == 1:
        out_ref[...] = x
        return
    n = out_ref.shape[0]
    slab_ref[0] = x[:, :128]
    slab_ref[1] = x[:, 128:]
    for r in range(d):
        for half in range(2):
            c0 = 256 * r + 128 * half
            out_ref[:, c0:c0 + 128] = slab_ref[half, pl.ds(r, n, stride=d), :]


DILATIONS = tuple(d for _, d in DIL_GROUPS)


PIECE_W = (256,) * 9 + (512, 128, 128, GATE_W)
PIECE_D = DILATIONS * 3 + (1, 1, 1, 1)


def _inproj_fwd(h, g, w_t, b_in, name, hosted=()):
    t = h.shape[0]
    tm, nc = 512, 256
    dilated = [(gi, d) for gi, d in enumerate(DILATIONS) if d > 1]

    def body(h_ref, g_ref, w_ref, b_ref, u_ref, zq_ref, zg_ref, *rest):
        zf_refs, slabs = rest[:len(dilated)], rest[len(dilated):]
        xhat, _ = _rms(h_ref[...], g_ref[...])
        u = (xhat * g_ref[...]).astype(BF16)
        u_ref[...] = u
        for c in range(D_IN // nc):
            z = _nt(u, w_ref[c * nc:(c + 1) * nc, :]) + b_ref[:, c * nc:(c + 1) * nc]
            if c < QKV_W // nc:
                zq_ref[:, c * nc:(c + 1) * nc] = z.astype(BF16)
            else:
                zg_ref[:, c * nc - QKV_W:(c + 1) * nc - QKV_W] = z.astype(BF16)
            part, gi = divmod(c, len(DILATIONS))
            for k, (gk, d) in enumerate(dilated):
                if part < 3 and gi == gk:
                    slab, n = slabs[3 * k + part], tm // d
                    slab[0] = z[:, :128]
                    slab[1] = z[:, 128:]
                    for r in range(d):
                        for half in range(2):
                            c0 = 768 * r + 256 * part + 128 * half
                            zf_refs[k][:, c0:c0 + 128] = slab[half, pl.ds(r, n, stride=d), :].astype(BF16)

    row = lambda w: pl.BlockSpec((tm, w), lambda i: (i, 0))
    full = _resident
    sds = jax.ShapeDtypeStruct
    return _call(
        body, name=name, grid=(t // tm,),
        in_specs=[row(D_MODEL), full(g), full(w_t), full(b_in)],
        out_specs=[row(D_MODEL), row(QKV_W), row(GATE_W)]
        + [pl.BlockSpec((tm // d, d * 768), lambda i: (i, 0)) for _, d in dilated],
        out_shape=[sds((t, D_MODEL), BF16), sds((t, QKV_W), BF16), sds((t, GATE_W), BF16)]
        + [sds((t // d, d * 768), BF16) for _, d in dilated],
        scratch=[pltpu.VMEM((2, tm, 128), F32)] * (3 * len(dilated)),
        sem=("parallel",), args=(h, g, w_t, b_in), hosted=hosted)


def _inproj_bwd(pieces, h, g, dh_res, w_t, name):
    t = h.shape[0]
    tm = 256
    npiece = len(PIECE_W)
    offs = np.concatenate([[0], np.cumsum(PIECE_W)]).tolist()

    def body(*refs):
        p_refs = refs[:npiece]
        h_ref, g_ref, dhr_ref, w_ref, dh_ref, dz_ref, db_ref, dg_ref = refs[npiece:npiece + 8]
        slabs = list(refs[npiece + 8:])
        i = pl.program_id(0)

        @pl.when(i == 0)
        def _():
            db_ref[...] = jnp.zeros_like(db_ref)
            dg_ref[...] = jnp.zeros_like(dg_ref)

        du = jnp.zeros((tm, D_MODEL), F32)
        for k in range(npiece):
            o, w = offs[k], PIECE_W[k]
            token_order = _unfold(p_refs[k], slabs.pop(), PIECE_D[k]).astype(BF16) if PIECE_D[k] > 1 else None
            for c0 in range(0, w, 512):
                cw = min(512, w - c0)
                pz = p_refs[k][:, c0:c0 + cw] if token_order is None else token_order
                dz_ref[:, o + c0:o + c0 + cw] = pz
                db_ref[:, o + c0:o + c0 + cw] += jnp.sum(pz.astype(F32), axis=0, keepdims=True)
                du = du + _nn(pz, w_ref[o + c0:o + c0 + cw, :])
        xhat, r = _rms(h_ref[...], g_ref[...])
        dx, dg = _rms_bwd(du, xhat, r, g_ref[...])
        dh_ref[...] = dhr_ref[...] + dx
        dg_ref[...] += dg

    row = lambda w: pl.BlockSpec((tm, w), lambda i: (i, 0))
    full = lambda shp: pl.BlockSpec(shp, lambda i: (0, 0))
    return pl.pallas_call(
        body, name=name, grid=(t // tm,),
        in_specs=[pl.BlockSpec((tm // d, d * w), lambda i: (i, 0)) for w, d in zip(PIECE_W, PIECE_D)]
        + [row(D_MODEL), _resident(g), row(D_MODEL), _resident(w_t)],
        out_specs=[row(D_MODEL), row(D_IN), full((1, D_IN)), full((1, D_MODEL))],
        out_shape=[jax.ShapeDtypeStruct((t, D_MODEL), F32), jax.ShapeDtypeStruct((t, D_IN), BF16),
                   jax.ShapeDtypeStruct((1, D_IN), F32), jax.ShapeDtypeStruct((1, D_MODEL), F32)],
        scratch_shapes=[pltpu.VMEM((2, tm, 128), F32)] * sum(d > 1 for d in PIECE_D),
        compiler_params=_params("arbitrary"),
    )(*pieces, h, g, dh_res, w_t)


def _t5_bucket(dist):
    max_exact = N_BUCKETS // 2
    n = jnp.maximum(dist, 0)
    nf = jnp.maximum(n, 1).astype(jnp.float32)
    large = max_exact + (jnp.log(nf / max_exact) / math.log(MAX_DISTANCE / max_exact)
                         * (N_BUCKETS - max_exact)).astype(jnp.int32)
    large = jnp.minimum(large, N_BUCKETS - 1)
    return jnp.where(n < max_exact, n, large)


ATT_CFG = ((1, 128, 0, 4), (4, 128, 4, 4), (16, 128, 8, 4), (1, B_WINDOW - 1, A_HEADS, 8))


def _bucket_tiles():
    qi = jnp.arange(BLOCK)[:, None]
    ki = jnp.arange(2 * BLOCK)[None, :]
    dist = qi + BLOCK - ki
    return jnp.stack([_t5_bucket(dist * cfg[0]) for cfg in ATT_CFG]).astype(jnp.int32)


def _band(max_steps):
    row = lax.broadcasted_iota(jnp.int32, (BLOCK, 2 * BLOCK), 0)
    col = lax.broadcasted_iota(jnp.int32, (BLOCK, 2 * BLOCK), 1)
    dist = row + BLOCK - col
    return (dist >= 0) & (dist <= max_steps)


def _bias_build(table, buckets):
    def body(tab_ref, bt_ref, out_ref):
        col = lax.broadcasted_iota(jnp.int32, (BLOCK, 2 * BLOCK), 1)
        for ci, (_, max_steps, h0, nh) in enumerate(ATT_CFG):
            bt = bt_ref[ci]
            band = _band(max_steps)
            for h in range(h0, h0 + nh):
                acc = lax.fori_loop(0, N_BUCKETS, lambda b, acc: jnp.where(bt == b, tab_ref[b, h], acc),
                                    jnp.zeros((BLOCK, 2 * BLOCK), F32))
                out_ref[0, BLOCK * h:BLOCK * (h + 1), :] = jnp.where(band & (col >= BLOCK), acc, NEG)
                out_ref[1, BLOCK * h:BLOCK * (h + 1), :] = jnp.where(band, acc, NEG)

    return pl.pallas_call(
        body, name="bias_build",
        in_specs=[pl.BlockSpec(memory_space=pltpu.SMEM), pl.BlockSpec(memory_space=pltpu.VMEM)],
        out_specs=pl.BlockSpec(memory_space=pltpu.VMEM),
        out_shape=jax.ShapeDtypeStruct((2, TOTAL_HEADS * BLOCK, 2 * BLOCK), F32),
    )(table, buckets)


def _bias_reduce(dbias, buckets, dsink_rows):
    def body(db_ref, bt_ref, ds_ref, out_ref, sink_ref):
        ri = lax.broadcasted_iota(jnp.int32, (N_BUCKETS, 128), 0)
        ci = lax.broadcasted_iota(jnp.int32, (N_BUCKETS, 128), 1)

        def per_bucket(b, acc):
            for cfg_i, (_, _, h0, nh) in enumerate(ATT_CFG):
                hit = bt_ref[cfg_i] == b
                for h in range(h0, h0 + nh):
                    val = jnp.sum(jnp.where(hit, db_ref[BLOCK * h:BLOCK * (h + 1), :], 0.0))
                    acc = jnp.where((ri == b) & (ci == h), val, acc)
            return acc

        out_ref[...] = lax.fori_loop(0, N_BUCKETS, per_bucket, jnp.zeros((N_BUCKETS, 128), F32))
        for h in range(8):
            sink_ref[h:h + 1, :] = jnp.sum(ds_ref[BLOCK * h:BLOCK * (h + 1), :], axis=0, keepdims=True)

    return pl.pallas_call(
        body, name="bias_reduce",
        in_specs=[pl.BlockSpec(memory_space=pltpu.VMEM)] * 3,
        out_specs=[pl.BlockSpec(memory_space=pltpu.VMEM)] * 2,
        out_shape=[jax.ShapeDtypeStruct((N_BUCKETS, 128), F32), jax.ShapeDtypeStruct((8, 128), F32)],
    )(dbias, buckets, dsink_rows)


class _Att:
    def __init__(self, cfg_i):
        stride, _, h0, nh = ATT_CFG[cfg_i]
        self.d = stride if cfg_i < 3 else 1
        self.h0, self.nh = h0, nh
        self.row_w = QKV_W if self.d == 1 else 3 * 256
        if cfg_i < 3:
            self.nq, self.wkv = 1, 256
            self.q_unit = [cfg_i if self.d == 1 else 0]
            self.k_unit, self.v_unit = (3 + cfg_i, 6 + cfg_i) if self.d == 1 else (1, 2)
            self.sinks = False
        else:
            self.nq, self.wkv = 2, 128
            self.q_unit = [9, 10]
            self.k_unit, self.v_unit = 22, 23
            self.sinks = True
        self.wq = 256 * self.nq


def _att_in_specs(cf, bsz):
    uq, ukv = cf.row_w // 256, cf.row_w // cf.wkv
    specs = [pl.BlockSpec((bsz, BLOCK, 256), functools.partial(lambda r, j, u: (0, j, r * uq + u), u=u))
             for u in cf.q_unit]
    for unit in (cf.k_unit, cf.v_unit):
        specs.append(pl.BlockSpec((bsz, BLOCK, cf.wkv),
                                  functools.partial(lambda r, j, u: (0, jnp.maximum(j - 1, 0), r * ukv + u), u=unit)))
        specs.append(pl.BlockSpec((bsz, BLOCK, cf.wkv),
                                  functools.partial(lambda r, j, u: (0, j, r * ukv + u), u=unit)))
    for qb in range(cf.nq):
        specs.append(pl.BlockSpec((None, HEADS_PER_BLOCK * BLOCK, 2 * BLOCK), functools.partial(
            lambda r, j, u: (jnp.minimum(j, 1), u, 0), u=cf.h0 // HEADS_PER_BLOCK + qb)))
    if cf.sinks:
        specs += [pl.BlockSpec((HEADS_PER_BLOCK * BLOCK, 128), functools.partial(lambda r, j, u: (u, 0), u=qb))
                  for qb in range(cf.nq)]
    return specs


HEADS_PER_BLOCK = 4


def _head_masks(rows):
    head = lax.broadcasted_iota(jnp.int32, (rows, 256), 1) // HEAD_DIM
    return [head == h for h in range(HEADS_PER_BLOCK)]


def _stack_heads(x, masks):
    return jnp.concatenate([jnp.where(m, x, jnp.zeros_like(x)) for m in masks], axis=0)


def _unstack_heads(x4, masks):
    blocks = [x4[BLOCK * h:BLOCK * (h + 1)] for h in range(HEADS_PER_BLOCK)]
    return jnp.where(masks[0], blocks[0], jnp.where(masks[1], blocks[1], jnp.where(masks[2], blocks[2], blocks[3])))


def _row_value(x):
    return jnp.max(x, axis=-1, keepdims=True)


def _kv_operand(cf, x, qb):
    if cf.wkv == 256:
        return x
    lane = lax.broadcasted_iota(jnp.int32, x.shape, 1)
    swapped = pltpu.roll(x, HEAD_DIM, 1)
    half = jnp.where(lane < HEAD_DIM, x, swapped) if qb == 0 else jnp.where(lane < HEAD_DIM, swapped, x)
    return jnp.concatenate([half, half], axis=1)


def _kv_fold(cf, grads):
    if cf.wkv == 256:
        return grads[0]
    folded = []
    for g in grads:
        x = g[:, :128] + g[:, 128:]
        folded.append(x + pltpu.roll(x, HEAD_DIM, 1))
    lane = lax.broadcasted_iota(jnp.int32, folded[0].shape, 1)
    return jnp.where(lane < HEAD_DIM, folded[0], folded[1])


def _attn_fwd(cf, zf, bias, sinks, name, hosted=()):
    bsz, l, _ = zf.shape
    nb = l // BLOCK

    def body(*refs):
        refs = list(refs)
        q_refs = [refs.pop(0) for _ in range(cf.nq)]
        kp_ref, kc_ref, vp_ref, vc_ref = [refs.pop(0) for _ in range(4)]
        bias_refs = [refs.pop(0) for _ in range(cf.nq)]
        sink_refs = [refs.pop(0) for _ in range(cf.nq)] if cf.sinks else None
        o_ref, lse_ref = refs
        masks, kv_masks = _head_masks(BLOCK), _head_masks(2 * BLOCK)
        sinks4 = [_row_value(ref[...]) for ref in sink_refs] if cf.sinks else None
        for bi in range(bsz):
            k = jnp.concatenate([kp_ref[bi], kc_ref[bi]], axis=0)
            v = jnp.concatenate([vp_ref[bi], vc_ref[bi]], axis=0)
            for qb in range(cf.nq):
                cols = slice(256 * qb, 256 * (qb + 1))
                kb, vb = _kv_operand(cf, k, qb), _kv_operand(cf, v, qb)
                q4 = _stack_heads(q_refs[qb][bi] * (HEAD_DIM ** -0.5), masks)
                s = _nt(q4, kb) + bias_refs[qb][...]
                m = jnp.max(s, axis=-1, keepdims=True)
                if cf.sinks:
                    sk = sinks4[qb]
                    m = jnp.maximum(m, sk)
                p = jnp.exp(s - m)
                den = jnp.sum(p, axis=-1, keepdims=True)
                if cf.sinks:
                    den = den + jnp.exp(sk - m)
                pn = (p * (1.0 / den)).astype(BF16)
                p_lanes = jnp.concatenate([pn[BLOCK * h:BLOCK * (h + 1)] for h in range(HEADS_PER_BLOCK)], axis=1)
                v4 = jnp.concatenate([jnp.where(mk, vb, jnp.zeros_like(vb)) for mk in kv_masks], axis=0)
                o_ref[bi, :, cols] = _nn(p_lanes, v4)
                lse_ref[bi, :, cols] = _unstack_heads(
                    jnp.broadcast_to(m + jnp.log(den), (HEADS_PER_BLOCK * BLOCK, 256)), masks)

    in_specs = _att_in_specs(cf, bsz)
    args = [zf] * (cf.nq + 4) + [bias] * cf.nq + ([sinks] * cf.nq if cf.sinks else [])
    out = pl.BlockSpec((bsz, BLOCK, cf.wq), lambda r, j: (0, j, r))
    shape = jax.ShapeDtypeStruct((bsz, l, cf.d * cf.wq), F32)
    return _call(
        body, name=name, grid=(cf.d, nb), in_specs=in_specs, out_specs=[out, out], out_shape=[shape, shape],
        sem=("parallel", "arbitrary"), args=args, hosted=hosted)


def _attn_bwd(cf, zf, bias, sinks, stats, dy, name, hosted=()):
    bsz, l, _ = zf.shape
    nb = l // BLOCK

    def body(*refs):
        refs = list(refs)
        q_refs = [refs.pop(0) for _ in range(cf.nq)]
        kp_ref, kc_ref, vp_ref, vc_ref = [refs.pop(0) for _ in range(4)]
        bias_refs = [refs.pop(0) for _ in range(cf.nq)]
        sink_refs = [refs.pop(0) for _ in range(cf.nq)] if cf.sinks else None
        st_ref, dy_ref, dq_ref, dk_ref, dv_ref, dbias_ref = [refs.pop(0) for _ in range(6)]
        dsink_ref = refs.pop(0) if cf.sinks else None
        dk_acc, dv_acc = refs
        r, j = pl.program_id(0), pl.program_id(1)

        @pl.when((r == 0) & (j == 0))
        def _():
            dbias_ref[...] = jnp.zeros_like(dbias_ref)
            if cf.sinks:
                dsink_ref[...] = jnp.zeros_like(dsink_ref)

        @pl.when(j == 0)
        def _():
            dk_acc[...] = jnp.zeros_like(dk_acc)
            dv_acc[...] = jnp.zeros_like(dv_acc)

        masks = _head_masks(BLOCK)
        first_half = lax.broadcasted_iota(jnp.int32, (BLOCK, 256), 1) % HEAD_DIM < HEAD_DIM // 2
        cur = pl.ds(pl.multiple_of(j * BLOCK, BLOCK), BLOCK)
        prev = pl.ds(pl.multiple_of(jnp.maximum(j - 1, 0) * BLOCK, BLOCK), BLOCK)
        sinks4 = [_row_value(ref[...]) for ref in sink_refs] if cf.sinks else None
        ds_sum, dsink_sum = [None] * cf.nq, [None] * cf.nq
        for bi in range(bsz):
            k = jnp.concatenate([kp_ref[bi], kc_ref[bi]], axis=0)
            v = jnp.concatenate([vp_ref[bi], vc_ref[bi]], axis=0)
            dk_blocks, dv_blocks = [], []
            for qb in range(cf.nq):
                cols = slice(256 * qb, 256 * (qb + 1))
                kb, vb = _kv_operand(cf, k, qb), _kv_operand(cf, v, qb)
                q4 = _stack_heads(q_refs[qb][bi] * (HEAD_DIM ** -0.5), masks)
                st = st_ref[bi, :, cols]
                lt4 = jnp.concatenate([_row_value(jnp.where(mk & first_half, st, NEG)) for mk in masks], axis=0)
                e4 = jnp.concatenate([_row_value(jnp.where(mk & ~first_half, st, NEG)) for mk in masks], axis=0)
                pa = jnp.exp(_nt(q4, kb) + bias_refs[qb][...] - lt4)
                dy = dy_ref[bi, :, cols]
                dy4 = _stack_heads(dy.astype(BF16), masks)
                ds = pa * (_nt(dy4, vb) - e4)
                ds_sum[qb] = ds if ds_sum[qb] is None else ds_sum[qb] + ds
                if cf.sinks:
                    dsk = jnp.exp(sinks4[qb] - lt4) * e4
                    dsink_sum[qb] = dsk if dsink_sum[qb] is None else dsink_sum[qb] + dsk
                dsb = ds.astype(BF16)
                dq_ref[bi, :, cols] = (_unstack_heads(_nn(dsb, kb), masks) * (HEAD_DIM ** -0.5)).astype(dq_ref.dtype)
                dk_blocks.append(_tn(dsb, q4))
                dv_blocks.append(_tn(pa.astype(BF16), dy4))
            dk_new, dv_new = _kv_fold(cf, dk_blocks), _kv_fold(cf, dv_blocks)
            dk_acc[bi, cur, :] += dk_new[BLOCK:]
            dv_acc[bi, cur, :] += dv_new[BLOCK:]
            dk_acc[bi, prev, :] += dk_new[:BLOCK]
            dv_acc[bi, prev, :] += dv_new[:BLOCK]
        for qb in range(cf.nq):
            rows = slice(HEADS_PER_BLOCK * BLOCK * qb, HEADS_PER_BLOCK * BLOCK * (qb + 1))
            dbias_ref[rows, :] += ds_sum[qb]
            if cf.sinks:
                dsink_ref[rows, :] -= dsink_sum[qb]

        @pl.when(j == nb - 1)
        def _():
            dk_ref[...] = dk_acc[...].astype(dk_ref.dtype)
            dv_ref[...] = dv_acc[...].astype(dv_ref.dtype)

    tok = pl.BlockSpec((bsz, BLOCK, cf.wq), lambda r, j: (0, j, r))
    in_specs = _att_in_specs(cf, bsz) + [tok] * 2
    args = [zf] * (cf.nq + 4) + [bias] * cf.nq + ([sinks] * cf.nq if cf.sinks else []) + [stats, dy]
    seq = pl.BlockSpec((bsz, l, cf.wkv), lambda r, j: (0, 0, r))
    out_specs = [tok, seq, seq, pl.BlockSpec((cf.nh * BLOCK, 2 * BLOCK), lambda r, j: (0, 0))]
    grad_dtype = BF16 if cf.d == 1 else F32
    out_shape = [jax.ShapeDtypeStruct((bsz, l, cf.d * cf.wq), grad_dtype),
                 jax.ShapeDtypeStruct((bsz, l, cf.d * cf.wkv), grad_dtype),
                 jax.ShapeDtypeStruct((bsz, l, cf.d * cf.wkv), grad_dtype),
                 jax.ShapeDtypeStruct((cf.nh * BLOCK, 2 * BLOCK), F32)]
    if cf.sinks:
        out_specs.append(pl.BlockSpec((cf.nh * BLOCK, 128), lambda r, j: (0, 0)))
        out_shape.append(jax.ShapeDtypeStruct((cf.nh * BLOCK, 128), F32))
    return _call(
        body, name=name, grid=(cf.d, nb), in_specs=in_specs, out_specs=out_specs, out_shape=out_shape,
        scratch=[pltpu.VMEM((bsz, l, cf.wkv), F32), pltpu.VMEM((bsz, l, cf.wkv), F32)],
        sem=("arbitrary", "arbitrary"), args=args, hosted=hosted)


def _merge_fwd(o_a, lse_a, o_b, zg, h, wa_t, wb_t, wout, name):
    t = h.shape[0]
    tm = 512

    def body(o1, o2, o3, l1, l2, l3, ob_ref, zg_ref, h_ref, wa_ref, wb_ref, wo_ref, h2_ref, ya_ref, lt_ref, *slabs):
        o = [_unfold(ref, slabs[i], d) for i, (ref, d) in enumerate(zip((o1, o2, o3), DILATIONS))]
        l = [_unfold(ref, slabs[3 + i], d) for i, (ref, d) in enumerate(zip((l1, l2, l3), DILATIONS))]
        m = jnp.maximum(jnp.maximum(l[0], l[1]), l[2])
        e1, e2, e3 = jnp.exp(l[0] - m), jnp.exp(l[1] - m), jnp.exp(l[2] - m)
        se = e1 + e2 + e3
        ya = (e1 / se) * o[0] + (e2 / se) * o[1] + (e3 / se) * o[2]
        ya_ref[...] = ya
        lt_ref[...] = m + jnp.log(se)
        pa = _nt(ya.astype(BF16), wa_ref[...])
        pb = _nt(ob_ref[...].astype(BF16), wb_ref[...])
        merged = (jax.nn.sigmoid(zg_ref[:, :D_MODEL].astype(F32)) * pa
                  + jax.nn.sigmoid(zg_ref[:, D_MODEL:].astype(F32)) * pb)
        h2_ref[...] = h_ref[...] + _nn(merged.astype(BF16), wo_ref[...])

    row = lambda w: pl.BlockSpec((tm, w), lambda i: (i, 0))
    folded = [pl.BlockSpec((tm // d, d * 256), lambda i: (i, 0)) for d in DILATIONS]
    full = _resident
    return pl.pallas_call(
        body, name=name, grid=(t // tm,),
        in_specs=folded + folded + [row(512), row(GATE_W), row(D_MODEL), full(wa_t), full(wb_t), full(wout)],
        out_specs=[row(D_MODEL), row(256), row(256)],
        out_shape=[jax.ShapeDtypeStruct((t, D_MODEL), F32), jax.ShapeDtypeStruct((t, 256), F32),
                   jax.ShapeDtypeStruct((t, 256), F32)],
        scratch_shapes=[pltpu.VMEM((2, tm, 128), F32)] * 6,
        compiler_params=_params("parallel"),
    )(*o_a, *lse_a, o_b, zg, h, wa_t, wb_t, wout)


def _head_stats(lt, dy, y):
    lane = lax.broadcasted_iota(jnp.int32, dy.shape, 1)
    prod = dy * y
    dot = jnp.zeros_like(prod)
    for hd in range(dy.shape[1] // HEAD_DIM):
        mine = lane // HEAD_DIM == hd
        dot = jnp.where(mine, jnp.sum(jnp.where(mine, prod, 0.0), axis=-1, keepdims=True), dot)
    return jnp.where(lane % HEAD_DIM < HEAD_DIM // 2, lt, dot)


def _merge_bwd(dh, ya, lse_tot, o_b, lse_b, zg, wa_t, wb_t, wout, name, hosted=()):
    t = dh.shape[0]
    tm = 256

    def body(dh_ref, ya_ref, lt_ref, ob_ref, lb_ref, zg_ref, wa_ref, wb_ref, wo_ref,
             mg_ref, dpa_ref, dpb_ref, dzg_ref, dy1, dy2, dy3, st1, st2, st3, dyb_ref, stb_ref, *slabs):
        dm = _nt(dh_ref[...].astype(BF16), wo_ref[...])
        pa = _nt(ya_ref[...].astype(BF16), wa_ref[...])
        pb = _nt(ob_ref[...].astype(BF16), wb_ref[...])
        sa = jax.nn.sigmoid(zg_ref[:, :D_MODEL].astype(F32))
        sb = jax.nn.sigmoid(zg_ref[:, D_MODEL:].astype(F32))
        mg_ref[...] = (sa * pa + sb * pb).astype(BF16)
        dpa = (dm * sa).astype(BF16)
        dpb = (dm * sb).astype(BF16)
        dpa_ref[...] = dpa
        dpb_ref[...] = dpb
        dzg_ref[:, :D_MODEL] = (dm * pa * (sa * (1.0 - sa))).astype(BF16)
        dzg_ref[:, D_MODEL:] = (dm * pb * (sb * (1.0 - sb))).astype(BF16)
        dya = _nn(dpa, wa_ref[...])
        dyb = _nn(dpb, wb_ref[...])
        stats = _head_stats(lt_ref[...], dya, ya_ref[...])
        for i, d in enumerate(DILATIONS):
            _fold(dya, slabs[i], (dy1, dy2, dy3)[i], d)
            _fold(stats, slabs[3 + i], (st1, st2, st3)[i], d)
        dyb_ref[...] = dyb
        stb_ref[...] = _head_stats(lb_ref[...], dyb, ob_ref[...])

    row = lambda w: pl.BlockSpec((tm, w), lambda i: (i, 0))
    folded = [pl.BlockSpec((tm // d, d * 256), lambda i: (i, 0)) for d in DILATIONS]
    full = _resident
    sds = jax.ShapeDtypeStruct
    fshape = [sds((t // d, d * 256), F32) for d in DILATIONS]
    return _call(
        body, name=name, grid=(t // tm,),
        in_specs=[row(D_MODEL), row(256), row(256), row(512), row(512), row(GATE_W), full(wa_t), full(wb_t),
                  full(wout)],
        out_specs=[row(D_MODEL), row(D_MODEL), row(D_MODEL), row(GATE_W)] + folded + folded + [row(512), row(512)],
        out_shape=[sds((t, D_MODEL), BF16), sds((t, D_MODEL), BF16), sds((t, D_MODEL), BF16),
                   sds((t, GATE_W), BF16)] + fshape + fshape + [sds((t, 512), F32), sds((t, 512), F32)],
        scratch=[pltpu.VMEM((2, tm, 128), F32)] * 6,
        sem=("parallel",), args=(dh, ya, lse_tot, o_b, lse_b, zg, wa_t, wb_t, wout), hosted=hosted)


def _loss_head(h, g, target, name):
    t = h.shape[0]
    tm = 1024

    def body(h_ref, g_ref, t_ref, dh_ref, loss_ref, dg_ref):
        @pl.when(pl.program_id(0) == 0)
        def _():
            loss_ref[...] = jnp.zeros_like(loss_ref)
            dg_ref[...] = jnp.zeros_like(dg_ref)

        xhat, r = _rms(h_ref[...], g_ref[...])
        err = xhat * g_ref[...] - t_ref[...]
        loss_ref[...] += 0.5 * jnp.sum(jnp.mean(err * err, axis=-1, keepdims=True), axis=0, keepdims=True)
        dx, dg = _rms_bwd(err * (1.0 / D_MODEL), xhat, r, g_ref[...])
        dh_ref[...] = dx
        dg_ref[...] += dg

    row = pl.BlockSpec((tm, D_MODEL), lambda i: (i, 0))
    vec = pl.BlockSpec((1, D_MODEL), lambda i: (0, 0))
    return pl.pallas_call(
        body, name=name, grid=(t // tm,),
        in_specs=[row, vec, row], out_specs=[row, pl.BlockSpec((1, 128), lambda i: (0, 0)), vec],
        out_shape=[jax.ShapeDtypeStruct((t, D_MODEL), F32), jax.ShapeDtypeStruct((1, 128), F32),
                   jax.ShapeDtypeStruct((1, D_MODEL), F32)],
        compiler_params=_params("arbitrary"),
    )(h, g, target)


def _pair_sum(grad, got, name):
    _, _, r, cdim = grad.shape
    core = lax.axis_index("c").astype(jnp.int32).reshape(1)

    def body(core_ref, g_ref, s_ref, o_ref):
        o_ref[...] = (g_ref[...] + s_ref[...]).astype(BF16)

    return pl.pallas_call(
        body, name=name,
        grid_spec=pltpu.PrefetchScalarGridSpec(
            num_scalar_prefetch=1, grid=(N_CHIP,),
            in_specs=[pl.BlockSpec((None, None, r, cdim), lambda q, core_ref: (q, core_ref[0], 0, 0)),
                      pl.BlockSpec((None, None, r, cdim), lambda q, core_ref: (q, 0, 0, 0))],
            out_specs=pl.BlockSpec((None, r, cdim), lambda q, core_ref: (q, 0, 0))),
        out_shape=jax.ShapeDtypeStruct((N_CHIP, r, cdim), BF16),
        compiler_params=_params("parallel"),
    )(core, grad, got)


def _adamw_math(w, g, m, v):
    m = ADAM_B1 * m + (1.0 - ADAM_B1) * g
    v = ADAM_B2 * v + (1.0 - ADAM_B2) * jnp.square(g)
    m_hat = m / (1.0 - ADAM_B1 ** ADAM_STEP)
    v_hat = v / (1.0 - ADAM_B2 ** ADAM_STEP)
    delta = -ADAM_LR * (m_hat / (jnp.sqrt(v_hat) + ADAM_EPS) + ADAM_WD * w)
    return delta, m, v


def _update(w, parts, m, v, transposed, name):
    r, c = parts.shape[1:]

    def body(w_ref, p_ref, m_ref, v_ref, g_ref, d_ref, m2_ref, v2_ref):
        def total(rows):
            return ((p_ref[0, rows].astype(F32) + p_ref[1, rows].astype(F32)) + p_ref[2, rows].astype(F32)) \
                + p_ref[3, rows].astype(F32)

        def update(at, g):
            g_ref[at] = g
            d_ref[at], m2_ref[at], v2_ref[at] = _adamw_math(w_ref[at], g, m_ref[at], v_ref[at])

        if not transposed:
            update((slice(None), slice(None)), total(slice(None)))
            return
        for r0 in range(0, r, 128):
            n = min(128, r - r0)
            gt = total(slice(r0, r0 + n))
            if n < 128:
                gt = jnp.concatenate([gt, jnp.zeros((128 - n, c), F32)], axis=0)
            update((slice(None), slice(r0, r0 + n)), gt.T[:, :n])

    sds = jax.ShapeDtypeStruct(w.shape, F32)
    return pl.pallas_call(body, name=name, out_shape=[sds] * 4,
                          compiler_params=pltpu.CompilerParams(vmem_limit_bytes=VMEM_LIMIT))(w, parts, m, v)


SMALL_ROWS = 80


def _small_update(g, w, m, v, name):
    def body(g_ref, w_ref, m_ref, v_ref, gs_ref, d_ref, m2_ref, v2_ref, got_ref, send_sems, recv_sems):
        x, y, c = _place()
        me = 4 * x + 2 * y + c
        got_ref[me] = g_ref[...]
        copies = []
        for k in range(1, N_DEV):
            peer = (x ^ (k >> 2), y ^ ((k >> 1) & 1), c ^ (k & 1))
            cp = pltpu.make_async_remote_copy(
                src_ref=g_ref, dst_ref=got_ref.at[me], send_sem=send_sems.at[k - 1], recv_sem=recv_sems.at[k - 1],
                device_id=peer, device_id_type=MESH)
            cp.start()
            copies.append(cp)
        for cp in copies:
            cp.wait()
        total = got_ref[0]
        for k in range(1, N_DEV):
            total = total + got_ref[k]
        gs_ref[...] = total
        d_ref[...], m2_ref[...], v2_ref[...] = _adamw_math(w_ref[...], total, m_ref[...], v_ref[...])

    sds = jax.ShapeDtypeStruct((SMALL_ROWS, 128), F32)
    vm = pl.BlockSpec(memory_space=pltpu.VMEM)
    return pl.pallas_call(
        body, name=name, in_specs=[vm] * 4, out_specs=[vm] * 4, out_shape=[sds] * 4,
        scratch_shapes=[pltpu.VMEM((N_DEV, SMALL_ROWS, 128), F32), pltpu.SemaphoreType.DMA((N_DEV - 1,)),
                        pltpu.SemaphoreType.DMA((N_DEV - 1,))],
    )(g, w, m, v)


def _pack_small(gains, b_in, rel_bias, sinks, last):
    rows = [a.reshape(8, 128) for a in gains] + [b_in.reshape(40, 128), rel_bias.reshape(5, 128),
                                                 jnp.pad(sinks.reshape(1, 8), ((0, 0), (0, 120))), last]
    rows.append(jnp.zeros((SMALL_ROWS - 79, 128), F32))
    return jnp.concatenate(rows, axis=0)


def _unpack_small(p, like):
    out = [p[8 * i:8 * i + 8].reshape(like[i].shape) for i in range(4)]
    out.append(p[32:72].reshape(like[4].shape))
    out.append(p[72:77].reshape(like[5].shape))
    out.append(p[77, :8].reshape(like[6].shape))
    return out


def kernel(x, ffn1_norm, ffn1_w_gate, ffn1_w_up, ffn1_w_down, mix_norm, w_in, b_in, w_branch_a, w_branch_b, w_out, sinks, rel_bias, ffn2_norm, ffn2_w_gate, ffn2_w_up, ffn2_w_down, final_norm, loss_target, m_ffn1_norm, m_ffn1_w_gate, m_ffn1_w_up, m_ffn1_w_down, m_mix_norm, m_w_in, m_b_in, m_w_branch_a, m_w_branch_b, m_w_out, m_sinks, m_rel_bias, m_ffn2_norm, m_ffn2_w_gate, m_ffn2_w_up, m_ffn2_w_down, m_final_norm, v_ffn1_norm, v_ffn1_w_gate, v_ffn1_w_up, v_ffn1_w_down, v_mix_norm, v_w_in, v_b_in, v_w_branch_a, v_w_branch_b, v_w_out, v_sinks, v_rel_bias, v_ffn2_norm, v_ffn2_w_gate, v_ffn2_w_up, v_ffn2_w_down, v_final_norm):
    bsz, seq, _ = x.shape
    t = bsz * seq
    xt = x.reshape(t, D_MODEL)
    target = loss_target.reshape(t, D_MODEL)

    big = [("ffn1_w_gate", ffn1_w_gate, m_ffn1_w_gate, v_ffn1_w_gate, True),
           ("ffn1_w_up", ffn1_w_up, m_ffn1_w_up, v_ffn1_w_up, True),
           ("ffn1_w_down", ffn1_w_down, m_ffn1_w_down, v_ffn1_w_down, False),
           ("w_in", w_in, m_w_in, v_w_in, True),
           ("w_branch_a", w_branch_a, m_w_branch_a, v_w_branch_a, True),
           ("w_branch_b", w_branch_b, m_w_branch_b, v_w_branch_b, True),
           ("w_out", w_out, m_w_out, v_w_out, False),
           ("ffn2_w_gate", ffn2_w_gate, m_ffn2_w_gate, v_ffn2_w_gate, True),
           ("ffn2_w_up", ffn2_w_up, m_ffn2_w_up, v_ffn2_w_up, True),
           ("ffn2_w_down", ffn2_w_down, m_ffn2_w_down, v_ffn2_w_down, False)]
    shard = {nm: (w[0].T if tr else w[0]).astype(BF16) for nm, w, _, _, tr in big}
    full = {}

    def gather(names):
        return _Gather([shard[nm] for nm in names])

    def keep(names, got):
        for nm, gw in zip(names, got):
            full[nm] = gw.reshape(-1, gw.shape[-1])

    ffn1_names = ["ffn1_w_gate", "ffn1_w_up", "ffn1_w_down"]
    mix_names = ["w_in", "w_branch_a", "w_branch_b", "w_out"]
    ffn2_names = ["ffn2_w_gate", "ffn2_w_up", "ffn2_w_down"]
    g1, gm, g2, gf = ffn1_norm, mix_norm, ffn2_norm, final_norm.reshape(1, D_MODEL)

    keep(ffn1_names, _exchange([gather(ffn1_names)], "gather_ffn1")[0])
    (h1, n1, a1, b1, hff1), (got,) = _ffn_fwd(xt, g1, full["ffn1_w_gate"], full["ffn1_w_up"], full["ffn1_w_down"], "ffn1_fwd",
                                hosted=[gather(mix_names)])
    keep(mix_names, got)
    (u, zq, zg, *zdil), (got,) = _inproj_fwd(h1, gm, full["w_in"], b_in, "inproj_fwd", hosted=[gather(ffn2_names[:2])])
    keep(ffn2_names[:2], got)
    buckets = _bucket_tiles()
    bias = _bias_build(rel_bias, buckets)
    sink_rows = jnp.broadcast_to(sinks.reshape(8, 1, 1), (8, BLOCK, 128)).reshape(8 * BLOCK, 128)
    cfs = [_Att(i) for i in range(4)]
    zfold = []
    for i, cf in enumerate(cfs):
        if cf.d == 1:
            zfold.append(zq.reshape(bsz, seq, QKV_W))
        else:
            zfold.append(zdil[i - 1].reshape(bsz, seq // cf.d, cf.d * cf.row_w))
    att = [None] * 4
    for i in (3, 0, 1, 2):
        cf = cfs[i]
        (o, lse), got = _attn_fwd(cf, zfold[i], bias, sink_rows, f"attn{i}_fwd",
                                  hosted=[gather(ffn2_names[2:])] if i == 3 else ())
        if i == 3:
            keep(ffn2_names[2:], got[0])
        att[i] = (o.reshape(t // cf.d, cf.d * cf.wq), lse.reshape(t // cf.d, cf.d * cf.wq))
    o_b, lse_b = att[3]
    h2, ya, lse_tot = _merge_fwd([a[0] for a in att[:3]], [a[1] for a in att[:3]], o_b, zg, h1,
                                 full["w_branch_a"], full["w_branch_b"], full["w_out"], "merge_fwd")
    (h3, n2, a2, b2, hff2), _ = _ffn_fwd(h2, g2, full["ffn2_w_gate"], full["ffn2_w_up"], full["ffn2_w_down"], "ffn2_fwd")
    dh3, loss_part, dgf = _loss_head(h3, gf, target, "loss_head")

    grads, pair, from_chips = {}, {}, {}

    def by_owner(nm):
        return grads[nm].reshape(N_CHIP, 2, -1, grads[nm].shape[-1])

    def to_core(names):
        return _CoreExchange([by_owner(nm) for nm in names])

    def pair_up(names, got):
        for nm, sib in zip(names, got):
            pair[nm] = _pair_sum(by_owner(nm), sib, f"pair_sum_{nm}")

    def to_chips(names):
        return _ChipExchange([pair[nm] for nm in names])

    def landed(names, got):
        for nm, parts in zip(names, got):
            from_chips[nm] = parts

    dh2, dg2, da, db = _ffn_bwd(h2, a2, b2, g2, dh3, full["ffn2_w_gate"], full["ffn2_w_up"],
                                          full["ffn2_w_down"], "ffn2_bwd")
    grads["ffn2_w_gate"], _ = _tn_matmul(da, n2, 1408, "ffn2_dgate")
    grads["ffn2_w_up"], _ = _tn_matmul(db, n2, 1408, "ffn2_dup")
    grads["ffn2_w_down"], _ = _tn_matmul(hff2, dh3, 1408, "ffn2_ddown", scale=0.5)
    (merged, dpa, dpb, dzg, *cot), (got,) = _merge_bwd(
        dh2, ya, lse_tot, o_b, lse_b, zg, full["w_branch_a"], full["w_branch_b"], full["w_out"], "merge_bwd",
        hosted=[to_core(ffn2_names)])
    dys, sts = cot[0:3] + [cot[6]], cot[3:6] + [cot[7]]
    pair_up(ffn2_names, got)
    dq, dk, dv, dbias, dsink = [None] * 4, [None] * 4, [None] * 4, [None] * 4, None
    for i in (3, 0, 1, 2):
        cf = cfs[i]
        shp = (bsz, seq // cf.d, cf.d * cf.wq)
        hosted = {3: lambda: [to_chips(ffn2_names[:2])], 0: lambda: [to_chips(ffn2_names[2:])]}.get(i, list)()
        res, got = _attn_bwd(cf, zfold[i], bias, sink_rows, sts[i].reshape(shp), dys[i].reshape(shp),
                             f"attn{i}_bwd", hosted=hosted)
        if i == 3:
            landed(ffn2_names[:2], got[0])
        elif i == 0:
            landed(ffn2_names[2:], got[0])
        dq[i] = res[0].reshape(t // cf.d, cf.d * cf.wq)
        dk[i] = res[1].reshape(t // cf.d, cf.d * cf.wkv)
        dv[i] = res[2].reshape(t // cf.d, cf.d * cf.wkv)
        dbias[i] = res[3]
        if cf.sinks:
            dsink = res[4]
    pieces = dq[:3] + dk[:3] + dv[:3] + [dq[3], dk[3], dv[3], dzg]
    dh1, dz, db_in, dgm = _inproj_bwd(pieces, h1, gm, dh2, full["w_in"], "inproj_bwd")
    dx, dg1, da, db = _ffn_bwd(xt, a1, b1, g1, dh1, full["ffn1_w_gate"], full["ffn1_w_up"],
                                         full["ffn1_w_down"], "ffn1_bwd")
    grads["w_in"], _ = _tn_matmul(dz, u, 1280, "dw_in")
    grads["ffn1_w_down"], (got,) = _tn_matmul(hff1, dh1, 1408, "ffn1_ddown", hosted=[to_core(["w_in"])], scale=0.5)
    pair_up(["w_in"], got)
    grads["ffn1_w_gate"], got = _tn_matmul(da, n1, 1408, "ffn1_dgate",
                                           hosted=[to_chips(["w_in"]), to_core(["ffn1_w_down"])])
    landed(["w_in"], got[0])
    pair_up(["ffn1_w_down"], got[1])
    grads["ffn1_w_up"], got = _tn_matmul(db, n1, 1408, "ffn1_dup",
                                         hosted=[to_chips(["ffn1_w_down"]), to_core(["ffn1_w_gate"])])
    landed(["ffn1_w_down"], got[0])
    pair_up(["ffn1_w_gate"], got[1])
    grads["w_out"], got = _tn_matmul(merged, dh2, 1024, "dw_out",
                                     hosted=[to_chips(["ffn1_w_gate"]), to_core(["ffn1_w_up"])])
    landed(["ffn1_w_gate"], got[0])
    pair_up(["ffn1_w_up"], got[1])
    grads["w_branch_b"], got = _tn_matmul(dpb, o_b, 1024, "dw_branch_b",
                                          hosted=[to_chips(["ffn1_w_up"]), to_core(["w_out"])])
    landed(["ffn1_w_up"], got[0])
    pair_up(["w_out"], got[1])
    grads["w_branch_a"], got = _tn_matmul(dpa, ya, 1024, "dw_branch_a",
                                          hosted=[to_chips(["w_out"]), to_core(["w_branch_b"])])
    landed(["w_out"], got[0])
    pair_up(["w_branch_b"], got[1])
    got = _exchange([to_chips(["w_branch_b"]), to_core(["w_branch_a"])], "reduce_scatter_tail1")
    landed(["w_branch_b"], got[0])
    pair_up(["w_branch_a"], got[1])
    landed(["w_branch_a"], _exchange([to_chips(["w_branch_a"])], "reduce_scatter_tail2")[0])
    dtable, dsinks = _bias_reduce(jnp.concatenate(dbias, axis=0), buckets, dsink)

    out_g, out_d, out_m, out_v = {}, {}, {}, {}
    for nm, w, m, v, tr in big:
        g, d2, m2, v2 = _update(w[0], from_chips[nm], m[0], v[0], tr, f"update_{nm}")
        out_g[nm], out_d[nm], out_m[nm], out_v[nm] = g[None], d2[None], m2[None], v2[None]

    small = [("ffn1_norm", ffn1_norm, m_ffn1_norm, v_ffn1_norm), ("mix_norm", mix_norm, m_mix_norm, v_mix_norm),
             ("ffn2_norm", ffn2_norm, m_ffn2_norm, v_ffn2_norm), ("final_norm", final_norm, m_final_norm, v_final_norm),
             ("b_in", b_in, m_b_in, v_b_in), ("rel_bias", rel_bias, m_rel_bias, v_rel_bias),
             ("sinks", sinks, m_sinks, v_sinks)]
    zero_row = jnp.zeros((1, 128), F32)
    pack = lambda arrs, last: _pack_small(arrs[:4], arrs[4], arrs[5], arrs[6], last)
    g_small = pack([dg1, dgm, dg2, dgf, db_in, dtable[:, :TOTAL_HEADS], dsinks[:, 0]], loss_part)
    packed = [pack([s[k] for s in small], zero_row) for k in (1, 2, 3)]
    gs, ds, ms, vs = _small_update(g_small, *packed, "small_update")
    like = [s[1] for s in small]
    for nm_s, g_, d_, m_, v_ in zip([s[0] for s in small], _unpack_small(gs, like), _unpack_small(ds, like),
                                    _unpack_small(ms, like), _unpack_small(vs, like)):
        out_g[nm_s], out_d[nm_s], out_m[nm_s], out_v[nm_s] = g_, d_, m_, v_
    loss = gs[78, 0]

    order = ["ffn1_norm", "ffn1_w_gate", "ffn1_w_up", "ffn1_w_down", "mix_norm", "w_in", "b_in", "w_branch_a",
             "w_branch_b", "w_out", "sinks", "rel_bias", "ffn2_norm", "ffn2_w_gate", "ffn2_w_up", "ffn2_w_down",
             "final_norm"]
    return (loss, dx.reshape(x.shape), *[out_g[k] for k in order], *[out_d[k] for k in order],
            *[out_m[k] for k in order], *[out_v[k] for k in order])
```

```python
import functools
import math

import numpy as np
import jax
import jax.numpy as jnp
from jax import lax
from jax.experimental import pallas as pl
from jax.experimental.pallas import tpu as pltpu

D_MODEL = 1024
D_FF = 2816
FF_CHUNK = 256
HEAD_DIM = 64
BLOCK = 128
N_BUCKETS = 32
MAX_DISTANCE = 2048
A_HEADS = 12
TOTAL_HEADS = 20
DIL_GROUPS = ((128, 1), (512, 4), (2048, 16))
B_WINDOW = 128
QKV_W = 3072
GATE_W = 2048
D_IN = QKV_W + GATE_W
EPS = 1e-6
NEG = -1e30
N_DEV = 8
N_CHIP = 4
ADAM_LR, ADAM_B1, ADAM_B2, ADAM_EPS, ADAM_WD, ADAM_STEP = 0.001, 0.9, 0.999, 1e-08, 0.01, 10
VMEM_LIMIT = 56 * 1024 * 1024
MESH = pl.DeviceIdType.MESH
BF16 = jnp.bfloat16
F32 = jnp.float32
ANY = pl.BlockSpec(memory_space=pl.ANY)


def _params(*sem):
    return pltpu.CompilerParams(dimension_semantics=sem, vmem_limit_bytes=VMEM_LIMIT)


def _resident(a):
    return pl.BlockSpec(a.shape, lambda i: (0, 0), pipeline_mode=pl.Buffered(1))


def _place():
    return lax.axis_index("x"), lax.axis_index("y"), lax.axis_index("c")


class _Gather:
    def __init__(self, shards):
        self.ins = list(shards)
        n = self.n = len(shards)
        self.out_shape = [jax.ShapeDtypeStruct((N_DEV,) + s.shape, s.dtype) for s in shards]
        self.scratch = [pltpu.SemaphoreType.DMA((7 * n,)), pltpu.SemaphoreType.DMA((7 * n,)),
                        pltpu.SemaphoreType.DMA((n,))]

    def _copies(self, ins, outs, sems):
        send_sems, recv_sems, local_sems = sems
        x, y, c = _place()
        me, sibling = (x, y, c), (x, y, 1 - c)
        chips = [(1 - x, y), (x, 1 - y), (1 - x, 1 - y)]

        def copy(i, k, block, to, src=None):
            dst = outs[i].at[4 * block[0] + 2 * block[1] + block[2]]
            return pltpu.make_async_remote_copy(
                src_ref=dst if src is None else src, dst_ref=dst, send_sem=send_sems.at[7 * i + k],
                recv_sem=recv_sems.at[7 * i + k], device_id=to, device_id_type=MESH)

        n = self.n
        mine = [pltpu.make_async_copy(ins[i], outs[i].at[4 * x + 2 * y + c], local_sems.at[i]) for i in range(n)]
        first = [copy(i, 0, me, sibling, src=ins[i]) for i in range(n)]
        first += [copy(i, 1 + j, me, (*chip, c), src=ins[i]) for i in range(n) for j, chip in enumerate(chips)]
        landed = [copy(i, 1 + j, (*chip, c), me) for j, chip in enumerate(chips) for i in range(n)]
        passed = [copy(i, 4 + j, (*chip, c), sibling) for j, chip in enumerate(chips) for i in range(n)]
        from_sibling = [copy(i, 0, sibling, me) for i in range(n)]
        from_sibling += [copy(i, 4 + j, (*chip, 1 - c), me) for i in range(n) for j, chip in enumerate(chips)]
        return mine, first, landed, passed, from_sibling

    def start(self, ins, outs, sems):
        mine, first, _, _, _ = self._copies(ins, outs, sems)
        for cp in mine + first:
            cp.start()

    def mid(self, ins, outs, sems):
        _, _, landed, passed, _ = self._copies(ins, outs, sems)
        for got, fwd in zip(landed, passed):
            got.wait_recv()
            fwd.start()

    def end(self, ins, outs, sems):
        mine, first, _, passed, from_sibling = self._copies(ins, outs, sems)
        for cp in from_sibling:
            cp.wait_recv()
        for cp in first + passed:
            cp.wait_send()
        for cp in mine:
            cp.wait()


class _CoreExchange:
    def __init__(self, grads):
        self.ins = list(grads)
        n = self.n = len(grads)
        self.out_shape = [jax.ShapeDtypeStruct((N_CHIP, 1) + g.shape[2:], g.dtype) for g in grads]
        self.scratch = [pltpu.SemaphoreType.DMA((n,)), pltpu.SemaphoreType.DMA((n,))]

    def _copies(self, ins, outs, sems):
        x, y, c = _place()
        return [pltpu.make_async_remote_copy(
            src_ref=ins[i].at[:, pl.ds(1 - c, 1)], dst_ref=outs[i], send_sem=sems[0].at[i],
            recv_sem=sems[1].at[i], device_id=(x, y, 1 - c), device_id_type=MESH) for i in range(self.n)]

    def start(self, ins, outs, sems):
        for cp in self._copies(ins, outs, sems):
            cp.start()

    mid = None

    def end(self, ins, outs, sems):
        for cp in self._copies(ins, outs, sems):
            cp.wait()


class _ChipExchange:
    def __init__(self, parts):
        self.ins = list(parts)
        n = self.n = len(parts)
        self.out_shape = [jax.ShapeDtypeStruct(p.shape, p.dtype) for p in parts]
        self.scratch = [pltpu.SemaphoreType.DMA((3 * n,)), pltpu.SemaphoreType.DMA((3 * n,)),
                        pltpu.SemaphoreType.DMA((n,))]

    def _copies(self, ins, outs, sems):
        send_sems, recv_sems, local_sems = sems
        x, y, c = _place()
        my_chip = 2 * x + y
        copies = []
        for i in range(self.n):
            copies.append(pltpu.make_async_copy(ins[i].at[my_chip], outs[i].at[my_chip], local_sems.at[i]))
            for k, (qx, qy) in enumerate([(1 - x, y), (x, 1 - y), (1 - x, 1 - y)]):
                copies.append(pltpu.make_async_remote_copy(
                    src_ref=ins[i].at[2 * qx + qy], dst_ref=outs[i].at[my_chip], send_sem=send_sems.at[3 * i + k],
                    recv_sem=recv_sems.at[3 * i + k], device_id=(qx, qy, c), device_id_type=MESH))
        return copies

    def start(self, ins, outs, sems):
        for cp in self._copies(ins, outs, sems):
            cp.start()

    mid = None

    def end(self, ins, outs, sems):
        for cp in self._copies(ins, outs, sems):
            cp.wait()


def _call(body, *, name, grid, in_specs, out_specs, out_shape, args, scratch=(), sem=None, hosted=()):
    n_in, n_out, n_scr = len(in_specs), len(out_specs), len(scratch)
    x_in = [len(p.ins) for p in hosted]
    x_scr = [len(p.scratch) for p in hosted]
    steps = int(np.prod(grid))

    def wrapped(*refs):
        refs = list(refs)
        ins, refs = refs[:n_in], refs[n_in:]
        x_ins = [[refs.pop(0) for _ in range(k)] for k in x_in]
        outs, refs = refs[:n_out], refs[n_out:]
        x_outs = [[refs.pop(0) for _ in range(k)] for k in x_in]
        scr, refs = refs[:n_scr], refs[n_scr:]
        x_sems = [[refs.pop(0) for _ in range(k)] for k in x_scr]
        step = 0
        for d in range(len(grid)):
            step = step * grid[d] + pl.program_id(d)

        def phase(which, at):
            fns = [(getattr(p, which), a) for p, a in zip(hosted, zip(x_ins, x_outs, x_sems)) if getattr(p, which)]
            if fns:
                @pl.when(step == at)
                def _():
                    for fn, a in fns:
                        fn(*a)

        phase("start", 0)
        phase("mid", (3 * steps) // 4)
        body(*ins, *outs, *scr)
        phase("end", steps - 1)

    results = pl.pallas_call(
        wrapped, name=name, grid=grid,
        in_specs=list(in_specs) + [ANY] * sum(x_in), out_specs=list(out_specs) + [ANY] * sum(x_in),
        out_shape=list(out_shape) + [s for p in hosted for s in p.out_shape],
        scratch_shapes=list(scratch) + [s for p in hosted for s in p.scratch],
        compiler_params=_params(*(("arbitrary",) * len(grid) if hosted else sem)),
    )(*args, *[a for p in hosted for a in p.ins])
    own, rest = list(results[:n_out]), list(results[n_out:])
    return own, [[rest.pop(0) for _ in range(k)] for k in x_in]


def _exchange(programs, name):
    x_in = [len(p.ins) for p in programs]
    x_scr = [len(p.scratch) for p in programs]

    def body(*refs):
        refs = list(refs)
        x_ins = [[refs.pop(0) for _ in range(k)] for k in x_in]
        x_outs = [[refs.pop(0) for _ in range(k)] for k in x_in]
        x_sems = [[refs.pop(0) for _ in range(k)] for k in x_scr]
        for which in ("start", "mid", "end"):
            for p, a in zip(programs, zip(x_ins, x_outs, x_sems)):
                if getattr(p, which):
                    getattr(p, which)(*a)

    results = list(pl.pallas_call(
        body, name=name, in_specs=[ANY] * sum(x_in), out_specs=[ANY] * sum(x_in),
        out_shape=[s for p in programs for s in p.out_shape],
        scratch_shapes=[s for p in programs for s in p.scratch],
    )(*[a for p in programs for a in p.ins]))
    return [[results.pop(0) for _ in range(k)] for k in x_in]


def _nt(a, b):
    return lax.dot_general(a, b, (((1,), (1,)), ((), ())), preferred_element_type=F32)


def _nn(a, b):
    return lax.dot_general(a, b, (((1,), (0,)), ((), ())), preferred_element_type=F32)


def _tn(a, b):
    return lax.dot_general(a, b, (((0,), (0,)), ((), ())), preferred_element_type=F32)


def _rms(x, g):
    r = lax.rsqrt(jnp.mean(x * x, axis=-1, keepdims=True) + EPS)
    return x * r, r


def _rms_bwd(dn, xhat, r, g):
    dg = jnp.sum(dn * xhat, axis=0, keepdims=True)
    dxh = dn * g
    dx = r * (dxh - xhat * jnp.mean(dxh * xhat, axis=-1, keepdims=True))
    return dx, dg


def _ffn_fwd(x, g, wg_t, wu_t, wd, name, hosted=(), head=None):
    t = x.shape[0]
    tm = 256

    def body(*refs):
        x_ref, g_ref, wg_ref, wu_ref, wd_ref = refs[:5]
        h_ref, n_ref, a_ref, b_ref, hff_ref = refs[-5 if head is None else -7:][:5]
        xhat, _ = _rms(x_ref[...], g_ref[...])
        n = (xhat * g_ref[...]).astype(BF16)
        n_ref[...] = n
        for c in range(0, D_FF, FF_CHUNK):
            cols = slice(c, c + FF_CHUNK)
            a = _nt(n, wg_ref[cols, :])
            b = _nt(n, wu_ref[cols, :])
            a_ref[:, cols] = a.astype(BF16)
            b_ref[:, cols] = b.astype(BF16)
            hff_ref[:, cols] = (a * jax.nn.sigmoid(a) * b).astype(BF16)
        h = x_ref[...] + 0.5 * _nn(hff_ref[...], wd_ref[...])
        if head is None:
            h_ref[...] = h
            return
        gf_ref, t_ref, loss_ref, dgf_ref = refs[5], refs[6], refs[-2], refs[-1]

        @pl.when(pl.program_id(0) == 0)
        def _():
            loss_ref[...] = jnp.zeros_like(loss_ref)
            dgf_ref[...] = jnp.zeros_like(dgf_ref)

        yhat, r = _rms(h, gf_ref[...])
        err = yhat * gf_ref[...] - t_ref[...]
        loss_ref[...] += 0.5 * jnp.sum(jnp.mean(err * err, axis=-1, keepdims=True), axis=0, keepdims=True)
        h_ref[...], dgf = _rms_bwd(err * (1.0 / D_MODEL), yhat, r, gf_ref[...])
        dgf_ref[...] += dgf

    row = pl.BlockSpec((tm, D_MODEL), lambda i: (i, 0))
    hid = pl.BlockSpec((tm, D_FF), lambda i: (i, 0))
    sds = jax.ShapeDtypeStruct
    in_specs = [row, _resident(g), _resident(wg_t), _resident(wu_t), _resident(wd)]
    out_specs = [row, row, hid, hid, hid]
    out_shape = [sds((t, D_MODEL), F32), sds((t, D_MODEL), BF16), sds((t, D_FF), BF16), sds((t, D_FF), BF16),
                 sds((t, D_FF), BF16)]
    args = (x, g, wg_t, wu_t, wd)
    if head is not None:
        in_specs += [_resident(head[0]), row]
        out_specs += [pl.BlockSpec((1, 128), lambda i: (0, 0)), pl.BlockSpec((1, D_MODEL), lambda i: (0, 0))]
        out_shape += [sds((1, 128), F32), sds((1, D_MODEL), F32)]
        args += tuple(head)
    return _call(body, name=name, grid=(t // tm,), in_specs=in_specs, out_specs=out_specs, out_shape=out_shape,
                 sem=("parallel",) if head is None else ("arbitrary",), args=args, hosted=hosted)


def _ffn_bwd(x, a_pre, b_pre, g, dh, wg_t, wu_t, wd, name):
    t = x.shape[0]
    tm = 256

    def body(x_ref, a_ref, b_ref, g_ref, dh_ref, wg_ref, wu_ref, wd_ref,
             dx_ref, dg_ref, da_ref, db_ref):
        @pl.when(pl.program_id(0) == 0)
        def _():
            dg_ref[...] = jnp.zeros_like(dg_ref)

        dhh = (0.5 * dh_ref[...]).astype(BF16)
        for c in range(0, D_FF, FF_CHUNK):
            cols = slice(c, c + FF_CHUNK)
            a = a_ref[:, cols].astype(F32)
            b = b_ref[:, cols].astype(F32)
            s = jax.nn.sigmoid(a)
            silu = a * s
            dhff = _nt(dhh, wd_ref[cols, :])
            da_ref[:, cols] = (dhff * b * (s * (1.0 + a * (1.0 - s)))).astype(BF16)
            db_ref[:, cols] = (dhff * silu).astype(BF16)
        dn = _nn(da_ref[...], wg_ref[...]) + _nn(db_ref[...], wu_ref[...])
        xhat, r = _rms(x_ref[...], g_ref[...])
        dx, dg = _rms_bwd(dn, xhat, r, g_ref[...])
        dx_ref[...] = dh_ref[...] + dx
        dg_ref[...] += dg

    row = pl.BlockSpec((tm, D_MODEL), lambda i: (i, 0))
    hid = pl.BlockSpec((tm, D_FF), lambda i: (i, 0))
    return pl.pallas_call(
        body, name=name, grid=(t // tm,),
        in_specs=[row, hid, hid, _resident(g), row, _resident(wg_t), _resident(wu_t), _resident(wd)],
        out_specs=[row, pl.BlockSpec((1, D_MODEL), lambda i: (0, 0)), hid, hid],
        out_shape=[jax.ShapeDtypeStruct((t, D_MODEL), F32), jax.ShapeDtypeStruct((1, D_MODEL), F32),
                   jax.ShapeDtypeStruct((t, D_FF), BF16), jax.ShapeDtypeStruct((t, D_FF), BF16)],
        compiler_params=_params("arbitrary"),
    )(x, a_pre, b_pre, g, dh, wg_t, wu_t, wd)


def _tn_matmul(a, b, rc, name, hosted=(), scale=None):
    t, r = a.shape
    c = b.shape[1]
    tk = 1024

    def body(a_ref, b_ref, o_ref):
        @pl.when(pl.program_id(1) == 0)
        def _():
            o_ref[...] = jnp.zeros_like(o_ref)

        o_ref[...] += _tn(a_ref[...].astype(BF16), b_ref[...].astype(BF16))
        if scale is not None:
            @pl.when(pl.program_id(1) == t // tk - 1)
            def _():
                o_ref[...] *= scale

    (out,), got = _call(
        body, name=name, grid=(r // rc, t // tk),
        in_specs=[pl.BlockSpec((tk, rc), lambda i, k: (k, i)), pl.BlockSpec((tk, c), lambda i, k: (k, 0))],
        out_specs=[pl.BlockSpec((rc, c), lambda i, k: (i, 0))],
        out_shape=[jax.ShapeDtypeStruct((r, c), F32)],
        sem=("parallel", "arbitrary"), args=(a, b), hosted=hosted)
    return out, got


def _unfold(blk_ref, slab_ref, d):
    if d == 1:
        return blk_ref[...]
    n = blk_ref.shape[0]
    for r in range(d):
        for half in range(2):
            c0 = 256 * r + 128 * half
            slab_ref[half, pl.ds(r, n, stride=d), :] = blk_ref[:, c0:c0 + 128]
    return jnp.concatenate([slab_ref[0], slab_ref[1]], axis=1)


def _fold(x, slab_ref, out_ref, d):
    if d == 1:
        out_ref[...] = x
        return
    n = out_ref.shape[0]
    slab_ref[0] = x[:, :128]
    slab_ref[1] = x[:, 128:]
    for r in range(d):
        for half in range(2):
            c0 = 256 * r + 128 * half
            out_ref[:, c0:c0 + 128] = slab_ref[half, pl.ds(r, n, stride=d), :]


DILATIONS = tuple(d for _, d in DIL_GROUPS)


PIECE_W = (256,) * 9 + (512, 128, 128, GATE_W)
PIECE_D = DILATIONS * 3 + (1, 1, 1, 1)


def _inproj_fwd(h, g, w_t, b_in, name, hosted=()):
    t = h.shape[0]
    tm, nc = 512, 256
    dilated = [(gi, d) for gi, d in enumerate(DILATIONS) if d > 1]

    def body(h_ref, g_ref, w_ref, b_ref, u_ref, zq_ref, zg_ref, *rest):
        zf_refs, slabs = rest[:len(dilated)], rest[len(dilated):]
        xhat, _ = _rms(h_ref[...], g_ref[...])
        u = (xhat * g_ref[...]).astype(BF16)
        u_ref[...] = u
        for c in range(D_IN // nc):
            z = _nt(u, w_ref[c * nc:(c + 1) * nc, :]) + b_ref[:, c * nc:(c + 1) * nc]
            if c < QKV_W // nc:
                zq_ref[:, c * nc:(c + 1) * nc] = z.astype(BF16)
            else:
                zg_ref[:, c * nc - QKV_W:(c + 1) * nc - QKV_W] = z.astype(BF16)
            part, gi = divmod(c, len(DILATIONS))
            for k, (gk, d) in enumerate(dilated):
                if part < 3 and gi == gk:
                    slab, n = slabs[3 * k + part], tm // d
                    slab[0] = z[:, :128]
                    slab[1] = z[:, 128:]
                    for r in range(d):
                        for half in range(2):
                            c0 = 768 * r + 256 * part + 128 * half
                            zf_refs[k][:, c0:c0 + 128] = slab[half, pl.ds(r, n, stride=d), :].astype(BF16)

    row = lambda w: pl.BlockSpec((tm, w), lambda i: (i, 0))
    full = _resident
    sds = jax.ShapeDtypeStruct
    return _call(
        body, name=name, grid=(t // tm,),
        in_specs=[row(D_MODEL), full(g), full(w_t), full(b_in)],
        out_specs=[row(D_MODEL), row(QKV_W), row(GATE_W)]
        + [pl.BlockSpec((tm // d, d * 768), lambda i: (i, 0)) for _, d in dilated],
        out_shape=[sds((t, D_MODEL), BF16), sds((t, QKV_W), BF16), sds((t, GATE_W), BF16)]
        + [sds((t // d, d * 768), BF16) for _, d in dilated],
        scratch=[pltpu.VMEM((2, tm, 128), F32)] * (3 * len(dilated)),
        sem=("parallel",), args=(h, g, w_t, b_in), hosted=hosted)


def _inproj_bwd(pieces, h, g, dh_res, w_t, name):
    t = h.shape[0]
    tm = 256
    npiece = len(PIECE_W)
    offs = np.concatenate([[0], np.cumsum(PIECE_W)]).tolist()

    def body(*refs):
        p_refs = refs[:npiece]
        h_ref, g_ref, dhr_ref, w_ref, dh_ref, dz_ref, db_ref, dg_ref = refs[npiece:npiece + 8]
        slabs = list(refs[npiece + 8:])
        i = pl.program_id(0)

        @pl.when(i == 0)
        def _():
            db_ref[...] = jnp.zeros_like(db_ref)
            dg_ref[...] = jnp.zeros_like(dg_ref)

        du = jnp.zeros((tm, D_MODEL), F32)
        for k in range(npiece):
            o, w = offs[k], PIECE_W[k]
            token_order = _unfold(p_refs[k], slabs.pop(), PIECE_D[k]).astype(BF16) if PIECE_D[k] > 1 else None
            for c0 in range(0, w, 512):
                cw = min(512, w - c0)
                pz = p_refs[k][:, c0:c0 + cw] if token_order is None else token_order
                dz_ref[:, o + c0:o + c0 + cw] = pz
                db_ref[:, o + c0:o + c0 + cw] += jnp.sum(pz.astype(F32), axis=0, keepdims=True)
                du = du + _nn(pz, w_ref[o + c0:o + c0 + cw, :])
        xhat, r = _rms(h_ref[...], g_ref[...])
        dx, dg = _rms_bwd(du, xhat, r, g_ref[...])
        dh_ref[...] = dhr_ref[...] + dx
        dg_ref[...] += dg

    row = lambda w: pl.BlockSpec((tm, w), lambda i: (i, 0))
    full = lambda shp: pl.BlockSpec(shp, lambda i: (0, 0))
    return pl.pallas_call(
        body, name=name, grid=(t // tm,),
        in_specs=[pl.BlockSpec((tm // d, d * w), lambda i: (i, 0)) for w, d in zip(PIECE_W, PIECE_D)]
        + [row(D_MODEL), _resident(g), row(D_MODEL), _resident(w_t)],
        out_specs=[row(D_MODEL), row(D_IN), full((1, D_IN)), full((1, D_MODEL))],
        out_shape=[jax.ShapeDtypeStruct((t, D_MODEL), F32), jax.ShapeDtypeStruct((t, D_IN), BF16),
                   jax.ShapeDtypeStruct((1, D_IN), F32), jax.ShapeDtypeStruct((1, D_MODEL), F32)],
        scratch_shapes=[pltpu.VMEM((2, tm, 128), F32)] * sum(d > 1 for d in PIECE_D),
        compiler_params=_params("arbitrary"),
    )(*pieces, h, g, dh_res, w_t)


def _t5_bucket(dist):
    max_exact = N_BUCKETS // 2
    n = jnp.maximum(dist, 0)
    nf = jnp.maximum(n, 1).astype(jnp.float32)
    large = max_exact + (jnp.log(nf / max_exact) / math.log(MAX_DISTANCE / max_exact)
                         * (N_BUCKETS - max_exact)).astype(jnp.int32)
    large = jnp.minimum(large, N_BUCKETS - 1)
    return jnp.where(n < max_exact, n, large)


ATT_CFG = ((1, 128, 0, 4), (4, 128, 4, 4), (16, 128, 8, 4), (1, B_WINDOW - 1, A_HEADS, 8))


def _bucket_tiles():
    qi = jnp.arange(BLOCK)[:, None]
    ki = jnp.arange(2 * BLOCK)[None, :]
    dist = qi + BLOCK - ki
    return jnp.stack([_t5_bucket(dist * cfg[0]) for cfg in ATT_CFG]).astype(jnp.int32)


def _band(max_steps):
    row = lax.broadcasted_iota(jnp.int32, (BLOCK, 2 * BLOCK), 0)
    col = lax.broadcasted_iota(jnp.int32, (BLOCK, 2 * BLOCK), 1)
    dist = row + BLOCK - col
    return (dist >= 0) & (dist <= max_steps)


def _bias_build(table, buckets):
    def body(tab_ref, bt_ref, out_ref):
        col = lax.broadcasted_iota(jnp.int32, (BLOCK, 2 * BLOCK), 1)
        for ci, (_, max_steps, h0, nh) in enumerate(ATT_CFG):
            bt = bt_ref[ci]
            band = _band(max_steps)
            for h in range(h0, h0 + nh):
                acc = lax.fori_loop(0, N_BUCKETS, lambda b, acc: jnp.where(bt == b, tab_ref[b, h], acc),
                                    jnp.zeros((BLOCK, 2 * BLOCK), F32))
                out_ref[0, BLOCK * h:BLOCK * (h + 1), :] = jnp.where(band & (col >= BLOCK), acc, NEG)
                out_ref[1, BLOCK * h:BLOCK * (h + 1), :] = jnp.where(band, acc, NEG)

    return pl.pallas_call(
        body, name="bias_build",
        in_specs=[pl.BlockSpec(memory_space=pltpu.SMEM), pl.BlockSpec(memory_space=pltpu.VMEM)],
        out_specs=pl.BlockSpec(memory_space=pltpu.VMEM),
        out_shape=jax.ShapeDtypeStruct((2, TOTAL_HEADS * BLOCK, 2 * BLOCK), F32),
    )(table, buckets)


def _bias_reduce(dbias, buckets, dsink_rows):
    def body(db_ref, bt_ref, ds_ref, out_ref, sink_ref):
        ri = lax.broadcasted_iota(jnp.int32, (N_BUCKETS, 128), 0)
        ci = lax.broadcasted_iota(jnp.int32, (N_BUCKETS, 128), 1)

        def per_bucket(b, acc):
            for cfg_i, (_, _, h0, nh) in enumerate(ATT_CFG):
                hit = bt_ref[cfg_i] == b
                for h in range(h0, h0 + nh):
                    val = jnp.sum(jnp.where(hit, db_ref[BLOCK * h:BLOCK * (h + 1), :], 0.0))
                    acc = jnp.where((ri == b) & (ci == h), val, acc)
            return acc

        out_ref[...] = lax.fori_loop(0, N_BUCKETS, per_bucket, jnp.zeros((N_BUCKETS, 128), F32))
        for h in range(8):
            sink_ref[h:h + 1, :] = jnp.sum(ds_ref[BLOCK * h:BLOCK * (h + 1), :], axis=0, keepdims=True)

    return pl.pallas_call(
        body, name="bias_reduce",
        in_specs=[pl.BlockSpec(memory_space=pltpu.VMEM)] * 3,
        out_specs=[pl.BlockSpec(memory_space=pltpu.VMEM)] * 2,
        out_shape=[jax.ShapeDtypeStruct((N_BUCKETS, 128), F32), jax.ShapeDtypeStruct((8, 128), F32)],
    )(dbias, buckets, dsink_rows)


class _Att:
    def __init__(self, cfg_i):
        stride, _, h0, nh = ATT_CFG[cfg_i]
        self.d = stride if cfg_i < 3 else 1
        self.h0, self.nh = h0, nh
        self.row_w = QKV_W if self.d == 1 else 3 * 256
        if cfg_i < 3:
            self.nq, self.wkv = 1, 256
            self.q_unit = [cfg_i if self.d == 1 else 0]
            self.k_unit, self.v_unit = (3 + cfg_i, 6 + cfg_i) if self.d == 1 else (1, 2)
            self.sinks = False
        else:
            self.nq, self.wkv = 2, 128
            self.q_unit = [9, 10]
            self.k_unit, self.v_unit = 22, 23
            self.sinks = True
        self.wq = 256 * self.nq


def _att_in_specs(cf, bsz):
    uq, ukv = cf.row_w // 256, cf.row_w // cf.wkv
    specs = [pl.BlockSpec((bsz, BLOCK, 256), functools.partial(lambda r, j, u: (0, j, r * uq + u), u=u))
             for u in cf.q_unit]
    for unit in (cf.k_unit, cf.v_unit):
        specs.append(pl.BlockSpec((bsz, BLOCK, cf.wkv),
                                  functools.partial(lambda r, j, u: (0, jnp.maximum(j - 1, 0), r * ukv + u), u=unit)))
        specs.append(pl.BlockSpec((bsz, BLOCK, cf.wkv),
                                  functools.partial(lambda r, j, u: (0, j, r * ukv + u), u=unit)))
    for qb in range(cf.nq):
        specs.append(pl.BlockSpec((None, HEADS_PER_BLOCK * BLOCK, 2 * BLOCK), functools.partial(
            lambda r, j, u: (jnp.minimum(j, 1), u, 0), u=cf.h0 // HEADS_PER_BLOCK + qb)))
    if cf.sinks:
        specs += [pl.BlockSpec((HEADS_PER_BLOCK * BLOCK, 128), functools.partial(lambda r, j, u: (u, 0), u=qb))
                  for qb in range(cf.nq)]
    return specs


HEADS_PER_BLOCK = 4


def _head_masks(rows):
    head = lax.broadcasted_iota(jnp.int32, (rows, 256), 1) // HEAD_DIM
    return [head == h for h in range(HEADS_PER_BLOCK)]


def _stack_heads(x, masks):
    return jnp.concatenate([jnp.where(m, x, jnp.zeros_like(x)) for m in masks], axis=0)


def _unstack_heads(x4, masks):
    blocks = [x4[BLOCK * h:BLOCK * (h + 1)] for h in range(HEADS_PER_BLOCK)]
    return jnp.where(masks[0], blocks[0], jnp.where(masks[1], blocks[1], jnp.where(masks[2], blocks[2], blocks[3])))


def _row_value(x):
    return jnp.max(x, axis=-1, keepdims=True)


def _kv_operand(cf, x, qb):
    if cf.wkv == 256:
        return x
    lane = lax.broadcasted_iota(jnp.int32, x.shape, 1)
    swapped = pltpu.roll(x, HEAD_DIM, 1)
    half = jnp.where(lane < HEAD_DIM, x, swapped) if qb == 0 else jnp.where(lane < HEAD_DIM, swapped, x)
    return jnp.concatenate([half, half], axis=1)


def _kv_fold(cf, grads):
    if cf.wkv == 256:
        return grads[0]
    folded = []
    for g in grads:
        x = g[:, :128] + g[:, 128:]
        folded.append(x + pltpu.roll(x, HEAD_DIM, 1))
    lane = lax.broadcasted_iota(jnp.int32, folded[0].shape, 1)
    return jnp.where(lane < HEAD_DIM, folded[0], folded[1])


def _attn_fwd(cf, zf, bias, sinks, name, hosted=()):
    bsz, l, _ = zf.shape
    nb = l // BLOCK

    def body(*refs):
        refs = list(refs)
        q_refs = [refs.pop(0) for _ in range(cf.nq)]
        kp_ref, kc_ref, vp_ref, vc_ref = [refs.pop(0) for _ in range(4)]
        bias_refs = [refs.pop(0) for _ in range(cf.nq)]
        sink_refs = [refs.pop(0) for _ in range(cf.nq)] if cf.sinks else None
        o_ref, lse_ref = refs
        masks, kv_masks = _head_masks(BLOCK), _head_masks(2 * BLOCK)
        sinks4 = [_row_value(ref[...]) for ref in sink_refs] if cf.sinks else None
        for bi in range(bsz):
            k = jnp.concatenate([kp_ref[bi], kc_ref[bi]], axis=0)
            v = jnp.concatenate([vp_ref[bi], vc_ref[bi]], axis=0)
            for qb in range(cf.nq):
                cols = slice(256 * qb, 256 * (qb + 1))
                kb, vb = _kv_operand(cf, k, qb), _kv_operand(cf, v, qb)
                q4 = _stack_heads(q_refs[qb][bi] * (HEAD_DIM ** -0.5), masks)
                s = _nt(q4, kb) + bias_refs[qb][...]
                m = jnp.max(s, axis=-1, keepdims=True)
                if cf.sinks:
                    sk = sinks4[qb]
                    m = jnp.maximum(m, sk)
                p = jnp.exp(s - m)
                den = jnp.sum(p, axis=-1, keepdims=True)
                if cf.sinks:
                    den = den + jnp.exp(sk - m)
                pn = (p * (1.0 / den)).astype(BF16)
                p_lanes = jnp.concatenate([pn[BLOCK * h:BLOCK * (h + 1)] for h in range(HEADS_PER_BLOCK)], axis=1)
                v4 = jnp.concatenate([jnp.where(mk, vb, jnp.zeros_like(vb)) for mk in kv_masks], axis=0)
                o_ref[bi, :, cols] = _nn(p_lanes, v4)
                lse_ref[bi, :, cols] = _unstack_heads(
                    jnp.broadcast_to(m + jnp.log(den), (HEADS_PER_BLOCK * BLOCK, 256)), masks)

    in_specs = _att_in_specs(cf, bsz)
    args = [zf] * (cf.nq + 4) + [bias] * cf.nq + ([sinks] * cf.nq if cf.sinks else [])
    out = pl.BlockSpec((bsz, BLOCK, cf.wq), lambda r, j: (0, j, r))
    shape = jax.ShapeDtypeStruct((bsz, l, cf.d * cf.wq), F32)
    return _call(
        body, name=name, grid=(cf.d, nb), in_specs=in_specs, out_specs=[out, out], out_shape=[shape, shape],
        sem=("parallel", "arbitrary"), args=args, hosted=hosted)


def _attn_bwd(cf, zf, bias, sinks, stats, dy, name, hosted=()):
    bsz, l, _ = zf.shape
    nb = l // BLOCK

    def body(*refs):
        refs = list(refs)
        q_refs = [refs.pop(0) for _ in range(cf.nq)]
        kp_ref, kc_ref, vp_ref, vc_ref = [refs.pop(0) for _ in range(4)]
        bias_refs = [refs.pop(0) for _ in range(cf.nq)]
        sink_refs = [refs.pop(0) for _ in range(cf.nq)] if cf.sinks else None
        st_ref, dy_ref, dq_ref, dk_ref, dv_ref, dbias_ref = [refs.pop(0) for _ in range(6)]
        dsink_ref = refs.pop(0) if cf.sinks else None
        dk_acc, dv_acc = refs
        r, j = pl.program_id(0), pl.program_id(1)

        @pl.when((r == 0) & (j == 0))
        def _():
            dbias_ref[...] = jnp.zeros_like(dbias_ref)
            if cf.sinks:
                dsink_ref[...] = jnp.zeros_like(dsink_ref)

        @pl.when(j == 0)
        def _():
            dk_acc[...] = jnp.zeros_like(dk_acc)
            dv_acc[...] = jnp.zeros_like(dv_acc)

        masks = _head_masks(BLOCK)
        first_half = lax.broadcasted_iota(jnp.int32, (BLOCK, 256), 1) % HEAD_DIM < HEAD_DIM // 2
        cur = pl.ds(pl.multiple_of(j * BLOCK, BLOCK), BLOCK)
        prev = pl.ds(pl.multiple_of(jnp.maximum(j - 1, 0) * BLOCK, BLOCK), BLOCK)
        sinks4 = [_row_value(ref[...]) for ref in sink_refs] if cf.sinks else None
        ds_sum, dsink_sum = [None] * cf.nq, [None] * cf.nq
        for bi in range(bsz):
            k = jnp.concatenate([kp_ref[bi], kc_ref[bi]], axis=0)
            v = jnp.concatenate([vp_ref[bi], vc_ref[bi]], axis=0)
            dk_blocks, dv_blocks = [], []
            for qb in range(cf.nq):
                cols = slice(256 * qb, 256 * (qb + 1))
                kb, vb = _kv_operand(cf, k, qb), _kv_operand(cf, v, qb)
                q4 = _stack_heads(q_refs[qb][bi] * (HEAD_DIM ** -0.5), masks)
                st = st_ref[bi, :, cols]
                lt4 = jnp.concatenate([_row_value(jnp.where(mk & first_half, st, NEG)) for mk in masks], axis=0)
                e4 = jnp.concatenate([_row_value(jnp.where(mk & ~first_half, st, NEG)) for mk in masks], axis=0)
                pa = jnp.exp(_nt(q4, kb) + bias_refs[qb][...] - lt4)
                dy = dy_ref[bi, :, cols]
                dy4 = _stack_heads(dy.astype(BF16), masks)
                ds = pa * (_nt(dy4, vb) - e4)
                ds_sum[qb] = ds if ds_sum[qb] is None else ds_sum[qb] + ds
                if cf.sinks:
                    dsk = jnp.exp(sinks4[qb] - lt4) * e4
                    dsink_sum[qb] = dsk if dsink_sum[qb] is None else dsink_sum[qb] + dsk
                dsb = ds.astype(BF16)
                dq_ref[bi, :, cols] = (_unstack_heads(_nn(dsb, kb), masks) * (HEAD_DIM ** -0.5)).astype(dq_ref.dtype)
                dk_blocks.append(_tn(dsb, q4))
                dv_blocks.append(_tn(pa.astype(BF16), dy4))
            dk_new, dv_new = _kv_fold(cf, dk_blocks), _kv_fold(cf, dv_blocks)
            dk_acc[bi, cur, :] += dk_new[BLOCK:]
            dv_acc[bi, cur, :] += dv_new[BLOCK:]
            dk_acc[bi, prev, :] += dk_new[:BLOCK]
            dv_acc[bi, prev, :] += dv_new[:BLOCK]
        for qb in range(cf.nq):
            rows = slice(HEADS_PER_BLOCK * BLOCK * qb, HEADS_PER_BLOCK * BLOCK * (qb + 1))
            dbias_ref[rows, :] += ds_sum[qb]
            if cf.sinks:
                dsink_ref[rows, :] -= dsink_sum[qb]

        @pl.when(j == nb - 1)
        def _():
            dk_ref[...] = dk_acc[...].astype(dk_ref.dtype)
            dv_ref[...] = dv_acc[...].astype(dv_ref.dtype)

    tok = pl.BlockSpec((bsz, BLOCK, cf.wq), lambda r, j: (0, j, r))
    in_specs = _att_in_specs(cf, bsz) + [tok] * 2
    args = [zf] * (cf.nq + 4) + [bias] * cf.nq + ([sinks] * cf.nq if cf.sinks else []) + [stats, dy]
    seq = pl.BlockSpec((bsz, l, cf.wkv), lambda r, j: (0, 0, r))
    out_specs = [tok, seq, seq, pl.BlockSpec((cf.nh * BLOCK, 2 * BLOCK), lambda r, j: (0, 0))]
    grad_dtype = BF16 if cf.d == 1 else F32
    out_shape = [jax.ShapeDtypeStruct((bsz, l, cf.d * cf.wq), grad_dtype),
                 jax.ShapeDtypeStruct((bsz, l, cf.d * cf.wkv), grad_dtype),
                 jax.ShapeDtypeStruct((bsz, l, cf.d * cf.wkv), grad_dtype),
                 jax.ShapeDtypeStruct((cf.nh * BLOCK, 2 * BLOCK), F32)]
    if cf.sinks:
        out_specs.append(pl.BlockSpec((cf.nh * BLOCK, 128), lambda r, j: (0, 0)))
        out_shape.append(jax.ShapeDtypeStruct((cf.nh * BLOCK, 128), F32))
    return _call(
        body, name=name, grid=(cf.d, nb), in_specs=in_specs, out_specs=out_specs, out_shape=out_shape,
        scratch=[pltpu.VMEM((bsz, l, cf.wkv), F32), pltpu.VMEM((bsz, l, cf.wkv), F32)],
        sem=("arbitrary", "arbitrary"), args=args, hosted=hosted)


def _merge_fwd(o_a, lse_a, o_b, zg, h, wa_t, wb_t, wout, name):
    t = h.shape[0]
    tm = 512

    def body(o1, o2, o3, l1, l2, l3, ob_ref, zg_ref, h_ref, wa_ref, wb_ref, wo_ref, h2_ref, ya_ref, lt_ref, *slabs):
        o = [_unfold(ref, slabs[i], d) for i, (ref, d) in enumerate(zip((o1, o2, o3), DILATIONS))]
        l = [_unfold(ref, slabs[3 + i], d) for i, (ref, d) in enumerate(zip((l1, l2, l3), DILATIONS))]
        m = jnp.maximum(jnp.maximum(l[0], l[1]), l[2])
        e1, e2, e3 = jnp.exp(l[0] - m), jnp.exp(l[1] - m), jnp.exp(l[2] - m)
        se = e1 + e2 + e3
        ya = (e1 / se) * o[0] + (e2 / se) * o[1] + (e3 / se) * o[2]
        ya_ref[...] = ya
        lt_ref[...] = m + jnp.log(se)
        pa = _nt(ya.astype(BF16), wa_ref[...])
        pb = _nt(ob_ref[...].astype(BF16), wb_ref[...])
        merged = (jax.nn.sigmoid(zg_ref[:, :D_MODEL].astype(F32)) * pa
                  + jax.nn.sigmoid(zg_ref[:, D_MODEL:].astype(F32)) * pb)
        h2_ref[...] = h_ref[...] + _nn(merged.astype(BF16), wo_ref[...])

    row = lambda w: pl.BlockSpec((tm, w), lambda i: (i, 0))
    folded = [pl.BlockSpec((tm // d, d * 256), lambda i: (i, 0)) for d in DILATIONS]
    full = _resident
    return pl.pallas_call(
        body, name=name, grid=(t // tm,),
        in_specs=folded + folded + [row(512), row(GATE_W), row(D_MODEL), full(wa_t), full(wb_t), full(wout)],
        out_specs=[row(D_MODEL), row(256), row(256)],
        out_shape=[jax.ShapeDtypeStruct((t, D_MODEL), F32), jax.ShapeDtypeStruct((t, 256), F32),
                   jax.ShapeDtypeStruct((t, 256), F32)],
        scratch_shapes=[pltpu.VMEM((2, tm, 128), F32)] * 6,
        compiler_params=_params("parallel"),
    )(*o_a, *lse_a, o_b, zg, h, wa_t, wb_t, wout)


def _head_stats(lt, dy, y):
    lane = lax.broadcasted_iota(jnp.int32, dy.shape, 1)
    prod = dy * y
    dot = jnp.zeros_like(prod)
    for hd in range(dy.shape[1] // HEAD_DIM):
        mine = lane // HEAD_DIM == hd
        dot = jnp.where(mine, jnp.sum(jnp.where(mine, prod, 0.0), axis=-1, keepdims=True), dot)
    return jnp.where(lane % HEAD_DIM < HEAD_DIM // 2, lt, dot)


def _merge_bwd(dh, ya, lse_tot, o_b, lse_b, zg, wa_t, wb_t, wout, name, hosted=()):
    t = dh.shape[0]
    tm = 256

    def body(dh_ref, ya_ref, lt_ref, ob_ref, lb_ref, zg_ref, wa_ref, wb_ref, wo_ref,
             mg_ref, dpa_ref, dpb_ref, dzg_ref, dy1, dy2, dy3, st1, st2, st3, dyb_ref, stb_ref, *slabs):
        dm = _nt(dh_ref[...].astype(BF16), wo_ref[...])
        pa = _nt(ya_ref[...].astype(BF16), wa_ref[...])
        pb = _nt(ob_ref[...].astype(BF16), wb_ref[...])
        sa = jax.nn.sigmoid(zg_ref[:, :D_MODEL].astype(F32))
        sb = jax.nn.sigmoid(zg_ref[:, D_MODEL:].astype(F32))
        mg_ref[...] = (sa * pa + sb * pb).astype(BF16)
        dpa = (dm * sa).astype(BF16)
        dpb = (dm * sb).astype(BF16)
        dpa_ref[...] = dpa
        dpb_ref[...] = dpb
        dzg_ref[:, :D_MODEL] = (dm * pa * (sa * (1.0 - sa))).astype(BF16)
        dzg_ref[:, D_MODEL:] = (dm * pb * (sb * (1.0 - sb))).astype(BF16)
        dya = _nn(dpa, wa_ref[...])
        dyb = _nn(dpb, wb_ref[...])
        stats = _head_stats(lt_ref[...], dya, ya_ref[...])
        for i, d in enumerate(DILATIONS):
            _fold(dya, slabs[i], (dy1, dy2, dy3)[i], d)
            _fold(stats, slabs[3 + i], (st1, st2, st3)[i], d)
        dyb_ref[...] = dyb
        stb_ref[...] = _head_stats(lb_ref[...], dyb, ob_ref[...])

    row = lambda w: pl.BlockSpec((tm, w), lambda i: (i, 0))
    folded = [pl.BlockSpec((tm // d, d * 256), lambda i: (i, 0)) for d in DILATIONS]
    full = _resident
    sds = jax.ShapeDtypeStruct
    fshape = [sds((t // d, d * 256), F32) for d in DILATIONS]
    return _call(
        body, name=name, grid=(t // tm,),
        in_specs=[row(D_MODEL), row(256), row(256), row(512), row(512), row(GATE_W), full(wa_t), full(wb_t),
                  full(wout)],
        out_specs=[row(D_MODEL), row(D_MODEL), row(D_MODEL), row(GATE_W)] + folded + folded + [row(512), row(512)],
        out_shape=[sds((t, D_MODEL), BF16), sds((t, D_MODEL), BF16), sds((t, D_MODEL), BF16),
                   sds((t, GATE_W), BF16)] + fshape + fshape + [sds((t, 512), F32), sds((t, 512), F32)],
        scratch=[pltpu.VMEM((2, tm, 128), F32)] * 6,
        sem=("parallel",), args=(dh, ya, lse_tot, o_b, lse_b, zg, wa_t, wb_t, wout), hosted=hosted)


def _pair_sum(grad, got, name):
    _, _, r, cdim = grad.shape
    core = lax.axis_index("c").astype(jnp.int32).reshape(1)

    def body(core_ref, g_ref, s_ref, o_ref):
        o_ref[...] = (g_ref[...] + s_ref[...]).astype(BF16)

    return pl.pallas_call(
        body, name=name,
        grid_spec=pltpu.PrefetchScalarGridSpec(
            num_scalar_prefetch=1, grid=(N_CHIP,),
            in_specs=[pl.BlockSpec((None, None, r, cdim), lambda q, core_ref: (q, core_ref[0], 0, 0)),
                      pl.BlockSpec((None, None, r, cdim), lambda q, core_ref: (q, 0, 0, 0))],
            out_specs=pl.BlockSpec((None, r, cdim), lambda q, core_ref: (q, 0, 0))),
        out_shape=jax.ShapeDtypeStruct((N_CHIP, r, cdim), BF16),
        compiler_params=_params("parallel"),
    )(core, grad, got)


def _adamw_math(w, g, m, v):
    m = ADAM_B1 * m + (1.0 - ADAM_B1) * g
    v = ADAM_B2 * v + (1.0 - ADAM_B2) * jnp.square(g)
    m_hat = m / (1.0 - ADAM_B1 ** ADAM_STEP)
    v_hat = v / (1.0 - ADAM_B2 ** ADAM_STEP)
    delta = -ADAM_LR * (m_hat / (jnp.sqrt(v_hat) + ADAM_EPS) + ADAM_WD * w)
    return delta, m, v


def _update(w, parts, m, v, transposed, name):
    r, c = parts.shape[1:]

    def body(w_ref, p_ref, m_ref, v_ref, g_ref, d_ref, m2_ref, v2_ref):
        def total(rows):
            return ((p_ref[0, rows].astype(F32) + p_ref[1, rows].astype(F32)) + p_ref[2, rows].astype(F32)) \
                + p_ref[3, rows].astype(F32)

        def update(at, g):
            g_ref[at] = g
            d_ref[at], m2_ref[at], v2_ref[at] = _adamw_math(w_ref[at], g, m_ref[at], v_ref[at])

        if not transposed:
            update((slice(None), slice(None)), total(slice(None)))
            return
        for r0 in range(0, r, 128):
            n = min(128, r - r0)
            gt = total(slice(r0, r0 + n))
            if n < 128:
                gt = jnp.concatenate([gt, jnp.zeros((128 - n, c), F32)], axis=0)
            update((slice(None), slice(r0, r0 + n)), gt.T[:, :n])

    sds = jax.ShapeDtypeStruct(w.shape, F32)
    return pl.pallas_call(body, name=name, out_shape=[sds] * 4,
                          compiler_params=pltpu.CompilerParams(vmem_limit_bytes=VMEM_LIMIT))(w, parts, m, v)


SMALL_ROWS = 80


def _small_update(g, w, m, v, name):
    def body(g_ref, w_ref, m_ref, v_ref, gs_ref, d_ref, m2_ref, v2_ref, got_ref, send_sems, recv_sems):
        x, y, c = _place()
        me = 4 * x + 2 * y + c
        got_ref[me] = g_ref[...]
        copies = []
        for k in range(1, N_DEV):
            peer = (x ^ (k >> 2), y ^ ((k >> 1) & 1), c ^ (k & 1))
            cp = pltpu.make_async_remote_copy(
                src_ref=g_ref, dst_ref=got_ref.at[me], send_sem=send_sems.at[k - 1], recv_sem=recv_sems.at[k - 1],
                device_id=peer, device_id_type=MESH)
            cp.start()
            copies.append(cp)
        for cp in copies:
            cp.wait()
        total = got_ref[0]
        for k in range(1, N_DEV):
            total = total + got_ref[k]
        gs_ref[...] = total
        d_ref[...], m2_ref[...], v2_ref[...] = _adamw_math(w_ref[...], total, m_ref[...], v_ref[...])

    sds = jax.ShapeDtypeStruct((SMALL_ROWS, 128), F32)
    vm = pl.BlockSpec(memory_space=pltpu.VMEM)
    return pl.pallas_call(
        body, name=name, in_specs=[vm] * 4, out_specs=[vm] * 4, out_shape=[sds] * 4,
        scratch_shapes=[pltpu.VMEM((N_DEV, SMALL_ROWS, 128), F32), pltpu.SemaphoreType.DMA((N_DEV - 1,)),
                        pltpu.SemaphoreType.DMA((N_DEV - 1,))],
    )(g, w, m, v)


def _pack_small(gains, b_in, rel_bias, sinks, last):
    rows = [a.reshape(8, 128) for a in gains] + [b_in.reshape(40, 128), rel_bias.reshape(5, 128),
                                                 jnp.pad(sinks.reshape(1, 8), ((0, 0), (0, 120))), last]
    rows.append(jnp.zeros((SMALL_ROWS - 79, 128), F32))
    return jnp.concatenate(rows, axis=0)


def _unpack_small(p, like):
    out = [p[8 * i:8 * i + 8].reshape(like[i].shape) for i in range(4)]
    out.append(p[32:72].reshape(like[4].shape))
    out.append(p[72:77].reshape(like[5].shape))
    out.append(p[77, :8].reshape(like[6].shape))
    return out


def kernel(x, ffn1_norm, ffn1_w_gate, ffn1_w_up, ffn1_w_down, mix_norm, w_in, b_in, w_branch_a, w_branch_b, w_out, sinks, rel_bias, ffn2_norm, ffn2_w_gate, ffn2_w_up, ffn2_w_down, final_norm, loss_target, m_ffn1_norm, m_ffn1_w_gate, m_ffn1_w_up, m_ffn1_w_down, m_mix_norm, m_w_in, m_b_in, m_w_branch_a, m_w_branch_b, m_w_out, m_sinks, m_rel_bias, m_ffn2_norm, m_ffn2_w_gate, m_ffn2_w_up, m_ffn2_w_down, m_final_norm, v_ffn1_norm, v_ffn1_w_gate, v_ffn1_w_up, v_ffn1_w_down, v_mix_norm, v_w_in, v_b_in, v_w_branch_a, v_w_branch_b, v_w_out, v_sinks, v_rel_bias, v_ffn2_norm, v_ffn2_w_gate, v_ffn2_w_up, v_ffn2_w_down, v_final_norm):
    bsz, seq, _ = x.shape
    t = bsz * seq
    xt = x.reshape(t, D_MODEL)
    target = loss_target.reshape(t, D_MODEL)

    big = [("ffn1_w_gate", ffn1_w_gate, m_ffn1_w_gate, v_ffn1_w_gate, True),
           ("ffn1_w_up", ffn1_w_up, m_ffn1_w_up, v_ffn1_w_up, True),
           ("ffn1_w_down", ffn1_w_down, m_ffn1_w_down, v_ffn1_w_down, False),
           ("w_in", w_in, m_w_in, v_w_in, True),
           ("w_branch_a", w_branch_a, m_w_branch_a, v_w_branch_a, True),
           ("w_branch_b", w_branch_b, m_w_branch_b, v_w_branch_b, True),
           ("w_out", w_out, m_w_out, v_w_out, False),
           ("ffn2_w_gate", ffn2_w_gate, m_ffn2_w_gate, v_ffn2_w_gate, True),
           ("ffn2_w_up", ffn2_w_up, m_ffn2_w_up, v_ffn2_w_up, True),
           ("ffn2_w_down", ffn2_w_down, m_ffn2_w_down, v_ffn2_w_down, False)]
    shard = {nm: (w[0].T if tr else w[0]).astype(BF16) for nm, w, _, _, tr in big}
    full = {}

    def gather(names):
        return _Gather([shard[nm] for nm in names])

    def keep(names, got):
        for nm, gw in zip(names, got):
            full[nm] = gw.reshape(-1, gw.shape[-1])

    ffn1_names = ["ffn1_w_gate", "ffn1_w_up", "ffn1_w_down"]
    mix_names = ["w_in", "w_branch_a", "w_branch_b", "w_out"]
    ffn2_names = ["ffn2_w_gate", "ffn2_w_up", "ffn2_w_down"]
    g1, gm, g2, gf = ffn1_norm, mix_norm, ffn2_norm, final_norm.reshape(1, D_MODEL)

    keep(ffn1_names, _exchange([gather(ffn1_names)], "gather_ffn1")[0])
    (h1, n1, a1, b1, hff1), (got,) = _ffn_fwd(xt, g1, full["ffn1_w_gate"], full["ffn1_w_up"], full["ffn1_w_down"], "ffn1_fwd",
                                hosted=[gather(mix_names)])
    keep(mix_names, got)
    (u, zq, zg, *zdil), (got,) = _inproj_fwd(h1, gm, full["w_in"], b_in, "inproj_fwd", hosted=[gather(ffn2_names[:2])])
    keep(ffn2_names[:2], got)
    buckets = _bucket_tiles()
    bias = _bias_build(rel_bias, buckets)
    sink_rows = jnp.broadcast_to(sinks.reshape(8, 1, 1), (8, BLOCK, 128)).reshape(8 * BLOCK, 128)
    cfs = [_Att(i) for i in range(4)]
    zfold = []
    for i, cf in enumerate(cfs):
        if cf.d == 1:
            zfold.append(zq.reshape(bsz, seq, QKV_W))
        else:
            zfold.append(zdil[i - 1].reshape(bsz, seq // cf.d, cf.d * cf.row_w))
    att = [None] * 4
    for i in (3, 0, 1, 2):
        cf = cfs[i]
        (o, lse), got = _attn_fwd(cf, zfold[i], bias, sink_rows, f"attn{i}_fwd",
                                  hosted=[gather(ffn2_names[2:])] if i == 3 else ())
        if i == 3:
            keep(ffn2_names[2:], got[0])
        att[i] = (o.reshape(t // cf.d, cf.d * cf.wq), lse.reshape(t // cf.d, cf.d * cf.wq))
    o_b, lse_b = att[3]
    h2, ya, lse_tot = _merge_fwd([a[0] for a in att[:3]], [a[1] for a in att[:3]], o_b, zg, h1,
                                 full["w_branch_a"], full["w_branch_b"], full["w_out"], "merge_fwd")
    (dh3, n2, a2, b2, hff2, loss_part, dgf), _ = _ffn_fwd(
        h2, g2, full["ffn2_w_gate"], full["ffn2_w_up"], full["ffn2_w_down"], "ffn2_fwd", head=(gf, target))

    grads, pair, from_chips = {}, {}, {}

    def by_owner(nm):
        return grads[nm].reshape(N_CHIP, 2, -1, grads[nm].shape[-1])

    def to_core(names):
        return _CoreExchange([by_owner(nm) for nm in names])

    def pair_up(names, got):
        for nm, sib in zip(names, got):
            pair[nm] = _pair_sum(by_owner(nm), sib, f"pair_sum_{nm}")

    def to_chips(names):
        return _ChipExchange([pair[nm] for nm in names])

    def landed(names, got):
        for nm, parts in zip(names, got):
            from_chips[nm] = parts

    dh2, dg2, da, db = _ffn_bwd(h2, a2, b2, g2, dh3, full["ffn2_w_gate"], full["ffn2_w_up"],
                                          full["ffn2_w_down"], "ffn2_bwd")
    grads["ffn2_w_gate"], _ = _tn_matmul(da, n2, 1408, "ffn2_dgate")
    grads["ffn2_w_up"], _ = _tn_matmul(db, n2, 1408, "ffn2_dup")
    grads["ffn2_w_down"], _ = _tn_matmul(hff2, dh3, 1408, "ffn2_ddown", scale=0.5)
    (merged, dpa, dpb, dzg, *cot), (got,) = _merge_bwd(
        dh2, ya, lse_tot, o_b, lse_b, zg, full["w_branch_a"], full["w_branch_b"], full["w_out"], "merge_bwd",
        hosted=[to_core(ffn2_names)])
    dys, sts = cot[0:3] + [cot[6]], cot[3:6] + [cot[7]]
    pair_up(ffn2_names, got)
    dq, dk, dv, dbias, dsink = [None] * 4, [None] * 4, [None] * 4, [None] * 4, None
    for i in (3, 0, 1, 2):
        cf = cfs[i]
        shp = (bsz, seq // cf.d, cf.d * cf.wq)
        hosted = {3: lambda: [to_chips(ffn2_names[:2])], 0: lambda: [to_chips(ffn2_names[2:])]}.get(i, list)()
        res, got = _attn_bwd(cf, zfold[i], bias, sink_rows, sts[i].reshape(shp), dys[i].reshape(shp),
                             f"attn{i}_bwd", hosted=hosted)
        if i == 3:
            landed(ffn2_names[:2], got[0])
        elif i == 0:
            landed(ffn2_names[2:], got[0])
        dq[i] = res[0].reshape(t // cf.d, cf.d * cf.wq)
        dk[i] = res[1].reshape(t // cf.d, cf.d * cf.wkv)
        dv[i] = res[2].reshape(t // cf.d, cf.d * cf.wkv)
        dbias[i] = res[3]
        if cf.sinks:
            dsink = res[4]
    pieces = dq[:3] + dk[:3] + dv[:3] + [dq[3], dk[3], dv[3], dzg]
    dh1, dz, db_in, dgm = _inproj_bwd(pieces, h1, gm, dh2, full["w_in"], "inproj_bwd")
    dx, dg1, da, db = _ffn_bwd(xt, a1, b1, g1, dh1, full["ffn1_w_gate"], full["ffn1_w_up"],
                                         full["ffn1_w_down"], "ffn1_bwd")
    grads["w_in"], _ = _tn_matmul(dz, u, 1280, "dw_in")
    grads["ffn1_w_down"], (got,) = _tn_matmul(hff1, dh1, 1408, "ffn1_ddown", hosted=[to_core(["w_in"])], scale=0.5)
    pair_up(["w_in"], got)
    grads["ffn1_w_gate"], got = _tn_matmul(da, n1, 1408, "ffn1_dgate",
                                           hosted=[to_chips(["w_in"]), to_core(["ffn1_w_down"])])
    landed(["w_in"], got[0])
    pair_up(["ffn1_w_down"], got[1])
    grads["ffn1_w_up"], got = _tn_matmul(db, n1, 1408, "ffn1_dup",
                                         hosted=[to_chips(["ffn1_w_down"]), to_core(["ffn1_w_gate"])])
    landed(["ffn1_w_down"], got[0])
    pair_up(["ffn1_w_gate"], got[1])
    grads["w_out"], got = _tn_matmul(merged, dh2, 1024, "dw_out",
                                     hosted=[to_chips(["ffn1_w_gate"]), to_core(["ffn1_w_up"])])
    landed(["ffn1_w_gate"], got[0])
    pair_up(["ffn1_w_up"], got[1])
    grads["w_branch_b"], got = _tn_matmul(dpb, o_b, 1024, "dw_branch_b",
                                          hosted=[to_chips(["ffn1_w_up"]), to_core(["w_out"])])
    landed(["ffn1_w_up"], got[0])
    pair_up(["w_out"], got[1])
    grads["w_branch_a"], got = _tn_matmul(dpa, ya, 1024, "dw_branch_a",
                                          hosted=[to_chips(["w_out"]), to_core(["w_branch_b"])])
    landed(["w_out"], got[0])
    pair_up(["w_branch_b"], got[1])
    got = _exchange([to_chips(["w_branch_b"]), to_core(["w_branch_a"])], "reduce_scatter_tail1")
    landed(["w_branch_b"], got[0])
    pair_up(["w_branch_a"], got[1])
    landed(["w_branch_a"], _exchange([to_chips(["w_branch_a"])], "reduce_scatter_tail2")[0])
    dtable, dsinks = _bias_reduce(jnp.concatenate(dbias, axis=0), buckets, dsink)

    out_g, out_d, out_m, out_v = {}, {}, {}, {}
    for nm, w, m, v, tr in big:
        around = tr and w.shape[-1] % 128 != 0
        wmv = [a[0].T if around else a[0] for a in (w, m, v)]
        res = _update(wmv[0], from_chips[nm], wmv[1], wmv[2], tr and not around, f"update_{nm}")
        out_g[nm], out_d[nm], out_m[nm], out_v[nm] = [(a.T if around else a)[None] for a in res]

    small = [("ffn1_norm", ffn1_norm, m_ffn1_norm, v_ffn1_norm), ("mix_norm", mix_norm, m_mix_norm, v_mix_norm),
             ("ffn2_norm", ffn2_norm, m_ffn2_norm, v_ffn2_norm), ("final_norm", final_norm, m_final_norm, v_final_norm),
             ("b_in", b_in, m_b_in, v_b_in), ("rel_bias", rel_bias, m_rel_bias, v_rel_bias),
             ("sinks", sinks, m_sinks, v_sinks)]
    zero_row = jnp.zeros((1, 128), F32)
    pack = lambda arrs, last: _pack_small(arrs[:4], arrs[4], arrs[5], arrs[6], last)
    g_small = pack([dg1, dgm, dg2, dgf, db_in, dtable[:, :TOTAL_HEADS], dsinks[:, 0]], loss_part)
    packed = [pack([s[k] for s in small], zero_row) for k in (1, 2, 3)]
    gs, ds, ms, vs = _small_update(g_small, *packed, "small_update")
    like = [s[1] for s in small]
    for nm_s, g_, d_, m_, v_ in zip([s[0] for s in small], _unpack_small(gs, like), _unpack_small(ds, like),
                                    _unpack_small(ms, like), _unpack_small(vs, like)):
        out_g[nm_s], out_d[nm_s], out_m[nm_s], out_v[nm_s] = g_, d_, m_, v_
    loss = gs[78, 0]

    order = ["ffn1_norm", "ffn1_w_gate", "ffn1_w_up", "ffn1_w_down", "mix_norm", "w_in", "b_in", "w_branch_a",
             "w_branch_b", "w_out", "sinks", "rel_bias", "ffn2_norm", "ffn2_w_gate", "ffn2_w_up", "ffn2_w_down",
             "final_norm"]
    return (loss, dx.reshape(x.shape), *[out_g[k] for k in order], *[out_d[k] for k in order],
            *[out_m[k] for k in order], *[out_v[k] for k in order])
```

```python
import functools
import math

import numpy as np
import jax
import jax.numpy as jnp
from jax import lax
from jax.experimental import pallas as pl
from jax.experimental.pallas import tpu as pltpu

D_MODEL = 1024
D_FF = 2816
FF_CHUNK = 256
HEAD_DIM = 64
BLOCK = 128
N_BUCKETS = 32
MAX_DISTANCE = 2048
A_HEADS = 12
TOTAL_HEADS = 20
DIL_GROUPS = ((128, 1), (512, 4), (2048, 16))
B_WINDOW = 128
QKV_W = 3072
GATE_W = 2048
D_IN = QKV_W + GATE_W
EPS = 1e-6
NEG = -1e30
N_DEV = 8
N_CHIP = 4
ADAM_LR, ADAM_B1, ADAM_B2, ADAM_EPS, ADAM_WD, ADAM_STEP = 0.001, 0.9, 0.999, 1e-08, 0.01, 10
VMEM_LIMIT = 56 * 1024 * 1024
MESH = pl.DeviceIdType.MESH
BF16 = jnp.bfloat16
F32 = jnp.float32
ANY = pl.BlockSpec(memory_space=pl.ANY)


def _params(*sem):
    return pltpu.CompilerParams(dimension_semantics=sem, vmem_limit_bytes=VMEM_LIMIT)


def _resident(a):
    return pl.BlockSpec(a.shape, lambda i: (0, 0), pipeline_mode=pl.Buffered(1))


def _place():
    return lax.axis_index("x"), lax.axis_index("y"), lax.axis_index("c")


class _Gather:
    def __init__(self, shards):
        self.ins = list(shards)
        n = self.n = len(shards)
        self.out_shape = [jax.ShapeDtypeStruct((N_DEV,) + s.shape, s.dtype) for s in shards]
        self.scratch = [pltpu.SemaphoreType.DMA((7 * n,)), pltpu.SemaphoreType.DMA((7 * n,)),
                        pltpu.SemaphoreType.DMA((n,))]

    def _copies(self, ins, outs, sems):
        send_sems, recv_sems, local_sems = sems
        x, y, c = _place()
        me, sibling = (x, y, c), (x, y, 1 - c)
        chips = [(1 - x, y), (x, 1 - y), (1 - x, 1 - y)]

        def copy(i, k, block, to, src=None):
            dst = outs[i].at[4 * block[0] + 2 * block[1] + block[2]]
            return pltpu.make_async_remote_copy(
                src_ref=dst if src is None else src, dst_ref=dst, send_sem=send_sems.at[7 * i + k],
                recv_sem=recv_sems.at[7 * i + k], device_id=to, device_id_type=MESH)

        n = self.n
        mine = [pltpu.make_async_copy(ins[i], outs[i].at[4 * x + 2 * y + c], local_sems.at[i]) for i in range(n)]
        first = [copy(i, 0, me, sibling, src=ins[i]) for i in range(n)]
        first += [copy(i, 1 + j, me, (*chip, c), src=ins[i]) for i in range(n) for j, chip in enumerate(chips)]
        landed = [copy(i, 1 + j, (*chip, c), me) for j, chip in enumerate(chips) for i in range(n)]
        passed = [copy(i, 4 + j, (*chip, c), sibling) for j, chip in enumerate(chips) for i in range(n)]
        from_sibling = [copy(i, 0, sibling, me) for i in range(n)]
        from_sibling += [copy(i, 4 + j, (*chip, 1 - c), me) for i in range(n) for j, chip in enumerate(chips)]
        return mine, first, landed, passed, from_sibling

    def start(self, ins, outs, sems):
        mine, first, _, _, _ = self._copies(ins, outs, sems)
        for cp in mine + first:
            cp.start()

    def mid(self, ins, outs, sems):
        _, _, landed, passed, _ = self._copies(ins, outs, sems)
        for got, fwd in zip(landed, passed):
            got.wait_recv()
            fwd.start()

    def end(self, ins, outs, sems):
        mine, first, _, passed, from_sibling = self._copies(ins, outs, sems)
        for cp in from_sibling:
            cp.wait_recv()
        for cp in first + passed:
            cp.wait_send()
        for cp in mine:
            cp.wait()


class _CoreExchange:
    def __init__(self, grads):
        self.ins = list(grads)
        n = self.n = len(grads)
        self.out_shape = [jax.ShapeDtypeStruct((N_CHIP, 1) + g.shape[2:], g.dtype) for g in grads]
        self.scratch = [pltpu.SemaphoreType.DMA((n,)), pltpu.SemaphoreType.DMA((n,))]

    def _copies(self, ins, outs, sems):
        x, y, c = _place()
        return [pltpu.make_async_remote_copy(
            src_ref=ins[i].at[:, pl.ds(1 - c, 1)], dst_ref=outs[i], send_sem=sems[0].at[i],
            recv_sem=sems[1].at[i], device_id=(x, y, 1 - c), device_id_type=MESH) for i in range(self.n)]

    def start(self, ins, outs, sems):
        for cp in self._copies(ins, outs, sems):
            cp.start()

    mid = None

    def end(self, ins, outs, sems):
        for cp in self._copies(ins, outs, sems):
            cp.wait()


class _ChipExchange:
    def __init__(self, parts):
        self.ins = list(parts)
        n = self.n = len(parts)
        self.out_shape = [jax.ShapeDtypeStruct(p.shape, p.dtype) for p in parts]
        self.scratch = [pltpu.SemaphoreType.DMA((3 * n,)), pltpu.SemaphoreType.DMA((3 * n,)),
                        pltpu.SemaphoreType.DMA((n,))]

    def _copies(self, ins, outs, sems):
        send_sems, recv_sems, local_sems = sems
        x, y, c = _place()
        my_chip = 2 * x + y
        copies = []
        for i in range(self.n):
            copies.append(pltpu.make_async_copy(ins[i].at[my_chip], outs[i].at[my_chip], local_sems.at[i]))
            for k, (qx, qy) in enumerate([(1 - x, y), (x, 1 - y), (1 - x, 1 - y)]):
                copies.append(pltpu.make_async_remote_copy(
                    src_ref=ins[i].at[2 * qx + qy], dst_ref=outs[i].at[my_chip], send_sem=send_sems.at[3 * i + k],
                    recv_sem=recv_sems.at[3 * i + k], device_id=(qx, qy, c), device_id_type=MESH))
        return copies

    def start(self, ins, outs, sems):
        for cp in self._copies(ins, outs, sems):
            cp.start()

    mid = None

    def end(self, ins, outs, sems):
        for cp in self._copies(ins, outs, sems):
            cp.wait()


def _call(body, *, name, grid, in_specs, out_specs, out_shape, args, scratch=(), sem=None, hosted=()):
    n_in, n_out, n_scr = len(in_specs), len(out_specs), len(scratch)
    x_in = [len(p.ins) for p in hosted]
    x_scr = [len(p.scratch) for p in hosted]
    steps = int(np.prod(grid))

    def wrapped(*refs):
        refs = list(refs)
        ins, refs = refs[:n_in], refs[n_in:]
        x_ins = [[refs.pop(0) for _ in range(k)] for k in x_in]
        outs, refs = refs[:n_out], refs[n_out:]
        x_outs = [[refs.pop(0) for _ in range(k)] for k in x_in]
        scr, refs = refs[:n_scr], refs[n_scr:]
        x_sems = [[refs.pop(0) for _ in range(k)] for k in x_scr]
        step = 0
        for d in range(len(grid)):
            step = step * grid[d] + pl.program_id(d)

        def phase(which, at):
            fns = [(getattr(p, which), a) for p, a in zip(hosted, zip(x_ins, x_outs, x_sems)) if getattr(p, which)]
            if fns:
                @pl.when(step == at)
                def _():
                    for fn, a in fns:
                        fn(*a)

        phase("start", 0)
        phase("mid", (3 * steps) // 4)
        body(*ins, *outs, *scr)
        phase("end", steps - 1)

    results = pl.pallas_call(
        wrapped, name=name, grid=grid,
        in_specs=list(in_specs) + [ANY] * sum(x_in), out_specs=list(out_specs) + [ANY] * sum(x_in),
        out_shape=list(out_shape) + [s for p in hosted for s in p.out_shape],
        scratch_shapes=list(scratch) + [s for p in hosted for s in p.scratch],
        compiler_params=_params(*(("arbitrary",) * len(grid) if hosted else sem)),
    )(*args, *[a for p in hosted for a in p.ins])
    own, rest = list(results[:n_out]), list(results[n_out:])
    return own, [[rest.pop(0) for _ in range(k)] for k in x_in]


def _exchange(programs, name):
    x_in = [len(p.ins) for p in programs]
    x_scr = [len(p.scratch) for p in programs]

    def body(*refs):
        refs = list(refs)
        x_ins = [[refs.pop(0) for _ in range(k)] for k in x_in]
        x_outs = [[refs.pop(0) for _ in range(k)] for k in x_in]
        x_sems = [[refs.pop(0) for _ in range(k)] for k in x_scr]
        for which in ("start", "mid", "end"):
            for p, a in zip(programs, zip(x_ins, x_outs, x_sems)):
                if getattr(p, which):
                    getattr(p, which)(*a)

    results = list(pl.pallas_call(
        body, name=name, in_specs=[ANY] * sum(x_in), out_specs=[ANY] * sum(x_in),
        out_shape=[s for p in programs for s in p.out_shape],
        scratch_shapes=[s for p in programs for s in p.scratch],
    )(*[a for p in programs for a in p.ins]))
    return [[results.pop(0) for _ in range(k)] for k in x_in]


def _nt(a, b):
    return lax.dot_general(a, b, (((1,), (1,)), ((), ())), preferred_element_type=F32)


def _nn(a, b):
    return lax.dot_general(a, b, (((1,), (0,)), ((), ())), preferred_element_type=F32)


def _tn(a, b):
    return lax.dot_general(a, b, (((0,), (0,)), ((), ())), preferred_element_type=F32)


def _rms(x, g):
    r = lax.rsqrt(jnp.mean(x * x, axis=-1, keepdims=True) + EPS)
    return x * r, r


def _rms_bwd(dn, xhat, r, g):
    dg = jnp.sum(dn * xhat, axis=0, keepdims=True)
    dxh = dn * g
    dx = r * (dxh - xhat * jnp.mean(dxh * xhat, axis=-1, keepdims=True))
    return dx, dg


def _ffn_fwd(x, g, wg_t, wu_t, wd, name, hosted=(), head=None):
    t = x.shape[0]
    tm = 256

    def body(*refs):
        x_ref, g_ref, wg_ref, wu_ref, wd_ref = refs[:5]
        h_ref, n_ref, a_ref, b_ref, hff_ref = refs[-5 if head is None else -7:][:5]
        xhat, _ = _rms(x_ref[...], g_ref[...])
        n = (xhat * g_ref[...]).astype(BF16)
        n_ref[...] = n
        for c in range(0, D_FF, FF_CHUNK):
            cols = slice(c, c + FF_CHUNK)
            a = _nt(n, wg_ref[cols, :])
            b = _nt(n, wu_ref[cols, :])
            a_ref[:, cols] = a.astype(BF16)
            b_ref[:, cols] = b.astype(BF16)
            hff_ref[:, cols] = (a * jax.nn.sigmoid(a) * b).astype(BF16)
        h = x_ref[...] + 0.5 * _nn(hff_ref[...], wd_ref[...])
        if head is None:
            h_ref[...] = h
            return
        gf_ref, t_ref, loss_ref, dgf_ref = refs[5], refs[6], refs[-2], refs[-1]

        @pl.when(pl.program_id(0) == 0)
        def _():
            loss_ref[...] = jnp.zeros_like(loss_ref)
            dgf_ref[...] = jnp.zeros_like(dgf_ref)

        yhat, r = _rms(h, gf_ref[...])
        err = yhat * gf_ref[...] - t_ref[...]
        loss_ref[...] += 0.5 * jnp.sum(jnp.mean(err * err, axis=-1, keepdims=True), axis=0, keepdims=True)
        h_ref[...], dgf = _rms_bwd(err * (1.0 / D_MODEL), yhat, r, gf_ref[...])
        dgf_ref[...] += dgf

    row = pl.BlockSpec((tm, D_MODEL), lambda i: (i, 0))
    hid = pl.BlockSpec((tm, D_FF), lambda i: (i, 0))
    sds = jax.ShapeDtypeStruct
    in_specs = [row, _resident(g), _resident(wg_t), _resident(wu_t), _resident(wd)]
    out_specs = [row, row, hid, hid, hid]
    out_shape = [sds((t, D_MODEL), F32), sds((t, D_MODEL), BF16), sds((t, D_FF), BF16), sds((t, D_FF), BF16),
                 sds((t, D_FF), BF16)]
    args = (x, g, wg_t, wu_t, wd)
    if head is not None:
        in_specs += [_resident(head[0]), row]
        out_specs += [pl.BlockSpec((1, 128), lambda i: (0, 0)), pl.BlockSpec((1, D_MODEL), lambda i: (0, 0))]
        out_shape += [sds((1, 128), F32), sds((1, D_MODEL), F32)]
        args += tuple(head)
    return _call(body, name=name, grid=(t // tm,), in_specs=in_specs, out_specs=out_specs, out_shape=out_shape,
                 sem=("parallel",) if head is None else ("arbitrary",), args=args, hosted=hosted)


def _ffn_bwd(x, a_pre, b_pre, g, dh, wg_t, wu_t, wd, name):
    t = x.shape[0]
    tm = 256

    def body(x_ref, a_ref, b_ref, g_ref, dh_ref, wg_ref, wu_ref, wd_ref,
             dx_ref, dg_ref, da_ref, db_ref):
        @pl.when(pl.program_id(0) == 0)
        def _():
            dg_ref[...] = jnp.zeros_like(dg_ref)

        dhh = (0.5 * dh_ref[...]).astype(BF16)
        for c in range(0, D_FF, FF_CHUNK):
            cols = slice(c, c + FF_CHUNK)
            a = a_ref[:, cols].astype(F32)
            b = b_ref[:, cols].astype(F32)
            s = jax.nn.sigmoid(a)
            silu = a * s
            dhff = _nt(dhh, wd_ref[cols, :])
            da_ref[:, cols] = (dhff * b * (s * (1.0 + a * (1.0 - s)))).astype(BF16)
            db_ref[:, cols] = (dhff * silu).astype(BF16)
        dn = _nn(da_ref[...], wg_ref[...]) + _nn(db_ref[...], wu_ref[...])
        xhat, r = _rms(x_ref[...], g_ref[...])
        dx, dg = _rms_bwd(dn, xhat, r, g_ref[...])
        dx_ref[...] = dh_ref[...] + dx
        dg_ref[...] += dg

    row = pl.BlockSpec((tm, D_MODEL), lambda i: (i, 0))
    hid = pl.BlockSpec((tm, D_FF), lambda i: (i, 0))
    return pl.pallas_call(
        body, name=name, grid=(t // tm,),
        in_specs=[row, hid, hid, _resident(g), row, _resident(wg_t), _resident(wu_t), _resident(wd)],
        out_specs=[row, pl.BlockSpec((1, D_MODEL), lambda i: (0, 0)), hid, hid],
        out_shape=[jax.ShapeDtypeStruct((t, D_MODEL), F32), jax.ShapeDtypeStruct((1, D_MODEL), F32),
                   jax.ShapeDtypeStruct((t, D_FF), BF16), jax.ShapeDtypeStruct((t, D_FF), BF16)],
        compiler_params=_params("arbitrary"),
    )(x, a_pre, b_pre, g, dh, wg_t, wu_t, wd)


def _tn_matmul(a, b, rc, name, hosted=(), scale=None):
    t, r = a.shape
    c = b.shape[1]
    tk = 1024

    def body(a_ref, b_ref, o_ref):
        @pl.when(pl.program_id(1) == 0)
        def _():
            o_ref[...] = jnp.zeros_like(o_ref)

        o_ref[...] += _tn(a_ref[...].astype(BF16), b_ref[...].astype(BF16))
        if scale is not None:
            @pl.when(pl.program_id(1) == t // tk - 1)
            def _():
                o_ref[...] *= scale

    (out,), got = _call(
        body, name=name, grid=(r // rc, t // tk),
        in_specs=[pl.BlockSpec((tk, rc), lambda i, k: (k, i)), pl.BlockSpec((tk, c), lambda i, k: (k, 0))],
        out_specs=[pl.BlockSpec((rc, c), lambda i, k: (i, 0))],
        out_shape=[jax.ShapeDtypeStruct((r, c), F32)],
        sem=("parallel", "arbitrary"), args=(a, b), hosted=hosted)
    return out, got


def _unfold(blk_ref, slab_ref, d):
    if d == 1:
        return blk_ref[...]
    n = blk_ref.shape[0]
    for r in range(d):
        for half in range(2):
            c0 = 256 * r + 128 * half
            slab_ref[half, pl.ds(r, n, stride=d), :] = blk_ref[:, c0:c0 + 128]
    return jnp.concatenate([slab_ref[0], slab_ref[1]], axis=1)


def _fold(x, slab_ref, out_ref, d):
    if d == 1:
        out_ref[...] = x.astype(out_ref.dtype)
        return
    n, w = out_ref.shape[0], x.shape[1]
    for part in range(w // 128):
        slab_ref[part] = x[:, 128 * part:128 * (part + 1)]
    for r in range(d):
        for part in range(w // 128):
            c0 = w * r + 128 * part
            out_ref[:, c0:c0 + 128] = slab_ref[part, pl.ds(r, n, stride=d), :].astype(out_ref.dtype)


DILATIONS = tuple(d for _, d in DIL_GROUPS)


PIECE_W = (256,) * 9 + (512, 128, 128, GATE_W)
PIECE_D = DILATIONS * 3 + (1, 1, 1, 1)


def _inproj_fwd(h, g, w_t, b_in, name, hosted=()):
    t = h.shape[0]
    tm, nc = 512, 256
    dilated = [(gi, d) for gi, d in enumerate(DILATIONS) if d > 1]

    def body(h_ref, g_ref, w_ref, b_ref, u_ref, zq_ref, zg_ref, *rest):
        zf_refs, slabs = rest[:len(dilated)], rest[len(dilated):]
        xhat, _ = _rms(h_ref[...], g_ref[...])
        u = (xhat * g_ref[...]).astype(BF16)
        u_ref[...] = u
        for c in range(D_IN // nc):
            z = _nt(u, w_ref[c * nc:(c + 1) * nc, :]) + b_ref[:, c * nc:(c + 1) * nc]
            if c < QKV_W // nc:
                zq_ref[:, c * nc:(c + 1) * nc] = z.astype(BF16)
            else:
                zg_ref[:, c * nc - QKV_W:(c + 1) * nc - QKV_W] = z.astype(BF16)
            part, gi = divmod(c, len(DILATIONS))
            for k, (gk, d) in enumerate(dilated):
                if part < 3 and gi == gk:
                    slab, n = slabs[3 * k + part], tm // d
                    slab[0] = z[:, :128]
                    slab[1] = z[:, 128:]
                    for r in range(d):
                        for half in range(2):
                            c0 = 768 * r + 256 * part + 128 * half
                            zf_refs[k][:, c0:c0 + 128] = slab[half, pl.ds(r, n, stride=d), :].astype(BF16)

    row = lambda w: pl.BlockSpec((tm, w), lambda i: (i, 0))
    full = _resident
    sds = jax.ShapeDtypeStruct
    return _call(
        body, name=name, grid=(t // tm,),
        in_specs=[row(D_MODEL), full(g), full(w_t), full(b_in)],
        out_specs=[row(D_MODEL), row(QKV_W), row(GATE_W)]
        + [pl.BlockSpec((tm // d, d * 768), lambda i: (i, 0)) for _, d in dilated],
        out_shape=[sds((t, D_MODEL), BF16), sds((t, QKV_W), BF16), sds((t, GATE_W), BF16)]
        + [sds((t // d, d * 768), BF16) for _, d in dilated],
        scratch=[pltpu.VMEM((2, tm, 128), F32)] * (3 * len(dilated)),
        sem=("parallel",), args=(h, g, w_t, b_in), hosted=hosted)


def _inproj_bwd(pieces, h, g, dh_res, w_t, name):
    t = h.shape[0]
    tm = 256
    npiece = len(PIECE_W)
    offs = np.concatenate([[0], np.cumsum(PIECE_W)]).tolist()

    def body(*refs):
        p_refs = refs[:npiece]
        h_ref, g_ref, dhr_ref, w_ref, dh_ref, dz_ref, db_ref, dg_ref = refs[npiece:npiece + 8]
        slabs = list(refs[npiece + 8:])
        i = pl.program_id(0)

        @pl.when(i == 0)
        def _():
            db_ref[...] = jnp.zeros_like(db_ref)
            dg_ref[...] = jnp.zeros_like(dg_ref)

        du = jnp.zeros((tm, D_MODEL), F32)
        for k in range(npiece):
            o, w = offs[k], PIECE_W[k]
            token_order = _unfold(p_refs[k], slabs.pop(), PIECE_D[k]).astype(BF16) if PIECE_D[k] > 1 else None
            for c0 in range(0, w, 512):
                cw = min(512, w - c0)
                pz = p_refs[k][:, c0:c0 + cw] if token_order is None else token_order
                dz_ref[:, o + c0:o + c0 + cw] = pz
                db_ref[:, o + c0:o + c0 + cw] += jnp.sum(pz.astype(F32), axis=0, keepdims=True)
                du = du + _nn(pz, w_ref[o + c0:o + c0 + cw, :])
        xhat, r = _rms(h_ref[...], g_ref[...])
        dx, dg = _rms_bwd(du, xhat, r, g_ref[...])
        dh_ref[...] = dhr_ref[...] + dx
        dg_ref[...] += dg

    row = lambda w: pl.BlockSpec((tm, w), lambda i: (i, 0))
    full = lambda shp: pl.BlockSpec(shp, lambda i: (0, 0))
    return pl.pallas_call(
        body, name=name, grid=(t // tm,),
        in_specs=[pl.BlockSpec((tm // d, d * w), lambda i: (i, 0)) for w, d in zip(PIECE_W, PIECE_D)]
        + [row(D_MODEL), _resident(g), row(D_MODEL), _resident(w_t)],
        out_specs=[row(D_MODEL), row(D_IN), full((1, D_IN)), full((1, D_MODEL))],
        out_shape=[jax.ShapeDtypeStruct((t, D_MODEL), F32), jax.ShapeDtypeStruct((t, D_IN), BF16),
                   jax.ShapeDtypeStruct((1, D_IN), F32), jax.ShapeDtypeStruct((1, D_MODEL), F32)],
        scratch_shapes=[pltpu.VMEM((2, tm, 128), F32)] * sum(d > 1 for d in PIECE_D),
        compiler_params=_params("arbitrary"),
    )(*pieces, h, g, dh_res, w_t)


def _t5_bucket(dist):
    max_exact = N_BUCKETS // 2
    n = jnp.maximum(dist, 0)
    nf = jnp.maximum(n, 1).astype(jnp.float32)
    large = max_exact + (jnp.log(nf / max_exact) / math.log(MAX_DISTANCE / max_exact)
                         * (N_BUCKETS - max_exact)).astype(jnp.int32)
    large = jnp.minimum(large, N_BUCKETS - 1)
    return jnp.where(n < max_exact, n, large)


ATT_CFG = ((1, 128, 0, 4), (4, 128, 4, 4), (16, 128, 8, 4), (1, B_WINDOW - 1, A_HEADS, 8))


def _bucket_tiles():
    qi = jnp.arange(BLOCK)[:, None]
    ki = jnp.arange(2 * BLOCK)[None, :]
    dist = qi + BLOCK - ki
    return jnp.stack([_t5_bucket(dist * cfg[0]) for cfg in ATT_CFG]).astype(jnp.int32)


def _band(max_steps):
    row = lax.broadcasted_iota(jnp.int32, (BLOCK, 2 * BLOCK), 0)
    col = lax.broadcasted_iota(jnp.int32, (BLOCK, 2 * BLOCK), 1)
    dist = row + BLOCK - col
    return (dist >= 0) & (dist <= max_steps)


def _bias_build(table, buckets):
    def body(tab_ref, bt_ref, out_ref):
        col = lax.broadcasted_iota(jnp.int32, (BLOCK, 2 * BLOCK), 1)
        for ci, (_, max_steps, h0, nh) in enumerate(ATT_CFG):
            bt = bt_ref[ci]
            band = _band(max_steps)
            for h in range(h0, h0 + nh):
                acc = lax.fori_loop(0, N_BUCKETS, lambda b, acc: jnp.where(bt == b, tab_ref[b, h], acc),
                                    jnp.zeros((BLOCK, 2 * BLOCK), F32))
                out_ref[0, BLOCK * h:BLOCK * (h + 1), :] = jnp.where(band & (col >= BLOCK), acc, NEG)
                out_ref[1, BLOCK * h:BLOCK * (h + 1), :] = jnp.where(band, acc, NEG)

    return pl.pallas_call(
        body, name="bias_build",
        in_specs=[pl.BlockSpec(memory_space=pltpu.SMEM), pl.BlockSpec(memory_space=pltpu.VMEM)],
        out_specs=pl.BlockSpec(memory_space=pltpu.VMEM),
        out_shape=jax.ShapeDtypeStruct((2, TOTAL_HEADS * BLOCK, 2 * BLOCK), F32),
    )(table, buckets)


def _bias_reduce(dbias, buckets, dsink_rows):
    def body(db_ref, bt_ref, ds_ref, out_ref, sink_ref):
        ri = lax.broadcasted_iota(jnp.int32, (N_BUCKETS, 128), 0)
        ci = lax.broadcasted_iota(jnp.int32, (N_BUCKETS, 128), 1)

        def per_bucket(b, acc):
            for cfg_i, (_, _, h0, nh) in enumerate(ATT_CFG):
                hit = bt_ref[cfg_i] == b
                for h in range(h0, h0 + nh):
                    val = jnp.sum(jnp.where(hit, db_ref[BLOCK * h:BLOCK * (h + 1), :], 0.0))
                    acc = jnp.where((ri == b) & (ci == h), val, acc)
            return acc

        out_ref[...] = lax.fori_loop(0, N_BUCKETS, per_bucket, jnp.zeros((N_BUCKETS, 128), F32))
        for h in range(8):
            sink_ref[h:h + 1, :] = jnp.sum(ds_ref[BLOCK * h:BLOCK * (h + 1), :], axis=0, keepdims=True)

    return pl.pallas_call(
        body, name="bias_reduce",
        in_specs=[pl.BlockSpec(memory_space=pltpu.VMEM)] * 3,
        out_specs=[pl.BlockSpec(memory_space=pltpu.VMEM)] * 2,
        out_shape=[jax.ShapeDtypeStruct((N_BUCKETS, 128), F32), jax.ShapeDtypeStruct((8, 128), F32)],
    )(dbias, buckets, dsink_rows)


class _Att:
    def __init__(self, cfg_i):
        stride, _, h0, nh = ATT_CFG[cfg_i]
        self.d = stride if cfg_i < 3 else 1
        self.h0, self.nh = h0, nh
        self.row_w = QKV_W if self.d == 1 else 3 * 256
        if cfg_i < 3:
            self.nq, self.wkv = 1, 256
            self.q_unit = [cfg_i if self.d == 1 else 0]
            self.k_unit, self.v_unit = (3 + cfg_i, 6 + cfg_i) if self.d == 1 else (1, 2)
            self.sinks = False
        else:
            self.nq, self.wkv = 2, 128
            self.q_unit = [9, 10]
            self.k_unit, self.v_unit = 22, 23
            self.sinks = True
        self.wq = 256 * self.nq


def _att_in_specs(cf, bsz):
    uq, ukv = cf.row_w // 256, cf.row_w // cf.wkv
    specs = [pl.BlockSpec((bsz, BLOCK, 256), functools.partial(lambda r, j, u: (0, j, r * uq + u), u=u))
             for u in cf.q_unit]
    for unit in (cf.k_unit, cf.v_unit):
        specs.append(pl.BlockSpec((bsz, BLOCK, cf.wkv),
                                  functools.partial(lambda r, j, u: (0, jnp.maximum(j - 1, 0), r * ukv + u), u=unit)))
        specs.append(pl.BlockSpec((bsz, BLOCK, cf.wkv),
                                  functools.partial(lambda r, j, u: (0, j, r * ukv + u), u=unit)))
    for qb in range(cf.nq):
        specs.append(pl.BlockSpec((None, HEADS_PER_BLOCK * BLOCK, 2 * BLOCK), functools.partial(
            lambda r, j, u: (jnp.minimum(j, 1), u, 0), u=cf.h0 // HEADS_PER_BLOCK + qb)))
    if cf.sinks:
        specs += [pl.BlockSpec((HEADS_PER_BLOCK * BLOCK, 128), functools.partial(lambda r, j, u: (u, 0), u=qb))
                  for qb in range(cf.nq)]
    return specs


HEADS_PER_BLOCK = 4


def _head_masks(rows):
    head = lax.broadcasted_iota(jnp.int32, (rows, 256), 1) // HEAD_DIM
    return [head == h for h in range(HEADS_PER_BLOCK)]


def _stack_heads(x, masks):
    return jnp.concatenate([jnp.where(m, x, jnp.zeros_like(x)) for m in masks], axis=0)


def _unstack_heads(x4, masks):
    blocks = [x4[BLOCK * h:BLOCK * (h + 1)] for h in range(HEADS_PER_BLOCK)]
    return jnp.where(masks[0], blocks[0], jnp.where(masks[1], blocks[1], jnp.where(masks[2], blocks[2], blocks[3])))


def _row_value(x):
    return jnp.max(x, axis=-1, keepdims=True)


def _kv_operand(cf, x, qb):
    if cf.wkv == 256:
        return x
    lane = lax.broadcasted_iota(jnp.int32, x.shape, 1)
    swapped = pltpu.roll(x, HEAD_DIM, 1)
    half = jnp.where(lane < HEAD_DIM, x, swapped) if qb == 0 else jnp.where(lane < HEAD_DIM, swapped, x)
    return jnp.concatenate([half, half], axis=1)


def _kv_fold(cf, grads):
    if cf.wkv == 256:
        return grads[0]
    folded = []
    for g in grads:
        x = g[:, :128] + g[:, 128:]
        folded.append(x + pltpu.roll(x, HEAD_DIM, 1))
    lane = lax.broadcasted_iota(jnp.int32, folded[0].shape, 1)
    return jnp.where(lane < HEAD_DIM, folded[0], folded[1])


def _attn_fwd(cf, zf, bias, sinks, name, hosted=()):
    bsz, l, _ = zf.shape
    nb = l // BLOCK

    def body(*refs):
        refs = list(refs)
        q_refs = [refs.pop(0) for _ in range(cf.nq)]
        kp_ref, kc_ref, vp_ref, vc_ref = [refs.pop(0) for _ in range(4)]
        bias_refs = [refs.pop(0) for _ in range(cf.nq)]
        sink_refs = [refs.pop(0) for _ in range(cf.nq)] if cf.sinks else None
        o_ref, lse_ref = refs
        masks, kv_masks = _head_masks(BLOCK), _head_masks(2 * BLOCK)
        sinks4 = [_row_value(ref[...]) for ref in sink_refs] if cf.sinks else None
        for bi in range(bsz):
            k = jnp.concatenate([kp_ref[bi], kc_ref[bi]], axis=0)
            v = jnp.concatenate([vp_ref[bi], vc_ref[bi]], axis=0)
            for qb in range(cf.nq):
                cols = slice(256 * qb, 256 * (qb + 1))
                kb, vb = _kv_operand(cf, k, qb), _kv_operand(cf, v, qb)
                q4 = _stack_heads(q_refs[qb][bi] * (HEAD_DIM ** -0.5), masks)
                s = _nt(q4, kb) + bias_refs[qb][...]
                m = jnp.max(s, axis=-1, keepdims=True)
                if cf.sinks:
                    sk = sinks4[qb]
                    m = jnp.maximum(m, sk)
                p = jnp.exp(s - m)
                den = jnp.sum(p, axis=-1, keepdims=True)
                if cf.sinks:
                    den = den + jnp.exp(sk - m)
                pn = (p * (1.0 / den)).astype(BF16)
                p_lanes = jnp.concatenate([pn[BLOCK * h:BLOCK * (h + 1)] for h in range(HEADS_PER_BLOCK)], axis=1)
                v4 = jnp.concatenate([jnp.where(mk, vb, jnp.zeros_like(vb)) for mk in kv_masks], axis=0)
                o_ref[bi, :, cols] = _nn(p_lanes, v4)
                lse_ref[bi, :, cols] = _unstack_heads(
                    jnp.broadcast_to(m + jnp.log(den), (HEADS_PER_BLOCK * BLOCK, 256)), masks)

    in_specs = _att_in_specs(cf, bsz)
    args = [zf] * (cf.nq + 4) + [bias] * cf.nq + ([sinks] * cf.nq if cf.sinks else [])
    out = pl.BlockSpec((bsz, BLOCK, cf.wq), lambda r, j: (0, j, r))
    shape = jax.ShapeDtypeStruct((bsz, l, cf.d * cf.wq), F32)
    return _call(
        body, name=name, grid=(cf.d, nb), in_specs=in_specs, out_specs=[out, out], out_shape=[shape, shape],
        sem=("parallel", "arbitrary"), args=args, hosted=hosted)


def _attn_bwd(cf, zf, bias, sinks, stats, dy, name, hosted=()):
    bsz, l, _ = zf.shape
    nb = l // BLOCK

    def body(*refs):
        refs = list(refs)
        q_refs = [refs.pop(0) for _ in range(cf.nq)]
        kp_ref, kc_ref, vp_ref, vc_ref = [refs.pop(0) for _ in range(4)]
        bias_refs = [refs.pop(0) for _ in range(cf.nq)]
        sink_refs = [refs.pop(0) for _ in range(cf.nq)] if cf.sinks else None
        st_ref, dy_ref, dq_ref, dk_ref, dv_ref, dbias_ref = [refs.pop(0) for _ in range(6)]
        dsink_ref = refs.pop(0) if cf.sinks else None
        dk_acc, dv_acc = refs
        r, j = pl.program_id(0), pl.program_id(1)

        @pl.when((r == 0) & (j == 0))
        def _():
            dbias_ref[...] = jnp.zeros_like(dbias_ref)
            if cf.sinks:
                dsink_ref[...] = jnp.zeros_like(dsink_ref)

        @pl.when(j == 0)
        def _():
            dk_acc[...] = jnp.zeros_like(dk_acc)
            dv_acc[...] = jnp.zeros_like(dv_acc)

        masks = _head_masks(BLOCK)
        cur = pl.ds(pl.multiple_of(j * BLOCK, BLOCK), BLOCK)
        prev = pl.ds(pl.multiple_of(jnp.maximum(j - 1, 0) * BLOCK, BLOCK), BLOCK)
        sinks4 = [_row_value(ref[...]) for ref in sink_refs] if cf.sinks else None
        ds_sum, dsink_sum = [None] * cf.nq, [None] * cf.nq
        for bi in range(bsz):
            k = jnp.concatenate([kp_ref[bi], kc_ref[bi]], axis=0)
            v = jnp.concatenate([vp_ref[bi], vc_ref[bi]], axis=0)
            dk_blocks, dv_blocks = [], []
            for qb in range(cf.nq):
                cols = slice(256 * qb, 256 * (qb + 1))
                kb, vb = _kv_operand(cf, k, qb), _kv_operand(cf, v, qb)
                q4 = _stack_heads(q_refs[qb][bi] * (HEAD_DIM ** -0.5), masks)
                st = st_ref[bi, :, 128 * qb:128 * (qb + 1)]
                lt4, e4 = _read_stats(st, 0), _read_stats(st, 1)
                pa = jnp.exp(_nt(q4, kb) + bias_refs[qb][...] - lt4)
                dy = dy_ref[bi, :, cols]
                dy4 = _stack_heads(dy, masks)
                ds = pa * (_nt(dy4, vb) - e4)
                ds_sum[qb] = ds if ds_sum[qb] is None else ds_sum[qb] + ds
                if cf.sinks:
                    dsk = jnp.exp(sinks4[qb] - lt4) * e4
                    dsink_sum[qb] = dsk if dsink_sum[qb] is None else dsink_sum[qb] + dsk
                dsb = ds.astype(BF16)
                dq_ref[bi, :, cols] = (_unstack_heads(_nn(dsb, kb), masks) * (HEAD_DIM ** -0.5)).astype(dq_ref.dtype)
                dk_blocks.append(_tn(dsb, q4))
                dv_blocks.append(_tn(pa.astype(BF16), dy4))
            dk_new, dv_new = _kv_fold(cf, dk_blocks), _kv_fold(cf, dv_blocks)
            dk_acc[bi, cur, :] += dk_new[BLOCK:]
            dv_acc[bi, cur, :] += dv_new[BLOCK:]
            dk_acc[bi, prev, :] += dk_new[:BLOCK]
            dv_acc[bi, prev, :] += dv_new[:BLOCK]
        for qb in range(cf.nq):
            rows = slice(HEADS_PER_BLOCK * BLOCK * qb, HEADS_PER_BLOCK * BLOCK * (qb + 1))
            dbias_ref[rows, :] += ds_sum[qb]
            if cf.sinks:
                dsink_ref[rows, :] -= dsink_sum[qb]

        @pl.when(j == nb - 1)
        def _():
            dk_ref[...] = dk_acc[...].astype(dk_ref.dtype)
            dv_ref[...] = dv_acc[...].astype(dv_ref.dtype)

    tok = pl.BlockSpec((bsz, BLOCK, cf.wq), lambda r, j: (0, j, r))
    in_specs = _att_in_specs(cf, bsz) + [pl.BlockSpec((bsz, BLOCK, cf.wq // 2), lambda r, j: (0, j, r)), tok]
    args = [zf] * (cf.nq + 4) + [bias] * cf.nq + ([sinks] * cf.nq if cf.sinks else []) + [stats, dy]
    seq = pl.BlockSpec((bsz, l, cf.wkv), lambda r, j: (0, 0, r))
    out_specs = [tok, seq, seq, pl.BlockSpec((cf.nh * BLOCK, 2 * BLOCK), lambda r, j: (0, 0))]
    grad_dtype = BF16 if cf.d == 1 else F32
    out_shape = [jax.ShapeDtypeStruct((bsz, l, cf.d * cf.wq), grad_dtype),
                 jax.ShapeDtypeStruct((bsz, l, cf.d * cf.wkv), grad_dtype),
                 jax.ShapeDtypeStruct((bsz, l, cf.d * cf.wkv), grad_dtype),
                 jax.ShapeDtypeStruct((cf.nh * BLOCK, 2 * BLOCK), F32)]
    if cf.sinks:
        out_specs.append(pl.BlockSpec((cf.nh * BLOCK, 128), lambda r, j: (0, 0)))
        out_shape.append(jax.ShapeDtypeStruct((cf.nh * BLOCK, 128), F32))
    return _call(
        body, name=name, grid=(cf.d, nb), in_specs=in_specs, out_specs=out_specs, out_shape=out_shape,
        scratch=[pltpu.VMEM((bsz, l, cf.wkv), F32), pltpu.VMEM((bsz, l, cf.wkv), F32)],
        sem=("arbitrary", "arbitrary"), args=args, hosted=hosted)


def _merge_fwd(o_a, lse_a, o_b, zg, h, wa_t, wb_t, wout, name):
    t = h.shape[0]
    tm = 512

    def body(o1, o2, o3, l1, l2, l3, ob_ref, zg_ref, h_ref, wa_ref, wb_ref, wo_ref, h2_ref, ya_ref, lt_ref, *slabs):
        o = [_unfold(ref, slabs[i], d) for i, (ref, d) in enumerate(zip((o1, o2, o3), DILATIONS))]
        l = [_unfold(ref, slabs[3 + i], d) for i, (ref, d) in enumerate(zip((l1, l2, l3), DILATIONS))]
        m = jnp.maximum(jnp.maximum(l[0], l[1]), l[2])
        e1, e2, e3 = jnp.exp(l[0] - m), jnp.exp(l[1] - m), jnp.exp(l[2] - m)
        se = e1 + e2 + e3
        ya = (e1 / se) * o[0] + (e2 / se) * o[1] + (e3 / se) * o[2]
        ya_ref[...] = ya
        lt_ref[...] = m + jnp.log(se)
        pa = _nt(ya.astype(BF16), wa_ref[...])
        pb = _nt(ob_ref[...].astype(BF16), wb_ref[...])
        merged = (jax.nn.sigmoid(zg_ref[:, :D_MODEL].astype(F32)) * pa
                  + jax.nn.sigmoid(zg_ref[:, D_MODEL:].astype(F32)) * pb)
        h2_ref[...] = h_ref[...] + _nn(merged.astype(BF16), wo_ref[...])

    row = lambda w: pl.BlockSpec((tm, w), lambda i: (i, 0))
    folded = [pl.BlockSpec((tm // d, d * 256), lambda i: (i, 0)) for d in DILATIONS]
    full = _resident
    return pl.pallas_call(
        body, name=name, grid=(t // tm,),
        in_specs=folded + folded + [row(512), row(GATE_W), row(D_MODEL), full(wa_t), full(wb_t), full(wout)],
        out_specs=[row(D_MODEL), row(256), row(256)],
        out_shape=[jax.ShapeDtypeStruct((t, D_MODEL), F32), jax.ShapeDtypeStruct((t, 256), F32),
                   jax.ShapeDtypeStruct((t, 256), F32)],
        scratch_shapes=[pltpu.VMEM((2, tm, 128), F32)] * 6,
        compiler_params=_params("parallel"),
    )(*o_a, *lse_a, o_b, zg, h, wa_t, wb_t, wout)


STAT_LANES = 32


def _head_stats(lt, dy, y):
    rows, w = dy.shape
    lane = lax.broadcasted_iota(jnp.int32, (rows, w), 1)
    out_lane = lax.broadcasted_iota(jnp.int32, (rows, w // 2), 1)
    prod = dy * y
    out = jnp.zeros((rows, w // 2), F32)
    for hd in range(w // HEAD_DIM):
        mine = lane // HEAD_DIM == hd
        dot = jnp.sum(jnp.where(mine, prod, 0.0), axis=-1, keepdims=True)
        both = jnp.where(out_lane % STAT_LANES < STAT_LANES // 2, _row_value(jnp.where(mine, lt, NEG)), dot)
        out = jnp.where(out_lane // STAT_LANES == hd, both, out)
    return out


def _read_stats(st, part):
    lane = lax.broadcasted_iota(jnp.int32, st.shape, 1)
    half = (lane % STAT_LANES < STAT_LANES // 2) == (part == 0)
    return jnp.concatenate([_row_value(jnp.where((lane // STAT_LANES == hd) & half, st, NEG))
                            for hd in range(HEADS_PER_BLOCK)], axis=0)


def _merge_bwd(dh, ya, lse_tot, o_b, lse_b, zg, wa_t, wb_t, wout, name, hosted=()):
    t = dh.shape[0]
    tm = 256

    def body(dh_ref, ya_ref, lt_ref, ob_ref, lb_ref, zg_ref, wa_ref, wb_ref, wo_ref,
             mg_ref, dpa_ref, dpb_ref, dzg_ref, dy1, dy2, dy3, st1, st2, st3, dyb_ref, stb_ref, *slabs):
        dm = _nt(dh_ref[...].astype(BF16), wo_ref[...])
        pa = _nt(ya_ref[...].astype(BF16), wa_ref[...])
        pb = _nt(ob_ref[...].astype(BF16), wb_ref[...])
        sa = jax.nn.sigmoid(zg_ref[:, :D_MODEL].astype(F32))
        sb = jax.nn.sigmoid(zg_ref[:, D_MODEL:].astype(F32))
        mg_ref[...] = (sa * pa + sb * pb).astype(BF16)
        dpa = (dm * sa).astype(BF16)
        dpb = (dm * sb).astype(BF16)
        dpa_ref[...] = dpa
        dpb_ref[...] = dpb
        dzg_ref[:, :D_MODEL] = (dm * pa * (sa * (1.0 - sa))).astype(BF16)
        dzg_ref[:, D_MODEL:] = (dm * pb * (sb * (1.0 - sb))).astype(BF16)
        dya = _nn(dpa, wa_ref[...])
        dyb = _nn(dpb, wb_ref[...])
        stats = _head_stats(lt_ref[...], dya, ya_ref[...])
        for i, d in enumerate(DILATIONS):
            _fold(dya, slabs[i], (dy1, dy2, dy3)[i], d)
            _fold(stats, slabs[3 + i], (st1, st2, st3)[i], d)
        dyb_ref[...] = dyb.astype(BF16)
        stb_ref[...] = _head_stats(lb_ref[...], dyb, ob_ref[...])

    row = lambda w: pl.BlockSpec((tm, w), lambda i: (i, 0))
    folded = [pl.BlockSpec((tm // d, d * 256), lambda i: (i, 0)) for d in DILATIONS]
    full = _resident
    sds = jax.ShapeDtypeStruct
    dy_shape = [sds((t // d, d * 256), BF16) for d in DILATIONS]
    st_shape = [sds((t // d, d * 128), F32) for d in DILATIONS]
    st_specs = [pl.BlockSpec((tm // d, d * 128), lambda i: (i, 0)) for d in DILATIONS]
    return _call(
        body, name=name, grid=(t // tm,),
        in_specs=[row(D_MODEL), row(256), row(256), row(512), row(512), row(GATE_W), full(wa_t), full(wb_t),
                  full(wout)],
        out_specs=[row(D_MODEL), row(D_MODEL), row(D_MODEL), row(GATE_W)] + folded + st_specs + [row(512), row(256)],
        out_shape=[sds((t, D_MODEL), BF16), sds((t, D_MODEL), BF16), sds((t, D_MODEL), BF16),
                   sds((t, GATE_W), BF16)] + dy_shape + st_shape + [sds((t, 512), BF16), sds((t, 256), F32)],
        scratch=[pltpu.VMEM((2, tm, 128), F32)] * 6,
        sem=("parallel",), args=(dh, ya, lse_tot, o_b, lse_b, zg, wa_t, wb_t, wout), hosted=hosted)


def _pair_sum(grad, got, name):
    _, _, r, cdim = grad.shape
    core = lax.axis_index("c").astype(jnp.int32).reshape(1)

    def body(core_ref, g_ref, s_ref, o_ref):
        o_ref[...] = (g_ref[...] + s_ref[...]).astype(BF16)

    return pl.pallas_call(
        body, name=name,
        grid_spec=pltpu.PrefetchScalarGridSpec(
            num_scalar_prefetch=1, grid=(N_CHIP,),
            in_specs=[pl.BlockSpec((None, None, r, cdim), lambda q, core_ref: (q, core_ref[0], 0, 0)),
                      pl.BlockSpec((None, None, r, cdim), lambda q, core_ref: (q, 0, 0, 0))],
            out_specs=pl.BlockSpec((None, r, cdim), lambda q, core_ref: (q, 0, 0))),
        out_shape=jax.ShapeDtypeStruct((N_CHIP, r, cdim), BF16),
        compiler_params=_params("parallel"),
    )(core, grad, got)


def _adamw_math(w, g, m, v):
    m = ADAM_B1 * m + (1.0 - ADAM_B1) * g
    v = ADAM_B2 * v + (1.0 - ADAM_B2) * jnp.square(g)
    m_hat = m / (1.0 - ADAM_B1 ** ADAM_STEP)
    v_hat = v / (1.0 - ADAM_B2 ** ADAM_STEP)
    delta = -ADAM_LR * (m_hat / (jnp.sqrt(v_hat) + ADAM_EPS) + ADAM_WD * w)
    return delta, m, v


def _update(w, parts, m, v, transposed, name):
    r, c = parts.shape[1:]

    def body(w_ref, p_ref, m_ref, v_ref, g_ref, d_ref, m2_ref, v2_ref):
        def total(rows):
            return ((p_ref[0, rows].astype(F32) + p_ref[1, rows].astype(F32)) + p_ref[2, rows].astype(F32)) \
                + p_ref[3, rows].astype(F32)

        def update(at, g):
            g_ref[at] = g
            d_ref[at], m2_ref[at], v2_ref[at] = _adamw_math(w_ref[at], g, m_ref[at], v_ref[at])

        if not transposed:
            update((slice(None), slice(None)), total(slice(None)))
            return
        for r0 in range(0, r, 128):
            n = min(128, r - r0)
            gt = total(slice(r0, r0 + n))
            if n < 128:
                gt = jnp.concatenate([gt, jnp.zeros((128 - n, c), F32)], axis=0)
            update((slice(None), slice(r0, r0 + n)), gt.T[:, :n])

    sds = jax.ShapeDtypeStruct(w.shape, F32)
    return pl.pallas_call(body, name=name, out_shape=[sds] * 4,
                          compiler_params=pltpu.CompilerParams(vmem_limit_bytes=VMEM_LIMIT))(w, parts, m, v)


SMALL_ROWS = 80


def _small_update(g, w, m, v, name):
    def body(g_ref, w_ref, m_ref, v_ref, gs_ref, d_ref, m2_ref, v2_ref, got_ref, send_sems, recv_sems):
        x, y, c = _place()
        me = 4 * x + 2 * y + c
        got_ref[me] = g_ref[...]
        copies = []
        for k in range(1, N_DEV):
            peer = (x ^ (k >> 2), y ^ ((k >> 1) & 1), c ^ (k & 1))
            cp = pltpu.make_async_remote_copy(
                src_ref=g_ref, dst_ref=got_ref.at[me], send_sem=send_sems.at[k - 1], recv_sem=recv_sems.at[k - 1],
                device_id=peer, device_id_type=MESH)
            cp.start()
            copies.append(cp)
        for cp in copies:
            cp.wait()
        total = got_ref[0]
        for k in range(1, N_DEV):
            total = total + got_ref[k]
        gs_ref[...] = total
        d_ref[...], m2_ref[...], v2_ref[...] = _adamw_math(w_ref[...], total, m_ref[...], v_ref[...])

    sds = jax.ShapeDtypeStruct((SMALL_ROWS, 128), F32)
    vm = pl.BlockSpec(memory_space=pltpu.VMEM)
    return pl.pallas_call(
        body, name=name, in_specs=[vm] * 4, out_specs=[vm] * 4, out_shape=[sds] * 4,
        scratch_shapes=[pltpu.VMEM((N_DEV, SMALL_ROWS, 128), F32), pltpu.SemaphoreType.DMA((N_DEV - 1,)),
                        pltpu.SemaphoreType.DMA((N_DEV - 1,))],
    )(g, w, m, v)


def _pack_small(gains, b_in, rel_bias, sinks, last):
    rows = [a.reshape(8, 128) for a in gains] + [b_in.reshape(40, 128), rel_bias.reshape(5, 128),
                                                 jnp.pad(sinks.reshape(1, 8), ((0, 0), (0, 120))), last]
    rows.append(jnp.zeros((SMALL_ROWS - 79, 128), F32))
    return jnp.concatenate(rows, axis=0)


def _unpack_small(p, like):
    out = [p[8 * i:8 * i + 8].reshape(like[i].shape) for i in range(4)]
    out.append(p[32:72].reshape(like[4].shape))
    out.append(p[72:77].reshape(like[5].shape))
    out.append(p[77, :8].reshape(like[6].shape))
    return out


def kernel(x, ffn1_norm, ffn1_w_gate, ffn1_w_up, ffn1_w_down, mix_norm, w_in, b_in, w_branch_a, w_branch_b, w_out, sinks, rel_bias, ffn2_norm, ffn2_w_gate, ffn2_w_up, ffn2_w_down, final_norm, loss_target, m_ffn1_norm, m_ffn1_w_gate, m_ffn1_w_up, m_ffn1_w_down, m_mix_norm, m_w_in, m_b_in, m_w_branch_a, m_w_branch_b, m_w_out, m_sinks, m_rel_bias, m_ffn2_norm, m_ffn2_w_gate, m_ffn2_w_up, m_ffn2_w_down, m_final_norm, v_ffn1_norm, v_ffn1_w_gate, v_ffn1_w_up, v_ffn1_w_down, v_mix_norm, v_w_in, v_b_in, v_w_branch_a, v_w_branch_b, v_w_out, v_sinks, v_rel_bias, v_ffn2_norm, v_ffn2_w_gate, v_ffn2_w_up, v_ffn2_w_down, v_final_norm):
    bsz, seq, _ = x.shape
    t = bsz * seq
    xt = x.reshape(t, D_MODEL)
    target = loss_target.reshape(t, D_MODEL)

    big = [("ffn1_w_gate", ffn1_w_gate, m_ffn1_w_gate, v_ffn1_w_gate, True),
           ("ffn1_w_up", ffn1_w_up, m_ffn1_w_up, v_ffn1_w_up, True),
           ("ffn1_w_down", ffn1_w_down, m_ffn1_w_down, v_ffn1_w_down, False),
           ("w_in", w_in, m_w_in, v_w_in, True),
           ("w_branch_a", w_branch_a, m_w_branch_a, v_w_branch_a, True),
           ("w_branch_b", w_branch_b, m_w_branch_b, v_w_branch_b, True),
           ("w_out", w_out, m_w_out, v_w_out, False),
           ("ffn2_w_gate", ffn2_w_gate, m_ffn2_w_gate, v_ffn2_w_gate, True),
           ("ffn2_w_up", ffn2_w_up, m_ffn2_w_up, v_ffn2_w_up, True),
           ("ffn2_w_down", ffn2_w_down, m_ffn2_w_down, v_ffn2_w_down, False)]
    shard = {nm: (w[0].T if tr else w[0]).astype(BF16) for nm, w, _, _, tr in big}
    full = {}

    def gather(names):
        return _Gather([shard[nm] for nm in names])

    def keep(names, got):
        for nm, gw in zip(names, got):
            full[nm] = gw.reshape(-1, gw.shape[-1])

    ffn1_names = ["ffn1_w_gate", "ffn1_w_up", "ffn1_w_down"]
    mix_names = ["w_in", "w_branch_a", "w_branch_b", "w_out"]
    ffn2_names = ["ffn2_w_gate", "ffn2_w_up", "ffn2_w_down"]
    g1, gm, g2, gf = ffn1_norm, mix_norm, ffn2_norm, final_norm.reshape(1, D_MODEL)

    keep(ffn1_names, _exchange([gather(ffn1_names)], "gather_ffn1")[0])
    (h1, n1, a1, b1, hff1), (got,) = _ffn_fwd(xt, g1, full["ffn1_w_gate"], full["ffn1_w_up"], full["ffn1_w_down"], "ffn1_fwd",
                                hosted=[gather(mix_names)])
    keep(mix_names, got)
    (u, zq, zg, *zdil), (got,) = _inproj_fwd(h1, gm, full["w_in"], b_in, "inproj_fwd", hosted=[gather(ffn2_names[:2])])
    keep(ffn2_names[:2], got)
    buckets = _bucket_tiles()
    bias = _bias_build(rel_bias, buckets)
    sink_rows = jnp.broadcast_to(sinks.reshape(8, 1, 1), (8, BLOCK, 128)).reshape(8 * BLOCK, 128)
    cfs = [_Att(i) for i in range(4)]
    zfold = []
    for i, cf in enumerate(cfs):
        if cf.d == 1:
            zfold.append(zq.reshape(bsz, seq, QKV_W))
        else:
            zfold.append(zdil[i - 1].reshape(bsz, seq // cf.d, cf.d * cf.row_w))
    att = [None] * 4
    for i in (3, 0, 1, 2):
        cf = cfs[i]
        (o, lse), got = _attn_fwd(cf, zfold[i], bias, sink_rows, f"attn{i}_fwd",
                                  hosted=[gather(ffn2_names[2:])] if i == 3 else ())
        if i == 3:
            keep(ffn2_names[2:], got[0])
        att[i] = (o.reshape(t // cf.d, cf.d * cf.wq), lse.reshape(t // cf.d, cf.d * cf.wq))
    o_b, lse_b = att[3]
    h2, ya, lse_tot = _merge_fwd([a[0] for a in att[:3]], [a[1] for a in att[:3]], o_b, zg, h1,
                                 full["w_branch_a"], full["w_branch_b"], full["w_out"], "merge_fwd")
    (dh3, n2, a2, b2, hff2, loss_part, dgf), _ = _ffn_fwd(
        h2, g2, full["ffn2_w_gate"], full["ffn2_w_up"], full["ffn2_w_down"], "ffn2_fwd", head=(gf, target))

    grads, pair, from_chips = {}, {}, {}

    def by_owner(nm):
        return grads[nm].reshape(N_CHIP, 2, -1, grads[nm].shape[-1])

    def to_core(names):
        return _CoreExchange([by_owner(nm) for nm in names])

    def pair_up(names, got):
        for nm, sib in zip(names, got):
            pair[nm] = _pair_sum(by_owner(nm), sib, f"pair_sum_{nm}")

    def to_chips(names):
        return _ChipExchange([pair[nm] for nm in names])

    def landed(names, got):
        for nm, parts in zip(names, got):
            from_chips[nm] = parts

    dh2, dg2, da, db = _ffn_bwd(h2, a2, b2, g2, dh3, full["ffn2_w_gate"], full["ffn2_w_up"],
                                          full["ffn2_w_down"], "ffn2_bwd")
    grads["ffn2_w_gate"], _ = _tn_matmul(da, n2, 1408, "ffn2_dgate")
    grads["ffn2_w_up"], _ = _tn_matmul(db, n2, 1408, "ffn2_dup")
    grads["ffn2_w_down"], _ = _tn_matmul(hff2, dh3, 1408, "ffn2_ddown", scale=0.5)
    (merged, dpa, dpb, dzg, *cot), (got,) = _merge_bwd(
        dh2, ya, lse_tot, o_b, lse_b, zg, full["w_branch_a"], full["w_branch_b"], full["w_out"], "merge_bwd",
        hosted=[to_core(ffn2_names)])
    dys, sts = cot[0:3] + [cot[6]], cot[3:6] + [cot[7]]
    pair_up(ffn2_names, got)
    dq, dk, dv, dbias, dsink = [None] * 4, [None] * 4, [None] * 4, [None] * 4, None
    for i in (3, 0, 1, 2):
        cf = cfs[i]
        shp = (bsz, seq // cf.d, cf.d * cf.wq)
        hosted = {3: lambda: [to_chips(ffn2_names[:2])], 0: lambda: [to_chips(ffn2_names[2:])]}.get(i, list)()
        res, got = _attn_bwd(cf, zfold[i], bias, sink_rows, sts[i].reshape(shp[:2] + (shp[2] // 2,)), dys[i].reshape(shp),
                             f"attn{i}_bwd", hosted=hosted)
        if i == 3:
            landed(ffn2_names[:2], got[0])
        elif i == 0:
            landed(ffn2_names[2:], got[0])
        dq[i] = res[0].reshape(t // cf.d, cf.d * cf.wq)
        dk[i] = res[1].reshape(t // cf.d, cf.d * cf.wkv)
        dv[i] = res[2].reshape(t // cf.d, cf.d * cf.wkv)
        dbias[i] = res[3]
        if cf.sinks:
            dsink = res[4]
    pieces = dq[:3] + dk[:3] + dv[:3] + [dq[3], dk[3], dv[3], dzg]
    dh1, dz, db_in, dgm = _inproj_bwd(pieces, h1, gm, dh2, full["w_in"], "inproj_bwd")
    dx, dg1, da, db = _ffn_bwd(xt, a1, b1, g1, dh1, full["ffn1_w_gate"], full["ffn1_w_up"],
                                         full["ffn1_w_down"], "ffn1_bwd")
    grads["w_in"], _ = _tn_matmul(dz, u, 1280, "dw_in")
    grads["ffn1_w_down"], (got,) = _tn_matmul(hff1, dh1, 1408, "ffn1_ddown", hosted=[to_core(["w_in"])], scale=0.5)
    pair_up(["w_in"], got)
    grads["ffn1_w_gate"], got = _tn_matmul(da, n1, 1408, "ffn1_dgate",
                                           hosted=[to_chips(["w_in"]), to_core(["ffn1_w_down"])])
    landed(["w_in"], got[0])
    pair_up(["ffn1_w_down"], got[1])
    grads["ffn1_w_up"], got = _tn_matmul(db, n1, 1408, "ffn1_dup",
                                         hosted=[to_chips(["ffn1_w_down"]), to_core(["ffn1_w_gate"])])
    landed(["ffn1_w_down"], got[0])
    pair_up(["ffn1_w_gate"], got[1])
    grads["w_out"], got = _tn_matmul(merged, dh2, 1024, "dw_out",
                                     hosted=[to_chips(["ffn1_w_gate"]), to_core(["ffn1_w_up"])])
    landed(["ffn1_w_gate"], got[0])
    pair_up(["ffn1_w_up"], got[1])
    grads["w_branch_b"], got = _tn_matmul(dpb, o_b, 1024, "dw_branch_b",
                                          hosted=[to_chips(["ffn1_w_up"]), to_core(["w_out"])])
    landed(["ffn1_w_up"], got[0])
    pair_up(["w_out"], got[1])
    grads["w_branch_a"], got = _tn_matmul(dpa, ya, 1024, "dw_branch_a",
                                          hosted=[to_chips(["w_out"]), to_core(["w_branch_b"])])
    landed(["w_out"], got[0])
    pair_up(["w_branch_b"], got[1])
    got = _exchange([to_chips(["w_branch_b"]), to_core(["w_branch_a"])], "reduce_scatter_tail1")
    landed(["w_branch_b"], got[0])
    pair_up(["w_branch_a"], got[1])
    landed(["w_branch_a"], _exchange([to_chips(["w_branch_a"])], "reduce_scatter_tail2")[0])
    dtable, dsinks = _bias_reduce(jnp.concatenate(dbias, axis=0), buckets, dsink)

    out_g, out_d, out_m, out_v = {}, {}, {}, {}
    for nm, w, m, v, tr in big:
        around = tr and w.shape[-1] % 128 != 0
        wmv = [a[0].T if around else a[0] for a in (w, m, v)]
        res = _update(wmv[0], from_chips[nm], wmv[1], wmv[2], tr and not around, f"update_{nm}")
        out_g[nm], out_d[nm], out_m[nm], out_v[nm] = [(a.T if around else a)[None] for a in res]

    small = [("ffn1_norm", ffn1_norm, m_ffn1_norm, v_ffn1_norm), ("mix_norm", mix_norm, m_mix_norm, v_mix_norm),
             ("ffn2_norm", ffn2_norm, m_ffn2_norm, v_ffn2_norm), ("final_norm", final_norm, m_final_norm, v_final_norm),
             ("b_in", b_in, m_b_in, v_b_in), ("rel_bias", rel_bias, m_rel_bias, v_rel_bias),
             ("sinks", sinks, m_sinks, v_sinks)]
    zero_row = jnp.zeros((1, 128), F32)
    pack = lambda arrs, last: _pack_small(arrs[:4], arrs[4], arrs[5], arrs[6], last)
    g_small = pack([dg1, dgm, dg2, dgf, db_in, dtable[:, :TOTAL_HEADS], dsinks[:, 0]], loss_part)
    packed = [pack([s[k] for s in small], zero_row) for k in (1, 2, 3)]
    gs, ds, ms, vs = _small_update(g_small, *packed, "small_update")
    like = [s[1] for s in small]
    for nm_s, g_, d_, m_, v_ in zip([s[0] for s in small], _unpack_small(gs, like), _unpack_small(ds, like),
                                    _unpack_small(ms, like), _unpack_small(vs, like)):
        out_g[nm_s], out_d[nm_s], out_m[nm_s], out_v[nm_s] = g_, d_, m_, v_
    loss = gs[78, 0]

    order = ["ffn1_norm", "ffn1_w_gate", "ffn1_w_up", "ffn1_w_down", "mix_norm", "w_in", "b_in", "w_branch_a",
             "w_branch_b", "w_out", "sinks", "rel_bias", "ffn2_norm", "ffn2_w_gate", "ffn2_w_up", "ffn2_w_down",
             "final_norm"]
    return (loss, dx.reshape(x.shape), *[out_g[k] for k in order], *[out_d[k] for k in order],
            *[out_m[k] for k in order], *[out_v[k] for k in order])
```

```python
import functools
import math

import numpy as np
import jax
import jax.numpy as jnp
from jax import lax
from jax.experimental import pallas as pl
from jax.experimental.pallas import tpu as pltpu

D_MODEL = 1024
D_FF = 2816
FF_CHUNK = 256
HEAD_DIM = 64
BLOCK = 128
N_BUCKETS = 32
MAX_DISTANCE = 2048
A_HEADS = 12
TOTAL_HEADS = 20
DIL_GROUPS = ((128, 1), (512, 4), (2048, 16))
B_WINDOW = 128
QKV_W = 3072
GATE_W = 2048
D_IN = QKV_W + GATE_W
EPS = 1e-6
NEG = -1e30
N_DEV = 8
N_CHIP = 4
ADAM_LR, ADAM_B1, ADAM_B2, ADAM_EPS, ADAM_WD, ADAM_STEP = 0.001, 0.9, 0.999, 1e-08, 0.01, 10
VMEM_LIMIT = 56 * 1024 * 1024
MESH = pl.DeviceIdType.MESH
BF16 = jnp.bfloat16
F32 = jnp.float32
ANY = pl.BlockSpec(memory_space=pl.ANY)


def _params(*sem):
    return pltpu.CompilerParams(dimension_semantics=sem, vmem_limit_bytes=VMEM_LIMIT)


def _resident(a):
    return pl.BlockSpec(a.shape, lambda i: (0, 0), pipeline_mode=pl.Buffered(1))


def _place():
    return lax.axis_index("x"), lax.axis_index("y"), lax.axis_index("c")


class _Gather:
    def __init__(self, shards):
        self.ins = list(shards)
        n = self.n = len(shards)
        self.out_shape = [jax.ShapeDtypeStruct((N_DEV,) + s.shape, s.dtype) for s in shards]
        self.scratch = [pltpu.SemaphoreType.DMA((7 * n,)), pltpu.SemaphoreType.DMA((7 * n,)),
                        pltpu.SemaphoreType.DMA((n,))]

    def _copies(self, ins, outs, sems):
        send_sems, recv_sems, local_sems = sems
        x, y, c = _place()
        me, sibling = (x, y, c), (x, y, 1 - c)
        chips = [(1 - x, y), (x, 1 - y), (1 - x, 1 - y)]

        def copy(i, k, block, to, src=None):
            dst = outs[i].at[4 * block[0] + 2 * block[1] + block[2]]
            return pltpu.make_async_remote_copy(
                src_ref=dst if src is None else src, dst_ref=dst, send_sem=send_sems.at[7 * i + k],
                recv_sem=recv_sems.at[7 * i + k], device_id=to, device_id_type=MESH)

        n = self.n
        mine = [pltpu.make_async_copy(ins[i], outs[i].at[4 * x + 2 * y + c], local_sems.at[i]) for i in range(n)]
        first = [copy(i, 0, me, sibling, src=ins[i]) for i in range(n)]
        first += [copy(i, 1 + j, me, (*chip, c), src=ins[i]) for i in range(n) for j, chip in enumerate(chips)]
        landed = [copy(i, 1 + j, (*chip, c), me) for j, chip in enumerate(chips) for i in range(n)]
        passed = [copy(i, 4 + j, (*chip, c), sibling) for j, chip in enumerate(chips) for i in range(n)]
        from_sibling = [copy(i, 0, sibling, me) for i in range(n)]
        from_sibling += [copy(i, 4 + j, (*chip, 1 - c), me) for i in range(n) for j, chip in enumerate(chips)]
        return mine, first, landed, passed, from_sibling

    def start(self, ins, outs, sems):
        mine, first, _, _, _ = self._copies(ins, outs, sems)
        for cp in mine + first:
            cp.start()

    def mid(self, ins, outs, sems):
        _, _, landed, passed, _ = self._copies(ins, outs, sems)
        for got, fwd in zip(landed, passed):
            got.wait_recv()
            fwd.start()

    def end(self, ins, outs, sems):
        mine, first, _, passed, from_sibling = self._copies(ins, outs, sems)
        for cp in from_sibling:
            cp.wait_recv()
        for cp in first + passed:
            cp.wait_send()
        for cp in mine:
            cp.wait()


class _CoreExchange:
    def __init__(self, grads):
        self.ins = list(grads)
        n = self.n = len(grads)
        self.out_shape = [jax.ShapeDtypeStruct((N_CHIP, 1) + g.shape[2:], g.dtype) for g in grads]
        self.scratch = [pltpu.SemaphoreType.DMA((n,)), pltpu.SemaphoreType.DMA((n,))]

    def _copies(self, ins, outs, sems):
        x, y, c = _place()
        return [pltpu.make_async_remote_copy(
            src_ref=ins[i].at[:, pl.ds(1 - c, 1)], dst_ref=outs[i], send_sem=sems[0].at[i],
            recv_sem=sems[1].at[i], device_id=(x, y, 1 - c), device_id_type=MESH) for i in range(self.n)]

    def start(self, ins, outs, sems):
        for cp in self._copies(ins, outs, sems):
            cp.start()

    mid = None

    def end(self, ins, outs, sems):
        for cp in self._copies(ins, outs, sems):
            cp.wait()


class _ChipExchange:
    def __init__(self, parts):
        self.ins = list(parts)
        n = self.n = len(parts)
        self.out_shape = [jax.ShapeDtypeStruct(p.shape, p.dtype) for p in parts]
        self.scratch = [pltpu.SemaphoreType.DMA((3 * n,)), pltpu.SemaphoreType.DMA((3 * n,)),
                        pltpu.SemaphoreType.DMA((n,))]

    def _copies(self, ins, outs, sems):
        send_sems, recv_sems, local_sems = sems
        x, y, c = _place()
        my_chip = 2 * x + y
        copies = []
        for i in range(self.n):
            copies.append(pltpu.make_async_copy(ins[i].at[my_chip], outs[i].at[my_chip], local_sems.at[i]))
            for k, (qx, qy) in enumerate([(1 - x, y), (x, 1 - y), (1 - x, 1 - y)]):
                copies.append(pltpu.make_async_remote_copy(
                    src_ref=ins[i].at[2 * qx + qy], dst_ref=outs[i].at[my_chip], send_sem=send_sems.at[3 * i + k],
                    recv_sem=recv_sems.at[3 * i + k], device_id=(qx, qy, c), device_id_type=MESH))
        return copies

    def start(self, ins, outs, sems):
        for cp in self._copies(ins, outs, sems):
            cp.start()

    mid = None

    def end(self, ins, outs, sems):
        for cp in self._copies(ins, outs, sems):
            cp.wait()


def _call(body, *, name, grid, in_specs, out_specs, out_shape, args, scratch=(), sem=None, hosted=()):
    n_in, n_out, n_scr = len(in_specs), len(out_specs), len(scratch)
    x_in = [len(p.ins) for p in hosted]
    x_scr = [len(p.scratch) for p in hosted]
    steps = int(np.prod(grid))

    def wrapped(*refs):
        refs = list(refs)
        ins, refs = refs[:n_in], refs[n_in:]
        x_ins = [[refs.pop(0) for _ in range(k)] for k in x_in]
        outs, refs = refs[:n_out], refs[n_out:]
        x_outs = [[refs.pop(0) for _ in range(k)] for k in x_in]
        scr, refs = refs[:n_scr], refs[n_scr:]
        x_sems = [[refs.pop(0) for _ in range(k)] for k in x_scr]
        step = 0
        for d in range(len(grid)):
            step = step * grid[d] + pl.program_id(d)

        def phase(which, at):
            fns = [(getattr(p, which), a) for p, a in zip(hosted, zip(x_ins, x_outs, x_sems)) if getattr(p, which)]
            if fns:
                @pl.when(step == at)
                def _():
                    for fn, a in fns:
                        fn(*a)

        phase("start", 0)
        phase("mid", (3 * steps) // 4)
        body(*ins, *outs, *scr)
        phase("end", steps - 1)

    results = pl.pallas_call(
        wrapped, name=name, grid=grid,
        in_specs=list(in_specs) + [ANY] * sum(x_in), out_specs=list(out_specs) + [ANY] * sum(x_in),
        out_shape=list(out_shape) + [s for p in hosted for s in p.out_shape],
        scratch_shapes=list(scratch) + [s for p in hosted for s in p.scratch],
        compiler_params=_params(*(("arbitrary",) * len(grid) if hosted else sem)),
    )(*args, *[a for p in hosted for a in p.ins])
    own, rest = list(results[:n_out]), list(results[n_out:])
    return own, [[rest.pop(0) for _ in range(k)] for k in x_in]


def _exchange(programs, name):
    x_in = [len(p.ins) for p in programs]
    x_scr = [len(p.scratch) for p in programs]

    def body(*refs):
        refs = list(refs)
        x_ins = [[refs.pop(0) for _ in range(k)] for k in x_in]
        x_outs = [[refs.pop(0) for _ in range(k)] for k in x_in]
        x_sems = [[refs.pop(0) for _ in range(k)] for k in x_scr]
        for which in ("start", "mid", "end"):
            for p, a in zip(programs, zip(x_ins, x_outs, x_sems)):
                if getattr(p, which):
                    getattr(p, which)(*a)

    results = list(pl.pallas_call(
        body, name=name, in_specs=[ANY] * sum(x_in), out_specs=[ANY] * sum(x_in),
        out_shape=[s for p in programs for s in p.out_shape],
        scratch_shapes=[s for p in programs for s in p.scratch],
    )(*[a for p in programs for a in p.ins]))
    return [[results.pop(0) for _ in range(k)] for k in x_in]


def _nt(a, b):
    return lax.dot_general(a, b, (((1,), (1,)), ((), ())), preferred_element_type=F32)


def _nn(a, b):
    return lax.dot_general(a, b, (((1,), (0,)), ((), ())), preferred_element_type=F32)


def _tn(a, b):
    return lax.dot_general(a, b, (((0,), (0,)), ((), ())), preferred_element_type=F32)


def _rms(x, g):
    r = lax.rsqrt(jnp.mean(x * x, axis=-1, keepdims=True) + EPS)
    return x * r, r


def _rms_bwd(dn, xhat, r, g):
    dg = jnp.sum(dn * xhat, axis=0, keepdims=True)
    dxh = dn * g
    dx = r * (dxh - xhat * jnp.mean(dxh * xhat, axis=-1, keepdims=True))
    return dx, dg


def _ffn_fwd(x, g, wg_t, wu_t, wd, name, hosted=(), head=None):
    t = x.shape[0]
    tm = 256

    def body(*refs):
        x_ref, g_ref, wg_ref, wu_ref, wd_ref = refs[:5]
        h_ref, n_ref, a_ref, b_ref, hff_ref = refs[-5 if head is None else -7:][:5]
        xhat, _ = _rms(x_ref[...], g_ref[...])
        n = (xhat * g_ref[...]).astype(BF16)
        n_ref[...] = n
        for c in range(0, D_FF, FF_CHUNK):
            cols = slice(c, c + FF_CHUNK)
            a = _nt(n, wg_ref[cols, :])
            b = _nt(n, wu_ref[cols, :])
            a_ref[:, cols] = a.astype(BF16)
            b_ref[:, cols] = b.astype(BF16)
            hff_ref[:, cols] = (a * jax.nn.sigmoid(a) * b).astype(BF16)
        h = x_ref[...] + 0.5 * _nn(hff_ref[...], wd_ref[...])
        if head is None:
            h_ref[...] = h
            return
        gf_ref, t_ref, loss_ref, dgf_ref = refs[5], refs[6], refs[-2], refs[-1]

        @pl.when(pl.program_id(0) == 0)
        def _():
            loss_ref[...] = jnp.zeros_like(loss_ref)
            dgf_ref[...] = jnp.zeros_like(dgf_ref)

        yhat, r = _rms(h, gf_ref[...])
        err = yhat * gf_ref[...] - t_ref[...]
        loss_ref[...] += 0.5 * jnp.sum(jnp.mean(err * err, axis=-1, keepdims=True), axis=0, keepdims=True)
        h_ref[...], dgf = _rms_bwd(err * (1.0 / D_MODEL), yhat, r, gf_ref[...])
        dgf_ref[...] += dgf

    row = pl.BlockSpec((tm, D_MODEL), lambda i: (i, 0))
    hid = pl.BlockSpec((tm, D_FF), lambda i: (i, 0))
    sds = jax.ShapeDtypeStruct
    in_specs = [row, _resident(g), _resident(wg_t), _resident(wu_t), _resident(wd)]
    out_specs = [row, row, hid, hid, hid]
    out_shape = [sds((t, D_MODEL), F32), sds((t, D_MODEL), BF16), sds((t, D_FF), BF16), sds((t, D_FF), BF16),
                 sds((t, D_FF), BF16)]
    args = (x, g, wg_t, wu_t, wd)
    if head is not None:
        in_specs += [_resident(head[0]), row]
        out_specs += [pl.BlockSpec((1, 128), lambda i: (0, 0)), pl.BlockSpec((1, D_MODEL), lambda i: (0, 0))]
        out_shape += [sds((1, 128), F32), sds((1, D_MODEL), F32)]
        args += tuple(head)
    return _call(body, name=name, grid=(t // tm,), in_specs=in_specs, out_specs=out_specs, out_shape=out_shape,
                 sem=("parallel",) if head is None else ("arbitrary",), args=args, hosted=hosted)


def _ffn_bwd(x, a_pre, b_pre, g, dh, wg_t, wu_t, wd, name):
    t = x.shape[0]
    tm = 256

    def body(x_ref, a_ref, b_ref, g_ref, dh_ref, wg_ref, wu_ref, wd_ref,
             dx_ref, dg_ref, da_ref, db_ref):
        @pl.when(pl.program_id(0) == 0)
        def _():
            dg_ref[...] = jnp.zeros_like(dg_ref)

        dhh = (0.5 * dh_ref[...]).astype(BF16)
        for c in range(0, D_FF, FF_CHUNK):
            cols = slice(c, c + FF_CHUNK)
            a = a_ref[:, cols].astype(F32)
            b = b_ref[:, cols].astype(F32)
            s = jax.nn.sigmoid(a)
            silu = a * s
            dhff = _nt(dhh, wd_ref[cols, :])
            da_ref[:, cols] = (dhff * b * (s * (1.0 + a * (1.0 - s)))).astype(BF16)
            db_ref[:, cols] = (dhff * silu).astype(BF16)
        dn = _nn(da_ref[...], wg_ref[...]) + _nn(db_ref[...], wu_ref[...])
        xhat, r = _rms(x_ref[...], g_ref[...])
        dx, dg = _rms_bwd(dn, xhat, r, g_ref[...])
        dx_ref[...] = dh_ref[...] + dx
        dg_ref[...] += dg

    row = pl.BlockSpec((tm, D_MODEL), lambda i: (i, 0))
    hid = pl.BlockSpec((tm, D_FF), lambda i: (i, 0))
    return pl.pallas_call(
        body, name=name, grid=(t // tm,),
        in_specs=[row, hid, hid, _resident(g), row, _resident(wg_t), _resident(wu_t), _resident(wd)],
        out_specs=[row, pl.BlockSpec((1, D_MODEL), lambda i: (0, 0)), hid, hid],
        out_shape=[jax.ShapeDtypeStruct((t, D_MODEL), F32), jax.ShapeDtypeStruct((1, D_MODEL), F32),
                   jax.ShapeDtypeStruct((t, D_FF), BF16), jax.ShapeDtypeStruct((t, D_FF), BF16)],
        compiler_params=_params("arbitrary"),
    )(x, a_pre, b_pre, g, dh, wg_t, wu_t, wd)


def _tn_matmul(a, b, rc, name, hosted=(), scale=None):
    t, r = a.shape
    c = b.shape[1]
    tk = min(t, 2048)

    def body(a_ref, b_ref, o_ref):
        @pl.when(pl.program_id(1) == 0)
        def _():
            o_ref[...] = jnp.zeros_like(o_ref)

        o_ref[...] += _tn(a_ref[...].astype(BF16), b_ref[...].astype(BF16))
        if scale is not None:
            @pl.when(pl.program_id(1) == t // tk - 1)
            def _():
                o_ref[...] *= scale

    (out,), got = _call(
        body, name=name, grid=(r // rc, t // tk),
        in_specs=[pl.BlockSpec((tk, rc), lambda i, k: (k, i)), pl.BlockSpec((tk, c), lambda i, k: (k, 0))],
        out_specs=[pl.BlockSpec((rc, c), lambda i, k: (i, 0))],
        out_shape=[jax.ShapeDtypeStruct((r, c), F32)],
        sem=("parallel", "arbitrary"), args=(a, b), hosted=hosted)
    return out, got


def _unfold(blk_ref, slab_ref, d):
    if d == 1:
        return blk_ref[...]
    n = blk_ref.shape[0]
    for r in range(d):
        for half in range(2):
            c0 = 256 * r + 128 * half
            slab_ref[half, pl.ds(r, n, stride=d), :] = blk_ref[:, c0:c0 + 128]
    return jnp.concatenate([slab_ref[0], slab_ref[1]], axis=1)


def _fold(x, slab_ref, out_ref, d):
    if d == 1:
        out_ref[...] = x.astype(out_ref.dtype)
        return
    n, w = out_ref.shape[0], x.shape[1]
    for part in range(w // 128):
        slab_ref[part] = x[:, 128 * part:128 * (part + 1)]
    for r in range(d):
        for part in range(w // 128):
            c0 = w * r + 128 * part
            out_ref[:, c0:c0 + 128] = slab_ref[part, pl.ds(r, n, stride=d), :].astype(out_ref.dtype)


DILATIONS = tuple(d for _, d in DIL_GROUPS)


PIECE_W = (256,) * 9 + (512, 128, 128, GATE_W)
PIECE_D = DILATIONS * 3 + (1, 1, 1, 1)


def _inproj_fwd(h, g, w_t, b_in, name, hosted=()):
    t = h.shape[0]
    tm, nc = 512, 256
    dilated = [(gi, d) for gi, d in enumerate(DILATIONS) if d > 1]

    def body(h_ref, g_ref, w_ref, b_ref, u_ref, zq_ref, zg_ref, *rest):
        zf_refs, slabs = rest[:len(dilated)], rest[len(dilated):]
        xhat, _ = _rms(h_ref[...], g_ref[...])
        u = (xhat * g_ref[...]).astype(BF16)
        u_ref[...] = u
        for c in range(D_IN // nc):
            z = _nt(u, w_ref[c * nc:(c + 1) * nc, :]) + b_ref[:, c * nc:(c + 1) * nc]
            if c < QKV_W // nc:
                zq_ref[:, c * nc:(c + 1) * nc] = z.astype(BF16)
            else:
                zg_ref[:, c * nc - QKV_W:(c + 1) * nc - QKV_W] = z.astype(BF16)
            part, gi = divmod(c, len(DILATIONS))
            for k, (gk, d) in enumerate(dilated):
                if part < 3 and gi == gk:
                    slab, n = slabs[3 * k + part], tm // d
                    slab[0] = z[:, :128]
                    slab[1] = z[:, 128:]
                    for r in range(d):
                        for half in range(2):
                            c0 = 768 * r + 256 * part + 128 * half
                            zf_refs[k][:, c0:c0 + 128] = slab[half, pl.ds(r, n, stride=d), :].astype(BF16)

    row = lambda w: pl.BlockSpec((tm, w), lambda i: (i, 0))
    full = _resident
    sds = jax.ShapeDtypeStruct
    return _call(
        body, name=name, grid=(t // tm,),
        in_specs=[row(D_MODEL), full(g), full(w_t), full(b_in)],
        out_specs=[row(D_MODEL), row(QKV_W), row(GATE_W)]
        + [pl.BlockSpec((tm // d, d * 768), lambda i: (i, 0)) for _, d in dilated],
        out_shape=[sds((t, D_MODEL), BF16), sds((t, QKV_W), BF16), sds((t, GATE_W), BF16)]
        + [sds((t // d, d * 768), BF16) for _, d in dilated],
        scratch=[pltpu.VMEM((2, tm, 128), F32)] * (3 * len(dilated)),
        sem=("parallel",), args=(h, g, w_t, b_in), hosted=hosted)


def _inproj_bwd(pieces, h, g, dh_res, w_t, name):
    t = h.shape[0]
    tm = 256
    npiece = len(PIECE_W)
    offs = np.concatenate([[0], np.cumsum(PIECE_W)]).tolist()

    def body(*refs):
        p_refs = refs[:npiece]
        h_ref, g_ref, dhr_ref, w_ref, dh_ref, dz_ref, db_ref, dg_ref = refs[npiece:npiece + 8]
        slabs = list(refs[npiece + 8:])
        i = pl.program_id(0)

        @pl.when(i == 0)
        def _():
            db_ref[...] = jnp.zeros_like(db_ref)
            dg_ref[...] = jnp.zeros_like(dg_ref)

        du = jnp.zeros((tm, D_MODEL), F32)
        for k in range(npiece):
            o, w = offs[k], PIECE_W[k]
            token_order = _unfold(p_refs[k], slabs.pop(), PIECE_D[k]).astype(BF16) if PIECE_D[k] > 1 else None
            for c0 in range(0, w, 512):
                cw = min(512, w - c0)
                pz = p_refs[k][:, c0:c0 + cw] if token_order is None else token_order
                dz_ref[:, o + c0:o + c0 + cw] = pz
                db_ref[:, o + c0:o + c0 + cw] += jnp.sum(pz.astype(F32), axis=0, keepdims=True)
                du = du + _nn(pz, w_ref[o + c0:o + c0 + cw, :])
        xhat, r = _rms(h_ref[...], g_ref[...])
        dx, dg = _rms_bwd(du, xhat, r, g_ref[...])
        dh_ref[...] = dhr_ref[...] + dx
        dg_ref[...] += dg

    row = lambda w: pl.BlockSpec((tm, w), lambda i: (i, 0))
    full = lambda shp: pl.BlockSpec(shp, lambda i: (0, 0))
    return pl.pallas_call(
        body, name=name, grid=(t // tm,),
        in_specs=[pl.BlockSpec((tm // d, d * w), lambda i: (i, 0)) for w, d in zip(PIECE_W, PIECE_D)]
        + [row(D_MODEL), _resident(g), row(D_MODEL), _resident(w_t)],
        out_specs=[row(D_MODEL), row(D_IN), full((1, D_IN)), full((1, D_MODEL))],
        out_shape=[jax.ShapeDtypeStruct((t, D_MODEL), F32), jax.ShapeDtypeStruct((t, D_IN), BF16),
                   jax.ShapeDtypeStruct((1, D_IN), F32), jax.ShapeDtypeStruct((1, D_MODEL), F32)],
        scratch_shapes=[pltpu.VMEM((2, tm, 128), F32)] * sum(d > 1 for d in PIECE_D),
        compiler_params=_params("arbitrary"),
    )(*pieces, h, g, dh_res, w_t)


def _t5_bucket(dist):
    max_exact = N_BUCKETS // 2
    n = jnp.maximum(dist, 0)
    nf = jnp.maximum(n, 1).astype(jnp.float32)
    large = max_exact + (jnp.log(nf / max_exact) / math.log(MAX_DISTANCE / max_exact)
                         * (N_BUCKETS - max_exact)).astype(jnp.int32)
    large = jnp.minimum(large, N_BUCKETS - 1)
    return jnp.where(n < max_exact, n, large)


ATT_CFG = ((1, 128, 0, 4), (4, 128, 4, 4), (16, 128, 8, 4), (1, B_WINDOW - 1, A_HEADS, 8))


def _bucket_tiles():
    qi = jnp.arange(BLOCK)[:, None]
    ki = jnp.arange(2 * BLOCK)[None, :]
    dist = qi + BLOCK - ki
    return jnp.stack([_t5_bucket(dist * cfg[0]) for cfg in ATT_CFG]).astype(jnp.int32)


def _band(max_steps):
    row = lax.broadcasted_iota(jnp.int32, (BLOCK, 2 * BLOCK), 0)
    col = lax.broadcasted_iota(jnp.int32, (BLOCK, 2 * BLOCK), 1)
    dist = row + BLOCK - col
    return (dist >= 0) & (dist <= max_steps)


def _bias_build(table, buckets):
    def body(tab_ref, bt_ref, out_ref):
        col = lax.broadcasted_iota(jnp.int32, (BLOCK, 2 * BLOCK), 1)
        for ci, (_, max_steps, h0, nh) in enumerate(ATT_CFG):
            bt = bt_ref[ci]
            band = _band(max_steps)
            for h in range(h0, h0 + nh):
                acc = lax.fori_loop(0, N_BUCKETS, lambda b, acc: jnp.where(bt == b, tab_ref[b, h], acc),
                                    jnp.zeros((BLOCK, 2 * BLOCK), F32))
                out_ref[0, BLOCK * h:BLOCK * (h + 1), :] = jnp.where(band & (col >= BLOCK), acc, NEG)
                out_ref[1, BLOCK * h:BLOCK * (h + 1), :] = jnp.where(band, acc, NEG)

    return pl.pallas_call(
        body, name="bias_build",
        in_specs=[pl.BlockSpec(memory_space=pltpu.SMEM), pl.BlockSpec(memory_space=pltpu.VMEM)],
        out_specs=pl.BlockSpec(memory_space=pltpu.VMEM),
        out_shape=jax.ShapeDtypeStruct((2, TOTAL_HEADS * BLOCK, 2 * BLOCK), F32),
    )(table, buckets)


def _bias_reduce(dbias, buckets, dsink_rows):
    def body(db_ref, bt_ref, ds_ref, out_ref, sink_ref):
        ri = lax.broadcasted_iota(jnp.int32, (N_BUCKETS, 128), 0)
        ci = lax.broadcasted_iota(jnp.int32, (N_BUCKETS, 128), 1)

        def per_bucket(b, acc):
            for cfg_i, (_, _, h0, nh) in enumerate(ATT_CFG):
                hit = bt_ref[cfg_i] == b
                for h in range(h0, h0 + nh):
                    val = jnp.sum(jnp.where(hit, db_ref[BLOCK * h:BLOCK * (h + 1), :], 0.0))
                    acc = jnp.where((ri == b) & (ci == h), val, acc)
            return acc

        out_ref[...] = lax.fori_loop(0, N_BUCKETS, per_bucket, jnp.zeros((N_BUCKETS, 128), F32))
        for h in range(8):
            sink_ref[h:h + 1, :] = jnp.sum(ds_ref[BLOCK * h:BLOCK * (h + 1), :], axis=0, keepdims=True)

    return pl.pallas_call(
        body, name="bias_reduce",
        in_specs=[pl.BlockSpec(memory_space=pltpu.VMEM)] * 3,
        out_specs=[pl.BlockSpec(memory_space=pltpu.VMEM)] * 2,
        out_shape=[jax.ShapeDtypeStruct((N_BUCKETS, 128), F32), jax.ShapeDtypeStruct((8, 128), F32)],
    )(dbias, buckets, dsink_rows)


class _Att:
    def __init__(self, cfg_i):
        stride, _, h0, nh = ATT_CFG[cfg_i]
        self.d = stride if cfg_i < 3 else 1
        self.h0, self.nh = h0, nh
        self.row_w = QKV_W if self.d == 1 else 3 * 256
        if cfg_i < 3:
            self.nq, self.wkv = 1, 256
            self.q_unit = [cfg_i if self.d == 1 else 0]
            self.k_unit, self.v_unit = (3 + cfg_i, 6 + cfg_i) if self.d == 1 else (1, 2)
            self.sinks = False
        else:
            self.nq, self.wkv = 2, 128
            self.q_unit = [9, 10]
            self.k_unit, self.v_unit = 22, 23
            self.sinks = True
        self.wq = 256 * self.nq


def _att_in_specs(cf, bsz):
    uq, ukv = cf.row_w // 256, cf.row_w // cf.wkv
    specs = [pl.BlockSpec((bsz, BLOCK, 256), functools.partial(lambda r, j, u: (0, j, r * uq + u), u=u))
             for u in cf.q_unit]
    for unit in (cf.k_unit, cf.v_unit):
        specs.append(pl.BlockSpec((bsz, BLOCK, cf.wkv),
                                  functools.partial(lambda r, j, u: (0, jnp.maximum(j - 1, 0), r * ukv + u), u=unit)))
        specs.append(pl.BlockSpec((bsz, BLOCK, cf.wkv),
                                  functools.partial(lambda r, j, u: (0, j, r * ukv + u), u=unit)))
    for qb in range(cf.nq):
        specs.append(pl.BlockSpec((None, HEADS_PER_BLOCK * BLOCK, 2 * BLOCK), functools.partial(
            lambda r, j, u: (jnp.minimum(j, 1), u, 0), u=cf.h0 // HEADS_PER_BLOCK + qb)))
    if cf.sinks:
        specs += [pl.BlockSpec((HEADS_PER_BLOCK * BLOCK, 128), functools.partial(lambda r, j, u: (u, 0), u=qb))
                  for qb in range(cf.nq)]
    return specs


HEADS_PER_BLOCK = 4


def _head_masks(rows):
    head = lax.broadcasted_iota(jnp.int32, (rows, 256), 1) // HEAD_DIM
    return [head == h for h in range(HEADS_PER_BLOCK)]


def _stack_heads(x, masks):
    return jnp.concatenate([jnp.where(m, x, jnp.zeros_like(x)) for m in masks], axis=0)


def _unstack_heads(x4, masks):
    blocks = [x4[BLOCK * h:BLOCK * (h + 1)] for h in range(HEADS_PER_BLOCK)]
    return jnp.where(masks[0], blocks[0], jnp.where(masks[1], blocks[1], jnp.where(masks[2], blocks[2], blocks[3])))


def _row_value(x):
    return jnp.max(x, axis=-1, keepdims=True)


def _kv_operands(cf, x):
    if cf.wkv == 256:
        return [x]
    lane = lax.broadcasted_iota(jnp.int32, x.shape, 1)
    swapped = pltpu.roll(x, HEAD_DIM, 1)
    halves = [jnp.where(lane < HEAD_DIM, x, swapped), jnp.where(lane < HEAD_DIM, swapped, x)]
    return [jnp.concatenate([half, half], axis=1) for half in halves]


def _kv_fold(cf, grads):
    if cf.wkv == 256:
        return grads[0]
    folded = []
    for g in grads:
        x = g[:, :128] + g[:, 128:]
        folded.append(x + pltpu.roll(x, HEAD_DIM, 1))
    lane = lax.broadcasted_iota(jnp.int32, folded[0].shape, 1)
    return jnp.where(lane < HEAD_DIM, folded[0], folded[1])


def _attn_fwd(cf, zf, bias, sinks, name, hosted=()):
    bsz, l, _ = zf.shape
    nb = l // BLOCK

    def body(*refs):
        refs = list(refs)
        q_refs = [refs.pop(0) for _ in range(cf.nq)]
        kp_ref, kc_ref, vp_ref, vc_ref = [refs.pop(0) for _ in range(4)]
        bias_refs = [refs.pop(0) for _ in range(cf.nq)]
        sink_refs = [refs.pop(0) for _ in range(cf.nq)] if cf.sinks else None
        o_ref, lse_ref = refs
        masks, kv_masks = _head_masks(BLOCK), _head_masks(2 * BLOCK)
        sinks4 = [_row_value(ref[...]) for ref in sink_refs] if cf.sinks else None
        for bi in range(bsz):
            k = jnp.concatenate([kp_ref[bi], kc_ref[bi]], axis=0)
            v = jnp.concatenate([vp_ref[bi], vc_ref[bi]], axis=0)
            k_ops, v_ops = _kv_operands(cf, k), _kv_operands(cf, v)
            for qb in range(cf.nq):
                cols = slice(256 * qb, 256 * (qb + 1))
                kb, vb = k_ops[qb], v_ops[qb]
                q4 = _stack_heads(q_refs[qb][bi] * (HEAD_DIM ** -0.5), masks)
                s = _nt(q4, kb) + bias_refs[qb][...]
                m = jnp.max(s, axis=-1, keepdims=True)
                if cf.sinks:
                    sk = sinks4[qb]
                    m = jnp.maximum(m, sk)
                p = jnp.exp(s - m)
                den = jnp.sum(p, axis=-1, keepdims=True)
                if cf.sinks:
                    den = den + jnp.exp(sk - m)
                pn = (p * (1.0 / den)).astype(BF16)
                p_lanes = jnp.concatenate([pn[BLOCK * h:BLOCK * (h + 1)] for h in range(HEADS_PER_BLOCK)], axis=1)
                v4 = jnp.concatenate([jnp.where(mk, vb, jnp.zeros_like(vb)) for mk in kv_masks], axis=0)
                o_ref[bi, :, cols] = _nn(p_lanes, v4)
                lse_ref[bi, :, cols] = _unstack_heads(
                    jnp.broadcast_to(m + jnp.log(den), (HEADS_PER_BLOCK * BLOCK, 256)), masks)

    in_specs = _att_in_specs(cf, bsz)
    args = [zf] * (cf.nq + 4) + [bias] * cf.nq + ([sinks] * cf.nq if cf.sinks else [])
    out = pl.BlockSpec((bsz, BLOCK, cf.wq), lambda r, j: (0, j, r))
    shape = jax.ShapeDtypeStruct((bsz, l, cf.d * cf.wq), F32)
    return _call(
        body, name=name, grid=(cf.d, nb), in_specs=in_specs, out_specs=[out, out], out_shape=[shape, shape],
        sem=("parallel", "arbitrary"), args=args, hosted=hosted)


def _attn_bwd(cf, zf, bias, sinks, stats, dy, name, hosted=()):
    bsz, l, _ = zf.shape
    nb = l // BLOCK

    def body(*refs):
        refs = list(refs)
        q_refs = [refs.pop(0) for _ in range(cf.nq)]
        kp_ref, kc_ref, vp_ref, vc_ref = [refs.pop(0) for _ in range(4)]
        bias_refs = [refs.pop(0) for _ in range(cf.nq)]
        sink_refs = [refs.pop(0) for _ in range(cf.nq)] if cf.sinks else None
        st_ref, dy_ref, dq_ref, dk_ref, dv_ref, dbias_ref = [refs.pop(0) for _ in range(6)]
        dsink_ref = refs.pop(0) if cf.sinks else None
        dk_acc, dv_acc = refs
        r, j = pl.program_id(0), pl.program_id(1)

        @pl.when((r == 0) & (j == 0))
        def _():
            dbias_ref[...] = jnp.zeros_like(dbias_ref)
            if cf.sinks:
                dsink_ref[...] = jnp.zeros_like(dsink_ref)

        @pl.when(j == 0)
        def _():
            dk_acc[...] = jnp.zeros_like(dk_acc)
            dv_acc[...] = jnp.zeros_like(dv_acc)

        masks = _head_masks(BLOCK)
        cur = pl.ds(pl.multiple_of(j * BLOCK, BLOCK), BLOCK)
        prev = pl.ds(pl.multiple_of(jnp.maximum(j - 1, 0) * BLOCK, BLOCK), BLOCK)
        sinks4 = [_row_value(ref[...]) for ref in sink_refs] if cf.sinks else None
        ds_sum, dsink_sum = [None] * cf.nq, [None] * cf.nq
        for bi in range(bsz):
            k = jnp.concatenate([kp_ref[bi], kc_ref[bi]], axis=0)
            v = jnp.concatenate([vp_ref[bi], vc_ref[bi]], axis=0)
            dk_blocks, dv_blocks = [], []
            k_ops, v_ops = _kv_operands(cf, k), _kv_operands(cf, v)
            for qb in range(cf.nq):
                cols = slice(256 * qb, 256 * (qb + 1))
                kb, vb = k_ops[qb], v_ops[qb]
                q4 = _stack_heads(q_refs[qb][bi] * (HEAD_DIM ** -0.5), masks)
                st = st_ref[bi, :, 128 * qb:128 * (qb + 1)]
                lt4, e4 = _read_stats(st, 0), _read_stats(st, 1)
                pa = jnp.exp(_nt(q4, kb) + bias_refs[qb][...] - lt4)
                dy = dy_ref[bi, :, cols]
                dy4 = _stack_heads(dy, masks)
                ds = pa * (_nt(dy4, vb) - e4)
                ds_sum[qb] = ds if ds_sum[qb] is None else ds_sum[qb] + ds
                if cf.sinks:
                    dsk = jnp.exp(sinks4[qb] - lt4) * e4
                    dsink_sum[qb] = dsk if dsink_sum[qb] is None else dsink_sum[qb] + dsk
                dsb = ds.astype(BF16)
                dq_ref[bi, :, cols] = (_unstack_heads(_nn(dsb, kb), masks) * (HEAD_DIM ** -0.5)).astype(dq_ref.dtype)
                dk_blocks.append(_tn(dsb, q4))
                dv_blocks.append(_tn(pa.astype(BF16), dy4))
            dk_new, dv_new = _kv_fold(cf, dk_blocks), _kv_fold(cf, dv_blocks)
            dk_acc[bi, cur, :] += dk_new[BLOCK:]
            dv_acc[bi, cur, :] += dv_new[BLOCK:]
            dk_acc[bi, prev, :] += dk_new[:BLOCK]
            dv_acc[bi, prev, :] += dv_new[:BLOCK]
        for qb in range(cf.nq):
            rows = slice(HEADS_PER_BLOCK * BLOCK * qb, HEADS_PER_BLOCK * BLOCK * (qb + 1))
            dbias_ref[rows, :] += ds_sum[qb]
            if cf.sinks:
                dsink_ref[rows, :] -= dsink_sum[qb]

        @pl.when(j == nb - 1)
        def _():
            dk_ref[...] = dk_acc[...].astype(dk_ref.dtype)
            dv_ref[...] = dv_acc[...].astype(dv_ref.dtype)

    tok = pl.BlockSpec((bsz, BLOCK, cf.wq), lambda r, j: (0, j, r))
    in_specs = _att_in_specs(cf, bsz) + [pl.BlockSpec((bsz, BLOCK, cf.wq // 2), lambda r, j: (0, j, r)), tok]
    args = [zf] * (cf.nq + 4) + [bias] * cf.nq + ([sinks] * cf.nq if cf.sinks else []) + [stats, dy]
    seq = pl.BlockSpec((bsz, l, cf.wkv), lambda r, j: (0, 0, r))
    out_specs = [tok, seq, seq, pl.BlockSpec((cf.nh * BLOCK, 2 * BLOCK), lambda r, j: (0, 0))]
    grad_dtype = BF16 if cf.d == 1 else F32
    out_shape = [jax.ShapeDtypeStruct((bsz, l, cf.d * cf.wq), grad_dtype),
                 jax.ShapeDtypeStruct((bsz, l, cf.d * cf.wkv), grad_dtype),
                 jax.ShapeDtypeStruct((bsz, l, cf.d * cf.wkv), grad_dtype),
                 jax.ShapeDtypeStruct((cf.nh * BLOCK, 2 * BLOCK), F32)]
    if cf.sinks:
        out_specs.append(pl.BlockSpec((cf.nh * BLOCK, 128), lambda r, j: (0, 0)))
        out_shape.append(jax.ShapeDtypeStruct((cf.nh * BLOCK, 128), F32))
    return _call(
        body, name=name, grid=(cf.d, nb), in_specs=in_specs, out_specs=out_specs, out_shape=out_shape,
        scratch=[pltpu.VMEM((bsz, l, cf.wkv), F32), pltpu.VMEM((bsz, l, cf.wkv), F32)],
        sem=("arbitrary", "arbitrary"), args=args, hosted=hosted)


def _merge_fwd(o_a, lse_a, o_b, zg, h, wa_t, wb_t, wout, name):
    t = h.shape[0]
    tm = 512

    def body(o1, o2, o3, l1, l2, l3, ob_ref, zg_ref, h_ref, wa_ref, wb_ref, wo_ref, h2_ref, ya_ref, lt_ref, *slabs):
        o = [_unfold(ref, slabs[i], d) for i, (ref, d) in enumerate(zip((o1, o2, o3), DILATIONS))]
        l = [_unfold(ref, slabs[3 + i], d) for i, (ref, d) in enumerate(zip((l1, l2, l3), DILATIONS))]
        m = jnp.maximum(jnp.maximum(l[0], l[1]), l[2])
        e1, e2, e3 = jnp.exp(l[0] - m), jnp.exp(l[1] - m), jnp.exp(l[2] - m)
        se = e1 + e2 + e3
        ya = (e1 / se) * o[0] + (e2 / se) * o[1] + (e3 / se) * o[2]
        ya_ref[...] = ya
        lt_ref[...] = m + jnp.log(se)
        pa = _nt(ya.astype(BF16), wa_ref[...])
        pb = _nt(ob_ref[...].astype(BF16), wb_ref[...])
        merged = (jax.nn.sigmoid(zg_ref[:, :D_MODEL].astype(F32)) * pa
                  + jax.nn.sigmoid(zg_ref[:, D_MODEL:].astype(F32)) * pb)
        h2_ref[...] = h_ref[...] + _nn(merged.astype(BF16), wo_ref[...])

    row = lambda w: pl.BlockSpec((tm, w), lambda i: (i, 0))
    folded = [pl.BlockSpec((tm // d, d * 256), lambda i: (i, 0)) for d in DILATIONS]
    full = _resident
    return pl.pallas_call(
        body, name=name, grid=(t // tm,),
        in_specs=folded + folded + [row(512), row(GATE_W), row(D_MODEL), full(wa_t), full(wb_t), full(wout)],
        out_specs=[row(D_MODEL), row(256), row(256)],
        out_shape=[jax.ShapeDtypeStruct((t, D_MODEL), F32), jax.ShapeDtypeStruct((t, 256), F32),
                   jax.ShapeDtypeStruct((t, 256), F32)],
        scratch_shapes=[pltpu.VMEM((2, tm, 128), F32)] * 6,
        compiler_params=_params("parallel"),
    )(*o_a, *lse_a, o_b, zg, h, wa_t, wb_t, wout)


STAT_LANES = 32


def _head_stats(lt, dy, y):
    rows, w = dy.shape
    lane = lax.broadcasted_iota(jnp.int32, (rows, w), 1)
    out_lane = lax.broadcasted_iota(jnp.int32, (rows, w // 2), 1)
    prod = dy * y
    out = jnp.zeros((rows, w // 2), F32)
    for hd in range(w // HEAD_DIM):
        mine = lane // HEAD_DIM == hd
        dot = jnp.sum(jnp.where(mine, prod, 0.0), axis=-1, keepdims=True)
        both = jnp.where(out_lane % STAT_LANES < STAT_LANES // 2, _row_value(jnp.where(mine, lt, NEG)), dot)
        out = jnp.where(out_lane // STAT_LANES == hd, both, out)
    return out


def _read_stats(st, part):
    lane = lax.broadcasted_iota(jnp.int32, st.shape, 1)
    half = (lane % STAT_LANES < STAT_LANES // 2) == (part == 0)
    return jnp.concatenate([_row_value(jnp.where((lane // STAT_LANES == hd) & half, st, NEG))
                            for hd in range(HEADS_PER_BLOCK)], axis=0)


def _merge_bwd(dh, ya, lse_tot, o_b, lse_b, zg, wa_t, wb_t, wout, name, hosted=()):
    t = dh.shape[0]
    tm = 256

    def body(dh_ref, ya_ref, lt_ref, ob_ref, lb_ref, zg_ref, wa_ref, wb_ref, wo_ref,
             mg_ref, dpa_ref, dpb_ref, dzg_ref, dy1, dy2, dy3, st1, st2, st3, dyb_ref, stb_ref, *slabs):
        dm = _nt(dh_ref[...].astype(BF16), wo_ref[...])
        pa = _nt(ya_ref[...].astype(BF16), wa_ref[...])
        pb = _nt(ob_ref[...].astype(BF16), wb_ref[...])
        sa = jax.nn.sigmoid(zg_ref[:, :D_MODEL].astype(F32))
        sb = jax.nn.sigmoid(zg_ref[:, D_MODEL:].astype(F32))
        mg_ref[...] = (sa * pa + sb * pb).astype(BF16)
        dpa = (dm * sa).astype(BF16)
        dpb = (dm * sb).astype(BF16)
        dpa_ref[...] = dpa
        dpb_ref[...] = dpb
        dzg_ref[:, :D_MODEL] = (dm * pa * (sa * (1.0 - sa))).astype(BF16)
        dzg_ref[:, D_MODEL:] = (dm * pb * (sb * (1.0 - sb))).astype(BF16)
        dya = _nn(dpa, wa_ref[...])
        dyb = _nn(dpb, wb_ref[...])
        stats = _head_stats(lt_ref[...], dya, ya_ref[...])
        for i, d in enumerate(DILATIONS):
            _fold(dya, slabs[i], (dy1, dy2, dy3)[i], d)
            _fold(stats, slabs[3 + i], (st1, st2, st3)[i], d)
        dyb_ref[...] = dyb.astype(BF16)
        stb_ref[...] = _head_stats(lb_ref[...], dyb, ob_ref[...])

    row = lambda w: pl.BlockSpec((tm, w), lambda i: (i, 0))
    folded = [pl.BlockSpec((tm // d, d * 256), lambda i: (i, 0)) for d in DILATIONS]
    full = _resident
    sds = jax.ShapeDtypeStruct
    dy_shape = [sds((t // d, d * 256), BF16) for d in DILATIONS]
    st_shape = [sds((t // d, d * 128), F32) for d in DILATIONS]
    st_specs = [pl.BlockSpec((tm // d, d * 128), lambda i: (i, 0)) for d in DILATIONS]
    return _call(
        body, name=name, grid=(t // tm,),
        in_specs=[row(D_MODEL), row(256), row(256), row(512), row(512), row(GATE_W), full(wa_t), full(wb_t),
                  full(wout)],
        out_specs=[row(D_MODEL), row(D_MODEL), row(D_MODEL), row(GATE_W)] + folded + st_specs + [row(512), row(256)],
        out_shape=[sds((t, D_MODEL), BF16), sds((t, D_MODEL), BF16), sds((t, D_MODEL), BF16),
                   sds((t, GATE_W), BF16)] + dy_shape + st_shape + [sds((t, 512), BF16), sds((t, 256), F32)],
        scratch=[pltpu.VMEM((2, tm, 128), F32)] * 6,
        sem=("parallel",), args=(dh, ya, lse_tot, o_b, lse_b, zg, wa_t, wb_t, wout), hosted=hosted)


def _pair_sum(grad, got, name):
    _, _, r, cdim = grad.shape
    core = lax.axis_index("c").astype(jnp.int32).reshape(1)

    def body(core_ref, g_ref, s_ref, o_ref):
        o_ref[...] = (g_ref[...] + s_ref[...]).astype(BF16)

    return pl.pallas_call(
        body, name=name,
        grid_spec=pltpu.PrefetchScalarGridSpec(
            num_scalar_prefetch=1, grid=(N_CHIP,),
            in_specs=[pl.BlockSpec((None, None, r, cdim), lambda q, core_ref: (q, core_ref[0], 0, 0)),
                      pl.BlockSpec((None, None, r, cdim), lambda q, core_ref: (q, 0, 0, 0))],
            out_specs=pl.BlockSpec((None, r, cdim), lambda q, core_ref: (q, 0, 0))),
        out_shape=jax.ShapeDtypeStruct((N_CHIP, r, cdim), BF16),
        compiler_params=_params("parallel"),
    )(core, grad, got)


def _adamw_math(w, g, m, v):
    m = ADAM_B1 * m + (1.0 - ADAM_B1) * g
    v = ADAM_B2 * v + (1.0 - ADAM_B2) * jnp.square(g)
    m_hat = m / (1.0 - ADAM_B1 ** ADAM_STEP)
    v_hat = v / (1.0 - ADAM_B2 ** ADAM_STEP)
    delta = -ADAM_LR * (m_hat / (jnp.sqrt(v_hat) + ADAM_EPS) + ADAM_WD * w)
    return delta, m, v


def _update(w, parts, m, v, transposed, name):
    r, c = parts.shape[1:]

    def body(w_ref, p_ref, m_ref, v_ref, g_ref, d_ref, m2_ref, v2_ref):
        def total(rows):
            return ((p_ref[0, rows].astype(F32) + p_ref[1, rows].astype(F32)) + p_ref[2, rows].astype(F32)) \
                + p_ref[3, rows].astype(F32)

        def update(at, g):
            g_ref[at] = g
            d_ref[at], m2_ref[at], v2_ref[at] = _adamw_math(w_ref[at], g, m_ref[at], v_ref[at])

        if not transposed:
            update((slice(None), slice(None)), total(slice(None)))
            return
        for r0 in range(0, r, 128):
            n = min(128, r - r0)
            gt = total(slice(r0, r0 + n))
            if n < 128:
                gt = jnp.concatenate([gt, jnp.zeros((128 - n, c), F32)], axis=0)
            update((slice(None), slice(r0, r0 + n)), gt.T[:, :n])

    sds = jax.ShapeDtypeStruct(w.shape, F32)
    return pl.pallas_call(body, name=name, out_shape=[sds] * 4,
                          compiler_params=pltpu.CompilerParams(vmem_limit_bytes=VMEM_LIMIT))(w, parts, m, v)


SMALL_ROWS = 80


def _small_update(g, w, m, v, name):
    def body(g_ref, w_ref, m_ref, v_ref, gs_ref, d_ref, m2_ref, v2_ref, got_ref, send_sems, recv_sems):
        x, y, c = _place()
        me = 4 * x + 2 * y + c
        got_ref[me] = g_ref[...]
        copies = []
        for k in range(1, N_DEV):
            peer = (x ^ (k >> 2), y ^ ((k >> 1) & 1), c ^ (k & 1))
            cp = pltpu.make_async_remote_copy(
                src_ref=g_ref, dst_ref=got_ref.at[me], send_sem=send_sems.at[k - 1], recv_sem=recv_sems.at[k - 1],
                device_id=peer, device_id_type=MESH)
            cp.start()
            copies.append(cp)
        for cp in copies:
            cp.wait()
        total = got_ref[0]
        for k in range(1, N_DEV):
            total = total + got_ref[k]
        gs_ref[...] = total
        d_ref[...], m2_ref[...], v2_ref[...] = _adamw_math(w_ref[...], total, m_ref[...], v_ref[...])

    sds = jax.ShapeDtypeStruct((SMALL_ROWS, 128), F32)
    vm = pl.BlockSpec(memory_space=pltpu.VMEM)
    return pl.pallas_call(
        body, name=name, in_specs=[vm] * 4, out_specs=[vm] * 4, out_shape=[sds] * 4,
        scratch_shapes=[pltpu.VMEM((N_DEV, SMALL_ROWS, 128), F32), pltpu.SemaphoreType.DMA((N_DEV - 1,)),
                        pltpu.SemaphoreType.DMA((N_DEV - 1,))],
    )(g, w, m, v)


def _pack_small(gains, b_in, rel_bias, sinks, last):
    rows = [a.reshape(8, 128) for a in gains] + [b_in.reshape(40, 128), rel_bias.reshape(5, 128),
                                                 jnp.pad(sinks.reshape(1, 8), ((0, 0), (0, 120))), last]
    rows.append(jnp.zeros((SMALL_ROWS - 79, 128), F32))
    return jnp.concatenate(rows, axis=0)


def _unpack_small(p, like):
    out = [p[8 * i:8 * i + 8].reshape(like[i].shape) for i in range(4)]
    out.append(p[32:72].reshape(like[4].shape))
    out.append(p[72:77].reshape(like[5].shape))
    out.append(p[77, :8].reshape(like[6].shape))
    return out


def kernel(x, ffn1_norm, ffn1_w_gate, ffn1_w_up, ffn1_w_down, mix_norm, w_in, b_in, w_branch_a, w_branch_b, w_out, sinks, rel_bias, ffn2_norm, ffn2_w_gate, ffn2_w_up, ffn2_w_down, final_norm, loss_target, m_ffn1_norm, m_ffn1_w_gate, m_ffn1_w_up, m_ffn1_w_down, m_mix_norm, m_w_in, m_b_in, m_w_branch_a, m_w_branch_b, m_w_out, m_sinks, m_rel_bias, m_ffn2_norm, m_ffn2_w_gate, m_ffn2_w_up, m_ffn2_w_down, m_final_norm, v_ffn1_norm, v_ffn1_w_gate, v_ffn1_w_up, v_ffn1_w_down, v_mix_norm, v_w_in, v_b_in, v_w_branch_a, v_w_branch_b, v_w_out, v_sinks, v_rel_bias, v_ffn2_norm, v_ffn2_w_gate, v_ffn2_w_up, v_ffn2_w_down, v_final_norm):
    bsz, seq, _ = x.shape
    t = bsz * seq
    xt = x.reshape(t, D_MODEL)
    target = loss_target.reshape(t, D_MODEL)

    big = [("ffn1_w_gate", ffn1_w_gate, m_ffn1_w_gate, v_ffn1_w_gate, True),
           ("ffn1_w_up", ffn1_w_up, m_ffn1_w_up, v_ffn1_w_up, True),
           ("ffn1_w_down", ffn1_w_down, m_ffn1_w_down, v_ffn1_w_down, False),
           ("w_in", w_in, m_w_in, v_w_in, True),
           ("w_branch_a", w_branch_a, m_w_branch_a, v_w_branch_a, True),
           ("w_branch_b", w_branch_b, m_w_branch_b, v_w_branch_b, True),
           ("w_out", w_out, m_w_out, v_w_out, False),
           ("ffn2_w_gate", ffn2_w_gate, m_ffn2_w_gate, v_ffn2_w_gate, True),
           ("ffn2_w_up", ffn2_w_up, m_ffn2_w_up, v_ffn2_w_up, True),
           ("ffn2_w_down", ffn2_w_down, m_ffn2_w_down, v_ffn2_w_down, False)]
    shard = {nm: (w[0].T if tr else w[0]).astype(BF16) for nm, w, _, _, tr in big}
    full = {}

    def gather(names):
        return _Gather([shard[nm] for nm in names])

    def keep(names, got):
        for nm, gw in zip(names, got):
            full[nm] = gw.reshape(-1, gw.shape[-1])

    ffn1_names = ["ffn1_w_gate", "ffn1_w_up", "ffn1_w_down"]
    mix_names = ["w_in", "w_branch_a", "w_branch_b", "w_out"]
    ffn2_names = ["ffn2_w_gate", "ffn2_w_up", "ffn2_w_down"]
    g1, gm, g2, gf = ffn1_norm, mix_norm, ffn2_norm, final_norm.reshape(1, D_MODEL)

    keep(ffn1_names, _exchange([gather(ffn1_names)], "gather_ffn1")[0])
    (h1, n1, a1, b1, hff1), (got,) = _ffn_fwd(xt, g1, full["ffn1_w_gate"], full["ffn1_w_up"], full["ffn1_w_down"], "ffn1_fwd",
                                hosted=[gather(mix_names)])
    keep(mix_names, got)
    (u, zq, zg, *zdil), (got,) = _inproj_fwd(h1, gm, full["w_in"], b_in, "inproj_fwd", hosted=[gather(ffn2_names[:2])])
    keep(ffn2_names[:2], got)
    buckets = _bucket_tiles()
    bias = _bias_build(rel_bias, buckets)
    sink_rows = jnp.broadcast_to(sinks.reshape(8, 1, 1), (8, BLOCK, 128)).reshape(8 * BLOCK, 128)
    cfs = [_Att(i) for i in range(4)]
    zfold = []
    for i, cf in enumerate(cfs):
        if cf.d == 1:
            zfold.append(zq.reshape(bsz, seq, QKV_W))
        else:
            zfold.append(zdil[i - 1].reshape(bsz, seq // cf.d, cf.d * cf.row_w))
    att = [None] * 4
    for i in (3, 0, 1, 2):
        cf = cfs[i]
        (o, lse), got = _attn_fwd(cf, zfold[i], bias, sink_rows, f"attn{i}_fwd",
                                  hosted=[gather(ffn2_names[2:])] if i == 3 else ())
        if i == 3:
            keep(ffn2_names[2:], got[0])
        att[i] = (o.reshape(t // cf.d, cf.d * cf.wq), lse.reshape(t // cf.d, cf.d * cf.wq))
    o_b, lse_b = att[3]
    h2, ya, lse_tot = _merge_fwd([a[0] for a in att[:3]], [a[1] for a in att[:3]], o_b, zg, h1,
                                 full["w_branch_a"], full["w_branch_b"], full["w_out"], "merge_fwd")
    (dh3, n2, a2, b2, hff2, loss_part, dgf), _ = _ffn_fwd(
        h2, g2, full["ffn2_w_gate"], full["ffn2_w_up"], full["ffn2_w_down"], "ffn2_fwd", head=(gf, target))

    grads, pair, from_chips = {}, {}, {}

    def by_owner(nm):
        return grads[nm].reshape(N_CHIP, 2, -1, grads[nm].shape[-1])

    def to_core(names):
        return _CoreExchange([by_owner(nm) for nm in names])

    def pair_up(names, got):
        for nm, sib in zip(names, got):
            pair[nm] = _pair_sum(by_owner(nm), sib, f"pair_sum_{nm}")

    def to_chips(names):
        return _ChipExchange([pair[nm] for nm in names])

    def landed(names, got):
        for nm, parts in zip(names, got):
            from_chips[nm] = parts

    dh2, dg2, da, db = _ffn_bwd(h2, a2, b2, g2, dh3, full["ffn2_w_gate"], full["ffn2_w_up"],
                                          full["ffn2_w_down"], "ffn2_bwd")
    grads["ffn2_w_gate"], _ = _tn_matmul(da, n2, 1408, "ffn2_dgate")
    grads["ffn2_w_up"], _ = _tn_matmul(db, n2, 1408, "ffn2_dup")
    grads["ffn2_w_down"], _ = _tn_matmul(hff2, dh3, 1408, "ffn2_ddown", scale=0.5)
    (merged, dpa, dpb, dzg, *cot), (got,) = _merge_bwd(
        dh2, ya, lse_tot, o_b, lse_b, zg, full["w_branch_a"], full["w_branch_b"], full["w_out"], "merge_bwd",
        hosted=[to_core(ffn2_names)])
    dys, sts = cot[0:3] + [cot[6]], cot[3:6] + [cot[7]]
    pair_up(ffn2_names, got)
    dq, dk, dv, dbias, dsink = [None] * 4, [None] * 4, [None] * 4, [None] * 4, None
    for i in (3, 0, 1, 2):
        cf = cfs[i]
        shp = (bsz, seq // cf.d, cf.d * cf.wq)
        hosted = {3: lambda: [to_chips(ffn2_names[:2])], 0: lambda: [to_chips(ffn2_names[2:])]}.get(i, list)()
        res, got = _attn_bwd(cf, zfold[i], bias, sink_rows, sts[i].reshape(shp[:2] + (shp[2] // 2,)), dys[i].reshape(shp),
                             f"attn{i}_bwd", hosted=hosted)
        if i == 3:
            landed(ffn2_names[:2], got[0])
        elif i == 0:
            landed(ffn2_names[2:], got[0])
        dq[i] = res[0].reshape(t // cf.d, cf.d * cf.wq)
        dk[i] = res[1].reshape(t // cf.d, cf.d * cf.wkv)
        dv[i] = res[2].reshape(t // cf.d, cf.d * cf.wkv)
        dbias[i] = res[3]
        if cf.sinks:
            dsink = res[4]
    pieces = dq[:3] + dk[:3] + dv[:3] + [dq[3], dk[3], dv[3], dzg]
    dh1, dz, db_in, dgm = _inproj_bwd(pieces, h1, gm, dh2, full["w_in"], "inproj_bwd")
    dx, dg1, da, db = _ffn_bwd(xt, a1, b1, g1, dh1, full["ffn1_w_gate"], full["ffn1_w_up"],
                                         full["ffn1_w_down"], "ffn1_bwd")
    grads["w_in"], _ = _tn_matmul(dz, u, 1280, "dw_in")
    grads["ffn1_w_down"], (got,) = _tn_matmul(hff1, dh1, 1408, "ffn1_ddown", hosted=[to_core(["w_in"])], scale=0.5)
    pair_up(["w_in"], got)
    grads["ffn1_w_gate"], got = _tn_matmul(da, n1, 1408, "ffn1_dgate",
                                           hosted=[to_chips(["w_in"]), to_core(["ffn1_w_down"])])
    landed(["w_in"], got[0])
    pair_up(["ffn1_w_down"], got[1])
    grads["ffn1_w_up"], got = _tn_matmul(db, n1, 1408, "ffn1_dup",
                                         hosted=[to_chips(["ffn1_w_down"]), to_core(["ffn1_w_gate"])])
    landed(["ffn1_w_down"], got[0])
    pair_up(["ffn1_w_gate"], got[1])
    grads["w_out"], got = _tn_matmul(merged, dh2, 1024, "dw_out",
                                     hosted=[to_chips(["ffn1_w_gate"]), to_core(["ffn1_w_up"])])
    landed(["ffn1_w_gate"], got[0])
    pair_up(["ffn1_w_up"], got[1])
    grads["w_branch_b"], got = _tn_matmul(dpb, o_b, 1024, "dw_branch_b",
                                          hosted=[to_chips(["ffn1_w_up"]), to_core(["w_out"])])
    landed(["ffn1_w_up"], got[0])
    pair_up(["w_out"], got[1])
    grads["w_branch_a"], got = _tn_matmul(dpa, ya, 1024, "dw_branch_a",
                                          hosted=[to_chips(["w_out"]), to_core(["w_branch_b"])])
    landed(["w_out"], got[0])
    pair_up(["w_branch_b"], got[1])
    got = _exchange([to_chips(["w_branch_b"]), to_core(["w_branch_a"])], "reduce_scatter_tail1")
    landed(["w_branch_b"], got[0])
    pair_up(["w_branch_a"], got[1])
    landed(["w_branch_a"], _exchange([to_chips(["w_branch_a"])], "reduce_scatter_tail2")[0])
    dtable, dsinks = _bias_reduce(jnp.concatenate(dbias, axis=0), buckets, dsink)

    out_g, out_d, out_m, out_v = {}, {}, {}, {}
    for nm, w, m, v, tr in big:
        around = tr and w.shape[-1] % 128 != 0
        wmv = [a[0].T if around else a[0] for a in (w, m, v)]
        res = _update(wmv[0], from_chips[nm], wmv[1], wmv[2], tr and not around, f"update_{nm}")
        out_g[nm], out_d[nm], out_m[nm], out_v[nm] = [(a.T if around else a)[None] for a in res]

    small = [("ffn1_norm", ffn1_norm, m_ffn1_norm, v_ffn1_norm), ("mix_norm", mix_norm, m_mix_norm, v_mix_norm),
             ("ffn2_norm", ffn2_norm, m_ffn2_norm, v_ffn2_norm), ("final_norm", final_norm, m_final_norm, v_final_norm),
             ("b_in", b_in, m_b_in, v_b_in), ("rel_bias", rel_bias, m_rel_bias, v_rel_bias),
             ("sinks", sinks, m_sinks, v_sinks)]
    zero_row = jnp.zeros((1, 128), F32)
    pack = lambda arrs, last: _pack_small(arrs[:4], arrs[4], arrs[5], arrs[6], last)
    g_small = pack([dg1, dgm, dg2, dgf, db_in, dtable[:, :TOTAL_HEADS], dsinks[:, 0]], loss_part)
    packed = [pack([s[k] for s in small], zero_row) for k in (1, 2, 3)]
    gs, ds, ms, vs = _small_update(g_small, *packed, "small_update")
    like = [s[1] for s in small]
    for nm_s, g_, d_, m_, v_ in zip([s[0] for s in small], _unpack_small(gs, like), _unpack_small(ds, like),
                                    _unpack_small(ms, like), _unpack_small(vs, like)):
        out_g[nm_s], out_d[nm_s], out_m[nm_s], out_v[nm_s] = g_, d_, m_, v_
    loss = gs[78, 0]

    order = ["ffn1_norm", "ffn1_w_gate", "ffn1_w_up", "ffn1_w_down", "mix_norm", "w_in", "b_in", "w_branch_a",
             "w_branch_b", "w_out", "sinks", "rel_bias", "ffn2_norm", "ffn2_w_gate", "ffn2_w_up", "ffn2_w_down",
             "final_norm"]
    return (loss, dx.reshape(x.shape), *[out_g[k] for k in order], *[out_d[k] for k in order],
            *[out_m[k] for k in order], *[out_v[k] for k in order])
```

```python
import functools
import math

import numpy as np
import jax
import jax.numpy as jnp
from jax import lax
from jax.experimental import pallas as pl
from jax.experimental.pallas import tpu as pltpu

D_MODEL = 1024
D_FF = 2816
FF_CHUNK = 256
HEAD_DIM = 64
BLOCK = 128
N_BUCKETS = 32
MAX_DISTANCE = 2048
A_HEADS = 12
TOTAL_HEADS = 20
DIL_GROUPS = ((128, 1), (512, 4), (2048, 16))
B_WINDOW = 128
QKV_W = 3072
GATE_W = 2048
D_IN = QKV_W + GATE_W
EPS = 1e-6
NEG = -1e30
N_DEV = 8
N_CHIP = 4
ADAM_LR, ADAM_B1, ADAM_B2, ADAM_EPS, ADAM_WD, ADAM_STEP = 0.001, 0.9, 0.999, 1e-08, 0.01, 10
VMEM_LIMIT = 56 * 1024 * 1024
MESH = pl.DeviceIdType.MESH
BF16 = jnp.bfloat16
F32 = jnp.float32
ANY = pl.BlockSpec(memory_space=pl.ANY)


def _params(*sem):
    return pltpu.CompilerParams(dimension_semantics=sem, vmem_limit_bytes=VMEM_LIMIT)


def _resident(a):
    return pl.BlockSpec(a.shape, lambda i: (0, 0), pipeline_mode=pl.Buffered(1))


def _place():
    return lax.axis_index("x"), lax.axis_index("y"), lax.axis_index("c")


class _Gather:
    def __init__(self, shards):
        self.ins = list(shards)
        n = self.n = len(shards)
        self.out_shape = [jax.ShapeDtypeStruct((N_DEV,) + s.shape, s.dtype) for s in shards]
        self.scratch = [pltpu.SemaphoreType.DMA((7 * n,)), pltpu.SemaphoreType.DMA((7 * n,)),
                        pltpu.SemaphoreType.DMA((n,))]

    def _copies(self, ins, outs, sems):
        send_sems, recv_sems, local_sems = sems
        x, y, c = _place()
        me, sibling = (x, y, c), (x, y, 1 - c)
        chips = [(1 - x, y), (x, 1 - y), (1 - x, 1 - y)]

        def copy(i, k, block, to, src=None):
            dst = outs[i].at[4 * block[0] + 2 * block[1] + block[2]]
            return pltpu.make_async_remote_copy(
                src_ref=dst if src is None else src, dst_ref=dst, send_sem=send_sems.at[7 * i + k],
                recv_sem=recv_sems.at[7 * i + k], device_id=to, device_id_type=MESH)

        n = self.n
        mine = [pltpu.make_async_copy(ins[i], outs[i].at[4 * x + 2 * y + c], local_sems.at[i]) for i in range(n)]
        first = [copy(i, 0, me, sibling, src=ins[i]) for i in range(n)]
        first += [copy(i, 1 + j, me, (*chip, c), src=ins[i]) for i in range(n) for j, chip in enumerate(chips)]
        landed = [copy(i, 1 + j, (*chip, c), me) for j, chip in enumerate(chips) for i in range(n)]
        passed = [copy(i, 4 + j, (*chip, c), sibling) for j, chip in enumerate(chips) for i in range(n)]
        from_sibling = [copy(i, 0, sibling, me) for i in range(n)]
        from_sibling += [copy(i, 4 + j, (*chip, 1 - c), me) for i in range(n) for j, chip in enumerate(chips)]
        return mine, first, landed, passed, from_sibling

    def start(self, ins, outs, sems):
        mine, first, _, _, _ = self._copies(ins, outs, sems)
        for cp in mine + first:
            cp.start()

    def mid(self, ins, outs, sems):
        _, _, landed, passed, _ = self._copies(ins, outs, sems)
        for got, fwd in zip(landed, passed):
            got.wait_recv()
            fwd.start()

    def end(self, ins, outs, sems):
        mine, first, _, passed, from_sibling = self._copies(ins, outs, sems)
        for cp in from_sibling:
            cp.wait_recv()
        for cp in first + passed:
            cp.wait_send()
        for cp in mine:
            cp.wait()


class _CoreExchange:
    def __init__(self, grads):
        self.ins = list(grads)
        n = self.n = len(grads)
        self.out_shape = [jax.ShapeDtypeStruct((N_CHIP, 1) + g.shape[2:], g.dtype) for g in grads]
        self.scratch = [pltpu.SemaphoreType.DMA((n,)), pltpu.SemaphoreType.DMA((n,))]

    def _copies(self, ins, outs, sems):
        x, y, c = _place()
        return [pltpu.make_async_remote_copy(
            src_ref=ins[i].at[:, pl.ds(1 - c, 1)], dst_ref=outs[i], send_sem=sems[0].at[i],
            recv_sem=sems[1].at[i], device_id=(x, y, 1 - c), device_id_type=MESH) for i in range(self.n)]

    def start(self, ins, outs, sems):
        for cp in self._copies(ins, outs, sems):
            cp.start()

    mid = None

    def end(self, ins, outs, sems):
        for cp in self._copies(ins, outs, sems):
            cp.wait()


class _ChipExchange:
    def __init__(self, parts):
        self.ins = list(parts)
        n = self.n = len(parts)
        self.out_shape = [jax.ShapeDtypeStruct(p.shape, p.dtype) for p in parts]
        self.scratch = [pltpu.SemaphoreType.DMA((3 * n,)), pltpu.SemaphoreType.DMA((3 * n,)),
                        pltpu.SemaphoreType.DMA((n,))]

    def _copies(self, ins, outs, sems):
        send_sems, recv_sems, local_sems = sems
        x, y, c = _place()
        my_chip = 2 * x + y
        copies = []
        for i in range(self.n):
            copies.append(pltpu.make_async_copy(ins[i].at[my_chip], outs[i].at[my_chip], local_sems.at[i]))
            for k, (qx, qy) in enumerate([(1 - x, y), (x, 1 - y), (1 - x, 1 - y)]):
                copies.append(pltpu.make_async_remote_copy(
                    src_ref=ins[i].at[2 * qx + qy], dst_ref=outs[i].at[my_chip], send_sem=send_sems.at[3 * i + k],
                    recv_sem=recv_sems.at[3 * i + k], device_id=(qx, qy, c), device_id_type=MESH))
        return copies

    def start(self, ins, outs, sems):
        for cp in self._copies(ins, outs, sems):
            cp.start()

    mid = None

    def end(self, ins, outs, sems):
        for cp in self._copies(ins, outs, sems):
            cp.wait()


def _call(body, *, name, grid, in_specs, out_specs, out_shape, args, scratch=(), sem=None, hosted=()):
    n_in, n_out, n_scr = len(in_specs), len(out_specs), len(scratch)
    x_in = [len(p.ins) for p in hosted]
    x_scr = [len(p.scratch) for p in hosted]
    steps = int(np.prod(grid))

    def wrapped(*refs):
        refs = list(refs)
        ins, refs = refs[:n_in], refs[n_in:]
        x_ins = [[refs.pop(0) for _ in range(k)] for k in x_in]
        outs, refs = refs[:n_out], refs[n_out:]
        x_outs = [[refs.pop(0) for _ in range(k)] for k in x_in]
        scr, refs = refs[:n_scr], refs[n_scr:]
        x_sems = [[refs.pop(0) for _ in range(k)] for k in x_scr]
        step = 0
        for d in range(len(grid)):
            step = step * grid[d] + pl.program_id(d)

        def phase(which, at):
            fns = [(getattr(p, which), a) for p, a in zip(hosted, zip(x_ins, x_outs, x_sems)) if getattr(p, which)]
            if fns:
                @pl.when(step == at)
                def _():
                    for fn, a in fns:
                        fn(*a)

        phase("start", 0)
        if steps > 1:
            phase("mid", (3 * steps) // 4)
        body(*ins, *outs, *scr)
        if steps == 1:
            phase("mid", 0)
        phase("end", steps - 1)

    results = pl.pallas_call(
        wrapped, name=name, grid=grid,
        in_specs=list(in_specs) + [ANY] * sum(x_in), out_specs=list(out_specs) + [ANY] * sum(x_in),
        out_shape=list(out_shape) + [s for p in hosted for s in p.out_shape],
        scratch_shapes=list(scratch) + [s for p in hosted for s in p.scratch],
        compiler_params=_params(*(("arbitrary",) * len(grid) if hosted else sem)),
    )(*args, *[a for p in hosted for a in p.ins])
    own, rest = list(results[:n_out]), list(results[n_out:])
    return own, [[rest.pop(0) for _ in range(k)] for k in x_in]


def _exchange(programs, name):
    x_in = [len(p.ins) for p in programs]
    x_scr = [len(p.scratch) for p in programs]

    def body(*refs):
        refs = list(refs)
        x_ins = [[refs.pop(0) for _ in range(k)] for k in x_in]
        x_outs = [[refs.pop(0) for _ in range(k)] for k in x_in]
        x_sems = [[refs.pop(0) for _ in range(k)] for k in x_scr]
        for which in ("start", "mid", "end"):
            for p, a in zip(programs, zip(x_ins, x_outs, x_sems)):
                if getattr(p, which):
                    getattr(p, which)(*a)

    results = list(pl.pallas_call(
        body, name=name, in_specs=[ANY] * sum(x_in), out_specs=[ANY] * sum(x_in),
        out_shape=[s for p in programs for s in p.out_shape],
        scratch_shapes=[s for p in programs for s in p.scratch],
    )(*[a for p in programs for a in p.ins]))
    return [[results.pop(0) for _ in range(k)] for k in x_in]


def _nt(a, b):
    return lax.dot_general(a, b, (((1,), (1,)), ((), ())), preferred_element_type=F32)


def _nn(a, b):
    return lax.dot_general(a, b, (((1,), (0,)), ((), ())), preferred_element_type=F32)


def _tn(a, b):
    return lax.dot_general(a, b, (((0,), (0,)), ((), ())), preferred_element_type=F32)


def _rms(x, g):
    r = lax.rsqrt(jnp.mean(x * x, axis=-1, keepdims=True) + EPS)
    return x * r, r


def _rms_bwd(dn, xhat, r, g):
    dg = jnp.sum(dn * xhat, axis=0, keepdims=True)
    dxh = dn * g
    dx = r * (dxh - xhat * jnp.mean(dxh * xhat, axis=-1, keepdims=True))
    return dx, dg


def _ffn_fwd(x, g, wg_t, wu_t, wd, name, hosted=(), head=None):
    t = x.shape[0]
    tm = 256

    def body(*refs):
        x_ref, g_ref, wg_ref, wu_ref, wd_ref = refs[:5]
        h_ref, n_ref, a_ref, b_ref, hff_ref = refs[-5 if head is None else -7:][:5]
        xhat, _ = _rms(x_ref[...], g_ref[...])
        n = (xhat * g_ref[...]).astype(BF16)
        n_ref[...] = n
        for c in range(0, D_FF, FF_CHUNK):
            cols = slice(c, c + FF_CHUNK)
            a = _nt(n, wg_ref[cols, :])
            b = _nt(n, wu_ref[cols, :])
            a_ref[:, cols] = a.astype(BF16)
            b_ref[:, cols] = b.astype(BF16)
            hff_ref[:, cols] = (a * jax.nn.sigmoid(a) * b).astype(BF16)
        h = x_ref[...] + 0.5 * _nn(hff_ref[...], wd_ref[...])
        if head is None:
            h_ref[...] = h
            return
        gf_ref, t_ref, loss_ref, dgf_ref = refs[5], refs[6], refs[-2], refs[-1]

        @pl.when(pl.program_id(0) == 0)
        def _():
            loss_ref[...] = jnp.zeros_like(loss_ref)
            dgf_ref[...] = jnp.zeros_like(dgf_ref)

        yhat, r = _rms(h, gf_ref[...])
        err = yhat * gf_ref[...] - t_ref[...]
        loss_ref[...] += 0.5 * jnp.sum(jnp.mean(err * err, axis=-1, keepdims=True), axis=0, keepdims=True)
        h_ref[...], dgf = _rms_bwd(err * (1.0 / D_MODEL), yhat, r, gf_ref[...])
        dgf_ref[...] += dgf

    row = pl.BlockSpec((tm, D_MODEL), lambda i: (i, 0))
    hid = pl.BlockSpec((tm, D_FF), lambda i: (i, 0))
    sds = jax.ShapeDtypeStruct
    in_specs = [row, _resident(g), _resident(wg_t), _resident(wu_t), _resident(wd)]
    out_specs = [row, row, hid, hid, hid]
    out_shape = [sds((t, D_MODEL), F32), sds((t, D_MODEL), BF16), sds((t, D_FF), BF16), sds((t, D_FF), BF16),
                 sds((t, D_FF), BF16)]
    args = (x, g, wg_t, wu_t, wd)
    if head is not None:
        in_specs += [_resident(head[0]), row]
        out_specs += [pl.BlockSpec((1, 128), lambda i: (0, 0)), pl.BlockSpec((1, D_MODEL), lambda i: (0, 0))]
        out_shape += [sds((1, 128), F32), sds((1, D_MODEL), F32)]
        args += tuple(head)
    return _call(body, name=name, grid=(t // tm,), in_specs=in_specs, out_specs=out_specs, out_shape=out_shape,
                 sem=("parallel",) if head is None else ("arbitrary",), args=args, hosted=hosted)


def _ffn_bwd(x, a_pre, b_pre, g, dh, wg_t, wu_t, wd, name):
    t = x.shape[0]
    tm = 256

    def body(x_ref, a_ref, b_ref, g_ref, dh_ref, wg_ref, wu_ref, wd_ref,
             dx_ref, dg_ref, da_ref, db_ref):
        @pl.when(pl.program_id(0) == 0)
        def _():
            dg_ref[...] = jnp.zeros_like(dg_ref)

        dhh = (0.5 * dh_ref[...]).astype(BF16)
        for c in range(0, D_FF, FF_CHUNK):
            cols = slice(c, c + FF_CHUNK)
            a = a_ref[:, cols].astype(F32)
            b = b_ref[:, cols].astype(F32)
            s = jax.nn.sigmoid(a)
            silu = a * s
            dhff = _nt(dhh, wd_ref[cols, :])
            da_ref[:, cols] = (dhff * b * (s * (1.0 + a * (1.0 - s)))).astype(BF16)
            db_ref[:, cols] = (dhff * silu).astype(BF16)
        dn = _nn(da_ref[...], wg_ref[...]) + _nn(db_ref[...], wu_ref[...])
        xhat, r = _rms(x_ref[...], g_ref[...])
        dx, dg = _rms_bwd(dn, xhat, r, g_ref[...])
        dx_ref[...] = dh_ref[...] + dx
        dg_ref[...] += dg

    row = pl.BlockSpec((tm, D_MODEL), lambda i: (i, 0))
    hid = pl.BlockSpec((tm, D_FF), lambda i: (i, 0))
    return pl.pallas_call(
        body, name=name, grid=(t // tm,),
        in_specs=[row, hid, hid, _resident(g), row, _resident(wg_t), _resident(wu_t), _resident(wd)],
        out_specs=[row, pl.BlockSpec((1, D_MODEL), lambda i: (0, 0)), hid, hid],
        out_shape=[jax.ShapeDtypeStruct((t, D_MODEL), F32), jax.ShapeDtypeStruct((1, D_MODEL), F32),
                   jax.ShapeDtypeStruct((t, D_FF), BF16), jax.ShapeDtypeStruct((t, D_FF), BF16)],
        compiler_params=_params("arbitrary"),
    )(x, a_pre, b_pre, g, dh, wg_t, wu_t, wd)


def _tn_matmul(a, b, rc, name, hosted=(), scale=None):
    t, r = a.shape
    c = b.shape[1]
    tk = min(t, 2048)

    def body(a_ref, b_ref, o_ref):
        @pl.when(pl.program_id(1) == 0)
        def _():
            o_ref[...] = jnp.zeros_like(o_ref)

        o_ref[...] += _tn(a_ref[...].astype(BF16), b_ref[...].astype(BF16))
        if scale is not None:
            @pl.when(pl.program_id(1) == t // tk - 1)
            def _():
                o_ref[...] *= scale

    (out,), got = _call(
        body, name=name, grid=(r // rc, t // tk),
        in_specs=[pl.BlockSpec((tk, rc), lambda i, k: (k, i)), pl.BlockSpec((tk, c), lambda i, k: (k, 0))],
        out_specs=[pl.BlockSpec((rc, c), lambda i, k: (i, 0))],
        out_shape=[jax.ShapeDtypeStruct((r, c), F32)],
        sem=("parallel", "arbitrary"), args=(a, b), hosted=hosted)
    return out, got


def _unfold(blk_ref, slab_ref, d):
    if d == 1:
        return blk_ref[...]
    n = blk_ref.shape[0]
    for r in range(d):
        for half in range(2):
            c0 = 256 * r + 128 * half
            slab_ref[half, pl.ds(r, n, stride=d), :] = blk_ref[:, c0:c0 + 128]
    return jnp.concatenate([slab_ref[0], slab_ref[1]], axis=1)


def _fold(x, slab_ref, out_ref, d):
    if d == 1:
        out_ref[...] = x.astype(out_ref.dtype)
        return
    n, w = out_ref.shape[0], x.shape[1]
    for part in range(w // 128):
        slab_ref[part] = x[:, 128 * part:128 * (part + 1)]
    for r in range(d):
        for part in range(w // 128):
            c0 = w * r + 128 * part
            out_ref[:, c0:c0 + 128] = slab_ref[part, pl.ds(r, n, stride=d), :].astype(out_ref.dtype)


DILATIONS = tuple(d for _, d in DIL_GROUPS)


PIECE_W = (256,) * 9 + (512, 128, 128, GATE_W)
PIECE_D = DILATIONS * 3 + (1, 1, 1, 1)


def _inproj_fwd(h, g, w_t, b_in, name, hosted=()):
    t = h.shape[0]
    tm, nc = 512, 256
    dilated = [(gi, d) for gi, d in enumerate(DILATIONS) if d > 1]

    def body(h_ref, g_ref, w_ref, b_ref, u_ref, zq_ref, zg_ref, *rest):
        zf_refs, slabs = rest[:len(dilated)], rest[len(dilated):]
        xhat, _ = _rms(h_ref[...], g_ref[...])
        u = (xhat * g_ref[...]).astype(BF16)
        u_ref[...] = u
        for c in range(D_IN // nc):
            z = _nt(u, w_ref[c * nc:(c + 1) * nc, :]) + b_ref[:, c * nc:(c + 1) * nc]
            if c < QKV_W // nc:
                zq_ref[:, c * nc:(c + 1) * nc] = z.astype(BF16)
            else:
                zg_ref[:, c * nc - QKV_W:(c + 1) * nc - QKV_W] = z.astype(BF16)
            part, gi = divmod(c, len(DILATIONS))
            for k, (gk, d) in enumerate(dilated):
                if part < 3 and gi == gk:
                    slab, n = slabs[3 * k + part], tm // d
                    slab[0] = z[:, :128]
                    slab[1] = z[:, 128:]
                    for r in range(d):
                        for half in range(2):
                            c0 = 768 * r + 256 * part + 128 * half
                            zf_refs[k][:, c0:c0 + 128] = slab[half, pl.ds(r, n, stride=d), :].astype(BF16)

    row = lambda w: pl.BlockSpec((tm, w), lambda i: (i, 0))
    full = _resident
    sds = jax.ShapeDtypeStruct
    return _call(
        body, name=name, grid=(t // tm,),
        in_specs=[row(D_MODEL), full(g), full(w_t), full(b_in)],
        out_specs=[row(D_MODEL), row(QKV_W), row(GATE_W)]
        + [pl.BlockSpec((tm // d, d * 768), lambda i: (i, 0)) for _, d in dilated],
        out_shape=[sds((t, D_MODEL), BF16), sds((t, QKV_W), BF16), sds((t, GATE_W), BF16)]
        + [sds((t // d, d * 768), BF16) for _, d in dilated],
        scratch=[pltpu.VMEM((2, tm, 128), F32)] * (3 * len(dilated)),
        sem=("parallel",), args=(h, g, w_t, b_in), hosted=hosted)


def _inproj_bwd(pieces, h, g, dh_res, w_t, name):
    t = h.shape[0]
    tm = 256
    npiece = len(PIECE_W)
    offs = np.concatenate([[0], np.cumsum(PIECE_W)]).tolist()

    def body(*refs):
        p_refs = refs[:npiece]
        h_ref, g_ref, dhr_ref, w_ref, dh_ref, dz_ref, db_ref, dg_ref = refs[npiece:npiece + 8]
        slabs = list(refs[npiece + 8:])
        i = pl.program_id(0)

        @pl.when(i == 0)
        def _():
            db_ref[...] = jnp.zeros_like(db_ref)
            dg_ref[...] = jnp.zeros_like(dg_ref)

        du = jnp.zeros((tm, D_MODEL), F32)
        for k in range(npiece):
            o, w = offs[k], PIECE_W[k]
            token_order = _unfold(p_refs[k], slabs.pop(), PIECE_D[k]).astype(BF16) if PIECE_D[k] > 1 else None
            for c0 in range(0, w, 512):
                cw = min(512, w - c0)
                pz = p_refs[k][:, c0:c0 + cw] if token_order is None else token_order
                dz_ref[:, o + c0:o + c0 + cw] = pz
                db_ref[:, o + c0:o + c0 + cw] += jnp.sum(pz.astype(F32), axis=0, keepdims=True)
                du = du + _nn(pz, w_ref[o + c0:o + c0 + cw, :])
        xhat, r = _rms(h_ref[...], g_ref[...])
        dx, dg = _rms_bwd(du, xhat, r, g_ref[...])
        dh_ref[...] = dhr_ref[...] + dx
        dg_ref[...] += dg

    row = lambda w: pl.BlockSpec((tm, w), lambda i: (i, 0))
    full = lambda shp: pl.BlockSpec(shp, lambda i: (0, 0))
    return pl.pallas_call(
        body, name=name, grid=(t // tm,),
        in_specs=[pl.BlockSpec((tm // d, d * w), lambda i: (i, 0)) for w, d in zip(PIECE_W, PIECE_D)]
        + [row(D_MODEL), _resident(g), row(D_MODEL), _resident(w_t)],
        out_specs=[row(D_MODEL), row(D_IN), full((1, D_IN)), full((1, D_MODEL))],
        out_shape=[jax.ShapeDtypeStruct((t, D_MODEL), F32), jax.ShapeDtypeStruct((t, D_IN), BF16),
                   jax.ShapeDtypeStruct((1, D_IN), F32), jax.ShapeDtypeStruct((1, D_MODEL), F32)],
        scratch_shapes=[pltpu.VMEM((2, tm, 128), F32)] * sum(d > 1 for d in PIECE_D),
        compiler_params=_params("arbitrary"),
    )(*pieces, h, g, dh_res, w_t)


def _t5_bucket(dist):
    max_exact = N_BUCKETS // 2
    n = jnp.maximum(dist, 0)
    nf = jnp.maximum(n, 1).astype(jnp.float32)
    large = max_exact + (jnp.log(nf / max_exact) / math.log(MAX_DISTANCE / max_exact)
                         * (N_BUCKETS - max_exact)).astype(jnp.int32)
    large = jnp.minimum(large, N_BUCKETS - 1)
    return jnp.where(n < max_exact, n, large)


ATT_CFG = ((1, 128, 0, 4), (4, 128, 4, 4), (16, 128, 8, 4), (1, B_WINDOW - 1, A_HEADS, 8))


def _bucket_tiles():
    qi = jnp.arange(BLOCK)[:, None]
    ki = jnp.arange(2 * BLOCK)[None, :]
    dist = qi + BLOCK - ki
    return jnp.stack([_t5_bucket(dist * cfg[0]) for cfg in ATT_CFG]).astype(jnp.int32)


def _band(max_steps):
    row = lax.broadcasted_iota(jnp.int32, (BLOCK, 2 * BLOCK), 0)
    col = lax.broadcasted_iota(jnp.int32, (BLOCK, 2 * BLOCK), 1)
    dist = row + BLOCK - col
    return (dist >= 0) & (dist <= max_steps)


def _bias_build(table, buckets, hosted=()):
    def body(tab_ref, bt_ref, out_ref):
        col = lax.broadcasted_iota(jnp.int32, (BLOCK, 2 * BLOCK), 1)
        for ci, (_, max_steps, h0, nh) in enumerate(ATT_CFG):
            bt = bt_ref[ci]
            band = _band(max_steps)
            for h in range(h0, h0 + nh):
                acc = lax.fori_loop(0, N_BUCKETS, lambda b, acc: jnp.where(bt == b, tab_ref[b, h], acc),
                                    jnp.zeros((BLOCK, 2 * BLOCK), F32))
                out_ref[0, BLOCK * h:BLOCK * (h + 1), :] = jnp.where(band & (col >= BLOCK), acc, NEG)
                out_ref[1, BLOCK * h:BLOCK * (h + 1), :] = jnp.where(band, acc, NEG)

    return _call(
        body, name="bias_build", grid=(1,),
        in_specs=[pl.BlockSpec(memory_space=pltpu.SMEM), pl.BlockSpec(memory_space=pltpu.VMEM)],
        out_specs=[pl.BlockSpec(memory_space=pltpu.VMEM)],
        out_shape=[jax.ShapeDtypeStruct((2, TOTAL_HEADS * BLOCK, 2 * BLOCK), F32)],
        sem=("arbitrary",), args=(table, buckets), hosted=hosted)


def _bias_reduce(dbias, buckets, dsink_rows):
    def body(db_ref, bt_ref, ds_ref, out_ref, sink_ref):
        ri = lax.broadcasted_iota(jnp.int32, (N_BUCKETS, 128), 0)
        ci = lax.broadcasted_iota(jnp.int32, (N_BUCKETS, 128), 1)

        def per_bucket(b, acc):
            for cfg_i, (_, _, h0, nh) in enumerate(ATT_CFG):
                hit = bt_ref[cfg_i] == b
                for h in range(h0, h0 + nh):
                    val = jnp.sum(jnp.where(hit, db_ref[BLOCK * h:BLOCK * (h + 1), :], 0.0))
                    acc = jnp.where((ri == b) & (ci == h), val, acc)
            return acc

        out_ref[...] = lax.fori_loop(0, N_BUCKETS, per_bucket, jnp.zeros((N_BUCKETS, 128), F32))
        for h in range(8):
            sink_ref[h:h + 1, :] = jnp.sum(ds_ref[BLOCK * h:BLOCK * (h + 1), :], axis=0, keepdims=True)

    return pl.pallas_call(
        body, name="bias_reduce",
        in_specs=[pl.BlockSpec(memory_space=pltpu.VMEM)] * 3,
        out_specs=[pl.BlockSpec(memory_space=pltpu.VMEM)] * 2,
        out_shape=[jax.ShapeDtypeStruct((N_BUCKETS, 128), F32), jax.ShapeDtypeStruct((8, 128), F32)],
    )(dbias, buckets, dsink_rows)


class _Att:
    def __init__(self, cfg_i):
        stride, _, h0, nh = ATT_CFG[cfg_i]
        self.d = stride if cfg_i < 3 else 1
        self.h0, self.nh = h0, nh
        self.row_w = QKV_W if self.d == 1 else 3 * 256
        if cfg_i < 3:
            self.nq, self.wkv = 1, 256
            self.q_unit = [cfg_i if self.d == 1 else 0]
            self.k_unit, self.v_unit = (3 + cfg_i, 6 + cfg_i) if self.d == 1 else (1, 2)
            self.sinks = False
        else:
            self.nq, self.wkv = 2, 128
            self.q_unit = [9, 10]
            self.k_unit, self.v_unit = 22, 23
            self.sinks = True
        self.wq = 256 * self.nq


def _att_in_specs(cf, bsz):
    uq, ukv = cf.row_w // 256, cf.row_w // cf.wkv
    specs = [pl.BlockSpec((bsz, BLOCK, 256), functools.partial(lambda r, j, u: (0, j, r * uq + u), u=u))
             for u in cf.q_unit]
    for unit in (cf.k_unit, cf.v_unit):
        specs.append(pl.BlockSpec((bsz, BLOCK, cf.wkv),
                                  functools.partial(lambda r, j, u: (0, jnp.maximum(j - 1, 0), r * ukv + u), u=unit)))
        specs.append(pl.BlockSpec((bsz, BLOCK, cf.wkv),
                                  functools.partial(lambda r, j, u: (0, j, r * ukv + u), u=unit)))
    for qb in range(cf.nq):
        specs.append(pl.BlockSpec((None, HEADS_PER_BLOCK * BLOCK, 2 * BLOCK), functools.partial(
            lambda r, j, u: (jnp.minimum(j, 1), u, 0), u=cf.h0 // HEADS_PER_BLOCK + qb)))
    if cf.sinks:
        specs += [pl.BlockSpec((HEADS_PER_BLOCK * BLOCK, 128), functools.partial(lambda r, j, u: (u, 0), u=qb))
                  for qb in range(cf.nq)]
    return specs


HEADS_PER_BLOCK = 4


def _head_masks(rows):
    head = lax.broadcasted_iota(jnp.int32, (rows, 256), 1) // HEAD_DIM
    return [head == h for h in range(HEADS_PER_BLOCK)]


def _stack_heads(x, masks):
    return jnp.concatenate([jnp.where(m, x, jnp.zeros_like(x)) for m in masks], axis=0)


def _unstack_heads(x4, masks):
    blocks = [x4[BLOCK * h:BLOCK * (h + 1)] for h in range(HEADS_PER_BLOCK)]
    return jnp.where(masks[0], blocks[0], jnp.where(masks[1], blocks[1], jnp.where(masks[2], blocks[2], blocks[3])))


def _row_value(x):
    return jnp.max(x, axis=-1, keepdims=True)


def _kv_operands(cf, x):
    if cf.wkv == 256:
        return [x]
    lane = lax.broadcasted_iota(jnp.int32, x.shape, 1)
    swapped = pltpu.roll(x, HEAD_DIM, 1)
    halves = [jnp.where(lane < HEAD_DIM, x, swapped), jnp.where(lane < HEAD_DIM, swapped, x)]
    return [jnp.concatenate([half, half], axis=1) for half in halves]


def _kv_fold(cf, grads):
    if cf.wkv == 256:
        return grads[0]
    folded = []
    for g in grads:
        x = g[:, :128] + g[:, 128:]
        folded.append(x + pltpu.roll(x, HEAD_DIM, 1))
    lane = lax.broadcasted_iota(jnp.int32, folded[0].shape, 1)
    return jnp.where(lane < HEAD_DIM, folded[0], folded[1])


def _attn_fwd(cf, zf, bias, sinks, name, hosted=()):
    bsz, l, _ = zf.shape
    nb = l // BLOCK

    def body(*refs):
        refs = list(refs)
        q_refs = [refs.pop(0) for _ in range(cf.nq)]
        kp_ref, kc_ref, vp_ref, vc_ref = [refs.pop(0) for _ in range(4)]
        bias_refs = [refs.pop(0) for _ in range(cf.nq)]
        sink_refs = [refs.pop(0) for _ in range(cf.nq)] if cf.sinks else None
        o_ref, lse_ref = refs
        masks, kv_masks = _head_masks(BLOCK), _head_masks(2 * BLOCK)
        sinks4 = [_row_value(ref[...]) for ref in sink_refs] if cf.sinks else None
        for bi in range(bsz):
            k = jnp.concatenate([kp_ref[bi], kc_ref[bi]], axis=0)
            v = jnp.concatenate([vp_ref[bi], vc_ref[bi]], axis=0)
            k_ops, v_ops = _kv_operands(cf, k), _kv_operands(cf, v)
            for qb in range(cf.nq):
                cols = slice(256 * qb, 256 * (qb + 1))
                kb, vb = k_ops[qb], v_ops[qb]
                q4 = _stack_heads(q_refs[qb][bi] * (HEAD_DIM ** -0.5), masks)
                s = _nt(q4, kb) + bias_refs[qb][...]
                m = jnp.max(s, axis=-1, keepdims=True)
                if cf.sinks:
                    sk = sinks4[qb]
                    m = jnp.maximum(m, sk)
                p = jnp.exp(s - m)
                den = jnp.sum(p, axis=-1, keepdims=True)
                if cf.sinks:
                    den = den + jnp.exp(sk - m)
                pn = (p * (1.0 / den)).astype(BF16)
                p_lanes = jnp.concatenate([pn[BLOCK * h:BLOCK * (h + 1)] for h in range(HEADS_PER_BLOCK)], axis=1)
                v4 = jnp.concatenate([jnp.where(mk, vb, jnp.zeros_like(vb)) for mk in kv_masks], axis=0)
                o_ref[bi, :, cols] = _nn(p_lanes, v4)
                lse_ref[bi, :, cols] = _unstack_heads(
                    jnp.broadcast_to(m + jnp.log(den), (HEADS_PER_BLOCK * BLOCK, 256)), masks)

    in_specs = _att_in_specs(cf, bsz)
    args = [zf] * (cf.nq + 4) + [bias] * cf.nq + ([sinks] * cf.nq if cf.sinks else [])
    out = pl.BlockSpec((bsz, BLOCK, cf.wq), lambda r, j: (0, j, r))
    shape = jax.ShapeDtypeStruct((bsz, l, cf.d * cf.wq), F32)
    return _call(
        body, name=name, grid=(cf.d, nb), in_specs=in_specs, out_specs=[out, out], out_shape=[shape, shape],
        sem=("parallel", "arbitrary"), args=args, hosted=hosted)


def _attn_bwd(cf, zf, bias, sinks, stats, dy, name, hosted=()):
    bsz, l, _ = zf.shape
    nb = l // BLOCK

    def body(*refs):
        refs = list(refs)
        q_refs = [refs.pop(0) for _ in range(cf.nq)]
        kp_ref, kc_ref, vp_ref, vc_ref = [refs.pop(0) for _ in range(4)]
        bias_refs = [refs.pop(0) for _ in range(cf.nq)]
        sink_refs = [refs.pop(0) for _ in range(cf.nq)] if cf.sinks else None
        st_ref, dy_ref, dq_ref, dk_ref, dv_ref, dbias_ref = [refs.pop(0) for _ in range(6)]
        dsink_ref = refs.pop(0) if cf.sinks else None
        dk_acc, dv_acc = refs
        r, j = pl.program_id(0), pl.program_id(1)

        @pl.when((r == 0) & (j == 0))
        def _():
            dbias_ref[...] = jnp.zeros_like(dbias_ref)
            if cf.sinks:
                dsink_ref[...] = jnp.zeros_like(dsink_ref)

        @pl.when(j == 0)
        def _():
            dk_acc[...] = jnp.zeros_like(dk_acc)
            dv_acc[...] = jnp.zeros_like(dv_acc)

        masks = _head_masks(BLOCK)
        cur = pl.ds(pl.multiple_of(j * BLOCK, BLOCK), BLOCK)
        prev = pl.ds(pl.multiple_of(jnp.maximum(j - 1, 0) * BLOCK, BLOCK), BLOCK)
        sinks4 = [_row_value(ref[...]) for ref in sink_refs] if cf.sinks else None
        ds_sum, dsink_sum = [None] * cf.nq, [None] * cf.nq
        for bi in range(bsz):
            k = jnp.concatenate([kp_ref[bi], kc_ref[bi]], axis=0)
            v = jnp.concatenate([vp_ref[bi], vc_ref[bi]], axis=0)
            dk_blocks, dv_blocks = [], []
            k_ops, v_ops = _kv_operands(cf, k), _kv_operands(cf, v)
            for qb in range(cf.nq):
                cols = slice(256 * qb, 256 * (qb + 1))
                kb, vb = k_ops[qb], v_ops[qb]
                q4 = _stack_heads(q_refs[qb][bi] * (HEAD_DIM ** -0.5), masks)
                st = st_ref[bi, :, 128 * qb:128 * (qb + 1)]
                lt4, e4 = _read_stats(st, 0), _read_stats(st, 1)
                pa = jnp.exp(_nt(q4, kb) + bias_refs[qb][...] - lt4)
                dy = dy_ref[bi, :, cols]
                dy4 = _stack_heads(dy, masks)
                ds = pa * (_nt(dy4, vb) - e4)
                ds_sum[qb] = ds if ds_sum[qb] is None else ds_sum[qb] + ds
                if cf.sinks:
                    dsk = jnp.exp(sinks4[qb] - lt4) * e4
                    dsink_sum[qb] = dsk if dsink_sum[qb] is None else dsink_sum[qb] + dsk
                dsb = ds.astype(BF16)
                dq_ref[bi, :, cols] = (_unstack_heads(_nn(dsb, kb), masks) * (HEAD_DIM ** -0.5)).astype(dq_ref.dtype)
                dk_blocks.append(_tn(dsb, q4))
                dv_blocks.append(_tn(pa.astype(BF16), dy4))
            dk_new, dv_new = _kv_fold(cf, dk_blocks), _kv_fold(cf, dv_blocks)
            dk_acc[bi, cur, :] += dk_new[BLOCK:]
            dv_acc[bi, cur, :] += dv_new[BLOCK:]
            dk_acc[bi, prev, :] += dk_new[:BLOCK]
            dv_acc[bi, prev, :] += dv_new[:BLOCK]
        for qb in range(cf.nq):
            rows = slice(HEADS_PER_BLOCK * BLOCK * qb, HEADS_PER_BLOCK * BLOCK * (qb + 1))
            dbias_ref[rows, :] += ds_sum[qb]
            if cf.sinks:
                dsink_ref[rows, :] -= dsink_sum[qb]

        @pl.when(j == nb - 1)
        def _():
            dk_ref[...] = dk_acc[...].astype(dk_ref.dtype)
            dv_ref[...] = dv_acc[...].astype(dv_ref.dtype)

    tok = pl.BlockSpec((bsz, BLOCK, cf.wq), lambda r, j: (0, j, r))
    in_specs = _att_in_specs(cf, bsz) + [pl.BlockSpec((bsz, BLOCK, cf.wq // 2), lambda r, j: (0, j, r)), tok]
    args = [zf] * (cf.nq + 4) + [bias] * cf.nq + ([sinks] * cf.nq if cf.sinks else []) + [stats, dy]
    seq = pl.BlockSpec((bsz, l, cf.wkv), lambda r, j: (0, 0, r))
    out_specs = [tok, seq, seq, pl.BlockSpec((cf.nh * BLOCK, 2 * BLOCK), lambda r, j: (0, 0))]
    grad_dtype = BF16 if cf.d == 1 else F32
    out_shape = [jax.ShapeDtypeStruct((bsz, l, cf.d * cf.wq), grad_dtype),
                 jax.ShapeDtypeStruct((bsz, l, cf.d * cf.wkv), grad_dtype),
                 jax.ShapeDtypeStruct((bsz, l, cf.d * cf.wkv), grad_dtype),
                 jax.ShapeDtypeStruct((cf.nh * BLOCK, 2 * BLOCK), F32)]
    if cf.sinks:
        out_specs.append(pl.BlockSpec((cf.nh * BLOCK, 128), lambda r, j: (0, 0)))
        out_shape.append(jax.ShapeDtypeStruct((cf.nh * BLOCK, 128), F32))
    return _call(
        body, name=name, grid=(cf.d, nb), in_specs=in_specs, out_specs=out_specs, out_shape=out_shape,
        scratch=[pltpu.VMEM((bsz, l, cf.wkv), F32), pltpu.VMEM((bsz, l, cf.wkv), F32)],
        sem=("arbitrary", "arbitrary"), args=args, hosted=hosted)


def _merge_fwd(o_a, lse_a, o_b, zg, h, wa_t, wb_t, wout, name):
    t = h.shape[0]
    tm = 512

    def body(o1, o2, o3, l1, l2, l3, ob_ref, zg_ref, h_ref, wa_ref, wb_ref, wo_ref, h2_ref, ya_ref, lt_ref, *slabs):
        o = [_unfold(ref, slabs[i], d) for i, (ref, d) in enumerate(zip((o1, o2, o3), DILATIONS))]
        l = [_unfold(ref, slabs[3 + i], d) for i, (ref, d) in enumerate(zip((l1, l2, l3), DILATIONS))]
        m = jnp.maximum(jnp.maximum(l[0], l[1]), l[2])
        e1, e2, e3 = jnp.exp(l[0] - m), jnp.exp(l[1] - m), jnp.exp(l[2] - m)
        se = e1 + e2 + e3
        ya = (e1 / se) * o[0] + (e2 / se) * o[1] + (e3 / se) * o[2]
        ya_ref[...] = ya
        lt_ref[...] = m + jnp.log(se)
        pa = _nt(ya.astype(BF16), wa_ref[...])
        pb = _nt(ob_ref[...].astype(BF16), wb_ref[...])
        merged = (jax.nn.sigmoid(zg_ref[:, :D_MODEL].astype(F32)) * pa
                  + jax.nn.sigmoid(zg_ref[:, D_MODEL:].astype(F32)) * pb)
        h2_ref[...] = h_ref[...] + _nn(merged.astype(BF16), wo_ref[...])

    row = lambda w: pl.BlockSpec((tm, w), lambda i: (i, 0))
    folded = [pl.BlockSpec((tm // d, d * 256), lambda i: (i, 0)) for d in DILATIONS]
    full = _resident
    return pl.pallas_call(
        body, name=name, grid=(t // tm,),
        in_specs=folded + folded + [row(512), row(GATE_W), row(D_MODEL), full(wa_t), full(wb_t), full(wout)],
        out_specs=[row(D_MODEL), row(256), row(256)],
        out_shape=[jax.ShapeDtypeStruct((t, D_MODEL), F32), jax.ShapeDtypeStruct((t, 256), F32),
                   jax.ShapeDtypeStruct((t, 256), F32)],
        scratch_shapes=[pltpu.VMEM((2, tm, 128), F32)] * 6,
        compiler_params=_params("parallel"),
    )(*o_a, *lse_a, o_b, zg, h, wa_t, wb_t, wout)


STAT_LANES = 32


def _head_stats(lt, dy, y):
    rows, w = dy.shape
    lane = lax.broadcasted_iota(jnp.int32, (rows, w), 1)
    out_lane = lax.broadcasted_iota(jnp.int32, (rows, w // 2), 1)
    prod = dy * y
    out = jnp.zeros((rows, w // 2), F32)
    for hd in range(w // HEAD_DIM):
        mine = lane // HEAD_DIM == hd
        dot = jnp.sum(jnp.where(mine, prod, 0.0), axis=-1, keepdims=True)
        both = jnp.where(out_lane % STAT_LANES < STAT_LANES // 2, _row_value(jnp.where(mine, lt, NEG)), dot)
        out = jnp.where(out_lane // STAT_LANES == hd, both, out)
    return out


def _read_stats(st, part):
    lane = lax.broadcasted_iota(jnp.int32, st.shape, 1)
    half = (lane % STAT_LANES < STAT_LANES // 2) == (part == 0)
    return jnp.concatenate([_row_value(jnp.where((lane // STAT_LANES == hd) & half, st, NEG))
                            for hd in range(HEADS_PER_BLOCK)], axis=0)


def _merge_bwd(dh, ya, lse_tot, o_b, lse_b, zg, wa_t, wb_t, wout, name, hosted=()):
    t = dh.shape[0]
    tm = 256

    def body(dh_ref, ya_ref, lt_ref, ob_ref, lb_ref, zg_ref, wa_ref, wb_ref, wo_ref,
             mg_ref, dpa_ref, dpb_ref, dzg_ref, dy1, dy2, dy3, st1, st2, st3, dyb_ref, stb_ref, *slabs):
        dm = _nt(dh_ref[...].astype(BF16), wo_ref[...])
        pa = _nt(ya_ref[...].astype(BF16), wa_ref[...])
        pb = _nt(ob_ref[...].astype(BF16), wb_ref[...])
        sa = jax.nn.sigmoid(zg_ref[:, :D_MODEL].astype(F32))
        sb = jax.nn.sigmoid(zg_ref[:, D_MODEL:].astype(F32))
        mg_ref[...] = (sa * pa + sb * pb).astype(BF16)
        dpa = (dm * sa).astype(BF16)
        dpb = (dm * sb).astype(BF16)
        dpa_ref[...] = dpa
        dpb_ref[...] = dpb
        dzg_ref[:, :D_MODEL] = (dm * pa * (sa * (1.0 - sa))).astype(BF16)
        dzg_ref[:, D_MODEL:] = (dm * pb * (sb * (1.0 - sb))).astype(BF16)
        dya = _nn(dpa, wa_ref[...])
        dyb = _nn(dpb, wb_ref[...])
        stats = _head_stats(lt_ref[...], dya, ya_ref[...])
        for i, d in enumerate(DILATIONS):
            _fold(dya, slabs[i], (dy1, dy2, dy3)[i], d)
            _fold(stats, slabs[3 + i], (st1, st2, st3)[i], d)
        dyb_ref[...] = dyb.astype(BF16)
        stb_ref[...] = _head_stats(lb_ref[...], dyb, ob_ref[...])

    row = lambda w: pl.BlockSpec((tm, w), lambda i: (i, 0))
    folded = [pl.BlockSpec((tm // d, d * 256), lambda i: (i, 0)) for d in DILATIONS]
    full = _resident
    sds = jax.ShapeDtypeStruct
    dy_shape = [sds((t // d, d * 256), BF16) for d in DILATIONS]
    st_shape = [sds((t // d, d * 128), F32) for d in DILATIONS]
    st_specs = [pl.BlockSpec((tm // d, d * 128), lambda i: (i, 0)) for d in DILATIONS]
    return _call(
        body, name=name, grid=(t // tm,),
        in_specs=[row(D_MODEL), row(256), row(256), row(512), row(512), row(GATE_W), full(wa_t), full(wb_t),
                  full(wout)],
        out_specs=[row(D_MODEL), row(D_MODEL), row(D_MODEL), row(GATE_W)] + folded + st_specs + [row(512), row(256)],
        out_shape=[sds((t, D_MODEL), BF16), sds((t, D_MODEL), BF16), sds((t, D_MODEL), BF16),
                   sds((t, GATE_W), BF16)] + dy_shape + st_shape + [sds((t, 512), BF16), sds((t, 256), F32)],
        scratch=[pltpu.VMEM((2, tm, 128), F32)] * 6,
        sem=("parallel",), args=(dh, ya, lse_tot, o_b, lse_b, zg, wa_t, wb_t, wout), hosted=hosted)


def _pair_sum(grad, got, name):
    _, _, r, cdim = grad.shape
    core = lax.axis_index("c").astype(jnp.int32).reshape(1)

    def body(core_ref, g_ref, s_ref, o_ref):
        o_ref[...] = (g_ref[...] + s_ref[...]).astype(BF16)

    return pl.pallas_call(
        body, name=name,
        grid_spec=pltpu.PrefetchScalarGridSpec(
            num_scalar_prefetch=1, grid=(N_CHIP,),
            in_specs=[pl.BlockSpec((None, None, r, cdim), lambda q, core_ref: (q, core_ref[0], 0, 0)),
                      pl.BlockSpec((None, None, r, cdim), lambda q, core_ref: (q, 0, 0, 0))],
            out_specs=pl.BlockSpec((None, r, cdim), lambda q, core_ref: (q, 0, 0))),
        out_shape=jax.ShapeDtypeStruct((N_CHIP, r, cdim), BF16),
        compiler_params=_params("parallel"),
    )(core, grad, got)


def _adamw_math(w, g, m, v):
    m = ADAM_B1 * m + (1.0 - ADAM_B1) * g
    v = ADAM_B2 * v + (1.0 - ADAM_B2) * jnp.square(g)
    m_hat = m / (1.0 - ADAM_B1 ** ADAM_STEP)
    v_hat = v / (1.0 - ADAM_B2 ** ADAM_STEP)
    delta = -ADAM_LR * (m_hat / (jnp.sqrt(v_hat) + ADAM_EPS) + ADAM_WD * w)
    return delta, m, v


def _update(w, parts, m, v, transposed, name):
    r, c = parts.shape[1:]

    def body(w_ref, p_ref, m_ref, v_ref, g_ref, d_ref, m2_ref, v2_ref):
        def total(rows):
            return ((p_ref[0, rows].astype(F32) + p_ref[1, rows].astype(F32)) + p_ref[2, rows].astype(F32)) \
                + p_ref[3, rows].astype(F32)

        def update(at, g):
            g_ref[at] = g
            d_ref[at], m2_ref[at], v2_ref[at] = _adamw_math(w_ref[at], g, m_ref[at], v_ref[at])

        if not transposed:
            update((slice(None), slice(None)), total(slice(None)))
            return
        for r0 in range(0, r, 128):
            n = min(128, r - r0)
            gt = total(slice(r0, r0 + n))
            if n < 128:
                gt = jnp.concatenate([gt, jnp.zeros((128 - n, c), F32)], axis=0)
            update((slice(None), slice(r0, r0 + n)), gt.T[:, :n])

    sds = jax.ShapeDtypeStruct(w.shape, F32)
    return pl.pallas_call(body, name=name, out_shape=[sds] * 4,
                          compiler_params=pltpu.CompilerParams(vmem_limit_bytes=VMEM_LIMIT))(w, parts, m, v)


SMALL_ROWS = 80


def _small_update(g, w, m, v, name):
    def body(g_ref, w_ref, m_ref, v_ref, gs_ref, d_ref, m2_ref, v2_ref, got_ref, send_sems, recv_sems):
        x, y, c = _place()
        me = 4 * x + 2 * y + c
        got_ref[me] = g_ref[...]
        copies = []
        for k in range(1, N_DEV):
            peer = (x ^ (k >> 2), y ^ ((k >> 1) & 1), c ^ (k & 1))
            cp = pltpu.make_async_remote_copy(
                src_ref=g_ref, dst_ref=got_ref.at[me], send_sem=send_sems.at[k - 1], recv_sem=recv_sems.at[k - 1],
                device_id=peer, device_id_type=MESH)
            cp.start()
            copies.append(cp)
        for cp in copies:
            cp.wait()
        total = got_ref[0]
        for k in range(1, N_DEV):
            total = total + got_ref[k]
        gs_ref[...] = total
        d_ref[...], m2_ref[...], v2_ref[...] = _adamw_math(w_ref[...], total, m_ref[...], v_ref[...])

    sds = jax.ShapeDtypeStruct((SMALL_ROWS, 128), F32)
    vm = pl.BlockSpec(memory_space=pltpu.VMEM)
    return pl.pallas_call(
        body, name=name, in_specs=[vm] * 4, out_specs=[vm] * 4, out_shape=[sds] * 4,
        scratch_shapes=[pltpu.VMEM((N_DEV, SMALL_ROWS, 128), F32), pltpu.SemaphoreType.DMA((N_DEV - 1,)),
                        pltpu.SemaphoreType.DMA((N_DEV - 1,))],
    )(g, w, m, v)


def _pack_small(gains, b_in, rel_bias, sinks, last):
    rows = [a.reshape(8, 128) for a in gains] + [b_in.reshape(40, 128), rel_bias.reshape(5, 128),
                                                 jnp.pad(sinks.reshape(1, 8), ((0, 0), (0, 120))), last]
    rows.append(jnp.zeros((SMALL_ROWS - 79, 128), F32))
    return jnp.concatenate(rows, axis=0)


def _unpack_small(p, like):
    out = [p[8 * i:8 * i + 8].reshape(like[i].shape) for i in range(4)]
    out.append(p[32:72].reshape(like[4].shape))
    out.append(p[72:77].reshape(like[5].shape))
    out.append(p[77, :8].reshape(like[6].shape))
    return out


def kernel(x, ffn1_norm, ffn1_w_gate, ffn1_w_up, ffn1_w_down, mix_norm, w_in, b_in, w_branch_a, w_branch_b, w_out, sinks, rel_bias, ffn2_norm, ffn2_w_gate, ffn2_w_up, ffn2_w_down, final_norm, loss_target, m_ffn1_norm, m_ffn1_w_gate, m_ffn1_w_up, m_ffn1_w_down, m_mix_norm, m_w_in, m_b_in, m_w_branch_a, m_w_branch_b, m_w_out, m_sinks, m_rel_bias, m_ffn2_norm, m_ffn2_w_gate, m_ffn2_w_up, m_ffn2_w_down, m_final_norm, v_ffn1_norm, v_ffn1_w_gate, v_ffn1_w_up, v_ffn1_w_down, v_mix_norm, v_w_in, v_b_in, v_w_branch_a, v_w_branch_b, v_w_out, v_sinks, v_rel_bias, v_ffn2_norm, v_ffn2_w_gate, v_ffn2_w_up, v_ffn2_w_down, v_final_norm):
    bsz, seq, _ = x.shape
    t = bsz * seq
    xt = x.reshape(t, D_MODEL)
    target = loss_target.reshape(t, D_MODEL)

    big = [("ffn1_w_gate", ffn1_w_gate, m_ffn1_w_gate, v_ffn1_w_gate, True),
           ("ffn1_w_up", ffn1_w_up, m_ffn1_w_up, v_ffn1_w_up, True),
           ("ffn1_w_down", ffn1_w_down, m_ffn1_w_down, v_ffn1_w_down, False),
           ("w_in", w_in, m_w_in, v_w_in, True),
           ("w_branch_a", w_branch_a, m_w_branch_a, v_w_branch_a, True),
           ("w_branch_b", w_branch_b, m_w_branch_b, v_w_branch_b, True),
           ("w_out", w_out, m_w_out, v_w_out, False),
           ("ffn2_w_gate", ffn2_w_gate, m_ffn2_w_gate, v_ffn2_w_gate, True),
           ("ffn2_w_up", ffn2_w_up, m_ffn2_w_up, v_ffn2_w_up, True),
           ("ffn2_w_down", ffn2_w_down, m_ffn2_w_down, v_ffn2_w_down, False)]
    shard = {nm: (w[0].T if tr else w[0]).astype(BF16) for nm, w, _, _, tr in big}
    full = {}

    def gather(names):
        return _Gather([shard[nm] for nm in names])

    def keep(names, got):
        for nm, gw in zip(names, got):
            full[nm] = gw.reshape(-1, gw.shape[-1])

    ffn1_names = ["ffn1_w_gate", "ffn1_w_up", "ffn1_w_down"]
    mix_names = ["w_in", "w_branch_a", "w_branch_b", "w_out"]
    ffn2_names = ["ffn2_w_gate", "ffn2_w_up", "ffn2_w_down"]
    g1, gm, g2, gf = ffn1_norm, mix_norm, ffn2_norm, final_norm.reshape(1, D_MODEL)

    buckets = _bucket_tiles()
    (bias,), (got,) = _bias_build(rel_bias, buckets, hosted=[gather(ffn1_names)])
    keep(ffn1_names, got)
    (h1, n1, a1, b1, hff1), (got,) = _ffn_fwd(xt, g1, full["ffn1_w_gate"], full["ffn1_w_up"], full["ffn1_w_down"], "ffn1_fwd",
                                hosted=[gather(mix_names)])
    keep(mix_names, got)
    (u, zq, zg, *zdil), (got,) = _inproj_fwd(h1, gm, full["w_in"], b_in, "inproj_fwd", hosted=[gather(ffn2_names[:2])])
    keep(ffn2_names[:2], got)
    sink_rows = jnp.broadcast_to(sinks.reshape(8, 1, 1), (8, BLOCK, 128)).reshape(8 * BLOCK, 128)
    cfs = [_Att(i) for i in range(4)]
    zfold = []
    for i, cf in enumerate(cfs):
        if cf.d == 1:
            zfold.append(zq.reshape(bsz, seq, QKV_W))
        else:
            zfold.append(zdil[i - 1].reshape(bsz, seq // cf.d, cf.d * cf.row_w))
    att = [None] * 4
    for i in (3, 0, 1, 2):
        cf = cfs[i]
        (o, lse), got = _attn_fwd(cf, zfold[i], bias, sink_rows, f"attn{i}_fwd",
                                  hosted=[gather(ffn2_names[2:])] if i == 3 else ())
        if i == 3:
            keep(ffn2_names[2:], got[0])
        att[i] = (o.reshape(t // cf.d, cf.d * cf.wq), lse.reshape(t // cf.d, cf.d * cf.wq))
    o_b, lse_b = att[3]
    h2, ya, lse_tot = _merge_fwd([a[0] for a in att[:3]], [a[1] for a in att[:3]], o_b, zg, h1,
                                 full["w_branch_a"], full["w_branch_b"], full["w_out"], "merge_fwd")
    (dh3, n2, a2, b2, hff2, loss_part, dgf), _ = _ffn_fwd(
        h2, g2, full["ffn2_w_gate"], full["ffn2_w_up"], full["ffn2_w_down"], "ffn2_fwd", head=(gf, target))

    grads, pair, from_chips = {}, {}, {}

    def by_owner(nm):
        return grads[nm].reshape(N_CHIP, 2, -1, grads[nm].shape[-1])

    def to_core(names):
        return _CoreExchange([by_owner(nm) for nm in names])

    def pair_up(names, got):
        for nm, sib in zip(names, got):
            pair[nm] = _pair_sum(by_owner(nm), sib, f"pair_sum_{nm}")

    def to_chips(names):
        return _ChipExchange([pair[nm] for nm in names])

    def landed(names, got):
        for nm, parts in zip(names, got):
            from_chips[nm] = parts

    dh2, dg2, da, db = _ffn_bwd(h2, a2, b2, g2, dh3, full["ffn2_w_gate"], full["ffn2_w_up"],
                                          full["ffn2_w_down"], "ffn2_bwd")
    grads["ffn2_w_gate"], _ = _tn_matmul(da, n2, 1408, "ffn2_dgate")
    grads["ffn2_w_up"], _ = _tn_matmul(db, n2, 1408, "ffn2_dup")
    grads["ffn2_w_down"], _ = _tn_matmul(hff2, dh3, 1408, "ffn2_ddown", scale=0.5)
    (merged, dpa, dpb, dzg, *cot), (got,) = _merge_bwd(
        dh2, ya, lse_tot, o_b, lse_b, zg, full["w_branch_a"], full["w_branch_b"], full["w_out"], "merge_bwd",
        hosted=[to_core(ffn2_names)])
    dys, sts = cot[0:3] + [cot[6]], cot[3:6] + [cot[7]]
    pair_up(ffn2_names, got)
    dq, dk, dv, dbias, dsink = [None] * 4, [None] * 4, [None] * 4, [None] * 4, None
    for i in (3, 0, 1, 2):
        cf = cfs[i]
        shp = (bsz, seq // cf.d, cf.d * cf.wq)
        hosted = {3: lambda: [to_chips(ffn2_names[:2])], 0: lambda: [to_chips(ffn2_names[2:])]}.get(i, list)()
        res, got = _attn_bwd(cf, zfold[i], bias, sink_rows, sts[i].reshape(shp[:2] + (shp[2] // 2,)), dys[i].reshape(shp),
                             f"attn{i}_bwd", hosted=hosted)
        if i == 3:
            landed(ffn2_names[:2], got[0])
        elif i == 0:
            landed(ffn2_names[2:], got[0])
        dq[i] = res[0].reshape(t // cf.d, cf.d * cf.wq)
        dk[i] = res[1].reshape(t // cf.d, cf.d * cf.wkv)
        dv[i] = res[2].reshape(t // cf.d, cf.d * cf.wkv)
        dbias[i] = res[3]
        if cf.sinks:
            dsink = res[4]
    pieces = dq[:3] + dk[:3] + dv[:3] + [dq[3], dk[3], dv[3], dzg]
    dh1, dz, db_in, dgm = _inproj_bwd(pieces, h1, gm, dh2, full["w_in"], "inproj_bwd")
    dx, dg1, da, db = _ffn_bwd(xt, a1, b1, g1, dh1, full["ffn1_w_gate"], full["ffn1_w_up"],
                                         full["ffn1_w_down"], "ffn1_bwd")
    grads["w_in"], _ = _tn_matmul(dz, u, 1280, "dw_in")
    grads["ffn1_w_down"], (got,) = _tn_matmul(hff1, dh1, 1408, "ffn1_ddown", hosted=[to_core(["w_in"])], scale=0.5)
    pair_up(["w_in"], got)
    grads["ffn1_w_gate"], got = _tn_matmul(da, n1, 1408, "ffn1_dgate",
                                           hosted=[to_chips(["w_in"]), to_core(["ffn1_w_down"])])
    landed(["w_in"], got[0])
    pair_up(["ffn1_w_down"], got[1])
    grads["ffn1_w_up"], got = _tn_matmul(db, n1, 1408, "ffn1_dup",
                                         hosted=[to_chips(["ffn1_w_down"]), to_core(["ffn1_w_gate"])])
    landed(["ffn1_w_down"], got[0])
    pair_up(["ffn1_w_gate"], got[1])
    grads["w_out"], got = _tn_matmul(merged, dh2, 1024, "dw_out",
                                     hosted=[to_chips(["ffn1_w_gate"]), to_core(["ffn1_w_up"])])
    landed(["ffn1_w_gate"], got[0])
    pair_up(["ffn1_w_up"], got[1])
    grads["w_branch_b"], got = _tn_matmul(dpb, o_b, 1024, "dw_branch_b",
                                          hosted=[to_chips(["ffn1_w_up"]), to_core(["w_out"])])
    landed(["ffn1_w_up"], got[0])
    pair_up(["w_out"], got[1])
    grads["w_branch_a"], got = _tn_matmul(dpa, ya, 1024, "dw_branch_a",
                                          hosted=[to_chips(["w_out"]), to_core(["w_branch_b"])])
    landed(["w_out"], got[0])
    pair_up(["w_branch_b"], got[1])
    got = _exchange([to_chips(["w_branch_b"]), to_core(["w_branch_a"])], "reduce_scatter_tail1")
    landed(["w_branch_b"], got[0])
    pair_up(["w_branch_a"], got[1])
    landed(["w_branch_a"], _exchange([to_chips(["w_branch_a"])], "reduce_scatter_tail2")[0])
    dtable, dsinks = _bias_reduce(jnp.concatenate(dbias, axis=0), buckets, dsink)

    out_g, out_d, out_m, out_v = {}, {}, {}, {}
    for nm, w, m, v, tr in big:
        around = tr and w.shape[-1] % 128 != 0
        wmv = [a[0].T if around else a[0] for a in (w, m, v)]
        res = _update(wmv[0], from_chips[nm], wmv[1], wmv[2], tr and not around, f"update_{nm}")
        out_g[nm], out_d[nm], out_m[nm], out_v[nm] = [(a.T if around else a)[None] for a in res]

    small = [("ffn1_norm", ffn1_norm, m_ffn1_norm, v_ffn1_norm), ("mix_norm", mix_norm, m_mix_norm, v_mix_norm),
             ("ffn2_norm", ffn2_norm, m_ffn2_norm, v_ffn2_norm), ("final_norm", final_norm, m_final_norm, v_final_norm),
             ("b_in", b_in, m_b_in, v_b_in), ("rel_bias", rel_bias, m_rel_bias, v_rel_bias),
             ("sinks", sinks, m_sinks, v_sinks)]
    zero_row = jnp.zeros((1, 128), F32)
    pack = lambda arrs, last: _pack_small(arrs[:4], arrs[4], arrs[5], arrs[6], last)
    g_small = pack([dg1, dgm, dg2, dgf, db_in, dtable[:, :TOTAL_HEADS], dsinks[:, 0]], loss_part)
    packed = [pack([s[k] for s in small], zero_row) for k in (1, 2, 3)]
    gs, ds, ms, vs = _small_update(g_small, *packed, "small_update")
    like = [s[1] for s in small]
    for nm_s, g_, d_, m_, v_ in zip([s[0] for s in small], _unpack_small(gs, like), _unpack_small(ds, like),
                                    _unpack_small(ms, like), _unpack_small(vs, like)):
        out_g[nm_s], out_d[nm_s], out_m[nm_s], out_v[nm_s] = g_, d_, m_, v_
    loss = gs[78, 0]

    order = ["ffn1_norm", "ffn1_w_gate", "ffn1_w_up", "ffn1_w_down", "mix_norm", "w_in", "b_in", "w_branch_a",
             "w_branch_b", "w_out", "sinks", "rel_bias", "ffn2_norm", "ffn2_w_gate", "ffn2_w_up", "ffn2_w_down",
             "final_norm"]
    return (loss, dx.reshape(x.shape), *[out_g[k] for k in order], *[out_d[k] for k in order],
            *[out_m[k] for k in order], *[out_v[k] for k in order])
```

```python
import functools
import math

import numpy as np
import jax
import jax.numpy as jnp
from jax import lax
from jax.experimental import pallas as pl
from jax.experimental.pallas import tpu as pltpu

D_MODEL = 1024
D_FF = 2816
FF_CHUNK = 256
HEAD_DIM = 64
BLOCK = 128
N_BUCKETS = 32
MAX_DISTANCE = 2048
A_HEADS = 12
TOTAL_HEADS = 20
DIL_GROUPS = ((128, 1), (512, 4), (2048, 16))
B_WINDOW = 128
QKV_W = 3072
GATE_W = 2048
D_IN = QKV_W + GATE_W
EPS = 1e-6
NEG = -1e30
N_DEV = 8
N_CHIP = 4
ADAM_LR, ADAM_B1, ADAM_B2, ADAM_EPS, ADAM_WD, ADAM_STEP = 0.001, 0.9, 0.999, 1e-08, 0.01, 10
VMEM_LIMIT = 56 * 1024 * 1024
MESH = pl.DeviceIdType.MESH
BF16 = jnp.bfloat16
F32 = jnp.float32
ANY = pl.BlockSpec(memory_space=pl.ANY)


def _params(*sem):
    return pltpu.CompilerParams(dimension_semantics=sem, vmem_limit_bytes=VMEM_LIMIT)


def _resident(a):
    return pl.BlockSpec(a.shape, lambda i: (0, 0), pipeline_mode=pl.Buffered(1))


def _place():
    return lax.axis_index("x"), lax.axis_index("y"), lax.axis_index("c")


class _Gather:
    def __init__(self, shards):
        self.ins = list(shards)
        n = self.n = len(shards)
        self.out_shape = [jax.ShapeDtypeStruct((N_DEV,) + s.shape, s.dtype) for s in shards]
        self.scratch = [pltpu.SemaphoreType.DMA((7 * n,)), pltpu.SemaphoreType.DMA((7 * n,)),
                        pltpu.SemaphoreType.DMA((n,))]

    def _copies(self, ins, outs, sems):
        send_sems, recv_sems, local_sems = sems
        x, y, c = _place()
        me, sibling = (x, y, c), (x, y, 1 - c)
        chips = [(1 - x, y), (x, 1 - y), (1 - x, 1 - y)]

        def copy(i, k, block, to, src=None):
            dst = outs[i].at[4 * block[0] + 2 * block[1] + block[2]]
            return pltpu.make_async_remote_copy(
                src_ref=dst if src is None else src, dst_ref=dst, send_sem=send_sems.at[7 * i + k],
                recv_sem=recv_sems.at[7 * i + k], device_id=to, device_id_type=MESH)

        n = self.n
        south = c == 0
        relayed = (jnp.where(south, 1 - x, x), jnp.where(south, y, 1 - y), c)
        relay_to = (jnp.where(south, x, 1 - x), jnp.where(south, 1 - y, y), c)
        mine = [pltpu.make_async_copy(ins[i], outs[i].at[4 * x + 2 * y + c], local_sems.at[i]) for i in range(n)]
        first = [copy(i, 0, me, sibling, src=ins[i]) for i in range(n)]
        first += [copy(i, 1 + j, me, (*chips[j], c), src=ins[i]) for i in range(n) for j in range(2)]
        landed = [copy(i, 1 + j, (*chips[j], c), me) for j in range(2) for i in range(n)]
        passed = [copy(i, 4 + j, (*chips[j], c), sibling) for j in range(2) for i in range(n)]
        relays = [copy(i, 3, relayed, relay_to) for i in range(n)]
        diag_landed = [copy(i, 3, (*chips[2], c), me) for i in range(n)]
        diag_passed = [copy(i, 6, (*chips[2], c), sibling) for i in range(n)]
        from_sibling = [copy(i, 0, sibling, me) for i in range(n)]
        from_sibling += [copy(i, 4 + j, (*chip, 1 - c), me) for i in range(n) for j, chip in enumerate(chips)]
        return mine, first, landed, passed, relays, diag_landed, diag_passed, from_sibling

    def start(self, ins, outs, sems):
        mine, first = self._copies(ins, outs, sems)[:2]
        for cp in mine + first:
            cp.start()

    def mid(self, ins, outs, sems):
        _, _, landed, passed, relays, _, _, _ = self._copies(ins, outs, sems)
        for got in landed:
            got.wait_recv()
        for cp in relays + passed:
            cp.start()

    def end(self, ins, outs, sems):
        mine, first, _, passed, relays, diag_landed, diag_passed, from_sibling = self._copies(ins, outs, sems)
        for got, fwd in zip(diag_landed, diag_passed):
            got.wait_recv()
            fwd.start()
        for cp in from_sibling:
            cp.wait_recv()
        for cp in first + passed + relays + diag_passed:
            cp.wait_send()
        for cp in mine:
            cp.wait()


class _CoreExchange:
    def __init__(self, grads):
        self.ins = list(grads)
        n = self.n = len(grads)
        self.out_shape = [jax.ShapeDtypeStruct((N_CHIP, 1) + g.shape[2:], g.dtype) for g in grads]
        self.scratch = [pltpu.SemaphoreType.DMA((n,)), pltpu.SemaphoreType.DMA((n,))]

    def _copies(self, ins, outs, sems):
        x, y, c = _place()
        return [pltpu.make_async_remote_copy(
            src_ref=ins[i].at[:, pl.ds(1 - c, 1)], dst_ref=outs[i], send_sem=sems[0].at[i],
            recv_sem=sems[1].at[i], device_id=(x, y, 1 - c), device_id_type=MESH) for i in range(self.n)]

    def start(self, ins, outs, sems):
        for cp in self._copies(ins, outs, sems):
            cp.start()

    mid = None

    def end(self, ins, outs, sems):
        for cp in self._copies(ins, outs, sems):
            cp.wait()


class _ChipExchange:
    def __init__(self, parts):
        self.ins = list(parts)
        n = self.n = len(parts)
        self.out_shape = [jax.ShapeDtypeStruct(p.shape, p.dtype) for p in parts]
        self.scratch = [pltpu.SemaphoreType.DMA((3 * n,)), pltpu.SemaphoreType.DMA((3 * n,)),
                        pltpu.SemaphoreType.DMA((n,))]

    def _copies(self, ins, outs, sems):
        send_sems, recv_sems, local_sems = sems
        x, y, c = _place()
        my_chip = 2 * x + y
        copies = []
        for i in range(self.n):
            copies.append(pltpu.make_async_copy(ins[i].at[my_chip], outs[i].at[my_chip], local_sems.at[i]))
            for k, (qx, qy) in enumerate([(1 - x, y), (x, 1 - y), (1 - x, 1 - y)]):
                copies.append(pltpu.make_async_remote_copy(
                    src_ref=ins[i].at[2 * qx + qy], dst_ref=outs[i].at[my_chip], send_sem=send_sems.at[3 * i + k],
                    recv_sem=recv_sems.at[3 * i + k], device_id=(qx, qy, c), device_id_type=MESH))
        return copies

    def start(self, ins, outs, sems):
        for cp in self._copies(ins, outs, sems):
            cp.start()

    mid = None

    def end(self, ins, outs, sems):
        for cp in self._copies(ins, outs, sems):
            cp.wait()


def _call(body, *, name, grid, in_specs, out_specs, out_shape, args, scratch=(), sem=None, hosted=()):
    n_in, n_out, n_scr = len(in_specs), len(out_specs), len(scratch)
    x_in = [len(p.ins) for p in hosted]
    x_scr = [len(p.scratch) for p in hosted]
    steps = int(np.prod(grid))

    def wrapped(*refs):
        refs = list(refs)
        ins, refs = refs[:n_in], refs[n_in:]
        x_ins = [[refs.pop(0) for _ in range(k)] for k in x_in]
        outs, refs = refs[:n_out], refs[n_out:]
        x_outs = [[refs.pop(0) for _ in range(k)] for k in x_in]
        scr, refs = refs[:n_scr], refs[n_scr:]
        x_sems = [[refs.pop(0) for _ in range(k)] for k in x_scr]
        step = 0
        for d in range(len(grid)):
            step = step * grid[d] + pl.program_id(d)

        def phase(which, at):
            fns = [(getattr(p, which), a) for p, a in zip(hosted, zip(x_ins, x_outs, x_sems)) if getattr(p, which)]
            if fns:
                @pl.when(step == at)
                def _():
                    for fn, a in fns:
                        fn(*a)

        phase("start", 0)
        if steps > 1:
            phase("mid", (3 * steps) // 4)
        body(*ins, *outs, *scr)
        if steps == 1:
            phase("mid", 0)
        phase("end", steps - 1)

    results = pl.pallas_call(
        wrapped, name=name, grid=grid,
        in_specs=list(in_specs) + [ANY] * sum(x_in), out_specs=list(out_specs) + [ANY] * sum(x_in),
        out_shape=list(out_shape) + [s for p in hosted for s in p.out_shape],
        scratch_shapes=list(scratch) + [s for p in hosted for s in p.scratch],
        compiler_params=_params(*(("arbitrary",) * len(grid) if hosted else sem)),
    )(*args, *[a for p in hosted for a in p.ins])
    own, rest = list(results[:n_out]), list(results[n_out:])
    return own, [[rest.pop(0) for _ in range(k)] for k in x_in]


def _exchange(programs, name):
    x_in = [len(p.ins) for p in programs]
    x_scr = [len(p.scratch) for p in programs]

    def body(*refs):
        refs = list(refs)
        x_ins = [[refs.pop(0) for _ in range(k)] for k in x_in]
        x_outs = [[refs.pop(0) for _ in range(k)] for k in x_in]
        x_sems = [[refs.pop(0) for _ in range(k)] for k in x_scr]
        for which in ("start", "mid", "end"):
            for p, a in zip(programs, zip(x_ins, x_outs, x_sems)):
                if getattr(p, which):
                    getattr(p, which)(*a)

    results = list(pl.pallas_call(
        body, name=name, in_specs=[ANY] * sum(x_in), out_specs=[ANY] * sum(x_in),
        out_shape=[s for p in programs for s in p.out_shape],
        scratch_shapes=[s for p in programs for s in p.scratch],
    )(*[a for p in programs for a in p.ins]))
    return [[results.pop(0) for _ in range(k)] for k in x_in]


def _nt(a, b):
    return lax.dot_general(a, b, (((1,), (1,)), ((), ())), preferred_element_type=F32)


def _nn(a, b):
    return lax.dot_general(a, b, (((1,), (0,)), ((), ())), preferred_element_type=F32)


def _tn(a, b):
    return lax.dot_general(a, b, (((0,), (0,)), ((), ())), preferred_element_type=F32)


def _rms(x, g):
    r = lax.rsqrt(jnp.mean(x * x, axis=-1, keepdims=True) + EPS)
    return x * r, r


def _rms_bwd(dn, xhat, r, g):
    dg = jnp.sum(dn * xhat, axis=0, keepdims=True)
    dxh = dn * g
    dx = r * (dxh - xhat * jnp.mean(dxh * xhat, axis=-1, keepdims=True))
    return dx, dg


def _ffn_fwd(x, g, wg_t, wu_t, wd, name, hosted=(), head=None):
    t = x.shape[0]
    tm = 256

    def body(*refs):
        x_ref, g_ref, wg_ref, wu_ref, wd_ref = refs[:5]
        h_ref, n_ref, a_ref, b_ref, hff_ref = refs[-5 if head is None else -7:][:5]
        xhat, _ = _rms(x_ref[...], g_ref[...])
        n = (xhat * g_ref[...]).astype(BF16)
        n_ref[...] = n
        for c in range(0, D_FF, FF_CHUNK):
            cols = slice(c, c + FF_CHUNK)
            a = _nt(n, wg_ref[cols, :])
            b = _nt(n, wu_ref[cols, :])
            a_ref[:, cols] = a.astype(BF16)
            b_ref[:, cols] = b.astype(BF16)
            hff_ref[:, cols] = (a * jax.nn.sigmoid(a) * b).astype(BF16)
        h = x_ref[...] + 0.5 * _nn(hff_ref[...], wd_ref[...])
        if head is None:
            h_ref[...] = h
            return
        gf_ref, t_ref, loss_ref, dgf_ref = refs[5], refs[6], refs[-2], refs[-1]

        @pl.when(pl.program_id(0) == 0)
        def _():
            loss_ref[...] = jnp.zeros_like(loss_ref)
            dgf_ref[...] = jnp.zeros_like(dgf_ref)

        yhat, r = _rms(h, gf_ref[...])
        err = yhat * gf_ref[...] - t_ref[...]
        loss_ref[...] += 0.5 * jnp.sum(jnp.mean(err * err, axis=-1, keepdims=True), axis=0, keepdims=True)
        h_ref[...], dgf = _rms_bwd(err * (1.0 / D_MODEL), yhat, r, gf_ref[...])
        dgf_ref[...] += dgf

    row = pl.BlockSpec((tm, D_MODEL), lambda i: (i, 0))
    hid = pl.BlockSpec((tm, D_FF), lambda i: (i, 0))
    sds = jax.ShapeDtypeStruct
    in_specs = [row, _resident(g), _resident(wg_t), _resident(wu_t), _resident(wd)]
    out_specs = [row, row, hid, hid, hid]
    out_shape = [sds((t, D_MODEL), F32), sds((t, D_MODEL), BF16), sds((t, D_FF), BF16), sds((t, D_FF), BF16),
                 sds((t, D_FF), BF16)]
    args = (x, g, wg_t, wu_t, wd)
    if head is not None:
        in_specs += [_resident(head[0]), row]
        out_specs += [pl.BlockSpec((1, 128), lambda i: (0, 0)), pl.BlockSpec((1, D_MODEL), lambda i: (0, 0))]
        out_shape += [sds((1, 128), F32), sds((1, D_MODEL), F32)]
        args += tuple(head)
    return _call(body, name=name, grid=(t // tm,), in_specs=in_specs, out_specs=out_specs, out_shape=out_shape,
                 sem=("parallel",) if head is None else ("arbitrary",), args=args, hosted=hosted)


def _ffn_bwd(x, a_pre, b_pre, g, dh, wg_t, wu_t, wd, name):
    t = x.shape[0]
    tm = 256

    def body(x_ref, a_ref, b_ref, g_ref, dh_ref, wg_ref, wu_ref, wd_ref,
             dx_ref, dg_ref, da_ref, db_ref):
        @pl.when(pl.program_id(0) == 0)
        def _():
            dg_ref[...] = jnp.zeros_like(dg_ref)

        dhh = (0.5 * dh_ref[...]).astype(BF16)
        for c in range(0, D_FF, FF_CHUNK):
            cols = slice(c, c + FF_CHUNK)
            a = a_ref[:, cols].astype(F32)
            b = b_ref[:, cols].astype(F32)
            s = jax.nn.sigmoid(a)
            silu = a * s
            dhff = _nt(dhh, wd_ref[cols, :])
            da_ref[:, cols] = (dhff * b * (s * (1.0 + a * (1.0 - s)))).astype(BF16)
            db_ref[:, cols] = (dhff * silu).astype(BF16)
        dn = _nn(da_ref[...], wg_ref[...]) + _nn(db_ref[...], wu_ref[...])
        xhat, r = _rms(x_ref[...], g_ref[...])
        dx, dg = _rms_bwd(dn, xhat, r, g_ref[...])
        dx_ref[...] = dh_ref[...] + dx
        dg_ref[...] += dg

    row = pl.BlockSpec((tm, D_MODEL), lambda i: (i, 0))
    hid = pl.BlockSpec((tm, D_FF), lambda i: (i, 0))
    return pl.pallas_call(
        body, name=name, grid=(t // tm,),
        in_specs=[row, hid, hid, _resident(g), row, _resident(wg_t), _resident(wu_t), _resident(wd)],
        out_specs=[row, pl.BlockSpec((1, D_MODEL), lambda i: (0, 0)), hid, hid],
        out_shape=[jax.ShapeDtypeStruct((t, D_MODEL), F32), jax.ShapeDtypeStruct((1, D_MODEL), F32),
                   jax.ShapeDtypeStruct((t, D_FF), BF16), jax.ShapeDtypeStruct((t, D_FF), BF16)],
        compiler_params=_params("arbitrary"),
    )(x, a_pre, b_pre, g, dh, wg_t, wu_t, wd)


def _tn_matmul(a, b, rc, name, hosted=(), scale=None):
    t, r = a.shape
    c = b.shape[1]
    tk = min(t, 2048)

    def body(a_ref, b_ref, o_ref):
        @pl.when(pl.program_id(1) == 0)
        def _():
            o_ref[...] = jnp.zeros_like(o_ref)

        o_ref[...] += _tn(a_ref[...].astype(BF16), b_ref[...].astype(BF16))
        if scale is not None:
            @pl.when(pl.program_id(1) == t // tk - 1)
            def _():
                o_ref[...] *= scale

    (out,), got = _call(
        body, name=name, grid=(r // rc, t // tk),
        in_specs=[pl.BlockSpec((tk, rc), lambda i, k: (k, i)), pl.BlockSpec((tk, c), lambda i, k: (k, 0))],
        out_specs=[pl.BlockSpec((rc, c), lambda i, k: (i, 0))],
        out_shape=[jax.ShapeDtypeStruct((r, c), F32)],
        sem=("parallel", "arbitrary"), args=(a, b), hosted=hosted)
    return out, got


def _unfold(blk_ref, slab_ref, d):
    if d == 1:
        return blk_ref[...]
    n = blk_ref.shape[0]
    for r in range(d):
        for half in range(2):
            c0 = 256 * r + 128 * half
            slab_ref[half, pl.ds(r, n, stride=d), :] = blk_ref[:, c0:c0 + 128]
    return jnp.concatenate([slab_ref[0], slab_ref[1]], axis=1)


def _fold(x, slab_ref, out_ref, d):
    if d == 1:
        out_ref[...] = x.astype(out_ref.dtype)
        return
    n, w = out_ref.shape[0], x.shape[1]
    for part in range(w // 128):
        slab_ref[part] = x[:, 128 * part:128 * (part + 1)]
    for r in range(d):
        for part in range(w // 128):
            c0 = w * r + 128 * part
            out_ref[:, c0:c0 + 128] = slab_ref[part, pl.ds(r, n, stride=d), :].astype(out_ref.dtype)


DILATIONS = tuple(d for _, d in DIL_GROUPS)


PIECE_W = (256,) * 9 + (512, 128, 128, GATE_W)
PIECE_D = DILATIONS * 3 + (1, 1, 1, 1)


def _inproj_fwd(h, g, w_t, b_in, name, hosted=()):
    t = h.shape[0]
    tm, nc = 512, 256
    dilated = [(gi, d) for gi, d in enumerate(DILATIONS) if d > 1]

    def body(h_ref, g_ref, w_ref, b_ref, u_ref, zq_ref, zg_ref, *rest):
        zf_refs, slabs = rest[:len(dilated)], rest[len(dilated):]
        xhat, _ = _rms(h_ref[...], g_ref[...])
        u = (xhat * g_ref[...]).astype(BF16)
        u_ref[...] = u
        for c in range(D_IN // nc):
            z = _nt(u, w_ref[c * nc:(c + 1) * nc, :]) + b_ref[:, c * nc:(c + 1) * nc]
            if c < QKV_W // nc:
                zq_ref[:, c * nc:(c + 1) * nc] = z.astype(BF16)
            else:
                zg_ref[:, c * nc - QKV_W:(c + 1) * nc - QKV_W] = z.astype(BF16)
            part, gi = divmod(c, len(DILATIONS))
            for k, (gk, d) in enumerate(dilated):
                if part < 3 and gi == gk:
                    slab, n = slabs[3 * k + part], tm // d
                    slab[0] = z[:, :128]
                    slab[1] = z[:, 128:]
                    for r in range(d):
                        for half in range(2):
                            c0 = 768 * r + 256 * part + 128 * half
                            zf_refs[k][:, c0:c0 + 128] = slab[half, pl.ds(r, n, stride=d), :].astype(BF16)

    row = lambda w: pl.BlockSpec((tm, w), lambda i: (i, 0))
    full = _resident
    sds = jax.ShapeDtypeStruct
    return _call(
        body, name=name, grid=(t // tm,),
        in_specs=[row(D_MODEL), full(g), full(w_t), full(b_in)],
        out_specs=[row(D_MODEL), row(QKV_W), row(GATE_W)]
        + [pl.BlockSpec((tm // d, d * 768), lambda i: (i, 0)) for _, d in dilated],
        out_shape=[sds((t, D_MODEL), BF16), sds((t, QKV_W), BF16), sds((t, GATE_W), BF16)]
        + [sds((t // d, d * 768), BF16) for _, d in dilated],
        scratch=[pltpu.VMEM((2, tm, 128), F32)] * (3 * len(dilated)),
        sem=("parallel",), args=(h, g, w_t, b_in), hosted=hosted)


def _inproj_bwd(pieces, h, g, dh_res, w_t, name):
    t = h.shape[0]
    tm = 256
    npiece = len(PIECE_W)
    offs = np.concatenate([[0], np.cumsum(PIECE_W)]).tolist()

    def body(*refs):
        p_refs = refs[:npiece]
        h_ref, g_ref, dhr_ref, w_ref, dh_ref, dz_ref, db_ref, dg_ref = refs[npiece:npiece + 8]
        slabs = list(refs[npiece + 8:])
        i = pl.program_id(0)

        @pl.when(i == 0)
        def _():
            db_ref[...] = jnp.zeros_like(db_ref)
            dg_ref[...] = jnp.zeros_like(dg_ref)

        du = jnp.zeros((tm, D_MODEL), F32)
        for k in range(npiece):
            o, w = offs[k], PIECE_W[k]
            token_order = _unfold(p_refs[k], slabs.pop(), PIECE_D[k]).astype(BF16) if PIECE_D[k] > 1 else None
            for c0 in range(0, w, 512):
                cw = min(512, w - c0)
                pz = p_refs[k][:, c0:c0 + cw] if token_order is None else token_order
                dz_ref[:, o + c0:o + c0 + cw] = pz
                db_ref[:, o + c0:o + c0 + cw] += jnp.sum(pz.astype(F32), axis=0, keepdims=True)
                du = du + _nn(pz, w_ref[o + c0:o + c0 + cw, :])
        xhat, r = _rms(h_ref[...], g_ref[...])
        dx, dg = _rms_bwd(du, xhat, r, g_ref[...])
        dh_ref[...] = dhr_ref[...] + dx
        dg_ref[...] += dg

    row = lambda w: pl.BlockSpec((tm, w), lambda i: (i, 0))
    full = lambda shp: pl.BlockSpec(shp, lambda i: (0, 0))
    return pl.pallas_call(
        body, name=name, grid=(t // tm,),
        in_specs=[pl.BlockSpec((tm // d, d * w), lambda i: (i, 0)) for w, d in zip(PIECE_W, PIECE_D)]
        + [row(D_MODEL), _resident(g), row(D_MODEL), _resident(w_t)],
        out_specs=[row(D_MODEL), row(D_IN), full((1, D_IN)), full((1, D_MODEL))],
        out_shape=[jax.ShapeDtypeStruct((t, D_MODEL), F32), jax.ShapeDtypeStruct((t, D_IN), BF16),
                   jax.ShapeDtypeStruct((1, D_IN), F32), jax.ShapeDtypeStruct((1, D_MODEL), F32)],
        scratch_shapes=[pltpu.VMEM((2, tm, 128), F32)] * sum(d > 1 for d in PIECE_D),
        compiler_params=_params("arbitrary"),
    )(*pieces, h, g, dh_res, w_t)


def _t5_bucket(dist):
    max_exact = N_BUCKETS // 2
    n = jnp.maximum(dist, 0)
    nf = jnp.maximum(n, 1).astype(jnp.float32)
    large = max_exact + (jnp.log(nf / max_exact) / math.log(MAX_DISTANCE / max_exact)
                         * (N_BUCKETS - max_exact)).astype(jnp.int32)
    large = jnp.minimum(large, N_BUCKETS - 1)
    return jnp.where(n < max_exact, n, large)


ATT_CFG = ((1, 128, 0, 4), (4, 128, 4, 4), (16, 128, 8, 4), (1, B_WINDOW - 1, A_HEADS, 8))


def _bucket_tiles():
    qi = jnp.arange(BLOCK)[:, None]
    ki = jnp.arange(2 * BLOCK)[None, :]
    dist = qi + BLOCK - ki
    return jnp.stack([_t5_bucket(dist * cfg[0]) for cfg in ATT_CFG]).astype(jnp.int32)


def _band(max_steps):
    row = lax.broadcasted_iota(jnp.int32, (BLOCK, 2 * BLOCK), 0)
    col = lax.broadcasted_iota(jnp.int32, (BLOCK, 2 * BLOCK), 1)
    dist = row + BLOCK - col
    return (dist >= 0) & (dist <= max_steps)


def _bias_build(table, buckets, hosted=()):
    def body(tab_ref, bt_ref, out_ref):
        col = lax.broadcasted_iota(jnp.int32, (BLOCK, 2 * BLOCK), 1)
        for ci, (_, max_steps, h0, nh) in enumerate(ATT_CFG):
            bt = bt_ref[ci]
            band = _band(max_steps)
            for h in range(h0, h0 + nh):
                acc = lax.fori_loop(0, N_BUCKETS, lambda b, acc: jnp.where(bt == b, tab_ref[b, h], acc),
                                    jnp.zeros((BLOCK, 2 * BLOCK), F32))
                out_ref[0, BLOCK * h:BLOCK * (h + 1), :] = jnp.where(band & (col >= BLOCK), acc, NEG)
                out_ref[1, BLOCK * h:BLOCK * (h + 1), :] = jnp.where(band, acc, NEG)

    return _call(
        body, name="bias_build", grid=(1,),
        in_specs=[pl.BlockSpec(memory_space=pltpu.SMEM), pl.BlockSpec(memory_space=pltpu.VMEM)],
        out_specs=[pl.BlockSpec(memory_space=pltpu.VMEM)],
        out_shape=[jax.ShapeDtypeStruct((2, TOTAL_HEADS * BLOCK, 2 * BLOCK), F32)],
        sem=("arbitrary",), args=(table, buckets), hosted=hosted)


def _bias_reduce(dbias, buckets, dsink_rows):
    def body(db_ref, bt_ref, ds_ref, out_ref, sink_ref):
        ri = lax.broadcasted_iota(jnp.int32, (N_BUCKETS, 128), 0)
        ci = lax.broadcasted_iota(jnp.int32, (N_BUCKETS, 128), 1)

        def per_bucket(b, acc):
            for cfg_i, (_, _, h0, nh) in enumerate(ATT_CFG):
                hit = bt_ref[cfg_i] == b
                for h in range(h0, h0 + nh):
                    val = jnp.sum(jnp.where(hit, db_ref[BLOCK * h:BLOCK * (h + 1), :], 0.0))
                    acc = jnp.where((ri == b) & (ci == h), val, acc)
            return acc

        out_ref[...] = lax.fori_loop(0, N_BUCKETS, per_bucket, jnp.zeros((N_BUCKETS, 128), F32))
        for h in range(8):
            sink_ref[h:h + 1, :] = jnp.sum(ds_ref[BLOCK * h:BLOCK * (h + 1), :], axis=0, keepdims=True)

    return pl.pallas_call(
        body, name="bias_reduce",
        in_specs=[pl.BlockSpec(memory_space=pltpu.VMEM)] * 3,
        out_specs=[pl.BlockSpec(memory_space=pltpu.VMEM)] * 2,
        out_shape=[jax.ShapeDtypeStruct((N_BUCKETS, 128), F32), jax.ShapeDtypeStruct((8, 128), F32)],
    )(dbias, buckets, dsink_rows)


class _Att:
    def __init__(self, cfg_i):
        stride, _, h0, nh = ATT_CFG[cfg_i]
        self.d = stride if cfg_i < 3 else 1
        self.h0, self.nh = h0, nh
        self.row_w = QKV_W if self.d == 1 else 3 * 256
        if cfg_i < 3:
            self.nq, self.wkv = 1, 256
            self.q_unit = [cfg_i if self.d == 1 else 0]
            self.k_unit, self.v_unit = (3 + cfg_i, 6 + cfg_i) if self.d == 1 else (1, 2)
            self.sinks = False
        else:
            self.nq, self.wkv = 2, 128
            self.q_unit = [9, 10]
            self.k_unit, self.v_unit = 22, 23
            self.sinks = True
        self.wq = 256 * self.nq


def _att_in_specs(cf, bsz):
    uq, ukv = cf.row_w // 256, cf.row_w // cf.wkv
    specs = [pl.BlockSpec((bsz, BLOCK, 256), functools.partial(lambda r, j, u: (0, j, r * uq + u), u=u))
             for u in cf.q_unit]
    for unit in (cf.k_unit, cf.v_unit):
        specs.append(pl.BlockSpec((bsz, BLOCK, cf.wkv),
                                  functools.partial(lambda r, j, u: (0, jnp.maximum(j - 1, 0), r * ukv + u), u=unit)))
        specs.append(pl.BlockSpec((bsz, BLOCK, cf.wkv),
                                  functools.partial(lambda r, j, u: (0, j, r * ukv + u), u=unit)))
    for qb in range(cf.nq):
        specs.append(pl.BlockSpec((None, HEADS_PER_BLOCK * BLOCK, 2 * BLOCK), functools.partial(
            lambda r, j, u: (jnp.minimum(j, 1), u, 0), u=cf.h0 // HEADS_PER_BLOCK + qb)))
    if cf.sinks:
        specs += [pl.BlockSpec((HEADS_PER_BLOCK * BLOCK, 128), functools.partial(lambda r, j, u: (u, 0), u=qb))
                  for qb in range(cf.nq)]
    return specs


HEADS_PER_BLOCK = 4


def _head_masks(rows):
    head = lax.broadcasted_iota(jnp.int32, (rows, 256), 1) // HEAD_DIM
    return [head == h for h in range(HEADS_PER_BLOCK)]


def _stack_heads(x, masks):
    return jnp.concatenate([jnp.where(m, x, jnp.zeros_like(x)) for m in masks], axis=0)


def _unstack_heads(x4, masks):
    blocks = [x4[BLOCK * h:BLOCK * (h + 1)] for h in range(HEADS_PER_BLOCK)]
    return jnp.where(masks[0], blocks[0], jnp.where(masks[1], blocks[1], jnp.where(masks[2], blocks[2], blocks[3])))


def _row_value(x):
    return jnp.max(x, axis=-1, keepdims=True)


def _kv_operands(cf, x):
    if cf.wkv == 256:
        return [x]
    lane = lax.broadcasted_iota(jnp.int32, x.shape, 1)
    swapped = pltpu.roll(x, HEAD_DIM, 1)
    halves = [jnp.where(lane < HEAD_DIM, x, swapped), jnp.where(lane < HEAD_DIM, swapped, x)]
    return [jnp.concatenate([half, half], axis=1) for half in halves]


def _kv_fold(cf, grads):
    if cf.wkv == 256:
        return grads[0]
    folded = []
    for g in grads:
        x = g[:, :128] + g[:, 128:]
        folded.append(x + pltpu.roll(x, HEAD_DIM, 1))
    lane = lax.broadcasted_iota(jnp.int32, folded[0].shape, 1)
    return jnp.where(lane < HEAD_DIM, folded[0], folded[1])


def _attn_fwd(cf, zf, bias, sinks, name, hosted=()):
    bsz, l, _ = zf.shape
    nb = l // BLOCK

    def body(*refs):
        refs = list(refs)
        q_refs = [refs.pop(0) for _ in range(cf.nq)]
        kp_ref, kc_ref, vp_ref, vc_ref = [refs.pop(0) for _ in range(4)]
        bias_refs = [refs.pop(0) for _ in range(cf.nq)]
        sink_refs = [refs.pop(0) for _ in range(cf.nq)] if cf.sinks else None
        o_ref, lse_ref = refs
        masks, kv_masks = _head_masks(BLOCK), _head_masks(2 * BLOCK)
        sinks4 = [_row_value(ref[...]) for ref in sink_refs] if cf.sinks else None
        for bi in range(bsz):
            k = jnp.concatenate([kp_ref[bi], kc_ref[bi]], axis=0)
            v = jnp.concatenate([vp_ref[bi], vc_ref[bi]], axis=0)
            k_ops, v_ops = _kv_operands(cf, k), _kv_operands(cf, v)
            for qb in range(cf.nq):
                cols = slice(256 * qb, 256 * (qb + 1))
                kb, vb = k_ops[qb], v_ops[qb]
                q4 = _stack_heads(q_refs[qb][bi] * (HEAD_DIM ** -0.5), masks)
                s = _nt(q4, kb) + bias_refs[qb][...]
                m = jnp.max(s, axis=-1, keepdims=True)
                if cf.sinks:
                    sk = sinks4[qb]
                    m = jnp.maximum(m, sk)
                p = jnp.exp(s - m)
                den = jnp.sum(p, axis=-1, keepdims=True)
                if cf.sinks:
                    den = den + jnp.exp(sk - m)
                pn = (p * (1.0 / den)).astype(BF16)
                p_lanes = jnp.concatenate([pn[BLOCK * h:BLOCK * (h + 1)] for h in range(HEADS_PER_BLOCK)], axis=1)
                v4 = jnp.concatenate([jnp.where(mk, vb, jnp.zeros_like(vb)) for mk in kv_masks], axis=0)
                o_ref[bi, :, cols] = _nn(p_lanes, v4)
                lse_ref[bi, :, cols] = _unstack_heads(
                    jnp.broadcast_to(m + jnp.log(den), (HEADS_PER_BLOCK * BLOCK, 256)), masks)

    in_specs = _att_in_specs(cf, bsz)
    args = [zf] * (cf.nq + 4) + [bias] * cf.nq + ([sinks] * cf.nq if cf.sinks else [])
    out = pl.BlockSpec((bsz, BLOCK, cf.wq), lambda r, j: (0, j, r))
    shape = jax.ShapeDtypeStruct((bsz, l, cf.d * cf.wq), F32)
    return _call(
        body, name=name, grid=(cf.d, nb), in_specs=in_specs, out_specs=[out, out], out_shape=[shape, shape],
        sem=("parallel", "arbitrary"), args=args, hosted=hosted)


def _attn_bwd(cf, zf, bias, sinks, stats, dy, name, hosted=()):
    bsz, l, _ = zf.shape
    nb = l // BLOCK

    def body(*refs):
        refs = list(refs)
        q_refs = [refs.pop(0) for _ in range(cf.nq)]
        kp_ref, kc_ref, vp_ref, vc_ref = [refs.pop(0) for _ in range(4)]
        bias_refs = [refs.pop(0) for _ in range(cf.nq)]
        sink_refs = [refs.pop(0) for _ in range(cf.nq)] if cf.sinks else None
        st_ref, dy_ref, dq_ref, dk_ref, dv_ref, dbias_ref = [refs.pop(0) for _ in range(6)]
        dsink_ref = refs.pop(0) if cf.sinks else None
        dk_acc, dv_acc = refs
        r, j = pl.program_id(0), pl.program_id(1)

        @pl.when((r == 0) & (j == 0))
        def _():
            dbias_ref[...] = jnp.zeros_like(dbias_ref)
            if cf.sinks:
                dsink_ref[...] = jnp.zeros_like(dsink_ref)

        @pl.when(j == 0)
        def _():
            dk_acc[...] = jnp.zeros_like(dk_acc)
            dv_acc[...] = jnp.zeros_like(dv_acc)

        masks = _head_masks(BLOCK)
        cur = pl.ds(pl.multiple_of(j * BLOCK, BLOCK), BLOCK)
        prev = pl.ds(pl.multiple_of(jnp.maximum(j - 1, 0) * BLOCK, BLOCK), BLOCK)
        sinks4 = [_row_value(ref[...]) for ref in sink_refs] if cf.sinks else None
        ds_sum, dsink_sum = [None] * cf.nq, [None] * cf.nq
        for bi in range(bsz):
            k = jnp.concatenate([kp_ref[bi], kc_ref[bi]], axis=0)
            v = jnp.concatenate([vp_ref[bi], vc_ref[bi]], axis=0)
            dk_blocks, dv_blocks = [], []
            k_ops, v_ops = _kv_operands(cf, k), _kv_operands(cf, v)
            for qb in range(cf.nq):
                cols = slice(256 * qb, 256 * (qb + 1))
                kb, vb = k_ops[qb], v_ops[qb]
                q4 = _stack_heads(q_refs[qb][bi] * (HEAD_DIM ** -0.5), masks)
                st = st_ref[bi, :, 128 * qb:128 * (qb + 1)]
                lt4, e4 = _read_stats(st, 0), _read_stats(st, 1)
                pa = jnp.exp(_nt(q4, kb) + bias_refs[qb][...] - lt4)
                dy = dy_ref[bi, :, cols]
                dy4 = _stack_heads(dy, masks)
                ds = pa * (_nt(dy4, vb) - e4)
                ds_sum[qb] = ds if ds_sum[qb] is None else ds_sum[qb] + ds
                if cf.sinks:
                    dsk = jnp.exp(sinks4[qb] - lt4) * e4
                    dsink_sum[qb] = dsk if dsink_sum[qb] is None else dsink_sum[qb] + dsk
                dsb = ds.astype(BF16)
                dq_ref[bi, :, cols] = (_unstack_heads(_nn(dsb, kb), masks) * (HEAD_DIM ** -0.5)).astype(dq_ref.dtype)
                dk_blocks.append(_tn(dsb, q4))
                dv_blocks.append(_tn(pa.astype(BF16), dy4))
            dk_new, dv_new = _kv_fold(cf, dk_blocks), _kv_fold(cf, dv_blocks)
            dk_acc[bi, cur, :] += dk_new[BLOCK:]
            dv_acc[bi, cur, :] += dv_new[BLOCK:]
            dk_acc[bi, prev, :] += dk_new[:BLOCK]
            dv_acc[bi, prev, :] += dv_new[:BLOCK]
        for qb in range(cf.nq):
            rows = slice(HEADS_PER_BLOCK * BLOCK * qb, HEADS_PER_BLOCK * BLOCK * (qb + 1))
            dbias_ref[rows, :] += ds_sum[qb]
            if cf.sinks:
                dsink_ref[rows, :] -= dsink_sum[qb]

        @pl.when(j == nb - 1)
        def _():
            dk_ref[...] = dk_acc[...].astype(dk_ref.dtype)
            dv_ref[...] = dv_acc[...].astype(dv_ref.dtype)

    tok = pl.BlockSpec((bsz, BLOCK, cf.wq), lambda r, j: (0, j, r))
    in_specs = _att_in_specs(cf, bsz) + [pl.BlockSpec((bsz, BLOCK, cf.wq // 2), lambda r, j: (0, j, r)), tok]
    args = [zf] * (cf.nq + 4) + [bias] * cf.nq + ([sinks] * cf.nq if cf.sinks else []) + [stats, dy]
    seq = pl.BlockSpec((bsz, l, cf.wkv), lambda r, j: (0, 0, r))
    out_specs = [tok, seq, seq, pl.BlockSpec((cf.nh * BLOCK, 2 * BLOCK), lambda r, j: (0, 0))]
    grad_dtype = BF16 if cf.d == 1 else F32
    out_shape = [jax.ShapeDtypeStruct((bsz, l, cf.d * cf.wq), grad_dtype),
                 jax.ShapeDtypeStruct((bsz, l, cf.d * cf.wkv), grad_dtype),
                 jax.ShapeDtypeStruct((bsz, l, cf.d * cf.wkv), grad_dtype),
                 jax.ShapeDtypeStruct((cf.nh * BLOCK, 2 * BLOCK), F32)]
    if cf.sinks:
        out_specs.append(pl.BlockSpec((cf.nh * BLOCK, 128), lambda r, j: (0, 0)))
        out_shape.append(jax.ShapeDtypeStruct((cf.nh * BLOCK, 128), F32))
    return _call(
        body, name=name, grid=(cf.d, nb), in_specs=in_specs, out_specs=out_specs, out_shape=out_shape,
        scratch=[pltpu.VMEM((bsz, l, cf.wkv), F32), pltpu.VMEM((bsz, l, cf.wkv), F32)],
        sem=("arbitrary", "arbitrary"), args=args, hosted=hosted)


def _merge_fwd(o_a, lse_a, o_b, zg, h, wa_t, wb_t, wout, name):
    t = h.shape[0]
    tm = 512

    def body(o1, o2, o3, l1, l2, l3, ob_ref, zg_ref, h_ref, wa_ref, wb_ref, wo_ref, h2_ref, ya_ref, lt_ref, *slabs):
        o = [_unfold(ref, slabs[i], d) for i, (ref, d) in enumerate(zip((o1, o2, o3), DILATIONS))]
        l = [_unfold(ref, slabs[3 + i], d) for i, (ref, d) in enumerate(zip((l1, l2, l3), DILATIONS))]
        m = jnp.maximum(jnp.maximum(l[0], l[1]), l[2])
        e1, e2, e3 = jnp.exp(l[0] - m), jnp.exp(l[1] - m), jnp.exp(l[2] - m)
        se = e1 + e2 + e3
        ya = (e1 / se) * o[0] + (e2 / se) * o[1] + (e3 / se) * o[2]
        ya_ref[...] = ya
        lt_ref[...] = m + jnp.log(se)
        pa = _nt(ya.astype(BF16), wa_ref[...])
        pb = _nt(ob_ref[...].astype(BF16), wb_ref[...])
        merged = (jax.nn.sigmoid(zg_ref[:, :D_MODEL].astype(F32)) * pa
                  + jax.nn.sigmoid(zg_ref[:, D_MODEL:].astype(F32)) * pb)
        h2_ref[...] = h_ref[...] + _nn(merged.astype(BF16), wo_ref[...])

    row = lambda w: pl.BlockSpec((tm, w), lambda i: (i, 0))
    folded = [pl.BlockSpec((tm // d, d * 256), lambda i: (i, 0)) for d in DILATIONS]
    full = _resident
    return pl.pallas_call(
        body, name=name, grid=(t // tm,),
        in_specs=folded + folded + [row(512), row(GATE_W), row(D_MODEL), full(wa_t), full(wb_t), full(wout)],
        out_specs=[row(D_MODEL), row(256), row(256)],
        out_shape=[jax.ShapeDtypeStruct((t, D_MODEL), F32), jax.ShapeDtypeStruct((t, 256), F32),
                   jax.ShapeDtypeStruct((t, 256), F32)],
        scratch_shapes=[pltpu.VMEM((2, tm, 128), F32)] * 6,
        compiler_params=_params("parallel"),
    )(*o_a, *lse_a, o_b, zg, h, wa_t, wb_t, wout)


STAT_LANES = 32


def _head_stats(lt, dy, y):
    rows, w = dy.shape
    lane = lax.broadcasted_iota(jnp.int32, (rows, w), 1)
    out_lane = lax.broadcasted_iota(jnp.int32, (rows, w // 2), 1)
    prod = dy * y
    out = jnp.zeros((rows, w // 2), F32)
    for hd in range(w // HEAD_DIM):
        mine = lane // HEAD_DIM == hd
        dot = jnp.sum(jnp.where(mine, prod, 0.0), axis=-1, keepdims=True)
        both = jnp.where(out_lane % STAT_LANES < STAT_LANES // 2, _row_value(jnp.where(mine, lt, NEG)), dot)
        out = jnp.where(out_lane // STAT_LANES == hd, both, out)
    return out


def _read_stats(st, part):
    lane = lax.broadcasted_iota(jnp.int32, st.shape, 1)
    half = (lane % STAT_LANES < STAT_LANES // 2) == (part == 0)
    return jnp.concatenate([_row_value(jnp.where((lane // STAT_LANES == hd) & half, st, NEG))
                            for hd in range(HEADS_PER_BLOCK)], axis=0)


def _merge_bwd(dh, ya, lse_tot, o_b, lse_b, zg, wa_t, wb_t, wout, name, hosted=()):
    t = dh.shape[0]
    tm = 256

    def body(dh_ref, ya_ref, lt_ref, ob_ref, lb_ref, zg_ref, wa_ref, wb_ref, wo_ref,
             mg_ref, dpa_ref, dpb_ref, dzg_ref, dy1, dy2, dy3, st1, st2, st3, dyb_ref, stb_ref, *slabs):
        dm = _nt(dh_ref[...].astype(BF16), wo_ref[...])
        pa = _nt(ya_ref[...].astype(BF16), wa_ref[...])
        pb = _nt(ob_ref[...].astype(BF16), wb_ref[...])
        sa = jax.nn.sigmoid(zg_ref[:, :D_MODEL].astype(F32))
        sb = jax.nn.sigmoid(zg_ref[:, D_MODEL:].astype(F32))
        mg_ref[...] = (sa * pa + sb * pb).astype(BF16)
        dpa = (dm * sa).astype(BF16)
        dpb = (dm * sb).astype(BF16)
        dpa_ref[...] = dpa
        dpb_ref[...] = dpb
        dzg_ref[:, :D_MODEL] = (dm * pa * (sa * (1.0 - sa))).astype(BF16)
        dzg_ref[:, D_MODEL:] = (dm * pb * (sb * (1.0 - sb))).astype(BF16)
        dya = _nn(dpa, wa_ref[...])
        dyb = _nn(dpb, wb_ref[...])
        stats = _head_stats(lt_ref[...], dya, ya_ref[...])
        for i, d in enumerate(DILATIONS):
            _fold(dya, slabs[i], (dy1, dy2, dy3)[i], d)
            _fold(stats, slabs[3 + i], (st1, st2, st3)[i], d)
        dyb_ref[...] = dyb.astype(BF16)
        stb_ref[...] = _head_stats(lb_ref[...], dyb, ob_ref[...])

    row = lambda w: pl.BlockSpec((tm, w), lambda i: (i, 0))
    folded = [pl.BlockSpec((tm // d, d * 256), lambda i: (i, 0)) for d in DILATIONS]
    full = _resident
    sds = jax.ShapeDtypeStruct
    dy_shape = [sds((t // d, d * 256), BF16) for d in DILATIONS]
    st_shape = [sds((t // d, d * 128), F32) for d in DILATIONS]
    st_specs = [pl.BlockSpec((tm // d, d * 128), lambda i: (i, 0)) for d in DILATIONS]
    return _call(
        body, name=name, grid=(t // tm,),
        in_specs=[row(D_MODEL), row(256), row(256), row(512), row(512), row(GATE_W), full(wa_t), full(wb_t),
                  full(wout)],
        out_specs=[row(D_MODEL), row(D_MODEL), row(D_MODEL), row(GATE_W)] + folded + st_specs + [row(512), row(256)],
        out_shape=[sds((t, D_MODEL), BF16), sds((t, D_MODEL), BF16), sds((t, D_MODEL), BF16),
                   sds((t, GATE_W), BF16)] + dy_shape + st_shape + [sds((t, 512), BF16), sds((t, 256), F32)],
        scratch=[pltpu.VMEM((2, tm, 128), F32)] * 6,
        sem=("parallel",), args=(dh, ya, lse_tot, o_b, lse_b, zg, wa_t, wb_t, wout), hosted=hosted)


def _pair_sum(grad, got, name):
    _, _, r, cdim = grad.shape
    core = lax.axis_index("c").astype(jnp.int32).reshape(1)

    def body(core_ref, g_ref, s_ref, o_ref):
        o_ref[...] = (g_ref[...] + s_ref[...]).astype(BF16)

    return pl.pallas_call(
        body, name=name,
        grid_spec=pltpu.PrefetchScalarGridSpec(
            num_scalar_prefetch=1, grid=(N_CHIP,),
            in_specs=[pl.BlockSpec((None, None, r, cdim), lambda q, core_ref: (q, core_ref[0], 0, 0)),
                      pl.BlockSpec((None, None, r, cdim), lambda q, core_ref: (q, 0, 0, 0))],
            out_specs=pl.BlockSpec((None, r, cdim), lambda q, core_ref: (q, 0, 0))),
        out_shape=jax.ShapeDtypeStruct((N_CHIP, r, cdim), BF16),
        compiler_params=_params("parallel"),
    )(core, grad, got)


def _adamw_math(w, g, m, v):
    m = ADAM_B1 * m + (1.0 - ADAM_B1) * g
    v = ADAM_B2 * v + (1.0 - ADAM_B2) * jnp.square(g)
    m_hat = m / (1.0 - ADAM_B1 ** ADAM_STEP)
    v_hat = v / (1.0 - ADAM_B2 ** ADAM_STEP)
    delta = -ADAM_LR * (m_hat / (jnp.sqrt(v_hat) + ADAM_EPS) + ADAM_WD * w)
    return delta, m, v


def _update(w, parts, m, v, transposed, name):
    r, c = parts.shape[1:]

    def body(w_ref, p_ref, m_ref, v_ref, g_ref, d_ref, m2_ref, v2_ref):
        def total(rows):
            return ((p_ref[0, rows].astype(F32) + p_ref[1, rows].astype(F32)) + p_ref[2, rows].astype(F32)) \
                + p_ref[3, rows].astype(F32)

        def update(at, g):
            g_ref[at] = g
            d_ref[at], m2_ref[at], v2_ref[at] = _adamw_math(w_ref[at], g, m_ref[at], v_ref[at])

        if not transposed:
            update((slice(None), slice(None)), total(slice(None)))
            return
        for r0 in range(0, r, 128):
            n = min(128, r - r0)
            gt = total(slice(r0, r0 + n))
            if n < 128:
                gt = jnp.concatenate([gt, jnp.zeros((128 - n, c), F32)], axis=0)
            update((slice(None), slice(r0, r0 + n)), gt.T[:, :n])

    sds = jax.ShapeDtypeStruct(w.shape, F32)
    return pl.pallas_call(body, name=name, out_shape=[sds] * 4,
                          compiler_params=pltpu.CompilerParams(vmem_limit_bytes=VMEM_LIMIT))(w, parts, m, v)


SMALL_ROWS = 80


def _small_update(g, w, m, v, name):
    def body(g_ref, w_ref, m_ref, v_ref, gs_ref, d_ref, m2_ref, v2_ref, got_ref, send_sems, recv_sems):
        x, y, c = _place()
        me = 4 * x + 2 * y + c
        got_ref[me] = g_ref[...]
        copies = []
        for k in range(1, N_DEV):
            peer = (x ^ (k >> 2), y ^ ((k >> 1) & 1), c ^ (k & 1))
            cp = pltpu.make_async_remote_copy(
                src_ref=g_ref, dst_ref=got_ref.at[me], send_sem=send_sems.at[k - 1], recv_sem=recv_sems.at[k - 1],
                device_id=peer, device_id_type=MESH)
            cp.start()
            copies.append(cp)
        for cp in copies:
            cp.wait()
        total = got_ref[0]
        for k in range(1, N_DEV):
            total = total + got_ref[k]
        gs_ref[...] = total
        d_ref[...], m2_ref[...], v2_ref[...] = _adamw_math(w_ref[...], total, m_ref[...], v_ref[...])

    sds = jax.ShapeDtypeStruct((SMALL_ROWS, 128), F32)
    vm = pl.BlockSpec(memory_space=pltpu.VMEM)
    return pl.pallas_call(
        body, name=name, in_specs=[vm] * 4, out_specs=[vm] * 4, out_shape=[sds] * 4,
        scratch_shapes=[pltpu.VMEM((N_DEV, SMALL_ROWS, 128), F32), pltpu.SemaphoreType.DMA((N_DEV - 1,)),
                        pltpu.SemaphoreType.DMA((N_DEV - 1,))],
    )(g, w, m, v)


def _pack_small(gains, b_in, rel_bias, sinks, last):
    rows = [a.reshape(8, 128) for a in gains] + [b_in.reshape(40, 128), rel_bias.reshape(5, 128),
                                                 jnp.pad(sinks.reshape(1, 8), ((0, 0), (0, 120))), last]
    rows.append(jnp.zeros((SMALL_ROWS - 79, 128), F32))
    return jnp.concatenate(rows, axis=0)


def _unpack_small(p, like):
    out = [p[8 * i:8 * i + 8].reshape(like[i].shape) for i in range(4)]
    out.append(p[32:72].reshape(like[4].shape))
    out.append(p[72:77].reshape(like[5].shape))
    out.append(p[77, :8].reshape(like[6].shape))
    return out


def kernel(x, ffn1_norm, ffn1_w_gate, ffn1_w_up, ffn1_w_down, mix_norm, w_in, b_in, w_branch_a, w_branch_b, w_out, sinks, rel_bias, ffn2_norm, ffn2_w_gate, ffn2_w_up, ffn2_w_down, final_norm, loss_target, m_ffn1_norm, m_ffn1_w_gate, m_ffn1_w_up, m_ffn1_w_down, m_mix_norm, m_w_in, m_b_in, m_w_branch_a, m_w_branch_b, m_w_out, m_sinks, m_rel_bias, m_ffn2_norm, m_ffn2_w_gate, m_ffn2_w_up, m_ffn2_w_down, m_final_norm, v_ffn1_norm, v_ffn1_w_gate, v_ffn1_w_up, v_ffn1_w_down, v_mix_norm, v_w_in, v_b_in, v_w_branch_a, v_w_branch_b, v_w_out, v_sinks, v_rel_bias, v_ffn2_norm, v_ffn2_w_gate, v_ffn2_w_up, v_ffn2_w_down, v_final_norm):
    bsz, seq, _ = x.shape
    t = bsz * seq
    xt = x.reshape(t, D_MODEL)
    target = loss_target.reshape(t, D_MODEL)

    big = [("ffn1_w_gate", ffn1_w_gate, m_ffn1_w_gate, v_ffn1_w_gate, True),
           ("ffn1_w_up", ffn1_w_up, m_ffn1_w_up, v_ffn1_w_up, True),
           ("ffn1_w_down", ffn1_w_down, m_ffn1_w_down, v_ffn1_w_down, False),
           ("w_in", w_in, m_w_in, v_w_in, True),
           ("w_branch_a", w_branch_a, m_w_branch_a, v_w_branch_a, True),
           ("w_branch_b", w_branch_b, m_w_branch_b, v_w_branch_b, True),
           ("w_out", w_out, m_w_out, v_w_out, False),
           ("ffn2_w_gate", ffn2_w_gate, m_ffn2_w_gate, v_ffn2_w_gate, True),
           ("ffn2_w_up", ffn2_w_up, m_ffn2_w_up, v_ffn2_w_up, True),
           ("ffn2_w_down", ffn2_w_down, m_ffn2_w_down, v_ffn2_w_down, False)]
    shard = {nm: (w[0].T if tr else w[0]).astype(BF16) for nm, w, _, _, tr in big}
    full = {}

    def gather(names):
        return _Gather([shard[nm] for nm in names])

    def keep(names, got):
        for nm, gw in zip(names, got):
            full[nm] = gw.reshape(-1, gw.shape[-1])

    ffn1_names = ["ffn1_w_gate", "ffn1_w_up", "ffn1_w_down"]
    mix_names = ["w_in", "w_branch_a", "w_branch_b", "w_out"]
    ffn2_names = ["ffn2_w_gate", "ffn2_w_up", "ffn2_w_down"]
    g1, gm, g2, gf = ffn1_norm, mix_norm, ffn2_norm, final_norm.reshape(1, D_MODEL)

    buckets = _bucket_tiles()
    (bias,), (got,) = _bias_build(rel_bias, buckets, hosted=[gather(ffn1_names)])
    keep(ffn1_names, got)
    (h1, n1, a1, b1, hff1), (got,) = _ffn_fwd(xt, g1, full["ffn1_w_gate"], full["ffn1_w_up"], full["ffn1_w_down"], "ffn1_fwd",
                                hosted=[gather(mix_names)])
    keep(mix_names, got)
    (u, zq, zg, *zdil), (got,) = _inproj_fwd(h1, gm, full["w_in"], b_in, "inproj_fwd", hosted=[gather(ffn2_names[:2])])
    keep(ffn2_names[:2], got)
    sink_rows = jnp.broadcast_to(sinks.reshape(8, 1, 1), (8, BLOCK, 128)).reshape(8 * BLOCK, 128)
    cfs = [_Att(i) for i in range(4)]
    zfold = []
    for i, cf in enumerate(cfs):
        if cf.d == 1:
            zfold.append(zq.reshape(bsz, seq, QKV_W))
        else:
            zfold.append(zdil[i - 1].reshape(bsz, seq // cf.d, cf.d * cf.row_w))
    att = [None] * 4
    for i in (3, 0, 1, 2):
        cf = cfs[i]
        (o, lse), got = _attn_fwd(cf, zfold[i], bias, sink_rows, f"attn{i}_fwd",
                                  hosted=[gather(ffn2_names[2:])] if i == 3 else ())
        if i == 3:
            keep(ffn2_names[2:], got[0])
        att[i] = (o.reshape(t // cf.d, cf.d * cf.wq), lse.reshape(t // cf.d, cf.d * cf.wq))
    o_b, lse_b = att[3]
    h2, ya, lse_tot = _merge_fwd([a[0] for a in att[:3]], [a[1] for a in att[:3]], o_b, zg, h1,
                                 full["w_branch_a"], full["w_branch_b"], full["w_out"], "merge_fwd")
    (dh3, n2, a2, b2, hff2, loss_part, dgf), _ = _ffn_fwd(
        h2, g2, full["ffn2_w_gate"], full["ffn2_w_up"], full["ffn2_w_down"], "ffn2_fwd", head=(gf, target))

    grads, pair, from_chips = {}, {}, {}

    def by_owner(nm):
        return grads[nm].reshape(N_CHIP, 2, -1, grads[nm].shape[-1])

    def to_core(names):
        return _CoreExchange([by_owner(nm) for nm in names])

    def pair_up(names, got):
        for nm, sib in zip(names, got):
            pair[nm] = _pair_sum(by_owner(nm), sib, f"pair_sum_{nm}")

    def to_chips(names):
        return _ChipExchange([pair[nm] for nm in names])

    def landed(names, got):
        for nm, parts in zip(names, got):
            from_chips[nm] = parts

    dh2, dg2, da, db = _ffn_bwd(h2, a2, b2, g2, dh3, full["ffn2_w_gate"], full["ffn2_w_up"],
                                          full["ffn2_w_down"], "ffn2_bwd")
    grads["ffn2_w_gate"], _ = _tn_matmul(da, n2, 1408, "ffn2_dgate")
    grads["ffn2_w_up"], _ = _tn_matmul(db, n2, 1408, "ffn2_dup")
    grads["ffn2_w_down"], _ = _tn_matmul(hff2, dh3, 1408, "ffn2_ddown", scale=0.5)
    (merged, dpa, dpb, dzg, *cot), (got,) = _merge_bwd(
        dh2, ya, lse_tot, o_b, lse_b, zg, full["w_branch_a"], full["w_branch_b"], full["w_out"], "merge_bwd",
        hosted=[to_core(ffn2_names)])
    dys, sts = cot[0:3] + [cot[6]], cot[3:6] + [cot[7]]
    pair_up(ffn2_names, got)
    dq, dk, dv, dbias, dsink = [None] * 4, [None] * 4, [None] * 4, [None] * 4, None
    for i in (3, 0, 1, 2):
        cf = cfs[i]
        shp = (bsz, seq // cf.d, cf.d * cf.wq)
        hosted = {3: lambda: [to_chips(ffn2_names[:2])], 0: lambda: [to_chips(ffn2_names[2:])]}.get(i, list)()
        res, got = _attn_bwd(cf, zfold[i], bias, sink_rows, sts[i].reshape(shp[:2] + (shp[2] // 2,)), dys[i].reshape(shp),
                             f"attn{i}_bwd", hosted=hosted)
        if i == 3:
            landed(ffn2_names[:2], got[0])
        elif i == 0:
            landed(ffn2_names[2:], got[0])
        dq[i] = res[0].reshape(t // cf.d, cf.d * cf.wq)
        dk[i] = res[1].reshape(t // cf.d, cf.d * cf.wkv)
        dv[i] = res[2].reshape(t // cf.d, cf.d * cf.wkv)
        dbias[i] = res[3]
        if cf.sinks:
            dsink = res[4]
    pieces = dq[:3] + dk[:3] + dv[:3] + [dq[3], dk[3], dv[3], dzg]
    dh1, dz, db_in, dgm = _inproj_bwd(pieces, h1, gm, dh2, full["w_in"], "inproj_bwd")
    dx, dg1, da, db = _ffn_bwd(xt, a1, b1, g1, dh1, full["ffn1_w_gate"], full["ffn1_w_up"],
                                         full["ffn1_w_down"], "ffn1_bwd")
    grads["w_in"], _ = _tn_matmul(dz, u, 1280, "dw_in")
    grads["ffn1_w_down"], (got,) = _tn_matmul(hff1, dh1, 1408, "ffn1_ddown", hosted=[to_core(["w_in"])], scale=0.5)
    pair_up(["w_in"], got)
    grads["ffn1_w_gate"], got = _tn_matmul(da, n1, 1408, "ffn1_dgate",
                                           hosted=[to_chips(["w_in"]), to_core(["ffn1_w_down"])])
    landed(["w_in"], got[0])
    pair_up(["ffn1_w_down"], got[1])
    grads["ffn1_w_up"], got = _tn_matmul(db, n1, 1408, "ffn1_dup",
                                         hosted=[to_chips(["ffn1_w_down"]), to_core(["ffn1_w_gate"])])
    landed(["ffn1_w_down"], got[0])
    pair_up(["ffn1_w_gate"], got[1])
    grads["w_out"], got = _tn_matmul(merged, dh2, 1024, "dw_out",
                                     hosted=[to_chips(["ffn1_w_gate"]), to_core(["ffn1_w_up"])])
    landed(["ffn1_w_gate"], got[0])
    pair_up(["ffn1_w_up"], got[1])
    grads["w_branch_b"], got = _tn_matmul(dpb, o_b, 1024, "dw_branch_b",
                                          hosted=[to_chips(["ffn1_w_up"]), to_core(["w_out"])])
    landed(["ffn1_w_up"], got[0])
    pair_up(["w_out"], got[1])
    grads["w_branch_a"], got = _tn_matmul(dpa, ya, 1024, "dw_branch_a",
                                          hosted=[to_chips(["w_out"]), to_core(["w_branch_b"])])
    landed(["w_out"], got[0])
    pair_up(["w_branch_b"], got[1])
    got = _exchange([to_chips(["w_branch_b"]), to_core(["w_branch_a"])], "reduce_scatter_tail1")
    landed(["w_branch_b"], got[0])
    pair_up(["w_branch_a"], got[1])
    landed(["w_branch_a"], _exchange([to_chips(["w_branch_a"])], "reduce_scatter_tail2")[0])
    dtable, dsinks = _bias_reduce(jnp.concatenate(dbias, axis=0), buckets, dsink)

    out_g, out_d, out_m, out_v = {}, {}, {}, {}
    for nm, w, m, v, tr in big:
        around = tr and w.shape[-1] % 128 != 0
        wmv = [a[0].T if around else a[0] for a in (w, m, v)]
        res = _update(wmv[0], from_chips[nm], wmv[1], wmv[2], tr and not around, f"update_{nm}")
        out_g[nm], out_d[nm], out_m[nm], out_v[nm] = [(a.T if around else a)[None] for a in res]

    small = [("ffn1_norm", ffn1_norm, m_ffn1_norm, v_ffn1_norm), ("mix_norm", mix_norm, m_mix_norm, v_mix_norm),
             ("ffn2_norm", ffn2_norm, m_ffn2_norm, v_ffn2_norm), ("final_norm", final_norm, m_final_norm, v_final_norm),
             ("b_in", b_in, m_b_in, v_b_in), ("rel_bias", rel_bias, m_rel_bias, v_rel_bias),
             ("sinks", sinks, m_sinks, v_sinks)]
    zero_row = jnp.zeros((1, 128), F32)
    pack = lambda arrs, last: _pack_small(arrs[:4], arrs[4], arrs[5], arrs[6], last)
    g_small = pack([dg1, dgm, dg2, dgf, db_in, dtable[:, :TOTAL_HEADS], dsinks[:, 0]], loss_part)
    packed = [pack([s[k] for s in small], zero_row) for k in (1, 2, 3)]
    gs, ds, ms, vs = _small_update(g_small, *packed, "small_update")
    like = [s[1] for s in small]
    for nm_s, g_, d_, m_, v_ in zip([s[0] for s in small], _unpack_small(gs, like), _unpack_small(ds, like),
                                    _unpack_small(ms, like), _unpack_small(vs, like)):
        out_g[nm_s], out_d[nm_s], out_m[nm_s], out_v[nm_s] = g_, d_, m_, v_
    loss = gs[78, 0]

    order = ["ffn1_norm", "ffn1_w_gate", "ffn1_w_up", "ffn1_w_down", "mix_norm", "w_in", "b_in", "w_branch_a",
             "w_branch_b", "w_out", "sinks", "rel_bias", "ffn2_norm", "ffn2_w_gate", "ffn2_w_up", "ffn2_w_down",
             "final_norm"]
    return (loss, dx.reshape(x.shape), *[out_g[k] for k in order], *[out_d[k] for k in order],
            *[out_m[k] for k in order], *[out_v[k] for k in order])
```

```python
import functools
import math

import numpy as np
import jax
import jax.numpy as jnp
from jax import lax
from jax.experimental import pallas as pl
from jax.experimental.pallas import tpu as pltpu

D_MODEL = 1024
D_FF = 2816
FF_CHUNK = 256
HEAD_DIM = 64
BLOCK = 128
N_BUCKETS = 32
MAX_DISTANCE = 2048
A_HEADS = 12
TOTAL_HEADS = 20
DIL_GROUPS = ((128, 1), (512, 4), (2048, 16))
B_WINDOW = 128
QKV_W = 3072
GATE_W = 2048
D_IN = QKV_W + GATE_W
EPS = 1e-6
NEG = -1e30
N_DEV = 8
N_CHIP = 4
ADAM_LR, ADAM_B1, ADAM_B2, ADAM_EPS, ADAM_WD, ADAM_STEP = 0.001, 0.9, 0.999, 1e-08, 0.01, 10
VMEM_LIMIT = 56 * 1024 * 1024
MESH = pl.DeviceIdType.MESH
BF16 = jnp.bfloat16
F32 = jnp.float32
ANY = pl.BlockSpec(memory_space=pl.ANY)


def _params(*sem):
    return pltpu.CompilerParams(dimension_semantics=sem, vmem_limit_bytes=VMEM_LIMIT)


def _resident(a):
    return pl.BlockSpec(a.shape, lambda i: (0, 0), pipeline_mode=pl.Buffered(1))


def _place():
    return lax.axis_index("x"), lax.axis_index("y"), lax.axis_index("c")


class _Gather:
    def __init__(self, shards):
        self.ins = list(shards)
        n = self.n = len(shards)
        self.out_shape = [jax.ShapeDtypeStruct((N_DEV,) + s.shape, s.dtype) for s in shards]
        self.scratch = [pltpu.SemaphoreType.DMA((7 * n,)), pltpu.SemaphoreType.DMA((7 * n,)),
                        pltpu.SemaphoreType.DMA((n,))]

    def _copies(self, ins, outs, sems):
        send_sems, recv_sems, local_sems = sems
        x, y, c = _place()
        me, sibling = (x, y, c), (x, y, 1 - c)
        chips = [(1 - x, y), (x, 1 - y), (1 - x, 1 - y)]

        def copy(i, k, block, to, src=None):
            dst = outs[i].at[4 * block[0] + 2 * block[1] + block[2]]
            return pltpu.make_async_remote_copy(
                src_ref=dst if src is None else src, dst_ref=dst, send_sem=send_sems.at[7 * i + k],
                recv_sem=recv_sems.at[7 * i + k], device_id=to, device_id_type=MESH)

        n = self.n
        south = c == 0
        relayed = (jnp.where(south, 1 - x, x), jnp.where(south, y, 1 - y), c)
        relay_to = (jnp.where(south, x, 1 - x), jnp.where(south, 1 - y, y), c)
        mine = [pltpu.make_async_copy(ins[i], outs[i].at[4 * x + 2 * y + c], local_sems.at[i]) for i in range(n)]
        first = [copy(i, 0, me, sibling, src=ins[i]) for i in range(n)]
        first += [copy(i, 1 + j, me, (*chips[j], c), src=ins[i]) for i in range(n) for j in range(2)]
        landed = [copy(i, 1 + j, (*chips[j], c), me) for j in range(2) for i in range(n)]
        passed = [copy(i, 4 + j, (*chips[j], c), sibling) for j in range(2) for i in range(n)]
        relays = [copy(i, 3, relayed, relay_to) for i in range(n)]
        diag_landed = [copy(i, 3, (*chips[2], c), me) for i in range(n)]
        diag_passed = [copy(i, 6, (*chips[2], c), sibling) for i in range(n)]
        from_sibling = [copy(i, 0, sibling, me) for i in range(n)]
        from_sibling += [copy(i, 4 + j, (*chip, 1 - c), me) for i in range(n) for j, chip in enumerate(chips)]
        return mine, first, landed, passed, relays, diag_landed, diag_passed, from_sibling

    def start(self, ins, outs, sems):
        mine, first = self._copies(ins, outs, sems)[:2]
        for cp in mine + first:
            cp.start()

    def mid(self, ins, outs, sems):
        _, _, landed, passed, relays, _, _, _ = self._copies(ins, outs, sems)
        for got in landed:
            got.wait_recv()
        for cp in relays + passed:
            cp.start()

    def end(self, ins, outs, sems):
        mine, first, _, passed, relays, diag_landed, diag_passed, from_sibling = self._copies(ins, outs, sems)
        for got, fwd in zip(diag_landed, diag_passed):
            got.wait_recv()
            fwd.start()
        for cp in from_sibling:
            cp.wait_recv()
        for cp in first + passed + relays + diag_passed:
            cp.wait_send()
        for cp in mine:
            cp.wait()


class _CoreExchange:
    def __init__(self, grads):
        self.ins = list(grads)
        n = self.n = len(grads)
        self.out_shape = [jax.ShapeDtypeStruct((N_CHIP, 1) + g.shape[2:], g.dtype) for g in grads]
        self.scratch = [pltpu.SemaphoreType.DMA((n,)), pltpu.SemaphoreType.DMA((n,))]

    def _copies(self, ins, outs, sems):
        x, y, c = _place()
        return [pltpu.make_async_remote_copy(
            src_ref=ins[i].at[:, pl.ds(1 - c, 1)], dst_ref=outs[i], send_sem=sems[0].at[i],
            recv_sem=sems[1].at[i], device_id=(x, y, 1 - c), device_id_type=MESH) for i in range(self.n)]

    def start(self, ins, outs, sems):
        for cp in self._copies(ins, outs, sems):
            cp.start()

    mid = None

    def end(self, ins, outs, sems):
        for cp in self._copies(ins, outs, sems):
            cp.wait()


class _ChipExchange:
    def __init__(self, parts):
        self.ins = list(parts)
        n = self.n = len(parts)
        self.out_shape = [jax.ShapeDtypeStruct(p.shape, p.dtype) for p in parts]
        self.scratch = [pltpu.SemaphoreType.DMA((3 * n,)), pltpu.SemaphoreType.DMA((3 * n,)),
                        pltpu.SemaphoreType.DMA((n,))]

    def _copies(self, ins, outs, sems):
        send_sems, recv_sems, local_sems = sems
        x, y, c = _place()
        my_chip = 2 * x + y
        copies = []
        for i in range(self.n):
            copies.append(pltpu.make_async_copy(ins[i].at[my_chip], outs[i].at[my_chip], local_sems.at[i]))
            for k, (qx, qy) in enumerate([(1 - x, y), (x, 1 - y), (1 - x, 1 - y)]):
                copies.append(pltpu.make_async_remote_copy(
                    src_ref=ins[i].at[2 * qx + qy], dst_ref=outs[i].at[my_chip], send_sem=send_sems.at[3 * i + k],
                    recv_sem=recv_sems.at[3 * i + k], device_id=(qx, qy, c), device_id_type=MESH))
        return copies

    def start(self, ins, outs, sems):
        for cp in self._copies(ins, outs, sems):
            cp.start()

    mid = None

    def end(self, ins, outs, sems):
        for cp in self._copies(ins, outs, sems):
            cp.wait()


def _call(body, *, name, grid, in_specs, out_specs, out_shape, args, scratch=(), sem=None, hosted=()):
    n_in, n_out, n_scr = len(in_specs), len(out_specs), len(scratch)
    x_in = [len(p.ins) for p in hosted]
    x_scr = [len(p.scratch) for p in hosted]
    steps = int(np.prod(grid))

    def wrapped(*refs):
        refs = list(refs)
        ins, refs = refs[:n_in], refs[n_in:]
        x_ins = [[refs.pop(0) for _ in range(k)] for k in x_in]
        outs, refs = refs[:n_out], refs[n_out:]
        x_outs = [[refs.pop(0) for _ in range(k)] for k in x_in]
        scr, refs = refs[:n_scr], refs[n_scr:]
        x_sems = [[refs.pop(0) for _ in range(k)] for k in x_scr]
        step = 0
        for d in range(len(grid)):
            step = step * grid[d] + pl.program_id(d)

        def phase(which, at):
            fns = [(getattr(p, which), a) for p, a in zip(hosted, zip(x_ins, x_outs, x_sems)) if getattr(p, which)]
            if fns:
                @pl.when(step == at)
                def _():
                    for fn, a in fns:
                        fn(*a)

        phase("start", 0)
        if steps > 1:
            phase("mid", steps // 2)
        body(*ins, *outs, *scr)
        if steps == 1:
            phase("mid", 0)
        phase("end", steps - 1)

    results = pl.pallas_call(
        wrapped, name=name, grid=grid,
        in_specs=list(in_specs) + [ANY] * sum(x_in), out_specs=list(out_specs) + [ANY] * sum(x_in),
        out_shape=list(out_shape) + [s for p in hosted for s in p.out_shape],
        scratch_shapes=list(scratch) + [s for p in hosted for s in p.scratch],
        compiler_params=_params(*(("arbitrary",) * len(grid) if hosted else sem)),
    )(*args, *[a for p in hosted for a in p.ins])
    own, rest = list(results[:n_out]), list(results[n_out:])
    return own, [[rest.pop(0) for _ in range(k)] for k in x_in]


def _exchange(programs, name):
    x_in = [len(p.ins) for p in programs]
    x_scr = [len(p.scratch) for p in programs]

    def body(*refs):
        refs = list(refs)
        x_ins = [[refs.pop(0) for _ in range(k)] for k in x_in]
        x_outs = [[refs.pop(0) for _ in range(k)] for k in x_in]
        x_sems = [[refs.pop(0) for _ in range(k)] for k in x_scr]
        for which in ("start", "mid", "end"):
            for p, a in zip(programs, zip(x_ins, x_outs, x_sems)):
                if getattr(p, which):
                    getattr(p, which)(*a)

    results = list(pl.pallas_call(
        body, name=name, in_specs=[ANY] * sum(x_in), out_specs=[ANY] * sum(x_in),
        out_shape=[s for p in programs for s in p.out_shape],
        scratch_shapes=[s for p in programs for s in p.scratch],
    )(*[a for p in programs for a in p.ins]))
    return [[results.pop(0) for _ in range(k)] for k in x_in]


def _nt(a, b):
    return lax.dot_general(a, b, (((1,), (1,)), ((), ())), preferred_element_type=F32)


def _nn(a, b):
    return lax.dot_general(a, b, (((1,), (0,)), ((), ())), preferred_element_type=F32)


def _tn(a, b):
    return lax.dot_general(a, b, (((0,), (0,)), ((), ())), preferred_element_type=F32)


def _rms(x, g):
    r = lax.rsqrt(jnp.mean(x * x, axis=-1, keepdims=True) + EPS)
    return x * r, r


def _rms_bwd(dn, xhat, r, g):
    dg = jnp.sum(dn * xhat, axis=0, keepdims=True)
    dxh = dn * g
    dx = r * (dxh - xhat * jnp.mean(dxh * xhat, axis=-1, keepdims=True))
    return dx, dg


def _ffn_fwd(x, g, wg_t, wu_t, wd, name, hosted=(), head=None):
    t = x.shape[0]
    tm = 256

    def body(*refs):
        x_ref, g_ref, wg_ref, wu_ref, wd_ref = refs[:5]
        h_ref, n_ref, a_ref, b_ref, hff_ref = refs[-5 if head is None else -7:][:5]
        xhat, _ = _rms(x_ref[...], g_ref[...])
        n = (xhat * g_ref[...]).astype(BF16)
        n_ref[...] = n
        for c in range(0, D_FF, FF_CHUNK):
            cols = slice(c, c + FF_CHUNK)
            a = _nt(n, wg_ref[cols, :])
            b = _nt(n, wu_ref[cols, :])
            a_ref[:, cols] = a.astype(BF16)
            b_ref[:, cols] = b.astype(BF16)
            hff_ref[:, cols] = (a * jax.nn.sigmoid(a) * b).astype(BF16)
        h = x_ref[...] + 0.5 * _nn(hff_ref[...], wd_ref[...])
        if head is None:
            h_ref[...] = h
            return
        gf_ref, t_ref, loss_ref, dgf_ref = refs[5], refs[6], refs[-2], refs[-1]

        @pl.when(pl.program_id(0) == 0)
        def _():
            loss_ref[...] = jnp.zeros_like(loss_ref)
            dgf_ref[...] = jnp.zeros_like(dgf_ref)

        yhat, r = _rms(h, gf_ref[...])
        err = yhat * gf_ref[...] - t_ref[...]
        loss_ref[...] += 0.5 * jnp.sum(jnp.mean(err * err, axis=-1, keepdims=True), axis=0, keepdims=True)
        h_ref[...], dgf = _rms_bwd(err * (1.0 / D_MODEL), yhat, r, gf_ref[...])
        dgf_ref[...] += dgf

    row = pl.BlockSpec((tm, D_MODEL), lambda i: (i, 0))
    hid = pl.BlockSpec((tm, D_FF), lambda i: (i, 0))
    sds = jax.ShapeDtypeStruct
    in_specs = [row, _resident(g), _resident(wg_t), _resident(wu_t), _resident(wd)]
    out_specs = [row, row, hid, hid, hid]
    out_shape = [sds((t, D_MODEL), F32), sds((t, D_MODEL), BF16), sds((t, D_FF), BF16), sds((t, D_FF), BF16),
                 sds((t, D_FF), BF16)]
    args = (x, g, wg_t, wu_t, wd)
    if head is not None:
        in_specs += [_resident(head[0]), row]
        out_specs += [pl.BlockSpec((1, 128), lambda i: (0, 0)), pl.BlockSpec((1, D_MODEL), lambda i: (0, 0))]
        out_shape += [sds((1, 128), F32), sds((1, D_MODEL), F32)]
        args += tuple(head)
    return _call(body, name=name, grid=(t // tm,), in_specs=in_specs, out_specs=out_specs, out_shape=out_shape,
                 sem=("parallel",) if head is None else ("arbitrary",), args=args, hosted=hosted)


def _ffn_bwd(x, a_pre, b_pre, g, dh, wg_t, wu_t, wd, name):
    t = x.shape[0]
    tm = 256

    def body(x_ref, a_ref, b_ref, g_ref, dh_ref, wg_ref, wu_ref, wd_ref,
             dx_ref, dg_ref, da_ref, db_ref):
        @pl.when(pl.program_id(0) == 0)
        def _():
            dg_ref[...] = jnp.zeros_like(dg_ref)

        dhh = (0.5 * dh_ref[...]).astype(BF16)
        for c in range(0, D_FF, FF_CHUNK):
            cols = slice(c, c + FF_CHUNK)
            a = a_ref[:, cols].astype(F32)
            b = b_ref[:, cols].astype(F32)
            s = jax.nn.sigmoid(a)
            silu = a * s
            dhff = _nt(dhh, wd_ref[cols, :])
            da_ref[:, cols] = (dhff * b * (s * (1.0 + a * (1.0 - s)))).astype(BF16)
            db_ref[:, cols] = (dhff * silu).astype(BF16)
        dn = _nn(da_ref[...], wg_ref[...]) + _nn(db_ref[...], wu_ref[...])
        xhat, r = _rms(x_ref[...], g_ref[...])
        dx, dg = _rms_bwd(dn, xhat, r, g_ref[...])
        dx_ref[...] = dh_ref[...] + dx
        dg_ref[...] += dg

    row = pl.BlockSpec((tm, D_MODEL), lambda i: (i, 0))
    hid = pl.BlockSpec((tm, D_FF), lambda i: (i, 0))
    return pl.pallas_call(
        body, name=name, grid=(t // tm,),
        in_specs=[row, hid, hid, _resident(g), row, _resident(wg_t), _resident(wu_t), _resident(wd)],
        out_specs=[row, pl.BlockSpec((1, D_MODEL), lambda i: (0, 0)), hid, hid],
        out_shape=[jax.ShapeDtypeStruct((t, D_MODEL), F32), jax.ShapeDtypeStruct((1, D_MODEL), F32),
                   jax.ShapeDtypeStruct((t, D_FF), BF16), jax.ShapeDtypeStruct((t, D_FF), BF16)],
        compiler_params=_params("arbitrary"),
    )(x, a_pre, b_pre, g, dh, wg_t, wu_t, wd)


def _tn_matmul(a, b, rc, name, hosted=(), scale=None):
    t, r = a.shape
    c = b.shape[1]
    tk = min(t, 2048)

    def body(a_ref, b_ref, o_ref):
        @pl.when(pl.program_id(1) == 0)
        def _():
            o_ref[...] = jnp.zeros_like(o_ref)

        o_ref[...] += _tn(a_ref[...].astype(BF16), b_ref[...].astype(BF16))
        if scale is not None:
            @pl.when(pl.program_id(1) == t // tk - 1)
            def _():
                o_ref[...] *= scale

    (out,), got = _call(
        body, name=name, grid=(r // rc, t // tk),
        in_specs=[pl.BlockSpec((tk, rc), lambda i, k: (k, i)), pl.BlockSpec((tk, c), lambda i, k: (k, 0))],
        out_specs=[pl.BlockSpec((rc, c), lambda i, k: (i, 0))],
        out_shape=[jax.ShapeDtypeStruct((r, c), F32)],
        sem=("parallel", "arbitrary"), args=(a, b), hosted=hosted)
    return out, got


def _unfold(blk_ref, slab_ref, d):
    if d == 1:
        return blk_ref[...]
    n = blk_ref.shape[0]
    for r in range(d):
        for half in range(2):
            c0 = 256 * r + 128 * half
            slab_ref[half, pl.ds(r, n, stride=d), :] = blk_ref[:, c0:c0 + 128]
    return jnp.concatenate([slab_ref[0], slab_ref[1]], axis=1)


def _fold(x, slab_ref, out_ref, d):
    if d == 1:
        out_ref[...] = x.astype(out_ref.dtype)
        return
    n, w = out_ref.shape[0], x.shape[1]
    for part in range(w // 128):
        slab_ref[part] = x[:, 128 * part:128 * (part + 1)]
    for r in range(d):
        for part in range(w // 128):
            c0 = w * r + 128 * part
            out_ref[:, c0:c0 + 128] = slab_ref[part, pl.ds(r, n, stride=d), :].astype(out_ref.dtype)


DILATIONS = tuple(d for _, d in DIL_GROUPS)


PIECE_W = (256,) * 9 + (512, 128, 128, GATE_W)
PIECE_D = DILATIONS * 3 + (1, 1, 1, 1)


def _inproj_fwd(h, g, w_t, b_in, name, hosted=()):
    t = h.shape[0]
    tm, nc = 512, 256
    dilated = [(gi, d) for gi, d in enumerate(DILATIONS) if d > 1]

    def body(h_ref, g_ref, w_ref, b_ref, u_ref, zq_ref, zg_ref, *rest):
        zf_refs, slabs = rest[:len(dilated)], rest[len(dilated):]
        xhat, _ = _rms(h_ref[...], g_ref[...])
        u = (xhat * g_ref[...]).astype(BF16)
        u_ref[...] = u
        for c in range(D_IN // nc):
            z = _nt(u, w_ref[c * nc:(c + 1) * nc, :]) + b_ref[:, c * nc:(c + 1) * nc]
            if c < QKV_W // nc:
                zq_ref[:, c * nc:(c + 1) * nc] = z.astype(BF16)
            else:
                zg_ref[:, c * nc - QKV_W:(c + 1) * nc - QKV_W] = z.astype(BF16)
            part, gi = divmod(c, len(DILATIONS))
            for k, (gk, d) in enumerate(dilated):
                if part < 3 and gi == gk:
                    slab, n = slabs[3 * k + part], tm // d
                    slab[0] = z[:, :128]
                    slab[1] = z[:, 128:]
                    for r in range(d):
                        for half in range(2):
                            c0 = 768 * r + 256 * part + 128 * half
                            zf_refs[k][:, c0:c0 + 128] = slab[half, pl.ds(r, n, stride=d), :].astype(BF16)

    row = lambda w: pl.BlockSpec((tm, w), lambda i: (i, 0))
    full = _resident
    sds = jax.ShapeDtypeStruct
    return _call(
        body, name=name, grid=(t // tm,),
        in_specs=[row(D_MODEL), full(g), full(w_t), full(b_in)],
        out_specs=[row(D_MODEL), row(QKV_W), row(GATE_W)]
        + [pl.BlockSpec((tm // d, d * 768), lambda i: (i, 0)) for _, d in dilated],
        out_shape=[sds((t, D_MODEL), BF16), sds((t, QKV_W), BF16), sds((t, GATE_W), BF16)]
        + [sds((t // d, d * 768), BF16) for _, d in dilated],
        scratch=[pltpu.VMEM((2, tm, 128), F32)] * (3 * len(dilated)),
        sem=("parallel",), args=(h, g, w_t, b_in), hosted=hosted)


def _inproj_bwd(pieces, h, g, dh_res, w_t, name):
    t = h.shape[0]
    tm = 256
    npiece = len(PIECE_W)
    offs = np.concatenate([[0], np.cumsum(PIECE_W)]).tolist()

    def body(*refs):
        p_refs = refs[:npiece]
        h_ref, g_ref, dhr_ref, w_ref, dh_ref, dz_ref, db_ref, dg_ref = refs[npiece:npiece + 8]
        slabs = list(refs[npiece + 8:])
        i = pl.program_id(0)

        @pl.when(i == 0)
        def _():
            db_ref[...] = jnp.zeros_like(db_ref)
            dg_ref[...] = jnp.zeros_like(dg_ref)

        du = jnp.zeros((tm, D_MODEL), F32)
        for k in range(npiece):
            o, w = offs[k], PIECE_W[k]
            token_order = _unfold(p_refs[k], slabs.pop(), PIECE_D[k]).astype(BF16) if PIECE_D[k] > 1 else None
            for c0 in range(0, w, 512):
                cw = min(512, w - c0)
                pz = p_refs[k][:, c0:c0 + cw] if token_order is None else token_order
                dz_ref[:, o + c0:o + c0 + cw] = pz
                db_ref[:, o + c0:o + c0 + cw] += jnp.sum(pz.astype(F32), axis=0, keepdims=True)
                du = du + _nn(pz, w_ref[o + c0:o + c0 + cw, :])
        xhat, r = _rms(h_ref[...], g_ref[...])
        dx, dg = _rms_bwd(du, xhat, r, g_ref[...])
        dh_ref[...] = dhr_ref[...] + dx
        dg_ref[...] += dg

    row = lambda w: pl.BlockSpec((tm, w), lambda i: (i, 0))
    full = lambda shp: pl.BlockSpec(shp, lambda i: (0, 0))
    return pl.pallas_call(
        body, name=name, grid=(t // tm,),
        in_specs=[pl.BlockSpec((tm // d, d * w), lambda i: (i, 0)) for w, d in zip(PIECE_W, PIECE_D)]
        + [row(D_MODEL), _resident(g), row(D_MODEL), _resident(w_t)],
        out_specs=[row(D_MODEL), row(D_IN), full((1, D_IN)), full((1, D_MODEL))],
        out_shape=[jax.ShapeDtypeStruct((t, D_MODEL), F32), jax.ShapeDtypeStruct((t, D_IN), BF16),
                   jax.ShapeDtypeStruct((1, D_IN), F32), jax.ShapeDtypeStruct((1, D_MODEL), F32)],
        scratch_shapes=[pltpu.VMEM((2, tm, 128), F32)] * sum(d > 1 for d in PIECE_D),
        compiler_params=_params("arbitrary"),
    )(*pieces, h, g, dh_res, w_t)


def _t5_bucket(dist):
    max_exact = N_BUCKETS // 2
    n = jnp.maximum(dist, 0)
    nf = jnp.maximum(n, 1).astype(jnp.float32)
    large = max_exact + (jnp.log(nf / max_exact) / math.log(MAX_DISTANCE / max_exact)
                         * (N_BUCKETS - max_exact)).astype(jnp.int32)
    large = jnp.minimum(large, N_BUCKETS - 1)
    return jnp.where(n < max_exact, n, large)


ATT_CFG = ((1, 128, 0, 4), (4, 128, 4, 4), (16, 128, 8, 4), (1, B_WINDOW - 1, A_HEADS, 8))


def _bucket_tiles():
    qi = jnp.arange(BLOCK)[:, None]
    ki = jnp.arange(2 * BLOCK)[None, :]
    dist = qi + BLOCK - ki
    return jnp.stack([_t5_bucket(dist * cfg[0]) for cfg in ATT_CFG]).astype(jnp.int32)


def _band(max_steps):
    row = lax.broadcasted_iota(jnp.int32, (BLOCK, 2 * BLOCK), 0)
    col = lax.broadcasted_iota(jnp.int32, (BLOCK, 2 * BLOCK), 1)
    dist = row + BLOCK - col
    return (dist >= 0) & (dist <= max_steps)


def _bias_build(table, buckets, hosted=()):
    def body(tab_ref, bt_ref, out_ref):
        col = lax.broadcasted_iota(jnp.int32, (BLOCK, 2 * BLOCK), 1)
        for ci, (_, max_steps, h0, nh) in enumerate(ATT_CFG):
            bt = bt_ref[ci]
            band = _band(max_steps)
            for h in range(h0, h0 + nh):
                acc = lax.fori_loop(0, N_BUCKETS, lambda b, acc: jnp.where(bt == b, tab_ref[b, h], acc),
                                    jnp.zeros((BLOCK, 2 * BLOCK), F32))
                out_ref[0, BLOCK * h:BLOCK * (h + 1), :] = jnp.where(band & (col >= BLOCK), acc, NEG)
                out_ref[1, BLOCK * h:BLOCK * (h + 1), :] = jnp.where(band, acc, NEG)

    return _call(
        body, name="bias_build", grid=(1,),
        in_specs=[pl.BlockSpec(memory_space=pltpu.SMEM), pl.BlockSpec(memory_space=pltpu.VMEM)],
        out_specs=[pl.BlockSpec(memory_space=pltpu.VMEM)],
        out_shape=[jax.ShapeDtypeStruct((2, TOTAL_HEADS * BLOCK, 2 * BLOCK), F32)],
        sem=("arbitrary",), args=(table, buckets), hosted=hosted)


def _bias_reduce(dbias, buckets, dsink_rows):
    def body(db_ref, bt_ref, ds_ref, out_ref, sink_ref):
        ri = lax.broadcasted_iota(jnp.int32, (N_BUCKETS, 128), 0)
        ci = lax.broadcasted_iota(jnp.int32, (N_BUCKETS, 128), 1)

        def per_bucket(b, acc):
            for cfg_i, (_, _, h0, nh) in enumerate(ATT_CFG):
                hit = bt_ref[cfg_i] == b
                for h in range(h0, h0 + nh):
                    val = jnp.sum(jnp.where(hit, db_ref[BLOCK * h:BLOCK * (h + 1), :], 0.0))
                    acc = jnp.where((ri == b) & (ci == h), val, acc)
            return acc

        out_ref[...] = lax.fori_loop(0, N_BUCKETS, per_bucket, jnp.zeros((N_BUCKETS, 128), F32))
        for h in range(8):
            sink_ref[h:h + 1, :] = jnp.sum(ds_ref[BLOCK * h:BLOCK * (h + 1), :], axis=0, keepdims=True)

    return pl.pallas_call(
        body, name="bias_reduce",
        in_specs=[pl.BlockSpec(memory_space=pltpu.VMEM)] * 3,
        out_specs=[pl.BlockSpec(memory_space=pltpu.VMEM)] * 2,
        out_shape=[jax.ShapeDtypeStruct((N_BUCKETS, 128), F32), jax.ShapeDtypeStruct((8, 128), F32)],
    )(dbias, buckets, dsink_rows)


class _Att:
    def __init__(self, cfg_i):
        stride, _, h0, nh = ATT_CFG[cfg_i]
        self.d = stride if cfg_i < 3 else 1
        self.h0, self.nh = h0, nh
        self.row_w = QKV_W if self.d == 1 else 3 * 256
        if cfg_i < 3:
            self.nq, self.wkv = 1, 256
            self.q_unit = [cfg_i if self.d == 1 else 0]
            self.k_unit, self.v_unit = (3 + cfg_i, 6 + cfg_i) if self.d == 1 else (1, 2)
            self.sinks = False
        else:
            self.nq, self.wkv = 2, 128
            self.q_unit = [9, 10]
            self.k_unit, self.v_unit = 22, 23
            self.sinks = True
        self.wq = 256 * self.nq


def _att_in_specs(cf, bsz):
    uq, ukv = cf.row_w // 256, cf.row_w // cf.wkv
    specs = [pl.BlockSpec((bsz, BLOCK, 256), functools.partial(lambda r, j, u: (0, j, r * uq + u), u=u))
             for u in cf.q_unit]
    for unit in (cf.k_unit, cf.v_unit):
        specs.append(pl.BlockSpec((bsz, BLOCK, cf.wkv),
                                  functools.partial(lambda r, j, u: (0, jnp.maximum(j - 1, 0), r * ukv + u), u=unit)))
        specs.append(pl.BlockSpec((bsz, BLOCK, cf.wkv),
                                  functools.partial(lambda r, j, u: (0, j, r * ukv + u), u=unit)))
    for qb in range(cf.nq):
        specs.append(pl.BlockSpec((None, HEADS_PER_BLOCK * BLOCK, 2 * BLOCK), functools.partial(
            lambda r, j, u: (jnp.minimum(j, 1), u, 0), u=cf.h0 // HEADS_PER_BLOCK + qb)))
    if cf.sinks:
        specs += [pl.BlockSpec((HEADS_PER_BLOCK * BLOCK, 128), functools.partial(lambda r, j, u: (u, 0), u=qb))
                  for qb in range(cf.nq)]
    return specs


HEADS_PER_BLOCK = 4


def _head_masks(rows):
    head = lax.broadcasted_iota(jnp.int32, (rows, 256), 1) // HEAD_DIM
    return [head == h for h in range(HEADS_PER_BLOCK)]


def _stack_heads(x, masks):
    return jnp.concatenate([jnp.where(m, x, jnp.zeros_like(x)) for m in masks], axis=0)


def _unstack_heads(x4, masks):
    blocks = [x4[BLOCK * h:BLOCK * (h + 1)] for h in range(HEADS_PER_BLOCK)]
    return jnp.where(masks[0], blocks[0], jnp.where(masks[1], blocks[1], jnp.where(masks[2], blocks[2], blocks[3])))


def _row_value(x):
    return jnp.max(x, axis=-1, keepdims=True)


def _kv_operands(cf, x):
    if cf.wkv == 256:
        return [x]
    lane = lax.broadcasted_iota(jnp.int32, x.shape, 1)
    swapped = pltpu.roll(x, HEAD_DIM, 1)
    halves = [jnp.where(lane < HEAD_DIM, x, swapped), jnp.where(lane < HEAD_DIM, swapped, x)]
    return [jnp.concatenate([half, half], axis=1) for half in halves]


def _kv_fold(cf, grads):
    if cf.wkv == 256:
        return grads[0]
    folded = []
    for g in grads:
        x = g[:, :128] + g[:, 128:]
        folded.append(x + pltpu.roll(x, HEAD_DIM, 1))
    lane = lax.broadcasted_iota(jnp.int32, folded[0].shape, 1)
    return jnp.where(lane < HEAD_DIM, folded[0], folded[1])


def _attn_fwd(cf, zf, bias, sinks, name, hosted=()):
    bsz, l, _ = zf.shape
    nb = l // BLOCK

    def body(*refs):
        refs = list(refs)
        q_refs = [refs.pop(0) for _ in range(cf.nq)]
        kp_ref, kc_ref, vp_ref, vc_ref = [refs.pop(0) for _ in range(4)]
        bias_refs = [refs.pop(0) for _ in range(cf.nq)]
        sink_refs = [refs.pop(0) for _ in range(cf.nq)] if cf.sinks else None
        o_ref, lse_ref = refs
        masks, kv_masks = _head_masks(BLOCK), _head_masks(2 * BLOCK)
        sinks4 = [_row_value(ref[...]) for ref in sink_refs] if cf.sinks else None
        for bi in range(bsz):
            k = jnp.concatenate([kp_ref[bi], kc_ref[bi]], axis=0)
            v = jnp.concatenate([vp_ref[bi], vc_ref[bi]], axis=0)
            k_ops, v_ops = _kv_operands(cf, k), _kv_operands(cf, v)
            for qb in range(cf.nq):
                cols = slice(256 * qb, 256 * (qb + 1))
                kb, vb = k_ops[qb], v_ops[qb]
                q4 = _stack_heads(q_refs[qb][bi] * (HEAD_DIM ** -0.5), masks)
                s = _nt(q4, kb) + bias_refs[qb][...]
                m = jnp.max(s, axis=-1, keepdims=True)
                if cf.sinks:
                    sk = sinks4[qb]
                    m = jnp.maximum(m, sk)
                p = jnp.exp(s - m)
                den = jnp.sum(p, axis=-1, keepdims=True)
                if cf.sinks:
                    den = den + jnp.exp(sk - m)
                pn = (p * (1.0 / den)).astype(BF16)
                p_lanes = jnp.concatenate([pn[BLOCK * h:BLOCK * (h + 1)] for h in range(HEADS_PER_BLOCK)], axis=1)
                v4 = jnp.concatenate([jnp.where(mk, vb, jnp.zeros_like(vb)) for mk in kv_masks], axis=0)
                o_ref[bi, :, cols] = _nn(p_lanes, v4)
                lse_ref[bi, :, cols] = _unstack_heads(
                    jnp.broadcast_to(m + jnp.log(den), (HEADS_PER_BLOCK * BLOCK, 256)), masks)

    in_specs = _att_in_specs(cf, bsz)
    args = [zf] * (cf.nq + 4) + [bias] * cf.nq + ([sinks] * cf.nq if cf.sinks else [])
    out = pl.BlockSpec((bsz, BLOCK, cf.wq), lambda r, j: (0, j, r))
    shape = jax.ShapeDtypeStruct((bsz, l, cf.d * cf.wq), F32)
    return _call(
        body, name=name, grid=(cf.d, nb), in_specs=in_specs, out_specs=[out, out], out_shape=[shape, shape],
        sem=("parallel", "arbitrary"), args=args, hosted=hosted)


def _attn_bwd(cf, zf, bias, sinks, stats, dy, name, hosted=()):
    bsz, l, _ = zf.shape
    nb = l // BLOCK

    def body(*refs):
        refs = list(refs)
        q_refs = [refs.pop(0) for _ in range(cf.nq)]
        kp_ref, kc_ref, vp_ref, vc_ref = [refs.pop(0) for _ in range(4)]
        bias_refs = [refs.pop(0) for _ in range(cf.nq)]
        sink_refs = [refs.pop(0) for _ in range(cf.nq)] if cf.sinks else None
        st_ref, dy_ref, dq_ref, dk_ref, dv_ref, dbias_ref = [refs.pop(0) for _ in range(6)]
        dsink_ref = refs.pop(0) if cf.sinks else None
        dk_acc, dv_acc = refs
        r, j = pl.program_id(0), pl.program_id(1)

        @pl.when((r == 0) & (j == 0))
        def _():
            dbias_ref[...] = jnp.zeros_like(dbias_ref)
            if cf.sinks:
                dsink_ref[...] = jnp.zeros_like(dsink_ref)

        @pl.when(j == 0)
        def _():
            dk_acc[...] = jnp.zeros_like(dk_acc)
            dv_acc[...] = jnp.zeros_like(dv_acc)

        masks = _head_masks(BLOCK)
        cur = pl.ds(pl.multiple_of(j * BLOCK, BLOCK), BLOCK)
        prev = pl.ds(pl.multiple_of(jnp.maximum(j - 1, 0) * BLOCK, BLOCK), BLOCK)
        sinks4 = [_row_value(ref[...]) for ref in sink_refs] if cf.sinks else None
        ds_sum, dsink_sum = [None] * cf.nq, [None] * cf.nq
        for bi in range(bsz):
            k = jnp.concatenate([kp_ref[bi], kc_ref[bi]], axis=0)
            v = jnp.concatenate([vp_ref[bi], vc_ref[bi]], axis=0)
            dk_blocks, dv_blocks = [], []
            k_ops, v_ops = _kv_operands(cf, k), _kv_operands(cf, v)
            for qb in range(cf.nq):
                cols = slice(256 * qb, 256 * (qb + 1))
                kb, vb = k_ops[qb], v_ops[qb]
                q4 = _stack_heads(q_refs[qb][bi] * (HEAD_DIM ** -0.5), masks)
                st = st_ref[bi, :, 128 * qb:128 * (qb + 1)]
                lt4, e4 = _read_stats(st, 0), _read_stats(st, 1)
                pa = jnp.exp(_nt(q4, kb) + bias_refs[qb][...] - lt4)
                dy = dy_ref[bi, :, cols]
                dy4 = _stack_heads(dy, masks)
                ds = pa * (_nt(dy4, vb) - e4)
                ds_sum[qb] = ds if ds_sum[qb] is None else ds_sum[qb] + ds
                if cf.sinks:
                    dsk = jnp.exp(sinks4[qb] - lt4) * e4
                    dsink_sum[qb] = dsk if dsink_sum[qb] is None else dsink_sum[qb] + dsk
                dsb = ds.astype(BF16)
                dq_ref[bi, :, cols] = (_unstack_heads(_nn(dsb, kb), masks) * (HEAD_DIM ** -0.5)).astype(dq_ref.dtype)
                dk_blocks.append(_tn(dsb, q4))
                dv_blocks.append(_tn(pa.astype(BF16), dy4))
            dk_new, dv_new = _kv_fold(cf, dk_blocks), _kv_fold(cf, dv_blocks)
            dk_acc[bi, cur, :] += dk_new[BLOCK:]
            dv_acc[bi, cur, :] += dv_new[BLOCK:]
            dk_acc[bi, prev, :] += dk_new[:BLOCK]
            dv_acc[bi, prev, :] += dv_new[:BLOCK]
        for qb in range(cf.nq):
            rows = slice(HEADS_PER_BLOCK * BLOCK * qb, HEADS_PER_BLOCK * BLOCK * (qb + 1))
            dbias_ref[rows, :] += ds_sum[qb]
            if cf.sinks:
                dsink_ref[rows, :] -= dsink_sum[qb]

        @pl.when(j == nb - 1)
        def _():
            dk_ref[...] = dk_acc[...].astype(dk_ref.dtype)
            dv_ref[...] = dv_acc[...].astype(dv_ref.dtype)

    tok = pl.BlockSpec((bsz, BLOCK, cf.wq), lambda r, j: (0, j, r))
    in_specs = _att_in_specs(cf, bsz) + [pl.BlockSpec((bsz, BLOCK, cf.wq // 2), lambda r, j: (0, j, r)), tok]
    args = [zf] * (cf.nq + 4) + [bias] * cf.nq + ([sinks] * cf.nq if cf.sinks else []) + [stats, dy]
    seq = pl.BlockSpec((bsz, l, cf.wkv), lambda r, j: (0, 0, r))
    out_specs = [tok, seq, seq, pl.BlockSpec((cf.nh * BLOCK, 2 * BLOCK), lambda r, j: (0, 0))]
    grad_dtype = BF16 if cf.d == 1 else F32
    out_shape = [jax.ShapeDtypeStruct((bsz, l, cf.d * cf.wq), grad_dtype),
                 jax.ShapeDtypeStruct((bsz, l, cf.d * cf.wkv), grad_dtype),
                 jax.ShapeDtypeStruct((bsz, l, cf.d * cf.wkv), grad_dtype),
                 jax.ShapeDtypeStruct((cf.nh * BLOCK, 2 * BLOCK), F32)]
    if cf.sinks:
        out_specs.append(pl.BlockSpec((cf.nh * BLOCK, 128), lambda r, j: (0, 0)))
        out_shape.append(jax.ShapeDtypeStruct((cf.nh * BLOCK, 128), F32))
    return _call(
        body, name=name, grid=(cf.d, nb), in_specs=in_specs, out_specs=out_specs, out_shape=out_shape,
        scratch=[pltpu.VMEM((bsz, l, cf.wkv), F32), pltpu.VMEM((bsz, l, cf.wkv), F32)],
        sem=("arbitrary", "arbitrary"), args=args, hosted=hosted)


def _merge_fwd(o_a, lse_a, o_b, zg, h, wa_t, wb_t, wout, name):
    t = h.shape[0]
    tm = 512

    def body(o1, o2, o3, l1, l2, l3, ob_ref, zg_ref, h_ref, wa_ref, wb_ref, wo_ref, h2_ref, ya_ref, lt_ref, *slabs):
        o = [_unfold(ref, slabs[i], d) for i, (ref, d) in enumerate(zip((o1, o2, o3), DILATIONS))]
        l = [_unfold(ref, slabs[3 + i], d) for i, (ref, d) in enumerate(zip((l1, l2, l3), DILATIONS))]
        m = jnp.maximum(jnp.maximum(l[0], l[1]), l[2])
        e1, e2, e3 = jnp.exp(l[0] - m), jnp.exp(l[1] - m), jnp.exp(l[2] - m)
        se = e1 + e2 + e3
        ya = (e1 / se) * o[0] + (e2 / se) * o[1] + (e3 / se) * o[2]
        ya_ref[...] = ya
        lt_ref[...] = m + jnp.log(se)
        pa = _nt(ya.astype(BF16), wa_ref[...])
        pb = _nt(ob_ref[...].astype(BF16), wb_ref[...])
        merged = (jax.nn.sigmoid(zg_ref[:, :D_MODEL].astype(F32)) * pa
                  + jax.nn.sigmoid(zg_ref[:, D_MODEL:].astype(F32)) * pb)
        h2_ref[...] = h_ref[...] + _nn(merged.astype(BF16), wo_ref[...])

    row = lambda w: pl.BlockSpec((tm, w), lambda i: (i, 0))
    folded = [pl.BlockSpec((tm // d, d * 256), lambda i: (i, 0)) for d in DILATIONS]
    full = _resident
    return pl.pallas_call(
        body, name=name, grid=(t // tm,),
        in_specs=folded + folded + [row(512), row(GATE_W), row(D_MODEL), full(wa_t), full(wb_t), full(wout)],
        out_specs=[row(D_MODEL), row(256), row(256)],
        out_shape=[jax.ShapeDtypeStruct((t, D_MODEL), F32), jax.ShapeDtypeStruct((t, 256), F32),
                   jax.ShapeDtypeStruct((t, 256), F32)],
        scratch_shapes=[pltpu.VMEM((2, tm, 128), F32)] * 6,
        compiler_params=_params("parallel"),
    )(*o_a, *lse_a, o_b, zg, h, wa_t, wb_t, wout)


STAT_LANES = 32


def _head_stats(lt, dy, y):
    rows, w = dy.shape
    lane = lax.broadcasted_iota(jnp.int32, (rows, w), 1)
    out_lane = lax.broadcasted_iota(jnp.int32, (rows, w // 2), 1)
    prod = dy * y
    out = jnp.zeros((rows, w // 2), F32)
    for hd in range(w // HEAD_DIM):
        mine = lane // HEAD_DIM == hd
        dot = jnp.sum(jnp.where(mine, prod, 0.0), axis=-1, keepdims=True)
        both = jnp.where(out_lane % STAT_LANES < STAT_LANES // 2, _row_value(jnp.where(mine, lt, NEG)), dot)
        out = jnp.where(out_lane // STAT_LANES == hd, both, out)
    return out


def _read_stats(st, part):
    lane = lax.broadcasted_iota(jnp.int32, st.shape, 1)
    half = (lane % STAT_LANES < STAT_LANES // 2) == (part == 0)
    return jnp.concatenate([_row_value(jnp.where((lane // STAT_LANES == hd) & half, st, NEG))
                            for hd in range(HEADS_PER_BLOCK)], axis=0)


def _merge_bwd(dh, ya, lse_tot, o_b, lse_b, zg, wa_t, wb_t, wout, name, hosted=()):
    t = dh.shape[0]
    tm = 256

    def body(dh_ref, ya_ref, lt_ref, ob_ref, lb_ref, zg_ref, wa_ref, wb_ref, wo_ref,
             mg_ref, dpa_ref, dpb_ref, dzg_ref, dy1, dy2, dy3, st1, st2, st3, dyb_ref, stb_ref, *slabs):
        dm = _nt(dh_ref[...].astype(BF16), wo_ref[...])
        pa = _nt(ya_ref[...].astype(BF16), wa_ref[...])
        pb = _nt(ob_ref[...].astype(BF16), wb_ref[...])
        sa = jax.nn.sigmoid(zg_ref[:, :D_MODEL].astype(F32))
        sb = jax.nn.sigmoid(zg_ref[:, D_MODEL:].astype(F32))
        mg_ref[...] = (sa * pa + sb * pb).astype(BF16)
        dpa = (dm * sa).astype(BF16)
        dpb = (dm * sb).astype(BF16)
        dpa_ref[...] = dpa
        dpb_ref[...] = dpb
        dzg_ref[:, :D_MODEL] = (dm * pa * (sa * (1.0 - sa))).astype(BF16)
        dzg_ref[:, D_MODEL:] = (dm * pb * (sb * (1.0 - sb))).astype(BF16)
        dya = _nn(dpa, wa_ref[...])
        dyb = _nn(dpb, wb_ref[...])
        stats = _head_stats(lt_ref[...], dya, ya_ref[...])
        for i, d in enumerate(DILATIONS):
            _fold(dya, slabs[i], (dy1, dy2, dy3)[i], d)
            _fold(stats, slabs[3 + i], (st1, st2, st3)[i], d)
        dyb_ref[...] = dyb.astype(BF16)
        stb_ref[...] = _head_stats(lb_ref[...], dyb, ob_ref[...])

    row = lambda w: pl.BlockSpec((tm, w), lambda i: (i, 0))
    folded = [pl.BlockSpec((tm // d, d * 256), lambda i: (i, 0)) for d in DILATIONS]
    full = _resident
    sds = jax.ShapeDtypeStruct
    dy_shape = [sds((t // d, d * 256), BF16) for d in DILATIONS]
    st_shape = [sds((t // d, d * 128), F32) for d in DILATIONS]
    st_specs = [pl.BlockSpec((tm // d, d * 128), lambda i: (i, 0)) for d in DILATIONS]
    return _call(
        body, name=name, grid=(t // tm,),
        in_specs=[row(D_MODEL), row(256), row(256), row(512), row(512), row(GATE_W), full(wa_t), full(wb_t),
                  full(wout)],
        out_specs=[row(D_MODEL), row(D_MODEL), row(D_MODEL), row(GATE_W)] + folded + st_specs + [row(512), row(256)],
        out_shape=[sds((t, D_MODEL), BF16), sds((t, D_MODEL), BF16), sds((t, D_MODEL), BF16),
                   sds((t, GATE_W), BF16)] + dy_shape + st_shape + [sds((t, 512), BF16), sds((t, 256), F32)],
        scratch=[pltpu.VMEM((2, tm, 128), F32)] * 6,
        sem=("parallel",), args=(dh, ya, lse_tot, o_b, lse_b, zg, wa_t, wb_t, wout), hosted=hosted)


def _pair_sum(grad, got, name):
    _, _, r, cdim = grad.shape
    core = lax.axis_index("c").astype(jnp.int32).reshape(1)

    def body(core_ref, g_ref, s_ref, o_ref):
        o_ref[...] = (g_ref[...] + s_ref[...]).astype(BF16)

    return pl.pallas_call(
        body, name=name,
        grid_spec=pltpu.PrefetchScalarGridSpec(
            num_scalar_prefetch=1, grid=(N_CHIP,),
            in_specs=[pl.BlockSpec((None, None, r, cdim), lambda q, core_ref: (q, core_ref[0], 0, 0)),
                      pl.BlockSpec((None, None, r, cdim), lambda q, core_ref: (q, 0, 0, 0))],
            out_specs=pl.BlockSpec((None, r, cdim), lambda q, core_ref: (q, 0, 0))),
        out_shape=jax.ShapeDtypeStruct((N_CHIP, r, cdim), BF16),
        compiler_params=_params("parallel"),
    )(core, grad, got)


def _adamw_math(w, g, m, v):
    m = ADAM_B1 * m + (1.0 - ADAM_B1) * g
    v = ADAM_B2 * v + (1.0 - ADAM_B2) * jnp.square(g)
    m_hat = m / (1.0 - ADAM_B1 ** ADAM_STEP)
    v_hat = v / (1.0 - ADAM_B2 ** ADAM_STEP)
    delta = -ADAM_LR * (m_hat / (jnp.sqrt(v_hat) + ADAM_EPS) + ADAM_WD * w)
    return delta, m, v


def _update(w, parts, m, v, transposed, name):
    r, c = parts.shape[1:]

    def body(w_ref, p_ref, m_ref, v_ref, g_ref, d_ref, m2_ref, v2_ref):
        def total(rows):
            return ((p_ref[0, rows].astype(F32) + p_ref[1, rows].astype(F32)) + p_ref[2, rows].astype(F32)) \
                + p_ref[3, rows].astype(F32)

        def update(at, g):
            g_ref[at] = g
            d_ref[at], m2_ref[at], v2_ref[at] = _adamw_math(w_ref[at], g, m_ref[at], v_ref[at])

        if not transposed:
            update((slice(None), slice(None)), total(slice(None)))
            return
        for r0 in range(0, r, 128):
            n = min(128, r - r0)
            gt = total(slice(r0, r0 + n))
            if n < 128:
                gt = jnp.concatenate([gt, jnp.zeros((128 - n, c), F32)], axis=0)
            update((slice(None), slice(r0, r0 + n)), gt.T[:, :n])

    sds = jax.ShapeDtypeStruct(w.shape, F32)
    return pl.pallas_call(body, name=name, out_shape=[sds] * 4,
                          compiler_params=pltpu.CompilerParams(vmem_limit_bytes=VMEM_LIMIT))(w, parts, m, v)


SMALL_ROWS = 80


def _small_update(g, w, m, v, name):
    def body(g_ref, w_ref, m_ref, v_ref, gs_ref, d_ref, m2_ref, v2_ref, got_ref, send_sems, recv_sems):
        x, y, c = _place()
        me = 4 * x + 2 * y + c
        got_ref[me] = g_ref[...]
        copies = []
        for k in range(1, N_DEV):
            peer = (x ^ (k >> 2), y ^ ((k >> 1) & 1), c ^ (k & 1))
            cp = pltpu.make_async_remote_copy(
                src_ref=g_ref, dst_ref=got_ref.at[me], send_sem=send_sems.at[k - 1], recv_sem=recv_sems.at[k - 1],
                device_id=peer, device_id_type=MESH)
            cp.start()
            copies.append(cp)
        for cp in copies:
            cp.wait()
        total = got_ref[0]
        for k in range(1, N_DEV):
            total = total + got_ref[k]
        gs_ref[...] = total
        d_ref[...], m2_ref[...], v2_ref[...] = _adamw_math(w_ref[...], total, m_ref[...], v_ref[...])

    sds = jax.ShapeDtypeStruct((SMALL_ROWS, 128), F32)
    vm = pl.BlockSpec(memory_space=pltpu.VMEM)
    return pl.pallas_call(
        body, name=name, in_specs=[vm] * 4, out_specs=[vm] * 4, out_shape=[sds] * 4,
        scratch_shapes=[pltpu.VMEM((N_DEV, SMALL_ROWS, 128), F32), pltpu.SemaphoreType.DMA((N_DEV - 1,)),
                        pltpu.SemaphoreType.DMA((N_DEV - 1,))],
    )(g, w, m, v)


def _pack_small(gains, b_in, rel_bias, sinks, last):
    rows = [a.reshape(8, 128) for a in gains] + [b_in.reshape(40, 128), rel_bias.reshape(5, 128),
                                                 jnp.pad(sinks.reshape(1, 8), ((0, 0), (0, 120))), last]
    rows.append(jnp.zeros((SMALL_ROWS - 79, 128), F32))
    return jnp.concatenate(rows, axis=0)


def _unpack_small(p, like):
    out = [p[8 * i:8 * i + 8].reshape(like[i].shape) for i in range(4)]
    out.append(p[32:72].reshape(like[4].shape))
    out.append(p[72:77].reshape(like[5].shape))
    out.append(p[77, :8].reshape(like[6].shape))
    return out


def kernel(x, ffn1_norm, ffn1_w_gate, ffn1_w_up, ffn1_w_down, mix_norm, w_in, b_in, w_branch_a, w_branch_b, w_out, sinks, rel_bias, ffn2_norm, ffn2_w_gate, ffn2_w_up, ffn2_w_down, final_norm, loss_target, m_ffn1_norm, m_ffn1_w_gate, m_ffn1_w_up, m_ffn1_w_down, m_mix_norm, m_w_in, m_b_in, m_w_branch_a, m_w_branch_b, m_w_out, m_sinks, m_rel_bias, m_ffn2_norm, m_ffn2_w_gate, m_ffn2_w_up, m_ffn2_w_down, m_final_norm, v_ffn1_norm, v_ffn1_w_gate, v_ffn1_w_up, v_ffn1_w_down, v_mix_norm, v_w_in, v_b_in, v_w_branch_a, v_w_branch_b, v_w_out, v_sinks, v_rel_bias, v_ffn2_norm, v_ffn2_w_gate, v_ffn2_w_up, v_ffn2_w_down, v_final_norm):
    bsz, seq, _ = x.shape
    t = bsz * seq
    xt = x.reshape(t, D_MODEL)
    target = loss_target.reshape(t, D_MODEL)

    big = [("ffn1_w_gate", ffn1_w_gate, m_ffn1_w_gate, v_ffn1_w_gate, True),
           ("ffn1_w_up", ffn1_w_up, m_ffn1_w_up, v_ffn1_w_up, True),
           ("ffn1_w_down", ffn1_w_down, m_ffn1_w_down, v_ffn1_w_down, False),
           ("w_in", w_in, m_w_in, v_w_in, True),
           ("w_branch_a", w_branch_a, m_w_branch_a, v_w_branch_a, True),
           ("w_branch_b", w_branch_b, m_w_branch_b, v_w_branch_b, True),
           ("w_out", w_out, m_w_out, v_w_out, False),
           ("ffn2_w_gate", ffn2_w_gate, m_ffn2_w_gate, v_ffn2_w_gate, True),
           ("ffn2_w_up", ffn2_w_up, m_ffn2_w_up, v_ffn2_w_up, True),
           ("ffn2_w_down", ffn2_w_down, m_ffn2_w_down, v_ffn2_w_down, False)]
    shard = {nm: (w[0].T if tr else w[0]).astype(BF16) for nm, w, _, _, tr in big}
    full = {}

    def gather(names):
        return _Gather([shard[nm] for nm in names])

    def keep(names, got):
        for nm, gw in zip(names, got):
            full[nm] = gw.reshape(-1, gw.shape[-1])

    ffn1_names = ["ffn1_w_gate", "ffn1_w_up", "ffn1_w_down"]
    mix_names = ["w_in", "w_branch_a", "w_branch_b", "w_out"]
    ffn2_names = ["ffn2_w_gate", "ffn2_w_up", "ffn2_w_down"]
    g1, gm, g2, gf = ffn1_norm, mix_norm, ffn2_norm, final_norm.reshape(1, D_MODEL)

    buckets = _bucket_tiles()
    (bias,), (got,) = _bias_build(rel_bias, buckets, hosted=[gather(ffn1_names)])
    keep(ffn1_names, got)
    (h1, n1, a1, b1, hff1), (got,) = _ffn_fwd(xt, g1, full["ffn1_w_gate"], full["ffn1_w_up"], full["ffn1_w_down"], "ffn1_fwd",
                                hosted=[gather(mix_names)])
    keep(mix_names, got)
    (u, zq, zg, *zdil), (got,) = _inproj_fwd(h1, gm, full["w_in"], b_in, "inproj_fwd", hosted=[gather(ffn2_names[:2])])
    keep(ffn2_names[:2], got)
    sink_rows = jnp.broadcast_to(sinks.reshape(8, 1, 1), (8, BLOCK, 128)).reshape(8 * BLOCK, 128)
    cfs = [_Att(i) for i in range(4)]
    zfold = []
    for i, cf in enumerate(cfs):
        if cf.d == 1:
            zfold.append(zq.reshape(bsz, seq, QKV_W))
        else:
            zfold.append(zdil[i - 1].reshape(bsz, seq // cf.d, cf.d * cf.row_w))
    att = [None] * 4
    for i in (3, 0, 1, 2):
        cf = cfs[i]
        (o, lse), got = _attn_fwd(cf, zfold[i], bias, sink_rows, f"attn{i}_fwd",
                                  hosted=[gather(ffn2_names[2:])] if i == 3 else ())
        if i == 3:
            keep(ffn2_names[2:], got[0])
        att[i] = (o.reshape(t // cf.d, cf.d * cf.wq), lse.reshape(t // cf.d, cf.d * cf.wq))
    o_b, lse_b = att[3]
    h2, ya, lse_tot = _merge_fwd([a[0] for a in att[:3]], [a[1] for a in att[:3]], o_b, zg, h1,
                                 full["w_branch_a"], full["w_branch_b"], full["w_out"], "merge_fwd")
    (dh3, n2, a2, b2, hff2, loss_part, dgf), _ = _ffn_fwd(
        h2, g2, full["ffn2_w_gate"], full["ffn2_w_up"], full["ffn2_w_down"], "ffn2_fwd", head=(gf, target))

    grads, pair, from_chips = {}, {}, {}

    def by_owner(nm):
        return grads[nm].reshape(N_CHIP, 2, -1, grads[nm].shape[-1])

    def to_core(names):
        return _CoreExchange([by_owner(nm) for nm in names])

    def pair_up(names, got):
        for nm, sib in zip(names, got):
            pair[nm] = _pair_sum(by_owner(nm), sib, f"pair_sum_{nm}")

    def to_chips(names):
        return _ChipExchange([pair[nm] for nm in names])

    def landed(names, got):
        for nm, parts in zip(names, got):
            from_chips[nm] = parts

    dh2, dg2, da, db = _ffn_bwd(h2, a2, b2, g2, dh3, full["ffn2_w_gate"], full["ffn2_w_up"],
                                          full["ffn2_w_down"], "ffn2_bwd")
    grads["ffn2_w_gate"], _ = _tn_matmul(da, n2, 1408, "ffn2_dgate")
    grads["ffn2_w_up"], _ = _tn_matmul(db, n2, 1408, "ffn2_dup")
    grads["ffn2_w_down"], _ = _tn_matmul(hff2, dh3, 1408, "ffn2_ddown", scale=0.5)
    (merged, dpa, dpb, dzg, *cot), (got,) = _merge_bwd(
        dh2, ya, lse_tot, o_b, lse_b, zg, full["w_branch_a"], full["w_branch_b"], full["w_out"], "merge_bwd",
        hosted=[to_core(ffn2_names)])
    dys, sts = cot[0:3] + [cot[6]], cot[3:6] + [cot[7]]
    pair_up(ffn2_names, got)
    dq, dk, dv, dbias, dsink = [None] * 4, [None] * 4, [None] * 4, [None] * 4, None
    for i in (3, 0, 1, 2):
        cf = cfs[i]
        shp = (bsz, seq // cf.d, cf.d * cf.wq)
        hosted = {3: lambda: [to_chips(ffn2_names[:2])], 0: lambda: [to_chips(ffn2_names[2:])]}.get(i, list)()
        res, got = _attn_bwd(cf, zfold[i], bias, sink_rows, sts[i].reshape(shp[:2] + (shp[2] // 2,)), dys[i].reshape(shp),
                             f"attn{i}_bwd", hosted=hosted)
        if i == 3:
            landed(ffn2_names[:2], got[0])
        elif i == 0:
            landed(ffn2_names[2:], got[0])
        dq[i] = res[0].reshape(t // cf.d, cf.d * cf.wq)
        dk[i] = res[1].reshape(t // cf.d, cf.d * cf.wkv)
        dv[i] = res[2].reshape(t // cf.d, cf.d * cf.wkv)
        dbias[i] = res[3]
        if cf.sinks:
            dsink = res[4]
    pieces = dq[:3] + dk[:3] + dv[:3] + [dq[3], dk[3], dv[3], dzg]
    dh1, dz, db_in, dgm = _inproj_bwd(pieces, h1, gm, dh2, full["w_in"], "inproj_bwd")
    dx, dg1, da, db = _ffn_bwd(xt, a1, b1, g1, dh1, full["ffn1_w_gate"], full["ffn1_w_up"],
                                         full["ffn1_w_down"], "ffn1_bwd")
    grads["w_in"], _ = _tn_matmul(dz, u, 1280, "dw_in")
    grads["ffn1_w_down"], (got,) = _tn_matmul(hff1, dh1, 1408, "ffn1_ddown", hosted=[to_core(["w_in"])], scale=0.5)
    pair_up(["w_in"], got)
    grads["ffn1_w_gate"], got = _tn_matmul(da, n1, 1408, "ffn1_dgate",
                                           hosted=[to_chips(["w_in"]), to_core(["ffn1_w_down"])])
    landed(["w_in"], got[0])
    pair_up(["ffn1_w_down"], got[1])
    grads["ffn1_w_up"], got = _tn_matmul(db, n1, 1408, "ffn1_dup",
                                         hosted=[to_chips(["ffn1_w_down"]), to_core(["ffn1_w_gate"])])
    landed(["ffn1_w_down"], got[0])
    pair_up(["ffn1_w_gate"], got[1])
    grads["w_out"], got = _tn_matmul(merged, dh2, 1024, "dw_out",
                                     hosted=[to_chips(["ffn1_w_gate"]), to_core(["ffn1_w_up"])])
    landed(["ffn1_w_gate"], got[0])
    pair_up(["ffn1_w_up"], got[1])
    grads["w_branch_b"], got = _tn_matmul(dpb, o_b, 1024, "dw_branch_b",
                                          hosted=[to_chips(["ffn1_w_up"]), to_core(["w_out"])])
    landed(["ffn1_w_up"], got[0])
    pair_up(["w_out"], got[1])
    grads["w_branch_a"], got = _tn_matmul(dpa, ya, 1024, "dw_branch_a",
                                          hosted=[to_chips(["w_out"]), to_core(["w_branch_b"])])
    landed(["w_out"], got[0])
    pair_up(["w_branch_b"], got[1])
    got = _exchange([to_chips(["w_branch_b"]), to_core(["w_branch_a"])], "reduce_scatter_tail1")
    landed(["w_branch_b"], got[0])
    pair_up(["w_branch_a"], got[1])
    landed(["w_branch_a"], _exchange([to_chips(["w_branch_a"])], "reduce_scatter_tail2")[0])
    dtable, dsinks = _bias_reduce(jnp.concatenate(dbias, axis=0), buckets, dsink)

    out_g, out_d, out_m, out_v = {}, {}, {}, {}
    for nm, w, m, v, tr in big:
        around = tr and w.shape[-1] % 128 != 0
        wmv = [a[0].T if around else a[0] for a in (w, m, v)]
        res = _update(wmv[0], from_chips[nm], wmv[1], wmv[2], tr and not around, f"update_{nm}")
        out_g[nm], out_d[nm], out_m[nm], out_v[nm] = [(a.T if around else a)[None] for a in res]

    small = [("ffn1_norm", ffn1_norm, m_ffn1_norm, v_ffn1_norm), ("mix_norm", mix_norm, m_mix_norm, v_mix_norm),
             ("ffn2_norm", ffn2_norm, m_ffn2_norm, v_ffn2_norm), ("final_norm", final_norm, m_final_norm, v_final_norm),
             ("b_in", b_in, m_b_in, v_b_in), ("rel_bias", rel_bias, m_rel_bias, v_rel_bias),
             ("sinks", sinks, m_sinks, v_sinks)]
    zero_row = jnp.zeros((1, 128), F32)
    pack = lambda arrs, last: _pack_small(arrs[:4], arrs[4], arrs[5], arrs[6], last)
    g_small = pack([dg1, dgm, dg2, dgf, db_in, dtable[:, :TOTAL_HEADS], dsinks[:, 0]], loss_part)
    packed = [pack([s[k] for s in small], zero_row) for k in (1, 2, 3)]
    gs, ds, ms, vs = _small_update(g_small, *packed, "small_update")
    like = [s[1] for s in small]
    for nm_s, g_, d_, m_, v_ in zip([s[0] for s in small], _unpack_small(gs, like), _unpack_small(ds, like),
                                    _unpack_small(ms, like), _unpack_small(vs, like)):
        out_g[nm_s], out_d[nm_s], out_m[nm_s], out_v[nm_s] = g_, d_, m_, v_
    loss = gs[78, 0]

    order = ["ffn1_norm", "ffn1_w_gate", "ffn1_w_up", "ffn1_w_down", "mix_norm", "w_in", "b_in", "w_branch_a",
             "w_branch_b", "w_out", "sinks", "rel_bias", "ffn2_norm", "ffn2_w_gate", "ffn2_w_up", "ffn2_w_down",
             "final_norm"]
    return (loss, dx.reshape(x.shape), *[out_g[k] for k in order], *[out_d[k] for k in order],
            *[out_m[k] for k in order], *[out_v[k] for k in order])
```

```python
import functools
import math

import numpy as np
import jax
import jax.numpy as jnp
from jax import lax
from jax.experimental import pallas as pl
from jax.experimental.pallas import tpu as pltpu

D_MODEL = 1024
D_FF = 2816
FF_CHUNK = 256
HEAD_DIM = 64
BLOCK = 128
N_BUCKETS = 32
MAX_DISTANCE = 2048
A_HEADS = 12
TOTAL_HEADS = 20
DIL_GROUPS = ((128, 1), (512, 4), (2048, 16))
B_WINDOW = 128
QKV_W = 3072
GATE_W = 2048
D_IN = QKV_W + GATE_W
EPS = 1e-6
NEG = -1e30
N_DEV = 8
N_CHIP = 4
ADAM_LR, ADAM_B1, ADAM_B2, ADAM_EPS, ADAM_WD, ADAM_STEP = 0.001, 0.9, 0.999, 1e-08, 0.01, 10
VMEM_LIMIT = 56 * 1024 * 1024
MESH = pl.DeviceIdType.MESH
BF16 = jnp.bfloat16
F32 = jnp.float32
ANY = pl.BlockSpec(memory_space=pl.ANY)


def _params(*sem):
    return pltpu.CompilerParams(dimension_semantics=sem, vmem_limit_bytes=VMEM_LIMIT)


def _resident(a):
    return pl.BlockSpec(a.shape, lambda i: (0, 0), pipeline_mode=pl.Buffered(1))


def _place():
    return lax.axis_index("x"), lax.axis_index("y"), lax.axis_index("c")


class _Gather:
    def __init__(self, shards):
        self.ins = list(shards)
        n = self.n = len(shards)
        self.out_shape = [jax.ShapeDtypeStruct((N_DEV,) + s.shape, s.dtype) for s in shards]
        self.scratch = [pltpu.SemaphoreType.DMA((7 * n,)), pltpu.SemaphoreType.DMA((7 * n,)),
                        pltpu.SemaphoreType.DMA((n,))]

    def _copies(self, ins, outs, sems):
        send_sems, recv_sems, local_sems = sems
        x, y, c = _place()
        me, sibling = (x, y, c), (x, y, 1 - c)
        chips = [(1 - x, y), (x, 1 - y), (1 - x, 1 - y)]

        def copy(i, k, block, to, src=None):
            dst = outs[i].at[4 * block[0] + 2 * block[1] + block[2]]
            return pltpu.make_async_remote_copy(
                src_ref=dst if src is None else src, dst_ref=dst, send_sem=send_sems.at[7 * i + k],
                recv_sem=recv_sems.at[7 * i + k], device_id=to, device_id_type=MESH)

        n = self.n
        south = c == 0
        relayed = (jnp.where(south, 1 - x, x), jnp.where(south, y, 1 - y), c)
        relay_to = (jnp.where(south, x, 1 - x), jnp.where(south, 1 - y, y), c)
        mine = [pltpu.make_async_copy(ins[i], outs[i].at[4 * x + 2 * y + c], local_sems.at[i]) for i in range(n)]
        first = [copy(i, 0, me, sibling, src=ins[i]) for i in range(n)]
        first += [copy(i, 1 + j, me, (*chips[j], c), src=ins[i]) for i in range(n) for j in range(2)]
        landed = [copy(i, 1 + j, (*chips[j], c), me) for j in range(2) for i in range(n)]
        passed = [copy(i, 4 + j, (*chips[j], c), sibling) for j in range(2) for i in range(n)]
        relays = [copy(i, 3, relayed, relay_to) for i in range(n)]
        diag_landed = [copy(i, 3, (*chips[2], c), me) for i in range(n)]
        diag_passed = [copy(i, 6, (*chips[2], c), sibling) for i in range(n)]
        from_sibling = [copy(i, 0, sibling, me) for i in range(n)]
        from_sibling += [copy(i, 4 + j, (*chip, 1 - c), me) for i in range(n) for j, chip in enumerate(chips)]
        return mine, first, landed, passed, relays, diag_landed, diag_passed, from_sibling

    def start(self, ins, outs, sems):
        mine, first = self._copies(ins, outs, sems)[:2]
        for cp in mine + first:
            cp.start()

    def mid(self, ins, outs, sems):
        _, _, landed, passed, relays, _, _, _ = self._copies(ins, outs, sems)
        for got in landed:
            got.wait_recv()
        for cp in relays + passed:
            cp.start()

    def end(self, ins, outs, sems):
        mine, first, _, passed, relays, diag_landed, diag_passed, from_sibling = self._copies(ins, outs, sems)
        for got, fwd in zip(diag_landed, diag_passed):
            got.wait_recv()
            fwd.start()
        for cp in from_sibling:
            cp.wait_recv()
        for cp in first + passed + relays + diag_passed:
            cp.wait_send()
        for cp in mine:
            cp.wait()


class _CoreExchange:
    def __init__(self, grads):
        self.ins = list(grads)
        n = self.n = len(grads)
        self.out_shape = [jax.ShapeDtypeStruct((N_CHIP, 1) + g.shape[2:], g.dtype) for g in grads]
        self.scratch = [pltpu.SemaphoreType.DMA((n,)), pltpu.SemaphoreType.DMA((n,))]

    def _copies(self, ins, outs, sems):
        x, y, c = _place()
        return [pltpu.make_async_remote_copy(
            src_ref=ins[i].at[:, pl.ds(1 - c, 1)], dst_ref=outs[i], send_sem=sems[0].at[i],
            recv_sem=sems[1].at[i], device_id=(x, y, 1 - c), device_id_type=MESH) for i in range(self.n)]

    def start(self, ins, outs, sems):
        for cp in self._copies(ins, outs, sems):
            cp.start()

    mid = None

    def end(self, ins, outs, sems):
        for cp in self._copies(ins, outs, sems):
            cp.wait()


class _ChipExchange:
    def __init__(self, parts):
        self.ins = list(parts)
        n = self.n = len(parts)
        self.out_shape = [jax.ShapeDtypeStruct(p.shape, p.dtype) for p in parts]
        self.scratch = [pltpu.SemaphoreType.DMA((3 * n,)), pltpu.SemaphoreType.DMA((3 * n,)),
                        pltpu.SemaphoreType.DMA((n,))]

    def _copies(self, ins, outs, sems):
        send_sems, recv_sems, local_sems = sems
        x, y, c = _place()
        my_chip = 2 * x + y
        copies = []
        for i in range(self.n):
            copies.append(pltpu.make_async_copy(ins[i].at[my_chip], outs[i].at[my_chip], local_sems.at[i]))
            for k, (qx, qy) in enumerate([(1 - x, y), (x, 1 - y), (1 - x, 1 - y)]):
                copies.append(pltpu.make_async_remote_copy(
                    src_ref=ins[i].at[2 * qx + qy], dst_ref=outs[i].at[my_chip], send_sem=send_sems.at[3 * i + k],
                    recv_sem=recv_sems.at[3 * i + k], device_id=(qx, qy, c), device_id_type=MESH))
        return copies

    def start(self, ins, outs, sems):
        for cp in self._copies(ins, outs, sems):
            cp.start()

    mid = None

    def end(self, ins, outs, sems):
        for cp in self._copies(ins, outs, sems):
            cp.wait()


def _call(body, *, name, grid, in_specs, out_specs, out_shape, args, scratch=(), sem=None, hosted=()):
    n_in, n_out, n_scr = len(in_specs), len(out_specs), len(scratch)
    x_in = [len(p.ins) for p in hosted]
    x_scr = [len(p.scratch) for p in hosted]
    steps = int(np.prod(grid))

    def wrapped(*refs):
        refs = list(refs)
        ins, refs = refs[:n_in], refs[n_in:]
        x_ins = [[refs.pop(0) for _ in range(k)] for k in x_in]
        outs, refs = refs[:n_out], refs[n_out:]
        x_outs = [[refs.pop(0) for _ in range(k)] for k in x_in]
        scr, refs = refs[:n_scr], refs[n_scr:]
        x_sems = [[refs.pop(0) for _ in range(k)] for k in x_scr]
        step = 0
        for d in range(len(grid)):
            step = step * grid[d] + pl.program_id(d)

        def phase(which, at):
            fns = [(getattr(p, which), a) for p, a in zip(hosted, zip(x_ins, x_outs, x_sems)) if getattr(p, which)]
            if fns:
                @pl.when(step == at)
                def _():
                    for fn, a in fns:
                        fn(*a)

        phase("start", 0)
        if steps > 1:
            phase("mid", steps // 2)
        body(*ins, *outs, *scr)
        if steps == 1:
            phase("mid", 0)
        phase("end", steps - 1)

    results = pl.pallas_call(
        wrapped, name=name, grid=grid,
        in_specs=list(in_specs) + [ANY] * sum(x_in), out_specs=list(out_specs) + [ANY] * sum(x_in),
        out_shape=list(out_shape) + [s for p in hosted for s in p.out_shape],
        scratch_shapes=list(scratch) + [s for p in hosted for s in p.scratch],
        compiler_params=_params(*(("arbitrary",) * len(grid) if hosted else sem)),
    )(*args, *[a for p in hosted for a in p.ins])
    own, rest = list(results[:n_out]), list(results[n_out:])
    return own, [[rest.pop(0) for _ in range(k)] for k in x_in]


def _exchange(programs, name):
    x_in = [len(p.ins) for p in programs]
    x_scr = [len(p.scratch) for p in programs]

    def body(*refs):
        refs = list(refs)
        x_ins = [[refs.pop(0) for _ in range(k)] for k in x_in]
        x_outs = [[refs.pop(0) for _ in range(k)] for k in x_in]
        x_sems = [[refs.pop(0) for _ in range(k)] for k in x_scr]
        for which in ("start", "mid", "end"):
            for p, a in zip(programs, zip(x_ins, x_outs, x_sems)):
                if getattr(p, which):
                    getattr(p, which)(*a)

    results = list(pl.pallas_call(
        body, name=name, in_specs=[ANY] * sum(x_in), out_specs=[ANY] * sum(x_in),
        out_shape=[s for p in programs for s in p.out_shape],
        scratch_shapes=[s for p in programs for s in p.scratch],
    )(*[a for p in programs for a in p.ins]))
    return [[results.pop(0) for _ in range(k)] for k in x_in]


def _nt(a, b):
    return lax.dot_general(a, b, (((1,), (1,)), ((), ())), preferred_element_type=F32)


def _nn(a, b):
    return lax.dot_general(a, b, (((1,), (0,)), ((), ())), preferred_element_type=F32)


def _tn(a, b):
    return lax.dot_general(a, b, (((0,), (0,)), ((), ())), preferred_element_type=F32)


def _rms(x, g):
    r = lax.rsqrt(jnp.mean(x * x, axis=-1, keepdims=True) + EPS)
    return x * r, r


def _rms_bwd(dn, xhat, r, g):
    dg = jnp.sum(dn * xhat, axis=0, keepdims=True)
    dxh = dn * g
    dx = r * (dxh - xhat * jnp.mean(dxh * xhat, axis=-1, keepdims=True))
    return dx, dg


def _ffn_fwd(x, g, wg_t, wu_t, wd, name, hosted=(), head=None):
    t = x.shape[0]
    tm = 256

    def body(*refs):
        x_ref, g_ref, wg_ref, wu_ref, wd_ref = refs[:5]
        h_ref, n_ref, a_ref, b_ref, hff_ref = refs[-5 if head is None else -7:][:5]
        xhat, _ = _rms(x_ref[...], g_ref[...])
        n = (xhat * g_ref[...]).astype(BF16)
        n_ref[...] = n
        for c in range(0, D_FF, FF_CHUNK):
            cols = slice(c, c + FF_CHUNK)
            a = _nt(n, wg_ref[cols, :])
            b = _nt(n, wu_ref[cols, :])
            a_ref[:, cols] = a.astype(BF16)
            b_ref[:, cols] = b.astype(BF16)
            hff_ref[:, cols] = (a * jax.nn.sigmoid(a) * b).astype(BF16)
        h = x_ref[...] + 0.5 * _nn(hff_ref[...], wd_ref[...])
        if head is None:
            h_ref[...] = h
            return
        gf_ref, t_ref, loss_ref, dgf_ref = refs[5], refs[6], refs[-2], refs[-1]

        @pl.when(pl.program_id(0) == 0)
        def _():
            loss_ref[...] = jnp.zeros_like(loss_ref)
            dgf_ref[...] = jnp.zeros_like(dgf_ref)

        yhat, r = _rms(h, gf_ref[...])
        err = yhat * gf_ref[...] - t_ref[...]
        loss_ref[...] += 0.5 * jnp.sum(jnp.mean(err * err, axis=-1, keepdims=True), axis=0, keepdims=True)
        h_ref[...], dgf = _rms_bwd(err * (1.0 / D_MODEL), yhat, r, gf_ref[...])
        dgf_ref[...] += dgf

    row = pl.BlockSpec((tm, D_MODEL), lambda i: (i, 0))
    hid = pl.BlockSpec((tm, D_FF), lambda i: (i, 0))
    sds = jax.ShapeDtypeStruct
    in_specs = [row, _resident(g), _resident(wg_t), _resident(wu_t), _resident(wd)]
    out_specs = [row, row, hid, hid, hid]
    out_shape = [sds((t, D_MODEL), F32), sds((t, D_MODEL), BF16), sds((t, D_FF), BF16), sds((t, D_FF), BF16),
                 sds((t, D_FF), BF16)]
    args = (x, g, wg_t, wu_t, wd)
    if head is not None:
        in_specs += [_resident(head[0]), row]
        out_specs += [pl.BlockSpec((1, 128), lambda i: (0, 0)), pl.BlockSpec((1, D_MODEL), lambda i: (0, 0))]
        out_shape += [sds((1, 128), F32), sds((1, D_MODEL), F32)]
        args += tuple(head)
    return _call(body, name=name, grid=(t // tm,), in_specs=in_specs, out_specs=out_specs, out_shape=out_shape,
                 sem=("parallel",) if head is None else ("arbitrary",), args=args, hosted=hosted)


def _ffn_bwd(x, a_pre, b_pre, g, dh, wg_t, wu_t, wd, name):
    t = x.shape[0]
    tm = 256

    def body(x_ref, a_ref, b_ref, g_ref, dh_ref, wg_ref, wu_ref, wd_ref,
             dx_ref, dg_ref, da_ref, db_ref):
        @pl.when(pl.program_id(0) == 0)
        def _():
            dg_ref[...] = jnp.zeros_like(dg_ref)

        dhh = (0.5 * dh_ref[...]).astype(BF16)
        for c in range(0, D_FF, FF_CHUNK):
            cols = slice(c, c + FF_CHUNK)
            a = a_ref[:, cols].astype(F32)
            b = b_ref[:, cols].astype(F32)
            s = jax.nn.sigmoid(a)
            silu = a * s
            dhff = _nt(dhh, wd_ref[cols, :])
            da_ref[:, cols] = (dhff * b * (s * (1.0 + a * (1.0 - s)))).astype(BF16)
            db_ref[:, cols] = (dhff * silu).astype(BF16)
        dn = _nn(da_ref[...], wg_ref[...]) + _nn(db_ref[...], wu_ref[...])
        xhat, r = _rms(x_ref[...], g_ref[...])
        dx, dg = _rms_bwd(dn, xhat, r, g_ref[...])
        dx_ref[...] = dh_ref[...] + dx
        dg_ref[...] += dg

    row = pl.BlockSpec((tm, D_MODEL), lambda i: (i, 0))
    hid = pl.BlockSpec((tm, D_FF), lambda i: (i, 0))
    return pl.pallas_call(
        body, name=name, grid=(t // tm,),
        in_specs=[row, hid, hid, _resident(g), row, _resident(wg_t), _resident(wu_t), _resident(wd)],
        out_specs=[row, pl.BlockSpec((1, D_MODEL), lambda i: (0, 0)), hid, hid],
        out_shape=[jax.ShapeDtypeStruct((t, D_MODEL), F32), jax.ShapeDtypeStruct((1, D_MODEL), F32),
                   jax.ShapeDtypeStruct((t, D_FF), BF16), jax.ShapeDtypeStruct((t, D_FF), BF16)],
        compiler_params=_params("arbitrary"),
    )(x, a_pre, b_pre, g, dh, wg_t, wu_t, wd)


def _tn_matmul(a, b, rc, name, hosted=(), scale=None):
    t, r = a.shape
    c = b.shape[1]
    tk = min(t, 2048)

    def body(a_ref, b_ref, o_ref):
        @pl.when(pl.program_id(1) == 0)
        def _():
            o_ref[...] = jnp.zeros_like(o_ref)

        o_ref[...] += _tn(a_ref[...].astype(BF16), b_ref[...].astype(BF16))
        if scale is not None:
            @pl.when(pl.program_id(1) == t // tk - 1)
            def _():
                o_ref[...] *= scale

    (out,), got = _call(
        body, name=name, grid=(r // rc, t // tk),
        in_specs=[pl.BlockSpec((tk, rc), lambda i, k: (k, i)), pl.BlockSpec((tk, c), lambda i, k: (k, 0))],
        out_specs=[pl.BlockSpec((rc, c), lambda i, k: (i, 0))],
        out_shape=[jax.ShapeDtypeStruct((r, c), F32)],
        sem=("parallel", "arbitrary"), args=(a, b), hosted=hosted)
    return out, got


def _unfold(blk_ref, slab_ref, d):
    if d == 1:
        return blk_ref[...]
    n = blk_ref.shape[0]
    for r in range(d):
        for half in range(2):
            c0 = 256 * r + 128 * half
            slab_ref[half, pl.ds(r, n, stride=d), :] = blk_ref[:, c0:c0 + 128]
    return jnp.concatenate([slab_ref[0], slab_ref[1]], axis=1)


def _fold(x, slab_ref, out_ref, d):
    if d == 1:
        out_ref[...] = x.astype(out_ref.dtype)
        return
    n, w = out_ref.shape[0], x.shape[1]
    for part in range(w // 128):
        slab_ref[part] = x[:, 128 * part:128 * (part + 1)]
    for r in range(d):
        for part in range(w // 128):
            c0 = w * r + 128 * part
            out_ref[:, c0:c0 + 128] = slab_ref[part, pl.ds(r, n, stride=d), :].astype(out_ref.dtype)


DILATIONS = tuple(d for _, d in DIL_GROUPS)


PIECE_W = (256,) * 9 + (512, 128, 128, GATE_W)
PIECE_D = DILATIONS * 3 + (1, 1, 1, 1)


def _inproj_fwd(h, g, w_t, b_in, name, hosted=()):
    t = h.shape[0]
    tm, nc = 512, 256
    dilated = [(gi, d) for gi, d in enumerate(DILATIONS) if d > 1]

    def body(h_ref, g_ref, w_ref, b_ref, u_ref, zq_ref, zg_ref, *rest):
        zf_refs, slabs = rest[:len(dilated)], rest[len(dilated):]
        xhat, _ = _rms(h_ref[...], g_ref[...])
        u = (xhat * g_ref[...]).astype(BF16)
        u_ref[...] = u
        for c in range(D_IN // nc):
            z = _nt(u, w_ref[c * nc:(c + 1) * nc, :]) + b_ref[:, c * nc:(c + 1) * nc]
            if c < QKV_W // nc:
                zq_ref[:, c * nc:(c + 1) * nc] = z.astype(BF16)
            else:
                zg_ref[:, c * nc - QKV_W:(c + 1) * nc - QKV_W] = z.astype(BF16)
            part, gi = divmod(c, len(DILATIONS))
            for k, (gk, d) in enumerate(dilated):
                if part < 3 and gi == gk:
                    slab, n = slabs[3 * k + part], tm // d
                    slab[0] = z[:, :128]
                    slab[1] = z[:, 128:]
                    for r in range(d):
                        for half in range(2):
                            c0 = 768 * r + 256 * part + 128 * half
                            zf_refs[k][:, c0:c0 + 128] = slab[half, pl.ds(r, n, stride=d), :].astype(BF16)

    row = lambda w: pl.BlockSpec((tm, w), lambda i: (i, 0))
    full = _resident
    sds = jax.ShapeDtypeStruct
    return _call(
        body, name=name, grid=(t // tm,),
        in_specs=[row(D_MODEL), full(g), full(w_t), full(b_in)],
        out_specs=[row(D_MODEL), row(QKV_W), row(GATE_W)]
        + [pl.BlockSpec((tm // d, d * 768), lambda i: (i, 0)) for _, d in dilated],
        out_shape=[sds((t, D_MODEL), BF16), sds((t, QKV_W), BF16), sds((t, GATE_W), BF16)]
        + [sds((t // d, d * 768), BF16) for _, d in dilated],
        scratch=[pltpu.VMEM((2, tm, 128), F32)] * (3 * len(dilated)),
        sem=("parallel",), args=(h, g, w_t, b_in), hosted=hosted)


def _inproj_bwd(pieces, h, g, dh_res, w_t, name):
    t = h.shape[0]
    tm = 256
    npiece = len(PIECE_W)
    offs = np.concatenate([[0], np.cumsum(PIECE_W)]).tolist()

    def body(*refs):
        p_refs = refs[:npiece]
        h_ref, g_ref, dhr_ref, w_ref, dh_ref, dz_ref, db_ref, dg_ref = refs[npiece:npiece + 8]
        slabs = list(refs[npiece + 8:])
        i = pl.program_id(0)

        @pl.when(i == 0)
        def _():
            db_ref[...] = jnp.zeros_like(db_ref)
            dg_ref[...] = jnp.zeros_like(dg_ref)

        du = jnp.zeros((tm, D_MODEL), F32)
        for k in range(npiece):
            o, w = offs[k], PIECE_W[k]
            token_order = _unfold(p_refs[k], slabs.pop(), PIECE_D[k]).astype(BF16) if PIECE_D[k] > 1 else None
            for c0 in range(0, w, 512):
                cw = min(512, w - c0)
                pz = p_refs[k][:, c0:c0 + cw] if token_order is None else token_order
                dz_ref[:, o + c0:o + c0 + cw] = pz
                db_ref[:, o + c0:o + c0 + cw] += jnp.sum(pz.astype(F32), axis=0, keepdims=True)
                du = du + _nn(pz, w_ref[o + c0:o + c0 + cw, :])
        xhat, r = _rms(h_ref[...], g_ref[...])
        dx, dg = _rms_bwd(du, xhat, r, g_ref[...])
        dh_ref[...] = dhr_ref[...] + dx
        dg_ref[...] += dg

    row = lambda w: pl.BlockSpec((tm, w), lambda i: (i, 0))
    full = lambda shp: pl.BlockSpec(shp, lambda i: (0, 0))
    return pl.pallas_call(
        body, name=name, grid=(t // tm,),
        in_specs=[pl.BlockSpec((tm // d, d * w), lambda i: (i, 0)) for w, d in zip(PIECE_W, PIECE_D)]
        + [row(D_MODEL), _resident(g), row(D_MODEL), _resident(w_t)],
        out_specs=[row(D_MODEL), row(D_IN), full((1, D_IN)), full((1, D_MODEL))],
        out_shape=[jax.ShapeDtypeStruct((t, D_MODEL), F32), jax.ShapeDtypeStruct((t, D_IN), BF16),
                   jax.ShapeDtypeStruct((1, D_IN), F32), jax.ShapeDtypeStruct((1, D_MODEL), F32)],
        scratch_shapes=[pltpu.VMEM((2, tm, 128), F32)] * sum(d > 1 for d in PIECE_D),
        compiler_params=_params("arbitrary"),
    )(*pieces, h, g, dh_res, w_t)


def _t5_bucket(dist):
    max_exact = N_BUCKETS // 2
    n = jnp.maximum(dist, 0)
    nf = jnp.maximum(n, 1).astype(jnp.float32)
    large = max_exact + (jnp.log(nf / max_exact) / math.log(MAX_DISTANCE / max_exact)
                         * (N_BUCKETS - max_exact)).astype(jnp.int32)
    large = jnp.minimum(large, N_BUCKETS - 1)
    return jnp.where(n < max_exact, n, large)


ATT_CFG = ((1, 128, 0, 4), (4, 128, 4, 4), (16, 128, 8, 4), (1, B_WINDOW - 1, A_HEADS, 8))


def _bucket_tiles():
    qi = jnp.arange(BLOCK)[:, None]
    ki = jnp.arange(2 * BLOCK)[None, :]
    dist = qi + BLOCK - ki
    return jnp.stack([_t5_bucket(dist * cfg[0]) for cfg in ATT_CFG]).astype(jnp.int32)


def _band(max_steps):
    row = lax.broadcasted_iota(jnp.int32, (BLOCK, 2 * BLOCK), 0)
    col = lax.broadcasted_iota(jnp.int32, (BLOCK, 2 * BLOCK), 1)
    dist = row + BLOCK - col
    return (dist >= 0) & (dist <= max_steps)


def _bias_build(table, buckets, hosted=()):
    def body(tab_ref, bt_ref, out_ref):
        col = lax.broadcasted_iota(jnp.int32, (BLOCK, 2 * BLOCK), 1)
        for ci, (_, max_steps, h0, nh) in enumerate(ATT_CFG):
            bt = bt_ref[ci]
            band = _band(max_steps)
            for h in range(h0, h0 + nh):
                acc = lax.fori_loop(0, N_BUCKETS, lambda b, acc: jnp.where(bt == b, tab_ref[b, h], acc),
                                    jnp.zeros((BLOCK, 2 * BLOCK), F32))
                out_ref[0, BLOCK * h:BLOCK * (h + 1), :] = jnp.where(band & (col >= BLOCK), acc, NEG)
                out_ref[1, BLOCK * h:BLOCK * (h + 1), :] = jnp.where(band, acc, NEG)

    return _call(
        body, name="bias_build", grid=(1,),
        in_specs=[pl.BlockSpec(memory_space=pltpu.SMEM), pl.BlockSpec(memory_space=pltpu.VMEM)],
        out_specs=[pl.BlockSpec(memory_space=pltpu.VMEM)],
        out_shape=[jax.ShapeDtypeStruct((2, TOTAL_HEADS * BLOCK, 2 * BLOCK), F32)],
        sem=("arbitrary",), args=(table, buckets), hosted=hosted)


def _bias_reduce(dbias, buckets, dsink_rows, hosted=()):
    def body(db_ref, bt_ref, ds_ref, out_ref, sink_ref):
        ri = lax.broadcasted_iota(jnp.int32, (N_BUCKETS, 128), 0)
        ci = lax.broadcasted_iota(jnp.int32, (N_BUCKETS, 128), 1)

        def per_bucket(b, acc):
            for cfg_i, (_, _, h0, nh) in enumerate(ATT_CFG):
                hit = bt_ref[cfg_i] == b
                for h in range(h0, h0 + nh):
                    val = jnp.sum(jnp.where(hit, db_ref[BLOCK * h:BLOCK * (h + 1), :], 0.0))
                    acc = jnp.where((ri == b) & (ci == h), val, acc)
            return acc

        out_ref[...] = lax.fori_loop(0, N_BUCKETS, per_bucket, jnp.zeros((N_BUCKETS, 128), F32))
        for h in range(8):
            sink_ref[h:h + 1, :] = jnp.sum(ds_ref[BLOCK * h:BLOCK * (h + 1), :], axis=0, keepdims=True)

    return _call(
        body, name="bias_reduce", grid=(1,),
        in_specs=[pl.BlockSpec(memory_space=pltpu.VMEM)] * 3,
        out_specs=[pl.BlockSpec(memory_space=pltpu.VMEM)] * 2,
        out_shape=[jax.ShapeDtypeStruct((N_BUCKETS, 128), F32), jax.ShapeDtypeStruct((8, 128), F32)],
        sem=("arbitrary",), args=(dbias, buckets, dsink_rows), hosted=hosted)


class _Att:
    def __init__(self, cfg_i):
        stride, _, h0, nh = ATT_CFG[cfg_i]
        self.d = stride if cfg_i < 3 else 1
        self.h0, self.nh = h0, nh
        self.row_w = QKV_W if self.d == 1 else 3 * 256
        if cfg_i < 3:
            self.nq, self.wkv = 1, 256
            self.q_unit = [cfg_i if self.d == 1 else 0]
            self.k_unit, self.v_unit = (3 + cfg_i, 6 + cfg_i) if self.d == 1 else (1, 2)
            self.sinks = False
        else:
            self.nq, self.wkv = 2, 128
            self.q_unit = [9, 10]
            self.k_unit, self.v_unit = 22, 23
            self.sinks = True
        self.wq = 256 * self.nq


def _att_in_specs(cf, bsz):
    uq, ukv = cf.row_w // 256, cf.row_w // cf.wkv
    specs = [pl.BlockSpec((bsz, BLOCK, 256), functools.partial(lambda r, j, u: (0, j, r * uq + u), u=u))
             for u in cf.q_unit]
    for unit in (cf.k_unit, cf.v_unit):
        specs.append(pl.BlockSpec((bsz, BLOCK, cf.wkv),
                                  functools.partial(lambda r, j, u: (0, jnp.maximum(j - 1, 0), r * ukv + u), u=unit)))
        specs.append(pl.BlockSpec((bsz, BLOCK, cf.wkv),
                                  functools.partial(lambda r, j, u: (0, j, r * ukv + u), u=unit)))
    for qb in range(cf.nq):
        specs.append(pl.BlockSpec((None, HEADS_PER_BLOCK * BLOCK, 2 * BLOCK), functools.partial(
            lambda r, j, u: (jnp.minimum(j, 1), u, 0), u=cf.h0 // HEADS_PER_BLOCK + qb)))
    if cf.sinks:
        specs += [pl.BlockSpec((HEADS_PER_BLOCK * BLOCK, 128), functools.partial(lambda r, j, u: (u, 0), u=qb))
                  for qb in range(cf.nq)]
    return specs


HEADS_PER_BLOCK = 4


def _head_masks(rows):
    head = lax.broadcasted_iota(jnp.int32, (rows, 256), 1) // HEAD_DIM
    return [head == h for h in range(HEADS_PER_BLOCK)]


def _stack_heads(x, masks):
    return jnp.concatenate([jnp.where(m, x, jnp.zeros_like(x)) for m in masks], axis=0)


def _unstack_heads(x4, masks):
    blocks = [x4[BLOCK * h:BLOCK * (h + 1)] for h in range(HEADS_PER_BLOCK)]
    return jnp.where(masks[0], blocks[0], jnp.where(masks[1], blocks[1], jnp.where(masks[2], blocks[2], blocks[3])))


def _row_value(x):
    return jnp.max(x, axis=-1, keepdims=True)


def _kv_operands(cf, x):
    if cf.wkv == 256:
        return [x]
    lane = lax.broadcasted_iota(jnp.int32, x.shape, 1)
    swapped = pltpu.roll(x, HEAD_DIM, 1)
    halves = [jnp.where(lane < HEAD_DIM, x, swapped), jnp.where(lane < HEAD_DIM, swapped, x)]
    return [jnp.concatenate([half, half], axis=1) for half in halves]


def _kv_fold(cf, grads):
    if cf.wkv == 256:
        return grads[0]
    folded = []
    for g in grads:
        x = g[:, :128] + g[:, 128:]
        folded.append(x + pltpu.roll(x, HEAD_DIM, 1))
    lane = lax.broadcasted_iota(jnp.int32, folded[0].shape, 1)
    return jnp.where(lane < HEAD_DIM, folded[0], folded[1])


def _attn_fwd(cf, zf, bias, sinks, name, hosted=()):
    bsz, l, _ = zf.shape
    nb = l // BLOCK

    def body(*refs):
        refs = list(refs)
        q_refs = [refs.pop(0) for _ in range(cf.nq)]
        kp_ref, kc_ref, vp_ref, vc_ref = [refs.pop(0) for _ in range(4)]
        bias_refs = [refs.pop(0) for _ in range(cf.nq)]
        sink_refs = [refs.pop(0) for _ in range(cf.nq)] if cf.sinks else None
        o_ref, lse_ref = refs
        masks, kv_masks = _head_masks(BLOCK), _head_masks(2 * BLOCK)
        sinks4 = [_row_value(ref[...]) for ref in sink_refs] if cf.sinks else None
        for bi in range(bsz):
            k = jnp.concatenate([kp_ref[bi], kc_ref[bi]], axis=0)
            v = jnp.concatenate([vp_ref[bi], vc_ref[bi]], axis=0)
            k_ops, v_ops = _kv_operands(cf, k), _kv_operands(cf, v)
            for qb in range(cf.nq):
                cols = slice(256 * qb, 256 * (qb + 1))
                kb, vb = k_ops[qb], v_ops[qb]
                q4 = _stack_heads(q_refs[qb][bi] * (HEAD_DIM ** -0.5), masks)
                s = _nt(q4, kb) + bias_refs[qb][...]
                m = jnp.max(s, axis=-1, keepdims=True)
                if cf.sinks:
                    sk = sinks4[qb]
                    m = jnp.maximum(m, sk)
                p = jnp.exp(s - m)
                den = jnp.sum(p, axis=-1, keepdims=True)
                if cf.sinks:
                    den = den + jnp.exp(sk - m)
                pn = (p * (1.0 / den)).astype(BF16)
                p_lanes = jnp.concatenate([pn[BLOCK * h:BLOCK * (h + 1)] for h in range(HEADS_PER_BLOCK)], axis=1)
                v4 = jnp.concatenate([jnp.where(mk, vb, jnp.zeros_like(vb)) for mk in kv_masks], axis=0)
                o_ref[bi, :, cols] = _nn(p_lanes, v4)
                lse_ref[bi, :, cols] = _unstack_heads(
                    jnp.broadcast_to(m + jnp.log(den), (HEADS_PER_BLOCK * BLOCK, 256)), masks)

    in_specs = _att_in_specs(cf, bsz)
    args = [zf] * (cf.nq + 4) + [bias] * cf.nq + ([sinks] * cf.nq if cf.sinks else [])
    out = pl.BlockSpec((bsz, BLOCK, cf.wq), lambda r, j: (0, j, r))
    shape = jax.ShapeDtypeStruct((bsz, l, cf.d * cf.wq), F32)
    return _call(
        body, name=name, grid=(cf.d, nb), in_specs=in_specs, out_specs=[out, out], out_shape=[shape, shape],
        sem=("parallel", "arbitrary"), args=args, hosted=hosted)


def _attn_bwd(cf, zf, bias, sinks, stats, dy, name, hosted=()):
    bsz, l, _ = zf.shape
    nb = l // BLOCK

    def body(*refs):
        refs = list(refs)
        q_refs = [refs.pop(0) for _ in range(cf.nq)]
        kp_ref, kc_ref, vp_ref, vc_ref = [refs.pop(0) for _ in range(4)]
        bias_refs = [refs.pop(0) for _ in range(cf.nq)]
        sink_refs = [refs.pop(0) for _ in range(cf.nq)] if cf.sinks else None
        st_ref, dy_ref, dq_ref, dk_ref, dv_ref, dbias_ref = [refs.pop(0) for _ in range(6)]
        dsink_ref = refs.pop(0) if cf.sinks else None
        dk_acc, dv_acc = refs
        r, j = pl.program_id(0), pl.program_id(1)

        @pl.when((r == 0) & (j == 0))
        def _():
            dbias_ref[...] = jnp.zeros_like(dbias_ref)
            if cf.sinks:
                dsink_ref[...] = jnp.zeros_like(dsink_ref)

        @pl.when(j == 0)
        def _():
            dk_acc[...] = jnp.zeros_like(dk_acc)
            dv_acc[...] = jnp.zeros_like(dv_acc)

        masks = _head_masks(BLOCK)
        cur = pl.ds(pl.multiple_of(j * BLOCK, BLOCK), BLOCK)
        prev = pl.ds(pl.multiple_of(jnp.maximum(j - 1, 0) * BLOCK, BLOCK), BLOCK)
        sinks4 = [_row_value(ref[...]) for ref in sink_refs] if cf.sinks else None
        ds_sum, dsink_sum = [None] * cf.nq, [None] * cf.nq
        for bi in range(bsz):
            k = jnp.concatenate([kp_ref[bi], kc_ref[bi]], axis=0)
            v = jnp.concatenate([vp_ref[bi], vc_ref[bi]], axis=0)
            dk_blocks, dv_blocks = [], []
            k_ops, v_ops = _kv_operands(cf, k), _kv_operands(cf, v)
            for qb in range(cf.nq):
                cols = slice(256 * qb, 256 * (qb + 1))
                kb, vb = k_ops[qb], v_ops[qb]
                q4 = _stack_heads(q_refs[qb][bi] * (HEAD_DIM ** -0.5), masks)
                st = st_ref[bi, :, 128 * qb:128 * (qb + 1)]
                lt4, e4 = _read_stats(st, 0), _read_stats(st, 1)
                pa = jnp.exp(_nt(q4, kb) + bias_refs[qb][...] - lt4)
                dy = dy_ref[bi, :, cols]
                dy4 = _stack_heads(dy, masks)
                ds = pa * (_nt(dy4, vb) - e4)
                ds_sum[qb] = ds if ds_sum[qb] is None else ds_sum[qb] + ds
                if cf.sinks:
                    dsk = jnp.exp(sinks4[qb] - lt4) * e4
                    dsink_sum[qb] = dsk if dsink_sum[qb] is None else dsink_sum[qb] + dsk
                dsb = ds.astype(BF16)
                dq_ref[bi, :, cols] = (_unstack_heads(_nn(dsb, kb), masks) * (HEAD_DIM ** -0.5)).astype(dq_ref.dtype)
                dk_blocks.append(_tn(dsb, q4))
                dv_blocks.append(_tn(pa.astype(BF16), dy4))
            dk_new, dv_new = _kv_fold(cf, dk_blocks), _kv_fold(cf, dv_blocks)
            dk_acc[bi, cur, :] += dk_new[BLOCK:]
            dv_acc[bi, cur, :] += dv_new[BLOCK:]
            dk_acc[bi, prev, :] += dk_new[:BLOCK]
            dv_acc[bi, prev, :] += dv_new[:BLOCK]
        for qb in range(cf.nq):
            rows = slice(HEADS_PER_BLOCK * BLOCK * qb, HEADS_PER_BLOCK * BLOCK * (qb + 1))
            dbias_ref[rows, :] += ds_sum[qb]
            if cf.sinks:
                dsink_ref[rows, :] -= dsink_sum[qb]

        @pl.when(j == nb - 1)
        def _():
            dk_ref[...] = dk_acc[...].astype(dk_ref.dtype)
            dv_ref[...] = dv_acc[...].astype(dv_ref.dtype)

    tok = pl.BlockSpec((bsz, BLOCK, cf.wq), lambda r, j: (0, j, r))
    in_specs = _att_in_specs(cf, bsz) + [pl.BlockSpec((bsz, BLOCK, cf.wq // 2), lambda r, j: (0, j, r)), tok]
    args = [zf] * (cf.nq + 4) + [bias] * cf.nq + ([sinks] * cf.nq if cf.sinks else []) + [stats, dy]
    seq = pl.BlockSpec((bsz, l, cf.wkv), lambda r, j: (0, 0, r))
    out_specs = [tok, seq, seq, pl.BlockSpec((cf.nh * BLOCK, 2 * BLOCK), lambda r, j: (0, 0))]
    grad_dtype = BF16 if cf.d == 1 else F32
    out_shape = [jax.ShapeDtypeStruct((bsz, l, cf.d * cf.wq), grad_dtype),
                 jax.ShapeDtypeStruct((bsz, l, cf.d * cf.wkv), grad_dtype),
                 jax.ShapeDtypeStruct((bsz, l, cf.d * cf.wkv), grad_dtype),
                 jax.ShapeDtypeStruct((cf.nh * BLOCK, 2 * BLOCK), F32)]
    if cf.sinks:
        out_specs.append(pl.BlockSpec((cf.nh * BLOCK, 128), lambda r, j: (0, 0)))
        out_shape.append(jax.ShapeDtypeStruct((cf.nh * BLOCK, 128), F32))
    return _call(
        body, name=name, grid=(cf.d, nb), in_specs=in_specs, out_specs=out_specs, out_shape=out_shape,
        scratch=[pltpu.VMEM((bsz, l, cf.wkv), F32), pltpu.VMEM((bsz, l, cf.wkv), F32)],
        sem=("arbitrary", "arbitrary"), args=args, hosted=hosted)


def _merge_fwd(o_a, lse_a, o_b, zg, h, wa_t, wb_t, wout, name):
    t = h.shape[0]
    tm = 512

    def body(o1, o2, o3, l1, l2, l3, ob_ref, zg_ref, h_ref, wa_ref, wb_ref, wo_ref, h2_ref, ya_ref, lt_ref, *slabs):
        o = [_unfold(ref, slabs[i], d) for i, (ref, d) in enumerate(zip((o1, o2, o3), DILATIONS))]
        l = [_unfold(ref, slabs[3 + i], d) for i, (ref, d) in enumerate(zip((l1, l2, l3), DILATIONS))]
        m = jnp.maximum(jnp.maximum(l[0], l[1]), l[2])
        e1, e2, e3 = jnp.exp(l[0] - m), jnp.exp(l[1] - m), jnp.exp(l[2] - m)
        se = e1 + e2 + e3
        ya = (e1 / se) * o[0] + (e2 / se) * o[1] + (e3 / se) * o[2]
        ya_ref[...] = ya
        lt_ref[...] = m + jnp.log(se)
        pa = _nt(ya.astype(BF16), wa_ref[...])
        pb = _nt(ob_ref[...].astype(BF16), wb_ref[...])
        merged = (jax.nn.sigmoid(zg_ref[:, :D_MODEL].astype(F32)) * pa
                  + jax.nn.sigmoid(zg_ref[:, D_MODEL:].astype(F32)) * pb)
        h2_ref[...] = h_ref[...] + _nn(merged.astype(BF16), wo_ref[...])

    row = lambda w: pl.BlockSpec((tm, w), lambda i: (i, 0))
    folded = [pl.BlockSpec((tm // d, d * 256), lambda i: (i, 0)) for d in DILATIONS]
    full = _resident
    return pl.pallas_call(
        body, name=name, grid=(t // tm,),
        in_specs=folded + folded + [row(512), row(GATE_W), row(D_MODEL), full(wa_t), full(wb_t), full(wout)],
        out_specs=[row(D_MODEL), row(256), row(256)],
        out_shape=[jax.ShapeDtypeStruct((t, D_MODEL), F32), jax.ShapeDtypeStruct((t, 256), F32),
                   jax.ShapeDtypeStruct((t, 256), F32)],
        scratch_shapes=[pltpu.VMEM((2, tm, 128), F32)] * 6,
        compiler_params=_params("parallel"),
    )(*o_a, *lse_a, o_b, zg, h, wa_t, wb_t, wout)


STAT_LANES = 32


def _head_stats(lt, dy, y):
    rows, w = dy.shape
    lane = lax.broadcasted_iota(jnp.int32, (rows, w), 1)
    out_lane = lax.broadcasted_iota(jnp.int32, (rows, w // 2), 1)
    prod = dy * y
    out = jnp.zeros((rows, w // 2), F32)
    for hd in range(w // HEAD_DIM):
        mine = lane // HEAD_DIM == hd
        dot = jnp.sum(jnp.where(mine, prod, 0.0), axis=-1, keepdims=True)
        both = jnp.where(out_lane % STAT_LANES < STAT_LANES // 2, _row_value(jnp.where(mine, lt, NEG)), dot)
        out = jnp.where(out_lane // STAT_LANES == hd, both, out)
    return out


def _read_stats(st, part):
    lane = lax.broadcasted_iota(jnp.int32, st.shape, 1)
    half = (lane % STAT_LANES < STAT_LANES // 2) == (part == 0)
    return jnp.concatenate([_row_value(jnp.where((lane // STAT_LANES == hd) & half, st, NEG))
                            for hd in range(HEADS_PER_BLOCK)], axis=0)


def _merge_bwd(dh, ya, lse_tot, o_b, lse_b, zg, wa_t, wb_t, wout, name, hosted=()):
    t = dh.shape[0]
    tm = 512

    def body(dh_ref, ya_ref, lt_ref, ob_ref, lb_ref, zg_ref, wa_ref, wb_ref, wo_ref,
             mg_ref, dpa_ref, dpb_ref, dzg_ref, dy1, dy2, dy3, st1, st2, st3, dyb_ref, stb_ref, *slabs):
        dm = _nt(dh_ref[...].astype(BF16), wo_ref[...])
        pa = _nt(ya_ref[...].astype(BF16), wa_ref[...])
        pb = _nt(ob_ref[...].astype(BF16), wb_ref[...])
        sa = jax.nn.sigmoid(zg_ref[:, :D_MODEL].astype(F32))
        sb = jax.nn.sigmoid(zg_ref[:, D_MODEL:].astype(F32))
        mg_ref[...] = (sa * pa + sb * pb).astype(BF16)
        dpa = (dm * sa).astype(BF16)
        dpb = (dm * sb).astype(BF16)
        dpa_ref[...] = dpa
        dpb_ref[...] = dpb
        dzg_ref[:, :D_MODEL] = (dm * pa * (sa * (1.0 - sa))).astype(BF16)
        dzg_ref[:, D_MODEL:] = (dm * pb * (sb * (1.0 - sb))).astype(BF16)
        dya = _nn(dpa, wa_ref[...])
        dyb = _nn(dpb, wb_ref[...])
        stats = _head_stats(lt_ref[...], dya, ya_ref[...])
        for i, d in enumerate(DILATIONS):
            _fold(dya, slabs[i], (dy1, dy2, dy3)[i], d)
            _fold(stats, slabs[3 + i], (st1, st2, st3)[i], d)
        dyb_ref[...] = dyb.astype(BF16)
        stb_ref[...] = _head_stats(lb_ref[...], dyb, ob_ref[...])

    row = lambda w: pl.BlockSpec((tm, w), lambda i: (i, 0))
    folded = [pl.BlockSpec((tm // d, d * 256), lambda i: (i, 0)) for d in DILATIONS]
    full = _resident
    sds = jax.ShapeDtypeStruct
    dy_shape = [sds((t // d, d * 256), BF16) for d in DILATIONS]
    st_shape = [sds((t // d, d * 128), F32) for d in DILATIONS]
    st_specs = [pl.BlockSpec((tm // d, d * 128), lambda i: (i, 0)) for d in DILATIONS]
    return _call(
        body, name=name, grid=(t // tm,),
        in_specs=[row(D_MODEL), row(256), row(256), row(512), row(512), row(GATE_W), full(wa_t), full(wb_t),
                  full(wout)],
        out_specs=[row(D_MODEL), row(D_MODEL), row(D_MODEL), row(GATE_W)] + folded + st_specs + [row(512), row(256)],
        out_shape=[sds((t, D_MODEL), BF16), sds((t, D_MODEL), BF16), sds((t, D_MODEL), BF16),
                   sds((t, GATE_W), BF16)] + dy_shape + st_shape + [sds((t, 512), BF16), sds((t, 256), F32)],
        scratch=[pltpu.VMEM((2, tm, 128), F32)] * 6,
        sem=("parallel",), args=(dh, ya, lse_tot, o_b, lse_b, zg, wa_t, wb_t, wout), hosted=hosted)


def _pair_sum(grad, got, name):
    _, _, r, cdim = grad.shape
    core = lax.axis_index("c").astype(jnp.int32).reshape(1)

    def body(core_ref, g_ref, s_ref, o_ref):
        o_ref[...] = (g_ref[...] + s_ref[...]).astype(BF16)

    return pl.pallas_call(
        body, name=name,
        grid_spec=pltpu.PrefetchScalarGridSpec(
            num_scalar_prefetch=1, grid=(N_CHIP,),
            in_specs=[pl.BlockSpec((None, None, r, cdim), lambda q, core_ref: (q, core_ref[0], 0, 0)),
                      pl.BlockSpec((None, None, r, cdim), lambda q, core_ref: (q, 0, 0, 0))],
            out_specs=pl.BlockSpec((None, r, cdim), lambda q, core_ref: (q, 0, 0))),
        out_shape=jax.ShapeDtypeStruct((N_CHIP, r, cdim), BF16),
        compiler_params=_params("parallel"),
    )(core, grad, got)


def _adamw_math(w, g, m, v):
    m = ADAM_B1 * m + (1.0 - ADAM_B1) * g
    v = ADAM_B2 * v + (1.0 - ADAM_B2) * jnp.square(g)
    m_hat = m / (1.0 - ADAM_B1 ** ADAM_STEP)
    v_hat = v / (1.0 - ADAM_B2 ** ADAM_STEP)
    delta = -ADAM_LR * (m_hat / (jnp.sqrt(v_hat) + ADAM_EPS) + ADAM_WD * w)
    return delta, m, v


def _update(w, parts, m, v, transposed, name, hosted=()):
    r, c = parts.shape[1:]

    def body(w_ref, p_ref, m_ref, v_ref, g_ref, d_ref, m2_ref, v2_ref):
        def total(rows):
            return ((p_ref[0, rows].astype(F32) + p_ref[1, rows].astype(F32)) + p_ref[2, rows].astype(F32)) \
                + p_ref[3, rows].astype(F32)

        def update(at, g):
            g_ref[at] = g
            d_ref[at], m2_ref[at], v2_ref[at] = _adamw_math(w_ref[at], g, m_ref[at], v_ref[at])

        if not transposed:
            update((slice(None), slice(None)), total(slice(None)))
            return
        for r0 in range(0, r, 128):
            n = min(128, r - r0)
            gt = total(slice(r0, r0 + n))
            if n < 128:
                gt = jnp.concatenate([gt, jnp.zeros((128 - n, c), F32)], axis=0)
            update((slice(None), slice(r0, r0 + n)), gt.T[:, :n])

    sds = jax.ShapeDtypeStruct(w.shape, F32)
    vm = pl.BlockSpec(memory_space=pltpu.VMEM)
    return _call(body, name=name, grid=(1,), in_specs=[vm] * 4, out_specs=[vm] * 4, out_shape=[sds] * 4,
                 sem=("arbitrary",), args=(w, parts, m, v), hosted=hosted)


SMALL_ROWS = 80


def _small_update(g, w, m, v, name):
    def body(g_ref, w_ref, m_ref, v_ref, gs_ref, d_ref, m2_ref, v2_ref, got_ref, send_sems, recv_sems):
        x, y, c = _place()
        me = 4 * x + 2 * y + c
        got_ref[me] = g_ref[...]
        copies = []
        for k in range(1, N_DEV):
            peer = (x ^ (k >> 2), y ^ ((k >> 1) & 1), c ^ (k & 1))
            cp = pltpu.make_async_remote_copy(
                src_ref=g_ref, dst_ref=got_ref.at[me], send_sem=send_sems.at[k - 1], recv_sem=recv_sems.at[k - 1],
                device_id=peer, device_id_type=MESH)
            cp.start()
            copies.append(cp)
        for cp in copies:
            cp.wait()
        total = got_ref[0]
        for k in range(1, N_DEV):
            total = total + got_ref[k]
        gs_ref[...] = total
        d_ref[...], m2_ref[...], v2_ref[...] = _adamw_math(w_ref[...], total, m_ref[...], v_ref[...])

    sds = jax.ShapeDtypeStruct((SMALL_ROWS, 128), F32)
    vm = pl.BlockSpec(memory_space=pltpu.VMEM)
    return pl.pallas_call(
        body, name=name, in_specs=[vm] * 4, out_specs=[vm] * 4, out_shape=[sds] * 4,
        scratch_shapes=[pltpu.VMEM((N_DEV, SMALL_ROWS, 128), F32), pltpu.SemaphoreType.DMA((N_DEV - 1,)),
                        pltpu.SemaphoreType.DMA((N_DEV - 1,))],
    )(g, w, m, v)


def _pack_small(gains, b_in, rel_bias, sinks, last):
    rows = [a.reshape(8, 128) for a in gains] + [b_in.reshape(40, 128), rel_bias.reshape(5, 128),
                                                 jnp.pad(sinks.reshape(1, 8), ((0, 0), (0, 120))), last]
    rows.append(jnp.zeros((SMALL_ROWS - 79, 128), F32))
    return jnp.concatenate(rows, axis=0)


def _unpack_small(p, like):
    out = [p[8 * i:8 * i + 8].reshape(like[i].shape) for i in range(4)]
    out.append(p[32:72].reshape(like[4].shape))
    out.append(p[72:77].reshape(like[5].shape))
    out.append(p[77, :8].reshape(like[6].shape))
    return out


def kernel(x, ffn1_norm, ffn1_w_gate, ffn1_w_up, ffn1_w_down, mix_norm, w_in, b_in, w_branch_a, w_branch_b, w_out, sinks, rel_bias, ffn2_norm, ffn2_w_gate, ffn2_w_up, ffn2_w_down, final_norm, loss_target, m_ffn1_norm, m_ffn1_w_gate, m_ffn1_w_up, m_ffn1_w_down, m_mix_norm, m_w_in, m_b_in, m_w_branch_a, m_w_branch_b, m_w_out, m_sinks, m_rel_bias, m_ffn2_norm, m_ffn2_w_gate, m_ffn2_w_up, m_ffn2_w_down, m_final_norm, v_ffn1_norm, v_ffn1_w_gate, v_ffn1_w_up, v_ffn1_w_down, v_mix_norm, v_w_in, v_b_in, v_w_branch_a, v_w_branch_b, v_w_out, v_sinks, v_rel_bias, v_ffn2_norm, v_ffn2_w_gate, v_ffn2_w_up, v_ffn2_w_down, v_final_norm):
    bsz, seq, _ = x.shape
    t = bsz * seq
    xt = x.reshape(t, D_MODEL)
    target = loss_target.reshape(t, D_MODEL)

    big = [("ffn1_w_gate", ffn1_w_gate, m_ffn1_w_gate, v_ffn1_w_gate, True),
           ("ffn1_w_up", ffn1_w_up, m_ffn1_w_up, v_ffn1_w_up, True),
           ("ffn1_w_down", ffn1_w_down, m_ffn1_w_down, v_ffn1_w_down, False),
           ("w_in", w_in, m_w_in, v_w_in, True),
           ("w_branch_a", w_branch_a, m_w_branch_a, v_w_branch_a, True),
           ("w_branch_b", w_branch_b, m_w_branch_b, v_w_branch_b, True),
           ("w_out", w_out, m_w_out, v_w_out, False),
           ("ffn2_w_gate", ffn2_w_gate, m_ffn2_w_gate, v_ffn2_w_gate, True),
           ("ffn2_w_up", ffn2_w_up, m_ffn2_w_up, v_ffn2_w_up, True),
           ("ffn2_w_down", ffn2_w_down, m_ffn2_w_down, v_ffn2_w_down, False)]
    shard = {nm: (w[0].T if tr else w[0]).astype(BF16) for nm, w, _, _, tr in big}
    full = {}

    def gather(names):
        return _Gather([shard[nm] for nm in names])

    def keep(names, got):
        for nm, gw in zip(names, got):
            full[nm] = gw.reshape(-1, gw.shape[-1])

    ffn1_names = ["ffn1_w_gate", "ffn1_w_up", "ffn1_w_down"]
    mix_names = ["w_in", "w_branch_a", "w_branch_b", "w_out"]
    ffn2_names = ["ffn2_w_gate", "ffn2_w_up", "ffn2_w_down"]
    g1, gm, g2, gf = ffn1_norm, mix_norm, ffn2_norm, final_norm.reshape(1, D_MODEL)

    buckets = _bucket_tiles()
    (bias,), (got,) = _bias_build(rel_bias, buckets, hosted=[gather(ffn1_names)])
    keep(ffn1_names, got)
    (h1, n1, a1, b1, hff1), (got,) = _ffn_fwd(xt, g1, full["ffn1_w_gate"], full["ffn1_w_up"], full["ffn1_w_down"], "ffn1_fwd",
                                hosted=[gather(mix_names)])
    keep(mix_names, got)
    (u, zq, zg, *zdil), (got,) = _inproj_fwd(h1, gm, full["w_in"], b_in, "inproj_fwd", hosted=[gather(ffn2_names[:2])])
    keep(ffn2_names[:2], got)
    sink_rows = jnp.broadcast_to(sinks.reshape(8, 1, 1), (8, BLOCK, 128)).reshape(8 * BLOCK, 128)
    cfs = [_Att(i) for i in range(4)]
    zfold = []
    for i, cf in enumerate(cfs):
        if cf.d == 1:
            zfold.append(zq.reshape(bsz, seq, QKV_W))
        else:
            zfold.append(zdil[i - 1].reshape(bsz, seq // cf.d, cf.d * cf.row_w))
    att = [None] * 4
    for i in (3, 0, 1, 2):
        cf = cfs[i]
        (o, lse), got = _attn_fwd(cf, zfold[i], bias, sink_rows, f"attn{i}_fwd",
                                  hosted=[gather(ffn2_names[2:])] if i == 3 else ())
        if i == 3:
            keep(ffn2_names[2:], got[0])
        att[i] = (o.reshape(t // cf.d, cf.d * cf.wq), lse.reshape(t // cf.d, cf.d * cf.wq))
    o_b, lse_b = att[3]
    h2, ya, lse_tot = _merge_fwd([a[0] for a in att[:3]], [a[1] for a in att[:3]], o_b, zg, h1,
                                 full["w_branch_a"], full["w_branch_b"], full["w_out"], "merge_fwd")
    (dh3, n2, a2, b2, hff2, loss_part, dgf), _ = _ffn_fwd(
        h2, g2, full["ffn2_w_gate"], full["ffn2_w_up"], full["ffn2_w_down"], "ffn2_fwd", head=(gf, target))

    grads, pair, from_chips = {}, {}, {}

    def by_owner(nm):
        return grads[nm].reshape(N_CHIP, 2, -1, grads[nm].shape[-1])

    def to_core(names):
        return _CoreExchange([by_owner(nm) for nm in names])

    def pair_up(names, got):
        for nm, sib in zip(names, got):
            pair[nm] = _pair_sum(by_owner(nm), sib, f"pair_sum_{nm}")

    def to_chips(names):
        return _ChipExchange([pair[nm] for nm in names])

    def landed(names, got):
        for nm, parts in zip(names, got):
            from_chips[nm] = parts

    dh2, dg2, da, db = _ffn_bwd(h2, a2, b2, g2, dh3, full["ffn2_w_gate"], full["ffn2_w_up"],
                                          full["ffn2_w_down"], "ffn2_bwd")
    grads["ffn2_w_gate"], _ = _tn_matmul(da, n2, 1408, "ffn2_dgate")
    grads["ffn2_w_up"], _ = _tn_matmul(db, n2, 1408, "ffn2_dup")
    grads["ffn2_w_down"], _ = _tn_matmul(hff2, dh3, 1408, "ffn2_ddown", scale=0.5)
    (merged, dpa, dpb, dzg, *cot), (got,) = _merge_bwd(
        dh2, ya, lse_tot, o_b, lse_b, zg, full["w_branch_a"], full["w_branch_b"], full["w_out"], "merge_bwd",
        hosted=[to_core(ffn2_names)])
    dys, sts = cot[0:3] + [cot[6]], cot[3:6] + [cot[7]]
    pair_up(ffn2_names, got)
    dq, dk, dv, dbias, dsink = [None] * 4, [None] * 4, [None] * 4, [None] * 4, None
    for i in (3, 0, 1, 2):
        cf = cfs[i]
        shp = (bsz, seq // cf.d, cf.d * cf.wq)
        hosted = {3: lambda: [to_chips(ffn2_names[:2])], 0: lambda: [to_chips(ffn2_names[2:])]}.get(i, list)()
        res, got = _attn_bwd(cf, zfold[i], bias, sink_rows, sts[i].reshape(shp[:2] + (shp[2] // 2,)), dys[i].reshape(shp),
                             f"attn{i}_bwd", hosted=hosted)
        if i == 3:
            landed(ffn2_names[:2], got[0])
        elif i == 0:
            landed(ffn2_names[2:], got[0])
        dq[i] = res[0].reshape(t // cf.d, cf.d * cf.wq)
        dk[i] = res[1].reshape(t // cf.d, cf.d * cf.wkv)
        dv[i] = res[2].reshape(t // cf.d, cf.d * cf.wkv)
        dbias[i] = res[3]
        if cf.sinks:
            dsink = res[4]
    pieces = dq[:3] + dk[:3] + dv[:3] + [dq[3], dk[3], dv[3], dzg]
    dh1, dz, db_in, dgm = _inproj_bwd(pieces, h1, gm, dh2, full["w_in"], "inproj_bwd")
    dx, dg1, da, db = _ffn_bwd(xt, a1, b1, g1, dh1, full["ffn1_w_gate"], full["ffn1_w_up"],
                                         full["ffn1_w_down"], "ffn1_bwd")
    grads["w_in"], _ = _tn_matmul(dz, u, 1280, "dw_in")
    grads["ffn1_w_down"], (got,) = _tn_matmul(hff1, dh1, 1408, "ffn1_ddown", hosted=[to_core(["w_in"])], scale=0.5)
    pair_up(["w_in"], got)
    grads["ffn1_w_gate"], got = _tn_matmul(da, n1, 1408, "ffn1_dgate",
                                           hosted=[to_chips(["w_in"]), to_core(["ffn1_w_down"])])
    landed(["w_in"], got[0])
    pair_up(["ffn1_w_down"], got[1])
    grads["ffn1_w_up"], got = _tn_matmul(db, n1, 1408, "ffn1_dup",
                                         hosted=[to_chips(["ffn1_w_down"]), to_core(["ffn1_w_gate"])])
    landed(["ffn1_w_down"], got[0])
    pair_up(["ffn1_w_gate"], got[1])
    grads["w_out"], got = _tn_matmul(merged, dh2, 1024, "dw_out",
                                     hosted=[to_chips(["ffn1_w_gate"]), to_core(["ffn1_w_up"])])
    landed(["ffn1_w_gate"], got[0])
    pair_up(["ffn1_w_up"], got[1])
    grads["w_branch_b"], got = _tn_matmul(dpb, o_b, 1024, "dw_branch_b",
                                          hosted=[to_chips(["ffn1_w_up"]), to_core(["w_out"])])
    landed(["ffn1_w_up"], got[0])
    pair_up(["w_out"], got[1])
    grads["w_branch_a"], got = _tn_matmul(dpa, ya, 1024, "dw_branch_a",
                                          hosted=[to_chips(["w_out"]), to_core(["w_branch_b"])])
    landed(["w_out"], got[0])
    pair_up(["w_branch_b"], got[1])
    (dtable, dsinks), got = _bias_reduce(jnp.concatenate(dbias, axis=0), buckets, dsink,
                                         hosted=[to_chips(["w_branch_b"]), to_core(["w_branch_a"])])
    landed(["w_branch_b"], got[0])
    pair_up(["w_branch_a"], got[1])

    out_g, out_d, out_m, out_v = {}, {}, {}, {}
    for idx, (nm, w, m, v, tr) in enumerate(big):
        around = tr and w.shape[-1] % 128 != 0
        wmv = [a[0].T if around else a[0] for a in (w, m, v)]
        res, got = _update(wmv[0], from_chips[nm], wmv[1], wmv[2], tr and not around, f"update_{nm}",
                           hosted=[to_chips(["w_branch_a"])] if idx == 0 else ())
        if idx == 0:
            landed(["w_branch_a"], got[0])
        out_g[nm], out_d[nm], out_m[nm], out_v[nm] = [(a.T if around else a)[None] for a in res]

    small = [("ffn1_norm", ffn1_norm, m_ffn1_norm, v_ffn1_norm), ("mix_norm", mix_norm, m_mix_norm, v_mix_norm),
             ("ffn2_norm", ffn2_norm, m_ffn2_norm, v_ffn2_norm), ("final_norm", final_norm, m_final_norm, v_final_norm),
             ("b_in", b_in, m_b_in, v_b_in), ("rel_bias", rel_bias, m_rel_bias, v_rel_bias),
             ("sinks", sinks, m_sinks, v_sinks)]
    zero_row = jnp.zeros((1, 128), F32)
    pack = lambda arrs, last: _pack_small(arrs[:4], arrs[4], arrs[5], arrs[6], last)
    g_small = pack([dg1, dgm, dg2, dgf, db_in, dtable[:, :TOTAL_HEADS], dsinks[:, 0]], loss_part)
    packed = [pack([s[k] for s in small], zero_row) for k in (1, 2, 3)]
    gs, ds, ms, vs = _small_update(g_small, *packed, "small_update")
    like = [s[1] for s in small]
    for nm_s, g_, d_, m_, v_ in zip([s[0] for s in small], _unpack_small(gs, like), _unpack_small(ds, like),
                                    _unpack_small(ms, like), _unpack_small(vs, like)):
        out_g[nm_s], out_d[nm_s], out_m[nm_s], out_v[nm_s] = g_, d_, m_, v_
    loss = gs[78, 0]

    order = ["ffn1_norm", "ffn1_w_gate", "ffn1_w_up", "ffn1_w_down", "mix_norm", "w_in", "b_in", "w_branch_a",
             "w_branch_b", "w_out", "sinks", "rel_bias", "ffn2_norm", "ffn2_w_gate", "ffn2_w_up", "ffn2_w_down",
             "final_norm"]
    return (loss, dx.reshape(x.shape), *[out_g[k] for k in order], *[out_d[k] for k in order],
            *[out_m[k] for k in order], *[out_v[k] for k in order])
```

```python
import functools
import math

import numpy as np
import jax
import jax.numpy as jnp
from jax import lax
from jax.experimental import pallas as pl
from jax.experimental.pallas import tpu as pltpu

D_MODEL = 1024
D_FF = 2816
FF_CHUNK = 256
HEAD_DIM = 64
BLOCK = 128
N_BUCKETS = 32
MAX_DISTANCE = 2048
A_HEADS = 12
TOTAL_HEADS = 20
DIL_GROUPS = ((128, 1), (512, 4), (2048, 16))
B_WINDOW = 128
QKV_W = 3072
GATE_W = 2048
D_IN = QKV_W + GATE_W
EPS = 1e-6
NEG = -1e30
N_DEV = 8
N_CHIP = 4
ADAM_LR, ADAM_B1, ADAM_B2, ADAM_EPS, ADAM_WD, ADAM_STEP = 0.001, 0.9, 0.999, 1e-08, 0.01, 10
VMEM_LIMIT = 56 * 1024 * 1024
MESH = pl.DeviceIdType.MESH
BF16 = jnp.bfloat16
F32 = jnp.float32
ANY = pl.BlockSpec(memory_space=pl.ANY)


def _params(*sem):
    return pltpu.CompilerParams(dimension_semantics=sem, vmem_limit_bytes=VMEM_LIMIT)


def _resident(a):
    return pl.BlockSpec(a.shape, lambda i: (0, 0), pipeline_mode=pl.Buffered(1))


def _place():
    return lax.axis_index("x"), lax.axis_index("y"), lax.axis_index("c")


class _Gather:
    def __init__(self, shards):
        self.ins = list(shards)
        n = self.n = len(shards)
        self.out_shape = [jax.ShapeDtypeStruct((N_DEV,) + s.shape, s.dtype) for s in shards]
        self.scratch = [pltpu.SemaphoreType.DMA((7 * n,)), pltpu.SemaphoreType.DMA((7 * n,)),
                        pltpu.SemaphoreType.DMA((n,))]

    def _copies(self, ins, outs, sems):
        send_sems, recv_sems, local_sems = sems
        x, y, c = _place()
        me, sibling = (x, y, c), (x, y, 1 - c)
        chips = [(1 - x, y), (x, 1 - y), (1 - x, 1 - y)]

        def copy(i, k, block, to, src=None):
            dst = outs[i].at[4 * block[0] + 2 * block[1] + block[2]]
            return pltpu.make_async_remote_copy(
                src_ref=dst if src is None else src, dst_ref=dst, send_sem=send_sems.at[7 * i + k],
                recv_sem=recv_sems.at[7 * i + k], device_id=to, device_id_type=MESH)

        n = self.n
        south = c == 0
        relayed = (jnp.where(south, 1 - x, x), jnp.where(south, y, 1 - y), c)
        relay_to = (jnp.where(south, x, 1 - x), jnp.where(south, 1 - y, y), c)
        mine = [pltpu.make_async_copy(ins[i], outs[i].at[4 * x + 2 * y + c], local_sems.at[i]) for i in range(n)]
        first = [copy(i, 0, me, sibling, src=ins[i]) for i in range(n)]
        first += [copy(i, 1 + j, me, (*chips[j], c), src=ins[i]) for i in range(n) for j in range(2)]
        landed = [copy(i, 1 + j, (*chips[j], c), me) for j in range(2) for i in range(n)]
        passed = [copy(i, 4 + j, (*chips[j], c), sibling) for j in range(2) for i in range(n)]
        relays = [copy(i, 3, relayed, relay_to) for i in range(n)]
        diag_landed = [copy(i, 3, (*chips[2], c), me) for i in range(n)]
        diag_passed = [copy(i, 6, (*chips[2], c), sibling) for i in range(n)]
        from_sibling = [copy(i, 0, sibling, me) for i in range(n)]
        from_sibling += [copy(i, 4 + j, (*chip, 1 - c), me) for i in range(n) for j, chip in enumerate(chips)]
        return mine, first, landed, passed, relays, diag_landed, diag_passed, from_sibling

    def start(self, ins, outs, sems):
        mine, first = self._copies(ins, outs, sems)[:2]
        for cp in mine + first:
            cp.start()

    def mid(self, ins, outs, sems):
        _, _, landed, passed, relays, _, _, _ = self._copies(ins, outs, sems)
        for got in landed:
            got.wait_recv()
        for cp in relays + passed:
            cp.start()

    def end(self, ins, outs, sems):
        mine, first, _, passed, relays, diag_landed, diag_passed, from_sibling = self._copies(ins, outs, sems)
        for got, fwd in zip(diag_landed, diag_passed):
            got.wait_recv()
            fwd.start()
        for cp in from_sibling:
            cp.wait_recv()
        for cp in first + passed + relays + diag_passed:
            cp.wait_send()
        for cp in mine:
            cp.wait()


class _CoreExchange:
    def __init__(self, grads):
        self.ins = list(grads)
        n = self.n = len(grads)
        self.out_shape = [jax.ShapeDtypeStruct((N_CHIP, 1) + g.shape[2:], g.dtype) for g in grads]
        self.scratch = [pltpu.SemaphoreType.DMA((n,)), pltpu.SemaphoreType.DMA((n,))]

    def _copies(self, ins, outs, sems):
        x, y, c = _place()
        return [pltpu.make_async_remote_copy(
            src_ref=ins[i].at[:, pl.ds(1 - c, 1)], dst_ref=outs[i], send_sem=sems[0].at[i],
            recv_sem=sems[1].at[i], device_id=(x, y, 1 - c), device_id_type=MESH) for i in range(self.n)]

    def start(self, ins, outs, sems):
        for cp in self._copies(ins, outs, sems):
            cp.start()

    mid = None

    def end(self, ins, outs, sems):
        for cp in self._copies(ins, outs, sems):
            cp.wait()


class _ChipExchange:
    def __init__(self, parts):
        self.ins = list(parts)
        n = self.n = len(parts)
        self.out_shape = [jax.ShapeDtypeStruct(p.shape, p.dtype) for p in parts]
        self.scratch = [pltpu.SemaphoreType.DMA((3 * n,)), pltpu.SemaphoreType.DMA((3 * n,)),
                        pltpu.SemaphoreType.DMA((n,))]

    def _copies(self, ins, outs, sems):
        send_sems, recv_sems, local_sems = sems
        x, y, c = _place()
        my_chip = 2 * x + y
        copies = []
        for i in range(self.n):
            copies.append(pltpu.make_async_copy(ins[i].at[my_chip], outs[i].at[my_chip], local_sems.at[i]))
            for k, (qx, qy) in enumerate([(1 - x, y), (x, 1 - y), (1 - x, 1 - y)]):
                copies.append(pltpu.make_async_remote_copy(
                    src_ref=ins[i].at[2 * qx + qy], dst_ref=outs[i].at[my_chip], send_sem=send_sems.at[3 * i + k],
                    recv_sem=recv_sems.at[3 * i + k], device_id=(qx, qy, c), device_id_type=MESH))
        return copies

    def start(self, ins, outs, sems):
        for cp in self._copies(ins, outs, sems):
            cp.start()

    mid = None

    def end(self, ins, outs, sems):
        for cp in self._copies(ins, outs, sems):
            cp.wait()


def _call(body, *, name, grid, in_specs, out_specs, out_shape, args, scratch=(), sem=None, hosted=()):
    n_in, n_out, n_scr = len(in_specs), len(out_specs), len(scratch)
    x_in = [len(p.ins) for p in hosted]
    x_scr = [len(p.scratch) for p in hosted]
    steps = int(np.prod(grid))

    def wrapped(*refs):
        refs = list(refs)
        ins, refs = refs[:n_in], refs[n_in:]
        x_ins = [[refs.pop(0) for _ in range(k)] for k in x_in]
        outs, refs = refs[:n_out], refs[n_out:]
        x_outs = [[refs.pop(0) for _ in range(k)] for k in x_in]
        scr, refs = refs[:n_scr], refs[n_scr:]
        x_sems = [[refs.pop(0) for _ in range(k)] for k in x_scr]
        step = 0
        for d in range(len(grid)):
            step = step * grid[d] + pl.program_id(d)

        def phase(which, at):
            fns = [(getattr(p, which), a) for p, a in zip(hosted, zip(x_ins, x_outs, x_sems)) if getattr(p, which)]
            if fns:
                @pl.when(step == at)
                def _():
                    for fn, a in fns:
                        fn(*a)

        phase("start", 0)
        if steps > 1:
            phase("mid", steps // 2)
        body(*ins, *outs, *scr)
        if steps == 1:
            phase("mid", 0)
        phase("end", steps - 1)

    results = pl.pallas_call(
        wrapped, name=name, grid=grid,
        in_specs=list(in_specs) + [ANY] * sum(x_in), out_specs=list(out_specs) + [ANY] * sum(x_in),
        out_shape=list(out_shape) + [s for p in hosted for s in p.out_shape],
        scratch_shapes=list(scratch) + [s for p in hosted for s in p.scratch],
        compiler_params=_params(*(("arbitrary",) * len(grid) if hosted else sem)),
    )(*args, *[a for p in hosted for a in p.ins])
    own, rest = list(results[:n_out]), list(results[n_out:])
    return own, [[rest.pop(0) for _ in range(k)] for k in x_in]


def _nt(a, b):
    return lax.dot_general(a, b, (((1,), (1,)), ((), ())), preferred_element_type=F32)


def _nn(a, b):
    return lax.dot_general(a, b, (((1,), (0,)), ((), ())), preferred_element_type=F32)


def _tn(a, b):
    return lax.dot_general(a, b, (((0,), (0,)), ((), ())), preferred_element_type=F32)


def _rms(x, g):
    r = lax.rsqrt(jnp.mean(x * x, axis=-1, keepdims=True) + EPS)
    return x * r, r


def _rms_bwd(dn, xhat, r, g):
    dg = jnp.sum(dn * xhat, axis=0, keepdims=True)
    dxh = dn * g
    dx = r * (dxh - xhat * jnp.mean(dxh * xhat, axis=-1, keepdims=True))
    return dx, dg


def _ffn_fwd(x, g, wg_t, wu_t, wd, name, hosted=(), head=None):
    t = x.shape[0]
    tm = 512 if head is None else 256

    def body(*refs):
        x_ref, g_ref, wg_ref, wu_ref, wd_ref = refs[:5]
        h_ref, n_ref, a_ref, b_ref, hff_ref = refs[-5 if head is None else -7:][:5]
        xhat, _ = _rms(x_ref[...], g_ref[...])
        n = (xhat * g_ref[...]).astype(BF16)
        n_ref[...] = n
        for c in range(0, D_FF, FF_CHUNK):
            cols = slice(c, c + FF_CHUNK)
            a = _nt(n, wg_ref[cols, :])
            b = _nt(n, wu_ref[cols, :])
            a_ref[:, cols] = a.astype(BF16)
            b_ref[:, cols] = b.astype(BF16)
            hff_ref[:, cols] = (a * jax.nn.sigmoid(a) * b).astype(BF16)
        h = x_ref[...] + 0.5 * _nn(hff_ref[...], wd_ref[...])
        if head is None:
            h_ref[...] = h
            return
        gf_ref, t_ref, loss_ref, dgf_ref = refs[5], refs[6], refs[-2], refs[-1]

        @pl.when(pl.program_id(0) == 0)
        def _():
            loss_ref[...] = jnp.zeros_like(loss_ref)
            dgf_ref[...] = jnp.zeros_like(dgf_ref)

        yhat, r = _rms(h, gf_ref[...])
        err = yhat * gf_ref[...] - t_ref[...]
        loss_ref[...] += 0.5 * jnp.sum(jnp.mean(err * err, axis=-1, keepdims=True), axis=0, keepdims=True)
        h_ref[...], dgf = _rms_bwd(err * (1.0 / D_MODEL), yhat, r, gf_ref[...])
        dgf_ref[...] += dgf

    row = pl.BlockSpec((tm, D_MODEL), lambda i: (i, 0))
    hid = pl.BlockSpec((tm, D_FF), lambda i: (i, 0))
    sds = jax.ShapeDtypeStruct
    in_specs = [row, _resident(g), _resident(wg_t), _resident(wu_t), _resident(wd)]
    out_specs = [row, row, hid, hid, hid]
    out_shape = [sds((t, D_MODEL), F32), sds((t, D_MODEL), BF16), sds((t, D_FF), BF16), sds((t, D_FF), BF16),
                 sds((t, D_FF), BF16)]
    args = (x, g, wg_t, wu_t, wd)
    if head is not None:
        in_specs += [_resident(head[0]), row]
        out_specs += [pl.BlockSpec((1, 128), lambda i: (0, 0)), pl.BlockSpec((1, D_MODEL), lambda i: (0, 0))]
        out_shape += [sds((1, 128), F32), sds((1, D_MODEL), F32)]
        args += tuple(head)
    return _call(body, name=name, grid=(t // tm,), in_specs=in_specs, out_specs=out_specs, out_shape=out_shape,
                 sem=("parallel",) if head is None else ("arbitrary",), args=args, hosted=hosted)


def _ffn_bwd(x, a_pre, b_pre, g, dh, wg_t, wu_t, wd, name):
    t = x.shape[0]
    tm = 256

    def body(x_ref, a_ref, b_ref, g_ref, dh_ref, wg_ref, wu_ref, wd_ref,
             dx_ref, dg_ref, da_ref, db_ref):
        @pl.when(pl.program_id(0) == 0)
        def _():
            dg_ref[...] = jnp.zeros_like(dg_ref)

        dhh = (0.5 * dh_ref[...]).astype(BF16)
        for c in range(0, D_FF, FF_CHUNK):
            cols = slice(c, c + FF_CHUNK)
            a = a_ref[:, cols].astype(F32)
            b = b_ref[:, cols].astype(F32)
            s = jax.nn.sigmoid(a)
            silu = a * s
            dhff = _nt(dhh, wd_ref[cols, :])
            da_ref[:, cols] = (dhff * b * (s * (1.0 + a * (1.0 - s)))).astype(BF16)
            db_ref[:, cols] = (dhff * silu).astype(BF16)
        dn = _nn(da_ref[...], wg_ref[...]) + _nn(db_ref[...], wu_ref[...])
        xhat, r = _rms(x_ref[...], g_ref[...])
        dx, dg = _rms_bwd(dn, xhat, r, g_ref[...])
        dx_ref[...] = dh_ref[...] + dx
        dg_ref[...] += dg

    row = pl.BlockSpec((tm, D_MODEL), lambda i: (i, 0))
    hid = pl.BlockSpec((tm, D_FF), lambda i: (i, 0))
    return pl.pallas_call(
        body, name=name, grid=(t // tm,),
        in_specs=[row, hid, hid, _resident(g), row, _resident(wg_t), _resident(wu_t), _resident(wd)],
        out_specs=[row, pl.BlockSpec((1, D_MODEL), lambda i: (0, 0)), hid, hid],
        out_shape=[jax.ShapeDtypeStruct((t, D_MODEL), F32), jax.ShapeDtypeStruct((1, D_MODEL), F32),
                   jax.ShapeDtypeStruct((t, D_FF), BF16), jax.ShapeDtypeStruct((t, D_FF), BF16)],
        compiler_params=_params("arbitrary"),
    )(x, a_pre, b_pre, g, dh, wg_t, wu_t, wd)


def _tn_matmul(a, b, rc, name, hosted=(), scale=None):
    t, r = a.shape
    c = b.shape[1]
    tk = min(t, 2048)

    def body(a_ref, b_ref, o_ref):
        @pl.when(pl.program_id(1) == 0)
        def _():
            o_ref[...] = jnp.zeros_like(o_ref)

        o_ref[...] += _tn(a_ref[...].astype(BF16), b_ref[...].astype(BF16))
        if scale is not None:
            @pl.when(pl.program_id(1) == t // tk - 1)
            def _():
                o_ref[...] *= scale

    (out,), got = _call(
        body, name=name, grid=(r // rc, t // tk),
        in_specs=[pl.BlockSpec((tk, rc), lambda i, k: (k, i)), pl.BlockSpec((tk, c), lambda i, k: (k, 0))],
        out_specs=[pl.BlockSpec((rc, c), lambda i, k: (i, 0))],
        out_shape=[jax.ShapeDtypeStruct((r, c), F32)],
        sem=("parallel", "arbitrary"), args=(a, b), hosted=hosted)
    return out, got


def _unfold(blk_ref, slab_ref, d):
    if d == 1:
        return blk_ref[...]
    n = blk_ref.shape[0]
    for r in range(d):
        for half in range(2):
            c0 = 256 * r + 128 * half
            slab_ref[half, pl.ds(r, n, stride=d), :] = blk_ref[:, c0:c0 + 128]
    return jnp.concatenate([slab_ref[0], slab_ref[1]], axis=1)


def _fold(x, slab_ref, out_ref, d):
    if d == 1:
        out_ref[...] = x.astype(out_ref.dtype)
        return
    n, w = out_ref.shape[0], x.shape[1]
    for part in range(w // 128):
        slab_ref[part] = x[:, 128 * part:128 * (part + 1)]
    for r in range(d):
        for part in range(w // 128):
            c0 = w * r + 128 * part
            out_ref[:, c0:c0 + 128] = slab_ref[part, pl.ds(r, n, stride=d), :].astype(out_ref.dtype)


DILATIONS = tuple(d for _, d in DIL_GROUPS)


PIECE_W = (256,) * 9 + (512, 128, 128, GATE_W)
PIECE_D = DILATIONS * 3 + (1, 1, 1, 1)


def _inproj_fwd(h, g, w_t, b_in, name, hosted=()):
    t = h.shape[0]
    tm, nc = 512, 256
    dilated = [(gi, d) for gi, d in enumerate(DILATIONS) if d > 1]

    def body(h_ref, g_ref, w_ref, b_ref, u_ref, zq_ref, zg_ref, *rest):
        zf_refs, slabs = rest[:len(dilated)], rest[len(dilated):]
        xhat, _ = _rms(h_ref[...], g_ref[...])
        u = (xhat * g_ref[...]).astype(BF16)
        u_ref[...] = u
        for c in range(D_IN // nc):
            z = _nt(u, w_ref[c * nc:(c + 1) * nc, :]) + b_ref[:, c * nc:(c + 1) * nc]
            if c < QKV_W // nc:
                zq_ref[:, c * nc:(c + 1) * nc] = z.astype(BF16)
            else:
                zg_ref[:, c * nc - QKV_W:(c + 1) * nc - QKV_W] = z.astype(BF16)
            part, gi = divmod(c, len(DILATIONS))
            for k, (gk, d) in enumerate(dilated):
                if part < 3 and gi == gk:
                    slab, n = slabs[3 * k + part], tm // d
                    slab[0] = z[:, :128]
                    slab[1] = z[:, 128:]
                    for r in range(d):
                        for half in range(2):
                            c0 = 768 * r + 256 * part + 128 * half
                            zf_refs[k][:, c0:c0 + 128] = slab[half, pl.ds(r, n, stride=d), :].astype(BF16)

    row = lambda w: pl.BlockSpec((tm, w), lambda i: (i, 0))
    full = _resident
    sds = jax.ShapeDtypeStruct
    return _call(
        body, name=name, grid=(t // tm,),
        in_specs=[row(D_MODEL), full(g), full(w_t), full(b_in)],
        out_specs=[row(D_MODEL), row(QKV_W), row(GATE_W)]
        + [pl.BlockSpec((tm // d, d * 768), lambda i: (i, 0)) for _, d in dilated],
        out_shape=[sds((t, D_MODEL), BF16), sds((t, QKV_W), BF16), sds((t, GATE_W), BF16)]
        + [sds((t // d, d * 768), BF16) for _, d in dilated],
        scratch=[pltpu.VMEM((2, tm, 128), F32)] * (3 * len(dilated)),
        sem=("parallel",), args=(h, g, w_t, b_in), hosted=hosted)


def _inproj_bwd(pieces, h, g, dh_res, w_t, name):
    t = h.shape[0]
    tm = 256
    npiece = len(PIECE_W)
    offs = np.concatenate([[0], np.cumsum(PIECE_W)]).tolist()

    def body(*refs):
        p_refs = refs[:npiece]
        h_ref, g_ref, dhr_ref, w_ref, dh_ref, dz_ref, db_ref, dg_ref = refs[npiece:npiece + 8]
        slabs = list(refs[npiece + 8:])
        i = pl.program_id(0)

        @pl.when(i == 0)
        def _():
            db_ref[...] = jnp.zeros_like(db_ref)
            dg_ref[...] = jnp.zeros_like(dg_ref)

        du = jnp.zeros((tm, D_MODEL), F32)
        for k in range(npiece):
            o, w = offs[k], PIECE_W[k]
            token_order = _unfold(p_refs[k], slabs.pop(), PIECE_D[k]).astype(BF16) if PIECE_D[k] > 1 else None
            for c0 in range(0, w, 512):
                cw = min(512, w - c0)
                pz = p_refs[k][:, c0:c0 + cw] if token_order is None else token_order
                dz_ref[:, o + c0:o + c0 + cw] = pz
                db_ref[:, o + c0:o + c0 + cw] += jnp.sum(pz.astype(F32), axis=0, keepdims=True)
                du = du + _nn(pz, w_ref[o + c0:o + c0 + cw, :])
        xhat, r = _rms(h_ref[...], g_ref[...])
        dx, dg = _rms_bwd(du, xhat, r, g_ref[...])
        dh_ref[...] = dhr_ref[...] + dx
        dg_ref[...] += dg

    row = lambda w: pl.BlockSpec((tm, w), lambda i: (i, 0))
    full = lambda shp: pl.BlockSpec(shp, lambda i: (0, 0))
    return pl.pallas_call(
        body, name=name, grid=(t // tm,),
        in_specs=[pl.BlockSpec((tm // d, d * w), lambda i: (i, 0)) for w, d in zip(PIECE_W, PIECE_D)]
        + [row(D_MODEL), _resident(g), row(D_MODEL), _resident(w_t)],
        out_specs=[row(D_MODEL), row(D_IN), full((1, D_IN)), full((1, D_MODEL))],
        out_shape=[jax.ShapeDtypeStruct((t, D_MODEL), F32), jax.ShapeDtypeStruct((t, D_IN), BF16),
                   jax.ShapeDtypeStruct((1, D_IN), F32), jax.ShapeDtypeStruct((1, D_MODEL), F32)],
        scratch_shapes=[pltpu.VMEM((2, tm, 128), F32)] * sum(d > 1 for d in PIECE_D),
        compiler_params=_params("arbitrary"),
    )(*pieces, h, g, dh_res, w_t)


def _t5_bucket(dist):
    max_exact = N_BUCKETS // 2
    n = jnp.maximum(dist, 0)
    nf = jnp.maximum(n, 1).astype(jnp.float32)
    large = max_exact + (jnp.log(nf / max_exact) / math.log(MAX_DISTANCE / max_exact)
                         * (N_BUCKETS - max_exact)).astype(jnp.int32)
    large = jnp.minimum(large, N_BUCKETS - 1)
    return jnp.where(n < max_exact, n, large)


ATT_CFG = ((1, 128, 0, 4), (4, 128, 4, 4), (16, 128, 8, 4), (1, B_WINDOW - 1, A_HEADS, 8))


def _bucket_tiles():
    qi = jnp.arange(BLOCK)[:, None]
    ki = jnp.arange(2 * BLOCK)[None, :]
    dist = qi + BLOCK - ki
    return jnp.stack([_t5_bucket(dist * cfg[0]) for cfg in ATT_CFG]).astype(jnp.int32)


def _band(max_steps):
    row = lax.broadcasted_iota(jnp.int32, (BLOCK, 2 * BLOCK), 0)
    col = lax.broadcasted_iota(jnp.int32, (BLOCK, 2 * BLOCK), 1)
    dist = row + BLOCK - col
    return (dist >= 0) & (dist <= max_steps)


def _bias_build(table, buckets, hosted=()):
    def body(tab_ref, bt_ref, out_ref):
        col = lax.broadcasted_iota(jnp.int32, (BLOCK, 2 * BLOCK), 1)
        for ci, (_, max_steps, h0, nh) in enumerate(ATT_CFG):
            bt = bt_ref[ci]
            band = _band(max_steps)
            for h in range(h0, h0 + nh):
                acc = lax.fori_loop(0, N_BUCKETS, lambda b, acc: jnp.where(bt == b, tab_ref[b, h], acc),
                                    jnp.zeros((BLOCK, 2 * BLOCK), F32))
                out_ref[0, BLOCK * h:BLOCK * (h + 1), :] = jnp.where(band & (col >= BLOCK), acc, NEG)
                out_ref[1, BLOCK * h:BLOCK * (h + 1), :] = jnp.where(band, acc, NEG)

    return _call(
        body, name="bias_build", grid=(1,),
        in_specs=[pl.BlockSpec(memory_space=pltpu.SMEM), pl.BlockSpec(memory_space=pltpu.VMEM)],
        out_specs=[pl.BlockSpec(memory_space=pltpu.VMEM)],
        out_shape=[jax.ShapeDtypeStruct((2, TOTAL_HEADS * BLOCK, 2 * BLOCK), F32)],
        sem=("arbitrary",), args=(table, buckets), hosted=hosted)


def _bias_reduce(dbias, buckets, dsink_rows, hosted=()):
    def body(db_ref, bt_ref, ds_ref, out_ref, sink_ref):
        ri = lax.broadcasted_iota(jnp.int32, (N_BUCKETS, 128), 0)
        ci = lax.broadcasted_iota(jnp.int32, (N_BUCKETS, 128), 1)

        def per_bucket(b, acc):
            for cfg_i, (_, _, h0, nh) in enumerate(ATT_CFG):
                hit = bt_ref[cfg_i] == b
                for h in range(h0, h0 + nh):
                    val = jnp.sum(jnp.where(hit, db_ref[BLOCK * h:BLOCK * (h + 1), :], 0.0))
                    acc = jnp.where((ri == b) & (ci == h), val, acc)
            return acc

        out_ref[...] = lax.fori_loop(0, N_BUCKETS, per_bucket, jnp.zeros((N_BUCKETS, 128), F32))
        for h in range(8):
            sink_ref[h:h + 1, :] = jnp.sum(ds_ref[BLOCK * h:BLOCK * (h + 1), :], axis=0, keepdims=True)

    return _call(
        body, name="bias_reduce", grid=(1,),
        in_specs=[pl.BlockSpec(memory_space=pltpu.VMEM)] * 3,
        out_specs=[pl.BlockSpec(memory_space=pltpu.VMEM)] * 2,
        out_shape=[jax.ShapeDtypeStruct((N_BUCKETS, 128), F32), jax.ShapeDtypeStruct((8, 128), F32)],
        sem=("arbitrary",), args=(dbias, buckets, dsink_rows), hosted=hosted)


class _Att:
    def __init__(self, cfg_i):
        stride, _, h0, nh = ATT_CFG[cfg_i]
        self.d = stride if cfg_i < 3 else 1
        self.h0, self.nh = h0, nh
        self.row_w = QKV_W if self.d == 1 else 3 * 256
        if cfg_i < 3:
            self.nq, self.wkv = 1, 256
            self.q_unit = [cfg_i if self.d == 1 else 0]
            self.k_unit, self.v_unit = (3 + cfg_i, 6 + cfg_i) if self.d == 1 else (1, 2)
            self.sinks = False
        else:
            self.nq, self.wkv = 2, 128
            self.q_unit = [9, 10]
            self.k_unit, self.v_unit = 22, 23
            self.sinks = True
        self.wq = 256 * self.nq


def _att_in_specs(cf, bsz):
    uq, ukv = cf.row_w // 256, cf.row_w // cf.wkv
    specs = [pl.BlockSpec((bsz, BLOCK, 256), functools.partial(lambda r, j, u: (0, j, r * uq + u), u=u))
             for u in cf.q_unit]
    for unit in (cf.k_unit, cf.v_unit):
        specs.append(pl.BlockSpec((bsz, BLOCK, cf.wkv),
                                  functools.partial(lambda r, j, u: (0, jnp.maximum(j - 1, 0), r * ukv + u), u=unit)))
        specs.append(pl.BlockSpec((bsz, BLOCK, cf.wkv),
                                  functools.partial(lambda r, j, u: (0, j, r * ukv + u), u=unit)))
    for qb in range(cf.nq):
        specs.append(pl.BlockSpec((None, HEADS_PER_BLOCK * BLOCK, 2 * BLOCK), functools.partial(
            lambda r, j, u: (jnp.minimum(j, 1), u, 0), u=cf.h0 // HEADS_PER_BLOCK + qb)))
    if cf.sinks:
        specs += [pl.BlockSpec((HEADS_PER_BLOCK * BLOCK, 128), functools.partial(lambda r, j, u: (u, 0), u=qb))
                  for qb in range(cf.nq)]
    return specs


HEADS_PER_BLOCK = 4


def _head_masks(rows):
    head = lax.broadcasted_iota(jnp.int32, (rows, 256), 1) // HEAD_DIM
    return [head == h for h in range(HEADS_PER_BLOCK)]


def _stack_heads(x, masks):
    return jnp.concatenate([jnp.where(m, x, jnp.zeros_like(x)) for m in masks], axis=0)


def _unstack_heads(x4, masks):
    blocks = [x4[BLOCK * h:BLOCK * (h + 1)] for h in range(HEADS_PER_BLOCK)]
    return jnp.where(masks[0], blocks[0], jnp.where(masks[1], blocks[1], jnp.where(masks[2], blocks[2], blocks[3])))


def _row_value(x):
    return jnp.max(x, axis=-1, keepdims=True)


def _kv_operands(cf, x):
    if cf.wkv == 256:
        return [x]
    lane = lax.broadcasted_iota(jnp.int32, x.shape, 1)
    swapped = pltpu.roll(x, HEAD_DIM, 1)
    halves = [jnp.where(lane < HEAD_DIM, x, swapped), jnp.where(lane < HEAD_DIM, swapped, x)]
    return [jnp.concatenate([half, half], axis=1) for half in halves]


def _kv_fold(cf, grads):
    if cf.wkv == 256:
        return grads[0]
    folded = []
    for g in grads:
        x = g[:, :128] + g[:, 128:]
        folded.append(x + pltpu.roll(x, HEAD_DIM, 1))
    lane = lax.broadcasted_iota(jnp.int32, folded[0].shape, 1)
    return jnp.where(lane < HEAD_DIM, folded[0], folded[1])


def _attn_fwd(cf, zf, bias, sinks, name, hosted=()):
    bsz, l, _ = zf.shape
    nb = l // BLOCK

    def body(*refs):
        refs = list(refs)
        q_refs = [refs.pop(0) for _ in range(cf.nq)]
        kp_ref, kc_ref, vp_ref, vc_ref = [refs.pop(0) for _ in range(4)]
        bias_refs = [refs.pop(0) for _ in range(cf.nq)]
        sink_refs = [refs.pop(0) for _ in range(cf.nq)] if cf.sinks else None
        o_ref, lse_ref = refs
        masks, kv_masks = _head_masks(BLOCK), _head_masks(2 * BLOCK)
        sinks4 = [_row_value(ref[...]) for ref in sink_refs] if cf.sinks else None
        for bi in range(bsz):
            k = jnp.concatenate([kp_ref[bi], kc_ref[bi]], axis=0)
            v = jnp.concatenate([vp_ref[bi], vc_ref[bi]], axis=0)
            k_ops, v_ops = _kv_operands(cf, k), _kv_operands(cf, v)
            for qb in range(cf.nq):
                cols = slice(256 * qb, 256 * (qb + 1))
                kb, vb = k_ops[qb], v_ops[qb]
                q4 = _stack_heads(q_refs[qb][bi] * (HEAD_DIM ** -0.5), masks)
                s = _nt(q4, kb) + bias_refs[qb][...]
                m = jnp.max(s, axis=-1, keepdims=True)
                if cf.sinks:
                    sk = sinks4[qb]
                    m = jnp.maximum(m, sk)
                p = jnp.exp(s - m)
                den = jnp.sum(p, axis=-1, keepdims=True)
                if cf.sinks:
                    den = den + jnp.exp(sk - m)
                pn = (p * (1.0 / den)).astype(BF16)
                p_lanes = jnp.concatenate([pn[BLOCK * h:BLOCK * (h + 1)] for h in range(HEADS_PER_BLOCK)], axis=1)
                v4 = jnp.concatenate([jnp.where(mk, vb, jnp.zeros_like(vb)) for mk in kv_masks], axis=0)
                o_ref[bi, :, cols] = _nn(p_lanes, v4)
                lse_ref[bi, :, cols] = _unstack_heads(
                    jnp.broadcast_to(m + jnp.log(den), (HEADS_PER_BLOCK * BLOCK, 256)), masks)

    in_specs = _att_in_specs(cf, bsz)
    args = [zf] * (cf.nq + 4) + [bias] * cf.nq + ([sinks] * cf.nq if cf.sinks else [])
    out = pl.BlockSpec((bsz, BLOCK, cf.wq), lambda r, j: (0, j, r))
    shape = jax.ShapeDtypeStruct((bsz, l, cf.d * cf.wq), F32)
    return _call(
        body, name=name, grid=(cf.d, nb), in_specs=in_specs, out_specs=[out, out], out_shape=[shape, shape],
        sem=("parallel", "arbitrary"), args=args, hosted=hosted)


def _attn_bwd(cf, zf, bias, sinks, stats, dy, name, hosted=()):
    bsz, l, _ = zf.shape
    nb = l // BLOCK

    def body(*refs):
        refs = list(refs)
        q_refs = [refs.pop(0) for _ in range(cf.nq)]
        kp_ref, kc_ref, vp_ref, vc_ref = [refs.pop(0) for _ in range(4)]
        bias_refs = [refs.pop(0) for _ in range(cf.nq)]
        sink_refs = [refs.pop(0) for _ in range(cf.nq)] if cf.sinks else None
        st_ref, dy_ref, dq_ref, dk_ref, dv_ref, dbias_ref = [refs.pop(0) for _ in range(6)]
        dsink_ref = refs.pop(0) if cf.sinks else None
        dk_acc, dv_acc = refs
        r, j = pl.program_id(0), pl.program_id(1)

        @pl.when((r == 0) & (j == 0))
        def _():
            dbias_ref[...] = jnp.zeros_like(dbias_ref)
            if cf.sinks:
                dsink_ref[...] = jnp.zeros_like(dsink_ref)

        @pl.when(j == 0)
        def _():
            dk_acc[...] = jnp.zeros_like(dk_acc)
            dv_acc[...] = jnp.zeros_like(dv_acc)

        masks = _head_masks(BLOCK)
        cur = pl.ds(pl.multiple_of(j * BLOCK, BLOCK), BLOCK)
        prev = pl.ds(pl.multiple_of(jnp.maximum(j - 1, 0) * BLOCK, BLOCK), BLOCK)
        sinks4 = [_row_value(ref[...]) for ref in sink_refs] if cf.sinks else None
        ds_sum, dsink_sum = [None] * cf.nq, [None] * cf.nq
        for bi in range(bsz):
            k = jnp.concatenate([kp_ref[bi], kc_ref[bi]], axis=0)
            v = jnp.concatenate([vp_ref[bi], vc_ref[bi]], axis=0)
            dk_blocks, dv_blocks = [], []
            k_ops, v_ops = _kv_operands(cf, k), _kv_operands(cf, v)
            for qb in range(cf.nq):
                cols = slice(256 * qb, 256 * (qb + 1))
                kb, vb = k_ops[qb], v_ops[qb]
                q4 = _stack_heads(q_refs[qb][bi] * (HEAD_DIM ** -0.5), masks)
                st = st_ref[bi, :, 128 * qb:128 * (qb + 1)]
                lt4, e4 = _read_stats(st, 0), _read_stats(st, 1)
                pa = jnp.exp(_nt(q4, kb) + bias_refs[qb][...] - lt4)
                dy = dy_ref[bi, :, cols]
                dy4 = _stack_heads(dy, masks)
                ds = pa * (_nt(dy4, vb) - e4)
                ds_sum[qb] = ds if ds_sum[qb] is None else ds_sum[qb] + ds
                if cf.sinks:
                    dsk = jnp.exp(sinks4[qb] - lt4) * e4
                    dsink_sum[qb] = dsk if dsink_sum[qb] is None else dsink_sum[qb] + dsk
                dsb = ds.astype(BF16)
                dq_ref[bi, :, cols] = (_unstack_heads(_nn(dsb, kb), masks) * (HEAD_DIM ** -0.5)).astype(dq_ref.dtype)
                dk_blocks.append(_tn(dsb, q4))
                dv_blocks.append(_tn(pa.astype(BF16), dy4))
            dk_new, dv_new = _kv_fold(cf, dk_blocks), _kv_fold(cf, dv_blocks)
            dk_acc[bi, cur, :] += dk_new[BLOCK:]
            dv_acc[bi, cur, :] += dv_new[BLOCK:]
            dk_acc[bi, prev, :] += dk_new[:BLOCK]
            dv_acc[bi, prev, :] += dv_new[:BLOCK]
        for qb in range(cf.nq):
            rows = slice(HEADS_PER_BLOCK * BLOCK * qb, HEADS_PER_BLOCK * BLOCK * (qb + 1))
            dbias_ref[rows, :] += ds_sum[qb]
            if cf.sinks:
                dsink_ref[rows, :] -= dsink_sum[qb]

        @pl.when(j == nb - 1)
        def _():
            dk_ref[...] = dk_acc[...].astype(dk_ref.dtype)
            dv_ref[...] = dv_acc[...].astype(dv_ref.dtype)

    tok = pl.BlockSpec((bsz, BLOCK, cf.wq), lambda r, j: (0, j, r))
    in_specs = _att_in_specs(cf, bsz) + [pl.BlockSpec((bsz, BLOCK, cf.wq // 2), lambda r, j: (0, j, r)), tok]
    args = [zf] * (cf.nq + 4) + [bias] * cf.nq + ([sinks] * cf.nq if cf.sinks else []) + [stats, dy]
    seq = pl.BlockSpec((bsz, l, cf.wkv), lambda r, j: (0, 0, r))
    out_specs = [tok, seq, seq, pl.BlockSpec((cf.nh * BLOCK, 2 * BLOCK), lambda r, j: (0, 0))]
    grad_dtype = BF16 if cf.d == 1 else F32
    out_shape = [jax.ShapeDtypeStruct((bsz, l, cf.d * cf.wq), grad_dtype),
                 jax.ShapeDtypeStruct((bsz, l, cf.d * cf.wkv), grad_dtype),
                 jax.ShapeDtypeStruct((bsz, l, cf.d * cf.wkv), grad_dtype),
                 jax.ShapeDtypeStruct((cf.nh * BLOCK, 2 * BLOCK), F32)]
    if cf.sinks:
        out_specs.append(pl.BlockSpec((cf.nh * BLOCK, 128), lambda r, j: (0, 0)))
        out_shape.append(jax.ShapeDtypeStruct((cf.nh * BLOCK, 128), F32))
    return _call(
        body, name=name, grid=(cf.d, nb), in_specs=in_specs, out_specs=out_specs, out_shape=out_shape,
        scratch=[pltpu.VMEM((bsz, l, cf.wkv), F32), pltpu.VMEM((bsz, l, cf.wkv), F32)],
        sem=("arbitrary", "arbitrary"), args=args, hosted=hosted)


def _merge_fwd(o_a, lse_a, o_b, zg, h, wa_t, wb_t, wout, name):
    t = h.shape[0]
    tm = 512

    def body(o1, o2, o3, l1, l2, l3, ob_ref, zg_ref, h_ref, wa_ref, wb_ref, wo_ref, h2_ref, ya_ref, lt_ref, *slabs):
        o = [_unfold(ref, slabs[i], d) for i, (ref, d) in enumerate(zip((o1, o2, o3), DILATIONS))]
        l = [_unfold(ref, slabs[3 + i], d) for i, (ref, d) in enumerate(zip((l1, l2, l3), DILATIONS))]
        m = jnp.maximum(jnp.maximum(l[0], l[1]), l[2])
        e1, e2, e3 = jnp.exp(l[0] - m), jnp.exp(l[1] - m), jnp.exp(l[2] - m)
        se = e1 + e2 + e3
        ya = (e1 / se) * o[0] + (e2 / se) * o[1] + (e3 / se) * o[2]
        ya_ref[...] = ya
        lt_ref[...] = m + jnp.log(se)
        pa = _nt(ya.astype(BF16), wa_ref[...])
        pb = _nt(ob_ref[...].astype(BF16), wb_ref[...])
        merged = (jax.nn.sigmoid(zg_ref[:, :D_MODEL].astype(F32)) * pa
                  + jax.nn.sigmoid(zg_ref[:, D_MODEL:].astype(F32)) * pb)
        h2_ref[...] = h_ref[...] + _nn(merged.astype(BF16), wo_ref[...])

    row = lambda w: pl.BlockSpec((tm, w), lambda i: (i, 0))
    folded = [pl.BlockSpec((tm // d, d * 256), lambda i: (i, 0)) for d in DILATIONS]
    full = _resident
    return pl.pallas_call(
        body, name=name, grid=(t // tm,),
        in_specs=folded + folded + [row(512), row(GATE_W), row(D_MODEL), full(wa_t), full(wb_t), full(wout)],
        out_specs=[row(D_MODEL), row(256), row(256)],
        out_shape=[jax.ShapeDtypeStruct((t, D_MODEL), F32), jax.ShapeDtypeStruct((t, 256), F32),
                   jax.ShapeDtypeStruct((t, 256), F32)],
        scratch_shapes=[pltpu.VMEM((2, tm, 128), F32)] * 6,
        compiler_params=_params("parallel"),
    )(*o_a, *lse_a, o_b, zg, h, wa_t, wb_t, wout)


STAT_LANES = 32


def _head_stats(lt, dy, y):
    rows, w = dy.shape
    lane = lax.broadcasted_iota(jnp.int32, (rows, w), 1)
    out_lane = lax.broadcasted_iota(jnp.int32, (rows, w // 2), 1)
    prod = dy * y
    out = jnp.zeros((rows, w // 2), F32)
    for hd in range(w // HEAD_DIM):
        mine = lane // HEAD_DIM == hd
        dot = jnp.sum(jnp.where(mine, prod, 0.0), axis=-1, keepdims=True)
        both = jnp.where(out_lane % STAT_LANES < STAT_LANES // 2, _row_value(jnp.where(mine, lt, NEG)), dot)
        out = jnp.where(out_lane // STAT_LANES == hd, both, out)
    return out


def _read_stats(st, part):
    lane = lax.broadcasted_iota(jnp.int32, st.shape, 1)
    half = (lane % STAT_LANES < STAT_LANES // 2) == (part == 0)
    return jnp.concatenate([_row_value(jnp.where((lane // STAT_LANES == hd) & half, st, NEG))
                            for hd in range(HEADS_PER_BLOCK)], axis=0)


def _merge_bwd(dh, ya, lse_tot, o_b, lse_b, zg, wa_t, wb_t, wout, name, hosted=()):
    t = dh.shape[0]
    tm = 512

    def body(dh_ref, ya_ref, lt_ref, ob_ref, lb_ref, zg_ref, wa_ref, wb_ref, wo_ref,
             mg_ref, dpa_ref, dpb_ref, dzg_ref, dy1, dy2, dy3, st1, st2, st3, dyb_ref, stb_ref, *slabs):
        dm = _nt(dh_ref[...].astype(BF16), wo_ref[...])
        pa = _nt(ya_ref[...].astype(BF16), wa_ref[...])
        pb = _nt(ob_ref[...].astype(BF16), wb_ref[...])
        sa = jax.nn.sigmoid(zg_ref[:, :D_MODEL].astype(F32))
        sb = jax.nn.sigmoid(zg_ref[:, D_MODEL:].astype(F32))
        mg_ref[...] = (sa * pa + sb * pb).astype(BF16)
        dpa = (dm * sa).astype(BF16)
        dpb = (dm * sb).astype(BF16)
        dpa_ref[...] = dpa
        dpb_ref[...] = dpb
        dzg_ref[:, :D_MODEL] = (dm * pa * (sa * (1.0 - sa))).astype(BF16)
        dzg_ref[:, D_MODEL:] = (dm * pb * (sb * (1.0 - sb))).astype(BF16)
        dya = _nn(dpa, wa_ref[...])
        dyb = _nn(dpb, wb_ref[...])
        stats = _head_stats(lt_ref[...], dya, ya_ref[...])
        for i, d in enumerate(DILATIONS):
            _fold(dya, slabs[i], (dy1, dy2, dy3)[i], d)
            _fold(stats, slabs[3 + i], (st1, st2, st3)[i], d)
        dyb_ref[...] = dyb.astype(BF16)
        stb_ref[...] = _head_stats(lb_ref[...], dyb, ob_ref[...])

    row = lambda w: pl.BlockSpec((tm, w), lambda i: (i, 0))
    folded = [pl.BlockSpec((tm // d, d * 256), lambda i: (i, 0)) for d in DILATIONS]
    full = _resident
    sds = jax.ShapeDtypeStruct
    dy_shape = [sds((t // d, d * 256), BF16) for d in DILATIONS]
    st_shape = [sds((t // d, d * 128), F32) for d in DILATIONS]
    st_specs = [pl.BlockSpec((tm // d, d * 128), lambda i: (i, 0)) for d in DILATIONS]
    return _call(
        body, name=name, grid=(t // tm,),
        in_specs=[row(D_MODEL), row(256), row(256), row(512), row(512), row(GATE_W), full(wa_t), full(wb_t),
                  full(wout)],
        out_specs=[row(D_MODEL), row(D_MODEL), row(D_MODEL), row(GATE_W)] + folded + st_specs + [row(512), row(256)],
        out_shape=[sds((t, D_MODEL), BF16), sds((t, D_MODEL), BF16), sds((t, D_MODEL), BF16),
                   sds((t, GATE_W), BF16)] + dy_shape + st_shape + [sds((t, 512), BF16), sds((t, 256), F32)],
        scratch=[pltpu.VMEM((2, tm, 128), F32)] * 6,
        sem=("parallel",), args=(dh, ya, lse_tot, o_b, lse_b, zg, wa_t, wb_t, wout), hosted=hosted)


def _pair_sum(grad, got, name):
    _, _, r, cdim = grad.shape
    core = lax.axis_index("c").astype(jnp.int32).reshape(1)

    def body(core_ref, g_ref, s_ref, o_ref):
        o_ref[...] = (g_ref[...] + s_ref[...]).astype(BF16)

    return pl.pallas_call(
        body, name=name,
        grid_spec=pltpu.PrefetchScalarGridSpec(
            num_scalar_prefetch=1, grid=(N_CHIP,),
            in_specs=[pl.BlockSpec((None, None, r, cdim), lambda q, core_ref: (q, core_ref[0], 0, 0)),
                      pl.BlockSpec((None, None, r, cdim), lambda q, core_ref: (q, 0, 0, 0))],
            out_specs=pl.BlockSpec((None, r, cdim), lambda q, core_ref: (q, 0, 0))),
        out_shape=jax.ShapeDtypeStruct((N_CHIP, r, cdim), BF16),
        compiler_params=_params("parallel"),
    )(core, grad, got)


def _adamw_math(w, g, m, v):
    m = ADAM_B1 * m + (1.0 - ADAM_B1) * g
    v = ADAM_B2 * v + (1.0 - ADAM_B2) * jnp.square(g)
    m_hat = m / (1.0 - ADAM_B1 ** ADAM_STEP)
    v_hat = v / (1.0 - ADAM_B2 ** ADAM_STEP)
    delta = -ADAM_LR * (m_hat / (jnp.sqrt(v_hat) + ADAM_EPS) + ADAM_WD * w)
    return delta, m, v


def _update(w, parts, m, v, transposed, name, hosted=()):
    r, c = parts.shape[1:]

    def body(w_ref, p_ref, m_ref, v_ref, g_ref, d_ref, m2_ref, v2_ref):
        def total(rows):
            return ((p_ref[0, rows].astype(F32) + p_ref[1, rows].astype(F32)) + p_ref[2, rows].astype(F32)) \
                + p_ref[3, rows].astype(F32)

        def update(at, g):
            g_ref[at] = g
            d_ref[at], m2_ref[at], v2_ref[at] = _adamw_math(w_ref[at], g, m_ref[at], v_ref[at])

        if not transposed:
            update((slice(None), slice(None)), total(slice(None)))
            return
        for r0 in range(0, r, 128):
            n = min(128, r - r0)
            gt = total(slice(r0, r0 + n))
            if n < 128:
                gt = jnp.concatenate([gt, jnp.zeros((128 - n, c), F32)], axis=0)
            update((slice(None), slice(r0, r0 + n)), gt.T[:, :n])

    sds = jax.ShapeDtypeStruct(w.shape, F32)
    vm = pl.BlockSpec(memory_space=pltpu.VMEM)
    return _call(body, name=name, grid=(1,), in_specs=[vm] * 4, out_specs=[vm] * 4, out_shape=[sds] * 4,
                 sem=("arbitrary",), args=(w, parts, m, v), hosted=hosted)


SMALL_ROWS = 80


def _small_update(g, w, m, v, name):
    def body(g_ref, w_ref, m_ref, v_ref, gs_ref, d_ref, m2_ref, v2_ref, got_ref, send_sems, recv_sems):
        x, y, c = _place()
        me = 4 * x + 2 * y + c
        got_ref[me] = g_ref[...]
        copies = []
        for k in range(1, N_DEV):
            peer = (x ^ (k >> 2), y ^ ((k >> 1) & 1), c ^ (k & 1))
            cp = pltpu.make_async_remote_copy(
                src_ref=g_ref, dst_ref=got_ref.at[me], send_sem=send_sems.at[k - 1], recv_sem=recv_sems.at[k - 1],
                device_id=peer, device_id_type=MESH)
            cp.start()
            copies.append(cp)
        for cp in copies:
            cp.wait()
        total = got_ref[0]
        for k in range(1, N_DEV):
            total = total + got_ref[k]
        gs_ref[...] = total
        d_ref[...], m2_ref[...], v2_ref[...] = _adamw_math(w_ref[...], total, m_ref[...], v_ref[...])

    sds = jax.ShapeDtypeStruct((SMALL_ROWS, 128), F32)
    vm = pl.BlockSpec(memory_space=pltpu.VMEM)
    return pl.pallas_call(
        body, name=name, in_specs=[vm] * 4, out_specs=[vm] * 4, out_shape=[sds] * 4,
        scratch_shapes=[pltpu.VMEM((N_DEV, SMALL_ROWS, 128), F32), pltpu.SemaphoreType.DMA((N_DEV - 1,)),
                        pltpu.SemaphoreType.DMA((N_DEV - 1,))],
    )(g, w, m, v)


def _pack_small(gains, b_in, rel_bias, sinks, last):
    rows = [a.reshape(8, 128) for a in gains] + [b_in.reshape(40, 128), rel_bias.reshape(5, 128),
                                                 jnp.pad(sinks.reshape(1, 8), ((0, 0), (0, 120))), last]
    rows.append(jnp.zeros((SMALL_ROWS - 79, 128), F32))
    return jnp.concatenate(rows, axis=0)


def _unpack_small(p, like):
    out = [p[8 * i:8 * i + 8].reshape(like[i].shape) for i in range(4)]
    out.append(p[32:72].reshape(like[4].shape))
    out.append(p[72:77].reshape(like[5].shape))
    out.append(p[77, :8].reshape(like[6].shape))
    return out


def kernel(x, ffn1_norm, ffn1_w_gate, ffn1_w_up, ffn1_w_down, mix_norm, w_in, b_in, w_branch_a, w_branch_b, w_out, sinks, rel_bias, ffn2_norm, ffn2_w_gate, ffn2_w_up, ffn2_w_down, final_norm, loss_target, m_ffn1_norm, m_ffn1_w_gate, m_ffn1_w_up, m_ffn1_w_down, m_mix_norm, m_w_in, m_b_in, m_w_branch_a, m_w_branch_b, m_w_out, m_sinks, m_rel_bias, m_ffn2_norm, m_ffn2_w_gate, m_ffn2_w_up, m_ffn2_w_down, m_final_norm, v_ffn1_norm, v_ffn1_w_gate, v_ffn1_w_up, v_ffn1_w_down, v_mix_norm, v_w_in, v_b_in, v_w_branch_a, v_w_branch_b, v_w_out, v_sinks, v_rel_bias, v_ffn2_norm, v_ffn2_w_gate, v_ffn2_w_up, v_ffn2_w_down, v_final_norm):
    bsz, seq, _ = x.shape
    t = bsz * seq
    xt = x.reshape(t, D_MODEL)
    target = loss_target.reshape(t, D_MODEL)

    big = [("ffn1_w_gate", ffn1_w_gate, m_ffn1_w_gate, v_ffn1_w_gate, True),
           ("ffn1_w_up", ffn1_w_up, m_ffn1_w_up, v_ffn1_w_up, True),
           ("ffn1_w_down", ffn1_w_down, m_ffn1_w_down, v_ffn1_w_down, False),
           ("w_in", w_in, m_w_in, v_w_in, True),
           ("w_branch_a", w_branch_a, m_w_branch_a, v_w_branch_a, True),
           ("w_branch_b", w_branch_b, m_w_branch_b, v_w_branch_b, True),
           ("w_out", w_out, m_w_out, v_w_out, False),
           ("ffn2_w_gate", ffn2_w_gate, m_ffn2_w_gate, v_ffn2_w_gate, True),
           ("ffn2_w_up", ffn2_w_up, m_ffn2_w_up, v_ffn2_w_up, True),
           ("ffn2_w_down", ffn2_w_down, m_ffn2_w_down, v_ffn2_w_down, False)]
    shard = {nm: (w[0].T if tr else w[0]).astype(BF16) for nm, w, _, _, tr in big}
    full = {}

    def gather(names):
        return _Gather([shard[nm] for nm in names])

    def keep(names, got):
        for nm, gw in zip(names, got):
            full[nm] = gw.reshape(-1, gw.shape[-1])

    ffn1_names = ["ffn1_w_gate", "ffn1_w_up", "ffn1_w_down"]
    mix_names = ["w_in", "w_branch_a", "w_branch_b", "w_out"]
    ffn2_names = ["ffn2_w_gate", "ffn2_w_up", "ffn2_w_down"]
    g1, gm, g2, gf = ffn1_norm, mix_norm, ffn2_norm, final_norm.reshape(1, D_MODEL)

    buckets = _bucket_tiles()
    (bias,), (got,) = _bias_build(rel_bias, buckets, hosted=[gather(ffn1_names)])
    keep(ffn1_names, got)
    (h1, n1, a1, b1, hff1), (got,) = _ffn_fwd(xt, g1, full["ffn1_w_gate"], full["ffn1_w_up"], full["ffn1_w_down"], "ffn1_fwd",
                                hosted=[gather(mix_names)])
    keep(mix_names, got)
    (u, zq, zg, *zdil), (got,) = _inproj_fwd(h1, gm, full["w_in"], b_in, "inproj_fwd", hosted=[gather(ffn2_names[:2])])
    keep(ffn2_names[:2], got)
    sink_rows = jnp.broadcast_to(sinks.reshape(8, 1, 1), (8, BLOCK, 128)).reshape(8 * BLOCK, 128)
    cfs = [_Att(i) for i in range(4)]
    zfold = []
    for i, cf in enumerate(cfs):
        if cf.d == 1:
            zfold.append(zq.reshape(bsz, seq, QKV_W))
        else:
            zfold.append(zdil[i - 1].reshape(bsz, seq // cf.d, cf.d * cf.row_w))
    att = [None] * 4
    for i in (3, 0, 1, 2):
        cf = cfs[i]
        (o, lse), got = _attn_fwd(cf, zfold[i], bias, sink_rows, f"attn{i}_fwd",
                                  hosted=[gather(ffn2_names[2:])] if i == 3 else ())
        if i == 3:
            keep(ffn2_names[2:], got[0])
        att[i] = (o.reshape(t // cf.d, cf.d * cf.wq), lse.reshape(t // cf.d, cf.d * cf.wq))
    o_b, lse_b = att[3]
    h2, ya, lse_tot = _merge_fwd([a[0] for a in att[:3]], [a[1] for a in att[:3]], o_b, zg, h1,
                                 full["w_branch_a"], full["w_branch_b"], full["w_out"], "merge_fwd")
    (dh3, n2, a2, b2, hff2, loss_part, dgf), _ = _ffn_fwd(
        h2, g2, full["ffn2_w_gate"], full["ffn2_w_up"], full["ffn2_w_down"], "ffn2_fwd", head=(gf, target))

    grads, pair, from_chips = {}, {}, {}

    def by_owner(nm):
        return grads[nm].reshape(N_CHIP, 2, -1, grads[nm].shape[-1])

    def to_core(names):
        return _CoreExchange([by_owner(nm) for nm in names])

    def pair_up(names, got):
        for nm, sib in zip(names, got):
            pair[nm] = _pair_sum(by_owner(nm), sib, f"pair_sum_{nm}")

    def to_chips(names):
        return _ChipExchange([pair[nm] for nm in names])

    def landed(names, got):
        for nm, parts in zip(names, got):
            from_chips[nm] = parts

    dh2, dg2, da, db = _ffn_bwd(h2, a2, b2, g2, dh3, full["ffn2_w_gate"], full["ffn2_w_up"],
                                          full["ffn2_w_down"], "ffn2_bwd")
    grads["ffn2_w_gate"], _ = _tn_matmul(da, n2, 1408, "ffn2_dgate")
    grads["ffn2_w_up"], _ = _tn_matmul(db, n2, 1408, "ffn2_dup")
    grads["ffn2_w_down"], _ = _tn_matmul(hff2, dh3, 1408, "ffn2_ddown", scale=0.5)
    (merged, dpa, dpb, dzg, *cot), (got,) = _merge_bwd(
        dh2, ya, lse_tot, o_b, lse_b, zg, full["w_branch_a"], full["w_branch_b"], full["w_out"], "merge_bwd",
        hosted=[to_core(ffn2_names)])
    dys, sts = cot[0:3] + [cot[6]], cot[3:6] + [cot[7]]
    pair_up(ffn2_names, got)
    dq, dk, dv, dbias, dsink = [None] * 4, [None] * 4, [None] * 4, [None] * 4, None
    for i in (3, 0, 1, 2):
        cf = cfs[i]
        shp = (bsz, seq // cf.d, cf.d * cf.wq)
        hosted = {3: lambda: [to_chips(ffn2_names[:2])], 0: lambda: [to_chips(ffn2_names[2:])]}.get(i, list)()
        res, got = _attn_bwd(cf, zfold[i], bias, sink_rows, sts[i].reshape(shp[:2] + (shp[2] // 2,)), dys[i].reshape(shp),
                             f"attn{i}_bwd", hosted=hosted)
        if i == 3:
            landed(ffn2_names[:2], got[0])
        elif i == 0:
            landed(ffn2_names[2:], got[0])
        dq[i] = res[0].reshape(t // cf.d, cf.d * cf.wq)
        dk[i] = res[1].reshape(t // cf.d, cf.d * cf.wkv)
        dv[i] = res[2].reshape(t // cf.d, cf.d * cf.wkv)
        dbias[i] = res[3]
        if cf.sinks:
            dsink = res[4]
    pieces = dq[:3] + dk[:3] + dv[:3] + [dq[3], dk[3], dv[3], dzg]
    dh1, dz, db_in, dgm = _inproj_bwd(pieces, h1, gm, dh2, full["w_in"], "inproj_bwd")
    dx, dg1, da, db = _ffn_bwd(xt, a1, b1, g1, dh1, full["ffn1_w_gate"], full["ffn1_w_up"],
                                         full["ffn1_w_down"], "ffn1_bwd")
    grads["w_in"], _ = _tn_matmul(dz, u, 1280, "dw_in")
    grads["ffn1_w_down"], (got,) = _tn_matmul(hff1, dh1, 1408, "ffn1_ddown", hosted=[to_core(["w_in"])], scale=0.5)
    pair_up(["w_in"], got)
    grads["ffn1_w_gate"], got = _tn_matmul(da, n1, 1408, "ffn1_dgate",
                                           hosted=[to_chips(["w_in"]), to_core(["ffn1_w_down"])])
    landed(["w_in"], got[0])
    pair_up(["ffn1_w_down"], got[1])
    grads["ffn1_w_up"], got = _tn_matmul(db, n1, 1408, "ffn1_dup",
                                         hosted=[to_chips(["ffn1_w_down"]), to_core(["ffn1_w_gate"])])
    landed(["ffn1_w_down"], got[0])
    pair_up(["ffn1_w_gate"], got[1])
    grads["w_out"], got = _tn_matmul(merged, dh2, 1024, "dw_out",
                                     hosted=[to_chips(["ffn1_w_gate"]), to_core(["ffn1_w_up"])])
    landed(["ffn1_w_gate"], got[0])
    pair_up(["ffn1_w_up"], got[1])
    grads["w_branch_b"], got = _tn_matmul(dpb, o_b, 1024, "dw_branch_b",
                                          hosted=[to_chips(["ffn1_w_up"]), to_core(["w_out"])])
    landed(["ffn1_w_up"], got[0])
    pair_up(["w_out"], got[1])
    grads["w_branch_a"], got = _tn_matmul(dpa, ya, 1024, "dw_branch_a",
                                          hosted=[to_chips(["w_out"]), to_core(["w_branch_b"])])
    landed(["w_out"], got[0])
    pair_up(["w_branch_b"], got[1])
    (dtable, dsinks), got = _bias_reduce(jnp.concatenate(dbias, axis=0), buckets, dsink,
                                         hosted=[to_chips(["w_branch_b"]), to_core(["w_branch_a"])])
    landed(["w_branch_b"], got[0])
    pair_up(["w_branch_a"], got[1])

    out_g, out_d, out_m, out_v = {}, {}, {}, {}
    for idx, (nm, w, m, v, tr) in enumerate(big):
        around = tr and w.shape[-1] % 128 != 0
        wmv = [a[0].T if around else a[0] for a in (w, m, v)]
        res, got = _update(wmv[0], from_chips[nm], wmv[1], wmv[2], tr and not around, f"update_{nm}",
                           hosted=[to_chips(["w_branch_a"])] if idx == 0 else ())
        if idx == 0:
            landed(["w_branch_a"], got[0])
        out_g[nm], out_d[nm], out_m[nm], out_v[nm] = [(a.T if around else a)[None] for a in res]

    small = [("ffn1_norm", ffn1_norm, m_ffn1_norm, v_ffn1_norm), ("mix_norm", mix_norm, m_mix_norm, v_mix_norm),
             ("ffn2_norm", ffn2_norm, m_ffn2_norm, v_ffn2_norm), ("final_norm", final_norm, m_final_norm, v_final_norm),
             ("b_in", b_in, m_b_in, v_b_in), ("rel_bias", rel_bias, m_rel_bias, v_rel_bias),
             ("sinks", sinks, m_sinks, v_sinks)]
    zero_row = jnp.zeros((1, 128), F32)
    pack = lambda arrs, last: _pack_small(arrs[:4], arrs[4], arrs[5], arrs[6], last)
    g_small = pack([dg1, dgm, dg2, dgf, db_in, dtable[:, :TOTAL_HEADS], dsinks[:, 0]], loss_part)
    packed = [pack([s[k] for s in small], zero_row) for k in (1, 2, 3)]
    gs, ds, ms, vs = _small_update(g_small, *packed, "small_update")
    like = [s[1] for s in small]
    for nm_s, g_, d_, m_, v_ in zip([s[0] for s in small], _unpack_small(gs, like), _unpack_small(ds, like),
                                    _unpack_small(ms, like), _unpack_small(vs, like)):
        out_g[nm_s], out_d[nm_s], out_m[nm_s], out_v[nm_s] = g_, d_, m_, v_
    loss = gs[78, 0]

    order = ["ffn1_norm", "ffn1_w_gate", "ffn1_w_up", "ffn1_w_down", "mix_norm", "w_in", "b_in", "w_branch_a",
             "w_branch_b", "w_out", "sinks", "rel_bias", "ffn2_norm", "ffn2_w_gate", "ffn2_w_up", "ffn2_w_down",
             "final_norm"]
    return (loss, dx.reshape(x.shape), *[out_g[k] for k in order], *[out_d[k] for k in order],
            *[out_m[k] for k in order], *[out_v[k] for k in order])
```

```python
import functools
import math

import numpy as np
import jax
import jax.numpy as jnp
from jax import lax
from jax.experimental import pallas as pl
from jax.experimental.pallas import tpu as pltpu

D_MODEL = 1024
D_FF = 2816
FF_CHUNK = 256
HEAD_DIM = 64
BLOCK = 128
N_BUCKETS = 32
MAX_DISTANCE = 2048
A_HEADS = 12
TOTAL_HEADS = 20
DIL_GROUPS = ((128, 1), (512, 4), (2048, 16))
B_WINDOW = 128
QKV_W = 3072
GATE_W = 2048
D_IN = QKV_W + GATE_W
EPS = 1e-6
NEG = -1e30
N_DEV = 8
N_CHIP = 4
ADAM_LR, ADAM_B1, ADAM_B2, ADAM_EPS, ADAM_WD, ADAM_STEP = 0.001, 0.9, 0.999, 1e-08, 0.01, 10
VMEM_LIMIT = 56 * 1024 * 1024
MESH = pl.DeviceIdType.MESH
BF16 = jnp.bfloat16
F32 = jnp.float32
ANY = pl.BlockSpec(memory_space=pl.ANY)


def _params(*sem):
    return pltpu.CompilerParams(dimension_semantics=sem, vmem_limit_bytes=VMEM_LIMIT)


def _resident(a):
    return pl.BlockSpec(a.shape, lambda i: (0, 0), pipeline_mode=pl.Buffered(1))


def _place():
    return lax.axis_index("x"), lax.axis_index("y"), lax.axis_index("c")


class _Gather:
    def __init__(self, shards):
        self.ins = list(shards)
        n = self.n = len(shards)
        self.out_shape = [jax.ShapeDtypeStruct((N_DEV,) + s.shape, s.dtype) for s in shards]
        self.scratch = [pltpu.SemaphoreType.DMA((7 * n,)), pltpu.SemaphoreType.DMA((7 * n,)),
                        pltpu.SemaphoreType.DMA((n,))]

    def _copies(self, ins, outs, sems):
        send_sems, recv_sems, local_sems = sems
        x, y, c = _place()
        me, sibling = (x, y, c), (x, y, 1 - c)
        chips = [(1 - x, y), (x, 1 - y), (1 - x, 1 - y)]

        def copy(i, k, block, to, src=None):
            dst = outs[i].at[4 * block[0] + 2 * block[1] + block[2]]
            return pltpu.make_async_remote_copy(
                src_ref=dst if src is None else src, dst_ref=dst, send_sem=send_sems.at[7 * i + k],
                recv_sem=recv_sems.at[7 * i + k], device_id=to, device_id_type=MESH)

        n = self.n
        south = c == 0
        relayed = (jnp.where(south, 1 - x, x), jnp.where(south, y, 1 - y), c)
        relay_to = (jnp.where(south, x, 1 - x), jnp.where(south, 1 - y, y), c)
        mine = [pltpu.make_async_copy(ins[i], outs[i].at[4 * x + 2 * y + c], local_sems.at[i]) for i in range(n)]
        first = [copy(i, 0, me, sibling, src=ins[i]) for i in range(n)]
        first += [copy(i, 1 + j, me, (*chips[j], c), src=ins[i]) for i in range(n) for j in range(2)]
        landed = [copy(i, 1 + j, (*chips[j], c), me) for j in range(2) for i in range(n)]
        passed = [copy(i, 4 + j, (*chips[j], c), sibling) for j in range(2) for i in range(n)]
        relays = [copy(i, 3, relayed, relay_to) for i in range(n)]
        diag_landed = [copy(i, 3, (*chips[2], c), me) for i in range(n)]
        diag_passed = [copy(i, 6, (*chips[2], c), sibling) for i in range(n)]
        from_sibling = [copy(i, 0, sibling, me) for i in range(n)]
        from_sibling += [copy(i, 4 + j, (*chip, 1 - c), me) for i in range(n) for j, chip in enumerate(chips)]
        return mine, first, landed, passed, relays, diag_landed, diag_passed, from_sibling

    def start(self, ins, outs, sems):
        mine, first = self._copies(ins, outs, sems)[:2]
        for cp in mine + first:
            cp.start()

    def mid(self, ins, outs, sems):
        _, _, landed, passed, relays, _, _, _ = self._copies(ins, outs, sems)
        for got in landed:
            got.wait_recv()
        for cp in relays + passed:
            cp.start()

    def end(self, ins, outs, sems):
        mine, first, _, passed, relays, diag_landed, diag_passed, from_sibling = self._copies(ins, outs, sems)
        for got, fwd in zip(diag_landed, diag_passed):
            got.wait_recv()
            fwd.start()
        for cp in from_sibling:
            cp.wait_recv()
        for cp in first + passed + relays + diag_passed:
            cp.wait_send()
        for cp in mine:
            cp.wait()


class _CoreExchange:
    def __init__(self, grads):
        self.ins = list(grads)
        n = self.n = len(grads)
        self.out_shape = [jax.ShapeDtypeStruct((N_CHIP, 1) + g.shape[2:], g.dtype) for g in grads]
        self.scratch = [pltpu.SemaphoreType.DMA((n,)), pltpu.SemaphoreType.DMA((n,))]

    def _copies(self, ins, outs, sems):
        x, y, c = _place()
        return [pltpu.make_async_remote_copy(
            src_ref=ins[i].at[:, pl.ds(1 - c, 1)], dst_ref=outs[i], send_sem=sems[0].at[i],
            recv_sem=sems[1].at[i], device_id=(x, y, 1 - c), device_id_type=MESH) for i in range(self.n)]

    def start(self, ins, outs, sems):
        for cp in self._copies(ins, outs, sems):
            cp.start()

    mid = None

    def end(self, ins, outs, sems):
        for cp in self._copies(ins, outs, sems):
            cp.wait()


class _ChipExchange:
    def __init__(self, parts):
        self.ins = list(parts)
        n = self.n = len(parts)
        self.out_shape = [jax.ShapeDtypeStruct(p.shape, p.dtype) for p in parts]
        self.scratch = [pltpu.SemaphoreType.DMA((3 * n,)), pltpu.SemaphoreType.DMA((3 * n,)),
                        pltpu.SemaphoreType.DMA((n,))]

    def _copies(self, ins, outs, sems):
        send_sems, recv_sems, local_sems = sems
        x, y, c = _place()
        my_chip = 2 * x + y
        copies = []
        for i in range(self.n):
            copies.append(pltpu.make_async_copy(ins[i].at[my_chip], outs[i].at[my_chip], local_sems.at[i]))
            for k, (qx, qy) in enumerate([(1 - x, y), (x, 1 - y), (1 - x, 1 - y)]):
                copies.append(pltpu.make_async_remote_copy(
                    src_ref=ins[i].at[2 * qx + qy], dst_ref=outs[i].at[my_chip], send_sem=send_sems.at[3 * i + k],
                    recv_sem=recv_sems.at[3 * i + k], device_id=(qx, qy, c), device_id_type=MESH))
        return copies

    def start(self, ins, outs, sems):
        for cp in self._copies(ins, outs, sems):
            cp.start()

    mid = None

    def end(self, ins, outs, sems):
        for cp in self._copies(ins, outs, sems):
            cp.wait()


def _call(body, *, name, grid, in_specs, out_specs, out_shape, args, scratch=(), sem=None, hosted=()):
    n_in, n_out, n_scr = len(in_specs), len(out_specs), len(scratch)
    x_in = [len(p.ins) for p in hosted]
    x_scr = [len(p.scratch) for p in hosted]
    steps = int(np.prod(grid))

    def wrapped(*refs):
        refs = list(refs)
        ins, refs = refs[:n_in], refs[n_in:]
        x_ins = [[refs.pop(0) for _ in range(k)] for k in x_in]
        outs, refs = refs[:n_out], refs[n_out:]
        x_outs = [[refs.pop(0) for _ in range(k)] for k in x_in]
        scr, refs = refs[:n_scr], refs[n_scr:]
        x_sems = [[refs.pop(0) for _ in range(k)] for k in x_scr]
        step = 0
        for d in range(len(grid)):
            step = step * grid[d] + pl.program_id(d)

        def phase(which, at):
            fns = [(getattr(p, which), a) for p, a in zip(hosted, zip(x_ins, x_outs, x_sems)) if getattr(p, which)]
            if fns:
                @pl.when(step == at)
                def _():
                    for fn, a in fns:
                        fn(*a)

        phase("start", 0)
        if steps > 1:
            phase("mid", steps // 2)
        body(*ins, *outs, *scr)
        if steps == 1:
            phase("mid", 0)
        phase("end", steps - 1)

    results = pl.pallas_call(
        wrapped, name=name, grid=grid,
        in_specs=list(in_specs) + [ANY] * sum(x_in), out_specs=list(out_specs) + [ANY] * sum(x_in),
        out_shape=list(out_shape) + [s for p in hosted for s in p.out_shape],
        scratch_shapes=list(scratch) + [s for p in hosted for s in p.scratch],
        compiler_params=_params(*(("arbitrary",) * len(grid) if hosted else sem)),
    )(*args, *[a for p in hosted for a in p.ins])
    own, rest = list(results[:n_out]), list(results[n_out:])
    return own, [[rest.pop(0) for _ in range(k)] for k in x_in]


def _nt(a, b):
    return lax.dot_general(a, b, (((1,), (1,)), ((), ())), preferred_element_type=F32)


def _nn(a, b):
    return lax.dot_general(a, b, (((1,), (0,)), ((), ())), preferred_element_type=F32)


def _tn(a, b):
    return lax.dot_general(a, b, (((0,), (0,)), ((), ())), preferred_element_type=F32)


def _rms(x, g):
    r = lax.rsqrt(jnp.mean(x * x, axis=-1, keepdims=True) + EPS)
    return x * r, r


def _rms_bwd(dn, xhat, r, g):
    dg = jnp.sum(dn * xhat, axis=0, keepdims=True)
    dxh = dn * g
    dx = r * (dxh - xhat * jnp.mean(dxh * xhat, axis=-1, keepdims=True))
    return dx, dg


def _ffn_fwd(x, g, wg_t, wu_t, wd, name, hosted=(), head=None):
    t = x.shape[0]
    tm = 512 if head is None else 256

    def body(*refs):
        x_ref, g_ref, wg_ref, wu_ref, wd_ref = refs[:5]
        h_ref, n_ref, a_ref, b_ref, hff_ref = refs[-5 if head is None else -7:][:5]
        xhat, _ = _rms(x_ref[...], g_ref[...])
        n = (xhat * g_ref[...]).astype(BF16)
        n_ref[...] = n
        for c in range(0, D_FF, FF_CHUNK):
            cols = slice(c, c + FF_CHUNK)
            a = _nt(n, wg_ref[cols, :])
            b = _nt(n, wu_ref[cols, :])
            a_ref[:, cols] = a.astype(BF16)
            b_ref[:, cols] = b.astype(BF16)
            hff_ref[:, cols] = (a * jax.nn.sigmoid(a) * b).astype(BF16)
        h = x_ref[...] + 0.5 * _nn(hff_ref[...], wd_ref[...])
        if head is None:
            h_ref[...] = h
            return
        gf_ref, t_ref, loss_ref, dgf_ref = refs[5], refs[6], refs[-2], refs[-1]

        @pl.when(pl.program_id(0) == 0)
        def _():
            loss_ref[...] = jnp.zeros_like(loss_ref)
            dgf_ref[...] = jnp.zeros_like(dgf_ref)

        yhat, r = _rms(h, gf_ref[...])
        err = yhat * gf_ref[...] - t_ref[...]
        loss_ref[...] += 0.5 * jnp.sum(jnp.mean(err * err, axis=-1, keepdims=True), axis=0, keepdims=True)
        h_ref[...], dgf = _rms_bwd(err * (1.0 / D_MODEL), yhat, r, gf_ref[...])
        dgf_ref[...] += dgf

    row = pl.BlockSpec((tm, D_MODEL), lambda i: (i, 0))
    hid = pl.BlockSpec((tm, D_FF), lambda i: (i, 0))
    sds = jax.ShapeDtypeStruct
    in_specs = [row, _resident(g), _resident(wg_t), _resident(wu_t), _resident(wd)]
    out_specs = [row, row, hid, hid, hid]
    out_shape = [sds((t, D_MODEL), F32), sds((t, D_MODEL), BF16), sds((t, D_FF), BF16), sds((t, D_FF), BF16),
                 sds((t, D_FF), BF16)]
    args = (x, g, wg_t, wu_t, wd)
    if head is not None:
        in_specs += [_resident(head[0]), row]
        out_specs += [pl.BlockSpec((1, 128), lambda i: (0, 0)), pl.BlockSpec((1, D_MODEL), lambda i: (0, 0))]
        out_shape += [sds((1, 128), F32), sds((1, D_MODEL), F32)]
        args += tuple(head)
    return _call(body, name=name, grid=(t // tm,), in_specs=in_specs, out_specs=out_specs, out_shape=out_shape,
                 sem=("parallel",) if head is None else ("arbitrary",), args=args, hosted=hosted)


def _ffn_bwd(x, a_pre, b_pre, g, dh, wg_t, wu_t, wd, name):
    t = x.shape[0]
    tm = 512

    def body(x_ref, a_ref, b_ref, g_ref, dh_ref, wg_ref, wu_ref, wd_ref,
             dx_ref, dg_ref, da_ref, db_ref):
        @pl.when(pl.program_id(0) == 0)
        def _():
            dg_ref[...] = jnp.zeros_like(dg_ref)

        dhh = (0.5 * dh_ref[...]).astype(BF16)
        for c in range(0, D_FF, FF_CHUNK):
            cols = slice(c, c + FF_CHUNK)
            a = a_ref[:, cols].astype(F32)
            b = b_ref[:, cols].astype(F32)
            s = jax.nn.sigmoid(a)
            silu = a * s
            dhff = _nt(dhh, wd_ref[cols, :])
            da_ref[:, cols] = (dhff * b * (s * (1.0 + a * (1.0 - s)))).astype(BF16)
            db_ref[:, cols] = (dhff * silu).astype(BF16)
        dn = _nn(da_ref[...], wg_ref[...]) + _nn(db_ref[...], wu_ref[...])
        xhat, r = _rms(x_ref[...], g_ref[...])
        dx, dg = _rms_bwd(dn, xhat, r, g_ref[...])
        dx_ref[...] = dh_ref[...] + dx
        dg_ref[...] += dg

    row = pl.BlockSpec((tm, D_MODEL), lambda i: (i, 0))
    hid = pl.BlockSpec((tm, D_FF), lambda i: (i, 0))
    return pl.pallas_call(
        body, name=name, grid=(t // tm,),
        in_specs=[row, hid, hid, _resident(g), row, _resident(wg_t), _resident(wu_t), _resident(wd)],
        out_specs=[row, pl.BlockSpec((1, D_MODEL), lambda i: (0, 0)), hid, hid],
        out_shape=[jax.ShapeDtypeStruct((t, D_MODEL), F32), jax.ShapeDtypeStruct((1, D_MODEL), F32),
                   jax.ShapeDtypeStruct((t, D_FF), BF16), jax.ShapeDtypeStruct((t, D_FF), BF16)],
        compiler_params=_params("arbitrary"),
    )(x, a_pre, b_pre, g, dh, wg_t, wu_t, wd)


def _tn_matmul(a, b, rc, name, hosted=(), scale=None):
    t, r = a.shape
    c = b.shape[1]
    tk = min(t, 2048)

    def body(a_ref, b_ref, o_ref):
        @pl.when(pl.program_id(1) == 0)
        def _():
            o_ref[...] = jnp.zeros_like(o_ref)

        o_ref[...] += _tn(a_ref[...].astype(BF16), b_ref[...].astype(BF16))
        if scale is not None:
            @pl.when(pl.program_id(1) == t // tk - 1)
            def _():
                o_ref[...] *= scale

    (out,), got = _call(
        body, name=name, grid=(r // rc, t // tk),
        in_specs=[pl.BlockSpec((tk, rc), lambda i, k: (k, i)), pl.BlockSpec((tk, c), lambda i, k: (k, 0))],
        out_specs=[pl.BlockSpec((rc, c), lambda i, k: (i, 0))],
        out_shape=[jax.ShapeDtypeStruct((r, c), F32)],
        sem=("parallel", "arbitrary"), args=(a, b), hosted=hosted)
    return out, got


def _unfold(blk_ref, slab_ref, d):
    if d == 1:
        return blk_ref[...]
    n = blk_ref.shape[0]
    for r in range(d):
        for half in range(2):
            c0 = 256 * r + 128 * half
            slab_ref[half, pl.ds(r, n, stride=d), :] = blk_ref[:, c0:c0 + 128]
    return jnp.concatenate([slab_ref[0], slab_ref[1]], axis=1)


def _fold(x, slab_ref, out_ref, d):
    if d == 1:
        out_ref[...] = x.astype(out_ref.dtype)
        return
    n, w = out_ref.shape[0], x.shape[1]
    for part in range(w // 128):
        slab_ref[part] = x[:, 128 * part:128 * (part + 1)]
    for r in range(d):
        for part in range(w // 128):
            c0 = w * r + 128 * part
            out_ref[:, c0:c0 + 128] = slab_ref[part, pl.ds(r, n, stride=d), :].astype(out_ref.dtype)


DILATIONS = tuple(d for _, d in DIL_GROUPS)


PIECE_W = (256,) * 9 + (512, 128, 128, GATE_W)
PIECE_D = DILATIONS * 3 + (1, 1, 1, 1)


def _inproj_fwd(h, g, w_t, b_in, name, hosted=()):
    t = h.shape[0]
    tm, nc = 512, 256
    dilated = [(gi, d) for gi, d in enumerate(DILATIONS) if d > 1]

    def body(h_ref, g_ref, w_ref, b_ref, u_ref, zq_ref, zg_ref, *rest):
        zf_refs, slabs = rest[:len(dilated)], rest[len(dilated):]
        xhat, _ = _rms(h_ref[...], g_ref[...])
        u = (xhat * g_ref[...]).astype(BF16)
        u_ref[...] = u
        for c in range(D_IN // nc):
            z = _nt(u, w_ref[c * nc:(c + 1) * nc, :]) + b_ref[:, c * nc:(c + 1) * nc]
            if c < QKV_W // nc:
                zq_ref[:, c * nc:(c + 1) * nc] = z.astype(BF16)
            else:
                zg_ref[:, c * nc - QKV_W:(c + 1) * nc - QKV_W] = z.astype(BF16)
            part, gi = divmod(c, len(DILATIONS))
            for k, (gk, d) in enumerate(dilated):
                if part < 3 and gi == gk:
                    slab, n = slabs[3 * k + part], tm // d
                    slab[0] = z[:, :128]
                    slab[1] = z[:, 128:]
                    for r in range(d):
                        for half in range(2):
                            c0 = 768 * r + 256 * part + 128 * half
                            zf_refs[k][:, c0:c0 + 128] = slab[half, pl.ds(r, n, stride=d), :].astype(BF16)

    row = lambda w: pl.BlockSpec((tm, w), lambda i: (i, 0))
    full = _resident
    sds = jax.ShapeDtypeStruct
    return _call(
        body, name=name, grid=(t // tm,),
        in_specs=[row(D_MODEL), full(g), full(w_t), full(b_in)],
        out_specs=[row(D_MODEL), row(QKV_W), row(GATE_W)]
        + [pl.BlockSpec((tm // d, d * 768), lambda i: (i, 0)) for _, d in dilated],
        out_shape=[sds((t, D_MODEL), BF16), sds((t, QKV_W), BF16), sds((t, GATE_W), BF16)]
        + [sds((t // d, d * 768), BF16) for _, d in dilated],
        scratch=[pltpu.VMEM((2, tm, 128), F32)] * (3 * len(dilated)),
        sem=("parallel",), args=(h, g, w_t, b_in), hosted=hosted)


def _inproj_bwd(pieces, h, g, dh_res, w_t, name):
    t = h.shape[0]
    tm = 256
    npiece = len(PIECE_W)
    offs = np.concatenate([[0], np.cumsum(PIECE_W)]).tolist()

    def body(*refs):
        p_refs = refs[:npiece]
        h_ref, g_ref, dhr_ref, w_ref, dh_ref, dz_ref, db_ref, dg_ref = refs[npiece:npiece + 8]
        slabs = list(refs[npiece + 8:])
        i = pl.program_id(0)

        @pl.when(i == 0)
        def _():
            db_ref[...] = jnp.zeros_like(db_ref)
            dg_ref[...] = jnp.zeros_like(dg_ref)

        du = jnp.zeros((tm, D_MODEL), F32)
        for k in range(npiece):
            o, w = offs[k], PIECE_W[k]
            token_order = _unfold(p_refs[k], slabs.pop(), PIECE_D[k]).astype(BF16) if PIECE_D[k] > 1 else None
            for c0 in range(0, w, 512):
                cw = min(512, w - c0)
                pz = p_refs[k][:, c0:c0 + cw] if token_order is None else token_order
                dz_ref[:, o + c0:o + c0 + cw] = pz
                db_ref[:, o + c0:o + c0 + cw] += jnp.sum(pz.astype(F32), axis=0, keepdims=True)
                du = du + _nn(pz, w_ref[o + c0:o + c0 + cw, :])
        xhat, r = _rms(h_ref[...], g_ref[...])
        dx, dg = _rms_bwd(du, xhat, r, g_ref[...])
        dh_ref[...] = dhr_ref[...] + dx
        dg_ref[...] += dg

    row = lambda w: pl.BlockSpec((tm, w), lambda i: (i, 0))
    full = lambda shp: pl.BlockSpec(shp, lambda i: (0, 0))
    return pl.pallas_call(
        body, name=name, grid=(t // tm,),
        in_specs=[pl.BlockSpec((tm // d, d * w), lambda i: (i, 0)) for w, d in zip(PIECE_W, PIECE_D)]
        + [row(D_MODEL), _resident(g), row(D_MODEL), _resident(w_t)],
        out_specs=[row(D_MODEL), row(D_IN), full((1, D_IN)), full((1, D_MODEL))],
        out_shape=[jax.ShapeDtypeStruct((t, D_MODEL), F32), jax.ShapeDtypeStruct((t, D_IN), BF16),
                   jax.ShapeDtypeStruct((1, D_IN), F32), jax.ShapeDtypeStruct((1, D_MODEL), F32)],
        scratch_shapes=[pltpu.VMEM((2, tm, 128), F32)] * sum(d > 1 for d in PIECE_D),
        compiler_params=_params("arbitrary"),
    )(*pieces, h, g, dh_res, w_t)


def _t5_bucket(dist):
    max_exact = N_BUCKETS // 2
    n = jnp.maximum(dist, 0)
    nf = jnp.maximum(n, 1).astype(jnp.float32)
    large = max_exact + (jnp.log(nf / max_exact) / math.log(MAX_DISTANCE / max_exact)
                         * (N_BUCKETS - max_exact)).astype(jnp.int32)
    large = jnp.minimum(large, N_BUCKETS - 1)
    return jnp.where(n < max_exact, n, large)


ATT_CFG = ((1, 128, 0, 4), (4, 128, 4, 4), (16, 128, 8, 4), (1, B_WINDOW - 1, A_HEADS, 8))


def _bucket_tiles():
    qi = jnp.arange(BLOCK)[:, None]
    ki = jnp.arange(2 * BLOCK)[None, :]
    dist = qi + BLOCK - ki
    return jnp.stack([_t5_bucket(dist * cfg[0]) for cfg in ATT_CFG]).astype(jnp.int32)


def _band(max_steps):
    row = lax.broadcasted_iota(jnp.int32, (BLOCK, 2 * BLOCK), 0)
    col = lax.broadcasted_iota(jnp.int32, (BLOCK, 2 * BLOCK), 1)
    dist = row + BLOCK - col
    return (dist >= 0) & (dist <= max_steps)


def _bias_build(table, buckets, hosted=()):
    def body(tab_ref, bt_ref, out_ref):
        col = lax.broadcasted_iota(jnp.int32, (BLOCK, 2 * BLOCK), 1)
        for ci, (_, max_steps, h0, nh) in enumerate(ATT_CFG):
            bt = bt_ref[ci]
            band = _band(max_steps)
            for h in range(h0, h0 + nh):
                acc = lax.fori_loop(0, N_BUCKETS, lambda b, acc: jnp.where(bt == b, tab_ref[b, h], acc),
                                    jnp.zeros((BLOCK, 2 * BLOCK), F32))
                out_ref[0, BLOCK * h:BLOCK * (h + 1), :] = jnp.where(band & (col >= BLOCK), acc, NEG)
                out_ref[1, BLOCK * h:BLOCK * (h + 1), :] = jnp.where(band, acc, NEG)

    return _call(
        body, name="bias_build", grid=(1,),
        in_specs=[pl.BlockSpec(memory_space=pltpu.SMEM), pl.BlockSpec(memory_space=pltpu.VMEM)],
        out_specs=[pl.BlockSpec(memory_space=pltpu.VMEM)],
        out_shape=[jax.ShapeDtypeStruct((2, TOTAL_HEADS * BLOCK, 2 * BLOCK), F32)],
        sem=("arbitrary",), args=(table, buckets), hosted=hosted)


def _bias_reduce(dbias, buckets, dsink_rows, hosted=()):
    def body(db_ref, bt_ref, ds_ref, out_ref, sink_ref):
        ri = lax.broadcasted_iota(jnp.int32, (N_BUCKETS, 128), 0)
        ci = lax.broadcasted_iota(jnp.int32, (N_BUCKETS, 128), 1)

        def per_bucket(b, acc):
            for cfg_i, (_, _, h0, nh) in enumerate(ATT_CFG):
                hit = bt_ref[cfg_i] == b
                for h in range(h0, h0 + nh):
                    val = jnp.sum(jnp.where(hit, db_ref[BLOCK * h:BLOCK * (h + 1), :], 0.0))
                    acc = jnp.where((ri == b) & (ci == h), val, acc)
            return acc

        out_ref[...] = lax.fori_loop(0, N_BUCKETS, per_bucket, jnp.zeros((N_BUCKETS, 128), F32))
        for h in range(8):
            sink_ref[h:h + 1, :] = jnp.sum(ds_ref[BLOCK * h:BLOCK * (h + 1), :], axis=0, keepdims=True)

    return _call(
        body, name="bias_reduce", grid=(1,),
        in_specs=[pl.BlockSpec(memory_space=pltpu.VMEM)] * 3,
        out_specs=[pl.BlockSpec(memory_space=pltpu.VMEM)] * 2,
        out_shape=[jax.ShapeDtypeStruct((N_BUCKETS, 128), F32), jax.ShapeDtypeStruct((8, 128), F32)],
        sem=("arbitrary",), args=(dbias, buckets, dsink_rows), hosted=hosted)


class _Att:
    def __init__(self, cfg_i):
        stride, _, h0, nh = ATT_CFG[cfg_i]
        self.d = stride if cfg_i < 3 else 1
        self.h0, self.nh = h0, nh
        self.row_w = QKV_W if self.d == 1 else 3 * 256
        if cfg_i < 3:
            self.nq, self.wkv = 1, 256
            self.q_unit = [cfg_i if self.d == 1 else 0]
            self.k_unit, self.v_unit = (3 + cfg_i, 6 + cfg_i) if self.d == 1 else (1, 2)
            self.sinks = False
        else:
            self.nq, self.wkv = 2, 128
            self.q_unit = [9, 10]
            self.k_unit, self.v_unit = 22, 23
            self.sinks = True
        self.wq = 256 * self.nq


def _att_in_specs(cf, bsz):
    uq, ukv = cf.row_w // 256, cf.row_w // cf.wkv
    specs = [pl.BlockSpec((bsz, BLOCK, 256), functools.partial(lambda r, j, u: (0, j, r * uq + u), u=u))
             for u in cf.q_unit]
    for unit in (cf.k_unit, cf.v_unit):
        specs.append(pl.BlockSpec((bsz, BLOCK, cf.wkv),
                                  functools.partial(lambda r, j, u: (0, jnp.maximum(j - 1, 0), r * ukv + u), u=unit)))
        specs.append(pl.BlockSpec((bsz, BLOCK, cf.wkv),
                                  functools.partial(lambda r, j, u: (0, j, r * ukv + u), u=unit)))
    for qb in range(cf.nq):
        specs.append(pl.BlockSpec((None, HEADS_PER_BLOCK * BLOCK, 2 * BLOCK), functools.partial(
            lambda r, j, u: (jnp.minimum(j, 1), u, 0), u=cf.h0 // HEADS_PER_BLOCK + qb)))
    if cf.sinks:
        specs += [pl.BlockSpec((HEADS_PER_BLOCK * BLOCK, 128), functools.partial(lambda r, j, u: (u, 0), u=qb))
                  for qb in range(cf.nq)]
    return specs


HEADS_PER_BLOCK = 4


def _head_masks(rows):
    head = lax.broadcasted_iota(jnp.int32, (rows, 256), 1) // HEAD_DIM
    return [head == h for h in range(HEADS_PER_BLOCK)]


def _stack_heads(x, masks):
    return jnp.concatenate([jnp.where(m, x, jnp.zeros_like(x)) for m in masks], axis=0)


def _unstack_heads(x4, masks):
    blocks = [x4[BLOCK * h:BLOCK * (h + 1)] for h in range(HEADS_PER_BLOCK)]
    return jnp.where(masks[0], blocks[0], jnp.where(masks[1], blocks[1], jnp.where(masks[2], blocks[2], blocks[3])))


def _row_value(x):
    return jnp.max(x, axis=-1, keepdims=True)


def _kv_operands(cf, x):
    if cf.wkv == 256:
        return [x]
    lane = lax.broadcasted_iota(jnp.int32, x.shape, 1)
    swapped = pltpu.roll(x, HEAD_DIM, 1)
    halves = [jnp.where(lane < HEAD_DIM, x, swapped), jnp.where(lane < HEAD_DIM, swapped, x)]
    return [jnp.concatenate([half, half], axis=1) for half in halves]


def _kv_fold(cf, grads):
    if cf.wkv == 256:
        return grads[0]
    folded = []
    for g in grads:
        x = g[:, :128] + g[:, 128:]
        folded.append(x + pltpu.roll(x, HEAD_DIM, 1))
    lane = lax.broadcasted_iota(jnp.int32, folded[0].shape, 1)
    return jnp.where(lane < HEAD_DIM, folded[0], folded[1])


def _attn_fwd(cf, zf, bias, sinks, name, hosted=()):
    bsz, l, _ = zf.shape
    nb = l // BLOCK

    def body(*refs):
        refs = list(refs)
        q_refs = [refs.pop(0) for _ in range(cf.nq)]
        kp_ref, kc_ref, vp_ref, vc_ref = [refs.pop(0) for _ in range(4)]
        bias_refs = [refs.pop(0) for _ in range(cf.nq)]
        sink_refs = [refs.pop(0) for _ in range(cf.nq)] if cf.sinks else None
        o_ref, lse_ref = refs
        masks, kv_masks = _head_masks(BLOCK), _head_masks(2 * BLOCK)
        sinks4 = [_row_value(ref[...]) for ref in sink_refs] if cf.sinks else None
        for bi in range(bsz):
            k = jnp.concatenate([kp_ref[bi], kc_ref[bi]], axis=0)
            v = jnp.concatenate([vp_ref[bi], vc_ref[bi]], axis=0)
            k_ops, v_ops = _kv_operands(cf, k), _kv_operands(cf, v)
            for qb in range(cf.nq):
                cols = slice(256 * qb, 256 * (qb + 1))
                kb, vb = k_ops[qb], v_ops[qb]
                q4 = _stack_heads(q_refs[qb][bi] * (HEAD_DIM ** -0.5), masks)
                s = _nt(q4, kb) + bias_refs[qb][...]
                m = jnp.max(s, axis=-1, keepdims=True)
                if cf.sinks:
                    sk = sinks4[qb]
                    m = jnp.maximum(m, sk)
                p = jnp.exp(s - m)
                den = jnp.sum(p, axis=-1, keepdims=True)
                if cf.sinks:
                    den = den + jnp.exp(sk - m)
                pn = (p * (1.0 / den)).astype(BF16)
                p_lanes = jnp.concatenate([pn[BLOCK * h:BLOCK * (h + 1)] for h in range(HEADS_PER_BLOCK)], axis=1)
                v4 = jnp.concatenate([jnp.where(mk, vb, jnp.zeros_like(vb)) for mk in kv_masks], axis=0)
                o_ref[bi, :, cols] = _nn(p_lanes, v4)
                lse_ref[bi, :, cols] = _unstack_heads(
                    jnp.broadcast_to(m + jnp.log(den), (HEADS_PER_BLOCK * BLOCK, 256)), masks)

    in_specs = _att_in_specs(cf, bsz)
    args = [zf] * (cf.nq + 4) + [bias] * cf.nq + ([sinks] * cf.nq if cf.sinks else [])
    out = pl.BlockSpec((bsz, BLOCK, cf.wq), lambda r, j: (0, j, r))
    shape = jax.ShapeDtypeStruct((bsz, l, cf.d * cf.wq), F32)
    return _call(
        body, name=name, grid=(cf.d, nb), in_specs=in_specs, out_specs=[out, out], out_shape=[shape, shape],
        sem=("parallel", "arbitrary"), args=args, hosted=hosted)


def _attn_bwd(cf, zf, bias, sinks, stats, dy, name, hosted=()):
    bsz, l, _ = zf.shape
    nb = l // BLOCK

    def body(*refs):
        refs = list(refs)
        q_refs = [refs.pop(0) for _ in range(cf.nq)]
        kp_ref, kc_ref, vp_ref, vc_ref = [refs.pop(0) for _ in range(4)]
        bias_refs = [refs.pop(0) for _ in range(cf.nq)]
        sink_refs = [refs.pop(0) for _ in range(cf.nq)] if cf.sinks else None
        st_ref, dy_ref, dq_ref, dk_ref, dv_ref, dbias_ref = [refs.pop(0) for _ in range(6)]
        dsink_ref = refs.pop(0) if cf.sinks else None
        dk_acc, dv_acc = refs
        r, j = pl.program_id(0), pl.program_id(1)

        @pl.when((r == 0) & (j == 0))
        def _():
            dbias_ref[...] = jnp.zeros_like(dbias_ref)
            if cf.sinks:
                dsink_ref[...] = jnp.zeros_like(dsink_ref)

        @pl.when(j == 0)
        def _():
            dk_acc[...] = jnp.zeros_like(dk_acc)
            dv_acc[...] = jnp.zeros_like(dv_acc)

        masks = _head_masks(BLOCK)
        cur = pl.ds(pl.multiple_of(j * BLOCK, BLOCK), BLOCK)
        prev = pl.ds(pl.multiple_of(jnp.maximum(j - 1, 0) * BLOCK, BLOCK), BLOCK)
        sinks4 = [_row_value(ref[...]) for ref in sink_refs] if cf.sinks else None
        ds_sum, dsink_sum = [None] * cf.nq, [None] * cf.nq
        for bi in range(bsz):
            k = jnp.concatenate([kp_ref[bi], kc_ref[bi]], axis=0)
            v = jnp.concatenate([vp_ref[bi], vc_ref[bi]], axis=0)
            dk_blocks, dv_blocks = [], []
            k_ops, v_ops = _kv_operands(cf, k), _kv_operands(cf, v)
            for qb in range(cf.nq):
                cols = slice(256 * qb, 256 * (qb + 1))
                kb, vb = k_ops[qb], v_ops[qb]
                q4 = _stack_heads(q_refs[qb][bi] * (HEAD_DIM ** -0.5), masks)
                st = st_ref[bi, :, 128 * qb:128 * (qb + 1)]
                lt4, e4 = _read_stats(st, 0), _read_stats(st, 1)
                pa = jnp.exp(_nt(q4, kb) + bias_refs[qb][...] - lt4)
                dy = dy_ref[bi, :, cols]
                dy4 = _stack_heads(dy, masks)
                ds = pa * (_nt(dy4, vb) - e4)
                ds_sum[qb] = ds if ds_sum[qb] is None else ds_sum[qb] + ds
                if cf.sinks:
                    dsk = jnp.exp(sinks4[qb] - lt4) * e4
                    dsink_sum[qb] = dsk if dsink_sum[qb] is None else dsink_sum[qb] + dsk
                dsb = ds.astype(BF16)
                dq_ref[bi, :, cols] = (_unstack_heads(_nn(dsb, kb), masks) * (HEAD_DIM ** -0.5)).astype(dq_ref.dtype)
                dk_blocks.append(_tn(dsb, q4))
                dv_blocks.append(_tn(pa.astype(BF16), dy4))
            dk_new, dv_new = _kv_fold(cf, dk_blocks), _kv_fold(cf, dv_blocks)
            dk_acc[bi, cur, :] += dk_new[BLOCK:]
            dv_acc[bi, cur, :] += dv_new[BLOCK:]
            dk_acc[bi, prev, :] += dk_new[:BLOCK]
            dv_acc[bi, prev, :] += dv_new[:BLOCK]
        for qb in range(cf.nq):
            rows = slice(HEADS_PER_BLOCK * BLOCK * qb, HEADS_PER_BLOCK * BLOCK * (qb + 1))
            dbias_ref[rows, :] += ds_sum[qb]
            if cf.sinks:
                dsink_ref[rows, :] -= dsink_sum[qb]

        @pl.when(j == nb - 1)
        def _():
            dk_ref[...] = dk_acc[...].astype(dk_ref.dtype)
            dv_ref[...] = dv_acc[...].astype(dv_ref.dtype)

    tok = pl.BlockSpec((bsz, BLOCK, cf.wq), lambda r, j: (0, j, r))
    in_specs = _att_in_specs(cf, bsz) + [pl.BlockSpec((bsz, BLOCK, cf.wq // 2), lambda r, j: (0, j, r)), tok]
    args = [zf] * (cf.nq + 4) + [bias] * cf.nq + ([sinks] * cf.nq if cf.sinks else []) + [stats, dy]
    seq = pl.BlockSpec((bsz, l, cf.wkv), lambda r, j: (0, 0, r))
    out_specs = [tok, seq, seq, pl.BlockSpec((cf.nh * BLOCK, 2 * BLOCK), lambda r, j: (0, 0))]
    grad_dtype = BF16 if cf.d == 1 else F32
    out_shape = [jax.ShapeDtypeStruct((bsz, l, cf.d * cf.wq), grad_dtype),
                 jax.ShapeDtypeStruct((bsz, l, cf.d * cf.wkv), grad_dtype),
                 jax.ShapeDtypeStruct((bsz, l, cf.d * cf.wkv), grad_dtype),
                 jax.ShapeDtypeStruct((cf.nh * BLOCK, 2 * BLOCK), F32)]
    if cf.sinks:
        out_specs.append(pl.BlockSpec((cf.nh * BLOCK, 128), lambda r, j: (0, 0)))
        out_shape.append(jax.ShapeDtypeStruct((cf.nh * BLOCK, 128), F32))
    return _call(
        body, name=name, grid=(cf.d, nb), in_specs=in_specs, out_specs=out_specs, out_shape=out_shape,
        scratch=[pltpu.VMEM((bsz, l, cf.wkv), F32), pltpu.VMEM((bsz, l, cf.wkv), F32)],
        sem=("arbitrary", "arbitrary"), args=args, hosted=hosted)


def _merge_fwd(o_a, lse_a, o_b, zg, h, wa_t, wb_t, wout, name):
    t = h.shape[0]
    tm = 512

    def body(o1, o2, o3, l1, l2, l3, ob_ref, zg_ref, h_ref, wa_ref, wb_ref, wo_ref, h2_ref, ya_ref, lt_ref, *slabs):
        o = [_unfold(ref, slabs[i], d) for i, (ref, d) in enumerate(zip((o1, o2, o3), DILATIONS))]
        l = [_unfold(ref, slabs[3 + i], d) for i, (ref, d) in enumerate(zip((l1, l2, l3), DILATIONS))]
        m = jnp.maximum(jnp.maximum(l[0], l[1]), l[2])
        e1, e2, e3 = jnp.exp(l[0] - m), jnp.exp(l[1] - m), jnp.exp(l[2] - m)
        se = e1 + e2 + e3
        ya = (e1 / se) * o[0] + (e2 / se) * o[1] + (e3 / se) * o[2]
        ya_ref[...] = ya
        lt_ref[...] = m + jnp.log(se)
        pa = _nt(ya.astype(BF16), wa_ref[...])
        pb = _nt(ob_ref[...].astype(BF16), wb_ref[...])
        merged = (jax.nn.sigmoid(zg_ref[:, :D_MODEL].astype(F32)) * pa
                  + jax.nn.sigmoid(zg_ref[:, D_MODEL:].astype(F32)) * pb)
        h2_ref[...] = h_ref[...] + _nn(merged.astype(BF16), wo_ref[...])

    row = lambda w: pl.BlockSpec((tm, w), lambda i: (i, 0))
    folded = [pl.BlockSpec((tm // d, d * 256), lambda i: (i, 0)) for d in DILATIONS]
    full = _resident
    return pl.pallas_call(
        body, name=name, grid=(t // tm,),
        in_specs=folded + folded + [row(512), row(GATE_W), row(D_MODEL), full(wa_t), full(wb_t), full(wout)],
        out_specs=[row(D_MODEL), row(256), row(256)],
        out_shape=[jax.ShapeDtypeStruct((t, D_MODEL), F32), jax.ShapeDtypeStruct((t, 256), F32),
                   jax.ShapeDtypeStruct((t, 256), F32)],
        scratch_shapes=[pltpu.VMEM((2, tm, 128), F32)] * 6,
        compiler_params=_params("parallel"),
    )(*o_a, *lse_a, o_b, zg, h, wa_t, wb_t, wout)


STAT_LANES = 32


def _head_stats(lt, dy, y):
    rows, w = dy.shape
    lane = lax.broadcasted_iota(jnp.int32, (rows, w), 1)
    out_lane = lax.broadcasted_iota(jnp.int32, (rows, w // 2), 1)
    prod = dy * y
    out = jnp.zeros((rows, w // 2), F32)
    for hd in range(w // HEAD_DIM):
        mine = lane // HEAD_DIM == hd
        dot = jnp.sum(jnp.where(mine, prod, 0.0), axis=-1, keepdims=True)
        both = jnp.where(out_lane % STAT_LANES < STAT_LANES // 2, _row_value(jnp.where(mine, lt, NEG)), dot)
        out = jnp.where(out_lane // STAT_LANES == hd, both, out)
    return out


def _read_stats(st, part):
    lane = lax.broadcasted_iota(jnp.int32, st.shape, 1)
    half = (lane % STAT_LANES < STAT_LANES // 2) == (part == 0)
    return jnp.concatenate([_row_value(jnp.where((lane // STAT_LANES == hd) & half, st, NEG))
                            for hd in range(HEADS_PER_BLOCK)], axis=0)


def _merge_bwd(dh, ya, lse_tot, o_b, lse_b, zg, wa_t, wb_t, wout, name, hosted=()):
    t = dh.shape[0]
    tm = 512

    def body(dh_ref, ya_ref, lt_ref, ob_ref, lb_ref, zg_ref, wa_ref, wb_ref, wo_ref,
             mg_ref, dpa_ref, dpb_ref, dzg_ref, dy1, dy2, dy3, st1, st2, st3, dyb_ref, stb_ref, *slabs):
        dm = _nt(dh_ref[...].astype(BF16), wo_ref[...])
        pa = _nt(ya_ref[...].astype(BF16), wa_ref[...])
        pb = _nt(ob_ref[...].astype(BF16), wb_ref[...])
        sa = jax.nn.sigmoid(zg_ref[:, :D_MODEL].astype(F32))
        sb = jax.nn.sigmoid(zg_ref[:, D_MODEL:].astype(F32))
        mg_ref[...] = (sa * pa + sb * pb).astype(BF16)
        dpa = (dm * sa).astype(BF16)
        dpb = (dm * sb).astype(BF16)
        dpa_ref[...] = dpa
        dpb_ref[...] = dpb
        dzg_ref[:, :D_MODEL] = (dm * pa * (sa * (1.0 - sa))).astype(BF16)
        dzg_ref[:, D_MODEL:] = (dm * pb * (sb * (1.0 - sb))).astype(BF16)
        dya = _nn(dpa, wa_ref[...])
        dyb = _nn(dpb, wb_ref[...])
        stats = _head_stats(lt_ref[...], dya, ya_ref[...])
        for i, d in enumerate(DILATIONS):
            _fold(dya, slabs[i], (dy1, dy2, dy3)[i], d)
            _fold(stats, slabs[3 + i], (st1, st2, st3)[i], d)
        dyb_ref[...] = dyb.astype(BF16)
        stb_ref[...] = _head_stats(lb_ref[...], dyb, ob_ref[...])

    row = lambda w: pl.BlockSpec((tm, w), lambda i: (i, 0))
    folded = [pl.BlockSpec((tm // d, d * 256), lambda i: (i, 0)) for d in DILATIONS]
    full = _resident
    sds = jax.ShapeDtypeStruct
    dy_shape = [sds((t // d, d * 256), BF16) for d in DILATIONS]
    st_shape = [sds((t // d, d * 128), F32) for d in DILATIONS]
    st_specs = [pl.BlockSpec((tm // d, d * 128), lambda i: (i, 0)) for d in DILATIONS]
    return _call(
        body, name=name, grid=(t // tm,),
        in_specs=[row(D_MODEL), row(256), row(256), row(512), row(512), row(GATE_W), full(wa_t), full(wb_t),
                  full(wout)],
        out_specs=[row(D_MODEL), row(D_MODEL), row(D_MODEL), row(GATE_W)] + folded + st_specs + [row(512), row(256)],
        out_shape=[sds((t, D_MODEL), BF16), sds((t, D_MODEL), BF16), sds((t, D_MODEL), BF16),
                   sds((t, GATE_W), BF16)] + dy_shape + st_shape + [sds((t, 512), BF16), sds((t, 256), F32)],
        scratch=[pltpu.VMEM((2, tm, 128), F32)] * 6,
        sem=("parallel",), args=(dh, ya, lse_tot, o_b, lse_b, zg, wa_t, wb_t, wout), hosted=hosted)


def _pair_sum(grad, got, name):
    _, _, r, cdim = grad.shape
    core = lax.axis_index("c").astype(jnp.int32).reshape(1)

    def body(core_ref, g_ref, s_ref, o_ref):
        o_ref[...] = (g_ref[...] + s_ref[...]).astype(BF16)

    return pl.pallas_call(
        body, name=name,
        grid_spec=pltpu.PrefetchScalarGridSpec(
            num_scalar_prefetch=1, grid=(N_CHIP,),
            in_specs=[pl.BlockSpec((None, None, r, cdim), lambda q, core_ref: (q, core_ref[0], 0, 0)),
                      pl.BlockSpec((None, None, r, cdim), lambda q, core_ref: (q, 0, 0, 0))],
            out_specs=pl.BlockSpec((None, r, cdim), lambda q, core_ref: (q, 0, 0))),
        out_shape=jax.ShapeDtypeStruct((N_CHIP, r, cdim), BF16),
        compiler_params=_params("parallel"),
    )(core, grad, got)


def _adamw_math(w, g, m, v):
    m = ADAM_B1 * m + (1.0 - ADAM_B1) * g
    v = ADAM_B2 * v + (1.0 - ADAM_B2) * jnp.square(g)
    m_hat = m / (1.0 - ADAM_B1 ** ADAM_STEP)
    v_hat = v / (1.0 - ADAM_B2 ** ADAM_STEP)
    delta = -ADAM_LR * (m_hat / (jnp.sqrt(v_hat) + ADAM_EPS) + ADAM_WD * w)
    return delta, m, v


def _update(w, parts, m, v, transposed, name, hosted=()):
    r, c = parts.shape[1:]

    def body(w_ref, p_ref, m_ref, v_ref, g_ref, d_ref, m2_ref, v2_ref):
        def total(rows):
            return ((p_ref[0, rows].astype(F32) + p_ref[1, rows].astype(F32)) + p_ref[2, rows].astype(F32)) \
                + p_ref[3, rows].astype(F32)

        def update(at, g):
            g_ref[at] = g
            d_ref[at], m2_ref[at], v2_ref[at] = _adamw_math(w_ref[at], g, m_ref[at], v_ref[at])

        if not transposed:
            update((slice(None), slice(None)), total(slice(None)))
            return
        for r0 in range(0, r, 128):
            n = min(128, r - r0)
            gt = total(slice(r0, r0 + n))
            if n < 128:
                gt = jnp.concatenate([gt, jnp.zeros((128 - n, c), F32)], axis=0)
            update((slice(None), slice(r0, r0 + n)), gt.T[:, :n])

    sds = jax.ShapeDtypeStruct(w.shape, F32)
    vm = pl.BlockSpec(memory_space=pltpu.VMEM)
    return _call(body, name=name, grid=(1,), in_specs=[vm] * 4, out_specs=[vm] * 4, out_shape=[sds] * 4,
                 sem=("arbitrary",), args=(w, parts, m, v), hosted=hosted)


SMALL_ROWS = 80


def _small_update(g, w, m, v, name):
    def body(g_ref, w_ref, m_ref, v_ref, gs_ref, d_ref, m2_ref, v2_ref, got_ref, send_sems, recv_sems):
        x, y, c = _place()
        me = 4 * x + 2 * y + c
        got_ref[me] = g_ref[...]
        copies = []
        for k in range(1, N_DEV):
            peer = (x ^ (k >> 2), y ^ ((k >> 1) & 1), c ^ (k & 1))
            cp = pltpu.make_async_remote_copy(
                src_ref=g_ref, dst_ref=got_ref.at[me], send_sem=send_sems.at[k - 1], recv_sem=recv_sems.at[k - 1],
                device_id=peer, device_id_type=MESH)
            cp.start()
            copies.append(cp)
        for cp in copies:
            cp.wait()
        total = got_ref[0]
        for k in range(1, N_DEV):
            total = total + got_ref[k]
        gs_ref[...] = total
        d_ref[...], m2_ref[...], v2_ref[...] = _adamw_math(w_ref[...], total, m_ref[...], v_ref[...])

    sds = jax.ShapeDtypeStruct((SMALL_ROWS, 128), F32)
    vm = pl.BlockSpec(memory_space=pltpu.VMEM)
    return pl.pallas_call(
        body, name=name, in_specs=[vm] * 4, out_specs=[vm] * 4, out_shape=[sds] * 4,
        scratch_shapes=[pltpu.VMEM((N_DEV, SMALL_ROWS, 128), F32), pltpu.SemaphoreType.DMA((N_DEV - 1,)),
                        pltpu.SemaphoreType.DMA((N_DEV - 1,))],
    )(g, w, m, v)


def _pack_small(gains, b_in, rel_bias, sinks, last):
    rows = [a.reshape(8, 128) for a in gains] + [b_in.reshape(40, 128), rel_bias.reshape(5, 128),
                                                 jnp.pad(sinks.reshape(1, 8), ((0, 0), (0, 120))), last]
    rows.append(jnp.zeros((SMALL_ROWS - 79, 128), F32))
    return jnp.concatenate(rows, axis=0)


def _unpack_small(p, like):
    out = [p[8 * i:8 * i + 8].reshape(like[i].shape) for i in range(4)]
    out.append(p[32:72].reshape(like[4].shape))
    out.append(p[72:77].reshape(like[5].shape))
    out.append(p[77, :8].reshape(like[6].shape))
    return out


def kernel(x, ffn1_norm, ffn1_w_gate, ffn1_w_up, ffn1_w_down, mix_norm, w_in, b_in, w_branch_a, w_branch_b, w_out, sinks, rel_bias, ffn2_norm, ffn2_w_gate, ffn2_w_up, ffn2_w_down, final_norm, loss_target, m_ffn1_norm, m_ffn1_w_gate, m_ffn1_w_up, m_ffn1_w_down, m_mix_norm, m_w_in, m_b_in, m_w_branch_a, m_w_branch_b, m_w_out, m_sinks, m_rel_bias, m_ffn2_norm, m_ffn2_w_gate, m_ffn2_w_up, m_ffn2_w_down, m_final_norm, v_ffn1_norm, v_ffn1_w_gate, v_ffn1_w_up, v_ffn1_w_down, v_mix_norm, v_w_in, v_b_in, v_w_branch_a, v_w_branch_b, v_w_out, v_sinks, v_rel_bias, v_ffn2_norm, v_ffn2_w_gate, v_ffn2_w_up, v_ffn2_w_down, v_final_norm):
    bsz, seq, _ = x.shape
    t = bsz * seq
    xt = x.reshape(t, D_MODEL)
    target = loss_target.reshape(t, D_MODEL)

    big = [("ffn1_w_gate", ffn1_w_gate, m_ffn1_w_gate, v_ffn1_w_gate, True),
           ("ffn1_w_up", ffn1_w_up, m_ffn1_w_up, v_ffn1_w_up, True),
           ("ffn1_w_down", ffn1_w_down, m_ffn1_w_down, v_ffn1_w_down, False),
           ("w_in", w_in, m_w_in, v_w_in, True),
           ("w_branch_a", w_branch_a, m_w_branch_a, v_w_branch_a, True),
           ("w_branch_b", w_branch_b, m_w_branch_b, v_w_branch_b, True),
           ("w_out", w_out, m_w_out, v_w_out, False),
           ("ffn2_w_gate", ffn2_w_gate, m_ffn2_w_gate, v_ffn2_w_gate, True),
           ("ffn2_w_up", ffn2_w_up, m_ffn2_w_up, v_ffn2_w_up, True),
           ("ffn2_w_down", ffn2_w_down, m_ffn2_w_down, v_ffn2_w_down, False)]
    shard = {nm: (w[0].T if tr else w[0]).astype(BF16) for nm, w, _, _, tr in big}
    full = {}

    def gather(names):
        return _Gather([shard[nm] for nm in names])

    def keep(names, got):
        for nm, gw in zip(names, got):
            full[nm] = gw.reshape(-1, gw.shape[-1])

    ffn1_names = ["ffn1_w_gate", "ffn1_w_up", "ffn1_w_down"]
    mix_names = ["w_in", "w_branch_a", "w_branch_b", "w_out"]
    ffn2_names = ["ffn2_w_gate", "ffn2_w_up", "ffn2_w_down"]
    g1, gm, g2, gf = ffn1_norm, mix_norm, ffn2_norm, final_norm.reshape(1, D_MODEL)

    buckets = _bucket_tiles()
    (bias,), (got,) = _bias_build(rel_bias, buckets, hosted=[gather(ffn1_names)])
    keep(ffn1_names, got)
    (h1, n1, a1, b1, hff1), (got,) = _ffn_fwd(xt, g1, full["ffn1_w_gate"], full["ffn1_w_up"], full["ffn1_w_down"], "ffn1_fwd",
                                hosted=[gather(mix_names)])
    keep(mix_names, got)
    (u, zq, zg, *zdil), (got,) = _inproj_fwd(h1, gm, full["w_in"], b_in, "inproj_fwd", hosted=[gather(ffn2_names[:2])])
    keep(ffn2_names[:2], got)
    sink_rows = jnp.broadcast_to(sinks.reshape(8, 1, 1), (8, BLOCK, 128)).reshape(8 * BLOCK, 128)
    cfs = [_Att(i) for i in range(4)]
    zfold = []
    for i, cf in enumerate(cfs):
        if cf.d == 1:
            zfold.append(zq.reshape(bsz, seq, QKV_W))
        else:
            zfold.append(zdil[i - 1].reshape(bsz, seq // cf.d, cf.d * cf.row_w))
    att = [None] * 4
    for i in (3, 0, 1, 2):
        cf = cfs[i]
        (o, lse), got = _attn_fwd(cf, zfold[i], bias, sink_rows, f"attn{i}_fwd",
                                  hosted=[gather(ffn2_names[2:])] if i == 3 else ())
        if i == 3:
            keep(ffn2_names[2:], got[0])
        att[i] = (o.reshape(t // cf.d, cf.d * cf.wq), lse.reshape(t // cf.d, cf.d * cf.wq))
    o_b, lse_b = att[3]
    h2, ya, lse_tot = _merge_fwd([a[0] for a in att[:3]], [a[1] for a in att[:3]], o_b, zg, h1,
                                 full["w_branch_a"], full["w_branch_b"], full["w_out"], "merge_fwd")
    (dh3, n2, a2, b2, hff2, loss_part, dgf), _ = _ffn_fwd(
        h2, g2, full["ffn2_w_gate"], full["ffn2_w_up"], full["ffn2_w_down"], "ffn2_fwd", head=(gf, target))

    grads, pair, from_chips = {}, {}, {}

    def by_owner(nm):
        return grads[nm].reshape(N_CHIP, 2, -1, grads[nm].shape[-1])

    def to_core(names):
        return _CoreExchange([by_owner(nm) for nm in names])

    def pair_up(names, got):
        for nm, sib in zip(names, got):
            pair[nm] = _pair_sum(by_owner(nm), sib, f"pair_sum_{nm}")

    def to_chips(names):
        return _ChipExchange([pair[nm] for nm in names])

    def landed(names, got):
        for nm, parts in zip(names, got):
            from_chips[nm] = parts

    dh2, dg2, da, db = _ffn_bwd(h2, a2, b2, g2, dh3, full["ffn2_w_gate"], full["ffn2_w_up"],
                                          full["ffn2_w_down"], "ffn2_bwd")
    grads["ffn2_w_gate"], _ = _tn_matmul(da, n2, 1408, "ffn2_dgate")
    grads["ffn2_w_up"], _ = _tn_matmul(db, n2, 1408, "ffn2_dup")
    grads["ffn2_w_down"], _ = _tn_matmul(hff2, dh3, 1408, "ffn2_ddown", scale=0.5)
    (merged, dpa, dpb, dzg, *cot), (got,) = _merge_bwd(
        dh2, ya, lse_tot, o_b, lse_b, zg, full["w_branch_a"], full["w_branch_b"], full["w_out"], "merge_bwd",
        hosted=[to_core(ffn2_names)])
    dys, sts = cot[0:3] + [cot[6]], cot[3:6] + [cot[7]]
    pair_up(ffn2_names, got)
    dq, dk, dv, dbias, dsink = [None] * 4, [None] * 4, [None] * 4, [None] * 4, None
    for i in (3, 0, 1, 2):
        cf = cfs[i]
        shp = (bsz, seq // cf.d, cf.d * cf.wq)
        hosted = {3: lambda: [to_chips(ffn2_names[:2])], 0: lambda: [to_chips(ffn2_names[2:])]}.get(i, list)()
        res, got = _attn_bwd(cf, zfold[i], bias, sink_rows, sts[i].reshape(shp[:2] + (shp[2] // 2,)), dys[i].reshape(shp),
                             f"attn{i}_bwd", hosted=hosted)
        if i == 3:
            landed(ffn2_names[:2], got[0])
        elif i == 0:
            landed(ffn2_names[2:], got[0])
        dq[i] = res[0].reshape(t // cf.d, cf.d * cf.wq)
        dk[i] = res[1].reshape(t // cf.d, cf.d * cf.wkv)
        dv[i] = res[2].reshape(t // cf.d, cf.d * cf.wkv)
        dbias[i] = res[3]
        if cf.sinks:
            dsink = res[4]
    pieces = dq[:3] + dk[:3] + dv[:3] + [dq[3], dk[3], dv[3], dzg]
    dh1, dz, db_in, dgm = _inproj_bwd(pieces, h1, gm, dh2, full["w_in"], "inproj_bwd")
    dx, dg1, da, db = _ffn_bwd(xt, a1, b1, g1, dh1, full["ffn1_w_gate"], full["ffn1_w_up"],
                                         full["ffn1_w_down"], "ffn1_bwd")
    grads["w_in"], _ = _tn_matmul(dz, u, 1280, "dw_in")
    grads["ffn1_w_down"], (got,) = _tn_matmul(hff1, dh1, 1408, "ffn1_ddown", hosted=[to_core(["w_in"])], scale=0.5)
    pair_up(["w_in"], got)
    grads["ffn1_w_gate"], got = _tn_matmul(da, n1, 1408, "ffn1_dgate",
                                           hosted=[to_chips(["w_in"]), to_core(["ffn1_w_down"])])
    landed(["w_in"], got[0])
    pair_up(["ffn1_w_down"], got[1])
    grads["ffn1_w_up"], got = _tn_matmul(db, n1, 1408, "ffn1_dup",
                                         hosted=[to_chips(["ffn1_w_down"]), to_core(["ffn1_w_gate"])])
    landed(["ffn1_w_down"], got[0])
    pair_up(["ffn1_w_gate"], got[1])
    grads["w_out"], got = _tn_matmul(merged, dh2, 1024, "dw_out",
                                     hosted=[to_chips(["ffn1_w_gate"]), to_core(["ffn1_w_up"])])
    landed(["ffn1_w_gate"], got[0])
    pair_up(["ffn1_w_up"], got[1])
    grads["w_branch_b"], got = _tn_matmul(dpb, o_b, 1024, "dw_branch_b",
                                          hosted=[to_chips(["ffn1_w_up"]), to_core(["w_out"])])
    landed(["ffn1_w_up"], got[0])
    pair_up(["w_out"], got[1])
    grads["w_branch_a"], got = _tn_matmul(dpa, ya, 1024, "dw_branch_a",
                                          hosted=[to_chips(["w_out"]), to_core(["w_branch_b"])])
    landed(["w_out"], got[0])
    pair_up(["w_branch_b"], got[1])
    (dtable, dsinks), got = _bias_reduce(jnp.concatenate(dbias, axis=0), buckets, dsink,
                                         hosted=[to_chips(["w_branch_b"]), to_core(["w_branch_a"])])
    landed(["w_branch_b"], got[0])
    pair_up(["w_branch_a"], got[1])

    out_g, out_d, out_m, out_v = {}, {}, {}, {}
    for idx, (nm, w, m, v, tr) in enumerate(big):
        around = tr and w.shape[-1] % 128 != 0
        wmv = [a[0].T if around else a[0] for a in (w, m, v)]
        res, got = _update(wmv[0], from_chips[nm], wmv[1], wmv[2], tr and not around, f"update_{nm}",
                           hosted=[to_chips(["w_branch_a"])] if idx == 0 else ())
        if idx == 0:
            landed(["w_branch_a"], got[0])
        out_g[nm], out_d[nm], out_m[nm], out_v[nm] = [(a.T if around else a)[None] for a in res]

    small = [("ffn1_norm", ffn1_norm, m_ffn1_norm, v_ffn1_norm), ("mix_norm", mix_norm, m_mix_norm, v_mix_norm),
             ("ffn2_norm", ffn2_norm, m_ffn2_norm, v_ffn2_norm), ("final_norm", final_norm, m_final_norm, v_final_norm),
             ("b_in", b_in, m_b_in, v_b_in), ("rel_bias", rel_bias, m_rel_bias, v_rel_bias),
             ("sinks", sinks, m_sinks, v_sinks)]
    zero_row = jnp.zeros((1, 128), F32)
    pack = lambda arrs, last: _pack_small(arrs[:4], arrs[4], arrs[5], arrs[6], last)
    g_small = pack([dg1, dgm, dg2, dgf, db_in, dtable[:, :TOTAL_HEADS], dsinks[:, 0]], loss_part)
    packed = [pack([s[k] for s in small], zero_row) for k in (1, 2, 3)]
    gs, ds, ms, vs = _small_update(g_small, *packed, "small_update")
    like = [s[1] for s in small]
    for nm_s, g_, d_, m_, v_ in zip([s[0] for s in small], _unpack_small(gs, like), _unpack_small(ds, like),
                                    _unpack_small(ms, like), _unpack_small(vs, like)):
        out_g[nm_s], out_d[nm_s], out_m[nm_s], out_v[nm_s] = g_, d_, m_, v_
    loss = gs[78, 0]

    order = ["ffn1_norm", "ffn1_w_gate", "ffn1_w_up", "ffn1_w_down", "mix_norm", "w_in", "b_in", "w_branch_a",
             "w_branch_b", "w_out", "sinks", "rel_bias", "ffn2_norm", "ffn2_w_gate", "ffn2_w_up", "ffn2_w_down",
             "final_norm"]
    return (loss, dx.reshape(x.shape), *[out_g[k] for k in order], *[out_d[k] for k in order],
            *[out_m[k] for k in order], *[out_v[k] for k in order])
```

```python
import functools
import math

import numpy as np
import jax
import jax.numpy as jnp
from jax import lax
from jax.experimental import pallas as pl
from jax.experimental.pallas import tpu as pltpu

D_MODEL = 1024
D_FF = 2816
FF_CHUNK = 256
HEAD_DIM = 64
BLOCK = 128
N_BUCKETS = 32
MAX_DISTANCE = 2048
A_HEADS = 12
TOTAL_HEADS = 20
DIL_GROUPS = ((128, 1), (512, 4), (2048, 16))
B_WINDOW = 128
QKV_W = 3072
GATE_W = 2048
D_IN = QKV_W + GATE_W
EPS = 1e-6
NEG = -1e30
N_DEV = 8
N_CHIP = 4
ADAM_LR, ADAM_B1, ADAM_B2, ADAM_EPS, ADAM_WD, ADAM_STEP = 0.001, 0.9, 0.999, 1e-08, 0.01, 10
VMEM_LIMIT = 56 * 1024 * 1024
MESH = pl.DeviceIdType.MESH
BF16 = jnp.bfloat16
F32 = jnp.float32
ANY = pl.BlockSpec(memory_space=pl.ANY)


def _params(*sem):
    return pltpu.CompilerParams(dimension_semantics=sem, vmem_limit_bytes=VMEM_LIMIT)


def _resident(a):
    return pl.BlockSpec(a.shape, lambda i: (0, 0), pipeline_mode=pl.Buffered(1))


def _place():
    return lax.axis_index("x"), lax.axis_index("y"), lax.axis_index("c")


class _Gather:
    def __init__(self, shards):
        self.ins = list(shards)
        n = self.n = len(shards)
        self.out_shape = [jax.ShapeDtypeStruct((N_DEV,) + s.shape, s.dtype) for s in shards]
        self.scratch = [pltpu.SemaphoreType.DMA((7 * n,)), pltpu.SemaphoreType.DMA((7 * n,)),
                        pltpu.SemaphoreType.DMA((n,))]

    def _copies(self, ins, outs, sems):
        send_sems, recv_sems, local_sems = sems
        x, y, c = _place()
        me, sibling = (x, y, c), (x, y, 1 - c)
        chips = [(1 - x, y), (x, 1 - y), (1 - x, 1 - y)]

        def copy(i, k, block, to, src=None):
            dst = outs[i].at[4 * block[0] + 2 * block[1] + block[2]]
            return pltpu.make_async_remote_copy(
                src_ref=dst if src is None else src, dst_ref=dst, send_sem=send_sems.at[7 * i + k],
                recv_sem=recv_sems.at[7 * i + k], device_id=to, device_id_type=MESH)

        n = self.n
        south = c == 0
        relayed = (jnp.where(south, 1 - x, x), jnp.where(south, y, 1 - y), c)
        relay_to = (jnp.where(south, x, 1 - x), jnp.where(south, 1 - y, y), c)
        mine = [pltpu.make_async_copy(ins[i], outs[i].at[4 * x + 2 * y + c], local_sems.at[i]) for i in range(n)]
        first = [copy(i, 0, me, sibling, src=ins[i]) for i in range(n)]
        first += [copy(i, 1 + j, me, (*chips[j], c), src=ins[i]) for i in range(n) for j in range(2)]
        landed = [copy(i, 1 + j, (*chips[j], c), me) for j in range(2) for i in range(n)]
        passed = [copy(i, 4 + j, (*chips[j], c), sibling) for j in range(2) for i in range(n)]
        relays = [copy(i, 3, relayed, relay_to) for i in range(n)]
        diag_landed = [copy(i, 3, (*chips[2], c), me) for i in range(n)]
        diag_passed = [copy(i, 6, (*chips[2], c), sibling) for i in range(n)]
        from_sibling = [copy(i, 0, sibling, me) for i in range(n)]
        from_sibling += [copy(i, 4 + j, (*chip, 1 - c), me) for i in range(n) for j, chip in enumerate(chips)]
        return mine, first, landed, passed, relays, diag_landed, diag_passed, from_sibling

    def start(self, ins, outs, sems):
        mine, first = self._copies(ins, outs, sems)[:2]
        for cp in mine + first:
            cp.start()

    def mid(self, ins, outs, sems):
        _, _, landed, passed, relays, _, _, _ = self._copies(ins, outs, sems)
        for got in landed:
            got.wait_recv()
        for cp in relays + passed:
            cp.start()

    def end(self, ins, outs, sems):
        mine, first, _, passed, relays, diag_landed, diag_passed, from_sibling = self._copies(ins, outs, sems)
        for got, fwd in zip(diag_landed, diag_passed):
            got.wait_recv()
            fwd.start()
        for cp in from_sibling:
            cp.wait_recv()
        for cp in first + passed + relays + diag_passed:
            cp.wait_send()
        for cp in mine:
            cp.wait()


class _CoreExchange:
    def __init__(self, grads):
        self.ins = list(grads)
        n = self.n = len(grads)
        self.out_shape = [jax.ShapeDtypeStruct((N_CHIP, 1) + g.shape[2:], g.dtype) for g in grads]
        self.scratch = [pltpu.SemaphoreType.DMA((n,)), pltpu.SemaphoreType.DMA((n,))]

    def _copies(self, ins, outs, sems):
        x, y, c = _place()
        return [pltpu.make_async_remote_copy(
            src_ref=ins[i].at[:, pl.ds(1 - c, 1)], dst_ref=outs[i], send_sem=sems[0].at[i],
            recv_sem=sems[1].at[i], device_id=(x, y, 1 - c), device_id_type=MESH) for i in range(self.n)]

    def start(self, ins, outs, sems):
        for cp in self._copies(ins, outs, sems):
            cp.start()

    mid = None

    def end(self, ins, outs, sems):
        for cp in self._copies(ins, outs, sems):
            cp.wait()


class _ChipExchange:
    def __init__(self, parts):
        self.ins = list(parts)
        n = self.n = len(parts)
        self.out_shape = [jax.ShapeDtypeStruct(p.shape, p.dtype) for p in parts]
        self.scratch = [pltpu.SemaphoreType.DMA((3 * n,)), pltpu.SemaphoreType.DMA((3 * n,)),
                        pltpu.SemaphoreType.DMA((n,))]

    def _copies(self, ins, outs, sems):
        send_sems, recv_sems, local_sems = sems
        x, y, c = _place()
        my_chip = 2 * x + y
        copies = []
        for i in range(self.n):
            copies.append(pltpu.make_async_copy(ins[i].at[my_chip], outs[i].at[my_chip], local_sems.at[i]))
            for k, (qx, qy) in enumerate([(1 - x, y), (x, 1 - y), (1 - x, 1 - y)]):
                copies.append(pltpu.make_async_remote_copy(
                    src_ref=ins[i].at[2 * qx + qy], dst_ref=outs[i].at[my_chip], send_sem=send_sems.at[3 * i + k],
                    recv_sem=recv_sems.at[3 * i + k], device_id=(qx, qy, c), device_id_type=MESH))
        return copies

    def start(self, ins, outs, sems):
        for cp in self._copies(ins, outs, sems):
            cp.start()

    mid = None

    def end(self, ins, outs, sems):
        for cp in self._copies(ins, outs, sems):
            cp.wait()


def _call(body, *, name, grid, in_specs, out_specs, out_shape, args, scratch=(), sem=None, hosted=()):
    n_in, n_out, n_scr = len(in_specs), len(out_specs), len(scratch)
    x_in = [len(p.ins) for p in hosted]
    x_scr = [len(p.scratch) for p in hosted]
    steps = int(np.prod(grid))

    def wrapped(*refs):
        refs = list(refs)
        ins, refs = refs[:n_in], refs[n_in:]
        x_ins = [[refs.pop(0) for _ in range(k)] for k in x_in]
        outs, refs = refs[:n_out], refs[n_out:]
        x_outs = [[refs.pop(0) for _ in range(k)] for k in x_in]
        scr, refs = refs[:n_scr], refs[n_scr:]
        x_sems = [[refs.pop(0) for _ in range(k)] for k in x_scr]
        step = 0
        for d in range(len(grid)):
            step = step * grid[d] + pl.program_id(d)

        def phase(which, at):
            fns = [(getattr(p, which), a) for p, a in zip(hosted, zip(x_ins, x_outs, x_sems)) if getattr(p, which)]
            if fns:
                @pl.when(step == at)
                def _():
                    for fn, a in fns:
                        fn(*a)

        phase("start", 0)
        if steps > 1:
            phase("mid", steps // 2)
        body(*ins, *outs, *scr)
        if steps == 1:
            phase("mid", 0)
        phase("end", steps - 1)

    results = pl.pallas_call(
        wrapped, name=name, grid=grid,
        in_specs=list(in_specs) + [ANY] * sum(x_in), out_specs=list(out_specs) + [ANY] * sum(x_in),
        out_shape=list(out_shape) + [s for p in hosted for s in p.out_shape],
        scratch_shapes=list(scratch) + [s for p in hosted for s in p.scratch],
        compiler_params=_params(*(("arbitrary",) * len(grid) if hosted else sem)),
    )(*args, *[a for p in hosted for a in p.ins])
    own, rest = list(results[:n_out]), list(results[n_out:])
    return own, [[rest.pop(0) for _ in range(k)] for k in x_in]


def _nt(a, b):
    return lax.dot_general(a, b, (((1,), (1,)), ((), ())), preferred_element_type=F32)


def _nn(a, b):
    return lax.dot_general(a, b, (((1,), (0,)), ((), ())), preferred_element_type=F32)


def _tn(a, b):
    return lax.dot_general(a, b, (((0,), (0,)), ((), ())), preferred_element_type=F32)


def _rms(x, g):
    r = lax.rsqrt(jnp.mean(x * x, axis=-1, keepdims=True) + EPS)
    return x * r, r


def _rms_bwd(dn, xhat, r, g):
    dg = jnp.sum(dn * xhat, axis=0, keepdims=True)
    dxh = dn * g
    dx = r * (dxh - xhat * jnp.mean(dxh * xhat, axis=-1, keepdims=True))
    return dx, dg


def _ffn_fwd(x, g, wg_t, wu_t, wd, name, hosted=(), head=None):
    t = x.shape[0]
    tm = 512 if head is None else 256

    def body(*refs):
        x_ref, g_ref, wg_ref, wu_ref, wd_ref = refs[:5]
        h_ref, n_ref, a_ref, b_ref, hff_ref = refs[-5 if head is None else -7:][:5]
        xhat, _ = _rms(x_ref[...], g_ref[...])
        n = (xhat * g_ref[...]).astype(BF16)
        n_ref[...] = n
        for c in range(0, D_FF, FF_CHUNK):
            cols = slice(c, c + FF_CHUNK)
            a = _nt(n, wg_ref[cols, :])
            b = _nt(n, wu_ref[cols, :])
            a_ref[:, cols] = a.astype(BF16)
            b_ref[:, cols] = b.astype(BF16)
            hff_ref[:, cols] = (a * jax.nn.sigmoid(a) * b).astype(BF16)
        h = x_ref[...] + 0.5 * _nn(hff_ref[...], wd_ref[...])
        if head is None:
            h_ref[...] = h
            return
        gf_ref, t_ref, loss_ref, dgf_ref = refs[5], refs[6], refs[-2], refs[-1]

        @pl.when(pl.program_id(0) == 0)
        def _():
            loss_ref[...] = jnp.zeros_like(loss_ref)
            dgf_ref[...] = jnp.zeros_like(dgf_ref)

        yhat, r = _rms(h, gf_ref[...])
        err = yhat * gf_ref[...] - t_ref[...]
        loss_ref[...] += 0.5 * jnp.sum(jnp.mean(err * err, axis=-1, keepdims=True), axis=0, keepdims=True)
        h_ref[...], dgf = _rms_bwd(err * (1.0 / D_MODEL), yhat, r, gf_ref[...])
        dgf_ref[...] += dgf

    row = pl.BlockSpec((tm, D_MODEL), lambda i: (i, 0))
    hid = pl.BlockSpec((tm, D_FF), lambda i: (i, 0))
    sds = jax.ShapeDtypeStruct
    in_specs = [row, _resident(g), _resident(wg_t), _resident(wu_t), _resident(wd)]
    out_specs = [row, row, hid, hid, hid]
    out_shape = [sds((t, D_MODEL), F32), sds((t, D_MODEL), BF16), sds((t, D_FF), BF16), sds((t, D_FF), BF16),
                 sds((t, D_FF), BF16)]
    args = (x, g, wg_t, wu_t, wd)
    if head is not None:
        in_specs += [_resident(head[0]), row]
        out_specs += [pl.BlockSpec((1, 128), lambda i: (0, 0)), pl.BlockSpec((1, D_MODEL), lambda i: (0, 0))]
        out_shape += [sds((1, 128), F32), sds((1, D_MODEL), F32)]
        args += tuple(head)
    return _call(body, name=name, grid=(t // tm,), in_specs=in_specs, out_specs=out_specs, out_shape=out_shape,
                 sem=("parallel",) if head is None else ("arbitrary",), args=args, hosted=hosted)


def _ffn_bwd(x, a_pre, b_pre, g, dh, wg_t, wu_t, wd, name):
    t = x.shape[0]
    tm = 512

    def body(x_ref, a_ref, b_ref, g_ref, dh_ref, wg_ref, wu_ref, wd_ref,
             dx_ref, dg_ref, da_ref, db_ref):
        @pl.when(pl.program_id(0) == 0)
        def _():
            dg_ref[...] = jnp.zeros_like(dg_ref)

        dhh = (0.5 * dh_ref[...]).astype(BF16)
        for c in range(0, D_FF, FF_CHUNK):
            cols = slice(c, c + FF_CHUNK)
            a = a_ref[:, cols].astype(F32)
            b = b_ref[:, cols].astype(F32)
            s = jax.nn.sigmoid(a)
            silu = a * s
            dhff = _nt(dhh, wd_ref[cols, :])
            da_ref[:, cols] = (dhff * b * (s * (1.0 + a * (1.0 - s)))).astype(BF16)
            db_ref[:, cols] = (dhff * silu).astype(BF16)
        dn = _nn(da_ref[...], wg_ref[...]) + _nn(db_ref[...], wu_ref[...])
        xhat, r = _rms(x_ref[...], g_ref[...])
        dx, dg = _rms_bwd(dn, xhat, r, g_ref[...])
        dx_ref[...] = dh_ref[...] + dx
        dg_ref[...] += dg

    row = pl.BlockSpec((tm, D_MODEL), lambda i: (i, 0))
    hid = pl.BlockSpec((tm, D_FF), lambda i: (i, 0))
    return pl.pallas_call(
        body, name=name, grid=(t // tm,),
        in_specs=[row, hid, hid, _resident(g), row, _resident(wg_t), _resident(wu_t), _resident(wd)],
        out_specs=[row, pl.BlockSpec((1, D_MODEL), lambda i: (0, 0)), hid, hid],
        out_shape=[jax.ShapeDtypeStruct((t, D_MODEL), F32), jax.ShapeDtypeStruct((1, D_MODEL), F32),
                   jax.ShapeDtypeStruct((t, D_FF), BF16), jax.ShapeDtypeStruct((t, D_FF), BF16)],
        compiler_params=_params("arbitrary"),
    )(x, a_pre, b_pre, g, dh, wg_t, wu_t, wd)


def _tn_matmul(a, b, rc, name, hosted=(), scale=None):
    t, r = a.shape
    c = b.shape[1]
    tk = min(t, 2048)

    def body(a_ref, b_ref, o_ref):
        @pl.when(pl.program_id(1) == 0)
        def _():
            o_ref[...] = jnp.zeros_like(o_ref)

        o_ref[...] += _tn(a_ref[...].astype(BF16), b_ref[...].astype(BF16))
        if scale is not None:
            @pl.when(pl.program_id(1) == t // tk - 1)
            def _():
                o_ref[...] *= scale

    (out,), got = _call(
        body, name=name, grid=(r // rc, t // tk),
        in_specs=[pl.BlockSpec((tk, rc), lambda i, k: (k, i)), pl.BlockSpec((tk, c), lambda i, k: (k, 0))],
        out_specs=[pl.BlockSpec((rc, c), lambda i, k: (i, 0))],
        out_shape=[jax.ShapeDtypeStruct((r, c), F32)],
        sem=("parallel", "arbitrary"), args=(a, b), hosted=hosted)
    return out, got


def _unfold(blk_ref, slab_ref, d):
    if d == 1:
        return blk_ref[...]
    n = blk_ref.shape[0]
    for r in range(d):
        for half in range(2):
            c0 = 256 * r + 128 * half
            slab_ref[half, pl.ds(r, n, stride=d), :] = blk_ref[:, c0:c0 + 128]
    return jnp.concatenate([slab_ref[0], slab_ref[1]], axis=1)


def _fold(x, slab_ref, out_ref, d):
    if d == 1:
        out_ref[...] = x.astype(out_ref.dtype)
        return
    n, w = out_ref.shape[0], x.shape[1]
    for part in range(w // 128):
        slab_ref[part] = x[:, 128 * part:128 * (part + 1)]
    for r in range(d):
        for part in range(w // 128):
            c0 = w * r + 128 * part
            out_ref[:, c0:c0 + 128] = slab_ref[part, pl.ds(r, n, stride=d), :].astype(out_ref.dtype)


DILATIONS = tuple(d for _, d in DIL_GROUPS)


PIECE_W = (256,) * 9 + (512, 128, 128, GATE_W)
PIECE_D = DILATIONS * 3 + (1, 1, 1, 1)


def _inproj_fwd(h, g, w_t, b_in, name, hosted=()):
    t = h.shape[0]
    tm, nc = 512, 256
    dilated = [(gi, d) for gi, d in enumerate(DILATIONS) if d > 1]

    def body(h_ref, g_ref, w_ref, b_ref, u_ref, zq_ref, zg_ref, *rest):
        zf_refs, slabs = rest[:len(dilated)], rest[len(dilated):]
        xhat, _ = _rms(h_ref[...], g_ref[...])
        u = (xhat * g_ref[...]).astype(BF16)
        u_ref[...] = u
        for c in range(D_IN // nc):
            z = _nt(u, w_ref[c * nc:(c + 1) * nc, :]) + b_ref[:, c * nc:(c + 1) * nc]
            if c < QKV_W // nc:
                zq_ref[:, c * nc:(c + 1) * nc] = z.astype(BF16)
            else:
                zg_ref[:, c * nc - QKV_W:(c + 1) * nc - QKV_W] = z.astype(BF16)
            part, gi = divmod(c, len(DILATIONS))
            for k, (gk, d) in enumerate(dilated):
                if part < 3 and gi == gk:
                    slab, n = slabs[3 * k + part], tm // d
                    slab[0] = z[:, :128]
                    slab[1] = z[:, 128:]
                    for r in range(d):
                        for half in range(2):
                            c0 = 768 * r + 256 * part + 128 * half
                            zf_refs[k][:, c0:c0 + 128] = slab[half, pl.ds(r, n, stride=d), :].astype(BF16)

    row = lambda w: pl.BlockSpec((tm, w), lambda i: (i, 0))
    full = _resident
    sds = jax.ShapeDtypeStruct
    return _call(
        body, name=name, grid=(t // tm,),
        in_specs=[row(D_MODEL), full(g), full(w_t), full(b_in)],
        out_specs=[row(D_MODEL), row(QKV_W), row(GATE_W)]
        + [pl.BlockSpec((tm // d, d * 768), lambda i: (i, 0)) for _, d in dilated],
        out_shape=[sds((t, D_MODEL), BF16), sds((t, QKV_W), BF16), sds((t, GATE_W), BF16)]
        + [sds((t // d, d * 768), BF16) for _, d in dilated],
        scratch=[pltpu.VMEM((2, tm, 128), F32)] * (3 * len(dilated)),
        sem=("parallel",), args=(h, g, w_t, b_in), hosted=hosted)


def _inproj_bwd(pieces, h, g, dh_res, w_t, name):
    t = h.shape[0]
    tm = 256
    npiece = len(PIECE_W)
    offs = np.concatenate([[0], np.cumsum(PIECE_W)]).tolist()

    def body(*refs):
        p_refs = refs[:npiece]
        h_ref, g_ref, dhr_ref, w_ref, dh_ref, dz_ref, db_ref, dg_ref = refs[npiece:npiece + 8]
        slabs = list(refs[npiece + 8:])
        i = pl.program_id(0)

        @pl.when(i == 0)
        def _():
            db_ref[...] = jnp.zeros_like(db_ref)
            dg_ref[...] = jnp.zeros_like(dg_ref)

        du = jnp.zeros((tm, D_MODEL), F32)
        for k in range(npiece):
            o, w = offs[k], PIECE_W[k]
            token_order = _unfold(p_refs[k], slabs.pop(), PIECE_D[k]).astype(BF16) if PIECE_D[k] > 1 else None
            for c0 in range(0, w, 512):
                cw = min(512, w - c0)
                pz = p_refs[k][:, c0:c0 + cw] if token_order is None else token_order
                dz_ref[:, o + c0:o + c0 + cw] = pz
                db_ref[:, o + c0:o + c0 + cw] += jnp.sum(pz.astype(F32), axis=0, keepdims=True)
                du = du + _nn(pz, w_ref[o + c0:o + c0 + cw, :])
        xhat, r = _rms(h_ref[...], g_ref[...])
        dx, dg = _rms_bwd(du, xhat, r, g_ref[...])
        dh_ref[...] = dhr_ref[...] + dx
        dg_ref[...] += dg

    row = lambda w: pl.BlockSpec((tm, w), lambda i: (i, 0))
    full = lambda shp: pl.BlockSpec(shp, lambda i: (0, 0))
    return pl.pallas_call(
        body, name=name, grid=(t // tm,),
        in_specs=[pl.BlockSpec((tm // d, d * w), lambda i: (i, 0)) for w, d in zip(PIECE_W, PIECE_D)]
        + [row(D_MODEL), _resident(g), row(D_MODEL), _resident(w_t)],
        out_specs=[row(D_MODEL), row(D_IN), full((1, D_IN)), full((1, D_MODEL))],
        out_shape=[jax.ShapeDtypeStruct((t, D_MODEL), F32), jax.ShapeDtypeStruct((t, D_IN), BF16),
                   jax.ShapeDtypeStruct((1, D_IN), F32), jax.ShapeDtypeStruct((1, D_MODEL), F32)],
        scratch_shapes=[pltpu.VMEM((2, tm, 128), F32)] * sum(d > 1 for d in PIECE_D),
        compiler_params=_params("arbitrary"),
    )(*pieces, h, g, dh_res, w_t)


def _t5_bucket(dist):
    max_exact = N_BUCKETS // 2
    n = jnp.maximum(dist, 0)
    nf = jnp.maximum(n, 1).astype(jnp.float32)
    large = max_exact + (jnp.log(nf / max_exact) / math.log(MAX_DISTANCE / max_exact)
                         * (N_BUCKETS - max_exact)).astype(jnp.int32)
    large = jnp.minimum(large, N_BUCKETS - 1)
    return jnp.where(n < max_exact, n, large)


ATT_CFG = ((1, 128, 0, 4), (4, 128, 4, 4), (16, 128, 8, 4), (1, B_WINDOW - 1, A_HEADS, 8))


def _bucket_tiles():
    qi = jnp.arange(BLOCK)[:, None]
    ki = jnp.arange(2 * BLOCK)[None, :]
    dist = qi + BLOCK - ki
    return jnp.stack([_t5_bucket(dist * cfg[0]) for cfg in ATT_CFG]).astype(jnp.int32)


def _band(max_steps):
    row = lax.broadcasted_iota(jnp.int32, (BLOCK, 2 * BLOCK), 0)
    col = lax.broadcasted_iota(jnp.int32, (BLOCK, 2 * BLOCK), 1)
    dist = row + BLOCK - col
    return (dist >= 0) & (dist <= max_steps)


def _bias_build(table, buckets, hosted=()):
    def body(tab_ref, bt_ref, out_ref):
        col = lax.broadcasted_iota(jnp.int32, (BLOCK, 2 * BLOCK), 1)
        for ci, (_, max_steps, h0, nh) in enumerate(ATT_CFG):
            bt = bt_ref[ci]
            band = _band(max_steps)
            for h in range(h0, h0 + nh):
                acc = lax.fori_loop(0, N_BUCKETS, lambda b, acc: jnp.where(bt == b, tab_ref[b, h], acc),
                                    jnp.zeros((BLOCK, 2 * BLOCK), F32))
                out_ref[0, BLOCK * h:BLOCK * (h + 1), :] = jnp.where(band & (col >= BLOCK), acc, NEG)
                out_ref[1, BLOCK * h:BLOCK * (h + 1), :] = jnp.where(band, acc, NEG)

    return _call(
        body, name="bias_build", grid=(1,),
        in_specs=[pl.BlockSpec(memory_space=pltpu.SMEM), pl.BlockSpec(memory_space=pltpu.VMEM)],
        out_specs=[pl.BlockSpec(memory_space=pltpu.VMEM)],
        out_shape=[jax.ShapeDtypeStruct((2, TOTAL_HEADS * BLOCK, 2 * BLOCK), F32)],
        sem=("arbitrary",), args=(table, buckets), hosted=hosted)


def _bias_reduce(dbias, buckets, dsink_rows, hosted=()):
    def body(db_ref, bt_ref, ds_ref, out_ref, sink_ref):
        ri = lax.broadcasted_iota(jnp.int32, (N_BUCKETS, 128), 0)
        ci = lax.broadcasted_iota(jnp.int32, (N_BUCKETS, 128), 1)

        def per_bucket(b, acc):
            for cfg_i, (_, _, h0, nh) in enumerate(ATT_CFG):
                hit = bt_ref[cfg_i] == b
                for h in range(h0, h0 + nh):
                    val = jnp.sum(jnp.where(hit, db_ref[BLOCK * h:BLOCK * (h + 1), :], 0.0))
                    acc = jnp.where((ri == b) & (ci == h), val, acc)
            return acc

        out_ref[...] = lax.fori_loop(0, N_BUCKETS, per_bucket, jnp.zeros((N_BUCKETS, 128), F32))
        for h in range(8):
            sink_ref[h:h + 1, :] = jnp.sum(ds_ref[BLOCK * h:BLOCK * (h + 1), :], axis=0, keepdims=True)

    return _call(
        body, name="bias_reduce", grid=(1,),
        in_specs=[pl.BlockSpec(memory_space=pltpu.VMEM)] * 3,
        out_specs=[pl.BlockSpec(memory_space=pltpu.VMEM)] * 2,
        out_shape=[jax.ShapeDtypeStruct((N_BUCKETS, 128), F32), jax.ShapeDtypeStruct((8, 128), F32)],
        sem=("arbitrary",), args=(dbias, buckets, dsink_rows), hosted=hosted)


class _Att:
    def __init__(self, cfg_i):
        stride, _, h0, nh = ATT_CFG[cfg_i]
        self.d = stride if cfg_i < 3 else 1
        self.h0, self.nh = h0, nh
        self.row_w = QKV_W if self.d == 1 else 3 * 256
        if cfg_i < 3:
            self.nq, self.wkv = 1, 256
            self.q_unit = [cfg_i if self.d == 1 else 0]
            self.k_unit, self.v_unit = (3 + cfg_i, 6 + cfg_i) if self.d == 1 else (1, 2)
            self.sinks = False
        else:
            self.nq, self.wkv = 2, 128
            self.q_unit = [9, 10]
            self.k_unit, self.v_unit = 22, 23
            self.sinks = True
        self.wq = 256 * self.nq


def _att_in_specs(cf, bsz):
    uq, ukv = cf.row_w // 256, cf.row_w // cf.wkv
    specs = [pl.BlockSpec((bsz, BLOCK, 256), functools.partial(lambda r, j, u: (0, j, r * uq + u), u=u))
             for u in cf.q_unit]
    for unit in (cf.k_unit, cf.v_unit):
        specs.append(pl.BlockSpec((bsz, BLOCK, cf.wkv),
                                  functools.partial(lambda r, j, u: (0, jnp.maximum(j - 1, 0), r * ukv + u), u=unit)))
        specs.append(pl.BlockSpec((bsz, BLOCK, cf.wkv),
                                  functools.partial(lambda r, j, u: (0, j, r * ukv + u), u=unit)))
    for qb in range(cf.nq):
        specs.append(pl.BlockSpec((None, HEADS_PER_BLOCK * BLOCK, 2 * BLOCK), functools.partial(
            lambda r, j, u: (jnp.minimum(j, 1), u, 0), u=cf.h0 // HEADS_PER_BLOCK + qb)))
    if cf.sinks:
        specs += [pl.BlockSpec((HEADS_PER_BLOCK * BLOCK, 128), functools.partial(lambda r, j, u: (u, 0), u=qb))
                  for qb in range(cf.nq)]
    return specs


HEADS_PER_BLOCK = 4


def _head_masks(rows):
    head = lax.broadcasted_iota(jnp.int32, (rows, 256), 1) // HEAD_DIM
    return [head == h for h in range(HEADS_PER_BLOCK)]


def _stack_heads(x, masks):
    return jnp.concatenate([jnp.where(m, x, jnp.zeros_like(x)) for m in masks], axis=0)


def _unstack_heads(x4, masks):
    blocks = [x4[BLOCK * h:BLOCK * (h + 1)] for h in range(HEADS_PER_BLOCK)]
    return jnp.where(masks[0], blocks[0], jnp.where(masks[1], blocks[1], jnp.where(masks[2], blocks[2], blocks[3])))


def _row_value(x):
    return jnp.max(x, axis=-1, keepdims=True)


def _kv_operands(cf, x):
    if cf.wkv == 256:
        return [x]
    lane = lax.broadcasted_iota(jnp.int32, x.shape, 1)
    swapped = pltpu.roll(x, HEAD_DIM, 1)
    halves = [jnp.where(lane < HEAD_DIM, x, swapped), jnp.where(lane < HEAD_DIM, swapped, x)]
    return [jnp.concatenate([half, half], axis=1) for half in halves]


def _kv_fold(cf, grads):
    if cf.wkv == 256:
        return grads[0]
    folded = []
    for g in grads:
        x = g[:, :128] + g[:, 128:]
        folded.append(x + pltpu.roll(x, HEAD_DIM, 1))
    lane = lax.broadcasted_iota(jnp.int32, folded[0].shape, 1)
    return jnp.where(lane < HEAD_DIM, folded[0], folded[1])


def _attn_fwd(cf, zf, bias, sinks, name, hosted=()):
    bsz, l, _ = zf.shape
    nb = l // BLOCK

    def body(*refs):
        refs = list(refs)
        q_refs = [refs.pop(0) for _ in range(cf.nq)]
        kp_ref, kc_ref, vp_ref, vc_ref = [refs.pop(0) for _ in range(4)]
        bias_refs = [refs.pop(0) for _ in range(cf.nq)]
        sink_refs = [refs.pop(0) for _ in range(cf.nq)] if cf.sinks else None
        o_ref, lse_ref = refs
        masks, kv_masks = _head_masks(BLOCK), _head_masks(2 * BLOCK)
        sinks4 = [_row_value(ref[...]) for ref in sink_refs] if cf.sinks else None
        for bi in range(bsz):
            k = jnp.concatenate([kp_ref[bi], kc_ref[bi]], axis=0)
            v = jnp.concatenate([vp_ref[bi], vc_ref[bi]], axis=0)
            k_ops, v_ops = _kv_operands(cf, k), _kv_operands(cf, v)
            for qb in range(cf.nq):
                cols = slice(256 * qb, 256 * (qb + 1))
                kb, vb = k_ops[qb], v_ops[qb]
                q4 = _stack_heads(q_refs[qb][bi] * (HEAD_DIM ** -0.5), masks)
                s = _nt(q4, kb) + bias_refs[qb][...]
                m = jnp.max(s, axis=-1, keepdims=True)
                if cf.sinks:
                    sk = sinks4[qb]
                    m = jnp.maximum(m, sk)
                p = jnp.exp(s - m)
                den = jnp.sum(p, axis=-1, keepdims=True)
                if cf.sinks:
                    den = den + jnp.exp(sk - m)
                pn = (p * (1.0 / den)).astype(BF16)
                p_lanes = jnp.concatenate([pn[BLOCK * h:BLOCK * (h + 1)] for h in range(HEADS_PER_BLOCK)], axis=1)
                v4 = jnp.concatenate([jnp.where(mk, vb, jnp.zeros_like(vb)) for mk in kv_masks], axis=0)
                o_ref[bi, :, cols] = _nn(p_lanes, v4)
                lse_ref[bi, :, cols] = _unstack_heads(
                    jnp.broadcast_to(m + jnp.log(den), (HEADS_PER_BLOCK * BLOCK, 256)), masks)

    in_specs = _att_in_specs(cf, bsz)
    args = [zf] * (cf.nq + 4) + [bias] * cf.nq + ([sinks] * cf.nq if cf.sinks else [])
    out = pl.BlockSpec((bsz, BLOCK, cf.wq), lambda r, j: (0, j, r))
    shape = jax.ShapeDtypeStruct((bsz, l, cf.d * cf.wq), F32)
    return _call(
        body, name=name, grid=(cf.d, nb), in_specs=in_specs, out_specs=[out, out], out_shape=[shape, shape],
        sem=("parallel", "arbitrary"), args=args, hosted=hosted)


def _attn_bwd(cf, zf, bias, sinks, stats, dy, name, hosted=()):
    bsz, l, _ = zf.shape
    nb = l // BLOCK

    def body(*refs):
        refs = list(refs)
        q_refs = [refs.pop(0) for _ in range(cf.nq)]
        kp_ref, kc_ref, vp_ref, vc_ref = [refs.pop(0) for _ in range(4)]
        bias_refs = [refs.pop(0) for _ in range(cf.nq)]
        sink_refs = [refs.pop(0) for _ in range(cf.nq)] if cf.sinks else None
        st_ref, dy_ref, dq_ref, dk_ref, dv_ref, dbias_ref = [refs.pop(0) for _ in range(6)]
        dsink_ref = refs.pop(0) if cf.sinks else None
        dk_acc, dv_acc = refs
        r, j = pl.program_id(0), pl.program_id(1)

        @pl.when((r == 0) & (j == 0))
        def _():
            dbias_ref[...] = jnp.zeros_like(dbias_ref)
            if cf.sinks:
                dsink_ref[...] = jnp.zeros_like(dsink_ref)

        @pl.when(j == 0)
        def _():
            dk_acc[...] = jnp.zeros_like(dk_acc)
            dv_acc[...] = jnp.zeros_like(dv_acc)

        masks = _head_masks(BLOCK)
        cur = pl.ds(pl.multiple_of(j * BLOCK, BLOCK), BLOCK)
        prev = pl.ds(pl.multiple_of(jnp.maximum(j - 1, 0) * BLOCK, BLOCK), BLOCK)
        sinks4 = [_row_value(ref[...]) for ref in sink_refs] if cf.sinks else None
        ds_sum, dsink_sum = [None] * cf.nq, [None] * cf.nq
        for bi in range(bsz):
            k = jnp.concatenate([kp_ref[bi], kc_ref[bi]], axis=0)
            v = jnp.concatenate([vp_ref[bi], vc_ref[bi]], axis=0)
            dk_blocks, dv_blocks = [], []
            k_ops, v_ops = _kv_operands(cf, k), _kv_operands(cf, v)
            for qb in range(cf.nq):
                cols = slice(256 * qb, 256 * (qb + 1))
                kb, vb = k_ops[qb], v_ops[qb]
                q4 = _stack_heads(q_refs[qb][bi] * (HEAD_DIM ** -0.5), masks)
                st = st_ref[bi, :, 128 * qb:128 * (qb + 1)]
                lt4, e4 = _read_stats(st, 0), _read_stats(st, 1)
                pa = jnp.exp(_nt(q4, kb) + bias_refs[qb][...] - lt4)
                dy = dy_ref[bi, :, cols]
                dy4 = _stack_heads(dy, masks)
                ds = pa * (_nt(dy4, vb) - e4)
                ds_sum[qb] = ds if ds_sum[qb] is None else ds_sum[qb] + ds
                if cf.sinks:
                    dsk = jnp.exp(sinks4[qb] - lt4) * e4
                    dsink_sum[qb] = dsk if dsink_sum[qb] is None else dsink_sum[qb] + dsk
                dsb = ds.astype(BF16)
                dq_ref[bi, :, cols] = (_unstack_heads(_nn(dsb, kb), masks) * (HEAD_DIM ** -0.5)).astype(dq_ref.dtype)
                dk_blocks.append(_tn(dsb, q4))
                dv_blocks.append(_tn(pa.astype(BF16), dy4))
            dk_new, dv_new = _kv_fold(cf, dk_blocks), _kv_fold(cf, dv_blocks)
            dk_acc[bi, cur, :] += dk_new[BLOCK:]
            dv_acc[bi, cur, :] += dv_new[BLOCK:]
            dk_acc[bi, prev, :] += dk_new[:BLOCK]
            dv_acc[bi, prev, :] += dv_new[:BLOCK]
        for qb in range(cf.nq):
            rows = slice(HEADS_PER_BLOCK * BLOCK * qb, HEADS_PER_BLOCK * BLOCK * (qb + 1))
            dbias_ref[rows, :] += ds_sum[qb]
            if cf.sinks:
                dsink_ref[rows, :] -= dsink_sum[qb]

        @pl.when(j == nb - 1)
        def _():
            dk_ref[...] = dk_acc[...].astype(dk_ref.dtype)
            dv_ref[...] = dv_acc[...].astype(dv_ref.dtype)

    tok = pl.BlockSpec((bsz, BLOCK, cf.wq), lambda r, j: (0, j, r))
    in_specs = _att_in_specs(cf, bsz) + [pl.BlockSpec((bsz, BLOCK, cf.wq // 2), lambda r, j: (0, j, r)), tok]
    args = [zf] * (cf.nq + 4) + [bias] * cf.nq + ([sinks] * cf.nq if cf.sinks else []) + [stats, dy]
    seq = pl.BlockSpec((bsz, l, cf.wkv), lambda r, j: (0, 0, r))
    out_specs = [tok, seq, seq, pl.BlockSpec((cf.nh * BLOCK, 2 * BLOCK), lambda r, j: (0, 0))]
    grad_dtype = BF16 if cf.d == 1 else F32
    out_shape = [jax.ShapeDtypeStruct((bsz, l, cf.d * cf.wq), grad_dtype),
                 jax.ShapeDtypeStruct((bsz, l, cf.d * cf.wkv), grad_dtype),
                 jax.ShapeDtypeStruct((bsz, l, cf.d * cf.wkv), grad_dtype),
                 jax.ShapeDtypeStruct((cf.nh * BLOCK, 2 * BLOCK), F32)]
    if cf.sinks:
        out_specs.append(pl.BlockSpec((cf.nh * BLOCK, 128), lambda r, j: (0, 0)))
        out_shape.append(jax.ShapeDtypeStruct((cf.nh * BLOCK, 128), F32))
    return _call(
        body, name=name, grid=(cf.d, nb), in_specs=in_specs, out_specs=out_specs, out_shape=out_shape,
        scratch=[pltpu.VMEM((bsz, l, cf.wkv), F32), pltpu.VMEM((bsz, l, cf.wkv), F32)],
        sem=("arbitrary", "arbitrary"), args=args, hosted=hosted)


def _merge_fwd(o_a, lse_a, o_b, zg, h, wa_t, wb_t, wout, name, hosted=()):
    t = h.shape[0]
    tm = 512

    def body(o1, o2, o3, l1, l2, l3, ob_ref, zg_ref, h_ref, wa_ref, wb_ref, wo_ref, h2_ref, ya_ref, lt_ref, *slabs):
        o = [_unfold(ref, slabs[i], d) for i, (ref, d) in enumerate(zip((o1, o2, o3), DILATIONS))]
        l = [_unfold(ref, slabs[3 + i], d) for i, (ref, d) in enumerate(zip((l1, l2, l3), DILATIONS))]
        m = jnp.maximum(jnp.maximum(l[0], l[1]), l[2])
        e1, e2, e3 = jnp.exp(l[0] - m), jnp.exp(l[1] - m), jnp.exp(l[2] - m)
        se = e1 + e2 + e3
        ya = (e1 / se) * o[0] + (e2 / se) * o[1] + (e3 / se) * o[2]
        ya_ref[...] = ya
        lt_ref[...] = m + jnp.log(se)
        pa = _nt(ya.astype(BF16), wa_ref[...])
        pb = _nt(ob_ref[...].astype(BF16), wb_ref[...])
        merged = (jax.nn.sigmoid(zg_ref[:, :D_MODEL].astype(F32)) * pa
                  + jax.nn.sigmoid(zg_ref[:, D_MODEL:].astype(F32)) * pb)
        h2_ref[...] = h_ref[...] + _nn(merged.astype(BF16), wo_ref[...])

    row = lambda w: pl.BlockSpec((tm, w), lambda i: (i, 0))
    folded = [pl.BlockSpec((tm // d, d * 256), lambda i: (i, 0)) for d in DILATIONS]
    full = _resident
    return _call(
        body, name=name, grid=(t // tm,),
        in_specs=folded + folded + [row(512), row(GATE_W), row(D_MODEL), full(wa_t), full(wb_t), full(wout)],
        out_specs=[row(D_MODEL), row(256), row(256)],
        out_shape=[jax.ShapeDtypeStruct((t, D_MODEL), F32), jax.ShapeDtypeStruct((t, 256), F32),
                   jax.ShapeDtypeStruct((t, 256), F32)],
        scratch=[pltpu.VMEM((2, tm, 128), F32)] * 6,
        sem=("parallel",), args=(*o_a, *lse_a, o_b, zg, h, wa_t, wb_t, wout), hosted=hosted)


STAT_LANES = 32


def _head_stats(lt, dy, y):
    rows, w = dy.shape
    lane = lax.broadcasted_iota(jnp.int32, (rows, w), 1)
    out_lane = lax.broadcasted_iota(jnp.int32, (rows, w // 2), 1)
    prod = dy * y
    out = jnp.zeros((rows, w // 2), F32)
    for hd in range(w // HEAD_DIM):
        mine = lane // HEAD_DIM == hd
        dot = jnp.sum(jnp.where(mine, prod, 0.0), axis=-1, keepdims=True)
        both = jnp.where(out_lane % STAT_LANES < STAT_LANES // 2, _row_value(jnp.where(mine, lt, NEG)), dot)
        out = jnp.where(out_lane // STAT_LANES == hd, both, out)
    return out


def _read_stats(st, part):
    lane = lax.broadcasted_iota(jnp.int32, st.shape, 1)
    half = (lane % STAT_LANES < STAT_LANES // 2) == (part == 0)
    return jnp.concatenate([_row_value(jnp.where((lane // STAT_LANES == hd) & half, st, NEG))
                            for hd in range(HEADS_PER_BLOCK)], axis=0)


def _merge_bwd(dh, ya, lse_tot, o_b, lse_b, zg, wa_t, wb_t, wout, name, hosted=()):
    t = dh.shape[0]
    tm = 512

    def body(dh_ref, ya_ref, lt_ref, ob_ref, lb_ref, zg_ref, wa_ref, wb_ref, wo_ref,
             mg_ref, dpa_ref, dpb_ref, dzg_ref, dy1, dy2, dy3, st1, st2, st3, dyb_ref, stb_ref, *slabs):
        dm = _nt(dh_ref[...].astype(BF16), wo_ref[...])
        pa = _nt(ya_ref[...].astype(BF16), wa_ref[...])
        pb = _nt(ob_ref[...].astype(BF16), wb_ref[...])
        sa = jax.nn.sigmoid(zg_ref[:, :D_MODEL].astype(F32))
        sb = jax.nn.sigmoid(zg_ref[:, D_MODEL:].astype(F32))
        mg_ref[...] = (sa * pa + sb * pb).astype(BF16)
        dpa = (dm * sa).astype(BF16)
        dpb = (dm * sb).astype(BF16)
        dpa_ref[...] = dpa
        dpb_ref[...] = dpb
        dzg_ref[:, :D_MODEL] = (dm * pa * (sa * (1.0 - sa))).astype(BF16)
        dzg_ref[:, D_MODEL:] = (dm * pb * (sb * (1.0 - sb))).astype(BF16)
        dya = _nn(dpa, wa_ref[...])
        dyb = _nn(dpb, wb_ref[...])
        stats = _head_stats(lt_ref[...], dya, ya_ref[...])
        for i, d in enumerate(DILATIONS):
            _fold(dya, slabs[i], (dy1, dy2, dy3)[i], d)
            _fold(stats, slabs[3 + i], (st1, st2, st3)[i], d)
        dyb_ref[...] = dyb.astype(BF16)
        stb_ref[...] = _head_stats(lb_ref[...], dyb, ob_ref[...])

    row = lambda w: pl.BlockSpec((tm, w), lambda i: (i, 0))
    folded = [pl.BlockSpec((tm // d, d * 256), lambda i: (i, 0)) for d in DILATIONS]
    full = _resident
    sds = jax.ShapeDtypeStruct
    dy_shape = [sds((t // d, d * 256), BF16) for d in DILATIONS]
    st_shape = [sds((t // d, d * 128), F32) for d in DILATIONS]
    st_specs = [pl.BlockSpec((tm // d, d * 128), lambda i: (i, 0)) for d in DILATIONS]
    return _call(
        body, name=name, grid=(t // tm,),
        in_specs=[row(D_MODEL), row(256), row(256), row(512), row(512), row(GATE_W), full(wa_t), full(wb_t),
                  full(wout)],
        out_specs=[row(D_MODEL), row(D_MODEL), row(D_MODEL), row(GATE_W)] + folded + st_specs + [row(512), row(256)],
        out_shape=[sds((t, D_MODEL), BF16), sds((t, D_MODEL), BF16), sds((t, D_MODEL), BF16),
                   sds((t, GATE_W), BF16)] + dy_shape + st_shape + [sds((t, 512), BF16), sds((t, 256), F32)],
        scratch=[pltpu.VMEM((2, tm, 128), F32)] * 6,
        sem=("parallel",), args=(dh, ya, lse_tot, o_b, lse_b, zg, wa_t, wb_t, wout), hosted=hosted)


def _pair_sum(grad, got, name):
    _, _, r, cdim = grad.shape
    core = lax.axis_index("c").astype(jnp.int32).reshape(1)

    def body(core_ref, g_ref, s_ref, o_ref):
        o_ref[...] = (g_ref[...] + s_ref[...]).astype(BF16)

    return pl.pallas_call(
        body, name=name,
        grid_spec=pltpu.PrefetchScalarGridSpec(
            num_scalar_prefetch=1, grid=(N_CHIP,),
            in_specs=[pl.BlockSpec((None, None, r, cdim), lambda q, core_ref: (q, core_ref[0], 0, 0)),
                      pl.BlockSpec((None, None, r, cdim), lambda q, core_ref: (q, 0, 0, 0))],
            out_specs=pl.BlockSpec((None, r, cdim), lambda q, core_ref: (q, 0, 0))),
        out_shape=jax.ShapeDtypeStruct((N_CHIP, r, cdim), BF16),
        compiler_params=_params("parallel"),
    )(core, grad, got)


def _adamw_math(w, g, m, v):
    m = ADAM_B1 * m + (1.0 - ADAM_B1) * g
    v = ADAM_B2 * v + (1.0 - ADAM_B2) * jnp.square(g)
    m_hat = m / (1.0 - ADAM_B1 ** ADAM_STEP)
    v_hat = v / (1.0 - ADAM_B2 ** ADAM_STEP)
    delta = -ADAM_LR * (m_hat / (jnp.sqrt(v_hat) + ADAM_EPS) + ADAM_WD * w)
    return delta, m, v


def _update(w, parts, m, v, transposed, name, hosted=()):
    r, c = parts.shape[1:]

    def body(w_ref, p_ref, m_ref, v_ref, g_ref, d_ref, m2_ref, v2_ref):
        def total(rows):
            return ((p_ref[0, rows].astype(F32) + p_ref[1, rows].astype(F32)) + p_ref[2, rows].astype(F32)) \
                + p_ref[3, rows].astype(F32)

        def update(at, g):
            g_ref[at] = g
            d_ref[at], m2_ref[at], v2_ref[at] = _adamw_math(w_ref[at], g, m_ref[at], v_ref[at])

        if not transposed:
            update((slice(None), slice(None)), total(slice(None)))
            return
        for r0 in range(0, r, 128):
            n = min(128, r - r0)
            gt = total(slice(r0, r0 + n))
            if n < 128:
                gt = jnp.concatenate([gt, jnp.zeros((128 - n, c), F32)], axis=0)
            update((slice(None), slice(r0, r0 + n)), gt.T[:, :n])

    sds = jax.ShapeDtypeStruct(w.shape, F32)
    vm = pl.BlockSpec(memory_space=pltpu.VMEM)
    return _call(body, name=name, grid=(1,), in_specs=[vm] * 4, out_specs=[vm] * 4, out_shape=[sds] * 4,
                 sem=("arbitrary",), args=(w, parts, m, v), hosted=hosted)


SMALL_ROWS = 80


def _small_update(g, w, m, v, name):
    def body(g_ref, w_ref, m_ref, v_ref, gs_ref, d_ref, m2_ref, v2_ref, got_ref, send_sems, recv_sems):
        x, y, c = _place()
        me = 4 * x + 2 * y + c
        got_ref[me] = g_ref[...]
        copies = []
        for k in range(1, N_DEV):
            peer = (x ^ (k >> 2), y ^ ((k >> 1) & 1), c ^ (k & 1))
            cp = pltpu.make_async_remote_copy(
                src_ref=g_ref, dst_ref=got_ref.at[me], send_sem=send_sems.at[k - 1], recv_sem=recv_sems.at[k - 1],
                device_id=peer, device_id_type=MESH)
            cp.start()
            copies.append(cp)
        for cp in copies:
            cp.wait()
        total = got_ref[0]
        for k in range(1, N_DEV):
            total = total + got_ref[k]
        gs_ref[...] = total
        d_ref[...], m2_ref[...], v2_ref[...] = _adamw_math(w_ref[...], total, m_ref[...], v_ref[...])

    sds = jax.ShapeDtypeStruct((SMALL_ROWS, 128), F32)
    vm = pl.BlockSpec(memory_space=pltpu.VMEM)
    return pl.pallas_call(
        body, name=name, in_specs=[vm] * 4, out_specs=[vm] * 4, out_shape=[sds] * 4,
        scratch_shapes=[pltpu.VMEM((N_DEV, SMALL_ROWS, 128), F32), pltpu.SemaphoreType.DMA((N_DEV - 1,)),
                        pltpu.SemaphoreType.DMA((N_DEV - 1,))],
    )(g, w, m, v)


def _pack_small(gains, b_in, rel_bias, sinks, last):
    rows = [a.reshape(8, 128) for a in gains] + [b_in.reshape(40, 128), rel_bias.reshape(5, 128),
                                                 jnp.pad(sinks.reshape(1, 8), ((0, 0), (0, 120))), last]
    rows.append(jnp.zeros((SMALL_ROWS - 79, 128), F32))
    return jnp.concatenate(rows, axis=0)


def _unpack_small(p, like):
    out = [p[8 * i:8 * i + 8].reshape(like[i].shape) for i in range(4)]
    out.append(p[32:72].reshape(like[4].shape))
    out.append(p[72:77].reshape(like[5].shape))
    out.append(p[77, :8].reshape(like[6].shape))
    return out


def kernel(x, ffn1_norm, ffn1_w_gate, ffn1_w_up, ffn1_w_down, mix_norm, w_in, b_in, w_branch_a, w_branch_b, w_out, sinks, rel_bias, ffn2_norm, ffn2_w_gate, ffn2_w_up, ffn2_w_down, final_norm, loss_target, m_ffn1_norm, m_ffn1_w_gate, m_ffn1_w_up, m_ffn1_w_down, m_mix_norm, m_w_in, m_b_in, m_w_branch_a, m_w_branch_b, m_w_out, m_sinks, m_rel_bias, m_ffn2_norm, m_ffn2_w_gate, m_ffn2_w_up, m_ffn2_w_down, m_final_norm, v_ffn1_norm, v_ffn1_w_gate, v_ffn1_w_up, v_ffn1_w_down, v_mix_norm, v_w_in, v_b_in, v_w_branch_a, v_w_branch_b, v_w_out, v_sinks, v_rel_bias, v_ffn2_norm, v_ffn2_w_gate, v_ffn2_w_up, v_ffn2_w_down, v_final_norm):
    bsz, seq, _ = x.shape
    t = bsz * seq
    xt = x.reshape(t, D_MODEL)
    target = loss_target.reshape(t, D_MODEL)

    big = [("ffn1_w_gate", ffn1_w_gate, m_ffn1_w_gate, v_ffn1_w_gate, True),
           ("ffn1_w_up", ffn1_w_up, m_ffn1_w_up, v_ffn1_w_up, True),
           ("ffn1_w_down", ffn1_w_down, m_ffn1_w_down, v_ffn1_w_down, False),
           ("w_in", w_in, m_w_in, v_w_in, True),
           ("w_branch_a", w_branch_a, m_w_branch_a, v_w_branch_a, True),
           ("w_branch_b", w_branch_b, m_w_branch_b, v_w_branch_b, True),
           ("w_out", w_out, m_w_out, v_w_out, False),
           ("ffn2_w_gate", ffn2_w_gate, m_ffn2_w_gate, v_ffn2_w_gate, True),
           ("ffn2_w_up", ffn2_w_up, m_ffn2_w_up, v_ffn2_w_up, True),
           ("ffn2_w_down", ffn2_w_down, m_ffn2_w_down, v_ffn2_w_down, False)]
    shard = {nm: (w[0].T if tr else w[0]).astype(BF16) for nm, w, _, _, tr in big}
    full = {}

    def gather(names):
        return _Gather([shard[nm] for nm in names])

    def keep(names, got):
        for nm, gw in zip(names, got):
            full[nm] = gw.reshape(-1, gw.shape[-1])

    ffn1_names = ["ffn1_w_gate", "ffn1_w_up", "ffn1_w_down"]
    mix_names = ["w_in", "w_branch_a", "w_branch_b", "w_out"]
    ffn2_names = ["ffn2_w_gate", "ffn2_w_up", "ffn2_w_down"]
    g1, gm, g2, gf = ffn1_norm, mix_norm, ffn2_norm, final_norm.reshape(1, D_MODEL)

    buckets = _bucket_tiles()
    (bias,), (got,) = _bias_build(rel_bias, buckets, hosted=[gather(ffn1_names)])
    keep(ffn1_names, got)
    (h1, n1, a1, b1, hff1), (got,) = _ffn_fwd(xt, g1, full["ffn1_w_gate"], full["ffn1_w_up"], full["ffn1_w_down"], "ffn1_fwd",
                                hosted=[gather(mix_names)])
    keep(mix_names, got)
    (u, zq, zg, *zdil), (got,) = _inproj_fwd(h1, gm, full["w_in"], b_in, "inproj_fwd", hosted=[gather(ffn2_names[:1])])
    keep(ffn2_names[:1], got)
    sink_rows = jnp.broadcast_to(sinks.reshape(8, 1, 1), (8, BLOCK, 128)).reshape(8 * BLOCK, 128)
    cfs = [_Att(i) for i in range(4)]
    zfold = []
    for i, cf in enumerate(cfs):
        if cf.d == 1:
            zfold.append(zq.reshape(bsz, seq, QKV_W))
        else:
            zfold.append(zdil[i - 1].reshape(bsz, seq // cf.d, cf.d * cf.row_w))
    att = [None] * 4
    for i in (3, 0, 1, 2):
        cf = cfs[i]
        (o, lse), got = _attn_fwd(cf, zfold[i], bias, sink_rows, f"attn{i}_fwd",
                                  hosted=[gather(ffn2_names[1:2])] if i == 3 else ())
        if i == 3:
            keep(ffn2_names[1:2], got[0])
        att[i] = (o.reshape(t // cf.d, cf.d * cf.wq), lse.reshape(t // cf.d, cf.d * cf.wq))
    o_b, lse_b = att[3]
    (h2, ya, lse_tot), (got,) = _merge_fwd([a[0] for a in att[:3]], [a[1] for a in att[:3]], o_b, zg, h1,
                                           full["w_branch_a"], full["w_branch_b"], full["w_out"], "merge_fwd",
                                           hosted=[gather(ffn2_names[2:])])
    keep(ffn2_names[2:], got)
    (dh3, n2, a2, b2, hff2, loss_part, dgf), _ = _ffn_fwd(
        h2, g2, full["ffn2_w_gate"], full["ffn2_w_up"], full["ffn2_w_down"], "ffn2_fwd", head=(gf, target))

    grads, pair, from_chips = {}, {}, {}

    def by_owner(nm):
        return grads[nm].reshape(N_CHIP, 2, -1, grads[nm].shape[-1])

    def to_core(names):
        return _CoreExchange([by_owner(nm) for nm in names])

    def pair_up(names, got):
        for nm, sib in zip(names, got):
            pair[nm] = _pair_sum(by_owner(nm), sib, f"pair_sum_{nm}")

    def to_chips(names):
        return _ChipExchange([pair[nm] for nm in names])

    def landed(names, got):
        for nm, parts in zip(names, got):
            from_chips[nm] = parts

    dh2, dg2, da, db = _ffn_bwd(h2, a2, b2, g2, dh3, full["ffn2_w_gate"], full["ffn2_w_up"],
                                          full["ffn2_w_down"], "ffn2_bwd")
    grads["ffn2_w_gate"], _ = _tn_matmul(da, n2, 1408, "ffn2_dgate")
    grads["ffn2_w_up"], _ = _tn_matmul(db, n2, 1408, "ffn2_dup")
    grads["ffn2_w_down"], _ = _tn_matmul(hff2, dh3, 1408, "ffn2_ddown", scale=0.5)
    (merged, dpa, dpb, dzg, *cot), (got,) = _merge_bwd(
        dh2, ya, lse_tot, o_b, lse_b, zg, full["w_branch_a"], full["w_branch_b"], full["w_out"], "merge_bwd",
        hosted=[to_core(ffn2_names)])
    dys, sts = cot[0:3] + [cot[6]], cot[3:6] + [cot[7]]
    pair_up(ffn2_names, got)
    dq, dk, dv, dbias, dsink = [None] * 4, [None] * 4, [None] * 4, [None] * 4, None
    for i in (3, 0, 1, 2):
        cf = cfs[i]
        shp = (bsz, seq // cf.d, cf.d * cf.wq)
        hosted = {3: lambda: [to_chips(ffn2_names[:2])], 0: lambda: [to_chips(ffn2_names[2:])]}.get(i, list)()
        res, got = _attn_bwd(cf, zfold[i], bias, sink_rows, sts[i].reshape(shp[:2] + (shp[2] // 2,)), dys[i].reshape(shp),
                             f"attn{i}_bwd", hosted=hosted)
        if i == 3:
            landed(ffn2_names[:2], got[0])
        elif i == 0:
            landed(ffn2_names[2:], got[0])
        dq[i] = res[0].reshape(t // cf.d, cf.d * cf.wq)
        dk[i] = res[1].reshape(t // cf.d, cf.d * cf.wkv)
        dv[i] = res[2].reshape(t // cf.d, cf.d * cf.wkv)
        dbias[i] = res[3]
        if cf.sinks:
            dsink = res[4]
    pieces = dq[:3] + dk[:3] + dv[:3] + [dq[3], dk[3], dv[3], dzg]
    dh1, dz, db_in, dgm = _inproj_bwd(pieces, h1, gm, dh2, full["w_in"], "inproj_bwd")
    dx, dg1, da, db = _ffn_bwd(xt, a1, b1, g1, dh1, full["ffn1_w_gate"], full["ffn1_w_up"],
                                         full["ffn1_w_down"], "ffn1_bwd")
    grads["w_in"], _ = _tn_matmul(dz, u, 1280, "dw_in")
    grads["ffn1_w_down"], (got,) = _tn_matmul(hff1, dh1, 1408, "ffn1_ddown", hosted=[to_core(["w_in"])], scale=0.5)
    pair_up(["w_in"], got)
    grads["ffn1_w_gate"], got = _tn_matmul(da, n1, 1408, "ffn1_dgate",
                                           hosted=[to_chips(["w_in"]), to_core(["ffn1_w_down"])])
    landed(["w_in"], got[0])
    pair_up(["ffn1_w_down"], got[1])
    grads["ffn1_w_up"], got = _tn_matmul(db, n1, 1408, "ffn1_dup",
                                         hosted=[to_chips(["ffn1_w_down"]), to_core(["ffn1_w_gate"])])
    landed(["ffn1_w_down"], got[0])
    pair_up(["ffn1_w_gate"], got[1])
    grads["w_out"], got = _tn_matmul(merged, dh2, 1024, "dw_out",
                                     hosted=[to_chips(["ffn1_w_gate"]), to_core(["ffn1_w_up"])])
    landed(["ffn1_w_gate"], got[0])
    pair_up(["ffn1_w_up"], got[1])
    grads["w_branch_b"], got = _tn_matmul(dpb, o_b, 1024, "dw_branch_b",
                                          hosted=[to_chips(["ffn1_w_up"]), to_core(["w_out"])])
    landed(["ffn1_w_up"], got[0])
    pair_up(["w_out"], got[1])
    grads["w_branch_a"], got = _tn_matmul(dpa, ya, 1024, "dw_branch_a",
                                          hosted=[to_chips(["w_out"]), to_core(["w_branch_b"])])
    landed(["w_out"], got[0])
    pair_up(["w_branch_b"], got[1])
    (dtable, dsinks), got = _bias_reduce(jnp.concatenate(dbias, axis=0), buckets, dsink,
                                         hosted=[to_chips(["w_branch_b"]), to_core(["w_branch_a"])])
    landed(["w_branch_b"], got[0])
    pair_up(["w_branch_a"], got[1])

    out_g, out_d, out_m, out_v = {}, {}, {}, {}
    for idx, (nm, w, m, v, tr) in enumerate(big):
        around = tr and w.shape[-1] % 128 != 0
        wmv = [a[0].T if around else a[0] for a in (w, m, v)]
        res, got = _update(wmv[0], from_chips[nm], wmv[1], wmv[2], tr and not around, f"update_{nm}",
                           hosted=[to_chips(["w_branch_a"])] if idx == 0 else ())
        if idx == 0:
            landed(["w_branch_a"], got[0])
        out_g[nm], out_d[nm], out_m[nm], out_v[nm] = [(a.T if around else a)[None] for a in res]

    small = [("ffn1_norm", ffn1_norm, m_ffn1_norm, v_ffn1_norm), ("mix_norm", mix_norm, m_mix_norm, v_mix_norm),
             ("ffn2_norm", ffn2_norm, m_ffn2_norm, v_ffn2_norm), ("final_norm", final_norm, m_final_norm, v_final_norm),
             ("b_in", b_in, m_b_in, v_b_in), ("rel_bias", rel_bias, m_rel_bias, v_rel_bias),
             ("sinks", sinks, m_sinks, v_sinks)]
    zero_row = jnp.zeros((1, 128), F32)
    pack = lambda arrs, last: _pack_small(arrs[:4], arrs[4], arrs[5], arrs[6], last)
    g_small = pack([dg1, dgm, dg2, dgf, db_in, dtable[:, :TOTAL_HEADS], dsinks[:, 0]], loss_part)
    packed = [pack([s[k] for s in small], zero_row) for k in (1, 2, 3)]
    gs, ds, ms, vs = _small_update(g_small, *packed, "small_update")
    like = [s[1] for s in small]
    for nm_s, g_, d_, m_, v_ in zip([s[0] for s in small], _unpack_small(gs, like), _unpack_small(ds, like),
                                    _unpack_small(ms, like), _unpack_small(vs, like)):
        out_g[nm_s], out_d[nm_s], out_m[nm_s], out_v[nm_s] = g_, d_, m_, v_
    loss = gs[78, 0]

    order = ["ffn1_norm", "ffn1_w_gate", "ffn1_w_up", "ffn1_w_down", "mix_norm", "w_in", "b_in", "w_branch_a",
             "w_branch_b", "w_out", "sinks", "rel_bias", "ffn2_norm", "ffn2_w_gate", "ffn2_w_up", "ffn2_w_down",
             "final_norm"]
    return (loss, dx.reshape(x.shape), *[out_g[k] for k in order], *[out_d[k] for k in order],
            *[out_m[k] for k in order], *[out_v[k] for k in order])
```

```python
import functools
import math

import numpy as np
import jax
import jax.numpy as jnp
from jax import lax
from jax.experimental import pallas as pl
from jax.experimental.pallas import tpu as pltpu

D_MODEL = 1024
D_FF = 2816
FF_CHUNK = 256
HEAD_DIM = 64
BLOCK = 128
N_BUCKETS = 32
MAX_DISTANCE = 2048
A_HEADS = 12
TOTAL_HEADS = 20
DIL_GROUPS = ((128, 1), (512, 4), (2048, 16))
B_WINDOW = 128
QKV_W = 3072
GATE_W = 2048
D_IN = QKV_W + GATE_W
EPS = 1e-6
NEG = -1e30
N_DEV = 8
N_CHIP = 4
ADAM_LR, ADAM_B1, ADAM_B2, ADAM_EPS, ADAM_WD, ADAM_STEP = 0.001, 0.9, 0.999, 1e-08, 0.01, 10
VMEM_LIMIT = 56 * 1024 * 1024
MESH = pl.DeviceIdType.MESH
BF16 = jnp.bfloat16
F32 = jnp.float32
ANY = pl.BlockSpec(memory_space=pl.ANY)


def _params(*sem):
    return pltpu.CompilerParams(dimension_semantics=sem, vmem_limit_bytes=VMEM_LIMIT)


def _resident(a):
    return pl.BlockSpec(a.shape, lambda i: (0, 0), pipeline_mode=pl.Buffered(1))


def _place():
    return lax.axis_index("x"), lax.axis_index("y"), lax.axis_index("c")


class _Gather:
    def __init__(self, shards):
        self.ins = list(shards)
        n = self.n = len(shards)
        self.out_shape = [jax.ShapeDtypeStruct((N_DEV,) + s.shape, s.dtype) for s in shards]
        self.scratch = [pltpu.SemaphoreType.DMA((7 * n,)), pltpu.SemaphoreType.DMA((7 * n,)),
                        pltpu.SemaphoreType.DMA((n,))]

    def _copies(self, ins, outs, sems):
        send_sems, recv_sems, local_sems = sems
        x, y, c = _place()
        me, sibling = (x, y, c), (x, y, 1 - c)
        chips = [(1 - x, y), (x, 1 - y), (1 - x, 1 - y)]

        def copy(i, k, block, to, src=None):
            dst = outs[i].at[4 * block[0] + 2 * block[1] + block[2]]
            return pltpu.make_async_remote_copy(
                src_ref=dst if src is None else src, dst_ref=dst, send_sem=send_sems.at[7 * i + k],
                recv_sem=recv_sems.at[7 * i + k], device_id=to, device_id_type=MESH)

        n = self.n
        south = c == 0
        relayed = (jnp.where(south, 1 - x, x), jnp.where(south, y, 1 - y), c)
        relay_to = (jnp.where(south, x, 1 - x), jnp.where(south, 1 - y, y), c)
        mine = [pltpu.make_async_copy(ins[i], outs[i].at[4 * x + 2 * y + c], local_sems.at[i]) for i in range(n)]
        first = [copy(i, 0, me, sibling, src=ins[i]) for i in range(n)]
        first += [copy(i, 1 + j, me, (*chips[j], c), src=ins[i]) for i in range(n) for j in range(2)]
        landed = [copy(i, 1 + j, (*chips[j], c), me) for j in range(2) for i in range(n)]
        passed = [copy(i, 4 + j, (*chips[j], c), sibling) for j in range(2) for i in range(n)]
        relays = [copy(i, 3, relayed, relay_to) for i in range(n)]
        diag_landed = [copy(i, 3, (*chips[2], c), me) for i in range(n)]
        diag_passed = [copy(i, 6, (*chips[2], c), sibling) for i in range(n)]
        from_sibling = [copy(i, 0, sibling, me) for i in range(n)]
        from_sibling += [copy(i, 4 + j, (*chip, 1 - c), me) for i in range(n) for j, chip in enumerate(chips)]
        return mine, first, landed, passed, relays, diag_landed, diag_passed, from_sibling

    def start(self, ins, outs, sems):
        mine, first = self._copies(ins, outs, sems)[:2]
        for cp in mine + first:
            cp.start()

    def mid(self, ins, outs, sems):
        _, _, landed, passed, relays, _, _, _ = self._copies(ins, outs, sems)
        for got in landed:
            got.wait_recv()
        for cp in relays + passed:
            cp.start()

    def end(self, ins, outs, sems):
        mine, first, _, passed, relays, diag_landed, diag_passed, from_sibling = self._copies(ins, outs, sems)
        for got, fwd in zip(diag_landed, diag_passed):
            got.wait_recv()
            fwd.start()
        for cp in from_sibling:
            cp.wait_recv()
        for cp in first + passed + relays + diag_passed:
            cp.wait_send()
        for cp in mine:
            cp.wait()


class _CoreExchange:
    def __init__(self, grads):
        self.ins = list(grads)
        n = self.n = len(grads)
        self.out_shape = [jax.ShapeDtypeStruct((N_CHIP, 1) + g.shape[2:], g.dtype) for g in grads]
        self.scratch = [pltpu.SemaphoreType.DMA((n,)), pltpu.SemaphoreType.DMA((n,))]

    def _copies(self, ins, outs, sems):
        x, y, c = _place()
        return [pltpu.make_async_remote_copy(
            src_ref=ins[i].at[:, pl.ds(1 - c, 1)], dst_ref=outs[i], send_sem=sems[0].at[i],
            recv_sem=sems[1].at[i], device_id=(x, y, 1 - c), device_id_type=MESH) for i in range(self.n)]

    def start(self, ins, outs, sems):
        for cp in self._copies(ins, outs, sems):
            cp.start()

    mid = None

    def end(self, ins, outs, sems):
        for cp in self._copies(ins, outs, sems):
            cp.wait()


class _ChipExchange:
    def __init__(self, parts):
        self.ins = list(parts)
        n = self.n = len(parts)
        self.out_shape = [jax.ShapeDtypeStruct(p.shape, p.dtype) for p in parts]
        self.scratch = [pltpu.SemaphoreType.DMA((3 * n,)), pltpu.SemaphoreType.DMA((3 * n,)),
                        pltpu.SemaphoreType.DMA((n,))]

    def _copies(self, ins, outs, sems):
        send_sems, recv_sems, local_sems = sems
        x, y, c = _place()
        my_chip = 2 * x + y
        copies = []
        for i in range(self.n):
            copies.append(pltpu.make_async_copy(ins[i].at[my_chip], outs[i].at[my_chip], local_sems.at[i]))
            for k, (qx, qy) in enumerate([(1 - x, y), (x, 1 - y), (1 - x, 1 - y)]):
                copies.append(pltpu.make_async_remote_copy(
                    src_ref=ins[i].at[2 * qx + qy], dst_ref=outs[i].at[my_chip], send_sem=send_sems.at[3 * i + k],
                    recv_sem=recv_sems.at[3 * i + k], device_id=(qx, qy, c), device_id_type=MESH))
        return copies

    def start(self, ins, outs, sems):
        for cp in self._copies(ins, outs, sems):
            cp.start()

    mid = None

    def end(self, ins, outs, sems):
        for cp in self._copies(ins, outs, sems):
            cp.wait()


def _call(body, *, name, grid, in_specs, out_specs, out_shape, args, scratch=(), sem=None, hosted=()):
    n_in, n_out, n_scr = len(in_specs), len(out_specs), len(scratch)
    x_in = [len(p.ins) for p in hosted]
    x_scr = [len(p.scratch) for p in hosted]
    steps = int(np.prod(grid))

    def wrapped(*refs):
        refs = list(refs)
        ins, refs = refs[:n_in], refs[n_in:]
        x_ins = [[refs.pop(0) for _ in range(k)] for k in x_in]
        outs, refs = refs[:n_out], refs[n_out:]
        x_outs = [[refs.pop(0) for _ in range(k)] for k in x_in]
        scr, refs = refs[:n_scr], refs[n_scr:]
        x_sems = [[refs.pop(0) for _ in range(k)] for k in x_scr]
        step = 0
        for d in range(len(grid)):
            step = step * grid[d] + pl.program_id(d)

        def phase(which, at):
            fns = [(getattr(p, which), a) for p, a in zip(hosted, zip(x_ins, x_outs, x_sems)) if getattr(p, which)]
            if fns:
                @pl.when(step == at)
                def _():
                    for fn, a in fns:
                        fn(*a)

        phase("start", 0)
        if steps > 1:
            phase("mid", steps // 2)
        body(*ins, *outs, *scr)
        if steps == 1:
            phase("mid", 0)
        phase("end", steps - 1)

    results = pl.pallas_call(
        wrapped, name=name, grid=grid,
        in_specs=list(in_specs) + [ANY] * sum(x_in), out_specs=list(out_specs) + [ANY] * sum(x_in),
        out_shape=list(out_shape) + [s for p in hosted for s in p.out_shape],
        scratch_shapes=list(scratch) + [s for p in hosted for s in p.scratch],
        compiler_params=_params(*(("arbitrary",) * len(grid) if hosted else sem)),
    )(*args, *[a for p in hosted for a in p.ins])
    own, rest = list(results[:n_out]), list(results[n_out:])
    return own, [[rest.pop(0) for _ in range(k)] for k in x_in]


def _nt(a, b):
    return lax.dot_general(a, b, (((1,), (1,)), ((), ())), preferred_element_type=F32)


def _nn(a, b):
    return lax.dot_general(a, b, (((1,), (0,)), ((), ())), preferred_element_type=F32)


def _tn(a, b):
    return lax.dot_general(a, b, (((0,), (0,)), ((), ())), preferred_element_type=F32)


def _rms(x, g):
    r = lax.rsqrt(jnp.mean(x * x, axis=-1, keepdims=True) + EPS)
    return x * r, r


def _rms_bwd(dn, xhat, r, g):
    dg = jnp.sum(dn * xhat, axis=0, keepdims=True)
    dxh = dn * g
    dx = r * (dxh - xhat * jnp.mean(dxh * xhat, axis=-1, keepdims=True))
    return dx, dg


def _ffn_fwd(x, g, wg_t, wu_t, wd, name, hosted=(), head=None):
    t = x.shape[0]
    tm = 512 if head is None else 256

    def body(*refs):
        x_ref, g_ref, wg_ref, wu_ref, wd_ref = refs[:5]
        h_ref, n_ref, a_ref, b_ref, hff_ref = refs[-5 if head is None else -7:][:5]
        xhat, _ = _rms(x_ref[...], g_ref[...])
        n = (xhat * g_ref[...]).astype(BF16)
        n_ref[...] = n
        for c in range(0, D_FF, FF_CHUNK):
            cols = slice(c, c + FF_CHUNK)
            a = _nt(n, wg_ref[cols, :])
            b = _nt(n, wu_ref[cols, :])
            a_ref[:, cols] = a.astype(BF16)
            b_ref[:, cols] = b.astype(BF16)
            hff_ref[:, cols] = (a * jax.nn.sigmoid(a) * b).astype(BF16)
        h = x_ref[...] + 0.5 * _nn(hff_ref[...], wd_ref[...])
        if head is None:
            h_ref[...] = h
            return
        gf_ref, t_ref, loss_ref, dgf_ref = refs[5], refs[6], refs[-2], refs[-1]

        @pl.when(pl.program_id(0) == 0)
        def _():
            loss_ref[...] = jnp.zeros_like(loss_ref)
            dgf_ref[...] = jnp.zeros_like(dgf_ref)

        yhat, r = _rms(h, gf_ref[...])
        err = yhat * gf_ref[...] - t_ref[...]
        loss_ref[...] += 0.5 * jnp.sum(jnp.mean(err * err, axis=-1, keepdims=True), axis=0, keepdims=True)
        h_ref[...], dgf = _rms_bwd(err * (1.0 / D_MODEL), yhat, r, gf_ref[...])
        dgf_ref[...] += dgf

    row = pl.BlockSpec((tm, D_MODEL), lambda i: (i, 0))
    hid = pl.BlockSpec((tm, D_FF), lambda i: (i, 0))
    sds = jax.ShapeDtypeStruct
    in_specs = [row, _resident(g), _resident(wg_t), _resident(wu_t), _resident(wd)]
    out_specs = [row, row, hid, hid, hid]
    out_shape = [sds((t, D_MODEL), F32), sds((t, D_MODEL), BF16), sds((t, D_FF), BF16), sds((t, D_FF), BF16),
                 sds((t, D_FF), BF16)]
    args = (x, g, wg_t, wu_t, wd)
    if head is not None:
        in_specs += [_resident(head[0]), row]
        out_specs += [pl.BlockSpec((1, 128), lambda i: (0, 0)), pl.BlockSpec((1, D_MODEL), lambda i: (0, 0))]
        out_shape += [sds((1, 128), F32), sds((1, D_MODEL), F32)]
        args += tuple(head)
    return _call(body, name=name, grid=(t // tm,), in_specs=in_specs, out_specs=out_specs, out_shape=out_shape,
                 sem=("parallel",) if head is None else ("arbitrary",), args=args, hosted=hosted)


def _ffn_bwd(x, a_pre, b_pre, g, dh, wg_t, wu_t, wd, name):
    t = x.shape[0]
    tm = 512

    def body(x_ref, a_ref, b_ref, g_ref, dh_ref, wg_ref, wu_ref, wd_ref,
             dx_ref, dg_ref, da_ref, db_ref):
        @pl.when(pl.program_id(0) == 0)
        def _():
            dg_ref[...] = jnp.zeros_like(dg_ref)

        dhh = (0.5 * dh_ref[...]).astype(BF16)
        for c in range(0, D_FF, FF_CHUNK):
            cols = slice(c, c + FF_CHUNK)
            a = a_ref[:, cols].astype(F32)
            b = b_ref[:, cols].astype(F32)
            s = jax.nn.sigmoid(a)
            silu = a * s
            dhff = _nt(dhh, wd_ref[cols, :])
            da_ref[:, cols] = (dhff * b * (s * (1.0 + a * (1.0 - s)))).astype(BF16)
            db_ref[:, cols] = (dhff * silu).astype(BF16)
        dn = _nn(da_ref[...], wg_ref[...]) + _nn(db_ref[...], wu_ref[...])
        xhat, r = _rms(x_ref[...], g_ref[...])
        dx, dg = _rms_bwd(dn, xhat, r, g_ref[...])
        dx_ref[...] = dh_ref[...] + dx
        dg_ref[...] += dg

    row = pl.BlockSpec((tm, D_MODEL), lambda i: (i, 0))
    hid = pl.BlockSpec((tm, D_FF), lambda i: (i, 0))
    return pl.pallas_call(
        body, name=name, grid=(t // tm,),
        in_specs=[row, hid, hid, _resident(g), row, _resident(wg_t), _resident(wu_t), _resident(wd)],
        out_specs=[row, pl.BlockSpec((1, D_MODEL), lambda i: (0, 0)), hid, hid],
        out_shape=[jax.ShapeDtypeStruct((t, D_MODEL), F32), jax.ShapeDtypeStruct((1, D_MODEL), F32),
                   jax.ShapeDtypeStruct((t, D_FF), BF16), jax.ShapeDtypeStruct((t, D_FF), BF16)],
        compiler_params=_params("arbitrary"),
    )(x, a_pre, b_pre, g, dh, wg_t, wu_t, wd)


def _tn_matmul(a, b, rc, name, hosted=(), scale=None):
    t, r = a.shape
    c = b.shape[1]
    tk = min(t, 2048)

    def body(a_ref, b_ref, o_ref):
        @pl.when(pl.program_id(1) == 0)
        def _():
            o_ref[...] = jnp.zeros_like(o_ref)

        o_ref[...] += _tn(a_ref[...].astype(BF16), b_ref[...].astype(BF16))
        if scale is not None:
            @pl.when(pl.program_id(1) == t // tk - 1)
            def _():
                o_ref[...] *= scale

    (out,), got = _call(
        body, name=name, grid=(r // rc, t // tk),
        in_specs=[pl.BlockSpec((tk, rc), lambda i, k: (k, i)), pl.BlockSpec((tk, c), lambda i, k: (k, 0))],
        out_specs=[pl.BlockSpec((rc, c), lambda i, k: (i, 0))],
        out_shape=[jax.ShapeDtypeStruct((r, c), F32)],
        sem=("parallel", "arbitrary"), args=(a, b), hosted=hosted)
    return out, got


def _unfold(blk_ref, slab_ref, d):
    if d == 1:
        return blk_ref[...]
    n = blk_ref.shape[0]
    for r in range(d):
        for half in range(2):
            c0 = 256 * r + 128 * half
            slab_ref[half, pl.ds(r, n, stride=d), :] = blk_ref[:, c0:c0 + 128]
    return jnp.concatenate([slab_ref[0], slab_ref[1]], axis=1)


def _fold(x, slab_ref, out_ref, d):
    if d == 1:
        out_ref[...] = x.astype(out_ref.dtype)
        return
    n, w = out_ref.shape[0], x.shape[1]
    for part in range(w // 128):
        slab_ref[part] = x[:, 128 * part:128 * (part + 1)]
    for r in range(d):
        for part in range(w // 128):
            c0 = w * r + 128 * part
            out_ref[:, c0:c0 + 128] = slab_ref[part, pl.ds(r, n, stride=d), :].astype(out_ref.dtype)


DILATIONS = tuple(d for _, d in DIL_GROUPS)


PIECE_W = (256,) * 9 + (512, 128, 128, GATE_W)
PIECE_D = DILATIONS * 3 + (1, 1, 1, 1)


def _inproj_fwd(h, g, w_t, b_in, name, hosted=()):
    t = h.shape[0]
    tm, nc = 512, 256
    dilated = [(gi, d) for gi, d in enumerate(DILATIONS) if d > 1]

    def body(h_ref, g_ref, w_ref, b_ref, u_ref, zq_ref, zg_ref, *rest):
        zf_refs, slabs = rest[:len(dilated)], rest[len(dilated):]
        xhat, _ = _rms(h_ref[...], g_ref[...])
        u = (xhat * g_ref[...]).astype(BF16)
        u_ref[...] = u
        for c in range(D_IN // nc):
            z = _nt(u, w_ref[c * nc:(c + 1) * nc, :]) + b_ref[:, c * nc:(c + 1) * nc]
            if c < QKV_W // nc:
                zq_ref[:, c * nc:(c + 1) * nc] = z.astype(BF16)
            else:
                zg_ref[:, c * nc - QKV_W:(c + 1) * nc - QKV_W] = z.astype(BF16)
            part, gi = divmod(c, len(DILATIONS))
            for k, (gk, d) in enumerate(dilated):
                if part < 3 and gi == gk:
                    slab, n = slabs[3 * k + part], tm // d
                    slab[0] = z[:, :128]
                    slab[1] = z[:, 128:]
                    for r in range(d):
                        for half in range(2):
                            c0 = 768 * r + 256 * part + 128 * half
                            zf_refs[k][:, c0:c0 + 128] = slab[half, pl.ds(r, n, stride=d), :].astype(BF16)

    row = lambda w: pl.BlockSpec((tm, w), lambda i: (i, 0))
    full = _resident
    sds = jax.ShapeDtypeStruct
    return _call(
        body, name=name, grid=(t // tm,),
        in_specs=[row(D_MODEL), full(g), full(w_t), full(b_in)],
        out_specs=[row(D_MODEL), row(QKV_W), row(GATE_W)]
        + [pl.BlockSpec((tm // d, d * 768), lambda i: (i, 0)) for _, d in dilated],
        out_shape=[sds((t, D_MODEL), BF16), sds((t, QKV_W), BF16), sds((t, GATE_W), BF16)]
        + [sds((t // d, d * 768), BF16) for _, d in dilated],
        scratch=[pltpu.VMEM((2, tm, 128), F32)] * (3 * len(dilated)),
        sem=("parallel",), args=(h, g, w_t, b_in), hosted=hosted)


def _inproj_bwd(pieces, h, g, dh_res, w_t, name):
    t = h.shape[0]
    tm = 256
    npiece = len(PIECE_W)
    offs = np.concatenate([[0], np.cumsum(PIECE_W)]).tolist()

    def body(*refs):
        p_refs = refs[:npiece]
        h_ref, g_ref, dhr_ref, w_ref, dh_ref, dz_ref, db_ref, dg_ref = refs[npiece:npiece + 8]
        slabs = list(refs[npiece + 8:])
        i = pl.program_id(0)

        @pl.when(i == 0)
        def _():
            db_ref[...] = jnp.zeros_like(db_ref)
            dg_ref[...] = jnp.zeros_like(dg_ref)

        du = jnp.zeros((tm, D_MODEL), F32)
        for k in range(npiece):
            o, w = offs[k], PIECE_W[k]
            token_order = _unfold(p_refs[k], slabs.pop(), PIECE_D[k]).astype(BF16) if PIECE_D[k] > 1 else None
            for c0 in range(0, w, 512):
                cw = min(512, w - c0)
                pz = p_refs[k][:, c0:c0 + cw] if token_order is None else token_order
                dz_ref[:, o + c0:o + c0 + cw] = pz
                db_ref[:, o + c0:o + c0 + cw] += jnp.sum(pz.astype(F32), axis=0, keepdims=True)
                du = du + _nn(pz, w_ref[o + c0:o + c0 + cw, :])
        xhat, r = _rms(h_ref[...], g_ref[...])
        dx, dg = _rms_bwd(du, xhat, r, g_ref[...])
        dh_ref[...] = dhr_ref[...] + dx
        dg_ref[...] += dg

    row = lambda w: pl.BlockSpec((tm, w), lambda i: (i, 0))
    full = lambda shp: pl.BlockSpec(shp, lambda i: (0, 0))
    return pl.pallas_call(
        body, name=name, grid=(t // tm,),
        in_specs=[pl.BlockSpec((tm // d, d * w), lambda i: (i, 0)) for w, d in zip(PIECE_W, PIECE_D)]
        + [row(D_MODEL), _resident(g), row(D_MODEL), _resident(w_t)],
        out_specs=[row(D_MODEL), row(D_IN), full((1, D_IN)), full((1, D_MODEL))],
        out_shape=[jax.ShapeDtypeStruct((t, D_MODEL), F32), jax.ShapeDtypeStruct((t, D_IN), BF16),
                   jax.ShapeDtypeStruct((1, D_IN), F32), jax.ShapeDtypeStruct((1, D_MODEL), F32)],
        scratch_shapes=[pltpu.VMEM((2, tm, 128), F32)] * sum(d > 1 for d in PIECE_D),
        compiler_params=_params("arbitrary"),
    )(*pieces, h, g, dh_res, w_t)


def _t5_bucket(dist):
    max_exact = N_BUCKETS // 2
    n = jnp.maximum(dist, 0)
    nf = jnp.maximum(n, 1).astype(jnp.float32)
    large = max_exact + (jnp.log(nf / max_exact) / math.log(MAX_DISTANCE / max_exact)
                         * (N_BUCKETS - max_exact)).astype(jnp.int32)
    large = jnp.minimum(large, N_BUCKETS - 1)
    return jnp.where(n < max_exact, n, large)


ATT_CFG = ((1, 128, 0, 4), (4, 128, 4, 4), (16, 128, 8, 4), (1, B_WINDOW - 1, A_HEADS, 8))


def _bucket_tiles():
    qi = jnp.arange(BLOCK)[:, None]
    ki = jnp.arange(2 * BLOCK)[None, :]
    dist = qi + BLOCK - ki
    return jnp.stack([_t5_bucket(dist * cfg[0]) for cfg in ATT_CFG]).astype(jnp.int32)


def _band(max_steps):
    row = lax.broadcasted_iota(jnp.int32, (BLOCK, 2 * BLOCK), 0)
    col = lax.broadcasted_iota(jnp.int32, (BLOCK, 2 * BLOCK), 1)
    dist = row + BLOCK - col
    return (dist >= 0) & (dist <= max_steps)


def _bias_build(table, buckets, hosted=()):
    def body(tab_ref, bt_ref, out_ref):
        col = lax.broadcasted_iota(jnp.int32, (BLOCK, 2 * BLOCK), 1)
        for ci, (_, max_steps, h0, nh) in enumerate(ATT_CFG):
            bt = bt_ref[ci]
            band = _band(max_steps)
            for h in range(h0, h0 + nh):
                acc = lax.fori_loop(0, N_BUCKETS, lambda b, acc: jnp.where(bt == b, tab_ref[b, h], acc),
                                    jnp.zeros((BLOCK, 2 * BLOCK), F32))
                out_ref[0, BLOCK * h:BLOCK * (h + 1), :] = jnp.where(band & (col >= BLOCK), acc, NEG)
                out_ref[1, BLOCK * h:BLOCK * (h + 1), :] = jnp.where(band, acc, NEG)

    return _call(
        body, name="bias_build", grid=(1,),
        in_specs=[pl.BlockSpec(memory_space=pltpu.SMEM), pl.BlockSpec(memory_space=pltpu.VMEM)],
        out_specs=[pl.BlockSpec(memory_space=pltpu.VMEM)],
        out_shape=[jax.ShapeDtypeStruct((2, TOTAL_HEADS * BLOCK, 2 * BLOCK), F32)],
        sem=("arbitrary",), args=(table, buckets), hosted=hosted)


def _bias_reduce(dbias, buckets, dsink_rows, hosted=()):
    def body(db_ref, bt_ref, ds_ref, out_ref, sink_ref):
        ri = lax.broadcasted_iota(jnp.int32, (N_BUCKETS, 128), 0)
        ci = lax.broadcasted_iota(jnp.int32, (N_BUCKETS, 128), 1)

        def per_bucket(b, acc):
            for cfg_i, (_, _, h0, nh) in enumerate(ATT_CFG):
                hit = bt_ref[cfg_i] == b
                for h in range(h0, h0 + nh):
                    val = jnp.sum(jnp.where(hit, db_ref[BLOCK * h:BLOCK * (h + 1), :], 0.0))
                    acc = jnp.where((ri == b) & (ci == h), val, acc)
            return acc

        out_ref[...] = lax.fori_loop(0, N_BUCKETS, per_bucket, jnp.zeros((N_BUCKETS, 128), F32))
        for h in range(8):
            sink_ref[h:h + 1, :] = jnp.sum(ds_ref[BLOCK * h:BLOCK * (h + 1), :], axis=0, keepdims=True)

    return _call(
        body, name="bias_reduce", grid=(1,),
        in_specs=[pl.BlockSpec(memory_space=pltpu.VMEM)] * 3,
        out_specs=[pl.BlockSpec(memory_space=pltpu.VMEM)] * 2,
        out_shape=[jax.ShapeDtypeStruct((N_BUCKETS, 128), F32), jax.ShapeDtypeStruct((8, 128), F32)],
        sem=("arbitrary",), args=(dbias, buckets, dsink_rows), hosted=hosted)


class _Att:
    def __init__(self, cfg_i):
        stride, _, h0, nh = ATT_CFG[cfg_i]
        self.d = stride if cfg_i < 3 else 1
        self.h0, self.nh = h0, nh
        self.row_w = QKV_W if self.d == 1 else 3 * 256
        if cfg_i < 3:
            self.nq, self.wkv = 1, 256
            self.q_unit = [cfg_i if self.d == 1 else 0]
            self.k_unit, self.v_unit = (3 + cfg_i, 6 + cfg_i) if self.d == 1 else (1, 2)
            self.sinks = False
        else:
            self.nq, self.wkv = 2, 128
            self.q_unit = [9, 10]
            self.k_unit, self.v_unit = 22, 23
            self.sinks = True
        self.wq = 256 * self.nq


def _att_in_specs(cf, bsz):
    uq, ukv = cf.row_w // 256, cf.row_w // cf.wkv
    specs = [pl.BlockSpec((bsz, BLOCK, 256), functools.partial(lambda r, j, u: (0, j, r * uq + u), u=u))
             for u in cf.q_unit]
    for unit in (cf.k_unit, cf.v_unit):
        specs.append(pl.BlockSpec((bsz, BLOCK, cf.wkv),
                                  functools.partial(lambda r, j, u: (0, jnp.maximum(j - 1, 0), r * ukv + u), u=unit)))
        specs.append(pl.BlockSpec((bsz, BLOCK, cf.wkv),
                                  functools.partial(lambda r, j, u: (0, j, r * ukv + u), u=unit)))
    for qb in range(cf.nq):
        specs.append(pl.BlockSpec((None, HEADS_PER_BLOCK * BLOCK, 2 * BLOCK), functools.partial(
            lambda r, j, u: (jnp.minimum(j, 1), u, 0), u=cf.h0 // HEADS_PER_BLOCK + qb)))
    if cf.sinks:
        specs += [pl.BlockSpec((HEADS_PER_BLOCK * BLOCK, 128), functools.partial(lambda r, j, u: (u, 0), u=qb))
                  for qb in range(cf.nq)]
    return specs


HEADS_PER_BLOCK = 4


def _head_masks(rows):
    head = lax.broadcasted_iota(jnp.int32, (rows, 256), 1) // HEAD_DIM
    return [head == h for h in range(HEADS_PER_BLOCK)]


def _stack_heads(x, masks):
    return jnp.concatenate([jnp.where(m, x, jnp.zeros_like(x)) for m in masks], axis=0)


def _unstack_heads(x4, masks):
    blocks = [x4[BLOCK * h:BLOCK * (h + 1)] for h in range(HEADS_PER_BLOCK)]
    return jnp.where(masks[0], blocks[0], jnp.where(masks[1], blocks[1], jnp.where(masks[2], blocks[2], blocks[3])))


def _row_value(x):
    return jnp.max(x, axis=-1, keepdims=True)


def _kv_operands(cf, x):
    if cf.wkv == 256:
        return [x]
    lane = lax.broadcasted_iota(jnp.int32, x.shape, 1)
    swapped = pltpu.roll(x, HEAD_DIM, 1)
    halves = [jnp.where(lane < HEAD_DIM, x, swapped), jnp.where(lane < HEAD_DIM, swapped, x)]
    return [jnp.concatenate([half, half], axis=1) for half in halves]


def _kv_fold(cf, grads):
    if cf.wkv == 256:
        return grads[0]
    folded = []
    for g in grads:
        x = g[:, :128] + g[:, 128:]
        folded.append(x + pltpu.roll(x, HEAD_DIM, 1))
    lane = lax.broadcasted_iota(jnp.int32, folded[0].shape, 1)
    return jnp.where(lane < HEAD_DIM, folded[0], folded[1])


def _attn_fwd(cf, zf, bias, sinks, name, hosted=()):
    bsz, l, _ = zf.shape
    nb = l // BLOCK

    def body(*refs):
        refs = list(refs)
        q_refs = [refs.pop(0) for _ in range(cf.nq)]
        kp_ref, kc_ref, vp_ref, vc_ref = [refs.pop(0) for _ in range(4)]
        bias_refs = [refs.pop(0) for _ in range(cf.nq)]
        sink_refs = [refs.pop(0) for _ in range(cf.nq)] if cf.sinks else None
        o_ref, lse_ref = refs
        masks, kv_masks = _head_masks(BLOCK), _head_masks(2 * BLOCK)
        sinks4 = [_row_value(ref[...]) for ref in sink_refs] if cf.sinks else None
        for bi in range(bsz):
            k = jnp.concatenate([kp_ref[bi], kc_ref[bi]], axis=0)
            v = jnp.concatenate([vp_ref[bi], vc_ref[bi]], axis=0)
            k_ops, v_ops = _kv_operands(cf, k), _kv_operands(cf, v)
            for qb in range(cf.nq):
                cols = slice(256 * qb, 256 * (qb + 1))
                kb, vb = k_ops[qb], v_ops[qb]
                q4 = _stack_heads(q_refs[qb][bi] * (HEAD_DIM ** -0.5), masks)
                s = _nt(q4, kb) + bias_refs[qb][...]
                m = jnp.max(s, axis=-1, keepdims=True)
                if cf.sinks:
                    sk = sinks4[qb]
                    m = jnp.maximum(m, sk)
                p = jnp.exp(s - m)
                den = jnp.sum(p, axis=-1, keepdims=True)
                if cf.sinks:
                    den = den + jnp.exp(sk - m)
                pn = (p * (1.0 / den)).astype(BF16)
                p_lanes = jnp.concatenate([pn[BLOCK * h:BLOCK * (h + 1)] for h in range(HEADS_PER_BLOCK)], axis=1)
                v4 = jnp.concatenate([jnp.where(mk, vb, jnp.zeros_like(vb)) for mk in kv_masks], axis=0)
                o_ref[bi, :, cols] = _nn(p_lanes, v4)
                lse_ref[bi, :, cols] = _unstack_heads(
                    jnp.broadcast_to(m + jnp.log(den), (HEADS_PER_BLOCK * BLOCK, 256)), masks)

    in_specs = _att_in_specs(cf, bsz)
    args = [zf] * (cf.nq + 4) + [bias] * cf.nq + ([sinks] * cf.nq if cf.sinks else [])
    out = pl.BlockSpec((bsz, BLOCK, cf.wq), lambda r, j: (0, j, r))
    shape = jax.ShapeDtypeStruct((bsz, l, cf.d * cf.wq), F32)
    return _call(
        body, name=name, grid=(cf.d, nb), in_specs=in_specs, out_specs=[out, out], out_shape=[shape, shape],
        sem=("parallel", "arbitrary"), args=args, hosted=hosted)


def _attn_bwd(cf, zf, bias, sinks, stats, dy, name, hosted=()):
    bsz, l, _ = zf.shape
    nb = l // BLOCK

    def body(*refs):
        refs = list(refs)
        q_refs = [refs.pop(0) for _ in range(cf.nq)]
        kp_ref, kc_ref, vp_ref, vc_ref = [refs.pop(0) for _ in range(4)]
        bias_refs = [refs.pop(0) for _ in range(cf.nq)]
        sink_refs = [refs.pop(0) for _ in range(cf.nq)] if cf.sinks else None
        st_ref, dy_ref, dq_ref, dk_ref, dv_ref, dbias_ref = [refs.pop(0) for _ in range(6)]
        dsink_ref = refs.pop(0) if cf.sinks else None
        dk_acc, dv_acc = refs
        r, j = pl.program_id(0), pl.program_id(1)

        @pl.when((r == 0) & (j == 0))
        def _():
            dbias_ref[...] = jnp.zeros_like(dbias_ref)
            if cf.sinks:
                dsink_ref[...] = jnp.zeros_like(dsink_ref)

        @pl.when(j == 0)
        def _():
            dk_acc[...] = jnp.zeros_like(dk_acc)
            dv_acc[...] = jnp.zeros_like(dv_acc)

        masks = _head_masks(BLOCK)
        cur = pl.ds(pl.multiple_of(j * BLOCK, BLOCK), BLOCK)
        prev = pl.ds(pl.multiple_of(jnp.maximum(j - 1, 0) * BLOCK, BLOCK), BLOCK)
        sinks4 = [_row_value(ref[...]) for ref in sink_refs] if cf.sinks else None
        ds_sum, dsink_sum = [None] * cf.nq, [None] * cf.nq
        for bi in range(bsz):
            k = jnp.concatenate([kp_ref[bi], kc_ref[bi]], axis=0)
            v = jnp.concatenate([vp_ref[bi], vc_ref[bi]], axis=0)
            dk_blocks, dv_blocks = [], []
            k_ops, v_ops = _kv_operands(cf, k), _kv_operands(cf, v)
            for qb in range(cf.nq):
                cols = slice(256 * qb, 256 * (qb + 1))
                kb, vb = k_ops[qb], v_ops[qb]
                q4 = _stack_heads(q_refs[qb][bi] * (HEAD_DIM ** -0.5), masks)
                st = st_ref[bi, :, 128 * qb:128 * (qb + 1)]
                lt4, e4 = _read_stats(st, 0), _read_stats(st, 1)
                pa = jnp.exp(_nt(q4, kb) + bias_refs[qb][...] - lt4)
                dy = dy_ref[bi, :, cols]
                dy4 = _stack_heads(dy, masks)
                ds = pa * (_nt(dy4, vb) - e4)
                ds_sum[qb] = ds if ds_sum[qb] is None else ds_sum[qb] + ds
                if cf.sinks:
                    dsk = jnp.exp(sinks4[qb] - lt4) * e4
                    dsink_sum[qb] = dsk if dsink_sum[qb] is None else dsink_sum[qb] + dsk
                dsb = ds.astype(BF16)
                dq_ref[bi, :, cols] = (_unstack_heads(_nn(dsb, kb), masks) * (HEAD_DIM ** -0.5)).astype(dq_ref.dtype)
                dk_blocks.append(_tn(dsb, q4))
                dv_blocks.append(_tn(pa.astype(BF16), dy4))
            dk_new, dv_new = _kv_fold(cf, dk_blocks), _kv_fold(cf, dv_blocks)
            dk_acc[bi, cur, :] += dk_new[BLOCK:]
            dv_acc[bi, cur, :] += dv_new[BLOCK:]
            dk_acc[bi, prev, :] += dk_new[:BLOCK]
            dv_acc[bi, prev, :] += dv_new[:BLOCK]
        for qb in range(cf.nq):
            rows = slice(HEADS_PER_BLOCK * BLOCK * qb, HEADS_PER_BLOCK * BLOCK * (qb + 1))
            dbias_ref[rows, :] += ds_sum[qb]
            if cf.sinks:
                dsink_ref[rows, :] -= dsink_sum[qb]

        @pl.when(j == nb - 1)
        def _():
            dk_ref[...] = dk_acc[...].astype(dk_ref.dtype)
            dv_ref[...] = dv_acc[...].astype(dv_ref.dtype)

    tok = pl.BlockSpec((bsz, BLOCK, cf.wq), lambda r, j: (0, j, r))
    in_specs = _att_in_specs(cf, bsz) + [pl.BlockSpec((bsz, BLOCK, cf.wq // 2), lambda r, j: (0, j, r)), tok]
    args = [zf] * (cf.nq + 4) + [bias] * cf.nq + ([sinks] * cf.nq if cf.sinks else []) + [stats, dy]
    seq = pl.BlockSpec((bsz, l, cf.wkv), lambda r, j: (0, 0, r))
    out_specs = [tok, seq, seq, pl.BlockSpec((cf.nh * BLOCK, 2 * BLOCK), lambda r, j: (0, 0))]
    grad_dtype = BF16 if cf.d == 1 else F32
    out_shape = [jax.ShapeDtypeStruct((bsz, l, cf.d * cf.wq), grad_dtype),
                 jax.ShapeDtypeStruct((bsz, l, cf.d * cf.wkv), grad_dtype),
                 jax.ShapeDtypeStruct((bsz, l, cf.d * cf.wkv), grad_dtype),
                 jax.ShapeDtypeStruct((cf.nh * BLOCK, 2 * BLOCK), F32)]
    if cf.sinks:
        out_specs.append(pl.BlockSpec((cf.nh * BLOCK, 128), lambda r, j: (0, 0)))
        out_shape.append(jax.ShapeDtypeStruct((cf.nh * BLOCK, 128), F32))
    return _call(
        body, name=name, grid=(cf.d, nb), in_specs=in_specs, out_specs=out_specs, out_shape=out_shape,
        scratch=[pltpu.VMEM((bsz, l, cf.wkv), F32), pltpu.VMEM((bsz, l, cf.wkv), F32)],
        sem=("arbitrary", "arbitrary"), args=args, hosted=hosted)


def _merge_fwd(o_a, lse_a, o_b, zg, h, wa_t, wb_t, wout, name, hosted=()):
    t = h.shape[0]
    tm = 512

    def body(o1, o2, o3, l1, l2, l3, ob_ref, zg_ref, h_ref, wa_ref, wb_ref, wo_ref, h2_ref, ya_ref, lt_ref, *slabs):
        o = [_unfold(ref, slabs[i], d) for i, (ref, d) in enumerate(zip((o1, o2, o3), DILATIONS))]
        l = [_unfold(ref, slabs[3 + i], d) for i, (ref, d) in enumerate(zip((l1, l2, l3), DILATIONS))]
        m = jnp.maximum(jnp.maximum(l[0], l[1]), l[2])
        e1, e2, e3 = jnp.exp(l[0] - m), jnp.exp(l[1] - m), jnp.exp(l[2] - m)
        se = e1 + e2 + e3
        ya = (e1 / se) * o[0] + (e2 / se) * o[1] + (e3 / se) * o[2]
        ya_ref[...] = ya
        lt_ref[...] = m + jnp.log(se)
        pa = _nt(ya.astype(BF16), wa_ref[...])
        pb = _nt(ob_ref[...].astype(BF16), wb_ref[...])
        merged = (jax.nn.sigmoid(zg_ref[:, :D_MODEL].astype(F32)) * pa
                  + jax.nn.sigmoid(zg_ref[:, D_MODEL:].astype(F32)) * pb)
        h2_ref[...] = h_ref[...] + _nn(merged.astype(BF16), wo_ref[...])

    row = lambda w: pl.BlockSpec((tm, w), lambda i: (i, 0))
    folded = [pl.BlockSpec((tm // d, d * 256), lambda i: (i, 0)) for d in DILATIONS]
    full = _resident
    return _call(
        body, name=name, grid=(t // tm,),
        in_specs=folded + folded + [row(512), row(GATE_W), row(D_MODEL), full(wa_t), full(wb_t), full(wout)],
        out_specs=[row(D_MODEL), row(256), row(256)],
        out_shape=[jax.ShapeDtypeStruct((t, D_MODEL), F32), jax.ShapeDtypeStruct((t, 256), F32),
                   jax.ShapeDtypeStruct((t, 256), F32)],
        scratch=[pltpu.VMEM((2, tm, 128), F32)] * 6,
        sem=("parallel",), args=(*o_a, *lse_a, o_b, zg, h, wa_t, wb_t, wout), hosted=hosted)


STAT_LANES = 32


def _head_stats(lt, dy, y):
    rows, w = dy.shape
    lane = lax.broadcasted_iota(jnp.int32, (rows, w), 1)
    out_lane = lax.broadcasted_iota(jnp.int32, (rows, w // 2), 1)
    prod = dy * y
    out = jnp.zeros((rows, w // 2), F32)
    for hd in range(w // HEAD_DIM):
        mine = lane // HEAD_DIM == hd
        dot = jnp.sum(jnp.where(mine, prod, 0.0), axis=-1, keepdims=True)
        both = jnp.where(out_lane % STAT_LANES < STAT_LANES // 2, _row_value(jnp.where(mine, lt, NEG)), dot)
        out = jnp.where(out_lane // STAT_LANES == hd, both, out)
    return out


def _read_stats(st, part):
    lane = lax.broadcasted_iota(jnp.int32, st.shape, 1)
    half = (lane % STAT_LANES < STAT_LANES // 2) == (part == 0)
    return jnp.concatenate([_row_value(jnp.where((lane // STAT_LANES == hd) & half, st, NEG))
                            for hd in range(HEADS_PER_BLOCK)], axis=0)


def _merge_bwd(dh, ya, lse_tot, o_b, lse_b, zg, wa_t, wb_t, wout, name, hosted=()):
    t = dh.shape[0]
    tm = 512

    def body(dh_ref, ya_ref, lt_ref, ob_ref, lb_ref, zg_ref, wa_ref, wb_ref, wo_ref,
             mg_ref, dpa_ref, dpb_ref, dzg_ref, dy1, dy2, dy3, st1, st2, st3, dyb_ref, stb_ref, *slabs):
        dm = _nt(dh_ref[...].astype(BF16), wo_ref[...])
        pa = _nt(ya_ref[...].astype(BF16), wa_ref[...])
        pb = _nt(ob_ref[...].astype(BF16), wb_ref[...])
        sa = jax.nn.sigmoid(zg_ref[:, :D_MODEL].astype(F32))
        sb = jax.nn.sigmoid(zg_ref[:, D_MODEL:].astype(F32))
        mg_ref[...] = (sa * pa + sb * pb).astype(BF16)
        dpa = (dm * sa).astype(BF16)
        dpb = (dm * sb).astype(BF16)
        dpa_ref[...] = dpa
        dpb_ref[...] = dpb
        dzg_ref[:, :D_MODEL] = (dm * pa * (sa * (1.0 - sa))).astype(BF16)
        dzg_ref[:, D_MODEL:] = (dm * pb * (sb * (1.0 - sb))).astype(BF16)
        dya = _nn(dpa, wa_ref[...])
        dyb = _nn(dpb, wb_ref[...])
        stats = _head_stats(lt_ref[...], dya, ya_ref[...])
        for i, d in enumerate(DILATIONS):
            _fold(dya, slabs[i], (dy1, dy2, dy3)[i], d)
            _fold(stats, slabs[3 + i], (st1, st2, st3)[i], d)
        dyb_ref[...] = dyb.astype(BF16)
        stb_ref[...] = _head_stats(lb_ref[...], dyb, ob_ref[...])

    row = lambda w: pl.BlockSpec((tm, w), lambda i: (i, 0))
    folded = [pl.BlockSpec((tm // d, d * 256), lambda i: (i, 0)) for d in DILATIONS]
    full = _resident
    sds = jax.ShapeDtypeStruct
    dy_shape = [sds((t // d, d * 256), BF16) for d in DILATIONS]
    st_shape = [sds((t // d, d * 128), F32) for d in DILATIONS]
    st_specs = [pl.BlockSpec((tm // d, d * 128), lambda i: (i, 0)) for d in DILATIONS]
    return _call(
        body, name=name, grid=(t // tm,),
        in_specs=[row(D_MODEL), row(256), row(256), row(512), row(512), row(GATE_W), full(wa_t), full(wb_t),
                  full(wout)],
        out_specs=[row(D_MODEL), row(D_MODEL), row(D_MODEL), row(GATE_W)] + folded + st_specs + [row(512), row(256)],
        out_shape=[sds((t, D_MODEL), BF16), sds((t, D_MODEL), BF16), sds((t, D_MODEL), BF16),
                   sds((t, GATE_W), BF16)] + dy_shape + st_shape + [sds((t, 512), BF16), sds((t, 256), F32)],
        scratch=[pltpu.VMEM((2, tm, 128), F32)] * 6,
        sem=("parallel",), args=(dh, ya, lse_tot, o_b, lse_b, zg, wa_t, wb_t, wout), hosted=hosted)


def _pair_sum(grad, got, name):
    _, _, r, cdim = grad.shape
    core = lax.axis_index("c").astype(jnp.int32).reshape(1)

    def body(core_ref, g_ref, s_ref, o_ref):
        o_ref[...] = (g_ref[...] + s_ref[...]).astype(BF16)

    return pl.pallas_call(
        body, name=name,
        grid_spec=pltpu.PrefetchScalarGridSpec(
            num_scalar_prefetch=1, grid=(N_CHIP,),
            in_specs=[pl.BlockSpec((None, None, r, cdim), lambda q, core_ref: (q, core_ref[0], 0, 0)),
                      pl.BlockSpec((None, None, r, cdim), lambda q, core_ref: (q, 0, 0, 0))],
            out_specs=pl.BlockSpec((None, r, cdim), lambda q, core_ref: (q, 0, 0))),
        out_shape=jax.ShapeDtypeStruct((N_CHIP, r, cdim), BF16),
        compiler_params=_params("parallel"),
    )(core, grad, got)


def _adamw_math(w, g, m, v):
    m = ADAM_B1 * m + (1.0 - ADAM_B1) * g
    v = ADAM_B2 * v + (1.0 - ADAM_B2) * jnp.square(g)
    m_hat = m / (1.0 - ADAM_B1 ** ADAM_STEP)
    v_hat = v / (1.0 - ADAM_B2 ** ADAM_STEP)
    delta = -ADAM_LR * (m_hat / (jnp.sqrt(v_hat) + ADAM_EPS) + ADAM_WD * w)
    return delta, m, v


def _update(w, parts, m, v, transposed, name, hosted=()):
    r, c = parts.shape[1:]

    def body(w_ref, p_ref, m_ref, v_ref, g_ref, d_ref, m2_ref, v2_ref):
        def total(rows):
            return ((p_ref[0, rows].astype(F32) + p_ref[1, rows].astype(F32)) + p_ref[2, rows].astype(F32)) \
                + p_ref[3, rows].astype(F32)

        def update(at, g):
            g_ref[at] = g
            d_ref[at], m2_ref[at], v2_ref[at] = _adamw_math(w_ref[at], g, m_ref[at], v_ref[at])

        if not transposed:
            update((slice(None), slice(None)), total(slice(None)))
            return
        for r0 in range(0, r, 128):
            n = min(128, r - r0)
            gt = total(slice(r0, r0 + n))
            if n < 128:
                gt = jnp.concatenate([gt, jnp.zeros((128 - n, c), F32)], axis=0)
            update((slice(None), slice(r0, r0 + n)), gt.T[:, :n])

    sds = jax.ShapeDtypeStruct(w.shape, F32)
    vm = pl.BlockSpec(memory_space=pltpu.VMEM)
    return _call(body, name=name, grid=(1,), in_specs=[vm] * 4, out_specs=[vm] * 4, out_shape=[sds] * 4,
                 sem=("arbitrary",), args=(w, parts, m, v), hosted=hosted)


SMALL_ROWS = 80


def _small_update(g, w, m, v, name):
    def body(g_ref, w_ref, m_ref, v_ref, gs_ref, d_ref, m2_ref, v2_ref, got_ref, send_sems, recv_sems):
        x, y, c = _place()
        me = 4 * x + 2 * y + c
        got_ref[me] = g_ref[...]
        copies = []
        for k in range(1, N_DEV):
            peer = (x ^ (k >> 2), y ^ ((k >> 1) & 1), c ^ (k & 1))
            cp = pltpu.make_async_remote_copy(
                src_ref=g_ref, dst_ref=got_ref.at[me], send_sem=send_sems.at[k - 1], recv_sem=recv_sems.at[k - 1],
                device_id=peer, device_id_type=MESH)
            cp.start()
            copies.append(cp)
        for cp in copies:
            cp.wait()
        total = got_ref[0]
        for k in range(1, N_DEV):
            total = total + got_ref[k]
        gs_ref[...] = total
        d_ref[...], m2_ref[...], v2_ref[...] = _adamw_math(w_ref[...], total, m_ref[...], v_ref[...])

    sds = jax.ShapeDtypeStruct((SMALL_ROWS, 128), F32)
    vm = pl.BlockSpec(memory_space=pltpu.VMEM)
    return pl.pallas_call(
        body, name=name, in_specs=[vm] * 4, out_specs=[vm] * 4, out_shape=[sds] * 4,
        scratch_shapes=[pltpu.VMEM((N_DEV, SMALL_ROWS, 128), F32), pltpu.SemaphoreType.DMA((N_DEV - 1,)),
                        pltpu.SemaphoreType.DMA((N_DEV - 1,))],
    )(g, w, m, v)


def _pack_small(gains, b_in, rel_bias, sinks, last):
    rows = [a.reshape(8, 128) for a in gains] + [b_in.reshape(40, 128), rel_bias.reshape(5, 128),
                                                 jnp.pad(sinks.reshape(1, 8), ((0, 0), (0, 120))), last]
    rows.append(jnp.zeros((SMALL_ROWS - 79, 128), F32))
    return jnp.concatenate(rows, axis=0)


def _unpack_small(p, like):
    out = [p[8 * i:8 * i + 8].reshape(like[i].shape) for i in range(4)]
    out.append(p[32:72].reshape(like[4].shape))
    out.append(p[72:77].reshape(like[5].shape))
    out.append(p[77, :8].reshape(like[6].shape))
    return out


def kernel(x, ffn1_norm, ffn1_w_gate, ffn1_w_up, ffn1_w_down, mix_norm, w_in, b_in, w_branch_a, w_branch_b, w_out, sinks, rel_bias, ffn2_norm, ffn2_w_gate, ffn2_w_up, ffn2_w_down, final_norm, loss_target, m_ffn1_norm, m_ffn1_w_gate, m_ffn1_w_up, m_ffn1_w_down, m_mix_norm, m_w_in, m_b_in, m_w_branch_a, m_w_branch_b, m_w_out, m_sinks, m_rel_bias, m_ffn2_norm, m_ffn2_w_gate, m_ffn2_w_up, m_ffn2_w_down, m_final_norm, v_ffn1_norm, v_ffn1_w_gate, v_ffn1_w_up, v_ffn1_w_down, v_mix_norm, v_w_in, v_b_in, v_w_branch_a, v_w_branch_b, v_w_out, v_sinks, v_rel_bias, v_ffn2_norm, v_ffn2_w_gate, v_ffn2_w_up, v_ffn2_w_down, v_final_norm):
    bsz, seq, _ = x.shape
    t = bsz * seq
    xt = x.reshape(t, D_MODEL)
    target = loss_target.reshape(t, D_MODEL)

    big = [("ffn1_w_gate", ffn1_w_gate, m_ffn1_w_gate, v_ffn1_w_gate, True),
           ("ffn1_w_up", ffn1_w_up, m_ffn1_w_up, v_ffn1_w_up, True),
           ("ffn1_w_down", ffn1_w_down, m_ffn1_w_down, v_ffn1_w_down, False),
           ("w_in", w_in, m_w_in, v_w_in, True),
           ("w_branch_a", w_branch_a, m_w_branch_a, v_w_branch_a, True),
           ("w_branch_b", w_branch_b, m_w_branch_b, v_w_branch_b, True),
           ("w_out", w_out, m_w_out, v_w_out, False),
           ("ffn2_w_gate", ffn2_w_gate, m_ffn2_w_gate, v_ffn2_w_gate, True),
           ("ffn2_w_up", ffn2_w_up, m_ffn2_w_up, v_ffn2_w_up, True),
           ("ffn2_w_down", ffn2_w_down, m_ffn2_w_down, v_ffn2_w_down, False)]
    shard = {nm: (w[0].T if tr else w[0]).astype(BF16) for nm, w, _, _, tr in big}
    full = {}

    def gather(names):
        return _Gather([shard[nm] for nm in names])

    def keep(names, got):
        for nm, gw in zip(names, got):
            full[nm] = gw.reshape(-1, gw.shape[-1])

    ffn1_names = ["ffn1_w_gate", "ffn1_w_up", "ffn1_w_down"]
    mix_names = ["w_in", "w_branch_a", "w_branch_b", "w_out"]
    ffn2_names = ["ffn2_w_gate", "ffn2_w_up", "ffn2_w_down"]
    g1, gm, g2, gf = ffn1_norm, mix_norm, ffn2_norm, final_norm.reshape(1, D_MODEL)

    buckets = _bucket_tiles()
    (bias,), (got,) = _bias_build(rel_bias, buckets, hosted=[gather(ffn1_names)])
    keep(ffn1_names, got)
    (h1, n1, a1, b1, hff1), (got,) = _ffn_fwd(xt, g1, full["ffn1_w_gate"], full["ffn1_w_up"], full["ffn1_w_down"], "ffn1_fwd",
                                hosted=[gather(mix_names + ffn2_names)])
    keep(mix_names + ffn2_names, got)
    (u, zq, zg, *zdil), _ = _inproj_fwd(h1, gm, full["w_in"], b_in, "inproj_fwd")
    sink_rows = jnp.broadcast_to(sinks.reshape(8, 1, 1), (8, BLOCK, 128)).reshape(8 * BLOCK, 128)
    cfs = [_Att(i) for i in range(4)]
    zfold = []
    for i, cf in enumerate(cfs):
        if cf.d == 1:
            zfold.append(zq.reshape(bsz, seq, QKV_W))
        else:
            zfold.append(zdil[i - 1].reshape(bsz, seq // cf.d, cf.d * cf.row_w))
    att = [None] * 4
    for i in (3, 0, 1, 2):
        cf = cfs[i]
        (o, lse), _ = _attn_fwd(cf, zfold[i], bias, sink_rows, f"attn{i}_fwd")
        att[i] = (o.reshape(t // cf.d, cf.d * cf.wq), lse.reshape(t // cf.d, cf.d * cf.wq))
    o_b, lse_b = att[3]
    (h2, ya, lse_tot), _ = _merge_fwd([a[0] for a in att[:3]], [a[1] for a in att[:3]], o_b, zg, h1,
                                      full["w_branch_a"], full["w_branch_b"], full["w_out"], "merge_fwd")
    (dh3, n2, a2, b2, hff2, loss_part, dgf), _ = _ffn_fwd(
        h2, g2, full["ffn2_w_gate"], full["ffn2_w_up"], full["ffn2_w_down"], "ffn2_fwd", head=(gf, target))

    grads, pair, from_chips = {}, {}, {}

    def by_owner(nm):
        return grads[nm].reshape(N_CHIP, 2, -1, grads[nm].shape[-1])

    def to_core(names):
        return _CoreExchange([by_owner(nm) for nm in names])

    def pair_up(names, got):
        for nm, sib in zip(names, got):
            pair[nm] = _pair_sum(by_owner(nm), sib, f"pair_sum_{nm}")

    def to_chips(names):
        return _ChipExchange([pair[nm] for nm in names])

    def landed(names, got):
        for nm, parts in zip(names, got):
            from_chips[nm] = parts

    dh2, dg2, da, db = _ffn_bwd(h2, a2, b2, g2, dh3, full["ffn2_w_gate"], full["ffn2_w_up"],
                                          full["ffn2_w_down"], "ffn2_bwd")
    grads["ffn2_w_gate"], _ = _tn_matmul(da, n2, 1408, "ffn2_dgate")
    grads["ffn2_w_up"], _ = _tn_matmul(db, n2, 1408, "ffn2_dup")
    grads["ffn2_w_down"], _ = _tn_matmul(hff2, dh3, 1408, "ffn2_ddown", scale=0.5)
    (merged, dpa, dpb, dzg, *cot), (got,) = _merge_bwd(
        dh2, ya, lse_tot, o_b, lse_b, zg, full["w_branch_a"], full["w_branch_b"], full["w_out"], "merge_bwd",
        hosted=[to_core(ffn2_names)])
    dys, sts = cot[0:3] + [cot[6]], cot[3:6] + [cot[7]]
    pair_up(ffn2_names, got)
    dq, dk, dv, dbias, dsink = [None] * 4, [None] * 4, [None] * 4, [None] * 4, None
    for i in (3, 0, 1, 2):
        cf = cfs[i]
        shp = (bsz, seq // cf.d, cf.d * cf.wq)
        hosted = [to_chips(ffn2_names)] if i == 3 else ()
        res, got = _attn_bwd(cf, zfold[i], bias, sink_rows, sts[i].reshape(shp[:2] + (shp[2] // 2,)), dys[i].reshape(shp),
                             f"attn{i}_bwd", hosted=hosted)
        if i == 3:
            landed(ffn2_names, got[0])
        dq[i] = res[0].reshape(t // cf.d, cf.d * cf.wq)
        dk[i] = res[1].reshape(t // cf.d, cf.d * cf.wkv)
        dv[i] = res[2].reshape(t // cf.d, cf.d * cf.wkv)
        dbias[i] = res[3]
        if cf.sinks:
            dsink = res[4]
    pieces = dq[:3] + dk[:3] + dv[:3] + [dq[3], dk[3], dv[3], dzg]
    dh1, dz, db_in, dgm = _inproj_bwd(pieces, h1, gm, dh2, full["w_in"], "inproj_bwd")
    dx, dg1, da, db = _ffn_bwd(xt, a1, b1, g1, dh1, full["ffn1_w_gate"], full["ffn1_w_up"],
                                         full["ffn1_w_down"], "ffn1_bwd")
    grads["w_in"], _ = _tn_matmul(dz, u, 1280, "dw_in")
    grads["ffn1_w_down"], (got,) = _tn_matmul(hff1, dh1, 1408, "ffn1_ddown", hosted=[to_core(["w_in"])], scale=0.5)
    pair_up(["w_in"], got)
    grads["ffn1_w_gate"], got = _tn_matmul(da, n1, 1408, "ffn1_dgate",
                                           hosted=[to_chips(["w_in"]), to_core(["ffn1_w_down"])])
    landed(["w_in"], got[0])
    pair_up(["ffn1_w_down"], got[1])
    grads["ffn1_w_up"], got = _tn_matmul(db, n1, 1408, "ffn1_dup",
                                         hosted=[to_chips(["ffn1_w_down"]), to_core(["ffn1_w_gate"])])
    landed(["ffn1_w_down"], got[0])
    pair_up(["ffn1_w_gate"], got[1])
    grads["w_out"], got = _tn_matmul(merged, dh2, 1024, "dw_out",
                                     hosted=[to_chips(["ffn1_w_gate"]), to_core(["ffn1_w_up"])])
    landed(["ffn1_w_gate"], got[0])
    pair_up(["ffn1_w_up"], got[1])
    grads["w_branch_b"], got = _tn_matmul(dpb, o_b, 1024, "dw_branch_b",
                                          hosted=[to_chips(["ffn1_w_up"]), to_core(["w_out"])])
    landed(["ffn1_w_up"], got[0])
    pair_up(["w_out"], got[1])
    grads["w_branch_a"], got = _tn_matmul(dpa, ya, 1024, "dw_branch_a",
                                          hosted=[to_chips(["w_out"]), to_core(["w_branch_b"])])
    landed(["w_out"], got[0])
    pair_up(["w_branch_b"], got[1])
    (dtable, dsinks), got = _bias_reduce(jnp.concatenate(dbias, axis=0), buckets, dsink,
                                         hosted=[to_chips(["w_branch_b"]), to_core(["w_branch_a"])])
    landed(["w_branch_b"], got[0])
    pair_up(["w_branch_a"], got[1])

    out_g, out_d, out_m, out_v = {}, {}, {}, {}
    for idx, (nm, w, m, v, tr) in enumerate(big):
        around = tr and w.shape[-1] % 128 != 0
        wmv = [a[0].T if around else a[0] for a in (w, m, v)]
        res, got = _update(wmv[0], from_chips[nm], wmv[1], wmv[2], tr and not around, f"update_{nm}",
                           hosted=[to_chips(["w_branch_a"])] if idx == 0 else ())
        if idx == 0:
            landed(["w_branch_a"], got[0])
        out_g[nm], out_d[nm], out_m[nm], out_v[nm] = [(a.T if around else a)[None] for a in res]

    small = [("ffn1_norm", ffn1_norm, m_ffn1_norm, v_ffn1_norm), ("mix_norm", mix_norm, m_mix_norm, v_mix_norm),
             ("ffn2_norm", ffn2_norm, m_ffn2_norm, v_ffn2_norm), ("final_norm", final_norm, m_final_norm, v_final_norm),
             ("b_in", b_in, m_b_in, v_b_in), ("rel_bias", rel_bias, m_rel_bias, v_rel_bias),
             ("sinks", sinks, m_sinks, v_sinks)]
    zero_row = jnp.zeros((1, 128), F32)
    pack = lambda arrs, last: _pack_small(arrs[:4], arrs[4], arrs[5], arrs[6], last)
    g_small = pack([dg1, dgm, dg2, dgf, db_in, dtable[:, :TOTAL_HEADS], dsinks[:, 0]], loss_part)
    packed = [pack([s[k] for s in small], zero_row) for k in (1, 2, 3)]
    gs, ds, ms, vs = _small_update(g_small, *packed, "small_update")
    like = [s[1] for s in small]
    for nm_s, g_, d_, m_, v_ in zip([s[0] for s in small], _unpack_small(gs, like), _unpack_small(ds, like),
                                    _unpack_small(ms, like), _unpack_small(vs, like)):
        out_g[nm_s], out_d[nm_s], out_m[nm_s], out_v[nm_s] = g_, d_, m_, v_
    loss = gs[78, 0]

    order = ["ffn1_norm", "ffn1_w_gate", "ffn1_w_up", "ffn1_w_down", "mix_norm", "w_in", "b_in", "w_branch_a",
             "w_branch_b", "w_out", "sinks", "rel_bias", "ffn2_norm", "ffn2_w_gate", "ffn2_w_up", "ffn2_w_down",
             "final_norm"]
    return (loss, dx.reshape(x.shape), *[out_g[k] for k in order], *[out_d[k] for k in order],
            *[out_m[k] for k in order], *[out_v[k] for k in order])
```

```python
import functools
import math

import numpy as np
import jax
import jax.numpy as jnp
from jax import lax
from jax.experimental import pallas as pl
from jax.experimental.pallas import tpu as pltpu

D_MODEL = 1024
D_FF = 2816
FF_CHUNK = 256
HEAD_DIM = 64
BLOCK = 128
N_BUCKETS = 32
MAX_DISTANCE = 2048
A_HEADS = 12
TOTAL_HEADS = 20
DIL_GROUPS = ((128, 1), (512, 4), (2048, 16))
B_WINDOW = 128
QKV_W = 3072
GATE_W = 2048
D_IN = QKV_W + GATE_W
EPS = 1e-6
NEG = -1e30
N_DEV = 8
N_CHIP = 4
ADAM_LR, ADAM_B1, ADAM_B2, ADAM_EPS, ADAM_WD, ADAM_STEP = 0.001, 0.9, 0.999, 1e-08, 0.01, 10
VMEM_LIMIT = 56 * 1024 * 1024
MESH = pl.DeviceIdType.MESH
BF16 = jnp.bfloat16
F32 = jnp.float32
ANY = pl.BlockSpec(memory_space=pl.ANY)


def _params(*sem):
    return pltpu.CompilerParams(dimension_semantics=sem, vmem_limit_bytes=VMEM_LIMIT)


def _resident(a):
    return pl.BlockSpec(a.shape, lambda i: (0, 0), pipeline_mode=pl.Buffered(1))


def _place():
    return lax.axis_index("x"), lax.axis_index("y"), lax.axis_index("c")


class _Gather:
    def __init__(self, shards):
        self.ins = list(shards)
        n = self.n = len(shards)
        self.out_shape = [jax.ShapeDtypeStruct((N_DEV,) + s.shape, s.dtype) for s in shards]
        self.scratch = [pltpu.SemaphoreType.DMA((7 * n,)), pltpu.SemaphoreType.DMA((7 * n,)),
                        pltpu.SemaphoreType.DMA((n,))]

    def _copies(self, ins, outs, sems):
        send_sems, recv_sems, local_sems = sems
        x, y, c = _place()
        me, sibling = (x, y, c), (x, y, 1 - c)
        chips = [(1 - x, y), (x, 1 - y), (1 - x, 1 - y)]

        def copy(i, k, block, to, src=None):
            dst = outs[i].at[4 * block[0] + 2 * block[1] + block[2]]
            return pltpu.make_async_remote_copy(
                src_ref=dst if src is None else src, dst_ref=dst, send_sem=send_sems.at[7 * i + k],
                recv_sem=recv_sems.at[7 * i + k], device_id=to, device_id_type=MESH)

        n = self.n
        south = c == 0
        relayed = (jnp.where(south, 1 - x, x), jnp.where(south, y, 1 - y), c)
        relay_to = (jnp.where(south, x, 1 - x), jnp.where(south, 1 - y, y), c)
        mine = [pltpu.make_async_copy(ins[i], outs[i].at[4 * x + 2 * y + c], local_sems.at[i]) for i in range(n)]
        first = [copy(i, 0, me, sibling, src=ins[i]) for i in range(n)]
        first += [copy(i, 1 + j, me, (*chips[j], c), src=ins[i]) for i in range(n) for j in range(2)]
        landed = [copy(i, 1 + j, (*chips[j], c), me) for j in range(2) for i in range(n)]
        passed = [copy(i, 4 + j, (*chips[j], c), sibling) for j in range(2) for i in range(n)]
        relays = [copy(i, 3, relayed, relay_to) for i in range(n)]
        diag_landed = [copy(i, 3, (*chips[2], c), me) for i in range(n)]
        diag_passed = [copy(i, 6, (*chips[2], c), sibling) for i in range(n)]
        from_sibling = [copy(i, 0, sibling, me) for i in range(n)]
        from_sibling += [copy(i, 4 + j, (*chip, 1 - c), me) for i in range(n) for j, chip in enumerate(chips)]
        return mine, first, landed, passed, relays, diag_landed, diag_passed, from_sibling

    def start(self, ins, outs, sems):
        mine, first = self._copies(ins, outs, sems)[:2]
        for cp in mine + first:
            cp.start()

    def mid(self, ins, outs, sems):
        _, _, landed, passed, relays, _, _, _ = self._copies(ins, outs, sems)
        for got in landed:
            got.wait_recv()
        for cp in relays + passed:
            cp.start()

    def end(self, ins, outs, sems):
        mine, first, _, passed, relays, diag_landed, diag_passed, from_sibling = self._copies(ins, outs, sems)
        for got, fwd in zip(diag_landed, diag_passed):
            got.wait_recv()
            fwd.start()
        for cp in from_sibling:
            cp.wait_recv()
        for cp in first + passed + relays + diag_passed:
            cp.wait_send()
        for cp in mine:
            cp.wait()


class _CoreExchange:
    def __init__(self, grads):
        self.ins = list(grads)
        n = self.n = len(grads)
        self.out_shape = [jax.ShapeDtypeStruct((N_CHIP, 1) + g.shape[2:], g.dtype) for g in grads]
        self.scratch = [pltpu.SemaphoreType.DMA((n,)), pltpu.SemaphoreType.DMA((n,))]

    def _copies(self, ins, outs, sems):
        x, y, c = _place()
        return [pltpu.make_async_remote_copy(
            src_ref=ins[i].at[:, pl.ds(1 - c, 1)], dst_ref=outs[i], send_sem=sems[0].at[i],
            recv_sem=sems[1].at[i], device_id=(x, y, 1 - c), device_id_type=MESH) for i in range(self.n)]

    def start(self, ins, outs, sems):
        for cp in self._copies(ins, outs, sems):
            cp.start()

    mid = None

    def end(self, ins, outs, sems):
        for cp in self._copies(ins, outs, sems):
            cp.wait()


class _ChipExchange:
    def __init__(self, parts):
        self.ins = list(parts)
        n = self.n = len(parts)
        self.out_shape = [jax.ShapeDtypeStruct(p.shape, p.dtype) for p in parts]
        self.scratch = [pltpu.SemaphoreType.DMA((3 * n,)), pltpu.SemaphoreType.DMA((3 * n,)),
                        pltpu.SemaphoreType.DMA((n,))]

    def _copies(self, ins, outs, sems):
        send_sems, recv_sems, local_sems = sems
        x, y, c = _place()
        my_chip = 2 * x + y
        copies = []
        for i in range(self.n):
            copies.append(pltpu.make_async_copy(ins[i].at[my_chip], outs[i].at[my_chip], local_sems.at[i]))
            for k, (qx, qy) in enumerate([(1 - x, y), (x, 1 - y), (1 - x, 1 - y)]):
                copies.append(pltpu.make_async_remote_copy(
                    src_ref=ins[i].at[2 * qx + qy], dst_ref=outs[i].at[my_chip], send_sem=send_sems.at[3 * i + k],
                    recv_sem=recv_sems.at[3 * i + k], device_id=(qx, qy, c), device_id_type=MESH))
        return copies

    def start(self, ins, outs, sems):
        for cp in self._copies(ins, outs, sems):
            cp.start()

    mid = None

    def end(self, ins, outs, sems):
        for cp in self._copies(ins, outs, sems):
            cp.wait()


def _call(body, *, name, grid, in_specs, out_specs, out_shape, args, scratch=(), sem=None, hosted=()):
    n_in, n_out, n_scr = len(in_specs), len(out_specs), len(scratch)
    x_in = [len(p.ins) for p in hosted]
    x_scr = [len(p.scratch) for p in hosted]
    steps = int(np.prod(grid))

    def wrapped(*refs):
        refs = list(refs)
        ins, refs = refs[:n_in], refs[n_in:]
        x_ins = [[refs.pop(0) for _ in range(k)] for k in x_in]
        outs, refs = refs[:n_out], refs[n_out:]
        x_outs = [[refs.pop(0) for _ in range(k)] for k in x_in]
        scr, refs = refs[:n_scr], refs[n_scr:]
        x_sems = [[refs.pop(0) for _ in range(k)] for k in x_scr]
        step = 0
        for d in range(len(grid)):
            step = step * grid[d] + pl.program_id(d)

        def phase(which, at):
            fns = [(getattr(p, which), a) for p, a in zip(hosted, zip(x_ins, x_outs, x_sems)) if getattr(p, which)]
            if fns:
                @pl.when(step == at)
                def _():
                    for fn, a in fns:
                        fn(*a)

        phase("start", 0)
        if steps > 1:
            phase("mid", (5 * steps) // 8)
        body(*ins, *outs, *scr)
        if steps == 1:
            phase("mid", 0)
        phase("end", steps - 1)

    results = pl.pallas_call(
        wrapped, name=name, grid=grid,
        in_specs=list(in_specs) + [ANY] * sum(x_in), out_specs=list(out_specs) + [ANY] * sum(x_in),
        out_shape=list(out_shape) + [s for p in hosted for s in p.out_shape],
        scratch_shapes=list(scratch) + [s for p in hosted for s in p.scratch],
        compiler_params=_params(*(("arbitrary",) * len(grid) if hosted else sem)),
    )(*args, *[a for p in hosted for a in p.ins])
    own, rest = list(results[:n_out]), list(results[n_out:])
    return own, [[rest.pop(0) for _ in range(k)] for k in x_in]


def _nt(a, b):
    return lax.dot_general(a, b, (((1,), (1,)), ((), ())), preferred_element_type=F32)


def _nn(a, b):
    return lax.dot_general(a, b, (((1,), (0,)), ((), ())), preferred_element_type=F32)


def _tn(a, b):
    return lax.dot_general(a, b, (((0,), (0,)), ((), ())), preferred_element_type=F32)


def _rms(x, g):
    r = lax.rsqrt(jnp.mean(x * x, axis=-1, keepdims=True) + EPS)
    return x * r, r


def _rms_bwd(dn, xhat, r, g):
    dg = jnp.sum(dn * xhat, axis=0, keepdims=True)
    dxh = dn * g
    dx = r * (dxh - xhat * jnp.mean(dxh * xhat, axis=-1, keepdims=True))
    return dx, dg


def _ffn_fwd(x, g, wg_t, wu_t, wd, name, hosted=(), head=None):
    t = x.shape[0]
    tm = 512 if head is None else 256

    def body(*refs):
        x_ref, g_ref, wg_ref, wu_ref, wd_ref = refs[:5]
        h_ref, n_ref, a_ref, b_ref, hff_ref = refs[-5 if head is None else -7:][:5]
        xhat, _ = _rms(x_ref[...], g_ref[...])
        n = (xhat * g_ref[...]).astype(BF16)
        n_ref[...] = n
        for c in range(0, D_FF, FF_CHUNK):
            cols = slice(c, c + FF_CHUNK)
            a = _nt(n, wg_ref[cols, :])
            b = _nt(n, wu_ref[cols, :])
            a_ref[:, cols] = a.astype(BF16)
            b_ref[:, cols] = b.astype(BF16)
            hff_ref[:, cols] = (a * jax.nn.sigmoid(a) * b).astype(BF16)
        h = x_ref[...] + 0.5 * _nn(hff_ref[...], wd_ref[...])
        if head is None:
            h_ref[...] = h
            return
        gf_ref, t_ref, loss_ref, dgf_ref = refs[5], refs[6], refs[-2], refs[-1]

        @pl.when(pl.program_id(0) == 0)
        def _():
            loss_ref[...] = jnp.zeros_like(loss_ref)
            dgf_ref[...] = jnp.zeros_like(dgf_ref)

        yhat, r = _rms(h, gf_ref[...])
        err = yhat * gf_ref[...] - t_ref[...]
        loss_ref[...] += 0.5 * jnp.sum(jnp.mean(err * err, axis=-1, keepdims=True), axis=0, keepdims=True)
        h_ref[...], dgf = _rms_bwd(err * (1.0 / D_MODEL), yhat, r, gf_ref[...])
        dgf_ref[...] += dgf

    row = pl.BlockSpec((tm, D_MODEL), lambda i: (i, 0))
    hid = pl.BlockSpec((tm, D_FF), lambda i: (i, 0))
    sds = jax.ShapeDtypeStruct
    in_specs = [row, _resident(g), _resident(wg_t), _resident(wu_t), _resident(wd)]
    out_specs = [row, row, hid, hid, hid]
    out_shape = [sds((t, D_MODEL), F32), sds((t, D_MODEL), BF16), sds((t, D_FF), BF16), sds((t, D_FF), BF16),
                 sds((t, D_FF), BF16)]
    args = (x, g, wg_t, wu_t, wd)
    if head is not None:
        in_specs += [_resident(head[0]), row]
        out_specs += [pl.BlockSpec((1, 128), lambda i: (0, 0)), pl.BlockSpec((1, D_MODEL), lambda i: (0, 0))]
        out_shape += [sds((1, 128), F32), sds((1, D_MODEL), F32)]
        args += tuple(head)
    return _call(body, name=name, grid=(t // tm,), in_specs=in_specs, out_specs=out_specs, out_shape=out_shape,
                 sem=("parallel",) if head is None else ("arbitrary",), args=args, hosted=hosted)


def _ffn_bwd(x, a_pre, b_pre, g, dh, wg_t, wu_t, wd, name):
    t = x.shape[0]
    tm = 512

    def body(x_ref, a_ref, b_ref, g_ref, dh_ref, wg_ref, wu_ref, wd_ref,
             dx_ref, dg_ref, da_ref, db_ref):
        @pl.when(pl.program_id(0) == 0)
        def _():
            dg_ref[...] = jnp.zeros_like(dg_ref)

        dhh = (0.5 * dh_ref[...]).astype(BF16)
        for c in range(0, D_FF, FF_CHUNK):
            cols = slice(c, c + FF_CHUNK)
            a = a_ref[:, cols].astype(F32)
            b = b_ref[:, cols].astype(F32)
            s = jax.nn.sigmoid(a)
            silu = a * s
            dhff = _nt(dhh, wd_ref[cols, :])
            da_ref[:, cols] = (dhff * b * (s * (1.0 + a * (1.0 - s)))).astype(BF16)
            db_ref[:, cols] = (dhff * silu).astype(BF16)
        dn = _nn(da_ref[...], wg_ref[...]) + _nn(db_ref[...], wu_ref[...])
        xhat, r = _rms(x_ref[...], g_ref[...])
        dx, dg = _rms_bwd(dn, xhat, r, g_ref[...])
        dx_ref[...] = dh_ref[...] + dx
        dg_ref[...] += dg

    row = pl.BlockSpec((tm, D_MODEL), lambda i: (i, 0))
    hid = pl.BlockSpec((tm, D_FF), lambda i: (i, 0))
    return pl.pallas_call(
        body, name=name, grid=(t // tm,),
        in_specs=[row, hid, hid, _resident(g), row, _resident(wg_t), _resident(wu_t), _resident(wd)],
        out_specs=[row, pl.BlockSpec((1, D_MODEL), lambda i: (0, 0)), hid, hid],
        out_shape=[jax.ShapeDtypeStruct((t, D_MODEL), F32), jax.ShapeDtypeStruct((1, D_MODEL), F32),
                   jax.ShapeDtypeStruct((t, D_FF), BF16), jax.ShapeDtypeStruct((t, D_FF), BF16)],
        compiler_params=_params("arbitrary"),
    )(x, a_pre, b_pre, g, dh, wg_t, wu_t, wd)


def _tn_matmul(a, b, rc, name, hosted=(), scale=None):
    t, r = a.shape
    c = b.shape[1]
    tk = min(t, 2048)

    def body(a_ref, b_ref, o_ref):
        @pl.when(pl.program_id(1) == 0)
        def _():
            o_ref[...] = jnp.zeros_like(o_ref)

        o_ref[...] += _tn(a_ref[...].astype(BF16), b_ref[...].astype(BF16))
        if scale is not None:
            @pl.when(pl.program_id(1) == t // tk - 1)
            def _():
                o_ref[...] *= scale

    (out,), got = _call(
        body, name=name, grid=(r // rc, t // tk),
        in_specs=[pl.BlockSpec((tk, rc), lambda i, k: (k, i)), pl.BlockSpec((tk, c), lambda i, k: (k, 0))],
        out_specs=[pl.BlockSpec((rc, c), lambda i, k: (i, 0))],
        out_shape=[jax.ShapeDtypeStruct((r, c), F32)],
        sem=("parallel", "arbitrary"), args=(a, b), hosted=hosted)
    return out, got


def _unfold(blk_ref, slab_ref, d):
    if d == 1:
        return blk_ref[...]
    n = blk_ref.shape[0]
    for r in range(d):
        for half in range(2):
            c0 = 256 * r + 128 * half
            slab_ref[half, pl.ds(r, n, stride=d), :] = blk_ref[:, c0:c0 + 128]
    return jnp.concatenate([slab_ref[0], slab_ref[1]], axis=1)


def _fold(x, slab_ref, out_ref, d):
    if d == 1:
        out_ref[...] = x.astype(out_ref.dtype)
        return
    n, w = out_ref.shape[0], x.shape[1]
    for part in range(w // 128):
        slab_ref[part] = x[:, 128 * part:128 * (part + 1)]
    for r in range(d):
        for part in range(w // 128):
            c0 = w * r + 128 * part
            out_ref[:, c0:c0 + 128] = slab_ref[part, pl.ds(r, n, stride=d), :].astype(out_ref.dtype)


DILATIONS = tuple(d for _, d in DIL_GROUPS)


PIECE_W = (256,) * 9 + (512, 128, 128, GATE_W)
PIECE_D = DILATIONS * 3 + (1, 1, 1, 1)


def _inproj_fwd(h, g, w_t, b_in, name, hosted=()):
    t = h.shape[0]
    tm, nc = 512, 256
    dilated = [(gi, d) for gi, d in enumerate(DILATIONS) if d > 1]

    def body(h_ref, g_ref, w_ref, b_ref, u_ref, zq_ref, zg_ref, *rest):
        zf_refs, slabs = rest[:len(dilated)], rest[len(dilated):]
        xhat, _ = _rms(h_ref[...], g_ref[...])
        u = (xhat * g_ref[...]).astype(BF16)
        u_ref[...] = u
        for c in range(D_IN // nc):
            z = _nt(u, w_ref[c * nc:(c + 1) * nc, :]) + b_ref[:, c * nc:(c + 1) * nc]
            if c < QKV_W // nc:
                zq_ref[:, c * nc:(c + 1) * nc] = z.astype(BF16)
            else:
                zg_ref[:, c * nc - QKV_W:(c + 1) * nc - QKV_W] = z.astype(BF16)
            part, gi = divmod(c, len(DILATIONS))
            for k, (gk, d) in enumerate(dilated):
                if part < 3 and gi == gk:
                    slab, n = slabs[3 * k + part], tm // d
                    slab[0] = z[:, :128]
                    slab[1] = z[:, 128:]
                    for r in range(d):
                        for half in range(2):
                            c0 = 768 * r + 256 * part + 128 * half
                            zf_refs[k][:, c0:c0 + 128] = slab[half, pl.ds(r, n, stride=d), :].astype(BF16)

    row = lambda w: pl.BlockSpec((tm, w), lambda i: (i, 0))
    full = _resident
    sds = jax.ShapeDtypeStruct
    return _call(
        body, name=name, grid=(t // tm,),
        in_specs=[row(D_MODEL), full(g), full(w_t), full(b_in)],
        out_specs=[row(D_MODEL), row(QKV_W), row(GATE_W)]
        + [pl.BlockSpec((tm // d, d * 768), lambda i: (i, 0)) for _, d in dilated],
        out_shape=[sds((t, D_MODEL), BF16), sds((t, QKV_W), BF16), sds((t, GATE_W), BF16)]
        + [sds((t // d, d * 768), BF16) for _, d in dilated],
        scratch=[pltpu.VMEM((2, tm, 128), F32)] * (3 * len(dilated)),
        sem=("parallel",), args=(h, g, w_t, b_in), hosted=hosted)


def _inproj_bwd(pieces, h, g, dh_res, w_t, name):
    t = h.shape[0]
    tm = 256
    npiece = len(PIECE_W)
    offs = np.concatenate([[0], np.cumsum(PIECE_W)]).tolist()

    def body(*refs):
        p_refs = refs[:npiece]
        h_ref, g_ref, dhr_ref, w_ref, dh_ref, dz_ref, db_ref, dg_ref = refs[npiece:npiece + 8]
        slabs = list(refs[npiece + 8:])
        i = pl.program_id(0)

        @pl.when(i == 0)
        def _():
            db_ref[...] = jnp.zeros_like(db_ref)
            dg_ref[...] = jnp.zeros_like(dg_ref)

        du = jnp.zeros((tm, D_MODEL), F32)
        for k in range(npiece):
            o, w = offs[k], PIECE_W[k]
            token_order = _unfold(p_refs[k], slabs.pop(), PIECE_D[k]).astype(BF16) if PIECE_D[k] > 1 else None
            for c0 in range(0, w, 512):
                cw = min(512, w - c0)
                pz = p_refs[k][:, c0:c0 + cw] if token_order is None else token_order
                dz_ref[:, o + c0:o + c0 + cw] = pz
                db_ref[:, o + c0:o + c0 + cw] += jnp.sum(pz.astype(F32), axis=0, keepdims=True)
                du = du + _nn(pz, w_ref[o + c0:o + c0 + cw, :])
        xhat, r = _rms(h_ref[...], g_ref[...])
        dx, dg = _rms_bwd(du, xhat, r, g_ref[...])
        dh_ref[...] = dhr_ref[...] + dx
        dg_ref[...] += dg

    row = lambda w: pl.BlockSpec((tm, w), lambda i: (i, 0))
    full = lambda shp: pl.BlockSpec(shp, lambda i: (0, 0))
    return pl.pallas_call(
        body, name=name, grid=(t // tm,),
        in_specs=[pl.BlockSpec((tm // d, d * w), lambda i: (i, 0)) for w, d in zip(PIECE_W, PIECE_D)]
        + [row(D_MODEL), _resident(g), row(D_MODEL), _resident(w_t)],
        out_specs=[row(D_MODEL), row(D_IN), full((1, D_IN)), full((1, D_MODEL))],
        out_shape=[jax.ShapeDtypeStruct((t, D_MODEL), F32), jax.ShapeDtypeStruct((t, D_IN), BF16),
                   jax.ShapeDtypeStruct((1, D_IN), F32), jax.ShapeDtypeStruct((1, D_MODEL), F32)],
        scratch_shapes=[pltpu.VMEM((2, tm, 128), F32)] * sum(d > 1 for d in PIECE_D),
        compiler_params=_params("arbitrary"),
    )(*pieces, h, g, dh_res, w_t)


def _t5_bucket(dist):
    max_exact = N_BUCKETS // 2
    n = jnp.maximum(dist, 0)
    nf = jnp.maximum(n, 1).astype(jnp.float32)
    large = max_exact + (jnp.log(nf / max_exact) / math.log(MAX_DISTANCE / max_exact)
                         * (N_BUCKETS - max_exact)).astype(jnp.int32)
    large = jnp.minimum(large, N_BUCKETS - 1)
    return jnp.where(n < max_exact, n, large)


ATT_CFG = ((1, 128, 0, 4), (4, 128, 4, 4), (16, 128, 8, 4), (1, B_WINDOW - 1, A_HEADS, 8))


def _bucket_tiles():
    qi = jnp.arange(BLOCK)[:, None]
    ki = jnp.arange(2 * BLOCK)[None, :]
    dist = qi + BLOCK - ki
    return jnp.stack([_t5_bucket(dist * cfg[0]) for cfg in ATT_CFG]).astype(jnp.int32)


def _band(max_steps):
    row = lax.broadcasted_iota(jnp.int32, (BLOCK, 2 * BLOCK), 0)
    col = lax.broadcasted_iota(jnp.int32, (BLOCK, 2 * BLOCK), 1)
    dist = row + BLOCK - col
    return (dist >= 0) & (dist <= max_steps)


def _bias_build(table, buckets, hosted=()):
    def body(tab_ref, bt_ref, out_ref):
        col = lax.broadcasted_iota(jnp.int32, (BLOCK, 2 * BLOCK), 1)
        for ci, (_, max_steps, h0, nh) in enumerate(ATT_CFG):
            bt = bt_ref[ci]
            band = _band(max_steps)
            for h in range(h0, h0 + nh):
                acc = lax.fori_loop(0, N_BUCKETS, lambda b, acc: jnp.where(bt == b, tab_ref[b, h], acc),
                                    jnp.zeros((BLOCK, 2 * BLOCK), F32))
                out_ref[0, BLOCK * h:BLOCK * (h + 1), :] = jnp.where(band & (col >= BLOCK), acc, NEG)
                out_ref[1, BLOCK * h:BLOCK * (h + 1), :] = jnp.where(band, acc, NEG)

    return _call(
        body, name="bias_build", grid=(1,),
        in_specs=[pl.BlockSpec(memory_space=pltpu.SMEM), pl.BlockSpec(memory_space=pltpu.VMEM)],
        out_specs=[pl.BlockSpec(memory_space=pltpu.VMEM)],
        out_shape=[jax.ShapeDtypeStruct((2, TOTAL_HEADS * BLOCK, 2 * BLOCK), F32)],
        sem=("arbitrary",), args=(table, buckets), hosted=hosted)


def _bias_reduce(dbias, buckets, dsink_rows, hosted=()):
    def body(db_ref, bt_ref, ds_ref, out_ref, sink_ref):
        ri = lax.broadcasted_iota(jnp.int32, (N_BUCKETS, 128), 0)
        ci = lax.broadcasted_iota(jnp.int32, (N_BUCKETS, 128), 1)

        def per_bucket(b, acc):
            for cfg_i, (_, _, h0, nh) in enumerate(ATT_CFG):
                hit = bt_ref[cfg_i] == b
                for h in range(h0, h0 + nh):
                    val = jnp.sum(jnp.where(hit, db_ref[BLOCK * h:BLOCK * (h + 1), :], 0.0))
                    acc = jnp.where((ri == b) & (ci == h), val, acc)
            return acc

        out_ref[...] = lax.fori_loop(0, N_BUCKETS, per_bucket, jnp.zeros((N_BUCKETS, 128), F32))
        for h in range(8):
            sink_ref[h:h + 1, :] = jnp.sum(ds_ref[BLOCK * h:BLOCK * (h + 1), :], axis=0, keepdims=True)

    return _call(
        body, name="bias_reduce", grid=(1,),
        in_specs=[pl.BlockSpec(memory_space=pltpu.VMEM)] * 3,
        out_specs=[pl.BlockSpec(memory_space=pltpu.VMEM)] * 2,
        out_shape=[jax.ShapeDtypeStruct((N_BUCKETS, 128), F32), jax.ShapeDtypeStruct((8, 128), F32)],
        sem=("arbitrary",), args=(dbias, buckets, dsink_rows), hosted=hosted)


class _Att:
    def __init__(self, cfg_i):
        stride, _, h0, nh = ATT_CFG[cfg_i]
        self.d = stride if cfg_i < 3 else 1
        self.h0, self.nh = h0, nh
        self.row_w = QKV_W if self.d == 1 else 3 * 256
        if cfg_i < 3:
            self.nq, self.wkv = 1, 256
            self.q_unit = [cfg_i if self.d == 1 else 0]
            self.k_unit, self.v_unit = (3 + cfg_i, 6 + cfg_i) if self.d == 1 else (1, 2)
            self.sinks = False
        else:
            self.nq, self.wkv = 2, 128
            self.q_unit = [9, 10]
            self.k_unit, self.v_unit = 22, 23
            self.sinks = True
        self.wq = 256 * self.nq


def _att_in_specs(cf, bsz):
    uq, ukv = cf.row_w // 256, cf.row_w // cf.wkv
    specs = [pl.BlockSpec((bsz, BLOCK, 256), functools.partial(lambda r, j, u: (0, j, r * uq + u), u=u))
             for u in cf.q_unit]
    for unit in (cf.k_unit, cf.v_unit):
        specs.append(pl.BlockSpec((bsz, BLOCK, cf.wkv),
                                  functools.partial(lambda r, j, u: (0, jnp.maximum(j - 1, 0), r * ukv + u), u=unit)))
        specs.append(pl.BlockSpec((bsz, BLOCK, cf.wkv),
                                  functools.partial(lambda r, j, u: (0, j, r * ukv + u), u=unit)))
    for qb in range(cf.nq):
        specs.append(pl.BlockSpec((None, HEADS_PER_BLOCK * BLOCK, 2 * BLOCK), functools.partial(
            lambda r, j, u: (jnp.minimum(j, 1), u, 0), u=cf.h0 // HEADS_PER_BLOCK + qb)))
    if cf.sinks:
        specs += [pl.BlockSpec((HEADS_PER_BLOCK * BLOCK, 128), functools.partial(lambda r, j, u: (u, 0), u=qb))
                  for qb in range(cf.nq)]
    return specs


HEADS_PER_BLOCK = 4


def _head_masks(rows):
    head = lax.broadcasted_iota(jnp.int32, (rows, 256), 1) // HEAD_DIM
    return [head == h for h in range(HEADS_PER_BLOCK)]


def _stack_heads(x, masks):
    return jnp.concatenate([jnp.where(m, x, jnp.zeros_like(x)) for m in masks], axis=0)


def _unstack_heads(x4, masks):
    blocks = [x4[BLOCK * h:BLOCK * (h + 1)] for h in range(HEADS_PER_BLOCK)]
    return jnp.where(masks[0], blocks[0], jnp.where(masks[1], blocks[1], jnp.where(masks[2], blocks[2], blocks[3])))


def _row_value(x):
    return jnp.max(x, axis=-1, keepdims=True)


def _kv_operands(cf, x):
    if cf.wkv == 256:
        return [x]
    lane = lax.broadcasted_iota(jnp.int32, x.shape, 1)
    swapped = pltpu.roll(x, HEAD_DIM, 1)
    halves = [jnp.where(lane < HEAD_DIM, x, swapped), jnp.where(lane < HEAD_DIM, swapped, x)]
    return [jnp.concatenate([half, half], axis=1) for half in halves]


def _kv_fold(cf, grads):
    if cf.wkv == 256:
        return grads[0]
    folded = []
    for g in grads:
        x = g[:, :128] + g[:, 128:]
        folded.append(x + pltpu.roll(x, HEAD_DIM, 1))
    lane = lax.broadcasted_iota(jnp.int32, folded[0].shape, 1)
    return jnp.where(lane < HEAD_DIM, folded[0], folded[1])


def _attn_fwd(cf, zf, bias, sinks, name, hosted=()):
    bsz, l, _ = zf.shape
    nb = l // BLOCK

    def body(*refs):
        refs = list(refs)
        q_refs = [refs.pop(0) for _ in range(cf.nq)]
        kp_ref, kc_ref, vp_ref, vc_ref = [refs.pop(0) for _ in range(4)]
        bias_refs = [refs.pop(0) for _ in range(cf.nq)]
        sink_refs = [refs.pop(0) for _ in range(cf.nq)] if cf.sinks else None
        o_ref, lse_ref = refs
        masks, kv_masks = _head_masks(BLOCK), _head_masks(2 * BLOCK)
        sinks4 = [_row_value(ref[...]) for ref in sink_refs] if cf.sinks else None
        for bi in range(bsz):
            k = jnp.concatenate([kp_ref[bi], kc_ref[bi]], axis=0)
            v = jnp.concatenate([vp_ref[bi], vc_ref[bi]], axis=0)
            k_ops, v_ops = _kv_operands(cf, k), _kv_operands(cf, v)
            for qb in range(cf.nq):
                cols = slice(256 * qb, 256 * (qb + 1))
                kb, vb = k_ops[qb], v_ops[qb]
                q4 = _stack_heads(q_refs[qb][bi] * (HEAD_DIM ** -0.5), masks)
                s = _nt(q4, kb) + bias_refs[qb][...]
                m = jnp.max(s, axis=-1, keepdims=True)
                if cf.sinks:
                    sk = sinks4[qb]
                    m = jnp.maximum(m, sk)
                p = jnp.exp(s - m)
                den = jnp.sum(p, axis=-1, keepdims=True)
                if cf.sinks:
                    den = den + jnp.exp(sk - m)
                pn = (p * (1.0 / den)).astype(BF16)
                p_lanes = jnp.concatenate([pn[BLOCK * h:BLOCK * (h + 1)] for h in range(HEADS_PER_BLOCK)], axis=1)
                v4 = jnp.concatenate([jnp.where(mk, vb, jnp.zeros_like(vb)) for mk in kv_masks], axis=0)
                o_ref[bi, :, cols] = _nn(p_lanes, v4)
                lse_ref[bi, :, cols] = _unstack_heads(
                    jnp.broadcast_to(m + jnp.log(den), (HEADS_PER_BLOCK * BLOCK, 256)), masks)

    in_specs = _att_in_specs(cf, bsz)
    args = [zf] * (cf.nq + 4) + [bias] * cf.nq + ([sinks] * cf.nq if cf.sinks else [])
    out = pl.BlockSpec((bsz, BLOCK, cf.wq), lambda r, j: (0, j, r))
    shape = jax.ShapeDtypeStruct((bsz, l, cf.d * cf.wq), F32)
    return _call(
        body, name=name, grid=(cf.d, nb), in_specs=in_specs, out_specs=[out, out], out_shape=[shape, shape],
        sem=("parallel", "arbitrary"), args=args, hosted=hosted)


def _attn_bwd(cf, zf, bias, sinks, stats, dy, name, hosted=()):
    bsz, l, _ = zf.shape
    nb = l // BLOCK

    def body(*refs):
        refs = list(refs)
        q_refs = [refs.pop(0) for _ in range(cf.nq)]
        kp_ref, kc_ref, vp_ref, vc_ref = [refs.pop(0) for _ in range(4)]
        bias_refs = [refs.pop(0) for _ in range(cf.nq)]
        sink_refs = [refs.pop(0) for _ in range(cf.nq)] if cf.sinks else None
        st_ref, dy_ref, dq_ref, dk_ref, dv_ref, dbias_ref = [refs.pop(0) for _ in range(6)]
        dsink_ref = refs.pop(0) if cf.sinks else None
        dk_acc, dv_acc = refs
        r, j = pl.program_id(0), pl.program_id(1)

        @pl.when((r == 0) & (j == 0))
        def _():
            dbias_ref[...] = jnp.zeros_like(dbias_ref)
            if cf.sinks:
                dsink_ref[...] = jnp.zeros_like(dsink_ref)

        @pl.when(j == 0)
        def _():
            dk_acc[...] = jnp.zeros_like(dk_acc)
            dv_acc[...] = jnp.zeros_like(dv_acc)

        masks = _head_masks(BLOCK)
        cur = pl.ds(pl.multiple_of(j * BLOCK, BLOCK), BLOCK)
        prev = pl.ds(pl.multiple_of(jnp.maximum(j - 1, 0) * BLOCK, BLOCK), BLOCK)
        sinks4 = [_row_value(ref[...]) for ref in sink_refs] if cf.sinks else None
        ds_sum, dsink_sum = [None] * cf.nq, [None] * cf.nq
        for bi in range(bsz):
            k = jnp.concatenate([kp_ref[bi], kc_ref[bi]], axis=0)
            v = jnp.concatenate([vp_ref[bi], vc_ref[bi]], axis=0)
            dk_blocks, dv_blocks = [], []
            k_ops, v_ops = _kv_operands(cf, k), _kv_operands(cf, v)
            for qb in range(cf.nq):
                cols = slice(256 * qb, 256 * (qb + 1))
                kb, vb = k_ops[qb], v_ops[qb]
                q4 = _stack_heads(q_refs[qb][bi] * (HEAD_DIM ** -0.5), masks)
                st = st_ref[bi, :, 128 * qb:128 * (qb + 1)]
                lt4, e4 = _read_stats(st, 0), _read_stats(st, 1)
                pa = jnp.exp(_nt(q4, kb) + bias_refs[qb][...] - lt4)
                dy = dy_ref[bi, :, cols]
                dy4 = _stack_heads(dy, masks)
                ds = pa * (_nt(dy4, vb) - e4)
                ds_sum[qb] = ds if ds_sum[qb] is None else ds_sum[qb] + ds
                if cf.sinks:
                    dsk = jnp.exp(sinks4[qb] - lt4) * e4
                    dsink_sum[qb] = dsk if dsink_sum[qb] is None else dsink_sum[qb] + dsk
                dsb = ds.astype(BF16)
                dq_ref[bi, :, cols] = (_unstack_heads(_nn(dsb, kb), masks) * (HEAD_DIM ** -0.5)).astype(dq_ref.dtype)
                dk_blocks.append(_tn(dsb, q4))
                dv_blocks.append(_tn(pa.astype(BF16), dy4))
            dk_new, dv_new = _kv_fold(cf, dk_blocks), _kv_fold(cf, dv_blocks)
            dk_acc[bi, cur, :] += dk_new[BLOCK:]
            dv_acc[bi, cur, :] += dv_new[BLOCK:]
            dk_acc[bi, prev, :] += dk_new[:BLOCK]
            dv_acc[bi, prev, :] += dv_new[:BLOCK]
        for qb in range(cf.nq):
            rows = slice(HEADS_PER_BLOCK * BLOCK * qb, HEADS_PER_BLOCK * BLOCK * (qb + 1))
            dbias_ref[rows, :] += ds_sum[qb]
            if cf.sinks:
                dsink_ref[rows, :] -= dsink_sum[qb]

        @pl.when(j == nb - 1)
        def _():
            dk_ref[...] = dk_acc[...].astype(dk_ref.dtype)
            dv_ref[...] = dv_acc[...].astype(dv_ref.dtype)

    tok = pl.BlockSpec((bsz, BLOCK, cf.wq), lambda r, j: (0, j, r))
    in_specs = _att_in_specs(cf, bsz) + [pl.BlockSpec((bsz, BLOCK, cf.wq // 2), lambda r, j: (0, j, r)), tok]
    args = [zf] * (cf.nq + 4) + [bias] * cf.nq + ([sinks] * cf.nq if cf.sinks else []) + [stats, dy]
    seq = pl.BlockSpec((bsz, l, cf.wkv), lambda r, j: (0, 0, r))
    out_specs = [tok, seq, seq, pl.BlockSpec((cf.nh * BLOCK, 2 * BLOCK), lambda r, j: (0, 0))]
    grad_dtype = BF16 if cf.d == 1 else F32
    out_shape = [jax.ShapeDtypeStruct((bsz, l, cf.d * cf.wq), grad_dtype),
                 jax.ShapeDtypeStruct((bsz, l, cf.d * cf.wkv), grad_dtype),
                 jax.ShapeDtypeStruct((bsz, l, cf.d * cf.wkv), grad_dtype),
                 jax.ShapeDtypeStruct((cf.nh * BLOCK, 2 * BLOCK), F32)]
    if cf.sinks:
        out_specs.append(pl.BlockSpec((cf.nh * BLOCK, 128), lambda r, j: (0, 0)))
        out_shape.append(jax.ShapeDtypeStruct((cf.nh * BLOCK, 128), F32))
    return _call(
        body, name=name, grid=(cf.d, nb), in_specs=in_specs, out_specs=out_specs, out_shape=out_shape,
        scratch=[pltpu.VMEM((bsz, l, cf.wkv), F32), pltpu.VMEM((bsz, l, cf.wkv), F32)],
        sem=("arbitrary", "arbitrary"), args=args, hosted=hosted)


def _merge_fwd(o_a, lse_a, o_b, zg, h, wa_t, wb_t, wout, name, hosted=()):
    t = h.shape[0]
    tm = 512

    def body(o1, o2, o3, l1, l2, l3, ob_ref, zg_ref, h_ref, wa_ref, wb_ref, wo_ref, h2_ref, ya_ref, lt_ref, *slabs):
        o = [_unfold(ref, slabs[i], d) for i, (ref, d) in enumerate(zip((o1, o2, o3), DILATIONS))]
        l = [_unfold(ref, slabs[3 + i], d) for i, (ref, d) in enumerate(zip((l1, l2, l3), DILATIONS))]
        m = jnp.maximum(jnp.maximum(l[0], l[1]), l[2])
        e1, e2, e3 = jnp.exp(l[0] - m), jnp.exp(l[1] - m), jnp.exp(l[2] - m)
        se = e1 + e2 + e3
        ya = (e1 / se) * o[0] + (e2 / se) * o[1] + (e3 / se) * o[2]
        ya_ref[...] = ya
        lt_ref[...] = m + jnp.log(se)
        pa = _nt(ya.astype(BF16), wa_ref[...])
        pb = _nt(ob_ref[...].astype(BF16), wb_ref[...])
        merged = (jax.nn.sigmoid(zg_ref[:, :D_MODEL].astype(F32)) * pa
                  + jax.nn.sigmoid(zg_ref[:, D_MODEL:].astype(F32)) * pb)
        h2_ref[...] = h_ref[...] + _nn(merged.astype(BF16), wo_ref[...])

    row = lambda w: pl.BlockSpec((tm, w), lambda i: (i, 0))
    folded = [pl.BlockSpec((tm // d, d * 256), lambda i: (i, 0)) for d in DILATIONS]
    full = _resident
    return _call(
        body, name=name, grid=(t // tm,),
        in_specs=folded + folded + [row(512), row(GATE_W), row(D_MODEL), full(wa_t), full(wb_t), full(wout)],
        out_specs=[row(D_MODEL), row(256), row(256)],
        out_shape=[jax.ShapeDtypeStruct((t, D_MODEL), F32), jax.ShapeDtypeStruct((t, 256), F32),
                   jax.ShapeDtypeStruct((t, 256), F32)],
        scratch=[pltpu.VMEM((2, tm, 128), F32)] * 6,
        sem=("parallel",), args=(*o_a, *lse_a, o_b, zg, h, wa_t, wb_t, wout), hosted=hosted)


STAT_LANES = 32


def _head_stats(lt, dy, y):
    rows, w = dy.shape
    lane = lax.broadcasted_iota(jnp.int32, (rows, w), 1)
    out_lane = lax.broadcasted_iota(jnp.int32, (rows, w // 2), 1)
    prod = dy * y
    out = jnp.zeros((rows, w // 2), F32)
    for hd in range(w // HEAD_DIM):
        mine = lane // HEAD_DIM == hd
        dot = jnp.sum(jnp.where(mine, prod, 0.0), axis=-1, keepdims=True)
        both = jnp.where(out_lane % STAT_LANES < STAT_LANES // 2, _row_value(jnp.where(mine, lt, NEG)), dot)
        out = jnp.where(out_lane // STAT_LANES == hd, both, out)
    return out


def _read_stats(st, part):
    lane = lax.broadcasted_iota(jnp.int32, st.shape, 1)
    half = (lane % STAT_LANES < STAT_LANES // 2) == (part == 0)
    return jnp.concatenate([_row_value(jnp.where((lane // STAT_LANES == hd) & half, st, NEG))
                            for hd in range(HEADS_PER_BLOCK)], axis=0)


def _merge_bwd(dh, ya, lse_tot, o_b, lse_b, zg, wa_t, wb_t, wout, name, hosted=()):
    t = dh.shape[0]
    tm = 512

    def body(dh_ref, ya_ref, lt_ref, ob_ref, lb_ref, zg_ref, wa_ref, wb_ref, wo_ref,
             mg_ref, dpa_ref, dpb_ref, dzg_ref, dy1, dy2, dy3, st1, st2, st3, dyb_ref, stb_ref, *slabs):
        dm = _nt(dh_ref[...].astype(BF16), wo_ref[...])
        pa = _nt(ya_ref[...].astype(BF16), wa_ref[...])
        pb = _nt(ob_ref[...].astype(BF16), wb_ref[...])
        sa = jax.nn.sigmoid(zg_ref[:, :D_MODEL].astype(F32))
        sb = jax.nn.sigmoid(zg_ref[:, D_MODEL:].astype(F32))
        mg_ref[...] = (sa * pa + sb * pb).astype(BF16)
        dpa = (dm * sa).astype(BF16)
        dpb = (dm * sb).astype(BF16)
        dpa_ref[...] = dpa
        dpb_ref[...] = dpb
        dzg_ref[:, :D_MODEL] = (dm * pa * (sa * (1.0 - sa))).astype(BF16)
        dzg_ref[:, D_MODEL:] = (dm * pb * (sb * (1.0 - sb))).astype(BF16)
        dya = _nn(dpa, wa_ref[...])
        dyb = _nn(dpb, wb_ref[...])
        stats = _head_stats(lt_ref[...], dya, ya_ref[...])
        for i, d in enumerate(DILATIONS):
            _fold(dya, slabs[i], (dy1, dy2, dy3)[i], d)
            _fold(stats, slabs[3 + i], (st1, st2, st3)[i], d)
        dyb_ref[...] = dyb.astype(BF16)
        stb_ref[...] = _head_stats(lb_ref[...], dyb, ob_ref[...])

    row = lambda w: pl.BlockSpec((tm, w), lambda i: (i, 0))
    folded = [pl.BlockSpec((tm // d, d * 256), lambda i: (i, 0)) for d in DILATIONS]
    full = _resident
    sds = jax.ShapeDtypeStruct
    dy_shape = [sds((t // d, d * 256), BF16) for d in DILATIONS]
    st_shape = [sds((t // d, d * 128), F32) for d in DILATIONS]
    st_specs = [pl.BlockSpec((tm // d, d * 128), lambda i: (i, 0)) for d in DILATIONS]
    return _call(
        body, name=name, grid=(t // tm,),
        in_specs=[row(D_MODEL), row(256), row(256), row(512), row(512), row(GATE_W), full(wa_t), full(wb_t),
                  full(wout)],
        out_specs=[row(D_MODEL), row(D_MODEL), row(D_MODEL), row(GATE_W)] + folded + st_specs + [row(512), row(256)],
        out_shape=[sds((t, D_MODEL), BF16), sds((t, D_MODEL), BF16), sds((t, D_MODEL), BF16),
                   sds((t, GATE_W), BF16)] + dy_shape + st_shape + [sds((t, 512), BF16), sds((t, 256), F32)],
        scratch=[pltpu.VMEM((2, tm, 128), F32)] * 6,
        sem=("parallel",), args=(dh, ya, lse_tot, o_b, lse_b, zg, wa_t, wb_t, wout), hosted=hosted)


def _pair_sum(grad, got, name):
    _, _, r, cdim = grad.shape
    core = lax.axis_index("c").astype(jnp.int32).reshape(1)

    def body(core_ref, g_ref, s_ref, o_ref):
        o_ref[...] = (g_ref[...] + s_ref[...]).astype(BF16)

    return pl.pallas_call(
        body, name=name,
        grid_spec=pltpu.PrefetchScalarGridSpec(
            num_scalar_prefetch=1, grid=(N_CHIP,),
            in_specs=[pl.BlockSpec((None, None, r, cdim), lambda q, core_ref: (q, core_ref[0], 0, 0)),
                      pl.BlockSpec((None, None, r, cdim), lambda q, core_ref: (q, 0, 0, 0))],
            out_specs=pl.BlockSpec((None, r, cdim), lambda q, core_ref: (q, 0, 0))),
        out_shape=jax.ShapeDtypeStruct((N_CHIP, r, cdim), BF16),
        compiler_params=_params("parallel"),
    )(core, grad, got)


def _adamw_math(w, g, m, v):
    m = ADAM_B1 * m + (1.0 - ADAM_B1) * g
    v = ADAM_B2 * v + (1.0 - ADAM_B2) * jnp.square(g)
    m_hat = m / (1.0 - ADAM_B1 ** ADAM_STEP)
    v_hat = v / (1.0 - ADAM_B2 ** ADAM_STEP)
    delta = -ADAM_LR * (m_hat / (jnp.sqrt(v_hat) + ADAM_EPS) + ADAM_WD * w)
    return delta, m, v


def _update(w, parts, m, v, transposed, name, hosted=()):
    r, c = parts.shape[1:]

    def body(w_ref, p_ref, m_ref, v_ref, g_ref, d_ref, m2_ref, v2_ref):
        def total(rows):
            return ((p_ref[0, rows].astype(F32) + p_ref[1, rows].astype(F32)) + p_ref[2, rows].astype(F32)) \
                + p_ref[3, rows].astype(F32)

        def update(at, g):
            g_ref[at] = g
            d_ref[at], m2_ref[at], v2_ref[at] = _adamw_math(w_ref[at], g, m_ref[at], v_ref[at])

        if not transposed:
            update((slice(None), slice(None)), total(slice(None)))
            return
        for r0 in range(0, r, 128):
            n = min(128, r - r0)
            gt = total(slice(r0, r0 + n))
            if n < 128:
                gt = jnp.concatenate([gt, jnp.zeros((128 - n, c), F32)], axis=0)
            update((slice(None), slice(r0, r0 + n)), gt.T[:, :n])

    sds = jax.ShapeDtypeStruct(w.shape, F32)
    vm = pl.BlockSpec(memory_space=pltpu.VMEM)
    return _call(body, name=name, grid=(1,), in_specs=[vm] * 4, out_specs=[vm] * 4, out_shape=[sds] * 4,
                 sem=("arbitrary",), args=(w, parts, m, v), hosted=hosted)


SMALL_ROWS = 80


def _small_update(g, w, m, v, name):
    def body(g_ref, w_ref, m_ref, v_ref, gs_ref, d_ref, m2_ref, v2_ref, got_ref, send_sems, recv_sems):
        x, y, c = _place()
        me = 4 * x + 2 * y + c
        got_ref[me] = g_ref[...]
        copies = []
        for k in range(1, N_DEV):
            peer = (x ^ (k >> 2), y ^ ((k >> 1) & 1), c ^ (k & 1))
            cp = pltpu.make_async_remote_copy(
                src_ref=g_ref, dst_ref=got_ref.at[me], send_sem=send_sems.at[k - 1], recv_sem=recv_sems.at[k - 1],
                device_id=peer, device_id_type=MESH)
            cp.start()
            copies.append(cp)
        for cp in copies:
            cp.wait()
        total = got_ref[0]
        for k in range(1, N_DEV):
            total = total + got_ref[k]
        gs_ref[...] = total
        d_ref[...], m2_ref[...], v2_ref[...] = _adamw_math(w_ref[...], total, m_ref[...], v_ref[...])

    sds = jax.ShapeDtypeStruct((SMALL_ROWS, 128), F32)
    vm = pl.BlockSpec(memory_space=pltpu.VMEM)
    return pl.pallas_call(
        body, name=name, in_specs=[vm] * 4, out_specs=[vm] * 4, out_shape=[sds] * 4,
        scratch_shapes=[pltpu.VMEM((N_DEV, SMALL_ROWS, 128), F32), pltpu.SemaphoreType.DMA((N_DEV - 1,)),
                        pltpu.SemaphoreType.DMA((N_DEV - 1,))],
    )(g, w, m, v)


def _pack_small(gains, b_in, rel_bias, sinks, last):
    rows = [a.reshape(8, 128) for a in gains] + [b_in.reshape(40, 128), rel_bias.reshape(5, 128),
                                                 jnp.pad(sinks.reshape(1, 8), ((0, 0), (0, 120))), last]
    rows.append(jnp.zeros((SMALL_ROWS - 79, 128), F32))
    return jnp.concatenate(rows, axis=0)


def _unpack_small(p, like):
    out = [p[8 * i:8 * i + 8].reshape(like[i].shape) for i in range(4)]
    out.append(p[32:72].reshape(like[4].shape))
    out.append(p[72:77].reshape(like[5].shape))
    out.append(p[77, :8].reshape(like[6].shape))
    return out


def kernel(x, ffn1_norm, ffn1_w_gate, ffn1_w_up, ffn1_w_down, mix_norm, w_in, b_in, w_branch_a, w_branch_b, w_out, sinks, rel_bias, ffn2_norm, ffn2_w_gate, ffn2_w_up, ffn2_w_down, final_norm, loss_target, m_ffn1_norm, m_ffn1_w_gate, m_ffn1_w_up, m_ffn1_w_down, m_mix_norm, m_w_in, m_b_in, m_w_branch_a, m_w_branch_b, m_w_out, m_sinks, m_rel_bias, m_ffn2_norm, m_ffn2_w_gate, m_ffn2_w_up, m_ffn2_w_down, m_final_norm, v_ffn1_norm, v_ffn1_w_gate, v_ffn1_w_up, v_ffn1_w_down, v_mix_norm, v_w_in, v_b_in, v_w_branch_a, v_w_branch_b, v_w_out, v_sinks, v_rel_bias, v_ffn2_norm, v_ffn2_w_gate, v_ffn2_w_up, v_ffn2_w_down, v_final_norm):
    bsz, seq, _ = x.shape
    t = bsz * seq
    xt = x.reshape(t, D_MODEL)
    target = loss_target.reshape(t, D_MODEL)

    big = [("ffn1_w_gate", ffn1_w_gate, m_ffn1_w_gate, v_ffn1_w_gate, True),
           ("ffn1_w_up", ffn1_w_up, m_ffn1_w_up, v_ffn1_w_up, True),
           ("ffn1_w_down", ffn1_w_down, m_ffn1_w_down, v_ffn1_w_down, False),
           ("w_in", w_in, m_w_in, v_w_in, True),
           ("w_branch_a", w_branch_a, m_w_branch_a, v_w_branch_a, True),
           ("w_branch_b", w_branch_b, m_w_branch_b, v_w_branch_b, True),
           ("w_out", w_out, m_w_out, v_w_out, False),
           ("ffn2_w_gate", ffn2_w_gate, m_ffn2_w_gate, v_ffn2_w_gate, True),
           ("ffn2_w_up", ffn2_w_up, m_ffn2_w_up, v_ffn2_w_up, True),
           ("ffn2_w_down", ffn2_w_down, m_ffn2_w_down, v_ffn2_w_down, False)]
    shard = {nm: (w[0].T if tr else w[0]).astype(BF16) for nm, w, _, _, tr in big}
    full = {}

    def gather(names):
        return _Gather([shard[nm] for nm in names])

    def keep(names, got):
        for nm, gw in zip(names, got):
            full[nm] = gw.reshape(-1, gw.shape[-1])

    ffn1_names = ["ffn1_w_gate", "ffn1_w_up", "ffn1_w_down"]
    mix_names = ["w_in", "w_branch_a", "w_branch_b", "w_out"]
    ffn2_names = ["ffn2_w_gate", "ffn2_w_up", "ffn2_w_down"]
    g1, gm, g2, gf = ffn1_norm, mix_norm, ffn2_norm, final_norm.reshape(1, D_MODEL)

    buckets = _bucket_tiles()
    (bias,), (got,) = _bias_build(rel_bias, buckets, hosted=[gather(ffn1_names)])
    keep(ffn1_names, got)
    (h1, n1, a1, b1, hff1), (got,) = _ffn_fwd(xt, g1, full["ffn1_w_gate"], full["ffn1_w_up"], full["ffn1_w_down"], "ffn1_fwd",
                                hosted=[gather(mix_names + ffn2_names)])
    keep(mix_names + ffn2_names, got)
    (u, zq, zg, *zdil), _ = _inproj_fwd(h1, gm, full["w_in"], b_in, "inproj_fwd")
    sink_rows = jnp.broadcast_to(sinks.reshape(8, 1, 1), (8, BLOCK, 128)).reshape(8 * BLOCK, 128)
    cfs = [_Att(i) for i in range(4)]
    zfold = []
    for i, cf in enumerate(cfs):
        if cf.d == 1:
            zfold.append(zq.reshape(bsz, seq, QKV_W))
        else:
            zfold.append(zdil[i - 1].reshape(bsz, seq // cf.d, cf.d * cf.row_w))
    att = [None] * 4
    for i in (3, 0, 1, 2):
        cf = cfs[i]
        (o, lse), _ = _attn_fwd(cf, zfold[i], bias, sink_rows, f"attn{i}_fwd")
        att[i] = (o.reshape(t // cf.d, cf.d * cf.wq), lse.reshape(t // cf.d, cf.d * cf.wq))
    o_b, lse_b = att[3]
    (h2, ya, lse_tot), _ = _merge_fwd([a[0] for a in att[:3]], [a[1] for a in att[:3]], o_b, zg, h1,
                                      full["w_branch_a"], full["w_branch_b"], full["w_out"], "merge_fwd")
    (dh3, n2, a2, b2, hff2, loss_part, dgf), _ = _ffn_fwd(
        h2, g2, full["ffn2_w_gate"], full["ffn2_w_up"], full["ffn2_w_down"], "ffn2_fwd", head=(gf, target))

    grads, pair, from_chips = {}, {}, {}

    def by_owner(nm):
        return grads[nm].reshape(N_CHIP, 2, -1, grads[nm].shape[-1])

    def to_core(names):
        return _CoreExchange([by_owner(nm) for nm in names])

    def pair_up(names, got):
        for nm, sib in zip(names, got):
            pair[nm] = _pair_sum(by_owner(nm), sib, f"pair_sum_{nm}")

    def to_chips(names):
        return _ChipExchange([pair[nm] for nm in names])

    def landed(names, got):
        for nm, parts in zip(names, got):
            from_chips[nm] = parts

    dh2, dg2, da, db = _ffn_bwd(h2, a2, b2, g2, dh3, full["ffn2_w_gate"], full["ffn2_w_up"],
                                          full["ffn2_w_down"], "ffn2_bwd")
    grads["ffn2_w_gate"], _ = _tn_matmul(da, n2, 1408, "ffn2_dgate")
    grads["ffn2_w_up"], _ = _tn_matmul(db, n2, 1408, "ffn2_dup")
    grads["ffn2_w_down"], _ = _tn_matmul(hff2, dh3, 1408, "ffn2_ddown", scale=0.5)
    (merged, dpa, dpb, dzg, *cot), (got,) = _merge_bwd(
        dh2, ya, lse_tot, o_b, lse_b, zg, full["w_branch_a"], full["w_branch_b"], full["w_out"], "merge_bwd",
        hosted=[to_core(ffn2_names)])
    dys, sts = cot[0:3] + [cot[6]], cot[3:6] + [cot[7]]
    pair_up(ffn2_names, got)
    dq, dk, dv, dbias, dsink = [None] * 4, [None] * 4, [None] * 4, [None] * 4, None
    for i in (3, 0, 1, 2):
        cf = cfs[i]
        shp = (bsz, seq // cf.d, cf.d * cf.wq)
        hosted = [to_chips(ffn2_names)] if i == 3 else ()
        res, got = _attn_bwd(cf, zfold[i], bias, sink_rows, sts[i].reshape(shp[:2] + (shp[2] // 2,)), dys[i].reshape(shp),
                             f"attn{i}_bwd", hosted=hosted)
        if i == 3:
            landed(ffn2_names, got[0])
        dq[i] = res[0].reshape(t // cf.d, cf.d * cf.wq)
        dk[i] = res[1].reshape(t // cf.d, cf.d * cf.wkv)
        dv[i] = res[2].reshape(t // cf.d, cf.d * cf.wkv)
        dbias[i] = res[3]
        if cf.sinks:
            dsink = res[4]
    pieces = dq[:3] + dk[:3] + dv[:3] + [dq[3], dk[3], dv[3], dzg]
    dh1, dz, db_in, dgm = _inproj_bwd(pieces, h1, gm, dh2, full["w_in"], "inproj_bwd")
    dx, dg1, da, db = _ffn_bwd(xt, a1, b1, g1, dh1, full["ffn1_w_gate"], full["ffn1_w_up"],
                                         full["ffn1_w_down"], "ffn1_bwd")
    grads["w_in"], _ = _tn_matmul(dz, u, 1280, "dw_in")
    grads["ffn1_w_down"], (got,) = _tn_matmul(hff1, dh1, 1408, "ffn1_ddown", hosted=[to_core(["w_in"])], scale=0.5)
    pair_up(["w_in"], got)
    grads["ffn1_w_gate"], got = _tn_matmul(da, n1, 1408, "ffn1_dgate",
                                           hosted=[to_chips(["w_in"]), to_core(["ffn1_w_down"])])
    landed(["w_in"], got[0])
    pair_up(["ffn1_w_down"], got[1])
    grads["ffn1_w_up"], got = _tn_matmul(db, n1, 1408, "ffn1_dup",
                                         hosted=[to_chips(["ffn1_w_down"]), to_core(["ffn1_w_gate"])])
    landed(["ffn1_w_down"], got[0])
    pair_up(["ffn1_w_gate"], got[1])
    grads["w_out"], got = _tn_matmul(merged, dh2, 1024, "dw_out",
                                     hosted=[to_chips(["ffn1_w_gate"]), to_core(["ffn1_w_up"])])
    landed(["ffn1_w_gate"], got[0])
    pair_up(["ffn1_w_up"], got[1])
    grads["w_branch_b"], got = _tn_matmul(dpb, o_b, 1024, "dw_branch_b",
                                          hosted=[to_chips(["ffn1_w_up"]), to_core(["w_out"])])
    landed(["ffn1_w_up"], got[0])
    pair_up(["w_out"], got[1])
    grads["w_branch_a"], got = _tn_matmul(dpa, ya, 1024, "dw_branch_a",
                                          hosted=[to_chips(["w_out"]), to_core(["w_branch_b"])])
    landed(["w_out"], got[0])
    pair_up(["w_branch_b"], got[1])
    (dtable, dsinks), got = _bias_reduce(jnp.concatenate(dbias, axis=0), buckets, dsink,
                                         hosted=[to_chips(["w_branch_b"]), to_core(["w_branch_a"])])
    landed(["w_branch_b"], got[0])
    pair_up(["w_branch_a"], got[1])

    out_g, out_d, out_m, out_v = {}, {}, {}, {}
    for idx, (nm, w, m, v, tr) in enumerate(big):
        around = tr and w.shape[-1] % 128 != 0
        wmv = [a[0].T if around else a[0] for a in (w, m, v)]
        res, got = _update(wmv[0], from_chips[nm], wmv[1], wmv[2], tr and not around, f"update_{nm}",
                           hosted=[to_chips(["w_branch_a"])] if idx == 0 else ())
        if idx == 0:
            landed(["w_branch_a"], got[0])
        out_g[nm], out_d[nm], out_m[nm], out_v[nm] = [(a.T if around else a)[None] for a in res]

    small = [("ffn1_norm", ffn1_norm, m_ffn1_norm, v_ffn1_norm), ("mix_norm", mix_norm, m_mix_norm, v_mix_norm),
             ("ffn2_norm", ffn2_norm, m_ffn2_norm, v_ffn2_norm), ("final_norm", final_norm, m_final_norm, v_final_norm),
             ("b_in", b_in, m_b_in, v_b_in), ("rel_bias", rel_bias, m_rel_bias, v_rel_bias),
             ("sinks", sinks, m_sinks, v_sinks)]
    zero_row = jnp.zeros((1, 128), F32)
    pack = lambda arrs, last: _pack_small(arrs[:4], arrs[4], arrs[5], arrs[6], last)
    g_small = pack([dg1, dgm, dg2, dgf, db_in, dtable[:, :TOTAL_HEADS], dsinks[:, 0]], loss_part)
    packed = [pack([s[k] for s in small], zero_row) for k in (1, 2, 3)]
    gs, ds, ms, vs = _small_update(g_small, *packed, "small_update")
    like = [s[1] for s in small]
    for nm_s, g_, d_, m_, v_ in zip([s[0] for s in small], _unpack_small(gs, like), _unpack_small(ds, like),
                                    _unpack_small(ms, like), _unpack_small(vs, like)):
        out_g[nm_s], out_d[nm_s], out_m[nm_s], out_v[nm_s] = g_, d_, m_, v_
    loss = gs[78, 0]

    order = ["ffn1_norm", "ffn1_w_gate", "ffn1_w_up", "ffn1_w_down", "mix_norm", "w_in", "b_in", "w_branch_a",
             "w_branch_b", "w_out", "sinks", "rel_bias", "ffn2_norm", "ffn2_w_gate", "ffn2_w_up", "ffn2_w_down",
             "final_norm"]
    return (loss, dx.reshape(x.shape), *[out_g[k] for k in order], *[out_d[k] for k in order],
            *[out_m[k] for k in order], *[out_v[k] for k in order])
```

```python
import functools
import math

import numpy as np
import jax
import jax.numpy as jnp
from jax import lax
from jax.experimental import pallas as pl
from jax.experimental.pallas import tpu as pltpu

D_MODEL = 1024
D_FF = 2816
FF_CHUNK = 256
HEAD_DIM = 64
BLOCK = 128
N_BUCKETS = 32
MAX_DISTANCE = 2048
A_HEADS = 12
TOTAL_HEADS = 20
DIL_GROUPS = ((128, 1), (512, 4), (2048, 16))
B_WINDOW = 128
QKV_W = 3072
GATE_W = 2048
D_IN = QKV_W + GATE_W
EPS = 1e-6
NEG = -1e30
N_DEV = 8
N_CHIP = 4
ADAM_LR, ADAM_B1, ADAM_B2, ADAM_EPS, ADAM_WD, ADAM_STEP = 0.001, 0.9, 0.999, 1e-08, 0.01, 10
VMEM_LIMIT = 56 * 1024 * 1024
MESH = pl.DeviceIdType.MESH
BF16 = jnp.bfloat16
F32 = jnp.float32
ANY = pl.BlockSpec(memory_space=pl.ANY)


def _params(*sem):
    return pltpu.CompilerParams(dimension_semantics=sem, vmem_limit_bytes=VMEM_LIMIT)


def _resident(a):
    return pl.BlockSpec(a.shape, lambda i: (0, 0), pipeline_mode=pl.Buffered(1))


def _place():
    return lax.axis_index("x"), lax.axis_index("y"), lax.axis_index("c")


class _Gather:
    def __init__(self, shards):
        self.ins = list(shards)
        n = self.n = len(shards)
        self.out_shape = [jax.ShapeDtypeStruct((N_DEV,) + s.shape, s.dtype) for s in shards]
        self.scratch = [pltpu.SemaphoreType.DMA((7 * n,)), pltpu.SemaphoreType.DMA((7 * n,)),
                        pltpu.SemaphoreType.DMA((n,))]

    def _copies(self, ins, outs, sems):
        send_sems, recv_sems, local_sems = sems
        x, y, c = _place()
        me, sibling = (x, y, c), (x, y, 1 - c)
        chips = [(1 - x, y), (x, 1 - y), (1 - x, 1 - y)]

        def copy(i, k, block, to, src=None):
            dst = outs[i].at[4 * block[0] + 2 * block[1] + block[2]]
            return pltpu.make_async_remote_copy(
                src_ref=dst if src is None else src, dst_ref=dst, send_sem=send_sems.at[7 * i + k],
                recv_sem=recv_sems.at[7 * i + k], device_id=to, device_id_type=MESH)

        n = self.n
        south = c == 0
        relayed = (jnp.where(south, 1 - x, x), jnp.where(south, y, 1 - y), c)
        relay_to = (jnp.where(south, x, 1 - x), jnp.where(south, 1 - y, y), c)
        mine = [pltpu.make_async_copy(ins[i], outs[i].at[4 * x + 2 * y + c], local_sems.at[i]) for i in range(n)]
        first = [copy(i, 0, me, sibling, src=ins[i]) for i in range(n)]
        first += [copy(i, 1 + j, me, (*chips[j], c), src=ins[i]) for i in range(n) for j in range(2)]
        landed = [copy(i, 1 + j, (*chips[j], c), me) for j in range(2) for i in range(n)]
        passed = [copy(i, 4 + j, (*chips[j], c), sibling) for j in range(2) for i in range(n)]
        relays = [copy(i, 3, relayed, relay_to) for i in range(n)]
        diag_landed = [copy(i, 3, (*chips[2], c), me) for i in range(n)]
        diag_passed = [copy(i, 6, (*chips[2], c), sibling) for i in range(n)]
        from_sibling = [copy(i, 0, sibling, me) for i in range(n)]
        from_sibling += [copy(i, 4 + j, (*chip, 1 - c), me) for i in range(n) for j, chip in enumerate(chips)]
        return mine, first, landed, passed, relays, diag_landed, diag_passed, from_sibling

    def start(self, ins, outs, sems):
        mine, first = self._copies(ins, outs, sems)[:2]
        for cp in mine + first:
            cp.start()

    def mid(self, ins, outs, sems):
        _, _, landed, passed, relays, _, _, _ = self._copies(ins, outs, sems)
        for got in landed:
            got.wait_recv()
        for cp in relays + passed:
            cp.start()

    def end(self, ins, outs, sems):
        mine, first, _, passed, relays, diag_landed, diag_passed, from_sibling = self._copies(ins, outs, sems)
        for got, fwd in zip(diag_landed, diag_passed):
            got.wait_recv()
            fwd.start()
        for cp in from_sibling:
            cp.wait_recv()
        for cp in first + passed + relays + diag_passed:
            cp.wait_send()
        for cp in mine:
            cp.wait()


class _CoreExchange:
    def __init__(self, grads):
        self.ins = list(grads)
        n = self.n = len(grads)
        self.out_shape = [jax.ShapeDtypeStruct((N_CHIP, 1) + g.shape[2:], g.dtype) for g in grads]
        self.scratch = [pltpu.SemaphoreType.DMA((n,)), pltpu.SemaphoreType.DMA((n,))]

    def _copies(self, ins, outs, sems):
        x, y, c = _place()
        return [pltpu.make_async_remote_copy(
            src_ref=ins[i].at[:, pl.ds(1 - c, 1)], dst_ref=outs[i], send_sem=sems[0].at[i],
            recv_sem=sems[1].at[i], device_id=(x, y, 1 - c), device_id_type=MESH) for i in range(self.n)]

    def start(self, ins, outs, sems):
        for cp in self._copies(ins, outs, sems):
            cp.start()

    mid = None

    def end(self, ins, outs, sems):
        for cp in self._copies(ins, outs, sems):
            cp.wait()


class _ChipExchange:
    def __init__(self, parts):
        self.ins = list(parts)
        n = self.n = len(parts)
        self.out_shape = [jax.ShapeDtypeStruct(p.shape, p.dtype) for p in parts]
        self.scratch = [pltpu.SemaphoreType.DMA((3 * n,)), pltpu.SemaphoreType.DMA((3 * n,)),
                        pltpu.SemaphoreType.DMA((n,))]

    def _copies(self, ins, outs, sems):
        send_sems, recv_sems, local_sems = sems
        x, y, c = _place()
        my_chip = 2 * x + y
        copies = []
        for i in range(self.n):
            copies.append(pltpu.make_async_copy(ins[i].at[my_chip], outs[i].at[my_chip], local_sems.at[i]))
            for k, (qx, qy) in enumerate([(1 - x, y), (x, 1 - y), (1 - x, 1 - y)]):
                copies.append(pltpu.make_async_remote_copy(
                    src_ref=ins[i].at[2 * qx + qy], dst_ref=outs[i].at[my_chip], send_sem=send_sems.at[3 * i + k],
                    recv_sem=recv_sems.at[3 * i + k], device_id=(qx, qy, c), device_id_type=MESH))
        return copies

    def start(self, ins, outs, sems):
        for cp in self._copies(ins, outs, sems):
            cp.start()

    mid = None

    def end(self, ins, outs, sems):
        for cp in self._copies(ins, outs, sems):
            cp.wait()


def _call(body, *, name, grid, in_specs, out_specs, out_shape, args, scratch=(), sem=None, hosted=()):
    n_in, n_out, n_scr = len(in_specs), len(out_specs), len(scratch)
    x_in = [len(p.ins) for p in hosted]
    x_scr = [len(p.scratch) for p in hosted]
    steps = int(np.prod(grid))

    def wrapped(*refs):
        refs = list(refs)
        ins, refs = refs[:n_in], refs[n_in:]
        x_ins = [[refs.pop(0) for _ in range(k)] for k in x_in]
        outs, refs = refs[:n_out], refs[n_out:]
        x_outs = [[refs.pop(0) for _ in range(k)] for k in x_in]
        scr, refs = refs[:n_scr], refs[n_scr:]
        x_sems = [[refs.pop(0) for _ in range(k)] for k in x_scr]
        step = 0
        for d in range(len(grid)):
            step = step * grid[d] + pl.program_id(d)

        def phase(which, at):
            fns = [(getattr(p, which), a) for p, a in zip(hosted, zip(x_ins, x_outs, x_sems)) if getattr(p, which)]
            if fns:
                @pl.when(step == at)
                def _():
                    for fn, a in fns:
                        fn(*a)

        phase("start", 0)
        if steps > 1:
            phase("mid", (5 * steps) // 8)
        body(*ins, *outs, *scr)
        if steps == 1:
            phase("mid", 0)
        phase("end", steps - 1)

    results = pl.pallas_call(
        wrapped, name=name, grid=grid,
        in_specs=list(in_specs) + [ANY] * sum(x_in), out_specs=list(out_specs) + [ANY] * sum(x_in),
        out_shape=list(out_shape) + [s for p in hosted for s in p.out_shape],
        scratch_shapes=list(scratch) + [s for p in hosted for s in p.scratch],
        compiler_params=_params(*(("arbitrary",) * len(grid) if hosted else sem)),
    )(*args, *[a for p in hosted for a in p.ins])
    own, rest = list(results[:n_out]), list(results[n_out:])
    return own, [[rest.pop(0) for _ in range(k)] for k in x_in]


def _nt(a, b):
    return lax.dot_general(a, b, (((1,), (1,)), ((), ())), preferred_element_type=F32)


def _nn(a, b):
    return lax.dot_general(a, b, (((1,), (0,)), ((), ())), preferred_element_type=F32)


def _tn(a, b):
    return lax.dot_general(a, b, (((0,), (0,)), ((), ())), preferred_element_type=F32)


def _rms(x, g):
    r = lax.rsqrt(jnp.mean(x * x, axis=-1, keepdims=True) + EPS)
    return x * r, r


def _rms_bwd(dn, xhat, r, g):
    dg = jnp.sum(dn * xhat, axis=0, keepdims=True)
    dxh = dn * g
    dx = r * (dxh - xhat * jnp.mean(dxh * xhat, axis=-1, keepdims=True))
    return dx, dg


def _ffn_fwd(x, g, wg_t, wu_t, wd, name, hosted=(), head=None):
    t = x.shape[0]
    tm = 512 if head is None else 256

    def body(*refs):
        x_ref, g_ref, wg_ref, wu_ref, wd_ref = refs[:5]
        h_ref, n_ref, a_ref, b_ref, hff_ref = refs[-5 if head is None else -7:][:5]
        xhat, _ = _rms(x_ref[...], g_ref[...])
        n = (xhat * g_ref[...]).astype(BF16)
        n_ref[...] = n
        for c in range(0, D_FF, FF_CHUNK):
            cols = slice(c, c + FF_CHUNK)
            a = _nt(n, wg_ref[cols, :])
            b = _nt(n, wu_ref[cols, :])
            a_ref[:, cols] = a.astype(BF16)
            b_ref[:, cols] = b.astype(BF16)
            hff_ref[:, cols] = (a * jax.nn.sigmoid(a) * b).astype(BF16)
        h = x_ref[...] + 0.5 * _nn(hff_ref[...], wd_ref[...])
        if head is None:
            h_ref[...] = h
            return
        gf_ref, t_ref, loss_ref, dgf_ref = refs[5], refs[6], refs[-2], refs[-1]

        @pl.when(pl.program_id(0) == 0)
        def _():
            loss_ref[...] = jnp.zeros_like(loss_ref)
            dgf_ref[...] = jnp.zeros_like(dgf_ref)

        yhat, r = _rms(h, gf_ref[...])
        err = yhat * gf_ref[...] - t_ref[...]
        loss_ref[...] += 0.5 * jnp.sum(jnp.mean(err * err, axis=-1, keepdims=True), axis=0, keepdims=True)
        h_ref[...], dgf = _rms_bwd(err * (1.0 / D_MODEL), yhat, r, gf_ref[...])
        dgf_ref[...] += dgf

    row = pl.BlockSpec((tm, D_MODEL), lambda i: (i, 0))
    hid = pl.BlockSpec((tm, D_FF), lambda i: (i, 0))
    sds = jax.ShapeDtypeStruct
    in_specs = [row, _resident(g), _resident(wg_t), _resident(wu_t), _resident(wd)]
    out_specs = [row, row, hid, hid, hid]
    out_shape = [sds((t, D_MODEL), F32), sds((t, D_MODEL), BF16), sds((t, D_FF), BF16), sds((t, D_FF), BF16),
                 sds((t, D_FF), BF16)]
    args = (x, g, wg_t, wu_t, wd)
    if head is not None:
        in_specs += [_resident(head[0]), row]
        out_specs += [pl.BlockSpec((1, 128), lambda i: (0, 0)), pl.BlockSpec((1, D_MODEL), lambda i: (0, 0))]
        out_shape += [sds((1, 128), F32), sds((1, D_MODEL), F32)]
        args += tuple(head)
    return _call(body, name=name, grid=(t // tm,), in_specs=in_specs, out_specs=out_specs, out_shape=out_shape,
                 sem=("parallel",) if head is None else ("arbitrary",), args=args, hosted=hosted)


def _ffn_bwd(x, a_pre, b_pre, g, dh, wg_t, wu_t, wd, name):
    t = x.shape[0]
    tm = 512

    def body(x_ref, a_ref, b_ref, g_ref, dh_ref, wg_ref, wu_ref, wd_ref,
             dx_ref, dg_ref, da_ref, db_ref):
        @pl.when(pl.program_id(0) == 0)
        def _():
            dg_ref[...] = jnp.zeros_like(dg_ref)

        dhh = (0.5 * dh_ref[...]).astype(BF16)
        for c in range(0, D_FF, FF_CHUNK):
            cols = slice(c, c + FF_CHUNK)
            a = a_ref[:, cols].astype(F32)
            b = b_ref[:, cols].astype(F32)
            s = jax.nn.sigmoid(a)
            silu = a * s
            dhff = _nt(dhh, wd_ref[cols, :])
            da_ref[:, cols] = (dhff * b * (s * (1.0 + a * (1.0 - s)))).astype(BF16)
            db_ref[:, cols] = (dhff * silu).astype(BF16)
        dn = _nn(da_ref[...], wg_ref[...]) + _nn(db_ref[...], wu_ref[...])
        xhat, r = _rms(x_ref[...], g_ref[...])
        dx, dg = _rms_bwd(dn, xhat, r, g_ref[...])
        dx_ref[...] = dh_ref[...] + dx
        dg_ref[...] += dg

    row = pl.BlockSpec((tm, D_MODEL), lambda i: (i, 0))
    hid = pl.BlockSpec((tm, D_FF), lambda i: (i, 0))
    return pl.pallas_call(
        body, name=name, grid=(t // tm,),
        in_specs=[row, hid, hid, _resident(g), row, _resident(wg_t), _resident(wu_t), _resident(wd)],
        out_specs=[row, pl.BlockSpec((1, D_MODEL), lambda i: (0, 0)), hid, hid],
        out_shape=[jax.ShapeDtypeStruct((t, D_MODEL), F32), jax.ShapeDtypeStruct((1, D_MODEL), F32),
                   jax.ShapeDtypeStruct((t, D_FF), BF16), jax.ShapeDtypeStruct((t, D_FF), BF16)],
        compiler_params=_params("arbitrary"),
    )(x, a_pre, b_pre, g, dh, wg_t, wu_t, wd)


def _tn_matmul(a, b, rc, name, hosted=(), scale=None):
    t, r = a.shape
    c = b.shape[1]
    tk = min(t, 2048)

    def body(a_ref, b_ref, o_ref):
        @pl.when(pl.program_id(1) == 0)
        def _():
            o_ref[...] = jnp.zeros_like(o_ref)

        o_ref[...] += _tn(a_ref[...].astype(BF16), b_ref[...].astype(BF16))
        if scale is not None:
            @pl.when(pl.program_id(1) == t // tk - 1)
            def _():
                o_ref[...] *= scale

    (out,), got = _call(
        body, name=name, grid=(r // rc, t // tk),
        in_specs=[pl.BlockSpec((tk, rc), lambda i, k: (k, i)), pl.BlockSpec((tk, c), lambda i, k: (k, 0))],
        out_specs=[pl.BlockSpec((rc, c), lambda i, k: (i, 0))],
        out_shape=[jax.ShapeDtypeStruct((r, c), F32)],
        sem=("parallel", "arbitrary"), args=(a, b), hosted=hosted)
    return out, got


def _unfold(blk_ref, slab_ref, d):
    if d == 1:
        return blk_ref[...]
    n = blk_ref.shape[0]
    for r in range(d):
        for half in range(2):
            c0 = 256 * r + 128 * half
            slab_ref[half, pl.ds(r, n, stride=d), :] = blk_ref[:, c0:c0 + 128]
    return jnp.concatenate([slab_ref[0], slab_ref[1]], axis=1)


def _fold(x, slab_ref, out_ref, d):
    if d == 1:
        out_ref[...] = x.astype(out_ref.dtype)
        return
    n, w = out_ref.shape[0], x.shape[1]
    for part in range(w // 128):
        slab_ref[part] = x[:, 128 * part:128 * (part + 1)]
    for r in range(d):
        for part in range(w // 128):
            c0 = w * r + 128 * part
            out_ref[:, c0:c0 + 128] = slab_ref[part, pl.ds(r, n, stride=d), :].astype(out_ref.dtype)


DILATIONS = tuple(d for _, d in DIL_GROUPS)


PIECE_W = (256,) * 9 + (512, 128, 128, GATE_W)
PIECE_D = DILATIONS * 3 + (1, 1, 1, 1)


def _inproj_fwd(h, g, w_t, b_in, name, hosted=()):
    t = h.shape[0]
    tm, nc = 512, 256
    dilated = [(gi, d) for gi, d in enumerate(DILATIONS) if d > 1]

    def body(h_ref, g_ref, w_ref, b_ref, u_ref, zq_ref, zg_ref, *rest):
        zf_refs, slabs = rest[:len(dilated)], rest[len(dilated):]
        xhat, _ = _rms(h_ref[...], g_ref[...])
        u = (xhat * g_ref[...]).astype(BF16)
        u_ref[...] = u
        for c in range(D_IN // nc):
            z = _nt(u, w_ref[c * nc:(c + 1) * nc, :]) + b_ref[:, c * nc:(c + 1) * nc]
            if c < QKV_W // nc:
                zq_ref[:, c * nc:(c + 1) * nc] = z.astype(BF16)
            else:
                zg_ref[:, c * nc - QKV_W:(c + 1) * nc - QKV_W] = z.astype(BF16)
            part, gi = divmod(c, len(DILATIONS))
            for k, (gk, d) in enumerate(dilated):
                if part < 3 and gi == gk:
                    slab, n = slabs[3 * k + part], tm // d
                    slab[0] = z[:, :128]
                    slab[1] = z[:, 128:]
                    for r in range(d):
                        for half in range(2):
                            c0 = 768 * r + 256 * part + 128 * half
                            zf_refs[k][:, c0:c0 + 128] = slab[half, pl.ds(r, n, stride=d), :].astype(BF16)

    row = lambda w: pl.BlockSpec((tm, w), lambda i: (i, 0))
    full = _resident
    sds = jax.ShapeDtypeStruct
    return _call(
        body, name=name, grid=(t // tm,),
        in_specs=[row(D_MODEL), full(g), full(w_t), full(b_in)],
        out_specs=[row(D_MODEL), row(QKV_W), row(GATE_W)]
        + [pl.BlockSpec((tm // d, d * 768), lambda i: (i, 0)) for _, d in dilated],
        out_shape=[sds((t, D_MODEL), BF16), sds((t, QKV_W), BF16), sds((t, GATE_W), BF16)]
        + [sds((t // d, d * 768), BF16) for _, d in dilated],
        scratch=[pltpu.VMEM((2, tm, 128), F32)] * (3 * len(dilated)),
        sem=("parallel",), args=(h, g, w_t, b_in), hosted=hosted)


def _inproj_bwd(pieces, h, g, dh_res, w_t, name):
    t = h.shape[0]
    tm = 256
    npiece = len(PIECE_W)
    offs = np.concatenate([[0], np.cumsum(PIECE_W)]).tolist()

    def body(*refs):
        p_refs = refs[:npiece]
        h_ref, g_ref, dhr_ref, w_ref, dh_ref, dz_ref, db_ref, dg_ref = refs[npiece:npiece + 8]
        slabs = list(refs[npiece + 8:])
        i = pl.program_id(0)

        @pl.when(i == 0)
        def _():
            db_ref[...] = jnp.zeros_like(db_ref)
            dg_ref[...] = jnp.zeros_like(dg_ref)

        du = jnp.zeros((tm, D_MODEL), F32)
        for k in range(npiece):
            o, w = offs[k], PIECE_W[k]
            token_order = _unfold(p_refs[k], slabs.pop(), PIECE_D[k]).astype(BF16) if PIECE_D[k] > 1 else None
            for c0 in range(0, w, 512):
                cw = min(512, w - c0)
                pz = p_refs[k][:, c0:c0 + cw] if token_order is None else token_order
                dz_ref[:, o + c0:o + c0 + cw] = pz
                db_ref[:, o + c0:o + c0 + cw] += jnp.sum(pz.astype(F32), axis=0, keepdims=True)
                du = du + _nn(pz, w_ref[o + c0:o + c0 + cw, :])
        xhat, r = _rms(h_ref[...], g_ref[...])
        dx, dg = _rms_bwd(du, xhat, r, g_ref[...])
        dh_ref[...] = dhr_ref[...] + dx
        dg_ref[...] += dg

    row = lambda w: pl.BlockSpec((tm, w), lambda i: (i, 0))
    full = lambda shp: pl.BlockSpec(shp, lambda i: (0, 0))
    return pl.pallas_call(
        body, name=name, grid=(t // tm,),
        in_specs=[pl.BlockSpec((tm // d, d * w), lambda i: (i, 0)) for w, d in zip(PIECE_W, PIECE_D)]
        + [row(D_MODEL), _resident(g), row(D_MODEL), _resident(w_t)],
        out_specs=[row(D_MODEL), row(D_IN), full((1, D_IN)), full((1, D_MODEL))],
        out_shape=[jax.ShapeDtypeStruct((t, D_MODEL), F32), jax.ShapeDtypeStruct((t, D_IN), BF16),
                   jax.ShapeDtypeStruct((1, D_IN), F32), jax.ShapeDtypeStruct((1, D_MODEL), F32)],
        scratch_shapes=[pltpu.VMEM((2, tm, 128), F32)] * sum(d > 1 for d in PIECE_D),
        compiler_params=_params("arbitrary"),
    )(*pieces, h, g, dh_res, w_t)


def _t5_bucket(dist):
    max_exact = N_BUCKETS // 2
    n = jnp.maximum(dist, 0)
    nf = jnp.maximum(n, 1).astype(jnp.float32)
    large = max_exact + (jnp.log(nf / max_exact) / math.log(MAX_DISTANCE / max_exact)
                         * (N_BUCKETS - max_exact)).astype(jnp.int32)
    large = jnp.minimum(large, N_BUCKETS - 1)
    return jnp.where(n < max_exact, n, large)


ATT_CFG = ((1, 128, 0, 4), (4, 128, 4, 4), (16, 128, 8, 4), (1, B_WINDOW - 1, A_HEADS, 8))


def _bucket_tiles():
    qi = jnp.arange(BLOCK)[:, None]
    ki = jnp.arange(2 * BLOCK)[None, :]
    dist = qi + BLOCK - ki
    return jnp.stack([_t5_bucket(dist * cfg[0]) for cfg in ATT_CFG]).astype(jnp.int32)


def _band(max_steps):
    row = lax.broadcasted_iota(jnp.int32, (BLOCK, 2 * BLOCK), 0)
    col = lax.broadcasted_iota(jnp.int32, (BLOCK, 2 * BLOCK), 1)
    dist = row + BLOCK - col
    return (dist >= 0) & (dist <= max_steps)


def _bias_build(table, buckets, hosted=()):
    def body(tab_ref, bt_ref, out_ref):
        col = lax.broadcasted_iota(jnp.int32, (BLOCK, 2 * BLOCK), 1)
        for ci, (_, max_steps, h0, nh) in enumerate(ATT_CFG):
            bt = bt_ref[ci]
            band = _band(max_steps)
            for h in range(h0, h0 + nh):
                acc = lax.fori_loop(0, N_BUCKETS, lambda b, acc: jnp.where(bt == b, tab_ref[b, h], acc),
                                    jnp.zeros((BLOCK, 2 * BLOCK), F32))
                out_ref[0, BLOCK * h:BLOCK * (h + 1), :] = jnp.where(band & (col >= BLOCK), acc, NEG)
                out_ref[1, BLOCK * h:BLOCK * (h + 1), :] = jnp.where(band, acc, NEG)

    return _call(
        body, name="bias_build", grid=(1,),
        in_specs=[pl.BlockSpec(memory_space=pltpu.SMEM), pl.BlockSpec(memory_space=pltpu.VMEM)],
        out_specs=[pl.BlockSpec(memory_space=pltpu.VMEM)],
        out_shape=[jax.ShapeDtypeStruct((2, TOTAL_HEADS * BLOCK, 2 * BLOCK), F32)],
        sem=("arbitrary",), args=(table, buckets), hosted=hosted)


def _bias_reduce(dbias, buckets, dsink_rows, hosted=()):
    def body(db_ref, bt_ref, ds_ref, out_ref, sink_ref):
        ri = lax.broadcasted_iota(jnp.int32, (N_BUCKETS, 128), 0)
        ci = lax.broadcasted_iota(jnp.int32, (N_BUCKETS, 128), 1)

        def per_bucket(b, acc):
            for cfg_i, (_, _, h0, nh) in enumerate(ATT_CFG):
                hit = bt_ref[cfg_i] == b
                for h in range(h0, h0 + nh):
                    val = jnp.sum(jnp.where(hit, db_ref[BLOCK * h:BLOCK * (h + 1), :], 0.0))
                    acc = jnp.where((ri == b) & (ci == h), val, acc)
            return acc

        out_ref[...] = lax.fori_loop(0, N_BUCKETS, per_bucket, jnp.zeros((N_BUCKETS, 128), F32))
        for h in range(8):
            sink_ref[h:h + 1, :] = jnp.sum(ds_ref[BLOCK * h:BLOCK * (h + 1), :], axis=0, keepdims=True)

    return _call(
        body, name="bias_reduce", grid=(1,),
        in_specs=[pl.BlockSpec(memory_space=pltpu.VMEM)] * 3,
        out_specs=[pl.BlockSpec(memory_space=pltpu.VMEM)] * 2,
        out_shape=[jax.ShapeDtypeStruct((N_BUCKETS, 128), F32), jax.ShapeDtypeStruct((8, 128), F32)],
        sem=("arbitrary",), args=(dbias, buckets, dsink_rows), hosted=hosted)


class _Att:
    def __init__(self, cfg_i):
        stride, _, h0, nh = ATT_CFG[cfg_i]
        self.d = stride if cfg_i < 3 else 1
        self.h0, self.nh = h0, nh
        self.row_w = QKV_W if self.d == 1 else 3 * 256
        if cfg_i < 3:
            self.nq, self.wkv = 1, 256
            self.q_unit = [cfg_i if self.d == 1 else 0]
            self.k_unit, self.v_unit = (3 + cfg_i, 6 + cfg_i) if self.d == 1 else (1, 2)
            self.sinks = False
        else:
            self.nq, self.wkv = 2, 128
            self.q_unit = [9, 10]
            self.k_unit, self.v_unit = 22, 23
            self.sinks = True
        self.wq = 256 * self.nq


def _att_in_specs(cf, bsz):
    uq, ukv = cf.row_w // 256, cf.row_w // cf.wkv
    specs = [pl.BlockSpec((bsz, BLOCK, 256), functools.partial(lambda r, j, u: (0, j, r * uq + u), u=u))
             for u in cf.q_unit]
    for unit in (cf.k_unit, cf.v_unit):
        specs.append(pl.BlockSpec((bsz, BLOCK, cf.wkv),
                                  functools.partial(lambda r, j, u: (0, jnp.maximum(j - 1, 0), r * ukv + u), u=unit)))
        specs.append(pl.BlockSpec((bsz, BLOCK, cf.wkv),
                                  functools.partial(lambda r, j, u: (0, j, r * ukv + u), u=unit)))
    for qb in range(cf.nq):
        specs.append(pl.BlockSpec((None, HEADS_PER_BLOCK * BLOCK, 2 * BLOCK), functools.partial(
            lambda r, j, u: (jnp.minimum(j, 1), u, 0), u=cf.h0 // HEADS_PER_BLOCK + qb)))
    if cf.sinks:
        specs += [pl.BlockSpec((HEADS_PER_BLOCK * BLOCK, 128), functools.partial(lambda r, j, u: (u, 0), u=qb))
                  for qb in range(cf.nq)]
    return specs


HEADS_PER_BLOCK = 4


def _head_masks(rows):
    head = lax.broadcasted_iota(jnp.int32, (rows, 256), 1) // HEAD_DIM
    return [head == h for h in range(HEADS_PER_BLOCK)]


def _stack_heads(x, masks):
    return jnp.concatenate([jnp.where(m, x, jnp.zeros_like(x)) for m in masks], axis=0)


def _unstack_heads(x4, masks):
    blocks = [x4[BLOCK * h:BLOCK * (h + 1)] for h in range(HEADS_PER_BLOCK)]
    return jnp.where(masks[0], blocks[0], jnp.where(masks[1], blocks[1], jnp.where(masks[2], blocks[2], blocks[3])))


def _row_value(x):
    return jnp.max(x, axis=-1, keepdims=True)


def _kv_operands(cf, x):
    if cf.wkv == 256:
        return [x]
    lane = lax.broadcasted_iota(jnp.int32, x.shape, 1)
    swapped = pltpu.roll(x, HEAD_DIM, 1)
    halves = [jnp.where(lane < HEAD_DIM, x, swapped), jnp.where(lane < HEAD_DIM, swapped, x)]
    return [jnp.concatenate([half, half], axis=1) for half in halves]


def _kv_fold(cf, grads):
    if cf.wkv == 256:
        return grads[0]
    folded = []
    for g in grads:
        x = g[:, :128] + g[:, 128:]
        folded.append(x + pltpu.roll(x, HEAD_DIM, 1))
    lane = lax.broadcasted_iota(jnp.int32, folded[0].shape, 1)
    return jnp.where(lane < HEAD_DIM, folded[0], folded[1])


def _attn_fwd(cf, zf, bias, sinks, name, hosted=()):
    bsz, l, _ = zf.shape
    nb = l // BLOCK

    def body(*refs):
        refs = list(refs)
        q_refs = [refs.pop(0) for _ in range(cf.nq)]
        kp_ref, kc_ref, vp_ref, vc_ref = [refs.pop(0) for _ in range(4)]
        bias_refs = [refs.pop(0) for _ in range(cf.nq)]
        sink_refs = [refs.pop(0) for _ in range(cf.nq)] if cf.sinks else None
        o_ref, lse_ref = refs
        masks, kv_masks = _head_masks(BLOCK), _head_masks(2 * BLOCK)
        sinks4 = [_row_value(ref[...]) for ref in sink_refs] if cf.sinks else None
        for bi in range(bsz):
            k = jnp.concatenate([kp_ref[bi], kc_ref[bi]], axis=0)
            v = jnp.concatenate([vp_ref[bi], vc_ref[bi]], axis=0)
            k_ops, v_ops = _kv_operands(cf, k), _kv_operands(cf, v)
            for qb in range(cf.nq):
                cols = slice(256 * qb, 256 * (qb + 1))
                kb, vb = k_ops[qb], v_ops[qb]
                q4 = _stack_heads(q_refs[qb][bi] * (HEAD_DIM ** -0.5), masks)
                s = _nt(q4, kb) + bias_refs[qb][...]
                m = jnp.max(s, axis=-1, keepdims=True)
                if cf.sinks:
                    sk = sinks4[qb]
                    m = jnp.maximum(m, sk)
                p = jnp.exp(s - m)
                den = jnp.sum(p, axis=-1, keepdims=True)
                if cf.sinks:
                    den = den + jnp.exp(sk - m)
                pn = (p * (1.0 / den)).astype(BF16)
                p_lanes = jnp.concatenate([pn[BLOCK * h:BLOCK * (h + 1)] for h in range(HEADS_PER_BLOCK)], axis=1)
                v4 = jnp.concatenate([jnp.where(mk, vb, jnp.zeros_like(vb)) for mk in kv_masks], axis=0)
                o_ref[bi, :, cols] = _nn(p_lanes, v4)
                lse_ref[bi, :, cols] = _unstack_heads(
                    jnp.broadcast_to(m + jnp.log(den), (HEADS_PER_BLOCK * BLOCK, 256)), masks)

    in_specs = _att_in_specs(cf, bsz)
    args = [zf] * (cf.nq + 4) + [bias] * cf.nq + ([sinks] * cf.nq if cf.sinks else [])
    out = pl.BlockSpec((bsz, BLOCK, cf.wq), lambda r, j: (0, j, r))
    shape = jax.ShapeDtypeStruct((bsz, l, cf.d * cf.wq), F32)
    return _call(
        body, name=name, grid=(cf.d, nb), in_specs=in_specs, out_specs=[out, out], out_shape=[shape, shape],
        sem=("parallel", "arbitrary"), args=args, hosted=hosted)


def _attn_bwd(cf, zf, bias, sinks, stats, dy, name, hosted=()):
    bsz, l, _ = zf.shape
    nb = l // BLOCK

    def body(*refs):
        refs = list(refs)
        q_refs = [refs.pop(0) for _ in range(cf.nq)]
        kp_ref, kc_ref, vp_ref, vc_ref = [refs.pop(0) for _ in range(4)]
        bias_refs = [refs.pop(0) for _ in range(cf.nq)]
        sink_refs = [refs.pop(0) for _ in range(cf.nq)] if cf.sinks else None
        st_ref, dy_ref, dq_ref, dk_ref, dv_ref, dbias_ref = [refs.pop(0) for _ in range(6)]
        dsink_ref = refs.pop(0) if cf.sinks else None
        dk_acc, dv_acc = refs
        r, j = pl.program_id(0), pl.program_id(1)

        @pl.when((r == 0) & (j == 0))
        def _():
            dbias_ref[...] = jnp.zeros_like(dbias_ref)
            if cf.sinks:
                dsink_ref[...] = jnp.zeros_like(dsink_ref)

        @pl.when(j == 0)
        def _():
            dk_acc[...] = jnp.zeros_like(dk_acc)
            dv_acc[...] = jnp.zeros_like(dv_acc)

        masks = _head_masks(BLOCK)
        cur = pl.ds(pl.multiple_of(j * BLOCK, BLOCK), BLOCK)
        prev = pl.ds(pl.multiple_of(jnp.maximum(j - 1, 0) * BLOCK, BLOCK), BLOCK)
        sinks4 = [_row_value(ref[...]) for ref in sink_refs] if cf.sinks else None
        ds_sum, dsink_sum = [None] * cf.nq, [None] * cf.nq
        for bi in range(bsz):
            k = jnp.concatenate([kp_ref[bi], kc_ref[bi]], axis=0)
            v = jnp.concatenate([vp_ref[bi], vc_ref[bi]], axis=0)
            dk_blocks, dv_blocks = [], []
            k_ops, v_ops = _kv_operands(cf, k), _kv_operands(cf, v)
            for qb in range(cf.nq):
                cols = slice(256 * qb, 256 * (qb + 1))
                kb, vb = k_ops[qb], v_ops[qb]
                q4 = _stack_heads(q_refs[qb][bi] * (HEAD_DIM ** -0.5), masks)
                st = st_ref[bi, :, 128 * qb:128 * (qb + 1)]
                lt4, e4 = _read_stats(st, 0), _read_stats(st, 1)
                pa = jnp.exp(_nt(q4, kb) + bias_refs[qb][...] - lt4)
                dy = dy_ref[bi, :, cols]
                dy4 = _stack_heads(dy, masks)
                ds = pa * (_nt(dy4, vb) - e4)
                ds_sum[qb] = ds if ds_sum[qb] is None else ds_sum[qb] + ds
                if cf.sinks:
                    dsk = jnp.exp(sinks4[qb] - lt4) * e4
                    dsink_sum[qb] = dsk if dsink_sum[qb] is None else dsink_sum[qb] + dsk
                dsb = ds.astype(BF16)
                dq_ref[bi, :, cols] = (_unstack_heads(_nn(dsb, kb), masks) * (HEAD_DIM ** -0.5)).astype(dq_ref.dtype)
                dk_blocks.append(_tn(dsb, q4))
                dv_blocks.append(_tn(pa.astype(BF16), dy4))
            dk_new, dv_new = _kv_fold(cf, dk_blocks), _kv_fold(cf, dv_blocks)
            dk_acc[bi, cur, :] += dk_new[BLOCK:]
            dv_acc[bi, cur, :] += dv_new[BLOCK:]
            dk_acc[bi, prev, :] += dk_new[:BLOCK]
            dv_acc[bi, prev, :] += dv_new[:BLOCK]
        for qb in range(cf.nq):
            rows = slice(HEADS_PER_BLOCK * BLOCK * qb, HEADS_PER_BLOCK * BLOCK * (qb + 1))
            dbias_ref[rows, :] += ds_sum[qb]
            if cf.sinks:
                dsink_ref[rows, :] -= dsink_sum[qb]

        @pl.when(j == nb - 1)
        def _():
            dk_ref[...] = dk_acc[...].astype(dk_ref.dtype)
            dv_ref[...] = dv_acc[...].astype(dv_ref.dtype)

    tok = pl.BlockSpec((bsz, BLOCK, cf.wq), lambda r, j: (0, j, r))
    in_specs = _att_in_specs(cf, bsz) + [pl.BlockSpec((bsz, BLOCK, cf.wq // 2), lambda r, j: (0, j, r)), tok]
    args = [zf] * (cf.nq + 4) + [bias] * cf.nq + ([sinks] * cf.nq if cf.sinks else []) + [stats, dy]
    seq = pl.BlockSpec((bsz, l, cf.wkv), lambda r, j: (0, 0, r))
    out_specs = [tok, seq, seq, pl.BlockSpec((cf.nh * BLOCK, 2 * BLOCK), lambda r, j: (0, 0))]
    grad_dtype = BF16 if cf.d == 1 else F32
    out_shape = [jax.ShapeDtypeStruct((bsz, l, cf.d * cf.wq), grad_dtype),
                 jax.ShapeDtypeStruct((bsz, l, cf.d * cf.wkv), grad_dtype),
                 jax.ShapeDtypeStruct((bsz, l, cf.d * cf.wkv), grad_dtype),
                 jax.ShapeDtypeStruct((cf.nh * BLOCK, 2 * BLOCK), F32)]
    if cf.sinks:
        out_specs.append(pl.BlockSpec((cf.nh * BLOCK, 128), lambda r, j: (0, 0)))
        out_shape.append(jax.ShapeDtypeStruct((cf.nh * BLOCK, 128), F32))
    return _call(
        body, name=name, grid=(cf.d, nb), in_specs=in_specs, out_specs=out_specs, out_shape=out_shape,
        scratch=[pltpu.VMEM((bsz, l, cf.wkv), F32), pltpu.VMEM((bsz, l, cf.wkv), F32)],
        sem=("arbitrary", "arbitrary"), args=args, hosted=hosted)


def _merge_fwd(o_a, lse_a, o_b, zg, h, wa_t, wb_t, wout, name, hosted=()):
    t = h.shape[0]
    tm = 512

    def body(o1, o2, o3, l1, l2, l3, ob_ref, zg_ref, h_ref, wa_ref, wb_ref, wo_ref, h2_ref, ya_ref, lt_ref, *slabs):
        o = [_unfold(ref, slabs[i], d) for i, (ref, d) in enumerate(zip((o1, o2, o3), DILATIONS))]
        l = [_unfold(ref, slabs[3 + i], d) for i, (ref, d) in enumerate(zip((l1, l2, l3), DILATIONS))]
        m = jnp.maximum(jnp.maximum(l[0], l[1]), l[2])
        e1, e2, e3 = jnp.exp(l[0] - m), jnp.exp(l[1] - m), jnp.exp(l[2] - m)
        se = e1 + e2 + e3
        ya = (e1 / se) * o[0] + (e2 / se) * o[1] + (e3 / se) * o[2]
        ya_ref[...] = ya
        lt_ref[...] = m + jnp.log(se)
        pa = _nt(ya.astype(BF16), wa_ref[...])
        pb = _nt(ob_ref[...].astype(BF16), wb_ref[...])
        merged = (jax.nn.sigmoid(zg_ref[:, :D_MODEL].astype(F32)) * pa
                  + jax.nn.sigmoid(zg_ref[:, D_MODEL:].astype(F32)) * pb)
        h2_ref[...] = h_ref[...] + _nn(merged.astype(BF16), wo_ref[...])

    row = lambda w: pl.BlockSpec((tm, w), lambda i: (i, 0))
    folded = [pl.BlockSpec((tm // d, d * 256), lambda i: (i, 0)) for d in DILATIONS]
    full = _resident
    return _call(
        body, name=name, grid=(t // tm,),
        in_specs=folded + folded + [row(512), row(GATE_W), row(D_MODEL), full(wa_t), full(wb_t), full(wout)],
        out_specs=[row(D_MODEL), row(256), row(256)],
        out_shape=[jax.ShapeDtypeStruct((t, D_MODEL), F32), jax.ShapeDtypeStruct((t, 256), F32),
                   jax.ShapeDtypeStruct((t, 256), F32)],
        scratch=[pltpu.VMEM((2, tm, 128), F32)] * 6,
        sem=("parallel",), args=(*o_a, *lse_a, o_b, zg, h, wa_t, wb_t, wout), hosted=hosted)


STAT_LANES = 32


def _head_stats(lt, dy, y):
    rows, w = dy.shape
    lane = lax.broadcasted_iota(jnp.int32, (rows, w), 1)
    out_lane = lax.broadcasted_iota(jnp.int32, (rows, w // 2), 1)
    prod = dy * y
    out = jnp.zeros((rows, w // 2), F32)
    for hd in range(w // HEAD_DIM):
        mine = lane // HEAD_DIM == hd
        dot = jnp.sum(jnp.where(mine, prod, 0.0), axis=-1, keepdims=True)
        both = jnp.where(out_lane % STAT_LANES < STAT_LANES // 2, _row_value(jnp.where(mine, lt, NEG)), dot)
        out = jnp.where(out_lane // STAT_LANES == hd, both, out)
    return out


def _read_stats(st, part):
    lane = lax.broadcasted_iota(jnp.int32, st.shape, 1)
    half = (lane % STAT_LANES < STAT_LANES // 2) == (part == 0)
    return jnp.concatenate([_row_value(jnp.where((lane // STAT_LANES == hd) & half, st, NEG))
                            for hd in range(HEADS_PER_BLOCK)], axis=0)


def _merge_bwd(dh, ya, lse_tot, o_b, lse_b, zg, wa_t, wb_t, wout, name, hosted=()):
    t = dh.shape[0]
    tm = 512

    def body(dh_ref, ya_ref, lt_ref, ob_ref, lb_ref, zg_ref, wa_ref, wb_ref, wo_ref,
             mg_ref, dpa_ref, dpb_ref, dzg_ref, dy1, dy2, dy3, st1, st2, st3, dyb_ref, stb_ref, *slabs):
        dm = _nt(dh_ref[...].astype(BF16), wo_ref[...])
        pa = _nt(ya_ref[...].astype(BF16), wa_ref[...])
        pb = _nt(ob_ref[...].astype(BF16), wb_ref[...])
        sa = jax.nn.sigmoid(zg_ref[:, :D_MODEL].astype(F32))
        sb = jax.nn.sigmoid(zg_ref[:, D_MODEL:].astype(F32))
        mg_ref[...] = (sa * pa + sb * pb).astype(BF16)
        dpa = (dm * sa).astype(BF16)
        dpb = (dm * sb).astype(BF16)
        dpa_ref[...] = dpa
        dpb_ref[...] = dpb
        dzg_ref[:, :D_MODEL] = (dm * pa * (sa * (1.0 - sa))).astype(BF16)
        dzg_ref[:, D_MODEL:] = (dm * pb * (sb * (1.0 - sb))).astype(BF16)
        dya = _nn(dpa, wa_ref[...])
        dyb = _nn(dpb, wb_ref[...])
        stats = _head_stats(lt_ref[...], dya, ya_ref[...])
        for i, d in enumerate(DILATIONS):
            _fold(dya, slabs[i], (dy1, dy2, dy3)[i], d)
            _fold(stats, slabs[3 + i], (st1, st2, st3)[i], d)
        dyb_ref[...] = dyb.astype(BF16)
        stb_ref[...] = _head_stats(lb_ref[...], dyb, ob_ref[...])

    row = lambda w: pl.BlockSpec((tm, w), lambda i: (i, 0))
    folded = [pl.BlockSpec((tm // d, d * 256), lambda i: (i, 0)) for d in DILATIONS]
    full = _resident
    sds = jax.ShapeDtypeStruct
    dy_shape = [sds((t // d, d * 256), BF16) for d in DILATIONS]
    st_shape = [sds((t // d, d * 128), F32) for d in DILATIONS]
    st_specs = [pl.BlockSpec((tm // d, d * 128), lambda i: (i, 0)) for d in DILATIONS]
    return _call(
        body, name=name, grid=(t // tm,),
        in_specs=[row(D_MODEL), row(256), row(256), row(512), row(512), row(GATE_W), full(wa_t), full(wb_t),
                  full(wout)],
        out_specs=[row(D_MODEL), row(D_MODEL), row(D_MODEL), row(GATE_W)] + folded + st_specs + [row(512), row(256)],
        out_shape=[sds((t, D_MODEL), BF16), sds((t, D_MODEL), BF16), sds((t, D_MODEL), BF16),
                   sds((t, GATE_W), BF16)] + dy_shape + st_shape + [sds((t, 512), BF16), sds((t, 256), F32)],
        scratch=[pltpu.VMEM((2, tm, 128), F32)] * 6,
        sem=("parallel",), args=(dh, ya, lse_tot, o_b, lse_b, zg, wa_t, wb_t, wout), hosted=hosted)


def _pair_sum(grad, got, name):
    _, _, r, cdim = grad.shape
    core = lax.axis_index("c").astype(jnp.int32).reshape(1)

    def body(core_ref, g_ref, s_ref, o_ref):
        o_ref[...] = (g_ref[...] + s_ref[...]).astype(BF16)

    return pl.pallas_call(
        body, name=name,
        grid_spec=pltpu.PrefetchScalarGridSpec(
            num_scalar_prefetch=1, grid=(N_CHIP,),
            in_specs=[pl.BlockSpec((None, None, r, cdim), lambda q, core_ref: (q, core_ref[0], 0, 0)),
                      pl.BlockSpec((None, None, r, cdim), lambda q, core_ref: (q, 0, 0, 0))],
            out_specs=pl.BlockSpec((None, r, cdim), lambda q, core_ref: (q, 0, 0))),
        out_shape=jax.ShapeDtypeStruct((N_CHIP, r, cdim), BF16),
        compiler_params=_params("parallel"),
    )(core, grad, got)


def _adamw_math(w, g, m, v):
    m = ADAM_B1 * m + (1.0 - ADAM_B1) * g
    v = ADAM_B2 * v + (1.0 - ADAM_B2) * jnp.square(g)
    m_hat = m / (1.0 - ADAM_B1 ** ADAM_STEP)
    v_hat = v / (1.0 - ADAM_B2 ** ADAM_STEP)
    delta = -ADAM_LR * (m_hat / (jnp.sqrt(v_hat) + ADAM_EPS) + ADAM_WD * w)
    return delta, m, v


def _update(w, parts, m, v, transposed, name, hosted=()):
    r, c = parts.shape[1:]

    def body(w_ref, p_ref, m_ref, v_ref, g_ref, d_ref, m2_ref, v2_ref):
        def total(rows):
            return ((p_ref[0, rows].astype(F32) + p_ref[1, rows].astype(F32)) + p_ref[2, rows].astype(F32)) \
                + p_ref[3, rows].astype(F32)

        def update(at, g):
            g_ref[at] = g
            d_ref[at], m2_ref[at], v2_ref[at] = _adamw_math(w_ref[at], g, m_ref[at], v_ref[at])

        if not transposed:
            update((slice(None), slice(None)), total(slice(None)))
            return
        for r0 in range(0, r, 128):
            n = min(128, r - r0)
            gt = total(slice(r0, r0 + n))
            if n < 128:
                gt = jnp.concatenate([gt, jnp.zeros((128 - n, c), F32)], axis=0)
            update((slice(None), slice(r0, r0 + n)), gt.T[:, :n])

    sds = jax.ShapeDtypeStruct(w.shape, F32)
    vm = pl.BlockSpec(memory_space=pltpu.VMEM)
    return _call(body, name=name, grid=(1,), in_specs=[vm] * 4, out_specs=[vm] * 4, out_shape=[sds] * 4,
                 sem=("arbitrary",), args=(w, parts, m, v), hosted=hosted)


SMALL_ROWS = 80


def _small_update(g, w, m, v, name):
    def body(g_ref, w_ref, m_ref, v_ref, gs_ref, d_ref, m2_ref, v2_ref, got_ref, send_sems, recv_sems):
        x, y, c = _place()
        me = 4 * x + 2 * y + c
        got_ref[me] = g_ref[...]
        copies = []
        for k in range(1, N_DEV):
            peer = (x ^ (k >> 2), y ^ ((k >> 1) & 1), c ^ (k & 1))
            cp = pltpu.make_async_remote_copy(
                src_ref=g_ref, dst_ref=got_ref.at[me], send_sem=send_sems.at[k - 1], recv_sem=recv_sems.at[k - 1],
                device_id=peer, device_id_type=MESH)
            cp.start()
            copies.append(cp)
        for cp in copies:
            cp.wait()
        total = got_ref[0]
        for k in range(1, N_DEV):
            total = total + got_ref[k]
        gs_ref[...] = total
        d_ref[...], m2_ref[...], v2_ref[...] = _adamw_math(w_ref[...], total, m_ref[...], v_ref[...])

    sds = jax.ShapeDtypeStruct((SMALL_ROWS, 128), F32)
    vm = pl.BlockSpec(memory_space=pltpu.VMEM)
    return pl.pallas_call(
        body, name=name, in_specs=[vm] * 4, out_specs=[vm] * 4, out_shape=[sds] * 4,
        scratch_shapes=[pltpu.VMEM((N_DEV, SMALL_ROWS, 128), F32), pltpu.SemaphoreType.DMA((N_DEV - 1,)),
                        pltpu.SemaphoreType.DMA((N_DEV - 1,))],
    )(g, w, m, v)


def _pack_small(gains, b_in, rel_bias, sinks, last):
    rows = [a.reshape(8, 128) for a in gains] + [b_in.reshape(40, 128), rel_bias.reshape(5, 128),
                                                 jnp.pad(sinks.reshape(1, 8), ((0, 0), (0, 120))), last]
    rows.append(jnp.zeros((SMALL_ROWS - 79, 128), F32))
    return jnp.concatenate(rows, axis=0)


def _unpack_small(p, like):
    out = [p[8 * i:8 * i + 8].reshape(like[i].shape) for i in range(4)]
    out.append(p[32:72].reshape(like[4].shape))
    out.append(p[72:77].reshape(like[5].shape))
    out.append(p[77, :8].reshape(like[6].shape))
    return out


def kernel(x, ffn1_norm, ffn1_w_gate, ffn1_w_up, ffn1_w_down, mix_norm, w_in, b_in, w_branch_a, w_branch_b, w_out, sinks, rel_bias, ffn2_norm, ffn2_w_gate, ffn2_w_up, ffn2_w_down, final_norm, loss_target, m_ffn1_norm, m_ffn1_w_gate, m_ffn1_w_up, m_ffn1_w_down, m_mix_norm, m_w_in, m_b_in, m_w_branch_a, m_w_branch_b, m_w_out, m_sinks, m_rel_bias, m_ffn2_norm, m_ffn2_w_gate, m_ffn2_w_up, m_ffn2_w_down, m_final_norm, v_ffn1_norm, v_ffn1_w_gate, v_ffn1_w_up, v_ffn1_w_down, v_mix_norm, v_w_in, v_b_in, v_w_branch_a, v_w_branch_b, v_w_out, v_sinks, v_rel_bias, v_ffn2_norm, v_ffn2_w_gate, v_ffn2_w_up, v_ffn2_w_down, v_final_norm):
    bsz, seq, _ = x.shape
    t = bsz * seq
    xt = x.reshape(t, D_MODEL)
    target = loss_target.reshape(t, D_MODEL)

    big = [("ffn1_w_gate", ffn1_w_gate, m_ffn1_w_gate, v_ffn1_w_gate, True),
           ("ffn1_w_up", ffn1_w_up, m_ffn1_w_up, v_ffn1_w_up, True),
           ("ffn1_w_down", ffn1_w_down, m_ffn1_w_down, v_ffn1_w_down, False),
           ("w_in", w_in, m_w_in, v_w_in, True),
           ("w_branch_a", w_branch_a, m_w_branch_a, v_w_branch_a, True),
           ("w_branch_b", w_branch_b, m_w_branch_b, v_w_branch_b, True),
           ("w_out", w_out, m_w_out, v_w_out, False),
           ("ffn2_w_gate", ffn2_w_gate, m_ffn2_w_gate, v_ffn2_w_gate, True),
           ("ffn2_w_up", ffn2_w_up, m_ffn2_w_up, v_ffn2_w_up, True),
           ("ffn2_w_down", ffn2_w_down, m_ffn2_w_down, v_ffn2_w_down, False)]
    shard = {nm: (w[0].T if tr else w[0]).astype(BF16) for nm, w, _, _, tr in big}
    full = {}

    def gather(names):
        return _Gather([shard[nm] for nm in names])

    def keep(names, got):
        for nm, gw in zip(names, got):
            full[nm] = gw.reshape(-1, gw.shape[-1])

    ffn1_names = ["ffn1_w_gate", "ffn1_w_up", "ffn1_w_down"]
    mix_names = ["w_in", "w_branch_a", "w_branch_b", "w_out"]
    ffn2_names = ["ffn2_w_gate", "ffn2_w_up", "ffn2_w_down"]
    g1, gm, g2, gf = ffn1_norm, mix_norm, ffn2_norm, final_norm.reshape(1, D_MODEL)

    buckets = _bucket_tiles()
    (bias,), (got,) = _bias_build(rel_bias, buckets, hosted=[gather(ffn1_names)])
    keep(ffn1_names, got)
    (h1, n1, a1, b1, hff1), (got,) = _ffn_fwd(xt, g1, full["ffn1_w_gate"], full["ffn1_w_up"], full["ffn1_w_down"], "ffn1_fwd",
                                hosted=[gather(mix_names + ffn2_names)])
    keep(mix_names + ffn2_names, got)
    (u, zq, zg, *zdil), _ = _inproj_fwd(h1, gm, full["w_in"], b_in, "inproj_fwd")
    sink_rows = jnp.broadcast_to(sinks.reshape(8, 1, 1), (8, BLOCK, 128)).reshape(8 * BLOCK, 128)
    cfs = [_Att(i) for i in range(4)]
    zfold = []
    for i, cf in enumerate(cfs):
        if cf.d == 1:
            zfold.append(zq.reshape(bsz, seq, QKV_W))
        else:
            zfold.append(zdil[i - 1].reshape(bsz, seq // cf.d, cf.d * cf.row_w))
    att = [None] * 4
    for i in (3, 0, 1, 2):
        cf = cfs[i]
        (o, lse), _ = _attn_fwd(cf, zfold[i], bias, sink_rows, f"attn{i}_fwd")
        att[i] = (o.reshape(t // cf.d, cf.d * cf.wq), lse.reshape(t // cf.d, cf.d * cf.wq))
    o_b, lse_b = att[3]
    (h2, ya, lse_tot), _ = _merge_fwd([a[0] for a in att[:3]], [a[1] for a in att[:3]], o_b, zg, h1,
                                      full["w_branch_a"], full["w_branch_b"], full["w_out"], "merge_fwd")
    (dh3, n2, a2, b2, hff2, loss_part, dgf), _ = _ffn_fwd(
        h2, g2, full["ffn2_w_gate"], full["ffn2_w_up"], full["ffn2_w_down"], "ffn2_fwd", head=(gf, target))

    grads, pair, from_chips = {}, {}, {}

    def by_owner(nm):
        return grads[nm].reshape(N_CHIP, 2, -1, grads[nm].shape[-1])

    def to_core(names):
        return _CoreExchange([by_owner(nm) for nm in names])

    def pair_up(names, got):
        for nm, sib in zip(names, got):
            pair[nm] = _pair_sum(by_owner(nm), sib, f"pair_sum_{nm}")

    def to_chips(names):
        return _ChipExchange([pair[nm] for nm in names])

    def landed(names, got):
        for nm, parts in zip(names, got):
            from_chips[nm] = parts

    dh2, dg2, da, db = _ffn_bwd(h2, a2, b2, g2, dh3, full["ffn2_w_gate"], full["ffn2_w_up"],
                                          full["ffn2_w_down"], "ffn2_bwd")
    grads["ffn2_w_gate"], _ = _tn_matmul(da, n2, 1408, "ffn2_dgate")
    grads["ffn2_w_up"], _ = _tn_matmul(db, n2, 1408, "ffn2_dup")
    grads["ffn2_w_down"], _ = _tn_matmul(hff2, dh3, 1408, "ffn2_ddown", scale=0.5)
    (merged, dpa, dpb, dzg, *cot), (got,) = _merge_bwd(
        dh2, ya, lse_tot, o_b, lse_b, zg, full["w_branch_a"], full["w_branch_b"], full["w_out"], "merge_bwd",
        hosted=[to_core(ffn2_names)])
    dys, sts = cot[0:3] + [cot[6]], cot[3:6] + [cot[7]]
    pair_up(ffn2_names, got)
    dq, dk, dv, dbias, dsink = [None] * 4, [None] * 4, [None] * 4, [None] * 4, None
    for i in (3, 0, 1, 2):
        cf = cfs[i]
        shp = (bsz, seq // cf.d, cf.d * cf.wq)
        hosted = [to_chips(ffn2_names)] if i == 3 else ()
        res, got = _attn_bwd(cf, zfold[i], bias, sink_rows, sts[i].reshape(shp[:2] + (shp[2] // 2,)), dys[i].reshape(shp),
                             f"attn{i}_bwd", hosted=hosted)
        if i == 3:
            landed(ffn2_names, got[0])
        dq[i] = res[0].reshape(t // cf.d, cf.d * cf.wq)
        dk[i] = res[1].reshape(t // cf.d, cf.d * cf.wkv)
        dv[i] = res[2].reshape(t // cf.d, cf.d * cf.wkv)
        dbias[i] = res[3]
        if cf.sinks:
            dsink = res[4]
    pieces = dq[:3] + dk[:3] + dv[:3] + [dq[3], dk[3], dv[3], dzg]
    dh1, dz, db_in, dgm = _inproj_bwd(pieces, h1, gm, dh2, full["w_in"], "inproj_bwd")
    dx, dg1, da, db = _ffn_bwd(xt, a1, b1, g1, dh1, full["ffn1_w_gate"], full["ffn1_w_up"],
                                         full["ffn1_w_down"], "ffn1_bwd")
    grads["w_in"], _ = _tn_matmul(dz, u, 1280, "dw_in")
    grads["ffn1_w_down"], (got,) = _tn_matmul(hff1, dh1, 1408, "ffn1_ddown", hosted=[to_core(["w_in"])], scale=0.5)
    pair_up(["w_in"], got)
    grads["ffn1_w_gate"], got = _tn_matmul(da, n1, 1408, "ffn1_dgate",
                                           hosted=[to_chips(["w_in"]), to_core(["ffn1_w_down"])])
    landed(["w_in"], got[0])
    pair_up(["ffn1_w_down"], got[1])
    grads["ffn1_w_up"], got = _tn_matmul(db, n1, 1408, "ffn1_dup",
                                         hosted=[to_chips(["ffn1_w_down"]), to_core(["ffn1_w_gate"])])
    landed(["ffn1_w_down"], got[0])
    pair_up(["ffn1_w_gate"], got[1])
    grads["w_out"], got = _tn_matmul(merged, dh2, 1024, "dw_out",
                                     hosted=[to_chips(["ffn1_w_gate"]), to_core(["ffn1_w_up"])])
    landed(["ffn1_w_gate"], got[0])
    pair_up(["ffn1_w_up"], got[1])
    grads["w_branch_b"], got = _tn_matmul(dpb, o_b, 1024, "dw_branch_b", hosted=[to_chips(["ffn1_w_up"])])
    landed(["ffn1_w_up"], got[0])
    grads["w_branch_a"], _ = _tn_matmul(dpa, ya, 1024, "dw_branch_a")
    last_names = ["w_out", "w_branch_b", "w_branch_a"]
    (dtable, dsinks), got = _bias_reduce(jnp.concatenate(dbias, axis=0), buckets, dsink,
                                         hosted=[to_core(last_names)])
    pair_up(last_names, got[0])

    out_g, out_d, out_m, out_v = {}, {}, {}, {}
    for idx, (nm, w, m, v, tr) in enumerate(big):
        around = tr and w.shape[-1] % 128 != 0
        wmv = [a[0].T if around else a[0] for a in (w, m, v)]
        res, got = _update(wmv[0], from_chips[nm], wmv[1], wmv[2], tr and not around, f"update_{nm}",
                           hosted=[to_chips(last_names)] if idx == 0 else ())
        if idx == 0:
            landed(last_names, got[0])
        out_g[nm], out_d[nm], out_m[nm], out_v[nm] = [(a.T if around else a)[None] for a in res]

    small = [("ffn1_norm", ffn1_norm, m_ffn1_norm, v_ffn1_norm), ("mix_norm", mix_norm, m_mix_norm, v_mix_norm),
             ("ffn2_norm", ffn2_norm, m_ffn2_norm, v_ffn2_norm), ("final_norm", final_norm, m_final_norm, v_final_norm),
             ("b_in", b_in, m_b_in, v_b_in), ("rel_bias", rel_bias, m_rel_bias, v_rel_bias),
             ("sinks", sinks, m_sinks, v_sinks)]
    zero_row = jnp.zeros((1, 128), F32)
    pack = lambda arrs, last: _pack_small(arrs[:4], arrs[4], arrs[5], arrs[6], last)
    g_small = pack([dg1, dgm, dg2, dgf, db_in, dtable[:, :TOTAL_HEADS], dsinks[:, 0]], loss_part)
    packed = [pack([s[k] for s in small], zero_row) for k in (1, 2, 3)]
    gs, ds, ms, vs = _small_update(g_small, *packed, "small_update")
    like = [s[1] for s in small]
    for nm_s, g_, d_, m_, v_ in zip([s[0] for s in small], _unpack_small(gs, like), _unpack_small(ds, like),
                                    _unpack_small(ms, like), _unpack_small(vs, like)):
        out_g[nm_s], out_d[nm_s], out_m[nm_s], out_v[nm_s] = g_, d_, m_, v_
    loss = gs[78, 0]

    order = ["ffn1_norm", "ffn1_w_gate", "ffn1_w_up", "ffn1_w_down", "mix_norm", "w_in", "b_in", "w_branch_a",
             "w_branch_b", "w_out", "sinks", "rel_bias", "ffn2_norm", "ffn2_w_gate", "ffn2_w_up", "ffn2_w_down",
             "final_norm"]
    return (loss, dx.reshape(x.shape), *[out_g[k] for k in order], *[out_d[k] for k in order],
            *[out_m[k] for k in order], *[out_v[k] for k in order])
```

```python
import functools
import math

import numpy as np
import jax
import jax.numpy as jnp
from jax import lax
from jax.experimental import pallas as pl
from jax.experimental.pallas import tpu as pltpu

D_MODEL = 1024
D_FF = 2816
FF_CHUNK = 256
HEAD_DIM = 64
BLOCK = 128
N_BUCKETS = 32
MAX_DISTANCE = 2048
A_HEADS = 12
TOTAL_HEADS = 20
DIL_GROUPS = ((128, 1), (512, 4), (2048, 16))
B_WINDOW = 128
QKV_W = 3072
GATE_W = 2048
D_IN = QKV_W + GATE_W
EPS = 1e-6
NEG = -1e30
N_DEV = 8
N_CHIP = 4
ADAM_LR, ADAM_B1, ADAM_B2, ADAM_EPS, ADAM_WD, ADAM_STEP = 0.001, 0.9, 0.999, 1e-08, 0.01, 10
VMEM_LIMIT = 56 * 1024 * 1024
MESH = pl.DeviceIdType.MESH
BF16 = jnp.bfloat16
F32 = jnp.float32
ANY = pl.BlockSpec(memory_space=pl.ANY)


def _params(*sem):
    return pltpu.CompilerParams(dimension_semantics=sem, vmem_limit_bytes=VMEM_LIMIT)


def _resident(a):
    return pl.BlockSpec(a.shape, lambda i: (0, 0), pipeline_mode=pl.Buffered(1))


def _place():
    return lax.axis_index("x"), lax.axis_index("y"), lax.axis_index("c")


class _Gather:
    def __init__(self, shards):
        self.ins = list(shards)
        n = self.n = len(shards)
        self.out_shape = [jax.ShapeDtypeStruct((N_DEV,) + s.shape, s.dtype) for s in shards]
        self.scratch = [pltpu.SemaphoreType.DMA((7 * n,)), pltpu.SemaphoreType.DMA((7 * n,)),
                        pltpu.SemaphoreType.DMA((n,))]

    def _copies(self, ins, outs, sems):
        send_sems, recv_sems, local_sems = sems
        x, y, c = _place()
        me, sibling = (x, y, c), (x, y, 1 - c)
        chips = [(1 - x, y), (x, 1 - y), (1 - x, 1 - y)]

        def copy(i, k, block, to, src=None):
            dst = outs[i].at[4 * block[0] + 2 * block[1] + block[2]]
            return pltpu.make_async_remote_copy(
                src_ref=dst if src is None else src, dst_ref=dst, send_sem=send_sems.at[7 * i + k],
                recv_sem=recv_sems.at[7 * i + k], device_id=to, device_id_type=MESH)

        n = self.n
        south = c == 0
        relayed = (jnp.where(south, 1 - x, x), jnp.where(south, y, 1 - y), c)
        relay_to = (jnp.where(south, x, 1 - x), jnp.where(south, 1 - y, y), c)
        mine = [pltpu.make_async_copy(ins[i], outs[i].at[4 * x + 2 * y + c], local_sems.at[i]) for i in range(n)]
        first = [copy(i, 0, me, sibling, src=ins[i]) for i in range(n)]
        first += [copy(i, 1 + j, me, (*chips[j], c), src=ins[i]) for i in range(n) for j in range(2)]
        landed = [copy(i, 1 + j, (*chips[j], c), me) for j in range(2) for i in range(n)]
        passed = [copy(i, 4 + j, (*chips[j], c), sibling) for j in range(2) for i in range(n)]
        relays = [copy(i, 3, relayed, relay_to) for i in range(n)]
        diag_landed = [copy(i, 3, (*chips[2], c), me) for i in range(n)]
        diag_passed = [copy(i, 6, (*chips[2], c), sibling) for i in range(n)]
        from_sibling = [copy(i, 0, sibling, me) for i in range(n)]
        from_sibling += [copy(i, 4 + j, (*chip, 1 - c), me) for i in range(n) for j, chip in enumerate(chips)]
        return mine, first, landed, passed, relays, diag_landed, diag_passed, from_sibling

    def start(self, ins, outs, sems):
        mine, first = self._copies(ins, outs, sems)[:2]
        for cp in mine + first:
            cp.start()

    def mid(self, ins, outs, sems):
        _, _, landed, passed, relays, _, _, _ = self._copies(ins, outs, sems)
        for got in landed:
            got.wait_recv()
        for cp in relays + passed:
            cp.start()

    def end(self, ins, outs, sems):
        mine, first, _, passed, relays, diag_landed, diag_passed, from_sibling = self._copies(ins, outs, sems)
        for got, fwd in zip(diag_landed, diag_passed):
            got.wait_recv()
            fwd.start()
        for cp in from_sibling:
            cp.wait_recv()
        for cp in first + passed + relays + diag_passed:
            cp.wait_send()
        for cp in mine:
            cp.wait()


class _CoreExchange:
    def __init__(self, grads):
        self.ins = list(grads)
        n = self.n = len(grads)
        self.out_shape = [jax.ShapeDtypeStruct((N_CHIP, 1) + g.shape[2:], g.dtype) for g in grads]
        self.scratch = [pltpu.SemaphoreType.DMA((n,)), pltpu.SemaphoreType.DMA((n,))]

    def _copies(self, ins, outs, sems):
        x, y, c = _place()
        return [pltpu.make_async_remote_copy(
            src_ref=ins[i].at[:, pl.ds(1 - c, 1)], dst_ref=outs[i], send_sem=sems[0].at[i],
            recv_sem=sems[1].at[i], device_id=(x, y, 1 - c), device_id_type=MESH) for i in range(self.n)]

    def start(self, ins, outs, sems):
        for cp in self._copies(ins, outs, sems):
            cp.start()

    mid = None

    def end(self, ins, outs, sems):
        for cp in self._copies(ins, outs, sems):
            cp.wait()


class _ChipExchange:
    def __init__(self, parts):
        self.ins = list(parts)
        n = self.n = len(parts)
        self.out_shape = [jax.ShapeDtypeStruct(p.shape, p.dtype) for p in parts]
        self.scratch = [pltpu.SemaphoreType.DMA((3 * n,)), pltpu.SemaphoreType.DMA((3 * n,)),
                        pltpu.SemaphoreType.DMA((n,))]

    def _copies(self, ins, outs, sems):
        send_sems, recv_sems, local_sems = sems
        x, y, c = _place()
        my_chip = 2 * x + y
        copies = []
        for i in range(self.n):
            copies.append(pltpu.make_async_copy(ins[i].at[my_chip], outs[i].at[my_chip], local_sems.at[i]))
            for k, (qx, qy) in enumerate([(1 - x, y), (x, 1 - y), (1 - x, 1 - y)]):
                copies.append(pltpu.make_async_remote_copy(
                    src_ref=ins[i].at[2 * qx + qy], dst_ref=outs[i].at[my_chip], send_sem=send_sems.at[3 * i + k],
                    recv_sem=recv_sems.at[3 * i + k], device_id=(qx, qy, c), device_id_type=MESH))
        return copies

    def start(self, ins, outs, sems):
        for cp in self._copies(ins, outs, sems):
            cp.start()

    mid = None

    def end(self, ins, outs, sems):
        for cp in self._copies(ins, outs, sems):
            cp.wait()


def _call(body, *, name, grid, in_specs, out_specs, out_shape, args, scratch=(), sem=None, hosted=()):
    n_in, n_out, n_scr = len(in_specs), len(out_specs), len(scratch)
    x_in = [len(p.ins) for p in hosted]
    x_scr = [len(p.scratch) for p in hosted]
    steps = int(np.prod(grid))

    def wrapped(*refs):
        refs = list(refs)
        ins, refs = refs[:n_in], refs[n_in:]
        x_ins = [[refs.pop(0) for _ in range(k)] for k in x_in]
        outs, refs = refs[:n_out], refs[n_out:]
        x_outs = [[refs.pop(0) for _ in range(k)] for k in x_in]
        scr, refs = refs[:n_scr], refs[n_scr:]
        x_sems = [[refs.pop(0) for _ in range(k)] for k in x_scr]
        step = 0
        for d in range(len(grid)):
            step = step * grid[d] + pl.program_id(d)

        def phase(which, at):
            fns = [(getattr(p, which), a) for p, a in zip(hosted, zip(x_ins, x_outs, x_sems)) if getattr(p, which)]
            if fns:
                @pl.when(step == at)
                def _():
                    for fn, a in fns:
                        fn(*a)

        phase("start", 0)
        if steps > 1:
            phase("mid", (5 * steps) // 8)
        body(*ins, *outs, *scr)
        if steps == 1:
            phase("mid", 0)
        phase("end", steps - 1)

    results = pl.pallas_call(
        wrapped, name=name, grid=grid,
        in_specs=list(in_specs) + [ANY] * sum(x_in), out_specs=list(out_specs) + [ANY] * sum(x_in),
        out_shape=list(out_shape) + [s for p in hosted for s in p.out_shape],
        scratch_shapes=list(scratch) + [s for p in hosted for s in p.scratch],
        compiler_params=_params(*(("arbitrary",) * len(grid) if hosted else sem)),
    )(*args, *[a for p in hosted for a in p.ins])
    own, rest = list(results[:n_out]), list(results[n_out:])
    return own, [[rest.pop(0) for _ in range(k)] for k in x_in]


def _nt(a, b):
    return lax.dot_general(a, b, (((1,), (1,)), ((), ())), preferred_element_type=F32)


def _nn(a, b):
    return lax.dot_general(a, b, (((1,), (0,)), ((), ())), preferred_element_type=F32)


def _tn(a, b):
    return lax.dot_general(a, b, (((0,), (0,)), ((), ())), preferred_element_type=F32)


def _rms(x, g):
    r = lax.rsqrt(jnp.mean(x * x, axis=-1, keepdims=True) + EPS)
    return x * r, r


def _rms_bwd(dn, xhat, r, g):
    dg = jnp.sum(dn * xhat, axis=0, keepdims=True)
    dxh = dn * g
    dx = r * (dxh - xhat * jnp.mean(dxh * xhat, axis=-1, keepdims=True))
    return dx, dg


def _ffn_fwd(x, g, wg_t, wu_t, wd, name, hosted=(), head=None):
    t = x.shape[0]
    tm = 512

    def body(*refs):
        x_ref, g_ref, wg_ref, wu_ref, wd_ref = refs[:5]
        h_ref, n_ref, a_ref, b_ref, hff_ref = refs[-5 if head is None else -7:][:5]
        xhat, _ = _rms(x_ref[...], g_ref[...])
        n = (xhat * g_ref[...]).astype(BF16)
        n_ref[...] = n
        for c in range(0, D_FF, FF_CHUNK):
            cols = slice(c, c + FF_CHUNK)
            a = _nt(n, wg_ref[cols, :])
            b = _nt(n, wu_ref[cols, :])
            a_ref[:, cols] = a.astype(BF16)
            b_ref[:, cols] = b.astype(BF16)
            hff_ref[:, cols] = (a * jax.nn.sigmoid(a) * b).astype(BF16)
        h = x_ref[...] + 0.5 * _nn(hff_ref[...], wd_ref[...])
        if head is None:
            h_ref[...] = h
            return
        gf_ref, t_ref, loss_ref, dgf_ref = refs[5], refs[6], refs[-2], refs[-1]

        @pl.when(pl.program_id(0) == 0)
        def _():
            loss_ref[...] = jnp.zeros_like(loss_ref)
            dgf_ref[...] = jnp.zeros_like(dgf_ref)

        yhat, r = _rms(h, gf_ref[...])
        err = yhat * gf_ref[...] - t_ref[...]
        loss_ref[...] += 0.5 * jnp.sum(jnp.mean(err * err, axis=-1, keepdims=True), axis=0, keepdims=True)
        h_ref[...], dgf = _rms_bwd(err * (1.0 / D_MODEL), yhat, r, gf_ref[...])
        dgf_ref[...] += dgf

    row = pl.BlockSpec((tm, D_MODEL), lambda i: (i, 0))
    hid = pl.BlockSpec((tm, D_FF), lambda i: (i, 0))
    sds = jax.ShapeDtypeStruct
    in_specs = [row, _resident(g), _resident(wg_t), _resident(wu_t), _resident(wd)]
    out_specs = [row, row, hid, hid, hid]
    out_shape = [sds((t, D_MODEL), F32), sds((t, D_MODEL), BF16), sds((t, D_FF), BF16), sds((t, D_FF), BF16),
                 sds((t, D_FF), BF16)]
    args = (x, g, wg_t, wu_t, wd)
    if head is not None:
        in_specs += [_resident(head[0]), row]
        out_specs += [pl.BlockSpec((1, 128), lambda i: (0, 0)), pl.BlockSpec((1, D_MODEL), lambda i: (0, 0))]
        out_shape += [sds((1, 128), F32), sds((1, D_MODEL), F32)]
        args += tuple(head)
    return _call(body, name=name, grid=(t // tm,), in_specs=in_specs, out_specs=out_specs, out_shape=out_shape,
                 sem=("parallel",) if head is None else ("arbitrary",), args=args, hosted=hosted)


def _ffn_bwd(x, a_pre, b_pre, g, dh, wg_t, wu_t, wd, name):
    t = x.shape[0]
    tm = 512

    def body(x_ref, a_ref, b_ref, g_ref, dh_ref, wg_ref, wu_ref, wd_ref,
             dx_ref, dg_ref, da_ref, db_ref):
        @pl.when(pl.program_id(0) == 0)
        def _():
            dg_ref[...] = jnp.zeros_like(dg_ref)

        dhh = (0.5 * dh_ref[...]).astype(BF16)
        for c in range(0, D_FF, FF_CHUNK):
            cols = slice(c, c + FF_CHUNK)
            a = a_ref[:, cols].astype(F32)
            b = b_ref[:, cols].astype(F32)
            s = jax.nn.sigmoid(a)
            silu = a * s
            dhff = _nt(dhh, wd_ref[cols, :])
            da_ref[:, cols] = (dhff * b * (s * (1.0 + a * (1.0 - s)))).astype(BF16)
            db_ref[:, cols] = (dhff * silu).astype(BF16)
        dn = _nn(da_ref[...], wg_ref[...]) + _nn(db_ref[...], wu_ref[...])
        xhat, r = _rms(x_ref[...], g_ref[...])
        dx, dg = _rms_bwd(dn, xhat, r, g_ref[...])
        dx_ref[...] = dh_ref[...] + dx
        dg_ref[...] += dg

    row = pl.BlockSpec((tm, D_MODEL), lambda i: (i, 0))
    hid = pl.BlockSpec((tm, D_FF), lambda i: (i, 0))
    return pl.pallas_call(
        body, name=name, grid=(t // tm,),
        in_specs=[row, hid, hid, _resident(g), row, _resident(wg_t), _resident(wu_t), _resident(wd)],
        out_specs=[row, pl.BlockSpec((1, D_MODEL), lambda i: (0, 0)), hid, hid],
        out_shape=[jax.ShapeDtypeStruct((t, D_MODEL), F32), jax.ShapeDtypeStruct((1, D_MODEL), F32),
                   jax.ShapeDtypeStruct((t, D_FF), BF16), jax.ShapeDtypeStruct((t, D_FF), BF16)],
        compiler_params=_params("arbitrary"),
    )(x, a_pre, b_pre, g, dh, wg_t, wu_t, wd)


def _tn_matmul(a, b, rc, name, hosted=(), scale=None):
    t, r = a.shape
    c = b.shape[1]
    tk = min(t, 2048)

    def body(a_ref, b_ref, o_ref):
        @pl.when(pl.program_id(1) == 0)
        def _():
            o_ref[...] = jnp.zeros_like(o_ref)

        o_ref[...] += _tn(a_ref[...].astype(BF16), b_ref[...].astype(BF16))
        if scale is not None:
            @pl.when(pl.program_id(1) == t // tk - 1)
            def _():
                o_ref[...] *= scale

    (out,), got = _call(
        body, name=name, grid=(r // rc, t // tk),
        in_specs=[pl.BlockSpec((tk, rc), lambda i, k: (k, i)), pl.BlockSpec((tk, c), lambda i, k: (k, 0))],
        out_specs=[pl.BlockSpec((rc, c), lambda i, k: (i, 0))],
        out_shape=[jax.ShapeDtypeStruct((r, c), F32)],
        sem=("parallel", "arbitrary"), args=(a, b), hosted=hosted)
    return out, got


def _unfold(blk_ref, slab_ref, d):
    if d == 1:
        return blk_ref[...]
    n = blk_ref.shape[0]
    for r in range(d):
        for half in range(2):
            c0 = 256 * r + 128 * half
            slab_ref[half, pl.ds(r, n, stride=d), :] = blk_ref[:, c0:c0 + 128]
    return jnp.concatenate([slab_ref[0], slab_ref[1]], axis=1)


def _fold(x, slab_ref, out_ref, d):
    if d == 1:
        out_ref[...] = x.astype(out_ref.dtype)
        return
    n, w = out_ref.shape[0], x.shape[1]
    for part in range(w // 128):
        slab_ref[part] = x[:, 128 * part:128 * (part + 1)]
    for r in range(d):
        for part in range(w // 128):
            c0 = w * r + 128 * part
            out_ref[:, c0:c0 + 128] = slab_ref[part, pl.ds(r, n, stride=d), :].astype(out_ref.dtype)


DILATIONS = tuple(d for _, d in DIL_GROUPS)


PIECE_W = (256,) * 9 + (512, 128, 128, GATE_W)
PIECE_D = DILATIONS * 3 + (1, 1, 1, 1)


def _inproj_fwd(h, g, w_t, b_in, name, hosted=()):
    t = h.shape[0]
    tm, nc = 512, 256
    dilated = [(gi, d) for gi, d in enumerate(DILATIONS) if d > 1]

    def body(h_ref, g_ref, w_ref, b_ref, u_ref, zq_ref, zg_ref, *rest):
        zf_refs, slabs = rest[:len(dilated)], rest[len(dilated):]
        xhat, _ = _rms(h_ref[...], g_ref[...])
        u = (xhat * g_ref[...]).astype(BF16)
        u_ref[...] = u
        for c in range(D_IN // nc):
            z = _nt(u, w_ref[c * nc:(c + 1) * nc, :]) + b_ref[:, c * nc:(c + 1) * nc]
            if c < QKV_W // nc:
                zq_ref[:, c * nc:(c + 1) * nc] = z.astype(BF16)
            else:
                zg_ref[:, c * nc - QKV_W:(c + 1) * nc - QKV_W] = z.astype(BF16)
            part, gi = divmod(c, len(DILATIONS))
            for k, (gk, d) in enumerate(dilated):
                if part < 3 and gi == gk:
                    slab, n = slabs[3 * k + part], tm // d
                    slab[0] = z[:, :128]
                    slab[1] = z[:, 128:]
                    for r in range(d):
                        for half in range(2):
                            c0 = 768 * r + 256 * part + 128 * half
                            zf_refs[k][:, c0:c0 + 128] = slab[half, pl.ds(r, n, stride=d), :].astype(BF16)

    row = lambda w: pl.BlockSpec((tm, w), lambda i: (i, 0))
    full = _resident
    sds = jax.ShapeDtypeStruct
    return _call(
        body, name=name, grid=(t // tm,),
        in_specs=[row(D_MODEL), full(g), full(w_t), full(b_in)],
        out_specs=[row(D_MODEL), row(QKV_W), row(GATE_W)]
        + [pl.BlockSpec((tm // d, d * 768), lambda i: (i, 0)) for _, d in dilated],
        out_shape=[sds((t, D_MODEL), BF16), sds((t, QKV_W), BF16), sds((t, GATE_W), BF16)]
        + [sds((t // d, d * 768), BF16) for _, d in dilated],
        scratch=[pltpu.VMEM((2, tm, 128), F32)] * (3 * len(dilated)),
        sem=("parallel",), args=(h, g, w_t, b_in), hosted=hosted)


def _inproj_bwd(pieces, h, g, dh_res, w_t, name):
    t = h.shape[0]
    tm = 256
    npiece = len(PIECE_W)
    offs = np.concatenate([[0], np.cumsum(PIECE_W)]).tolist()

    def body(*refs):
        p_refs = refs[:npiece]
        h_ref, g_ref, dhr_ref, w_ref, dh_ref, dz_ref, db_ref, dg_ref = refs[npiece:npiece + 8]
        slabs = list(refs[npiece + 8:])
        i = pl.program_id(0)

        @pl.when(i == 0)
        def _():
            db_ref[...] = jnp.zeros_like(db_ref)
            dg_ref[...] = jnp.zeros_like(dg_ref)

        du = jnp.zeros((tm, D_MODEL), F32)
        for k in range(npiece):
            o, w = offs[k], PIECE_W[k]
            token_order = _unfold(p_refs[k], slabs.pop(), PIECE_D[k]).astype(BF16) if PIECE_D[k] > 1 else None
            for c0 in range(0, w, 512):
                cw = min(512, w - c0)
                pz = p_refs[k][:, c0:c0 + cw] if token_order is None else token_order
                dz_ref[:, o + c0:o + c0 + cw] = pz
                db_ref[:, o + c0:o + c0 + cw] += jnp.sum(pz.astype(F32), axis=0, keepdims=True)
                du = du + _nn(pz, w_ref[o + c0:o + c0 + cw, :])
        xhat, r = _rms(h_ref[...], g_ref[...])
        dx, dg = _rms_bwd(du, xhat, r, g_ref[...])
        dh_ref[...] = dhr_ref[...] + dx
        dg_ref[...] += dg

    row = lambda w: pl.BlockSpec((tm, w), lambda i: (i, 0))
    full = lambda shp: pl.BlockSpec(shp, lambda i: (0, 0))
    return pl.pallas_call(
        body, name=name, grid=(t // tm,),
        in_specs=[pl.BlockSpec((tm // d, d * w), lambda i: (i, 0)) for w, d in zip(PIECE_W, PIECE_D)]
        + [row(D_MODEL), _resident(g), row(D_MODEL), _resident(w_t)],
        out_specs=[row(D_MODEL), row(D_IN), full((1, D_IN)), full((1, D_MODEL))],
        out_shape=[jax.ShapeDtypeStruct((t, D_MODEL), F32), jax.ShapeDtypeStruct((t, D_IN), BF16),
                   jax.ShapeDtypeStruct((1, D_IN), F32), jax.ShapeDtypeStruct((1, D_MODEL), F32)],
        scratch_shapes=[pltpu.VMEM((2, tm, 128), F32)] * sum(d > 1 for d in PIECE_D),
        compiler_params=_params("arbitrary"),
    )(*pieces, h, g, dh_res, w_t)


def _t5_bucket(dist):
    max_exact = N_BUCKETS // 2
    n = jnp.maximum(dist, 0)
    nf = jnp.maximum(n, 1).astype(jnp.float32)
    large = max_exact + (jnp.log(nf / max_exact) / math.log(MAX_DISTANCE / max_exact)
                         * (N_BUCKETS - max_exact)).astype(jnp.int32)
    large = jnp.minimum(large, N_BUCKETS - 1)
    return jnp.where(n < max_exact, n, large)


ATT_CFG = ((1, 128, 0, 4), (4, 128, 4, 4), (16, 128, 8, 4), (1, B_WINDOW - 1, A_HEADS, 8))


def _bucket_tiles():
    qi = jnp.arange(BLOCK)[:, None]
    ki = jnp.arange(2 * BLOCK)[None, :]
    dist = qi + BLOCK - ki
    return jnp.stack([_t5_bucket(dist * cfg[0]) for cfg in ATT_CFG]).astype(jnp.int32)


def _band(max_steps):
    row = lax.broadcasted_iota(jnp.int32, (BLOCK, 2 * BLOCK), 0)
    col = lax.broadcasted_iota(jnp.int32, (BLOCK, 2 * BLOCK), 1)
    dist = row + BLOCK - col
    return (dist >= 0) & (dist <= max_steps)


def _bias_build(table, buckets, hosted=()):
    def body(tab_ref, bt_ref, out_ref):
        col = lax.broadcasted_iota(jnp.int32, (BLOCK, 2 * BLOCK), 1)
        for ci, (_, max_steps, h0, nh) in enumerate(ATT_CFG):
            bt = bt_ref[ci]
            band = _band(max_steps)
            for h in range(h0, h0 + nh):
                acc = lax.fori_loop(0, N_BUCKETS, lambda b, acc: jnp.where(bt == b, tab_ref[b, h], acc),
                                    jnp.zeros((BLOCK, 2 * BLOCK), F32))
                out_ref[0, BLOCK * h:BLOCK * (h + 1), :] = jnp.where(band & (col >= BLOCK), acc, NEG)
                out_ref[1, BLOCK * h:BLOCK * (h + 1), :] = jnp.where(band, acc, NEG)

    return _call(
        body, name="bias_build", grid=(1,),
        in_specs=[pl.BlockSpec(memory_space=pltpu.SMEM), pl.BlockSpec(memory_space=pltpu.VMEM)],
        out_specs=[pl.BlockSpec(memory_space=pltpu.VMEM)],
        out_shape=[jax.ShapeDtypeStruct((2, TOTAL_HEADS * BLOCK, 2 * BLOCK), F32)],
        sem=("arbitrary",), args=(table, buckets), hosted=hosted)


def _bias_reduce(dbias, buckets, dsink_rows, hosted=()):
    def body(db_ref, bt_ref, ds_ref, out_ref, sink_ref):
        ri = lax.broadcasted_iota(jnp.int32, (N_BUCKETS, 128), 0)
        ci = lax.broadcasted_iota(jnp.int32, (N_BUCKETS, 128), 1)

        def per_bucket(b, acc):
            for cfg_i, (_, _, h0, nh) in enumerate(ATT_CFG):
                hit = bt_ref[cfg_i] == b
                for h in range(h0, h0 + nh):
                    val = jnp.sum(jnp.where(hit, db_ref[BLOCK * h:BLOCK * (h + 1), :], 0.0))
                    acc = jnp.where((ri == b) & (ci == h), val, acc)
            return acc

        out_ref[...] = lax.fori_loop(0, N_BUCKETS, per_bucket, jnp.zeros((N_BUCKETS, 128), F32))
        for h in range(8):
            sink_ref[h:h + 1, :] = jnp.sum(ds_ref[BLOCK * h:BLOCK * (h + 1), :], axis=0, keepdims=True)

    return _call(
        body, name="bias_reduce", grid=(1,),
        in_specs=[pl.BlockSpec(memory_space=pltpu.VMEM)] * 3,
        out_specs=[pl.BlockSpec(memory_space=pltpu.VMEM)] * 2,
        out_shape=[jax.ShapeDtypeStruct((N_BUCKETS, 128), F32), jax.ShapeDtypeStruct((8, 128), F32)],
        sem=("arbitrary",), args=(dbias, buckets, dsink_rows), hosted=hosted)


class _Att:
    def __init__(self, cfg_i):
        stride, _, h0, nh = ATT_CFG[cfg_i]
        self.d = stride if cfg_i < 3 else 1
        self.h0, self.nh = h0, nh
        self.row_w = QKV_W if self.d == 1 else 3 * 256
        if cfg_i < 3:
            self.nq, self.wkv = 1, 256
            self.q_unit = [cfg_i if self.d == 1 else 0]
            self.k_unit, self.v_unit = (3 + cfg_i, 6 + cfg_i) if self.d == 1 else (1, 2)
            self.sinks = False
        else:
            self.nq, self.wkv = 2, 128
            self.q_unit = [9, 10]
            self.k_unit, self.v_unit = 22, 23
            self.sinks = True
        self.wq = 256 * self.nq


def _att_in_specs(cf, bsz):
    uq, ukv = cf.row_w // 256, cf.row_w // cf.wkv
    specs = [pl.BlockSpec((bsz, BLOCK, 256), functools.partial(lambda r, j, u: (0, j, r * uq + u), u=u))
             for u in cf.q_unit]
    for unit in (cf.k_unit, cf.v_unit):
        specs.append(pl.BlockSpec((bsz, BLOCK, cf.wkv),
                                  functools.partial(lambda r, j, u: (0, jnp.maximum(j - 1, 0), r * ukv + u), u=unit)))
        specs.append(pl.BlockSpec((bsz, BLOCK, cf.wkv),
                                  functools.partial(lambda r, j, u: (0, j, r * ukv + u), u=unit)))
    for qb in range(cf.nq):
        specs.append(pl.BlockSpec((None, HEADS_PER_BLOCK * BLOCK, 2 * BLOCK), functools.partial(
            lambda r, j, u: (jnp.minimum(j, 1), u, 0), u=cf.h0 // HEADS_PER_BLOCK + qb)))
    if cf.sinks:
        specs += [pl.BlockSpec((HEADS_PER_BLOCK * BLOCK, 128), functools.partial(lambda r, j, u: (u, 0), u=qb))
                  for qb in range(cf.nq)]
    return specs


HEADS_PER_BLOCK = 4


def _head_masks(rows):
    head = lax.broadcasted_iota(jnp.int32, (rows, 256), 1) // HEAD_DIM
    return [head == h for h in range(HEADS_PER_BLOCK)]


def _stack_heads(x, masks):
    return jnp.concatenate([jnp.where(m, x, jnp.zeros_like(x)) for m in masks], axis=0)


def _unstack_heads(x4, masks):
    blocks = [x4[BLOCK * h:BLOCK * (h + 1)] for h in range(HEADS_PER_BLOCK)]
    return jnp.where(masks[0], blocks[0], jnp.where(masks[1], blocks[1], jnp.where(masks[2], blocks[2], blocks[3])))


def _row_value(x):
    return jnp.max(x, axis=-1, keepdims=True)


def _kv_operands(cf, x):
    if cf.wkv == 256:
        return [x]
    lane = lax.broadcasted_iota(jnp.int32, x.shape, 1)
    swapped = pltpu.roll(x, HEAD_DIM, 1)
    halves = [jnp.where(lane < HEAD_DIM, x, swapped), jnp.where(lane < HEAD_DIM, swapped, x)]
    return [jnp.concatenate([half, half], axis=1) for half in halves]


def _kv_fold(cf, grads):
    if cf.wkv == 256:
        return grads[0]
    folded = []
    for g in grads:
        x = g[:, :128] + g[:, 128:]
        folded.append(x + pltpu.roll(x, HEAD_DIM, 1))
    lane = lax.broadcasted_iota(jnp.int32, folded[0].shape, 1)
    return jnp.where(lane < HEAD_DIM, folded[0], folded[1])


def _attn_fwd(cf, zf, bias, sinks, name, hosted=()):
    bsz, l, _ = zf.shape
    nb = l // BLOCK

    def body(*refs):
        refs = list(refs)
        q_refs = [refs.pop(0) for _ in range(cf.nq)]
        kp_ref, kc_ref, vp_ref, vc_ref = [refs.pop(0) for _ in range(4)]
        bias_refs = [refs.pop(0) for _ in range(cf.nq)]
        sink_refs = [refs.pop(0) for _ in range(cf.nq)] if cf.sinks else None
        o_ref, lse_ref = refs
        masks, kv_masks = _head_masks(BLOCK), _head_masks(2 * BLOCK)
        sinks4 = [_row_value(ref[...]) for ref in sink_refs] if cf.sinks else None
        for bi in range(bsz):
            k = jnp.concatenate([kp_ref[bi], kc_ref[bi]], axis=0)
            v = jnp.concatenate([vp_ref[bi], vc_ref[bi]], axis=0)
            k_ops, v_ops = _kv_operands(cf, k), _kv_operands(cf, v)
            for qb in range(cf.nq):
                cols = slice(256 * qb, 256 * (qb + 1))
                kb, vb = k_ops[qb], v_ops[qb]
                q4 = _stack_heads(q_refs[qb][bi] * (HEAD_DIM ** -0.5), masks)
                s = _nt(q4, kb) + bias_refs[qb][...]
                m = jnp.max(s, axis=-1, keepdims=True)
                if cf.sinks:
                    sk = sinks4[qb]
                    m = jnp.maximum(m, sk)
                p = jnp.exp(s - m)
                den = jnp.sum(p, axis=-1, keepdims=True)
                if cf.sinks:
                    den = den + jnp.exp(sk - m)
                pn = (p * (1.0 / den)).astype(BF16)
                p_lanes = jnp.concatenate([pn[BLOCK * h:BLOCK * (h + 1)] for h in range(HEADS_PER_BLOCK)], axis=1)
                v4 = jnp.concatenate([jnp.where(mk, vb, jnp.zeros_like(vb)) for mk in kv_masks], axis=0)
                o_ref[bi, :, cols] = _nn(p_lanes, v4)
                lse_ref[bi, :, cols] = _unstack_heads(
                    jnp.broadcast_to(m + jnp.log(den), (HEADS_PER_BLOCK * BLOCK, 256)), masks)

    in_specs = _att_in_specs(cf, bsz)
    args = [zf] * (cf.nq + 4) + [bias] * cf.nq + ([sinks] * cf.nq if cf.sinks else [])
    out = pl.BlockSpec((bsz, BLOCK, cf.wq), lambda r, j: (0, j, r))
    shape = jax.ShapeDtypeStruct((bsz, l, cf.d * cf.wq), F32)
    return _call(
        body, name=name, grid=(cf.d, nb), in_specs=in_specs, out_specs=[out, out], out_shape=[shape, shape],
        sem=("parallel", "arbitrary"), args=args, hosted=hosted)


def _attn_bwd(cf, zf, bias, sinks, stats, dy, name, hosted=()):
    bsz, l, _ = zf.shape
    nb = l // BLOCK

    def body(*refs):
        refs = list(refs)
        q_refs = [refs.pop(0) for _ in range(cf.nq)]
        kp_ref, kc_ref, vp_ref, vc_ref = [refs.pop(0) for _ in range(4)]
        bias_refs = [refs.pop(0) for _ in range(cf.nq)]
        sink_refs = [refs.pop(0) for _ in range(cf.nq)] if cf.sinks else None
        st_ref, dy_ref, dq_ref, dk_ref, dv_ref, dbias_ref = [refs.pop(0) for _ in range(6)]
        dsink_ref = refs.pop(0) if cf.sinks else None
        dk_acc, dv_acc = refs
        r, j = pl.program_id(0), pl.program_id(1)

        @pl.when((r == 0) & (j == 0))
        def _():
            dbias_ref[...] = jnp.zeros_like(dbias_ref)
            if cf.sinks:
                dsink_ref[...] = jnp.zeros_like(dsink_ref)

        @pl.when(j == 0)
        def _():
            dk_acc[...] = jnp.zeros_like(dk_acc)
            dv_acc[...] = jnp.zeros_like(dv_acc)

        masks = _head_masks(BLOCK)
        cur = pl.ds(pl.multiple_of(j * BLOCK, BLOCK), BLOCK)
        prev = pl.ds(pl.multiple_of(jnp.maximum(j - 1, 0) * BLOCK, BLOCK), BLOCK)
        sinks4 = [_row_value(ref[...]) for ref in sink_refs] if cf.sinks else None
        ds_sum, dsink_sum = [None] * cf.nq, [None] * cf.nq
        for bi in range(bsz):
            k = jnp.concatenate([kp_ref[bi], kc_ref[bi]], axis=0)
            v = jnp.concatenate([vp_ref[bi], vc_ref[bi]], axis=0)
            dk_blocks, dv_blocks = [], []
            k_ops, v_ops = _kv_operands(cf, k), _kv_operands(cf, v)
            for qb in range(cf.nq):
                cols = slice(256 * qb, 256 * (qb + 1))
                kb, vb = k_ops[qb], v_ops[qb]
                q4 = _stack_heads(q_refs[qb][bi] * (HEAD_DIM ** -0.5), masks)
                st = st_ref[bi, :, 128 * qb:128 * (qb + 1)]
                lt4, e4 = _read_stats(st, 0), _read_stats(st, 1)
                pa = jnp.exp(_nt(q4, kb) + bias_refs[qb][...] - lt4)
                dy = dy_ref[bi, :, cols]
                dy4 = _stack_heads(dy, masks)
                ds = pa * (_nt(dy4, vb) - e4)
                ds_sum[qb] = ds if ds_sum[qb] is None else ds_sum[qb] + ds
                if cf.sinks:
                    dsk = jnp.exp(sinks4[qb] - lt4) * e4
                    dsink_sum[qb] = dsk if dsink_sum[qb] is None else dsink_sum[qb] + dsk
                dsb = ds.astype(BF16)
                dq_ref[bi, :, cols] = (_unstack_heads(_nn(dsb, kb), masks) * (HEAD_DIM ** -0.5)).astype(dq_ref.dtype)
                dk_blocks.append(_tn(dsb, q4))
                dv_blocks.append(_tn(pa.astype(BF16), dy4))
            dk_new, dv_new = _kv_fold(cf, dk_blocks), _kv_fold(cf, dv_blocks)
            dk_acc[bi, cur, :] += dk_new[BLOCK:]
            dv_acc[bi, cur, :] += dv_new[BLOCK:]
            dk_acc[bi, prev, :] += dk_new[:BLOCK]
            dv_acc[bi, prev, :] += dv_new[:BLOCK]
        for qb in range(cf.nq):
            rows = slice(HEADS_PER_BLOCK * BLOCK * qb, HEADS_PER_BLOCK * BLOCK * (qb + 1))
            dbias_ref[rows, :] += ds_sum[qb]
            if cf.sinks:
                dsink_ref[rows, :] -= dsink_sum[qb]

        @pl.when(j == nb - 1)
        def _():
            dk_ref[...] = dk_acc[...].astype(dk_ref.dtype)
            dv_ref[...] = dv_acc[...].astype(dv_ref.dtype)

    tok = pl.BlockSpec((bsz, BLOCK, cf.wq), lambda r, j: (0, j, r))
    in_specs = _att_in_specs(cf, bsz) + [pl.BlockSpec((bsz, BLOCK, cf.wq // 2), lambda r, j: (0, j, r)), tok]
    args = [zf] * (cf.nq + 4) + [bias] * cf.nq + ([sinks] * cf.nq if cf.sinks else []) + [stats, dy]
    seq = pl.BlockSpec((bsz, l, cf.wkv), lambda r, j: (0, 0, r))
    out_specs = [tok, seq, seq, pl.BlockSpec((cf.nh * BLOCK, 2 * BLOCK), lambda r, j: (0, 0))]
    grad_dtype = BF16 if cf.d == 1 else F32
    out_shape = [jax.ShapeDtypeStruct((bsz, l, cf.d * cf.wq), grad_dtype),
                 jax.ShapeDtypeStruct((bsz, l, cf.d * cf.wkv), grad_dtype),
                 jax.ShapeDtypeStruct((bsz, l, cf.d * cf.wkv), grad_dtype),
                 jax.ShapeDtypeStruct((cf.nh * BLOCK, 2 * BLOCK), F32)]
    if cf.sinks:
        out_specs.append(pl.BlockSpec((cf.nh * BLOCK, 128), lambda r, j: (0, 0)))
        out_shape.append(jax.ShapeDtypeStruct((cf.nh * BLOCK, 128), F32))
    return _call(
        body, name=name, grid=(cf.d, nb), in_specs=in_specs, out_specs=out_specs, out_shape=out_shape,
        scratch=[pltpu.VMEM((bsz, l, cf.wkv), F32), pltpu.VMEM((bsz, l, cf.wkv), F32)],
        sem=("arbitrary", "arbitrary"), args=args, hosted=hosted)


def _merge_fwd(o_a, lse_a, o_b, zg, h, wa_t, wb_t, wout, name, hosted=()):
    t = h.shape[0]
    tm = 512

    def body(o1, o2, o3, l1, l2, l3, ob_ref, zg_ref, h_ref, wa_ref, wb_ref, wo_ref, h2_ref, ya_ref, lt_ref, *slabs):
        o = [_unfold(ref, slabs[i], d) for i, (ref, d) in enumerate(zip((o1, o2, o3), DILATIONS))]
        l = [_unfold(ref, slabs[3 + i], d) for i, (ref, d) in enumerate(zip((l1, l2, l3), DILATIONS))]
        m = jnp.maximum(jnp.maximum(l[0], l[1]), l[2])
        e1, e2, e3 = jnp.exp(l[0] - m), jnp.exp(l[1] - m), jnp.exp(l[2] - m)
        se = e1 + e2 + e3
        ya = (e1 / se) * o[0] + (e2 / se) * o[1] + (e3 / se) * o[2]
        ya_ref[...] = ya
        lt_ref[...] = m + jnp.log(se)
        pa = _nt(ya.astype(BF16), wa_ref[...])
        pb = _nt(ob_ref[...].astype(BF16), wb_ref[...])
        merged = (jax.nn.sigmoid(zg_ref[:, :D_MODEL].astype(F32)) * pa
                  + jax.nn.sigmoid(zg_ref[:, D_MODEL:].astype(F32)) * pb)
        h2_ref[...] = h_ref[...] + _nn(merged.astype(BF16), wo_ref[...])

    row = lambda w: pl.BlockSpec((tm, w), lambda i: (i, 0))
    folded = [pl.BlockSpec((tm // d, d * 256), lambda i: (i, 0)) for d in DILATIONS]
    full = _resident
    return _call(
        body, name=name, grid=(t // tm,),
        in_specs=folded + folded + [row(512), row(GATE_W), row(D_MODEL), full(wa_t), full(wb_t), full(wout)],
        out_specs=[row(D_MODEL), row(256), row(256)],
        out_shape=[jax.ShapeDtypeStruct((t, D_MODEL), F32), jax.ShapeDtypeStruct((t, 256), F32),
                   jax.ShapeDtypeStruct((t, 256), F32)],
        scratch=[pltpu.VMEM((2, tm, 128), F32)] * 6,
        sem=("parallel",), args=(*o_a, *lse_a, o_b, zg, h, wa_t, wb_t, wout), hosted=hosted)


STAT_LANES = 32


def _head_stats(lt, dy, y):
    rows, w = dy.shape
    lane = lax.broadcasted_iota(jnp.int32, (rows, w), 1)
    out_lane = lax.broadcasted_iota(jnp.int32, (rows, w // 2), 1)
    prod = dy * y
    out = jnp.zeros((rows, w // 2), F32)
    for hd in range(w // HEAD_DIM):
        mine = lane // HEAD_DIM == hd
        dot = jnp.sum(jnp.where(mine, prod, 0.0), axis=-1, keepdims=True)
        both = jnp.where(out_lane % STAT_LANES < STAT_LANES // 2, _row_value(jnp.where(mine, lt, NEG)), dot)
        out = jnp.where(out_lane // STAT_LANES == hd, both, out)
    return out


def _read_stats(st, part):
    lane = lax.broadcasted_iota(jnp.int32, st.shape, 1)
    half = (lane % STAT_LANES < STAT_LANES // 2) == (part == 0)
    return jnp.concatenate([_row_value(jnp.where((lane // STAT_LANES == hd) & half, st, NEG))
                            for hd in range(HEADS_PER_BLOCK)], axis=0)


def _merge_bwd(dh, ya, lse_tot, o_b, lse_b, zg, wa_t, wb_t, wout, name, hosted=()):
    t = dh.shape[0]
    tm = 512

    def body(dh_ref, ya_ref, lt_ref, ob_ref, lb_ref, zg_ref, wa_ref, wb_ref, wo_ref,
             mg_ref, dpa_ref, dpb_ref, dzg_ref, dy1, dy2, dy3, st1, st2, st3, dyb_ref, stb_ref, *slabs):
        dm = _nt(dh_ref[...].astype(BF16), wo_ref[...])
        pa = _nt(ya_ref[...].astype(BF16), wa_ref[...])
        pb = _nt(ob_ref[...].astype(BF16), wb_ref[...])
        sa = jax.nn.sigmoid(zg_ref[:, :D_MODEL].astype(F32))
        sb = jax.nn.sigmoid(zg_ref[:, D_MODEL:].astype(F32))
        mg_ref[...] = (sa * pa + sb * pb).astype(BF16)
        dpa = (dm * sa).astype(BF16)
        dpb = (dm * sb).astype(BF16)
        dpa_ref[...] = dpa
        dpb_ref[...] = dpb
        dzg_ref[:, :D_MODEL] = (dm * pa * (sa * (1.0 - sa))).astype(BF16)
        dzg_ref[:, D_MODEL:] = (dm * pb * (sb * (1.0 - sb))).astype(BF16)
        dya = _nn(dpa, wa_ref[...])
        dyb = _nn(dpb, wb_ref[...])
        stats = _head_stats(lt_ref[...], dya, ya_ref[...])
        for i, d in enumerate(DILATIONS):
            _fold(dya, slabs[i], (dy1, dy2, dy3)[i], d)
            _fold(stats, slabs[3 + i], (st1, st2, st3)[i], d)
        dyb_ref[...] = dyb.astype(BF16)
        stb_ref[...] = _head_stats(lb_ref[...], dyb, ob_ref[...])

    row = lambda w: pl.BlockSpec((tm, w), lambda i: (i, 0))
    folded = [pl.BlockSpec((tm // d, d * 256), lambda i: (i, 0)) for d in DILATIONS]
    full = _resident
    sds = jax.ShapeDtypeStruct
    dy_shape = [sds((t // d, d * 256), BF16) for d in DILATIONS]
    st_shape = [sds((t // d, d * 128), F32) for d in DILATIONS]
    st_specs = [pl.BlockSpec((tm // d, d * 128), lambda i: (i, 0)) for d in DILATIONS]
    return _call(
        body, name=name, grid=(t // tm,),
        in_specs=[row(D_MODEL), row(256), row(256), row(512), row(512), row(GATE_W), full(wa_t), full(wb_t),
                  full(wout)],
        out_specs=[row(D_MODEL), row(D_MODEL), row(D_MODEL), row(GATE_W)] + folded + st_specs + [row(512), row(256)],
        out_shape=[sds((t, D_MODEL), BF16), sds((t, D_MODEL), BF16), sds((t, D_MODEL), BF16),
                   sds((t, GATE_W), BF16)] + dy_shape + st_shape + [sds((t, 512), BF16), sds((t, 256), F32)],
        scratch=[pltpu.VMEM((2, tm, 128), F32)] * 6,
        sem=("parallel",), args=(dh, ya, lse_tot, o_b, lse_b, zg, wa_t, wb_t, wout), hosted=hosted)


def _pair_sum(grad, got, name):
    _, _, r, cdim = grad.shape
    core = lax.axis_index("c").astype(jnp.int32).reshape(1)

    def body(core_ref, g_ref, s_ref, o_ref):
        o_ref[...] = (g_ref[...] + s_ref[...]).astype(BF16)

    return pl.pallas_call(
        body, name=name,
        grid_spec=pltpu.PrefetchScalarGridSpec(
            num_scalar_prefetch=1, grid=(N_CHIP,),
            in_specs=[pl.BlockSpec((None, None, r, cdim), lambda q, core_ref: (q, core_ref[0], 0, 0)),
                      pl.BlockSpec((None, None, r, cdim), lambda q, core_ref: (q, 0, 0, 0))],
            out_specs=pl.BlockSpec((None, r, cdim), lambda q, core_ref: (q, 0, 0))),
        out_shape=jax.ShapeDtypeStruct((N_CHIP, r, cdim), BF16),
        compiler_params=_params("parallel"),
    )(core, grad, got)


def _adamw_math(w, g, m, v):
    m = ADAM_B1 * m + (1.0 - ADAM_B1) * g
    v = ADAM_B2 * v + (1.0 - ADAM_B2) * jnp.square(g)
    m_hat = m / (1.0 - ADAM_B1 ** ADAM_STEP)
    v_hat = v / (1.0 - ADAM_B2 ** ADAM_STEP)
    delta = -ADAM_LR * (m_hat / (jnp.sqrt(v_hat) + ADAM_EPS) + ADAM_WD * w)
    return delta, m, v


def _update(w, parts, m, v, transposed, name, hosted=()):
    r, c = parts.shape[1:]

    def body(w_ref, p_ref, m_ref, v_ref, g_ref, d_ref, m2_ref, v2_ref):
        def total(rows):
            return ((p_ref[0, rows].astype(F32) + p_ref[1, rows].astype(F32)) + p_ref[2, rows].astype(F32)) \
                + p_ref[3, rows].astype(F32)

        def update(at, g):
            g_ref[at] = g
            d_ref[at], m2_ref[at], v2_ref[at] = _adamw_math(w_ref[at], g, m_ref[at], v_ref[at])

        if not transposed:
            update((slice(None), slice(None)), total(slice(None)))
            return
        for r0 in range(0, r, 128):
            n = min(128, r - r0)
            gt = total(slice(r0, r0 + n))
            if n < 128:
                gt = jnp.concatenate([gt, jnp.zeros((128 - n, c), F32)], axis=0)
            update((slice(None), slice(r0, r0 + n)), gt.T[:, :n])

    sds = jax.ShapeDtypeStruct(w.shape, F32)
    vm = pl.BlockSpec(memory_space=pltpu.VMEM)
    return _call(body, name=name, grid=(1,), in_specs=[vm] * 4, out_specs=[vm] * 4, out_shape=[sds] * 4,
                 sem=("arbitrary",), args=(w, parts, m, v), hosted=hosted)


SMALL_ROWS = 80


def _small_update(g, w, m, v, name):
    def body(g_ref, w_ref, m_ref, v_ref, gs_ref, d_ref, m2_ref, v2_ref, got_ref, send_sems, recv_sems):
        x, y, c = _place()
        me = 4 * x + 2 * y + c
        got_ref[me] = g_ref[...]
        copies = []
        for k in range(1, N_DEV):
            peer = (x ^ (k >> 2), y ^ ((k >> 1) & 1), c ^ (k & 1))
            cp = pltpu.make_async_remote_copy(
                src_ref=g_ref, dst_ref=got_ref.at[me], send_sem=send_sems.at[k - 1], recv_sem=recv_sems.at[k - 1],
                device_id=peer, device_id_type=MESH)
            cp.start()
            copies.append(cp)
        for cp in copies:
            cp.wait()
        total = got_ref[0]
        for k in range(1, N_DEV):
            total = total + got_ref[k]
        gs_ref[...] = total
        d_ref[...], m2_ref[...], v2_ref[...] = _adamw_math(w_ref[...], total, m_ref[...], v_ref[...])

    sds = jax.ShapeDtypeStruct((SMALL_ROWS, 128), F32)
    vm = pl.BlockSpec(memory_space=pltpu.VMEM)
    return pl.pallas_call(
        body, name=name, in_specs=[vm] * 4, out_specs=[vm] * 4, out_shape=[sds] * 4,
        scratch_shapes=[pltpu.VMEM((N_DEV, SMALL_ROWS, 128), F32), pltpu.SemaphoreType.DMA((N_DEV - 1,)),
                        pltpu.SemaphoreType.DMA((N_DEV - 1,))],
    )(g, w, m, v)


def _pack_small(gains, b_in, rel_bias, sinks, last):
    rows = [a.reshape(8, 128) for a in gains] + [b_in.reshape(40, 128), rel_bias.reshape(5, 128),
                                                 jnp.pad(sinks.reshape(1, 8), ((0, 0), (0, 120))), last]
    rows.append(jnp.zeros((SMALL_ROWS - 79, 128), F32))
    return jnp.concatenate(rows, axis=0)


def _unpack_small(p, like):
    out = [p[8 * i:8 * i + 8].reshape(like[i].shape) for i in range(4)]
    out.append(p[32:72].reshape(like[4].shape))
    out.append(p[72:77].reshape(like[5].shape))
    out.append(p[77, :8].reshape(like[6].shape))
    return out


def kernel(x, ffn1_norm, ffn1_w_gate, ffn1_w_up, ffn1_w_down, mix_norm, w_in, b_in, w_branch_a, w_branch_b, w_out, sinks, rel_bias, ffn2_norm, ffn2_w_gate, ffn2_w_up, ffn2_w_down, final_norm, loss_target, m_ffn1_norm, m_ffn1_w_gate, m_ffn1_w_up, m_ffn1_w_down, m_mix_norm, m_w_in, m_b_in, m_w_branch_a, m_w_branch_b, m_w_out, m_sinks, m_rel_bias, m_ffn2_norm, m_ffn2_w_gate, m_ffn2_w_up, m_ffn2_w_down, m_final_norm, v_ffn1_norm, v_ffn1_w_gate, v_ffn1_w_up, v_ffn1_w_down, v_mix_norm, v_w_in, v_b_in, v_w_branch_a, v_w_branch_b, v_w_out, v_sinks, v_rel_bias, v_ffn2_norm, v_ffn2_w_gate, v_ffn2_w_up, v_ffn2_w_down, v_final_norm):
    bsz, seq, _ = x.shape
    t = bsz * seq
    xt = x.reshape(t, D_MODEL)
    target = loss_target.reshape(t, D_MODEL)

    big = [("ffn1_w_gate", ffn1_w_gate, m_ffn1_w_gate, v_ffn1_w_gate, True),
           ("ffn1_w_up", ffn1_w_up, m_ffn1_w_up, v_ffn1_w_up, True),
           ("ffn1_w_down", ffn1_w_down, m_ffn1_w_down, v_ffn1_w_down, False),
           ("w_in", w_in, m_w_in, v_w_in, True),
           ("w_branch_a", w_branch_a, m_w_branch_a, v_w_branch_a, True),
           ("w_branch_b", w_branch_b, m_w_branch_b, v_w_branch_b, True),
           ("w_out", w_out, m_w_out, v_w_out, False),
           ("ffn2_w_gate", ffn2_w_gate, m_ffn2_w_gate, v_ffn2_w_gate, True),
           ("ffn2_w_up", ffn2_w_up, m_ffn2_w_up, v_ffn2_w_up, True),
           ("ffn2_w_down", ffn2_w_down, m_ffn2_w_down, v_ffn2_w_down, False)]
    shard = {nm: (w[0].T if tr else w[0]).astype(BF16) for nm, w, _, _, tr in big}
    full = {}

    def gather(names):
        return _Gather([shard[nm] for nm in names])

    def keep(names, got):
        for nm, gw in zip(names, got):
            full[nm] = gw.reshape(-1, gw.shape[-1])

    ffn1_names = ["ffn1_w_gate", "ffn1_w_up", "ffn1_w_down"]
    mix_names = ["w_in", "w_branch_a", "w_branch_b", "w_out"]
    ffn2_names = ["ffn2_w_gate", "ffn2_w_up", "ffn2_w_down"]
    g1, gm, g2, gf = ffn1_norm, mix_norm, ffn2_norm, final_norm.reshape(1, D_MODEL)

    buckets = _bucket_tiles()
    (bias,), (got,) = _bias_build(rel_bias, buckets, hosted=[gather(ffn1_names)])
    keep(ffn1_names, got)
    (h1, n1, a1, b1, hff1), (got,) = _ffn_fwd(xt, g1, full["ffn1_w_gate"], full["ffn1_w_up"], full["ffn1_w_down"], "ffn1_fwd",
                                hosted=[gather(mix_names + ffn2_names)])
    keep(mix_names + ffn2_names, got)
    (u, zq, zg, *zdil), _ = _inproj_fwd(h1, gm, full["w_in"], b_in, "inproj_fwd")
    sink_rows = jnp.broadcast_to(sinks.reshape(8, 1, 1), (8, BLOCK, 128)).reshape(8 * BLOCK, 128)
    cfs = [_Att(i) for i in range(4)]
    zfold = []
    for i, cf in enumerate(cfs):
        if cf.d == 1:
            zfold.append(zq.reshape(bsz, seq, QKV_W))
        else:
            zfold.append(zdil[i - 1].reshape(bsz, seq // cf.d, cf.d * cf.row_w))
    att = [None] * 4
    for i in (3, 0, 1, 2):
        cf = cfs[i]
        (o, lse), _ = _attn_fwd(cf, zfold[i], bias, sink_rows, f"attn{i}_fwd")
        att[i] = (o.reshape(t // cf.d, cf.d * cf.wq), lse.reshape(t // cf.d, cf.d * cf.wq))
    o_b, lse_b = att[3]
    (h2, ya, lse_tot), _ = _merge_fwd([a[0] for a in att[:3]], [a[1] for a in att[:3]], o_b, zg, h1,
                                      full["w_branch_a"], full["w_branch_b"], full["w_out"], "merge_fwd")
    (dh3, n2, a2, b2, hff2, loss_part, dgf), _ = _ffn_fwd(
        h2, g2, full["ffn2_w_gate"], full["ffn2_w_up"], full["ffn2_w_down"], "ffn2_fwd", head=(gf, target))

    grads, pair, from_chips = {}, {}, {}

    def by_owner(nm):
        return grads[nm].reshape(N_CHIP, 2, -1, grads[nm].shape[-1])

    def to_core(names):
        return _CoreExchange([by_owner(nm) for nm in names])

    def pair_up(names, got):
        for nm, sib in zip(names, got):
            pair[nm] = _pair_sum(by_owner(nm), sib, f"pair_sum_{nm}")

    def to_chips(names):
        return _ChipExchange([pair[nm] for nm in names])

    def landed(names, got):
        for nm, parts in zip(names, got):
            from_chips[nm] = parts

    dh2, dg2, da, db = _ffn_bwd(h2, a2, b2, g2, dh3, full["ffn2_w_gate"], full["ffn2_w_up"],
                                          full["ffn2_w_down"], "ffn2_bwd")
    grads["ffn2_w_gate"], _ = _tn_matmul(da, n2, 1408, "ffn2_dgate")
    grads["ffn2_w_up"], _ = _tn_matmul(db, n2, 1408, "ffn2_dup")
    grads["ffn2_w_down"], _ = _tn_matmul(hff2, dh3, 1408, "ffn2_ddown", scale=0.5)
    (merged, dpa, dpb, dzg, *cot), (got,) = _merge_bwd(
        dh2, ya, lse_tot, o_b, lse_b, zg, full["w_branch_a"], full["w_branch_b"], full["w_out"], "merge_bwd",
        hosted=[to_core(ffn2_names)])
    dys, sts = cot[0:3] + [cot[6]], cot[3:6] + [cot[7]]
    pair_up(ffn2_names, got)
    dq, dk, dv, dbias, dsink = [None] * 4, [None] * 4, [None] * 4, [None] * 4, None
    for i in (3, 0, 1, 2):
        cf = cfs[i]
        shp = (bsz, seq // cf.d, cf.d * cf.wq)
        hosted = [to_chips(ffn2_names)] if i == 3 else ()
        res, got = _attn_bwd(cf, zfold[i], bias, sink_rows, sts[i].reshape(shp[:2] + (shp[2] // 2,)), dys[i].reshape(shp),
                             f"attn{i}_bwd", hosted=hosted)
        if i == 3:
            landed(ffn2_names, got[0])
        dq[i] = res[0].reshape(t // cf.d, cf.d * cf.wq)
        dk[i] = res[1].reshape(t // cf.d, cf.d * cf.wkv)
        dv[i] = res[2].reshape(t // cf.d, cf.d * cf.wkv)
        dbias[i] = res[3]
        if cf.sinks:
            dsink = res[4]
    pieces = dq[:3] + dk[:3] + dv[:3] + [dq[3], dk[3], dv[3], dzg]
    dh1, dz, db_in, dgm = _inproj_bwd(pieces, h1, gm, dh2, full["w_in"], "inproj_bwd")
    dx, dg1, da, db = _ffn_bwd(xt, a1, b1, g1, dh1, full["ffn1_w_gate"], full["ffn1_w_up"],
                                         full["ffn1_w_down"], "ffn1_bwd")
    grads["w_in"], _ = _tn_matmul(dz, u, 1280, "dw_in")
    grads["ffn1_w_down"], (got,) = _tn_matmul(hff1, dh1, 1408, "ffn1_ddown", hosted=[to_core(["w_in"])], scale=0.5)
    pair_up(["w_in"], got)
    grads["ffn1_w_gate"], got = _tn_matmul(da, n1, 1408, "ffn1_dgate",
                                           hosted=[to_chips(["w_in"]), to_core(["ffn1_w_down"])])
    landed(["w_in"], got[0])
    pair_up(["ffn1_w_down"], got[1])
    grads["ffn1_w_up"], got = _tn_matmul(db, n1, 1408, "ffn1_dup",
                                         hosted=[to_chips(["ffn1_w_down"]), to_core(["ffn1_w_gate"])])
    landed(["ffn1_w_down"], got[0])
    pair_up(["ffn1_w_gate"], got[1])
    grads["w_out"], got = _tn_matmul(merged, dh2, 1024, "dw_out",
                                     hosted=[to_chips(["ffn1_w_gate"]), to_core(["ffn1_w_up"])])
    landed(["ffn1_w_gate"], got[0])
    pair_up(["ffn1_w_up"], got[1])
    grads["w_branch_b"], got = _tn_matmul(dpb, o_b, 1024, "dw_branch_b",
                                          hosted=[to_chips(["ffn1_w_up"]), to_core(["w_out"])])
    landed(["ffn1_w_up"], got[0])
    pair_up(["w_out"], got[1])
    grads["w_branch_a"], got = _tn_matmul(dpa, ya, 1024, "dw_branch_a",
                                          hosted=[to_chips(["w_out"]), to_core(["w_branch_b"])])
    landed(["w_out"], got[0])
    pair_up(["w_branch_b"], got[1])
    (dtable, dsinks), got = _bias_reduce(jnp.concatenate(dbias, axis=0), buckets, dsink,
                                         hosted=[to_chips(["w_branch_b"]), to_core(["w_branch_a"])])
    landed(["w_branch_b"], got[0])
    pair_up(["w_branch_a"], got[1])

    out_g, out_d, out_m, out_v = {}, {}, {}, {}
    for idx, (nm, w, m, v, tr) in enumerate(big):
        around = tr and w.shape[-1] % 128 != 0
        wmv = [a[0].T if around else a[0] for a in (w, m, v)]
        res, got = _update(wmv[0], from_chips[nm], wmv[1], wmv[2], tr and not around, f"update_{nm}",
                           hosted=[to_chips(["w_branch_a"])] if idx == 0 else ())
        if idx == 0:
            landed(["w_branch_a"], got[0])
        out_g[nm], out_d[nm], out_m[nm], out_v[nm] = [(a.T if around else a)[None] for a in res]

    small = [("ffn1_norm", ffn1_norm, m_ffn1_norm, v_ffn1_norm), ("mix_norm", mix_norm, m_mix_norm, v_mix_norm),
             ("ffn2_norm", ffn2_norm, m_ffn2_norm, v_ffn2_norm), ("final_norm", final_norm, m_final_norm, v_final_norm),
             ("b_in", b_in, m_b_in, v_b_in), ("rel_bias", rel_bias, m_rel_bias, v_rel_bias),
             ("sinks", sinks, m_sinks, v_sinks)]
    zero_row = jnp.zeros((1, 128), F32)
    pack = lambda arrs, last: _pack_small(arrs[:4], arrs[4], arrs[5], arrs[6], last)
    g_small = pack([dg1, dgm, dg2, dgf, db_in, dtable[:, :TOTAL_HEADS], dsinks[:, 0]], loss_part)
    packed = [pack([s[k] for s in small], zero_row) for k in (1, 2, 3)]
    gs, ds, ms, vs = _small_update(g_small, *packed, "small_update")
    like = [s[1] for s in small]
    for nm_s, g_, d_, m_, v_ in zip([s[0] for s in small], _unpack_small(gs, like), _unpack_small(ds, like),
                                    _unpack_small(ms, like), _unpack_small(vs, like)):
        out_g[nm_s], out_d[nm_s], out_m[nm_s], out_v[nm_s] = g_, d_, m_, v_
    loss = gs[78, 0]

    order = ["ffn1_norm", "ffn1_w_gate", "ffn1_w_up", "ffn1_w_down", "mix_norm", "w_in", "b_in", "w_branch_a",
             "w_branch_b", "w_out", "sinks", "rel_bias", "ffn2_norm", "ffn2_w_gate", "ffn2_w_up", "ffn2_w_down",
             "final_norm"]
    return (loss, dx.reshape(x.shape), *[out_g[k] for k in order], *[out_d[k] for k in order],
            *[out_m[k] for k in order], *[out_v[k] for k in order])
```

```python
import functools
import math

import numpy as np
import jax
import jax.numpy as jnp
from jax import lax
from jax.experimental import pallas as pl
from jax.experimental.pallas import tpu as pltpu

D_MODEL = 1024
D_FF = 2816
FF_CHUNK = 256
HEAD_DIM = 64
BLOCK = 128
N_BUCKETS = 32
MAX_DISTANCE = 2048
A_HEADS = 12
TOTAL_HEADS = 20
DIL_GROUPS = ((128, 1), (512, 4), (2048, 16))
B_WINDOW = 128
QKV_W = 3072
GATE_W = 2048
D_IN = QKV_W + GATE_W
EPS = 1e-6
NEG = -1e30
N_DEV = 8
N_CHIP = 4
ADAM_LR, ADAM_B1, ADAM_B2, ADAM_EPS, ADAM_WD, ADAM_STEP = 0.001, 0.9, 0.999, 1e-08, 0.01, 10
VMEM_LIMIT = 56 * 1024 * 1024
MESH = pl.DeviceIdType.MESH
BF16 = jnp.bfloat16
F32 = jnp.float32
ANY = pl.BlockSpec(memory_space=pl.ANY)


def _params(*sem):
    return pltpu.CompilerParams(dimension_semantics=sem, vmem_limit_bytes=VMEM_LIMIT)


def _resident(a):
    return pl.BlockSpec(a.shape, lambda i: (0, 0), pipeline_mode=pl.Buffered(1))


def _place():
    return lax.axis_index("x"), lax.axis_index("y"), lax.axis_index("c")


class _Gather:
    def __init__(self, shards):
        self.ins = list(shards)
        n = self.n = len(shards)
        self.out_shape = [jax.ShapeDtypeStruct((N_DEV,) + s.shape, s.dtype) for s in shards]
        self.scratch = [pltpu.SemaphoreType.DMA((7 * n,)), pltpu.SemaphoreType.DMA((7 * n,)),
                        pltpu.SemaphoreType.DMA((n,))]

    def _copies(self, ins, outs, sems):
        send_sems, recv_sems, local_sems = sems
        x, y, c = _place()
        me, sibling = (x, y, c), (x, y, 1 - c)
        chips = [(1 - x, y), (x, 1 - y), (1 - x, 1 - y)]

        def copy(i, k, block, to, src=None):
            dst = outs[i].at[4 * block[0] + 2 * block[1] + block[2]]
            return pltpu.make_async_remote_copy(
                src_ref=dst if src is None else src, dst_ref=dst, send_sem=send_sems.at[7 * i + k],
                recv_sem=recv_sems.at[7 * i + k], device_id=to, device_id_type=MESH)

        n = self.n
        south = c == 0
        relayed = (jnp.where(south, 1 - x, x), jnp.where(south, y, 1 - y), c)
        relay_to = (jnp.where(south, x, 1 - x), jnp.where(south, 1 - y, y), c)
        mine = [pltpu.make_async_copy(ins[i], outs[i].at[4 * x + 2 * y + c], local_sems.at[i]) for i in range(n)]
        first = [copy(i, 0, me, sibling, src=ins[i]) for i in range(n)]
        first += [copy(i, 1 + j, me, (*chips[j], c), src=ins[i]) for i in range(n) for j in range(2)]
        landed = [copy(i, 1 + j, (*chips[j], c), me) for j in range(2) for i in range(n)]
        passed = [copy(i, 4 + j, (*chips[j], c), sibling) for j in range(2) for i in range(n)]
        relays = [copy(i, 3, relayed, relay_to) for i in range(n)]
        diag_landed = [copy(i, 3, (*chips[2], c), me) for i in range(n)]
        diag_passed = [copy(i, 6, (*chips[2], c), sibling) for i in range(n)]
        from_sibling = [copy(i, 0, sibling, me) for i in range(n)]
        from_sibling += [copy(i, 4 + j, (*chip, 1 - c), me) for i in range(n) for j, chip in enumerate(chips)]
        return mine, first, landed, passed, relays, diag_landed, diag_passed, from_sibling

    def start(self, ins, outs, sems):
        mine, first = self._copies(ins, outs, sems)[:2]
        for cp in mine + first:
            cp.start()

    def mid(self, ins, outs, sems):
        _, _, landed, passed, relays, _, _, _ = self._copies(ins, outs, sems)
        for got in landed:
            got.wait_recv()
        for cp in relays + passed:
            cp.start()

    def end(self, ins, outs, sems):
        mine, first, _, passed, relays, diag_landed, diag_passed, from_sibling = self._copies(ins, outs, sems)
        for got, fwd in zip(diag_landed, diag_passed):
            got.wait_recv()
            fwd.start()
        for cp in from_sibling:
            cp.wait_recv()
        for cp in first + passed + relays + diag_passed:
            cp.wait_send()
        for cp in mine:
            cp.wait()


class _CoreExchange:
    def __init__(self, grads):
        self.ins = list(grads)
        n = self.n = len(grads)
        self.out_shape = [jax.ShapeDtypeStruct((N_CHIP, 1) + g.shape[2:], g.dtype) for g in grads]
        self.scratch = [pltpu.SemaphoreType.DMA((n,)), pltpu.SemaphoreType.DMA((n,))]

    def _copies(self, ins, outs, sems):
        x, y, c = _place()
        return [pltpu.make_async_remote_copy(
            src_ref=ins[i].at[:, pl.ds(1 - c, 1)], dst_ref=outs[i], send_sem=sems[0].at[i],
            recv_sem=sems[1].at[i], device_id=(x, y, 1 - c), device_id_type=MESH) for i in range(self.n)]

    def start(self, ins, outs, sems):
        for cp in self._copies(ins, outs, sems):
            cp.start()

    mid = None

    def end(self, ins, outs, sems):
        for cp in self._copies(ins, outs, sems):
            cp.wait()


class _ChipExchange:
    def __init__(self, parts):
        self.ins = list(parts)
        n = self.n = len(parts)
        self.out_shape = [jax.ShapeDtypeStruct(p.shape, p.dtype) for p in parts]
        self.scratch = [pltpu.SemaphoreType.DMA((3 * n,)), pltpu.SemaphoreType.DMA((3 * n,)),
                        pltpu.SemaphoreType.DMA((n,))]

    def _copies(self, ins, outs, sems):
        send_sems, recv_sems, local_sems = sems
        x, y, c = _place()
        my_chip = 2 * x + y
        copies = []
        for i in range(self.n):
            copies.append(pltpu.make_async_copy(ins[i].at[my_chip], outs[i].at[my_chip], local_sems.at[i]))
            for k, (qx, qy) in enumerate([(1 - x, y), (x, 1 - y), (1 - x, 1 - y)]):
                copies.append(pltpu.make_async_remote_copy(
                    src_ref=ins[i].at[2 * qx + qy], dst_ref=outs[i].at[my_chip], send_sem=send_sems.at[3 * i + k],
                    recv_sem=recv_sems.at[3 * i + k], device_id=(qx, qy, c), device_id_type=MESH))
        return copies

    def start(self, ins, outs, sems):
        for cp in self._copies(ins, outs, sems):
            cp.start()

    mid = None

    def end(self, ins, outs, sems):
        for cp in self._copies(ins, outs, sems):
            cp.wait()


def _call(body, *, name, grid, in_specs, out_specs, out_shape, args, scratch=(), sem=None, hosted=()):
    n_in, n_out, n_scr = len(in_specs), len(out_specs), len(scratch)
    x_in = [len(p.ins) for p in hosted]
    x_scr = [len(p.scratch) for p in hosted]
    steps = int(np.prod(grid))

    def wrapped(*refs):
        refs = list(refs)
        ins, refs = refs[:n_in], refs[n_in:]
        x_ins = [[refs.pop(0) for _ in range(k)] for k in x_in]
        outs, refs = refs[:n_out], refs[n_out:]
        x_outs = [[refs.pop(0) for _ in range(k)] for k in x_in]
        scr, refs = refs[:n_scr], refs[n_scr:]
        x_sems = [[refs.pop(0) for _ in range(k)] for k in x_scr]
        step = 0
        for d in range(len(grid)):
            step = step * grid[d] + pl.program_id(d)

        def phase(which, at):
            fns = [(getattr(p, which), a) for p, a in zip(hosted, zip(x_ins, x_outs, x_sems)) if getattr(p, which)]
            if fns:
                @pl.when(step == at)
                def _():
                    for fn, a in fns:
                        fn(*a)

        phase("start", 0)
        if steps > 1:
            phase("mid", (5 * steps) // 8)
        body(*ins, *outs, *scr)
        if steps == 1:
            phase("mid", 0)
        phase("end", steps - 1)

    results = pl.pallas_call(
        wrapped, name=name, grid=grid,
        in_specs=list(in_specs) + [ANY] * sum(x_in), out_specs=list(out_specs) + [ANY] * sum(x_in),
        out_shape=list(out_shape) + [s for p in hosted for s in p.out_shape],
        scratch_shapes=list(scratch) + [s for p in hosted for s in p.scratch],
        compiler_params=_params(*(("arbitrary",) * len(grid) if hosted else sem)),
    )(*args, *[a for p in hosted for a in p.ins])
    own, rest = list(results[:n_out]), list(results[n_out:])
    return own, [[rest.pop(0) for _ in range(k)] for k in x_in]


def _nt(a, b):
    return lax.dot_general(a, b, (((1,), (1,)), ((), ())), preferred_element_type=F32)


def _nn(a, b):
    return lax.dot_general(a, b, (((1,), (0,)), ((), ())), preferred_element_type=F32)


def _tn(a, b):
    return lax.dot_general(a, b, (((0,), (0,)), ((), ())), preferred_element_type=F32)


def _rms(x, g):
    r = lax.rsqrt(jnp.mean(x * x, axis=-1, keepdims=True) + EPS)
    return x * r, r


def _rms_bwd(dn, xhat, r, g):
    dg = jnp.sum(dn * xhat, axis=0, keepdims=True)
    dxh = dn * g
    dx = r * (dxh - xhat * jnp.mean(dxh * xhat, axis=-1, keepdims=True))
    return dx, dg


def _ffn_fwd(x, g, wg_t, wu_t, wd, name, hosted=(), head=None):
    t = x.shape[0]
    tm = 512

    def body(*refs):
        x_ref, g_ref, wg_ref, wu_ref, wd_ref = refs[:5]
        h_ref, n_ref, a_ref, b_ref, hff_ref = refs[-5 if head is None else -7:][:5]
        xhat, _ = _rms(x_ref[...], g_ref[...])
        n = (xhat * g_ref[...]).astype(BF16)
        n_ref[...] = n
        for c in range(0, D_FF, FF_CHUNK):
            cols = slice(c, c + FF_CHUNK)
            a = _nt(n, wg_ref[cols, :])
            b = _nt(n, wu_ref[cols, :])
            a_ref[:, cols] = a.astype(BF16)
            b_ref[:, cols] = b.astype(BF16)
            hff_ref[:, cols] = (a * jax.nn.sigmoid(a) * b).astype(BF16)
        h = x_ref[...] + 0.5 * _nn(hff_ref[...], wd_ref[...])
        if head is None:
            h_ref[...] = h
            return
        gf_ref, t_ref, loss_ref, dgf_ref = refs[5], refs[6], refs[-2], refs[-1]

        @pl.when(pl.program_id(0) == 0)
        def _():
            loss_ref[...] = jnp.zeros_like(loss_ref)
            dgf_ref[...] = jnp.zeros_like(dgf_ref)

        yhat, r = _rms(h, gf_ref[...])
        err = yhat * gf_ref[...] - t_ref[...]
        loss_ref[...] += 0.5 * jnp.sum(jnp.mean(err * err, axis=-1, keepdims=True), axis=0, keepdims=True)
        h_ref[...], dgf = _rms_bwd(err * (1.0 / D_MODEL), yhat, r, gf_ref[...])
        dgf_ref[...] += dgf

    row = pl.BlockSpec((tm, D_MODEL), lambda i: (i, 0))
    hid = pl.BlockSpec((tm, D_FF), lambda i: (i, 0))
    sds = jax.ShapeDtypeStruct
    in_specs = [row, _resident(g), _resident(wg_t), _resident(wu_t), _resident(wd)]
    out_specs = [row, row, hid, hid, hid]
    out_shape = [sds((t, D_MODEL), F32), sds((t, D_MODEL), BF16), sds((t, D_FF), BF16), sds((t, D_FF), BF16),
                 sds((t, D_FF), BF16)]
    args = (x, g, wg_t, wu_t, wd)
    if head is not None:
        in_specs += [_resident(head[0]), row]
        out_specs += [pl.BlockSpec((1, 128), lambda i: (0, 0)), pl.BlockSpec((1, D_MODEL), lambda i: (0, 0))]
        out_shape += [sds((1, 128), F32), sds((1, D_MODEL), F32)]
        args += tuple(head)
    return _call(body, name=name, grid=(t // tm,), in_specs=in_specs, out_specs=out_specs, out_shape=out_shape,
                 sem=("parallel",) if head is None else ("arbitrary",), args=args, hosted=hosted)


def _ffn_bwd(x, a_pre, b_pre, g, dh, wg_t, wu_t, wd, name):
    t = x.shape[0]
    tm = 512

    def body(x_ref, a_ref, b_ref, g_ref, dh_ref, wg_ref, wu_ref, wd_ref,
             dx_ref, dg_ref, da_ref, db_ref):
        @pl.when(pl.program_id(0) == 0)
        def _():
            dg_ref[...] = jnp.zeros_like(dg_ref)

        dhh = (0.5 * dh_ref[...]).astype(BF16)
        for c in range(0, D_FF, FF_CHUNK):
            cols = slice(c, c + FF_CHUNK)
            a = a_ref[:, cols].astype(F32)
            b = b_ref[:, cols].astype(F32)
            s = jax.nn.sigmoid(a)
            silu = a * s
            dhff = _nt(dhh, wd_ref[cols, :])
            da_ref[:, cols] = (dhff * b * (s * (1.0 + a * (1.0 - s)))).astype(BF16)
            db_ref[:, cols] = (dhff * silu).astype(BF16)
        dn = _nn(da_ref[...], wg_ref[...]) + _nn(db_ref[...], wu_ref[...])
        xhat, r = _rms(x_ref[...], g_ref[...])
        dx, dg = _rms_bwd(dn, xhat, r, g_ref[...])
        dx_ref[...] = dh_ref[...] + dx
        dg_ref[...] += dg

    row = pl.BlockSpec((tm, D_MODEL), lambda i: (i, 0))
    hid = pl.BlockSpec((tm, D_FF), lambda i: (i, 0))
    return pl.pallas_call(
        body, name=name, grid=(t // tm,),
        in_specs=[row, hid, hid, _resident(g), row, _resident(wg_t), _resident(wu_t), _resident(wd)],
        out_specs=[row, pl.BlockSpec((1, D_MODEL), lambda i: (0, 0)), hid, hid],
        out_shape=[jax.ShapeDtypeStruct((t, D_MODEL), F32), jax.ShapeDtypeStruct((1, D_MODEL), F32),
                   jax.ShapeDtypeStruct((t, D_FF), BF16), jax.ShapeDtypeStruct((t, D_FF), BF16)],
        compiler_params=_params("arbitrary"),
    )(x, a_pre, b_pre, g, dh, wg_t, wu_t, wd)


def _tn_matmul(a, b, rc, name, hosted=(), scale=None):
    t, r = a.shape
    c = b.shape[1]
    tk = min(t, 2048)

    def body(a_ref, b_ref, o_ref):
        @pl.when(pl.program_id(1) == 0)
        def _():
            o_ref[...] = jnp.zeros_like(o_ref)

        o_ref[...] += _tn(a_ref[...].astype(BF16), b_ref[...].astype(BF16))
        if scale is not None:
            @pl.when(pl.program_id(1) == t // tk - 1)
            def _():
                o_ref[...] *= scale

    (out,), got = _call(
        body, name=name, grid=(r // rc, t // tk),
        in_specs=[pl.BlockSpec((tk, rc), lambda i, k: (k, i)), pl.BlockSpec((tk, c), lambda i, k: (k, 0))],
        out_specs=[pl.BlockSpec((rc, c), lambda i, k: (i, 0))],
        out_shape=[jax.ShapeDtypeStruct((r, c), F32)],
        sem=("parallel", "arbitrary"), args=(a, b), hosted=hosted)
    return out, got


def _unfold(blk_ref, slab_ref, d):
    if d == 1:
        return blk_ref[...]
    n = blk_ref.shape[0]
    for r in range(d):
        for half in range(2):
            c0 = 256 * r + 128 * half
            slab_ref[half, pl.ds(r, n, stride=d), :] = blk_ref[:, c0:c0 + 128]
    return jnp.concatenate([slab_ref[0], slab_ref[1]], axis=1)


def _fold(x, slab_ref, out_ref, d):
    if d == 1:
        out_ref[...] = x.astype(out_ref.dtype)
        return
    n, w = out_ref.shape[0], x.shape[1]
    for part in range(w // 128):
        slab_ref[part] = x[:, 128 * part:128 * (part + 1)]
    for r in range(d):
        for part in range(w // 128):
            c0 = w * r + 128 * part
            out_ref[:, c0:c0 + 128] = slab_ref[part, pl.ds(r, n, stride=d), :].astype(out_ref.dtype)


DILATIONS = tuple(d for _, d in DIL_GROUPS)


PIECE_W = (256,) * 9 + (512, 128, 128, GATE_W)
PIECE_D = DILATIONS * 3 + (1, 1, 1, 1)


def _inproj_fwd(h, g, w_t, b_in, name, hosted=()):
    t = h.shape[0]
    tm, nc = 512, 256
    dilated = [(gi, d) for gi, d in enumerate(DILATIONS) if d > 1]

    def body(h_ref, g_ref, w_ref, b_ref, u_ref, zq_ref, zg_ref, *rest):
        zf_refs, slabs = rest[:len(dilated)], rest[len(dilated):]
        xhat, _ = _rms(h_ref[...], g_ref[...])
        u = (xhat * g_ref[...]).astype(BF16)
        u_ref[...] = u
        for c in range(D_IN // nc):
            z = _nt(u, w_ref[c * nc:(c + 1) * nc, :]) + b_ref[:, c * nc:(c + 1) * nc]
            if c < QKV_W // nc:
                zq_ref[:, c * nc:(c + 1) * nc] = z.astype(BF16)
            else:
                zg_ref[:, c * nc - QKV_W:(c + 1) * nc - QKV_W] = z.astype(BF16)
            part, gi = divmod(c, len(DILATIONS))
            for k, (gk, d) in enumerate(dilated):
                if part < 3 and gi == gk:
                    slab, n = slabs[3 * k + part], tm // d
                    slab[0] = z[:, :128]
                    slab[1] = z[:, 128:]
                    for r in range(d):
                        for half in range(2):
                            c0 = 768 * r + 256 * part + 128 * half
                            zf_refs[k][:, c0:c0 + 128] = slab[half, pl.ds(r, n, stride=d), :].astype(BF16)

    row = lambda w: pl.BlockSpec((tm, w), lambda i: (i, 0))
    full = _resident
    sds = jax.ShapeDtypeStruct
    return _call(
        body, name=name, grid=(t // tm,),
        in_specs=[row(D_MODEL), full(g), full(w_t), full(b_in)],
        out_specs=[row(D_MODEL), row(QKV_W), row(GATE_W)]
        + [pl.BlockSpec((tm // d, d * 768), lambda i: (i, 0)) for _, d in dilated],
        out_shape=[sds((t, D_MODEL), BF16), sds((t, QKV_W), BF16), sds((t, GATE_W), BF16)]
        + [sds((t // d, d * 768), BF16) for _, d in dilated],
        scratch=[pltpu.VMEM((2, tm, 128), F32)] * (3 * len(dilated)),
        sem=("parallel",), args=(h, g, w_t, b_in), hosted=hosted)


def _inproj_bwd(pieces, h, g, dh_res, w_t, name):
    t = h.shape[0]
    tm = 512
    npiece = len(PIECE_W)
    offs = np.concatenate([[0], np.cumsum(PIECE_W)]).tolist()

    def body(*refs):
        p_refs = refs[:npiece]
        h_ref, g_ref, dhr_ref, w_ref, dh_ref, dz_ref, db_ref, dg_ref = refs[npiece:npiece + 8]
        slabs = list(refs[npiece + 8:])
        i = pl.program_id(0)

        @pl.when(i == 0)
        def _():
            db_ref[...] = jnp.zeros_like(db_ref)
            dg_ref[...] = jnp.zeros_like(dg_ref)

        du = jnp.zeros((tm, D_MODEL), F32)
        for k in range(npiece):
            o, w = offs[k], PIECE_W[k]
            token_order = _unfold(p_refs[k], slabs.pop(), PIECE_D[k]).astype(BF16) if PIECE_D[k] > 1 else None
            for c0 in range(0, w, 512):
                cw = min(512, w - c0)
                pz = p_refs[k][:, c0:c0 + cw] if token_order is None else token_order
                dz_ref[:, o + c0:o + c0 + cw] = pz
                db_ref[:, o + c0:o + c0 + cw] += jnp.sum(pz.astype(F32), axis=0, keepdims=True)
                du = du + _nn(pz, w_ref[o + c0:o + c0 + cw, :])
        xhat, r = _rms(h_ref[...], g_ref[...])
        dx, dg = _rms_bwd(du, xhat, r, g_ref[...])
        dh_ref[...] = dhr_ref[...] + dx
        dg_ref[...] += dg

    row = lambda w: pl.BlockSpec((tm, w), lambda i: (i, 0))
    full = lambda shp: pl.BlockSpec(shp, lambda i: (0, 0))
    return pl.pallas_call(
        body, name=name, grid=(t // tm,),
        in_specs=[pl.BlockSpec((tm // d, d * w), lambda i: (i, 0)) for w, d in zip(PIECE_W, PIECE_D)]
        + [row(D_MODEL), _resident(g), row(D_MODEL), _resident(w_t)],
        out_specs=[row(D_MODEL), row(D_IN), full((1, D_IN)), full((1, D_MODEL))],
        out_shape=[jax.ShapeDtypeStruct((t, D_MODEL), F32), jax.ShapeDtypeStruct((t, D_IN), BF16),
                   jax.ShapeDtypeStruct((1, D_IN), F32), jax.ShapeDtypeStruct((1, D_MODEL), F32)],
        scratch_shapes=[pltpu.VMEM((2, tm, 128), F32)] * sum(d > 1 for d in PIECE_D),
        compiler_params=_params("arbitrary"),
    )(*pieces, h, g, dh_res, w_t)


def _t5_bucket(dist):
    max_exact = N_BUCKETS // 2
    n = jnp.maximum(dist, 0)
    nf = jnp.maximum(n, 1).astype(jnp.float32)
    large = max_exact + (jnp.log(nf / max_exact) / math.log(MAX_DISTANCE / max_exact)
                         * (N_BUCKETS - max_exact)).astype(jnp.int32)
    large = jnp.minimum(large, N_BUCKETS - 1)
    return jnp.where(n < max_exact, n, large)


ATT_CFG = ((1, 128, 0, 4), (4, 128, 4, 4), (16, 128, 8, 4), (1, B_WINDOW - 1, A_HEADS, 8))


def _bucket_tiles():
    qi = jnp.arange(BLOCK)[:, None]
    ki = jnp.arange(2 * BLOCK)[None, :]
    dist = qi + BLOCK - ki
    return jnp.stack([_t5_bucket(dist * cfg[0]) for cfg in ATT_CFG]).astype(jnp.int32)


def _band(max_steps):
    row = lax.broadcasted_iota(jnp.int32, (BLOCK, 2 * BLOCK), 0)
    col = lax.broadcasted_iota(jnp.int32, (BLOCK, 2 * BLOCK), 1)
    dist = row + BLOCK - col
    return (dist >= 0) & (dist <= max_steps)


def _bias_build(table, buckets, hosted=()):
    def body(tab_ref, bt_ref, out_ref):
        col = lax.broadcasted_iota(jnp.int32, (BLOCK, 2 * BLOCK), 1)
        for ci, (_, max_steps, h0, nh) in enumerate(ATT_CFG):
            bt = bt_ref[ci]
            band = _band(max_steps)
            for h in range(h0, h0 + nh):
                acc = lax.fori_loop(0, N_BUCKETS, lambda b, acc: jnp.where(bt == b, tab_ref[b, h], acc),
                                    jnp.zeros((BLOCK, 2 * BLOCK), F32))
                out_ref[0, BLOCK * h:BLOCK * (h + 1), :] = jnp.where(band & (col >= BLOCK), acc, NEG)
                out_ref[1, BLOCK * h:BLOCK * (h + 1), :] = jnp.where(band, acc, NEG)

    return _call(
        body, name="bias_build", grid=(1,),
        in_specs=[pl.BlockSpec(memory_space=pltpu.SMEM), pl.BlockSpec(memory_space=pltpu.VMEM)],
        out_specs=[pl.BlockSpec(memory_space=pltpu.VMEM)],
        out_shape=[jax.ShapeDtypeStruct((2, TOTAL_HEADS * BLOCK, 2 * BLOCK), F32)],
        sem=("arbitrary",), args=(table, buckets), hosted=hosted)


def _bias_reduce(dbias, buckets, dsink_rows, hosted=()):
    def body(db_ref, bt_ref, ds_ref, out_ref, sink_ref):
        ri = lax.broadcasted_iota(jnp.int32, (N_BUCKETS, 128), 0)
        ci = lax.broadcasted_iota(jnp.int32, (N_BUCKETS, 128), 1)

        def per_bucket(b, acc):
            for cfg_i, (_, _, h0, nh) in enumerate(ATT_CFG):
                hit = bt_ref[cfg_i] == b
                for h in range(h0, h0 + nh):
                    val = jnp.sum(jnp.where(hit, db_ref[BLOCK * h:BLOCK * (h + 1), :], 0.0))
                    acc = jnp.where((ri == b) & (ci == h), val, acc)
            return acc

        out_ref[...] = lax.fori_loop(0, N_BUCKETS, per_bucket, jnp.zeros((N_BUCKETS, 128), F32))
        for h in range(8):
            sink_ref[h:h + 1, :] = jnp.sum(ds_ref[BLOCK * h:BLOCK * (h + 1), :], axis=0, keepdims=True)

    return _call(
        body, name="bias_reduce", grid=(1,),
        in_specs=[pl.BlockSpec(memory_space=pltpu.VMEM)] * 3,
        out_specs=[pl.BlockSpec(memory_space=pltpu.VMEM)] * 2,
        out_shape=[jax.ShapeDtypeStruct((N_BUCKETS, 128), F32), jax.ShapeDtypeStruct((8, 128), F32)],
        sem=("arbitrary",), args=(dbias, buckets, dsink_rows), hosted=hosted)


class _Att:
    def __init__(self, cfg_i):
        stride, _, h0, nh = ATT_CFG[cfg_i]
        self.d = stride if cfg_i < 3 else 1
        self.h0, self.nh = h0, nh
        self.row_w = QKV_W if self.d == 1 else 3 * 256
        if cfg_i < 3:
            self.nq, self.wkv = 1, 256
            self.q_unit = [cfg_i if self.d == 1 else 0]
            self.k_unit, self.v_unit = (3 + cfg_i, 6 + cfg_i) if self.d == 1 else (1, 2)
            self.sinks = False
        else:
            self.nq, self.wkv = 2, 128
            self.q_unit = [9, 10]
            self.k_unit, self.v_unit = 22, 23
            self.sinks = True
        self.wq = 256 * self.nq


def _att_in_specs(cf, bsz):
    uq, ukv = cf.row_w // 256, cf.row_w // cf.wkv
    specs = [pl.BlockSpec((bsz, BLOCK, 256), functools.partial(lambda r, j, u: (0, j, r * uq + u), u=u))
             for u in cf.q_unit]
    for unit in (cf.k_unit, cf.v_unit):
        specs.append(pl.BlockSpec((bsz, BLOCK, cf.wkv),
                                  functools.partial(lambda r, j, u: (0, jnp.maximum(j - 1, 0), r * ukv + u), u=unit)))
        specs.append(pl.BlockSpec((bsz, BLOCK, cf.wkv),
                                  functools.partial(lambda r, j, u: (0, j, r * ukv + u), u=unit)))
    for qb in range(cf.nq):
        specs.append(pl.BlockSpec((None, HEADS_PER_BLOCK * BLOCK, 2 * BLOCK), functools.partial(
            lambda r, j, u: (jnp.minimum(j, 1), u, 0), u=cf.h0 // HEADS_PER_BLOCK + qb)))
    if cf.sinks:
        specs += [pl.BlockSpec((HEADS_PER_BLOCK * BLOCK, 128), functools.partial(lambda r, j, u: (u, 0), u=qb))
                  for qb in range(cf.nq)]
    return specs


HEADS_PER_BLOCK = 4


def _head_masks(rows):
    head = lax.broadcasted_iota(jnp.int32, (rows, 256), 1) // HEAD_DIM
    return [head == h for h in range(HEADS_PER_BLOCK)]


def _stack_heads(x, masks):
    return jnp.concatenate([jnp.where(m, x, jnp.zeros_like(x)) for m in masks], axis=0)


def _unstack_heads(x4, masks):
    blocks = [x4[BLOCK * h:BLOCK * (h + 1)] for h in range(HEADS_PER_BLOCK)]
    return jnp.where(masks[0], blocks[0], jnp.where(masks[1], blocks[1], jnp.where(masks[2], blocks[2], blocks[3])))


def _row_value(x):
    return jnp.max(x, axis=-1, keepdims=True)


def _kv_operands(cf, x):
    if cf.wkv == 256:
        return [x]
    lane = lax.broadcasted_iota(jnp.int32, x.shape, 1)
    swapped = pltpu.roll(x, HEAD_DIM, 1)
    halves = [jnp.where(lane < HEAD_DIM, x, swapped), jnp.where(lane < HEAD_DIM, swapped, x)]
    return [jnp.concatenate([half, half], axis=1) for half in halves]


def _kv_fold(cf, grads):
    if cf.wkv == 256:
        return grads[0]
    folded = []
    for g in grads:
        x = g[:, :128] + g[:, 128:]
        folded.append(x + pltpu.roll(x, HEAD_DIM, 1))
    lane = lax.broadcasted_iota(jnp.int32, folded[0].shape, 1)
    return jnp.where(lane < HEAD_DIM, folded[0], folded[1])


def _attn_fwd(cf, zf, bias, sinks, name, hosted=()):
    bsz, l, _ = zf.shape
    nb = l // BLOCK

    def body(*refs):
        refs = list(refs)
        q_refs = [refs.pop(0) for _ in range(cf.nq)]
        kp_ref, kc_ref, vp_ref, vc_ref = [refs.pop(0) for _ in range(4)]
        bias_refs = [refs.pop(0) for _ in range(cf.nq)]
        sink_refs = [refs.pop(0) for _ in range(cf.nq)] if cf.sinks else None
        o_ref, lse_ref = refs
        masks, kv_masks = _head_masks(BLOCK), _head_masks(2 * BLOCK)
        sinks4 = [_row_value(ref[...]) for ref in sink_refs] if cf.sinks else None
        for bi in range(bsz):
            k = jnp.concatenate([kp_ref[bi], kc_ref[bi]], axis=0)
            v = jnp.concatenate([vp_ref[bi], vc_ref[bi]], axis=0)
            k_ops, v_ops = _kv_operands(cf, k), _kv_operands(cf, v)
            for qb in range(cf.nq):
                cols = slice(256 * qb, 256 * (qb + 1))
                kb, vb = k_ops[qb], v_ops[qb]
                q4 = _stack_heads(q_refs[qb][bi] * (HEAD_DIM ** -0.5), masks)
                s = _nt(q4, kb) + bias_refs[qb][...]
                m = jnp.max(s, axis=-1, keepdims=True)
                if cf.sinks:
                    sk = sinks4[qb]
                    m = jnp.maximum(m, sk)
                p = jnp.exp(s - m)
                den = jnp.sum(p, axis=-1, keepdims=True)
                if cf.sinks:
                    den = den + jnp.exp(sk - m)
                pn = (p * (1.0 / den)).astype(BF16)
                p_lanes = jnp.concatenate([pn[BLOCK * h:BLOCK * (h + 1)] for h in range(HEADS_PER_BLOCK)], axis=1)
                v4 = jnp.concatenate([jnp.where(mk, vb, jnp.zeros_like(vb)) for mk in kv_masks], axis=0)
                o_ref[bi, :, cols] = _nn(p_lanes, v4)
                lse_ref[bi, :, cols] = _unstack_heads(
                    jnp.broadcast_to(m + jnp.log(den), (HEADS_PER_BLOCK * BLOCK, 256)), masks)

    in_specs = _att_in_specs(cf, bsz)
    args = [zf] * (cf.nq + 4) + [bias] * cf.nq + ([sinks] * cf.nq if cf.sinks else [])
    out = pl.BlockSpec((bsz, BLOCK, cf.wq), lambda r, j: (0, j, r))
    shape = jax.ShapeDtypeStruct((bsz, l, cf.d * cf.wq), F32)
    return _call(
        body, name=name, grid=(cf.d, nb), in_specs=in_specs, out_specs=[out, out], out_shape=[shape, shape],
        sem=("parallel", "arbitrary"), args=args, hosted=hosted)


def _attn_bwd(cf, zf, bias, sinks, stats, dy, name, hosted=()):
    bsz, l, _ = zf.shape
    nb = l // BLOCK

    def body(*refs):
        refs = list(refs)
        q_refs = [refs.pop(0) for _ in range(cf.nq)]
        kp_ref, kc_ref, vp_ref, vc_ref = [refs.pop(0) for _ in range(4)]
        bias_refs = [refs.pop(0) for _ in range(cf.nq)]
        sink_refs = [refs.pop(0) for _ in range(cf.nq)] if cf.sinks else None
        st_ref, dy_ref, dq_ref, dk_ref, dv_ref, dbias_ref = [refs.pop(0) for _ in range(6)]
        dsink_ref = refs.pop(0) if cf.sinks else None
        dk_acc, dv_acc = refs
        r, j = pl.program_id(0), pl.program_id(1)

        @pl.when((r == 0) & (j == 0))
        def _():
            dbias_ref[...] = jnp.zeros_like(dbias_ref)
            if cf.sinks:
                dsink_ref[...] = jnp.zeros_like(dsink_ref)

        @pl.when(j == 0)
        def _():
            dk_acc[...] = jnp.zeros_like(dk_acc)
            dv_acc[...] = jnp.zeros_like(dv_acc)

        masks = _head_masks(BLOCK)
        cur = pl.ds(pl.multiple_of(j * BLOCK, BLOCK), BLOCK)
        prev = pl.ds(pl.multiple_of(jnp.maximum(j - 1, 0) * BLOCK, BLOCK), BLOCK)
        sinks4 = [_row_value(ref[...]) for ref in sink_refs] if cf.sinks else None
        ds_sum, dsink_sum = [None] * cf.nq, [None] * cf.nq
        for bi in range(bsz):
            k = jnp.concatenate([kp_ref[bi], kc_ref[bi]], axis=0)
            v = jnp.concatenate([vp_ref[bi], vc_ref[bi]], axis=0)
            dk_blocks, dv_blocks = [], []
            k_ops, v_ops = _kv_operands(cf, k), _kv_operands(cf, v)
            for qb in range(cf.nq):
                cols = slice(256 * qb, 256 * (qb + 1))
                kb, vb = k_ops[qb], v_ops[qb]
                q4 = _stack_heads(q_refs[qb][bi] * (HEAD_DIM ** -0.5), masks)
                st = st_ref[bi, :, 128 * qb:128 * (qb + 1)]
                lt4, e4 = _read_stats(st, 0), _read_stats(st, 1)
                pa = jnp.exp(_nt(q4, kb) + bias_refs[qb][...] - lt4)
                dy = dy_ref[bi, :, cols]
                dy4 = _stack_heads(dy, masks)
                ds = pa * (_nt(dy4, vb) - e4)
                ds_sum[qb] = ds if ds_sum[qb] is None else ds_sum[qb] + ds
                if cf.sinks:
                    dsk = jnp.exp(sinks4[qb] - lt4) * e4
                    dsink_sum[qb] = dsk if dsink_sum[qb] is None else dsink_sum[qb] + dsk
                dsb = ds.astype(BF16)
                dq_ref[bi, :, cols] = (_unstack_heads(_nn(dsb, kb), masks) * (HEAD_DIM ** -0.5)).astype(dq_ref.dtype)
                dk_blocks.append(_tn(dsb, q4))
                dv_blocks.append(_tn(pa.astype(BF16), dy4))
            dk_new, dv_new = _kv_fold(cf, dk_blocks), _kv_fold(cf, dv_blocks)
            dk_acc[bi, cur, :] += dk_new[BLOCK:]
            dv_acc[bi, cur, :] += dv_new[BLOCK:]
            dk_acc[bi, prev, :] += dk_new[:BLOCK]
            dv_acc[bi, prev, :] += dv_new[:BLOCK]
        for qb in range(cf.nq):
            rows = slice(HEADS_PER_BLOCK * BLOCK * qb, HEADS_PER_BLOCK * BLOCK * (qb + 1))
            dbias_ref[rows, :] += ds_sum[qb]
            if cf.sinks:
                dsink_ref[rows, :] -= dsink_sum[qb]

        @pl.when(j == nb - 1)
        def _():
            dk_ref[...] = dk_acc[...].astype(dk_ref.dtype)
            dv_ref[...] = dv_acc[...].astype(dv_ref.dtype)

    tok = pl.BlockSpec((bsz, BLOCK, cf.wq), lambda r, j: (0, j, r))
    in_specs = _att_in_specs(cf, bsz) + [pl.BlockSpec((bsz, BLOCK, cf.wq // 2), lambda r, j: (0, j, r)), tok]
    args = [zf] * (cf.nq + 4) + [bias] * cf.nq + ([sinks] * cf.nq if cf.sinks else []) + [stats, dy]
    seq = pl.BlockSpec((bsz, l, cf.wkv), lambda r, j: (0, 0, r))
    out_specs = [tok, seq, seq, pl.BlockSpec((cf.nh * BLOCK, 2 * BLOCK), lambda r, j: (0, 0))]
    grad_dtype = BF16 if cf.d == 1 else F32
    out_shape = [jax.ShapeDtypeStruct((bsz, l, cf.d * cf.wq), grad_dtype),
                 jax.ShapeDtypeStruct((bsz, l, cf.d * cf.wkv), grad_dtype),
                 jax.ShapeDtypeStruct((bsz, l, cf.d * cf.wkv), grad_dtype),
                 jax.ShapeDtypeStruct((cf.nh * BLOCK, 2 * BLOCK), F32)]
    if cf.sinks:
        out_specs.append(pl.BlockSpec((cf.nh * BLOCK, 128), lambda r, j: (0, 0)))
        out_shape.append(jax.ShapeDtypeStruct((cf.nh * BLOCK, 128), F32))
    return _call(
        body, name=name, grid=(cf.d, nb), in_specs=in_specs, out_specs=out_specs, out_shape=out_shape,
        scratch=[pltpu.VMEM((bsz, l, cf.wkv), F32), pltpu.VMEM((bsz, l, cf.wkv), F32)],
        sem=("arbitrary", "arbitrary"), args=args, hosted=hosted)


def _merge_fwd(o_a, lse_a, o_b, zg, h, wa_t, wb_t, wout, name, hosted=()):
    t = h.shape[0]
    tm = 512

    def body(o1, o2, o3, l1, l2, l3, ob_ref, zg_ref, h_ref, wa_ref, wb_ref, wo_ref, h2_ref, ya_ref, lt_ref, *slabs):
        o = [_unfold(ref, slabs[i], d) for i, (ref, d) in enumerate(zip((o1, o2, o3), DILATIONS))]
        l = [_unfold(ref, slabs[3 + i], d) for i, (ref, d) in enumerate(zip((l1, l2, l3), DILATIONS))]
        m = jnp.maximum(jnp.maximum(l[0], l[1]), l[2])
        e1, e2, e3 = jnp.exp(l[0] - m), jnp.exp(l[1] - m), jnp.exp(l[2] - m)
        se = e1 + e2 + e3
        ya = (e1 / se) * o[0] + (e2 / se) * o[1] + (e3 / se) * o[2]
        ya_ref[...] = ya
        lt_ref[...] = m + jnp.log(se)
        pa = _nt(ya.astype(BF16), wa_ref[...])
        pb = _nt(ob_ref[...].astype(BF16), wb_ref[...])
        merged = (jax.nn.sigmoid(zg_ref[:, :D_MODEL].astype(F32)) * pa
                  + jax.nn.sigmoid(zg_ref[:, D_MODEL:].astype(F32)) * pb)
        h2_ref[...] = h_ref[...] + _nn(merged.astype(BF16), wo_ref[...])

    row = lambda w: pl.BlockSpec((tm, w), lambda i: (i, 0))
    folded = [pl.BlockSpec((tm // d, d * 256), lambda i: (i, 0)) for d in DILATIONS]
    full = _resident
    return _call(
        body, name=name, grid=(t // tm,),
        in_specs=folded + folded + [row(512), row(GATE_W), row(D_MODEL), full(wa_t), full(wb_t), full(wout)],
        out_specs=[row(D_MODEL), row(256), row(256)],
        out_shape=[jax.ShapeDtypeStruct((t, D_MODEL), F32), jax.ShapeDtypeStruct((t, 256), F32),
                   jax.ShapeDtypeStruct((t, 256), F32)],
        scratch=[pltpu.VMEM((2, tm, 128), F32)] * 6,
        sem=("parallel",), args=(*o_a, *lse_a, o_b, zg, h, wa_t, wb_t, wout), hosted=hosted)


STAT_LANES = 32


def _head_stats(lt, dy, y):
    rows, w = dy.shape
    lane = lax.broadcasted_iota(jnp.int32, (rows, w), 1)
    out_lane = lax.broadcasted_iota(jnp.int32, (rows, w // 2), 1)
    prod = dy * y
    out = jnp.zeros((rows, w // 2), F32)
    for hd in range(w // HEAD_DIM):
        mine = lane // HEAD_DIM == hd
        dot = jnp.sum(jnp.where(mine, prod, 0.0), axis=-1, keepdims=True)
        both = jnp.where(out_lane % STAT_LANES < STAT_LANES // 2, _row_value(jnp.where(mine, lt, NEG)), dot)
        out = jnp.where(out_lane // STAT_LANES == hd, both, out)
    return out


def _read_stats(st, part):
    lane = lax.broadcasted_iota(jnp.int32, st.shape, 1)
    half = (lane % STAT_LANES < STAT_LANES // 2) == (part == 0)
    return jnp.concatenate([_row_value(jnp.where((lane // STAT_LANES == hd) & half, st, NEG))
                            for hd in range(HEADS_PER_BLOCK)], axis=0)


def _merge_bwd(dh, ya, lse_tot, o_b, lse_b, zg, wa_t, wb_t, wout, name, hosted=()):
    t = dh.shape[0]
    tm = 512

    def body(dh_ref, ya_ref, lt_ref, ob_ref, lb_ref, zg_ref, wa_ref, wb_ref, wo_ref,
             mg_ref, dpa_ref, dpb_ref, dzg_ref, dy1, dy2, dy3, st1, st2, st3, dyb_ref, stb_ref, *slabs):
        dm = _nt(dh_ref[...].astype(BF16), wo_ref[...])
        pa = _nt(ya_ref[...].astype(BF16), wa_ref[...])
        pb = _nt(ob_ref[...].astype(BF16), wb_ref[...])
        sa = jax.nn.sigmoid(zg_ref[:, :D_MODEL].astype(F32))
        sb = jax.nn.sigmoid(zg_ref[:, D_MODEL:].astype(F32))
        mg_ref[...] = (sa * pa + sb * pb).astype(BF16)
        dpa = (dm * sa).astype(BF16)
        dpb = (dm * sb).astype(BF16)
        dpa_ref[...] = dpa
        dpb_ref[...] = dpb
        dzg_ref[:, :D_MODEL] = (dm * pa * (sa * (1.0 - sa))).astype(BF16)
        dzg_ref[:, D_MODEL:] = (dm * pb * (sb * (1.0 - sb))).astype(BF16)
        dya = _nn(dpa, wa_ref[...])
        dyb = _nn(dpb, wb_ref[...])
        stats = _head_stats(lt_ref[...], dya, ya_ref[...])
        for i, d in enumerate(DILATIONS):
            _fold(dya, slabs[i], (dy1, dy2, dy3)[i], d)
            _fold(stats, slabs[3 + i], (st1, st2, st3)[i], d)
        dyb_ref[...] = dyb.astype(BF16)
        stb_ref[...] = _head_stats(lb_ref[...], dyb, ob_ref[...])

    row = lambda w: pl.BlockSpec((tm, w), lambda i: (i, 0))
    folded = [pl.BlockSpec((tm // d, d * 256), lambda i: (i, 0)) for d in DILATIONS]
    full = _resident
    sds = jax.ShapeDtypeStruct
    dy_shape = [sds((t // d, d * 256), BF16) for d in DILATIONS]
    st_shape = [sds((t // d, d * 128), F32) for d in DILATIONS]
    st_specs = [pl.BlockSpec((tm // d, d * 128), lambda i: (i, 0)) for d in DILATIONS]
    return _call(
        body, name=name, grid=(t // tm,),
        in_specs=[row(D_MODEL), row(256), row(256), row(512), row(512), row(GATE_W), full(wa_t), full(wb_t),
                  full(wout)],
        out_specs=[row(D_MODEL), row(D_MODEL), row(D_MODEL), row(GATE_W)] + folded + st_specs + [row(512), row(256)],
        out_shape=[sds((t, D_MODEL), BF16), sds((t, D_MODEL), BF16), sds((t, D_MODEL), BF16),
                   sds((t, GATE_W), BF16)] + dy_shape + st_shape + [sds((t, 512), BF16), sds((t, 256), F32)],
        scratch=[pltpu.VMEM((2, tm, 128), F32)] * 6,
        sem=("parallel",), args=(dh, ya, lse_tot, o_b, lse_b, zg, wa_t, wb_t, wout), hosted=hosted)


def _pair_sum(grad, got, name):
    _, _, r, cdim = grad.shape
    core = lax.axis_index("c").astype(jnp.int32).reshape(1)

    def body(core_ref, g_ref, s_ref, o_ref):
        o_ref[...] = (g_ref[...] + s_ref[...]).astype(BF16)

    return pl.pallas_call(
        body, name=name,
        grid_spec=pltpu.PrefetchScalarGridSpec(
            num_scalar_prefetch=1, grid=(N_CHIP,),
            in_specs=[pl.BlockSpec((None, None, r, cdim), lambda q, core_ref: (q, core_ref[0], 0, 0)),
                      pl.BlockSpec((None, None, r, cdim), lambda q, core_ref: (q, 0, 0, 0))],
            out_specs=pl.BlockSpec((None, r, cdim), lambda q, core_ref: (q, 0, 0))),
        out_shape=jax.ShapeDtypeStruct((N_CHIP, r, cdim), BF16),
        compiler_params=_params("parallel"),
    )(core, grad, got)


def _adamw_math(w, g, m, v):
    m = ADAM_B1 * m + (1.0 - ADAM_B1) * g
    v = ADAM_B2 * v + (1.0 - ADAM_B2) * jnp.square(g)
    m_hat = m / (1.0 - ADAM_B1 ** ADAM_STEP)
    v_hat = v / (1.0 - ADAM_B2 ** ADAM_STEP)
    delta = -ADAM_LR * (m_hat / (jnp.sqrt(v_hat) + ADAM_EPS) + ADAM_WD * w)
    return delta, m, v


def _update(w, parts, m, v, transposed, name, hosted=()):
    r, c = parts.shape[1:]

    def body(w_ref, p_ref, m_ref, v_ref, g_ref, d_ref, m2_ref, v2_ref):
        def total(rows):
            return ((p_ref[0, rows].astype(F32) + p_ref[1, rows].astype(F32)) + p_ref[2, rows].astype(F32)) \
                + p_ref[3, rows].astype(F32)

        def update(at, g):
            g_ref[at] = g
            d_ref[at], m2_ref[at], v2_ref[at] = _adamw_math(w_ref[at], g, m_ref[at], v_ref[at])

        if not transposed:
            update((slice(None), slice(None)), total(slice(None)))
            return
        for r0 in range(0, r, 128):
            n = min(128, r - r0)
            gt = total(slice(r0, r0 + n))
            if n < 128:
                gt = jnp.concatenate([gt, jnp.zeros((128 - n, c), F32)], axis=0)
            update((slice(None), slice(r0, r0 + n)), gt.T[:, :n])

    sds = jax.ShapeDtypeStruct(w.shape, F32)
    vm = pl.BlockSpec(memory_space=pltpu.VMEM)
    return _call(body, name=name, grid=(1,), in_specs=[vm] * 4, out_specs=[vm] * 4, out_shape=[sds] * 4,
                 sem=("arbitrary",), args=(w, parts, m, v), hosted=hosted)


SMALL_ROWS = 80


def _small_update(g, w, m, v, name):
    def body(g_ref, w_ref, m_ref, v_ref, gs_ref, d_ref, m2_ref, v2_ref, got_ref, send_sems, recv_sems):
        x, y, c = _place()
        me = 4 * x + 2 * y + c
        got_ref[me] = g_ref[...]
        copies = []
        for k in range(1, N_DEV):
            peer = (x ^ (k >> 2), y ^ ((k >> 1) & 1), c ^ (k & 1))
            cp = pltpu.make_async_remote_copy(
                src_ref=g_ref, dst_ref=got_ref.at[me], send_sem=send_sems.at[k - 1], recv_sem=recv_sems.at[k - 1],
                device_id=peer, device_id_type=MESH)
            cp.start()
            copies.append(cp)
        for cp in copies:
            cp.wait()
        total = got_ref[0]
        for k in range(1, N_DEV):
            total = total + got_ref[k]
        gs_ref[...] = total
        d_ref[...], m2_ref[...], v2_ref[...] = _adamw_math(w_ref[...], total, m_ref[...], v_ref[...])

    sds = jax.ShapeDtypeStruct((SMALL_ROWS, 128), F32)
    vm = pl.BlockSpec(memory_space=pltpu.VMEM)
    return pl.pallas_call(
        body, name=name, in_specs=[vm] * 4, out_specs=[vm] * 4, out_shape=[sds] * 4,
        scratch_shapes=[pltpu.VMEM((N_DEV, SMALL_ROWS, 128), F32), pltpu.SemaphoreType.DMA((N_DEV - 1,)),
                        pltpu.SemaphoreType.DMA((N_DEV - 1,))],
    )(g, w, m, v)


def _pack_small(gains, b_in, rel_bias, sinks, last):
    rows = [a.reshape(8, 128) for a in gains] + [b_in.reshape(40, 128), rel_bias.reshape(5, 128),
                                                 jnp.pad(sinks.reshape(1, 8), ((0, 0), (0, 120))), last]
    rows.append(jnp.zeros((SMALL_ROWS - 79, 128), F32))
    return jnp.concatenate(rows, axis=0)


def _unpack_small(p, like):
    out = [p[8 * i:8 * i + 8].reshape(like[i].shape) for i in range(4)]
    out.append(p[32:72].reshape(like[4].shape))
    out.append(p[72:77].reshape(like[5].shape))
    out.append(p[77, :8].reshape(like[6].shape))
    return out


def kernel(x, ffn1_norm, ffn1_w_gate, ffn1_w_up, ffn1_w_down, mix_norm, w_in, b_in, w_branch_a, w_branch_b, w_out, sinks, rel_bias, ffn2_norm, ffn2_w_gate, ffn2_w_up, ffn2_w_down, final_norm, loss_target, m_ffn1_norm, m_ffn1_w_gate, m_ffn1_w_up, m_ffn1_w_down, m_mix_norm, m_w_in, m_b_in, m_w_branch_a, m_w_branch_b, m_w_out, m_sinks, m_rel_bias, m_ffn2_norm, m_ffn2_w_gate, m_ffn2_w_up, m_ffn2_w_down, m_final_norm, v_ffn1_norm, v_ffn1_w_gate, v_ffn1_w_up, v_ffn1_w_down, v_mix_norm, v_w_in, v_b_in, v_w_branch_a, v_w_branch_b, v_w_out, v_sinks, v_rel_bias, v_ffn2_norm, v_ffn2_w_gate, v_ffn2_w_up, v_ffn2_w_down, v_final_norm):
    bsz, seq, _ = x.shape
    t = bsz * seq
    xt = x.reshape(t, D_MODEL)
    target = loss_target.reshape(t, D_MODEL)

    big = [("ffn1_w_gate", ffn1_w_gate, m_ffn1_w_gate, v_ffn1_w_gate, True),
           ("ffn1_w_up", ffn1_w_up, m_ffn1_w_up, v_ffn1_w_up, True),
           ("ffn1_w_down", ffn1_w_down, m_ffn1_w_down, v_ffn1_w_down, False),
           ("w_in", w_in, m_w_in, v_w_in, True),
           ("w_branch_a", w_branch_a, m_w_branch_a, v_w_branch_a, True),
           ("w_branch_b", w_branch_b, m_w_branch_b, v_w_branch_b, True),
           ("w_out", w_out, m_w_out, v_w_out, False),
           ("ffn2_w_gate", ffn2_w_gate, m_ffn2_w_gate, v_ffn2_w_gate, True),
           ("ffn2_w_up", ffn2_w_up, m_ffn2_w_up, v_ffn2_w_up, True),
           ("ffn2_w_down", ffn2_w_down, m_ffn2_w_down, v_ffn2_w_down, False)]
    shard = {nm: (w[0].T if tr else w[0]).astype(BF16) for nm, w, _, _, tr in big}
    full = {}

    def gather(names):
        return _Gather([shard[nm] for nm in names])

    def keep(names, got):
        for nm, gw in zip(names, got):
            full[nm] = gw.reshape(-1, gw.shape[-1])

    ffn1_names = ["ffn1_w_gate", "ffn1_w_up", "ffn1_w_down"]
    mix_names = ["w_in", "w_branch_a", "w_branch_b", "w_out"]
    ffn2_names = ["ffn2_w_gate", "ffn2_w_up", "ffn2_w_down"]
    g1, gm, g2, gf = ffn1_norm, mix_norm, ffn2_norm, final_norm.reshape(1, D_MODEL)

    buckets = _bucket_tiles()
    (bias,), (got,) = _bias_build(rel_bias, buckets, hosted=[gather(ffn1_names)])
    keep(ffn1_names, got)
    (h1, n1, a1, b1, hff1), (got,) = _ffn_fwd(xt, g1, full["ffn1_w_gate"], full["ffn1_w_up"], full["ffn1_w_down"], "ffn1_fwd",
                                hosted=[gather(mix_names + ffn2_names)])
    keep(mix_names + ffn2_names, got)
    (u, zq, zg, *zdil), _ = _inproj_fwd(h1, gm, full["w_in"], b_in, "inproj_fwd")
    sink_rows = jnp.broadcast_to(sinks.reshape(8, 1, 1), (8, BLOCK, 128)).reshape(8 * BLOCK, 128)
    cfs = [_Att(i) for i in range(4)]
    zfold = []
    for i, cf in enumerate(cfs):
        if cf.d == 1:
            zfold.append(zq.reshape(bsz, seq, QKV_W))
        else:
            zfold.append(zdil[i - 1].reshape(bsz, seq // cf.d, cf.d * cf.row_w))
    att = [None] * 4
    for i in (3, 0, 1, 2):
        cf = cfs[i]
        (o, lse), _ = _attn_fwd(cf, zfold[i], bias, sink_rows, f"attn{i}_fwd")
        att[i] = (o.reshape(t // cf.d, cf.d * cf.wq), lse.reshape(t // cf.d, cf.d * cf.wq))
    o_b, lse_b = att[3]
    (h2, ya, lse_tot), _ = _merge_fwd([a[0] for a in att[:3]], [a[1] for a in att[:3]], o_b, zg, h1,
                                      full["w_branch_a"], full["w_branch_b"], full["w_out"], "merge_fwd")
    (dh3, n2, a2, b2, hff2, loss_part, dgf), _ = _ffn_fwd(
        h2, g2, full["ffn2_w_gate"], full["ffn2_w_up"], full["ffn2_w_down"], "ffn2_fwd", head=(gf, target))

    grads, pair, from_chips = {}, {}, {}

    def by_owner(nm):
        return grads[nm].reshape(N_CHIP, 2, -1, grads[nm].shape[-1])

    def to_core(names):
        return _CoreExchange([by_owner(nm) for nm in names])

    def pair_up(names, got):
        for nm, sib in zip(names, got):
            pair[nm] = _pair_sum(by_owner(nm), sib, f"pair_sum_{nm}")

    def to_chips(names):
        return _ChipExchange([pair[nm] for nm in names])

    def landed(names, got):
        for nm, parts in zip(names, got):
            from_chips[nm] = parts

    dh2, dg2, da, db = _ffn_bwd(h2, a2, b2, g2, dh3, full["ffn2_w_gate"], full["ffn2_w_up"],
                                          full["ffn2_w_down"], "ffn2_bwd")
    grads["ffn2_w_gate"], _ = _tn_matmul(da, n2, 1408, "ffn2_dgate")
    grads["ffn2_w_up"], _ = _tn_matmul(db, n2, 1408, "ffn2_dup")
    grads["ffn2_w_down"], _ = _tn_matmul(hff2, dh3, 1408, "ffn2_ddown", scale=0.5)
    (merged, dpa, dpb, dzg, *cot), (got,) = _merge_bwd(
        dh2, ya, lse_tot, o_b, lse_b, zg, full["w_branch_a"], full["w_branch_b"], full["w_out"], "merge_bwd",
        hosted=[to_core(ffn2_names)])
    dys, sts = cot[0:3] + [cot[6]], cot[3:6] + [cot[7]]
    pair_up(ffn2_names, got)
    dq, dk, dv, dbias, dsink = [None] * 4, [None] * 4, [None] * 4, [None] * 4, None
    for i in (3, 0, 1, 2):
        cf = cfs[i]
        shp = (bsz, seq // cf.d, cf.d * cf.wq)
        hosted = [to_chips(ffn2_names)] if i == 3 else ()
        res, got = _attn_bwd(cf, zfold[i], bias, sink_rows, sts[i].reshape(shp[:2] + (shp[2] // 2,)), dys[i].reshape(shp),
                             f"attn{i}_bwd", hosted=hosted)
        if i == 3:
            landed(ffn2_names, got[0])
        dq[i] = res[0].reshape(t // cf.d, cf.d * cf.wq)
        dk[i] = res[1].reshape(t // cf.d, cf.d * cf.wkv)
        dv[i] = res[2].reshape(t // cf.d, cf.d * cf.wkv)
        dbias[i] = res[3]
        if cf.sinks:
            dsink = res[4]
    pieces = dq[:3] + dk[:3] + dv[:3] + [dq[3], dk[3], dv[3], dzg]
    dh1, dz, db_in, dgm = _inproj_bwd(pieces, h1, gm, dh2, full["w_in"], "inproj_bwd")
    dx, dg1, da, db = _ffn_bwd(xt, a1, b1, g1, dh1, full["ffn1_w_gate"], full["ffn1_w_up"],
                                         full["ffn1_w_down"], "ffn1_bwd")
    grads["w_in"], _ = _tn_matmul(dz, u, 1280, "dw_in")
    grads["ffn1_w_down"], (got,) = _tn_matmul(hff1, dh1, 1408, "ffn1_ddown", hosted=[to_core(["w_in"])], scale=0.5)
    pair_up(["w_in"], got)
    grads["ffn1_w_gate"], got = _tn_matmul(da, n1, 1408, "ffn1_dgate",
                                           hosted=[to_chips(["w_in"]), to_core(["ffn1_w_down"])])
    landed(["w_in"], got[0])
    pair_up(["ffn1_w_down"], got[1])
    grads["ffn1_w_up"], got = _tn_matmul(db, n1, 1408, "ffn1_dup",
                                         hosted=[to_chips(["ffn1_w_down"]), to_core(["ffn1_w_gate"])])
    landed(["ffn1_w_down"], got[0])
    pair_up(["ffn1_w_gate"], got[1])
    grads["w_out"], got = _tn_matmul(merged, dh2, 1024, "dw_out",
                                     hosted=[to_chips(["ffn1_w_gate"]), to_core(["ffn1_w_up"])])
    landed(["ffn1_w_gate"], got[0])
    pair_up(["ffn1_w_up"], got[1])
    grads["w_branch_b"], got = _tn_matmul(dpb, o_b, 1024, "dw_branch_b",
                                          hosted=[to_chips(["ffn1_w_up"]), to_core(["w_out"])])
    landed(["ffn1_w_up"], got[0])
    pair_up(["w_out"], got[1])
    grads["w_branch_a"], got = _tn_matmul(dpa, ya, 1024, "dw_branch_a",
                                          hosted=[to_chips(["w_out"]), to_core(["w_branch_b"])])
    landed(["w_out"], got[0])
    pair_up(["w_branch_b"], got[1])
    (dtable, dsinks), got = _bias_reduce(jnp.concatenate(dbias, axis=0), buckets, dsink,
                                         hosted=[to_chips(["w_branch_b"]), to_core(["w_branch_a"])])
    landed(["w_branch_b"], got[0])
    pair_up(["w_branch_a"], got[1])

    out_g, out_d, out_m, out_v = {}, {}, {}, {}
    for idx, (nm, w, m, v, tr) in enumerate(big):
        around = tr and w.shape[-1] % 128 != 0
        wmv = [a[0].T if around else a[0] for a in (w, m, v)]
        res, got = _update(wmv[0], from_chips[nm], wmv[1], wmv[2], tr and not around, f"update_{nm}",
                           hosted=[to_chips(["w_branch_a"])] if idx == 0 else ())
        if idx == 0:
            landed(["w_branch_a"], got[0])
        out_g[nm], out_d[nm], out_m[nm], out_v[nm] = [(a.T if around else a)[None] for a in res]

    small = [("ffn1_norm", ffn1_norm, m_ffn1_norm, v_ffn1_norm), ("mix_norm", mix_norm, m_mix_norm, v_mix_norm),
             ("ffn2_norm", ffn2_norm, m_ffn2_norm, v_ffn2_norm), ("final_norm", final_norm, m_final_norm, v_final_norm),
             ("b_in", b_in, m_b_in, v_b_in), ("rel_bias", rel_bias, m_rel_bias, v_rel_bias),
             ("sinks", sinks, m_sinks, v_sinks)]
    zero_row = jnp.zeros((1, 128), F32)
    pack = lambda arrs, last: _pack_small(arrs[:4], arrs[4], arrs[5], arrs[6], last)
    g_small = pack([dg1, dgm, dg2, dgf, db_in, dtable[:, :TOTAL_HEADS], dsinks[:, 0]], loss_part)
    packed = [pack([s[k] for s in small], zero_row) for k in (1, 2, 3)]
    gs, ds, ms, vs = _small_update(g_small, *packed, "small_update")
    like = [s[1] for s in small]
    for nm_s, g_, d_, m_, v_ in zip([s[0] for s in small], _unpack_small(gs, like), _unpack_small(ds, like),
                                    _unpack_small(ms, like), _unpack_small(vs, like)):
        out_g[nm_s], out_d[nm_s], out_m[nm_s], out_v[nm_s] = g_, d_, m_, v_
    loss = gs[78, 0]

    order = ["ffn1_norm", "ffn1_w_gate", "ffn1_w_up", "ffn1_w_down", "mix_norm", "w_in", "b_in", "w_branch_a",
             "w_branch_b", "w_out", "sinks", "rel_bias", "ffn2_norm", "ffn2_w_gate", "ffn2_w_up", "ffn2_w_down",
             "final_norm"]
    return (loss, dx.reshape(x.shape), *[out_g[k] for k in order], *[out_d[k] for k in order],
            *[out_m[k] for k in order], *[out_v[k] for k in order])
```
